```python
import jax, jax.numpy as jnp
from jax import lax
import numpy as np

D_MODEL = 1024
BATCH = 8
SEQ = 2048
DEPTH = 2

CHUNK = 64
N_MIXERS = 2
N_LAYERS_A = (DEPTH + 1) // 2
N_LAYERS_B = DEPTH // 2
RMS_EPS = 1e-6

GLA_HEADS = 4
KEY_DIM = D_MODEL // 2
VALUE_DIM = D_MODEL
HEAD_K = KEY_DIM // GLA_HEADS
HEAD_V = VALUE_DIM // GLA_HEADS
GATE_RANK = 16
GATE_NORMALIZER = 16.0
PROJ_A = 2 * KEY_DIM + 2 * VALUE_DIM + GATE_RANK
SPLITS_A = (KEY_DIM, 2 * KEY_DIM, 2 * KEY_DIM + VALUE_DIM, 2 * KEY_DIM + 2 * VALUE_DIM)

CONV_DIM = D_MODEL
CONV_WIDTH = 3

D_FF = 2816
FFN_CONV_WIDTH = 3

kernel_name = 'hybrid_gla_shortconv_convffn_trunk'


def _rmsnorm(x, g):
    xf = x.astype(jnp.float32)
    y = xf * lax.rsqrt(jnp.mean(xf * xf, axis=-1, keepdims=True) + RMS_EPS)
    return (y * g.astype(jnp.float32)).astype(x.dtype)


def _causal_dwconv(z, w):
    width, ch = w.shape
    return lax.conv_general_dilated(
        z, w[:, None, :].astype(z.dtype), window_strides=(1,),
        padding=[(width - 1, 0)], dimension_numbers=('NWC', 'WIO', 'NWC'),
        feature_group_count=ch)


def _gla_mixer(x, w_in, w_gate_up, b_gate, gn, w_out):
    bsz, seq, _ = x.shape
    n_chunks = seq // CHUNK
    proj = x @ w_in
    q, k, v, r, gl = jnp.split(proj, SPLITS_A, axis=-1)
    log_a = jax.nn.log_sigmoid((gl @ w_gate_up + b_gate).astype(jnp.float32)) / GATE_NORMALIZER

    def chunked(t, d):
        return t.astype(jnp.float32).reshape(bsz, n_chunks, CHUNK, GLA_HEADS, d)

    q = chunked(q, HEAD_K) * (HEAD_K ** -0.5)
    k = chunked(k, HEAD_K)
    v = chunked(v, HEAD_V)
    log_a = chunked(log_a, HEAD_K)
    cum = jnp.cumsum(log_a, axis=2)
    tot = cum[:, :, -1]
    k_dec = k * jnp.exp(tot[:, :, None] - cum)
    upd = jnp.einsum('bnlhk,bnlhv->nbhkv', k_dec, v)
    q_n = jnp.moveaxis(q, 1, 0)
    decay = jnp.moveaxis(jnp.exp(tot), 1, 0)

    def step(state, inp):
        q_c, d_c, u_c = inp
        state = d_c[..., None] * state + u_c
        return state, jnp.einsum('blhk,bhkv->blhv', q_c, state)

    s0 = jnp.zeros((bsz, GLA_HEADS, HEAD_K, HEAD_V), jnp.float32)
    _, o = lax.scan(step, s0, (q_n, decay, upd))
    o = jnp.moveaxis(o, 0, 1).reshape(bsz, seq, GLA_HEADS, HEAD_V)
    o = o * lax.rsqrt(jnp.mean(o * o, axis=-1, keepdims=True) + RMS_EPS)
    o = o.reshape(bsz, seq, VALUE_DIM) * gn.astype(jnp.float32) * jax.nn.silu(r.astype(jnp.float32))
    return o.astype(x.dtype) @ w_out


def _short_conv_mixer(x, w_in, conv_w, w_out):
    b_gate, c_gate, h = jnp.split(x @ w_in, 3, axis=-1)
    return (b_gate * _causal_dwconv(c_gate * h, conv_w)) @ w_out


def _conv_ffn(x, w_up, conv_w, w_down):
    g, u = jnp.split(x @ w_up, 2, axis=-1)
    return (jax.nn.silu(_causal_dwconv(g, conv_w)) * u) @ w_down


def _fwd_setup_inputs(seed: int = 0) -> dict:
    key = jax.random.key(seed)
    ks = jax.random.split(key, 20)
    nrm = jax.random.normal
    f32 = jnp.float32
    d = D_MODEL
    return {
        'x': nrm(ks[0], (BATCH, SEQ, d), f32),
        'a_norm': 1.0 + 0.05 * nrm(ks[1], (N_LAYERS_A, d), f32),
        'a_w_in': nrm(ks[2], (N_LAYERS_A, d, PROJ_A), f32) * d ** -0.5,
        'a_w_gate_up': nrm(ks[3], (N_LAYERS_A, GATE_RANK, KEY_DIM), f32) * GATE_RANK ** -0.5,
        'a_b_gate': 0.1 * nrm(ks[4], (N_LAYERS_A, KEY_DIM), f32),
        'a_gn': 1.0 + 0.05 * nrm(ks[5], (N_LAYERS_A, VALUE_DIM), f32),
        'a_w_out': nrm(ks[6], (N_LAYERS_A, VALUE_DIM, d), f32) * VALUE_DIM ** -0.5,
        'b_norm': 1.0 + 0.05 * nrm(ks[7], (N_LAYERS_B, d), f32),
        'b_w_in': nrm(ks[8], (N_LAYERS_B, d, 3 * CONV_DIM), f32) * d ** -0.5,
        'b_conv': nrm(ks[9], (N_LAYERS_B, CONV_WIDTH, CONV_DIM), f32) * CONV_WIDTH ** -0.5,
        'b_w_out': nrm(ks[10], (N_LAYERS_B, CONV_DIM, d), f32) * CONV_DIM ** -0.5,
        'f_norm': 1.0 + 0.05 * nrm(ks[11], (DEPTH, d), f32),
        'f_w_up': nrm(ks[12], (DEPTH, d, 2 * D_FF), f32) * d ** -0.5,
        'f_conv': nrm(ks[13], (DEPTH, FFN_CONV_WIDTH, D_FF), f32) * FFN_CONV_WIDTH ** -0.5,
        'f_w_down': nrm(ks[14], (DEPTH, D_FF, d), f32) * D_FF ** -0.5,
        'final_norm': 1.0 + 0.05 * nrm(ks[15], (d,), f32),
    }


def _fwd_reference(x, a_norm, a_w_in, a_w_gate_up, a_b_gate, a_gn, a_w_out,
              b_norm, b_w_in, b_conv, b_w_out,
              f_norm, f_w_up, f_conv, f_w_down, final_norm):
    for i in range(DEPTH):
        j = i // N_MIXERS
        if i % N_MIXERS == 0:
            h = _rmsnorm(x, a_norm[j])
            x = x + _gla_mixer(h, a_w_in[j], a_w_gate_up[j], a_b_gate[j], a_gn[j], a_w_out[j])
        else:
            h = _rmsnorm(x, b_norm[j])
            x = x + _short_conv_mixer(h, b_w_in[j], b_conv[j], b_w_out[j])
        h = _rmsnorm(x, f_norm[i])
        x = x + _conv_ffn(h, f_w_up[i], f_conv[i], f_w_down[i])
    return _rmsnorm(x, final_norm)


import jax as _jax
import jax.numpy as _jnp

TWIN_FORMAT = 'train_step'
FWD_PARAMS = ['x', 'a_norm', 'a_w_in', 'a_w_gate_up', 'a_b_gate', 'a_gn', 'a_w_out', 'b_norm', 'b_w_in', 'b_conv', 'b_w_out', 'f_norm', 'f_w_up', 'f_conv', 'f_w_down', 'final_norm']
TWIN_WEIGHTS = ['a_norm', 'a_w_in', 'a_w_gate_up', 'a_b_gate', 'a_gn', 'a_w_out', 'b_norm', 'b_w_in', 'b_conv', 'b_w_out', 'f_norm', 'f_w_up', 'f_conv', 'f_w_down', 'final_norm']
TWIN_DIFF_INPUT = 'x'
TWIN_INPUTS = ['x', 'a_norm', 'a_w_in', 'a_w_gate_up', 'a_b_gate', 'a_gn', 'a_w_out', 'b_norm', 'b_w_in', 'b_conv', 'b_w_out', 'f_norm', 'f_w_up', 'f_conv', 'f_w_down', 'final_norm', 'loss_target', 'm_a_norm', 'm_a_w_in', 'm_a_w_gate_up', 'm_a_b_gate', 'm_a_gn', 'm_a_w_out', 'm_b_norm', 'm_b_w_in', 'm_b_conv', 'm_b_w_out', 'm_f_norm', 'm_f_w_up', 'm_f_conv', 'm_f_w_down', 'm_final_norm', 'v_a_norm', 'v_a_w_in', 'v_a_w_gate_up', 'v_a_b_gate', 'v_a_gn', 'v_a_w_out', 'v_b_norm', 'v_b_w_in', 'v_b_conv', 'v_b_w_out', 'v_f_norm', 'v_f_w_up', 'v_f_conv', 'v_f_w_down', 'v_final_norm']
TWIN_OUTPUTS = ['loss', 'grad_x', 'grad_a_norm', 'grad_a_w_in', 'grad_a_w_gate_up', 'grad_a_b_gate', 'grad_a_gn', 'grad_a_w_out', 'grad_b_norm', 'grad_b_w_in', 'grad_b_conv', 'grad_b_w_out', 'grad_f_norm', 'grad_f_w_up', 'grad_f_conv', 'grad_f_w_down', 'grad_final_norm', 'delta_a_norm', 'delta_a_w_in', 'delta_a_w_gate_up', 'delta_a_b_gate', 'delta_a_gn', 'delta_a_w_out', 'delta_b_norm', 'delta_b_w_in', 'delta_b_conv', 'delta_b_w_out', 'delta_f_norm', 'delta_f_w_up', 'delta_f_conv', 'delta_f_w_down', 'delta_final_norm', 'new_m_a_norm', 'new_m_a_w_in', 'new_m_a_w_gate_up', 'new_m_a_b_gate', 'new_m_a_gn', 'new_m_a_w_out', 'new_m_b_norm', 'new_m_b_w_in', 'new_m_b_conv', 'new_m_b_w_out', 'new_m_f_norm', 'new_m_f_w_up', 'new_m_f_conv', 'new_m_f_w_down', 'new_m_final_norm', 'new_v_a_norm', 'new_v_a_w_in', 'new_v_a_w_gate_up', 'new_v_a_b_gate', 'new_v_a_gn', 'new_v_a_w_out', 'new_v_b_norm', 'new_v_b_w_in', 'new_v_b_conv', 'new_v_b_w_out', 'new_v_f_norm', 'new_v_f_w_up', 'new_v_f_conv', 'new_v_f_w_down', 'new_v_final_norm']
TWIN_LEAF_KINDS = {'loss': 'loss', 'grad_x': 'grad_x', 'grad_a_norm': 'grad_w', 'grad_a_w_in': 'grad_w', 'grad_a_w_gate_up': 'grad_w', 'grad_a_b_gate': 'grad_w', 'grad_a_gn': 'grad_w', 'grad_a_w_out': 'grad_w', 'grad_b_norm': 'grad_w', 'grad_b_w_in': 'grad_w', 'grad_b_conv': 'grad_w', 'grad_b_w_out': 'grad_w', 'grad_f_norm': 'grad_w', 'grad_f_w_up': 'grad_w', 'grad_f_conv': 'grad_w', 'grad_f_w_down': 'grad_w', 'grad_final_norm': 'grad_w', 'delta_a_norm': 'delta_w', 'delta_a_w_in': 'delta_w', 'delta_a_w_gate_up': 'delta_w', 'delta_a_b_gate': 'delta_w', 'delta_a_gn': 'delta_w', 'delta_a_w_out': 'delta_w', 'delta_b_norm': 'delta_w', 'delta_b_w_in': 'delta_w', 'delta_b_conv': 'delta_w', 'delta_b_w_out': 'delta_w', 'delta_f_norm': 'delta_w', 'delta_f_w_up': 'delta_w', 'delta_f_conv': 'delta_w', 'delta_f_w_down': 'delta_w', 'delta_final_norm': 'delta_w', 'new_m_a_norm': 'new_m', 'new_m_a_w_in': 'new_m', 'new_m_a_w_gate_up': 'new_m', 'new_m_a_b_gate': 'new_m', 'new_m_a_gn': 'new_m', 'new_m_a_w_out': 'new_m', 'new_m_b_norm': 'new_m', 'new_m_b_w_in': 'new_m', 'new_m_b_conv': 'new_m', 'new_m_b_w_out': 'new_m', 'new_m_f_norm': 'new_m', 'new_m_f_w_up': 'new_m', 'new_m_f_conv': 'new_m', 'new_m_f_w_down': 'new_m', 'new_m_final_norm': 'new_m', 'new_v_a_norm': 'new_v', 'new_v_a_w_in': 'new_v', 'new_v_a_w_gate_up': 'new_v', 'new_v_a_b_gate': 'new_v', 'new_v_a_gn': 'new_v', 'new_v_a_w_out': 'new_v', 'new_v_b_norm': 'new_v', 'new_v_b_w_in': 'new_v', 'new_v_b_conv': 'new_v', 'new_v_b_w_out': 'new_v', 'new_v_f_norm': 'new_v', 'new_v_f_w_up': 'new_v', 'new_v_f_conv': 'new_v', 'new_v_f_w_down': 'new_v', 'new_v_final_norm': 'new_v'}


def _forward(args):
    return _fwd_reference(*[args[k] for k in FWD_PARAMS])


def _output_shape():
    out = _jax.eval_shape(lambda: _forward(_fwd_setup_inputs(0)))
    return out.shape, out.dtype

N_MICROBATCH = 1
ADAM_LR = 0.001
ADAM_B1 = 0.9
ADAM_B2 = 0.999
ADAM_EPS = 1e-08
ADAM_WD = 0.01
ADAM_STEP = 10
PER_EXAMPLE_BATCH_AXIS = {'x': 0, 'loss_target': 0}
SHARED_INPUTS = []
_WEIGHT_DTYPES = {'a_norm': _jnp.float32, 'a_w_in': _jnp.float32, 'a_w_gate_up': _jnp.float32, 'a_b_gate': _jnp.float32, 'a_gn': _jnp.float32, 'a_w_out': _jnp.float32, 'b_norm': _jnp.float32, 'b_w_in': _jnp.float32, 'b_conv': _jnp.float32, 'b_w_out': _jnp.float32, 'f_norm': _jnp.float32, 'f_w_up': _jnp.float32, 'f_conv': _jnp.float32, 'f_w_down': _jnp.float32, 'final_norm': _jnp.float32}
MOMENT_SCALE = {'a_norm': 2.217786e-01, 'a_w_in': 1.185622e-01, 'a_w_gate_up': 1.642443e-02, 'a_b_gate': 6.586426e-02, 'a_gn': 9.971083e-02, 'a_w_out': 1.006326e-01, 'b_norm': 1.402733e-01, 'b_w_in': 8.102636e-02, 'b_conv': 7.980874e-02, 'b_w_out': 8.160175e-02, 'f_norm': 9.719569e-02, 'f_w_up': 4.080921e-02, 'f_conv': 4.134923e-02, 'f_w_down': 6.665452e-02, 'final_norm': 1.598966e+01}


def _to_microbatches(a, axis):
    t = _jnp.moveaxis(a, axis, 0)
    t = t.reshape((N_MICROBATCH, t.shape[0] // N_MICROBATCH) + t.shape[1:])
    return _jnp.moveaxis(t, 1, axis + 1)


def setup_inputs(seed: int = 0) -> dict:
    inp = _fwd_setup_inputs(seed)
    key = _jax.random.fold_in(_jax.random.key(seed), 7919)
    shape, _ = _output_shape()
    out = dict(inp)
    out["loss_target"] = _jax.random.normal(_jax.random.fold_in(key, 0), shape, _jnp.float32)
    for i, name in enumerate(TWIN_WEIGHTS):
        w = inp[name].astype(_jnp.float32)
        if MOMENT_SCALE is None:
            s = _jnp.sqrt(_jnp.mean(_jnp.square(w)) + 1e-30)
        else:
            s = MOMENT_SCALE[name]
        km, kv = _jax.random.split(_jax.random.fold_in(key, i + 1))
        out[name] = w
        out["m_" + name] = s * _jax.random.normal(km, w.shape, _jnp.float32)
        out["v_" + name] = (s * s) * _jax.random.uniform(kv, w.shape, _jnp.float32, 0.5, 1.5)
    if N_MICROBATCH > 1:
        for name, axis in PER_EXAMPLE_BATCH_AXIS.items():
            out[name] = _to_microbatches(out[name], axis)
    return {'x': out['x'], 'a_norm': out['a_norm'], 'a_w_in': out['a_w_in'], 'a_w_gate_up': out['a_w_gate_up'], 'a_b_gate': out['a_b_gate'], 'a_gn': out['a_gn'], 'a_w_out': out['a_w_out'], 'b_norm': out['b_norm'], 'b_w_in': out['b_w_in'], 'b_conv': out['b_conv'], 'b_w_out': out['b_w_out'], 'f_norm': out['f_norm'], 'f_w_up': out['f_w_up'], 'f_conv': out['f_conv'], 'f_w_down': out['f_w_down'], 'final_norm': out['final_norm'], 'loss_target': out['loss_target'], 'm_a_norm': out['m_a_norm'], 'm_a_w_in': out['m_a_w_in'], 'm_a_w_gate_up': out['m_a_w_gate_up'], 'm_a_b_gate': out['m_a_b_gate'], 'm_a_gn': out['m_a_gn'], 'm_a_w_out': out['m_a_w_out'], 'm_b_norm': out['m_b_norm'], 'm_b_w_in': out['m_b_w_in'], 'm_b_conv': out['m_b_conv'], 'm_b_w_out': out['m_b_w_out'], 'm_f_norm': out['m_f_norm'], 'm_f_w_up': out['m_f_w_up'], 'm_f_conv': out['m_f_conv'], 'm_f_w_down': out['m_f_w_down'], 'm_final_norm': out['m_final_norm'], 'v_a_norm': out['v_a_norm'], 'v_a_w_in': out['v_a_w_in'], 'v_a_w_gate_up': out['v_a_w_gate_up'], 'v_a_b_gate': out['v_a_b_gate'], 'v_a_gn': out['v_a_gn'], 'v_a_w_out': out['v_a_w_out'], 'v_b_norm': out['v_b_norm'], 'v_b_w_in': out['v_b_w_in'], 'v_b_conv': out['v_b_conv'], 'v_b_w_out': out['v_b_w_out'], 'v_f_norm': out['v_f_norm'], 'v_f_w_up': out['v_f_w_up'], 'v_f_conv': out['v_f_conv'], 'v_f_w_down': out['v_f_w_down'], 'v_final_norm': out['v_final_norm']}


def _loss(weights, diff, rest, loss_target):
    with _jax.named_scope("forward"):
        args = {**rest, TWIN_DIFF_INPUT: diff, **{k: w.astype(_WEIGHT_DTYPES[k]) for k, w in weights.items()}}
        y = _forward(args)
    with _jax.named_scope("loss_head"):
        err = _jnp.square(y.astype(_jnp.float32) - loss_target)
        return 0.5 * _jnp.sum(_jnp.mean(err, axis=-1)) if err.ndim else 0.5 * err


def _adamw(w, g, m, v):
    m = ADAM_B1 * m + (1.0 - ADAM_B1) * g
    v = ADAM_B2 * v + (1.0 - ADAM_B2) * _jnp.square(g)
    m_hat = m / (1.0 - ADAM_B1 ** ADAM_STEP)
    v_hat = v / (1.0 - ADAM_B2 ** ADAM_STEP)
    delta = -ADAM_LR * (m_hat / (_jnp.sqrt(v_hat) + ADAM_EPS) + ADAM_WD * w)
    return delta, m, v


def reference(x, a_norm, a_w_in, a_w_gate_up, a_b_gate, a_gn, a_w_out, b_norm, b_w_in, b_conv, b_w_out, f_norm, f_w_up, f_conv, f_w_down, final_norm, loss_target, m_a_norm, m_a_w_in, m_a_w_gate_up, m_a_b_gate, m_a_gn, m_a_w_out, m_b_norm, m_b_w_in, m_b_conv, m_b_w_out, m_f_norm, m_f_w_up, m_f_conv, m_f_w_down, m_final_norm, v_a_norm, v_a_w_in, v_a_w_gate_up, v_a_b_gate, v_a_gn, v_a_w_out, v_b_norm, v_b_w_in, v_b_conv, v_b_w_out, v_f_norm, v_f_w_up, v_f_conv, v_f_w_down, v_final_norm):
    given = dict(x=x, a_norm=a_norm, a_w_in=a_w_in, a_w_gate_up=a_w_gate_up, a_b_gate=a_b_gate, a_gn=a_gn, a_w_out=a_w_out, b_norm=b_norm, b_w_in=b_w_in, b_conv=b_conv, b_w_out=b_w_out, f_norm=f_norm, f_w_up=f_w_up, f_conv=f_conv, f_w_down=f_w_down, final_norm=final_norm, loss_target=loss_target, m_a_norm=m_a_norm, m_a_w_in=m_a_w_in, m_a_w_gate_up=m_a_w_gate_up, m_a_b_gate=m_a_b_gate, m_a_gn=m_a_gn, m_a_w_out=m_a_w_out, m_b_norm=m_b_norm, m_b_w_in=m_b_w_in, m_b_conv=m_b_conv, m_b_w_out=m_b_w_out, m_f_norm=m_f_norm, m_f_w_up=m_f_w_up, m_f_conv=m_f_conv, m_f_w_down=m_f_w_down, m_final_norm=m_final_norm, v_a_norm=v_a_norm, v_a_w_in=v_a_w_in, v_a_w_gate_up=v_a_w_gate_up, v_a_b_gate=v_a_b_gate, v_a_gn=v_a_gn, v_a_w_out=v_a_w_out, v_b_norm=v_b_norm, v_b_w_in=v_b_w_in, v_b_conv=v_b_conv, v_b_w_out=v_b_w_out, v_f_norm=v_f_norm, v_f_w_up=v_f_w_up, v_f_conv=v_f_conv, v_f_w_down=v_f_w_down, v_final_norm=v_final_norm)
    weights = {n: given[n] for n in TWIN_WEIGHTS}
    shared = {n: given[n] for n in SHARED_INPUTS}
    per_example = {n: given[n] for n in ['x']}
    grad_fn = _jax.value_and_grad(_loss, argnums=(0, 1))

    def one_microbatch(ex, loss_target):
        ex = dict(ex)
        diff = ex.pop(TWIN_DIFF_INPUT)
        return grad_fn(weights, diff, {**shared, **ex}, loss_target)

    if N_MICROBATCH == 1:
        loss, (grad_w, grad_x) = one_microbatch(per_example, given["loss_target"])
    else:
        def body(carry, xs):
            loss_sum, grad_sum = carry
            l_k, (gw_k, gx_k) = one_microbatch(xs[0], xs[1])
            with _jax.named_scope("update"):
                return (loss_sum + l_k, _jax.tree.map(_jnp.add, grad_sum, gw_k)), gx_k

        init = (_jnp.zeros((), _jnp.float32), _jax.tree.map(_jnp.zeros_like, weights))
        (loss, grad_w), grad_x = _jax.lax.scan(body, init, (per_example, given["loss_target"]))
    with _jax.named_scope("update"):
        delta_w, new_m, new_v = {}, {}, {}
        for n in TWIN_WEIGHTS:
            delta_w[n], new_m[n], new_v[n] = _adamw(weights[n], grad_w[n], given["m_" + n], given["v_" + n])
    return (loss, grad_x, *[grad_w[n] for n in TWIN_WEIGHTS], *[delta_w[n] for n in TWIN_WEIGHTS],
            *[new_m[n] for n in TWIN_WEIGHTS], *[new_v[n] for n in TWIN_WEIGHTS])
```

```python
import jax
import jax.numpy as jnp
from jax import lax
from jax.experimental import pallas as pl
from jax.experimental.pallas import tpu as pltpu

F32 = jnp.float32
BF16 = jnp.bfloat16

N_DEV = 8
SEQ = 2048
D_MODEL = 1024
CHUNK = 64
N_CHUNKS = SEQ // CHUNK
RMS_EPS = 1e-6
GLA_HEADS = 4
KEY_DIM = 512
VALUE_DIM = 1024
HEAD_K = KEY_DIM // GLA_HEADS
HEAD_V = VALUE_DIM // GLA_HEADS
GATE_RANK = 16
GATE_PAD = 128
GATE_NORMALIZER = 16.0
PROJ_A = 2 * KEY_DIM + 2 * VALUE_DIM + GATE_RANK
PROJ_A_PAD = 2 * KEY_DIM + 2 * VALUE_DIM + GATE_PAD
A_SHARD = PROJ_A // N_DEV
B_SHARD = 3 * D_MODEL // N_DEV
D_FF = 2816
FF_BLOCK = 2 * D_FF // N_DEV
FF_BLOCKS = D_FF // FF_BLOCK
ADAM_LR = 0.001
ADAM_B1 = 0.9
ADAM_B2 = 0.999
ADAM_EPS = 1e-08
ADAM_WD = 0.01
ADAM_STEP = 10
MESH_AXES = ("x", "y", "c")

VMEM_LIMIT = 56 * 1024 * 1024
ROW_CHUNK = 256
HALO = 16


def _params(sem=None, vmem=VMEM_LIMIT):
    return pltpu.CompilerParams(dimension_semantics=sem, vmem_limit_bytes=vmem)


NN = ((1,), (0,))
NT = ((1,), (1,))
TN = ((0,), (0,))


def _matmul(name, a, a_spec, b, b_spec, dims, grid, nk, out_shape, out_spec, acc_shape=None, res=None, res_spec=None):
    has_res = res is not None

    def body(*refs):
        a_ref, b_ref = refs[0], refs[1]
        r_ref = refs[2] if has_res else None
        o_ref = refs[2 + has_res]
        acc_ref = refs[3 + has_res] if nk > 1 else None

        def product():
            return lax.dot_general(a_ref[...].astype(BF16), b_ref[...].astype(BF16), (dims, ((), ())),
                                   preferred_element_type=F32)

        def finish(v):
            if has_res:
                v = v + r_ref[...]
            o_ref[...] = v.astype(o_ref.dtype)

        if nk == 1:
            finish(product())
        else:
            k = pl.program_id(len(grid) - 1)
            p = product()

            @pl.when(k == 0)
            def _():
                acc_ref[...] = p

            @pl.when(k > 0)
            def _():
                acc_ref[...] += p

            @pl.when(k == nk - 1)
            def _():
                finish(acc_ref[...])

    operands = [a, b] + ([res] if has_res else [])
    in_specs = [a_spec, b_spec] + ([res_spec] if has_res else [])
    sem = ("parallel",) * (len(grid) - 1) + (("arbitrary",) if nk > 1 else ("parallel",))
    return pl.pallas_call(
        body, name=name, grid=grid, in_specs=in_specs, out_specs=out_spec, out_shape=out_shape,
        scratch_shapes=[pltpu.VMEM(acc_shape, F32)] if nk > 1 else [],
        compiler_params=_params(sem),
    )(*operands)


TM = 1024
TKS = 1024
N_TM = SEQ // TM
N_TKS = SEQ // TKS
PA_TILE = 640
N_PA = PROJ_A_PAD // PA_TILE


def _spec(shape, fn):
    return pl.BlockSpec(shape, fn)


def _proj_nn(name, h, w, n_tile, n_tiles):
    n = n_tile * n_tiles
    return _matmul(name, h, _spec((TM, D_MODEL), lambda j, i: (i, 0)), w, _spec((D_MODEL, n_tile), lambda j, i: (0, j)), NN,
                   (n_tiles, N_TM), 1, jax.ShapeDtypeStruct((SEQ, n), BF16), _spec((TM, n_tile), lambda j, i: (i, j)))


def _proj_blocks_nn(name, h, w_blocks, n_tile, flat_out):
    nb = w_blocks.shape[0]
    if flat_out:
        out_shape = jax.ShapeDtypeStruct((SEQ, nb * n_tile), BF16)
        out_spec = _spec((TM, n_tile), lambda j, i: (i, j))
    else:
        out_shape = jax.ShapeDtypeStruct((nb, SEQ, n_tile), BF16)
        out_spec = _spec((None, TM, n_tile), lambda j, i: (j, i, 0))
    return _matmul(name, h, _spec((TM, D_MODEL), lambda j, i: (i, 0)), w_blocks,
                   _spec((None, D_MODEL, n_tile), lambda j, i: (j, 0, 0)), NN, (nb, N_TM), 1, out_shape, out_spec)


def _out_nn(name, a, w, x):
    return _matmul(name, a, _spec((TM, D_MODEL), lambda i: (i, 0)), w, _spec((D_MODEL, D_MODEL), lambda i: (0, 0)), NN,
                   (N_TM,), 1, jax.ShapeDtypeStruct((SEQ, D_MODEL), F32), _spec((TM, D_MODEL), lambda i: (i, 0)),
                   res=x, res_spec=_spec((TM, D_MODEL), lambda i: (i, 0)))


def _down_nn(name, a_blocks, w_blocks, x):
    nb = a_blocks.shape[0]
    return _matmul(name, a_blocks, _spec((None, TM, FF_BLOCK), lambda i, k: (k, i, 0)), w_blocks,
                   _spec((None, FF_BLOCK, D_MODEL), lambda i, k: (k, 0, 0)), NN, (N_TM, nb), nb,
                   jax.ShapeDtypeStruct((SEQ, D_MODEL), F32), _spec((TM, D_MODEL), lambda i, k: (i, 0)),
                   acc_shape=(TM, D_MODEL), res=x, res_spec=_spec((TM, D_MODEL), lambda i, k: (i, 0)))


def _back_nt(name, dy, w):
    n = w.shape[0]
    return _matmul(name, dy, _spec((TM, D_MODEL), lambda i: (i, 0)), w, _spec((n, D_MODEL), lambda i: (0, 0)), NT,
                   (N_TM,), 1, jax.ShapeDtypeStruct((SEQ, n), BF16), _spec((TM, n), lambda i: (i, 0)))


def _back_blocks_nt(name, dy, w_blocks):
    nb = w_blocks.shape[0]
    return _matmul(name, dy, _spec((TM, D_MODEL), lambda j, i: (i, 0)), w_blocks,
                   _spec((None, FF_BLOCK, D_MODEL), lambda j, i: (j, 0, 0)), NT, (nb, N_TM), 1,
                   jax.ShapeDtypeStruct((nb, SEQ, FF_BLOCK), BF16), _spec((None, TM, FF_BLOCK), lambda j, i: (j, i, 0)))


def _back_sum_blocks_nt(name, d_blocks, w_blocks):
    nb, _, n = d_blocks.shape
    return _matmul(name, d_blocks, _spec((None, TM, n), lambda i, k: (k, i, 0)), w_blocks,
                   _spec((None, D_MODEL, n), lambda i, k: (k, 0, 0)), NT, (N_TM, nb), nb,
                   jax.ShapeDtypeStruct((SEQ, D_MODEL), F32), _spec((TM, D_MODEL), lambda i, k: (i, 0)), acc_shape=(TM, D_MODEL))


def _back_sum_cols_nt(name, d, w_blocks=None, w=None, n_tile=None):
    nb = d.shape[1] // n_tile
    if w_blocks is not None:
        b, b_spec = w_blocks, _spec((None, D_MODEL, n_tile), lambda i, k: (k, 0, 0))
    else:
        b, b_spec = w, _spec((D_MODEL, n_tile), lambda i, k: (0, k))
    return _matmul(name, d, _spec((TM, n_tile), lambda i, k: (i, k)), b, b_spec, NT, (N_TM, nb), nb,
                   jax.ShapeDtypeStruct((SEQ, D_MODEL), F32), _spec((TM, D_MODEL), lambda i, k: (i, 0)), acc_shape=(TM, D_MODEL))


def _wgrad_tn(name, a, a_cols, d, d_cols, out_blocks):
    nb = d.shape[1] // d_cols
    if out_blocks:
        out_shape = jax.ShapeDtypeStruct((nb, a_cols, d_cols), BF16)
        out_spec = _spec((None, a_cols, d_cols), lambda j, k: (j, 0, 0))
    else:
        out_shape = jax.ShapeDtypeStruct((a_cols, nb * d_cols), BF16)
        out_spec = _spec((a_cols, d_cols), lambda j, k: (0, j))
    return _matmul(name, a, _spec((TKS, a_cols), lambda j, k: (k, 0)), d, _spec((TKS, d_cols), lambda j, k: (k, j)), TN,
                   (nb, N_TKS), N_TKS, out_shape, out_spec, acc_shape=(a_cols, d_cols))


def _wgrad_a_blocks_tn(name, a_blocks, d):
    nb = a_blocks.shape[0]
    return _matmul(name, a_blocks, _spec((None, TKS, FF_BLOCK), lambda j, k: (j, k, 0)), d,
                   _spec((TKS, D_MODEL), lambda j, k: (k, 0)), TN, (nb, N_TKS), N_TKS,
                   jax.ShapeDtypeStruct((nb, FF_BLOCK, D_MODEL), BF16), _spec((None, FF_BLOCK, D_MODEL), lambda j, k: (j, 0, 0)),
                   acc_shape=(FF_BLOCK, D_MODEL))


def _wgrad_d_blocks_tn(name, a, d_blocks):
    nb = d_blocks.shape[0]
    return _matmul(name, a, _spec((TKS, D_MODEL), lambda j, k: (k, 0)), d_blocks,
                   _spec((None, TKS, FF_BLOCK), lambda j, k: (j, k, 0)), TN, (nb, N_TKS), N_TKS,
                   jax.ShapeDtypeStruct((nb, D_MODEL, FF_BLOCK), BF16), _spec((None, D_MODEL, FF_BLOCK), lambda j, k: (j, 0, 0)),
                   acc_shape=(D_MODEL, FF_BLOCK))


NORM_ROWS = 512


def _rstd(x):
    return lax.rsqrt(jnp.mean(x * x, axis=-1, keepdims=True) + RMS_EPS)


def _norm_fwd(name, x, gamma):
    def body(x_ref, g_ref, h_ref):
        x = x_ref[...]
        h_ref[...] = (x * _rstd(x) * g_ref[...]).astype(BF16)

    row = _spec((NORM_ROWS, D_MODEL), lambda i: (i, 0))
    return pl.pallas_call(
        body, name=name, grid=(SEQ // NORM_ROWS,), in_specs=[row, _spec((1, D_MODEL), lambda i: (0, 0))], out_specs=row,
        out_shape=jax.ShapeDtypeStruct((SEQ, D_MODEL), BF16), compiler_params=_params(("parallel",)),
    )(x, gamma)


def _norm_bwd_rows(x, gamma, dh):
    r = _rstd(x)
    xh = x * r
    dxh = dh * gamma
    dx = r * (dxh - xh * jnp.mean(dxh * xh, axis=-1, keepdims=True))
    return dx, jnp.sum(dh * xh, axis=0, keepdims=True)


def _norm_bwd(name, x, gamma, dh, dx_in):
    def body(x_ref, g_ref, dh_ref, dxi_ref, dx_ref, dg_ref):
        dx, dg = _norm_bwd_rows(x_ref[...], g_ref[...], dh_ref[...].astype(F32))
        dx_ref[...] = dxi_ref[...] + dx

        @pl.when(pl.program_id(0) == 0)
        def _():
            dg_ref[...] = dg

        @pl.when(pl.program_id(0) > 0)
        def _():
            dg_ref[...] += dg

    row = _spec((NORM_ROWS, D_MODEL), lambda i: (i, 0))
    vec = _spec((1, D_MODEL), lambda i: (0, 0))
    return pl.pallas_call(
        body, name=name, grid=(SEQ // NORM_ROWS,), in_specs=[row, vec, row, row], out_specs=[row, vec],
        out_shape=[jax.ShapeDtypeStruct((SEQ, D_MODEL), F32), jax.ShapeDtypeStruct((1, D_MODEL), F32)],
        compiler_params=_params(("arbitrary",)),
    )(x, gamma, dh, dx_in)


def _loss_head(x, gamma, target):
    def body(x_ref, g_ref, t_ref, loss_ref, dx_ref, dg_ref):
        x = x_ref[...]
        gamma = g_ref[...]
        err = x * _rstd(x) * gamma - t_ref[...]
        dy = err * (1.0 / D_MODEL)
        dx, dg = _norm_bwd_rows(x, gamma, dy)
        dx_ref[...] = dx
        part = 0.5 * jnp.sum(jnp.sum(err * err, axis=-1, keepdims=True) * (1.0 / D_MODEL), axis=0, keepdims=True)
        part = jnp.broadcast_to(part, loss_ref.shape)

        @pl.when(pl.program_id(0) == 0)
        def _():
            dg_ref[...] = dg
            loss_ref[...] = part

        @pl.when(pl.program_id(0) > 0)
        def _():
            dg_ref[...] += dg
            loss_ref[...] += part

    row = _spec((NORM_ROWS, D_MODEL), lambda i: (i, 0))
    vec = _spec((1, D_MODEL), lambda i: (0, 0))
    return pl.pallas_call(
        body, name="loss_head", grid=(SEQ // NORM_ROWS,), in_specs=[row, vec, row],
        out_specs=[_spec((1, 128), lambda i: (0, 0)), row, vec],
        out_shape=[jax.ShapeDtypeStruct((1, 128), F32), jax.ShapeDtypeStruct((SEQ, D_MODEL), F32),
                   jax.ShapeDtypeStruct((1, D_MODEL), F32)],
        compiler_params=_params(("arbitrary",)),
    )(x, gamma, target)


def _sigmoid(x):
    return 1.0 / (1.0 + jnp.exp(-x))


def _rows(ref, c):
    return ref[pl.ds(pl.multiple_of(c * ROW_CHUNK, ROW_CHUNK), ROW_CHUNK), :].astype(F32)


def _rows_before(ref, c):
    start = pl.multiple_of(jnp.maximum(c * ROW_CHUNK - HALO, 0), HALO)
    rows = ref[pl.ds(start, HALO), :].astype(F32)
    return jnp.where(c > 0, rows, 0.0)


def _rows_after(ref, c, n_chunks):
    start = pl.multiple_of(jnp.minimum((c + 1) * ROW_CHUNK, SEQ - HALO), HALO)
    rows = ref[pl.ds(start, HALO), :].astype(F32)
    return jnp.where(c < n_chunks - 1, rows, 0.0)


def _shift_down(z, before, n):
    row = lax.broadcasted_iota(jnp.int32, z.shape, 0)
    out = pltpu.roll(z, n, 0)
    for r in range(n):
        out = jnp.where(row == r, before[HALO - n + r:HALO - n + r + 1, :], out)
    return out


def _shift_up(z, after, n):
    rows = z.shape[0]
    row = lax.broadcasted_iota(jnp.int32, z.shape, 0)
    out = pltpu.roll(z, rows - n, 0)
    for r in range(n):
        out = jnp.where(row == rows - n + r, after[r:r + 1, :], out)
    return out


def _conv_rows(z, before, w):
    z1 = _shift_down(z, before, 1)
    z2 = _shift_down(z, before, 2)
    return w[2:3, :] * z + w[1:2, :] * z1 + w[0:1, :] * z2, z1, z2


def _conv_t_rows(dy, after, w):
    return w[2:3, :] * dy + w[1:2, :] * _shift_up(dy, after, 1) + w[0:1, :] * _shift_up(dy, after, 2)


N_ROW_CHUNKS = SEQ // ROW_CHUNK


def _ffn_mid_fwd(name, gu, conv_w):
    def body(gu_ref, w_ref, a_ref):
        w = w_ref[...]

        def chunk(c, carry):
            g = _rows(gu_ref.at[0], c)
            u = _rows(gu_ref.at[1], c)
            gc, _, _ = _conv_rows(g, _rows_before(gu_ref.at[0], c), w)
            a_ref[pl.ds(pl.multiple_of(c * ROW_CHUNK, ROW_CHUNK), ROW_CHUNK), :] = (gc * _sigmoid(gc) * u).astype(BF16)
            return carry

        lax.fori_loop(0, N_ROW_CHUNKS, chunk, 0)

    return pl.pallas_call(
        body, name=name, grid=(FF_BLOCKS,),
        in_specs=[_spec((2, None, SEQ, FF_BLOCK), lambda j: (0, j, 0, 0)), _spec((None, 3, FF_BLOCK), lambda j: (j, 0, 0))],
        out_specs=_spec((None, SEQ, FF_BLOCK), lambda j: (j, 0, 0)),
        out_shape=jax.ShapeDtypeStruct((FF_BLOCKS, SEQ, FF_BLOCK), BF16), compiler_params=_params(("parallel",)),
    )(gu, conv_w)


def _ffn_mid_bwd(name, gu, conv_w, da):
    def body(gu_ref, w_ref, da_ref, dgu_ref, dw_ref, dgc_ref):
        w = w_ref[...]

        def first(c, acc):
            g = _rows(gu_ref.at[0], c)
            u = _rows(gu_ref.at[1], c)
            d = _rows(da_ref, c)
            gc, g1, g2 = _conv_rows(g, _rows_before(gu_ref.at[0], c), w)
            sg = _sigmoid(gc)
            rows = pl.ds(pl.multiple_of(c * ROW_CHUNK, ROW_CHUNK), ROW_CHUNK)
            dgu_ref[1, rows, :] = (d * gc * sg).astype(BF16)
            dgc = d * u * (sg * (1.0 + gc * (1.0 - sg)))
            dgc_ref[rows, :] = dgc
            return (acc[0] + jnp.sum(dgc * g2, axis=0, keepdims=True), acc[1] + jnp.sum(dgc * g1, axis=0, keepdims=True),
                    acc[2] + jnp.sum(dgc * g, axis=0, keepdims=True))

        zero = jnp.zeros((1, FF_BLOCK), F32)
        acc = lax.fori_loop(0, N_ROW_CHUNKS, first, (zero, zero, zero))
        for r in range(3):
            dw_ref[r:r + 1, :] = acc[r]

        def second(c, carry):
            dgc = _rows(dgc_ref, c)
            dg = _conv_t_rows(dgc, _rows_after(dgc_ref, c, N_ROW_CHUNKS), w)
            dgu_ref[0, pl.ds(pl.multiple_of(c * ROW_CHUNK, ROW_CHUNK), ROW_CHUNK), :] = dg.astype(BF16)
            return carry

        lax.fori_loop(0, N_ROW_CHUNKS, second, 0)

    pair = _spec((2, None, SEQ, FF_BLOCK), lambda j: (0, j, 0, 0))
    wspec = _spec((None, 3, FF_BLOCK), lambda j: (j, 0, 0))
    return pl.pallas_call(
        body, name=name, grid=(FF_BLOCKS,),
        in_specs=[pair, wspec, _spec((None, SEQ, FF_BLOCK), lambda j: (j, 0, 0))], out_specs=[pair, wspec],
        out_shape=[jax.ShapeDtypeStruct((2, FF_BLOCKS, SEQ, FF_BLOCK), BF16), jax.ShapeDtypeStruct((FF_BLOCKS, 3, FF_BLOCK), F32)],
        scratch_shapes=[pltpu.VMEM((SEQ, FF_BLOCK), F32)], compiler_params=_params(("parallel",)),
    )(gu, conv_w, da)


SC_COLS = 256
N_SC = D_MODEL // SC_COLS


def _sc_specs():
    return [_spec((SEQ, SC_COLS), lambda j, part=part: (0, part * N_SC + j)) for part in range(3)]


def _sc_mid_fwd(p, conv_w):
    def body(b_ref, c_ref, h_ref, w_ref, y_ref):
        w = w_ref[...]

        def chunk(c, carry):
            z = _rows(c_ref, c) * _rows(h_ref, c)
            before = _rows_before(c_ref, c) * _rows_before(h_ref, c)
            zc, _, _ = _conv_rows(z, before, w)
            y_ref[pl.ds(pl.multiple_of(c * ROW_CHUNK, ROW_CHUNK), ROW_CHUNK), :] = (_rows(b_ref, c) * zc).astype(BF16)
            return carry

        lax.fori_loop(0, N_ROW_CHUNKS, chunk, 0)

    col = _spec((SEQ, SC_COLS), lambda j: (0, j))
    return pl.pallas_call(
        body, name="sc_mid_fwd", grid=(N_SC,), in_specs=_sc_specs() + [_spec((3, SC_COLS), lambda j: (0, j))], out_specs=col,
        out_shape=jax.ShapeDtypeStruct((SEQ, D_MODEL), BF16), compiler_params=_params(("parallel",)),
    )(p, p, p, conv_w)


def _sc_mid_bwd(p, conv_w, dy):
    def body(b_ref, c_ref, h_ref, w_ref, dy_ref, db_ref, dc_ref, dh_ref, dw_ref, dzc_ref):
        w = w_ref[...]

        def first(c, acc):
            z = _rows(c_ref, c) * _rows(h_ref, c)
            before = _rows_before(c_ref, c) * _rows_before(h_ref, c)
            zc, z1, z2 = _conv_rows(z, before, w)
            d = _rows(dy_ref, c)
            rows = pl.ds(pl.multiple_of(c * ROW_CHUNK, ROW_CHUNK), ROW_CHUNK)
            db_ref[rows, :] = (d * zc).astype(BF16)
            dzc = d * _rows(b_ref, c)
            dzc_ref[rows, :] = dzc
            return (acc[0] + jnp.sum(dzc * z2, axis=0, keepdims=True), acc[1] + jnp.sum(dzc * z1, axis=0, keepdims=True),
                    acc[2] + jnp.sum(dzc * z, axis=0, keepdims=True))

        zero = jnp.zeros((1, SC_COLS), F32)
        acc = lax.fori_loop(0, N_ROW_CHUNKS, first, (zero, zero, zero))
        for r in range(3):
            dw_ref[r:r + 1, :] = acc[r]

        def second(c, carry):
            dz = _conv_t_rows(_rows(dzc_ref, c), _rows_after(dzc_ref, c, N_ROW_CHUNKS), w)
            rows = pl.ds(pl.multiple_of(c * ROW_CHUNK, ROW_CHUNK), ROW_CHUNK)
            dc_ref[rows, :] = (dz * _rows(h_ref, c)).astype(BF16)
            dh_ref[rows, :] = (dz * _rows(c_ref, c)).astype(BF16)
            return carry

        lax.fori_loop(0, N_ROW_CHUNKS, second, 0)

    col = _spec((SEQ, SC_COLS), lambda j: (0, j))
    wspec = _spec((3, SC_COLS), lambda j: (0, j))
    act = jax.ShapeDtypeStruct((SEQ, D_MODEL), BF16)
    return pl.pallas_call(
        body, name="sc_mid_bwd", grid=(N_SC,), in_specs=_sc_specs() + [wspec, col], out_specs=[col, col, col, wspec],
        out_shape=[act, act, act, jax.ShapeDtypeStruct((3, D_MODEL), F32)],
        scratch_shapes=[pltpu.VMEM((SEQ, SC_COLS), F32)], compiler_params=_params(("parallel",)),
    )(p, p, p, conv_w, dy)


GLA_GROUP = 4
GLA_ROWS = GLA_GROUP * CHUNK
N_GROUPS = N_CHUNKS // GLA_GROUP
Q0, K0, V0, R0, G0 = 0, KEY_DIM, 2 * KEY_DIM, 2 * KEY_DIM + VALUE_DIM, 2 * KEY_DIM + 2 * VALUE_DIM


def _tri(strict):
    r = lax.broadcasted_iota(jnp.int32, (CHUNK, CHUNK), 0)
    c = lax.broadcasted_iota(jnp.int32, (CHUNK, CHUNK), 1)
    return jnp.where(c < r if strict else c <= r, 1.0, 0.0).astype(F32)


def _cumsum_rows(tri, x):
    return jnp.dot(tri, x, preferred_element_type=F32, precision=lax.Precision.HIGHEST)


def _gate_logits(gl, wgu, b_gate):
    return jnp.dot(gl, wgu, preferred_element_type=F32) + b_gate


def _log_decay(logits):
    return (jnp.minimum(logits, 0.0) - jnp.log(1.0 + jnp.exp(-jnp.abs(logits)))) * (1.0 / GATE_NORMALIZER)


def _head(x, h, width):
    return x[:, h * width:(h + 1) * width]


def _gla_fwd(proj, wgu, b_gate, gn):
    def body(p_ref, wgu_ref, b_ref, gn_ref, o_ref, og_ref, st_ref, state):
        @pl.when(pl.program_id(0) == 0)
        def _():
            state[...] = jnp.zeros_like(state)

        tri = _tri(False)
        la = _log_decay(_gate_logits(p_ref[:, G0:G0 + GATE_PAD], wgu_ref[...], b_ref[...]))
        for c in range(GLA_GROUP):
            rows = slice(c * CHUNK, (c + 1) * CHUNK)
            cum = _cumsum_rows(tri, la[rows])
            tot = cum[CHUNK - 1:CHUNK, :]
            kd = (p_ref[rows, K0:K0 + KEY_DIM].astype(F32) * jnp.exp(tot - cum)).astype(BF16)
            decay = jnp.exp(tot)
            q = (p_ref[rows, Q0:Q0 + KEY_DIM].astype(F32) * (HEAD_K ** -0.5)).astype(BF16)
            v = p_ref[rows, V0:V0 + VALUE_DIM]
            for h in range(GLA_HEADS):
                upd = lax.dot_general(_head(v, h, HEAD_V), _head(kd, h, HEAD_K), (TN, ((), ())), preferred_element_type=F32)
                s = state[h] * _head(decay, h, HEAD_K) + upd
                state[h] = s
                st_ref[c, h] = s
                o_ref[rows, h * HEAD_V:(h + 1) * HEAD_V] = lax.dot_general(
                    _head(q, h, HEAD_K), s.astype(BF16), (NT, ((), ())), preferred_element_type=F32)
        r = p_ref[:, R0:R0 + VALUE_DIM].astype(F32)
        gate = r * _sigmoid(r) * gn_ref[...]
        for h in range(GLA_HEADS):
            cols = slice(h * HEAD_V, (h + 1) * HEAD_V)
            o = o_ref[:, cols]
            og_ref[:, cols] = (o * _rstd(o) * gate[:, cols]).astype(BF16)

    rows = _spec((GLA_ROWS, VALUE_DIM), lambda i: (i, 0))
    const = lambda shape: _spec(shape, lambda i: (0,) * len(shape))
    return pl.pallas_call(
        body, name="gla_fwd", grid=(N_GROUPS,),
        in_specs=[_spec((GLA_ROWS, PROJ_A_PAD), lambda i: (i, 0)), const((GATE_PAD, KEY_DIM)), const((1, KEY_DIM)),
                  const((1, VALUE_DIM))],
        out_specs=[rows, rows, _spec((GLA_GROUP, GLA_HEADS, HEAD_V, HEAD_K), lambda i: (i, 0, 0, 0))],
        out_shape=[jax.ShapeDtypeStruct((SEQ, VALUE_DIM), F32), jax.ShapeDtypeStruct((SEQ, VALUE_DIM), BF16),
                   jax.ShapeDtypeStruct((N_CHUNKS, GLA_HEADS, HEAD_V, HEAD_K), F32)],
        scratch_shapes=[pltpu.VMEM((GLA_HEADS, HEAD_V, HEAD_K), F32)], compiler_params=_params(("arbitrary",)),
    )(proj, wgu, b_gate, gn)


def _gla_bwd(proj, wgu, b_gate, gn, o, states, dog):
    last = N_GROUPS - 1

    def body(p_ref, wgu_ref, b_ref, gn_ref, o_ref, st_ref, stp_ref, dog_ref, dp_ref, dwgu_ref, db_ref, dgn_ref, carry, do_buf):
        step = pl.program_id(0)

        @pl.when(step == 0)
        def _():
            carry[...] = jnp.zeros_like(carry)

        r = p_ref[:, R0:R0 + VALUE_DIM].astype(F32)
        sr = _sigmoid(r)
        silu = r * sr
        gn_row = gn_ref[...]
        dog_rows = dog_ref[...].astype(F32)
        dn = dog_rows * silu
        dgn_cols = []
        for h in range(GLA_HEADS):
            cols = slice(h * HEAD_V, (h + 1) * HEAD_V)
            oh = o_ref[:, cols]
            rs = _rstd(oh)
            ohat = oh * rs
            dn_h = dn[:, cols]
            dgn_cols.append(jnp.sum(dn_h * ohat, axis=0, keepdims=True))
            dohat = dn_h * gn_row[:, cols]
            do_buf[:, cols] = rs * (dohat - ohat * jnp.mean(dohat * ohat, axis=-1, keepdims=True))
            n_h = ohat * gn_row[:, cols]
            dp_ref[:, R0 + h * HEAD_V:R0 + (h + 1) * HEAD_V] = (
                dog_rows[:, cols] * n_h * (sr[:, cols] * (1.0 + r[:, cols] * (1.0 - sr[:, cols])))).astype(BF16)
        dgn = jnp.concatenate(dgn_cols, axis=1)

        tri = _tri(False)
        tri_strict = _tri(True)
        gl = p_ref[:, G0:G0 + GATE_PAD]
        logits = _gate_logits(gl, wgu_ref[...], b_ref[...])
        la = _log_decay(logits)
        dlogit_rows = []
        for c in reversed(range(GLA_GROUP)):
            rows = slice(c * CHUNK, (c + 1) * CHUNK)
            cum = _cumsum_rows(tri, la[rows])
            tot = cum[CHUNK - 1:CHUNK, :]
            fade = jnp.exp(tot - cum)
            k = p_ref[rows, K0:K0 + KEY_DIM].astype(F32)
            kd32 = k * fade
            kd = kd32.astype(BF16)
            decay = jnp.exp(tot)
            q = (p_ref[rows, Q0:Q0 + KEY_DIM].astype(F32) * (HEAD_K ** -0.5)).astype(BF16)
            v = p_ref[rows, V0:V0 + VALUE_DIM]
            do = do_buf[rows, :].astype(BF16)
            dkd_cols, ddecay_cols = [], []
            for h in range(GLA_HEADS):
                do_h = _head(do, h, HEAD_V)
                s = st_ref[c, h]
                dq = jnp.dot(do_h, s.astype(BF16), preferred_element_type=F32) * (HEAD_K ** -0.5)
                dp_ref[rows, Q0 + h * HEAD_K:Q0 + (h + 1) * HEAD_K] = dq.astype(BF16)
                g = carry[h] + lax.dot_general(do_h, _head(q, h, HEAD_K), (TN, ((), ())), preferred_element_type=F32)
                g16 = g.astype(BF16)
                dkd_cols.append(jnp.dot(_head(v, h, HEAD_V), g16, preferred_element_type=F32))
                dv = lax.dot_general(_head(kd, h, HEAD_K), g16, (NT, ((), ())), preferred_element_type=F32)
                dp_ref[rows, V0 + h * HEAD_V:V0 + (h + 1) * HEAD_V] = dv.astype(BF16)
                if c > 0:
                    s_prev = st_ref[c - 1, h]
                else:
                    s_prev = jnp.where(step < last, stp_ref[0, h], 0.0)
                ddecay_cols.append(jnp.sum(g * s_prev, axis=0, keepdims=True))
                carry[h] = g * _head(decay, h, HEAD_K)
            dkd = jnp.concatenate(dkd_cols, axis=1)
            ddecay = jnp.concatenate(ddecay_cols, axis=1)
            dp_ref[rows, K0:K0 + KEY_DIM] = (dkd * fade).astype(BF16)
            e = dkd * kd32
            dla = ddecay * decay + _cumsum_rows(tri_strict, e)
            dlogit_rows.append(dla * (1.0 / GATE_NORMALIZER) * (1.0 - _sigmoid(logits[rows])))
        dlogit = jnp.concatenate(dlogit_rows[::-1], axis=0)
        dlogit16 = dlogit.astype(BF16)
        dp_ref[:, G0:G0 + GATE_PAD] = lax.dot_general(
            dlogit16, wgu_ref[...], (NT, ((), ())), preferred_element_type=F32).astype(BF16)
        dwgu = lax.dot_general(gl, dlogit16, (TN, ((), ())), preferred_element_type=F32)
        db = jnp.sum(dlogit, axis=0, keepdims=True)

        @pl.when(step == 0)
        def _():
            dwgu_ref[...] = dwgu
            db_ref[...] = db
            dgn_ref[...] = dgn

        @pl.when(step > 0)
        def _():
            dwgu_ref[...] += dwgu
            db_ref[...] += db
            dgn_ref[...] += dgn

    rev = lambda i: (last - i, 0)
    rows = _spec((GLA_ROWS, VALUE_DIM), rev)
    const = lambda shape: _spec(shape, lambda i: (0,) * len(shape))
    st_shape = (GLA_HEADS, HEAD_V, HEAD_K)
    return pl.pallas_call(
        body, name="gla_bwd", grid=(N_GROUPS,),
        in_specs=[_spec((GLA_ROWS, PROJ_A_PAD), rev), const((GATE_PAD, KEY_DIM)), const((1, KEY_DIM)), const((1, VALUE_DIM)),
                  rows, _spec((GLA_GROUP,) + st_shape, lambda i: (last - i, 0, 0, 0)),
                  _spec((1,) + st_shape, lambda i: (jnp.maximum((last - i) * GLA_GROUP - 1, 0), 0, 0, 0)), rows],
        out_specs=[_spec((GLA_ROWS, PROJ_A_PAD), rev), const((GATE_PAD, KEY_DIM)), const((1, KEY_DIM)), const((1, VALUE_DIM))],
        out_shape=[jax.ShapeDtypeStruct((SEQ, PROJ_A_PAD), BF16), jax.ShapeDtypeStruct((GATE_PAD, KEY_DIM), F32),
                   jax.ShapeDtypeStruct((1, KEY_DIM), F32), jax.ShapeDtypeStruct((1, VALUE_DIM), F32)],
        scratch_shapes=[pltpu.VMEM(st_shape, F32), pltpu.VMEM((GLA_ROWS, VALUE_DIM), F32)],
        compiler_params=_params(("arbitrary",)),
    )(proj, wgu, b_gate, gn, o, states, states, dog)


def _ffn_fwd(tag, x, gamma, w_up, conv_w, w_down):
    h = _norm_fwd(f"ffn{tag}_norm", x, gamma)
    gu = _proj_blocks_nn(f"ffn{tag}_up", h, w_up, FF_BLOCK, False).reshape(2, FF_BLOCKS, SEQ, FF_BLOCK)
    a = _ffn_mid_fwd(f"ffn{tag}_mid", gu, conv_w)
    return _down_nn(f"ffn{tag}_down", a, w_down, x), (h, gu, a)


def _ffn_bwd(tag, x, gamma, w_up, conv_w, w_down, saved, dx):
    h, gu, a = saved
    da = _back_blocks_nt(f"ffn{tag}_da", dx, w_down)
    d_w_down = _wgrad_a_blocks_tn(f"ffn{tag}_dwdown", a, dx)
    dgu, d_conv = _ffn_mid_bwd(f"ffn{tag}_mid_bwd", gu, conv_w, da)
    dgu = dgu.reshape(2 * FF_BLOCKS, SEQ, FF_BLOCK)
    dh = _back_sum_blocks_nt(f"ffn{tag}_dh", dgu, w_up)
    d_w_up = _wgrad_d_blocks_tn(f"ffn{tag}_dwup", h, dgu)
    dx, d_gamma = _norm_bwd(f"ffn{tag}_norm_bwd", x, gamma, dh, dx)
    return dx, d_gamma, d_w_up, d_conv, d_w_down


def _local_step(x, target, w):
    x0 = x
    h0 = _norm_fwd("a_norm", x0, w["a_norm"])
    proj = _proj_nn("a_in", h0, w["a_w_in"], PA_TILE, N_PA)
    o, og, states = _gla_fwd(proj, w["a_w_gate_up"], w["a_b_gate"], w["a_gn"])
    x1 = _out_nn("a_out", og, w["a_w_out"], x0)
    x2, ffn0 = _ffn_fwd(0, x1, w["f_norm"][0:1], w["f_w_up"][0], w["f_conv"][0], w["f_w_down"][0])
    h2 = _norm_fwd("b_norm", x2, w["b_norm"])
    p = _proj_blocks_nn("b_in", h2, w["b_w_in"], B_SHARD, True)
    y = _sc_mid_fwd(p, w["b_conv"])
    x3 = _out_nn("b_out", y, w["b_w_out"], x2)
    x4, ffn1 = _ffn_fwd(1, x3, w["f_norm"][1:2], w["f_w_up"][1], w["f_conv"][1], w["f_w_down"][1])
    loss, dx, d_final_norm = _loss_head(x4, w["final_norm"], target)

    dx, d_f_norm1, d_up1, d_fconv1, d_down1 = _ffn_bwd(
        1, x3, w["f_norm"][1:2], w["f_w_up"][1], w["f_conv"][1], w["f_w_down"][1], ffn1, dx)

    dy = _back_nt("b_dy", dx, w["b_w_out"])
    d_b_w_out = _wgrad_tn("b_dwout", y, D_MODEL, dx, D_MODEL, False)
    db, dc, dhh, d_b_conv = _sc_mid_bwd(p, w["b_conv"], dy)
    dp = jnp.concatenate([db, dc, dhh], axis=1)
    dh2 = _back_sum_cols_nt("b_dh", dp, w_blocks=w["b_w_in"], n_tile=B_SHARD)
    d_b_w_in = _wgrad_tn("b_dwin", h2, D_MODEL, dp, B_SHARD, True)
    dx, d_b_norm = _norm_bwd("b_norm_bwd", x2, w["b_norm"], dh2, dx)

    dx, d_f_norm0, d_up0, d_fconv0, d_down0 = _ffn_bwd(
        0, x1, w["f_norm"][0:1], w["f_w_up"][0], w["f_conv"][0], w["f_w_down"][0], ffn0, dx)

    dog = _back_nt("a_dog", dx, w["a_w_out"])
    d_a_w_out = _wgrad_tn("a_dwout", og, D_MODEL, dx, D_MODEL, False)
    dproj, d_wgu, d_b_gate, d_gn = _gla_bwd(proj, w["a_w_gate_up"], w["a_b_gate"], w["a_gn"], o, states, dog)
    dh0 = _back_sum_cols_nt("a_dh", dproj, w=w["a_w_in"], n_tile=PA_TILE)
    d_a_w_in = _wgrad_tn("a_dwin", h0, D_MODEL, dproj, PA_TILE, False)
    dx, d_a_norm = _norm_bwd("a_norm_bwd", x0, w["a_norm"], dh0, dx)

    grads = dict(
        a_norm=d_a_norm, a_w_in=d_a_w_in, a_w_gate_up=d_wgu, a_b_gate=d_b_gate, a_gn=d_gn, a_w_out=d_a_w_out,
        b_norm=d_b_norm, b_w_in=d_b_w_in, b_conv=d_b_conv, b_w_out=d_b_w_out,
        f_norm=(d_f_norm0, d_f_norm1), f_w_up=(d_up0, d_up1), f_conv=(d_fconv0, d_fconv1), f_w_down=(d_down0, d_down1),
        final_norm=d_final_norm)
    return loss[0, 0], dx, grads


MESH_ID = pl.DeviceIdType.MESH
ANY = pl.BlockSpec(memory_space=pl.ANY)
N_PEERS = N_DEV - 1


def _position():
    return lax.axis_index("x"), lax.axis_index("y"), lax.axis_index("c")


def _slot(px, py, pc):
    return 4 * px + 2 * py + pc


def _all_gather(shards):
    n = len(shards)

    def body(*refs):
        ins, outs = refs[:n], refs[n:2 * n]
        send_sems, recv_sems, local_sems = refs[2 * n:]
        x, y, c = _position()
        me, sibling = (x, y, c), (x, y, 1 - c)
        chips = [(1 - x, y), (x, 1 - y), (1 - x, 1 - y)]

        def copy(t, k, block, to, from_input=False):
            dst = outs[t].at[_slot(*block)]
            return pltpu.make_async_remote_copy(
                src_ref=ins[t] if from_input else dst, dst_ref=dst, send_sem=send_sems.at[t, k], recv_sem=recv_sems.at[t, k],
                device_id=to, device_id_type=MESH_ID)

        mine = [pltpu.make_async_copy(ins[t], outs[t].at[_slot(*me)], local_sems.at[t]) for t in range(n)]
        for cp in mine:
            cp.start()
        first = []
        for t in range(n):
            first.append(copy(t, 0, me, sibling, True))
            first += [copy(t, 1 + j, me, (*chip, c), True) for j, chip in enumerate(chips)]
        for cp in first:
            cp.start()
        passed = []
        for t in range(n):
            for j, chip in enumerate(chips):
                copy(t, 1 + j, (*chip, c), me).wait_recv()
                fwd = copy(t, 4 + j, (*chip, c), sibling)
                fwd.start()
                passed.append(fwd)
        for t in range(n):
            copy(t, 0, sibling, me).wait_recv()
            for j, chip in enumerate(chips):
                copy(t, 4 + j, (*chip, 1 - c), me).wait_recv()
        for cp in first + passed:
            cp.wait_send()
        for cp in mine:
            cp.wait()

    return pl.pallas_call(
        body, name="weight_gather", in_specs=[ANY] * n, out_specs=[ANY] * n,
        out_shape=[jax.ShapeDtypeStruct((N_DEV,) + s.shape, s.dtype) for s in shards],
        scratch_shapes=[pltpu.SemaphoreType.DMA((n, N_PEERS)), pltpu.SemaphoreType.DMA((n, N_PEERS)), pltpu.SemaphoreType.DMA((n,))],
    )(*shards)


def _exchange(parts, shared):
    n = len(parts)

    def body(*refs):
        ins, outs = refs[:n], refs[n:2 * n]
        send_sems, recv_sems, local_sems = refs[2 * n:]
        x, y, c = _position()
        my_slot = _slot(x, y, c)

        def src(t, slot):
            return ins[t] if shared[t] else ins[t].at[slot]

        def peer(k):
            return x ^ (k >> 2), y ^ ((k >> 1) & 1), c ^ (k & 1)

        def copy(t, k):
            return pltpu.make_async_remote_copy(
                src_ref=src(t, _slot(*peer(k))), dst_ref=outs[t].at[my_slot], send_sem=send_sems.at[t, k - 1],
                recv_sem=recv_sems.at[t, k - 1], device_id=peer(k), device_id_type=MESH_ID)

        def arrival(t, k):
            landed = outs[t].at[_slot(*peer(k))]
            return pltpu.make_async_remote_copy(
                src_ref=landed, dst_ref=landed, send_sem=send_sems.at[t, k - 1], recv_sem=recv_sems.at[t, k - 1],
                device_id=peer(k), device_id_type=MESH_ID)

        mine = [pltpu.make_async_copy(src(t, my_slot), outs[t].at[my_slot], local_sems.at[t]) for t in range(n)]
        for cp in mine:
            cp.start()
        sent = [copy(t, k) for t in range(n) for k in range(1, N_DEV)]
        for cp in sent:
            cp.start()
        for t in range(n):
            for k in range(1, N_DEV):
                arrival(t, k).wait_recv()
        for cp in sent:
            cp.wait_send()
        for cp in mine:
            cp.wait()

    landing = [jax.ShapeDtypeStruct(((N_DEV,) + p.shape) if sh else p.shape, p.dtype) for p, sh in zip(parts, shared)]
    return pl.pallas_call(
        body, name="grad_exchange", in_specs=[ANY] * n, out_specs=[ANY] * n, out_shape=landing,
        scratch_shapes=[pltpu.SemaphoreType.DMA((n, N_PEERS)), pltpu.SemaphoreType.DMA((n, N_PEERS)), pltpu.SemaphoreType.DMA((n,))],
    )(*parts)


ADAM_ROWS = 256


def _adam_update(w, g, m, v):
    m = ADAM_B1 * m + (1.0 - ADAM_B1) * g
    v = ADAM_B2 * v + (1.0 - ADAM_B2) * (g * g)
    m_hat = m / (1.0 - ADAM_B1 ** ADAM_STEP)
    v_hat = v / (1.0 - ADAM_B2 ** ADAM_STEP)
    delta = -ADAM_LR * (m_hat / (jnp.sqrt(v_hat) + ADAM_EPS) + ADAM_WD * w)
    return delta, m, v


def _sum_slots(ref):
    total = ref[0].astype(F32)
    for d in range(1, N_DEV):
        total = total + ref[d].astype(F32)
    return total


def _adamw_sum(name, landed, w, m, v):
    layers, rows, cols = w.shape
    tr = ADAM_ROWS if rows % ADAM_ROWS == 0 else rows
    nt = rows // tr

    def body(*refs):
        parts = refs[:layers]
        w_ref, m_ref, v_ref, g_ref, d_ref, nm_ref, nv_ref = refs[layers:]
        layer = pl.program_id(0)
        g = _sum_slots(parts[0])
        for q in range(1, layers):
            g = jnp.where(layer == q, _sum_slots(parts[q]), g)
        delta, new_m, new_v = _adam_update(w_ref[...], g, m_ref[...], v_ref[...])
        g_ref[...] = g
        d_ref[...] = delta
        nm_ref[...] = new_m
        nv_ref[...] = new_v

    def part_spec(q):
        return _spec((N_DEV, tr, cols), lambda l, i: (0, jnp.where(l == q, i, jnp.where(l < q, 0, nt - 1)), 0))

    tile = _spec((None, tr, cols), lambda l, i: (l, i, 0))
    out = jax.ShapeDtypeStruct((layers, rows, cols), F32)
    return pl.pallas_call(
        body, name=name, grid=(layers, nt), in_specs=[part_spec(q) for q in range(layers)] + [tile] * 3,
        out_specs=[tile] * 4, out_shape=[out] * 4, compiler_params=_params(("arbitrary", "arbitrary")),
    )(*landed, w, m, v)


def _sum_small(landed):
    def body(in_ref, out_ref):
        out_ref[...] = _sum_slots(in_ref)

    return pl.pallas_call(body, name="small_grad_sum", out_shape=jax.ShapeDtypeStruct(landed.shape[1:], F32))(landed)


def _adamw_small(name, g, w, m, v):
    def body(g_ref, w_ref, m_ref, v_ref, d_ref, nm_ref, nv_ref):
        d_ref[...], nm_ref[...], nv_ref[...] = _adam_update(w_ref[...], g_ref[...], m_ref[...], v_ref[...])

    out = jax.ShapeDtypeStruct(w.shape, F32)
    return pl.pallas_call(body, name=name, out_shape=[out] * 3)(g, w, m, v)


LANES = 128
SUBLANES = 8
F_CONV_SHARD = D_FF // N_DEV
GATE_SHARD = KEY_DIM // N_DEV
NORM_SHARD = D_MODEL // N_DEV


def _tile_rows(a):
    flat = a.reshape(-1)
    size = -(-flat.shape[0] // (SUBLANES * LANES)) * SUBLANES * LANES
    return jnp.pad(flat, (0, size - flat.shape[0])).reshape(-1, LANES)


def _pack_rows(pieces):
    return jnp.concatenate([_tile_rows(p) for p in pieces], axis=0)


def _unpack_rows(packed, shapes):
    out, row = [], 0
    for shape in shapes:
        size = 1
        for s in shape:
            size *= s
        rows = -(-size // (SUBLANES * LANES)) * SUBLANES
        piece = packed[..., row:row + rows, :]
        out.append(piece.reshape(piece.shape[:-2] + (rows * LANES,))[..., :size])
        row += rows
    return out


SMALL_SHARDS = ((GATE_RANK, GATE_SHARD), (1, NORM_SHARD), (3, NORM_SHARD), (2, 3, F_CONV_SHARD))


def _unpack_small_shards(g):
    gate, b_norm, b_conv, f_conv = _unpack_rows(g, SMALL_SHARDS)
    gate = gate.reshape(N_DEV, GATE_RANK, GATE_SHARD).transpose(1, 0, 2).reshape(GATE_RANK, KEY_DIM)
    b_norm = b_norm.reshape(1, D_MODEL)
    b_conv = b_conv.reshape(N_DEV, 3, NORM_SHARD).transpose(1, 0, 2).reshape(3, D_MODEL)
    f_conv = f_conv.reshape(N_DEV, 2, 3, F_CONV_SHARD).transpose(1, 2, 0, 3).reshape(2, 3, D_FF)
    return gate, b_norm, b_conv, f_conv


def _conv_blocks(f_conv):
    return f_conv.reshape(2, 3, FF_BLOCKS, FF_BLOCK).transpose(0, 2, 1, 3)


def _conv_unblocks(f_conv):
    return f_conv.transpose(1, 0, 2).reshape(3, D_FF)


SMALL_LAYOUT = (("a_norm", (1, D_MODEL)), ("a_w_gate_up", (GATE_RANK, KEY_DIM)), ("a_b_gate", (1, KEY_DIM)), ("a_gn", (1, VALUE_DIM)),
                ("b_norm", (1, D_MODEL)), ("b_conv", (3, D_MODEL)), ("f_norm0", (1, D_MODEL)), ("f_norm1", (1, D_MODEL)),
                ("f_conv0", (3, D_FF)), ("f_conv1", (3, D_FF)), ("final_norm", (1, D_MODEL)))


def _pack_small_grads(g):
    full = dict(g)
    full["a_w_gate_up"] = g["a_w_gate_up"][:GATE_RANK]
    for layer in range(2):
        full[f"f_norm{layer}"] = g["f_norm"][layer]
        full[f"f_conv{layer}"] = _conv_unblocks(g["f_conv"][layer])
    return _pack_rows([full[name] for name, _ in SMALL_LAYOUT])


def _unpack_small_grads(packed):
    pieces = _unpack_rows(packed, [shape for _, shape in SMALL_LAYOUT])
    out = {name: piece.reshape(shape) for (name, shape), piece in zip(SMALL_LAYOUT, pieces)}
    out["f_norm"] = jnp.stack([out["f_norm0"][0], out["f_norm1"][0]])
    out["f_conv"] = jnp.stack([out["f_conv0"], out["f_conv1"]])
    return out


def kernel(x, a_norm, a_w_in, a_w_gate_up, a_b_gate, a_gn, a_w_out, b_norm, b_w_in, b_conv, b_w_out, f_norm, f_w_up, f_conv, f_w_down, final_norm, loss_target, m_a_norm, m_a_w_in, m_a_w_gate_up, m_a_b_gate, m_a_gn, m_a_w_out, m_b_norm, m_b_w_in, m_b_conv, m_b_w_out, m_f_norm, m_f_w_up, m_f_conv, m_f_w_down, m_final_norm, v_a_norm, v_a_w_in, v_a_w_gate_up, v_a_b_gate, v_a_gn, v_a_w_out, v_b_norm, v_b_w_in, v_b_conv, v_b_w_out, v_f_norm, v_f_w_up, v_f_conv, v_f_w_down, v_final_norm):
    my_slot = _slot(*_position())

    shards = [a_w_in[0], a_w_out[0], b_w_in[0], b_w_out[0], f_w_up[0], f_w_up[1], f_w_down[0], f_w_down[1]]
    gathered = _all_gather([s.astype(BF16) for s in shards] + [_pack_rows([a_w_gate_up[0], b_norm, b_conv[0], f_conv])])
    gate_full, b_norm_full, b_conv_full, f_conv_full = _unpack_small_shards(gathered[8])
    a_w_in_full = gathered[0].transpose(1, 0, 2).reshape(D_MODEL, PROJ_A)
    weights = dict(
        a_norm=a_norm, a_w_in=jnp.pad(a_w_in_full, ((0, 0), (0, PROJ_A_PAD - PROJ_A))),
        a_w_gate_up=jnp.pad(gate_full, ((0, GATE_PAD - GATE_RANK), (0, 0))).astype(BF16), a_b_gate=a_b_gate, a_gn=a_gn,
        a_w_out=gathered[1].reshape(D_MODEL, D_MODEL), b_norm=b_norm_full, b_w_in=gathered[2], b_conv=b_conv_full,
        b_w_out=gathered[3].reshape(D_MODEL, D_MODEL), f_norm=f_norm, f_w_up=(gathered[4], gathered[5]),
        f_conv=_conv_blocks(f_conv_full),
        f_w_down=(gathered[6].reshape(FF_BLOCKS, FF_BLOCK, D_MODEL), gathered[7].reshape(FF_BLOCKS, FF_BLOCK, D_MODEL)),
        final_norm=final_norm.reshape(1, D_MODEL))

    loss, dx, g = _local_step(x[0], loss_target[0], weights)
    loss = lax.psum(loss, MESH_AXES)

    parts = [
        g["a_w_in"][:, :PROJ_A].reshape(D_MODEL, N_DEV, A_SHARD).transpose(1, 0, 2),
        g["a_w_out"].reshape(N_DEV, D_MODEL // N_DEV, D_MODEL), g["b_w_in"], g["b_w_out"].reshape(N_DEV, D_MODEL // N_DEV, D_MODEL),
        g["f_w_up"][0], g["f_w_up"][1], g["f_w_down"][0].reshape(N_DEV, F_CONV_SHARD, D_MODEL),
        g["f_w_down"][1].reshape(N_DEV, F_CONV_SHARD, D_MODEL), _pack_small_grads(g)]
    landed = _exchange(parts, [False] * 8 + [True])

    big = dict(
        a_w_in=_adamw_sum("adam_a_w_in", [landed[0]], a_w_in, m_a_w_in, v_a_w_in),
        a_w_out=_adamw_sum("adam_a_w_out", [landed[1]], a_w_out, m_a_w_out, v_a_w_out),
        b_w_in=_adamw_sum("adam_b_w_in", [landed[2]], b_w_in, m_b_w_in, v_b_w_in),
        b_w_out=_adamw_sum("adam_b_w_out", [landed[3]], b_w_out, m_b_w_out, v_b_w_out),
        f_w_up=_adamw_sum("adam_f_w_up", [landed[4], landed[5]], f_w_up, m_f_w_up, v_f_w_up),
        f_w_down=_adamw_sum("adam_f_w_down", [landed[6], landed[7]], f_w_down, m_f_w_down, v_f_w_down))
    small_g = _unpack_small_grads(_sum_small(landed[8]))
    small_g["a_w_gate_up"] = lax.dynamic_slice_in_dim(small_g["a_w_gate_up"], my_slot * GATE_SHARD, GATE_SHARD, axis=1)
    small_g["b_norm"] = lax.dynamic_slice_in_dim(small_g["b_norm"], my_slot * NORM_SHARD, NORM_SHARD, axis=1)
    small_g["b_conv"] = lax.dynamic_slice_in_dim(small_g["b_conv"], my_slot * NORM_SHARD, NORM_SHARD, axis=1)
    small_g["f_conv"] = lax.dynamic_slice_in_dim(small_g["f_conv"], my_slot * F_CONV_SHARD, F_CONV_SHARD, axis=2)
    small_w = dict(
        a_norm=(a_norm, m_a_norm, v_a_norm), a_w_gate_up=(a_w_gate_up, m_a_w_gate_up, v_a_w_gate_up),
        a_b_gate=(a_b_gate, m_a_b_gate, v_a_b_gate), a_gn=(a_gn, m_a_gn, v_a_gn), b_norm=(b_norm, m_b_norm, v_b_norm),
        b_conv=(b_conv, m_b_conv, v_b_conv), f_norm=(f_norm, m_f_norm, v_f_norm), f_conv=(f_conv, m_f_conv, v_f_conv),
        final_norm=(final_norm, m_final_norm, v_final_norm))
    small = {}
    for name, (w, m, v) in small_w.items():
        flat = (w.shape[-1],) if w.ndim == 1 else w.shape[-2:]
        two_d = (-1, flat[-1])
        grad = small_g[name].reshape(w.shape)
        delta, new_m, new_v = _adamw_small(
            "adam_" + name, grad.reshape(two_d), w.reshape(two_d), m.reshape(two_d), v.reshape(two_d))
        small[name] = (grad, delta.reshape(w.shape), new_m.reshape(w.shape), new_v.reshape(w.shape))

    order = ["a_norm", "a_w_in", "a_w_gate_up", "a_b_gate", "a_gn", "a_w_out", "b_norm", "b_w_in", "b_conv", "b_w_out",
             "f_norm", "f_w_up", "f_conv", "f_w_down", "final_norm"]
    results = {**big, **small}
    outputs = [loss, dx.reshape(1, SEQ, D_MODEL)]
    for kind in range(4):
        outputs += [results[name][kind] for name in order]
    return tuple(outputs)
```

```python
import jax
import jax.numpy as jnp
from jax import lax
from jax.experimental import pallas as pl
from jax.experimental.pallas import tpu as pltpu

F32 = jnp.float32
BF16 = jnp.bfloat16

N_DEV = 8
SEQ = 2048
D_MODEL = 1024
CHUNK = 64
N_CHUNKS = SEQ // CHUNK
RMS_EPS = 1e-6
GLA_HEADS = 4
KEY_DIM = 512
VALUE_DIM = 1024
HEAD_K = KEY_DIM // GLA_HEADS
HEAD_V = VALUE_DIM // GLA_HEADS
GATE_RANK = 16
GATE_PAD = 128
GATE_NORMALIZER = 16.0
PROJ_A = 2 * KEY_DIM + 2 * VALUE_DIM + GATE_RANK
PROJ_A_PAD = 2 * KEY_DIM + 2 * VALUE_DIM + GATE_PAD
A_SHARD = PROJ_A // N_DEV
B_SHARD = 3 * D_MODEL // N_DEV
D_FF = 2816
FF_BLOCK = 2 * D_FF // N_DEV
FF_BLOCKS = D_FF // FF_BLOCK
ADAM_LR = 0.001
ADAM_B1 = 0.9
ADAM_B2 = 0.999
ADAM_EPS = 1e-08
ADAM_WD = 0.01
ADAM_STEP = 10
MESH_AXES = ("x", "y", "c")

VMEM_LIMIT = 56 * 1024 * 1024
ROW_CHUNK = 256
HALO = 16


def _params(sem=None, vmem=VMEM_LIMIT):
    return pltpu.CompilerParams(dimension_semantics=sem, vmem_limit_bytes=vmem)


NN = ((1,), (0,))
NT = ((1,), (1,))
TN = ((0,), (0,))


def _matmul(name, a, a_spec, b, b_spec, dims, grid, nk, out_shape, out_spec, acc_shape=None, res=None, res_spec=None):
    has_res = res is not None

    def body(*refs):
        a_ref, b_ref = refs[0], refs[1]
        r_ref = refs[2] if has_res else None
        o_ref = refs[2 + has_res]
        acc_ref = refs[3 + has_res] if nk > 1 else None

        def product():
            return lax.dot_general(a_ref[...].astype(BF16), b_ref[...].astype(BF16), (dims, ((), ())),
                                   preferred_element_type=F32)

        def finish(v):
            if has_res:
                v = v + r_ref[...]
            o_ref[...] = v.astype(o_ref.dtype)

        if nk == 1:
            finish(product())
        else:
            k = pl.program_id(len(grid) - 1)
            p = product()

            @pl.when(k == 0)
            def _():
                acc_ref[...] = p

            @pl.when(k > 0)
            def _():
                acc_ref[...] += p

            @pl.when(k == nk - 1)
            def _():
                finish(acc_ref[...])

    operands = [a, b] + ([res] if has_res else [])
    in_specs = [a_spec, b_spec] + ([res_spec] if has_res else [])
    sem = ("parallel",) * (len(grid) - 1) + (("arbitrary",) if nk > 1 else ("parallel",))
    return pl.pallas_call(
        body, name=name, grid=grid, in_specs=in_specs, out_specs=out_spec, out_shape=out_shape,
        scratch_shapes=[pltpu.VMEM(acc_shape, F32)] if nk > 1 else [],
        compiler_params=_params(sem),
    )(*operands)


TM = 1024
TKS = 1024
N_TM = SEQ // TM
N_TKS = SEQ // TKS
PA_TILE = 640
N_PA = PROJ_A_PAD // PA_TILE


def _spec(shape, fn):
    return pl.BlockSpec(shape, fn)


def _proj_nn(name, h, w, n_tile, n_tiles):
    n = n_tile * n_tiles
    return _matmul(name, h, _spec((TM, D_MODEL), lambda j, i: (i, 0)), w, _spec((D_MODEL, n_tile), lambda j, i: (0, j)), NN,
                   (n_tiles, N_TM), 1, jax.ShapeDtypeStruct((SEQ, n), BF16), _spec((TM, n_tile), lambda j, i: (i, j)))


def _proj_blocks_nn(name, h, w_blocks, n_tile, flat_out):
    nb = w_blocks.shape[0]
    if flat_out:
        out_shape = jax.ShapeDtypeStruct((SEQ, nb * n_tile), BF16)
        out_spec = _spec((TM, n_tile), lambda j, i: (i, j))
    else:
        out_shape = jax.ShapeDtypeStruct((nb, SEQ, n_tile), BF16)
        out_spec = _spec((None, TM, n_tile), lambda j, i: (j, i, 0))
    return _matmul(name, h, _spec((TM, D_MODEL), lambda j, i: (i, 0)), w_blocks,
                   _spec((None, D_MODEL, n_tile), lambda j, i: (j, 0, 0)), NN, (nb, N_TM), 1, out_shape, out_spec)


def _out_nn(name, a, w, x):
    return _matmul(name, a, _spec((TM, D_MODEL), lambda i: (i, 0)), w, _spec((D_MODEL, D_MODEL), lambda i: (0, 0)), NN,
                   (N_TM,), 1, jax.ShapeDtypeStruct((SEQ, D_MODEL), F32), _spec((TM, D_MODEL), lambda i: (i, 0)),
                   res=x, res_spec=_spec((TM, D_MODEL), lambda i: (i, 0)))


def _down_nn(name, a_blocks, w_blocks, x):
    nb = a_blocks.shape[0]
    return _matmul(name, a_blocks, _spec((None, TM, FF_BLOCK), lambda i, k: (k, i, 0)), w_blocks,
                   _spec((None, FF_BLOCK, D_MODEL), lambda i, k: (k, 0, 0)), NN, (N_TM, nb), nb,
                   jax.ShapeDtypeStruct((SEQ, D_MODEL), F32), _spec((TM, D_MODEL), lambda i, k: (i, 0)),
                   acc_shape=(TM, D_MODEL), res=x, res_spec=_spec((TM, D_MODEL), lambda i, k: (i, 0)))


def _back_nt(name, dy, w):
    n = w.shape[0]
    return _matmul(name, dy, _spec((TM, D_MODEL), lambda i: (i, 0)), w, _spec((n, D_MODEL), lambda i: (0, 0)), NT,
                   (N_TM,), 1, jax.ShapeDtypeStruct((SEQ, n), BF16), _spec((TM, n), lambda i: (i, 0)))


def _back_blocks_nt(name, dy, w_blocks):
    nb = w_blocks.shape[0]
    return _matmul(name, dy, _spec((TM, D_MODEL), lambda j, i: (i, 0)), w_blocks,
                   _spec((None, FF_BLOCK, D_MODEL), lambda j, i: (j, 0, 0)), NT, (nb, N_TM), 1,
                   jax.ShapeDtypeStruct((nb, SEQ, FF_BLOCK), BF16), _spec((None, TM, FF_BLOCK), lambda j, i: (j, i, 0)))


def _back_sum_blocks_nt(name, d_blocks, w_blocks):
    nb, _, n = d_blocks.shape
    return _matmul(name, d_blocks, _spec((None, TM, n), lambda i, k: (k, i, 0)), w_blocks,
                   _spec((None, D_MODEL, n), lambda i, k: (k, 0, 0)), NT, (N_TM, nb), nb,
                   jax.ShapeDtypeStruct((SEQ, D_MODEL), F32), _spec((TM, D_MODEL), lambda i, k: (i, 0)), acc_shape=(TM, D_MODEL))


def _back_sum_cols_nt(name, d, w_blocks=None, w=None, n_tile=None):
    nb = d.shape[1] // n_tile
    if w_blocks is not None:
        b, b_spec = w_blocks, _spec((None, D_MODEL, n_tile), lambda i, k: (k, 0, 0))
    else:
        b, b_spec = w, _spec((D_MODEL, n_tile), lambda i, k: (0, k))
    return _matmul(name, d, _spec((TM, n_tile), lambda i, k: (i, k)), b, b_spec, NT, (N_TM, nb), nb,
                   jax.ShapeDtypeStruct((SEQ, D_MODEL), F32), _spec((TM, D_MODEL), lambda i, k: (i, 0)), acc_shape=(TM, D_MODEL))


def _wgrad_tn(name, a, a_cols, d, d_cols, out_blocks):
    nb = d.shape[1] // d_cols
    if out_blocks:
        out_shape = jax.ShapeDtypeStruct((nb, a_cols, d_cols), BF16)
        out_spec = _spec((None, a_cols, d_cols), lambda j, k: (j, 0, 0))
    else:
        out_shape = jax.ShapeDtypeStruct((a_cols, nb * d_cols), BF16)
        out_spec = _spec((a_cols, d_cols), lambda j, k: (0, j))
    return _matmul(name, a, _spec((TKS, a_cols), lambda j, k: (k, 0)), d, _spec((TKS, d_cols), lambda j, k: (k, j)), TN,
                   (nb, N_TKS), N_TKS, out_shape, out_spec, acc_shape=(a_cols, d_cols))


def _wgrad_a_blocks_tn(name, a_blocks, d):
    nb = a_blocks.shape[0]
    return _matmul(name, a_blocks, _spec((None, TKS, FF_BLOCK), lambda j, k: (j, k, 0)), d,
                   _spec((TKS, D_MODEL), lambda j, k: (k, 0)), TN, (nb, N_TKS), N_TKS,
                   jax.ShapeDtypeStruct((nb, FF_BLOCK, D_MODEL), BF16), _spec((None, FF_BLOCK, D_MODEL), lambda j, k: (j, 0, 0)),
                   acc_shape=(FF_BLOCK, D_MODEL))


def _wgrad_d_blocks_tn(name, a, d_blocks):
    nb = d_blocks.shape[0]
    return _matmul(name, a, _spec((TKS, D_MODEL), lambda j, k: (k, 0)), d_blocks,
                   _spec((None, TKS, FF_BLOCK), lambda j, k: (j, k, 0)), TN, (nb, N_TKS), N_TKS,
                   jax.ShapeDtypeStruct((nb, D_MODEL, FF_BLOCK), BF16), _spec((None, D_MODEL, FF_BLOCK), lambda j, k: (j, 0, 0)),
                   acc_shape=(D_MODEL, FF_BLOCK))


NORM_ROWS = 512


def _rstd(x):
    return lax.rsqrt(jnp.mean(x * x, axis=-1, keepdims=True) + RMS_EPS)


def _norm_fwd(name, x, gamma):
    def body(x_ref, g_ref, h_ref):
        x = x_ref[...]
        h_ref[...] = (x * _rstd(x) * g_ref[...]).astype(BF16)

    row = _spec((NORM_ROWS, D_MODEL), lambda i: (i, 0))
    return pl.pallas_call(
        body, name=name, grid=(SEQ // NORM_ROWS,), in_specs=[row, _spec((1, D_MODEL), lambda i: (0, 0))], out_specs=row,
        out_shape=jax.ShapeDtypeStruct((SEQ, D_MODEL), BF16), compiler_params=_params(("parallel",)),
    )(x, gamma)


def _norm_bwd_rows(x, gamma, dh):
    r = _rstd(x)
    xh = x * r
    dxh = dh * gamma
    dx = r * (dxh - xh * jnp.mean(dxh * xh, axis=-1, keepdims=True))
    return dx, jnp.sum(dh * xh, axis=0, keepdims=True)


def _norm_bwd(name, x, gamma, dh, dx_in):
    def body(x_ref, g_ref, dh_ref, dxi_ref, dx_ref, dg_ref):
        dx, dg = _norm_bwd_rows(x_ref[...], g_ref[...], dh_ref[...].astype(F32))
        dx_ref[...] = dxi_ref[...] + dx

        @pl.when(pl.program_id(0) == 0)
        def _():
            dg_ref[...] = dg

        @pl.when(pl.program_id(0) > 0)
        def _():
            dg_ref[...] += dg

    row = _spec((NORM_ROWS, D_MODEL), lambda i: (i, 0))
    vec = _spec((1, D_MODEL), lambda i: (0, 0))
    return pl.pallas_call(
        body, name=name, grid=(SEQ // NORM_ROWS,), in_specs=[row, vec, row, row], out_specs=[row, vec],
        out_shape=[jax.ShapeDtypeStruct((SEQ, D_MODEL), F32), jax.ShapeDtypeStruct((1, D_MODEL), F32)],
        compiler_params=_params(("arbitrary",)),
    )(x, gamma, dh, dx_in)


def _loss_head(x, gamma, target):
    def body(x_ref, g_ref, t_ref, loss_ref, dx_ref, dg_ref):
        x = x_ref[...]
        gamma = g_ref[...]
        err = x * _rstd(x) * gamma - t_ref[...]
        dy = err * (1.0 / D_MODEL)
        dx, dg = _norm_bwd_rows(x, gamma, dy)
        dx_ref[...] = dx
        part = 0.5 * jnp.sum(jnp.sum(err * err, axis=-1, keepdims=True) * (1.0 / D_MODEL), axis=0, keepdims=True)
        part = jnp.broadcast_to(part, loss_ref.shape)

        @pl.when(pl.program_id(0) == 0)
        def _():
            dg_ref[...] = dg
            loss_ref[...] = part

        @pl.when(pl.program_id(0) > 0)
        def _():
            dg_ref[...] += dg
            loss_ref[...] += part

    row = _spec((NORM_ROWS, D_MODEL), lambda i: (i, 0))
    vec = _spec((1, D_MODEL), lambda i: (0, 0))
    return pl.pallas_call(
        body, name="loss_head", grid=(SEQ // NORM_ROWS,), in_specs=[row, vec, row],
        out_specs=[_spec((1, 128), lambda i: (0, 0)), row, vec],
        out_shape=[jax.ShapeDtypeStruct((1, 128), F32), jax.ShapeDtypeStruct((SEQ, D_MODEL), F32),
                   jax.ShapeDtypeStruct((1, D_MODEL), F32)],
        compiler_params=_params(("arbitrary",)),
    )(x, gamma, target)


def _sigmoid(x):
    return 1.0 / (1.0 + jnp.exp(-x))


def _rows(ref, c):
    return ref[pl.ds(pl.multiple_of(c * ROW_CHUNK, ROW_CHUNK), ROW_CHUNK), :].astype(F32)


def _rows_before(ref, c):
    start = pl.multiple_of(jnp.maximum(c * ROW_CHUNK - HALO, 0), HALO)
    rows = ref[pl.ds(start, HALO), :].astype(F32)
    return jnp.where(c > 0, rows, 0.0)


def _rows_after(ref, c, n_chunks):
    start = pl.multiple_of(jnp.minimum((c + 1) * ROW_CHUNK, SEQ - HALO), HALO)
    rows = ref[pl.ds(start, HALO), :].astype(F32)
    return jnp.where(c < n_chunks - 1, rows, 0.0)


def _shift_down(z, before, n):
    row = lax.broadcasted_iota(jnp.int32, z.shape, 0)
    out = pltpu.roll(z, n, 0)
    for r in range(n):
        out = jnp.where(row == r, before[HALO - n + r:HALO - n + r + 1, :], out)
    return out


def _shift_up(z, after, n):
    rows = z.shape[0]
    row = lax.broadcasted_iota(jnp.int32, z.shape, 0)
    out = pltpu.roll(z, rows - n, 0)
    for r in range(n):
        out = jnp.where(row == rows - n + r, after[r:r + 1, :], out)
    return out


def _conv_rows(z, before, w):
    z1 = _shift_down(z, before, 1)
    z2 = _shift_down(z, before, 2)
    return w[2:3, :] * z + w[1:2, :] * z1 + w[0:1, :] * z2, z1, z2


def _conv_t_rows(dy, after, w):
    return w[2:3, :] * dy + w[1:2, :] * _shift_up(dy, after, 1) + w[0:1, :] * _shift_up(dy, after, 2)


N_ROW_CHUNKS = SEQ // ROW_CHUNK


def _ffn_mid_fwd(name, gu, conv_w):
    def body(gu_ref, w_ref, a_ref):
        w = w_ref[...]

        def chunk(c, carry):
            g = _rows(gu_ref.at[0], c)
            u = _rows(gu_ref.at[1], c)
            gc, _, _ = _conv_rows(g, _rows_before(gu_ref.at[0], c), w)
            a_ref[pl.ds(pl.multiple_of(c * ROW_CHUNK, ROW_CHUNK), ROW_CHUNK), :] = (gc * _sigmoid(gc) * u).astype(BF16)
            return carry

        lax.fori_loop(0, N_ROW_CHUNKS, chunk, 0)

    return pl.pallas_call(
        body, name=name, grid=(FF_BLOCKS,),
        in_specs=[_spec((2, None, SEQ, FF_BLOCK), lambda j: (0, j, 0, 0)), _spec((None, 3, FF_BLOCK), lambda j: (j, 0, 0))],
        out_specs=_spec((None, SEQ, FF_BLOCK), lambda j: (j, 0, 0)),
        out_shape=jax.ShapeDtypeStruct((FF_BLOCKS, SEQ, FF_BLOCK), BF16), compiler_params=_params(("parallel",)),
    )(gu, conv_w)


def _ffn_mid_bwd(name, gu, conv_w, da):
    def body(gu_ref, w_ref, da_ref, dgu_ref, dw_ref, dgc_ref):
        w = w_ref[...]

        def first(c, acc):
            g = _rows(gu_ref.at[0], c)
            u = _rows(gu_ref.at[1], c)
            d = _rows(da_ref, c)
            gc, g1, g2 = _conv_rows(g, _rows_before(gu_ref.at[0], c), w)
            sg = _sigmoid(gc)
            rows = pl.ds(pl.multiple_of(c * ROW_CHUNK, ROW_CHUNK), ROW_CHUNK)
            dgu_ref[1, rows, :] = (d * gc * sg).astype(BF16)
            dgc = d * u * (sg * (1.0 + gc * (1.0 - sg)))
            dgc_ref[rows, :] = dgc
            return (acc[0] + jnp.sum(dgc * g2, axis=0, keepdims=True), acc[1] + jnp.sum(dgc * g1, axis=0, keepdims=True),
                    acc[2] + jnp.sum(dgc * g, axis=0, keepdims=True))

        zero = jnp.zeros((1, FF_BLOCK), F32)
        acc = lax.fori_loop(0, N_ROW_CHUNKS, first, (zero, zero, zero))
        for r in range(3):
            dw_ref[r:r + 1, :] = acc[r]

        def second(c, carry):
            dgc = _rows(dgc_ref, c)
            dg = _conv_t_rows(dgc, _rows_after(dgc_ref, c, N_ROW_CHUNKS), w)
            dgu_ref[0, pl.ds(pl.multiple_of(c * ROW_CHUNK, ROW_CHUNK), ROW_CHUNK), :] = dg.astype(BF16)
            return carry

        lax.fori_loop(0, N_ROW_CHUNKS, second, 0)

    pair = _spec((2, None, SEQ, FF_BLOCK), lambda j: (0, j, 0, 0))
    wspec = _spec((None, 3, FF_BLOCK), lambda j: (j, 0, 0))
    return pl.pallas_call(
        body, name=name, grid=(FF_BLOCKS,),
        in_specs=[pair, wspec, _spec((None, SEQ, FF_BLOCK), lambda j: (j, 0, 0))], out_specs=[pair, wspec],
        out_shape=[jax.ShapeDtypeStruct((2, FF_BLOCKS, SEQ, FF_BLOCK), BF16), jax.ShapeDtypeStruct((FF_BLOCKS, 3, FF_BLOCK), F32)],
        scratch_shapes=[pltpu.VMEM((SEQ, FF_BLOCK), F32)], compiler_params=_params(("parallel",)),
    )(gu, conv_w, da)


SC_COLS = 256
N_SC = D_MODEL // SC_COLS


def _sc_specs():
    return [_spec((SEQ, SC_COLS), lambda j, part=part: (0, part * N_SC + j)) for part in range(3)]


def _sc_mid_fwd(p, conv_w):
    def body(b_ref, c_ref, h_ref, w_ref, y_ref):
        w = w_ref[...]

        def chunk(c, carry):
            z = _rows(c_ref, c) * _rows(h_ref, c)
            before = _rows_before(c_ref, c) * _rows_before(h_ref, c)
            zc, _, _ = _conv_rows(z, before, w)
            y_ref[pl.ds(pl.multiple_of(c * ROW_CHUNK, ROW_CHUNK), ROW_CHUNK), :] = (_rows(b_ref, c) * zc).astype(BF16)
            return carry

        lax.fori_loop(0, N_ROW_CHUNKS, chunk, 0)

    col = _spec((SEQ, SC_COLS), lambda j: (0, j))
    return pl.pallas_call(
        body, name="sc_mid_fwd", grid=(N_SC,), in_specs=_sc_specs() + [_spec((3, SC_COLS), lambda j: (0, j))], out_specs=col,
        out_shape=jax.ShapeDtypeStruct((SEQ, D_MODEL), BF16), compiler_params=_params(("parallel",)),
    )(p, p, p, conv_w)


def _sc_mid_bwd(p, conv_w, dy):
    def body(b_ref, c_ref, h_ref, w_ref, dy_ref, db_ref, dc_ref, dh_ref, dw_ref, dzc_ref):
        w = w_ref[...]

        def first(c, acc):
            z = _rows(c_ref, c) * _rows(h_ref, c)
            before = _rows_before(c_ref, c) * _rows_before(h_ref, c)
            zc, z1, z2 = _conv_rows(z, before, w)
            d = _rows(dy_ref, c)
            rows = pl.ds(pl.multiple_of(c * ROW_CHUNK, ROW_CHUNK), ROW_CHUNK)
            db_ref[rows, :] = (d * zc).astype(BF16)
            dzc = d * _rows(b_ref, c)
            dzc_ref[rows, :] = dzc
            return (acc[0] + jnp.sum(dzc * z2, axis=0, keepdims=True), acc[1] + jnp.sum(dzc * z1, axis=0, keepdims=True),
                    acc[2] + jnp.sum(dzc * z, axis=0, keepdims=True))

        zero = jnp.zeros((1, SC_COLS), F32)
        acc = lax.fori_loop(0, N_ROW_CHUNKS, first, (zero, zero, zero))
        for r in range(3):
            dw_ref[r:r + 1, :] = acc[r]

        def second(c, carry):
            dz = _conv_t_rows(_rows(dzc_ref, c), _rows_after(dzc_ref, c, N_ROW_CHUNKS), w)
            rows = pl.ds(pl.multiple_of(c * ROW_CHUNK, ROW_CHUNK), ROW_CHUNK)
            dc_ref[rows, :] = (dz * _rows(h_ref, c)).astype(BF16)
            dh_ref[rows, :] = (dz * _rows(c_ref, c)).astype(BF16)
            return carry

        lax.fori_loop(0, N_ROW_CHUNKS, second, 0)

    col = _spec((SEQ, SC_COLS), lambda j: (0, j))
    wspec = _spec((3, SC_COLS), lambda j: (0, j))
    act = jax.ShapeDtypeStruct((SEQ, D_MODEL), BF16)
    return pl.pallas_call(
        body, name="sc_mid_bwd", grid=(N_SC,), in_specs=_sc_specs() + [wspec, col], out_specs=[col, col, col, wspec],
        out_shape=[act, act, act, jax.ShapeDtypeStruct((3, D_MODEL), F32)],
        scratch_shapes=[pltpu.VMEM((SEQ, SC_COLS), F32)], compiler_params=_params(("parallel",)),
    )(p, p, p, conv_w, dy)


GLA_GROUP = 4
GLA_ROWS = GLA_GROUP * CHUNK
N_GROUPS = N_CHUNKS // GLA_GROUP
Q0, K0, V0, R0, G0 = 0, KEY_DIM, 2 * KEY_DIM, 2 * KEY_DIM + VALUE_DIM, 2 * KEY_DIM + 2 * VALUE_DIM


def _tri(strict):
    r = lax.broadcasted_iota(jnp.int32, (CHUNK, CHUNK), 0)
    c = lax.broadcasted_iota(jnp.int32, (CHUNK, CHUNK), 1)
    return jnp.where(c < r if strict else c <= r, 1.0, 0.0).astype(F32)


def _cumsum_rows(tri, x):
    return jnp.dot(tri, x, preferred_element_type=F32, precision=lax.Precision.HIGHEST)


def _gate_logits(gl, wgu, b_gate):
    return jnp.dot(gl, wgu, preferred_element_type=F32) + b_gate


def _log_decay(logits):
    return (jnp.minimum(logits, 0.0) - jnp.log(1.0 + jnp.exp(-jnp.abs(logits)))) * (1.0 / GATE_NORMALIZER)


def _head(x, h, width):
    return x[:, h * width:(h + 1) * width]


def _gla_fwd(proj, wgu, b_gate, gn):
    def body(p_ref, wgu_ref, b_ref, gn_ref, o_ref, og_ref, st_ref, state):
        @pl.when(pl.program_id(0) == 0)
        def _():
            state[...] = jnp.zeros_like(state)

        tri = _tri(False)
        la = _log_decay(_gate_logits(p_ref[:, G0:G0 + GATE_PAD], wgu_ref[...], b_ref[...]))
        for c in range(GLA_GROUP):
            rows = slice(c * CHUNK, (c + 1) * CHUNK)
            cum = _cumsum_rows(tri, la[rows])
            tot = cum[CHUNK - 1:CHUNK, :]
            kd = (p_ref[rows, K0:K0 + KEY_DIM].astype(F32) * jnp.exp(tot - cum)).astype(BF16)
            decay = jnp.exp(tot)
            q = (p_ref[rows, Q0:Q0 + KEY_DIM].astype(F32) * (HEAD_K ** -0.5)).astype(BF16)
            v = p_ref[rows, V0:V0 + VALUE_DIM]
            for h in range(GLA_HEADS):
                upd = lax.dot_general(_head(v, h, HEAD_V), _head(kd, h, HEAD_K), (TN, ((), ())), preferred_element_type=F32)
                s = state[h] * _head(decay, h, HEAD_K) + upd
                state[h] = s
                st_ref[c, h] = s
                o_ref[rows, h * HEAD_V:(h + 1) * HEAD_V] = lax.dot_general(
                    _head(q, h, HEAD_K), s.astype(BF16), (NT, ((), ())), preferred_element_type=F32)
        r = p_ref[:, R0:R0 + VALUE_DIM].astype(F32)
        gate = r * _sigmoid(r) * gn_ref[...]
        for h in range(GLA_HEADS):
            cols = slice(h * HEAD_V, (h + 1) * HEAD_V)
            o = o_ref[:, cols]
            og_ref[:, cols] = (o * _rstd(o) * gate[:, cols]).astype(BF16)

    rows = _spec((GLA_ROWS, VALUE_DIM), lambda i: (i, 0))
    const = lambda shape: _spec(shape, lambda i: (0,) * len(shape))
    return pl.pallas_call(
        body, name="gla_fwd", grid=(N_GROUPS,),
        in_specs=[_spec((GLA_ROWS, PROJ_A_PAD), lambda i: (i, 0)), const((GATE_PAD, KEY_DIM)), const((1, KEY_DIM)),
                  const((1, VALUE_DIM))],
        out_specs=[rows, rows, _spec((GLA_GROUP, GLA_HEADS, HEAD_V, HEAD_K), lambda i: (i, 0, 0, 0))],
        out_shape=[jax.ShapeDtypeStruct((SEQ, VALUE_DIM), F32), jax.ShapeDtypeStruct((SEQ, VALUE_DIM), BF16),
                   jax.ShapeDtypeStruct((N_CHUNKS, GLA_HEADS, HEAD_V, HEAD_K), F32)],
        scratch_shapes=[pltpu.VMEM((GLA_HEADS, HEAD_V, HEAD_K), F32)], compiler_params=_params(("arbitrary",)),
    )(proj, wgu, b_gate, gn)


def _gla_bwd(proj, wgu, b_gate, gn, o, states, dog):
    last = N_GROUPS - 1

    def body(p_ref, wgu_ref, b_ref, gn_ref, o_ref, st_ref, stp_ref, dog_ref, dp_ref, dwgu_ref, db_ref, dgn_ref, carry, do_buf):
        step = pl.program_id(0)

        @pl.when(step == 0)
        def _():
            carry[...] = jnp.zeros_like(carry)

        r = p_ref[:, R0:R0 + VALUE_DIM].astype(F32)
        sr = _sigmoid(r)
        silu = r * sr
        gn_row = gn_ref[...]
        dog_rows = dog_ref[...].astype(F32)
        dn = dog_rows * silu
        dgn_cols = []
        for h in range(GLA_HEADS):
            cols = slice(h * HEAD_V, (h + 1) * HEAD_V)
            oh = o_ref[:, cols]
            rs = _rstd(oh)
            ohat = oh * rs
            dn_h = dn[:, cols]
            dgn_cols.append(jnp.sum(dn_h * ohat, axis=0, keepdims=True))
            dohat = dn_h * gn_row[:, cols]
            do_buf[:, cols] = rs * (dohat - ohat * jnp.mean(dohat * ohat, axis=-1, keepdims=True))
            n_h = ohat * gn_row[:, cols]
            dp_ref[:, R0 + h * HEAD_V:R0 + (h + 1) * HEAD_V] = (
                dog_rows[:, cols] * n_h * (sr[:, cols] * (1.0 + r[:, cols] * (1.0 - sr[:, cols])))).astype(BF16)
        dgn = jnp.concatenate(dgn_cols, axis=1)

        tri = _tri(False)
        tri_strict = _tri(True)
        gl = p_ref[:, G0:G0 + GATE_PAD]
        logits = _gate_logits(gl, wgu_ref[...], b_ref[...])
        la = _log_decay(logits)
        dlogit_rows = []
        for c in reversed(range(GLA_GROUP)):
            rows = slice(c * CHUNK, (c + 1) * CHUNK)
            cum = _cumsum_rows(tri, la[rows])
            tot = cum[CHUNK - 1:CHUNK, :]
            fade = jnp.exp(tot - cum)
            k = p_ref[rows, K0:K0 + KEY_DIM].astype(F32)
            kd32 = k * fade
            kd = kd32.astype(BF16)
            decay = jnp.exp(tot)
            q = (p_ref[rows, Q0:Q0 + KEY_DIM].astype(F32) * (HEAD_K ** -0.5)).astype(BF16)
            v = p_ref[rows, V0:V0 + VALUE_DIM]
            do = do_buf[rows, :].astype(BF16)
            dkd_cols, ddecay_cols = [], []
            for h in range(GLA_HEADS):
                do_h = _head(do, h, HEAD_V)
                s = st_ref[c, h]
                dq = jnp.dot(do_h, s.astype(BF16), preferred_element_type=F32) * (HEAD_K ** -0.5)
                dp_ref[rows, Q0 + h * HEAD_K:Q0 + (h + 1) * HEAD_K] = dq.astype(BF16)
                g = carry[h] + lax.dot_general(do_h, _head(q, h, HEAD_K), (TN, ((), ())), preferred_element_type=F32)
                g16 = g.astype(BF16)
                dkd_cols.append(jnp.dot(_head(v, h, HEAD_V), g16, preferred_element_type=F32))
                dv = lax.dot_general(_head(kd, h, HEAD_K), g16, (NT, ((), ())), preferred_element_type=F32)
                dp_ref[rows, V0 + h * HEAD_V:V0 + (h + 1) * HEAD_V] = dv.astype(BF16)
                if c > 0:
                    s_prev = st_ref[c - 1, h]
                else:
                    s_prev = jnp.where(step < last, stp_ref[0, h], 0.0)
                ddecay_cols.append(jnp.sum(g * s_prev, axis=0, keepdims=True))
                carry[h] = g * _head(decay, h, HEAD_K)
            dkd = jnp.concatenate(dkd_cols, axis=1)
            ddecay = jnp.concatenate(ddecay_cols, axis=1)
            dp_ref[rows, K0:K0 + KEY_DIM] = (dkd * fade).astype(BF16)
            e = dkd * kd32
            dla = ddecay * decay + _cumsum_rows(tri_strict, e)
            dlogit_rows.append(dla * (1.0 / GATE_NORMALIZER) * (1.0 - _sigmoid(logits[rows])))
        dlogit = jnp.concatenate(dlogit_rows[::-1], axis=0)
        dlogit16 = dlogit.astype(BF16)
        dp_ref[:, G0:G0 + GATE_PAD] = lax.dot_general(
            dlogit16, wgu_ref[...], (NT, ((), ())), preferred_element_type=F32).astype(BF16)
        dwgu = lax.dot_general(gl, dlogit16, (TN, ((), ())), preferred_element_type=F32)
        db = jnp.sum(dlogit, axis=0, keepdims=True)

        @pl.when(step == 0)
        def _():
            dwgu_ref[...] = dwgu
            db_ref[...] = db
            dgn_ref[...] = dgn

        @pl.when(step > 0)
        def _():
            dwgu_ref[...] += dwgu
            db_ref[...] += db
            dgn_ref[...] += dgn

    rev = lambda i: (last - i, 0)
    rows = _spec((GLA_ROWS, VALUE_DIM), rev)
    const = lambda shape: _spec(shape, lambda i: (0,) * len(shape))
    st_shape = (GLA_HEADS, HEAD_V, HEAD_K)
    return pl.pallas_call(
        body, name="gla_bwd", grid=(N_GROUPS,),
        in_specs=[_spec((GLA_ROWS, PROJ_A_PAD), rev), const((GATE_PAD, KEY_DIM)), const((1, KEY_DIM)), const((1, VALUE_DIM)),
                  rows, _spec((GLA_GROUP,) + st_shape, lambda i: (last - i, 0, 0, 0)),
                  _spec((1,) + st_shape, lambda i: (jnp.maximum((last - i) * GLA_GROUP - 1, 0), 0, 0, 0)), rows],
        out_specs=[_spec((GLA_ROWS, PROJ_A_PAD), rev), const((GATE_PAD, KEY_DIM)), const((1, KEY_DIM)), const((1, VALUE_DIM))],
        out_shape=[jax.ShapeDtypeStruct((SEQ, PROJ_A_PAD), BF16), jax.ShapeDtypeStruct((GATE_PAD, KEY_DIM), F32),
                   jax.ShapeDtypeStruct((1, KEY_DIM), F32), jax.ShapeDtypeStruct((1, VALUE_DIM), F32)],
        scratch_shapes=[pltpu.VMEM(st_shape, F32), pltpu.VMEM((GLA_ROWS, VALUE_DIM), F32)],
        compiler_params=_params(("arbitrary",)),
    )(proj, wgu, b_gate, gn, o, states, states, dog)


def _ffn_fwd(tag, x, gamma, w_up, conv_w, w_down):
    h = _norm_fwd(f"ffn{tag}_norm", x, gamma)
    gu = _proj_blocks_nn(f"ffn{tag}_up", h, w_up, FF_BLOCK, False).reshape(2, FF_BLOCKS, SEQ, FF_BLOCK)
    a = _ffn_mid_fwd(f"ffn{tag}_mid", gu, conv_w)
    return _down_nn(f"ffn{tag}_down", a, w_down, x), (h, gu, a)


def _ffn_bwd(tag, x, gamma, w_up, conv_w, w_down, saved, dx):
    h, gu, a = saved
    da = _back_blocks_nt(f"ffn{tag}_da", dx, w_down)
    d_w_down = _wgrad_a_blocks_tn(f"ffn{tag}_dwdown", a, dx)
    dgu, d_conv = _ffn_mid_bwd(f"ffn{tag}_mid_bwd", gu, conv_w, da)
    dgu = dgu.reshape(2 * FF_BLOCKS, SEQ, FF_BLOCK)
    dh = _back_sum_blocks_nt(f"ffn{tag}_dh", dgu, w_up)
    d_w_up = _wgrad_d_blocks_tn(f"ffn{tag}_dwup", h, dgu)
    dx, d_gamma = _norm_bwd(f"ffn{tag}_norm_bwd", x, gamma, dh, dx)
    return dx, d_gamma, d_w_up, d_conv, d_w_down


def _local_step(x, target, w, fetch=None, emit=None):
    if fetch is None:
        local = dict(a=(w.get("a_w_in"), w.get("a_w_out")), b=(w.get("b_w_in"), w.get("b_w_out")))
        for layer in range(2):
            local[f"f{layer}"] = (w["f_w_up"][layer], w["f_w_down"][layer]) if "f_w_up" in w else None
        fetch = lambda group, after: local[group]
    if emit is None:
        emit = lambda group, grads, dx: dx
    f_norm = (w["f_norm"][0:1], w["f_norm"][1:2])

    x0 = x
    a_w_in, a_w_out = fetch("a", x0)
    h0 = _norm_fwd("a_norm", x0, w["a_norm"])
    proj = _proj_nn("a_in", h0, a_w_in, PA_TILE, N_PA)
    o, og, states = _gla_fwd(proj, w["a_w_gate_up"], w["a_b_gate"], w["a_gn"])
    x1 = _out_nn("a_out", og, a_w_out, x0)
    up0, down0 = fetch("f0", x1)
    x2, ffn0 = _ffn_fwd(0, x1, f_norm[0], up0, w["f_conv"][0], down0)
    b_w_in, b_w_out = fetch("b", x2)
    h2 = _norm_fwd("b_norm", x2, w["b_norm"])
    p = _proj_blocks_nn("b_in", h2, b_w_in, B_SHARD, True)
    y = _sc_mid_fwd(p, w["b_conv"])
    x3 = _out_nn("b_out", y, b_w_out, x2)
    up1, down1 = fetch("f1", x3)
    x4, ffn1 = _ffn_fwd(1, x3, f_norm[1], up1, w["f_conv"][1], down1)
    loss, dx, d_final_norm = _loss_head(x4, w["final_norm"], target)

    dx, d_f_norm1, d_up1, d_fconv1, d_down1 = _ffn_bwd(1, x3, f_norm[1], up1, w["f_conv"][1], down1, ffn1, dx)
    dx = emit("f1", (d_up1, d_down1), dx)

    dy = _back_nt("b_dy", dx, b_w_out)
    d_b_w_out = _wgrad_tn("b_dwout", y, D_MODEL, dx, D_MODEL, False)
    db, dc, dhh, d_b_conv = _sc_mid_bwd(p, w["b_conv"], dy)
    dp = jnp.concatenate([db, dc, dhh], axis=1)
    dh2 = _back_sum_cols_nt("b_dh", dp, w_blocks=b_w_in, n_tile=B_SHARD)
    d_b_w_in = _wgrad_tn("b_dwin", h2, D_MODEL, dp, B_SHARD, True)
    dx, d_b_norm = _norm_bwd("b_norm_bwd", x2, w["b_norm"], dh2, dx)
    dx = emit("b", (d_b_w_in, d_b_w_out), dx)

    dx, d_f_norm0, d_up0, d_fconv0, d_down0 = _ffn_bwd(0, x1, f_norm[0], up0, w["f_conv"][0], down0, ffn0, dx)
    dx = emit("f0", (d_up0, d_down0), dx)

    dog = _back_nt("a_dog", dx, a_w_out)
    d_a_w_out = _wgrad_tn("a_dwout", og, D_MODEL, dx, D_MODEL, False)
    dproj, d_wgu, d_b_gate, d_gn = _gla_bwd(proj, w["a_w_gate_up"], w["a_b_gate"], w["a_gn"], o, states, dog)
    dh0 = _back_sum_cols_nt("a_dh", dproj, w=a_w_in, n_tile=PA_TILE)
    d_a_w_in = _wgrad_tn("a_dwin", h0, D_MODEL, dproj, PA_TILE, False)
    dx, d_a_norm = _norm_bwd("a_norm_bwd", x0, w["a_norm"], dh0, dx)

    grads = dict(
        a_norm=d_a_norm, a_w_in=d_a_w_in, a_w_gate_up=d_wgu, a_b_gate=d_b_gate, a_gn=d_gn, a_w_out=d_a_w_out,
        b_norm=d_b_norm, b_w_in=d_b_w_in, b_conv=d_b_conv, b_w_out=d_b_w_out,
        f_norm=(d_f_norm0, d_f_norm1), f_w_up=(d_up0, d_up1), f_conv=(d_fconv0, d_fconv1), f_w_down=(d_down0, d_down1),
        final_norm=d_final_norm)
    return loss[0, 0], dx, grads


MESH_ID = pl.DeviceIdType.MESH
ANY = pl.BlockSpec(memory_space=pl.ANY)
N_PEERS = N_DEV - 1


def _position():
    return lax.axis_index("x"), lax.axis_index("y"), lax.axis_index("c")


def _slot(px, py, pc):
    return 4 * px + 2 * py + pc


def _all_gather(shards):
    n = len(shards)

    def body(*refs):
        ins, outs = refs[:n], refs[n:2 * n]
        send_sems, recv_sems, local_sems = refs[2 * n:]
        x, y, c = _position()
        me, sibling = (x, y, c), (x, y, 1 - c)
        chips = [(1 - x, y), (x, 1 - y), (1 - x, 1 - y)]

        def copy(t, k, block, to, from_input=False):
            dst = outs[t].at[_slot(*block)]
            return pltpu.make_async_remote_copy(
                src_ref=ins[t] if from_input else dst, dst_ref=dst, send_sem=send_sems.at[t, k], recv_sem=recv_sems.at[t, k],
                device_id=to, device_id_type=MESH_ID)

        mine = [pltpu.make_async_copy(ins[t], outs[t].at[_slot(*me)], local_sems.at[t]) for t in range(n)]
        for cp in mine:
            cp.start()
        first = []
        for t in range(n):
            first.append(copy(t, 0, me, sibling, True))
            first += [copy(t, 1 + j, me, (*chip, c), True) for j, chip in enumerate(chips)]
        for cp in first:
            cp.start()
        passed = []
        for t in range(n):
            for j, chip in enumerate(chips):
                copy(t, 1 + j, (*chip, c), me).wait_recv()
                fwd = copy(t, 4 + j, (*chip, c), sibling)
                fwd.start()
                passed.append(fwd)
        for t in range(n):
            copy(t, 0, sibling, me).wait_recv()
            for j, chip in enumerate(chips):
                copy(t, 4 + j, (*chip, 1 - c), me).wait_recv()
        for cp in first + passed:
            cp.wait_send()
        for cp in mine:
            cp.wait()

    return pl.pallas_call(
        body, name="weight_gather", in_specs=[ANY] * n, out_specs=[ANY] * n,
        out_shape=[jax.ShapeDtypeStruct((N_DEV,) + s.shape, s.dtype) for s in shards],
        scratch_shapes=[pltpu.SemaphoreType.DMA((n, N_PEERS)), pltpu.SemaphoreType.DMA((n, N_PEERS)), pltpu.SemaphoreType.DMA((n,))],
    )(*shards)


HBM = pl.BlockSpec(memory_space=pltpu.HBM)
SEM = pl.BlockSpec(memory_space=pltpu.SEMAPHORE)
DATAFLOW = pltpu.SideEffectType.DATAFLOW_SIDE_EFFECTING
ALL_PEERS = (1, 2, 3, 4, 5, 6, 7)
SIBLING_AND_SAME_CORE = (1, 2, 4, 6)
SAME_CORE = (2, 4, 6)


def _flip(x, y, c, k):
    return x ^ (k >> 2), y ^ ((k >> 1) & 1), c ^ (k & 1)


def _in_hbm(a):
    return pltpu.with_memory_space_constraint(a, pltpu.HBM)


def _send_copy(parts, landing, shared, send_sems, recv_sems, t, s, k):
    x, y, c = _position()
    peer = _flip(x, y, c, k)
    src = parts[t] if shared[t] else parts[t].at[_slot(*peer)]
    return pltpu.make_async_remote_copy(
        src_ref=src, dst_ref=landing[t].at[_slot(x, y, c)], send_sem=send_sems.at[s], recv_sem=recv_sems.at[s],
        device_id=peer, device_id_type=MESH_ID)


def _send_arrival(landing, send_sems, recv_sems, t, s, k):
    x, y, c = _position()
    peer = _flip(x, y, c, k)
    landed = landing[t].at[_slot(*peer)]
    return pltpu.make_async_remote_copy(
        src_ref=landed, dst_ref=landed, send_sem=send_sems.at[s], recv_sem=recv_sems.at[s],
        device_id=peer, device_id_type=MESH_ID)


def _send_start(name, parts, shared, peers, carry):
    n = len(parts)

    def body(*refs):
        parts_in, landing_in = refs[:n], refs[n:2 * n]
        send_sems, recv_sems = refs[2 * n + 1], refs[2 * n + 2]
        local_sems = refs[-1]
        my_slot = _slot(*_position())
        for t in range(n):
            for j, k in enumerate(peers):
                _send_copy(parts_in, landing_in, shared, send_sems, recv_sems, t, t * len(peers) + j, k).start()
        mine = [pltpu.make_async_copy(parts_in[t] if shared[t] else parts_in[t].at[my_slot], landing_in[t].at[my_slot],
                                      local_sems.at[t]) for t in range(n)]
        for cp in mine:
            cp.start()
        for cp in mine:
            cp.wait()

    landing = [lax.empty(((N_DEV,) + p.shape) if sh else p.shape, p.dtype) for p, sh in zip(parts, shared)]
    operands = [_in_hbm(a) for a in list(parts) + landing + [carry]]
    sems = pltpu.SemaphoreType.DMA((n * len(peers),))
    out = pl.pallas_call(
        body, name=name, in_specs=[HBM] * (2 * n + 1), out_specs=[SEM, SEM] + [HBM] * (2 * n + 1),
        out_shape=[sems, sems] + [pltpu.HBM(a.shape, a.dtype) for a in operands],
        input_output_aliases={i: 2 + i for i in range(2 * n + 1)}, scratch_shapes=[pltpu.SemaphoreType.DMA((n,))],
        compiler_params=pltpu.CompilerParams(has_side_effects=DATAFLOW),
    )(*operands)
    handle = dict(sems=out[:2], parts=out[2:2 + n], landing=out[2 + n:2 + 2 * n], shared=shared, peers=peers)
    return handle, out[2 + 2 * n]


def _send_wait(name, handle, after):
    n = len(handle["parts"])
    shared, peers = handle["shared"], handle["peers"]

    def body(*refs):
        parts_in, landing_in = refs[:n], refs[n:2 * n]
        send_sems, recv_sems = refs[2 * n], refs[2 * n + 1]
        for t in range(n):
            for j, k in enumerate(peers):
                s = t * len(peers) + j
                _send_copy(parts_in, landing_in, shared, send_sems, recv_sems, t, s, k).wait_send()
                _send_arrival(landing_in, send_sems, recv_sems, t, s, k).wait_recv()

    operands = list(handle["parts"]) + list(handle["landing"])
    out = pl.pallas_call(
        body, name=name, in_specs=[HBM] * (2 * n) + [SEM, SEM, ANY], out_specs=[HBM] * (2 * n),
        out_shape=[pltpu.HBM(a.shape, a.dtype) for a in operands], input_output_aliases={i: i for i in range(2 * n)},
        compiler_params=pltpu.CompilerParams(has_side_effects=DATAFLOW),
    )(*operands, *handle["sems"], after)
    return out[n:]


def _forward_to_sibling(name, gathered):
    n = len(gathered)

    def body(*refs):
        ins, outs = refs[:n], refs[n:2 * n]
        send_sems, recv_sems = refs[2 * n:]
        x, y, c = _position()
        sent = []
        for t in range(n):
            for j, k in enumerate(SAME_CORE):
                block = _slot(*_flip(x, y, c, k))
                sent.append(pltpu.make_async_remote_copy(
                    src_ref=ins[t].at[block], dst_ref=outs[t].at[block], send_sem=send_sems.at[t, j], recv_sem=recv_sems.at[t, j],
                    device_id=(x, y, 1 - c), device_id_type=MESH_ID))
        for cp in sent:
            cp.start()
        for t in range(n):
            for j, k in enumerate(SAME_CORE):
                landed = outs[t].at[_slot(*_flip(x, y, 1 - c, k))]
                pltpu.make_async_remote_copy(
                    src_ref=landed, dst_ref=landed, send_sem=send_sems.at[t, j], recv_sem=recv_sems.at[t, j],
                    device_id=(x, y, 1 - c), device_id_type=MESH_ID).wait_recv()
        for cp in sent:
            cp.wait_send()

    sems = pltpu.SemaphoreType.DMA((n, len(SAME_CORE)))
    return pl.pallas_call(
        body, name=name, in_specs=[ANY] * n, out_specs=[ANY] * n,
        out_shape=[jax.ShapeDtypeStruct(a.shape, a.dtype) for a in gathered], input_output_aliases={i: i for i in range(n)},
        scratch_shapes=[sems, sems],
    )(*gathered)


ADAM_ROWS = 256


def _adam_update(w, g, m, v):
    m = ADAM_B1 * m + (1.0 - ADAM_B1) * g
    v = ADAM_B2 * v + (1.0 - ADAM_B2) * (g * g)
    m_hat = m / (1.0 - ADAM_B1 ** ADAM_STEP)
    v_hat = v / (1.0 - ADAM_B2 ** ADAM_STEP)
    delta = -ADAM_LR * (m_hat / (jnp.sqrt(v_hat) + ADAM_EPS) + ADAM_WD * w)
    return delta, m, v


def _sum_slots(ref):
    total = ref[0].astype(F32)
    for d in range(1, N_DEV):
        total = total + ref[d].astype(F32)
    return total


def _adamw_sum(name, landed, w, m, v):
    layers, rows, cols = w.shape
    tr = ADAM_ROWS if rows % ADAM_ROWS == 0 else rows
    nt = rows // tr

    def body(*refs):
        parts = refs[:layers]
        w_ref, m_ref, v_ref, g_ref, d_ref, nm_ref, nv_ref = refs[layers:]
        layer = pl.program_id(0)
        g = _sum_slots(parts[0])
        for q in range(1, layers):
            g = jnp.where(layer == q, _sum_slots(parts[q]), g)
        delta, new_m, new_v = _adam_update(w_ref[...], g, m_ref[...], v_ref[...])
        g_ref[...] = g
        d_ref[...] = delta
        nm_ref[...] = new_m
        nv_ref[...] = new_v

    def part_spec(q):
        return _spec((N_DEV, tr, cols), lambda l, i: (0, jnp.where(l == q, i, jnp.where(l < q, 0, nt - 1)), 0))

    tile = _spec((None, tr, cols), lambda l, i: (l, i, 0))
    out = jax.ShapeDtypeStruct((layers, rows, cols), F32)
    return pl.pallas_call(
        body, name=name, grid=(layers, nt), in_specs=[part_spec(q) for q in range(layers)] + [tile] * 3,
        out_specs=[tile] * 4, out_shape=[out] * 4, compiler_params=_params(("arbitrary", "arbitrary")),
    )(*landed, w, m, v)


def _sum_small(landed):
    def body(in_ref, out_ref):
        out_ref[...] = _sum_slots(in_ref)

    return pl.pallas_call(body, name="small_grad_sum", out_shape=jax.ShapeDtypeStruct(landed.shape[1:], F32))(landed)


def _adamw_small(name, g, w, m, v):
    def body(g_ref, w_ref, m_ref, v_ref, d_ref, nm_ref, nv_ref):
        d_ref[...], nm_ref[...], nv_ref[...] = _adam_update(w_ref[...], g_ref[...], m_ref[...], v_ref[...])

    out = jax.ShapeDtypeStruct(w.shape, F32)
    return pl.pallas_call(body, name=name, out_shape=[out] * 3)(g, w, m, v)


LANES = 128
SUBLANES = 8
F_CONV_SHARD = D_FF // N_DEV
GATE_SHARD = KEY_DIM // N_DEV
NORM_SHARD = D_MODEL // N_DEV


def _tile_rows(a):
    flat = a.reshape(-1)
    size = -(-flat.shape[0] // (SUBLANES * LANES)) * SUBLANES * LANES
    return jnp.pad(flat, (0, size - flat.shape[0])).reshape(-1, LANES)


def _pack_rows(pieces):
    return jnp.concatenate([_tile_rows(p) for p in pieces], axis=0)


def _unpack_rows(packed, shapes):
    out, row = [], 0
    for shape in shapes:
        size = 1
        for s in shape:
            size *= s
        rows = -(-size // (SUBLANES * LANES)) * SUBLANES
        piece = packed[..., row:row + rows, :]
        out.append(piece.reshape(piece.shape[:-2] + (rows * LANES,))[..., :size])
        row += rows
    return out


SMALL_SHARDS = ((GATE_RANK, GATE_SHARD), (1, NORM_SHARD), (3, NORM_SHARD), (2, 3, F_CONV_SHARD))


def _unpack_small_shards(g):
    gate, b_norm, b_conv, f_conv = _unpack_rows(g, SMALL_SHARDS)
    gate = gate.reshape(N_DEV, GATE_RANK, GATE_SHARD).transpose(1, 0, 2).reshape(GATE_RANK, KEY_DIM)
    b_norm = b_norm.reshape(1, D_MODEL)
    b_conv = b_conv.reshape(N_DEV, 3, NORM_SHARD).transpose(1, 0, 2).reshape(3, D_MODEL)
    f_conv = f_conv.reshape(N_DEV, 2, 3, F_CONV_SHARD).transpose(1, 2, 0, 3).reshape(2, 3, D_FF)
    return gate, b_norm, b_conv, f_conv


def _conv_blocks(f_conv):
    return f_conv.reshape(2, 3, FF_BLOCKS, FF_BLOCK).transpose(0, 2, 1, 3)


def _conv_unblocks(f_conv):
    return f_conv.transpose(1, 0, 2).reshape(3, D_FF)


SMALL_LAYOUT = (("a_norm", (1, D_MODEL)), ("a_w_gate_up", (GATE_RANK, KEY_DIM)), ("a_b_gate", (1, KEY_DIM)), ("a_gn", (1, VALUE_DIM)),
                ("b_norm", (1, D_MODEL)), ("b_conv", (3, D_MODEL)), ("f_norm0", (1, D_MODEL)), ("f_norm1", (1, D_MODEL)),
                ("f_conv0", (3, D_FF)), ("f_conv1", (3, D_FF)), ("final_norm", (1, D_MODEL)))


def _pack_small_grads(g):
    full = dict(g)
    full["a_w_gate_up"] = g["a_w_gate_up"][:GATE_RANK]
    for layer in range(2):
        full[f"f_norm{layer}"] = g["f_norm"][layer]
        full[f"f_conv{layer}"] = _conv_unblocks(g["f_conv"][layer])
    return _pack_rows([full[name] for name, _ in SMALL_LAYOUT])


def _unpack_small_grads(packed):
    pieces = _unpack_rows(packed, [shape for _, shape in SMALL_LAYOUT])
    out = {name: piece.reshape(shape) for (name, shape), piece in zip(SMALL_LAYOUT, pieces)}
    out["f_norm"] = jnp.stack([out["f_norm0"][0], out["f_norm1"][0]])
    out["f_conv"] = jnp.stack([out["f_conv0"], out["f_conv1"]])
    return out


def kernel(x, a_norm, a_w_in, a_w_gate_up, a_b_gate, a_gn, a_w_out, b_norm, b_w_in, b_conv, b_w_out, f_norm, f_w_up, f_conv, f_w_down, final_norm, loss_target, m_a_norm, m_a_w_in, m_a_w_gate_up, m_a_b_gate, m_a_gn, m_a_w_out, m_b_norm, m_b_w_in, m_b_conv, m_b_w_out, m_f_norm, m_f_w_up, m_f_conv, m_f_w_down, m_final_norm, v_a_norm, v_a_w_in, v_a_w_gate_up, v_a_b_gate, v_a_gn, v_a_w_out, v_b_norm, v_b_w_in, v_b_conv, v_b_w_out, v_f_norm, v_f_w_up, v_f_conv, v_f_w_down, v_final_norm):
    my_slot = _slot(*_position())

    first = _all_gather([a_w_in[0].astype(BF16), a_w_out[0].astype(BF16), _pack_rows([a_w_gate_up[0], b_norm, b_conv[0], f_conv])])
    carry, gathers = first[0], {}
    for group, w_in, w_out in (("f0", f_w_up[0], f_w_down[0]), ("b", b_w_in[0], b_w_out[0]), ("f1", f_w_up[1], f_w_down[1])):
        gathers[group], carry = _send_start(
            f"gather_{group}_start", [w_in.astype(BF16), w_out.astype(BF16)], [True, True], SIBLING_AND_SAME_CORE, carry)
    gate_full, b_norm_full, b_conv_full, f_conv_full = _unpack_small_shards(first[2])
    a_w_in_full = jnp.pad(carry.transpose(1, 0, 2).reshape(D_MODEL, PROJ_A), ((0, 0), (0, PROJ_A_PAD - PROJ_A)))
    weights = dict(
        a_norm=a_norm, a_w_gate_up=jnp.pad(gate_full, ((0, GATE_PAD - GATE_RANK), (0, 0))).astype(BF16), a_b_gate=a_b_gate,
        a_gn=a_gn, b_norm=b_norm_full, b_conv=b_conv_full, f_norm=f_norm, f_conv=_conv_blocks(f_conv_full),
        final_norm=final_norm.reshape(1, D_MODEL))

    def fetch(group, after):
        if group == "a":
            return a_w_in_full, first[1].reshape(D_MODEL, D_MODEL)
        landed = _send_wait(f"gather_{group}_wait", gathers[group], after)
        w_in, w_out = _forward_to_sibling(f"gather_{group}_forward", landed)
        if group == "b":
            return w_in, w_out.reshape(D_MODEL, D_MODEL)
        return w_in, w_out.reshape(FF_BLOCKS, FF_BLOCK, D_MODEL)

    exchanges = {}

    def owner_blocks(d_out):
        return d_out.reshape((N_DEV, -1, D_MODEL))

    def emit(group, grads, dx):
        exchanges[group], dx = _send_start(
            f"grads_{group}_start", [grads[0], owner_blocks(grads[1])], [False, False], ALL_PEERS, dx)
        return dx

    loss, dx, g = _local_step(x[0], loss_target[0], weights, fetch, emit)
    loss = lax.psum(loss, MESH_AXES)
    parts = [g["a_w_in"][:, :PROJ_A].reshape(D_MODEL, N_DEV, A_SHARD).transpose(1, 0, 2), owner_blocks(g["a_w_out"]),
             _pack_small_grads(g)]
    exchanges["a"], _ = _send_start("grads_a_start", parts, [False, False, True], ALL_PEERS, jnp.zeros((SUBLANES, LANES), F32))

    landed_f1 = _send_wait("grads_f1_wait", exchanges["f1"], dx)
    landed_b = _send_wait("grads_b_wait", exchanges["b"], landed_f1[0])
    big = dict(
        b_w_in=_adamw_sum("adam_b_w_in", [landed_b[0]], b_w_in, m_b_w_in, v_b_w_in),
        b_w_out=_adamw_sum("adam_b_w_out", [landed_b[1]], b_w_out, m_b_w_out, v_b_w_out))
    landed_f0 = _send_wait("grads_f0_wait", exchanges["f0"], big["b_w_out"][0])
    big.update(
        f_w_up=_adamw_sum("adam_f_w_up", [landed_f0[0], landed_f1[0]], f_w_up, m_f_w_up, v_f_w_up),
        f_w_down=_adamw_sum("adam_f_w_down", [landed_f0[1], landed_f1[1]], f_w_down, m_f_w_down, v_f_w_down))
    landed_a = _send_wait("grads_a_wait", exchanges["a"], big["f_w_down"][0])
    big.update(
        a_w_in=_adamw_sum("adam_a_w_in", [landed_a[0]], a_w_in, m_a_w_in, v_a_w_in),
        a_w_out=_adamw_sum("adam_a_w_out", [landed_a[1]], a_w_out, m_a_w_out, v_a_w_out))
    small_g = _unpack_small_grads(_sum_small(landed_a[2]))
    small_g["a_w_gate_up"] = lax.dynamic_slice_in_dim(small_g["a_w_gate_up"], my_slot * GATE_SHARD, GATE_SHARD, axis=1)
    small_g["b_norm"] = lax.dynamic_slice_in_dim(small_g["b_norm"], my_slot * NORM_SHARD, NORM_SHARD, axis=1)
    small_g["b_conv"] = lax.dynamic_slice_in_dim(small_g["b_conv"], my_slot * NORM_SHARD, NORM_SHARD, axis=1)
    small_g["f_conv"] = lax.dynamic_slice_in_dim(small_g["f_conv"], my_slot * F_CONV_SHARD, F_CONV_SHARD, axis=2)
    small_w = dict(
        a_norm=(a_norm, m_a_norm, v_a_norm), a_w_gate_up=(a_w_gate_up, m_a_w_gate_up, v_a_w_gate_up),
        a_b_gate=(a_b_gate, m_a_b_gate, v_a_b_gate), a_gn=(a_gn, m_a_gn, v_a_gn), b_norm=(b_norm, m_b_norm, v_b_norm),
        b_conv=(b_conv, m_b_conv, v_b_conv), f_norm=(f_norm, m_f_norm, v_f_norm), f_conv=(f_conv, m_f_conv, v_f_conv),
        final_norm=(final_norm, m_final_norm, v_final_norm))
    small = {}
    for name, (w, m, v) in small_w.items():
        flat = (w.shape[-1],) if w.ndim == 1 else w.shape[-2:]
        two_d = (-1, flat[-1])
        grad = small_g[name].reshape(w.shape)
        delta, new_m, new_v = _adamw_small(
            "adam_" + name, grad.reshape(two_d), w.reshape(two_d), m.reshape(two_d), v.reshape(two_d))
        small[name] = (grad, delta.reshape(w.shape), new_m.reshape(w.shape), new_v.reshape(w.shape))

    order = ["a_norm", "a_w_in", "a_w_gate_up", "a_b_gate", "a_gn", "a_w_out", "b_norm", "b_w_in", "b_conv", "b_w_out",
             "f_norm", "f_w_up", "f_conv", "f_w_down", "final_norm"]
    results = {**big, **small}
    outputs = [loss, dx.reshape(1, SEQ, D_MODEL)]
    for kind in range(4):
        outputs += [results[name][kind] for name in order]
    return tuple(outputs)
```

```python
import jax
import jax.numpy as jnp
from jax import lax
from jax.experimental import pallas as pl
from jax.experimental.pallas import tpu as pltpu

F32 = jnp.float32
BF16 = jnp.bfloat16

N_DEV = 8
SEQ = 2048
D_MODEL = 1024
CHUNK = 64
N_CHUNKS = SEQ // CHUNK
RMS_EPS = 1e-6
GLA_HEADS = 4
KEY_DIM = 512
VALUE_DIM = 1024
HEAD_K = KEY_DIM // GLA_HEADS
HEAD_V = VALUE_DIM // GLA_HEADS
GATE_RANK = 16
GATE_PAD = 128
GATE_NORMALIZER = 16.0
PROJ_A = 2 * KEY_DIM + 2 * VALUE_DIM + GATE_RANK
PROJ_A_PAD = 2 * KEY_DIM + 2 * VALUE_DIM + GATE_PAD
A_SHARD = PROJ_A // N_DEV
B_SHARD = 3 * D_MODEL // N_DEV
D_FF = 2816
FF_BLOCK = 2 * D_FF // N_DEV
FF_BLOCKS = D_FF // FF_BLOCK
ADAM_LR = 0.001
ADAM_B1 = 0.9
ADAM_B2 = 0.999
ADAM_EPS = 1e-08
ADAM_WD = 0.01
ADAM_STEP = 10
MESH_AXES = ("x", "y", "c")

VMEM_LIMIT = 56 * 1024 * 1024
ROW_CHUNK = 256
HALO = 16


def _params(sem=None, vmem=VMEM_LIMIT):
    return pltpu.CompilerParams(dimension_semantics=sem, vmem_limit_bytes=vmem)


NN = ((1,), (0,))
NT = ((1,), (1,))
TN = ((0,), (0,))


def _matmul(name, a, a_spec, b, b_spec, dims, grid, nk, out_shape, out_spec, acc_shape=None, res=None, res_spec=None):
    has_res = res is not None

    def body(*refs):
        a_ref, b_ref = refs[0], refs[1]
        r_ref = refs[2] if has_res else None
        o_ref = refs[2 + has_res]
        acc_ref = refs[3 + has_res] if nk > 1 else None

        def product():
            return lax.dot_general(a_ref[...].astype(BF16), b_ref[...].astype(BF16), (dims, ((), ())),
                                   preferred_element_type=F32)

        def finish(v):
            if has_res:
                v = v + r_ref[...]
            o_ref[...] = v.astype(o_ref.dtype)

        if nk == 1:
            finish(product())
        else:
            k = pl.program_id(len(grid) - 1)
            p = product()

            @pl.when(k == 0)
            def _():
                acc_ref[...] = p

            @pl.when(k > 0)
            def _():
                acc_ref[...] += p

            @pl.when(k == nk - 1)
            def _():
                finish(acc_ref[...])

    operands = [a, b] + ([res] if has_res else [])
    in_specs = [a_spec, b_spec] + ([res_spec] if has_res else [])
    sem = ("parallel",) * (len(grid) - 1) + (("arbitrary",) if nk > 1 else ("parallel",))
    return pl.pallas_call(
        body, name=name, grid=grid, in_specs=in_specs, out_specs=out_spec, out_shape=out_shape,
        scratch_shapes=[pltpu.VMEM(acc_shape, F32)] if nk > 1 else [],
        compiler_params=_params(sem),
    )(*operands)


TM = 1024
TKS = 1024
N_TM = SEQ // TM
N_TKS = SEQ // TKS
PA_TILE = 640
N_PA = PROJ_A_PAD // PA_TILE


def _spec(shape, fn):
    return pl.BlockSpec(shape, fn)


def _proj_nn(name, h, w, n_tile, n_tiles):
    n = n_tile * n_tiles
    return _matmul(name, h, _spec((TM, D_MODEL), lambda j, i: (i, 0)), w, _spec((D_MODEL, n_tile), lambda j, i: (0, j)), NN,
                   (n_tiles, N_TM), 1, jax.ShapeDtypeStruct((SEQ, n), BF16), _spec((TM, n_tile), lambda j, i: (i, j)))


def _proj_blocks_nn(name, h, w_blocks, n_tile, flat_out):
    nb = w_blocks.shape[0]
    if flat_out:
        out_shape = jax.ShapeDtypeStruct((SEQ, nb * n_tile), BF16)
        out_spec = _spec((TM, n_tile), lambda j, i: (i, j))
    else:
        out_shape = jax.ShapeDtypeStruct((nb, SEQ, n_tile), BF16)
        out_spec = _spec((None, TM, n_tile), lambda j, i: (j, i, 0))
    return _matmul(name, h, _spec((TM, D_MODEL), lambda j, i: (i, 0)), w_blocks,
                   _spec((None, D_MODEL, n_tile), lambda j, i: (j, 0, 0)), NN, (nb, N_TM), 1, out_shape, out_spec)


def _out_nn(name, a, w, x):
    return _matmul(name, a, _spec((TM, D_MODEL), lambda i: (i, 0)), w, _spec((D_MODEL, D_MODEL), lambda i: (0, 0)), NN,
                   (N_TM,), 1, jax.ShapeDtypeStruct((SEQ, D_MODEL), F32), _spec((TM, D_MODEL), lambda i: (i, 0)),
                   res=x, res_spec=_spec((TM, D_MODEL), lambda i: (i, 0)))


def _down_nn(name, a_blocks, w_blocks, x):
    nb = a_blocks.shape[0]
    return _matmul(name, a_blocks, _spec((None, TM, FF_BLOCK), lambda i, k: (k, i, 0)), w_blocks,
                   _spec((None, FF_BLOCK, D_MODEL), lambda i, k: (k, 0, 0)), NN, (N_TM, nb), nb,
                   jax.ShapeDtypeStruct((SEQ, D_MODEL), F32), _spec((TM, D_MODEL), lambda i, k: (i, 0)),
                   acc_shape=(TM, D_MODEL), res=x, res_spec=_spec((TM, D_MODEL), lambda i, k: (i, 0)))


def _back_nt(name, dy, w):
    n = w.shape[0]
    return _matmul(name, dy, _spec((TM, D_MODEL), lambda i: (i, 0)), w, _spec((n, D_MODEL), lambda i: (0, 0)), NT,
                   (N_TM,), 1, jax.ShapeDtypeStruct((SEQ, n), BF16), _spec((TM, n), lambda i: (i, 0)))


def _back_blocks_nt(name, dy, w_blocks):
    nb = w_blocks.shape[0]
    return _matmul(name, dy, _spec((TM, D_MODEL), lambda j, i: (i, 0)), w_blocks,
                   _spec((None, FF_BLOCK, D_MODEL), lambda j, i: (j, 0, 0)), NT, (nb, N_TM), 1,
                   jax.ShapeDtypeStruct((nb, SEQ, FF_BLOCK), BF16), _spec((None, TM, FF_BLOCK), lambda j, i: (j, i, 0)))


def _back_sum_blocks_nt(name, d_blocks, w_blocks):
    nb, _, n = d_blocks.shape
    return _matmul(name, d_blocks, _spec((None, TM, n), lambda i, k: (k, i, 0)), w_blocks,
                   _spec((None, D_MODEL, n), lambda i, k: (k, 0, 0)), NT, (N_TM, nb), nb,
                   jax.ShapeDtypeStruct((SEQ, D_MODEL), F32), _spec((TM, D_MODEL), lambda i, k: (i, 0)), acc_shape=(TM, D_MODEL))


def _back_sum_cols_nt(name, d, w_blocks=None, w=None, n_tile=None):
    nb = d.shape[1] // n_tile
    if w_blocks is not None:
        b, b_spec = w_blocks, _spec((None, D_MODEL, n_tile), lambda i, k: (k, 0, 0))
    else:
        b, b_spec = w, _spec((D_MODEL, n_tile), lambda i, k: (0, k))
    return _matmul(name, d, _spec((TM, n_tile), lambda i, k: (i, k)), b, b_spec, NT, (N_TM, nb), nb,
                   jax.ShapeDtypeStruct((SEQ, D_MODEL), F32), _spec((TM, D_MODEL), lambda i, k: (i, 0)), acc_shape=(TM, D_MODEL))


def _wgrad_tn(name, a, a_cols, d, d_cols, out_blocks):
    nb = d.shape[1] // d_cols
    if out_blocks:
        out_shape = jax.ShapeDtypeStruct((nb, a_cols, d_cols), BF16)
        out_spec = _spec((None, a_cols, d_cols), lambda j, k: (j, 0, 0))
    else:
        out_shape = jax.ShapeDtypeStruct((a_cols, nb * d_cols), BF16)
        out_spec = _spec((a_cols, d_cols), lambda j, k: (0, j))
    return _matmul(name, a, _spec((TKS, a_cols), lambda j, k: (k, 0)), d, _spec((TKS, d_cols), lambda j, k: (k, j)), TN,
                   (nb, N_TKS), N_TKS, out_shape, out_spec, acc_shape=(a_cols, d_cols))


def _wgrad_a_blocks_tn(name, a_blocks, d):
    nb = a_blocks.shape[0]
    return _matmul(name, a_blocks, _spec((None, TKS, FF_BLOCK), lambda j, k: (j, k, 0)), d,
                   _spec((TKS, D_MODEL), lambda j, k: (k, 0)), TN, (nb, N_TKS), N_TKS,
                   jax.ShapeDtypeStruct((nb, FF_BLOCK, D_MODEL), BF16), _spec((None, FF_BLOCK, D_MODEL), lambda j, k: (j, 0, 0)),
                   acc_shape=(FF_BLOCK, D_MODEL))


def _wgrad_d_blocks_tn(name, a, d_blocks):
    nb = d_blocks.shape[0]
    return _matmul(name, a, _spec((TKS, D_MODEL), lambda j, k: (k, 0)), d_blocks,
                   _spec((None, TKS, FF_BLOCK), lambda j, k: (j, k, 0)), TN, (nb, N_TKS), N_TKS,
                   jax.ShapeDtypeStruct((nb, D_MODEL, FF_BLOCK), BF16), _spec((None, D_MODEL, FF_BLOCK), lambda j, k: (j, 0, 0)),
                   acc_shape=(D_MODEL, FF_BLOCK))


NORM_ROWS = 512


def _rstd(x):
    return lax.rsqrt(jnp.mean(x * x, axis=-1, keepdims=True) + RMS_EPS)


def _norm_fwd(name, x, gamma):
    def body(x_ref, g_ref, h_ref):
        x = x_ref[...]
        h_ref[...] = (x * _rstd(x) * g_ref[...]).astype(BF16)

    row = _spec((NORM_ROWS, D_MODEL), lambda i: (i, 0))
    return pl.pallas_call(
        body, name=name, grid=(SEQ // NORM_ROWS,), in_specs=[row, _spec((1, D_MODEL), lambda i: (0, 0))], out_specs=row,
        out_shape=jax.ShapeDtypeStruct((SEQ, D_MODEL), BF16), compiler_params=_params(("parallel",)),
    )(x, gamma)


def _norm_bwd_rows(x, gamma, dh):
    r = _rstd(x)
    xh = x * r
    dxh = dh * gamma
    dx = r * (dxh - xh * jnp.mean(dxh * xh, axis=-1, keepdims=True))
    return dx, jnp.sum(dh * xh, axis=0, keepdims=True)


def _norm_bwd(name, x, gamma, dh, dx_in):
    def body(x_ref, g_ref, dh_ref, dxi_ref, dx_ref, dg_ref):
        dx, dg = _norm_bwd_rows(x_ref[...], g_ref[...], dh_ref[...].astype(F32))
        dx_ref[...] = dxi_ref[...] + dx

        @pl.when(pl.program_id(0) == 0)
        def _():
            dg_ref[...] = dg

        @pl.when(pl.program_id(0) > 0)
        def _():
            dg_ref[...] += dg

    row = _spec((NORM_ROWS, D_MODEL), lambda i: (i, 0))
    vec = _spec((1, D_MODEL), lambda i: (0, 0))
    return pl.pallas_call(
        body, name=name, grid=(SEQ // NORM_ROWS,), in_specs=[row, vec, row, row], out_specs=[row, vec],
        out_shape=[jax.ShapeDtypeStruct((SEQ, D_MODEL), F32), jax.ShapeDtypeStruct((1, D_MODEL), F32)],
        compiler_params=_params(("arbitrary",)),
    )(x, gamma, dh, dx_in)


def _loss_head(x, gamma, target):
    def body(x_ref, g_ref, t_ref, loss_ref, dx_ref, dg_ref):
        x = x_ref[...]
        gamma = g_ref[...]
        err = x * _rstd(x) * gamma - t_ref[...]
        dy = err * (1.0 / D_MODEL)
        dx, dg = _norm_bwd_rows(x, gamma, dy)
        dx_ref[...] = dx
        part = 0.5 * jnp.sum(jnp.sum(err * err, axis=-1, keepdims=True) * (1.0 / D_MODEL), axis=0, keepdims=True)
        part = jnp.broadcast_to(part, loss_ref.shape)

        @pl.when(pl.program_id(0) == 0)
        def _():
            dg_ref[...] = dg
            loss_ref[...] = part

        @pl.when(pl.program_id(0) > 0)
        def _():
            dg_ref[...] += dg
            loss_ref[...] += part

    row = _spec((NORM_ROWS, D_MODEL), lambda i: (i, 0))
    vec = _spec((1, D_MODEL), lambda i: (0, 0))
    return pl.pallas_call(
        body, name="loss_head", grid=(SEQ // NORM_ROWS,), in_specs=[row, vec, row],
        out_specs=[_spec((1, 128), lambda i: (0, 0)), row, vec],
        out_shape=[jax.ShapeDtypeStruct((1, 128), F32), jax.ShapeDtypeStruct((SEQ, D_MODEL), F32),
                   jax.ShapeDtypeStruct((1, D_MODEL), F32)],
        compiler_params=_params(("arbitrary",)),
    )(x, gamma, target)


def _sigmoid(x):
    return 1.0 / (1.0 + jnp.exp(-x))


def _rows(ref, c):
    return ref[pl.ds(pl.multiple_of(c * ROW_CHUNK, ROW_CHUNK), ROW_CHUNK), :].astype(F32)


def _rows_before(ref, c):
    start = pl.multiple_of(jnp.maximum(c * ROW_CHUNK - HALO, 0), HALO)
    rows = ref[pl.ds(start, HALO), :].astype(F32)
    return jnp.where(c > 0, rows, 0.0)


def _rows_after(ref, c, n_chunks):
    start = pl.multiple_of(jnp.minimum((c + 1) * ROW_CHUNK, SEQ - HALO), HALO)
    rows = ref[pl.ds(start, HALO), :].astype(F32)
    return jnp.where(c < n_chunks - 1, rows, 0.0)


def _shift_down(z, before, n):
    row = lax.broadcasted_iota(jnp.int32, z.shape, 0)
    out = pltpu.roll(z, n, 0)
    for r in range(n):
        out = jnp.where(row == r, before[HALO - n + r:HALO - n + r + 1, :], out)
    return out


def _shift_up(z, after, n):
    rows = z.shape[0]
    row = lax.broadcasted_iota(jnp.int32, z.shape, 0)
    out = pltpu.roll(z, rows - n, 0)
    for r in range(n):
        out = jnp.where(row == rows - n + r, after[r:r + 1, :], out)
    return out


def _conv_rows(z, before, w):
    z1 = _shift_down(z, before, 1)
    z2 = _shift_down(z, before, 2)
    return w[2:3, :] * z + w[1:2, :] * z1 + w[0:1, :] * z2, z1, z2


def _conv_t_rows(dy, after, w):
    return w[2:3, :] * dy + w[1:2, :] * _shift_up(dy, after, 1) + w[0:1, :] * _shift_up(dy, after, 2)


N_ROW_CHUNKS = SEQ // ROW_CHUNK


def _ffn_mid_fwd(name, gu, conv_w):
    def body(gu_ref, w_ref, a_ref):
        w = w_ref[...]

        def chunk(c, carry):
            g = _rows(gu_ref.at[0], c)
            u = _rows(gu_ref.at[1], c)
            gc, _, _ = _conv_rows(g, _rows_before(gu_ref.at[0], c), w)
            a_ref[pl.ds(pl.multiple_of(c * ROW_CHUNK, ROW_CHUNK), ROW_CHUNK), :] = (gc * _sigmoid(gc) * u).astype(BF16)
            return carry

        lax.fori_loop(0, N_ROW_CHUNKS, chunk, 0)

    return pl.pallas_call(
        body, name=name, grid=(FF_BLOCKS,),
        in_specs=[_spec((2, None, SEQ, FF_BLOCK), lambda j: (0, j, 0, 0)), _spec((None, 3, FF_BLOCK), lambda j: (j, 0, 0))],
        out_specs=_spec((None, SEQ, FF_BLOCK), lambda j: (j, 0, 0)),
        out_shape=jax.ShapeDtypeStruct((FF_BLOCKS, SEQ, FF_BLOCK), BF16), compiler_params=_params(("parallel",)),
    )(gu, conv_w)


def _ffn_mid_bwd(name, gu, conv_w, da):
    def body(gu_ref, w_ref, da_ref, dgu_ref, dw_ref, dgc_ref):
        w = w_ref[...]

        def first(c, acc):
            g = _rows(gu_ref.at[0], c)
            u = _rows(gu_ref.at[1], c)
            d = _rows(da_ref, c)
            gc, g1, g2 = _conv_rows(g, _rows_before(gu_ref.at[0], c), w)
            sg = _sigmoid(gc)
            rows = pl.ds(pl.multiple_of(c * ROW_CHUNK, ROW_CHUNK), ROW_CHUNK)
            dgu_ref[1, rows, :] = (d * gc * sg).astype(BF16)
            dgc = d * u * (sg * (1.0 + gc * (1.0 - sg)))
            dgc_ref[rows, :] = dgc
            return (acc[0] + jnp.sum(dgc * g2, axis=0, keepdims=True), acc[1] + jnp.sum(dgc * g1, axis=0, keepdims=True),
                    acc[2] + jnp.sum(dgc * g, axis=0, keepdims=True))

        zero = jnp.zeros((1, FF_BLOCK), F32)
        acc = lax.fori_loop(0, N_ROW_CHUNKS, first, (zero, zero, zero))
        for r in range(3):
            dw_ref[r:r + 1, :] = acc[r]

        def second(c, carry):
            dgc = _rows(dgc_ref, c)
            dg = _conv_t_rows(dgc, _rows_after(dgc_ref, c, N_ROW_CHUNKS), w)
            dgu_ref[0, pl.ds(pl.multiple_of(c * ROW_CHUNK, ROW_CHUNK), ROW_CHUNK), :] = dg.astype(BF16)
            return carry

        lax.fori_loop(0, N_ROW_CHUNKS, second, 0)

    pair = _spec((2, None, SEQ, FF_BLOCK), lambda j: (0, j, 0, 0))
    wspec = _spec((None, 3, FF_BLOCK), lambda j: (j, 0, 0))
    return pl.pallas_call(
        body, name=name, grid=(FF_BLOCKS,),
        in_specs=[pair, wspec, _spec((None, SEQ, FF_BLOCK), lambda j: (j, 0, 0))], out_specs=[pair, wspec],
        out_shape=[jax.ShapeDtypeStruct((2, FF_BLOCKS, SEQ, FF_BLOCK), BF16), jax.ShapeDtypeStruct((FF_BLOCKS, 3, FF_BLOCK), F32)],
        scratch_shapes=[pltpu.VMEM((SEQ, FF_BLOCK), F32)], compiler_params=_params(("parallel",)),
    )(gu, conv_w, da)


SC_COLS = 256
N_SC = D_MODEL // SC_COLS


def _sc_specs():
    return [_spec((SEQ, SC_COLS), lambda j, part=part: (0, part * N_SC + j)) for part in range(3)]


def _sc_mid_fwd(p, conv_w):
    def body(b_ref, c_ref, h_ref, w_ref, y_ref):
        w = w_ref[...]

        def chunk(c, carry):
            z = _rows(c_ref, c) * _rows(h_ref, c)
            before = _rows_before(c_ref, c) * _rows_before(h_ref, c)
            zc, _, _ = _conv_rows(z, before, w)
            y_ref[pl.ds(pl.multiple_of(c * ROW_CHUNK, ROW_CHUNK), ROW_CHUNK), :] = (_rows(b_ref, c) * zc).astype(BF16)
            return carry

        lax.fori_loop(0, N_ROW_CHUNKS, chunk, 0)

    col = _spec((SEQ, SC_COLS), lambda j: (0, j))
    return pl.pallas_call(
        body, name="sc_mid_fwd", grid=(N_SC,), in_specs=_sc_specs() + [_spec((3, SC_COLS), lambda j: (0, j))], out_specs=col,
        out_shape=jax.ShapeDtypeStruct((SEQ, D_MODEL), BF16), compiler_params=_params(("parallel",)),
    )(p, p, p, conv_w)


def _sc_mid_bwd(p, conv_w, dy):
    def body(b_ref, c_ref, h_ref, w_ref, dy_ref, db_ref, dc_ref, dh_ref, dw_ref, dzc_ref):
        w = w_ref[...]

        def first(c, acc):
            z = _rows(c_ref, c) * _rows(h_ref, c)
            before = _rows_before(c_ref, c) * _rows_before(h_ref, c)
            zc, z1, z2 = _conv_rows(z, before, w)
            d = _rows(dy_ref, c)
            rows = pl.ds(pl.multiple_of(c * ROW_CHUNK, ROW_CHUNK), ROW_CHUNK)
            db_ref[rows, :] = (d * zc).astype(BF16)
            dzc = d * _rows(b_ref, c)
            dzc_ref[rows, :] = dzc
            return (acc[0] + jnp.sum(dzc * z2, axis=0, keepdims=True), acc[1] + jnp.sum(dzc * z1, axis=0, keepdims=True),
                    acc[2] + jnp.sum(dzc * z, axis=0, keepdims=True))

        zero = jnp.zeros((1, SC_COLS), F32)
        acc = lax.fori_loop(0, N_ROW_CHUNKS, first, (zero, zero, zero))
        for r in range(3):
            dw_ref[r:r + 1, :] = acc[r]

        def second(c, carry):
            dz = _conv_t_rows(_rows(dzc_ref, c), _rows_after(dzc_ref, c, N_ROW_CHUNKS), w)
            rows = pl.ds(pl.multiple_of(c * ROW_CHUNK, ROW_CHUNK), ROW_CHUNK)
            dc_ref[rows, :] = (dz * _rows(h_ref, c)).astype(BF16)
            dh_ref[rows, :] = (dz * _rows(c_ref, c)).astype(BF16)
            return carry

        lax.fori_loop(0, N_ROW_CHUNKS, second, 0)

    col = _spec((SEQ, SC_COLS), lambda j: (0, j))
    wspec = _spec((3, SC_COLS), lambda j: (0, j))
    act = jax.ShapeDtypeStruct((SEQ, D_MODEL), BF16)
    return pl.pallas_call(
        body, name="sc_mid_bwd", grid=(N_SC,), in_specs=_sc_specs() + [wspec, col], out_specs=[col, col, col, wspec],
        out_shape=[act, act, act, jax.ShapeDtypeStruct((3, D_MODEL), F32)],
        scratch_shapes=[pltpu.VMEM((SEQ, SC_COLS), F32)], compiler_params=_params(("parallel",)),
    )(p, p, p, conv_w, dy)


GLA_GROUP = 4
GLA_ROWS = GLA_GROUP * CHUNK
N_GROUPS = N_CHUNKS // GLA_GROUP
Q0, K0, V0, R0, G0 = 0, KEY_DIM, 2 * KEY_DIM, 2 * KEY_DIM + VALUE_DIM, 2 * KEY_DIM + 2 * VALUE_DIM


def _tri(strict):
    r = lax.broadcasted_iota(jnp.int32, (CHUNK, CHUNK), 0)
    c = lax.broadcasted_iota(jnp.int32, (CHUNK, CHUNK), 1)
    return jnp.where(c < r if strict else c <= r, 1.0, 0.0).astype(F32)


def _cumsum_rows(tri, x):
    return jnp.dot(tri, x, preferred_element_type=F32, precision=lax.Precision.HIGHEST)


def _gate_logits(gl, wgu, b_gate):
    return jnp.dot(gl, wgu, preferred_element_type=F32) + b_gate


def _log_decay(logits):
    return (jnp.minimum(logits, 0.0) - jnp.log(1.0 + jnp.exp(-jnp.abs(logits)))) * (1.0 / GATE_NORMALIZER)


def _head(x, h, width):
    return x[:, h * width:(h + 1) * width]


def _gla_fwd(proj, wgu, b_gate, gn):
    def body(p_ref, wgu_ref, b_ref, gn_ref, o_ref, og_ref, st_ref, state):
        @pl.when(pl.program_id(0) == 0)
        def _():
            state[...] = jnp.zeros_like(state)

        tri = _tri(False)
        la = _log_decay(_gate_logits(p_ref[:, G0:G0 + GATE_PAD], wgu_ref[...], b_ref[...]))
        for c in range(GLA_GROUP):
            rows = slice(c * CHUNK, (c + 1) * CHUNK)
            cum = _cumsum_rows(tri, la[rows])
            tot = cum[CHUNK - 1:CHUNK, :]
            kd = (p_ref[rows, K0:K0 + KEY_DIM].astype(F32) * jnp.exp(tot - cum)).astype(BF16)
            decay = jnp.exp(tot)
            q = (p_ref[rows, Q0:Q0 + KEY_DIM].astype(F32) * (HEAD_K ** -0.5)).astype(BF16)
            v = p_ref[rows, V0:V0 + VALUE_DIM]
            for h in range(GLA_HEADS):
                upd = lax.dot_general(_head(v, h, HEAD_V), _head(kd, h, HEAD_K), (TN, ((), ())), preferred_element_type=F32)
                s = state[h] * _head(decay, h, HEAD_K) + upd
                state[h] = s
                st_ref[c, h] = s
                o_ref[rows, h * HEAD_V:(h + 1) * HEAD_V] = lax.dot_general(
                    _head(q, h, HEAD_K), s.astype(BF16), (NT, ((), ())), preferred_element_type=F32)
        r = p_ref[:, R0:R0 + VALUE_DIM].astype(F32)
        gate = r * _sigmoid(r) * gn_ref[...]
        for h in range(GLA_HEADS):
            cols = slice(h * HEAD_V, (h + 1) * HEAD_V)
            o = o_ref[:, cols]
            og_ref[:, cols] = (o * _rstd(o) * gate[:, cols]).astype(BF16)

    rows = _spec((GLA_ROWS, VALUE_DIM), lambda i: (i, 0))
    const = lambda shape: _spec(shape, lambda i: (0,) * len(shape))
    return pl.pallas_call(
        body, name="gla_fwd", grid=(N_GROUPS,),
        in_specs=[_spec((GLA_ROWS, PROJ_A_PAD), lambda i: (i, 0)), const((GATE_PAD, KEY_DIM)), const((1, KEY_DIM)),
                  const((1, VALUE_DIM))],
        out_specs=[rows, rows, _spec((GLA_GROUP, GLA_HEADS, HEAD_V, HEAD_K), lambda i: (i, 0, 0, 0))],
        out_shape=[jax.ShapeDtypeStruct((SEQ, VALUE_DIM), F32), jax.ShapeDtypeStruct((SEQ, VALUE_DIM), BF16),
                   jax.ShapeDtypeStruct((N_CHUNKS, GLA_HEADS, HEAD_V, HEAD_K), F32)],
        scratch_shapes=[pltpu.VMEM((GLA_HEADS, HEAD_V, HEAD_K), F32)], compiler_params=_params(("arbitrary",)),
    )(proj, wgu, b_gate, gn)


def _gla_bwd(proj, wgu, b_gate, gn, o, states, dog):
    last = N_GROUPS - 1

    def body(p_ref, wgu_ref, b_ref, gn_ref, o_ref, st_ref, stp_ref, dog_ref, dp_ref, dwgu_ref, db_ref, dgn_ref, carry, do_buf):
        step = pl.program_id(0)

        @pl.when(step == 0)
        def _():
            carry[...] = jnp.zeros_like(carry)

        r = p_ref[:, R0:R0 + VALUE_DIM].astype(F32)
        sr = _sigmoid(r)
        silu = r * sr
        gn_row = gn_ref[...]
        dog_rows = dog_ref[...].astype(F32)
        dn = dog_rows * silu
        dgn_cols = []
        for h in range(GLA_HEADS):
            cols = slice(h * HEAD_V, (h + 1) * HEAD_V)
            oh = o_ref[:, cols]
            rs = _rstd(oh)
            ohat = oh * rs
            dn_h = dn[:, cols]
            dgn_cols.append(jnp.sum(dn_h * ohat, axis=0, keepdims=True))
            dohat = dn_h * gn_row[:, cols]
            do_buf[:, cols] = rs * (dohat - ohat * jnp.mean(dohat * ohat, axis=-1, keepdims=True))
            n_h = ohat * gn_row[:, cols]
            dp_ref[:, R0 + h * HEAD_V:R0 + (h + 1) * HEAD_V] = (
                dog_rows[:, cols] * n_h * (sr[:, cols] * (1.0 + r[:, cols] * (1.0 - sr[:, cols])))).astype(BF16)
        dgn = jnp.concatenate(dgn_cols, axis=1)

        tri = _tri(False)
        tri_strict = _tri(True)
        gl = p_ref[:, G0:G0 + GATE_PAD]
        logits = _gate_logits(gl, wgu_ref[...], b_ref[...])
        la = _log_decay(logits)
        dlogit_rows = []
        for c in reversed(range(GLA_GROUP)):
            rows = slice(c * CHUNK, (c + 1) * CHUNK)
            cum = _cumsum_rows(tri, la[rows])
            tot = cum[CHUNK - 1:CHUNK, :]
            fade = jnp.exp(tot - cum)
            k = p_ref[rows, K0:K0 + KEY_DIM].astype(F32)
            kd32 = k * fade
            kd = kd32.astype(BF16)
            decay = jnp.exp(tot)
            q = (p_ref[rows, Q0:Q0 + KEY_DIM].astype(F32) * (HEAD_K ** -0.5)).astype(BF16)
            v = p_ref[rows, V0:V0 + VALUE_DIM]
            do = do_buf[rows, :].astype(BF16)
            dkd_cols, ddecay_cols = [], []
            for h in range(GLA_HEADS):
                do_h = _head(do, h, HEAD_V)
                s = st_ref[c, h]
                dq = jnp.dot(do_h, s.astype(BF16), preferred_element_type=F32) * (HEAD_K ** -0.5)
                dp_ref[rows, Q0 + h * HEAD_K:Q0 + (h + 1) * HEAD_K] = dq.astype(BF16)
                g = carry[h] + lax.dot_general(do_h, _head(q, h, HEAD_K), (TN, ((), ())), preferred_element_type=F32)
                g16 = g.astype(BF16)
                dkd_cols.append(jnp.dot(_head(v, h, HEAD_V), g16, preferred_element_type=F32))
                dv = lax.dot_general(_head(kd, h, HEAD_K), g16, (NT, ((), ())), preferred_element_type=F32)
                dp_ref[rows, V0 + h * HEAD_V:V0 + (h + 1) * HEAD_V] = dv.astype(BF16)
                if c > 0:
                    s_prev = st_ref[c - 1, h]
                else:
                    s_prev = jnp.where(step < last, stp_ref[0, h], 0.0)
                ddecay_cols.append(jnp.sum(g * s_prev, axis=0, keepdims=True))
                carry[h] = g * _head(decay, h, HEAD_K)
            dkd = jnp.concatenate(dkd_cols, axis=1)
            ddecay = jnp.concatenate(ddecay_cols, axis=1)
            dp_ref[rows, K0:K0 + KEY_DIM] = (dkd * fade).astype(BF16)
            e = dkd * kd32
            dla = ddecay * decay + _cumsum_rows(tri_strict, e)
            dlogit_rows.append(dla * (1.0 / GATE_NORMALIZER) * (1.0 - _sigmoid(logits[rows])))
        dlogit = jnp.concatenate(dlogit_rows[::-1], axis=0)
        dlogit16 = dlogit.astype(BF16)
        dp_ref[:, G0:G0 + GATE_PAD] = lax.dot_general(
            dlogit16, wgu_ref[...], (NT, ((), ())), preferred_element_type=F32).astype(BF16)
        dwgu = lax.dot_general(gl, dlogit16, (TN, ((), ())), preferred_element_type=F32)
        db = jnp.sum(dlogit, axis=0, keepdims=True)

        @pl.when(step == 0)
        def _():
            dwgu_ref[...] = dwgu
            db_ref[...] = db
            dgn_ref[...] = dgn

        @pl.when(step > 0)
        def _():
            dwgu_ref[...] += dwgu
            db_ref[...] += db
            dgn_ref[...] += dgn

    rev = lambda i: (last - i, 0)
    rows = _spec((GLA_ROWS, VALUE_DIM), rev)
    const = lambda shape: _spec(shape, lambda i: (0,) * len(shape))
    st_shape = (GLA_HEADS, HEAD_V, HEAD_K)
    return pl.pallas_call(
        body, name="gla_bwd", grid=(N_GROUPS,),
        in_specs=[_spec((GLA_ROWS, PROJ_A_PAD), rev), const((GATE_PAD, KEY_DIM)), const((1, KEY_DIM)), const((1, VALUE_DIM)),
                  rows, _spec((GLA_GROUP,) + st_shape, lambda i: (last - i, 0, 0, 0)),
                  _spec((1,) + st_shape, lambda i: (jnp.maximum((last - i) * GLA_GROUP - 1, 0), 0, 0, 0)), rows],
        out_specs=[_spec((GLA_ROWS, PROJ_A_PAD), rev), const((GATE_PAD, KEY_DIM)), const((1, KEY_DIM)), const((1, VALUE_DIM))],
        out_shape=[jax.ShapeDtypeStruct((SEQ, PROJ_A_PAD), BF16), jax.ShapeDtypeStruct((GATE_PAD, KEY_DIM), F32),
                   jax.ShapeDtypeStruct((1, KEY_DIM), F32), jax.ShapeDtypeStruct((1, VALUE_DIM), F32)],
        scratch_shapes=[pltpu.VMEM(st_shape, F32), pltpu.VMEM((GLA_ROWS, VALUE_DIM), F32)],
        compiler_params=_params(("arbitrary",)),
    )(proj, wgu, b_gate, gn, o, states, states, dog)


def _ffn_fwd(tag, x, gamma, w_up, conv_w, w_down):
    h = _norm_fwd(f"ffn{tag}_norm", x, gamma)
    gu = _proj_blocks_nn(f"ffn{tag}_up", h, w_up, FF_BLOCK, False).reshape(2, FF_BLOCKS, SEQ, FF_BLOCK)
    a = _ffn_mid_fwd(f"ffn{tag}_mid", gu, conv_w)
    return _down_nn(f"ffn{tag}_down", a, w_down, x), (h, gu, a)


def _ffn_bwd(tag, x, gamma, w_up, conv_w, w_down, saved, dx):
    h, gu, a = saved
    da = _back_blocks_nt(f"ffn{tag}_da", dx, w_down)
    d_w_down = _wgrad_a_blocks_tn(f"ffn{tag}_dwdown", a, dx)
    dgu, d_conv = _ffn_mid_bwd(f"ffn{tag}_mid_bwd", gu, conv_w, da)
    dgu = dgu.reshape(2 * FF_BLOCKS, SEQ, FF_BLOCK)
    dh = _back_sum_blocks_nt(f"ffn{tag}_dh", dgu, w_up)
    d_w_up = _wgrad_d_blocks_tn(f"ffn{tag}_dwup", h, dgu)
    dx, d_gamma = _norm_bwd(f"ffn{tag}_norm_bwd", x, gamma, dh, dx)
    return dx, d_gamma, d_w_up, d_conv, d_w_down


def _local_step(x, target, w, fetch=None, emit=None):
    if fetch is None:
        local = dict(a=(w.get("a_w_in"), w.get("a_w_out")), b=(w.get("b_w_in"), w.get("b_w_out")))
        for layer in range(2):
            local[f"f{layer}"] = (w["f_w_up"][layer], w["f_w_down"][layer]) if "f_w_up" in w else None
        fetch = lambda group, after: local[group]
    if emit is None:
        emit = lambda group, grads, dx: dx
    f_norm = (w["f_norm"][0:1], w["f_norm"][1:2])

    x0 = x
    a_w_in, a_w_out = fetch("a", x0)
    h0 = _norm_fwd("a_norm", x0, w["a_norm"])
    proj = _proj_nn("a_in", h0, a_w_in, PA_TILE, N_PA)
    o, og, states = _gla_fwd(proj, w["a_w_gate_up"], w["a_b_gate"], w["a_gn"])
    x1 = _out_nn("a_out", og, a_w_out, x0)
    up0, down0 = fetch("f0", x1)
    x2, ffn0 = _ffn_fwd(0, x1, f_norm[0], up0, w["f_conv"][0], down0)
    b_w_in, b_w_out = fetch("b", x2)
    h2 = _norm_fwd("b_norm", x2, w["b_norm"])
    p = _proj_blocks_nn("b_in", h2, b_w_in, B_SHARD, True)
    y = _sc_mid_fwd(p, w["b_conv"])
    x3 = _out_nn("b_out", y, b_w_out, x2)
    up1, down1 = fetch("f1", x3)
    x4, ffn1 = _ffn_fwd(1, x3, f_norm[1], up1, w["f_conv"][1], down1)
    loss, dx, d_final_norm = _loss_head(x4, w["final_norm"], target)

    dx, d_f_norm1, d_up1, d_fconv1, d_down1 = _ffn_bwd(1, x3, f_norm[1], up1, w["f_conv"][1], down1, ffn1, dx)
    dx = emit("f1", (d_up1, d_down1), dx)

    dy = _back_nt("b_dy", dx, b_w_out)
    d_b_w_out = _wgrad_tn("b_dwout", y, D_MODEL, dx, D_MODEL, False)
    db, dc, dhh, d_b_conv = _sc_mid_bwd(p, w["b_conv"], dy)
    dp = jnp.concatenate([db, dc, dhh], axis=1)
    dh2 = _back_sum_cols_nt("b_dh", dp, w_blocks=b_w_in, n_tile=B_SHARD)
    d_b_w_in = _wgrad_tn("b_dwin", h2, D_MODEL, dp, B_SHARD, True)
    dx, d_b_norm = _norm_bwd("b_norm_bwd", x2, w["b_norm"], dh2, dx)
    dx = emit("b", (d_b_w_in, d_b_w_out), dx)

    dx, d_f_norm0, d_up0, d_fconv0, d_down0 = _ffn_bwd(0, x1, f_norm[0], up0, w["f_conv"][0], down0, ffn0, dx)
    dx = emit("f0", (d_up0, d_down0), dx)

    dog = _back_nt("a_dog", dx, a_w_out)
    d_a_w_out = _wgrad_tn("a_dwout", og, D_MODEL, dx, D_MODEL, False)
    dproj, d_wgu, d_b_gate, d_gn = _gla_bwd(proj, w["a_w_gate_up"], w["a_b_gate"], w["a_gn"], o, states, dog)
    dh0 = _back_sum_cols_nt("a_dh", dproj, w=a_w_in, n_tile=PA_TILE)
    d_a_w_in = _wgrad_tn("a_dwin", h0, D_MODEL, dproj, PA_TILE, False)
    dx, d_a_norm = _norm_bwd("a_norm_bwd", x0, w["a_norm"], dh0, dx)

    grads = dict(
        a_norm=d_a_norm, a_w_in=d_a_w_in, a_w_gate_up=d_wgu, a_b_gate=d_b_gate, a_gn=d_gn, a_w_out=d_a_w_out,
        b_norm=d_b_norm, b_w_in=d_b_w_in, b_conv=d_b_conv, b_w_out=d_b_w_out,
        f_norm=(d_f_norm0, d_f_norm1), f_w_up=(d_up0, d_up1), f_conv=(d_fconv0, d_fconv1), f_w_down=(d_down0, d_down1),
        final_norm=d_final_norm)
    return loss[0, 0], dx, grads


MESH_ID = pl.DeviceIdType.MESH
ANY = pl.BlockSpec(memory_space=pl.ANY)
N_PEERS = N_DEV - 1


def _position():
    return lax.axis_index("x"), lax.axis_index("y"), lax.axis_index("c")


def _slot(px, py, pc):
    return 4 * px + 2 * py + pc


def _all_gather(shards):
    n = len(shards)

    def body(*refs):
        ins, outs = refs[:n], refs[n:2 * n]
        send_sems, recv_sems, local_sems = refs[2 * n:]
        x, y, c = _position()
        me, sibling = (x, y, c), (x, y, 1 - c)
        chips = [(1 - x, y), (x, 1 - y), (1 - x, 1 - y)]

        def copy(t, k, block, to, from_input=False):
            dst = outs[t].at[_slot(*block)]
            return pltpu.make_async_remote_copy(
                src_ref=ins[t] if from_input else dst, dst_ref=dst, send_sem=send_sems.at[t, k], recv_sem=recv_sems.at[t, k],
                device_id=to, device_id_type=MESH_ID)

        mine = [pltpu.make_async_copy(ins[t], outs[t].at[_slot(*me)], local_sems.at[t]) for t in range(n)]
        for cp in mine:
            cp.start()
        first = []
        for t in range(n):
            first.append(copy(t, 0, me, sibling, True))
            first += [copy(t, 1 + j, me, (*chip, c), True) for j, chip in enumerate(chips)]
        for cp in first:
            cp.start()
        passed = []
        for t in range(n):
            for j, chip in enumerate(chips):
                copy(t, 1 + j, (*chip, c), me).wait_recv()
                fwd = copy(t, 4 + j, (*chip, c), sibling)
                fwd.start()
                passed.append(fwd)
        for t in range(n):
            copy(t, 0, sibling, me).wait_recv()
            for j, chip in enumerate(chips):
                copy(t, 4 + j, (*chip, 1 - c), me).wait_recv()
        for cp in first + passed:
            cp.wait_send()
        for cp in mine:
            cp.wait()

    return pl.pallas_call(
        body, name="weight_gather", in_specs=[ANY] * n, out_specs=[ANY] * n,
        out_shape=[jax.ShapeDtypeStruct((N_DEV,) + s.shape, s.dtype) for s in shards],
        scratch_shapes=[pltpu.SemaphoreType.DMA((n, N_PEERS)), pltpu.SemaphoreType.DMA((n, N_PEERS)), pltpu.SemaphoreType.DMA((n,))],
    )(*shards)


HBM = pl.BlockSpec(memory_space=pltpu.HBM)
SEM = pl.BlockSpec(memory_space=pltpu.SEMAPHORE)
DATAFLOW = pltpu.SideEffectType.DATAFLOW_SIDE_EFFECTING
ALL_PEERS = (1, 2, 3, 4, 5, 6, 7)
SIBLING_AND_SAME_CORE = (1, 2, 4, 6)
SAME_CORE = (2, 4, 6)


def _flip(x, y, c, k):
    return x ^ (k >> 2), y ^ ((k >> 1) & 1), c ^ (k & 1)


def _in_hbm(a):
    return pltpu.with_memory_space_constraint(a, pltpu.HBM)


def _send_copy(parts, landing, shared, send_sems, recv_sems, t, s, k):
    x, y, c = _position()
    peer = _flip(x, y, c, k)
    src = parts[t] if shared[t] else parts[t].at[_slot(*peer)]
    return pltpu.make_async_remote_copy(
        src_ref=src, dst_ref=landing[t].at[_slot(x, y, c)], send_sem=send_sems.at[s], recv_sem=recv_sems.at[s],
        device_id=peer, device_id_type=MESH_ID)


def _send_arrival(landing, send_sems, recv_sems, t, s, k):
    x, y, c = _position()
    peer = _flip(x, y, c, k)
    landed = landing[t].at[_slot(*peer)]
    return pltpu.make_async_remote_copy(
        src_ref=landed, dst_ref=landed, send_sem=send_sems.at[s], recv_sem=recv_sems.at[s],
        device_id=peer, device_id_type=MESH_ID)


def _send_start(name, parts, shared, peers, carry):
    n = len(parts)

    def body(*refs):
        parts_in, landing_in = refs[:n], refs[n:2 * n]
        send_sems, recv_sems = refs[2 * n + 1], refs[2 * n + 2]
        local_sems = refs[-1]
        my_slot = _slot(*_position())
        mine = [pltpu.make_async_copy(parts_in[t] if shared[t] else parts_in[t].at[my_slot], landing_in[t].at[my_slot],
                                      local_sems.at[t]) for t in range(n)]
        for cp in mine:
            cp.start()
        for cp in mine:
            cp.wait()
        for t in range(n):
            for j, k in enumerate(peers):
                _send_copy(parts_in, landing_in, shared, send_sems, recv_sems, t, t * len(peers) + j, k).start()

    landing = [lax.empty(((N_DEV,) + p.shape) if sh else p.shape, p.dtype) for p, sh in zip(parts, shared)]
    operands = [_in_hbm(a) for a in list(parts) + landing + [carry]]
    sems = pltpu.SemaphoreType.DMA((n * len(peers),))
    out = pl.pallas_call(
        body, name=name, in_specs=[HBM] * (2 * n + 1), out_specs=[SEM, SEM] + [HBM] * (2 * n + 1),
        out_shape=[sems, sems] + [pltpu.HBM(a.shape, a.dtype) for a in operands],
        input_output_aliases={i: 2 + i for i in range(2 * n + 1)}, scratch_shapes=[pltpu.SemaphoreType.DMA((n,))],
        compiler_params=pltpu.CompilerParams(has_side_effects=DATAFLOW),
    )(*operands)
    handle = dict(sems=out[:2], parts=out[2:2 + n], landing=out[2 + n:2 + 2 * n], shared=shared, peers=peers)
    return handle, out[2 + 2 * n]


def _send_wait(name, handle, after):
    n = len(handle["parts"])
    shared, peers = handle["shared"], handle["peers"]

    def body(*refs):
        parts_in, landing_in = refs[:n], refs[n:2 * n]
        send_sems, recv_sems = refs[2 * n], refs[2 * n + 1]
        for t in range(n):
            for j, k in enumerate(peers):
                s = t * len(peers) + j
                _send_copy(parts_in, landing_in, shared, send_sems, recv_sems, t, s, k).wait_send()
                _send_arrival(landing_in, send_sems, recv_sems, t, s, k).wait_recv()

    operands = list(handle["parts"]) + list(handle["landing"])
    out = pl.pallas_call(
        body, name=name, in_specs=[HBM] * (2 * n) + [SEM, SEM, ANY], out_specs=[HBM] * (2 * n),
        out_shape=[pltpu.HBM(a.shape, a.dtype) for a in operands], input_output_aliases={i: i for i in range(2 * n)},
        compiler_params=pltpu.CompilerParams(has_side_effects=DATAFLOW),
    )(*operands, *handle["sems"], after)
    return out[n:]


def _forward_to_sibling(name, gathered):
    n = len(gathered)

    def body(*refs):
        ins, outs = refs[:n], refs[n:2 * n]
        send_sems, recv_sems = refs[2 * n:]
        x, y, c = _position()
        sent = []
        for t in range(n):
            for j, k in enumerate(SAME_CORE):
                block = _slot(*_flip(x, y, c, k))
                sent.append(pltpu.make_async_remote_copy(
                    src_ref=ins[t].at[block], dst_ref=outs[t].at[block], send_sem=send_sems.at[t, j], recv_sem=recv_sems.at[t, j],
                    device_id=(x, y, 1 - c), device_id_type=MESH_ID))
        for cp in sent:
            cp.start()
        for t in range(n):
            for j, k in enumerate(SAME_CORE):
                landed = outs[t].at[_slot(*_flip(x, y, 1 - c, k))]
                pltpu.make_async_remote_copy(
                    src_ref=landed, dst_ref=landed, send_sem=send_sems.at[t, j], recv_sem=recv_sems.at[t, j],
                    device_id=(x, y, 1 - c), device_id_type=MESH_ID).wait_recv()
        for cp in sent:
            cp.wait_send()

    sems = pltpu.SemaphoreType.DMA((n, len(SAME_CORE)))
    return pl.pallas_call(
        body, name=name, in_specs=[ANY] * n, out_specs=[ANY] * n,
        out_shape=[jax.ShapeDtypeStruct(a.shape, a.dtype) for a in gathered], input_output_aliases={i: i for i in range(n)},
        scratch_shapes=[sems, sems],
    )(*gathered)


ADAM_ROWS = 256


def _adam_update(w, g, m, v):
    m = ADAM_B1 * m + (1.0 - ADAM_B1) * g
    v = ADAM_B2 * v + (1.0 - ADAM_B2) * (g * g)
    m_hat = m / (1.0 - ADAM_B1 ** ADAM_STEP)
    v_hat = v / (1.0 - ADAM_B2 ** ADAM_STEP)
    delta = -ADAM_LR * (m_hat / (jnp.sqrt(v_hat) + ADAM_EPS) + ADAM_WD * w)
    return delta, m, v


def _sum_slots(ref):
    total = ref[0].astype(F32)
    for d in range(1, N_DEV):
        total = total + ref[d].astype(F32)
    return total


def _adamw_sum(name, landed, w, m, v):
    layers, rows, cols = w.shape
    tr = ADAM_ROWS if rows % ADAM_ROWS == 0 else rows
    nt = rows // tr

    def body(*refs):
        parts = refs[:layers]
        w_ref, m_ref, v_ref, g_ref, d_ref, nm_ref, nv_ref = refs[layers:]
        layer = pl.program_id(0)
        g = _sum_slots(parts[0])
        for q in range(1, layers):
            g = jnp.where(layer == q, _sum_slots(parts[q]), g)
        delta, new_m, new_v = _adam_update(w_ref[...], g, m_ref[...], v_ref[...])
        g_ref[...] = g
        d_ref[...] = delta
        nm_ref[...] = new_m
        nv_ref[...] = new_v

    def part_spec(q):
        return _spec((N_DEV, tr, cols), lambda l, i: (0, jnp.where(l == q, i, jnp.where(l < q, 0, nt - 1)), 0))

    tile = _spec((None, tr, cols), lambda l, i: (l, i, 0))
    out = jax.ShapeDtypeStruct((layers, rows, cols), F32)
    return pl.pallas_call(
        body, name=name, grid=(layers, nt), in_specs=[part_spec(q) for q in range(layers)] + [tile] * 3,
        out_specs=[tile] * 4, out_shape=[out] * 4, compiler_params=_params(("arbitrary", "arbitrary")),
    )(*landed, w, m, v)


def _sum_small(landed):
    def body(in_ref, out_ref):
        out_ref[...] = _sum_slots(in_ref)

    return pl.pallas_call(body, name="small_grad_sum", out_shape=jax.ShapeDtypeStruct(landed.shape[1:], F32))(landed)


def _adamw_small(name, g, w, m, v):
    def body(g_ref, w_ref, m_ref, v_ref, d_ref, nm_ref, nv_ref):
        d_ref[...], nm_ref[...], nv_ref[...] = _adam_update(w_ref[...], g_ref[...], m_ref[...], v_ref[...])

    out = jax.ShapeDtypeStruct(w.shape, F32)
    return pl.pallas_call(body, name=name, out_shape=[out] * 3)(g, w, m, v)


LANES = 128
SUBLANES = 8
F_CONV_SHARD = D_FF // N_DEV
GATE_SHARD = KEY_DIM // N_DEV
NORM_SHARD = D_MODEL // N_DEV


def _tile_rows(a):
    flat = a.reshape(-1)
    size = -(-flat.shape[0] // (SUBLANES * LANES)) * SUBLANES * LANES
    return jnp.pad(flat, (0, size - flat.shape[0])).reshape(-1, LANES)


def _pack_rows(pieces):
    return jnp.concatenate([_tile_rows(p) for p in pieces], axis=0)


def _unpack_rows(packed, shapes):
    out, row = [], 0
    for shape in shapes:
        size = 1
        for s in shape:
            size *= s
        rows = -(-size // (SUBLANES * LANES)) * SUBLANES
        piece = packed[..., row:row + rows, :]
        out.append(piece.reshape(piece.shape[:-2] + (rows * LANES,))[..., :size])
        row += rows
    return out


SMALL_SHARDS = ((GATE_RANK, GATE_SHARD), (1, NORM_SHARD), (3, NORM_SHARD), (2, 3, F_CONV_SHARD))


def _unpack_small_shards(g):
    gate, b_norm, b_conv, f_conv = _unpack_rows(g, SMALL_SHARDS)
    gate = gate.reshape(N_DEV, GATE_RANK, GATE_SHARD).transpose(1, 0, 2).reshape(GATE_RANK, KEY_DIM)
    b_norm = b_norm.reshape(1, D_MODEL)
    b_conv = b_conv.reshape(N_DEV, 3, NORM_SHARD).transpose(1, 0, 2).reshape(3, D_MODEL)
    f_conv = f_conv.reshape(N_DEV, 2, 3, F_CONV_SHARD).transpose(1, 2, 0, 3).reshape(2, 3, D_FF)
    return gate, b_norm, b_conv, f_conv


def _conv_blocks(f_conv):
    return f_conv.reshape(2, 3, FF_BLOCKS, FF_BLOCK).transpose(0, 2, 1, 3)


def _conv_unblocks(f_conv):
    return f_conv.transpose(1, 0, 2).reshape(3, D_FF)


SMALL_LAYOUT = (("a_norm", (1, D_MODEL)), ("a_w_gate_up", (GATE_RANK, KEY_DIM)), ("a_b_gate", (1, KEY_DIM)), ("a_gn", (1, VALUE_DIM)),
                ("b_norm", (1, D_MODEL)), ("b_conv", (3, D_MODEL)), ("f_norm0", (1, D_MODEL)), ("f_norm1", (1, D_MODEL)),
                ("f_conv0", (3, D_FF)), ("f_conv1", (3, D_FF)), ("final_norm", (1, D_MODEL)))


def _pack_small_grads(g):
    full = dict(g)
    full["a_w_gate_up"] = g["a_w_gate_up"][:GATE_RANK]
    for layer in range(2):
        full[f"f_norm{layer}"] = g["f_norm"][layer]
        full[f"f_conv{layer}"] = _conv_unblocks(g["f_conv"][layer])
    return _pack_rows([full[name] for name, _ in SMALL_LAYOUT])


def _unpack_small_grads(packed):
    pieces = _unpack_rows(packed, [shape for _, shape in SMALL_LAYOUT])
    out = {name: piece.reshape(shape) for (name, shape), piece in zip(SMALL_LAYOUT, pieces)}
    out["f_norm"] = jnp.stack([out["f_norm0"][0], out["f_norm1"][0]])
    out["f_conv"] = jnp.stack([out["f_conv0"], out["f_conv1"]])
    return out


def kernel(x, a_norm, a_w_in, a_w_gate_up, a_b_gate, a_gn, a_w_out, b_norm, b_w_in, b_conv, b_w_out, f_norm, f_w_up, f_conv, f_w_down, final_norm, loss_target, m_a_norm, m_a_w_in, m_a_w_gate_up, m_a_b_gate, m_a_gn, m_a_w_out, m_b_norm, m_b_w_in, m_b_conv, m_b_w_out, m_f_norm, m_f_w_up, m_f_conv, m_f_w_down, m_final_norm, v_a_norm, v_a_w_in, v_a_w_gate_up, v_a_b_gate, v_a_gn, v_a_w_out, v_b_norm, v_b_w_in, v_b_conv, v_b_w_out, v_f_norm, v_f_w_up, v_f_conv, v_f_w_down, v_final_norm):
    my_slot = _slot(*_position())

    first = _all_gather([a_w_in[0].astype(BF16), a_w_out[0].astype(BF16), _pack_rows([a_w_gate_up[0], b_norm, b_conv[0], f_conv])])
    carry, gathers = first[0], {}
    for group, w_in, w_out in (("f0", f_w_up[0], f_w_down[0]), ("b", b_w_in[0], b_w_out[0]), ("f1", f_w_up[1], f_w_down[1])):
        gathers[group], carry = _send_start(
            f"gather_{group}_start", [w_in.astype(BF16), w_out.astype(BF16)], [True, True], SIBLING_AND_SAME_CORE, carry)
    gate_full, b_norm_full, b_conv_full, f_conv_full = _unpack_small_shards(first[2])
    a_w_in_full = jnp.pad(carry.transpose(1, 0, 2).reshape(D_MODEL, PROJ_A), ((0, 0), (0, PROJ_A_PAD - PROJ_A)))
    weights = dict(
        a_norm=a_norm, a_w_gate_up=jnp.pad(gate_full, ((0, GATE_PAD - GATE_RANK), (0, 0))).astype(BF16), a_b_gate=a_b_gate,
        a_gn=a_gn, b_norm=b_norm_full, b_conv=b_conv_full, f_norm=f_norm, f_conv=_conv_blocks(f_conv_full),
        final_norm=final_norm.reshape(1, D_MODEL))

    def fetch(group, after):
        if group == "a":
            return a_w_in_full, first[1].reshape(D_MODEL, D_MODEL)
        landed = _send_wait(f"gather_{group}_wait", gathers[group], after)
        w_in, w_out = _forward_to_sibling(f"gather_{group}_forward", landed)
        if group == "b":
            return w_in, w_out.reshape(D_MODEL, D_MODEL)
        return w_in, w_out.reshape(FF_BLOCKS, FF_BLOCK, D_MODEL)

    exchanges = {}

    def owner_blocks(d_out):
        return d_out.reshape((N_DEV, -1, D_MODEL))

    def emit(group, grads, dx):
        exchanges[group], dx = _send_start(
            f"grads_{group}_start", [grads[0], owner_blocks(grads[1])], [False, False], ALL_PEERS, dx)
        return dx

    loss, dx, g = _local_step(x[0], loss_target[0], weights, fetch, emit)
    loss = lax.psum(loss, MESH_AXES)
    parts = [g["a_w_in"][:, :PROJ_A].reshape(D_MODEL, N_DEV, A_SHARD).transpose(1, 0, 2), owner_blocks(g["a_w_out"]),
             _pack_small_grads(g)]
    exchanges["a"], _ = _send_start("grads_a_start", parts, [False, False, True], ALL_PEERS, jnp.zeros((SUBLANES, LANES), F32))

    landed_f1 = _send_wait("grads_f1_wait", exchanges["f1"], dx)
    landed_b = _send_wait("grads_b_wait", exchanges["b"], landed_f1[0])
    big = dict(
        b_w_in=_adamw_sum("adam_b_w_in", [landed_b[0]], b_w_in, m_b_w_in, v_b_w_in),
        b_w_out=_adamw_sum("adam_b_w_out", [landed_b[1]], b_w_out, m_b_w_out, v_b_w_out))
    landed_f0 = _send_wait("grads_f0_wait", exchanges["f0"], big["b_w_out"][0])
    big.update(
        f_w_up=_adamw_sum("adam_f_w_up", [landed_f0[0], landed_f1[0]], f_w_up, m_f_w_up, v_f_w_up),
        f_w_down=_adamw_sum("adam_f_w_down", [landed_f0[1], landed_f1[1]], f_w_down, m_f_w_down, v_f_w_down))
    landed_a = _send_wait("grads_a_wait", exchanges["a"], big["f_w_down"][0])
    big.update(
        a_w_in=_adamw_sum("adam_a_w_in", [landed_a[0]], a_w_in, m_a_w_in, v_a_w_in),
        a_w_out=_adamw_sum("adam_a_w_out", [landed_a[1]], a_w_out, m_a_w_out, v_a_w_out))
    small_g = _unpack_small_grads(_sum_small(landed_a[2]))
    small_g["a_w_gate_up"] = lax.dynamic_slice_in_dim(small_g["a_w_gate_up"], my_slot * GATE_SHARD, GATE_SHARD, axis=1)
    small_g["b_norm"] = lax.dynamic_slice_in_dim(small_g["b_norm"], my_slot * NORM_SHARD, NORM_SHARD, axis=1)
    small_g["b_conv"] = lax.dynamic_slice_in_dim(small_g["b_conv"], my_slot * NORM_SHARD, NORM_SHARD, axis=1)
    small_g["f_conv"] = lax.dynamic_slice_in_dim(small_g["f_conv"], my_slot * F_CONV_SHARD, F_CONV_SHARD, axis=2)
    small_w = dict(
        a_norm=(a_norm, m_a_norm, v_a_norm), a_w_gate_up=(a_w_gate_up, m_a_w_gate_up, v_a_w_gate_up),
        a_b_gate=(a_b_gate, m_a_b_gate, v_a_b_gate), a_gn=(a_gn, m_a_gn, v_a_gn), b_norm=(b_norm, m_b_norm, v_b_norm),
        b_conv=(b_conv, m_b_conv, v_b_conv), f_norm=(f_norm, m_f_norm, v_f_norm), f_conv=(f_conv, m_f_conv, v_f_conv),
        final_norm=(final_norm, m_final_norm, v_final_norm))
    small = {}
    for name, (w, m, v) in small_w.items():
        flat = (w.shape[-1],) if w.ndim == 1 else w.shape[-2:]
        two_d = (-1, flat[-1])
        grad = small_g[name].reshape(w.shape)
        delta, new_m, new_v = _adamw_small(
            "adam_" + name, grad.reshape(two_d), w.reshape(two_d), m.reshape(two_d), v.reshape(two_d))
        small[name] = (grad, delta.reshape(w.shape), new_m.reshape(w.shape), new_v.reshape(w.shape))

    order = ["a_norm", "a_w_in", "a_w_gate_up", "a_b_gate", "a_gn", "a_w_out", "b_norm", "b_w_in", "b_conv", "b_w_out",
             "f_norm", "f_w_up", "f_conv", "f_w_down", "final_norm"]
    results = {**big, **small}
    outputs = [loss, dx.reshape(1, SEQ, D_MODEL)]
    for kind in range(4):
        outputs += [results[name][kind] for name in order]
    return tuple(outputs)
```

```python
import jax
import jax.numpy as jnp
from jax import lax
from jax.experimental import pallas as pl
from jax.experimental.pallas import tpu as pltpu
from jax.experimental.pallas import tpu_sc as plsc

F32 = jnp.float32
BF16 = jnp.bfloat16

N_DEV = 8
SEQ = 2048
D_MODEL = 1024
CHUNK = 64
N_CHUNKS = SEQ // CHUNK
RMS_EPS = 1e-6
GLA_HEADS = 4
KEY_DIM = 512
VALUE_DIM = 1024
HEAD_K = KEY_DIM // GLA_HEADS
HEAD_V = VALUE_DIM // GLA_HEADS
GATE_RANK = 16
GATE_PAD = 128
GATE_NORMALIZER = 16.0
PROJ_A = 2 * KEY_DIM + 2 * VALUE_DIM + GATE_RANK
PROJ_A_PAD = 2 * KEY_DIM + 2 * VALUE_DIM + GATE_PAD
A_SHARD = PROJ_A // N_DEV
B_SHARD = 3 * D_MODEL // N_DEV
D_FF = 2816
FF_BLOCK = 2 * D_FF // N_DEV
FF_BLOCKS = D_FF // FF_BLOCK
ADAM_LR = 0.001
ADAM_B1 = 0.9
ADAM_B2 = 0.999
ADAM_EPS = 1e-08
ADAM_WD = 0.01
ADAM_STEP = 10
MESH_AXES = ("x", "y", "c")

VMEM_LIMIT = 56 * 1024 * 1024
ROW_CHUNK = 256
HALO = 16


def _params(sem=None, vmem=VMEM_LIMIT):
    return pltpu.CompilerParams(dimension_semantics=sem, vmem_limit_bytes=vmem)


NN = ((1,), (0,))
NT = ((1,), (1,))
TN = ((0,), (0,))


def _matmul(name, a, a_spec, b, b_spec, dims, grid, nk, out_shape, out_spec, acc_shape=None, res=None, res_spec=None):
    has_res = res is not None

    def body(*refs):
        a_ref, b_ref = refs[0], refs[1]
        r_ref = refs[2] if has_res else None
        o_ref = refs[2 + has_res]
        acc_ref = refs[3 + has_res] if nk > 1 else None

        def product():
            return lax.dot_general(a_ref[...].astype(BF16), b_ref[...].astype(BF16), (dims, ((), ())),
                                   preferred_element_type=F32)

        def finish(v):
            if has_res:
                v = v + r_ref[...]
            o_ref[...] = v.astype(o_ref.dtype)

        if nk == 1:
            finish(product())
        else:
            k = pl.program_id(len(grid) - 1)
            p = product()

            @pl.when(k == 0)
            def _():
                acc_ref[...] = p

            @pl.when(k > 0)
            def _():
                acc_ref[...] += p

            @pl.when(k == nk - 1)
            def _():
                finish(acc_ref[...])

    operands = [a, b] + ([res] if has_res else [])
    in_specs = [a_spec, b_spec] + ([res_spec] if has_res else [])
    sem = ("parallel",) * (len(grid) - 1) + (("arbitrary",) if nk > 1 else ("parallel",))
    return pl.pallas_call(
        body, name=name, grid=grid, in_specs=in_specs, out_specs=out_spec, out_shape=out_shape,
        scratch_shapes=[pltpu.VMEM(acc_shape, F32)] if nk > 1 else [],
        compiler_params=_params(sem),
    )(*operands)


TM = 1024
TKS = 1024
N_TM = SEQ // TM
N_TKS = SEQ // TKS
PA_TILE = 640
N_PA = PROJ_A_PAD // PA_TILE


def _spec(shape, fn):
    return pl.BlockSpec(shape, fn)


def _proj_nn(name, h, w, n_tile, n_tiles):
    n = n_tile * n_tiles
    return _matmul(name, h, _spec((TM, D_MODEL), lambda j, i: (i, 0)), w, _spec((D_MODEL, n_tile), lambda j, i: (0, j)), NN,
                   (n_tiles, N_TM), 1, jax.ShapeDtypeStruct((SEQ, n), BF16), _spec((TM, n_tile), lambda j, i: (i, j)))


def _proj_blocks_nn(name, h, w_blocks, n_tile, flat_out):
    nb = w_blocks.shape[0]
    if flat_out:
        out_shape = jax.ShapeDtypeStruct((SEQ, nb * n_tile), BF16)
        out_spec = _spec((TM, n_tile), lambda j, i: (i, j))
    else:
        out_shape = jax.ShapeDtypeStruct((nb, SEQ, n_tile), BF16)
        out_spec = _spec((None, TM, n_tile), lambda j, i: (j, i, 0))
    return _matmul(name, h, _spec((TM, D_MODEL), lambda j, i: (i, 0)), w_blocks,
                   _spec((None, D_MODEL, n_tile), lambda j, i: (j, 0, 0)), NN, (nb, N_TM), 1, out_shape, out_spec)


def _out_nn(name, a, w, x):
    return _matmul(name, a, _spec((TM, D_MODEL), lambda i: (i, 0)), w, _spec((D_MODEL, D_MODEL), lambda i: (0, 0)), NN,
                   (N_TM,), 1, jax.ShapeDtypeStruct((SEQ, D_MODEL), F32), _spec((TM, D_MODEL), lambda i: (i, 0)),
                   res=x, res_spec=_spec((TM, D_MODEL), lambda i: (i, 0)))


def _down_nn(name, a_blocks, w_blocks, x):
    nb = a_blocks.shape[0]
    return _matmul(name, a_blocks, _spec((None, TM, FF_BLOCK), lambda i, k: (k, i, 0)), w_blocks,
                   _spec((None, FF_BLOCK, D_MODEL), lambda i, k: (k, 0, 0)), NN, (N_TM, nb), nb,
                   jax.ShapeDtypeStruct((SEQ, D_MODEL), F32), _spec((TM, D_MODEL), lambda i, k: (i, 0)),
                   acc_shape=(TM, D_MODEL), res=x, res_spec=_spec((TM, D_MODEL), lambda i, k: (i, 0)))


def _back_nt(name, dy, w):
    n = w.shape[0]
    return _matmul(name, dy, _spec((TM, D_MODEL), lambda i: (i, 0)), w, _spec((n, D_MODEL), lambda i: (0, 0)), NT,
                   (N_TM,), 1, jax.ShapeDtypeStruct((SEQ, n), BF16), _spec((TM, n), lambda i: (i, 0)))


def _back_blocks_nt(name, dy, w_blocks):
    nb = w_blocks.shape[0]
    return _matmul(name, dy, _spec((TM, D_MODEL), lambda j, i: (i, 0)), w_blocks,
                   _spec((None, FF_BLOCK, D_MODEL), lambda j, i: (j, 0, 0)), NT, (nb, N_TM), 1,
                   jax.ShapeDtypeStruct((nb, SEQ, FF_BLOCK), BF16), _spec((None, TM, FF_BLOCK), lambda j, i: (j, i, 0)))


def _back_sum_blocks_nt(name, d_blocks, w_blocks):
    nb, _, n = d_blocks.shape
    return _matmul(name, d_blocks, _spec((None, TM, n), lambda i, k: (k, i, 0)), w_blocks,
                   _spec((None, D_MODEL, n), lambda i, k: (k, 0, 0)), NT, (N_TM, nb), nb,
                   jax.ShapeDtypeStruct((SEQ, D_MODEL), F32), _spec((TM, D_MODEL), lambda i, k: (i, 0)), acc_shape=(TM, D_MODEL))


def _back_sum_cols_nt(name, d, w_blocks=None, w=None, n_tile=None):
    nb = d.shape[1] // n_tile
    if w_blocks is not None:
        b, b_spec = w_blocks, _spec((None, D_MODEL, n_tile), lambda i, k: (k, 0, 0))
    else:
        b, b_spec = w, _spec((D_MODEL, n_tile), lambda i, k: (0, k))
    return _matmul(name, d, _spec((TM, n_tile), lambda i, k: (i, k)), b, b_spec, NT, (N_TM, nb), nb,
                   jax.ShapeDtypeStruct((SEQ, D_MODEL), F32), _spec((TM, D_MODEL), lambda i, k: (i, 0)), acc_shape=(TM, D_MODEL))


def _wgrad_tn(name, a, a_cols, d, d_cols, out_blocks):
    nb = d.shape[1] // d_cols
    if out_blocks:
        out_shape = jax.ShapeDtypeStruct((nb, a_cols, d_cols), BF16)
        out_spec = _spec((None, a_cols, d_cols), lambda j, k: (j, 0, 0))
    else:
        out_shape = jax.ShapeDtypeStruct((a_cols, nb * d_cols), BF16)
        out_spec = _spec((a_cols, d_cols), lambda j, k: (0, j))
    return _matmul(name, a, _spec((TKS, a_cols), lambda j, k: (k, 0)), d, _spec((TKS, d_cols), lambda j, k: (k, j)), TN,
                   (nb, N_TKS), N_TKS, out_shape, out_spec, acc_shape=(a_cols, d_cols))


def _wgrad_a_blocks_tn(name, a_blocks, d):
    nb = a_blocks.shape[0]
    return _matmul(name, a_blocks, _spec((None, TKS, FF_BLOCK), lambda j, k: (j, k, 0)), d,
                   _spec((TKS, D_MODEL), lambda j, k: (k, 0)), TN, (nb, N_TKS), N_TKS,
                   jax.ShapeDtypeStruct((nb, FF_BLOCK, D_MODEL), BF16), _spec((None, FF_BLOCK, D_MODEL), lambda j, k: (j, 0, 0)),
                   acc_shape=(FF_BLOCK, D_MODEL))


def _wgrad_d_blocks_tn(name, a, d_blocks):
    nb = d_blocks.shape[0]
    return _matmul(name, a, _spec((TKS, D_MODEL), lambda j, k: (k, 0)), d_blocks,
                   _spec((None, TKS, FF_BLOCK), lambda j, k: (j, k, 0)), TN, (nb, N_TKS), N_TKS,
                   jax.ShapeDtypeStruct((nb, D_MODEL, FF_BLOCK), BF16), _spec((None, D_MODEL, FF_BLOCK), lambda j, k: (j, 0, 0)),
                   acc_shape=(D_MODEL, FF_BLOCK))


NORM_ROWS = 512


def _rstd(x):
    return lax.rsqrt(jnp.mean(x * x, axis=-1, keepdims=True) + RMS_EPS)


def _norm_fwd(name, x, gamma):
    def body(x_ref, g_ref, h_ref):
        x = x_ref[...]
        h_ref[...] = (x * _rstd(x) * g_ref[...]).astype(BF16)

    row = _spec((NORM_ROWS, D_MODEL), lambda i: (i, 0))
    return pl.pallas_call(
        body, name=name, grid=(SEQ // NORM_ROWS,), in_specs=[row, _spec((1, D_MODEL), lambda i: (0, 0))], out_specs=row,
        out_shape=jax.ShapeDtypeStruct((SEQ, D_MODEL), BF16), compiler_params=_params(("parallel",)),
    )(x, gamma)


def _norm_bwd_rows(x, gamma, dh):
    r = _rstd(x)
    xh = x * r
    dxh = dh * gamma
    dx = r * (dxh - xh * jnp.mean(dxh * xh, axis=-1, keepdims=True))
    return dx, jnp.sum(dh * xh, axis=0, keepdims=True)


def _norm_bwd(name, x, gamma, dh, dx_in):
    def body(x_ref, g_ref, dh_ref, dxi_ref, dx_ref, dg_ref):
        dx, dg = _norm_bwd_rows(x_ref[...], g_ref[...], dh_ref[...].astype(F32))
        dx_ref[...] = dxi_ref[...] + dx

        @pl.when(pl.program_id(0) == 0)
        def _():
            dg_ref[...] = dg

        @pl.when(pl.program_id(0) > 0)
        def _():
            dg_ref[...] += dg

    row = _spec((NORM_ROWS, D_MODEL), lambda i: (i, 0))
    vec = _spec((1, D_MODEL), lambda i: (0, 0))
    return pl.pallas_call(
        body, name=name, grid=(SEQ // NORM_ROWS,), in_specs=[row, vec, row, row], out_specs=[row, vec],
        out_shape=[jax.ShapeDtypeStruct((SEQ, D_MODEL), F32), jax.ShapeDtypeStruct((1, D_MODEL), F32)],
        compiler_params=_params(("arbitrary",)),
    )(x, gamma, dh, dx_in)


def _loss_head(x, gamma, target):
    def body(x_ref, g_ref, t_ref, loss_ref, dx_ref, dg_ref):
        x = x_ref[...]
        gamma = g_ref[...]
        err = x * _rstd(x) * gamma - t_ref[...]
        dy = err * (1.0 / D_MODEL)
        dx, dg = _norm_bwd_rows(x, gamma, dy)
        dx_ref[...] = dx
        part = 0.5 * jnp.sum(jnp.sum(err * err, axis=-1, keepdims=True) * (1.0 / D_MODEL), axis=0, keepdims=True)
        part = jnp.broadcast_to(part, loss_ref.shape)

        @pl.when(pl.program_id(0) == 0)
        def _():
            dg_ref[...] = dg
            loss_ref[...] = part

        @pl.when(pl.program_id(0) > 0)
        def _():
            dg_ref[...] += dg
            loss_ref[...] += part

    row = _spec((NORM_ROWS, D_MODEL), lambda i: (i, 0))
    vec = _spec((1, D_MODEL), lambda i: (0, 0))
    return pl.pallas_call(
        body, name="loss_head", grid=(SEQ // NORM_ROWS,), in_specs=[row, vec, row],
        out_specs=[_spec((1, 128), lambda i: (0, 0)), row, vec],
        out_shape=[jax.ShapeDtypeStruct((1, 128), F32), jax.ShapeDtypeStruct((SEQ, D_MODEL), F32),
                   jax.ShapeDtypeStruct((1, D_MODEL), F32)],
        compiler_params=_params(("arbitrary",)),
    )(x, gamma, target)


def _sigmoid(x):
    return 1.0 / (1.0 + jnp.exp(-x))


def _rows(ref, c):
    return ref[pl.ds(pl.multiple_of(c * ROW_CHUNK, ROW_CHUNK), ROW_CHUNK), :].astype(F32)


def _rows_before(ref, c):
    start = pl.multiple_of(jnp.maximum(c * ROW_CHUNK - HALO, 0), HALO)
    rows = ref[pl.ds(start, HALO), :].astype(F32)
    return jnp.where(c > 0, rows, 0.0)


def _rows_after(ref, c, n_chunks):
    start = pl.multiple_of(jnp.minimum((c + 1) * ROW_CHUNK, SEQ - HALO), HALO)
    rows = ref[pl.ds(start, HALO), :].astype(F32)
    return jnp.where(c < n_chunks - 1, rows, 0.0)


def _shift_down(z, before, n):
    row = lax.broadcasted_iota(jnp.int32, z.shape, 0)
    out = pltpu.roll(z, n, 0)
    for r in range(n):
        out = jnp.where(row == r, before[HALO - n + r:HALO - n + r + 1, :], out)
    return out


def _shift_up(z, after, n):
    rows = z.shape[0]
    row = lax.broadcasted_iota(jnp.int32, z.shape, 0)
    out = pltpu.roll(z, rows - n, 0)
    for r in range(n):
        out = jnp.where(row == rows - n + r, after[r:r + 1, :], out)
    return out


def _conv_rows(z, before, w):
    z1 = _shift_down(z, before, 1)
    z2 = _shift_down(z, before, 2)
    return w[2:3, :] * z + w[1:2, :] * z1 + w[0:1, :] * z2, z1, z2


def _conv_t_rows(dy, after, w):
    return w[2:3, :] * dy + w[1:2, :] * _shift_up(dy, after, 1) + w[0:1, :] * _shift_up(dy, after, 2)


N_ROW_CHUNKS = SEQ // ROW_CHUNK


def _ffn_mid_fwd(name, gu, conv_w):
    def body(gu_ref, w_ref, a_ref):
        w = w_ref[...]

        def chunk(c, carry):
            g = _rows(gu_ref.at[0], c)
            u = _rows(gu_ref.at[1], c)
            gc, _, _ = _conv_rows(g, _rows_before(gu_ref.at[0], c), w)
            a_ref[pl.ds(pl.multiple_of(c * ROW_CHUNK, ROW_CHUNK), ROW_CHUNK), :] = (gc * _sigmoid(gc) * u).astype(BF16)
            return carry

        lax.fori_loop(0, N_ROW_CHUNKS, chunk, 0)

    return pl.pallas_call(
        body, name=name, grid=(FF_BLOCKS,),
        in_specs=[_spec((2, None, SEQ, FF_BLOCK), lambda j: (0, j, 0, 0)), _spec((None, 3, FF_BLOCK), lambda j: (j, 0, 0))],
        out_specs=_spec((None, SEQ, FF_BLOCK), lambda j: (j, 0, 0)),
        out_shape=jax.ShapeDtypeStruct((FF_BLOCKS, SEQ, FF_BLOCK), BF16), compiler_params=_params(("parallel",)),
    )(gu, conv_w)


def _ffn_mid_bwd(name, gu, conv_w, da):
    def body(gu_ref, w_ref, da_ref, dgu_ref, dw_ref, dgc_ref):
        w = w_ref[...]

        def first(c, acc):
            g = _rows(gu_ref.at[0], c)
            u = _rows(gu_ref.at[1], c)
            d = _rows(da_ref, c)
            gc, g1, g2 = _conv_rows(g, _rows_before(gu_ref.at[0], c), w)
            sg = _sigmoid(gc)
            rows = pl.ds(pl.multiple_of(c * ROW_CHUNK, ROW_CHUNK), ROW_CHUNK)
            dgu_ref[1, rows, :] = (d * gc * sg).astype(BF16)
            dgc = d * u * (sg * (1.0 + gc * (1.0 - sg)))
            dgc_ref[rows, :] = dgc
            return (acc[0] + jnp.sum(dgc * g2, axis=0, keepdims=True), acc[1] + jnp.sum(dgc * g1, axis=0, keepdims=True),
                    acc[2] + jnp.sum(dgc * g, axis=0, keepdims=True))

        zero = jnp.zeros((1, FF_BLOCK), F32)
        acc = lax.fori_loop(0, N_ROW_CHUNKS, first, (zero, zero, zero))
        for r in range(3):
            dw_ref[r:r + 1, :] = acc[r]

        def second(c, carry):
            dgc = _rows(dgc_ref, c)
            dg = _conv_t_rows(dgc, _rows_after(dgc_ref, c, N_ROW_CHUNKS), w)
            dgu_ref[0, pl.ds(pl.multiple_of(c * ROW_CHUNK, ROW_CHUNK), ROW_CHUNK), :] = dg.astype(BF16)
            return carry

        lax.fori_loop(0, N_ROW_CHUNKS, second, 0)

    pair = _spec((2, None, SEQ, FF_BLOCK), lambda j: (0, j, 0, 0))
    wspec = _spec((None, 3, FF_BLOCK), lambda j: (j, 0, 0))
    return pl.pallas_call(
        body, name=name, grid=(FF_BLOCKS,),
        in_specs=[pair, wspec, _spec((None, SEQ, FF_BLOCK), lambda j: (j, 0, 0))], out_specs=[pair, wspec],
        out_shape=[jax.ShapeDtypeStruct((2, FF_BLOCKS, SEQ, FF_BLOCK), BF16), jax.ShapeDtypeStruct((FF_BLOCKS, 3, FF_BLOCK), F32)],
        scratch_shapes=[pltpu.VMEM((SEQ, FF_BLOCK), F32)], compiler_params=_params(("parallel",)),
    )(gu, conv_w, da)


SC_COLS = 256
N_SC = D_MODEL // SC_COLS


def _sc_specs():
    return [_spec((SEQ, SC_COLS), lambda j, part=part: (0, part * N_SC + j)) for part in range(3)]


def _sc_mid_fwd(p, conv_w):
    def body(b_ref, c_ref, h_ref, w_ref, y_ref):
        w = w_ref[...]

        def chunk(c, carry):
            z = _rows(c_ref, c) * _rows(h_ref, c)
            before = _rows_before(c_ref, c) * _rows_before(h_ref, c)
            zc, _, _ = _conv_rows(z, before, w)
            y_ref[pl.ds(pl.multiple_of(c * ROW_CHUNK, ROW_CHUNK), ROW_CHUNK), :] = (_rows(b_ref, c) * zc).astype(BF16)
            return carry

        lax.fori_loop(0, N_ROW_CHUNKS, chunk, 0)

    col = _spec((SEQ, SC_COLS), lambda j: (0, j))
    return pl.pallas_call(
        body, name="sc_mid_fwd", grid=(N_SC,), in_specs=_sc_specs() + [_spec((3, SC_COLS), lambda j: (0, j))], out_specs=col,
        out_shape=jax.ShapeDtypeStruct((SEQ, D_MODEL), BF16), compiler_params=_params(("parallel",)),
    )(p, p, p, conv_w)


def _sc_mid_bwd(p, conv_w, dy):
    def body(b_ref, c_ref, h_ref, w_ref, dy_ref, db_ref, dc_ref, dh_ref, dw_ref, dzc_ref):
        w = w_ref[...]

        def first(c, acc):
            z = _rows(c_ref, c) * _rows(h_ref, c)
            before = _rows_before(c_ref, c) * _rows_before(h_ref, c)
            zc, z1, z2 = _conv_rows(z, before, w)
            d = _rows(dy_ref, c)
            rows = pl.ds(pl.multiple_of(c * ROW_CHUNK, ROW_CHUNK), ROW_CHUNK)
            db_ref[rows, :] = (d * zc).astype(BF16)
            dzc = d * _rows(b_ref, c)
            dzc_ref[rows, :] = dzc
            return (acc[0] + jnp.sum(dzc * z2, axis=0, keepdims=True), acc[1] + jnp.sum(dzc * z1, axis=0, keepdims=True),
                    acc[2] + jnp.sum(dzc * z, axis=0, keepdims=True))

        zero = jnp.zeros((1, SC_COLS), F32)
        acc = lax.fori_loop(0, N_ROW_CHUNKS, first, (zero, zero, zero))
        for r in range(3):
            dw_ref[r:r + 1, :] = acc[r]

        def second(c, carry):
            dz = _conv_t_rows(_rows(dzc_ref, c), _rows_after(dzc_ref, c, N_ROW_CHUNKS), w)
            rows = pl.ds(pl.multiple_of(c * ROW_CHUNK, ROW_CHUNK), ROW_CHUNK)
            dc_ref[rows, :] = (dz * _rows(h_ref, c)).astype(BF16)
            dh_ref[rows, :] = (dz * _rows(c_ref, c)).astype(BF16)
            return carry

        lax.fori_loop(0, N_ROW_CHUNKS, second, 0)

    col = _spec((SEQ, SC_COLS), lambda j: (0, j))
    wspec = _spec((3, SC_COLS), lambda j: (0, j))
    act = jax.ShapeDtypeStruct((SEQ, D_MODEL), BF16)
    return pl.pallas_call(
        body, name="sc_mid_bwd", grid=(N_SC,), in_specs=_sc_specs() + [wspec, col], out_specs=[col, col, col, wspec],
        out_shape=[act, act, act, jax.ShapeDtypeStruct((3, D_MODEL), F32)],
        scratch_shapes=[pltpu.VMEM((SEQ, SC_COLS), F32)], compiler_params=_params(("parallel",)),
    )(p, p, p, conv_w, dy)


GLA_GROUP = 4
GLA_ROWS = GLA_GROUP * CHUNK
N_GROUPS = N_CHUNKS // GLA_GROUP
Q0, K0, V0, R0, G0 = 0, KEY_DIM, 2 * KEY_DIM, 2 * KEY_DIM + VALUE_DIM, 2 * KEY_DIM + 2 * VALUE_DIM


def _tri(strict):
    r = lax.broadcasted_iota(jnp.int32, (CHUNK, CHUNK), 0)
    c = lax.broadcasted_iota(jnp.int32, (CHUNK, CHUNK), 1)
    return jnp.where(c < r if strict else c <= r, 1.0, 0.0).astype(F32)


def _cumsum_rows(tri, x):
    return jnp.dot(tri, x, preferred_element_type=F32, precision=lax.Precision.HIGHEST)


def _gate_logits(gl, wgu, b_gate):
    return jnp.dot(gl, wgu, preferred_element_type=F32) + b_gate


def _log_decay(logits):
    return (jnp.minimum(logits, 0.0) - jnp.log(1.0 + jnp.exp(-jnp.abs(logits)))) * (1.0 / GATE_NORMALIZER)


def _head(x, h, width):
    return x[:, h * width:(h + 1) * width]


def _gla_fwd(proj, wgu, b_gate, gn):
    def body(p_ref, wgu_ref, b_ref, gn_ref, o_ref, og_ref, st_ref, state):
        @pl.when(pl.program_id(0) == 0)
        def _():
            state[...] = jnp.zeros_like(state)

        tri = _tri(False)
        la = _log_decay(_gate_logits(p_ref[:, G0:G0 + GATE_PAD], wgu_ref[...], b_ref[...]))
        for c in range(GLA_GROUP):
            rows = slice(c * CHUNK, (c + 1) * CHUNK)
            cum = _cumsum_rows(tri, la[rows])
            tot = cum[CHUNK - 1:CHUNK, :]
            kd = (p_ref[rows, K0:K0 + KEY_DIM].astype(F32) * jnp.exp(tot - cum)).astype(BF16)
            decay = jnp.exp(tot)
            q = (p_ref[rows, Q0:Q0 + KEY_DIM].astype(F32) * (HEAD_K ** -0.5)).astype(BF16)
            v = p_ref[rows, V0:V0 + VALUE_DIM]
            for h in range(GLA_HEADS):
                upd = lax.dot_general(_head(v, h, HEAD_V), _head(kd, h, HEAD_K), (TN, ((), ())), preferred_element_type=F32)
                s = state[h] * _head(decay, h, HEAD_K) + upd
                state[h] = s
                st_ref[c, h] = s
                o_ref[rows, h * HEAD_V:(h + 1) * HEAD_V] = lax.dot_general(
                    _head(q, h, HEAD_K), s.astype(BF16), (NT, ((), ())), preferred_element_type=F32)
        r = p_ref[:, R0:R0 + VALUE_DIM].astype(F32)
        gate = r * _sigmoid(r) * gn_ref[...]
        for h in range(GLA_HEADS):
            cols = slice(h * HEAD_V, (h + 1) * HEAD_V)
            o = o_ref[:, cols]
            og_ref[:, cols] = (o * _rstd(o) * gate[:, cols]).astype(BF16)

    rows = _spec((GLA_ROWS, VALUE_DIM), lambda i: (i, 0))
    const = lambda shape: _spec(shape, lambda i: (0,) * len(shape))
    return pl.pallas_call(
        body, name="gla_fwd", grid=(N_GROUPS,),
        in_specs=[_spec((GLA_ROWS, PROJ_A_PAD), lambda i: (i, 0)), const((GATE_PAD, KEY_DIM)), const((1, KEY_DIM)),
                  const((1, VALUE_DIM))],
        out_specs=[rows, rows, _spec((GLA_GROUP, GLA_HEADS, HEAD_V, HEAD_K), lambda i: (i, 0, 0, 0))],
        out_shape=[jax.ShapeDtypeStruct((SEQ, VALUE_DIM), F32), jax.ShapeDtypeStruct((SEQ, VALUE_DIM), BF16),
                   jax.ShapeDtypeStruct((N_CHUNKS, GLA_HEADS, HEAD_V, HEAD_K), F32)],
        scratch_shapes=[pltpu.VMEM((GLA_HEADS, HEAD_V, HEAD_K), F32)], compiler_params=_params(("arbitrary",)),
    )(proj, wgu, b_gate, gn)


def _gla_bwd(proj, wgu, b_gate, gn, o, states, dog):
    last = N_GROUPS - 1

    def body(p_ref, wgu_ref, b_ref, gn_ref, o_ref, st_ref, stp_ref, dog_ref, dp_ref, dwgu_ref, db_ref, dgn_ref, carry, do_buf):
        step = pl.program_id(0)

        @pl.when(step == 0)
        def _():
            carry[...] = jnp.zeros_like(carry)

        r = p_ref[:, R0:R0 + VALUE_DIM].astype(F32)
        sr = _sigmoid(r)
        silu = r * sr
        gn_row = gn_ref[...]
        dog_rows = dog_ref[...].astype(F32)
        dn = dog_rows * silu
        dgn_cols = []
        for h in range(GLA_HEADS):
            cols = slice(h * HEAD_V, (h + 1) * HEAD_V)
            oh = o_ref[:, cols]
            rs = _rstd(oh)
            ohat = oh * rs
            dn_h = dn[:, cols]
            dgn_cols.append(jnp.sum(dn_h * ohat, axis=0, keepdims=True))
            dohat = dn_h * gn_row[:, cols]
            do_buf[:, cols] = rs * (dohat - ohat * jnp.mean(dohat * ohat, axis=-1, keepdims=True))
            n_h = ohat * gn_row[:, cols]
            dp_ref[:, R0 + h * HEAD_V:R0 + (h + 1) * HEAD_V] = (
                dog_rows[:, cols] * n_h * (sr[:, cols] * (1.0 + r[:, cols] * (1.0 - sr[:, cols])))).astype(BF16)
        dgn = jnp.concatenate(dgn_cols, axis=1)

        tri = _tri(False)
        tri_strict = _tri(True)
        gl = p_ref[:, G0:G0 + GATE_PAD]
        logits = _gate_logits(gl, wgu_ref[...], b_ref[...])
        la = _log_decay(logits)
        dlogit_rows = []
        for c in reversed(range(GLA_GROUP)):
            rows = slice(c * CHUNK, (c + 1) * CHUNK)
            cum = _cumsum_rows(tri, la[rows])
            tot = cum[CHUNK - 1:CHUNK, :]
            fade = jnp.exp(tot - cum)
            k = p_ref[rows, K0:K0 + KEY_DIM].astype(F32)
            kd32 = k * fade
            kd = kd32.astype(BF16)
            decay = jnp.exp(tot)
            q = (p_ref[rows, Q0:Q0 + KEY_DIM].astype(F32) * (HEAD_K ** -0.5)).astype(BF16)
            v = p_ref[rows, V0:V0 + VALUE_DIM]
            do = do_buf[rows, :].astype(BF16)
            dkd_cols, ddecay_cols = [], []
            for h in range(GLA_HEADS):
                do_h = _head(do, h, HEAD_V)
                s = st_ref[c, h]
                dq = jnp.dot(do_h, s.astype(BF16), preferred_element_type=F32) * (HEAD_K ** -0.5)
                dp_ref[rows, Q0 + h * HEAD_K:Q0 + (h + 1) * HEAD_K] = dq.astype(BF16)
                g = carry[h] + lax.dot_general(do_h, _head(q, h, HEAD_K), (TN, ((), ())), preferred_element_type=F32)
                g16 = g.astype(BF16)
                dkd_cols.append(jnp.dot(_head(v, h, HEAD_V), g16, preferred_element_type=F32))
                dv = lax.dot_general(_head(kd, h, HEAD_K), g16, (NT, ((), ())), preferred_element_type=F32)
                dp_ref[rows, V0 + h * HEAD_V:V0 + (h + 1) * HEAD_V] = dv.astype(BF16)
                if c > 0:
                    s_prev = st_ref[c - 1, h]
                else:
                    s_prev = jnp.where(step < last, stp_ref[0, h], 0.0)
                ddecay_cols.append(jnp.sum(g * s_prev, axis=0, keepdims=True))
                carry[h] = g * _head(decay, h, HEAD_K)
            dkd = jnp.concatenate(dkd_cols, axis=1)
            ddecay = jnp.concatenate(ddecay_cols, axis=1)
            dp_ref[rows, K0:K0 + KEY_DIM] = (dkd * fade).astype(BF16)
            e = dkd * kd32
            dla = ddecay * decay + _cumsum_rows(tri_strict, e)
            dlogit_rows.append(dla * (1.0 / GATE_NORMALIZER) * (1.0 - _sigmoid(logits[rows])))
        dlogit = jnp.concatenate(dlogit_rows[::-1], axis=0)
        dlogit16 = dlogit.astype(BF16)
        dp_ref[:, G0:G0 + GATE_PAD] = lax.dot_general(
            dlogit16, wgu_ref[...], (NT, ((), ())), preferred_element_type=F32).astype(BF16)
        dwgu = lax.dot_general(gl, dlogit16, (TN, ((), ())), preferred_element_type=F32)
        db = jnp.sum(dlogit, axis=0, keepdims=True)

        @pl.when(step == 0)
        def _():
            dwgu_ref[...] = dwgu
            db_ref[...] = db
            dgn_ref[...] = dgn

        @pl.when(step > 0)
        def _():
            dwgu_ref[...] += dwgu
            db_ref[...] += db
            dgn_ref[...] += dgn

    rev = lambda i: (last - i, 0)
    rows = _spec((GLA_ROWS, VALUE_DIM), rev)
    const = lambda shape: _spec(shape, lambda i: (0,) * len(shape))
    st_shape = (GLA_HEADS, HEAD_V, HEAD_K)
    return pl.pallas_call(
        body, name="gla_bwd", grid=(N_GROUPS,),
        in_specs=[_spec((GLA_ROWS, PROJ_A_PAD), rev), const((GATE_PAD, KEY_DIM)), const((1, KEY_DIM)), const((1, VALUE_DIM)),
                  rows, _spec((GLA_GROUP,) + st_shape, lambda i: (last - i, 0, 0, 0)),
                  _spec((1,) + st_shape, lambda i: (jnp.maximum((last - i) * GLA_GROUP - 1, 0), 0, 0, 0)), rows],
        out_specs=[_spec((GLA_ROWS, PROJ_A_PAD), rev), const((GATE_PAD, KEY_DIM)), const((1, KEY_DIM)), const((1, VALUE_DIM))],
        out_shape=[jax.ShapeDtypeStruct((SEQ, PROJ_A_PAD), BF16), jax.ShapeDtypeStruct((GATE_PAD, KEY_DIM), F32),
                   jax.ShapeDtypeStruct((1, KEY_DIM), F32), jax.ShapeDtypeStruct((1, VALUE_DIM), F32)],
        scratch_shapes=[pltpu.VMEM(st_shape, F32), pltpu.VMEM((GLA_ROWS, VALUE_DIM), F32)],
        compiler_params=_params(("arbitrary",)),
    )(proj, wgu, b_gate, gn, o, states, states, dog)


def _ffn_fwd(tag, x, gamma, w_up, conv_w, w_down):
    h = _norm_fwd(f"ffn{tag}_norm", x, gamma)
    gu = _proj_blocks_nn(f"ffn{tag}_up", h, w_up, FF_BLOCK, False).reshape(2, FF_BLOCKS, SEQ, FF_BLOCK)
    a = _ffn_mid_fwd(f"ffn{tag}_mid", gu, conv_w)
    return _down_nn(f"ffn{tag}_down", a, w_down, x), (h, gu, a)


def _ffn_bwd(tag, x, gamma, w_up, conv_w, w_down, saved, dx):
    h, gu, a = saved
    da = _back_blocks_nt(f"ffn{tag}_da", dx, w_down)
    d_w_down = _wgrad_a_blocks_tn(f"ffn{tag}_dwdown", a, dx)
    dgu, d_conv = _ffn_mid_bwd(f"ffn{tag}_mid_bwd", gu, conv_w, da)
    dgu = dgu.reshape(2 * FF_BLOCKS, SEQ, FF_BLOCK)
    dh = _back_sum_blocks_nt(f"ffn{tag}_dh", dgu, w_up)
    d_w_up = _wgrad_d_blocks_tn(f"ffn{tag}_dwup", h, dgu)
    dx, d_gamma = _norm_bwd(f"ffn{tag}_norm_bwd", x, gamma, dh, dx)
    return dx, d_gamma, d_w_up, d_conv, d_w_down


def _local_step(x, target, w, fetch=None, emit=None):
    if fetch is None:
        local = dict(a=(w.get("a_w_in"), w.get("a_w_out")), b=(w.get("b_w_in"), w.get("b_w_out")))
        for layer in range(2):
            local[f"f{layer}"] = (w["f_w_up"][layer], w["f_w_down"][layer]) if "f_w_up" in w else None
        fetch = lambda group, after: local[group]
    if emit is None:
        emit = lambda group, grads, dx: dx
    f_norm = (w["f_norm"][0:1], w["f_norm"][1:2])

    x0 = x
    a_w_in, a_w_out = fetch("a", x0)
    h0 = _norm_fwd("a_norm", x0, w["a_norm"])
    proj = _proj_nn("a_in", h0, a_w_in, PA_TILE, N_PA)
    o, og, states = _gla_fwd(proj, w["a_w_gate_up"], w["a_b_gate"], w["a_gn"])
    x1 = _out_nn("a_out", og, a_w_out, x0)
    up0, down0 = fetch("f0", x1)
    x2, ffn0 = _ffn_fwd(0, x1, f_norm[0], up0, w["f_conv"][0], down0)
    b_w_in, b_w_out = fetch("b", x2)
    h2 = _norm_fwd("b_norm", x2, w["b_norm"])
    p = _proj_blocks_nn("b_in", h2, b_w_in, B_SHARD, True)
    y = _sc_mid_fwd(p, w["b_conv"])
    x3 = _out_nn("b_out", y, b_w_out, x2)
    up1, down1 = fetch("f1", x3)
    x4, ffn1 = _ffn_fwd(1, x3, f_norm[1], up1, w["f_conv"][1], down1)
    loss, dx, d_final_norm = _loss_head(x4, w["final_norm"], target)

    dx, d_f_norm1, d_up1, d_fconv1, d_down1 = _ffn_bwd(1, x3, f_norm[1], up1, w["f_conv"][1], down1, ffn1, dx)
    dx = emit("f1", (d_up1, d_down1), dx)

    dy = _back_nt("b_dy", dx, b_w_out)
    d_b_w_out = _wgrad_tn("b_dwout", y, D_MODEL, dx, D_MODEL, False)
    db, dc, dhh, d_b_conv = _sc_mid_bwd(p, w["b_conv"], dy)
    dp = jnp.concatenate([db, dc, dhh], axis=1)
    dh2 = _back_sum_cols_nt("b_dh", dp, w_blocks=b_w_in, n_tile=B_SHARD)
    d_b_w_in = _wgrad_tn("b_dwin", h2, D_MODEL, dp, B_SHARD, True)
    dx, d_b_norm = _norm_bwd("b_norm_bwd", x2, w["b_norm"], dh2, dx)
    dx = emit("b", (d_b_w_in, d_b_w_out), dx)

    dx, d_f_norm0, d_up0, d_fconv0, d_down0 = _ffn_bwd(0, x1, f_norm[0], up0, w["f_conv"][0], down0, ffn0, dx)
    dx = emit("f0", (d_up0, d_down0), dx)

    dog = _back_nt("a_dog", dx, a_w_out)
    d_a_w_out = _wgrad_tn("a_dwout", og, D_MODEL, dx, D_MODEL, False)
    dproj, d_wgu, d_b_gate, d_gn = _gla_bwd(proj, w["a_w_gate_up"], w["a_b_gate"], w["a_gn"], o, states, dog)
    dh0 = _back_sum_cols_nt("a_dh", dproj, w=a_w_in, n_tile=PA_TILE)
    d_a_w_in = _wgrad_tn("a_dwin", h0, D_MODEL, dproj, PA_TILE, False)
    dx, d_a_norm = _norm_bwd("a_norm_bwd", x0, w["a_norm"], dh0, dx)

    grads = dict(
        a_norm=d_a_norm, a_w_in=d_a_w_in, a_w_gate_up=d_wgu, a_b_gate=d_b_gate, a_gn=d_gn, a_w_out=d_a_w_out,
        b_norm=d_b_norm, b_w_in=d_b_w_in, b_conv=d_b_conv, b_w_out=d_b_w_out,
        f_norm=(d_f_norm0, d_f_norm1), f_w_up=(d_up0, d_up1), f_conv=(d_fconv0, d_fconv1), f_w_down=(d_down0, d_down1),
        final_norm=d_final_norm)
    return loss[0, 0], dx, grads


MESH_ID = pl.DeviceIdType.MESH
ANY = pl.BlockSpec(memory_space=pl.ANY)
N_PEERS = N_DEV - 1


def _position():
    return lax.axis_index("x"), lax.axis_index("y"), lax.axis_index("c")


def _slot(px, py, pc):
    return 4 * px + 2 * py + pc


def _all_gather(shards):
    n = len(shards)

    def body(*refs):
        ins, outs = refs[:n], refs[n:2 * n]
        send_sems, recv_sems, local_sems = refs[2 * n:]
        x, y, c = _position()
        me, sibling = (x, y, c), (x, y, 1 - c)
        chips = [(1 - x, y), (x, 1 - y), (1 - x, 1 - y)]

        def copy(t, k, block, to, from_input=False):
            dst = outs[t].at[_slot(*block)]
            return pltpu.make_async_remote_copy(
                src_ref=ins[t] if from_input else dst, dst_ref=dst, send_sem=send_sems.at[t, k], recv_sem=recv_sems.at[t, k],
                device_id=to, device_id_type=MESH_ID)

        mine = [pltpu.make_async_copy(ins[t], outs[t].at[_slot(*me)], local_sems.at[t]) for t in range(n)]
        for cp in mine:
            cp.start()
        first = []
        for t in range(n):
            first.append(copy(t, 0, me, sibling, True))
            first += [copy(t, 1 + j, me, (*chip, c), True) for j, chip in enumerate(chips)]
        for cp in first:
            cp.start()
        passed = []
        for t in range(n):
            for j, chip in enumerate(chips):
                copy(t, 1 + j, (*chip, c), me).wait_recv()
                fwd = copy(t, 4 + j, (*chip, c), sibling)
                fwd.start()
                passed.append(fwd)
        for t in range(n):
            copy(t, 0, sibling, me).wait_recv()
            for j, chip in enumerate(chips):
                copy(t, 4 + j, (*chip, 1 - c), me).wait_recv()
        for cp in first + passed:
            cp.wait_send()
        for cp in mine:
            cp.wait()

    return pl.pallas_call(
        body, name="weight_gather", in_specs=[ANY] * n, out_specs=[ANY] * n,
        out_shape=[jax.ShapeDtypeStruct((N_DEV,) + s.shape, s.dtype) for s in shards],
        scratch_shapes=[pltpu.SemaphoreType.DMA((n, N_PEERS)), pltpu.SemaphoreType.DMA((n, N_PEERS)), pltpu.SemaphoreType.DMA((n,))],
    )(*shards)


ALL_PEERS = (1, 2, 3, 4, 5, 6, 7)
SIBLING_AND_SAME_CORE = (1, 2, 4, 6)
SAME_CORE = (2, 4, 6)


def _flip(x, y, c, k):
    return x ^ (k >> 2), y ^ ((k >> 1) & 1), c ^ (k & 1)


def _send_copy(parts, landing, shared, send_sems, recv_sems, t, s, k):
    x, y, c = _position()
    peer = _flip(x, y, c, k)
    src = parts[t] if shared[t] else parts[t].at[_slot(*peer)]
    return pltpu.make_async_remote_copy(
        src_ref=src, dst_ref=landing[t].at[_slot(x, y, c)], send_sem=send_sems.at[s], recv_sem=recv_sems.at[s],
        device_id=peer, device_id_type=MESH_ID)


def _send_arrival(landing, send_sems, recv_sems, t, s, k):
    x, y, c = _position()
    peer = _flip(x, y, c, k)
    landed = landing[t].at[_slot(*peer)]
    return pltpu.make_async_remote_copy(
        src_ref=landed, dst_ref=landed, send_sem=send_sems.at[s], recv_sem=recv_sems.at[s],
        device_id=peer, device_id_type=MESH_ID)


def _handshake(peers):
    x, y, c = _position()
    barrier = pltpu.get_barrier_semaphore()
    for k in peers:
        pl.semaphore_signal(barrier, inc=1, device_id=_flip(x, y, c, k), device_id_type=MESH_ID)
    pl.semaphore_wait(barrier, len(peers))


def _sequencer(name, collective_id, n_copies, n_arrays, body):
    return pl.kernel(
        body, mesh=plsc.ScalarSubcoreMesh(axis_name="sequencer", num_cores=1), name=name,
        scratch_types=(pltpu.SemaphoreType.DMA((n_copies,)), pltpu.SemaphoreType.DMA((n_copies,)),
                       pltpu.SemaphoreType.DMA((n_arrays,))),
        compiler_params=pltpu.CompilerParams(collective_id=collective_id))


def _sequencer_exchange(name, collective_id, parts, shared):
    n, n_peers = len(parts), len(ALL_PEERS)
    src = [jax.new_ref(p, memory_space=pltpu.MemorySpace.HBM) for p in parts]
    landing = [jax.empty_ref(jax.ShapeDtypeStruct(((N_DEV,) + p.shape) if sh else p.shape, p.dtype),
                             memory_space=pltpu.MemorySpace.HBM) for p, sh in zip(parts, shared)]

    def body(send_sems, recv_sems, local_sems):
        _handshake(ALL_PEERS)
        my_slot = _slot(*_position())
        mine = [pltpu.make_async_copy(src[t] if shared[t] else src[t].at[my_slot], landing[t].at[my_slot], local_sems.at[t])
                for t in range(n)]
        for cp in mine:
            cp.start()
        sent = [_send_copy(src, landing, shared, send_sems, recv_sems, t, t * n_peers + j, k)
                for t in range(n) for j, k in enumerate(ALL_PEERS)]
        for cp in sent:
            cp.start()
        for t in range(n):
            for j, k in enumerate(ALL_PEERS):
                _send_arrival(landing, send_sems, recv_sems, t, t * n_peers + j, k).wait_recv()
        for cp in sent:
            cp.wait_send()
        for cp in mine:
            cp.wait()

    _sequencer(name, collective_id, n * n_peers, n, body)()
    return landing


def _sequencer_gather(name, collective_id, shards):
    n, per = len(shards), N_PEERS
    src = [jax.new_ref(s, memory_space=pltpu.MemorySpace.HBM) for s in shards]
    out = [jax.empty_ref(jax.ShapeDtypeStruct((N_DEV,) + s.shape, s.dtype), memory_space=pltpu.MemorySpace.HBM) for s in shards]

    def body(send_sems, recv_sems, local_sems):
        _handshake(SIBLING_AND_SAME_CORE)
        x, y, c = _position()
        me, sibling = (x, y, c), (x, y, 1 - c)

        def copy(t, j, block, to, from_input=False):
            dst = out[t].at[_slot(*block)]
            return pltpu.make_async_remote_copy(
                src_ref=src[t] if from_input else dst, dst_ref=dst, send_sem=send_sems.at[t * per + j],
                recv_sem=recv_sems.at[t * per + j], device_id=to, device_id_type=MESH_ID)

        mine = [pltpu.make_async_copy(src[t], out[t].at[_slot(*me)], local_sems.at[t]) for t in range(n)]
        for cp in mine:
            cp.start()
        sent = [copy(t, j, me, _flip(x, y, c, k), True) for t in range(n) for j, k in enumerate(SIBLING_AND_SAME_CORE)]
        for cp in sent:
            cp.start()
        for t in range(n):
            for j, k in enumerate(SAME_CORE):
                block = _flip(x, y, c, k)
                copy(t, 1 + j, block, me).wait_recv()
                forward = copy(t, 4 + j, block, sibling)
                forward.start()
                sent.append(forward)
        for t in range(n):
            copy(t, 0, sibling, me).wait_recv()
            for j, k in enumerate(SAME_CORE):
                copy(t, 4 + j, _flip(x, y, 1 - c, k), me).wait_recv()
        for cp in sent:
            cp.wait_send()
        for cp in mine:
            cp.wait()

    _sequencer(name, collective_id, n * per, n, body)()
    return out


ADAM_ROWS = 256


def _adam_update(w, g, m, v):
    m = ADAM_B1 * m + (1.0 - ADAM_B1) * g
    v = ADAM_B2 * v + (1.0 - ADAM_B2) * (g * g)
    m_hat = m / (1.0 - ADAM_B1 ** ADAM_STEP)
    v_hat = v / (1.0 - ADAM_B2 ** ADAM_STEP)
    delta = -ADAM_LR * (m_hat / (jnp.sqrt(v_hat) + ADAM_EPS) + ADAM_WD * w)
    return delta, m, v


def _sum_slots(ref):
    total = ref[0].astype(F32)
    for d in range(1, N_DEV):
        total = total + ref[d].astype(F32)
    return total


def _adamw_sum(name, landed, w, m, v):
    layers, rows, cols = w.shape
    tr = ADAM_ROWS if rows % ADAM_ROWS == 0 else rows
    nt = rows // tr

    def body(*refs):
        parts = refs[:layers]
        w_ref, m_ref, v_ref, g_ref, d_ref, nm_ref, nv_ref = refs[layers:]
        layer = pl.program_id(0)
        g = _sum_slots(parts[0])
        for q in range(1, layers):
            g = jnp.where(layer == q, _sum_slots(parts[q]), g)
        delta, new_m, new_v = _adam_update(w_ref[...], g, m_ref[...], v_ref[...])
        g_ref[...] = g
        d_ref[...] = delta
        nm_ref[...] = new_m
        nv_ref[...] = new_v

    def part_spec(q):
        return _spec((N_DEV, tr, cols), lambda l, i: (0, jnp.where(l == q, i, jnp.where(l < q, 0, nt - 1)), 0))

    tile = _spec((None, tr, cols), lambda l, i: (l, i, 0))
    out = jax.ShapeDtypeStruct((layers, rows, cols), F32)
    return pl.pallas_call(
        body, name=name, grid=(layers, nt), in_specs=[part_spec(q) for q in range(layers)] + [tile] * 3,
        out_specs=[tile] * 4, out_shape=[out] * 4, compiler_params=_params(("arbitrary", "arbitrary")),
    )(*landed, w, m, v)


def _sum_small(landed):
    def body(in_ref, out_ref):
        out_ref[...] = _sum_slots(in_ref)

    return pl.pallas_call(body, name="small_grad_sum", out_shape=jax.ShapeDtypeStruct(landed.shape[1:], F32))(landed)


def _adamw_small(name, g, w, m, v):
    def body(g_ref, w_ref, m_ref, v_ref, d_ref, nm_ref, nv_ref):
        d_ref[...], nm_ref[...], nv_ref[...] = _adam_update(w_ref[...], g_ref[...], m_ref[...], v_ref[...])

    out = jax.ShapeDtypeStruct(w.shape, F32)
    return pl.pallas_call(body, name=name, out_shape=[out] * 3)(g, w, m, v)


LANES = 128
SUBLANES = 8
F_CONV_SHARD = D_FF // N_DEV
GATE_SHARD = KEY_DIM // N_DEV
NORM_SHARD = D_MODEL // N_DEV


def _tile_rows(a):
    flat = a.reshape(-1)
    size = -(-flat.shape[0] // (SUBLANES * LANES)) * SUBLANES * LANES
    return jnp.pad(flat, (0, size - flat.shape[0])).reshape(-1, LANES)


def _pack_rows(pieces):
    return jnp.concatenate([_tile_rows(p) for p in pieces], axis=0)


def _unpack_rows(packed, shapes):
    out, row = [], 0
    for shape in shapes:
        size = 1
        for s in shape:
            size *= s
        rows = -(-size // (SUBLANES * LANES)) * SUBLANES
        piece = packed[..., row:row + rows, :]
        out.append(piece.reshape(piece.shape[:-2] + (rows * LANES,))[..., :size])
        row += rows
    return out


SMALL_SHARDS = ((GATE_RANK, GATE_SHARD), (1, NORM_SHARD), (3, NORM_SHARD), (2, 3, F_CONV_SHARD))


def _unpack_small_shards(g):
    gate, b_norm, b_conv, f_conv = _unpack_rows(g, SMALL_SHARDS)
    gate = gate.reshape(N_DEV, GATE_RANK, GATE_SHARD).transpose(1, 0, 2).reshape(GATE_RANK, KEY_DIM)
    b_norm = b_norm.reshape(1, D_MODEL)
    b_conv = b_conv.reshape(N_DEV, 3, NORM_SHARD).transpose(1, 0, 2).reshape(3, D_MODEL)
    f_conv = f_conv.reshape(N_DEV, 2, 3, F_CONV_SHARD).transpose(1, 2, 0, 3).reshape(2, 3, D_FF)
    return gate, b_norm, b_conv, f_conv


def _conv_blocks(f_conv):
    return f_conv.reshape(2, 3, FF_BLOCKS, FF_BLOCK).transpose(0, 2, 1, 3)


def _conv_unblocks(f_conv):
    return f_conv.transpose(1, 0, 2).reshape(3, D_FF)


SMALL_LAYOUT = (("a_norm", (1, D_MODEL)), ("a_w_gate_up", (GATE_RANK, KEY_DIM)), ("a_b_gate", (1, KEY_DIM)), ("a_gn", (1, VALUE_DIM)),
                ("b_norm", (1, D_MODEL)), ("b_conv", (3, D_MODEL)), ("f_norm0", (1, D_MODEL)), ("f_norm1", (1, D_MODEL)),
                ("f_conv0", (3, D_FF)), ("f_conv1", (3, D_FF)), ("final_norm", (1, D_MODEL)))


def _pack_small_grads(g):
    full = dict(g)
    full["a_w_gate_up"] = g["a_w_gate_up"][:GATE_RANK]
    for layer in range(2):
        full[f"f_norm{layer}"] = g["f_norm"][layer]
        full[f"f_conv{layer}"] = _conv_unblocks(g["f_conv"][layer])
    return _pack_rows([full[name] for name, _ in SMALL_LAYOUT])


def _unpack_small_grads(packed):
    pieces = _unpack_rows(packed, [shape for _, shape in SMALL_LAYOUT])
    out = {name: piece.reshape(shape) for (name, shape), piece in zip(SMALL_LAYOUT, pieces)}
    out["f_norm"] = jnp.stack([out["f_norm0"][0], out["f_norm1"][0]])
    out["f_conv"] = jnp.stack([out["f_conv0"], out["f_conv1"]])
    return out


def kernel(x, a_norm, a_w_in, a_w_gate_up, a_b_gate, a_gn, a_w_out, b_norm, b_w_in, b_conv, b_w_out, f_norm, f_w_up, f_conv, f_w_down, final_norm, loss_target, m_a_norm, m_a_w_in, m_a_w_gate_up, m_a_b_gate, m_a_gn, m_a_w_out, m_b_norm, m_b_w_in, m_b_conv, m_b_w_out, m_f_norm, m_f_w_up, m_f_conv, m_f_w_down, m_final_norm, v_a_norm, v_a_w_in, v_a_w_gate_up, v_a_b_gate, v_a_gn, v_a_w_out, v_b_norm, v_b_w_in, v_b_conv, v_b_w_out, v_f_norm, v_f_w_up, v_f_conv, v_f_w_down, v_final_norm):
    my_slot = _slot(*_position())

    first = _all_gather([a_w_in[0].astype(BF16), a_w_out[0].astype(BF16), _pack_rows([a_w_gate_up[0], b_norm, b_conv[0], f_conv])])
    gathers, small_shards = {}, first[2]
    later = (("f0", f_w_up[0], f_w_down[0]), ("b", b_w_in[0], b_w_out[0]), ("f1", f_w_up[1], f_w_down[1]))
    for collective_id, (group, w_in, w_out) in enumerate(later):
        w_in, w_out, small_shards = lax.optimization_barrier((w_in.astype(BF16), w_out.astype(BF16), small_shards))
        gathers[group] = _sequencer_gather(f"gather_{group}", collective_id, [w_in, w_out])
    gate_full, b_norm_full, b_conv_full, f_conv_full = _unpack_small_shards(small_shards)
    a_w_in_full = jnp.concatenate([first[0][d] for d in range(N_DEV)] + [jnp.zeros((D_MODEL, PROJ_A_PAD - PROJ_A), BF16)], axis=1)
    weights = dict(
        a_norm=a_norm, a_w_gate_up=jnp.pad(gate_full, ((0, GATE_PAD - GATE_RANK), (0, 0))).astype(BF16), a_b_gate=a_b_gate,
        a_gn=a_gn, b_norm=b_norm_full, b_conv=b_conv_full, f_norm=f_norm, f_conv=_conv_blocks(f_conv_full),
        final_norm=final_norm.reshape(1, D_MODEL))

    def fetch(group, after):
        if group == "a":
            return a_w_in_full, first[1].reshape(D_MODEL, D_MODEL)
        w_in, w_out = (ref[...] for ref in gathers[group])
        if group == "b":
            return w_in, w_out.reshape(D_MODEL, D_MODEL)
        return w_in, w_out.reshape(FF_BLOCKS, FF_BLOCK, D_MODEL)

    exchanges = {}

    def owner_blocks(d_out):
        return d_out.reshape((N_DEV, -1, D_MODEL))

    exchange_ids = dict(f1=3, b=4, f0=5, a=6)

    def emit(group, grads, dx):
        exchanges[group] = _sequencer_exchange(
            f"grads_{group}", exchange_ids[group], [grads[0], owner_blocks(grads[1])], [False, False])
        return dx

    loss, dx, g = _local_step(x[0], loss_target[0], weights, fetch, emit)
    loss = lax.psum(loss, MESH_AXES)
    d_a_w_in = jnp.stack([g["a_w_in"][:, d * A_SHARD:(d + 1) * A_SHARD] for d in range(N_DEV)])
    exchanges["a"] = _sequencer_exchange(
        "grads_a", exchange_ids["a"], [d_a_w_in, owner_blocks(g["a_w_out"]), _pack_small_grads(g)], [False, False, True])

    landed_b = [ref[...] for ref in exchanges["b"]]
    big = dict(
        b_w_in=_adamw_sum("adam_b_w_in", [landed_b[0]], b_w_in, m_b_w_in, v_b_w_in),
        b_w_out=_adamw_sum("adam_b_w_out", [landed_b[1]], b_w_out, m_b_w_out, v_b_w_out))
    landed_f0 = [ref[...] for ref in exchanges["f0"]]
    landed_f1 = [ref[...] for ref in exchanges["f1"]]
    big.update(
        f_w_up=_adamw_sum("adam_f_w_up", [landed_f0[0], landed_f1[0]], f_w_up, m_f_w_up, v_f_w_up),
        f_w_down=_adamw_sum("adam_f_w_down", [landed_f0[1], landed_f1[1]], f_w_down, m_f_w_down, v_f_w_down))
    landed_a = [ref[...] for ref in exchanges["a"]]
    big.update(
        a_w_in=_adamw_sum("adam_a_w_in", [landed_a[0]], a_w_in, m_a_w_in, v_a_w_in),
        a_w_out=_adamw_sum("adam_a_w_out", [landed_a[1]], a_w_out, m_a_w_out, v_a_w_out))
    small_g = _unpack_small_grads(_sum_small(landed_a[2]))
    small_g["a_w_gate_up"] = lax.dynamic_slice_in_dim(small_g["a_w_gate_up"], my_slot * GATE_SHARD, GATE_SHARD, axis=1)
    small_g["b_norm"] = lax.dynamic_slice_in_dim(small_g["b_norm"], my_slot * NORM_SHARD, NORM_SHARD, axis=1)
    small_g["b_conv"] = lax.dynamic_slice_in_dim(small_g["b_conv"], my_slot * NORM_SHARD, NORM_SHARD, axis=1)
    small_g["f_conv"] = lax.dynamic_slice_in_dim(small_g["f_conv"], my_slot * F_CONV_SHARD, F_CONV_SHARD, axis=2)
    small_w = dict(
        a_norm=(a_norm, m_a_norm, v_a_norm), a_w_gate_up=(a_w_gate_up, m_a_w_gate_up, v_a_w_gate_up),
        a_b_gate=(a_b_gate, m_a_b_gate, v_a_b_gate), a_gn=(a_gn, m_a_gn, v_a_gn), b_norm=(b_norm, m_b_norm, v_b_norm),
        b_conv=(b_conv, m_b_conv, v_b_conv), f_norm=(f_norm, m_f_norm, v_f_norm), f_conv=(f_conv, m_f_conv, v_f_conv),
        final_norm=(final_norm, m_final_norm, v_final_norm))
    small = {}
    for name, (w, m, v) in small_w.items():
        flat = (w.shape[-1],) if w.ndim == 1 else w.shape[-2:]
        two_d = (-1, flat[-1])
        grad = small_g[name].reshape(w.shape)
        delta, new_m, new_v = _adamw_small(
            "adam_" + name, grad.reshape(two_d), w.reshape(two_d), m.reshape(two_d), v.reshape(two_d))
        small[name] = (grad, delta.reshape(w.shape), new_m.reshape(w.shape), new_v.reshape(w.shape))

    order = ["a_norm", "a_w_in", "a_w_gate_up", "a_b_gate", "a_gn", "a_w_out", "b_norm", "b_w_in", "b_conv", "b_w_out",
             "f_norm", "f_w_up", "f_conv", "f_w_down", "final_norm"]
    results = {**big, **small}
    outputs = [loss, dx.reshape(1, SEQ, D_MODEL)]
    for kind in range(4):
        outputs += [results[name][kind] for name in order]
    return tuple(outputs)
```

```python
import jax
import jax.numpy as jnp
from jax import lax
from jax.experimental import pallas as pl
from jax.experimental.pallas import tpu as pltpu
from jax.experimental.pallas import tpu_sc as plsc

F32 = jnp.float32
BF16 = jnp.bfloat16

N_DEV = 8
SEQ = 2048
D_MODEL = 1024
CHUNK = 64
N_CHUNKS = SEQ // CHUNK
RMS_EPS = 1e-6
GLA_HEADS = 4
KEY_DIM = 512
VALUE_DIM = 1024
HEAD_K = KEY_DIM // GLA_HEADS
HEAD_V = VALUE_DIM // GLA_HEADS
GATE_RANK = 16
GATE_PAD = 128
GATE_NORMALIZER = 16.0
PROJ_A = 2 * KEY_DIM + 2 * VALUE_DIM + GATE_RANK
PROJ_A_PAD = 2 * KEY_DIM + 2 * VALUE_DIM + GATE_PAD
A_SHARD = PROJ_A // N_DEV
B_SHARD = 3 * D_MODEL // N_DEV
D_FF = 2816
FF_BLOCK = 2 * D_FF // N_DEV
FF_BLOCKS = D_FF // FF_BLOCK
ADAM_LR = 0.001
ADAM_B1 = 0.9
ADAM_B2 = 0.999
ADAM_EPS = 1e-08
ADAM_WD = 0.01
ADAM_STEP = 10
MESH_AXES = ("x", "y", "c")

VMEM_LIMIT = 56 * 1024 * 1024
ROW_CHUNK = 256
HALO = 16


def _params(sem=None, vmem=VMEM_LIMIT):
    return pltpu.CompilerParams(dimension_semantics=sem, vmem_limit_bytes=vmem)


NN = ((1,), (0,))
NT = ((1,), (1,))
TN = ((0,), (0,))


def _matmul(name, a, a_spec, b, b_spec, dims, grid, nk, out_shape, out_spec, acc_shape=None, res=None, res_spec=None):
    has_res = res is not None

    def body(*refs):
        a_ref, b_ref = refs[0], refs[1]
        r_ref = refs[2] if has_res else None
        o_ref = refs[2 + has_res]
        acc_ref = refs[3 + has_res] if nk > 1 else None

        def product():
            return lax.dot_general(a_ref[...].astype(BF16), b_ref[...].astype(BF16), (dims, ((), ())),
                                   preferred_element_type=F32)

        def finish(v):
            if has_res:
                v = v + r_ref[...]
            o_ref[...] = v.astype(o_ref.dtype)

        if nk == 1:
            finish(product())
        else:
            k = pl.program_id(len(grid) - 1)
            p = product()

            @pl.when(k == 0)
            def _():
                acc_ref[...] = p

            @pl.when(k > 0)
            def _():
                acc_ref[...] += p

            @pl.when(k == nk - 1)
            def _():
                finish(acc_ref[...])

    operands = [a, b] + ([res] if has_res else [])
    in_specs = [a_spec, b_spec] + ([res_spec] if has_res else [])
    sem = ("parallel",) * (len(grid) - 1) + (("arbitrary",) if nk > 1 else ("parallel",))
    return pl.pallas_call(
        body, name=name, grid=grid, in_specs=in_specs, out_specs=out_spec, out_shape=out_shape,
        scratch_shapes=[pltpu.VMEM(acc_shape, F32)] if nk > 1 else [],
        compiler_params=_params(sem),
    )(*operands)


TM = 1024
TKS = 1024
N_TM = SEQ // TM
N_TKS = SEQ // TKS
PA_TILE = 640
N_PA = PROJ_A_PAD // PA_TILE


def _spec(shape, fn):
    return pl.BlockSpec(shape, fn)


def _proj_nn(name, h, w, n_tile, n_tiles):
    n = n_tile * n_tiles
    return _matmul(name, h, _spec((TM, D_MODEL), lambda j, i: (i, 0)), w, _spec((D_MODEL, n_tile), lambda j, i: (0, j)), NN,
                   (n_tiles, N_TM), 1, jax.ShapeDtypeStruct((SEQ, n), BF16), _spec((TM, n_tile), lambda j, i: (i, j)))


def _proj_blocks_nn(name, h, w_blocks, n_tile, flat_out):
    nb = w_blocks.shape[0]
    if flat_out:
        out_shape = jax.ShapeDtypeStruct((SEQ, nb * n_tile), BF16)
        out_spec = _spec((TM, n_tile), lambda j, i: (i, j))
    else:
        out_shape = jax.ShapeDtypeStruct((nb, SEQ, n_tile), BF16)
        out_spec = _spec((None, TM, n_tile), lambda j, i: (j, i, 0))
    return _matmul(name, h, _spec((TM, D_MODEL), lambda j, i: (i, 0)), w_blocks,
                   _spec((None, D_MODEL, n_tile), lambda j, i: (j, 0, 0)), NN, (nb, N_TM), 1, out_shape, out_spec)


def _out_nn(name, a, w, x):
    return _matmul(name, a, _spec((TM, D_MODEL), lambda i: (i, 0)), w, _spec((D_MODEL, D_MODEL), lambda i: (0, 0)), NN,
                   (N_TM,), 1, jax.ShapeDtypeStruct((SEQ, D_MODEL), F32), _spec((TM, D_MODEL), lambda i: (i, 0)),
                   res=x, res_spec=_spec((TM, D_MODEL), lambda i: (i, 0)))


def _down_nn(name, a_blocks, w_blocks, x):
    nb = a_blocks.shape[0]
    return _matmul(name, a_blocks, _spec((None, TM, FF_BLOCK), lambda i, k: (k, i, 0)), w_blocks,
                   _spec((None, FF_BLOCK, D_MODEL), lambda i, k: (k, 0, 0)), NN, (N_TM, nb), nb,
                   jax.ShapeDtypeStruct((SEQ, D_MODEL), F32), _spec((TM, D_MODEL), lambda i, k: (i, 0)),
                   acc_shape=(TM, D_MODEL), res=x, res_spec=_spec((TM, D_MODEL), lambda i, k: (i, 0)))


def _back_nt(name, dy, w):
    n = w.shape[0]
    return _matmul(name, dy, _spec((TM, D_MODEL), lambda i: (i, 0)), w, _spec((n, D_MODEL), lambda i: (0, 0)), NT,
                   (N_TM,), 1, jax.ShapeDtypeStruct((SEQ, n), BF16), _spec((TM, n), lambda i: (i, 0)))


def _back_blocks_nt(name, dy, w_blocks):
    nb = w_blocks.shape[0]
    return _matmul(name, dy, _spec((TM, D_MODEL), lambda j, i: (i, 0)), w_blocks,
                   _spec((None, FF_BLOCK, D_MODEL), lambda j, i: (j, 0, 0)), NT, (nb, N_TM), 1,
                   jax.ShapeDtypeStruct((nb, SEQ, FF_BLOCK), BF16), _spec((None, TM, FF_BLOCK), lambda j, i: (j, i, 0)))


def _back_sum_blocks_nt(name, d_blocks, w_blocks):
    nb, _, n = d_blocks.shape
    return _matmul(name, d_blocks, _spec((None, TM, n), lambda i, k: (k, i, 0)), w_blocks,
                   _spec((None, D_MODEL, n), lambda i, k: (k, 0, 0)), NT, (N_TM, nb), nb,
                   jax.ShapeDtypeStruct((SEQ, D_MODEL), F32), _spec((TM, D_MODEL), lambda i, k: (i, 0)), acc_shape=(TM, D_MODEL))


def _back_sum_cols_nt(name, d, w_blocks=None, w=None, n_tile=None):
    nb = d.shape[1] // n_tile
    if w_blocks is not None:
        b, b_spec = w_blocks, _spec((None, D_MODEL, n_tile), lambda i, k: (k, 0, 0))
    else:
        b, b_spec = w, _spec((D_MODEL, n_tile), lambda i, k: (0, k))
    return _matmul(name, d, _spec((TM, n_tile), lambda i, k: (i, k)), b, b_spec, NT, (N_TM, nb), nb,
                   jax.ShapeDtypeStruct((SEQ, D_MODEL), F32), _spec((TM, D_MODEL), lambda i, k: (i, 0)), acc_shape=(TM, D_MODEL))


def _wgrad_tn(name, a, a_cols, d, d_cols, out_blocks):
    nb = d.shape[1] // d_cols
    if out_blocks:
        out_shape = jax.ShapeDtypeStruct((nb, a_cols, d_cols), BF16)
        out_spec = _spec((None, a_cols, d_cols), lambda j, k: (j, 0, 0))
    else:
        out_shape = jax.ShapeDtypeStruct((a_cols, nb * d_cols), BF16)
        out_spec = _spec((a_cols, d_cols), lambda j, k: (0, j))
    return _matmul(name, a, _spec((TKS, a_cols), lambda j, k: (k, 0)), d, _spec((TKS, d_cols), lambda j, k: (k, j)), TN,
                   (nb, N_TKS), N_TKS, out_shape, out_spec, acc_shape=(a_cols, d_cols))


def _wgrad_a_blocks_tn(name, a_blocks, d):
    nb = a_blocks.shape[0]
    return _matmul(name, a_blocks, _spec((None, TKS, FF_BLOCK), lambda j, k: (j, k, 0)), d,
                   _spec((TKS, D_MODEL), lambda j, k: (k, 0)), TN, (nb, N_TKS), N_TKS,
                   jax.ShapeDtypeStruct((nb, FF_BLOCK, D_MODEL), BF16), _spec((None, FF_BLOCK, D_MODEL), lambda j, k: (j, 0, 0)),
                   acc_shape=(FF_BLOCK, D_MODEL))


def _wgrad_d_blocks_tn(name, a, d_blocks):
    nb = d_blocks.shape[0]
    return _matmul(name, a, _spec((TKS, D_MODEL), lambda j, k: (k, 0)), d_blocks,
                   _spec((None, TKS, FF_BLOCK), lambda j, k: (j, k, 0)), TN, (nb, N_TKS), N_TKS,
                   jax.ShapeDtypeStruct((nb, D_MODEL, FF_BLOCK), BF16), _spec((None, D_MODEL, FF_BLOCK), lambda j, k: (j, 0, 0)),
                   acc_shape=(D_MODEL, FF_BLOCK))


NORM_ROWS = 512


def _rstd(x):
    return lax.rsqrt(jnp.mean(x * x, axis=-1, keepdims=True) + RMS_EPS)


def _norm_fwd(name, x, gamma):
    def body(x_ref, g_ref, h_ref):
        x = x_ref[...]
        h_ref[...] = (x * _rstd(x) * g_ref[...]).astype(BF16)

    row = _spec((NORM_ROWS, D_MODEL), lambda i: (i, 0))
    return pl.pallas_call(
        body, name=name, grid=(SEQ // NORM_ROWS,), in_specs=[row, _spec((1, D_MODEL), lambda i: (0, 0))], out_specs=row,
        out_shape=jax.ShapeDtypeStruct((SEQ, D_MODEL), BF16), compiler_params=_params(("parallel",)),
    )(x, gamma)


def _norm_bwd_rows(x, gamma, dh):
    r = _rstd(x)
    xh = x * r
    dxh = dh * gamma
    dx = r * (dxh - xh * jnp.mean(dxh * xh, axis=-1, keepdims=True))
    return dx, jnp.sum(dh * xh, axis=0, keepdims=True)


def _norm_bwd(name, x, gamma, dh, dx_in):
    def body(x_ref, g_ref, dh_ref, dxi_ref, dx_ref, dg_ref):
        dx, dg = _norm_bwd_rows(x_ref[...], g_ref[...], dh_ref[...].astype(F32))
        dx_ref[...] = dxi_ref[...] + dx

        @pl.when(pl.program_id(0) == 0)
        def _():
            dg_ref[...] = dg

        @pl.when(pl.program_id(0) > 0)
        def _():
            dg_ref[...] += dg

    row = _spec((NORM_ROWS, D_MODEL), lambda i: (i, 0))
    vec = _spec((1, D_MODEL), lambda i: (0, 0))
    return pl.pallas_call(
        body, name=name, grid=(SEQ // NORM_ROWS,), in_specs=[row, vec, row, row], out_specs=[row, vec],
        out_shape=[jax.ShapeDtypeStruct((SEQ, D_MODEL), F32), jax.ShapeDtypeStruct((1, D_MODEL), F32)],
        compiler_params=_params(("arbitrary",)),
    )(x, gamma, dh, dx_in)


def _loss_head(x, gamma, target):
    def body(x_ref, g_ref, t_ref, loss_ref, dx_ref, dg_ref):
        x = x_ref[...]
        gamma = g_ref[...]
        err = x * _rstd(x) * gamma - t_ref[...]
        dy = err * (1.0 / D_MODEL)
        dx, dg = _norm_bwd_rows(x, gamma, dy)
        dx_ref[...] = dx
        part = 0.5 * jnp.sum(jnp.sum(err * err, axis=-1, keepdims=True) * (1.0 / D_MODEL), axis=0, keepdims=True)
        part = jnp.broadcast_to(part, loss_ref.shape)

        @pl.when(pl.program_id(0) == 0)
        def _():
            dg_ref[...] = dg
            loss_ref[...] = part

        @pl.when(pl.program_id(0) > 0)
        def _():
            dg_ref[...] += dg
            loss_ref[...] += part

    row = _spec((NORM_ROWS, D_MODEL), lambda i: (i, 0))
    vec = _spec((1, D_MODEL), lambda i: (0, 0))
    return pl.pallas_call(
        body, name="loss_head", grid=(SEQ // NORM_ROWS,), in_specs=[row, vec, row],
        out_specs=[_spec((1, 128), lambda i: (0, 0)), row, vec],
        out_shape=[jax.ShapeDtypeStruct((1, 128), F32), jax.ShapeDtypeStruct((SEQ, D_MODEL), F32),
                   jax.ShapeDtypeStruct((1, D_MODEL), F32)],
        compiler_params=_params(("arbitrary",)),
    )(x, gamma, target)


def _sigmoid(x):
    return 1.0 / (1.0 + jnp.exp(-x))


def _rows(ref, c):
    return ref[pl.ds(pl.multiple_of(c * ROW_CHUNK, ROW_CHUNK), ROW_CHUNK), :].astype(F32)


def _rows_before(ref, c):
    start = pl.multiple_of(jnp.maximum(c * ROW_CHUNK - HALO, 0), HALO)
    rows = ref[pl.ds(start, HALO), :].astype(F32)
    return jnp.where(c > 0, rows, 0.0)


def _rows_after(ref, c, n_chunks):
    start = pl.multiple_of(jnp.minimum((c + 1) * ROW_CHUNK, SEQ - HALO), HALO)
    rows = ref[pl.ds(start, HALO), :].astype(F32)
    return jnp.where(c < n_chunks - 1, rows, 0.0)


def _shift_down(z, before, n):
    row = lax.broadcasted_iota(jnp.int32, z.shape, 0)
    out = pltpu.roll(z, n, 0)
    for r in range(n):
        out = jnp.where(row == r, before[HALO - n + r:HALO - n + r + 1, :], out)
    return out


def _shift_up(z, after, n):
    rows = z.shape[0]
    row = lax.broadcasted_iota(jnp.int32, z.shape, 0)
    out = pltpu.roll(z, rows - n, 0)
    for r in range(n):
        out = jnp.where(row == rows - n + r, after[r:r + 1, :], out)
    return out


def _conv_rows(z, before, w):
    z1 = _shift_down(z, before, 1)
    z2 = _shift_down(z, before, 2)
    return w[2:3, :] * z + w[1:2, :] * z1 + w[0:1, :] * z2, z1, z2


def _conv_t_rows(dy, after, w):
    return w[2:3, :] * dy + w[1:2, :] * _shift_up(dy, after, 1) + w[0:1, :] * _shift_up(dy, after, 2)


N_ROW_CHUNKS = SEQ // ROW_CHUNK


def _ffn_mid_fwd(name, gu, conv_w):
    def body(gu_ref, w_ref, a_ref):
        w = w_ref[...]

        def chunk(c, carry):
            g = _rows(gu_ref.at[0], c)
            u = _rows(gu_ref.at[1], c)
            gc, _, _ = _conv_rows(g, _rows_before(gu_ref.at[0], c), w)
            a_ref[pl.ds(pl.multiple_of(c * ROW_CHUNK, ROW_CHUNK), ROW_CHUNK), :] = (gc * _sigmoid(gc) * u).astype(BF16)
            return carry

        lax.fori_loop(0, N_ROW_CHUNKS, chunk, 0)

    return pl.pallas_call(
        body, name=name, grid=(FF_BLOCKS,),
        in_specs=[_spec((2, None, SEQ, FF_BLOCK), lambda j: (0, j, 0, 0)), _spec((None, 3, FF_BLOCK), lambda j: (j, 0, 0))],
        out_specs=_spec((None, SEQ, FF_BLOCK), lambda j: (j, 0, 0)),
        out_shape=jax.ShapeDtypeStruct((FF_BLOCKS, SEQ, FF_BLOCK), BF16), compiler_params=_params(("parallel",)),
    )(gu, conv_w)


def _ffn_mid_bwd(name, gu, conv_w, da):
    def body(gu_ref, w_ref, da_ref, dgu_ref, dw_ref, dgc_ref):
        w = w_ref[...]

        def first(c, acc):
            g = _rows(gu_ref.at[0], c)
            u = _rows(gu_ref.at[1], c)
            d = _rows(da_ref, c)
            gc, g1, g2 = _conv_rows(g, _rows_before(gu_ref.at[0], c), w)
            sg = _sigmoid(gc)
            rows = pl.ds(pl.multiple_of(c * ROW_CHUNK, ROW_CHUNK), ROW_CHUNK)
            dgu_ref[1, rows, :] = (d * gc * sg).astype(BF16)
            dgc = d * u * (sg * (1.0 + gc * (1.0 - sg)))
            dgc_ref[rows, :] = dgc
            return (acc[0] + jnp.sum(dgc * g2, axis=0, keepdims=True), acc[1] + jnp.sum(dgc * g1, axis=0, keepdims=True),
                    acc[2] + jnp.sum(dgc * g, axis=0, keepdims=True))

        zero = jnp.zeros((1, FF_BLOCK), F32)
        acc = lax.fori_loop(0, N_ROW_CHUNKS, first, (zero, zero, zero))
        for r in range(3):
            dw_ref[r:r + 1, :] = acc[r]

        def second(c, carry):
            dgc = _rows(dgc_ref, c)
            dg = _conv_t_rows(dgc, _rows_after(dgc_ref, c, N_ROW_CHUNKS), w)
            dgu_ref[0, pl.ds(pl.multiple_of(c * ROW_CHUNK, ROW_CHUNK), ROW_CHUNK), :] = dg.astype(BF16)
            return carry

        lax.fori_loop(0, N_ROW_CHUNKS, second, 0)

    pair = _spec((2, None, SEQ, FF_BLOCK), lambda j: (0, j, 0, 0))
    wspec = _spec((None, 3, FF_BLOCK), lambda j: (j, 0, 0))
    return pl.pallas_call(
        body, name=name, grid=(FF_BLOCKS,),
        in_specs=[pair, wspec, _spec((None, SEQ, FF_BLOCK), lambda j: (j, 0, 0))], out_specs=[pair, wspec],
        out_shape=[jax.ShapeDtypeStruct((2, FF_BLOCKS, SEQ, FF_BLOCK), BF16), jax.ShapeDtypeStruct((FF_BLOCKS, 3, FF_BLOCK), F32)],
        scratch_shapes=[pltpu.VMEM((SEQ, FF_BLOCK), F32)], compiler_params=_params(("parallel",)),
    )(gu, conv_w, da)


SC_COLS = 256
N_SC = D_MODEL // SC_COLS


def _sc_specs():
    return [_spec((SEQ, SC_COLS), lambda j, part=part: (0, part * N_SC + j)) for part in range(3)]


def _sc_mid_fwd(p, conv_w):
    def body(b_ref, c_ref, h_ref, w_ref, y_ref):
        w = w_ref[...]

        def chunk(c, carry):
            z = _rows(c_ref, c) * _rows(h_ref, c)
            before = _rows_before(c_ref, c) * _rows_before(h_ref, c)
            zc, _, _ = _conv_rows(z, before, w)
            y_ref[pl.ds(pl.multiple_of(c * ROW_CHUNK, ROW_CHUNK), ROW_CHUNK), :] = (_rows(b_ref, c) * zc).astype(BF16)
            return carry

        lax.fori_loop(0, N_ROW_CHUNKS, chunk, 0)

    col = _spec((SEQ, SC_COLS), lambda j: (0, j))
    return pl.pallas_call(
        body, name="sc_mid_fwd", grid=(N_SC,), in_specs=_sc_specs() + [_spec((3, SC_COLS), lambda j: (0, j))], out_specs=col,
        out_shape=jax.ShapeDtypeStruct((SEQ, D_MODEL), BF16), compiler_params=_params(("parallel",)),
    )(p, p, p, conv_w)


def _sc_mid_bwd(p, conv_w, dy):
    def body(b_ref, c_ref, h_ref, w_ref, dy_ref, db_ref, dc_ref, dh_ref, dw_ref, dzc_ref):
        w = w_ref[...]

        def first(c, acc):
            z = _rows(c_ref, c) * _rows(h_ref, c)
            before = _rows_before(c_ref, c) * _rows_before(h_ref, c)
            zc, z1, z2 = _conv_rows(z, before, w)
            d = _rows(dy_ref, c)
            rows = pl.ds(pl.multiple_of(c * ROW_CHUNK, ROW_CHUNK), ROW_CHUNK)
            db_ref[rows, :] = (d * zc).astype(BF16)
            dzc = d * _rows(b_ref, c)
            dzc_ref[rows, :] = dzc
            return (acc[0] + jnp.sum(dzc * z2, axis=0, keepdims=True), acc[1] + jnp.sum(dzc * z1, axis=0, keepdims=True),
                    acc[2] + jnp.sum(dzc * z, axis=0, keepdims=True))

        zero = jnp.zeros((1, SC_COLS), F32)
        acc = lax.fori_loop(0, N_ROW_CHUNKS, first, (zero, zero, zero))
        for r in range(3):
            dw_ref[r:r + 1, :] = acc[r]

        def second(c, carry):
            dz = _conv_t_rows(_rows(dzc_ref, c), _rows_after(dzc_ref, c, N_ROW_CHUNKS), w)
            rows = pl.ds(pl.multiple_of(c * ROW_CHUNK, ROW_CHUNK), ROW_CHUNK)
            dc_ref[rows, :] = (dz * _rows(h_ref, c)).astype(BF16)
            dh_ref[rows, :] = (dz * _rows(c_ref, c)).astype(BF16)
            return carry

        lax.fori_loop(0, N_ROW_CHUNKS, second, 0)

    col = _spec((SEQ, SC_COLS), lambda j: (0, j))
    wspec = _spec((3, SC_COLS), lambda j: (0, j))
    act = jax.ShapeDtypeStruct((SEQ, D_MODEL), BF16)
    return pl.pallas_call(
        body, name="sc_mid_bwd", grid=(N_SC,), in_specs=_sc_specs() + [wspec, col], out_specs=[col, col, col, wspec],
        out_shape=[act, act, act, jax.ShapeDtypeStruct((3, D_MODEL), F32)],
        scratch_shapes=[pltpu.VMEM((SEQ, SC_COLS), F32)], compiler_params=_params(("parallel",)),
    )(p, p, p, conv_w, dy)


GLA_GROUP = 4
GLA_ROWS = GLA_GROUP * CHUNK
N_GROUPS = N_CHUNKS // GLA_GROUP
Q0, K0, V0, R0, G0 = 0, KEY_DIM, 2 * KEY_DIM, 2 * KEY_DIM + VALUE_DIM, 2 * KEY_DIM + 2 * VALUE_DIM


def _tri(strict):
    r = lax.broadcasted_iota(jnp.int32, (CHUNK, CHUNK), 0)
    c = lax.broadcasted_iota(jnp.int32, (CHUNK, CHUNK), 1)
    return jnp.where(c < r if strict else c <= r, 1.0, 0.0).astype(F32)


def _cumsum_rows(tri, x):
    return jnp.dot(tri, x, preferred_element_type=F32, precision=lax.Precision.HIGHEST)


def _gate_logits(gl, wgu, b_gate):
    return jnp.dot(gl, wgu, preferred_element_type=F32) + b_gate


def _log_decay(logits):
    return (jnp.minimum(logits, 0.0) - jnp.log(1.0 + jnp.exp(-jnp.abs(logits)))) * (1.0 / GATE_NORMALIZER)


def _head(x, h, width):
    return x[:, h * width:(h + 1) * width]


def _gla_fwd(proj, wgu, b_gate, gn):
    def body(p_ref, wgu_ref, b_ref, gn_ref, o_ref, og_ref, st_ref, state):
        @pl.when(pl.program_id(0) == 0)
        def _():
            state[...] = jnp.zeros_like(state)

        tri = _tri(False)
        la = _log_decay(_gate_logits(p_ref[:, G0:G0 + GATE_PAD], wgu_ref[...], b_ref[...]))
        for c in range(GLA_GROUP):
            rows = slice(c * CHUNK, (c + 1) * CHUNK)
            cum = _cumsum_rows(tri, la[rows])
            tot = cum[CHUNK - 1:CHUNK, :]
            kd = (p_ref[rows, K0:K0 + KEY_DIM].astype(F32) * jnp.exp(tot - cum)).astype(BF16)
            decay = jnp.exp(tot)
            q = (p_ref[rows, Q0:Q0 + KEY_DIM].astype(F32) * (HEAD_K ** -0.5)).astype(BF16)
            v = p_ref[rows, V0:V0 + VALUE_DIM]
            for h in range(GLA_HEADS):
                upd = lax.dot_general(_head(v, h, HEAD_V), _head(kd, h, HEAD_K), (TN, ((), ())), preferred_element_type=F32)
                s = state[h] * _head(decay, h, HEAD_K) + upd
                state[h] = s
                st_ref[c, h] = s
                o_ref[rows, h * HEAD_V:(h + 1) * HEAD_V] = lax.dot_general(
                    _head(q, h, HEAD_K), s.astype(BF16), (NT, ((), ())), preferred_element_type=F32)
        r = p_ref[:, R0:R0 + VALUE_DIM].astype(F32)
        gate = r * _sigmoid(r) * gn_ref[...]
        for h in range(GLA_HEADS):
            cols = slice(h * HEAD_V, (h + 1) * HEAD_V)
            o = o_ref[:, cols]
            og_ref[:, cols] = (o * _rstd(o) * gate[:, cols]).astype(BF16)

    rows = _spec((GLA_ROWS, VALUE_DIM), lambda i: (i, 0))
    const = lambda shape: _spec(shape, lambda i: (0,) * len(shape))
    return pl.pallas_call(
        body, name="gla_fwd", grid=(N_GROUPS,),
        in_specs=[_spec((GLA_ROWS, PROJ_A_PAD), lambda i: (i, 0)), const((GATE_PAD, KEY_DIM)), const((1, KEY_DIM)),
                  const((1, VALUE_DIM))],
        out_specs=[rows, rows, _spec((GLA_GROUP, GLA_HEADS, HEAD_V, HEAD_K), lambda i: (i, 0, 0, 0))],
        out_shape=[jax.ShapeDtypeStruct((SEQ, VALUE_DIM), F32), jax.ShapeDtypeStruct((SEQ, VALUE_DIM), BF16),
                   jax.ShapeDtypeStruct((N_CHUNKS, GLA_HEADS, HEAD_V, HEAD_K), F32)],
        scratch_shapes=[pltpu.VMEM((GLA_HEADS, HEAD_V, HEAD_K), F32)], compiler_params=_params(("arbitrary",)),
    )(proj, wgu, b_gate, gn)


def _gla_bwd(proj, wgu, b_gate, gn, o, states, dog):
    last = N_GROUPS - 1

    def body(p_ref, wgu_ref, b_ref, gn_ref, o_ref, st_ref, stp_ref, dog_ref, dp_ref, dwgu_ref, db_ref, dgn_ref, carry, do_buf):
        step = pl.program_id(0)

        @pl.when(step == 0)
        def _():
            carry[...] = jnp.zeros_like(carry)

        r = p_ref[:, R0:R0 + VALUE_DIM].astype(F32)
        sr = _sigmoid(r)
        silu = r * sr
        gn_row = gn_ref[...]
        dog_rows = dog_ref[...].astype(F32)
        dn = dog_rows * silu
        dgn_cols = []
        for h in range(GLA_HEADS):
            cols = slice(h * HEAD_V, (h + 1) * HEAD_V)
            oh = o_ref[:, cols]
            rs = _rstd(oh)
            ohat = oh * rs
            dn_h = dn[:, cols]
            dgn_cols.append(jnp.sum(dn_h * ohat, axis=0, keepdims=True))
            dohat = dn_h * gn_row[:, cols]
            do_buf[:, cols] = rs * (dohat - ohat * jnp.mean(dohat * ohat, axis=-1, keepdims=True))
            n_h = ohat * gn_row[:, cols]
            dp_ref[:, R0 + h * HEAD_V:R0 + (h + 1) * HEAD_V] = (
                dog_rows[:, cols] * n_h * (sr[:, cols] * (1.0 + r[:, cols] * (1.0 - sr[:, cols])))).astype(BF16)
        dgn = jnp.concatenate(dgn_cols, axis=1)

        tri = _tri(False)
        tri_strict = _tri(True)
        gl = p_ref[:, G0:G0 + GATE_PAD]
        logits = _gate_logits(gl, wgu_ref[...], b_ref[...])
        la = _log_decay(logits)
        dlogit_rows = []
        for c in reversed(range(GLA_GROUP)):
            rows = slice(c * CHUNK, (c + 1) * CHUNK)
            cum = _cumsum_rows(tri, la[rows])
            tot = cum[CHUNK - 1:CHUNK, :]
            fade = jnp.exp(tot - cum)
            k = p_ref[rows, K0:K0 + KEY_DIM].astype(F32)
            kd32 = k * fade
            kd = kd32.astype(BF16)
            decay = jnp.exp(tot)
            q = (p_ref[rows, Q0:Q0 + KEY_DIM].astype(F32) * (HEAD_K ** -0.5)).astype(BF16)
            v = p_ref[rows, V0:V0 + VALUE_DIM]
            do = do_buf[rows, :].astype(BF16)
            dkd_cols, ddecay_cols = [], []
            for h in range(GLA_HEADS):
                do_h = _head(do, h, HEAD_V)
                s = st_ref[c, h]
                dq = jnp.dot(do_h, s.astype(BF16), preferred_element_type=F32) * (HEAD_K ** -0.5)
                dp_ref[rows, Q0 + h * HEAD_K:Q0 + (h + 1) * HEAD_K] = dq.astype(BF16)
                g = carry[h] + lax.dot_general(do_h, _head(q, h, HEAD_K), (TN, ((), ())), preferred_element_type=F32)
                g16 = g.astype(BF16)
                dkd_cols.append(jnp.dot(_head(v, h, HEAD_V), g16, preferred_element_type=F32))
                dv = lax.dot_general(_head(kd, h, HEAD_K), g16, (NT, ((), ())), preferred_element_type=F32)
                dp_ref[rows, V0 + h * HEAD_V:V0 + (h + 1) * HEAD_V] = dv.astype(BF16)
                if c > 0:
                    s_prev = st_ref[c - 1, h]
                else:
                    s_prev = jnp.where(step < last, stp_ref[0, h], 0.0)
                ddecay_cols.append(jnp.sum(g * s_prev, axis=0, keepdims=True))
                carry[h] = g * _head(decay, h, HEAD_K)
            dkd = jnp.concatenate(dkd_cols, axis=1)
            ddecay = jnp.concatenate(ddecay_cols, axis=1)
            dp_ref[rows, K0:K0 + KEY_DIM] = (dkd * fade).astype(BF16)
            e = dkd * kd32
            dla = ddecay * decay + _cumsum_rows(tri_strict, e)
            dlogit_rows.append(dla * (1.0 / GATE_NORMALIZER) * (1.0 - _sigmoid(logits[rows])))
        dlogit = jnp.concatenate(dlogit_rows[::-1], axis=0)
        dlogit16 = dlogit.astype(BF16)
        dp_ref[:, G0:G0 + GATE_PAD] = lax.dot_general(
            dlogit16, wgu_ref[...], (NT, ((), ())), preferred_element_type=F32).astype(BF16)
        dwgu = lax.dot_general(gl, dlogit16, (TN, ((), ())), preferred_element_type=F32)
        db = jnp.sum(dlogit, axis=0, keepdims=True)

        @pl.when(step == 0)
        def _():
            dwgu_ref[...] = dwgu
            db_ref[...] = db
            dgn_ref[...] = dgn

        @pl.when(step > 0)
        def _():
            dwgu_ref[...] += dwgu
            db_ref[...] += db
            dgn_ref[...] += dgn

    rev = lambda i: (last - i, 0)
    rows = _spec((GLA_ROWS, VALUE_DIM), rev)
    const = lambda shape: _spec(shape, lambda i: (0,) * len(shape))
    st_shape = (GLA_HEADS, HEAD_V, HEAD_K)
    return pl.pallas_call(
        body, name="gla_bwd", grid=(N_GROUPS,),
        in_specs=[_spec((GLA_ROWS, PROJ_A_PAD), rev), const((GATE_PAD, KEY_DIM)), const((1, KEY_DIM)), const((1, VALUE_DIM)),
                  rows, _spec((GLA_GROUP,) + st_shape, lambda i: (last - i, 0, 0, 0)),
                  _spec((1,) + st_shape, lambda i: (jnp.maximum((last - i) * GLA_GROUP - 1, 0), 0, 0, 0)), rows],
        out_specs=[_spec((GLA_ROWS, PROJ_A_PAD), rev), const((GATE_PAD, KEY_DIM)), const((1, KEY_DIM)), const((1, VALUE_DIM))],
        out_shape=[jax.ShapeDtypeStruct((SEQ, PROJ_A_PAD), BF16), jax.ShapeDtypeStruct((GATE_PAD, KEY_DIM), F32),
                   jax.ShapeDtypeStruct((1, KEY_DIM), F32), jax.ShapeDtypeStruct((1, VALUE_DIM), F32)],
        scratch_shapes=[pltpu.VMEM(st_shape, F32), pltpu.VMEM((GLA_ROWS, VALUE_DIM), F32)],
        compiler_params=_params(("arbitrary",)),
    )(proj, wgu, b_gate, gn, o, states, states, dog)


def _ffn_fwd(tag, x, gamma, w_up, conv_w, w_down):
    h = _norm_fwd(f"ffn{tag}_norm", x, gamma)
    gu = _proj_blocks_nn(f"ffn{tag}_up", h, w_up, FF_BLOCK, False).reshape(2, FF_BLOCKS, SEQ, FF_BLOCK)
    a = _ffn_mid_fwd(f"ffn{tag}_mid", gu, conv_w)
    return _down_nn(f"ffn{tag}_down", a, w_down, x), (h, gu, a)


def _ffn_bwd(tag, x, gamma, w_up, conv_w, w_down, saved, dx):
    h, gu, a = saved
    da = _back_blocks_nt(f"ffn{tag}_da", dx, w_down)
    d_w_down = _wgrad_a_blocks_tn(f"ffn{tag}_dwdown", a, dx)
    dgu, d_conv = _ffn_mid_bwd(f"ffn{tag}_mid_bwd", gu, conv_w, da)
    dgu = dgu.reshape(2 * FF_BLOCKS, SEQ, FF_BLOCK)
    dh = _back_sum_blocks_nt(f"ffn{tag}_dh", dgu, w_up)
    d_w_up = _wgrad_d_blocks_tn(f"ffn{tag}_dwup", h, dgu)
    dx, d_gamma = _norm_bwd(f"ffn{tag}_norm_bwd", x, gamma, dh, dx)
    return dx, d_gamma, d_w_up, d_conv, d_w_down


def _local_step(x, target, w, fetch=None, emit=None):
    if fetch is None:
        local = dict(a=(w.get("a_w_in"), w.get("a_w_out")), b=(w.get("b_w_in"), w.get("b_w_out")))
        for layer in range(2):
            local[f"f{layer}"] = (w["f_w_up"][layer], w["f_w_down"][layer]) if "f_w_up" in w else None
        fetch = lambda group, after: local[group]
    if emit is None:
        emit = lambda group, grads, dx: dx
    f_norm = (w["f_norm"][0:1], w["f_norm"][1:2])

    x0 = x
    a_w_in, a_w_out = fetch("a", x0)
    h0 = _norm_fwd("a_norm", x0, w["a_norm"])
    proj = _proj_nn("a_in", h0, a_w_in, PA_TILE, N_PA)
    o, og, states = _gla_fwd(proj, w["a_w_gate_up"], w["a_b_gate"], w["a_gn"])
    x1 = _out_nn("a_out", og, a_w_out, x0)
    up0, down0 = fetch("f0", x1)
    x2, ffn0 = _ffn_fwd(0, x1, f_norm[0], up0, w["f_conv"][0], down0)
    b_w_in, b_w_out = fetch("b", x2)
    h2 = _norm_fwd("b_norm", x2, w["b_norm"])
    p = _proj_blocks_nn("b_in", h2, b_w_in, B_SHARD, True)
    y = _sc_mid_fwd(p, w["b_conv"])
    x3 = _out_nn("b_out", y, b_w_out, x2)
    up1, down1 = fetch("f1", x3)
    x4, ffn1 = _ffn_fwd(1, x3, f_norm[1], up1, w["f_conv"][1], down1)
    loss, dx, d_final_norm = _loss_head(x4, w["final_norm"], target)

    dx, d_f_norm1, d_up1, d_fconv1, d_down1 = _ffn_bwd(1, x3, f_norm[1], up1, w["f_conv"][1], down1, ffn1, dx)
    dx = emit("f1", (d_up1, d_down1), dx)

    dy = _back_nt("b_dy", dx, b_w_out)
    d_b_w_out = _wgrad_tn("b_dwout", y, D_MODEL, dx, D_MODEL, False)
    db, dc, dhh, d_b_conv = _sc_mid_bwd(p, w["b_conv"], dy)
    dp = jnp.concatenate([db, dc, dhh], axis=1)
    dh2 = _back_sum_cols_nt("b_dh", dp, w_blocks=b_w_in, n_tile=B_SHARD)
    d_b_w_in = _wgrad_tn("b_dwin", h2, D_MODEL, dp, B_SHARD, True)
    dx, d_b_norm = _norm_bwd("b_norm_bwd", x2, w["b_norm"], dh2, dx)
    dx = emit("b", (d_b_w_in, d_b_w_out), dx)

    dx, d_f_norm0, d_up0, d_fconv0, d_down0 = _ffn_bwd(0, x1, f_norm[0], up0, w["f_conv"][0], down0, ffn0, dx)
    dx = emit("f0", (d_up0, d_down0), dx)

    dog = _back_nt("a_dog", dx, a_w_out)
    d_a_w_out = _wgrad_tn("a_dwout", og, D_MODEL, dx, D_MODEL, False)
    dproj, d_wgu, d_b_gate, d_gn = _gla_bwd(proj, w["a_w_gate_up"], w["a_b_gate"], w["a_gn"], o, states, dog)
    d_a_w_in = _wgrad_tn("a_dwin", h0, D_MODEL, dproj, PA_TILE, False)
    dproj = emit("a", (d_a_w_in, d_a_w_out), dproj)
    dh0 = _back_sum_cols_nt("a_dh", dproj, w=a_w_in, n_tile=PA_TILE)
    dx, d_a_norm = _norm_bwd("a_norm_bwd", x0, w["a_norm"], dh0, dx)

    grads = dict(
        a_norm=d_a_norm, a_w_in=d_a_w_in, a_w_gate_up=d_wgu, a_b_gate=d_b_gate, a_gn=d_gn, a_w_out=d_a_w_out,
        b_norm=d_b_norm, b_w_in=d_b_w_in, b_conv=d_b_conv, b_w_out=d_b_w_out,
        f_norm=(d_f_norm0, d_f_norm1), f_w_up=(d_up0, d_up1), f_conv=(d_fconv0, d_fconv1), f_w_down=(d_down0, d_down1),
        final_norm=d_final_norm)
    return loss[0, 0], dx, grads


MESH_ID = pl.DeviceIdType.MESH
ANY = pl.BlockSpec(memory_space=pl.ANY)
N_PEERS = N_DEV - 1


def _position():
    return lax.axis_index("x"), lax.axis_index("y"), lax.axis_index("c")


def _slot(px, py, pc):
    return 4 * px + 2 * py + pc


def _all_gather(shards):
    n = len(shards)

    def body(*refs):
        ins, outs = refs[:n], refs[n:2 * n]
        send_sems, recv_sems, local_sems = refs[2 * n:]
        x, y, c = _position()
        me, sibling = (x, y, c), (x, y, 1 - c)
        chips = [(1 - x, y), (x, 1 - y), (1 - x, 1 - y)]

        def copy(t, k, block, to, from_input=False):
            dst = outs[t].at[_slot(*block)]
            return pltpu.make_async_remote_copy(
                src_ref=ins[t] if from_input else dst, dst_ref=dst, send_sem=send_sems.at[t, k], recv_sem=recv_sems.at[t, k],
                device_id=to, device_id_type=MESH_ID)

        mine = [pltpu.make_async_copy(ins[t], outs[t].at[_slot(*me)], local_sems.at[t]) for t in range(n)]
        for cp in mine:
            cp.start()
        first = []
        for t in range(n):
            first.append(copy(t, 0, me, sibling, True))
            first += [copy(t, 1 + j, me, (*chip, c), True) for j, chip in enumerate(chips)]
        for cp in first:
            cp.start()
        passed = []
        for t in range(n):
            for j, chip in enumerate(chips):
                copy(t, 1 + j, (*chip, c), me).wait_recv()
                fwd = copy(t, 4 + j, (*chip, c), sibling)
                fwd.start()
                passed.append(fwd)
        for t in range(n):
            copy(t, 0, sibling, me).wait_recv()
            for j, chip in enumerate(chips):
                copy(t, 4 + j, (*chip, 1 - c), me).wait_recv()
        for cp in first + passed:
            cp.wait_send()
        for cp in mine:
            cp.wait()

    return pl.pallas_call(
        body, name="weight_gather", in_specs=[ANY] * n, out_specs=[ANY] * n,
        out_shape=[jax.ShapeDtypeStruct((N_DEV,) + s.shape, s.dtype) for s in shards],
        scratch_shapes=[pltpu.SemaphoreType.DMA((n, N_PEERS)), pltpu.SemaphoreType.DMA((n, N_PEERS)), pltpu.SemaphoreType.DMA((n,))],
    )(*shards)


ALL_PEERS = (1, 2, 3, 4, 5, 6, 7)
SIBLING_AND_SAME_CORE = (1, 2, 4, 6)
SAME_CORE = (2, 4, 6)


def _flip(x, y, c, k):
    return x ^ (k >> 2), y ^ ((k >> 1) & 1), c ^ (k & 1)


def _send_copy(parts, landing, shared, send_sems, recv_sems, t, s, k):
    x, y, c = _position()
    peer = _flip(x, y, c, k)
    src = parts[t] if shared[t] else parts[t].at[_slot(*peer)]
    return pltpu.make_async_remote_copy(
        src_ref=src, dst_ref=landing[t].at[_slot(x, y, c)], send_sem=send_sems.at[s], recv_sem=recv_sems.at[s],
        device_id=peer, device_id_type=MESH_ID)


def _send_arrival(landing, send_sems, recv_sems, t, s, k):
    x, y, c = _position()
    peer = _flip(x, y, c, k)
    landed = landing[t].at[_slot(*peer)]
    return pltpu.make_async_remote_copy(
        src_ref=landed, dst_ref=landed, send_sem=send_sems.at[s], recv_sem=recv_sems.at[s],
        device_id=peer, device_id_type=MESH_ID)


def _handshake(peers):
    x, y, c = _position()
    barrier = pltpu.get_barrier_semaphore()
    for k in peers:
        pl.semaphore_signal(barrier, inc=1, device_id=_flip(x, y, c, k), device_id_type=MESH_ID)
    pl.semaphore_wait(barrier, len(peers))


def _sequencer(name, collective_id, n_copies, body, operands, out_type):
    n_arrays = len(operands)
    return pl.kernel(
        body, out_type=out_type, mesh=plsc.ScalarSubcoreMesh(axis_name="sequencer", num_cores=1), name=name,
        scratch_types=(pltpu.SemaphoreType.DMA((n_copies,)), pltpu.SemaphoreType.DMA((n_copies,)),
                       pltpu.SemaphoreType.DMA((n_arrays,))),
        compiler_params=pltpu.CompilerParams(collective_id=collective_id))(*operands)


def _sequencer_exchange(name, collective_id, parts, shared):
    n, n_peers = len(parts), len(ALL_PEERS)

    def body(*refs):
        src, landing = refs[:n], refs[n:2 * n]
        send_sems, recv_sems, local_sems = refs[2 * n:]
        _handshake(ALL_PEERS)
        my_slot = _slot(*_position())
        mine = [pltpu.make_async_copy(src[t] if shared[t] else src[t].at[my_slot], landing[t].at[my_slot], local_sems.at[t])
                for t in range(n)]
        for cp in mine:
            cp.start()
        sent = [_send_copy(src, landing, shared, send_sems, recv_sems, t, t * n_peers + j, k)
                for t in range(n) for j, k in enumerate(ALL_PEERS)]
        for cp in sent:
            cp.start()
        for t in range(n):
            for j, k in enumerate(ALL_PEERS):
                _send_arrival(landing, send_sems, recv_sems, t, t * n_peers + j, k).wait_recv()
        for cp in sent:
            cp.wait_send()
        for cp in mine:
            cp.wait()

    landing = [jax.ShapeDtypeStruct(((N_DEV,) + p.shape) if sh else p.shape, p.dtype) for p, sh in zip(parts, shared)]
    return _sequencer(name, collective_id, n * n_peers, body, parts, landing)


def _sequencer_gather(name, collective_id, shards):
    n, per = len(shards), N_PEERS

    def body(*refs):
        src, out = refs[:n], refs[n:2 * n]
        send_sems, recv_sems, local_sems = refs[2 * n:]
        _handshake(SIBLING_AND_SAME_CORE)
        x, y, c = _position()
        me, sibling = (x, y, c), (x, y, 1 - c)

        def copy(t, j, block, to, from_input=False):
            dst = out[t].at[_slot(*block)]
            return pltpu.make_async_remote_copy(
                src_ref=src[t] if from_input else dst, dst_ref=dst, send_sem=send_sems.at[t * per + j],
                recv_sem=recv_sems.at[t * per + j], device_id=to, device_id_type=MESH_ID)

        mine = [pltpu.make_async_copy(src[t], out[t].at[_slot(*me)], local_sems.at[t]) for t in range(n)]
        for cp in mine:
            cp.start()
        sent = [copy(t, j, me, _flip(x, y, c, k), True) for t in range(n) for j, k in enumerate(SIBLING_AND_SAME_CORE)]
        for cp in sent:
            cp.start()
        for t in range(n):
            for j, k in enumerate(SAME_CORE):
                block = _flip(x, y, c, k)
                copy(t, 1 + j, block, me).wait_recv()
                forward = copy(t, 4 + j, block, sibling)
                forward.start()
                sent.append(forward)
        for t in range(n):
            copy(t, 0, sibling, me).wait_recv()
            for j, k in enumerate(SAME_CORE):
                copy(t, 4 + j, _flip(x, y, 1 - c, k), me).wait_recv()
        for cp in sent:
            cp.wait_send()
        for cp in mine:
            cp.wait()

    gathered = [jax.ShapeDtypeStruct((N_DEV,) + s.shape, s.dtype) for s in shards]
    return _sequencer(name, collective_id, n * per, body, shards, gathered)


ADAM_ROWS = 256


def _adam_update(w, g, m, v):
    m = ADAM_B1 * m + (1.0 - ADAM_B1) * g
    v = ADAM_B2 * v + (1.0 - ADAM_B2) * (g * g)
    m_hat = m / (1.0 - ADAM_B1 ** ADAM_STEP)
    v_hat = v / (1.0 - ADAM_B2 ** ADAM_STEP)
    delta = -ADAM_LR * (m_hat / (jnp.sqrt(v_hat) + ADAM_EPS) + ADAM_WD * w)
    return delta, m, v


def _sum_slots(ref):
    total = ref[0].astype(F32)
    for d in range(1, N_DEV):
        total = total + ref[d].astype(F32)
    return total


def _adamw_sum(name, landed, w, m, v):
    layers, rows, cols = w.shape
    tr = ADAM_ROWS if rows % ADAM_ROWS == 0 else rows
    nt = rows // tr

    def body(*refs):
        parts = refs[:layers]
        w_ref, m_ref, v_ref, g_ref, d_ref, nm_ref, nv_ref = refs[layers:]
        layer = pl.program_id(0)
        g = _sum_slots(parts[0])
        for q in range(1, layers):
            g = jnp.where(layer == q, _sum_slots(parts[q]), g)
        delta, new_m, new_v = _adam_update(w_ref[...], g, m_ref[...], v_ref[...])
        g_ref[...] = g
        d_ref[...] = delta
        nm_ref[...] = new_m
        nv_ref[...] = new_v

    def part_spec(q):
        return _spec((N_DEV, tr, cols), lambda l, i: (0, jnp.where(l == q, i, jnp.where(l < q, 0, nt - 1)), 0))

    tile = _spec((None, tr, cols), lambda l, i: (l, i, 0))
    out = jax.ShapeDtypeStruct((layers, rows, cols), F32)
    return pl.pallas_call(
        body, name=name, grid=(layers, nt), in_specs=[part_spec(q) for q in range(layers)] + [tile] * 3,
        out_specs=[tile] * 4, out_shape=[out] * 4, compiler_params=_params(("arbitrary", "arbitrary")),
    )(*landed, w, m, v)


def _sum_small(landed):
    def body(in_ref, out_ref):
        out_ref[...] = _sum_slots(in_ref)

    return pl.pallas_call(body, name="small_grad_sum", out_shape=jax.ShapeDtypeStruct(landed.shape[1:], F32))(landed)


def _adamw_small(name, g, w, m, v):
    def body(g_ref, w_ref, m_ref, v_ref, d_ref, nm_ref, nv_ref):
        d_ref[...], nm_ref[...], nv_ref[...] = _adam_update(w_ref[...], g_ref[...], m_ref[...], v_ref[...])

    out = jax.ShapeDtypeStruct(w.shape, F32)
    return pl.pallas_call(body, name=name, out_shape=[out] * 3)(g, w, m, v)


LANES = 128
SUBLANES = 8
F_CONV_SHARD = D_FF // N_DEV
GATE_SHARD = KEY_DIM // N_DEV
NORM_SHARD = D_MODEL // N_DEV


def _tile_rows(a):
    flat = a.reshape(-1)
    size = -(-flat.shape[0] // (SUBLANES * LANES)) * SUBLANES * LANES
    return jnp.pad(flat, (0, size - flat.shape[0])).reshape(-1, LANES)


def _pack_rows(pieces):
    return jnp.concatenate([_tile_rows(p) for p in pieces], axis=0)


def _unpack_rows(packed, shapes):
    out, row = [], 0
    for shape in shapes:
        size = 1
        for s in shape:
            size *= s
        rows = -(-size // (SUBLANES * LANES)) * SUBLANES
        piece = packed[..., row:row + rows, :]
        out.append(piece.reshape(piece.shape[:-2] + (rows * LANES,))[..., :size])
        row += rows
    return out


SMALL_SHARDS = ((GATE_RANK, GATE_SHARD), (1, NORM_SHARD), (3, NORM_SHARD), (2, 3, F_CONV_SHARD))


def _unpack_small_shards(g):
    gate, b_norm, b_conv, f_conv = _unpack_rows(g, SMALL_SHARDS)
    gate = gate.reshape(N_DEV, GATE_RANK, GATE_SHARD).transpose(1, 0, 2).reshape(GATE_RANK, KEY_DIM)
    b_norm = b_norm.reshape(1, D_MODEL)
    b_conv = b_conv.reshape(N_DEV, 3, NORM_SHARD).transpose(1, 0, 2).reshape(3, D_MODEL)
    f_conv = f_conv.reshape(N_DEV, 2, 3, F_CONV_SHARD).transpose(1, 2, 0, 3).reshape(2, 3, D_FF)
    return gate, b_norm, b_conv, f_conv


def _conv_blocks(f_conv):
    return f_conv.reshape(2, 3, FF_BLOCKS, FF_BLOCK).transpose(0, 2, 1, 3)


def _conv_unblocks(f_conv):
    return f_conv.transpose(1, 0, 2).reshape(3, D_FF)


SMALL_LAYOUT = (("a_norm", (1, D_MODEL)), ("a_w_gate_up", (GATE_RANK, KEY_DIM)), ("a_b_gate", (1, KEY_DIM)), ("a_gn", (1, VALUE_DIM)),
                ("b_norm", (1, D_MODEL)), ("b_conv", (3, D_MODEL)), ("f_norm0", (1, D_MODEL)), ("f_norm1", (1, D_MODEL)),
                ("f_conv0", (3, D_FF)), ("f_conv1", (3, D_FF)), ("final_norm", (1, D_MODEL)))


def _pack_small_grads(g):
    full = dict(g)
    full["a_w_gate_up"] = g["a_w_gate_up"][:GATE_RANK]
    for layer in range(2):
        full[f"f_norm{layer}"] = g["f_norm"][layer]
        full[f"f_conv{layer}"] = _conv_unblocks(g["f_conv"][layer])
    return _pack_rows([full[name] for name, _ in SMALL_LAYOUT])


def _unpack_small_grads(packed):
    pieces = _unpack_rows(packed, [shape for _, shape in SMALL_LAYOUT])
    out = {name: piece.reshape(shape) for (name, shape), piece in zip(SMALL_LAYOUT, pieces)}
    out["f_norm"] = jnp.stack([out["f_norm0"][0], out["f_norm1"][0]])
    out["f_conv"] = jnp.stack([out["f_conv0"], out["f_conv1"]])
    return out


def kernel(x, a_norm, a_w_in, a_w_gate_up, a_b_gate, a_gn, a_w_out, b_norm, b_w_in, b_conv, b_w_out, f_norm, f_w_up, f_conv, f_w_down, final_norm, loss_target, m_a_norm, m_a_w_in, m_a_w_gate_up, m_a_b_gate, m_a_gn, m_a_w_out, m_b_norm, m_b_w_in, m_b_conv, m_b_w_out, m_f_norm, m_f_w_up, m_f_conv, m_f_w_down, m_final_norm, v_a_norm, v_a_w_in, v_a_w_gate_up, v_a_b_gate, v_a_gn, v_a_w_out, v_b_norm, v_b_w_in, v_b_conv, v_b_w_out, v_f_norm, v_f_w_up, v_f_conv, v_f_w_down, v_final_norm):
    my_slot = _slot(*_position())

    first = _all_gather([a_w_in[0].astype(BF16), a_w_out[0].astype(BF16), _pack_rows([a_w_gate_up[0], b_norm, b_conv[0], f_conv])])
    gathers, small_shards = {}, first[2]
    later = (("f0", f_w_up[0], f_w_down[0]), ("b", b_w_in[0], b_w_out[0]), ("f1", f_w_up[1], f_w_down[1]))
    for collective_id, (group, w_in, w_out) in enumerate(later):
        w_in, w_out, small_shards = lax.optimization_barrier((w_in.astype(BF16), w_out.astype(BF16), small_shards))
        gathers[group] = _sequencer_gather(f"gather_{group}", collective_id, [w_in, w_out])
    gate_full, b_norm_full, b_conv_full, f_conv_full = _unpack_small_shards(small_shards)
    a_w_in_full = jnp.concatenate([first[0][d] for d in range(N_DEV)] + [jnp.zeros((D_MODEL, PROJ_A_PAD - PROJ_A), BF16)], axis=1)
    weights = dict(
        a_norm=a_norm, a_w_gate_up=jnp.pad(gate_full, ((0, GATE_PAD - GATE_RANK), (0, 0))).astype(BF16), a_b_gate=a_b_gate,
        a_gn=a_gn, b_norm=b_norm_full, b_conv=b_conv_full, f_norm=f_norm, f_conv=_conv_blocks(f_conv_full),
        final_norm=final_norm.reshape(1, D_MODEL))

    def fetch(group, after):
        if group == "a":
            return a_w_in_full, first[1].reshape(D_MODEL, D_MODEL)
        w_in, w_out = gathers[group]
        if group == "b":
            return w_in, w_out.reshape(D_MODEL, D_MODEL)
        return w_in, w_out.reshape(FF_BLOCKS, FF_BLOCK, D_MODEL)

    exchanges = {}

    def owner_blocks(d_out):
        return d_out.reshape((N_DEV, -1, D_MODEL))

    exchange_ids = dict(f1=3, b=4, f0=5, a=6)

    def emit(group, grads, dx):
        d_in, d_out = grads
        if group == "a":
            d_in = jnp.stack([d_in[:, d * A_SHARD:(d + 1) * A_SHARD] for d in range(N_DEV)])
        dx, d_in, d_out = lax.optimization_barrier((dx, d_in, owner_blocks(d_out)))
        exchanges[group] = _sequencer_exchange(f"grads_{group}", exchange_ids[group], [d_in, d_out], [False, False])
        return dx

    loss, dx, g = _local_step(x[0], loss_target[0], weights, fetch, emit)
    loss = lax.psum(loss, MESH_AXES)
    exchanges["small"] = _sequencer_exchange("grads_small", 7, [_pack_small_grads(g)], [True])

    landed_b, landed_f0, landed_f1, landed_a = (exchanges[group] for group in ("b", "f0", "f1", "a"))
    big = dict(
        b_w_in=_adamw_sum("adam_b_w_in", [landed_b[0]], b_w_in, m_b_w_in, v_b_w_in),
        b_w_out=_adamw_sum("adam_b_w_out", [landed_b[1]], b_w_out, m_b_w_out, v_b_w_out),
        f_w_up=_adamw_sum("adam_f_w_up", [landed_f0[0], landed_f1[0]], f_w_up, m_f_w_up, v_f_w_up),
        f_w_down=_adamw_sum("adam_f_w_down", [landed_f0[1], landed_f1[1]], f_w_down, m_f_w_down, v_f_w_down),
        a_w_in=_adamw_sum("adam_a_w_in", [landed_a[0]], a_w_in, m_a_w_in, v_a_w_in),
        a_w_out=_adamw_sum("adam_a_w_out", [landed_a[1]], a_w_out, m_a_w_out, v_a_w_out))
    small_g = _unpack_small_grads(_sum_small(exchanges["small"][0]))
    small_g["a_w_gate_up"] = lax.dynamic_slice_in_dim(small_g["a_w_gate_up"], my_slot * GATE_SHARD, GATE_SHARD, axis=1)
    small_g["b_norm"] = lax.dynamic_slice_in_dim(small_g["b_norm"], my_slot * NORM_SHARD, NORM_SHARD, axis=1)
    small_g["b_conv"] = lax.dynamic_slice_in_dim(small_g["b_conv"], my_slot * NORM_SHARD, NORM_SHARD, axis=1)
    small_g["f_conv"] = lax.dynamic_slice_in_dim(small_g["f_conv"], my_slot * F_CONV_SHARD, F_CONV_SHARD, axis=2)
    small_w = dict(
        a_norm=(a_norm, m_a_norm, v_a_norm), a_w_gate_up=(a_w_gate_up, m_a_w_gate_up, v_a_w_gate_up),
        a_b_gate=(a_b_gate, m_a_b_gate, v_a_b_gate), a_gn=(a_gn, m_a_gn, v_a_gn), b_norm=(b_norm, m_b_norm, v_b_norm),
        b_conv=(b_conv, m_b_conv, v_b_conv), f_norm=(f_norm, m_f_norm, v_f_norm), f_conv=(f_conv, m_f_conv, v_f_conv),
        final_norm=(final_norm, m_final_norm, v_final_norm))
    small = {}
    for name, (w, m, v) in small_w.items():
        flat = (w.shape[-1],) if w.ndim == 1 else w.shape[-2:]
        two_d = (-1, flat[-1])
        grad = small_g[name].reshape(w.shape)
        delta, new_m, new_v = _adamw_small(
            "adam_" + name, grad.reshape(two_d), w.reshape(two_d), m.reshape(two_d), v.reshape(two_d))
        small[name] = (grad, delta.reshape(w.shape), new_m.reshape(w.shape), new_v.reshape(w.shape))

    order = ["a_norm", "a_w_in", "a_w_gate_up", "a_b_gate", "a_gn", "a_w_out", "b_norm", "b_w_in", "b_conv", "b_w_out",
             "f_norm", "f_w_up", "f_conv", "f_w_down", "final_norm"]
    results = {**big, **small}
    outputs = [loss, dx.reshape(1, SEQ, D_MODEL)]
    for kind in range(4):
        outputs += [results[name][kind] for name in order]
    return tuple(outputs)
```

```python
import jax
import jax.numpy as jnp
from jax import lax
from jax.experimental import pallas as pl
from jax.experimental.pallas import tpu as pltpu
from jax.experimental.pallas import tpu_sc as plsc

F32 = jnp.float32
BF16 = jnp.bfloat16

N_DEV = 8
SEQ = 2048
D_MODEL = 1024
CHUNK = 64
N_CHUNKS = SEQ // CHUNK
RMS_EPS = 1e-6
GLA_HEADS = 4
KEY_DIM = 512
VALUE_DIM = 1024
HEAD_K = KEY_DIM // GLA_HEADS
HEAD_V = VALUE_DIM // GLA_HEADS
GATE_RANK = 16
GATE_PAD = 128
GATE_NORMALIZER = 16.0
PROJ_A = 2 * KEY_DIM + 2 * VALUE_DIM + GATE_RANK
PROJ_A_PAD = 2 * KEY_DIM + 2 * VALUE_DIM + GATE_PAD
A_SHARD = PROJ_A // N_DEV
B_SHARD = 3 * D_MODEL // N_DEV
D_FF = 2816
FF_BLOCK = 2 * D_FF // N_DEV
FF_BLOCKS = D_FF // FF_BLOCK
ADAM_LR = 0.001
ADAM_B1 = 0.9
ADAM_B2 = 0.999
ADAM_EPS = 1e-08
ADAM_WD = 0.01
ADAM_STEP = 10
MESH_AXES = ("x", "y", "c")

VMEM_LIMIT = 56 * 1024 * 1024
ROW_CHUNK = 256
HALO = 16


def _params(sem=None, vmem=VMEM_LIMIT):
    return pltpu.CompilerParams(dimension_semantics=sem, vmem_limit_bytes=vmem)


NN = ((1,), (0,))
NT = ((1,), (1,))
TN = ((0,), (0,))


def _matmul(name, a, a_spec, b, b_spec, dims, grid, nk, out_shape, out_spec, acc_shape=None, res=None, res_spec=None):
    has_res = res is not None

    def body(*refs):
        a_ref, b_ref = refs[0], refs[1]
        r_ref = refs[2] if has_res else None
        o_ref = refs[2 + has_res]
        acc_ref = refs[3 + has_res] if nk > 1 else None

        def product():
            return lax.dot_general(a_ref[...].astype(BF16), b_ref[...].astype(BF16), (dims, ((), ())),
                                   preferred_element_type=F32)

        def finish(v):
            if has_res:
                v = v + r_ref[...]
            o_ref[...] = v.astype(o_ref.dtype)

        if nk == 1:
            finish(product())
        else:
            k = pl.program_id(len(grid) - 1)
            p = product()

            @pl.when(k == 0)
            def _():
                acc_ref[...] = p

            @pl.when(k > 0)
            def _():
                acc_ref[...] += p

            @pl.when(k == nk - 1)
            def _():
                finish(acc_ref[...])

    operands = [a, b] + ([res] if has_res else [])
    in_specs = [a_spec, b_spec] + ([res_spec] if has_res else [])
    sem = ("parallel",) * (len(grid) - 1) + (("arbitrary",) if nk > 1 else ("parallel",))
    return pl.pallas_call(
        body, name=name, grid=grid, in_specs=in_specs, out_specs=out_spec, out_shape=out_shape,
        scratch_shapes=[pltpu.VMEM(acc_shape, F32)] if nk > 1 else [],
        compiler_params=_params(sem),
    )(*operands)


TM = 1024
TKS = 1024
N_TM = SEQ // TM
N_TKS = SEQ // TKS
PA_TILE = 640
N_PA = PROJ_A_PAD // PA_TILE


def _spec(shape, fn):
    return pl.BlockSpec(shape, fn)


def _proj_nn(name, h, w, n_tile, n_tiles):
    n = n_tile * n_tiles
    return _matmul(name, h, _spec((TM, D_MODEL), lambda j, i: (i, 0)), w, _spec((D_MODEL, n_tile), lambda j, i: (0, j)), NN,
                   (n_tiles, N_TM), 1, jax.ShapeDtypeStruct((SEQ, n), BF16), _spec((TM, n_tile), lambda j, i: (i, j)))


def _proj_blocks_nn(name, h, w_blocks, n_tile, flat_out):
    nb = w_blocks.shape[0]
    if flat_out:
        out_shape = jax.ShapeDtypeStruct((SEQ, nb * n_tile), BF16)
        out_spec = _spec((TM, n_tile), lambda j, i: (i, j))
    else:
        out_shape = jax.ShapeDtypeStruct((nb, SEQ, n_tile), BF16)
        out_spec = _spec((None, TM, n_tile), lambda j, i: (j, i, 0))
    return _matmul(name, h, _spec((TM, D_MODEL), lambda j, i: (i, 0)), w_blocks,
                   _spec((None, D_MODEL, n_tile), lambda j, i: (j, 0, 0)), NN, (nb, N_TM), 1, out_shape, out_spec)


def _out_nn(name, a, w, x):
    return _matmul(name, a, _spec((TM, D_MODEL), lambda i: (i, 0)), w, _spec((D_MODEL, D_MODEL), lambda i: (0, 0)), NN,
                   (N_TM,), 1, jax.ShapeDtypeStruct((SEQ, D_MODEL), F32), _spec((TM, D_MODEL), lambda i: (i, 0)),
                   res=x, res_spec=_spec((TM, D_MODEL), lambda i: (i, 0)))


def _down_nn(name, a_blocks, w_blocks, x):
    nb = a_blocks.shape[0]
    return _matmul(name, a_blocks, _spec((None, TM, FF_BLOCK), lambda i, k: (k, i, 0)), w_blocks,
                   _spec((None, FF_BLOCK, D_MODEL), lambda i, k: (k, 0, 0)), NN, (N_TM, nb), nb,
                   jax.ShapeDtypeStruct((SEQ, D_MODEL), F32), _spec((TM, D_MODEL), lambda i, k: (i, 0)),
                   acc_shape=(TM, D_MODEL), res=x, res_spec=_spec((TM, D_MODEL), lambda i, k: (i, 0)))


def _back_nt(name, dy, w):
    n = w.shape[0]
    return _matmul(name, dy, _spec((TM, D_MODEL), lambda i: (i, 0)), w, _spec((n, D_MODEL), lambda i: (0, 0)), NT,
                   (N_TM,), 1, jax.ShapeDtypeStruct((SEQ, n), BF16), _spec((TM, n), lambda i: (i, 0)))


def _back_blocks_nt(name, dy, w_blocks):
    nb = w_blocks.shape[0]
    return _matmul(name, dy, _spec((TM, D_MODEL), lambda j, i: (i, 0)), w_blocks,
                   _spec((None, FF_BLOCK, D_MODEL), lambda j, i: (j, 0, 0)), NT, (nb, N_TM), 1,
                   jax.ShapeDtypeStruct((nb, SEQ, FF_BLOCK), BF16), _spec((None, TM, FF_BLOCK), lambda j, i: (j, i, 0)))


def _back_sum_blocks_nt(name, d_blocks, w_blocks):
    nb, _, n = d_blocks.shape
    return _matmul(name, d_blocks, _spec((None, TM, n), lambda i, k: (k, i, 0)), w_blocks,
                   _spec((None, D_MODEL, n), lambda i, k: (k, 0, 0)), NT, (N_TM, nb), nb,
                   jax.ShapeDtypeStruct((SEQ, D_MODEL), F32), _spec((TM, D_MODEL), lambda i, k: (i, 0)), acc_shape=(TM, D_MODEL))


def _back_sum_cols_nt(name, d, w_blocks=None, w=None, n_tile=None):
    nb = d.shape[1] // n_tile
    if w_blocks is not None:
        b, b_spec = w_blocks, _spec((None, D_MODEL, n_tile), lambda i, k: (k, 0, 0))
    else:
        b, b_spec = w, _spec((D_MODEL, n_tile), lambda i, k: (0, k))
    return _matmul(name, d, _spec((TM, n_tile), lambda i, k: (i, k)), b, b_spec, NT, (N_TM, nb), nb,
                   jax.ShapeDtypeStruct((SEQ, D_MODEL), F32), _spec((TM, D_MODEL), lambda i, k: (i, 0)), acc_shape=(TM, D_MODEL))


def _wgrad_tn(name, a, a_cols, d, d_cols, out_blocks):
    nb = d.shape[1] // d_cols
    if out_blocks:
        out_shape = jax.ShapeDtypeStruct((nb, a_cols, d_cols), BF16)
        out_spec = _spec((None, a_cols, d_cols), lambda j, k: (j, 0, 0))
    else:
        out_shape = jax.ShapeDtypeStruct((a_cols, nb * d_cols), BF16)
        out_spec = _spec((a_cols, d_cols), lambda j, k: (0, j))
    return _matmul(name, a, _spec((TKS, a_cols), lambda j, k: (k, 0)), d, _spec((TKS, d_cols), lambda j, k: (k, j)), TN,
                   (nb, N_TKS), N_TKS, out_shape, out_spec, acc_shape=(a_cols, d_cols))


def _wgrad_a_blocks_tn(name, a_blocks, d):
    nb = a_blocks.shape[0]
    return _matmul(name, a_blocks, _spec((None, TKS, FF_BLOCK), lambda j, k: (j, k, 0)), d,
                   _spec((TKS, D_MODEL), lambda j, k: (k, 0)), TN, (nb, N_TKS), N_TKS,
                   jax.ShapeDtypeStruct((nb, FF_BLOCK, D_MODEL), BF16), _spec((None, FF_BLOCK, D_MODEL), lambda j, k: (j, 0, 0)),
                   acc_shape=(FF_BLOCK, D_MODEL))


def _wgrad_d_blocks_tn(name, a, d_blocks):
    nb = d_blocks.shape[0]
    return _matmul(name, a, _spec((TKS, D_MODEL), lambda j, k: (k, 0)), d_blocks,
                   _spec((None, TKS, FF_BLOCK), lambda j, k: (j, k, 0)), TN, (nb, N_TKS), N_TKS,
                   jax.ShapeDtypeStruct((nb, D_MODEL, FF_BLOCK), BF16), _spec((None, D_MODEL, FF_BLOCK), lambda j, k: (j, 0, 0)),
                   acc_shape=(D_MODEL, FF_BLOCK))


NORM_ROWS = 512


def _rstd(x):
    return lax.rsqrt(jnp.mean(x * x, axis=-1, keepdims=True) + RMS_EPS)


def _norm_fwd(name, x, gamma):
    def body(x_ref, g_ref, h_ref):
        x = x_ref[...]
        h_ref[...] = (x * _rstd(x) * g_ref[...]).astype(BF16)

    row = _spec((NORM_ROWS, D_MODEL), lambda i: (i, 0))
    return pl.pallas_call(
        body, name=name, grid=(SEQ // NORM_ROWS,), in_specs=[row, _spec((1, D_MODEL), lambda i: (0, 0))], out_specs=row,
        out_shape=jax.ShapeDtypeStruct((SEQ, D_MODEL), BF16), compiler_params=_params(("parallel",)),
    )(x, gamma)


def _norm_bwd_rows(x, gamma, dh):
    r = _rstd(x)
    xh = x * r
    dxh = dh * gamma
    dx = r * (dxh - xh * jnp.mean(dxh * xh, axis=-1, keepdims=True))
    return dx, jnp.sum(dh * xh, axis=0, keepdims=True)


def _norm_bwd(name, x, gamma, dh, dx_in):
    def body(x_ref, g_ref, dh_ref, dxi_ref, dx_ref, dg_ref):
        dx, dg = _norm_bwd_rows(x_ref[...], g_ref[...], dh_ref[...].astype(F32))
        dx_ref[...] = dxi_ref[...] + dx

        @pl.when(pl.program_id(0) == 0)
        def _():
            dg_ref[...] = dg

        @pl.when(pl.program_id(0) > 0)
        def _():
            dg_ref[...] += dg

    row = _spec((NORM_ROWS, D_MODEL), lambda i: (i, 0))
    vec = _spec((1, D_MODEL), lambda i: (0, 0))
    return pl.pallas_call(
        body, name=name, grid=(SEQ // NORM_ROWS,), in_specs=[row, vec, row, row], out_specs=[row, vec],
        out_shape=[jax.ShapeDtypeStruct((SEQ, D_MODEL), F32), jax.ShapeDtypeStruct((1, D_MODEL), F32)],
        compiler_params=_params(("arbitrary",)),
    )(x, gamma, dh, dx_in)


def _loss_head(x, gamma, target):
    def body(x_ref, g_ref, t_ref, loss_ref, dx_ref, dg_ref):
        x = x_ref[...]
        gamma = g_ref[...]
        err = x * _rstd(x) * gamma - t_ref[...]
        dy = err * (1.0 / D_MODEL)
        dx, dg = _norm_bwd_rows(x, gamma, dy)
        dx_ref[...] = dx
        part = 0.5 * jnp.sum(jnp.sum(err * err, axis=-1, keepdims=True) * (1.0 / D_MODEL), axis=0, keepdims=True)
        part = jnp.broadcast_to(part, loss_ref.shape)

        @pl.when(pl.program_id(0) == 0)
        def _():
            dg_ref[...] = dg
            loss_ref[...] = part

        @pl.when(pl.program_id(0) > 0)
        def _():
            dg_ref[...] += dg
            loss_ref[...] += part

    row = _spec((NORM_ROWS, D_MODEL), lambda i: (i, 0))
    vec = _spec((1, D_MODEL), lambda i: (0, 0))
    return pl.pallas_call(
        body, name="loss_head", grid=(SEQ // NORM_ROWS,), in_specs=[row, vec, row],
        out_specs=[_spec((1, 128), lambda i: (0, 0)), row, vec],
        out_shape=[jax.ShapeDtypeStruct((1, 128), F32), jax.ShapeDtypeStruct((SEQ, D_MODEL), F32),
                   jax.ShapeDtypeStruct((1, D_MODEL), F32)],
        compiler_params=_params(("arbitrary",)),
    )(x, gamma, target)


def _sigmoid(x):
    return 1.0 / (1.0 + jnp.exp(-x))


def _rows(ref, c):
    return ref[pl.ds(pl.multiple_of(c * ROW_CHUNK, ROW_CHUNK), ROW_CHUNK), :].astype(F32)


def _rows_before(ref, c):
    start = pl.multiple_of(jnp.maximum(c * ROW_CHUNK - HALO, 0), HALO)
    rows = ref[pl.ds(start, HALO), :].astype(F32)
    return jnp.where(c > 0, rows, 0.0)


def _rows_after(ref, c, n_chunks):
    start = pl.multiple_of(jnp.minimum((c + 1) * ROW_CHUNK, SEQ - HALO), HALO)
    rows = ref[pl.ds(start, HALO), :].astype(F32)
    return jnp.where(c < n_chunks - 1, rows, 0.0)


def _shift_down(z, before, n):
    row = lax.broadcasted_iota(jnp.int32, z.shape, 0)
    out = pltpu.roll(z, n, 0)
    for r in range(n):
        out = jnp.where(row == r, before[HALO - n + r:HALO - n + r + 1, :], out)
    return out


def _shift_up(z, after, n):
    rows = z.shape[0]
    row = lax.broadcasted_iota(jnp.int32, z.shape, 0)
    out = pltpu.roll(z, rows - n, 0)
    for r in range(n):
        out = jnp.where(row == rows - n + r, after[r:r + 1, :], out)
    return out


def _conv_rows(z, before, w):
    z1 = _shift_down(z, before, 1)
    z2 = _shift_down(z, before, 2)
    return w[2:3, :] * z + w[1:2, :] * z1 + w[0:1, :] * z2, z1, z2


def _conv_t_rows(dy, after, w):
    return w[2:3, :] * dy + w[1:2, :] * _shift_up(dy, after, 1) + w[0:1, :] * _shift_up(dy, after, 2)


N_ROW_CHUNKS = SEQ // ROW_CHUNK


def _ffn_mid_fwd(name, gu, conv_w):
    def body(gu_ref, w_ref, a_ref):
        w = w_ref[...]

        def chunk(c, carry):
            g = _rows(gu_ref.at[0], c)
            u = _rows(gu_ref.at[1], c)
            gc, _, _ = _conv_rows(g, _rows_before(gu_ref.at[0], c), w)
            a_ref[pl.ds(pl.multiple_of(c * ROW_CHUNK, ROW_CHUNK), ROW_CHUNK), :] = (gc * _sigmoid(gc) * u).astype(BF16)
            return carry

        lax.fori_loop(0, N_ROW_CHUNKS, chunk, 0)

    return pl.pallas_call(
        body, name=name, grid=(FF_BLOCKS,),
        in_specs=[_spec((2, None, SEQ, FF_BLOCK), lambda j: (0, j, 0, 0)), _spec((None, 3, FF_BLOCK), lambda j: (j, 0, 0))],
        out_specs=_spec((None, SEQ, FF_BLOCK), lambda j: (j, 0, 0)),
        out_shape=jax.ShapeDtypeStruct((FF_BLOCKS, SEQ, FF_BLOCK), BF16), compiler_params=_params(("parallel",)),
    )(gu, conv_w)


def _ffn_mid_bwd(name, gu, conv_w, da):
    def body(gu_ref, w_ref, da_ref, dgu_ref, dw_ref, dgc_ref):
        w = w_ref[...]

        def first(c, acc):
            g = _rows(gu_ref.at[0], c)
            u = _rows(gu_ref.at[1], c)
            d = _rows(da_ref, c)
            gc, g1, g2 = _conv_rows(g, _rows_before(gu_ref.at[0], c), w)
            sg = _sigmoid(gc)
            rows = pl.ds(pl.multiple_of(c * ROW_CHUNK, ROW_CHUNK), ROW_CHUNK)
            dgu_ref[1, rows, :] = (d * gc * sg).astype(BF16)
            dgc = d * u * (sg * (1.0 + gc * (1.0 - sg)))
            dgc_ref[rows, :] = dgc
            return (acc[0] + jnp.sum(dgc * g2, axis=0, keepdims=True), acc[1] + jnp.sum(dgc * g1, axis=0, keepdims=True),
                    acc[2] + jnp.sum(dgc * g, axis=0, keepdims=True))

        zero = jnp.zeros((1, FF_BLOCK), F32)
        acc = lax.fori_loop(0, N_ROW_CHUNKS, first, (zero, zero, zero))
        for r in range(3):
            dw_ref[r:r + 1, :] = acc[r]

        def second(c, carry):
            dgc = _rows(dgc_ref, c)
            dg = _conv_t_rows(dgc, _rows_after(dgc_ref, c, N_ROW_CHUNKS), w)
            dgu_ref[0, pl.ds(pl.multiple_of(c * ROW_CHUNK, ROW_CHUNK), ROW_CHUNK), :] = dg.astype(BF16)
            return carry

        lax.fori_loop(0, N_ROW_CHUNKS, second, 0)

    pair = _spec((2, None, SEQ, FF_BLOCK), lambda j: (0, j, 0, 0))
    wspec = _spec((None, 3, FF_BLOCK), lambda j: (j, 0, 0))
    return pl.pallas_call(
        body, name=name, grid=(FF_BLOCKS,),
        in_specs=[pair, wspec, _spec((None, SEQ, FF_BLOCK), lambda j: (j, 0, 0))], out_specs=[pair, wspec],
        out_shape=[jax.ShapeDtypeStruct((2, FF_BLOCKS, SEQ, FF_BLOCK), BF16), jax.ShapeDtypeStruct((FF_BLOCKS, 3, FF_BLOCK), F32)],
        scratch_shapes=[pltpu.VMEM((SEQ, FF_BLOCK), F32)], compiler_params=_params(("parallel",)),
    )(gu, conv_w, da)


SC_COLS = 256
N_SC = D_MODEL // SC_COLS


def _sc_specs():
    return [_spec((SEQ, SC_COLS), lambda j, part=part: (0, part * N_SC + j)) for part in range(3)]


def _sc_mid_fwd(p, conv_w):
    def body(b_ref, c_ref, h_ref, w_ref, y_ref):
        w = w_ref[...]

        def chunk(c, carry):
            z = _rows(c_ref, c) * _rows(h_ref, c)
            before = _rows_before(c_ref, c) * _rows_before(h_ref, c)
            zc, _, _ = _conv_rows(z, before, w)
            y_ref[pl.ds(pl.multiple_of(c * ROW_CHUNK, ROW_CHUNK), ROW_CHUNK), :] = (_rows(b_ref, c) * zc).astype(BF16)
            return carry

        lax.fori_loop(0, N_ROW_CHUNKS, chunk, 0)

    col = _spec((SEQ, SC_COLS), lambda j: (0, j))
    return pl.pallas_call(
        body, name="sc_mid_fwd", grid=(N_SC,), in_specs=_sc_specs() + [_spec((3, SC_COLS), lambda j: (0, j))], out_specs=col,
        out_shape=jax.ShapeDtypeStruct((SEQ, D_MODEL), BF16), compiler_params=_params(("parallel",)),
    )(p, p, p, conv_w)


def _sc_mid_bwd(p, conv_w, dy):
    def body(b_ref, c_ref, h_ref, w_ref, dy_ref, db_ref, dc_ref, dh_ref, dw_ref, dzc_ref):
        w = w_ref[...]

        def first(c, acc):
            z = _rows(c_ref, c) * _rows(h_ref, c)
            before = _rows_before(c_ref, c) * _rows_before(h_ref, c)
            zc, z1, z2 = _conv_rows(z, before, w)
            d = _rows(dy_ref, c)
            rows = pl.ds(pl.multiple_of(c * ROW_CHUNK, ROW_CHUNK), ROW_CHUNK)
            db_ref[rows, :] = (d * zc).astype(BF16)
            dzc = d * _rows(b_ref, c)
            dzc_ref[rows, :] = dzc
            return (acc[0] + jnp.sum(dzc * z2, axis=0, keepdims=True), acc[1] + jnp.sum(dzc * z1, axis=0, keepdims=True),
                    acc[2] + jnp.sum(dzc * z, axis=0, keepdims=True))

        zero = jnp.zeros((1, SC_COLS), F32)
        acc = lax.fori_loop(0, N_ROW_CHUNKS, first, (zero, zero, zero))
        for r in range(3):
            dw_ref[r:r + 1, :] = acc[r]

        def second(c, carry):
            dz = _conv_t_rows(_rows(dzc_ref, c), _rows_after(dzc_ref, c, N_ROW_CHUNKS), w)
            rows = pl.ds(pl.multiple_of(c * ROW_CHUNK, ROW_CHUNK), ROW_CHUNK)
            dc_ref[rows, :] = (dz * _rows(h_ref, c)).astype(BF16)
            dh_ref[rows, :] = (dz * _rows(c_ref, c)).astype(BF16)
            return carry

        lax.fori_loop(0, N_ROW_CHUNKS, second, 0)

    col = _spec((SEQ, SC_COLS), lambda j: (0, j))
    wspec = _spec((3, SC_COLS), lambda j: (0, j))
    act = jax.ShapeDtypeStruct((SEQ, D_MODEL), BF16)
    return pl.pallas_call(
        body, name="sc_mid_bwd", grid=(N_SC,), in_specs=_sc_specs() + [wspec, col], out_specs=[col, col, col, wspec],
        out_shape=[act, act, act, jax.ShapeDtypeStruct((3, D_MODEL), F32)],
        scratch_shapes=[pltpu.VMEM((SEQ, SC_COLS), F32)], compiler_params=_params(("parallel",)),
    )(p, p, p, conv_w, dy)


GLA_GROUP = 4
GLA_ROWS = GLA_GROUP * CHUNK
N_GROUPS = N_CHUNKS // GLA_GROUP
Q0, K0, V0, R0, G0 = 0, KEY_DIM, 2 * KEY_DIM, 2 * KEY_DIM + VALUE_DIM, 2 * KEY_DIM + 2 * VALUE_DIM


def _tri(strict):
    r = lax.broadcasted_iota(jnp.int32, (CHUNK, CHUNK), 0)
    c = lax.broadcasted_iota(jnp.int32, (CHUNK, CHUNK), 1)
    return jnp.where(c < r if strict else c <= r, 1.0, 0.0).astype(F32)


def _cumsum_rows(tri, x):
    return jnp.dot(tri, x, preferred_element_type=F32, precision=lax.Precision.HIGHEST)


def _gate_logits(gl, wgu, b_gate):
    return jnp.dot(gl, wgu, preferred_element_type=F32) + b_gate


def _log_decay(logits):
    return (jnp.minimum(logits, 0.0) - jnp.log(1.0 + jnp.exp(-jnp.abs(logits)))) * (1.0 / GATE_NORMALIZER)


def _head(x, h, width):
    return x[:, h * width:(h + 1) * width]


def _gla_fwd(proj, wgu, b_gate, gn):
    def body(p_ref, wgu_ref, b_ref, gn_ref, o_ref, og_ref, st_ref, state):
        @pl.when(pl.program_id(0) == 0)
        def _():
            state[...] = jnp.zeros_like(state)

        tri = _tri(False)
        la = _log_decay(_gate_logits(p_ref[:, G0:G0 + GATE_PAD], wgu_ref[...], b_ref[...]))
        for c in range(GLA_GROUP):
            rows = slice(c * CHUNK, (c + 1) * CHUNK)
            cum = _cumsum_rows(tri, la[rows])
            tot = cum[CHUNK - 1:CHUNK, :]
            kd = (p_ref[rows, K0:K0 + KEY_DIM].astype(F32) * jnp.exp(tot - cum)).astype(BF16)
            decay = jnp.exp(tot)
            q = (p_ref[rows, Q0:Q0 + KEY_DIM].astype(F32) * (HEAD_K ** -0.5)).astype(BF16)
            v = p_ref[rows, V0:V0 + VALUE_DIM]
            for h in range(GLA_HEADS):
                upd = lax.dot_general(_head(v, h, HEAD_V), _head(kd, h, HEAD_K), (TN, ((), ())), preferred_element_type=F32)
                s = state[h] * _head(decay, h, HEAD_K) + upd
                state[h] = s
                st_ref[c, h] = s
                o_ref[rows, h * HEAD_V:(h + 1) * HEAD_V] = lax.dot_general(
                    _head(q, h, HEAD_K), s.astype(BF16), (NT, ((), ())), preferred_element_type=F32)
        r = p_ref[:, R0:R0 + VALUE_DIM].astype(F32)
        gate = r * _sigmoid(r) * gn_ref[...]
        for h in range(GLA_HEADS):
            cols = slice(h * HEAD_V, (h + 1) * HEAD_V)
            o = o_ref[:, cols]
            og_ref[:, cols] = (o * _rstd(o) * gate[:, cols]).astype(BF16)

    rows = _spec((GLA_ROWS, VALUE_DIM), lambda i: (i, 0))
    const = lambda shape: _spec(shape, lambda i: (0,) * len(shape))
    return pl.pallas_call(
        body, name="gla_fwd", grid=(N_GROUPS,),
        in_specs=[_spec((GLA_ROWS, PROJ_A_PAD), lambda i: (i, 0)), const((GATE_PAD, KEY_DIM)), const((1, KEY_DIM)),
                  const((1, VALUE_DIM))],
        out_specs=[rows, rows, _spec((GLA_GROUP, GLA_HEADS, HEAD_V, HEAD_K), lambda i: (i, 0, 0, 0))],
        out_shape=[jax.ShapeDtypeStruct((SEQ, VALUE_DIM), F32), jax.ShapeDtypeStruct((SEQ, VALUE_DIM), BF16),
                   jax.ShapeDtypeStruct((N_CHUNKS, GLA_HEADS, HEAD_V, HEAD_K), F32)],
        scratch_shapes=[pltpu.VMEM((GLA_HEADS, HEAD_V, HEAD_K), F32)], compiler_params=_params(("arbitrary",)),
    )(proj, wgu, b_gate, gn)


def _gla_bwd(proj, wgu, b_gate, gn, o, states, dog):
    last = N_GROUPS - 1

    def body(p_ref, wgu_ref, b_ref, gn_ref, o_ref, st_ref, stp_ref, dog_ref, dp_ref, dwgu_ref, db_ref, dgn_ref, carry, do_buf):
        step = pl.program_id(0)

        @pl.when(step == 0)
        def _():
            carry[...] = jnp.zeros_like(carry)

        r = p_ref[:, R0:R0 + VALUE_DIM].astype(F32)
        sr = _sigmoid(r)
        silu = r * sr
        gn_row = gn_ref[...]
        dog_rows = dog_ref[...].astype(F32)
        dn = dog_rows * silu
        dgn_cols = []
        for h in range(GLA_HEADS):
            cols = slice(h * HEAD_V, (h + 1) * HEAD_V)
            oh = o_ref[:, cols]
            rs = _rstd(oh)
            ohat = oh * rs
            dn_h = dn[:, cols]
            dgn_cols.append(jnp.sum(dn_h * ohat, axis=0, keepdims=True))
            dohat = dn_h * gn_row[:, cols]
            do_buf[:, cols] = rs * (dohat - ohat * jnp.mean(dohat * ohat, axis=-1, keepdims=True))
            n_h = ohat * gn_row[:, cols]
            dp_ref[:, R0 + h * HEAD_V:R0 + (h + 1) * HEAD_V] = (
                dog_rows[:, cols] * n_h * (sr[:, cols] * (1.0 + r[:, cols] * (1.0 - sr[:, cols])))).astype(BF16)
        dgn = jnp.concatenate(dgn_cols, axis=1)

        tri = _tri(False)
        tri_strict = _tri(True)
        gl = p_ref[:, G0:G0 + GATE_PAD]
        logits = _gate_logits(gl, wgu_ref[...], b_ref[...])
        la = _log_decay(logits)
        dlogit_rows = []
        for c in reversed(range(GLA_GROUP)):
            rows = slice(c * CHUNK, (c + 1) * CHUNK)
            cum = _cumsum_rows(tri, la[rows])
            tot = cum[CHUNK - 1:CHUNK, :]
            fade = jnp.exp(tot - cum)
            k = p_ref[rows, K0:K0 + KEY_DIM].astype(F32)
            kd32 = k * fade
            kd = kd32.astype(BF16)
            decay = jnp.exp(tot)
            q = (p_ref[rows, Q0:Q0 + KEY_DIM].astype(F32) * (HEAD_K ** -0.5)).astype(BF16)
            v = p_ref[rows, V0:V0 + VALUE_DIM]
            do = do_buf[rows, :].astype(BF16)
            dkd_cols, ddecay_cols = [], []
            for h in range(GLA_HEADS):
                do_h = _head(do, h, HEAD_V)
                s = st_ref[c, h]
                dq = jnp.dot(do_h, s.astype(BF16), preferred_element_type=F32) * (HEAD_K ** -0.5)
                dp_ref[rows, Q0 + h * HEAD_K:Q0 + (h + 1) * HEAD_K] = dq.astype(BF16)
                g = carry[h] + lax.dot_general(do_h, _head(q, h, HEAD_K), (TN, ((), ())), preferred_element_type=F32)
                g16 = g.astype(BF16)
                dkd_cols.append(jnp.dot(_head(v, h, HEAD_V), g16, preferred_element_type=F32))
                dv = lax.dot_general(_head(kd, h, HEAD_K), g16, (NT, ((), ())), preferred_element_type=F32)
                dp_ref[rows, V0 + h * HEAD_V:V0 + (h + 1) * HEAD_V] = dv.astype(BF16)
                if c > 0:
                    s_prev = st_ref[c - 1, h]
                else:
                    s_prev = jnp.where(step < last, stp_ref[0, h], 0.0)
                ddecay_cols.append(jnp.sum(g * s_prev, axis=0, keepdims=True))
                carry[h] = g * _head(decay, h, HEAD_K)
            dkd = jnp.concatenate(dkd_cols, axis=1)
            ddecay = jnp.concatenate(ddecay_cols, axis=1)
            dp_ref[rows, K0:K0 + KEY_DIM] = (dkd * fade).astype(BF16)
            e = dkd * kd32
            dla = ddecay * decay + _cumsum_rows(tri_strict, e)
            dlogit_rows.append(dla * (1.0 / GATE_NORMALIZER) * (1.0 - _sigmoid(logits[rows])))
        dlogit = jnp.concatenate(dlogit_rows[::-1], axis=0)
        dlogit16 = dlogit.astype(BF16)
        dp_ref[:, G0:G0 + GATE_PAD] = lax.dot_general(
            dlogit16, wgu_ref[...], (NT, ((), ())), preferred_element_type=F32).astype(BF16)
        dwgu = lax.dot_general(gl, dlogit16, (TN, ((), ())), preferred_element_type=F32)
        db = jnp.sum(dlogit, axis=0, keepdims=True)

        @pl.when(step == 0)
        def _():
            dwgu_ref[...] = dwgu
            db_ref[...] = db
            dgn_ref[...] = dgn

        @pl.when(step > 0)
        def _():
            dwgu_ref[...] += dwgu
            db_ref[...] += db
            dgn_ref[...] += dgn

    rev = lambda i: (last - i, 0)
    rows = _spec((GLA_ROWS, VALUE_DIM), rev)
    const = lambda shape: _spec(shape, lambda i: (0,) * len(shape))
    st_shape = (GLA_HEADS, HEAD_V, HEAD_K)
    return pl.pallas_call(
        body, name="gla_bwd", grid=(N_GROUPS,),
        in_specs=[_spec((GLA_ROWS, PROJ_A_PAD), rev), const((GATE_PAD, KEY_DIM)), const((1, KEY_DIM)), const((1, VALUE_DIM)),
                  rows, _spec((GLA_GROUP,) + st_shape, lambda i: (last - i, 0, 0, 0)),
                  _spec((1,) + st_shape, lambda i: (jnp.maximum((last - i) * GLA_GROUP - 1, 0), 0, 0, 0)), rows],
        out_specs=[_spec((GLA_ROWS, PROJ_A_PAD), rev), const((GATE_PAD, KEY_DIM)), const((1, KEY_DIM)), const((1, VALUE_DIM))],
        out_shape=[jax.ShapeDtypeStruct((SEQ, PROJ_A_PAD), BF16), jax.ShapeDtypeStruct((GATE_PAD, KEY_DIM), F32),
                   jax.ShapeDtypeStruct((1, KEY_DIM), F32), jax.ShapeDtypeStruct((1, VALUE_DIM), F32)],
        scratch_shapes=[pltpu.VMEM(st_shape, F32), pltpu.VMEM((GLA_ROWS, VALUE_DIM), F32)],
        compiler_params=_params(("arbitrary",)),
    )(proj, wgu, b_gate, gn, o, states, states, dog)


def _ffn_fwd(tag, x, gamma, w_up, conv_w, w_down):
    h = _norm_fwd(f"ffn{tag}_norm", x, gamma)
    gu = _proj_blocks_nn(f"ffn{tag}_up", h, w_up, FF_BLOCK, False).reshape(2, FF_BLOCKS, SEQ, FF_BLOCK)
    a = _ffn_mid_fwd(f"ffn{tag}_mid", gu, conv_w)
    return _down_nn(f"ffn{tag}_down", a, w_down, x), (h, gu, a)


def _ffn_bwd(tag, x, gamma, w_up, conv_w, w_down, saved, dx):
    h, gu, a = saved
    da = _back_blocks_nt(f"ffn{tag}_da", dx, w_down)
    d_w_down = _wgrad_a_blocks_tn(f"ffn{tag}_dwdown", a, dx)
    dgu, d_conv = _ffn_mid_bwd(f"ffn{tag}_mid_bwd", gu, conv_w, da)
    dgu = dgu.reshape(2 * FF_BLOCKS, SEQ, FF_BLOCK)
    dh = _back_sum_blocks_nt(f"ffn{tag}_dh", dgu, w_up)
    d_w_up = _wgrad_d_blocks_tn(f"ffn{tag}_dwup", h, dgu)
    dx, d_gamma = _norm_bwd(f"ffn{tag}_norm_bwd", x, gamma, dh, dx)
    return dx, d_gamma, d_w_up, d_conv, d_w_down


def _local_step(x, target, w, fetch=None, emit=None):
    if fetch is None:
        local = dict(a=(w.get("a_w_in"), w.get("a_w_out")), b=(w.get("b_w_in"), w.get("b_w_out")))
        for layer in range(2):
            local[f"f{layer}"] = (w["f_w_up"][layer], w["f_w_down"][layer]) if "f_w_up" in w else None
        fetch = lambda group, after: local[group]
    if emit is None:
        emit = lambda group, grads, dx: dx
    f_norm = (w["f_norm"][0:1], w["f_norm"][1:2])

    x0 = x
    a_w_in, a_w_out = fetch("a", x0)
    h0 = _norm_fwd("a_norm", x0, w["a_norm"])
    proj = _proj_nn("a_in", h0, a_w_in, PA_TILE, N_PA)
    o, og, states = _gla_fwd(proj, w["a_w_gate_up"], w["a_b_gate"], w["a_gn"])
    x1 = _out_nn("a_out", og, a_w_out, x0)
    up0, down0 = fetch("f0", x1)
    x2, ffn0 = _ffn_fwd(0, x1, f_norm[0], up0, w["f_conv"][0], down0)
    b_w_in, b_w_out = fetch("b", x2)
    h2 = _norm_fwd("b_norm", x2, w["b_norm"])
    p = _proj_blocks_nn("b_in", h2, b_w_in, B_SHARD, True)
    y = _sc_mid_fwd(p, w["b_conv"])
    x3 = _out_nn("b_out", y, b_w_out, x2)
    up1, down1 = fetch("f1", x3)
    x4, ffn1 = _ffn_fwd(1, x3, f_norm[1], up1, w["f_conv"][1], down1)
    loss, dx, d_final_norm = _loss_head(x4, w["final_norm"], target)

    dx, d_f_norm1, d_up1, d_fconv1, d_down1 = _ffn_bwd(1, x3, f_norm[1], up1, w["f_conv"][1], down1, ffn1, dx)
    dx = emit("f1", (d_up1, d_down1), dx)

    dy = _back_nt("b_dy", dx, b_w_out)
    d_b_w_out = _wgrad_tn("b_dwout", y, D_MODEL, dx, D_MODEL, False)
    db, dc, dhh, d_b_conv = _sc_mid_bwd(p, w["b_conv"], dy)
    dp = jnp.concatenate([db, dc, dhh], axis=1)
    dh2 = _back_sum_cols_nt("b_dh", dp, w_blocks=b_w_in, n_tile=B_SHARD)
    d_b_w_in = _wgrad_tn("b_dwin", h2, D_MODEL, dp, B_SHARD, True)
    dx, d_b_norm = _norm_bwd("b_norm_bwd", x2, w["b_norm"], dh2, dx)
    dx = emit("b", (d_b_w_in, d_b_w_out), dx)

    dx, d_f_norm0, d_up0, d_fconv0, d_down0 = _ffn_bwd(0, x1, f_norm[0], up0, w["f_conv"][0], down0, ffn0, dx)
    dx = emit("f0", (d_up0, d_down0), dx)

    dog = _back_nt("a_dog", dx, a_w_out)
    d_a_w_out = _wgrad_tn("a_dwout", og, D_MODEL, dx, D_MODEL, False)
    dproj, d_wgu, d_b_gate, d_gn = _gla_bwd(proj, w["a_w_gate_up"], w["a_b_gate"], w["a_gn"], o, states, dog)
    d_a_w_in = _wgrad_tn("a_dwin", h0, D_MODEL, dproj, PA_TILE, False)
    dproj = emit("a", (d_a_w_in, d_a_w_out), dproj)
    dh0 = _back_sum_cols_nt("a_dh", dproj, w=a_w_in, n_tile=PA_TILE)
    dx, d_a_norm = _norm_bwd("a_norm_bwd", x0, w["a_norm"], dh0, dx)

    grads = dict(
        a_norm=d_a_norm, a_w_in=d_a_w_in, a_w_gate_up=d_wgu, a_b_gate=d_b_gate, a_gn=d_gn, a_w_out=d_a_w_out,
        b_norm=d_b_norm, b_w_in=d_b_w_in, b_conv=d_b_conv, b_w_out=d_b_w_out,
        f_norm=(d_f_norm0, d_f_norm1), f_w_up=(d_up0, d_up1), f_conv=(d_fconv0, d_fconv1), f_w_down=(d_down0, d_down1),
        final_norm=d_final_norm)
    return loss[0, 0], dx, grads


MESH_ID = pl.DeviceIdType.MESH
ANY = pl.BlockSpec(memory_space=pl.ANY)
N_PEERS = N_DEV - 1


def _position():
    return lax.axis_index("x"), lax.axis_index("y"), lax.axis_index("c")


def _slot(px, py, pc):
    return 4 * px + 2 * py + pc


def _all_gather(name, shards):
    n = len(shards)

    def body(*refs):
        ins, outs = refs[:n], refs[n:2 * n]
        send_sems, recv_sems, local_sems = refs[2 * n:]
        x, y, c = _position()
        me, sibling = (x, y, c), (x, y, 1 - c)
        chips = [(1 - x, y), (x, 1 - y), (1 - x, 1 - y)]

        def copy(t, k, block, to, from_input=False):
            dst = outs[t].at[_slot(*block)]
            return pltpu.make_async_remote_copy(
                src_ref=ins[t] if from_input else dst, dst_ref=dst, send_sem=send_sems.at[t, k], recv_sem=recv_sems.at[t, k],
                device_id=to, device_id_type=MESH_ID)

        mine = [pltpu.make_async_copy(ins[t], outs[t].at[_slot(*me)], local_sems.at[t]) for t in range(n)]
        for cp in mine:
            cp.start()
        first = []
        for t in range(n):
            first.append(copy(t, 0, me, sibling, True))
            first += [copy(t, 1 + j, me, (*chip, c), True) for j, chip in enumerate(chips)]
        for cp in first:
            cp.start()
        passed = []
        for t in range(n):
            for j, chip in enumerate(chips):
                copy(t, 1 + j, (*chip, c), me).wait_recv()
                fwd = copy(t, 4 + j, (*chip, c), sibling)
                fwd.start()
                passed.append(fwd)
        for t in range(n):
            copy(t, 0, sibling, me).wait_recv()
            for j, chip in enumerate(chips):
                copy(t, 4 + j, (*chip, 1 - c), me).wait_recv()
        for cp in first + passed:
            cp.wait_send()
        for cp in mine:
            cp.wait()

    return pl.pallas_call(
        body, name=name, in_specs=[ANY] * n, out_specs=[ANY] * n,
        out_shape=[jax.ShapeDtypeStruct((N_DEV,) + s.shape, s.dtype) for s in shards],
        scratch_shapes=[pltpu.SemaphoreType.DMA((n, N_PEERS)), pltpu.SemaphoreType.DMA((n, N_PEERS)), pltpu.SemaphoreType.DMA((n,))],
    )(*shards)


ALL_PEERS = (1, 2, 3, 4, 5, 6, 7)
SIBLING_AND_SAME_CORE = (1, 2, 4, 6)
SAME_CORE = (2, 4, 6)


def _flip(x, y, c, k):
    return x ^ (k >> 2), y ^ ((k >> 1) & 1), c ^ (k & 1)


def _send_copy(parts, landing, shared, send_sems, recv_sems, t, s, k):
    x, y, c = _position()
    peer = _flip(x, y, c, k)
    src = parts[t] if shared[t] else parts[t].at[_slot(*peer)]
    return pltpu.make_async_remote_copy(
        src_ref=src, dst_ref=landing[t].at[_slot(x, y, c)], send_sem=send_sems.at[s], recv_sem=recv_sems.at[s],
        device_id=peer, device_id_type=MESH_ID)


def _send_arrival(landing, send_sems, recv_sems, t, s, k):
    x, y, c = _position()
    peer = _flip(x, y, c, k)
    landed = landing[t].at[_slot(*peer)]
    return pltpu.make_async_remote_copy(
        src_ref=landed, dst_ref=landed, send_sem=send_sems.at[s], recv_sem=recv_sems.at[s],
        device_id=peer, device_id_type=MESH_ID)


def _handshake(peers):
    x, y, c = _position()
    barrier = pltpu.get_barrier_semaphore()
    for k in peers:
        pl.semaphore_signal(barrier, inc=1, device_id=_flip(x, y, c, k), device_id_type=MESH_ID)
    pl.semaphore_wait(barrier, len(peers))


def _sequencer(name, collective_id, n_copies, body, operands, out_type):
    n_arrays = len(operands)
    return pl.kernel(
        body, out_type=out_type, mesh=plsc.ScalarSubcoreMesh(axis_name="sequencer", num_cores=1), name=name,
        scratch_types=(pltpu.SemaphoreType.DMA((n_copies,)), pltpu.SemaphoreType.DMA((n_copies,)),
                       pltpu.SemaphoreType.DMA((n_arrays,))),
        compiler_params=pltpu.CompilerParams(collective_id=collective_id))(*operands)


def _sequencer_exchange(name, collective_id, parts, shared, after=()):
    n, n_peers, n_in = len(parts), len(ALL_PEERS), len(parts) + len(after)

    def body(*refs):
        src, landing = refs[:n], refs[n_in:n_in + n]
        send_sems, recv_sems, local_sems = refs[n_in + n:]
        _handshake(ALL_PEERS)
        my_slot = _slot(*_position())
        mine = [pltpu.make_async_copy(src[t] if shared[t] else src[t].at[my_slot], landing[t].at[my_slot], local_sems.at[t])
                for t in range(n)]
        for cp in mine:
            cp.start()
        sent = [_send_copy(src, landing, shared, send_sems, recv_sems, t, t * n_peers + j, k)
                for t in range(n) for j, k in enumerate(ALL_PEERS)]
        for cp in sent:
            cp.start()
        for t in range(n):
            for j, k in enumerate(ALL_PEERS):
                _send_arrival(landing, send_sems, recv_sems, t, t * n_peers + j, k).wait_recv()
        for cp in sent:
            cp.wait_send()
        for cp in mine:
            cp.wait()

    landing = [jax.ShapeDtypeStruct(((N_DEV,) + p.shape) if sh else p.shape, p.dtype) for p, sh in zip(parts, shared)]
    return _sequencer(name, collective_id, n * n_peers, body, list(parts) + list(after), landing)


def _sequencer_gather(name, collective_id, shards):
    n, per = len(shards), N_PEERS

    def body(*refs):
        src, out = refs[:n], refs[n:2 * n]
        send_sems, recv_sems, local_sems = refs[2 * n:]
        _handshake(SIBLING_AND_SAME_CORE)
        x, y, c = _position()
        me, sibling = (x, y, c), (x, y, 1 - c)

        def copy(t, j, block, to, from_input=False):
            dst = out[t].at[_slot(*block)]
            return pltpu.make_async_remote_copy(
                src_ref=src[t] if from_input else dst, dst_ref=dst, send_sem=send_sems.at[t * per + j],
                recv_sem=recv_sems.at[t * per + j], device_id=to, device_id_type=MESH_ID)

        mine = [pltpu.make_async_copy(src[t], out[t].at[_slot(*me)], local_sems.at[t]) for t in range(n)]
        for cp in mine:
            cp.start()
        sent = [copy(t, j, me, _flip(x, y, c, k), True) for t in range(n) for j, k in enumerate(SIBLING_AND_SAME_CORE)]
        for cp in sent:
            cp.start()
        for t in range(n):
            for j, k in enumerate(SAME_CORE):
                block = _flip(x, y, c, k)
                copy(t, 1 + j, block, me).wait_recv()
                forward = copy(t, 4 + j, block, sibling)
                forward.start()
                sent.append(forward)
        for t in range(n):
            copy(t, 0, sibling, me).wait_recv()
            for j, k in enumerate(SAME_CORE):
                copy(t, 4 + j, _flip(x, y, 1 - c, k), me).wait_recv()
        for cp in sent:
            cp.wait_send()
        for cp in mine:
            cp.wait()

    gathered = [jax.ShapeDtypeStruct((N_DEV,) + s.shape, s.dtype) for s in shards]
    return _sequencer(name, collective_id, n * per, body, shards, gathered)


ADAM_ROWS = 256


def _adam_update(w, g, m, v):
    m = ADAM_B1 * m + (1.0 - ADAM_B1) * g
    v = ADAM_B2 * v + (1.0 - ADAM_B2) * (g * g)
    m_hat = m / (1.0 - ADAM_B1 ** ADAM_STEP)
    v_hat = v / (1.0 - ADAM_B2 ** ADAM_STEP)
    delta = -ADAM_LR * (m_hat / (jnp.sqrt(v_hat) + ADAM_EPS) + ADAM_WD * w)
    return delta, m, v


def _sum_slots(ref):
    total = ref[0].astype(F32)
    for d in range(1, N_DEV):
        total = total + ref[d].astype(F32)
    return total


def _adamw_sum(name, landed, w, m, v):
    layers, rows, cols = w.shape
    tr = ADAM_ROWS if rows % ADAM_ROWS == 0 else rows
    nt = rows // tr

    def body(*refs):
        parts = refs[:layers]
        w_ref, m_ref, v_ref, g_ref, d_ref, nm_ref, nv_ref = refs[layers:]
        layer = pl.program_id(0)
        g = _sum_slots(parts[0])
        for q in range(1, layers):
            g = jnp.where(layer == q, _sum_slots(parts[q]), g)
        delta, new_m, new_v = _adam_update(w_ref[...], g, m_ref[...], v_ref[...])
        g_ref[...] = g
        d_ref[...] = delta
        nm_ref[...] = new_m
        nv_ref[...] = new_v

    def part_spec(q):
        return _spec((N_DEV, tr, cols), lambda l, i: (0, jnp.where(l == q, i, jnp.where(l < q, 0, nt - 1)), 0))

    tile = _spec((None, tr, cols), lambda l, i: (l, i, 0))
    out = jax.ShapeDtypeStruct((layers, rows, cols), F32)
    return pl.pallas_call(
        body, name=name, grid=(layers, nt), in_specs=[part_spec(q) for q in range(layers)] + [tile] * 3,
        out_specs=[tile] * 4, out_shape=[out] * 4, compiler_params=_params(("arbitrary", "arbitrary")),
    )(*landed, w, m, v)


def _sum_small(landed):
    def body(in_ref, out_ref):
        out_ref[...] = _sum_slots(in_ref)

    return pl.pallas_call(body, name="small_grad_sum", out_shape=jax.ShapeDtypeStruct(landed.shape[1:], F32))(landed)


def _adamw_small(name, g, w, m, v):
    def body(g_ref, w_ref, m_ref, v_ref, d_ref, nm_ref, nv_ref):
        d_ref[...], nm_ref[...], nv_ref[...] = _adam_update(w_ref[...], g_ref[...], m_ref[...], v_ref[...])

    out = jax.ShapeDtypeStruct(w.shape, F32)
    return pl.pallas_call(body, name=name, out_shape=[out] * 3)(g, w, m, v)


LANES = 128
SUBLANES = 8
F_CONV_SHARD = D_FF // N_DEV
GATE_SHARD = KEY_DIM // N_DEV
NORM_SHARD = D_MODEL // N_DEV


def _tile_rows(a):
    flat = a.reshape(-1)
    size = -(-flat.shape[0] // (SUBLANES * LANES)) * SUBLANES * LANES
    return jnp.pad(flat, (0, size - flat.shape[0])).reshape(-1, LANES)


def _pack_rows(pieces):
    return jnp.concatenate([_tile_rows(p) for p in pieces], axis=0)


def _unpack_rows(packed, shapes):
    out, row = [], 0
    for shape in shapes:
        size = 1
        for s in shape:
            size *= s
        rows = -(-size // (SUBLANES * LANES)) * SUBLANES
        piece = packed[..., row:row + rows, :]
        out.append(piece.reshape(piece.shape[:-2] + (rows * LANES,))[..., :size])
        row += rows
    return out


SMALL_SHARDS = ((GATE_RANK, GATE_SHARD), (1, NORM_SHARD), (3, NORM_SHARD), (2, 3, F_CONV_SHARD))


def _unpack_small_shards(g):
    gate, b_norm, b_conv, f_conv = _unpack_rows(g, SMALL_SHARDS)
    gate = gate.reshape(N_DEV, GATE_RANK, GATE_SHARD).transpose(1, 0, 2).reshape(GATE_RANK, KEY_DIM)
    b_norm = b_norm.reshape(1, D_MODEL)
    b_conv = b_conv.reshape(N_DEV, 3, NORM_SHARD).transpose(1, 0, 2).reshape(3, D_MODEL)
    f_conv = f_conv.reshape(N_DEV, 2, 3, F_CONV_SHARD).transpose(1, 2, 0, 3).reshape(2, 3, D_FF)
    return gate, b_norm, b_conv, f_conv


def _conv_blocks(f_conv):
    return f_conv.reshape(2, 3, FF_BLOCKS, FF_BLOCK).transpose(0, 2, 1, 3)


def _conv_unblocks(f_conv):
    return f_conv.transpose(1, 0, 2).reshape(3, D_FF)


SMALL_LAYOUT = (("a_norm", (1, D_MODEL)), ("a_w_gate_up", (GATE_RANK, KEY_DIM)), ("a_b_gate", (1, KEY_DIM)), ("a_gn", (1, VALUE_DIM)),
                ("b_norm", (1, D_MODEL)), ("b_conv", (3, D_MODEL)), ("f_norm0", (1, D_MODEL)), ("f_norm1", (1, D_MODEL)),
                ("f_conv0", (3, D_FF)), ("f_conv1", (3, D_FF)), ("final_norm", (1, D_MODEL)))


def _pack_small_grads(g):
    full = dict(g)
    full["a_w_gate_up"] = g["a_w_gate_up"][:GATE_RANK]
    for layer in range(2):
        full[f"f_norm{layer}"] = g["f_norm"][layer]
        full[f"f_conv{layer}"] = _conv_unblocks(g["f_conv"][layer])
    return _pack_rows([full[name] for name, _ in SMALL_LAYOUT])


def _unpack_small_grads(packed):
    pieces = _unpack_rows(packed, [shape for _, shape in SMALL_LAYOUT])
    out = {name: piece.reshape(shape) for (name, shape), piece in zip(SMALL_LAYOUT, pieces)}
    out["f_norm"] = jnp.stack([out["f_norm0"][0], out["f_norm1"][0]])
    out["f_conv"] = jnp.stack([out["f_conv0"], out["f_conv1"]])
    return out


def kernel(x, a_norm, a_w_in, a_w_gate_up, a_b_gate, a_gn, a_w_out, b_norm, b_w_in, b_conv, b_w_out, f_norm, f_w_up, f_conv, f_w_down, final_norm, loss_target, m_a_norm, m_a_w_in, m_a_w_gate_up, m_a_b_gate, m_a_gn, m_a_w_out, m_b_norm, m_b_w_in, m_b_conv, m_b_w_out, m_f_norm, m_f_w_up, m_f_conv, m_f_w_down, m_final_norm, v_a_norm, v_a_w_in, v_a_w_gate_up, v_a_b_gate, v_a_gn, v_a_w_out, v_b_norm, v_b_w_in, v_b_conv, v_b_w_out, v_f_norm, v_f_w_up, v_f_conv, v_f_w_down, v_final_norm):
    my_slot = _slot(*_position())

    first = _all_gather("weight_gather", [a_w_in[0].astype(BF16), a_w_out[0].astype(BF16), _pack_rows([a_w_gate_up[0], b_norm, b_conv[0], f_conv])])
    gathers, small_shards = {}, first[2]
    later = (("f0", f_w_up[0], f_w_down[0]), ("b", b_w_in[0], b_w_out[0]), ("f1", f_w_up[1], f_w_down[1]))
    for collective_id, (group, w_in, w_out) in enumerate(later):
        w_in, w_out, small_shards = lax.optimization_barrier((w_in.astype(BF16), w_out.astype(BF16), small_shards))
        gathers[group] = _sequencer_gather(f"gather_{group}", collective_id, [w_in, w_out])
    gate_full, b_norm_full, b_conv_full, f_conv_full = _unpack_small_shards(small_shards)
    a_w_in_full = jnp.concatenate([first[0][d] for d in range(N_DEV)] + [jnp.zeros((D_MODEL, PROJ_A_PAD - PROJ_A), BF16)], axis=1)
    weights = dict(
        a_norm=a_norm, a_w_gate_up=jnp.pad(gate_full, ((0, GATE_PAD - GATE_RANK), (0, 0))).astype(BF16), a_b_gate=a_b_gate,
        a_gn=a_gn, b_norm=b_norm_full, b_conv=b_conv_full, f_norm=f_norm, f_conv=_conv_blocks(f_conv_full),
        final_norm=final_norm.reshape(1, D_MODEL))

    def fetch(group, after):
        if group == "a":
            return a_w_in_full, first[1].reshape(D_MODEL, D_MODEL)
        w_in, w_out = gathers[group]
        if group == "b":
            return w_in, w_out.reshape(D_MODEL, D_MODEL)
        return w_in, w_out.reshape(FF_BLOCKS, FF_BLOCK, D_MODEL)

    exchanges = {}

    def owner_blocks(d_out):
        return d_out.reshape((N_DEV, -1, D_MODEL))

    exchange_ids = dict(f1=3, b=4, f0=5, a=6)

    def emit(group, grads, dx):
        d_in, d_out = grads
        if group == "a":
            d_in = jnp.stack([d_in[:, d * A_SHARD:(d + 1) * A_SHARD] for d in range(N_DEV)])
        dx, d_in, d_out = lax.optimization_barrier((dx, d_in, owner_blocks(d_out)))
        after = list(exchanges.values())[-1][:1] if exchanges else ()
        exchanges[group] = _sequencer_exchange(f"grads_{group}", exchange_ids[group], [d_in, d_out], [False, False], after)
        return dx

    loss, dx, g = _local_step(x[0], loss_target[0], weights, fetch, emit)
    loss = lax.psum(loss, MESH_AXES)
    small_landed = _all_gather("small_grad_gather", [_pack_small_grads(g)])[0]

    landed_b, landed_f0, landed_f1, landed_a = (exchanges[group] for group in ("b", "f0", "f1", "a"))
    big = dict(
        b_w_in=_adamw_sum("adam_b_w_in", [landed_b[0]], b_w_in, m_b_w_in, v_b_w_in),
        b_w_out=_adamw_sum("adam_b_w_out", [landed_b[1]], b_w_out, m_b_w_out, v_b_w_out),
        f_w_up=_adamw_sum("adam_f_w_up", [landed_f0[0], landed_f1[0]], f_w_up, m_f_w_up, v_f_w_up),
        f_w_down=_adamw_sum("adam_f_w_down", [landed_f0[1], landed_f1[1]], f_w_down, m_f_w_down, v_f_w_down),
        a_w_in=_adamw_sum("adam_a_w_in", [landed_a[0]], a_w_in, m_a_w_in, v_a_w_in),
        a_w_out=_adamw_sum("adam_a_w_out", [landed_a[1]], a_w_out, m_a_w_out, v_a_w_out))
    small_g = _unpack_small_grads(_sum_small(small_landed))
    small_g["a_w_gate_up"] = lax.dynamic_slice_in_dim(small_g["a_w_gate_up"], my_slot * GATE_SHARD, GATE_SHARD, axis=1)
    small_g["b_norm"] = lax.dynamic_slice_in_dim(small_g["b_norm"], my_slot * NORM_SHARD, NORM_SHARD, axis=1)
    small_g["b_conv"] = lax.dynamic_slice_in_dim(small_g["b_conv"], my_slot * NORM_SHARD, NORM_SHARD, axis=1)
    small_g["f_conv"] = lax.dynamic_slice_in_dim(small_g["f_conv"], my_slot * F_CONV_SHARD, F_CONV_SHARD, axis=2)
    small_w = dict(
        a_norm=(a_norm, m_a_norm, v_a_norm), a_w_gate_up=(a_w_gate_up, m_a_w_gate_up, v_a_w_gate_up),
        a_b_gate=(a_b_gate, m_a_b_gate, v_a_b_gate), a_gn=(a_gn, m_a_gn, v_a_gn), b_norm=(b_norm, m_b_norm, v_b_norm),
        b_conv=(b_conv, m_b_conv, v_b_conv), f_norm=(f_norm, m_f_norm, v_f_norm), f_conv=(f_conv, m_f_conv, v_f_conv),
        final_norm=(final_norm, m_final_norm, v_final_norm))
    small = {}
    for name, (w, m, v) in small_w.items():
        flat = (w.shape[-1],) if w.ndim == 1 else w.shape[-2:]
        two_d = (-1, flat[-1])
        grad = small_g[name].reshape(w.shape)
        delta, new_m, new_v = _adamw_small(
            "adam_" + name, grad.reshape(two_d), w.reshape(two_d), m.reshape(two_d), v.reshape(two_d))
        small[name] = (grad, delta.reshape(w.shape), new_m.reshape(w.shape), new_v.reshape(w.shape))

    order = ["a_norm", "a_w_in", "a_w_gate_up", "a_b_gate", "a_gn", "a_w_out", "b_norm", "b_w_in", "b_conv", "b_w_out",
             "f_norm", "f_w_up", "f_conv", "f_w_down", "final_norm"]
    results = {**big, **small}
    outputs = [loss, dx.reshape(1, SEQ, D_MODEL)]
    for kind in range(4):
        outputs += [results[name][kind] for name in order]
    return tuple(outputs)
```

```python
import jax
import jax.numpy as jnp
from jax import lax
from jax.experimental import pallas as pl
from jax.experimental.pallas import tpu as pltpu
from jax.experimental.pallas import tpu_sc as plsc

F32 = jnp.float32
BF16 = jnp.bfloat16

N_DEV = 8
SEQ = 2048
D_MODEL = 1024
CHUNK = 64
N_CHUNKS = SEQ // CHUNK
RMS_EPS = 1e-6
GLA_HEADS = 4
KEY_DIM = 512
VALUE_DIM = 1024
HEAD_K = KEY_DIM // GLA_HEADS
HEAD_V = VALUE_DIM // GLA_HEADS
GATE_RANK = 16
GATE_PAD = 128
GATE_NORMALIZER = 16.0
PROJ_A = 2 * KEY_DIM + 2 * VALUE_DIM + GATE_RANK
PROJ_A_PAD = 2 * KEY_DIM + 2 * VALUE_DIM + GATE_PAD
A_SHARD = PROJ_A // N_DEV
B_SHARD = 3 * D_MODEL // N_DEV
D_FF = 2816
FF_BLOCK = 2 * D_FF // N_DEV
FF_BLOCKS = D_FF // FF_BLOCK
ADAM_LR = 0.001
ADAM_B1 = 0.9
ADAM_B2 = 0.999
ADAM_EPS = 1e-08
ADAM_WD = 0.01
ADAM_STEP = 10
MESH_AXES = ("x", "y", "c")

VMEM_LIMIT = 56 * 1024 * 1024
ROW_CHUNK = 256
HALO = 16


def _params(sem=None, vmem=VMEM_LIMIT):
    return pltpu.CompilerParams(dimension_semantics=sem, vmem_limit_bytes=vmem)


NN = ((1,), (0,))
NT = ((1,), (1,))
TN = ((0,), (0,))


def _matmul(name, a, a_spec, b, b_spec, dims, grid, out_shape, out_spec, k_blocks=None, a_block_cols=None, res=None,
            res_spec=None, transpose_out=False):
    has_res = res is not None

    def body(*refs):
        a_ref, b_ref = refs[0], refs[1]
        r_ref = refs[2] if has_res else None
        o_ref = refs[2 + has_res]

        def product(lhs, rhs):
            return lax.dot_general(lhs.astype(BF16), rhs, (dims, ((), ())), preferred_element_type=F32)

        if k_blocks is None:
            v = product(a_ref[...], b_ref[...])
        else:
            v = None
            for k in range(k_blocks):
                lhs = a_ref[k] if a_block_cols is None else a_ref[:, k * a_block_cols:(k + 1) * a_block_cols]
                p = product(lhs, b_ref[k])
                v = p if v is None else v + p
        if transpose_out:
            v = v.T
        if has_res:
            v = v + r_ref[...]
        o_ref[...] = v.astype(o_ref.dtype)

    operands = [a, b] + ([res] if has_res else [])
    in_specs = [a_spec, b_spec] + ([res_spec] if has_res else [])
    return pl.pallas_call(
        body, name=name, grid=grid, in_specs=in_specs, out_specs=out_spec, out_shape=out_shape,
        compiler_params=_params(("parallel",) * len(grid)),
    )(*operands)


TM = 512
N_TM = SEQ // TM
PA_TILE = 640
N_PA = PROJ_A_PAD // PA_TILE
OUT_TILE = 256


def _spec(shape, fn):
    return pl.BlockSpec(shape, fn)


def _act(shape=(SEQ, D_MODEL), dtype=BF16):
    return jax.ShapeDtypeStruct(shape, dtype)


def _proj_rows_nt(name, h, wt, n_tile):
    n = wt.shape[0]
    return _matmul(name, h, _spec((SEQ, D_MODEL), lambda j: (0, 0)), wt, _spec((n_tile, D_MODEL), lambda j: (j, 0)), NT,
                   (n // n_tile,), _act((SEQ, n)), _spec((SEQ, n_tile), lambda j: (0, j)))


def _proj_blocks_nt(name, h, wt_blocks):
    nb, n, _ = wt_blocks.shape
    return _matmul(name, h, _spec((SEQ, D_MODEL), lambda j: (0, 0)), wt_blocks, _spec((None, n, D_MODEL), lambda j: (j, 0, 0)),
                   NT, (nb,), _act((nb, SEQ, n)), _spec((None, SEQ, n), lambda j: (j, 0, 0)))


def _proj_cols_nn(name, h, w_blocks):
    nb, _, n = w_blocks.shape
    return _matmul(name, h, _spec((SEQ, D_MODEL), lambda j: (0, 0)), w_blocks, _spec((None, D_MODEL, n), lambda j: (j, 0, 0)),
                   NN, (nb,), _act((SEQ, nb * n)), _spec((SEQ, n), lambda j: (0, j)))


def _square(name, a, w, dims, x=None):
    row = _spec((TM, D_MODEL), lambda i: (i, 0))
    return _matmul(name, a, row, w, _spec((D_MODEL, D_MODEL), lambda i: (0, 0)), dims, (N_TM,),
                   _act(dtype=F32 if x is not None else BF16), row, res=x, res_spec=row if x is not None else None)


def _sum_blocks_nn(name, a_blocks, w_blocks, x=None):
    nb, _, n = a_blocks.shape
    row = _spec((TM, D_MODEL), lambda i: (i, 0))
    return _matmul(name, a_blocks, _spec((nb, TM, n), lambda i: (0, i, 0)), w_blocks, _spec((nb, n, D_MODEL), lambda i: (0, 0, 0)),
                   NN, (N_TM,), _act(dtype=F32), row, k_blocks=nb, res=x, res_spec=row if x is not None else None)


def _sum_cols_nt(name, d, w_blocks):
    nb, _, n = w_blocks.shape
    return _matmul(name, d, _spec((TM, nb * n), lambda i: (i, 0)), w_blocks, _spec((nb, D_MODEL, n), lambda i: (0, 0, 0)), NT,
                   (N_TM,), _act(dtype=F32), _spec((TM, D_MODEL), lambda i: (i, 0)), k_blocks=nb, a_block_cols=n)


def _wide_nn(name, d, wt):
    n = wt.shape[0]
    return _matmul(name, d, _spec((TM, n), lambda i: (i, 0)), wt, _spec((n, D_MODEL), lambda i: (0, 0)), NN, (N_TM,),
                   _act(dtype=F32), _spec((TM, D_MODEL), lambda i: (i, 0)))


def _wgrad_blocks_tn(name, d_blocks, h):
    nb, _, n = d_blocks.shape
    return _matmul(name, d_blocks, _spec((None, SEQ, n), lambda j: (j, 0, 0)), h, _spec((SEQ, D_MODEL), lambda j: (0, 0)), TN,
                   (nb,), _act((nb, n, D_MODEL)), _spec((None, n, D_MODEL), lambda j: (j, 0, 0)))


def _wgrad_cols_tn(name, d, n_tile, h):
    n = d.shape[1]
    return _matmul(name, d, _spec((SEQ, n_tile), lambda j: (0, j)), h, _spec((SEQ, D_MODEL), lambda j: (0, 0)), TN,
                   (n // n_tile,), _act((n, D_MODEL)), _spec((n_tile, D_MODEL), lambda j: (j, 0)))


def _wgrad_cols_transposed_tn(name, h, d, n_tile):
    nb = d.shape[1] // n_tile
    return _matmul(name, d, _spec((SEQ, n_tile), lambda j: (0, j)), h, _spec((SEQ, D_MODEL), lambda j: (0, 0)), TN, (nb,),
                   _act((nb, D_MODEL, n_tile)), _spec((None, D_MODEL, n_tile), lambda j: (j, 0, 0)), transpose_out=True)


NORM_ROWS = 512


def _rstd(x):
    return lax.rsqrt(jnp.mean(x * x, axis=-1, keepdims=True) + RMS_EPS)


def _norm_fwd(name, x, gamma):
    def body(x_ref, g_ref, h_ref):
        x = x_ref[...]
        h_ref[...] = (x * _rstd(x) * g_ref[...]).astype(BF16)

    row = _spec((NORM_ROWS, D_MODEL), lambda i: (i, 0))
    return pl.pallas_call(
        body, name=name, grid=(SEQ // NORM_ROWS,), in_specs=[row, _spec((1, D_MODEL), lambda i: (0, 0))], out_specs=row,
        out_shape=jax.ShapeDtypeStruct((SEQ, D_MODEL), BF16), compiler_params=_params(("parallel",)),
    )(x, gamma)


def _norm_bwd_rows(x, gamma, dh):
    r = _rstd(x)
    xh = x * r
    dxh = dh * gamma
    dx = r * (dxh - xh * jnp.mean(dxh * xh, axis=-1, keepdims=True))
    return dx, jnp.sum(dh * xh, axis=0, keepdims=True)


def _norm_bwd(name, x, gamma, dh, dx_in):
    def body(x_ref, g_ref, dh_ref, dxi_ref, dx_ref, dx16_ref, dg_ref):
        dx, dg = _norm_bwd_rows(x_ref[...], g_ref[...], dh_ref[...].astype(F32))
        dx = dxi_ref[...] + dx
        dx_ref[...] = dx
        dx16_ref[...] = dx.astype(BF16)

        @pl.when(pl.program_id(0) == 0)
        def _():
            dg_ref[...] = dg

        @pl.when(pl.program_id(0) > 0)
        def _():
            dg_ref[...] += dg

    row = _spec((NORM_ROWS, D_MODEL), lambda i: (i, 0))
    vec = _spec((1, D_MODEL), lambda i: (0, 0))
    return pl.pallas_call(
        body, name=name, grid=(SEQ // NORM_ROWS,), in_specs=[row, vec, row, row], out_specs=[row, row, vec],
        out_shape=[_act(dtype=F32), _act(), jax.ShapeDtypeStruct((1, D_MODEL), F32)],
        compiler_params=_params(("arbitrary",)),
    )(x, gamma, dh, dx_in)


def _loss_head(x, gamma, target):
    def body(x_ref, g_ref, t_ref, loss_ref, dx_ref, dx16_ref, dg_ref):
        x = x_ref[...]
        gamma = g_ref[...]
        err = x * _rstd(x) * gamma - t_ref[...]
        dy = err * (1.0 / D_MODEL)
        dx, dg = _norm_bwd_rows(x, gamma, dy)
        dx_ref[...] = dx
        dx16_ref[...] = dx.astype(BF16)
        part = 0.5 * jnp.sum(jnp.sum(err * err, axis=-1, keepdims=True) * (1.0 / D_MODEL), axis=0, keepdims=True)
        part = jnp.broadcast_to(part, loss_ref.shape)

        @pl.when(pl.program_id(0) == 0)
        def _():
            dg_ref[...] = dg
            loss_ref[...] = part

        @pl.when(pl.program_id(0) > 0)
        def _():
            dg_ref[...] += dg
            loss_ref[...] += part

    row = _spec((NORM_ROWS, D_MODEL), lambda i: (i, 0))
    vec = _spec((1, D_MODEL), lambda i: (0, 0))
    return pl.pallas_call(
        body, name="loss_head", grid=(SEQ // NORM_ROWS,), in_specs=[row, vec, row],
        out_specs=[_spec((1, 128), lambda i: (0, 0)), row, row, vec],
        out_shape=[jax.ShapeDtypeStruct((1, 128), F32), _act(dtype=F32), _act(), jax.ShapeDtypeStruct((1, D_MODEL), F32)],
        compiler_params=_params(("arbitrary",)),
    )(x, gamma, target)


def _sigmoid(x):
    return 1.0 / (1.0 + jnp.exp(-x))


def _rows(ref, c):
    return ref[pl.ds(pl.multiple_of(c * ROW_CHUNK, ROW_CHUNK), ROW_CHUNK), :].astype(F32)


def _rows_before(ref, c):
    start = pl.multiple_of(jnp.maximum(c * ROW_CHUNK - HALO, 0), HALO)
    rows = ref[pl.ds(start, HALO), :].astype(F32)
    return jnp.where(c > 0, rows, 0.0)


def _rows_after(ref, c, n_chunks):
    start = pl.multiple_of(jnp.minimum((c + 1) * ROW_CHUNK, SEQ - HALO), HALO)
    rows = ref[pl.ds(start, HALO), :].astype(F32)
    return jnp.where(c < n_chunks - 1, rows, 0.0)


def _shift_down(z, before, n):
    row = lax.broadcasted_iota(jnp.int32, z.shape, 0)
    out = pltpu.roll(z, n, 0)
    for r in range(n):
        out = jnp.where(row == r, before[HALO - n + r:HALO - n + r + 1, :], out)
    return out


def _shift_up(z, after, n):
    rows = z.shape[0]
    row = lax.broadcasted_iota(jnp.int32, z.shape, 0)
    out = pltpu.roll(z, rows - n, 0)
    for r in range(n):
        out = jnp.where(row == rows - n + r, after[r:r + 1, :], out)
    return out


def _conv_rows(z, before, w):
    z1 = _shift_down(z, before, 1)
    z2 = _shift_down(z, before, 2)
    return w[2:3, :] * z + w[1:2, :] * z1 + w[0:1, :] * z2, z1, z2


def _conv_t_rows(dy, after, w):
    return w[2:3, :] * dy + w[1:2, :] * _shift_up(dy, after, 1) + w[0:1, :] * _shift_up(dy, after, 2)


N_ROW_CHUNKS = SEQ // ROW_CHUNK


def _ffn_mid_fwd(name, gu, conv_w):
    def body(gu_ref, w_ref, a_ref):
        w = w_ref[...]

        def chunk(c, carry):
            g = _rows(gu_ref.at[0], c)
            u = _rows(gu_ref.at[1], c)
            gc, _, _ = _conv_rows(g, _rows_before(gu_ref.at[0], c), w)
            a_ref[pl.ds(pl.multiple_of(c * ROW_CHUNK, ROW_CHUNK), ROW_CHUNK), :] = (gc * _sigmoid(gc) * u).astype(BF16)
            return carry

        lax.fori_loop(0, N_ROW_CHUNKS, chunk, 0)

    return pl.pallas_call(
        body, name=name, grid=(FF_BLOCKS,),
        in_specs=[_spec((2, None, SEQ, FF_BLOCK), lambda j: (0, j, 0, 0)), _spec((None, 3, FF_BLOCK), lambda j: (j, 0, 0))],
        out_specs=_spec((None, SEQ, FF_BLOCK), lambda j: (j, 0, 0)),
        out_shape=jax.ShapeDtypeStruct((FF_BLOCKS, SEQ, FF_BLOCK), BF16), compiler_params=_params(("parallel",)),
    )(gu, conv_w)


def _ffn_mid_bwd(name, gu, conv_w, da):
    def body(gu_ref, w_ref, da_ref, dgu_ref, dw_ref, dgc_ref):
        w = w_ref[...]

        def first(c, acc):
            g = _rows(gu_ref.at[0], c)
            u = _rows(gu_ref.at[1], c)
            d = _rows(da_ref, c)
            gc, g1, g2 = _conv_rows(g, _rows_before(gu_ref.at[0], c), w)
            sg = _sigmoid(gc)
            rows = pl.ds(pl.multiple_of(c * ROW_CHUNK, ROW_CHUNK), ROW_CHUNK)
            dgu_ref[1, rows, :] = (d * gc * sg).astype(BF16)
            dgc = d * u * (sg * (1.0 + gc * (1.0 - sg)))
            dgc_ref[rows, :] = dgc
            return (acc[0] + jnp.sum(dgc * g2, axis=0, keepdims=True), acc[1] + jnp.sum(dgc * g1, axis=0, keepdims=True),
                    acc[2] + jnp.sum(dgc * g, axis=0, keepdims=True))

        zero = jnp.zeros((1, FF_BLOCK), F32)
        acc = lax.fori_loop(0, N_ROW_CHUNKS, first, (zero, zero, zero))
        for r in range(3):
            dw_ref[r:r + 1, :] = acc[r]

        def second(c, carry):
            dgc = _rows(dgc_ref, c)
            dg = _conv_t_rows(dgc, _rows_after(dgc_ref, c, N_ROW_CHUNKS), w)
            dgu_ref[0, pl.ds(pl.multiple_of(c * ROW_CHUNK, ROW_CHUNK), ROW_CHUNK), :] = dg.astype(BF16)
            return carry

        lax.fori_loop(0, N_ROW_CHUNKS, second, 0)

    pair = _spec((2, None, SEQ, FF_BLOCK), lambda j: (0, j, 0, 0))
    wspec = _spec((None, 3, FF_BLOCK), lambda j: (j, 0, 0))
    return pl.pallas_call(
        body, name=name, grid=(FF_BLOCKS,),
        in_specs=[pair, wspec, _spec((None, SEQ, FF_BLOCK), lambda j: (j, 0, 0))], out_specs=[pair, wspec],
        out_shape=[jax.ShapeDtypeStruct((2, FF_BLOCKS, SEQ, FF_BLOCK), BF16), jax.ShapeDtypeStruct((FF_BLOCKS, 3, FF_BLOCK), F32)],
        scratch_shapes=[pltpu.VMEM((SEQ, FF_BLOCK), F32)], compiler_params=_params(("parallel",)),
    )(gu, conv_w, da)


SC_COLS = 256
N_SC = D_MODEL // SC_COLS


def _sc_specs():
    return [_spec((SEQ, SC_COLS), lambda j, part=part: (0, part * N_SC + j)) for part in range(3)]


def _sc_mid_fwd(p, conv_w):
    def body(b_ref, c_ref, h_ref, w_ref, y_ref):
        w = w_ref[...]

        def chunk(c, carry):
            z = _rows(c_ref, c) * _rows(h_ref, c)
            before = _rows_before(c_ref, c) * _rows_before(h_ref, c)
            zc, _, _ = _conv_rows(z, before, w)
            y_ref[pl.ds(pl.multiple_of(c * ROW_CHUNK, ROW_CHUNK), ROW_CHUNK), :] = (_rows(b_ref, c) * zc).astype(BF16)
            return carry

        lax.fori_loop(0, N_ROW_CHUNKS, chunk, 0)

    col = _spec((SEQ, SC_COLS), lambda j: (0, j))
    return pl.pallas_call(
        body, name="sc_mid_fwd", grid=(N_SC,), in_specs=_sc_specs() + [_spec((3, SC_COLS), lambda j: (0, j))], out_specs=col,
        out_shape=jax.ShapeDtypeStruct((SEQ, D_MODEL), BF16), compiler_params=_params(("parallel",)),
    )(p, p, p, conv_w)


def _sc_mid_bwd(p, conv_w, dy):
    def body(b_ref, c_ref, h_ref, w_ref, dy_ref, db_ref, dc_ref, dh_ref, dw_ref, dzc_ref):
        w = w_ref[...]

        def first(c, acc):
            z = _rows(c_ref, c) * _rows(h_ref, c)
            before = _rows_before(c_ref, c) * _rows_before(h_ref, c)
            zc, z1, z2 = _conv_rows(z, before, w)
            d = _rows(dy_ref, c)
            rows = pl.ds(pl.multiple_of(c * ROW_CHUNK, ROW_CHUNK), ROW_CHUNK)
            db_ref[rows, :] = (d * zc).astype(BF16)
            dzc = d * _rows(b_ref, c)
            dzc_ref[rows, :] = dzc
            return (acc[0] + jnp.sum(dzc * z2, axis=0, keepdims=True), acc[1] + jnp.sum(dzc * z1, axis=0, keepdims=True),
                    acc[2] + jnp.sum(dzc * z, axis=0, keepdims=True))

        zero = jnp.zeros((1, SC_COLS), F32)
        acc = lax.fori_loop(0, N_ROW_CHUNKS, first, (zero, zero, zero))
        for r in range(3):
            dw_ref[r:r + 1, :] = acc[r]

        def second(c, carry):
            dz = _conv_t_rows(_rows(dzc_ref, c), _rows_after(dzc_ref, c, N_ROW_CHUNKS), w)
            rows = pl.ds(pl.multiple_of(c * ROW_CHUNK, ROW_CHUNK), ROW_CHUNK)
            dc_ref[rows, :] = (dz * _rows(h_ref, c)).astype(BF16)
            dh_ref[rows, :] = (dz * _rows(c_ref, c)).astype(BF16)
            return carry

        lax.fori_loop(0, N_ROW_CHUNKS, second, 0)

    col = _spec((SEQ, SC_COLS), lambda j: (0, j))
    wspec = _spec((3, SC_COLS), lambda j: (0, j))
    act = jax.ShapeDtypeStruct((SEQ, D_MODEL), BF16)
    return pl.pallas_call(
        body, name="sc_mid_bwd", grid=(N_SC,), in_specs=_sc_specs() + [wspec, col], out_specs=[col, col, col, wspec],
        out_shape=[act, act, act, jax.ShapeDtypeStruct((3, D_MODEL), F32)],
        scratch_shapes=[pltpu.VMEM((SEQ, SC_COLS), F32)], compiler_params=_params(("parallel",)),
    )(p, p, p, conv_w, dy)


GLA_GROUP = 4
GLA_ROWS = GLA_GROUP * CHUNK
N_GROUPS = N_CHUNKS // GLA_GROUP
Q0, K0, V0, R0, G0 = 0, KEY_DIM, 2 * KEY_DIM, 2 * KEY_DIM + VALUE_DIM, 2 * KEY_DIM + 2 * VALUE_DIM


def _tri(strict):
    r = lax.broadcasted_iota(jnp.int32, (CHUNK, CHUNK), 0)
    c = lax.broadcasted_iota(jnp.int32, (CHUNK, CHUNK), 1)
    return jnp.where(c < r if strict else c <= r, 1.0, 0.0).astype(F32)


def _cumsum_rows(tri, x):
    return jnp.dot(tri, x, preferred_element_type=F32, precision=lax.Precision.HIGHEST)


def _gate_logits(gl, wgu, b_gate):
    return jnp.dot(gl, wgu, preferred_element_type=F32) + b_gate


def _log_decay(logits):
    return (jnp.minimum(logits, 0.0) - jnp.log(1.0 + jnp.exp(-jnp.abs(logits)))) * (1.0 / GATE_NORMALIZER)


def _head(x, h, width):
    return x[:, h * width:(h + 1) * width]


def _gla_fwd(proj, wgu, b_gate, gn):
    def body(p_ref, wgu_ref, b_ref, gn_ref, o_ref, og_ref, st_ref, state):
        @pl.when(pl.program_id(0) == 0)
        def _():
            state[...] = jnp.zeros_like(state)

        tri = _tri(False)
        la = _log_decay(_gate_logits(p_ref[:, G0:G0 + GATE_PAD], wgu_ref[...], b_ref[...]))
        for c in range(GLA_GROUP):
            rows = slice(c * CHUNK, (c + 1) * CHUNK)
            cum = _cumsum_rows(tri, la[rows])
            tot = cum[CHUNK - 1:CHUNK, :]
            kd = (p_ref[rows, K0:K0 + KEY_DIM].astype(F32) * jnp.exp(tot - cum)).astype(BF16)
            decay = jnp.exp(tot)
            q = (p_ref[rows, Q0:Q0 + KEY_DIM].astype(F32) * (HEAD_K ** -0.5)).astype(BF16)
            v = p_ref[rows, V0:V0 + VALUE_DIM]
            for h in range(GLA_HEADS):
                upd = lax.dot_general(_head(v, h, HEAD_V), _head(kd, h, HEAD_K), (TN, ((), ())), preferred_element_type=F32)
                s = state[h] * _head(decay, h, HEAD_K) + upd
                state[h] = s
                st_ref[c, h] = s
                o_ref[rows, h * HEAD_V:(h + 1) * HEAD_V] = lax.dot_general(
                    _head(q, h, HEAD_K), s.astype(BF16), (NT, ((), ())), preferred_element_type=F32)
        r = p_ref[:, R0:R0 + VALUE_DIM].astype(F32)
        gate = r * _sigmoid(r) * gn_ref[...]
        for h in range(GLA_HEADS):
            cols = slice(h * HEAD_V, (h + 1) * HEAD_V)
            o = o_ref[:, cols]
            og_ref[:, cols] = (o * _rstd(o) * gate[:, cols]).astype(BF16)

    rows = _spec((GLA_ROWS, VALUE_DIM), lambda i: (i, 0))
    const = lambda shape: _spec(shape, lambda i: (0,) * len(shape))
    return pl.pallas_call(
        body, name="gla_fwd", grid=(N_GROUPS,),
        in_specs=[_spec((GLA_ROWS, PROJ_A_PAD), lambda i: (i, 0)), const((GATE_PAD, KEY_DIM)), const((1, KEY_DIM)),
                  const((1, VALUE_DIM))],
        out_specs=[rows, rows, _spec((GLA_GROUP, GLA_HEADS, HEAD_V, HEAD_K), lambda i: (i, 0, 0, 0))],
        out_shape=[jax.ShapeDtypeStruct((SEQ, VALUE_DIM), F32), jax.ShapeDtypeStruct((SEQ, VALUE_DIM), BF16),
                   jax.ShapeDtypeStruct((N_CHUNKS, GLA_HEADS, HEAD_V, HEAD_K), F32)],
        scratch_shapes=[pltpu.VMEM((GLA_HEADS, HEAD_V, HEAD_K), F32)], compiler_params=_params(("arbitrary",)),
    )(proj, wgu, b_gate, gn)


def _gla_bwd(proj, wgu, b_gate, gn, o, states, dog):
    last = N_GROUPS - 1

    def body(p_ref, wgu_ref, b_ref, gn_ref, o_ref, st_ref, stp_ref, dog_ref, dp_ref, dwgu_ref, db_ref, dgn_ref, carry, do_buf):
        step = pl.program_id(0)

        @pl.when(step == 0)
        def _():
            carry[...] = jnp.zeros_like(carry)

        r = p_ref[:, R0:R0 + VALUE_DIM].astype(F32)
        sr = _sigmoid(r)
        silu = r * sr
        gn_row = gn_ref[...]
        dog_rows = dog_ref[...].astype(F32)
        dn = dog_rows * silu
        dgn_cols = []
        for h in range(GLA_HEADS):
            cols = slice(h * HEAD_V, (h + 1) * HEAD_V)
            oh = o_ref[:, cols]
            rs = _rstd(oh)
            ohat = oh * rs
            dn_h = dn[:, cols]
            dgn_cols.append(jnp.sum(dn_h * ohat, axis=0, keepdims=True))
            dohat = dn_h * gn_row[:, cols]
            do_buf[:, cols] = rs * (dohat - ohat * jnp.mean(dohat * ohat, axis=-1, keepdims=True))
            n_h = ohat * gn_row[:, cols]
            dp_ref[:, R0 + h * HEAD_V:R0 + (h + 1) * HEAD_V] = (
                dog_rows[:, cols] * n_h * (sr[:, cols] * (1.0 + r[:, cols] * (1.0 - sr[:, cols])))).astype(BF16)
        dgn = jnp.concatenate(dgn_cols, axis=1)

        tri = _tri(False)
        tri_strict = _tri(True)
        gl = p_ref[:, G0:G0 + GATE_PAD]
        logits = _gate_logits(gl, wgu_ref[...], b_ref[...])
        la = _log_decay(logits)
        dlogit_rows = []
        for c in reversed(range(GLA_GROUP)):
            rows = slice(c * CHUNK, (c + 1) * CHUNK)
            cum = _cumsum_rows(tri, la[rows])
            tot = cum[CHUNK - 1:CHUNK, :]
            fade = jnp.exp(tot - cum)
            k = p_ref[rows, K0:K0 + KEY_DIM].astype(F32)
            kd32 = k * fade
            kd = kd32.astype(BF16)
            decay = jnp.exp(tot)
            q = (p_ref[rows, Q0:Q0 + KEY_DIM].astype(F32) * (HEAD_K ** -0.5)).astype(BF16)
            v = p_ref[rows, V0:V0 + VALUE_DIM]
            do = do_buf[rows, :].astype(BF16)
            dkd_cols, ddecay_cols = [], []
            for h in range(GLA_HEADS):
                do_h = _head(do, h, HEAD_V)
                s = st_ref[c, h]
                dq = jnp.dot(do_h, s.astype(BF16), preferred_element_type=F32) * (HEAD_K ** -0.5)
                dp_ref[rows, Q0 + h * HEAD_K:Q0 + (h + 1) * HEAD_K] = dq.astype(BF16)
                g = carry[h] + lax.dot_general(do_h, _head(q, h, HEAD_K), (TN, ((), ())), preferred_element_type=F32)
                g16 = g.astype(BF16)
                dkd_cols.append(jnp.dot(_head(v, h, HEAD_V), g16, preferred_element_type=F32))
                dv = lax.dot_general(_head(kd, h, HEAD_K), g16, (NT, ((), ())), preferred_element_type=F32)
                dp_ref[rows, V0 + h * HEAD_V:V0 + (h + 1) * HEAD_V] = dv.astype(BF16)
                if c > 0:
                    s_prev = st_ref[c - 1, h]
                else:
                    s_prev = jnp.where(step < last, stp_ref[0, h], 0.0)
                ddecay_cols.append(jnp.sum(g * s_prev, axis=0, keepdims=True))
                carry[h] = g * _head(decay, h, HEAD_K)
            dkd = jnp.concatenate(dkd_cols, axis=1)
            ddecay = jnp.concatenate(ddecay_cols, axis=1)
            dp_ref[rows, K0:K0 + KEY_DIM] = (dkd * fade).astype(BF16)
            e = dkd * kd32
            dla = ddecay * decay + _cumsum_rows(tri_strict, e)
            dlogit_rows.append(dla * (1.0 / GATE_NORMALIZER) * (1.0 - _sigmoid(logits[rows])))
        dlogit = jnp.concatenate(dlogit_rows[::-1], axis=0)
        dlogit16 = dlogit.astype(BF16)
        dp_ref[:, G0:G0 + GATE_PAD] = lax.dot_general(
            dlogit16, wgu_ref[...], (NT, ((), ())), preferred_element_type=F32).astype(BF16)
        dwgu = lax.dot_general(gl, dlogit16, (TN, ((), ())), preferred_element_type=F32)
        db = jnp.sum(dlogit, axis=0, keepdims=True)

        @pl.when(step == 0)
        def _():
            dwgu_ref[...] = dwgu
            db_ref[...] = db
            dgn_ref[...] = dgn

        @pl.when(step > 0)
        def _():
            dwgu_ref[...] += dwgu
            db_ref[...] += db
            dgn_ref[...] += dgn

    rev = lambda i: (last - i, 0)
    rows = _spec((GLA_ROWS, VALUE_DIM), rev)
    const = lambda shape: _spec(shape, lambda i: (0,) * len(shape))
    st_shape = (GLA_HEADS, HEAD_V, HEAD_K)
    return pl.pallas_call(
        body, name="gla_bwd", grid=(N_GROUPS,),
        in_specs=[_spec((GLA_ROWS, PROJ_A_PAD), rev), const((GATE_PAD, KEY_DIM)), const((1, KEY_DIM)), const((1, VALUE_DIM)),
                  rows, _spec((GLA_GROUP,) + st_shape, lambda i: (last - i, 0, 0, 0)),
                  _spec((1,) + st_shape, lambda i: (jnp.maximum((last - i) * GLA_GROUP - 1, 0), 0, 0, 0)), rows],
        out_specs=[_spec((GLA_ROWS, PROJ_A_PAD), rev), const((GATE_PAD, KEY_DIM)), const((1, KEY_DIM)), const((1, VALUE_DIM))],
        out_shape=[jax.ShapeDtypeStruct((SEQ, PROJ_A_PAD), BF16), jax.ShapeDtypeStruct((GATE_PAD, KEY_DIM), F32),
                   jax.ShapeDtypeStruct((1, KEY_DIM), F32), jax.ShapeDtypeStruct((1, VALUE_DIM), F32)],
        scratch_shapes=[pltpu.VMEM(st_shape, F32), pltpu.VMEM((GLA_ROWS, VALUE_DIM), F32)],
        compiler_params=_params(("arbitrary",)),
    )(proj, wgu, b_gate, gn, o, states, states, dog)


def _ffn_fwd(tag, x, gamma, w_up_t, conv_w, w_down):
    h = _norm_fwd(f"ffn{tag}_norm", x, gamma)
    gu = _proj_blocks_nt(f"ffn{tag}_up", h, w_up_t).reshape(2, FF_BLOCKS, SEQ, FF_BLOCK)
    a = _ffn_mid_fwd(f"ffn{tag}_mid", gu, conv_w)
    return _sum_blocks_nn(f"ffn{tag}_down", a, w_down, x), (h, gu, a)


def _ffn_bwd(tag, x, gamma, w_up_t, conv_w, w_down, saved, dx, dx16):
    h, gu, a = saved
    da = _proj_blocks_nt(f"ffn{tag}_da", dx16, w_down)
    d_w_down = _wgrad_blocks_tn(f"ffn{tag}_dwdown", a, dx16)
    dgu, d_conv = _ffn_mid_bwd(f"ffn{tag}_mid_bwd", gu, conv_w, da)
    dgu = dgu.reshape(2 * FF_BLOCKS, SEQ, FF_BLOCK)
    dh = _sum_blocks_nn(f"ffn{tag}_dh", dgu, w_up_t)
    d_w_up_t = _wgrad_blocks_tn(f"ffn{tag}_dwup", dgu, h)
    dx, dx16, d_gamma = _norm_bwd(f"ffn{tag}_norm_bwd", x, gamma, dh, dx)
    return dx, dx16, d_gamma, d_w_up_t, d_conv, d_w_down


def _local_step(x, target, w, fetch=None, emit=None):
    if fetch is None:
        local = dict(a=(w.get("a_w_in"), w.get("a_w_out")), b=(w.get("b_w_in"), w.get("b_w_out")))
        for layer in range(2):
            local[f"f{layer}"] = (w["f_w_up"][layer], w["f_w_down"][layer]) if "f_w_up" in w else None
        fetch = lambda group, after: local[group]
    if emit is None:
        emit = lambda group, grads, dx: dx
    f_norm = (w["f_norm"][0:1], w["f_norm"][1:2])

    x0 = x
    a_w_in, a_w_out = fetch("a", x0)
    h0 = _norm_fwd("a_norm", x0, w["a_norm"])
    proj = _proj_rows_nt("a_in", h0, a_w_in, PA_TILE)
    o, og, states = _gla_fwd(proj, w["a_w_gate_up"], w["a_b_gate"], w["a_gn"])
    x1 = _square("a_out", og, a_w_out, NN, x0)
    up0, down0 = fetch("f0", x1)
    x2, ffn0 = _ffn_fwd(0, x1, f_norm[0], up0, w["f_conv"][0], down0)
    b_w_in, b_w_out = fetch("b", x2)
    h2 = _norm_fwd("b_norm", x2, w["b_norm"])
    p = _proj_cols_nn("b_in", h2, b_w_in)
    y = _sc_mid_fwd(p, w["b_conv"])
    x3 = _square("b_out", y, b_w_out, NN, x2)
    up1, down1 = fetch("f1", x3)
    x4, ffn1 = _ffn_fwd(1, x3, f_norm[1], up1, w["f_conv"][1], down1)
    loss, dx, dx16, d_final_norm = _loss_head(x4, w["final_norm"], target)

    dx, dx16, d_f_norm1, d_up1, d_fconv1, d_down1 = _ffn_bwd(1, x3, f_norm[1], up1, w["f_conv"][1], down1, ffn1, dx, dx16)
    dx16 = emit("f1", (d_up1, d_down1), dx16)

    dy = _square("b_dy", dx16, b_w_out, NT)
    d_b_w_out = _wgrad_cols_tn("b_dwout", y, OUT_TILE, dx16)
    db, dc, dhh, d_b_conv = _sc_mid_bwd(p, w["b_conv"], dy)
    dp = jnp.concatenate([db, dc, dhh], axis=1)
    dh2 = _sum_cols_nt("b_dh", dp, b_w_in)
    d_b_w_in = _wgrad_cols_transposed_tn("b_dwin", h2, dp, B_SHARD)
    dx, dx16, d_b_norm = _norm_bwd("b_norm_bwd", x2, w["b_norm"], dh2, dx)
    dx16 = emit("b", (d_b_w_in, d_b_w_out), dx16)

    dx, dx16, d_f_norm0, d_up0, d_fconv0, d_down0 = _ffn_bwd(0, x1, f_norm[0], up0, w["f_conv"][0], down0, ffn0, dx, dx16)
    dx16 = emit("f0", (d_up0, d_down0), dx16)

    dog = _square("a_dog", dx16, a_w_out, NT)
    d_a_w_out = _wgrad_cols_tn("a_dwout", og, OUT_TILE, dx16)
    dproj, d_wgu, d_b_gate, d_gn = _gla_bwd(proj, w["a_w_gate_up"], w["a_b_gate"], w["a_gn"], o, states, dog)
    d_a_w_in = _wgrad_cols_tn("a_dwin", dproj, PA_TILE, h0)
    dproj = emit("a", (d_a_w_in, d_a_w_out), dproj)
    dh0 = _wide_nn("a_dh", dproj, a_w_in)
    dx, _, d_a_norm = _norm_bwd("a_norm_bwd", x0, w["a_norm"], dh0, dx)

    grads = dict(
        a_norm=d_a_norm, a_w_in=d_a_w_in, a_w_gate_up=d_wgu, a_b_gate=d_b_gate, a_gn=d_gn, a_w_out=d_a_w_out,
        b_norm=d_b_norm, b_w_in=d_b_w_in, b_conv=d_b_conv, b_w_out=d_b_w_out,
        f_norm=(d_f_norm0, d_f_norm1), f_w_up=(d_up0, d_up1), f_conv=(d_fconv0, d_fconv1), f_w_down=(d_down0, d_down1),
        final_norm=d_final_norm)
    return loss[0, 0], dx, grads


MESH_ID = pl.DeviceIdType.MESH
ANY = pl.BlockSpec(memory_space=pl.ANY)
N_PEERS = N_DEV - 1


def _position():
    return lax.axis_index("x"), lax.axis_index("y"), lax.axis_index("c")


def _slot(px, py, pc):
    return 4 * px + 2 * py + pc


def _all_gather(name, shards):
    n = len(shards)

    def body(*refs):
        ins, outs = refs[:n], refs[n:2 * n]
        send_sems, recv_sems, local_sems = refs[2 * n:]
        x, y, c = _position()
        me, sibling = (x, y, c), (x, y, 1 - c)
        chips = [(1 - x, y), (x, 1 - y), (1 - x, 1 - y)]

        def copy(t, k, block, to, from_input=False):
            dst = outs[t].at[_slot(*block)]
            return pltpu.make_async_remote_copy(
                src_ref=ins[t] if from_input else dst, dst_ref=dst, send_sem=send_sems.at[t, k], recv_sem=recv_sems.at[t, k],
                device_id=to, device_id_type=MESH_ID)

        mine = [pltpu.make_async_copy(ins[t], outs[t].at[_slot(*me)], local_sems.at[t]) for t in range(n)]
        for cp in mine:
            cp.start()
        first = []
        for t in range(n):
            first.append(copy(t, 0, me, sibling, True))
            first += [copy(t, 1 + j, me, (*chip, c), True) for j, chip in enumerate(chips)]
        for cp in first:
            cp.start()
        passed = []
        for t in range(n):
            for j, chip in enumerate(chips):
                copy(t, 1 + j, (*chip, c), me).wait_recv()
                fwd = copy(t, 4 + j, (*chip, c), sibling)
                fwd.start()
                passed.append(fwd)
        for t in range(n):
            copy(t, 0, sibling, me).wait_recv()
            for j, chip in enumerate(chips):
                copy(t, 4 + j, (*chip, 1 - c), me).wait_recv()
        for cp in first + passed:
            cp.wait_send()
        for cp in mine:
            cp.wait()

    return pl.pallas_call(
        body, name=name, in_specs=[ANY] * n, out_specs=[ANY] * n,
        out_shape=[jax.ShapeDtypeStruct((N_DEV,) + s.shape, s.dtype) for s in shards],
        scratch_shapes=[pltpu.SemaphoreType.DMA((n, N_PEERS)), pltpu.SemaphoreType.DMA((n, N_PEERS)), pltpu.SemaphoreType.DMA((n,))],
    )(*shards)


ALL_PEERS = (1, 2, 3, 4, 5, 6, 7)
SIBLING_AND_SAME_CORE = (1, 2, 4, 6)
SAME_CORE = (2, 4, 6)


def _flip(x, y, c, k):
    return x ^ (k >> 2), y ^ ((k >> 1) & 1), c ^ (k & 1)


def _send_copy(parts, landing, shared, send_sems, recv_sems, t, s, k):
    x, y, c = _position()
    peer = _flip(x, y, c, k)
    src = parts[t] if shared[t] else parts[t].at[_slot(*peer)]
    return pltpu.make_async_remote_copy(
        src_ref=src, dst_ref=landing[t].at[_slot(x, y, c)], send_sem=send_sems.at[s], recv_sem=recv_sems.at[s],
        device_id=peer, device_id_type=MESH_ID)


def _send_arrival(landing, send_sems, recv_sems, t, s, k):
    x, y, c = _position()
    peer = _flip(x, y, c, k)
    landed = landing[t].at[_slot(*peer)]
    return pltpu.make_async_remote_copy(
        src_ref=landed, dst_ref=landed, send_sem=send_sems.at[s], recv_sem=recv_sems.at[s],
        device_id=peer, device_id_type=MESH_ID)


def _handshake(peers):
    x, y, c = _position()
    barrier = pltpu.get_barrier_semaphore()
    for k in peers:
        pl.semaphore_signal(barrier, inc=1, device_id=_flip(x, y, c, k), device_id_type=MESH_ID)
    pl.semaphore_wait(barrier, len(peers))


def _sequencer(name, collective_id, n_copies, body, operands, out_type):
    n_arrays = len(operands)
    return pl.kernel(
        body, out_type=out_type, mesh=plsc.ScalarSubcoreMesh(axis_name="sequencer", num_cores=1), name=name,
        scratch_types=(pltpu.SemaphoreType.DMA((n_copies,)), pltpu.SemaphoreType.DMA((n_copies,)),
                       pltpu.SemaphoreType.DMA((n_arrays,))),
        compiler_params=pltpu.CompilerParams(collective_id=collective_id))(*operands)


def _sequencer_exchange(name, collective_id, parts, shared, after=()):
    n, n_peers, n_in = len(parts), len(ALL_PEERS), len(parts) + len(after)

    def body(*refs):
        src, landing = refs[:n], refs[n_in:n_in + n]
        send_sems, recv_sems, local_sems = refs[n_in + n:]
        _handshake(ALL_PEERS)
        my_slot = _slot(*_position())
        mine = [pltpu.make_async_copy(src[t] if shared[t] else src[t].at[my_slot], landing[t].at[my_slot], local_sems.at[t])
                for t in range(n)]
        for cp in mine:
            cp.start()
        sent = [_send_copy(src, landing, shared, send_sems, recv_sems, t, t * n_peers + j, k)
                for t in range(n) for j, k in enumerate(ALL_PEERS)]
        for cp in sent:
            cp.start()
        for t in range(n):
            for j, k in enumerate(ALL_PEERS):
                _send_arrival(landing, send_sems, recv_sems, t, t * n_peers + j, k).wait_recv()
        for cp in sent:
            cp.wait_send()
        for cp in mine:
            cp.wait()

    landing = [jax.ShapeDtypeStruct(((N_DEV,) + p.shape) if sh else p.shape, p.dtype) for p, sh in zip(parts, shared)]
    return _sequencer(name, collective_id, n * n_peers, body, list(parts) + list(after), landing)


def _sequencer_gather(name, collective_id, shards):
    n, per = len(shards), N_PEERS

    def body(*refs):
        src, out = refs[:n], refs[n:2 * n]
        send_sems, recv_sems, local_sems = refs[2 * n:]
        _handshake(SIBLING_AND_SAME_CORE)
        x, y, c = _position()
        me, sibling = (x, y, c), (x, y, 1 - c)

        def copy(t, j, block, to, from_input=False):
            dst = out[t].at[_slot(*block)]
            return pltpu.make_async_remote_copy(
                src_ref=src[t] if from_input else dst, dst_ref=dst, send_sem=send_sems.at[t * per + j],
                recv_sem=recv_sems.at[t * per + j], device_id=to, device_id_type=MESH_ID)

        mine = [pltpu.make_async_copy(src[t], out[t].at[_slot(*me)], local_sems.at[t]) for t in range(n)]
        for cp in mine:
            cp.start()
        sent = [copy(t, j, me, _flip(x, y, c, k), True) for t in range(n) for j, k in enumerate(SIBLING_AND_SAME_CORE)]
        for cp in sent:
            cp.start()
        for t in range(n):
            for j, k in enumerate(SAME_CORE):
                block = _flip(x, y, c, k)
                copy(t, 1 + j, block, me).wait_recv()
                forward = copy(t, 4 + j, block, sibling)
                forward.start()
                sent.append(forward)
        for t in range(n):
            copy(t, 0, sibling, me).wait_recv()
            for j, k in enumerate(SAME_CORE):
                copy(t, 4 + j, _flip(x, y, 1 - c, k), me).wait_recv()
        for cp in sent:
            cp.wait_send()
        for cp in mine:
            cp.wait()

    gathered = [jax.ShapeDtypeStruct((N_DEV,) + s.shape, s.dtype) for s in shards]
    return _sequencer(name, collective_id, n * per, body, shards, gathered)


ADAM_ROWS = 256
BF16_ROWS = 16


def _adam_update(w, g, m, v):
    m = ADAM_B1 * m + (1.0 - ADAM_B1) * g
    v = ADAM_B2 * v + (1.0 - ADAM_B2) * (g * g)
    m_hat = m / (1.0 - ADAM_B1 ** ADAM_STEP)
    v_hat = v / (1.0 - ADAM_B2 ** ADAM_STEP)
    delta = -ADAM_LR * (m_hat / (jnp.sqrt(v_hat) + ADAM_EPS) + ADAM_WD * w)
    return delta, m, v


def _sum_slots(ref):
    total = ref[0].astype(F32)
    for d in range(1, N_DEV):
        total = total + ref[d].astype(F32)
    return total


def _adamw_sum(name, landed, w, m, v):
    layers, rows, cols = w.shape
    tiles = [t for t in range(ADAM_ROWS, 0, -BF16_ROWS) if rows % t == 0]
    tr = tiles[0] if tiles else rows
    nt = rows // tr

    def body(*refs):
        parts = refs[:layers]
        w_ref, m_ref, v_ref, g_ref, d_ref, nm_ref, nv_ref = refs[layers:]
        layer = pl.program_id(0)
        g = _sum_slots(parts[0])
        for q in range(1, layers):
            g = jnp.where(layer == q, _sum_slots(parts[q]), g)
        delta, new_m, new_v = _adam_update(w_ref[...], g, m_ref[...], v_ref[...])
        g_ref[...] = g
        d_ref[...] = delta
        nm_ref[...] = new_m
        nv_ref[...] = new_v

    def part_spec(q):
        return _spec((N_DEV, tr, cols), lambda l, i: (0, jnp.where(l == q, i, jnp.where(l < q, 0, nt - 1)), 0))

    tile = _spec((None, tr, cols), lambda l, i: (l, i, 0))
    out = jax.ShapeDtypeStruct((layers, rows, cols), F32)
    return pl.pallas_call(
        body, name=name, grid=(layers, nt), in_specs=[part_spec(q) for q in range(layers)] + [tile] * 3,
        out_specs=[tile] * 4, out_shape=[out] * 4, compiler_params=_params(("arbitrary", "arbitrary")),
    )(*landed, w, m, v)


def _sum_small(landed):
    def body(in_ref, out_ref):
        out_ref[...] = _sum_slots(in_ref)

    return pl.pallas_call(body, name="small_grad_sum", out_shape=jax.ShapeDtypeStruct(landed.shape[1:], F32))(landed)


def _adamw_small(name, g, w, m, v):
    def body(g_ref, w_ref, m_ref, v_ref, d_ref, nm_ref, nv_ref):
        d_ref[...], nm_ref[...], nv_ref[...] = _adam_update(w_ref[...], g_ref[...], m_ref[...], v_ref[...])

    out = jax.ShapeDtypeStruct(w.shape, F32)
    return pl.pallas_call(body, name=name, out_shape=[out] * 3)(g, w, m, v)


LANES = 128
SUBLANES = 8
F_CONV_SHARD = D_FF // N_DEV
GATE_SHARD = KEY_DIM // N_DEV
NORM_SHARD = D_MODEL // N_DEV


def _tile_rows(a):
    flat = a.reshape(-1)
    size = -(-flat.shape[0] // (SUBLANES * LANES)) * SUBLANES * LANES
    return jnp.pad(flat, (0, size - flat.shape[0])).reshape(-1, LANES)


def _pack_rows(pieces):
    return jnp.concatenate([_tile_rows(p) for p in pieces], axis=0)


def _unpack_rows(packed, shapes):
    out, row = [], 0
    for shape in shapes:
        size = 1
        for s in shape:
            size *= s
        rows = -(-size // (SUBLANES * LANES)) * SUBLANES
        piece = packed[..., row:row + rows, :]
        out.append(piece.reshape(piece.shape[:-2] + (rows * LANES,))[..., :size])
        row += rows
    return out


SMALL_SHARDS = ((GATE_RANK, GATE_SHARD), (1, NORM_SHARD), (3, NORM_SHARD), (2, 3, F_CONV_SHARD))


def _unpack_small_shards(g):
    gate, b_norm, b_conv, f_conv = _unpack_rows(g, SMALL_SHARDS)
    gate = gate.reshape(N_DEV, GATE_RANK, GATE_SHARD).transpose(1, 0, 2).reshape(GATE_RANK, KEY_DIM)
    b_norm = b_norm.reshape(1, D_MODEL)
    b_conv = b_conv.reshape(N_DEV, 3, NORM_SHARD).transpose(1, 0, 2).reshape(3, D_MODEL)
    f_conv = f_conv.reshape(N_DEV, 2, 3, F_CONV_SHARD).transpose(1, 2, 0, 3).reshape(2, 3, D_FF)
    return gate, b_norm, b_conv, f_conv


def _conv_blocks(f_conv):
    return f_conv.reshape(2, 3, FF_BLOCKS, FF_BLOCK).transpose(0, 2, 1, 3)


def _conv_unblocks(f_conv):
    return f_conv.transpose(1, 0, 2).reshape(3, D_FF)


SMALL_LAYOUT = (("a_norm", (1, D_MODEL)), ("a_w_gate_up", (GATE_RANK, KEY_DIM)), ("a_b_gate", (1, KEY_DIM)), ("a_gn", (1, VALUE_DIM)),
                ("b_norm", (1, D_MODEL)), ("b_conv", (3, D_MODEL)), ("f_norm0", (1, D_MODEL)), ("f_norm1", (1, D_MODEL)),
                ("f_conv0", (3, D_FF)), ("f_conv1", (3, D_FF)), ("final_norm", (1, D_MODEL)))


def _pack_small_grads(g):
    full = dict(g)
    full["a_w_gate_up"] = g["a_w_gate_up"][:GATE_RANK]
    for layer in range(2):
        full[f"f_norm{layer}"] = g["f_norm"][layer]
        full[f"f_conv{layer}"] = _conv_unblocks(g["f_conv"][layer])
    return _pack_rows([full[name] for name, _ in SMALL_LAYOUT])


def _unpack_small_grads(packed):
    pieces = _unpack_rows(packed, [shape for _, shape in SMALL_LAYOUT])
    out = {name: piece.reshape(shape) for (name, shape), piece in zip(SMALL_LAYOUT, pieces)}
    out["f_norm"] = jnp.stack([out["f_norm0"][0], out["f_norm1"][0]])
    out["f_conv"] = jnp.stack([out["f_conv0"], out["f_conv1"]])
    return out


def kernel(x, a_norm, a_w_in, a_w_gate_up, a_b_gate, a_gn, a_w_out, b_norm, b_w_in, b_conv, b_w_out, f_norm, f_w_up, f_conv, f_w_down, final_norm, loss_target, m_a_norm, m_a_w_in, m_a_w_gate_up, m_a_b_gate, m_a_gn, m_a_w_out, m_b_norm, m_b_w_in, m_b_conv, m_b_w_out, m_f_norm, m_f_w_up, m_f_conv, m_f_w_down, m_final_norm, v_a_norm, v_a_w_in, v_a_w_gate_up, v_a_b_gate, v_a_gn, v_a_w_out, v_b_norm, v_b_w_in, v_b_conv, v_b_w_out, v_f_norm, v_f_w_up, v_f_conv, v_f_w_down, v_final_norm):
    my_slot = _slot(*_position())

    transposed = lambda w: jnp.swapaxes(w, 1, 2)
    a_w_in_t, f_w_up_t = transposed(a_w_in), transposed(f_w_up)
    first = _all_gather("weight_gather", [a_w_in_t[0].astype(BF16), a_w_out[0].astype(BF16),
                                          _pack_rows([a_w_gate_up[0], b_norm, b_conv[0], f_conv])])
    gathers, small_shards = {}, first[2]
    later = (("f0", f_w_up_t[0], f_w_down[0]), ("b", b_w_in[0], b_w_out[0]), ("f1", f_w_up_t[1], f_w_down[1]))
    for collective_id, (group, w_in, w_out) in enumerate(later):
        w_in, w_out, small_shards = lax.optimization_barrier((w_in.astype(BF16), w_out.astype(BF16), small_shards))
        gathers[group] = _sequencer_gather(f"gather_{group}", collective_id, [w_in, w_out])
    gate_full, b_norm_full, b_conv_full, f_conv_full = _unpack_small_shards(small_shards)
    a_w_in_full = jnp.pad(first[0].reshape(PROJ_A, D_MODEL), ((0, PROJ_A_PAD - PROJ_A), (0, 0)))
    weights = dict(
        a_norm=a_norm, a_w_gate_up=jnp.pad(gate_full, ((0, GATE_PAD - GATE_RANK), (0, 0))).astype(BF16), a_b_gate=a_b_gate,
        a_gn=a_gn, b_norm=b_norm_full, b_conv=b_conv_full, f_norm=f_norm, f_conv=_conv_blocks(f_conv_full),
        final_norm=final_norm.reshape(1, D_MODEL))

    def fetch(group, after):
        if group == "a":
            return a_w_in_full, first[1].reshape(D_MODEL, D_MODEL)
        w_in, w_out = gathers[group]
        if group == "b":
            return w_in, w_out.reshape(D_MODEL, D_MODEL)
        return w_in, w_out.reshape(FF_BLOCKS, FF_BLOCK, D_MODEL)

    exchanges, pending = {}, []
    exchange_ids = dict(b=3, f0=4, a=5)

    def emit(group, grads, carry):
        d_in, d_out = grads
        if group == "a":
            d_in = d_in[:PROJ_A]
        d_in, d_out = d_in.reshape((N_DEV, -1) + d_in.shape[-1:]), d_out.reshape((N_DEV, -1, D_MODEL))
        carry, d_in, d_out = lax.optimization_barrier((carry, d_in, d_out))
        pending.extend([d_in, d_out])
        if group != "f1":
            after = list(exchanges.values())[-1][:1] if exchanges else ()
            exchanges[group] = _sequencer_exchange(
                f"grads_{group}", exchange_ids[group], list(pending), [False] * len(pending), after)
            pending.clear()
        return carry

    loss, dx, g = _local_step(x[0], loss_target[0], weights, fetch, emit)
    loss = lax.psum(loss, MESH_AXES)
    small_landed = _all_gather("small_grad_gather", [_pack_small_grads(g)])[0]

    (up1, down1, d_b_in, d_b_out), (up0, down0), (d_a_in, d_a_out) = (exchanges[group] for group in ("b", "f0", "a"))
    back = lambda results: tuple(transposed(r) for r in results)
    big = dict(
        b_w_in=_adamw_sum("adam_b_w_in", [d_b_in], b_w_in, m_b_w_in, v_b_w_in),
        b_w_out=_adamw_sum("adam_b_w_out", [d_b_out], b_w_out, m_b_w_out, v_b_w_out),
        f_w_up=back(_adamw_sum("adam_f_w_up", [up0, up1], f_w_up_t, transposed(m_f_w_up), transposed(v_f_w_up))),
        f_w_down=_adamw_sum("adam_f_w_down", [down0, down1], f_w_down, m_f_w_down, v_f_w_down),
        a_w_in=back(_adamw_sum("adam_a_w_in", [d_a_in], a_w_in_t, transposed(m_a_w_in), transposed(v_a_w_in))),
        a_w_out=_adamw_sum("adam_a_w_out", [d_a_out], a_w_out, m_a_w_out, v_a_w_out))
    small_g = _unpack_small_grads(_sum_small(small_landed))
    small_g["a_w_gate_up"] = lax.dynamic_slice_in_dim(small_g["a_w_gate_up"], my_slot * GATE_SHARD, GATE_SHARD, axis=1)
    small_g["b_norm"] = lax.dynamic_slice_in_dim(small_g["b_norm"], my_slot * NORM_SHARD, NORM_SHARD, axis=1)
    small_g["b_conv"] = lax.dynamic_slice_in_dim(small_g["b_conv"], my_slot * NORM_SHARD, NORM_SHARD, axis=1)
    small_g["f_conv"] = lax.dynamic_slice_in_dim(small_g["f_conv"], my_slot * F_CONV_SHARD, F_CONV_SHARD, axis=2)
    small_w = dict(
        a_norm=(a_norm, m_a_norm, v_a_norm), a_w_gate_up=(a_w_gate_up, m_a_w_gate_up, v_a_w_gate_up),
        a_b_gate=(a_b_gate, m_a_b_gate, v_a_b_gate), a_gn=(a_gn, m_a_gn, v_a_gn), b_norm=(b_norm, m_b_norm, v_b_norm),
        b_conv=(b_conv, m_b_conv, v_b_conv), f_norm=(f_norm, m_f_norm, v_f_norm), f_conv=(f_conv, m_f_conv, v_f_conv),
        final_norm=(final_norm, m_final_norm, v_final_norm))
    small = {}
    for name, (w, m, v) in small_w.items():
        flat = (w.shape[-1],) if w.ndim == 1 else w.shape[-2:]
        two_d = (-1, flat[-1])
        grad = small_g[name].reshape(w.shape)
        delta, new_m, new_v = _adamw_small(
            "adam_" + name, grad.reshape(two_d), w.reshape(two_d), m.reshape(two_d), v.reshape(two_d))
        small[name] = (grad, delta.reshape(w.shape), new_m.reshape(w.shape), new_v.reshape(w.shape))

    order = ["a_norm", "a_w_in", "a_w_gate_up", "a_b_gate", "a_gn", "a_w_out", "b_norm", "b_w_in", "b_conv", "b_w_out",
             "f_norm", "f_w_up", "f_conv", "f_w_down", "final_norm"]
    results = {**big, **small}
    outputs = [loss, dx.reshape(1, SEQ, D_MODEL)]
    for kind in range(4):
        outputs += [results[name][kind] for name in order]
    return tuple(outputs)
```

```python
import jax
import jax.numpy as jnp
from jax import lax
from jax.experimental import pallas as pl
from jax.experimental.pallas import tpu as pltpu
from jax.experimental.pallas import tpu_sc as plsc

F32 = jnp.float32
BF16 = jnp.bfloat16

N_DEV = 8
SEQ = 2048
D_MODEL = 1024
CHUNK = 64
N_CHUNKS = SEQ // CHUNK
RMS_EPS = 1e-6
GLA_HEADS = 4
KEY_DIM = 512
VALUE_DIM = 1024
HEAD_K = KEY_DIM // GLA_HEADS
HEAD_V = VALUE_DIM // GLA_HEADS
GATE_RANK = 16
GATE_PAD = 128
GATE_NORMALIZER = 16.0
PROJ_A = 2 * KEY_DIM + 2 * VALUE_DIM + GATE_RANK
PROJ_A_PAD = 2 * KEY_DIM + 2 * VALUE_DIM + GATE_PAD
A_SHARD = PROJ_A // N_DEV
B_SHARD = 3 * D_MODEL // N_DEV
D_FF = 2816
FF_BLOCK = 2 * D_FF // N_DEV
FF_BLOCKS = D_FF // FF_BLOCK
ADAM_LR = 0.001
ADAM_B1 = 0.9
ADAM_B2 = 0.999
ADAM_EPS = 1e-08
ADAM_WD = 0.01
ADAM_STEP = 10
MESH_AXES = ("x", "y", "c")

VMEM_LIMIT = 56 * 1024 * 1024
ROW_CHUNK = 256
HALO = 16


def _params(sem=None, vmem=VMEM_LIMIT):
    return pltpu.CompilerParams(dimension_semantics=sem, vmem_limit_bytes=vmem)


NN = ((1,), (0,))
NT = ((1,), (1,))
TN = ((0,), (0,))


def _matmul(name, a, a_spec, b, b_spec, dims, grid, out_shape, out_spec, k_blocks=None, a_block_cols=None, res=None,
            res_spec=None, transpose_out=False):
    has_res = res is not None

    def body(*refs):
        a_ref, b_ref = refs[0], refs[1]
        r_ref = refs[2] if has_res else None
        o_ref = refs[2 + has_res]

        def product(lhs, rhs):
            return lax.dot_general(lhs.astype(BF16), rhs, (dims, ((), ())), preferred_element_type=F32)

        if k_blocks is None:
            v = product(a_ref[...], b_ref[...])
        else:
            v = None
            for k in range(k_blocks):
                lhs = a_ref[k] if a_block_cols is None else a_ref[:, k * a_block_cols:(k + 1) * a_block_cols]
                p = product(lhs, b_ref[k])
                v = p if v is None else v + p
        if transpose_out:
            v = v.T
        if has_res:
            v = v + r_ref[...]
        o_ref[...] = v.astype(o_ref.dtype)

    operands = [a, b] + ([res] if has_res else [])
    in_specs = [a_spec, b_spec] + ([res_spec] if has_res else [])
    return pl.pallas_call(
        body, name=name, grid=grid, in_specs=in_specs, out_specs=out_spec, out_shape=out_shape,
        compiler_params=_params(("parallel",) * len(grid)),
    )(*operands)


TM = 512
N_TM = SEQ // TM
PA_TILE = 640
N_PA = PROJ_A_PAD // PA_TILE
OUT_TILE = 256


def _spec(shape, fn):
    return pl.BlockSpec(shape, fn)


def _act(shape=(SEQ, D_MODEL), dtype=BF16):
    return jax.ShapeDtypeStruct(shape, dtype)


def _proj_rows_nt(name, h, wt, n_tile):
    n = wt.shape[0]
    return _matmul(name, h, _spec((SEQ, D_MODEL), lambda j: (0, 0)), wt, _spec((n_tile, D_MODEL), lambda j: (j, 0)), NT,
                   (n // n_tile,), _act((SEQ, n)), _spec((SEQ, n_tile), lambda j: (0, j)))


def _proj_blocks_nt(name, h, wt_blocks):
    nb, n, _ = wt_blocks.shape
    return _matmul(name, h, _spec((SEQ, D_MODEL), lambda j: (0, 0)), wt_blocks, _spec((None, n, D_MODEL), lambda j: (j, 0, 0)),
                   NT, (nb,), _act((nb, SEQ, n)), _spec((None, SEQ, n), lambda j: (j, 0, 0)))


def _proj_cols_nn(name, h, w_blocks):
    nb, _, n = w_blocks.shape
    return _matmul(name, h, _spec((SEQ, D_MODEL), lambda j: (0, 0)), w_blocks, _spec((None, D_MODEL, n), lambda j: (j, 0, 0)),
                   NN, (nb,), _act((SEQ, nb * n)), _spec((SEQ, n), lambda j: (0, j)))


def _square(name, a, w, dims, x=None):
    row = _spec((TM, D_MODEL), lambda i: (i, 0))
    return _matmul(name, a, row, w, _spec((D_MODEL, D_MODEL), lambda i: (0, 0)), dims, (N_TM,),
                   _act(dtype=F32 if x is not None else BF16), row, res=x, res_spec=row if x is not None else None)


def _sum_blocks_nn(name, a_blocks, w_blocks, x=None):
    nb, _, n = a_blocks.shape
    row = _spec((TM, D_MODEL), lambda i: (i, 0))
    return _matmul(name, a_blocks, _spec((nb, TM, n), lambda i: (0, i, 0)), w_blocks, _spec((nb, n, D_MODEL), lambda i: (0, 0, 0)),
                   NN, (N_TM,), _act(dtype=F32), row, k_blocks=nb, res=x, res_spec=row if x is not None else None)


def _sum_cols_nt(name, d, w_blocks):
    nb, _, n = w_blocks.shape
    return _matmul(name, d, _spec((TM, nb * n), lambda i: (i, 0)), w_blocks, _spec((nb, D_MODEL, n), lambda i: (0, 0, 0)), NT,
                   (N_TM,), _act(dtype=F32), _spec((TM, D_MODEL), lambda i: (i, 0)), k_blocks=nb, a_block_cols=n)


def _wide_nn(name, d, wt):
    n = wt.shape[0]
    return _matmul(name, d, _spec((TM, n), lambda i: (i, 0)), wt, _spec((n, D_MODEL), lambda i: (0, 0)), NN, (N_TM,),
                   _act(dtype=F32), _spec((TM, D_MODEL), lambda i: (i, 0)))


def _wgrad_blocks_tn(name, d_blocks, h):
    nb, _, n = d_blocks.shape
    return _matmul(name, d_blocks, _spec((None, SEQ, n), lambda j: (j, 0, 0)), h, _spec((SEQ, D_MODEL), lambda j: (0, 0)), TN,
                   (nb,), _act((nb, n, D_MODEL)), _spec((None, n, D_MODEL), lambda j: (j, 0, 0)))


def _wgrad_cols_tn(name, d, n_tile, h):
    n = d.shape[1]
    return _matmul(name, d, _spec((SEQ, n_tile), lambda j: (0, j)), h, _spec((SEQ, D_MODEL), lambda j: (0, 0)), TN,
                   (n // n_tile,), _act((n, D_MODEL)), _spec((n_tile, D_MODEL), lambda j: (j, 0)))


def _wgrad_cols_transposed_tn(name, h, d, n_tile):
    nb = d.shape[1] // n_tile
    return _matmul(name, d, _spec((SEQ, n_tile), lambda j: (0, j)), h, _spec((SEQ, D_MODEL), lambda j: (0, 0)), TN, (nb,),
                   _act((nb, D_MODEL, n_tile)), _spec((None, D_MODEL, n_tile), lambda j: (j, 0, 0)), transpose_out=True)


NORM_ROWS = 512


def _rstd(x):
    return lax.rsqrt(jnp.mean(x * x, axis=-1, keepdims=True) + RMS_EPS)


def _norm_fwd(name, x, gamma):
    def body(x_ref, g_ref, h_ref):
        x = x_ref[...]
        h_ref[...] = (x * _rstd(x) * g_ref[...]).astype(BF16)

    row = _spec((NORM_ROWS, D_MODEL), lambda i: (i, 0))
    return pl.pallas_call(
        body, name=name, grid=(SEQ // NORM_ROWS,), in_specs=[row, _spec((1, D_MODEL), lambda i: (0, 0))], out_specs=row,
        out_shape=jax.ShapeDtypeStruct((SEQ, D_MODEL), BF16), compiler_params=_params(("parallel",)),
    )(x, gamma)


def _norm_bwd_rows(x, gamma, dh):
    r = _rstd(x)
    xh = x * r
    dxh = dh * gamma
    dx = r * (dxh - xh * jnp.mean(dxh * xh, axis=-1, keepdims=True))
    return dx, jnp.sum(dh * xh, axis=0, keepdims=True)


def _norm_bwd(name, x, gamma, dh, dx_in):
    def body(x_ref, g_ref, dh_ref, dxi_ref, dx_ref, dx16_ref, dg_ref):
        dx, dg = _norm_bwd_rows(x_ref[...], g_ref[...], dh_ref[...].astype(F32))
        dx = dxi_ref[...] + dx
        dx_ref[...] = dx
        dx16_ref[...] = dx.astype(BF16)

        @pl.when(pl.program_id(0) == 0)
        def _():
            dg_ref[...] = dg

        @pl.when(pl.program_id(0) > 0)
        def _():
            dg_ref[...] += dg

    row = _spec((NORM_ROWS, D_MODEL), lambda i: (i, 0))
    vec = _spec((1, D_MODEL), lambda i: (0, 0))
    return pl.pallas_call(
        body, name=name, grid=(SEQ // NORM_ROWS,), in_specs=[row, vec, row, row], out_specs=[row, row, vec],
        out_shape=[_act(dtype=F32), _act(), jax.ShapeDtypeStruct((1, D_MODEL), F32)],
        compiler_params=_params(("arbitrary",)),
    )(x, gamma, dh, dx_in)


def _loss_head(x, gamma, target):
    def body(x_ref, g_ref, t_ref, loss_ref, dx_ref, dx16_ref, dg_ref):
        x = x_ref[...]
        gamma = g_ref[...]
        err = x * _rstd(x) * gamma - t_ref[...]
        dy = err * (1.0 / D_MODEL)
        dx, dg = _norm_bwd_rows(x, gamma, dy)
        dx_ref[...] = dx
        dx16_ref[...] = dx.astype(BF16)
        part = 0.5 * jnp.sum(jnp.sum(err * err, axis=-1, keepdims=True) * (1.0 / D_MODEL), axis=0, keepdims=True)
        part = jnp.broadcast_to(part, loss_ref.shape)

        @pl.when(pl.program_id(0) == 0)
        def _():
            dg_ref[...] = dg
            loss_ref[...] = part

        @pl.when(pl.program_id(0) > 0)
        def _():
            dg_ref[...] += dg
            loss_ref[...] += part

    row = _spec((NORM_ROWS, D_MODEL), lambda i: (i, 0))
    vec = _spec((1, D_MODEL), lambda i: (0, 0))
    return pl.pallas_call(
        body, name="loss_head", grid=(SEQ // NORM_ROWS,), in_specs=[row, vec, row],
        out_specs=[_spec((1, 128), lambda i: (0, 0)), row, row, vec],
        out_shape=[jax.ShapeDtypeStruct((1, 128), F32), _act(dtype=F32), _act(), jax.ShapeDtypeStruct((1, D_MODEL), F32)],
        compiler_params=_params(("arbitrary",)),
    )(x, gamma, target)


def _sigmoid(x):
    return 1.0 / (1.0 + jnp.exp(-x))


def _rows(ref, c):
    return ref[pl.ds(pl.multiple_of(c * ROW_CHUNK, ROW_CHUNK), ROW_CHUNK), :].astype(F32)


def _rows_before(ref, c):
    start = pl.multiple_of(jnp.maximum(c * ROW_CHUNK - HALO, 0), HALO)
    rows = ref[pl.ds(start, HALO), :].astype(F32)
    return jnp.where(c > 0, rows, 0.0)


def _rows_after(ref, c, n_chunks):
    start = pl.multiple_of(jnp.minimum((c + 1) * ROW_CHUNK, SEQ - HALO), HALO)
    rows = ref[pl.ds(start, HALO), :].astype(F32)
    return jnp.where(c < n_chunks - 1, rows, 0.0)


def _shift_down(z, before, n):
    row = lax.broadcasted_iota(jnp.int32, z.shape, 0)
    out = pltpu.roll(z, n, 0)
    for r in range(n):
        out = jnp.where(row == r, before[HALO - n + r:HALO - n + r + 1, :], out)
    return out


def _shift_up(z, after, n):
    rows = z.shape[0]
    row = lax.broadcasted_iota(jnp.int32, z.shape, 0)
    out = pltpu.roll(z, rows - n, 0)
    for r in range(n):
        out = jnp.where(row == rows - n + r, after[r:r + 1, :], out)
    return out


def _conv_rows(z, before, w):
    z1 = _shift_down(z, before, 1)
    z2 = _shift_down(z, before, 2)
    return w[2:3, :] * z + w[1:2, :] * z1 + w[0:1, :] * z2, z1, z2


def _conv_t_rows(dy, after, w):
    return w[2:3, :] * dy + w[1:2, :] * _shift_up(dy, after, 1) + w[0:1, :] * _shift_up(dy, after, 2)


N_ROW_CHUNKS = SEQ // ROW_CHUNK


def _ffn_mid_fwd(name, gu, conv_w):
    def body(gu_ref, w_ref, a_ref):
        w = w_ref[...]

        def chunk(c, carry):
            g = _rows(gu_ref.at[0], c)
            u = _rows(gu_ref.at[1], c)
            gc, _, _ = _conv_rows(g, _rows_before(gu_ref.at[0], c), w)
            a_ref[pl.ds(pl.multiple_of(c * ROW_CHUNK, ROW_CHUNK), ROW_CHUNK), :] = (gc * _sigmoid(gc) * u).astype(BF16)
            return carry

        lax.fori_loop(0, N_ROW_CHUNKS, chunk, 0)

    return pl.pallas_call(
        body, name=name, grid=(FF_BLOCKS,),
        in_specs=[_spec((2, None, SEQ, FF_BLOCK), lambda j: (0, j, 0, 0)), _spec((None, 3, FF_BLOCK), lambda j: (j, 0, 0))],
        out_specs=_spec((None, SEQ, FF_BLOCK), lambda j: (j, 0, 0)),
        out_shape=jax.ShapeDtypeStruct((FF_BLOCKS, SEQ, FF_BLOCK), BF16), compiler_params=_params(("parallel",)),
    )(gu, conv_w)


def _ffn_mid_bwd(name, gu, conv_w, da):
    def body(gu_ref, w_ref, da_ref, dgu_ref, dw_ref, dgc_ref):
        w = w_ref[...]

        def first(c, acc):
            g = _rows(gu_ref.at[0], c)
            u = _rows(gu_ref.at[1], c)
            d = _rows(da_ref, c)
            gc, g1, g2 = _conv_rows(g, _rows_before(gu_ref.at[0], c), w)
            sg = _sigmoid(gc)
            rows = pl.ds(pl.multiple_of(c * ROW_CHUNK, ROW_CHUNK), ROW_CHUNK)
            dgu_ref[1, rows, :] = (d * gc * sg).astype(BF16)
            dgc = d * u * (sg * (1.0 + gc * (1.0 - sg)))
            dgc_ref[rows, :] = dgc
            return (acc[0] + jnp.sum(dgc * g2, axis=0, keepdims=True), acc[1] + jnp.sum(dgc * g1, axis=0, keepdims=True),
                    acc[2] + jnp.sum(dgc * g, axis=0, keepdims=True))

        zero = jnp.zeros((1, FF_BLOCK), F32)
        acc = lax.fori_loop(0, N_ROW_CHUNKS, first, (zero, zero, zero))
        for r in range(3):
            dw_ref[r:r + 1, :] = acc[r]

        def second(c, carry):
            dgc = _rows(dgc_ref, c)
            dg = _conv_t_rows(dgc, _rows_after(dgc_ref, c, N_ROW_CHUNKS), w)
            dgu_ref[0, pl.ds(pl.multiple_of(c * ROW_CHUNK, ROW_CHUNK), ROW_CHUNK), :] = dg.astype(BF16)
            return carry

        lax.fori_loop(0, N_ROW_CHUNKS, second, 0)

    pair = _spec((2, None, SEQ, FF_BLOCK), lambda j: (0, j, 0, 0))
    wspec = _spec((None, 3, FF_BLOCK), lambda j: (j, 0, 0))
    return pl.pallas_call(
        body, name=name, grid=(FF_BLOCKS,),
        in_specs=[pair, wspec, _spec((None, SEQ, FF_BLOCK), lambda j: (j, 0, 0))], out_specs=[pair, wspec],
        out_shape=[jax.ShapeDtypeStruct((2, FF_BLOCKS, SEQ, FF_BLOCK), BF16), jax.ShapeDtypeStruct((FF_BLOCKS, 3, FF_BLOCK), F32)],
        scratch_shapes=[pltpu.VMEM((SEQ, FF_BLOCK), F32)], compiler_params=_params(("parallel",)),
    )(gu, conv_w, da)


SC_COLS = 256
N_SC = D_MODEL // SC_COLS


def _sc_specs():
    return [_spec((SEQ, SC_COLS), lambda j, part=part: (0, part * N_SC + j)) for part in range(3)]


def _sc_mid_fwd(p, conv_w):
    def body(b_ref, c_ref, h_ref, w_ref, y_ref):
        w = w_ref[...]

        def chunk(c, carry):
            z = _rows(c_ref, c) * _rows(h_ref, c)
            before = _rows_before(c_ref, c) * _rows_before(h_ref, c)
            zc, _, _ = _conv_rows(z, before, w)
            y_ref[pl.ds(pl.multiple_of(c * ROW_CHUNK, ROW_CHUNK), ROW_CHUNK), :] = (_rows(b_ref, c) * zc).astype(BF16)
            return carry

        lax.fori_loop(0, N_ROW_CHUNKS, chunk, 0)

    col = _spec((SEQ, SC_COLS), lambda j: (0, j))
    return pl.pallas_call(
        body, name="sc_mid_fwd", grid=(N_SC,), in_specs=_sc_specs() + [_spec((3, SC_COLS), lambda j: (0, j))], out_specs=col,
        out_shape=jax.ShapeDtypeStruct((SEQ, D_MODEL), BF16), compiler_params=_params(("parallel",)),
    )(p, p, p, conv_w)


def _sc_mid_bwd(p, conv_w, dy):
    def body(b_ref, c_ref, h_ref, w_ref, dy_ref, db_ref, dc_ref, dh_ref, dw_ref, dzc_ref):
        w = w_ref[...]

        def first(c, acc):
            z = _rows(c_ref, c) * _rows(h_ref, c)
            before = _rows_before(c_ref, c) * _rows_before(h_ref, c)
            zc, z1, z2 = _conv_rows(z, before, w)
            d = _rows(dy_ref, c)
            rows = pl.ds(pl.multiple_of(c * ROW_CHUNK, ROW_CHUNK), ROW_CHUNK)
            db_ref[rows, :] = (d * zc).astype(BF16)
            dzc = d * _rows(b_ref, c)
            dzc_ref[rows, :] = dzc
            return (acc[0] + jnp.sum(dzc * z2, axis=0, keepdims=True), acc[1] + jnp.sum(dzc * z1, axis=0, keepdims=True),
                    acc[2] + jnp.sum(dzc * z, axis=0, keepdims=True))

        zero = jnp.zeros((1, SC_COLS), F32)
        acc = lax.fori_loop(0, N_ROW_CHUNKS, first, (zero, zero, zero))
        for r in range(3):
            dw_ref[r:r + 1, :] = acc[r]

        def second(c, carry):
            dz = _conv_t_rows(_rows(dzc_ref, c), _rows_after(dzc_ref, c, N_ROW_CHUNKS), w)
            rows = pl.ds(pl.multiple_of(c * ROW_CHUNK, ROW_CHUNK), ROW_CHUNK)
            dc_ref[rows, :] = (dz * _rows(h_ref, c)).astype(BF16)
            dh_ref[rows, :] = (dz * _rows(c_ref, c)).astype(BF16)
            return carry

        lax.fori_loop(0, N_ROW_CHUNKS, second, 0)

    col = _spec((SEQ, SC_COLS), lambda j: (0, j))
    wspec = _spec((3, SC_COLS), lambda j: (0, j))
    act = jax.ShapeDtypeStruct((SEQ, D_MODEL), BF16)
    return pl.pallas_call(
        body, name="sc_mid_bwd", grid=(N_SC,), in_specs=_sc_specs() + [wspec, col], out_specs=[col, col, col, wspec],
        out_shape=[act, act, act, jax.ShapeDtypeStruct((3, D_MODEL), F32)],
        scratch_shapes=[pltpu.VMEM((SEQ, SC_COLS), F32)], compiler_params=_params(("parallel",)),
    )(p, p, p, conv_w, dy)


GLA_GROUP = 4
GLA_ROWS = GLA_GROUP * CHUNK
N_GROUPS = N_CHUNKS // GLA_GROUP
Q0, K0, V0, R0, G0 = 0, KEY_DIM, 2 * KEY_DIM, 2 * KEY_DIM + VALUE_DIM, 2 * KEY_DIM + 2 * VALUE_DIM


def _tri(strict):
    r = lax.broadcasted_iota(jnp.int32, (CHUNK, CHUNK), 0)
    c = lax.broadcasted_iota(jnp.int32, (CHUNK, CHUNK), 1)
    return jnp.where(c < r if strict else c <= r, 1.0, 0.0).astype(F32)


def _cumsum_rows(tri, x):
    return jnp.dot(tri, x, preferred_element_type=F32, precision=lax.Precision.HIGHEST)


def _gate_logits(gl, wgu, b_gate):
    return jnp.dot(gl, wgu, preferred_element_type=F32) + b_gate


def _log_decay(logits):
    return (jnp.minimum(logits, 0.0) - jnp.log(1.0 + jnp.exp(-jnp.abs(logits)))) * (1.0 / GATE_NORMALIZER)


def _head(x, h, width):
    return x[:, h * width:(h + 1) * width]


def _gla_fwd(proj, wgu, b_gate, gn):
    def body(p_ref, wgu_ref, b_ref, gn_ref, o_ref, og_ref, st_ref, state):
        @pl.when(pl.program_id(0) == 0)
        def _():
            state[...] = jnp.zeros_like(state)

        tri = _tri(False)
        la = _log_decay(_gate_logits(p_ref[:, G0:G0 + GATE_PAD], wgu_ref[...], b_ref[...]))
        for c in range(GLA_GROUP):
            rows = slice(c * CHUNK, (c + 1) * CHUNK)
            cum = _cumsum_rows(tri, la[rows])
            tot = cum[CHUNK - 1:CHUNK, :]
            kd = (p_ref[rows, K0:K0 + KEY_DIM].astype(F32) * jnp.exp(tot - cum)).astype(BF16)
            decay = jnp.exp(tot)
            q = (p_ref[rows, Q0:Q0 + KEY_DIM].astype(F32) * (HEAD_K ** -0.5)).astype(BF16)
            v = p_ref[rows, V0:V0 + VALUE_DIM]
            for h in range(GLA_HEADS):
                upd = lax.dot_general(_head(v, h, HEAD_V), _head(kd, h, HEAD_K), (TN, ((), ())), preferred_element_type=F32)
                s = state[h] * _head(decay, h, HEAD_K) + upd
                state[h] = s
                st_ref[c, h] = s
                o_ref[rows, h * HEAD_V:(h + 1) * HEAD_V] = lax.dot_general(
                    _head(q, h, HEAD_K), s.astype(BF16), (NT, ((), ())), preferred_element_type=F32)
        r = p_ref[:, R0:R0 + VALUE_DIM].astype(F32)
        gate = r * _sigmoid(r) * gn_ref[...]
        for h in range(GLA_HEADS):
            cols = slice(h * HEAD_V, (h + 1) * HEAD_V)
            o = o_ref[:, cols]
            og_ref[:, cols] = (o * _rstd(o) * gate[:, cols]).astype(BF16)

    rows = _spec((GLA_ROWS, VALUE_DIM), lambda i: (i, 0))
    const = lambda shape: _spec(shape, lambda i: (0,) * len(shape))
    return pl.pallas_call(
        body, name="gla_fwd", grid=(N_GROUPS,),
        in_specs=[_spec((GLA_ROWS, PROJ_A_PAD), lambda i: (i, 0)), const((GATE_PAD, KEY_DIM)), const((1, KEY_DIM)),
                  const((1, VALUE_DIM))],
        out_specs=[rows, rows, _spec((GLA_GROUP, GLA_HEADS, HEAD_V, HEAD_K), lambda i: (i, 0, 0, 0))],
        out_shape=[jax.ShapeDtypeStruct((SEQ, VALUE_DIM), F32), jax.ShapeDtypeStruct((SEQ, VALUE_DIM), BF16),
                   jax.ShapeDtypeStruct((N_CHUNKS, GLA_HEADS, HEAD_V, HEAD_K), F32)],
        scratch_shapes=[pltpu.VMEM((GLA_HEADS, HEAD_V, HEAD_K), F32)], compiler_params=_params(("arbitrary",)),
    )(proj, wgu, b_gate, gn)


def _gla_bwd(proj, wgu, b_gate, gn, o, states, dog):
    last = N_GROUPS - 1

    def body(p_ref, wgu_ref, b_ref, gn_ref, o_ref, st_ref, stp_ref, dog_ref, dp_ref, dwgu_ref, db_ref, dgn_ref, carry, do_buf):
        step = pl.program_id(0)

        @pl.when(step == 0)
        def _():
            carry[...] = jnp.zeros_like(carry)

        r = p_ref[:, R0:R0 + VALUE_DIM].astype(F32)
        sr = _sigmoid(r)
        silu = r * sr
        gn_row = gn_ref[...]
        dog_rows = dog_ref[...].astype(F32)
        dn = dog_rows * silu
        dgn_cols = []
        for h in range(GLA_HEADS):
            cols = slice(h * HEAD_V, (h + 1) * HEAD_V)
            oh = o_ref[:, cols]
            rs = _rstd(oh)
            ohat = oh * rs
            dn_h = dn[:, cols]
            dgn_cols.append(jnp.sum(dn_h * ohat, axis=0, keepdims=True))
            dohat = dn_h * gn_row[:, cols]
            do_buf[:, cols] = rs * (dohat - ohat * jnp.mean(dohat * ohat, axis=-1, keepdims=True))
            n_h = ohat * gn_row[:, cols]
            dp_ref[:, R0 + h * HEAD_V:R0 + (h + 1) * HEAD_V] = (
                dog_rows[:, cols] * n_h * (sr[:, cols] * (1.0 + r[:, cols] * (1.0 - sr[:, cols])))).astype(BF16)
        dgn = jnp.concatenate(dgn_cols, axis=1)

        tri = _tri(False)
        tri_strict = _tri(True)
        gl = p_ref[:, G0:G0 + GATE_PAD]
        logits = _gate_logits(gl, wgu_ref[...], b_ref[...])
        la = _log_decay(logits)
        dlogit_rows = []
        for c in reversed(range(GLA_GROUP)):
            rows = slice(c * CHUNK, (c + 1) * CHUNK)
            cum = _cumsum_rows(tri, la[rows])
            tot = cum[CHUNK - 1:CHUNK, :]
            fade = jnp.exp(tot - cum)
            k = p_ref[rows, K0:K0 + KEY_DIM].astype(F32)
            kd32 = k * fade
            kd = kd32.astype(BF16)
            decay = jnp.exp(tot)
            q = (p_ref[rows, Q0:Q0 + KEY_DIM].astype(F32) * (HEAD_K ** -0.5)).astype(BF16)
            v = p_ref[rows, V0:V0 + VALUE_DIM]
            do = do_buf[rows, :].astype(BF16)
            dkd_cols, ddecay_cols = [], []
            for h in range(GLA_HEADS):
                do_h = _head(do, h, HEAD_V)
                s = st_ref[c, h]
                dq = jnp.dot(do_h, s.astype(BF16), preferred_element_type=F32) * (HEAD_K ** -0.5)
                dp_ref[rows, Q0 + h * HEAD_K:Q0 + (h + 1) * HEAD_K] = dq.astype(BF16)
                g = carry[h] + lax.dot_general(do_h, _head(q, h, HEAD_K), (TN, ((), ())), preferred_element_type=F32)
                g16 = g.astype(BF16)
                dkd_cols.append(jnp.dot(_head(v, h, HEAD_V), g16, preferred_element_type=F32))
                dv = lax.dot_general(_head(kd, h, HEAD_K), g16, (NT, ((), ())), preferred_element_type=F32)
                dp_ref[rows, V0 + h * HEAD_V:V0 + (h + 1) * HEAD_V] = dv.astype(BF16)
                if c > 0:
                    s_prev = st_ref[c - 1, h]
                else:
                    s_prev = jnp.where(step < last, stp_ref[0, h], 0.0)
                ddecay_cols.append(jnp.sum(g * s_prev, axis=0, keepdims=True))
                carry[h] = g * _head(decay, h, HEAD_K)
            dkd = jnp.concatenate(dkd_cols, axis=1)
            ddecay = jnp.concatenate(ddecay_cols, axis=1)
            dp_ref[rows, K0:K0 + KEY_DIM] = (dkd * fade).astype(BF16)
            e = dkd * kd32
            dla = ddecay * decay + _cumsum_rows(tri_strict, e)
            dlogit_rows.append(dla * (1.0 / GATE_NORMALIZER) * (1.0 - _sigmoid(logits[rows])))
        dlogit = jnp.concatenate(dlogit_rows[::-1], axis=0)
        dlogit16 = dlogit.astype(BF16)
        dp_ref[:, G0:G0 + GATE_PAD] = lax.dot_general(
            dlogit16, wgu_ref[...], (NT, ((), ())), preferred_element_type=F32).astype(BF16)
        dwgu = lax.dot_general(gl, dlogit16, (TN, ((), ())), preferred_element_type=F32)
        db = jnp.sum(dlogit, axis=0, keepdims=True)

        @pl.when(step == 0)
        def _():
            dwgu_ref[...] = dwgu
            db_ref[...] = db
            dgn_ref[...] = dgn

        @pl.when(step > 0)
        def _():
            dwgu_ref[...] += dwgu
            db_ref[...] += db
            dgn_ref[...] += dgn

    rev = lambda i: (last - i, 0)
    rows = _spec((GLA_ROWS, VALUE_DIM), rev)
    const = lambda shape: _spec(shape, lambda i: (0,) * len(shape))
    st_shape = (GLA_HEADS, HEAD_V, HEAD_K)
    return pl.pallas_call(
        body, name="gla_bwd", grid=(N_GROUPS,),
        in_specs=[_spec((GLA_ROWS, PROJ_A_PAD), rev), const((GATE_PAD, KEY_DIM)), const((1, KEY_DIM)), const((1, VALUE_DIM)),
                  rows, _spec((GLA_GROUP,) + st_shape, lambda i: (last - i, 0, 0, 0)),
                  _spec((1,) + st_shape, lambda i: (jnp.maximum((last - i) * GLA_GROUP - 1, 0), 0, 0, 0)), rows],
        out_specs=[_spec((GLA_ROWS, PROJ_A_PAD), rev), const((GATE_PAD, KEY_DIM)), const((1, KEY_DIM)), const((1, VALUE_DIM))],
        out_shape=[jax.ShapeDtypeStruct((SEQ, PROJ_A_PAD), BF16), jax.ShapeDtypeStruct((GATE_PAD, KEY_DIM), F32),
                   jax.ShapeDtypeStruct((1, KEY_DIM), F32), jax.ShapeDtypeStruct((1, VALUE_DIM), F32)],
        scratch_shapes=[pltpu.VMEM(st_shape, F32), pltpu.VMEM((GLA_ROWS, VALUE_DIM), F32)],
        compiler_params=_params(("arbitrary",)),
    )(proj, wgu, b_gate, gn, o, states, states, dog)


def _ffn_fwd(tag, x, gamma, w_up_t, conv_w, w_down):
    h = _norm_fwd(f"ffn{tag}_norm", x, gamma)
    gu = _proj_blocks_nt(f"ffn{tag}_up", h, w_up_t).reshape(2, FF_BLOCKS, SEQ, FF_BLOCK)
    a = _ffn_mid_fwd(f"ffn{tag}_mid", gu, conv_w)
    return _sum_blocks_nn(f"ffn{tag}_down", a, w_down, x), (h, gu, a)


def _ffn_bwd(tag, x, gamma, w_up_t, conv_w, w_down, saved, dx, dx16):
    h, gu, a = saved
    da = _proj_blocks_nt(f"ffn{tag}_da", dx16, w_down)
    d_w_down = _wgrad_blocks_tn(f"ffn{tag}_dwdown", a, dx16)
    dgu, d_conv = _ffn_mid_bwd(f"ffn{tag}_mid_bwd", gu, conv_w, da)
    dgu = dgu.reshape(2 * FF_BLOCKS, SEQ, FF_BLOCK)
    dh = _sum_blocks_nn(f"ffn{tag}_dh", dgu, w_up_t)
    d_w_up_t = _wgrad_blocks_tn(f"ffn{tag}_dwup", dgu, h)
    dx, dx16, d_gamma = _norm_bwd(f"ffn{tag}_norm_bwd", x, gamma, dh, dx)
    return dx, dx16, d_gamma, d_w_up_t, d_conv, d_w_down


def _local_step(x, target, w, fetch=None, emit=None):
    if fetch is None:
        local = dict(a=(w.get("a_w_in"), w.get("a_w_out")), b=(w.get("b_w_in"), w.get("b_w_out")))
        for layer in range(2):
            local[f"f{layer}"] = (w["f_w_up"][layer], w["f_w_down"][layer]) if "f_w_up" in w else None
        fetch = lambda group, after: local[group]
    if emit is None:
        emit = lambda group, grads, dx: dx
    f_norm = (w["f_norm"][0:1], w["f_norm"][1:2])

    x0 = x
    a_w_in, a_w_out = fetch("a", x0)
    h0 = _norm_fwd("a_norm", x0, w["a_norm"])
    proj = _proj_rows_nt("a_in", h0, a_w_in, PA_TILE)
    o, og, states = _gla_fwd(proj, w["a_w_gate_up"], w["a_b_gate"], w["a_gn"])
    x1 = _square("a_out", og, a_w_out, NN, x0)
    up0, down0 = fetch("f0", x1)
    x2, ffn0 = _ffn_fwd(0, x1, f_norm[0], up0, w["f_conv"][0], down0)
    b_w_in, b_w_out = fetch("b", x2)
    h2 = _norm_fwd("b_norm", x2, w["b_norm"])
    p = _proj_cols_nn("b_in", h2, b_w_in)
    y = _sc_mid_fwd(p, w["b_conv"])
    x3 = _square("b_out", y, b_w_out, NN, x2)
    up1, down1 = fetch("f1", x3)
    x4, ffn1 = _ffn_fwd(1, x3, f_norm[1], up1, w["f_conv"][1], down1)
    loss, dx, dx16, d_final_norm = _loss_head(x4, w["final_norm"], target)

    dx, dx16, d_f_norm1, d_up1, d_fconv1, d_down1 = _ffn_bwd(1, x3, f_norm[1], up1, w["f_conv"][1], down1, ffn1, dx, dx16)
    dx16 = emit("f1", (d_up1, d_down1), dx16)

    dy = _square("b_dy", dx16, b_w_out, NT)
    d_b_w_out = _wgrad_cols_tn("b_dwout", y, OUT_TILE, dx16)
    db, dc, dhh, d_b_conv = _sc_mid_bwd(p, w["b_conv"], dy)
    dp = jnp.concatenate([db, dc, dhh], axis=1)
    dh2 = _sum_cols_nt("b_dh", dp, b_w_in)
    d_b_w_in = _wgrad_cols_transposed_tn("b_dwin", h2, dp, B_SHARD)
    dx, dx16, d_b_norm = _norm_bwd("b_norm_bwd", x2, w["b_norm"], dh2, dx)
    dx16 = emit("b", (d_b_w_in, d_b_w_out), dx16)

    dx, dx16, d_f_norm0, d_up0, d_fconv0, d_down0 = _ffn_bwd(0, x1, f_norm[0], up0, w["f_conv"][0], down0, ffn0, dx, dx16)
    dx16 = emit("f0", (d_up0, d_down0), dx16)

    dog = _square("a_dog", dx16, a_w_out, NT)
    d_a_w_out = _wgrad_cols_tn("a_dwout", og, OUT_TILE, dx16)
    dproj, d_wgu, d_b_gate, d_gn = _gla_bwd(proj, w["a_w_gate_up"], w["a_b_gate"], w["a_gn"], o, states, dog)
    d_a_w_in = _wgrad_cols_tn("a_dwin", dproj, PA_TILE, h0)
    dproj = emit("a", (d_a_w_in, d_a_w_out), dproj)
    dh0 = _wide_nn("a_dh", dproj, a_w_in)
    dx, _, d_a_norm = _norm_bwd("a_norm_bwd", x0, w["a_norm"], dh0, dx)

    grads = dict(
        a_norm=d_a_norm, a_w_in=d_a_w_in, a_w_gate_up=d_wgu, a_b_gate=d_b_gate, a_gn=d_gn, a_w_out=d_a_w_out,
        b_norm=d_b_norm, b_w_in=d_b_w_in, b_conv=d_b_conv, b_w_out=d_b_w_out,
        f_norm=(d_f_norm0, d_f_norm1), f_w_up=(d_up0, d_up1), f_conv=(d_fconv0, d_fconv1), f_w_down=(d_down0, d_down1),
        final_norm=d_final_norm)
    return loss[0, 0], dx, grads


MESH_ID = pl.DeviceIdType.MESH
ANY = pl.BlockSpec(memory_space=pl.ANY)
N_PEERS = N_DEV - 1


def _position():
    return lax.axis_index("x"), lax.axis_index("y"), lax.axis_index("c")


def _slot(px, py, pc):
    return 4 * px + 2 * py + pc


def _all_gather(name, shards):
    n = len(shards)

    def body(*refs):
        ins, outs = refs[:n], refs[n:2 * n]
        send_sems, recv_sems, local_sems = refs[2 * n:]
        x, y, c = _position()
        me, sibling = (x, y, c), (x, y, 1 - c)
        chips = [(1 - x, y), (x, 1 - y), (1 - x, 1 - y)]

        def copy(t, k, block, to, from_input=False):
            dst = outs[t].at[_slot(*block)]
            return pltpu.make_async_remote_copy(
                src_ref=ins[t] if from_input else dst, dst_ref=dst, send_sem=send_sems.at[t, k], recv_sem=recv_sems.at[t, k],
                device_id=to, device_id_type=MESH_ID)

        mine = [pltpu.make_async_copy(ins[t], outs[t].at[_slot(*me)], local_sems.at[t]) for t in range(n)]
        for cp in mine:
            cp.start()
        first = []
        for t in range(n):
            first.append(copy(t, 0, me, sibling, True))
            first += [copy(t, 1 + j, me, (*chip, c), True) for j, chip in enumerate(chips)]
        for cp in first:
            cp.start()
        passed = []
        for t in range(n):
            for j, chip in enumerate(chips):
                copy(t, 1 + j, (*chip, c), me).wait_recv()
                fwd = copy(t, 4 + j, (*chip, c), sibling)
                fwd.start()
                passed.append(fwd)
        for t in range(n):
            copy(t, 0, sibling, me).wait_recv()
            for j, chip in enumerate(chips):
                copy(t, 4 + j, (*chip, 1 - c), me).wait_recv()
        for cp in first + passed:
            cp.wait_send()
        for cp in mine:
            cp.wait()

    return pl.pallas_call(
        body, name=name, in_specs=[ANY] * n, out_specs=[ANY] * n,
        out_shape=[jax.ShapeDtypeStruct((N_DEV,) + s.shape, s.dtype) for s in shards],
        scratch_shapes=[pltpu.SemaphoreType.DMA((n, N_PEERS)), pltpu.SemaphoreType.DMA((n, N_PEERS)), pltpu.SemaphoreType.DMA((n,))],
    )(*shards)


SIBLING_AND_SAME_CORE = (1, 2, 4, 6)
SAME_CORE = (2, 4, 6)


def _flip(x, y, c, k):
    return x ^ (k >> 2), y ^ ((k >> 1) & 1), c ^ (k & 1)


N_CHIPS = N_DEV // 2


def _chip(px, py):
    return 2 * px + py


def _pair_swap(name, parts):
    n = len(parts)

    def body(*refs):
        ins, outs = refs[:n], refs[n:2 * n]
        send_sems, recv_sems = refs[2 * n:]
        x, y, c = _position()
        sibling = (x, y, 1 - c)
        sent = []
        for t in range(n):
            for q in range(N_CHIPS):
                sent.append(pltpu.make_async_remote_copy(
                    src_ref=ins[t].at[2 * q + 1 - c], dst_ref=outs[t].at[q], send_sem=send_sems.at[t, q],
                    recv_sem=recv_sems.at[t, q], device_id=sibling, device_id_type=MESH_ID))
        for cp in sent:
            cp.start()
        for t in range(n):
            for q in range(N_CHIPS):
                landed = outs[t].at[q]
                pltpu.make_async_remote_copy(
                    src_ref=landed, dst_ref=landed, send_sem=send_sems.at[t, q], recv_sem=recv_sems.at[t, q],
                    device_id=sibling, device_id_type=MESH_ID).wait_recv()
        for cp in sent:
            cp.wait_send()

    sems = pltpu.SemaphoreType.DMA((n, N_CHIPS))
    return pl.pallas_call(
        body, name=name, in_specs=[ANY] * n, out_specs=[ANY] * n,
        out_shape=[jax.ShapeDtypeStruct((N_CHIPS,) + p.shape[1:], p.dtype) for p in parts], scratch_shapes=[sems, sems],
    )(*parts)


PAIR_ROWS = 256


def _pair_add(name, part, received, side):
    _, rows, cols = part.shape
    tiles = [t for t in range(PAIR_ROWS, 0, -BF16_ROWS) if rows % t == 0]
    tr = tiles[0] if tiles else rows

    def body(side_ref, p_ref, r_ref, o_ref):
        o_ref[...] = (p_ref[...].astype(F32) + r_ref[...].astype(F32)).astype(BF16)

    tile = _spec((None, tr, cols), lambda q, i, side_ref: (q, i, 0))
    return pl.pallas_call(
        body, name=name,
        grid_spec=pltpu.PrefetchScalarGridSpec(
            num_scalar_prefetch=1, grid=(N_CHIPS, rows // tr),
            in_specs=[_spec((None, tr, cols), lambda q, i, side_ref: (2 * q + side_ref[0], i, 0)), tile], out_specs=tile),
        out_shape=jax.ShapeDtypeStruct((N_CHIPS, rows, cols), BF16), compiler_params=_params(("parallel", "parallel")),
    )(side, part, received)


def _send_copy(parts, landing, send_sems, recv_sems, t, s, k):
    x, y, c = _position()
    px, py, _ = _flip(x, y, c, k)
    return pltpu.make_async_remote_copy(
        src_ref=parts[t].at[_chip(px, py)], dst_ref=landing[t].at[_chip(x, y)], send_sem=send_sems.at[s],
        recv_sem=recv_sems.at[s], device_id=(px, py, c), device_id_type=MESH_ID)


def _send_arrival(landing, send_sems, recv_sems, t, s, k):
    x, y, c = _position()
    px, py, _ = _flip(x, y, c, k)
    landed = landing[t].at[_chip(px, py)]
    return pltpu.make_async_remote_copy(
        src_ref=landed, dst_ref=landed, send_sem=send_sems.at[s], recv_sem=recv_sems.at[s],
        device_id=(px, py, c), device_id_type=MESH_ID)


def _handshake(peers):
    x, y, c = _position()
    barrier = pltpu.get_barrier_semaphore()
    for k in peers:
        pl.semaphore_signal(barrier, inc=1, device_id=_flip(x, y, c, k), device_id_type=MESH_ID)
    pl.semaphore_wait(barrier, len(peers))


def _sequencer(name, collective_id, n_copies, body, operands, out_type):
    n_arrays = len(operands)
    return pl.kernel(
        body, out_type=out_type, mesh=plsc.ScalarSubcoreMesh(axis_name="sequencer", num_cores=1), name=name,
        scratch_types=(pltpu.SemaphoreType.DMA((n_copies,)), pltpu.SemaphoreType.DMA((n_copies,)),
                       pltpu.SemaphoreType.DMA((n_arrays,))),
        compiler_params=pltpu.CompilerParams(collective_id=collective_id))(*operands)


def _sequencer_exchange(name, collective_id, parts, after=()):
    n, n_peers, n_in = len(parts), len(SAME_CORE), len(parts) + len(after)

    def body(*refs):
        src, landing = refs[:n], refs[n_in:n_in + n]
        send_sems, recv_sems, local_sems = refs[n_in + n:]
        _handshake(SAME_CORE)
        x, y, _ = _position()
        mine = [pltpu.make_async_copy(src[t].at[_chip(x, y)], landing[t].at[_chip(x, y)], local_sems.at[t]) for t in range(n)]
        for cp in mine:
            cp.start()
        sent = [_send_copy(src, landing, send_sems, recv_sems, t, t * n_peers + j, k)
                for t in range(n) for j, k in enumerate(SAME_CORE)]
        for cp in sent:
            cp.start()
        for t in range(n):
            for j, k in enumerate(SAME_CORE):
                _send_arrival(landing, send_sems, recv_sems, t, t * n_peers + j, k).wait_recv()
        for cp in sent:
            cp.wait_send()
        for cp in mine:
            cp.wait()

    landing = [jax.ShapeDtypeStruct(p.shape, p.dtype) for p in parts]
    return _sequencer(name, collective_id, n * n_peers, body, list(parts) + list(after), landing)


def _sequencer_gather(name, collective_id, shards):
    n, per = len(shards), N_PEERS

    def body(*refs):
        src, out = refs[:n], refs[n:2 * n]
        send_sems, recv_sems, local_sems = refs[2 * n:]
        _handshake(SIBLING_AND_SAME_CORE)
        x, y, c = _position()
        me, sibling = (x, y, c), (x, y, 1 - c)

        def copy(t, j, block, to, from_input=False):
            dst = out[t].at[_slot(*block)]
            return pltpu.make_async_remote_copy(
                src_ref=src[t] if from_input else dst, dst_ref=dst, send_sem=send_sems.at[t * per + j],
                recv_sem=recv_sems.at[t * per + j], device_id=to, device_id_type=MESH_ID)

        mine = [pltpu.make_async_copy(src[t], out[t].at[_slot(*me)], local_sems.at[t]) for t in range(n)]
        for cp in mine:
            cp.start()
        sent = [copy(t, j, me, _flip(x, y, c, k), True) for t in range(n) for j, k in enumerate(SIBLING_AND_SAME_CORE)]
        for cp in sent:
            cp.start()
        for t in range(n):
            for j, k in enumerate(SAME_CORE):
                block = _flip(x, y, c, k)
                copy(t, 1 + j, block, me).wait_recv()
                forward = copy(t, 4 + j, block, sibling)
                forward.start()
                sent.append(forward)
        for t in range(n):
            copy(t, 0, sibling, me).wait_recv()
            for j, k in enumerate(SAME_CORE):
                copy(t, 4 + j, _flip(x, y, 1 - c, k), me).wait_recv()
        for cp in sent:
            cp.wait_send()
        for cp in mine:
            cp.wait()

    gathered = [jax.ShapeDtypeStruct((N_DEV,) + s.shape, s.dtype) for s in shards]
    return _sequencer(name, collective_id, n * per, body, shards, gathered)


ADAM_ROWS = 256
BF16_ROWS = 16


def _adam_update(w, g, m, v):
    m = ADAM_B1 * m + (1.0 - ADAM_B1) * g
    v = ADAM_B2 * v + (1.0 - ADAM_B2) * (g * g)
    m_hat = m / (1.0 - ADAM_B1 ** ADAM_STEP)
    v_hat = v / (1.0 - ADAM_B2 ** ADAM_STEP)
    delta = -ADAM_LR * (m_hat / (jnp.sqrt(v_hat) + ADAM_EPS) + ADAM_WD * w)
    return delta, m, v


def _sum_slots(ref):
    total = ref[0].astype(F32)
    for d in range(1, ref.shape[0]):
        total = total + ref[d].astype(F32)
    return total


def _adamw_sum(name, landed, w, m, v):
    layers, rows, cols = w.shape
    tiles = [t for t in range(ADAM_ROWS, 0, -BF16_ROWS) if rows % t == 0]
    tr = tiles[0] if tiles else rows
    nt = rows // tr

    def body(*refs):
        parts = refs[:layers]
        w_ref, m_ref, v_ref, g_ref, d_ref, nm_ref, nv_ref = refs[layers:]
        layer = pl.program_id(0)
        g = _sum_slots(parts[0])
        for q in range(1, layers):
            g = jnp.where(layer == q, _sum_slots(parts[q]), g)
        delta, new_m, new_v = _adam_update(w_ref[...], g, m_ref[...], v_ref[...])
        g_ref[...] = g
        d_ref[...] = delta
        nm_ref[...] = new_m
        nv_ref[...] = new_v

    def part_spec(q):
        return _spec((N_CHIPS, tr, cols), lambda l, i: (0, jnp.where(l == q, i, jnp.where(l < q, 0, nt - 1)), 0))

    tile = _spec((None, tr, cols), lambda l, i: (l, i, 0))
    out = jax.ShapeDtypeStruct((layers, rows, cols), F32)
    return pl.pallas_call(
        body, name=name, grid=(layers, nt), in_specs=[part_spec(q) for q in range(layers)] + [tile] * 3,
        out_specs=[tile] * 4, out_shape=[out] * 4, compiler_params=_params(("arbitrary", "arbitrary")),
    )(*landed, w, m, v)


def _sum_small(landed):
    def body(in_ref, out_ref):
        out_ref[...] = _sum_slots(in_ref)

    return pl.pallas_call(body, name="small_grad_sum", out_shape=jax.ShapeDtypeStruct(landed.shape[1:], F32))(landed)


def _adamw_small(name, g, w, m, v):
    def body(g_ref, w_ref, m_ref, v_ref, d_ref, nm_ref, nv_ref):
        d_ref[...], nm_ref[...], nv_ref[...] = _adam_update(w_ref[...], g_ref[...], m_ref[...], v_ref[...])

    out = jax.ShapeDtypeStruct(w.shape, F32)
    return pl.pallas_call(body, name=name, out_shape=[out] * 3)(g, w, m, v)


LANES = 128
SUBLANES = 8
F_CONV_SHARD = D_FF // N_DEV
GATE_SHARD = KEY_DIM // N_DEV
NORM_SHARD = D_MODEL // N_DEV


def _tile_rows(a):
    flat = a.reshape(-1)
    size = -(-flat.shape[0] // (SUBLANES * LANES)) * SUBLANES * LANES
    return jnp.pad(flat, (0, size - flat.shape[0])).reshape(-1, LANES)


def _pack_rows(pieces):
    return jnp.concatenate([_tile_rows(p) for p in pieces], axis=0)


def _unpack_rows(packed, shapes):
    out, row = [], 0
    for shape in shapes:
        size = 1
        for s in shape:
            size *= s
        rows = -(-size // (SUBLANES * LANES)) * SUBLANES
        piece = packed[..., row:row + rows, :]
        out.append(piece.reshape(piece.shape[:-2] + (rows * LANES,))[..., :size])
        row += rows
    return out


SMALL_SHARDS = ((GATE_RANK, GATE_SHARD), (1, NORM_SHARD), (3, NORM_SHARD), (2, 3, F_CONV_SHARD))


def _unpack_small_shards(g):
    gate, b_norm, b_conv, f_conv = _unpack_rows(g, SMALL_SHARDS)
    gate = gate.reshape(N_DEV, GATE_RANK, GATE_SHARD).transpose(1, 0, 2).reshape(GATE_RANK, KEY_DIM)
    b_norm = b_norm.reshape(1, D_MODEL)
    b_conv = b_conv.reshape(N_DEV, 3, NORM_SHARD).transpose(1, 0, 2).reshape(3, D_MODEL)
    f_conv = f_conv.reshape(N_DEV, 2, 3, F_CONV_SHARD).transpose(1, 2, 0, 3).reshape(2, 3, D_FF)
    return gate, b_norm, b_conv, f_conv


def _conv_blocks(f_conv):
    return f_conv.reshape(2, 3, FF_BLOCKS, FF_BLOCK).transpose(0, 2, 1, 3)


def _conv_unblocks(f_conv):
    return f_conv.transpose(1, 0, 2).reshape(3, D_FF)


SMALL_LAYOUT = (("a_norm", (1, D_MODEL)), ("a_w_gate_up", (GATE_RANK, KEY_DIM)), ("a_b_gate", (1, KEY_DIM)), ("a_gn", (1, VALUE_DIM)),
                ("b_norm", (1, D_MODEL)), ("b_conv", (3, D_MODEL)), ("f_norm0", (1, D_MODEL)), ("f_norm1", (1, D_MODEL)),
                ("f_conv0", (3, D_FF)), ("f_conv1", (3, D_FF)), ("final_norm", (1, D_MODEL)))


def _pack_small_grads(g):
    full = dict(g)
    full["a_w_gate_up"] = g["a_w_gate_up"][:GATE_RANK]
    for layer in range(2):
        full[f"f_norm{layer}"] = g["f_norm"][layer]
        full[f"f_conv{layer}"] = _conv_unblocks(g["f_conv"][layer])
    return _pack_rows([full[name] for name, _ in SMALL_LAYOUT])


def _unpack_small_grads(packed):
    pieces = _unpack_rows(packed, [shape for _, shape in SMALL_LAYOUT])
    out = {name: piece.reshape(shape) for (name, shape), piece in zip(SMALL_LAYOUT, pieces)}
    out["f_norm"] = jnp.stack([out["f_norm0"][0], out["f_norm1"][0]])
    out["f_conv"] = jnp.stack([out["f_conv0"], out["f_conv1"]])
    return out


def kernel(x, a_norm, a_w_in, a_w_gate_up, a_b_gate, a_gn, a_w_out, b_norm, b_w_in, b_conv, b_w_out, f_norm, f_w_up, f_conv, f_w_down, final_norm, loss_target, m_a_norm, m_a_w_in, m_a_w_gate_up, m_a_b_gate, m_a_gn, m_a_w_out, m_b_norm, m_b_w_in, m_b_conv, m_b_w_out, m_f_norm, m_f_w_up, m_f_conv, m_f_w_down, m_final_norm, v_a_norm, v_a_w_in, v_a_w_gate_up, v_a_b_gate, v_a_gn, v_a_w_out, v_b_norm, v_b_w_in, v_b_conv, v_b_w_out, v_f_norm, v_f_w_up, v_f_conv, v_f_w_down, v_final_norm):
    my_slot = _slot(*_position())

    transposed = lambda w: jnp.swapaxes(w, 1, 2)
    a_w_in_t, f_w_up_t = transposed(a_w_in), transposed(f_w_up)
    first = _all_gather("weight_gather", [a_w_in_t[0].astype(BF16), a_w_out[0].astype(BF16),
                                          _pack_rows([a_w_gate_up[0], b_norm, b_conv[0], f_conv])])
    gathers, small_shards = {}, first[2]
    later = (("f0", f_w_up_t[0], f_w_down[0]), ("b", b_w_in[0], b_w_out[0]), ("f1", f_w_up_t[1], f_w_down[1]))
    for collective_id, (group, w_in, w_out) in enumerate(later):
        w_in, w_out, small_shards = lax.optimization_barrier((w_in.astype(BF16), w_out.astype(BF16), small_shards))
        gathers[group] = _sequencer_gather(f"gather_{group}", collective_id, [w_in, w_out])
    gate_full, b_norm_full, b_conv_full, f_conv_full = _unpack_small_shards(small_shards)
    a_w_in_full = jnp.pad(first[0].reshape(PROJ_A, D_MODEL), ((0, PROJ_A_PAD - PROJ_A), (0, 0)))
    weights = dict(
        a_norm=a_norm, a_w_gate_up=jnp.pad(gate_full, ((0, GATE_PAD - GATE_RANK), (0, 0))).astype(BF16), a_b_gate=a_b_gate,
        a_gn=a_gn, b_norm=b_norm_full, b_conv=b_conv_full, f_norm=f_norm, f_conv=_conv_blocks(f_conv_full),
        final_norm=final_norm.reshape(1, D_MODEL))

    def fetch(group, after):
        if group == "a":
            return a_w_in_full, first[1].reshape(D_MODEL, D_MODEL)
        w_in, w_out = gathers[group]
        if group == "b":
            return w_in, w_out.reshape(D_MODEL, D_MODEL)
        return w_in, w_out.reshape(FF_BLOCKS, FF_BLOCK, D_MODEL)

    exchanges, pending = {}, []
    exchange_ids = dict(b=3, f0=4, a=5)
    side = lax.axis_index("c").astype(jnp.int32).reshape(1)

    def emit(group, grads, carry):
        d_in, d_out = grads
        if group == "a":
            d_in = d_in[:PROJ_A]
        d_in, d_out = d_in.reshape((N_DEV, -1) + d_in.shape[-1:]), d_out.reshape((N_DEV, -1, D_MODEL))
        received = _pair_swap(f"pair_swap_{group}", [d_in, d_out])
        sums = [_pair_add(f"pair_add_{group}_{i}", part, got, side) for i, (part, got) in enumerate(zip((d_in, d_out), received))]
        carry, *sums = lax.optimization_barrier((carry, *sums))
        pending.extend(sums)
        if group != "f1":
            after = list(exchanges.values())[-1][:1] if exchanges else ()
            exchanges[group] = _sequencer_exchange(f"grads_{group}", exchange_ids[group], list(pending), after)
            pending.clear()
        return carry

    loss, dx, g = _local_step(x[0], loss_target[0], weights, fetch, emit)
    loss = lax.psum(loss, MESH_AXES)
    small_landed = _all_gather("small_grad_gather", [_pack_small_grads(g)])[0]

    (up1, down1, d_b_in, d_b_out), (up0, down0), (d_a_in, d_a_out) = (exchanges[group] for group in ("b", "f0", "a"))
    back = lambda results: tuple(transposed(r) for r in results)
    big = dict(
        b_w_in=_adamw_sum("adam_b_w_in", [d_b_in], b_w_in, m_b_w_in, v_b_w_in),
        b_w_out=_adamw_sum("adam_b_w_out", [d_b_out], b_w_out, m_b_w_out, v_b_w_out),
        f_w_up=back(_adamw_sum("adam_f_w_up", [up0, up1], f_w_up_t, transposed(m_f_w_up), transposed(v_f_w_up))),
        f_w_down=_adamw_sum("adam_f_w_down", [down0, down1], f_w_down, m_f_w_down, v_f_w_down),
        a_w_in=back(_adamw_sum("adam_a_w_in", [d_a_in], a_w_in_t, transposed(m_a_w_in), transposed(v_a_w_in))),
        a_w_out=_adamw_sum("adam_a_w_out", [d_a_out], a_w_out, m_a_w_out, v_a_w_out))
    small_g = _unpack_small_grads(_sum_small(small_landed))
    small_g["a_w_gate_up"] = lax.dynamic_slice_in_dim(small_g["a_w_gate_up"], my_slot * GATE_SHARD, GATE_SHARD, axis=1)
    small_g["b_norm"] = lax.dynamic_slice_in_dim(small_g["b_norm"], my_slot * NORM_SHARD, NORM_SHARD, axis=1)
    small_g["b_conv"] = lax.dynamic_slice_in_dim(small_g["b_conv"], my_slot * NORM_SHARD, NORM_SHARD, axis=1)
    small_g["f_conv"] = lax.dynamic_slice_in_dim(small_g["f_conv"], my_slot * F_CONV_SHARD, F_CONV_SHARD, axis=2)
    small_w = dict(
        a_norm=(a_norm, m_a_norm, v_a_norm), a_w_gate_up=(a_w_gate_up, m_a_w_gate_up, v_a_w_gate_up),
        a_b_gate=(a_b_gate, m_a_b_gate, v_a_b_gate), a_gn=(a_gn, m_a_gn, v_a_gn), b_norm=(b_norm, m_b_norm, v_b_norm),
        b_conv=(b_conv, m_b_conv, v_b_conv), f_norm=(f_norm, m_f_norm, v_f_norm), f_conv=(f_conv, m_f_conv, v_f_conv),
        final_norm=(final_norm, m_final_norm, v_final_norm))
    small = {}
    for name, (w, m, v) in small_w.items():
        flat = (w.shape[-1],) if w.ndim == 1 else w.shape[-2:]
        two_d = (-1, flat[-1])
        grad = small_g[name].reshape(w.shape)
        delta, new_m, new_v = _adamw_small(
            "adam_" + name, grad.reshape(two_d), w.reshape(two_d), m.reshape(two_d), v.reshape(two_d))
        small[name] = (grad, delta.reshape(w.shape), new_m.reshape(w.shape), new_v.reshape(w.shape))

    order = ["a_norm", "a_w_in", "a_w_gate_up", "a_b_gate", "a_gn", "a_w_out", "b_norm", "b_w_in", "b_conv", "b_w_out",
             "f_norm", "f_w_up", "f_conv", "f_w_down", "final_norm"]
    results = {**big, **small}
    outputs = [loss, dx.reshape(1, SEQ, D_MODEL)]
    for kind in range(4):
        outputs += [results[name][kind] for name in order]
    return tuple(outputs)
```

```python
import jax
import jax.numpy as jnp
from jax import lax
from jax.experimental import pallas as pl
from jax.experimental.pallas import tpu as pltpu
from jax.experimental.pallas import tpu_sc as plsc

F32 = jnp.float32
BF16 = jnp.bfloat16

N_DEV = 8
SEQ = 2048
D_MODEL = 1024
CHUNK = 64
N_CHUNKS = SEQ // CHUNK
RMS_EPS = 1e-6
GLA_HEADS = 4
KEY_DIM = 512
VALUE_DIM = 1024
HEAD_K = KEY_DIM // GLA_HEADS
HEAD_V = VALUE_DIM // GLA_HEADS
GATE_RANK = 16
GATE_PAD = 128
GATE_NORMALIZER = 16.0
PROJ_A = 2 * KEY_DIM + 2 * VALUE_DIM + GATE_RANK
PROJ_A_PAD = 2 * KEY_DIM + 2 * VALUE_DIM + GATE_PAD
A_SHARD = PROJ_A // N_DEV
B_SHARD = 3 * D_MODEL // N_DEV
D_FF = 2816
FF_BLOCK = 2 * D_FF // N_DEV
FF_BLOCKS = D_FF // FF_BLOCK
ADAM_LR = 0.001
ADAM_B1 = 0.9
ADAM_B2 = 0.999
ADAM_EPS = 1e-08
ADAM_WD = 0.01
ADAM_STEP = 10
MESH_AXES = ("x", "y", "c")

VMEM_LIMIT = 56 * 1024 * 1024
ROW_CHUNK = 256
HALO = 16


def _params(sem=None, vmem=VMEM_LIMIT):
    return pltpu.CompilerParams(dimension_semantics=sem, vmem_limit_bytes=vmem)


NN = ((1,), (0,))
NT = ((1,), (1,))
TN = ((0,), (0,))


def _matmul(name, a, a_spec, b, b_spec, dims, grid, out_shape, out_spec, k_blocks=None, a_block_cols=None, res=None,
            res_spec=None, transpose_out=False, norm=None):
    has_res = res is not None

    def body(*refs):
        a_ref, b_ref = refs[0], refs[1]
        r_ref = refs[2] if has_res else None

        def product(lhs, rhs):
            return lax.dot_general(lhs.astype(BF16), rhs, (dims, ((), ())), preferred_element_type=F32)

        if k_blocks is None:
            v = product(a_ref[...], b_ref[...])
        else:
            v = None
            for k in range(k_blocks):
                lhs = a_ref[k] if a_block_cols is None else a_ref[:, k * a_block_cols:(k + 1) * a_block_cols]
                p = product(lhs, b_ref[k])
                v = p if v is None else v + p
        if transpose_out:
            v = v.T
        if has_res:
            v = v + r_ref[...]
        if norm is None:
            o_ref = refs[2 + has_res]
            o_ref[...] = v.astype(o_ref.dtype)
            return
        x_ref, g_ref, dxi_ref, dx_ref, dx16_ref, dg_ref = refs[2 + has_res:]
        dx, dg = _norm_bwd_rows(x_ref[...], g_ref[...], v)
        dx = dxi_ref[...] + dx
        dx_ref[...] = dx
        dx16_ref[...] = dx.astype(BF16)

        @pl.when(pl.program_id(0) == 0)
        def _():
            dg_ref[...] = dg

        @pl.when(pl.program_id(0) > 0)
        def _():
            dg_ref[...] += dg

    operands = [a, b] + ([res] if has_res else [])
    in_specs = [a_spec, b_spec] + ([res_spec] if has_res else [])
    semantics = ("parallel",) * len(grid)
    if norm is not None:
        vec = _spec((1, D_MODEL), lambda i: (0, 0))
        operands += list(norm)
        in_specs += [out_spec, vec, out_spec]
        out_shape, out_spec = [_act(dtype=F32), _act(), jax.ShapeDtypeStruct((1, D_MODEL), F32)], [out_spec, out_spec, vec]
        semantics = ("arbitrary",)
    return pl.pallas_call(
        body, name=name, grid=grid, in_specs=in_specs, out_specs=out_spec, out_shape=out_shape,
        compiler_params=_params(semantics),
    )(*operands)


def _resident(shape):
    return pl.BlockSpec(shape, lambda *_: (0,) * len(shape), pipeline_mode=pl.Buffered(1))


TM = 512
N_TM = SEQ // TM
PA_TILE = 640
N_PA = PROJ_A_PAD // PA_TILE
OUT_TILE = 256


def _spec(shape, fn):
    return pl.BlockSpec(shape, fn)


def _act(shape=(SEQ, D_MODEL), dtype=BF16):
    return jax.ShapeDtypeStruct(shape, dtype)


def _proj_rows_nt(name, h, wt, n_tile):
    n = wt.shape[0]
    return _matmul(name, h, _resident((SEQ, D_MODEL)), wt, _spec((n_tile, D_MODEL), lambda j: (j, 0)), NT,
                   (n // n_tile,), _act((SEQ, n)), _spec((SEQ, n_tile), lambda j: (0, j)))


def _proj_blocks_nt(name, h, wt_blocks):
    nb, n, _ = wt_blocks.shape
    return _matmul(name, h, _resident((SEQ, D_MODEL)), wt_blocks, _spec((None, n, D_MODEL), lambda j: (j, 0, 0)),
                   NT, (nb,), _act((nb, SEQ, n)), _spec((None, SEQ, n), lambda j: (j, 0, 0)))


def _proj_cols_nn(name, h, w_blocks):
    nb, _, n = w_blocks.shape
    return _matmul(name, h, _resident((SEQ, D_MODEL)), w_blocks, _spec((None, D_MODEL, n), lambda j: (j, 0, 0)),
                   NN, (nb,), _act((SEQ, nb * n)), _spec((SEQ, n), lambda j: (0, j)))


def _square(name, a, w, dims, x=None):
    row = _spec((TM, D_MODEL), lambda i: (i, 0))
    return _matmul(name, a, row, w, _resident((D_MODEL, D_MODEL)), dims, (N_TM,),
                   _act(dtype=F32 if x is not None else BF16), row, res=x, res_spec=row if x is not None else None)


def _sum_blocks_nn(name, a_blocks, w_blocks, x=None, norm=None):
    nb, _, n = a_blocks.shape
    row = _spec((TM, D_MODEL), lambda i: (i, 0))
    return _matmul(name, a_blocks, _spec((nb, TM, n), lambda i: (0, i, 0)), w_blocks, _resident((nb, n, D_MODEL)),
                   NN, (N_TM,), _act(dtype=F32), row, k_blocks=nb, res=x, res_spec=row if x is not None else None, norm=norm)


def _sum_cols_nt(name, d, w_blocks, norm=None):
    nb, _, n = w_blocks.shape
    return _matmul(name, d, _spec((TM, nb * n), lambda i: (i, 0)), w_blocks, _resident((nb, D_MODEL, n)), NT,
                   (N_TM,), _act(dtype=F32), _spec((TM, D_MODEL), lambda i: (i, 0)), k_blocks=nb, a_block_cols=n, norm=norm)


def _wide_nn(name, d, wt, norm=None):
    n = wt.shape[0]
    return _matmul(name, d, _spec((TM, n), lambda i: (i, 0)), wt, _resident((n, D_MODEL)), NN, (N_TM,),
                   _act(dtype=F32), _spec((TM, D_MODEL), lambda i: (i, 0)), norm=norm)


def _wgrad_blocks_tn(name, d_blocks, h):
    nb, _, n = d_blocks.shape
    return _matmul(name, d_blocks, _spec((None, SEQ, n), lambda j: (j, 0, 0)), h, _resident((SEQ, D_MODEL)), TN,
                   (nb,), _act((nb, n, D_MODEL)), _spec((None, n, D_MODEL), lambda j: (j, 0, 0)))


def _wgrad_cols_tn(name, d, n_tile, h):
    n = d.shape[1]
    return _matmul(name, d, _spec((SEQ, n_tile), lambda j: (0, j)), h, _resident((SEQ, D_MODEL)), TN,
                   (n // n_tile,), _act((n, D_MODEL)), _spec((n_tile, D_MODEL), lambda j: (j, 0)))


def _wgrad_cols_transposed_tn(name, h, d, n_tile):
    nb = d.shape[1] // n_tile
    return _matmul(name, d, _spec((SEQ, n_tile), lambda j: (0, j)), h, _resident((SEQ, D_MODEL)), TN, (nb,),
                   _act((nb, D_MODEL, n_tile)), _spec((None, D_MODEL, n_tile), lambda j: (j, 0, 0)), transpose_out=True)


NORM_ROWS = 512


def _rstd(x):
    return lax.rsqrt(jnp.mean(x * x, axis=-1, keepdims=True) + RMS_EPS)


def _norm_fwd(name, x, gamma):
    def body(x_ref, g_ref, h_ref):
        x = x_ref[...]
        h_ref[...] = (x * _rstd(x) * g_ref[...]).astype(BF16)

    row = _spec((NORM_ROWS, D_MODEL), lambda i: (i, 0))
    return pl.pallas_call(
        body, name=name, grid=(SEQ // NORM_ROWS,), in_specs=[row, _spec((1, D_MODEL), lambda i: (0, 0))], out_specs=row,
        out_shape=jax.ShapeDtypeStruct((SEQ, D_MODEL), BF16), compiler_params=_params(("parallel",)),
    )(x, gamma)


def _norm_bwd_rows(x, gamma, dh):
    r = _rstd(x)
    xh = x * r
    dxh = dh * gamma
    dx = r * (dxh - xh * jnp.mean(dxh * xh, axis=-1, keepdims=True))
    return dx, jnp.sum(dh * xh, axis=0, keepdims=True)


def _loss_head(x, gamma, target):
    def body(x_ref, g_ref, t_ref, loss_ref, dx_ref, dx16_ref, dg_ref):
        x = x_ref[...]
        gamma = g_ref[...]
        err = x * _rstd(x) * gamma - t_ref[...]
        dy = err * (1.0 / D_MODEL)
        dx, dg = _norm_bwd_rows(x, gamma, dy)
        dx_ref[...] = dx
        dx16_ref[...] = dx.astype(BF16)
        part = 0.5 * jnp.sum(jnp.sum(err * err, axis=-1, keepdims=True) * (1.0 / D_MODEL), axis=0, keepdims=True)
        part = jnp.broadcast_to(part, loss_ref.shape)

        @pl.when(pl.program_id(0) == 0)
        def _():
            dg_ref[...] = dg
            loss_ref[...] = part

        @pl.when(pl.program_id(0) > 0)
        def _():
            dg_ref[...] += dg
            loss_ref[...] += part

    row = _spec((NORM_ROWS, D_MODEL), lambda i: (i, 0))
    vec = _spec((1, D_MODEL), lambda i: (0, 0))
    return pl.pallas_call(
        body, name="loss_head", grid=(SEQ // NORM_ROWS,), in_specs=[row, vec, row],
        out_specs=[_spec((1, 128), lambda i: (0, 0)), row, row, vec],
        out_shape=[jax.ShapeDtypeStruct((1, 128), F32), _act(dtype=F32), _act(), jax.ShapeDtypeStruct((1, D_MODEL), F32)],
        compiler_params=_params(("arbitrary",)),
    )(x, gamma, target)


def _sigmoid(x):
    return 1.0 / (1.0 + jnp.exp(-x))


def _rows(ref, c):
    return ref[pl.ds(pl.multiple_of(c * ROW_CHUNK, ROW_CHUNK), ROW_CHUNK), :].astype(F32)


def _rows_before(ref, c):
    start = pl.multiple_of(jnp.maximum(c * ROW_CHUNK - HALO, 0), HALO)
    rows = ref[pl.ds(start, HALO), :].astype(F32)
    return jnp.where(c > 0, rows, 0.0)


def _rows_after(ref, c, n_chunks):
    start = pl.multiple_of(jnp.minimum((c + 1) * ROW_CHUNK, SEQ - HALO), HALO)
    rows = ref[pl.ds(start, HALO), :].astype(F32)
    return jnp.where(c < n_chunks - 1, rows, 0.0)


def _shift_down(z, before, n):
    row = lax.broadcasted_iota(jnp.int32, z.shape, 0)
    out = pltpu.roll(z, n, 0)
    for r in range(n):
        out = jnp.where(row == r, before[HALO - n + r:HALO - n + r + 1, :], out)
    return out


def _shift_up(z, after, n):
    rows = z.shape[0]
    row = lax.broadcasted_iota(jnp.int32, z.shape, 0)
    out = pltpu.roll(z, rows - n, 0)
    for r in range(n):
        out = jnp.where(row == rows - n + r, after[r:r + 1, :], out)
    return out


def _conv_rows(z, before, w):
    z1 = _shift_down(z, before, 1)
    z2 = _shift_down(z, before, 2)
    return w[2:3, :] * z + w[1:2, :] * z1 + w[0:1, :] * z2, z1, z2


def _conv_t_rows(dy, after, w):
    return w[2:3, :] * dy + w[1:2, :] * _shift_up(dy, after, 1) + w[0:1, :] * _shift_up(dy, after, 2)


N_ROW_CHUNKS = SEQ // ROW_CHUNK


def _ffn_mid_fwd(name, gu, conv_w):
    def body(gu_ref, w_ref, a_ref):
        w = w_ref[...]

        def chunk(c, carry):
            g = _rows(gu_ref.at[0], c)
            u = _rows(gu_ref.at[1], c)
            gc, _, _ = _conv_rows(g, _rows_before(gu_ref.at[0], c), w)
            a_ref[pl.ds(pl.multiple_of(c * ROW_CHUNK, ROW_CHUNK), ROW_CHUNK), :] = (gc * _sigmoid(gc) * u).astype(BF16)
            return carry

        lax.fori_loop(0, N_ROW_CHUNKS, chunk, 0)

    return pl.pallas_call(
        body, name=name, grid=(FF_BLOCKS,),
        in_specs=[_spec((2, None, SEQ, FF_BLOCK), lambda j: (0, j, 0, 0)), _spec((None, 3, FF_BLOCK), lambda j: (j, 0, 0))],
        out_specs=_spec((None, SEQ, FF_BLOCK), lambda j: (j, 0, 0)),
        out_shape=jax.ShapeDtypeStruct((FF_BLOCKS, SEQ, FF_BLOCK), BF16), compiler_params=_params(("parallel",)),
    )(gu, conv_w)


def _ffn_mid_bwd(name, gu, conv_w, da):
    def body(gu_ref, w_ref, da_ref, dgu_ref, dw_ref, dgc_ref):
        w = w_ref[...]

        def first(c, acc):
            g = _rows(gu_ref.at[0], c)
            u = _rows(gu_ref.at[1], c)
            d = _rows(da_ref, c)
            gc, g1, g2 = _conv_rows(g, _rows_before(gu_ref.at[0], c), w)
            sg = _sigmoid(gc)
            rows = pl.ds(pl.multiple_of(c * ROW_CHUNK, ROW_CHUNK), ROW_CHUNK)
            dgu_ref[1, rows, :] = (d * gc * sg).astype(BF16)
            dgc = d * u * (sg * (1.0 + gc * (1.0 - sg)))
            dgc_ref[rows, :] = dgc
            return (acc[0] + jnp.sum(dgc * g2, axis=0, keepdims=True), acc[1] + jnp.sum(dgc * g1, axis=0, keepdims=True),
                    acc[2] + jnp.sum(dgc * g, axis=0, keepdims=True))

        zero = jnp.zeros((1, FF_BLOCK), F32)
        acc = lax.fori_loop(0, N_ROW_CHUNKS, first, (zero, zero, zero))
        for r in range(3):
            dw_ref[r:r + 1, :] = acc[r]

        def second(c, carry):
            dgc = _rows(dgc_ref, c)
            dg = _conv_t_rows(dgc, _rows_after(dgc_ref, c, N_ROW_CHUNKS), w)
            dgu_ref[0, pl.ds(pl.multiple_of(c * ROW_CHUNK, ROW_CHUNK), ROW_CHUNK), :] = dg.astype(BF16)
            return carry

        lax.fori_loop(0, N_ROW_CHUNKS, second, 0)

    pair = _spec((2, None, SEQ, FF_BLOCK), lambda j: (0, j, 0, 0))
    wspec = _spec((None, 3, FF_BLOCK), lambda j: (j, 0, 0))
    return pl.pallas_call(
        body, name=name, grid=(FF_BLOCKS,),
        in_specs=[pair, wspec, _spec((None, SEQ, FF_BLOCK), lambda j: (j, 0, 0))], out_specs=[pair, wspec],
        out_shape=[jax.ShapeDtypeStruct((2, FF_BLOCKS, SEQ, FF_BLOCK), BF16), jax.ShapeDtypeStruct((FF_BLOCKS, 3, FF_BLOCK), F32)],
        scratch_shapes=[pltpu.VMEM((SEQ, FF_BLOCK), F32)], compiler_params=_params(("parallel",)),
    )(gu, conv_w, da)


SC_COLS = 256
N_SC = D_MODEL // SC_COLS


def _sc_specs():
    return [_spec((SEQ, SC_COLS), lambda j, part=part: (0, part * N_SC + j)) for part in range(3)]


def _sc_mid_fwd(p, conv_w):
    def body(b_ref, c_ref, h_ref, w_ref, y_ref):
        w = w_ref[...]

        def chunk(c, carry):
            z = _rows(c_ref, c) * _rows(h_ref, c)
            before = _rows_before(c_ref, c) * _rows_before(h_ref, c)
            zc, _, _ = _conv_rows(z, before, w)
            y_ref[pl.ds(pl.multiple_of(c * ROW_CHUNK, ROW_CHUNK), ROW_CHUNK), :] = (_rows(b_ref, c) * zc).astype(BF16)
            return carry

        lax.fori_loop(0, N_ROW_CHUNKS, chunk, 0)

    col = _spec((SEQ, SC_COLS), lambda j: (0, j))
    return pl.pallas_call(
        body, name="sc_mid_fwd", grid=(N_SC,), in_specs=_sc_specs() + [_spec((3, SC_COLS), lambda j: (0, j))], out_specs=col,
        out_shape=jax.ShapeDtypeStruct((SEQ, D_MODEL), BF16), compiler_params=_params(("parallel",)),
    )(p, p, p, conv_w)


def _sc_mid_bwd(p, conv_w, dy):
    def body(b_ref, c_ref, h_ref, w_ref, dy_ref, db_ref, dc_ref, dh_ref, dw_ref, dzc_ref):
        w = w_ref[...]

        def first(c, acc):
            z = _rows(c_ref, c) * _rows(h_ref, c)
            before = _rows_before(c_ref, c) * _rows_before(h_ref, c)
            zc, z1, z2 = _conv_rows(z, before, w)
            d = _rows(dy_ref, c)
            rows = pl.ds(pl.multiple_of(c * ROW_CHUNK, ROW_CHUNK), ROW_CHUNK)
            db_ref[rows, :] = (d * zc).astype(BF16)
            dzc = d * _rows(b_ref, c)
            dzc_ref[rows, :] = dzc
            return (acc[0] + jnp.sum(dzc * z2, axis=0, keepdims=True), acc[1] + jnp.sum(dzc * z1, axis=0, keepdims=True),
                    acc[2] + jnp.sum(dzc * z, axis=0, keepdims=True))

        zero = jnp.zeros((1, SC_COLS), F32)
        acc = lax.fori_loop(0, N_ROW_CHUNKS, first, (zero, zero, zero))
        for r in range(3):
            dw_ref[r:r + 1, :] = acc[r]

        def second(c, carry):
            dz = _conv_t_rows(_rows(dzc_ref, c), _rows_after(dzc_ref, c, N_ROW_CHUNKS), w)
            rows = pl.ds(pl.multiple_of(c * ROW_CHUNK, ROW_CHUNK), ROW_CHUNK)
            dc_ref[rows, :] = (dz * _rows(h_ref, c)).astype(BF16)
            dh_ref[rows, :] = (dz * _rows(c_ref, c)).astype(BF16)
            return carry

        lax.fori_loop(0, N_ROW_CHUNKS, second, 0)

    col = _spec((SEQ, SC_COLS), lambda j: (0, j))
    wspec = _spec((3, SC_COLS), lambda j: (0, j))
    act = jax.ShapeDtypeStruct((SEQ, D_MODEL), BF16)
    return pl.pallas_call(
        body, name="sc_mid_bwd", grid=(N_SC,), in_specs=_sc_specs() + [wspec, col], out_specs=[col, col, col, wspec],
        out_shape=[act, act, act, jax.ShapeDtypeStruct((3, D_MODEL), F32)],
        scratch_shapes=[pltpu.VMEM((SEQ, SC_COLS), F32)], compiler_params=_params(("parallel",)),
    )(p, p, p, conv_w, dy)


GLA_GROUP = 4
GLA_ROWS = GLA_GROUP * CHUNK
N_GROUPS = N_CHUNKS // GLA_GROUP
Q0, K0, V0, R0, G0 = 0, KEY_DIM, 2 * KEY_DIM, 2 * KEY_DIM + VALUE_DIM, 2 * KEY_DIM + 2 * VALUE_DIM


def _tri(strict):
    r = lax.broadcasted_iota(jnp.int32, (CHUNK, CHUNK), 0)
    c = lax.broadcasted_iota(jnp.int32, (CHUNK, CHUNK), 1)
    return jnp.where(c < r if strict else c <= r, 1.0, 0.0).astype(F32)


def _cumsum_rows(tri, x):
    return jnp.dot(tri, x, preferred_element_type=F32, precision=lax.Precision.HIGHEST)


def _gate_logits(gl, wgu, b_gate):
    return jnp.dot(gl, wgu, preferred_element_type=F32) + b_gate


def _log_decay(logits):
    return (jnp.minimum(logits, 0.0) - jnp.log(1.0 + jnp.exp(-jnp.abs(logits)))) * (1.0 / GATE_NORMALIZER)


def _head(x, h, width):
    return x[:, h * width:(h + 1) * width]


def _gla_fwd(proj, wgu, b_gate, gn):
    def body(p_ref, wgu_ref, b_ref, gn_ref, o_ref, og_ref, st_ref, state):
        @pl.when(pl.program_id(0) == 0)
        def _():
            state[...] = jnp.zeros_like(state)

        tri = _tri(False)
        la = _log_decay(_gate_logits(p_ref[:, G0:G0 + GATE_PAD], wgu_ref[...], b_ref[...]))
        for c in range(GLA_GROUP):
            rows = slice(c * CHUNK, (c + 1) * CHUNK)
            cum = _cumsum_rows(tri, la[rows])
            tot = cum[CHUNK - 1:CHUNK, :]
            kd = (p_ref[rows, K0:K0 + KEY_DIM].astype(F32) * jnp.exp(tot - cum)).astype(BF16)
            decay = jnp.exp(tot)
            q = (p_ref[rows, Q0:Q0 + KEY_DIM].astype(F32) * (HEAD_K ** -0.5)).astype(BF16)
            v = p_ref[rows, V0:V0 + VALUE_DIM]
            for h in range(GLA_HEADS):
                upd = lax.dot_general(_head(v, h, HEAD_V), _head(kd, h, HEAD_K), (TN, ((), ())), preferred_element_type=F32)
                s = state[h] * _head(decay, h, HEAD_K) + upd
                state[h] = s
                st_ref[c, h] = s
                o_ref[rows, h * HEAD_V:(h + 1) * HEAD_V] = lax.dot_general(
                    _head(q, h, HEAD_K), s.astype(BF16), (NT, ((), ())), preferred_element_type=F32)
        r = p_ref[:, R0:R0 + VALUE_DIM].astype(F32)
        gate = r * _sigmoid(r) * gn_ref[...]
        for h in range(GLA_HEADS):
            cols = slice(h * HEAD_V, (h + 1) * HEAD_V)
            o = o_ref[:, cols]
            og_ref[:, cols] = (o * _rstd(o) * gate[:, cols]).astype(BF16)

    rows = _spec((GLA_ROWS, VALUE_DIM), lambda i: (i, 0))
    const = lambda shape: _spec(shape, lambda i: (0,) * len(shape))
    return pl.pallas_call(
        body, name="gla_fwd", grid=(N_GROUPS,),
        in_specs=[_spec((GLA_ROWS, PROJ_A_PAD), lambda i: (i, 0)), const((GATE_PAD, KEY_DIM)), const((1, KEY_DIM)),
                  const((1, VALUE_DIM))],
        out_specs=[rows, rows, _spec((GLA_GROUP, GLA_HEADS, HEAD_V, HEAD_K), lambda i: (i, 0, 0, 0))],
        out_shape=[jax.ShapeDtypeStruct((SEQ, VALUE_DIM), F32), jax.ShapeDtypeStruct((SEQ, VALUE_DIM), BF16),
                   jax.ShapeDtypeStruct((N_CHUNKS, GLA_HEADS, HEAD_V, HEAD_K), F32)],
        scratch_shapes=[pltpu.VMEM((GLA_HEADS, HEAD_V, HEAD_K), F32)], compiler_params=_params(("arbitrary",)),
    )(proj, wgu, b_gate, gn)


def _gla_bwd(proj, wgu, b_gate, gn, o, states, dog):
    last = N_GROUPS - 1

    def body(p_ref, wgu_ref, b_ref, gn_ref, o_ref, st_ref, stp_ref, dog_ref, dp_ref, dwgu_ref, db_ref, dgn_ref, carry, do_buf):
        step = pl.program_id(0)

        @pl.when(step == 0)
        def _():
            carry[...] = jnp.zeros_like(carry)

        r = p_ref[:, R0:R0 + VALUE_DIM].astype(F32)
        sr = _sigmoid(r)
        silu = r * sr
        gn_row = gn_ref[...]
        dog_rows = dog_ref[...].astype(F32)
        dn = dog_rows * silu
        dgn_cols = []
        for h in range(GLA_HEADS):
            cols = slice(h * HEAD_V, (h + 1) * HEAD_V)
            oh = o_ref[:, cols]
            rs = _rstd(oh)
            ohat = oh * rs
            dn_h = dn[:, cols]
            dgn_cols.append(jnp.sum(dn_h * ohat, axis=0, keepdims=True))
            dohat = dn_h * gn_row[:, cols]
            do_buf[:, cols] = rs * (dohat - ohat * jnp.mean(dohat * ohat, axis=-1, keepdims=True))
            n_h = ohat * gn_row[:, cols]
            dp_ref[:, R0 + h * HEAD_V:R0 + (h + 1) * HEAD_V] = (
                dog_rows[:, cols] * n_h * (sr[:, cols] * (1.0 + r[:, cols] * (1.0 - sr[:, cols])))).astype(BF16)
        dgn = jnp.concatenate(dgn_cols, axis=1)

        tri = _tri(False)
        tri_strict = _tri(True)
        gl = p_ref[:, G0:G0 + GATE_PAD]
        logits = _gate_logits(gl, wgu_ref[...], b_ref[...])
        la = _log_decay(logits)
        dlogit_rows = []
        for c in reversed(range(GLA_GROUP)):
            rows = slice(c * CHUNK, (c + 1) * CHUNK)
            cum = _cumsum_rows(tri, la[rows])
            tot = cum[CHUNK - 1:CHUNK, :]
            fade = jnp.exp(tot - cum)
            k = p_ref[rows, K0:K0 + KEY_DIM].astype(F32)
            kd32 = k * fade
            kd = kd32.astype(BF16)
            decay = jnp.exp(tot)
            q = (p_ref[rows, Q0:Q0 + KEY_DIM].astype(F32) * (HEAD_K ** -0.5)).astype(BF16)
            v = p_ref[rows, V0:V0 + VALUE_DIM]
            do = do_buf[rows, :].astype(BF16)
            dkd_cols, ddecay_cols = [], []
            for h in range(GLA_HEADS):
                do_h = _head(do, h, HEAD_V)
                s = st_ref[c, h]
                dq = jnp.dot(do_h, s.astype(BF16), preferred_element_type=F32) * (HEAD_K ** -0.5)
                dp_ref[rows, Q0 + h * HEAD_K:Q0 + (h + 1) * HEAD_K] = dq.astype(BF16)
                g = carry[h] + lax.dot_general(do_h, _head(q, h, HEAD_K), (TN, ((), ())), preferred_element_type=F32)
                g16 = g.astype(BF16)
                dkd_cols.append(jnp.dot(_head(v, h, HEAD_V), g16, preferred_element_type=F32))
                dv = lax.dot_general(_head(kd, h, HEAD_K), g16, (NT, ((), ())), preferred_element_type=F32)
                dp_ref[rows, V0 + h * HEAD_V:V0 + (h + 1) * HEAD_V] = dv.astype(BF16)
                if c > 0:
                    s_prev = st_ref[c - 1, h]
                else:
                    s_prev = jnp.where(step < last, stp_ref[0, h], 0.0)
                ddecay_cols.append(jnp.sum(g * s_prev, axis=0, keepdims=True))
                carry[h] = g * _head(decay, h, HEAD_K)
            dkd = jnp.concatenate(dkd_cols, axis=1)
            ddecay = jnp.concatenate(ddecay_cols, axis=1)
            dp_ref[rows, K0:K0 + KEY_DIM] = (dkd * fade).astype(BF16)
            e = dkd * kd32
            dla = ddecay * decay + _cumsum_rows(tri_strict, e)
            dlogit_rows.append(dla * (1.0 / GATE_NORMALIZER) * (1.0 - _sigmoid(logits[rows])))
        dlogit = jnp.concatenate(dlogit_rows[::-1], axis=0)
        dlogit16 = dlogit.astype(BF16)
        dp_ref[:, G0:G0 + GATE_PAD] = lax.dot_general(
            dlogit16, wgu_ref[...], (NT, ((), ())), preferred_element_type=F32).astype(BF16)
        dwgu = lax.dot_general(gl, dlogit16, (TN, ((), ())), preferred_element_type=F32)
        db = jnp.sum(dlogit, axis=0, keepdims=True)

        @pl.when(step == 0)
        def _():
            dwgu_ref[...] = dwgu
            db_ref[...] = db
            dgn_ref[...] = dgn

        @pl.when(step > 0)
        def _():
            dwgu_ref[...] += dwgu
            db_ref[...] += db
            dgn_ref[...] += dgn

    rev = lambda i: (last - i, 0)
    rows = _spec((GLA_ROWS, VALUE_DIM), rev)
    const = lambda shape: _spec(shape, lambda i: (0,) * len(shape))
    st_shape = (GLA_HEADS, HEAD_V, HEAD_K)
    return pl.pallas_call(
        body, name="gla_bwd", grid=(N_GROUPS,),
        in_specs=[_spec((GLA_ROWS, PROJ_A_PAD), rev), const((GATE_PAD, KEY_DIM)), const((1, KEY_DIM)), const((1, VALUE_DIM)),
                  rows, _spec((GLA_GROUP,) + st_shape, lambda i: (last - i, 0, 0, 0)),
                  _spec((1,) + st_shape, lambda i: (jnp.maximum((last - i) * GLA_GROUP - 1, 0), 0, 0, 0)), rows],
        out_specs=[_spec((GLA_ROWS, PROJ_A_PAD), rev), const((GATE_PAD, KEY_DIM)), const((1, KEY_DIM)), const((1, VALUE_DIM))],
        out_shape=[jax.ShapeDtypeStruct((SEQ, PROJ_A_PAD), BF16), jax.ShapeDtypeStruct((GATE_PAD, KEY_DIM), F32),
                   jax.ShapeDtypeStruct((1, KEY_DIM), F32), jax.ShapeDtypeStruct((1, VALUE_DIM), F32)],
        scratch_shapes=[pltpu.VMEM(st_shape, F32), pltpu.VMEM((GLA_ROWS, VALUE_DIM), F32)],
        compiler_params=_params(("arbitrary",)),
    )(proj, wgu, b_gate, gn, o, states, states, dog)


def _ffn_fwd(tag, x, gamma, w_up_t, conv_w, w_down):
    h = _norm_fwd(f"ffn{tag}_norm", x, gamma)
    gu = _proj_blocks_nt(f"ffn{tag}_up", h, w_up_t).reshape(2, FF_BLOCKS, SEQ, FF_BLOCK)
    a = _ffn_mid_fwd(f"ffn{tag}_mid", gu, conv_w)
    return _sum_blocks_nn(f"ffn{tag}_down", a, w_down, x), (h, gu, a)


def _ffn_bwd(tag, x, gamma, w_up_t, conv_w, w_down, saved, dx, dx16):
    h, gu, a = saved
    da = _proj_blocks_nt(f"ffn{tag}_da", dx16, w_down)
    d_w_down = _wgrad_blocks_tn(f"ffn{tag}_dwdown", a, dx16)
    dgu, d_conv = _ffn_mid_bwd(f"ffn{tag}_mid_bwd", gu, conv_w, da)
    dgu = dgu.reshape(2 * FF_BLOCKS, SEQ, FF_BLOCK)
    dx, dx16, d_gamma = _sum_blocks_nn(f"ffn{tag}_dh", dgu, w_up_t, norm=(x, gamma, dx))
    d_w_up_t = _wgrad_blocks_tn(f"ffn{tag}_dwup", dgu, h)
    return dx, dx16, d_gamma, d_w_up_t, d_conv, d_w_down


def _local_step(x, target, w, fetch=None, emit=None):
    if fetch is None:
        local = dict(a=(w.get("a_w_in"), w.get("a_w_out")), b=(w.get("b_w_in"), w.get("b_w_out")))
        for layer in range(2):
            local[f"f{layer}"] = (w["f_w_up"][layer], w["f_w_down"][layer]) if "f_w_up" in w else None
        fetch = lambda group, after: local[group]
    if emit is None:
        emit = lambda group, grads, dx: dx
    f_norm = (w["f_norm"][0:1], w["f_norm"][1:2])

    x0 = x
    a_w_in, a_w_out = fetch("a", x0)
    h0 = _norm_fwd("a_norm", x0, w["a_norm"])
    proj = _proj_rows_nt("a_in", h0, a_w_in, PA_TILE)
    o, og, states = _gla_fwd(proj, w["a_w_gate_up"], w["a_b_gate"], w["a_gn"])
    x1 = _square("a_out", og, a_w_out, NN, x0)
    up0, down0 = fetch("f0", x1)
    x2, ffn0 = _ffn_fwd(0, x1, f_norm[0], up0, w["f_conv"][0], down0)
    b_w_in, b_w_out = fetch("b", x2)
    h2 = _norm_fwd("b_norm", x2, w["b_norm"])
    p = _proj_cols_nn("b_in", h2, b_w_in)
    y = _sc_mid_fwd(p, w["b_conv"])
    x3 = _square("b_out", y, b_w_out, NN, x2)
    up1, down1 = fetch("f1", x3)
    x4, ffn1 = _ffn_fwd(1, x3, f_norm[1], up1, w["f_conv"][1], down1)
    loss, dx, dx16, d_final_norm = _loss_head(x4, w["final_norm"], target)

    dx, dx16, d_f_norm1, d_up1, d_fconv1, d_down1 = _ffn_bwd(1, x3, f_norm[1], up1, w["f_conv"][1], down1, ffn1, dx, dx16)
    dx16 = emit("f1", (d_up1, d_down1), dx16)

    dy = _square("b_dy", dx16, b_w_out, NT)
    d_b_w_out = _wgrad_cols_tn("b_dwout", y, OUT_TILE, dx16)
    db, dc, dhh, d_b_conv = _sc_mid_bwd(p, w["b_conv"], dy)
    dp = jnp.concatenate([db, dc, dhh], axis=1)
    dx, dx16, d_b_norm = _sum_cols_nt("b_dh", dp, b_w_in, norm=(x2, w["b_norm"], dx))
    d_b_w_in = _wgrad_cols_transposed_tn("b_dwin", h2, dp, B_SHARD)
    dx16 = emit("b", (d_b_w_in, d_b_w_out), dx16)

    dx, dx16, d_f_norm0, d_up0, d_fconv0, d_down0 = _ffn_bwd(0, x1, f_norm[0], up0, w["f_conv"][0], down0, ffn0, dx, dx16)
    dx16 = emit("f0", (d_up0, d_down0), dx16)

    dog = _square("a_dog", dx16, a_w_out, NT)
    d_a_w_out = _wgrad_cols_tn("a_dwout", og, OUT_TILE, dx16)
    dproj, d_wgu, d_b_gate, d_gn = _gla_bwd(proj, w["a_w_gate_up"], w["a_b_gate"], w["a_gn"], o, states, dog)
    d_a_w_in = _wgrad_cols_tn("a_dwin", dproj, PA_TILE, h0)
    dproj = emit("a", (d_a_w_in, d_a_w_out), dproj)
    dx, _, d_a_norm = _wide_nn("a_dh", dproj, a_w_in, norm=(x0, w["a_norm"], dx))

    grads = dict(
        a_norm=d_a_norm, a_w_in=d_a_w_in, a_w_gate_up=d_wgu, a_b_gate=d_b_gate, a_gn=d_gn, a_w_out=d_a_w_out,
        b_norm=d_b_norm, b_w_in=d_b_w_in, b_conv=d_b_conv, b_w_out=d_b_w_out,
        f_norm=(d_f_norm0, d_f_norm1), f_w_up=(d_up0, d_up1), f_conv=(d_fconv0, d_fconv1), f_w_down=(d_down0, d_down1),
        final_norm=d_final_norm)
    return loss[0, 0], dx, grads


MESH_ID = pl.DeviceIdType.MESH
ANY = pl.BlockSpec(memory_space=pl.ANY)
N_PEERS = N_DEV - 1


def _position():
    return lax.axis_index("x"), lax.axis_index("y"), lax.axis_index("c")


def _slot(px, py, pc):
    return 4 * px + 2 * py + pc


def _all_gather(name, shards):
    n = len(shards)

    def body(*refs):
        ins, outs = refs[:n], refs[n:2 * n]
        send_sems, recv_sems, local_sems = refs[2 * n:]
        x, y, c = _position()
        me, sibling = (x, y, c), (x, y, 1 - c)
        chips = [(1 - x, y), (x, 1 - y), (1 - x, 1 - y)]

        def copy(t, k, block, to, from_input=False):
            dst = outs[t].at[_slot(*block)]
            return pltpu.make_async_remote_copy(
                src_ref=ins[t] if from_input else dst, dst_ref=dst, send_sem=send_sems.at[t, k], recv_sem=recv_sems.at[t, k],
                device_id=to, device_id_type=MESH_ID)

        mine = [pltpu.make_async_copy(ins[t], outs[t].at[_slot(*me)], local_sems.at[t]) for t in range(n)]
        for cp in mine:
            cp.start()
        first = []
        for t in range(n):
            first.append(copy(t, 0, me, sibling, True))
            first += [copy(t, 1 + j, me, (*chip, c), True) for j, chip in enumerate(chips)]
        for cp in first:
            cp.start()
        passed = []
        for t in range(n):
            for j, chip in enumerate(chips):
                copy(t, 1 + j, (*chip, c), me).wait_recv()
                fwd = copy(t, 4 + j, (*chip, c), sibling)
                fwd.start()
                passed.append(fwd)
        for t in range(n):
            copy(t, 0, sibling, me).wait_recv()
            for j, chip in enumerate(chips):
                copy(t, 4 + j, (*chip, 1 - c), me).wait_recv()
        for cp in first + passed:
            cp.wait_send()
        for cp in mine:
            cp.wait()

    return pl.pallas_call(
        body, name=name, in_specs=[ANY] * n, out_specs=[ANY] * n,
        out_shape=[jax.ShapeDtypeStruct((N_DEV,) + s.shape, s.dtype) for s in shards],
        scratch_shapes=[pltpu.SemaphoreType.DMA((n, N_PEERS)), pltpu.SemaphoreType.DMA((n, N_PEERS)), pltpu.SemaphoreType.DMA((n,))],
    )(*shards)


SIBLING_AND_SAME_CORE = (1, 2, 4, 6)
SAME_CORE = (2, 4, 6)


def _flip(x, y, c, k):
    return x ^ (k >> 2), y ^ ((k >> 1) & 1), c ^ (k & 1)


N_CHIPS = N_DEV // 2


def _chip(px, py):
    return 2 * px + py


def _pair_swap(name, parts):
    n = len(parts)

    def body(*refs):
        ins, outs = refs[:n], refs[n:2 * n]
        send_sems, recv_sems = refs[2 * n:]
        x, y, c = _position()
        sibling = (x, y, 1 - c)
        sent = []
        for t in range(n):
            for q in range(N_CHIPS):
                sent.append(pltpu.make_async_remote_copy(
                    src_ref=ins[t].at[2 * q + 1 - c], dst_ref=outs[t].at[q], send_sem=send_sems.at[t, q],
                    recv_sem=recv_sems.at[t, q], device_id=sibling, device_id_type=MESH_ID))
        for cp in sent:
            cp.start()
        for t in range(n):
            for q in range(N_CHIPS):
                landed = outs[t].at[q]
                pltpu.make_async_remote_copy(
                    src_ref=landed, dst_ref=landed, send_sem=send_sems.at[t, q], recv_sem=recv_sems.at[t, q],
                    device_id=sibling, device_id_type=MESH_ID).wait_recv()
        for cp in sent:
            cp.wait_send()

    sems = pltpu.SemaphoreType.DMA((n, N_CHIPS))
    return pl.pallas_call(
        body, name=name, in_specs=[ANY] * n, out_specs=[ANY] * n,
        out_shape=[jax.ShapeDtypeStruct((N_CHIPS,) + p.shape[1:], p.dtype) for p in parts], scratch_shapes=[sems, sems],
    )(*parts)


PAIR_ROWS = 1024


def _pair_add(name, part, received, side):
    _, rows, cols = part.shape
    tiles = [t for t in range(PAIR_ROWS, 0, -BF16_ROWS) if rows % t == 0]
    tr = tiles[0] if tiles else rows

    def body(side_ref, p_ref, r_ref, o_ref):
        o_ref[...] = (p_ref[...].astype(F32) + r_ref[...].astype(F32)).astype(BF16)

    tile = _spec((None, tr, cols), lambda q, i, side_ref: (q, i, 0))
    return pl.pallas_call(
        body, name=name,
        grid_spec=pltpu.PrefetchScalarGridSpec(
            num_scalar_prefetch=1, grid=(N_CHIPS, rows // tr),
            in_specs=[_spec((None, tr, cols), lambda q, i, side_ref: (2 * q + side_ref[0], i, 0)), tile], out_specs=tile),
        out_shape=jax.ShapeDtypeStruct((N_CHIPS, rows, cols), BF16), compiler_params=_params(("parallel", "parallel")),
    )(side, part, received)


def _send_copy(parts, landing, send_sems, recv_sems, t, s, k):
    x, y, c = _position()
    px, py, _ = _flip(x, y, c, k)
    return pltpu.make_async_remote_copy(
        src_ref=parts[t].at[_chip(px, py)], dst_ref=landing[t].at[_chip(x, y)], send_sem=send_sems.at[s],
        recv_sem=recv_sems.at[s], device_id=(px, py, c), device_id_type=MESH_ID)


def _send_arrival(landing, send_sems, recv_sems, t, s, k):
    x, y, c = _position()
    px, py, _ = _flip(x, y, c, k)
    landed = landing[t].at[_chip(px, py)]
    return pltpu.make_async_remote_copy(
        src_ref=landed, dst_ref=landed, send_sem=send_sems.at[s], recv_sem=recv_sems.at[s],
        device_id=(px, py, c), device_id_type=MESH_ID)


def _handshake(peers):
    x, y, c = _position()
    barrier = pltpu.get_barrier_semaphore()
    for k in peers:
        pl.semaphore_signal(barrier, inc=1, device_id=_flip(x, y, c, k), device_id_type=MESH_ID)
    pl.semaphore_wait(barrier, len(peers))


def _sequencer(name, collective_id, n_copies, body, operands, out_type):
    n_arrays = len(operands)
    return pl.kernel(
        body, out_type=out_type, mesh=plsc.ScalarSubcoreMesh(axis_name="sequencer", num_cores=1), name=name,
        scratch_types=(pltpu.SemaphoreType.DMA((n_copies,)), pltpu.SemaphoreType.DMA((n_copies,)),
                       pltpu.SemaphoreType.DMA((n_arrays,))),
        compiler_params=pltpu.CompilerParams(collective_id=collective_id))(*operands)


def _sequencer_exchange(name, collective_id, parts, after=()):
    n, n_peers, n_in = len(parts), len(SAME_CORE), len(parts) + len(after)

    def body(*refs):
        src, landing = refs[:n], refs[n_in:n_in + n]
        send_sems, recv_sems, local_sems = refs[n_in + n:]
        _handshake(SAME_CORE)
        x, y, _ = _position()
        mine = [pltpu.make_async_copy(src[t].at[_chip(x, y)], landing[t].at[_chip(x, y)], local_sems.at[t]) for t in range(n)]
        for cp in mine:
            cp.start()
        sent = [_send_copy(src, landing, send_sems, recv_sems, t, t * n_peers + j, k)
                for t in range(n) for j, k in enumerate(SAME_CORE)]
        for cp in sent:
            cp.start()
        for t in range(n):
            for j, k in enumerate(SAME_CORE):
                _send_arrival(landing, send_sems, recv_sems, t, t * n_peers + j, k).wait_recv()
        for cp in sent:
            cp.wait_send()
        for cp in mine:
            cp.wait()

    landing = [jax.ShapeDtypeStruct(p.shape, p.dtype) for p in parts]
    return _sequencer(name, collective_id, n * n_peers, body, list(parts) + list(after), landing)


def _sequencer_gather(name, collective_id, shards):
    n, per = len(shards), N_PEERS

    def body(*refs):
        src, out = refs[:n], refs[n:2 * n]
        send_sems, recv_sems, local_sems = refs[2 * n:]
        _handshake(SIBLING_AND_SAME_CORE)
        x, y, c = _position()
        me, sibling = (x, y, c), (x, y, 1 - c)

        def copy(t, j, block, to, from_input=False):
            dst = out[t].at[_slot(*block)]
            return pltpu.make_async_remote_copy(
                src_ref=src[t] if from_input else dst, dst_ref=dst, send_sem=send_sems.at[t * per + j],
                recv_sem=recv_sems.at[t * per + j], device_id=to, device_id_type=MESH_ID)

        mine = [pltpu.make_async_copy(src[t], out[t].at[_slot(*me)], local_sems.at[t]) for t in range(n)]
        for cp in mine:
            cp.start()
        sent = [copy(t, j, me, _flip(x, y, c, k), True) for t in range(n) for j, k in enumerate(SIBLING_AND_SAME_CORE)]
        for cp in sent:
            cp.start()
        for t in range(n):
            for j, k in enumerate(SAME_CORE):
                block = _flip(x, y, c, k)
                copy(t, 1 + j, block, me).wait_recv()
                forward = copy(t, 4 + j, block, sibling)
                forward.start()
                sent.append(forward)
        for t in range(n):
            copy(t, 0, sibling, me).wait_recv()
            for j, k in enumerate(SAME_CORE):
                copy(t, 4 + j, _flip(x, y, 1 - c, k), me).wait_recv()
        for cp in sent:
            cp.wait_send()
        for cp in mine:
            cp.wait()

    gathered = [jax.ShapeDtypeStruct((N_DEV,) + s.shape, s.dtype) for s in shards]
    return _sequencer(name, collective_id, n * per, body, shards, gathered)


ADAM_ROWS = 512
BF16_ROWS = 16


def _adam_update(w, g, m, v):
    m = ADAM_B1 * m + (1.0 - ADAM_B1) * g
    v = ADAM_B2 * v + (1.0 - ADAM_B2) * (g * g)
    m_hat = m / (1.0 - ADAM_B1 ** ADAM_STEP)
    v_hat = v / (1.0 - ADAM_B2 ** ADAM_STEP)
    delta = -ADAM_LR * (m_hat / (jnp.sqrt(v_hat) + ADAM_EPS) + ADAM_WD * w)
    return delta, m, v


def _sum_slots(ref):
    total = ref[0].astype(F32)
    for d in range(1, ref.shape[0]):
        total = total + ref[d].astype(F32)
    return total


def _adamw_sum(name, landed, w, m, v):
    layers, rows, cols = w.shape
    tiles = [t for t in range(ADAM_ROWS, 0, -BF16_ROWS) if rows % t == 0]
    tr = tiles[0] if tiles else rows
    nt = rows // tr

    def body(*refs):
        parts = refs[:layers]
        w_ref, m_ref, v_ref, g_ref, d_ref, nm_ref, nv_ref = refs[layers:]
        layer = pl.program_id(0)
        g = _sum_slots(parts[0])
        for q in range(1, layers):
            g = jnp.where(layer == q, _sum_slots(parts[q]), g)
        delta, new_m, new_v = _adam_update(w_ref[...], g, m_ref[...], v_ref[...])
        g_ref[...] = g
        d_ref[...] = delta
        nm_ref[...] = new_m
        nv_ref[...] = new_v

    def part_spec(q):
        return _spec((N_CHIPS, tr, cols), lambda l, i: (0, jnp.where(l == q, i, jnp.where(l < q, 0, nt - 1)), 0))

    tile = _spec((None, tr, cols), lambda l, i: (l, i, 0))
    out = jax.ShapeDtypeStruct((layers, rows, cols), F32)
    return pl.pallas_call(
        body, name=name, grid=(layers, nt), in_specs=[part_spec(q) for q in range(layers)] + [tile] * 3,
        out_specs=[tile] * 4, out_shape=[out] * 4, compiler_params=_params(("arbitrary", "arbitrary")),
    )(*landed, w, m, v)


def _sum_small(landed):
    def body(in_ref, out_ref):
        out_ref[...] = _sum_slots(in_ref)

    return pl.pallas_call(body, name="small_grad_sum", out_shape=jax.ShapeDtypeStruct(landed.shape[1:], F32))(landed)


def _adamw_small(name, g, w, m, v):
    def body(g_ref, w_ref, m_ref, v_ref, d_ref, nm_ref, nv_ref):
        d_ref[...], nm_ref[...], nv_ref[...] = _adam_update(w_ref[...], g_ref[...], m_ref[...], v_ref[...])

    out = jax.ShapeDtypeStruct(w.shape, F32)
    return pl.pallas_call(body, name=name, out_shape=[out] * 3)(g, w, m, v)


LANES = 128
SUBLANES = 8
F_CONV_SHARD = D_FF // N_DEV
GATE_SHARD = KEY_DIM // N_DEV
NORM_SHARD = D_MODEL // N_DEV


def _tile_rows(a):
    flat = a.reshape(-1)
    size = -(-flat.shape[0] // (SUBLANES * LANES)) * SUBLANES * LANES
    return jnp.pad(flat, (0, size - flat.shape[0])).reshape(-1, LANES)


def _pack_rows(pieces):
    return jnp.concatenate([_tile_rows(p) for p in pieces], axis=0)


def _unpack_rows(packed, shapes):
    out, row = [], 0
    for shape in shapes:
        size = 1
        for s in shape:
            size *= s
        rows = -(-size // (SUBLANES * LANES)) * SUBLANES
        piece = packed[..., row:row + rows, :]
        out.append(piece.reshape(piece.shape[:-2] + (rows * LANES,))[..., :size])
        row += rows
    return out


SMALL_SHARDS = ((GATE_RANK, GATE_SHARD), (1, NORM_SHARD), (3, NORM_SHARD), (2, 3, F_CONV_SHARD))


def _unpack_small_shards(g):
    gate, b_norm, b_conv, f_conv = _unpack_rows(g, SMALL_SHARDS)
    gate = gate.reshape(N_DEV, GATE_RANK, GATE_SHARD).transpose(1, 0, 2).reshape(GATE_RANK, KEY_DIM)
    b_norm = b_norm.reshape(1, D_MODEL)
    b_conv = b_conv.reshape(N_DEV, 3, NORM_SHARD).transpose(1, 0, 2).reshape(3, D_MODEL)
    f_conv = f_conv.reshape(N_DEV, 2, 3, F_CONV_SHARD).transpose(1, 2, 0, 3).reshape(2, 3, D_FF)
    return gate, b_norm, b_conv, f_conv


def _conv_blocks(f_conv):
    return f_conv.reshape(2, 3, FF_BLOCKS, FF_BLOCK).transpose(0, 2, 1, 3)


def _conv_unblocks(f_conv):
    return f_conv.transpose(1, 0, 2).reshape(3, D_FF)


SMALL_LAYOUT = (("a_norm", (1, D_MODEL)), ("a_w_gate_up", (GATE_RANK, KEY_DIM)), ("a_b_gate", (1, KEY_DIM)), ("a_gn", (1, VALUE_DIM)),
                ("b_norm", (1, D_MODEL)), ("b_conv", (3, D_MODEL)), ("f_norm0", (1, D_MODEL)), ("f_norm1", (1, D_MODEL)),
                ("f_conv0", (3, D_FF)), ("f_conv1", (3, D_FF)), ("final_norm", (1, D_MODEL)))


def _pack_small_grads(g):
    full = dict(g)
    full["a_w_gate_up"] = g["a_w_gate_up"][:GATE_RANK]
    for layer in range(2):
        full[f"f_norm{layer}"] = g["f_norm"][layer]
        full[f"f_conv{layer}"] = _conv_unblocks(g["f_conv"][layer])
    return _pack_rows([full[name] for name, _ in SMALL_LAYOUT])


def _unpack_small_grads(packed):
    pieces = _unpack_rows(packed, [shape for _, shape in SMALL_LAYOUT])
    out = {name: piece.reshape(shape) for (name, shape), piece in zip(SMALL_LAYOUT, pieces)}
    out["f_norm"] = jnp.stack([out["f_norm0"][0], out["f_norm1"][0]])
    out["f_conv"] = jnp.stack([out["f_conv0"], out["f_conv1"]])
    return out


def kernel(x, a_norm, a_w_in, a_w_gate_up, a_b_gate, a_gn, a_w_out, b_norm, b_w_in, b_conv, b_w_out, f_norm, f_w_up, f_conv, f_w_down, final_norm, loss_target, m_a_norm, m_a_w_in, m_a_w_gate_up, m_a_b_gate, m_a_gn, m_a_w_out, m_b_norm, m_b_w_in, m_b_conv, m_b_w_out, m_f_norm, m_f_w_up, m_f_conv, m_f_w_down, m_final_norm, v_a_norm, v_a_w_in, v_a_w_gate_up, v_a_b_gate, v_a_gn, v_a_w_out, v_b_norm, v_b_w_in, v_b_conv, v_b_w_out, v_f_norm, v_f_w_up, v_f_conv, v_f_w_down, v_final_norm):
    my_slot = _slot(*_position())

    transposed = lambda w: jnp.swapaxes(w, 1, 2)
    a_w_in_t, f_w_up_t = transposed(a_w_in), transposed(f_w_up)
    first = _all_gather("weight_gather", [a_w_in_t[0].astype(BF16), a_w_out[0].astype(BF16),
                                          _pack_rows([a_w_gate_up[0], b_norm, b_conv[0], f_conv])])
    gathers, small_shards = {}, first[2]
    later = (("f0", f_w_up_t[0], f_w_down[0]), ("b", b_w_in[0], b_w_out[0]), ("f1", f_w_up_t[1], f_w_down[1]))
    for collective_id, (group, w_in, w_out) in enumerate(later):
        w_in, w_out, small_shards = lax.optimization_barrier((w_in.astype(BF16), w_out.astype(BF16), small_shards))
        gathers[group] = _sequencer_gather(f"gather_{group}", collective_id, [w_in, w_out])
    gate_full, b_norm_full, b_conv_full, f_conv_full = _unpack_small_shards(small_shards)
    a_w_in_full = jnp.pad(first[0].reshape(PROJ_A, D_MODEL), ((0, PROJ_A_PAD - PROJ_A), (0, 0)))
    weights = dict(
        a_norm=a_norm, a_w_gate_up=jnp.pad(gate_full, ((0, GATE_PAD - GATE_RANK), (0, 0))).astype(BF16), a_b_gate=a_b_gate,
        a_gn=a_gn, b_norm=b_norm_full, b_conv=b_conv_full, f_norm=f_norm, f_conv=_conv_blocks(f_conv_full),
        final_norm=final_norm.reshape(1, D_MODEL))

    def fetch(group, after):
        if group == "a":
            return a_w_in_full, first[1].reshape(D_MODEL, D_MODEL)
        w_in, w_out = gathers[group]
        if group == "b":
            return w_in, w_out.reshape(D_MODEL, D_MODEL)
        return w_in, w_out.reshape(FF_BLOCKS, FF_BLOCK, D_MODEL)

    exchanges, pending = {}, []
    exchange_ids = dict(b=3, f0=4, a=5)
    side = lax.axis_index("c").astype(jnp.int32).reshape(1)

    def emit(group, grads, carry):
        d_in, d_out = grads
        if group == "a":
            d_in = d_in[:PROJ_A]
        d_in, d_out = d_in.reshape((N_DEV, -1) + d_in.shape[-1:]), d_out.reshape((N_DEV, -1, D_MODEL))
        received = _pair_swap(f"pair_swap_{group}", [d_in, d_out])
        sums = [_pair_add(f"pair_add_{group}_{i}", part, got, side) for i, (part, got) in enumerate(zip((d_in, d_out), received))]
        carry, *sums = lax.optimization_barrier((carry, *sums))
        pending.extend(sums)
        if group != "f1":
            after = list(exchanges.values())[-1][:1] if exchanges else ()
            exchanges[group] = _sequencer_exchange(f"grads_{group}", exchange_ids[group], list(pending), after)
            pending.clear()
        return carry

    loss, dx, g = _local_step(x[0], loss_target[0], weights, fetch, emit)
    loss = lax.psum(loss, MESH_AXES)
    small_landed = _all_gather("small_grad_gather", [_pack_small_grads(g)])[0]

    (up1, down1, d_b_in, d_b_out), (up0, down0), (d_a_in, d_a_out) = (exchanges[group] for group in ("b", "f0", "a"))
    back = lambda results: tuple(transposed(r) for r in results)
    big = dict(
        b_w_in=_adamw_sum("adam_b_w_in", [d_b_in], b_w_in, m_b_w_in, v_b_w_in),
        b_w_out=_adamw_sum("adam_b_w_out", [d_b_out], b_w_out, m_b_w_out, v_b_w_out),
        f_w_up=back(_adamw_sum("adam_f_w_up", [up0, up1], f_w_up_t, transposed(m_f_w_up), transposed(v_f_w_up))),
        f_w_down=_adamw_sum("adam_f_w_down", [down0, down1], f_w_down, m_f_w_down, v_f_w_down),
        a_w_in=back(_adamw_sum("adam_a_w_in", [d_a_in], a_w_in_t, transposed(m_a_w_in), transposed(v_a_w_in))),
        a_w_out=_adamw_sum("adam_a_w_out", [d_a_out], a_w_out, m_a_w_out, v_a_w_out))
    small_g = _unpack_small_grads(_sum_small(small_landed))
    small_g["a_w_gate_up"] = lax.dynamic_slice_in_dim(small_g["a_w_gate_up"], my_slot * GATE_SHARD, GATE_SHARD, axis=1)
    small_g["b_norm"] = lax.dynamic_slice_in_dim(small_g["b_norm"], my_slot * NORM_SHARD, NORM_SHARD, axis=1)
    small_g["b_conv"] = lax.dynamic_slice_in_dim(small_g["b_conv"], my_slot * NORM_SHARD, NORM_SHARD, axis=1)
    small_g["f_conv"] = lax.dynamic_slice_in_dim(small_g["f_conv"], my_slot * F_CONV_SHARD, F_CONV_SHARD, axis=2)
    small_w = dict(
        a_norm=(a_norm, m_a_norm, v_a_norm), a_w_gate_up=(a_w_gate_up, m_a_w_gate_up, v_a_w_gate_up),
        a_b_gate=(a_b_gate, m_a_b_gate, v_a_b_gate), a_gn=(a_gn, m_a_gn, v_a_gn), b_norm=(b_norm, m_b_norm, v_b_norm),
        b_conv=(b_conv, m_b_conv, v_b_conv), f_norm=(f_norm, m_f_norm, v_f_norm), f_conv=(f_conv, m_f_conv, v_f_conv),
        final_norm=(final_norm, m_final_norm, v_final_norm))
    small = {}
    for name, (w, m, v) in small_w.items():
        flat = (w.shape[-1],) if w.ndim == 1 else w.shape[-2:]
        two_d = (-1, flat[-1])
        grad = small_g[name].reshape(w.shape)
        delta, new_m, new_v = _adamw_small(
            "adam_" + name, grad.reshape(two_d), w.reshape(two_d), m.reshape(two_d), v.reshape(two_d))
        small[name] = (grad, delta.reshape(w.shape), new_m.reshape(w.shape), new_v.reshape(w.shape))

    order = ["a_norm", "a_w_in", "a_w_gate_up", "a_b_gate", "a_gn", "a_w_out", "b_norm", "b_w_in", "b_conv", "b_w_out",
             "f_norm", "f_w_up", "f_conv", "f_w_down", "final_norm"]
    results = {**big, **small}
    outputs = [loss, dx.reshape(1, SEQ, D_MODEL)]
    for kind in range(4):
        outputs += [results[name][kind] for name in order]
    return tuple(outputs)
```

```python
import jax
import jax.numpy as jnp
from jax import lax
from jax.experimental import pallas as pl
from jax.experimental.pallas import tpu as pltpu
from jax.experimental.pallas import tpu_sc as plsc

F32 = jnp.float32
BF16 = jnp.bfloat16

N_DEV = 8
SEQ = 2048
D_MODEL = 1024
CHUNK = 64
N_CHUNKS = SEQ // CHUNK
RMS_EPS = 1e-6
GLA_HEADS = 4
KEY_DIM = 512
VALUE_DIM = 1024
HEAD_K = KEY_DIM // GLA_HEADS
HEAD_V = VALUE_DIM // GLA_HEADS
GATE_RANK = 16
GATE_PAD = 128
GATE_NORMALIZER = 16.0
PROJ_A = 2 * KEY_DIM + 2 * VALUE_DIM + GATE_RANK
PROJ_A_PAD = 2 * KEY_DIM + 2 * VALUE_DIM + GATE_PAD
A_SHARD = PROJ_A // N_DEV
B_SHARD = 3 * D_MODEL // N_DEV
D_FF = 2816
ADAM_LR = 0.001
ADAM_B1 = 0.9
ADAM_B2 = 0.999
ADAM_EPS = 1e-08
ADAM_WD = 0.01
ADAM_STEP = 10
MESH_AXES = ("x", "y", "c")

VMEM_LIMIT = 56 * 1024 * 1024
ROW_CHUNK = 256
HALO = 16


def _params(sem=None, vmem=VMEM_LIMIT):
    return pltpu.CompilerParams(dimension_semantics=sem, vmem_limit_bytes=vmem)


NN = ((1,), (0,))
NT = ((1,), (1,))
TN = ((0,), (0,))


def _matmul(name, a, a_spec, b, b_spec, dims, grid, out_shape, out_spec, k_blocks=None, a_block_cols=None, res=None,
            res_spec=None, transpose_out=False, norm=None):
    has_res = res is not None

    def body(*refs):
        a_ref, b_ref = refs[0], refs[1]
        r_ref = refs[2] if has_res else None

        def product(lhs, rhs):
            return lax.dot_general(lhs.astype(BF16), rhs, (dims, ((), ())), preferred_element_type=F32)

        if k_blocks is None:
            v = product(a_ref[...], b_ref[...])
        else:
            v = None
            for k in range(k_blocks):
                lhs = a_ref[k] if a_block_cols is None else a_ref[:, k * a_block_cols:(k + 1) * a_block_cols]
                p = product(lhs, b_ref[k])
                v = p if v is None else v + p
        if transpose_out:
            v = v.T
        if has_res:
            v = v + r_ref[...]
        if norm is None:
            o_ref = refs[2 + has_res]
            o_ref[...] = v.astype(o_ref.dtype)
            return
        x_ref, g_ref, dxi_ref, dx_ref, dx16_ref, dg_ref = refs[2 + has_res:]
        dx, dg = _norm_bwd_rows(x_ref[...], g_ref[...], v)
        dx = dxi_ref[...] + dx
        dx_ref[...] = dx
        dx16_ref[...] = dx.astype(BF16)

        @pl.when(pl.program_id(0) == 0)
        def _():
            dg_ref[...] = dg

        @pl.when(pl.program_id(0) > 0)
        def _():
            dg_ref[...] += dg

    operands = [a, b] + ([res] if has_res else [])
    in_specs = [a_spec, b_spec] + ([res_spec] if has_res else [])
    semantics = ("parallel",) * len(grid)
    if norm is not None:
        vec = _spec((1, D_MODEL), lambda i: (0, 0))
        operands += list(norm)
        in_specs += [out_spec, vec, out_spec]
        out_shape, out_spec = [_act(dtype=F32), _act(), jax.ShapeDtypeStruct((1, D_MODEL), F32)], [out_spec, out_spec, vec]
        semantics = ("arbitrary",)
    return pl.pallas_call(
        body, name=name, grid=grid, in_specs=in_specs, out_specs=out_spec, out_shape=out_shape,
        compiler_params=_params(semantics),
    )(*operands)


def _resident(shape):
    return pl.BlockSpec(shape, lambda *_: (0,) * len(shape), pipeline_mode=pl.Buffered(1))


TM = 512
N_TM = SEQ // TM
PA_TILE = 640
N_PA = PROJ_A_PAD // PA_TILE
OUT_TILE = 256


def _spec(shape, fn):
    return pl.BlockSpec(shape, fn)


def _act(shape=(SEQ, D_MODEL), dtype=BF16):
    return jax.ShapeDtypeStruct(shape, dtype)


def _proj_rows_nt(name, h, wt, n_tile):
    n = wt.shape[0]
    return _matmul(name, h, _resident((SEQ, D_MODEL)), wt, _spec((n_tile, D_MODEL), lambda j: (j, 0)), NT,
                   (n // n_tile,), _act((SEQ, n)), _spec((SEQ, n_tile), lambda j: (0, j)))


def _proj_cols_nn(name, h, w_blocks):
    nb, _, n = w_blocks.shape
    return _matmul(name, h, _resident((SEQ, D_MODEL)), w_blocks, _spec((None, D_MODEL, n), lambda j: (j, 0, 0)),
                   NN, (nb,), _act((SEQ, nb * n)), _spec((SEQ, n), lambda j: (0, j)))


def _square(name, a, w, dims, x=None):
    row = _spec((TM, D_MODEL), lambda i: (i, 0))
    return _matmul(name, a, row, w, _resident((D_MODEL, D_MODEL)), dims, (N_TM,),
                   _act(dtype=F32 if x is not None else BF16), row, res=x, res_spec=row if x is not None else None)


def _sum_blocks_nn(name, a_blocks, w_blocks, x=None, norm=None):
    nb, _, n = a_blocks.shape
    row = _spec((TM, D_MODEL), lambda i: (i, 0))
    return _matmul(name, a_blocks, _spec((nb, TM, n), lambda i: (0, i, 0)), w_blocks, _resident((nb, n, D_MODEL)),
                   NN, (N_TM,), _act(dtype=F32), row, k_blocks=nb, res=x, res_spec=row if x is not None else None, norm=norm)


def _sum_cols_nt(name, d, w_blocks, norm=None):
    nb, _, n = w_blocks.shape
    return _matmul(name, d, _spec((TM, nb * n), lambda i: (i, 0)), w_blocks, _resident((nb, D_MODEL, n)), NT,
                   (N_TM,), _act(dtype=F32), _spec((TM, D_MODEL), lambda i: (i, 0)), k_blocks=nb, a_block_cols=n, norm=norm)


def _wide_nn(name, d, wt, x=None, norm=None):
    n = wt.shape[0]
    row = _spec((TM, D_MODEL), lambda i: (i, 0))
    return _matmul(name, d, _spec((TM, n), lambda i: (i, 0)), wt, _resident((n, D_MODEL)), NN, (N_TM,),
                   _act(dtype=F32), row, res=x, res_spec=row if x is not None else None, norm=norm)


def _wide_nt(name, d, w):
    n = w.shape[0]
    return _matmul(name, d, _spec((TM, D_MODEL), lambda i: (i, 0)), w, _resident((n, D_MODEL)), NT, (N_TM,),
                   _act((SEQ, n)), _spec((TM, n), lambda i: (i, 0)))


def _proj_halves_nt(name, h, wt):
    _, n, _ = wt.shape
    return _matmul(name, h, _spec((TM, D_MODEL), lambda p, i: (i, 0)), wt, _spec((None, n, D_MODEL), lambda p, i: (p, 0, 0)), NT,
                   (2, N_TM), _act((2, SEQ, n)), _spec((None, TM, n), lambda p, i: (p, i, 0)))


def _wgrad_halves_tn(name, d, n_tile, h):
    _, _, n = d.shape
    return _matmul(name, d, _spec((None, SEQ, n_tile), lambda p, j: (p, 0, j)), h, _resident((SEQ, D_MODEL)), TN,
                   (2, n // n_tile), _act((2, n, D_MODEL)), _spec((None, n_tile, D_MODEL), lambda p, j: (p, j, 0)))


def _wgrad_cols_tn(name, d, n_tile, h):
    n = d.shape[1]
    return _matmul(name, d, _spec((SEQ, n_tile), lambda j: (0, j)), h, _resident((SEQ, D_MODEL)), TN,
                   (n // n_tile,), _act((n, D_MODEL)), _spec((n_tile, D_MODEL), lambda j: (j, 0)))


def _wgrad_cols_transposed_tn(name, h, d, n_tile):
    nb = d.shape[1] // n_tile
    return _matmul(name, d, _spec((SEQ, n_tile), lambda j: (0, j)), h, _resident((SEQ, D_MODEL)), TN, (nb,),
                   _act((nb, D_MODEL, n_tile)), _spec((None, D_MODEL, n_tile), lambda j: (j, 0, 0)), transpose_out=True)


NORM_ROWS = 512


def _rstd(x):
    return lax.rsqrt(jnp.mean(x * x, axis=-1, keepdims=True) + RMS_EPS)


def _norm_fwd(name, x, gamma):
    def body(x_ref, g_ref, h_ref):
        x = x_ref[...]
        h_ref[...] = (x * _rstd(x) * g_ref[...]).astype(BF16)

    row = _spec((NORM_ROWS, D_MODEL), lambda i: (i, 0))
    return pl.pallas_call(
        body, name=name, grid=(SEQ // NORM_ROWS,), in_specs=[row, _spec((1, D_MODEL), lambda i: (0, 0))], out_specs=row,
        out_shape=jax.ShapeDtypeStruct((SEQ, D_MODEL), BF16), compiler_params=_params(("parallel",)),
    )(x, gamma)


def _norm_bwd_rows(x, gamma, dh):
    r = _rstd(x)
    xh = x * r
    dxh = dh * gamma
    dx = r * (dxh - xh * jnp.mean(dxh * xh, axis=-1, keepdims=True))
    return dx, jnp.sum(dh * xh, axis=0, keepdims=True)


def _loss_head(x, gamma, target):
    def body(x_ref, g_ref, t_ref, loss_ref, dx_ref, dx16_ref, dg_ref):
        x = x_ref[...]
        gamma = g_ref[...]
        err = x * _rstd(x) * gamma - t_ref[...]
        dy = err * (1.0 / D_MODEL)
        dx, dg = _norm_bwd_rows(x, gamma, dy)
        dx_ref[...] = dx
        dx16_ref[...] = dx.astype(BF16)
        part = 0.5 * jnp.sum(jnp.sum(err * err, axis=-1, keepdims=True) * (1.0 / D_MODEL), axis=0, keepdims=True)
        part = jnp.broadcast_to(part, loss_ref.shape)

        @pl.when(pl.program_id(0) == 0)
        def _():
            dg_ref[...] = dg
            loss_ref[...] = part

        @pl.when(pl.program_id(0) > 0)
        def _():
            dg_ref[...] += dg
            loss_ref[...] += part

    row = _spec((NORM_ROWS, D_MODEL), lambda i: (i, 0))
    vec = _spec((1, D_MODEL), lambda i: (0, 0))
    return pl.pallas_call(
        body, name="loss_head", grid=(SEQ // NORM_ROWS,), in_specs=[row, vec, row],
        out_specs=[_spec((1, 128), lambda i: (0, 0)), row, row, vec],
        out_shape=[jax.ShapeDtypeStruct((1, 128), F32), _act(dtype=F32), _act(), jax.ShapeDtypeStruct((1, D_MODEL), F32)],
        compiler_params=_params(("arbitrary",)),
    )(x, gamma, target)


def _sigmoid(x):
    return 1.0 / (1.0 + jnp.exp(-x))


def _rows(ref, c):
    return ref[pl.ds(pl.multiple_of(c * ROW_CHUNK, ROW_CHUNK), ROW_CHUNK), :].astype(F32)


def _rows_before(ref, c):
    start = pl.multiple_of(jnp.maximum(c * ROW_CHUNK - HALO, 0), HALO)
    rows = ref[pl.ds(start, HALO), :].astype(F32)
    return jnp.where(c > 0, rows, 0.0)


def _rows_after(ref, c, n_chunks):
    start = pl.multiple_of(jnp.minimum((c + 1) * ROW_CHUNK, SEQ - HALO), HALO)
    rows = ref[pl.ds(start, HALO), :].astype(F32)
    return jnp.where(c < n_chunks - 1, rows, 0.0)


def _shift_down(z, before, n):
    row = lax.broadcasted_iota(jnp.int32, z.shape, 0)
    out = pltpu.roll(z, n, 0)
    for r in range(n):
        out = jnp.where(row == r, before[HALO - n + r:HALO - n + r + 1, :], out)
    return out


def _shift_up(z, after, n):
    rows = z.shape[0]
    row = lax.broadcasted_iota(jnp.int32, z.shape, 0)
    out = pltpu.roll(z, rows - n, 0)
    for r in range(n):
        out = jnp.where(row == rows - n + r, after[r:r + 1, :], out)
    return out


def _conv_rows(z, before, w):
    z1 = _shift_down(z, before, 1)
    z2 = _shift_down(z, before, 2)
    return w[2:3, :] * z + w[1:2, :] * z1 + w[0:1, :] * z2, z1, z2


def _conv_t_rows(dy, after, w):
    return w[2:3, :] * dy + w[1:2, :] * _shift_up(dy, after, 1) + w[0:1, :] * _shift_up(dy, after, 2)


N_ROW_CHUNKS = SEQ // ROW_CHUNK


FF_COLS = 256
N_FF_COLS = D_FF // FF_COLS
PAD_ROWS = 8


def _chunk(c, shift=0):
    return slice(c * ROW_CHUNK + shift, (c + 1) * ROW_CHUNK + shift)


def _load_padded_front(buf, ref):
    buf[0:PAD_ROWS, :] = jnp.zeros((PAD_ROWS, buf.shape[1]), F32)
    for c in range(N_ROW_CHUNKS):
        buf[_chunk(c, PAD_ROWS), :] = ref[_chunk(c), :].astype(F32)


def _conv_padded(buf, c, w):
    z, z1, z2 = (buf[_chunk(c, PAD_ROWS - n), :] for n in range(3))
    return w[2:3, :] * z + w[1:2, :] * z1 + w[0:1, :] * z2, z, z1, z2


def _ffn_mid_fwd(name, gu, conv_w):
    def body(gu_ref, w_ref, a_ref, g_buf):
        w = w_ref[...]
        _load_padded_front(g_buf, gu_ref.at[0])
        for c in range(N_ROW_CHUNKS):
            gc, _, _, _ = _conv_padded(g_buf, c, w)
            a_ref[_chunk(c), :] = (gc * _sigmoid(gc) * gu_ref[1, _chunk(c), :].astype(F32)).astype(BF16)

    col = _spec((SEQ, FF_COLS), lambda j: (0, j))
    return pl.pallas_call(
        body, name=name, grid=(N_FF_COLS,),
        in_specs=[_spec((2, SEQ, FF_COLS), lambda j: (0, 0, j)), _spec((3, FF_COLS), lambda j: (0, j))], out_specs=col,
        out_shape=_act((SEQ, D_FF)), scratch_shapes=[pltpu.VMEM((PAD_ROWS + SEQ, FF_COLS), F32)],
        compiler_params=_params(("parallel",)),
    )(gu, conv_w)


def _ffn_mid_bwd(name, gu, conv_w, da):
    def body(gu_ref, w_ref, da_ref, dgu_ref, dw_ref, g_buf, dgc_buf):
        w = w_ref[...]
        _load_padded_front(g_buf, gu_ref.at[0])
        dgc_buf[SEQ:SEQ + PAD_ROWS, :] = jnp.zeros((PAD_ROWS, FF_COLS), F32)
        acc = [jnp.zeros((1, FF_COLS), F32)] * 3
        for c in range(N_ROW_CHUNKS):
            gc, g, g1, g2 = _conv_padded(g_buf, c, w)
            u = gu_ref[1, _chunk(c), :].astype(F32)
            d = da_ref[_chunk(c), :].astype(F32)
            sg = _sigmoid(gc)
            dgu_ref[1, _chunk(c), :] = (d * gc * sg).astype(BF16)
            dgc = d * u * (sg * (1.0 + gc * (1.0 - sg)))
            dgc_buf[_chunk(c), :] = dgc
            acc = [acc[n] + jnp.sum(dgc * z, axis=0, keepdims=True) for n, z in enumerate((g2, g1, g))]
        for r in range(3):
            dw_ref[r:r + 1, :] = acc[r]
        for c in range(N_ROW_CHUNKS):
            dgc, d1, d2 = (dgc_buf[_chunk(c, n), :] for n in range(3))
            dgu_ref[0, _chunk(c), :] = (w[2:3, :] * dgc + w[1:2, :] * d1 + w[0:1, :] * d2).astype(BF16)

    pair = _spec((2, SEQ, FF_COLS), lambda j: (0, 0, j))
    wspec = _spec((3, FF_COLS), lambda j: (0, j))
    return pl.pallas_call(
        body, name=name, grid=(N_FF_COLS,), in_specs=[pair, wspec, _spec((SEQ, FF_COLS), lambda j: (0, j))],
        out_specs=[pair, wspec], out_shape=[_act((2, SEQ, D_FF)), jax.ShapeDtypeStruct((3, D_FF), F32)],
        scratch_shapes=[pltpu.VMEM((PAD_ROWS + SEQ, FF_COLS), F32), pltpu.VMEM((SEQ + PAD_ROWS, FF_COLS), F32)],
        compiler_params=_params(("parallel",)),
    )(gu, conv_w, da)


SC_COLS = 256
N_SC = D_MODEL // SC_COLS


def _sc_specs():
    return [_spec((SEQ, SC_COLS), lambda j, part=part: (0, part * N_SC + j)) for part in range(3)]


def _sc_mid_fwd(p, conv_w):
    def body(b_ref, c_ref, h_ref, w_ref, y_ref):
        w = w_ref[...]

        def chunk(c, carry):
            z = _rows(c_ref, c) * _rows(h_ref, c)
            before = _rows_before(c_ref, c) * _rows_before(h_ref, c)
            zc, _, _ = _conv_rows(z, before, w)
            y_ref[pl.ds(pl.multiple_of(c * ROW_CHUNK, ROW_CHUNK), ROW_CHUNK), :] = (_rows(b_ref, c) * zc).astype(BF16)
            return carry

        lax.fori_loop(0, N_ROW_CHUNKS, chunk, 0)

    col = _spec((SEQ, SC_COLS), lambda j: (0, j))
    return pl.pallas_call(
        body, name="sc_mid_fwd", grid=(N_SC,), in_specs=_sc_specs() + [_spec((3, SC_COLS), lambda j: (0, j))], out_specs=col,
        out_shape=jax.ShapeDtypeStruct((SEQ, D_MODEL), BF16), compiler_params=_params(("parallel",)),
    )(p, p, p, conv_w)


def _sc_mid_bwd(p, conv_w, dy):
    def body(b_ref, c_ref, h_ref, w_ref, dy_ref, db_ref, dc_ref, dh_ref, dw_ref, dzc_ref):
        w = w_ref[...]

        def first(c, acc):
            z = _rows(c_ref, c) * _rows(h_ref, c)
            before = _rows_before(c_ref, c) * _rows_before(h_ref, c)
            zc, z1, z2 = _conv_rows(z, before, w)
            d = _rows(dy_ref, c)
            rows = pl.ds(pl.multiple_of(c * ROW_CHUNK, ROW_CHUNK), ROW_CHUNK)
            db_ref[rows, :] = (d * zc).astype(BF16)
            dzc = d * _rows(b_ref, c)
            dzc_ref[rows, :] = dzc
            return (acc[0] + jnp.sum(dzc * z2, axis=0, keepdims=True), acc[1] + jnp.sum(dzc * z1, axis=0, keepdims=True),
                    acc[2] + jnp.sum(dzc * z, axis=0, keepdims=True))

        zero = jnp.zeros((1, SC_COLS), F32)
        acc = lax.fori_loop(0, N_ROW_CHUNKS, first, (zero, zero, zero))
        for r in range(3):
            dw_ref[r:r + 1, :] = acc[r]

        def second(c, carry):
            dz = _conv_t_rows(_rows(dzc_ref, c), _rows_after(dzc_ref, c, N_ROW_CHUNKS), w)
            rows = pl.ds(pl.multiple_of(c * ROW_CHUNK, ROW_CHUNK), ROW_CHUNK)
            dc_ref[rows, :] = (dz * _rows(h_ref, c)).astype(BF16)
            dh_ref[rows, :] = (dz * _rows(c_ref, c)).astype(BF16)
            return carry

        lax.fori_loop(0, N_ROW_CHUNKS, second, 0)

    col = _spec((SEQ, SC_COLS), lambda j: (0, j))
    wspec = _spec((3, SC_COLS), lambda j: (0, j))
    act = jax.ShapeDtypeStruct((SEQ, D_MODEL), BF16)
    return pl.pallas_call(
        body, name="sc_mid_bwd", grid=(N_SC,), in_specs=_sc_specs() + [wspec, col], out_specs=[col, col, col, wspec],
        out_shape=[act, act, act, jax.ShapeDtypeStruct((3, D_MODEL), F32)],
        scratch_shapes=[pltpu.VMEM((SEQ, SC_COLS), F32)], compiler_params=_params(("parallel",)),
    )(p, p, p, conv_w, dy)


GLA_GROUP = 4
GLA_ROWS = GLA_GROUP * CHUNK
N_GROUPS = N_CHUNKS // GLA_GROUP
Q0, K0, V0, R0, G0 = 0, KEY_DIM, 2 * KEY_DIM, 2 * KEY_DIM + VALUE_DIM, 2 * KEY_DIM + 2 * VALUE_DIM


def _tri(strict):
    r = lax.broadcasted_iota(jnp.int32, (CHUNK, CHUNK), 0)
    c = lax.broadcasted_iota(jnp.int32, (CHUNK, CHUNK), 1)
    return jnp.where(c < r if strict else c <= r, 1.0, 0.0).astype(F32)


def _cumsum_rows(tri, x):
    return jnp.dot(tri, x, preferred_element_type=F32, precision=lax.Precision.HIGHEST)


def _gate_logits(gl, wgu, b_gate):
    return jnp.dot(gl, wgu, preferred_element_type=F32) + b_gate


def _log_decay(logits):
    return (jnp.minimum(logits, 0.0) - jnp.log(1.0 + jnp.exp(-jnp.abs(logits)))) * (1.0 / GATE_NORMALIZER)


def _head(x, h, width):
    return x[:, h * width:(h + 1) * width]


def _gla_fwd(proj, wgu, b_gate, gn):
    def body(p_ref, wgu_ref, b_ref, gn_ref, o_ref, og_ref, st_ref, state):
        @pl.when(pl.program_id(0) == 0)
        def _():
            state[...] = jnp.zeros_like(state)

        tri = _tri(False)
        la = _log_decay(_gate_logits(p_ref[:, G0:G0 + GATE_PAD], wgu_ref[...], b_ref[...]))
        for c in range(GLA_GROUP):
            rows = slice(c * CHUNK, (c + 1) * CHUNK)
            cum = _cumsum_rows(tri, la[rows])
            tot = cum[CHUNK - 1:CHUNK, :]
            kd = (p_ref[rows, K0:K0 + KEY_DIM].astype(F32) * jnp.exp(tot - cum)).astype(BF16)
            decay = jnp.exp(tot)
            q = (p_ref[rows, Q0:Q0 + KEY_DIM].astype(F32) * (HEAD_K ** -0.5)).astype(BF16)
            v = p_ref[rows, V0:V0 + VALUE_DIM]
            for h in range(GLA_HEADS):
                upd = lax.dot_general(_head(v, h, HEAD_V), _head(kd, h, HEAD_K), (TN, ((), ())), preferred_element_type=F32)
                s = state[h] * _head(decay, h, HEAD_K) + upd
                state[h] = s
                st_ref[c, h] = s
                o_ref[rows, h * HEAD_V:(h + 1) * HEAD_V] = lax.dot_general(
                    _head(q, h, HEAD_K), s.astype(BF16), (NT, ((), ())), preferred_element_type=F32)
        r = p_ref[:, R0:R0 + VALUE_DIM].astype(F32)
        gate = r * _sigmoid(r) * gn_ref[...]
        for h in range(GLA_HEADS):
            cols = slice(h * HEAD_V, (h + 1) * HEAD_V)
            o = o_ref[:, cols]
            og_ref[:, cols] = (o * _rstd(o) * gate[:, cols]).astype(BF16)

    rows = _spec((GLA_ROWS, VALUE_DIM), lambda i: (i, 0))
    const = lambda shape: _spec(shape, lambda i: (0,) * len(shape))
    return pl.pallas_call(
        body, name="gla_fwd", grid=(N_GROUPS,),
        in_specs=[_spec((GLA_ROWS, PROJ_A_PAD), lambda i: (i, 0)), const((GATE_PAD, KEY_DIM)), const((1, KEY_DIM)),
                  const((1, VALUE_DIM))],
        out_specs=[rows, rows, _spec((GLA_GROUP, GLA_HEADS, HEAD_V, HEAD_K), lambda i: (i, 0, 0, 0))],
        out_shape=[jax.ShapeDtypeStruct((SEQ, VALUE_DIM), F32), jax.ShapeDtypeStruct((SEQ, VALUE_DIM), BF16),
                   jax.ShapeDtypeStruct((N_CHUNKS, GLA_HEADS, HEAD_V, HEAD_K), F32)],
        scratch_shapes=[pltpu.VMEM((GLA_HEADS, HEAD_V, HEAD_K), F32)], compiler_params=_params(("arbitrary",)),
    )(proj, wgu, b_gate, gn)


def _gla_bwd(proj, wgu, b_gate, gn, o, states, dog):
    last = N_GROUPS - 1

    def body(p_ref, wgu_ref, b_ref, gn_ref, o_ref, st_ref, stp_ref, dog_ref, dp_ref, dwgu_ref, db_ref, dgn_ref, carry, do_buf):
        step = pl.program_id(0)

        @pl.when(step == 0)
        def _():
            carry[...] = jnp.zeros_like(carry)

        r = p_ref[:, R0:R0 + VALUE_DIM].astype(F32)
        sr = _sigmoid(r)
        silu = r * sr
        gn_row = gn_ref[...]
        dog_rows = dog_ref[...].astype(F32)
        dn = dog_rows * silu
        dgn_cols = []
        for h in range(GLA_HEADS):
            cols = slice(h * HEAD_V, (h + 1) * HEAD_V)
            oh = o_ref[:, cols]
            rs = _rstd(oh)
            ohat = oh * rs
            dn_h = dn[:, cols]
            dgn_cols.append(jnp.sum(dn_h * ohat, axis=0, keepdims=True))
            dohat = dn_h * gn_row[:, cols]
            do_buf[:, cols] = rs * (dohat - ohat * jnp.mean(dohat * ohat, axis=-1, keepdims=True))
            n_h = ohat * gn_row[:, cols]
            dp_ref[:, R0 + h * HEAD_V:R0 + (h + 1) * HEAD_V] = (
                dog_rows[:, cols] * n_h * (sr[:, cols] * (1.0 + r[:, cols] * (1.0 - sr[:, cols])))).astype(BF16)
        dgn = jnp.concatenate(dgn_cols, axis=1)

        tri = _tri(False)
        tri_strict = _tri(True)
        gl = p_ref[:, G0:G0 + GATE_PAD]
        logits = _gate_logits(gl, wgu_ref[...], b_ref[...])
        la = _log_decay(logits)
        dlogit_rows = []
        for c in reversed(range(GLA_GROUP)):
            rows = slice(c * CHUNK, (c + 1) * CHUNK)
            cum = _cumsum_rows(tri, la[rows])
            tot = cum[CHUNK - 1:CHUNK, :]
            fade = jnp.exp(tot - cum)
            k = p_ref[rows, K0:K0 + KEY_DIM].astype(F32)
            kd32 = k * fade
            kd = kd32.astype(BF16)
            decay = jnp.exp(tot)
            q = (p_ref[rows, Q0:Q0 + KEY_DIM].astype(F32) * (HEAD_K ** -0.5)).astype(BF16)
            v = p_ref[rows, V0:V0 + VALUE_DIM]
            do = do_buf[rows, :].astype(BF16)
            dkd_cols, ddecay_cols = [], []
            for h in range(GLA_HEADS):
                do_h = _head(do, h, HEAD_V)
                s = st_ref[c, h]
                dq = jnp.dot(do_h, s.astype(BF16), preferred_element_type=F32) * (HEAD_K ** -0.5)
                dp_ref[rows, Q0 + h * HEAD_K:Q0 + (h + 1) * HEAD_K] = dq.astype(BF16)
                g = carry[h] + lax.dot_general(do_h, _head(q, h, HEAD_K), (TN, ((), ())), preferred_element_type=F32)
                g16 = g.astype(BF16)
                dkd_cols.append(jnp.dot(_head(v, h, HEAD_V), g16, preferred_element_type=F32))
                dv = lax.dot_general(_head(kd, h, HEAD_K), g16, (NT, ((), ())), preferred_element_type=F32)
                dp_ref[rows, V0 + h * HEAD_V:V0 + (h + 1) * HEAD_V] = dv.astype(BF16)
                if c > 0:
                    s_prev = st_ref[c - 1, h]
                else:
                    s_prev = jnp.where(step < last, stp_ref[0, h], 0.0)
                ddecay_cols.append(jnp.sum(g * s_prev, axis=0, keepdims=True))
                carry[h] = g * _head(decay, h, HEAD_K)
            dkd = jnp.concatenate(dkd_cols, axis=1)
            ddecay = jnp.concatenate(ddecay_cols, axis=1)
            dp_ref[rows, K0:K0 + KEY_DIM] = (dkd * fade).astype(BF16)
            e = dkd * kd32
            dla = ddecay * decay + _cumsum_rows(tri_strict, e)
            dlogit_rows.append(dla * (1.0 / GATE_NORMALIZER) * (1.0 - _sigmoid(logits[rows])))
        dlogit = jnp.concatenate(dlogit_rows[::-1], axis=0)
        dlogit16 = dlogit.astype(BF16)
        dp_ref[:, G0:G0 + GATE_PAD] = lax.dot_general(
            dlogit16, wgu_ref[...], (NT, ((), ())), preferred_element_type=F32).astype(BF16)
        dwgu = lax.dot_general(gl, dlogit16, (TN, ((), ())), preferred_element_type=F32)
        db = jnp.sum(dlogit, axis=0, keepdims=True)

        @pl.when(step == 0)
        def _():
            dwgu_ref[...] = dwgu
            db_ref[...] = db
            dgn_ref[...] = dgn

        @pl.when(step > 0)
        def _():
            dwgu_ref[...] += dwgu
            db_ref[...] += db
            dgn_ref[...] += dgn

    rev = lambda i: (last - i, 0)
    rows = _spec((GLA_ROWS, VALUE_DIM), rev)
    const = lambda shape: _spec(shape, lambda i: (0,) * len(shape))
    st_shape = (GLA_HEADS, HEAD_V, HEAD_K)
    return pl.pallas_call(
        body, name="gla_bwd", grid=(N_GROUPS,),
        in_specs=[_spec((GLA_ROWS, PROJ_A_PAD), rev), const((GATE_PAD, KEY_DIM)), const((1, KEY_DIM)), const((1, VALUE_DIM)),
                  rows, _spec((GLA_GROUP,) + st_shape, lambda i: (last - i, 0, 0, 0)),
                  _spec((1,) + st_shape, lambda i: (jnp.maximum((last - i) * GLA_GROUP - 1, 0), 0, 0, 0)), rows],
        out_specs=[_spec((GLA_ROWS, PROJ_A_PAD), rev), const((GATE_PAD, KEY_DIM)), const((1, KEY_DIM)), const((1, VALUE_DIM))],
        out_shape=[jax.ShapeDtypeStruct((SEQ, PROJ_A_PAD), BF16), jax.ShapeDtypeStruct((GATE_PAD, KEY_DIM), F32),
                   jax.ShapeDtypeStruct((1, KEY_DIM), F32), jax.ShapeDtypeStruct((1, VALUE_DIM), F32)],
        scratch_shapes=[pltpu.VMEM(st_shape, F32), pltpu.VMEM((GLA_ROWS, VALUE_DIM), F32)],
        compiler_params=_params(("arbitrary",)),
    )(proj, wgu, b_gate, gn, o, states, states, dog)


WGRAD_FF_TILE = D_FF // 2


def _ffn_fwd(tag, x, gamma, w_up_t, conv_w, w_down):
    h = _norm_fwd(f"ffn{tag}_norm", x, gamma)
    gu = _proj_halves_nt(f"ffn{tag}_up", h, w_up_t)
    a = _ffn_mid_fwd(f"ffn{tag}_mid", gu, conv_w)
    return _wide_nn(f"ffn{tag}_down", a, w_down, x=x), (h, gu, a)


def _ffn_bwd(tag, x, gamma, w_up_t, conv_w, w_down, saved, dx, dx16):
    h, gu, a = saved
    da = _wide_nt(f"ffn{tag}_da", dx16, w_down)
    d_w_down = _wgrad_cols_tn(f"ffn{tag}_dwdown", a, WGRAD_FF_TILE, dx16)
    dgu, d_conv = _ffn_mid_bwd(f"ffn{tag}_mid_bwd", gu, conv_w, da)
    dx, dx16, d_gamma = _sum_blocks_nn(f"ffn{tag}_dh", dgu, w_up_t, norm=(x, gamma, dx))
    d_w_up_t = _wgrad_halves_tn(f"ffn{tag}_dwup", dgu, WGRAD_FF_TILE, h)
    return dx, dx16, d_gamma, d_w_up_t, d_conv, d_w_down


def _local_step(x, target, w, fetch=None, emit=None):
    if fetch is None:
        local = dict(a=(w.get("a_w_in"), w.get("a_w_out")), b=(w.get("b_w_in"), w.get("b_w_out")))
        for layer in range(2):
            local[f"f{layer}"] = (w["f_w_up"][layer], w["f_w_down"][layer]) if "f_w_up" in w else None
        fetch = lambda group, after: local[group]
    if emit is None:
        emit = lambda group, grads, dx: dx
    f_norm = (w["f_norm"][0:1], w["f_norm"][1:2])

    x0 = x
    a_w_in, a_w_out = fetch("a", x0)
    h0 = _norm_fwd("a_norm", x0, w["a_norm"])
    proj = _proj_rows_nt("a_in", h0, a_w_in, PA_TILE)
    o, og, states = _gla_fwd(proj, w["a_w_gate_up"], w["a_b_gate"], w["a_gn"])
    x1 = _square("a_out", og, a_w_out, NN, x0)
    up0, down0 = fetch("f0", x1)
    x2, ffn0 = _ffn_fwd(0, x1, f_norm[0], up0, w["f_conv"][0], down0)
    b_w_in, b_w_out = fetch("b", x2)
    h2 = _norm_fwd("b_norm", x2, w["b_norm"])
    p = _proj_cols_nn("b_in", h2, b_w_in)
    y = _sc_mid_fwd(p, w["b_conv"])
    x3 = _square("b_out", y, b_w_out, NN, x2)
    up1, down1 = fetch("f1", x3)
    x4, ffn1 = _ffn_fwd(1, x3, f_norm[1], up1, w["f_conv"][1], down1)
    loss, dx, dx16, d_final_norm = _loss_head(x4, w["final_norm"], target)

    dx, dx16, d_f_norm1, d_up1, d_fconv1, d_down1 = _ffn_bwd(1, x3, f_norm[1], up1, w["f_conv"][1], down1, ffn1, dx, dx16)
    dx16 = emit("f1", (d_up1, d_down1), dx16)

    dy = _square("b_dy", dx16, b_w_out, NT)
    d_b_w_out = _wgrad_cols_tn("b_dwout", y, OUT_TILE, dx16)
    db, dc, dhh, d_b_conv = _sc_mid_bwd(p, w["b_conv"], dy)
    dp = jnp.concatenate([db, dc, dhh], axis=1)
    dx, dx16, d_b_norm = _sum_cols_nt("b_dh", dp, b_w_in, norm=(x2, w["b_norm"], dx))
    d_b_w_in = _wgrad_cols_transposed_tn("b_dwin", h2, dp, B_SHARD)
    dx16 = emit("b", (d_b_w_in, d_b_w_out), dx16)

    dx, dx16, d_f_norm0, d_up0, d_fconv0, d_down0 = _ffn_bwd(0, x1, f_norm[0], up0, w["f_conv"][0], down0, ffn0, dx, dx16)
    dx16 = emit("f0", (d_up0, d_down0), dx16)

    dog = _square("a_dog", dx16, a_w_out, NT)
    d_a_w_out = _wgrad_cols_tn("a_dwout", og, OUT_TILE, dx16)
    dproj, d_wgu, d_b_gate, d_gn = _gla_bwd(proj, w["a_w_gate_up"], w["a_b_gate"], w["a_gn"], o, states, dog)
    d_a_w_in = _wgrad_cols_tn("a_dwin", dproj, PA_TILE, h0)
    dproj = emit("a", (d_a_w_in, d_a_w_out), dproj)
    dx, _, d_a_norm = _wide_nn("a_dh", dproj, a_w_in, norm=(x0, w["a_norm"], dx))

    grads = dict(
        a_norm=d_a_norm, a_w_in=d_a_w_in, a_w_gate_up=d_wgu, a_b_gate=d_b_gate, a_gn=d_gn, a_w_out=d_a_w_out,
        b_norm=d_b_norm, b_w_in=d_b_w_in, b_conv=d_b_conv, b_w_out=d_b_w_out,
        f_norm=(d_f_norm0, d_f_norm1), f_w_up=(d_up0, d_up1), f_conv=(d_fconv0, d_fconv1), f_w_down=(d_down0, d_down1),
        final_norm=d_final_norm)
    return loss[0, 0], dx, grads


MESH_ID = pl.DeviceIdType.MESH
ANY = pl.BlockSpec(memory_space=pl.ANY)
N_PEERS = N_DEV - 1


def _position():
    return lax.axis_index("x"), lax.axis_index("y"), lax.axis_index("c")


def _slot(px, py, pc):
    return 4 * px + 2 * py + pc


def _all_gather(name, shards):
    n = len(shards)

    def body(*refs):
        ins, outs = refs[:n], refs[n:2 * n]
        send_sems, recv_sems, local_sems = refs[2 * n:]
        x, y, c = _position()
        me, sibling = (x, y, c), (x, y, 1 - c)
        chips = [(1 - x, y), (x, 1 - y), (1 - x, 1 - y)]

        def copy(t, k, block, to, from_input=False):
            dst = outs[t].at[_slot(*block)]
            return pltpu.make_async_remote_copy(
                src_ref=ins[t] if from_input else dst, dst_ref=dst, send_sem=send_sems.at[t, k], recv_sem=recv_sems.at[t, k],
                device_id=to, device_id_type=MESH_ID)

        mine = [pltpu.make_async_copy(ins[t], outs[t].at[_slot(*me)], local_sems.at[t]) for t in range(n)]
        for cp in mine:
            cp.start()
        first = []
        for t in range(n):
            first.append(copy(t, 0, me, sibling, True))
            first += [copy(t, 1 + j, me, (*chip, c), True) for j, chip in enumerate(chips)]
        for cp in first:
            cp.start()
        passed = []
        for t in range(n):
            for j, chip in enumerate(chips):
                copy(t, 1 + j, (*chip, c), me).wait_recv()
                fwd = copy(t, 4 + j, (*chip, c), sibling)
                fwd.start()
                passed.append(fwd)
        for t in range(n):
            copy(t, 0, sibling, me).wait_recv()
            for j, chip in enumerate(chips):
                copy(t, 4 + j, (*chip, 1 - c), me).wait_recv()
        for cp in first + passed:
            cp.wait_send()
        for cp in mine:
            cp.wait()

    return pl.pallas_call(
        body, name=name, in_specs=[ANY] * n, out_specs=[ANY] * n,
        out_shape=[jax.ShapeDtypeStruct((N_DEV,) + s.shape, s.dtype) for s in shards],
        scratch_shapes=[pltpu.SemaphoreType.DMA((n, N_PEERS)), pltpu.SemaphoreType.DMA((n, N_PEERS)), pltpu.SemaphoreType.DMA((n,))],
    )(*shards)


SIBLING_AND_SAME_CORE = (1, 2, 4, 6)
SAME_CORE = (2, 4, 6)


def _flip(x, y, c, k):
    return x ^ (k >> 2), y ^ ((k >> 1) & 1), c ^ (k & 1)


N_CHIPS = N_DEV // 2


def _chip(px, py):
    return 2 * px + py


def _pair_swap(name, parts):
    n = len(parts)

    def body(*refs):
        ins, outs = refs[:n], refs[n:2 * n]
        send_sems, recv_sems = refs[2 * n:]
        x, y, c = _position()
        sibling = (x, y, 1 - c)
        sent = []
        for t in range(n):
            for q in range(N_CHIPS):
                sent.append(pltpu.make_async_remote_copy(
                    src_ref=ins[t].at[2 * q + 1 - c], dst_ref=outs[t].at[q], send_sem=send_sems.at[t, q],
                    recv_sem=recv_sems.at[t, q], device_id=sibling, device_id_type=MESH_ID))
        for cp in sent:
            cp.start()
        for t in range(n):
            for q in range(N_CHIPS):
                landed = outs[t].at[q]
                pltpu.make_async_remote_copy(
                    src_ref=landed, dst_ref=landed, send_sem=send_sems.at[t, q], recv_sem=recv_sems.at[t, q],
                    device_id=sibling, device_id_type=MESH_ID).wait_recv()
        for cp in sent:
            cp.wait_send()

    sems = pltpu.SemaphoreType.DMA((n, N_CHIPS))
    return pl.pallas_call(
        body, name=name, in_specs=[ANY] * n, out_specs=[ANY] * n,
        out_shape=[jax.ShapeDtypeStruct((N_CHIPS,) + p.shape[1:], p.dtype) for p in parts], scratch_shapes=[sems, sems],
    )(*parts)


PAIR_ROWS = 1024


def _pair_add(name, part, received, side):
    _, rows, cols = part.shape
    tiles = [t for t in range(PAIR_ROWS, 0, -BF16_ROWS) if rows % t == 0]
    tr = tiles[0] if tiles else rows

    def body(side_ref, p_ref, r_ref, o_ref):
        o_ref[...] = (p_ref[...].astype(F32) + r_ref[...].astype(F32)).astype(BF16)

    tile = _spec((None, tr, cols), lambda q, i, side_ref: (q, i, 0))
    return pl.pallas_call(
        body, name=name,
        grid_spec=pltpu.PrefetchScalarGridSpec(
            num_scalar_prefetch=1, grid=(N_CHIPS, rows // tr),
            in_specs=[_spec((None, tr, cols), lambda q, i, side_ref: (2 * q + side_ref[0], i, 0)), tile], out_specs=tile),
        out_shape=jax.ShapeDtypeStruct((N_CHIPS, rows, cols), BF16), compiler_params=_params(("parallel", "parallel")),
    )(side, part, received)


def _send_copy(parts, landing, send_sems, recv_sems, t, s, k):
    x, y, c = _position()
    px, py, _ = _flip(x, y, c, k)
    return pltpu.make_async_remote_copy(
        src_ref=parts[t].at[_chip(px, py)], dst_ref=landing[t].at[_chip(x, y)], send_sem=send_sems.at[s],
        recv_sem=recv_sems.at[s], device_id=(px, py, c), device_id_type=MESH_ID)


def _send_arrival(landing, send_sems, recv_sems, t, s, k):
    x, y, c = _position()
    px, py, _ = _flip(x, y, c, k)
    landed = landing[t].at[_chip(px, py)]
    return pltpu.make_async_remote_copy(
        src_ref=landed, dst_ref=landed, send_sem=send_sems.at[s], recv_sem=recv_sems.at[s],
        device_id=(px, py, c), device_id_type=MESH_ID)


def _handshake(peers):
    x, y, c = _position()
    barrier = pltpu.get_barrier_semaphore()
    for k in peers:
        pl.semaphore_signal(barrier, inc=1, device_id=_flip(x, y, c, k), device_id_type=MESH_ID)
    pl.semaphore_wait(barrier, len(peers))


def _sequencer(name, collective_id, n_copies, body, operands, out_type):
    n_arrays = len(operands)
    return pl.kernel(
        body, out_type=out_type, mesh=plsc.ScalarSubcoreMesh(axis_name="sequencer", num_cores=1), name=name,
        scratch_types=(pltpu.SemaphoreType.DMA((n_copies,)), pltpu.SemaphoreType.DMA((n_copies,)),
                       pltpu.SemaphoreType.DMA((n_arrays,))),
        compiler_params=pltpu.CompilerParams(collective_id=collective_id))(*operands)


def _sequencer_exchange(name, collective_id, parts, after=()):
    n, n_peers, n_in = len(parts), len(SAME_CORE), len(parts) + len(after)

    def body(*refs):
        src, landing = refs[:n], refs[n_in:n_in + n]
        send_sems, recv_sems, local_sems = refs[n_in + n:]
        _handshake(SAME_CORE)
        x, y, _ = _position()
        mine = [pltpu.make_async_copy(src[t].at[_chip(x, y)], landing[t].at[_chip(x, y)], local_sems.at[t]) for t in range(n)]
        for cp in mine:
            cp.start()
        sent = [_send_copy(src, landing, send_sems, recv_sems, t, t * n_peers + j, k)
                for t in range(n) for j, k in enumerate(SAME_CORE)]
        for cp in sent:
            cp.start()
        for t in range(n):
            for j, k in enumerate(SAME_CORE):
                _send_arrival(landing, send_sems, recv_sems, t, t * n_peers + j, k).wait_recv()
        for cp in sent:
            cp.wait_send()
        for cp in mine:
            cp.wait()

    landing = [jax.ShapeDtypeStruct(p.shape, p.dtype) for p in parts]
    return _sequencer(name, collective_id, n * n_peers, body, list(parts) + list(after), landing)


def _sequencer_gather(name, collective_id, shards):
    n, per = len(shards), N_PEERS

    def body(*refs):
        src, out = refs[:n], refs[n:2 * n]
        send_sems, recv_sems, local_sems = refs[2 * n:]
        _handshake(SIBLING_AND_SAME_CORE)
        x, y, c = _position()
        me, sibling = (x, y, c), (x, y, 1 - c)

        def copy(t, j, block, to, from_input=False):
            dst = out[t].at[_slot(*block)]
            return pltpu.make_async_remote_copy(
                src_ref=src[t] if from_input else dst, dst_ref=dst, send_sem=send_sems.at[t * per + j],
                recv_sem=recv_sems.at[t * per + j], device_id=to, device_id_type=MESH_ID)

        mine = [pltpu.make_async_copy(src[t], out[t].at[_slot(*me)], local_sems.at[t]) for t in range(n)]
        for cp in mine:
            cp.start()
        sent = [copy(t, j, me, _flip(x, y, c, k), True) for t in range(n) for j, k in enumerate(SIBLING_AND_SAME_CORE)]
        for cp in sent:
            cp.start()
        for t in range(n):
            for j, k in enumerate(SAME_CORE):
                block = _flip(x, y, c, k)
                copy(t, 1 + j, block, me).wait_recv()
                forward = copy(t, 4 + j, block, sibling)
                forward.start()
                sent.append(forward)
        for t in range(n):
            copy(t, 0, sibling, me).wait_recv()
            for j, k in enumerate(SAME_CORE):
                copy(t, 4 + j, _flip(x, y, 1 - c, k), me).wait_recv()
        for cp in sent:
            cp.wait_send()
        for cp in mine:
            cp.wait()

    gathered = [jax.ShapeDtypeStruct((N_DEV,) + s.shape, s.dtype) for s in shards]
    return _sequencer(name, collective_id, n * per, body, shards, gathered)


ADAM_ROWS = 512
BF16_ROWS = 16


def _adam_update(w, g, m, v):
    m = ADAM_B1 * m + (1.0 - ADAM_B1) * g
    v = ADAM_B2 * v + (1.0 - ADAM_B2) * (g * g)
    m_hat = m / (1.0 - ADAM_B1 ** ADAM_STEP)
    v_hat = v / (1.0 - ADAM_B2 ** ADAM_STEP)
    delta = -ADAM_LR * (m_hat / (jnp.sqrt(v_hat) + ADAM_EPS) + ADAM_WD * w)
    return delta, m, v


def _sum_slots(ref):
    total = ref[0].astype(F32)
    for d in range(1, ref.shape[0]):
        total = total + ref[d].astype(F32)
    return total


def _adamw_sum(name, landed, w, m, v):
    layers, rows, cols = w.shape
    tiles = [t for t in range(ADAM_ROWS, 0, -BF16_ROWS) if rows % t == 0]
    tr = tiles[0] if tiles else rows
    nt = rows // tr

    def body(*refs):
        parts = refs[:layers]
        w_ref, m_ref, v_ref, g_ref, d_ref, nm_ref, nv_ref = refs[layers:]
        layer = pl.program_id(0)
        g = _sum_slots(parts[0])
        for q in range(1, layers):
            g = jnp.where(layer == q, _sum_slots(parts[q]), g)
        delta, new_m, new_v = _adam_update(w_ref[...], g, m_ref[...], v_ref[...])
        g_ref[...] = g
        d_ref[...] = delta
        nm_ref[...] = new_m
        nv_ref[...] = new_v

    def part_spec(q):
        return _spec((N_CHIPS, tr, cols), lambda l, i: (0, jnp.where(l == q, i, jnp.where(l < q, 0, nt - 1)), 0))

    tile = _spec((None, tr, cols), lambda l, i: (l, i, 0))
    out = jax.ShapeDtypeStruct((layers, rows, cols), F32)
    return pl.pallas_call(
        body, name=name, grid=(layers, nt), in_specs=[part_spec(q) for q in range(layers)] + [tile] * 3,
        out_specs=[tile] * 4, out_shape=[out] * 4, compiler_params=_params(("arbitrary", "arbitrary")),
    )(*landed, w, m, v)


def _sum_small(landed):
    def body(in_ref, out_ref):
        out_ref[...] = _sum_slots(in_ref)

    return pl.pallas_call(body, name="small_grad_sum", out_shape=jax.ShapeDtypeStruct(landed.shape[1:], F32))(landed)


def _adamw_small(name, g, w, m, v):
    def body(g_ref, w_ref, m_ref, v_ref, d_ref, nm_ref, nv_ref):
        d_ref[...], nm_ref[...], nv_ref[...] = _adam_update(w_ref[...], g_ref[...], m_ref[...], v_ref[...])

    out = jax.ShapeDtypeStruct(w.shape, F32)
    return pl.pallas_call(body, name=name, out_shape=[out] * 3)(g, w, m, v)


LANES = 128
SUBLANES = 8
F_CONV_SHARD = D_FF // N_DEV
GATE_SHARD = KEY_DIM // N_DEV
NORM_SHARD = D_MODEL // N_DEV


def _tile_rows(a):
    flat = a.reshape(-1)
    size = -(-flat.shape[0] // (SUBLANES * LANES)) * SUBLANES * LANES
    return jnp.pad(flat, (0, size - flat.shape[0])).reshape(-1, LANES)


def _pack_rows(pieces):
    return jnp.concatenate([_tile_rows(p) for p in pieces], axis=0)


def _unpack_rows(packed, shapes):
    out, row = [], 0
    for shape in shapes:
        size = 1
        for s in shape:
            size *= s
        rows = -(-size // (SUBLANES * LANES)) * SUBLANES
        piece = packed[..., row:row + rows, :]
        out.append(piece.reshape(piece.shape[:-2] + (rows * LANES,))[..., :size])
        row += rows
    return out


SMALL_SHARDS = ((GATE_RANK, GATE_SHARD), (1, NORM_SHARD), (3, NORM_SHARD), (2, 3, F_CONV_SHARD))


def _unpack_small_shards(g):
    gate, b_norm, b_conv, f_conv = _unpack_rows(g, SMALL_SHARDS)
    gate = gate.reshape(N_DEV, GATE_RANK, GATE_SHARD).transpose(1, 0, 2).reshape(GATE_RANK, KEY_DIM)
    b_norm = b_norm.reshape(1, D_MODEL)
    b_conv = b_conv.reshape(N_DEV, 3, NORM_SHARD).transpose(1, 0, 2).reshape(3, D_MODEL)
    f_conv = f_conv.reshape(N_DEV, 2, 3, F_CONV_SHARD).transpose(1, 2, 0, 3).reshape(2, 3, D_FF)
    return gate, b_norm, b_conv, f_conv


SMALL_LAYOUT = (("a_norm", (1, D_MODEL)), ("a_w_gate_up", (GATE_RANK, KEY_DIM)), ("a_b_gate", (1, KEY_DIM)), ("a_gn", (1, VALUE_DIM)),
                ("b_norm", (1, D_MODEL)), ("b_conv", (3, D_MODEL)), ("f_norm0", (1, D_MODEL)), ("f_norm1", (1, D_MODEL)),
                ("f_conv0", (3, D_FF)), ("f_conv1", (3, D_FF)), ("final_norm", (1, D_MODEL)))


def _pack_small_grads(g):
    full = dict(g)
    full["a_w_gate_up"] = g["a_w_gate_up"][:GATE_RANK]
    for layer in range(2):
        full[f"f_norm{layer}"] = g["f_norm"][layer]
        full[f"f_conv{layer}"] = g["f_conv"][layer]
    return _pack_rows([full[name] for name, _ in SMALL_LAYOUT])


def _unpack_small_grads(packed):
    pieces = _unpack_rows(packed, [shape for _, shape in SMALL_LAYOUT])
    out = {name: piece.reshape(shape) for (name, shape), piece in zip(SMALL_LAYOUT, pieces)}
    out["f_norm"] = jnp.stack([out["f_norm0"][0], out["f_norm1"][0]])
    out["f_conv"] = jnp.stack([out["f_conv0"], out["f_conv1"]])
    return out


def kernel(x, a_norm, a_w_in, a_w_gate_up, a_b_gate, a_gn, a_w_out, b_norm, b_w_in, b_conv, b_w_out, f_norm, f_w_up, f_conv, f_w_down, final_norm, loss_target, m_a_norm, m_a_w_in, m_a_w_gate_up, m_a_b_gate, m_a_gn, m_a_w_out, m_b_norm, m_b_w_in, m_b_conv, m_b_w_out, m_f_norm, m_f_w_up, m_f_conv, m_f_w_down, m_final_norm, v_a_norm, v_a_w_in, v_a_w_gate_up, v_a_b_gate, v_a_gn, v_a_w_out, v_b_norm, v_b_w_in, v_b_conv, v_b_w_out, v_f_norm, v_f_w_up, v_f_conv, v_f_w_down, v_final_norm):
    my_slot = _slot(*_position())

    transposed = lambda w: jnp.swapaxes(w, 1, 2)
    a_w_in_t, f_w_up_t = transposed(a_w_in), transposed(f_w_up)
    first = _all_gather("weight_gather", [a_w_in_t[0].astype(BF16), a_w_out[0].astype(BF16),
                                          _pack_rows([a_w_gate_up[0], b_norm, b_conv[0], f_conv])])
    gathers, small_shards = {}, first[2]
    later = (("f0", f_w_up_t[0], f_w_down[0]), ("b", b_w_in[0], b_w_out[0]), ("f1", f_w_up_t[1], f_w_down[1]))
    for collective_id, (group, w_in, w_out) in enumerate(later):
        w_in, w_out, small_shards = lax.optimization_barrier((w_in.astype(BF16), w_out.astype(BF16), small_shards))
        gathers[group] = _sequencer_gather(f"gather_{group}", collective_id, [w_in, w_out])
    gate_full, b_norm_full, b_conv_full, f_conv_full = _unpack_small_shards(small_shards)
    a_w_in_full = jnp.pad(first[0].reshape(PROJ_A, D_MODEL), ((0, PROJ_A_PAD - PROJ_A), (0, 0)))
    weights = dict(
        a_norm=a_norm, a_w_gate_up=jnp.pad(gate_full, ((0, GATE_PAD - GATE_RANK), (0, 0))).astype(BF16), a_b_gate=a_b_gate,
        a_gn=a_gn, b_norm=b_norm_full, b_conv=b_conv_full, f_norm=f_norm, f_conv=f_conv_full,
        final_norm=final_norm.reshape(1, D_MODEL))

    def fetch(group, after):
        if group == "a":
            return a_w_in_full, first[1].reshape(D_MODEL, D_MODEL)
        w_in, w_out = gathers[group]
        if group == "b":
            return w_in, w_out.reshape(D_MODEL, D_MODEL)
        return w_in.reshape(2, D_FF, D_MODEL), w_out.reshape(D_FF, D_MODEL)

    exchanges, pending = {}, []
    exchange_ids = dict(b=3, f0=4, a=5)
    side = lax.axis_index("c").astype(jnp.int32).reshape(1)

    def emit(group, grads, carry):
        d_in, d_out = grads
        if group == "a":
            d_in = d_in[:PROJ_A]
        d_in, d_out = d_in.reshape((N_DEV, -1) + d_in.shape[-1:]), d_out.reshape((N_DEV, -1, D_MODEL))
        received = _pair_swap(f"pair_swap_{group}", [d_in, d_out])
        sums = [_pair_add(f"pair_add_{group}_{i}", part, got, side) for i, (part, got) in enumerate(zip((d_in, d_out), received))]
        carry, *sums = lax.optimization_barrier((carry, *sums))
        pending.extend(sums)
        if group != "f1":
            after = list(exchanges.values())[-1][:1] if exchanges else ()
            exchanges[group] = _sequencer_exchange(f"grads_{group}", exchange_ids[group], list(pending), after)
            pending.clear()
        return carry

    loss, dx, g = _local_step(x[0], loss_target[0], weights, fetch, emit)
    loss = lax.psum(loss, MESH_AXES)
    small_landed = _all_gather("small_grad_gather", [_pack_small_grads(g)])[0]

    (up1, down1, d_b_in, d_b_out), (up0, down0), (d_a_in, d_a_out) = (exchanges[group] for group in ("b", "f0", "a"))
    back = lambda results: tuple(transposed(r) for r in results)
    big = dict(
        b_w_in=_adamw_sum("adam_b_w_in", [d_b_in], b_w_in, m_b_w_in, v_b_w_in),
        b_w_out=_adamw_sum("adam_b_w_out", [d_b_out], b_w_out, m_b_w_out, v_b_w_out),
        f_w_up=back(_adamw_sum("adam_f_w_up", [up0, up1], f_w_up_t, transposed(m_f_w_up), transposed(v_f_w_up))),
        f_w_down=_adamw_sum("adam_f_w_down", [down0, down1], f_w_down, m_f_w_down, v_f_w_down),
        a_w_in=back(_adamw_sum("adam_a_w_in", [d_a_in], a_w_in_t, transposed(m_a_w_in), transposed(v_a_w_in))),
        a_w_out=_adamw_sum("adam_a_w_out", [d_a_out], a_w_out, m_a_w_out, v_a_w_out))
    small_g = _unpack_small_grads(_sum_small(small_landed))
    small_g["a_w_gate_up"] = lax.dynamic_slice_in_dim(small_g["a_w_gate_up"], my_slot * GATE_SHARD, GATE_SHARD, axis=1)
    small_g["b_norm"] = lax.dynamic_slice_in_dim(small_g["b_norm"], my_slot * NORM_SHARD, NORM_SHARD, axis=1)
    small_g["b_conv"] = lax.dynamic_slice_in_dim(small_g["b_conv"], my_slot * NORM_SHARD, NORM_SHARD, axis=1)
    small_g["f_conv"] = lax.dynamic_slice_in_dim(small_g["f_conv"], my_slot * F_CONV_SHARD, F_CONV_SHARD, axis=2)
    small_w = dict(
        a_norm=(a_norm, m_a_norm, v_a_norm), a_w_gate_up=(a_w_gate_up, m_a_w_gate_up, v_a_w_gate_up),
        a_b_gate=(a_b_gate, m_a_b_gate, v_a_b_gate), a_gn=(a_gn, m_a_gn, v_a_gn), b_norm=(b_norm, m_b_norm, v_b_norm),
        b_conv=(b_conv, m_b_conv, v_b_conv), f_norm=(f_norm, m_f_norm, v_f_norm), f_conv=(f_conv, m_f_conv, v_f_conv),
        final_norm=(final_norm, m_final_norm, v_final_norm))
    small = {}
    for name, (w, m, v) in small_w.items():
        flat = (w.shape[-1],) if w.ndim == 1 else w.shape[-2:]
        two_d = (-1, flat[-1])
        grad = small_g[name].reshape(w.shape)
        delta, new_m, new_v = _adamw_small(
            "adam_" + name, grad.reshape(two_d), w.reshape(two_d), m.reshape(two_d), v.reshape(two_d))
        small[name] = (grad, delta.reshape(w.shape), new_m.reshape(w.shape), new_v.reshape(w.shape))

    order = ["a_norm", "a_w_in", "a_w_gate_up", "a_b_gate", "a_gn", "a_w_out", "b_norm", "b_w_in", "b_conv", "b_w_out",
             "f_norm", "f_w_up", "f_conv", "f_w_down", "final_norm"]
    results = {**big, **small}
    outputs = [loss, dx.reshape(1, SEQ, D_MODEL)]
    for kind in range(4):
        outputs += [results[name][kind] for name in order]
    return tuple(outputs)
```

```python
import jax
import jax.numpy as jnp
from jax import lax
from jax.experimental import pallas as pl
from jax.experimental.pallas import tpu as pltpu
from jax.experimental.pallas import tpu_sc as plsc

F32 = jnp.float32
BF16 = jnp.bfloat16

N_DEV = 8
SEQ = 2048
D_MODEL = 1024
CHUNK = 64
N_CHUNKS = SEQ // CHUNK
RMS_EPS = 1e-6
GLA_HEADS = 4
KEY_DIM = 512
VALUE_DIM = 1024
HEAD_K = KEY_DIM // GLA_HEADS
HEAD_V = VALUE_DIM // GLA_HEADS
GATE_RANK = 16
GATE_PAD = 128
GATE_NORMALIZER = 16.0
PROJ_A = 2 * KEY_DIM + 2 * VALUE_DIM + GATE_RANK
PROJ_A_PAD = 2 * KEY_DIM + 2 * VALUE_DIM + GATE_PAD
A_SHARD = PROJ_A // N_DEV
B_SHARD = 3 * D_MODEL // N_DEV
D_FF = 2816
ADAM_LR = 0.001
ADAM_B1 = 0.9
ADAM_B2 = 0.999
ADAM_EPS = 1e-08
ADAM_WD = 0.01
ADAM_STEP = 10
MESH_AXES = ("x", "y", "c")

VMEM_LIMIT = 56 * 1024 * 1024
ROW_CHUNK = 256
HALO = 16


def _params(sem=None, vmem=VMEM_LIMIT):
    return pltpu.CompilerParams(dimension_semantics=sem, vmem_limit_bytes=vmem)


NN = ((1,), (0,))
NT = ((1,), (1,))
TN = ((0,), (0,))


def _matmul(name, a, a_spec, b, b_spec, dims, grid, out_shape, out_spec, k_blocks=None, a_block_cols=None, res=None,
            res_spec=None, transpose_out=False, norm=None):
    has_res = res is not None

    def body(*refs):
        a_ref, b_ref = refs[0], refs[1]
        r_ref = refs[2] if has_res else None

        def product(lhs, rhs):
            return lax.dot_general(lhs.astype(BF16), rhs, (dims, ((), ())), preferred_element_type=F32)

        if k_blocks is None:
            v = product(a_ref[...], b_ref[...])
        else:
            v = None
            for k in range(k_blocks):
                lhs = a_ref[k] if a_block_cols is None else a_ref[:, k * a_block_cols:(k + 1) * a_block_cols]
                p = product(lhs, b_ref[k])
                v = p if v is None else v + p
        if transpose_out:
            v = v.T
        if has_res:
            v = v + r_ref[...]
        if norm is None:
            o_ref = refs[2 + has_res]
            o_ref[...] = v.astype(o_ref.dtype)
            return
        x_ref, g_ref, dxi_ref, dx_ref, dx16_ref, dg_ref = refs[2 + has_res:]
        dx, dg = _norm_bwd_rows(x_ref[...], g_ref[...], v)
        dx = dxi_ref[...] + dx
        dx_ref[...] = dx
        dx16_ref[...] = dx.astype(BF16)

        @pl.when(pl.program_id(0) == 0)
        def _():
            dg_ref[...] = dg

        @pl.when(pl.program_id(0) > 0)
        def _():
            dg_ref[...] += dg

    operands = [a, b] + ([res] if has_res else [])
    in_specs = [a_spec, b_spec] + ([res_spec] if has_res else [])
    semantics = ("parallel",) * len(grid)
    if norm is not None:
        vec = _spec((1, D_MODEL), lambda i: (0, 0))
        operands += list(norm)
        in_specs += [out_spec, vec, out_spec]
        out_shape, out_spec = [_act(dtype=F32), _act(), jax.ShapeDtypeStruct((1, D_MODEL), F32)], [out_spec, out_spec, vec]
        semantics = ("arbitrary",)
    return pl.pallas_call(
        body, name=name, grid=grid, in_specs=in_specs, out_specs=out_spec, out_shape=out_shape,
        compiler_params=_params(semantics),
    )(*operands)


def _resident(shape):
    return pl.BlockSpec(shape, lambda *_: (0,) * len(shape), pipeline_mode=pl.Buffered(1))


TM = 512
N_TM = SEQ // TM
PA_TILE = 640
N_PA = PROJ_A_PAD // PA_TILE
OUT_TILE = 256


def _spec(shape, fn):
    return pl.BlockSpec(shape, fn)


def _act(shape=(SEQ, D_MODEL), dtype=BF16):
    return jax.ShapeDtypeStruct(shape, dtype)


def _proj_rows_nt(name, h, wt, n_tile):
    n = wt.shape[0]
    return _matmul(name, h, _resident((SEQ, D_MODEL)), wt, _spec((n_tile, D_MODEL), lambda j: (j, 0)), NT,
                   (n // n_tile,), _act((SEQ, n)), _spec((SEQ, n_tile), lambda j: (0, j)))


def _proj_cols_nn(name, h, w_blocks):
    nb, _, n = w_blocks.shape
    return _matmul(name, h, _resident((SEQ, D_MODEL)), w_blocks, _spec((None, D_MODEL, n), lambda j: (j, 0, 0)),
                   NN, (nb,), _act((SEQ, nb * n)), _spec((SEQ, n), lambda j: (0, j)))


def _square(name, a, w, dims, x=None):
    row = _spec((TM, D_MODEL), lambda i: (i, 0))
    return _matmul(name, a, row, w, _resident((D_MODEL, D_MODEL)), dims, (N_TM,),
                   _act(dtype=F32 if x is not None else BF16), row, res=x, res_spec=row if x is not None else None)


def _sum_blocks_nn(name, a_blocks, w_blocks, x=None, norm=None):
    nb, _, n = a_blocks.shape
    row = _spec((TM, D_MODEL), lambda i: (i, 0))
    return _matmul(name, a_blocks, _spec((nb, TM, n), lambda i: (0, i, 0)), w_blocks, _resident((nb, n, D_MODEL)),
                   NN, (N_TM,), _act(dtype=F32), row, k_blocks=nb, res=x, res_spec=row if x is not None else None, norm=norm)


def _sum_cols_nt(name, d, w_blocks, norm=None):
    nb, _, n = w_blocks.shape
    return _matmul(name, d, _spec((TM, nb * n), lambda i: (i, 0)), w_blocks, _resident((nb, D_MODEL, n)), NT,
                   (N_TM,), _act(dtype=F32), _spec((TM, D_MODEL), lambda i: (i, 0)), k_blocks=nb, a_block_cols=n, norm=norm)


def _wide_nn(name, d, wt, x=None, norm=None):
    n = wt.shape[0]
    row = _spec((TM, D_MODEL), lambda i: (i, 0))
    return _matmul(name, d, _spec((TM, n), lambda i: (i, 0)), wt, _resident((n, D_MODEL)), NN, (N_TM,),
                   _act(dtype=F32), row, res=x, res_spec=row if x is not None else None, norm=norm)


def _wide_nt(name, d, w):
    n = w.shape[0]
    return _matmul(name, d, _spec((TM, D_MODEL), lambda i: (i, 0)), w, _resident((n, D_MODEL)), NT, (N_TM,),
                   _act((SEQ, n)), _spec((TM, n), lambda i: (i, 0)))


def _proj_halves_nt(name, h, wt):
    _, n, _ = wt.shape
    return _matmul(name, h, _spec((TM, D_MODEL), lambda p, i: (i, 0)), wt, _spec((None, n, D_MODEL), lambda p, i: (p, 0, 0)), NT,
                   (2, N_TM), _act((2, SEQ, n)), _spec((None, TM, n), lambda p, i: (p, i, 0)))


def _wgrad_halves_tn(name, d, n_tile, h):
    _, _, n = d.shape
    return _matmul(name, d, _spec((None, SEQ, n_tile), lambda p, j: (p, 0, j)), h, _resident((SEQ, D_MODEL)), TN,
                   (2, n // n_tile), _act((2, n, D_MODEL)), _spec((None, n_tile, D_MODEL), lambda p, j: (p, j, 0)))


def _wgrad_cols_tn(name, d, n_tile, h):
    n = d.shape[1]
    return _matmul(name, d, _spec((SEQ, n_tile), lambda j: (0, j)), h, _resident((SEQ, D_MODEL)), TN,
                   (n // n_tile,), _act((n, D_MODEL)), _spec((n_tile, D_MODEL), lambda j: (j, 0)))


def _wgrad_cols_transposed_tn(name, h, d, n_tile):
    nb = d.shape[1] // n_tile
    return _matmul(name, d, _spec((SEQ, n_tile), lambda j: (0, j)), h, _resident((SEQ, D_MODEL)), TN, (nb,),
                   _act((nb, D_MODEL, n_tile)), _spec((None, D_MODEL, n_tile), lambda j: (j, 0, 0)), transpose_out=True)


NORM_ROWS = 512


def _rstd(x):
    return lax.rsqrt(jnp.mean(x * x, axis=-1, keepdims=True) + RMS_EPS)


def _norm_fwd(name, x, gamma):
    def body(x_ref, g_ref, h_ref):
        x = x_ref[...]
        h_ref[...] = (x * _rstd(x) * g_ref[...]).astype(BF16)

    row = _spec((NORM_ROWS, D_MODEL), lambda i: (i, 0))
    return pl.pallas_call(
        body, name=name, grid=(SEQ // NORM_ROWS,), in_specs=[row, _spec((1, D_MODEL), lambda i: (0, 0))], out_specs=row,
        out_shape=jax.ShapeDtypeStruct((SEQ, D_MODEL), BF16), compiler_params=_params(("parallel",)),
    )(x, gamma)


def _norm_bwd_rows(x, gamma, dh):
    r = _rstd(x)
    xh = x * r
    dxh = dh * gamma
    dx = r * (dxh - xh * jnp.mean(dxh * xh, axis=-1, keepdims=True))
    return dx, jnp.sum(dh * xh, axis=0, keepdims=True)


def _loss_head(x, gamma, target):
    def body(x_ref, g_ref, t_ref, loss_ref, dx_ref, dx16_ref, dg_ref):
        x = x_ref[...]
        gamma = g_ref[...]
        err = x * _rstd(x) * gamma - t_ref[...]
        dy = err * (1.0 / D_MODEL)
        dx, dg = _norm_bwd_rows(x, gamma, dy)
        dx_ref[...] = dx
        dx16_ref[...] = dx.astype(BF16)
        part = 0.5 * jnp.sum(jnp.sum(err * err, axis=-1, keepdims=True) * (1.0 / D_MODEL), axis=0, keepdims=True)
        part = jnp.broadcast_to(part, loss_ref.shape)

        @pl.when(pl.program_id(0) == 0)
        def _():
            dg_ref[...] = dg
            loss_ref[...] = part

        @pl.when(pl.program_id(0) > 0)
        def _():
            dg_ref[...] += dg
            loss_ref[...] += part

    row = _spec((NORM_ROWS, D_MODEL), lambda i: (i, 0))
    vec = _spec((1, D_MODEL), lambda i: (0, 0))
    return pl.pallas_call(
        body, name="loss_head", grid=(SEQ // NORM_ROWS,), in_specs=[row, vec, row],
        out_specs=[_spec((1, 128), lambda i: (0, 0)), row, row, vec],
        out_shape=[jax.ShapeDtypeStruct((1, 128), F32), _act(dtype=F32), _act(), jax.ShapeDtypeStruct((1, D_MODEL), F32)],
        compiler_params=_params(("arbitrary",)),
    )(x, gamma, target)


def _sigmoid(x):
    return 1.0 / (1.0 + jnp.exp(-x))


def _rows(ref, c):
    return ref[pl.ds(pl.multiple_of(c * ROW_CHUNK, ROW_CHUNK), ROW_CHUNK), :].astype(F32)


def _rows_before(ref, c):
    start = pl.multiple_of(jnp.maximum(c * ROW_CHUNK - HALO, 0), HALO)
    rows = ref[pl.ds(start, HALO), :].astype(F32)
    return jnp.where(c > 0, rows, 0.0)


def _rows_after(ref, c, n_chunks):
    start = pl.multiple_of(jnp.minimum((c + 1) * ROW_CHUNK, SEQ - HALO), HALO)
    rows = ref[pl.ds(start, HALO), :].astype(F32)
    return jnp.where(c < n_chunks - 1, rows, 0.0)


def _shift_down(z, before, n):
    return pltpu.roll(jnp.concatenate([before, z], axis=0), n, 0)[HALO:]


def _shift_up(z, after, n):
    rows = z.shape[0]
    return pltpu.roll(jnp.concatenate([z, after], axis=0), rows + HALO - n, 0)[:rows]


def _conv_rows(z, before, w):
    z1 = _shift_down(z, before, 1)
    z2 = _shift_down(z, before, 2)
    return w[2:3, :] * z + w[1:2, :] * z1 + w[0:1, :] * z2, z1, z2


def _conv_t_rows(dy, after, w):
    return w[2:3, :] * dy + w[1:2, :] * _shift_up(dy, after, 1) + w[0:1, :] * _shift_up(dy, after, 2)


N_ROW_CHUNKS = SEQ // ROW_CHUNK


FF_COLS = 256
N_FF_COLS = D_FF // FF_COLS


def _ffn_mid_fwd(name, gu, conv_w):
    def body(gu_ref, w_ref, a_ref):
        w = w_ref[...]

        def chunk(c, carry):
            g = _rows(gu_ref.at[0], c)
            u = _rows(gu_ref.at[1], c)
            gc, _, _ = _conv_rows(g, _rows_before(gu_ref.at[0], c), w)
            a_ref[pl.ds(pl.multiple_of(c * ROW_CHUNK, ROW_CHUNK), ROW_CHUNK), :] = (gc * _sigmoid(gc) * u).astype(BF16)
            return carry

        lax.fori_loop(0, N_ROW_CHUNKS, chunk, 0)

    col = _spec((SEQ, FF_COLS), lambda j: (0, j))
    return pl.pallas_call(
        body, name=name, grid=(N_FF_COLS,),
        in_specs=[_spec((2, SEQ, FF_COLS), lambda j: (0, 0, j)), _spec((3, FF_COLS), lambda j: (0, j))], out_specs=col,
        out_shape=_act((SEQ, D_FF)), compiler_params=_params(("parallel",)),
    )(gu, conv_w)


def _ffn_mid_bwd(name, gu, conv_w, da):
    def body(gu_ref, w_ref, da_ref, dgu_ref, dw_ref, dgc_ref):
        w = w_ref[...]

        def first(c, acc):
            g = _rows(gu_ref.at[0], c)
            u = _rows(gu_ref.at[1], c)
            d = _rows(da_ref, c)
            gc, g1, g2 = _conv_rows(g, _rows_before(gu_ref.at[0], c), w)
            sg = _sigmoid(gc)
            rows = pl.ds(pl.multiple_of(c * ROW_CHUNK, ROW_CHUNK), ROW_CHUNK)
            dgu_ref[1, rows, :] = (d * gc * sg).astype(BF16)
            dgc = d * u * (sg * (1.0 + gc * (1.0 - sg)))
            dgc_ref[rows, :] = dgc
            return (acc[0] + jnp.sum(dgc * g2, axis=0, keepdims=True), acc[1] + jnp.sum(dgc * g1, axis=0, keepdims=True),
                    acc[2] + jnp.sum(dgc * g, axis=0, keepdims=True))

        zero = jnp.zeros((1, FF_COLS), F32)
        acc = lax.fori_loop(0, N_ROW_CHUNKS, first, (zero, zero, zero))
        for r in range(3):
            dw_ref[r:r + 1, :] = acc[r]

        def second(c, carry):
            dgc = _rows(dgc_ref, c)
            dg = _conv_t_rows(dgc, _rows_after(dgc_ref, c, N_ROW_CHUNKS), w)
            dgu_ref[0, pl.ds(pl.multiple_of(c * ROW_CHUNK, ROW_CHUNK), ROW_CHUNK), :] = dg.astype(BF16)
            return carry

        lax.fori_loop(0, N_ROW_CHUNKS, second, 0)

    pair = _spec((2, SEQ, FF_COLS), lambda j: (0, 0, j))
    wspec = _spec((3, FF_COLS), lambda j: (0, j))
    return pl.pallas_call(
        body, name=name, grid=(N_FF_COLS,), in_specs=[pair, wspec, _spec((SEQ, FF_COLS), lambda j: (0, j))],
        out_specs=[pair, wspec], out_shape=[_act((2, SEQ, D_FF)), jax.ShapeDtypeStruct((3, D_FF), F32)],
        scratch_shapes=[pltpu.VMEM((SEQ, FF_COLS), F32)],
        compiler_params=_params(("parallel",)),
    )(gu, conv_w, da)


SC_COLS = 256
N_SC = D_MODEL // SC_COLS


def _sc_specs():
    return [_spec((SEQ, SC_COLS), lambda j, part=part: (0, part * N_SC + j)) for part in range(3)]


def _sc_mid_fwd(p, conv_w):
    def body(b_ref, c_ref, h_ref, w_ref, y_ref):
        w = w_ref[...]

        def chunk(c, carry):
            z = _rows(c_ref, c) * _rows(h_ref, c)
            before = _rows_before(c_ref, c) * _rows_before(h_ref, c)
            zc, _, _ = _conv_rows(z, before, w)
            y_ref[pl.ds(pl.multiple_of(c * ROW_CHUNK, ROW_CHUNK), ROW_CHUNK), :] = (_rows(b_ref, c) * zc).astype(BF16)
            return carry

        lax.fori_loop(0, N_ROW_CHUNKS, chunk, 0)

    col = _spec((SEQ, SC_COLS), lambda j: (0, j))
    return pl.pallas_call(
        body, name="sc_mid_fwd", grid=(N_SC,), in_specs=_sc_specs() + [_spec((3, SC_COLS), lambda j: (0, j))], out_specs=col,
        out_shape=jax.ShapeDtypeStruct((SEQ, D_MODEL), BF16), compiler_params=_params(("parallel",)),
    )(p, p, p, conv_w)


def _sc_mid_bwd(p, conv_w, dy):
    def body(b_ref, c_ref, h_ref, w_ref, dy_ref, db_ref, dc_ref, dh_ref, dw_ref, dzc_ref):
        w = w_ref[...]

        def first(c, acc):
            z = _rows(c_ref, c) * _rows(h_ref, c)
            before = _rows_before(c_ref, c) * _rows_before(h_ref, c)
            zc, z1, z2 = _conv_rows(z, before, w)
            d = _rows(dy_ref, c)
            rows = pl.ds(pl.multiple_of(c * ROW_CHUNK, ROW_CHUNK), ROW_CHUNK)
            db_ref[rows, :] = (d * zc).astype(BF16)
            dzc = d * _rows(b_ref, c)
            dzc_ref[rows, :] = dzc
            return (acc[0] + jnp.sum(dzc * z2, axis=0, keepdims=True), acc[1] + jnp.sum(dzc * z1, axis=0, keepdims=True),
                    acc[2] + jnp.sum(dzc * z, axis=0, keepdims=True))

        zero = jnp.zeros((1, SC_COLS), F32)
        acc = lax.fori_loop(0, N_ROW_CHUNKS, first, (zero, zero, zero))
        for r in range(3):
            dw_ref[r:r + 1, :] = acc[r]

        def second(c, carry):
            dz = _conv_t_rows(_rows(dzc_ref, c), _rows_after(dzc_ref, c, N_ROW_CHUNKS), w)
            rows = pl.ds(pl.multiple_of(c * ROW_CHUNK, ROW_CHUNK), ROW_CHUNK)
            dc_ref[rows, :] = (dz * _rows(h_ref, c)).astype(BF16)
            dh_ref[rows, :] = (dz * _rows(c_ref, c)).astype(BF16)
            return carry

        lax.fori_loop(0, N_ROW_CHUNKS, second, 0)

    col = _spec((SEQ, SC_COLS), lambda j: (0, j))
    wspec = _spec((3, SC_COLS), lambda j: (0, j))
    act = jax.ShapeDtypeStruct((SEQ, D_MODEL), BF16)
    return pl.pallas_call(
        body, name="sc_mid_bwd", grid=(N_SC,), in_specs=_sc_specs() + [wspec, col], out_specs=[col, col, col, wspec],
        out_shape=[act, act, act, jax.ShapeDtypeStruct((3, D_MODEL), F32)],
        scratch_shapes=[pltpu.VMEM((SEQ, SC_COLS), F32)], compiler_params=_params(("parallel",)),
    )(p, p, p, conv_w, dy)


GLA_GROUP = 4
GLA_ROWS = GLA_GROUP * CHUNK
N_GROUPS = N_CHUNKS // GLA_GROUP
Q0, K0, V0, R0, G0 = 0, KEY_DIM, 2 * KEY_DIM, 2 * KEY_DIM + VALUE_DIM, 2 * KEY_DIM + 2 * VALUE_DIM


def _tri(strict):
    r = lax.broadcasted_iota(jnp.int32, (CHUNK, CHUNK), 0)
    c = lax.broadcasted_iota(jnp.int32, (CHUNK, CHUNK), 1)
    return jnp.where(c < r if strict else c <= r, 1.0, 0.0).astype(F32)


def _cumsum_rows(tri, x):
    return jnp.dot(tri, x, preferred_element_type=F32, precision=lax.Precision.HIGHEST)


def _gate_logits(gl, wgu, b_gate):
    return jnp.dot(gl, wgu, preferred_element_type=F32) + b_gate


def _log_decay(logits):
    return (jnp.minimum(logits, 0.0) - jnp.log(1.0 + jnp.exp(-jnp.abs(logits)))) * (1.0 / GATE_NORMALIZER)


def _head(x, h, width):
    return x[:, h * width:(h + 1) * width]


def _gla_fwd(proj, wgu, b_gate, gn):
    def body(p_ref, wgu_ref, b_ref, gn_ref, o_ref, og_ref, st_ref, state):
        @pl.when(pl.program_id(0) == 0)
        def _():
            state[...] = jnp.zeros_like(state)

        tri = _tri(False)
        la = _log_decay(_gate_logits(p_ref[:, G0:G0 + GATE_PAD], wgu_ref[...], b_ref[...]))
        for c in range(GLA_GROUP):
            rows = slice(c * CHUNK, (c + 1) * CHUNK)
            cum = _cumsum_rows(tri, la[rows])
            tot = cum[CHUNK - 1:CHUNK, :]
            kd = (p_ref[rows, K0:K0 + KEY_DIM].astype(F32) * jnp.exp(tot - cum)).astype(BF16)
            decay = jnp.exp(tot)
            q = (p_ref[rows, Q0:Q0 + KEY_DIM].astype(F32) * (HEAD_K ** -0.5)).astype(BF16)
            v = p_ref[rows, V0:V0 + VALUE_DIM]
            for h in range(GLA_HEADS):
                upd = lax.dot_general(_head(v, h, HEAD_V), _head(kd, h, HEAD_K), (TN, ((), ())), preferred_element_type=F32)
                s = state[h] * _head(decay, h, HEAD_K) + upd
                state[h] = s
                st_ref[c, h] = s
                o_ref[rows, h * HEAD_V:(h + 1) * HEAD_V] = lax.dot_general(
                    _head(q, h, HEAD_K), s.astype(BF16), (NT, ((), ())), preferred_element_type=F32)
        r = p_ref[:, R0:R0 + VALUE_DIM].astype(F32)
        gate = r * _sigmoid(r) * gn_ref[...]
        for h in range(GLA_HEADS):
            cols = slice(h * HEAD_V, (h + 1) * HEAD_V)
            o = o_ref[:, cols]
            og_ref[:, cols] = (o * _rstd(o) * gate[:, cols]).astype(BF16)

    rows = _spec((GLA_ROWS, VALUE_DIM), lambda i: (i, 0))
    const = lambda shape: _spec(shape, lambda i: (0,) * len(shape))
    return pl.pallas_call(
        body, name="gla_fwd", grid=(N_GROUPS,),
        in_specs=[_spec((GLA_ROWS, PROJ_A_PAD), lambda i: (i, 0)), const((GATE_PAD, KEY_DIM)), const((1, KEY_DIM)),
                  const((1, VALUE_DIM))],
        out_specs=[rows, rows, _spec((GLA_GROUP, GLA_HEADS, HEAD_V, HEAD_K), lambda i: (i, 0, 0, 0))],
        out_shape=[jax.ShapeDtypeStruct((SEQ, VALUE_DIM), F32), jax.ShapeDtypeStruct((SEQ, VALUE_DIM), BF16),
                   jax.ShapeDtypeStruct((N_CHUNKS, GLA_HEADS, HEAD_V, HEAD_K), F32)],
        scratch_shapes=[pltpu.VMEM((GLA_HEADS, HEAD_V, HEAD_K), F32)], compiler_params=_params(("arbitrary",)),
    )(proj, wgu, b_gate, gn)


def _gla_bwd(proj, wgu, b_gate, gn, o, states, dog):
    last = N_GROUPS - 1

    def body(p_ref, wgu_ref, b_ref, gn_ref, o_ref, st_ref, stp_ref, dog_ref, dp_ref, dwgu_ref, db_ref, dgn_ref, carry, do_buf):
        step = pl.program_id(0)

        @pl.when(step == 0)
        def _():
            carry[...] = jnp.zeros_like(carry)

        r = p_ref[:, R0:R0 + VALUE_DIM].astype(F32)
        sr = _sigmoid(r)
        silu = r * sr
        gn_row = gn_ref[...]
        dog_rows = dog_ref[...].astype(F32)
        dn = dog_rows * silu
        dgn_cols = []
        for h in range(GLA_HEADS):
            cols = slice(h * HEAD_V, (h + 1) * HEAD_V)
            oh = o_ref[:, cols]
            rs = _rstd(oh)
            ohat = oh * rs
            dn_h = dn[:, cols]
            dgn_cols.append(jnp.sum(dn_h * ohat, axis=0, keepdims=True))
            dohat = dn_h * gn_row[:, cols]
            do_buf[:, cols] = rs * (dohat - ohat * jnp.mean(dohat * ohat, axis=-1, keepdims=True))
            n_h = ohat * gn_row[:, cols]
            dp_ref[:, R0 + h * HEAD_V:R0 + (h + 1) * HEAD_V] = (
                dog_rows[:, cols] * n_h * (sr[:, cols] * (1.0 + r[:, cols] * (1.0 - sr[:, cols])))).astype(BF16)
        dgn = jnp.concatenate(dgn_cols, axis=1)

        tri = _tri(False)
        tri_strict = _tri(True)
        gl = p_ref[:, G0:G0 + GATE_PAD]
        logits = _gate_logits(gl, wgu_ref[...], b_ref[...])
        la = _log_decay(logits)
        dlogit_rows = []
        for c in reversed(range(GLA_GROUP)):
            rows = slice(c * CHUNK, (c + 1) * CHUNK)
            cum = _cumsum_rows(tri, la[rows])
            tot = cum[CHUNK - 1:CHUNK, :]
            fade = jnp.exp(tot - cum)
            k = p_ref[rows, K0:K0 + KEY_DIM].astype(F32)
            kd32 = k * fade
            kd = kd32.astype(BF16)
            decay = jnp.exp(tot)
            q = (p_ref[rows, Q0:Q0 + KEY_DIM].astype(F32) * (HEAD_K ** -0.5)).astype(BF16)
            v = p_ref[rows, V0:V0 + VALUE_DIM]
            do = do_buf[rows, :].astype(BF16)
            dkd_cols, ddecay_cols = [], []
            for h in range(GLA_HEADS):
                do_h = _head(do, h, HEAD_V)
                s = st_ref[c, h]
                dq = jnp.dot(do_h, s.astype(BF16), preferred_element_type=F32) * (HEAD_K ** -0.5)
                dp_ref[rows, Q0 + h * HEAD_K:Q0 + (h + 1) * HEAD_K] = dq.astype(BF16)
                g = carry[h] + lax.dot_general(do_h, _head(q, h, HEAD_K), (TN, ((), ())), preferred_element_type=F32)
                g16 = g.astype(BF16)
                dkd_cols.append(jnp.dot(_head(v, h, HEAD_V), g16, preferred_element_type=F32))
                dv = lax.dot_general(_head(kd, h, HEAD_K), g16, (NT, ((), ())), preferred_element_type=F32)
                dp_ref[rows, V0 + h * HEAD_V:V0 + (h + 1) * HEAD_V] = dv.astype(BF16)
                if c > 0:
                    s_prev = st_ref[c - 1, h]
                else:
                    s_prev = jnp.where(step < last, stp_ref[0, h], 0.0)
                ddecay_cols.append(jnp.sum(g * s_prev, axis=0, keepdims=True))
                carry[h] = g * _head(decay, h, HEAD_K)
            dkd = jnp.concatenate(dkd_cols, axis=1)
            ddecay = jnp.concatenate(ddecay_cols, axis=1)
            dp_ref[rows, K0:K0 + KEY_DIM] = (dkd * fade).astype(BF16)
            e = dkd * kd32
            dla = ddecay * decay + _cumsum_rows(tri_strict, e)
            dlogit_rows.append(dla * (1.0 / GATE_NORMALIZER) * (1.0 - _sigmoid(logits[rows])))
        dlogit = jnp.concatenate(dlogit_rows[::-1], axis=0)
        dlogit16 = dlogit.astype(BF16)
        dp_ref[:, G0:G0 + GATE_PAD] = lax.dot_general(
            dlogit16, wgu_ref[...], (NT, ((), ())), preferred_element_type=F32).astype(BF16)
        dwgu = lax.dot_general(gl, dlogit16, (TN, ((), ())), preferred_element_type=F32)
        db = jnp.sum(dlogit, axis=0, keepdims=True)

        @pl.when(step == 0)
        def _():
            dwgu_ref[...] = dwgu
            db_ref[...] = db
            dgn_ref[...] = dgn

        @pl.when(step > 0)
        def _():
            dwgu_ref[...] += dwgu
            db_ref[...] += db
            dgn_ref[...] += dgn

    rev = lambda i: (last - i, 0)
    rows = _spec((GLA_ROWS, VALUE_DIM), rev)
    const = lambda shape: _spec(shape, lambda i: (0,) * len(shape))
    st_shape = (GLA_HEADS, HEAD_V, HEAD_K)
    return pl.pallas_call(
        body, name="gla_bwd", grid=(N_GROUPS,),
        in_specs=[_spec((GLA_ROWS, PROJ_A_PAD), rev), const((GATE_PAD, KEY_DIM)), const((1, KEY_DIM)), const((1, VALUE_DIM)),
                  rows, _spec((GLA_GROUP,) + st_shape, lambda i: (last - i, 0, 0, 0)),
                  _spec((1,) + st_shape, lambda i: (jnp.maximum((last - i) * GLA_GROUP - 1, 0), 0, 0, 0)), rows],
        out_specs=[_spec((GLA_ROWS, PROJ_A_PAD), rev), const((GATE_PAD, KEY_DIM)), const((1, KEY_DIM)), const((1, VALUE_DIM))],
        out_shape=[jax.ShapeDtypeStruct((SEQ, PROJ_A_PAD), BF16), jax.ShapeDtypeStruct((GATE_PAD, KEY_DIM), F32),
                   jax.ShapeDtypeStruct((1, KEY_DIM), F32), jax.ShapeDtypeStruct((1, VALUE_DIM), F32)],
        scratch_shapes=[pltpu.VMEM(st_shape, F32), pltpu.VMEM((GLA_ROWS, VALUE_DIM), F32)],
        compiler_params=_params(("arbitrary",)),
    )(proj, wgu, b_gate, gn, o, states, states, dog)


WGRAD_FF_TILE = D_FF // 2


def _ffn_fwd(tag, x, gamma, w_up_t, conv_w, w_down):
    h = _norm_fwd(f"ffn{tag}_norm", x, gamma)
    gu = _proj_halves_nt(f"ffn{tag}_up", h, w_up_t)
    a = _ffn_mid_fwd(f"ffn{tag}_mid", gu, conv_w)
    return _wide_nn(f"ffn{tag}_down", a, w_down, x=x), (h, gu, a)


def _ffn_bwd(tag, x, gamma, w_up_t, conv_w, w_down, saved, dx, dx16):
    h, gu, a = saved
    da = _wide_nt(f"ffn{tag}_da", dx16, w_down)
    d_w_down = _wgrad_cols_tn(f"ffn{tag}_dwdown", a, WGRAD_FF_TILE, dx16)
    dgu, d_conv = _ffn_mid_bwd(f"ffn{tag}_mid_bwd", gu, conv_w, da)
    dx, dx16, d_gamma = _sum_blocks_nn(f"ffn{tag}_dh", dgu, w_up_t, norm=(x, gamma, dx))
    d_w_up_t = _wgrad_halves_tn(f"ffn{tag}_dwup", dgu, WGRAD_FF_TILE, h)
    return dx, dx16, d_gamma, d_w_up_t, d_conv, d_w_down


def _local_step(x, target, w, fetch=None, emit=None):
    if fetch is None:
        local = dict(a=(w.get("a_w_in"), w.get("a_w_out")), b=(w.get("b_w_in"), w.get("b_w_out")))
        for layer in range(2):
            local[f"f{layer}"] = (w["f_w_up"][layer], w["f_w_down"][layer]) if "f_w_up" in w else None
        fetch = lambda group, after: local[group]
    if emit is None:
        emit = lambda group, grads, dx: dx
    f_norm = (w["f_norm"][0:1], w["f_norm"][1:2])

    x0 = x
    a_w_in, a_w_out = fetch("a", x0)
    h0 = _norm_fwd("a_norm", x0, w["a_norm"])
    proj = _proj_rows_nt("a_in", h0, a_w_in, PA_TILE)
    o, og, states = _gla_fwd(proj, w["a_w_gate_up"], w["a_b_gate"], w["a_gn"])
    x1 = _square("a_out", og, a_w_out, NN, x0)
    up0, down0 = fetch("f0", x1)
    x2, ffn0 = _ffn_fwd(0, x1, f_norm[0], up0, w["f_conv"][0], down0)
    b_w_in, b_w_out = fetch("b", x2)
    h2 = _norm_fwd("b_norm", x2, w["b_norm"])
    p = _proj_cols_nn("b_in", h2, b_w_in)
    y = _sc_mid_fwd(p, w["b_conv"])
    x3 = _square("b_out", y, b_w_out, NN, x2)
    up1, down1 = fetch("f1", x3)
    x4, ffn1 = _ffn_fwd(1, x3, f_norm[1], up1, w["f_conv"][1], down1)
    loss, dx, dx16, d_final_norm = _loss_head(x4, w["final_norm"], target)

    dx, dx16, d_f_norm1, d_up1, d_fconv1, d_down1 = _ffn_bwd(1, x3, f_norm[1], up1, w["f_conv"][1], down1, ffn1, dx, dx16)
    dx16 = emit("f1", (d_up1, d_down1), dx16)

    dy = _square("b_dy", dx16, b_w_out, NT)
    d_b_w_out = _wgrad_cols_tn("b_dwout", y, OUT_TILE, dx16)
    db, dc, dhh, d_b_conv = _sc_mid_bwd(p, w["b_conv"], dy)
    dp = jnp.concatenate([db, dc, dhh], axis=1)
    dx, dx16, d_b_norm = _sum_cols_nt("b_dh", dp, b_w_in, norm=(x2, w["b_norm"], dx))
    d_b_w_in = _wgrad_cols_transposed_tn("b_dwin", h2, dp, B_SHARD)
    dx16 = emit("b", (d_b_w_in, d_b_w_out), dx16)

    dx, dx16, d_f_norm0, d_up0, d_fconv0, d_down0 = _ffn_bwd(0, x1, f_norm[0], up0, w["f_conv"][0], down0, ffn0, dx, dx16)
    dx16 = emit("f0", (d_up0, d_down0), dx16)

    dog = _square("a_dog", dx16, a_w_out, NT)
    d_a_w_out = _wgrad_cols_tn("a_dwout", og, OUT_TILE, dx16)
    dproj, d_wgu, d_b_gate, d_gn = _gla_bwd(proj, w["a_w_gate_up"], w["a_b_gate"], w["a_gn"], o, states, dog)
    d_a_w_in = _wgrad_cols_tn("a_dwin", dproj, PA_TILE, h0)
    dproj = emit("a", (d_a_w_in, d_a_w_out), dproj)
    dx, _, d_a_norm = _wide_nn("a_dh", dproj, a_w_in, norm=(x0, w["a_norm"], dx))

    grads = dict(
        a_norm=d_a_norm, a_w_in=d_a_w_in, a_w_gate_up=d_wgu, a_b_gate=d_b_gate, a_gn=d_gn, a_w_out=d_a_w_out,
        b_norm=d_b_norm, b_w_in=d_b_w_in, b_conv=d_b_conv, b_w_out=d_b_w_out,
        f_norm=(d_f_norm0, d_f_norm1), f_w_up=(d_up0, d_up1), f_conv=(d_fconv0, d_fconv1), f_w_down=(d_down0, d_down1),
        final_norm=d_final_norm)
    return loss[0, 0], dx, grads


MESH_ID = pl.DeviceIdType.MESH
ANY = pl.BlockSpec(memory_space=pl.ANY)
N_PEERS = N_DEV - 1


def _position():
    return lax.axis_index("x"), lax.axis_index("y"), lax.axis_index("c")


def _slot(px, py, pc):
    return 4 * px + 2 * py + pc


def _all_gather(name, shards):
    n = len(shards)

    def body(*refs):
        ins, outs = refs[:n], refs[n:2 * n]
        send_sems, recv_sems, local_sems = refs[2 * n:]
        x, y, c = _position()
        me, sibling = (x, y, c), (x, y, 1 - c)
        chips = [(1 - x, y), (x, 1 - y), (1 - x, 1 - y)]

        def copy(t, k, block, to, from_input=False):
            dst = outs[t].at[_slot(*block)]
            return pltpu.make_async_remote_copy(
                src_ref=ins[t] if from_input else dst, dst_ref=dst, send_sem=send_sems.at[t, k], recv_sem=recv_sems.at[t, k],
                device_id=to, device_id_type=MESH_ID)

        mine = [pltpu.make_async_copy(ins[t], outs[t].at[_slot(*me)], local_sems.at[t]) for t in range(n)]
        for cp in mine:
            cp.start()
        first = []
        for t in range(n):
            first.append(copy(t, 0, me, sibling, True))
            first += [copy(t, 1 + j, me, (*chip, c), True) for j, chip in enumerate(chips)]
        for cp in first:
            cp.start()
        passed = []
        for t in range(n):
            for j, chip in enumerate(chips):
                copy(t, 1 + j, (*chip, c), me).wait_recv()
                fwd = copy(t, 4 + j, (*chip, c), sibling)
                fwd.start()
                passed.append(fwd)
        for t in range(n):
            copy(t, 0, sibling, me).wait_recv()
            for j, chip in enumerate(chips):
                copy(t, 4 + j, (*chip, 1 - c), me).wait_recv()
        for cp in first + passed:
            cp.wait_send()
        for cp in mine:
            cp.wait()

    return pl.pallas_call(
        body, name=name, in_specs=[ANY] * n, out_specs=[ANY] * n,
        out_shape=[jax.ShapeDtypeStruct((N_DEV,) + s.shape, s.dtype) for s in shards],
        scratch_shapes=[pltpu.SemaphoreType.DMA((n, N_PEERS)), pltpu.SemaphoreType.DMA((n, N_PEERS)), pltpu.SemaphoreType.DMA((n,))],
    )(*shards)


SIBLING_AND_SAME_CORE = (1, 2, 4, 6)
SAME_CORE = (2, 4, 6)


def _flip(x, y, c, k):
    return x ^ (k >> 2), y ^ ((k >> 1) & 1), c ^ (k & 1)


N_CHIPS = N_DEV // 2


def _chip(px, py):
    return 2 * px + py


def _pair_swap(name, parts):
    n = len(parts)

    def body(*refs):
        ins, outs = refs[:n], refs[n:2 * n]
        send_sems, recv_sems = refs[2 * n:]
        x, y, c = _position()
        sibling = (x, y, 1 - c)
        sent = []
        for t in range(n):
            for q in range(N_CHIPS):
                sent.append(pltpu.make_async_remote_copy(
                    src_ref=ins[t].at[2 * q + 1 - c], dst_ref=outs[t].at[q], send_sem=send_sems.at[t, q],
                    recv_sem=recv_sems.at[t, q], device_id=sibling, device_id_type=MESH_ID))
        for cp in sent:
            cp.start()
        for t in range(n):
            for q in range(N_CHIPS):
                landed = outs[t].at[q]
                pltpu.make_async_remote_copy(
                    src_ref=landed, dst_ref=landed, send_sem=send_sems.at[t, q], recv_sem=recv_sems.at[t, q],
                    device_id=sibling, device_id_type=MESH_ID).wait_recv()
        for cp in sent:
            cp.wait_send()

    sems = pltpu.SemaphoreType.DMA((n, N_CHIPS))
    return pl.pallas_call(
        body, name=name, in_specs=[ANY] * n, out_specs=[ANY] * n,
        out_shape=[jax.ShapeDtypeStruct((N_CHIPS,) + p.shape[1:], p.dtype) for p in parts], scratch_shapes=[sems, sems],
    )(*parts)


PAIR_ROWS = 1024


def _pair_add(name, part, received, side):
    _, rows, cols = part.shape
    tiles = [t for t in range(PAIR_ROWS, 0, -BF16_ROWS) if rows % t == 0]
    tr = tiles[0] if tiles else rows

    def body(side_ref, p_ref, r_ref, o_ref):
        o_ref[...] = (p_ref[...].astype(F32) + r_ref[...].astype(F32)).astype(BF16)

    tile = _spec((None, tr, cols), lambda q, i, side_ref: (q, i, 0))
    return pl.pallas_call(
        body, name=name,
        grid_spec=pltpu.PrefetchScalarGridSpec(
            num_scalar_prefetch=1, grid=(N_CHIPS, rows // tr),
            in_specs=[_spec((None, tr, cols), lambda q, i, side_ref: (2 * q + side_ref[0], i, 0)), tile], out_specs=tile),
        out_shape=jax.ShapeDtypeStruct((N_CHIPS, rows, cols), BF16), compiler_params=_params(("parallel", "parallel")),
    )(side, part, received)


def _send_copy(parts, landing, send_sems, recv_sems, t, s, k):
    x, y, c = _position()
    px, py, _ = _flip(x, y, c, k)
    return pltpu.make_async_remote_copy(
        src_ref=parts[t].at[_chip(px, py)], dst_ref=landing[t].at[_chip(x, y)], send_sem=send_sems.at[s],
        recv_sem=recv_sems.at[s], device_id=(px, py, c), device_id_type=MESH_ID)


def _send_arrival(landing, send_sems, recv_sems, t, s, k):
    x, y, c = _position()
    px, py, _ = _flip(x, y, c, k)
    landed = landing[t].at[_chip(px, py)]
    return pltpu.make_async_remote_copy(
        src_ref=landed, dst_ref=landed, send_sem=send_sems.at[s], recv_sem=recv_sems.at[s],
        device_id=(px, py, c), device_id_type=MESH_ID)


def _handshake(peers):
    x, y, c = _position()
    barrier = pltpu.get_barrier_semaphore()
    for k in peers:
        pl.semaphore_signal(barrier, inc=1, device_id=_flip(x, y, c, k), device_id_type=MESH_ID)
    pl.semaphore_wait(barrier, len(peers))


def _sequencer(name, collective_id, n_copies, body, operands, out_type):
    n_arrays = len(operands)
    return pl.kernel(
        body, out_type=out_type, mesh=plsc.ScalarSubcoreMesh(axis_name="sequencer", num_cores=1), name=name,
        scratch_types=(pltpu.SemaphoreType.DMA((n_copies,)), pltpu.SemaphoreType.DMA((n_copies,)),
                       pltpu.SemaphoreType.DMA((n_arrays,))),
        compiler_params=pltpu.CompilerParams(collective_id=collective_id))(*operands)


def _sequencer_exchange(name, collective_id, parts, after=()):
    n, n_peers, n_in = len(parts), len(SAME_CORE), len(parts) + len(after)

    def body(*refs):
        src, landing = refs[:n], refs[n_in:n_in + n]
        send_sems, recv_sems, local_sems = refs[n_in + n:]
        _handshake(SAME_CORE)
        x, y, _ = _position()
        mine = [pltpu.make_async_copy(src[t].at[_chip(x, y)], landing[t].at[_chip(x, y)], local_sems.at[t]) for t in range(n)]
        for cp in mine:
            cp.start()
        sent = [_send_copy(src, landing, send_sems, recv_sems, t, t * n_peers + j, k)
                for t in range(n) for j, k in enumerate(SAME_CORE)]
        for cp in sent:
            cp.start()
        for t in range(n):
            for j, k in enumerate(SAME_CORE):
                _send_arrival(landing, send_sems, recv_sems, t, t * n_peers + j, k).wait_recv()
        for cp in sent:
            cp.wait_send()
        for cp in mine:
            cp.wait()

    landing = [jax.ShapeDtypeStruct(p.shape, p.dtype) for p in parts]
    return _sequencer(name, collective_id, n * n_peers, body, list(parts) + list(after), landing)


def _sequencer_gather(name, collective_id, shards):
    n, per = len(shards), N_PEERS

    def body(*refs):
        src, out = refs[:n], refs[n:2 * n]
        send_sems, recv_sems, local_sems = refs[2 * n:]
        _handshake(SIBLING_AND_SAME_CORE)
        x, y, c = _position()
        me, sibling = (x, y, c), (x, y, 1 - c)

        def copy(t, j, block, to, from_input=False):
            dst = out[t].at[_slot(*block)]
            return pltpu.make_async_remote_copy(
                src_ref=src[t] if from_input else dst, dst_ref=dst, send_sem=send_sems.at[t * per + j],
                recv_sem=recv_sems.at[t * per + j], device_id=to, device_id_type=MESH_ID)

        mine = [pltpu.make_async_copy(src[t], out[t].at[_slot(*me)], local_sems.at[t]) for t in range(n)]
        for cp in mine:
            cp.start()
        sent = [copy(t, j, me, _flip(x, y, c, k), True) for t in range(n) for j, k in enumerate(SIBLING_AND_SAME_CORE)]
        for cp in sent:
            cp.start()
        for t in range(n):
            for j, k in enumerate(SAME_CORE):
                block = _flip(x, y, c, k)
                copy(t, 1 + j, block, me).wait_recv()
                forward = copy(t, 4 + j, block, sibling)
                forward.start()
                sent.append(forward)
        for t in range(n):
            copy(t, 0, sibling, me).wait_recv()
            for j, k in enumerate(SAME_CORE):
                copy(t, 4 + j, _flip(x, y, 1 - c, k), me).wait_recv()
        for cp in sent:
            cp.wait_send()
        for cp in mine:
            cp.wait()

    gathered = [jax.ShapeDtypeStruct((N_DEV,) + s.shape, s.dtype) for s in shards]
    return _sequencer(name, collective_id, n * per, body, shards, gathered)


ADAM_ROWS = 512
BF16_ROWS = 16


def _adam_update(w, g, m, v):
    m = ADAM_B1 * m + (1.0 - ADAM_B1) * g
    v = ADAM_B2 * v + (1.0 - ADAM_B2) * (g * g)
    m_hat = m / (1.0 - ADAM_B1 ** ADAM_STEP)
    v_hat = v / (1.0 - ADAM_B2 ** ADAM_STEP)
    delta = -ADAM_LR * (m_hat / (jnp.sqrt(v_hat) + ADAM_EPS) + ADAM_WD * w)
    return delta, m, v


def _sum_slots(ref):
    total = ref[0].astype(F32)
    for d in range(1, ref.shape[0]):
        total = total + ref[d].astype(F32)
    return total


def _adamw_sum(name, landed, w, m, v):
    layers, rows, cols = w.shape
    tiles = [t for t in range(ADAM_ROWS, 0, -BF16_ROWS) if rows % t == 0]
    tr = tiles[0] if tiles else rows
    nt = rows // tr

    def body(*refs):
        parts = refs[:layers]
        w_ref, m_ref, v_ref, g_ref, d_ref, nm_ref, nv_ref = refs[layers:]
        layer = pl.program_id(0)
        g = _sum_slots(parts[0])
        for q in range(1, layers):
            g = jnp.where(layer == q, _sum_slots(parts[q]), g)
        delta, new_m, new_v = _adam_update(w_ref[...], g, m_ref[...], v_ref[...])
        g_ref[...] = g
        d_ref[...] = delta
        nm_ref[...] = new_m
        nv_ref[...] = new_v

    def part_spec(q):
        return _spec((N_CHIPS, tr, cols), lambda l, i: (0, jnp.where(l == q, i, jnp.where(l < q, 0, nt - 1)), 0))

    tile = _spec((None, tr, cols), lambda l, i: (l, i, 0))
    out = jax.ShapeDtypeStruct((layers, rows, cols), F32)
    return pl.pallas_call(
        body, name=name, grid=(layers, nt), in_specs=[part_spec(q) for q in range(layers)] + [tile] * 3,
        out_specs=[tile] * 4, out_shape=[out] * 4, compiler_params=_params(("arbitrary", "arbitrary")),
    )(*landed, w, m, v)


def _sum_small(landed):
    def body(in_ref, out_ref):
        out_ref[...] = _sum_slots(in_ref)

    return pl.pallas_call(body, name="small_grad_sum", out_shape=jax.ShapeDtypeStruct(landed.shape[1:], F32))(landed)


def _adamw_small(name, g, w, m, v):
    def body(g_ref, w_ref, m_ref, v_ref, d_ref, nm_ref, nv_ref):
        d_ref[...], nm_ref[...], nv_ref[...] = _adam_update(w_ref[...], g_ref[...], m_ref[...], v_ref[...])

    out = jax.ShapeDtypeStruct(w.shape, F32)
    return pl.pallas_call(body, name=name, out_shape=[out] * 3)(g, w, m, v)


LANES = 128
SUBLANES = 8
F_CONV_SHARD = D_FF // N_DEV
GATE_SHARD = KEY_DIM // N_DEV
NORM_SHARD = D_MODEL // N_DEV


def _tile_rows(a):
    flat = a.reshape(-1)
    size = -(-flat.shape[0] // (SUBLANES * LANES)) * SUBLANES * LANES
    return jnp.pad(flat, (0, size - flat.shape[0])).reshape(-1, LANES)


def _pack_rows(pieces):
    return jnp.concatenate([_tile_rows(p) for p in pieces], axis=0)


def _unpack_rows(packed, shapes):
    out, row = [], 0
    for shape in shapes:
        size = 1
        for s in shape:
            size *= s
        rows = -(-size // (SUBLANES * LANES)) * SUBLANES
        piece = packed[..., row:row + rows, :]
        out.append(piece.reshape(piece.shape[:-2] + (rows * LANES,))[..., :size])
        row += rows
    return out


SMALL_SHARDS = ((GATE_RANK, GATE_SHARD), (1, NORM_SHARD), (3, NORM_SHARD), (2, 3, F_CONV_SHARD))


def _unpack_small_shards(g):
    gate, b_norm, b_conv, f_conv = _unpack_rows(g, SMALL_SHARDS)
    gate = gate.reshape(N_DEV, GATE_RANK, GATE_SHARD).transpose(1, 0, 2).reshape(GATE_RANK, KEY_DIM)
    b_norm = b_norm.reshape(1, D_MODEL)
    b_conv = b_conv.reshape(N_DEV, 3, NORM_SHARD).transpose(1, 0, 2).reshape(3, D_MODEL)
    f_conv = f_conv.reshape(N_DEV, 2, 3, F_CONV_SHARD).transpose(1, 2, 0, 3).reshape(2, 3, D_FF)
    return gate, b_norm, b_conv, f_conv


SMALL_LAYOUT = (("a_norm", (1, D_MODEL)), ("a_w_gate_up", (GATE_RANK, KEY_DIM)), ("a_b_gate", (1, KEY_DIM)), ("a_gn", (1, VALUE_DIM)),
                ("b_norm", (1, D_MODEL)), ("b_conv", (3, D_MODEL)), ("f_norm0", (1, D_MODEL)), ("f_norm1", (1, D_MODEL)),
                ("f_conv0", (3, D_FF)), ("f_conv1", (3, D_FF)), ("final_norm", (1, D_MODEL)))


def _pack_small_grads(g):
    full = dict(g)
    full["a_w_gate_up"] = g["a_w_gate_up"][:GATE_RANK]
    for layer in range(2):
        full[f"f_norm{layer}"] = g["f_norm"][layer]
        full[f"f_conv{layer}"] = g["f_conv"][layer]
    return _pack_rows([full[name] for name, _ in SMALL_LAYOUT])


def _unpack_small_grads(packed):
    pieces = _unpack_rows(packed, [shape for _, shape in SMALL_LAYOUT])
    out = {name: piece.reshape(shape) for (name, shape), piece in zip(SMALL_LAYOUT, pieces)}
    out["f_norm"] = jnp.stack([out["f_norm0"][0], out["f_norm1"][0]])
    out["f_conv"] = jnp.stack([out["f_conv0"], out["f_conv1"]])
    return out


def kernel(x, a_norm, a_w_in, a_w_gate_up, a_b_gate, a_gn, a_w_out, b_norm, b_w_in, b_conv, b_w_out, f_norm, f_w_up, f_conv, f_w_down, final_norm, loss_target, m_a_norm, m_a_w_in, m_a_w_gate_up, m_a_b_gate, m_a_gn, m_a_w_out, m_b_norm, m_b_w_in, m_b_conv, m_b_w_out, m_f_norm, m_f_w_up, m_f_conv, m_f_w_down, m_final_norm, v_a_norm, v_a_w_in, v_a_w_gate_up, v_a_b_gate, v_a_gn, v_a_w_out, v_b_norm, v_b_w_in, v_b_conv, v_b_w_out, v_f_norm, v_f_w_up, v_f_conv, v_f_w_down, v_final_norm):
    my_slot = _slot(*_position())

    transposed = lambda w: jnp.swapaxes(w, 1, 2)
    a_w_in_t, f_w_up_t = transposed(a_w_in), transposed(f_w_up)
    first = _all_gather("weight_gather", [a_w_in_t[0].astype(BF16), a_w_out[0].astype(BF16),
                                          _pack_rows([a_w_gate_up[0], b_norm, b_conv[0], f_conv])])
    gathers, small_shards = {}, first[2]
    later = (("f0", f_w_up_t[0], f_w_down[0]), ("b", b_w_in[0], b_w_out[0]), ("f1", f_w_up_t[1], f_w_down[1]))
    for collective_id, (group, w_in, w_out) in enumerate(later):
        w_in, w_out, small_shards = lax.optimization_barrier((w_in.astype(BF16), w_out.astype(BF16), small_shards))
        gathers[group] = _sequencer_gather(f"gather_{group}", collective_id, [w_in, w_out])
    gate_full, b_norm_full, b_conv_full, f_conv_full = _unpack_small_shards(small_shards)
    a_w_in_full = jnp.pad(first[0].reshape(PROJ_A, D_MODEL), ((0, PROJ_A_PAD - PROJ_A), (0, 0)))
    weights = dict(
        a_norm=a_norm, a_w_gate_up=jnp.pad(gate_full, ((0, GATE_PAD - GATE_RANK), (0, 0))).astype(BF16), a_b_gate=a_b_gate,
        a_gn=a_gn, b_norm=b_norm_full, b_conv=b_conv_full, f_norm=f_norm, f_conv=f_conv_full,
        final_norm=final_norm.reshape(1, D_MODEL))

    def fetch(group, after):
        if group == "a":
            return a_w_in_full, first[1].reshape(D_MODEL, D_MODEL)
        w_in, w_out = gathers[group]
        if group == "b":
            return w_in, w_out.reshape(D_MODEL, D_MODEL)
        return w_in.reshape(2, D_FF, D_MODEL), w_out.reshape(D_FF, D_MODEL)

    exchanges, pending = {}, []
    exchange_ids = dict(b=3, f0=4, a=5)
    side = lax.axis_index("c").astype(jnp.int32).reshape(1)

    def emit(group, grads, carry):
        d_in, d_out = grads
        if group == "a":
            d_in = d_in[:PROJ_A]
        d_in, d_out = d_in.reshape((N_DEV, -1) + d_in.shape[-1:]), d_out.reshape((N_DEV, -1, D_MODEL))
        received = _pair_swap(f"pair_swap_{group}", [d_in, d_out])
        sums = [_pair_add(f"pair_add_{group}_{i}", part, got, side) for i, (part, got) in enumerate(zip((d_in, d_out), received))]
        carry, *sums = lax.optimization_barrier((carry, *sums))
        pending.extend(sums)
        if group != "f1":
            after = list(exchanges.values())[-1][:1] if exchanges else ()
            exchanges[group] = _sequencer_exchange(f"grads_{group}", exchange_ids[group], list(pending), after)
            pending.clear()
        return carry

    loss, dx, g = _local_step(x[0], loss_target[0], weights, fetch, emit)
    loss = lax.psum(loss, MESH_AXES)
    small_landed = _all_gather("small_grad_gather", [_pack_small_grads(g)])[0]

    (up1, down1, d_b_in, d_b_out), (up0, down0), (d_a_in, d_a_out) = (exchanges[group] for group in ("b", "f0", "a"))
    back = lambda results: tuple(transposed(r) for r in results)
    big = dict(
        b_w_in=_adamw_sum("adam_b_w_in", [d_b_in], b_w_in, m_b_w_in, v_b_w_in),
        b_w_out=_adamw_sum("adam_b_w_out", [d_b_out], b_w_out, m_b_w_out, v_b_w_out),
        f_w_up=back(_adamw_sum("adam_f_w_up", [up0, up1], f_w_up_t, transposed(m_f_w_up), transposed(v_f_w_up))),
        f_w_down=_adamw_sum("adam_f_w_down", [down0, down1], f_w_down, m_f_w_down, v_f_w_down),
        a_w_in=back(_adamw_sum("adam_a_w_in", [d_a_in], a_w_in_t, transposed(m_a_w_in), transposed(v_a_w_in))),
        a_w_out=_adamw_sum("adam_a_w_out", [d_a_out], a_w_out, m_a_w_out, v_a_w_out))
    small_g = _unpack_small_grads(_sum_small(small_landed))
    small_g["a_w_gate_up"] = lax.dynamic_slice_in_dim(small_g["a_w_gate_up"], my_slot * GATE_SHARD, GATE_SHARD, axis=1)
    small_g["b_norm"] = lax.dynamic_slice_in_dim(small_g["b_norm"], my_slot * NORM_SHARD, NORM_SHARD, axis=1)
    small_g["b_conv"] = lax.dynamic_slice_in_dim(small_g["b_conv"], my_slot * NORM_SHARD, NORM_SHARD, axis=1)
    small_g["f_conv"] = lax.dynamic_slice_in_dim(small_g["f_conv"], my_slot * F_CONV_SHARD, F_CONV_SHARD, axis=2)
    small_w = dict(
        a_norm=(a_norm, m_a_norm, v_a_norm), a_w_gate_up=(a_w_gate_up, m_a_w_gate_up, v_a_w_gate_up),
        a_b_gate=(a_b_gate, m_a_b_gate, v_a_b_gate), a_gn=(a_gn, m_a_gn, v_a_gn), b_norm=(b_norm, m_b_norm, v_b_norm),
        b_conv=(b_conv, m_b_conv, v_b_conv), f_norm=(f_norm, m_f_norm, v_f_norm), f_conv=(f_conv, m_f_conv, v_f_conv),
        final_norm=(final_norm, m_final_norm, v_final_norm))
    small = {}
    for name, (w, m, v) in small_w.items():
        flat = (w.shape[-1],) if w.ndim == 1 else w.shape[-2:]
        two_d = (-1, flat[-1])
        grad = small_g[name].reshape(w.shape)
        delta, new_m, new_v = _adamw_small(
            "adam_" + name, grad.reshape(two_d), w.reshape(two_d), m.reshape(two_d), v.reshape(two_d))
        small[name] = (grad, delta.reshape(w.shape), new_m.reshape(w.shape), new_v.reshape(w.shape))

    order = ["a_norm", "a_w_in", "a_w_gate_up", "a_b_gate", "a_gn", "a_w_out", "b_norm", "b_w_in", "b_conv", "b_w_out",
             "f_norm", "f_w_up", "f_conv", "f_w_down", "final_norm"]
    results = {**big, **small}
    outputs = [loss, dx.reshape(1, SEQ, D_MODEL)]
    for kind in range(4):
        outputs += [results[name][kind] for name in order]
    return tuple(outputs)
```

```python
import jax
import jax.numpy as jnp
from jax import lax
from jax.experimental import pallas as pl
from jax.experimental.pallas import tpu as pltpu
from jax.experimental.pallas import tpu_sc as plsc

F32 = jnp.float32
BF16 = jnp.bfloat16

N_DEV = 8
SEQ = 2048
D_MODEL = 1024
CHUNK = 64
N_CHUNKS = SEQ // CHUNK
RMS_EPS = 1e-6
GLA_HEADS = 4
KEY_DIM = 512
VALUE_DIM = 1024
HEAD_K = KEY_DIM // GLA_HEADS
HEAD_V = VALUE_DIM // GLA_HEADS
GATE_RANK = 16
GATE_PAD = 128
GATE_NORMALIZER = 16.0
PROJ_A = 2 * KEY_DIM + 2 * VALUE_DIM + GATE_RANK
PROJ_A_PAD = 2 * KEY_DIM + 2 * VALUE_DIM + GATE_PAD
A_SHARD = PROJ_A // N_DEV
B_SHARD = 3 * D_MODEL // N_DEV
D_FF = 2816
ADAM_LR = 0.001
ADAM_B1 = 0.9
ADAM_B2 = 0.999
ADAM_EPS = 1e-08
ADAM_WD = 0.01
ADAM_STEP = 10
MESH_AXES = ("x", "y", "c")

VMEM_LIMIT = 56 * 1024 * 1024
ROW_CHUNK = 256
HALO = 16


def _params(sem=None, vmem=VMEM_LIMIT):
    return pltpu.CompilerParams(dimension_semantics=sem, vmem_limit_bytes=vmem)


NN = ((1,), (0,))
NT = ((1,), (1,))
TN = ((0,), (0,))


def _matmul(name, a, a_spec, b, b_spec, dims, grid, out_shape, out_spec, k_blocks=None, a_block_cols=None, res=None,
            res_spec=None, transpose_out=False, norm=None, swap=()):
    has_res = res is not None
    n_swap = len(swap)

    def body(*refs):
        a_ref, b_ref = refs[0], refs[1]
        r_ref = refs[2] if has_res else None

        def product(lhs, rhs):
            return lax.dot_general(lhs.astype(BF16), rhs, (dims, ((), ())), preferred_element_type=F32)

        if k_blocks is None:
            v = product(a_ref[...], b_ref[...])
        else:
            v = None
            for k in range(k_blocks):
                lhs = a_ref[k] if a_block_cols is None else a_ref[:, k * a_block_cols:(k + 1) * a_block_cols]
                p = product(lhs, b_ref[k])
                v = p if v is None else v + p
        if transpose_out:
            v = v.T
        if has_res:
            v = v + r_ref[...]
        if norm is None:
            o_ref = refs[2 + has_res]
            o_ref[...] = v.astype(o_ref.dtype)
            return
        n_in = 5 + has_res
        x_ref, g_ref, dxi_ref = refs[2 + has_res:n_in]
        dx_ref, dx16_ref, dg_ref = refs[n_in + n_swap:n_in + n_swap + 3]
        if n_swap:
            copies = _pair_copies(refs[n_in:n_in + n_swap], refs[n_in + n_swap + 3:n_in + 2 * n_swap + 3], *refs[-2:])

            @pl.when(pl.program_id(0) == 0)
            def _():
                for send, _ in copies:
                    send.start()

            @pl.when(pl.program_id(0) == grid[0] - 1)
            def _():
                for send, arrival in copies:
                    arrival.wait_recv()
                    send.wait_send()

        dx, dg = _norm_bwd_rows(x_ref[...], g_ref[...], v)
        dx = dxi_ref[...] + dx
        dx_ref[...] = dx
        dx16_ref[...] = dx.astype(BF16)

        @pl.when(pl.program_id(0) == 0)
        def _():
            dg_ref[...] = dg

        @pl.when(pl.program_id(0) > 0)
        def _():
            dg_ref[...] += dg

    operands = [a, b] + ([res] if has_res else [])
    in_specs = [a_spec, b_spec] + ([res_spec] if has_res else [])
    semantics = ("parallel",) * len(grid)
    scratch = []
    if norm is not None:
        vec = _spec((1, D_MODEL), lambda i: (0, 0))
        any_space = pl.BlockSpec(memory_space=pl.ANY)
        operands += list(norm) + list(swap)
        in_specs += [out_spec, vec, out_spec] + [any_space] * n_swap
        out_shape = [_act(dtype=F32), _act(), jax.ShapeDtypeStruct((1, D_MODEL), F32)]
        out_shape += [jax.ShapeDtypeStruct((N_DEV // 2,) + p.shape[1:], p.dtype) for p in swap]
        out_spec = [out_spec, out_spec, vec] + [any_space] * n_swap
        semantics = ("arbitrary",)
        if n_swap:
            scratch = [pltpu.SemaphoreType.DMA((n_swap, N_DEV // 2))] * 2
    return pl.pallas_call(
        body, name=name, grid=grid, in_specs=in_specs, out_specs=out_spec, out_shape=out_shape, scratch_shapes=scratch,
        compiler_params=_params(semantics),
    )(*operands)


def _resident(shape):
    return pl.BlockSpec(shape, lambda *_: (0,) * len(shape), pipeline_mode=pl.Buffered(1))


TM = 512
N_TM = SEQ // TM
PA_TILE = 640
N_PA = PROJ_A_PAD // PA_TILE
OUT_TILE = 256


def _spec(shape, fn):
    return pl.BlockSpec(shape, fn)


def _act(shape=(SEQ, D_MODEL), dtype=BF16):
    return jax.ShapeDtypeStruct(shape, dtype)


def _proj_rows_nt(name, h, wt, n_tile):
    n = wt.shape[0]
    return _matmul(name, h, _resident((SEQ, D_MODEL)), wt, _spec((n_tile, D_MODEL), lambda j: (j, 0)), NT,
                   (n // n_tile,), _act((SEQ, n)), _spec((SEQ, n_tile), lambda j: (0, j)))


def _proj_cols_nn(name, h, w_blocks):
    nb, _, n = w_blocks.shape
    return _matmul(name, h, _resident((SEQ, D_MODEL)), w_blocks, _spec((None, D_MODEL, n), lambda j: (j, 0, 0)),
                   NN, (nb,), _act((SEQ, nb * n)), _spec((SEQ, n), lambda j: (0, j)))


def _square(name, a, w, dims, x=None):
    row = _spec((TM, D_MODEL), lambda i: (i, 0))
    return _matmul(name, a, row, w, _resident((D_MODEL, D_MODEL)), dims, (N_TM,),
                   _act(dtype=F32 if x is not None else BF16), row, res=x, res_spec=row if x is not None else None)


def _sum_blocks_nn(name, a_blocks, w_blocks, x=None, norm=None, swap=()):
    nb, _, n = a_blocks.shape
    row = _spec((TM, D_MODEL), lambda i: (i, 0))
    return _matmul(name, a_blocks, _spec((nb, TM, n), lambda i: (0, i, 0)), w_blocks, _resident((nb, n, D_MODEL)),
                   NN, (N_TM,), _act(dtype=F32), row, k_blocks=nb, res=x, res_spec=row if x is not None else None, norm=norm, swap=swap)


def _sum_cols_nt(name, d, w_blocks, norm=None, swap=()):
    nb, _, n = w_blocks.shape
    return _matmul(name, d, _spec((TM, nb * n), lambda i: (i, 0)), w_blocks, _resident((nb, D_MODEL, n)), NT,
                   (N_TM,), _act(dtype=F32), _spec((TM, D_MODEL), lambda i: (i, 0)), k_blocks=nb, a_block_cols=n, norm=norm, swap=swap)


def _wide_nn(name, d, wt, x=None, norm=None, swap=()):
    n = wt.shape[0]
    row = _spec((TM, D_MODEL), lambda i: (i, 0))
    return _matmul(name, d, _spec((TM, n), lambda i: (i, 0)), wt, _resident((n, D_MODEL)), NN, (N_TM,),
                   _act(dtype=F32), row, res=x, res_spec=row if x is not None else None, norm=norm, swap=swap)


def _wide_nt(name, d, w):
    n = w.shape[0]
    return _matmul(name, d, _spec((TM, D_MODEL), lambda i: (i, 0)), w, _resident((n, D_MODEL)), NT, (N_TM,),
                   _act((SEQ, n)), _spec((TM, n), lambda i: (i, 0)))


def _proj_halves_nt(name, h, wt):
    _, n, _ = wt.shape
    return _matmul(name, h, _spec((TM, D_MODEL), lambda p, i: (i, 0)), wt, _spec((None, n, D_MODEL), lambda p, i: (p, 0, 0)), NT,
                   (2, N_TM), _act((2, SEQ, n)), _spec((None, TM, n), lambda p, i: (p, i, 0)))


def _wgrad_halves_tn(name, d, n_tile, h):
    _, _, n = d.shape
    return _matmul(name, d, _spec((None, SEQ, n_tile), lambda p, j: (p, 0, j)), h, _resident((SEQ, D_MODEL)), TN,
                   (2, n // n_tile), _act((2, n, D_MODEL)), _spec((None, n_tile, D_MODEL), lambda p, j: (p, j, 0)))


def _wgrad_cols_tn(name, d, n_tile, h):
    n = d.shape[1]
    return _matmul(name, d, _spec((SEQ, n_tile), lambda j: (0, j)), h, _resident((SEQ, D_MODEL)), TN,
                   (n // n_tile,), _act((n, D_MODEL)), _spec((n_tile, D_MODEL), lambda j: (j, 0)))


def _wgrad_cols_transposed_tn(name, h, d, n_tile):
    nb = d.shape[1] // n_tile
    return _matmul(name, d, _spec((SEQ, n_tile), lambda j: (0, j)), h, _resident((SEQ, D_MODEL)), TN, (nb,),
                   _act((nb, D_MODEL, n_tile)), _spec((None, D_MODEL, n_tile), lambda j: (j, 0, 0)), transpose_out=True)


NORM_ROWS = 512


def _rstd(x):
    return lax.rsqrt(jnp.mean(x * x, axis=-1, keepdims=True) + RMS_EPS)


def _norm_fwd(name, x, gamma):
    def body(x_ref, g_ref, h_ref):
        x = x_ref[...]
        h_ref[...] = (x * _rstd(x) * g_ref[...]).astype(BF16)

    row = _spec((NORM_ROWS, D_MODEL), lambda i: (i, 0))
    return pl.pallas_call(
        body, name=name, grid=(SEQ // NORM_ROWS,), in_specs=[row, _spec((1, D_MODEL), lambda i: (0, 0))], out_specs=row,
        out_shape=jax.ShapeDtypeStruct((SEQ, D_MODEL), BF16), compiler_params=_params(("parallel",)),
    )(x, gamma)


def _norm_bwd_rows(x, gamma, dh):
    r = _rstd(x)
    xh = x * r
    dxh = dh * gamma
    dx = r * (dxh - xh * jnp.mean(dxh * xh, axis=-1, keepdims=True))
    return dx, jnp.sum(dh * xh, axis=0, keepdims=True)


def _loss_head(x, gamma, target):
    def body(x_ref, g_ref, t_ref, loss_ref, dx_ref, dx16_ref, dg_ref):
        x = x_ref[...]
        gamma = g_ref[...]
        err = x * _rstd(x) * gamma - t_ref[...]
        dy = err * (1.0 / D_MODEL)
        dx, dg = _norm_bwd_rows(x, gamma, dy)
        dx_ref[...] = dx
        dx16_ref[...] = dx.astype(BF16)
        part = 0.5 * jnp.sum(jnp.sum(err * err, axis=-1, keepdims=True) * (1.0 / D_MODEL), axis=0, keepdims=True)
        part = jnp.broadcast_to(part, loss_ref.shape)

        @pl.when(pl.program_id(0) == 0)
        def _():
            dg_ref[...] = dg
            loss_ref[...] = part

        @pl.when(pl.program_id(0) > 0)
        def _():
            dg_ref[...] += dg
            loss_ref[...] += part

    row = _spec((NORM_ROWS, D_MODEL), lambda i: (i, 0))
    vec = _spec((1, D_MODEL), lambda i: (0, 0))
    return pl.pallas_call(
        body, name="loss_head", grid=(SEQ // NORM_ROWS,), in_specs=[row, vec, row],
        out_specs=[_spec((1, 128), lambda i: (0, 0)), row, row, vec],
        out_shape=[jax.ShapeDtypeStruct((1, 128), F32), _act(dtype=F32), _act(), jax.ShapeDtypeStruct((1, D_MODEL), F32)],
        compiler_params=_params(("arbitrary",)),
    )(x, gamma, target)


def _sigmoid(x):
    return 1.0 / (1.0 + jnp.exp(-x))


def _rows(ref, c):
    return ref[pl.ds(pl.multiple_of(c * ROW_CHUNK, ROW_CHUNK), ROW_CHUNK), :].astype(F32)


def _rows_before(ref, c):
    start = pl.multiple_of(jnp.maximum(c * ROW_CHUNK - HALO, 0), HALO)
    rows = ref[pl.ds(start, HALO), :].astype(F32)
    return jnp.where(c > 0, rows, 0.0)


def _rows_after(ref, c, n_chunks):
    start = pl.multiple_of(jnp.minimum((c + 1) * ROW_CHUNK, SEQ - HALO), HALO)
    rows = ref[pl.ds(start, HALO), :].astype(F32)
    return jnp.where(c < n_chunks - 1, rows, 0.0)


def _shift_down(z, before, n):
    return pltpu.roll(jnp.concatenate([before, z], axis=0), n, 0)[HALO:]


def _shift_up(z, after, n):
    rows = z.shape[0]
    return pltpu.roll(jnp.concatenate([z, after], axis=0), rows + HALO - n, 0)[:rows]


def _conv_rows(z, before, w):
    z1 = _shift_down(z, before, 1)
    z2 = _shift_down(z, before, 2)
    return w[2:3, :] * z + w[1:2, :] * z1 + w[0:1, :] * z2, z1, z2


def _conv_t_rows(dy, after, w):
    return w[2:3, :] * dy + w[1:2, :] * _shift_up(dy, after, 1) + w[0:1, :] * _shift_up(dy, after, 2)


N_ROW_CHUNKS = SEQ // ROW_CHUNK


FF_COLS = 256
N_FF_COLS = D_FF // FF_COLS


def _ffn_mid_fwd(name, gu, conv_w):
    def body(gu_ref, w_ref, a_ref):
        w = w_ref[...]

        def chunk(c, carry):
            g = _rows(gu_ref.at[0], c)
            u = _rows(gu_ref.at[1], c)
            gc, _, _ = _conv_rows(g, _rows_before(gu_ref.at[0], c), w)
            a_ref[pl.ds(pl.multiple_of(c * ROW_CHUNK, ROW_CHUNK), ROW_CHUNK), :] = (gc * _sigmoid(gc) * u).astype(BF16)
            return carry

        lax.fori_loop(0, N_ROW_CHUNKS, chunk, 0)

    col = _spec((SEQ, FF_COLS), lambda j: (0, j))
    return pl.pallas_call(
        body, name=name, grid=(N_FF_COLS,),
        in_specs=[_spec((2, SEQ, FF_COLS), lambda j: (0, 0, j)), _spec((3, FF_COLS), lambda j: (0, j))], out_specs=col,
        out_shape=_act((SEQ, D_FF)), compiler_params=_params(("parallel",)),
    )(gu, conv_w)


def _ffn_mid_bwd(name, gu, conv_w, da):
    def body(gu_ref, w_ref, da_ref, dgu_ref, dw_ref, dgc_ref):
        w = w_ref[...]

        def first(c, acc):
            g = _rows(gu_ref.at[0], c)
            u = _rows(gu_ref.at[1], c)
            d = _rows(da_ref, c)
            gc, g1, g2 = _conv_rows(g, _rows_before(gu_ref.at[0], c), w)
            sg = _sigmoid(gc)
            rows = pl.ds(pl.multiple_of(c * ROW_CHUNK, ROW_CHUNK), ROW_CHUNK)
            dgu_ref[1, rows, :] = (d * gc * sg).astype(BF16)
            dgc = d * u * (sg * (1.0 + gc * (1.0 - sg)))
            dgc_ref[rows, :] = dgc
            return (acc[0] + jnp.sum(dgc * g2, axis=0, keepdims=True), acc[1] + jnp.sum(dgc * g1, axis=0, keepdims=True),
                    acc[2] + jnp.sum(dgc * g, axis=0, keepdims=True))

        zero = jnp.zeros((1, FF_COLS), F32)
        acc = lax.fori_loop(0, N_ROW_CHUNKS, first, (zero, zero, zero))
        for r in range(3):
            dw_ref[r:r + 1, :] = acc[r]

        def second(c, carry):
            dgc = _rows(dgc_ref, c)
            dg = _conv_t_rows(dgc, _rows_after(dgc_ref, c, N_ROW_CHUNKS), w)
            dgu_ref[0, pl.ds(pl.multiple_of(c * ROW_CHUNK, ROW_CHUNK), ROW_CHUNK), :] = dg.astype(BF16)
            return carry

        lax.fori_loop(0, N_ROW_CHUNKS, second, 0)

    pair = _spec((2, SEQ, FF_COLS), lambda j: (0, 0, j))
    wspec = _spec((3, FF_COLS), lambda j: (0, j))
    return pl.pallas_call(
        body, name=name, grid=(N_FF_COLS,), in_specs=[pair, wspec, _spec((SEQ, FF_COLS), lambda j: (0, j))],
        out_specs=[pair, wspec], out_shape=[_act((2, SEQ, D_FF)), jax.ShapeDtypeStruct((3, D_FF), F32)],
        scratch_shapes=[pltpu.VMEM((SEQ, FF_COLS), F32)],
        compiler_params=_params(("parallel",)),
    )(gu, conv_w, da)


SC_COLS = 256
N_SC = D_MODEL // SC_COLS


def _sc_specs():
    return [_spec((SEQ, SC_COLS), lambda j, part=part: (0, part * N_SC + j)) for part in range(3)]


def _sc_mid_fwd(p, conv_w):
    def body(b_ref, c_ref, h_ref, w_ref, y_ref):
        w = w_ref[...]

        def chunk(c, carry):
            z = _rows(c_ref, c) * _rows(h_ref, c)
            before = _rows_before(c_ref, c) * _rows_before(h_ref, c)
            zc, _, _ = _conv_rows(z, before, w)
            y_ref[pl.ds(pl.multiple_of(c * ROW_CHUNK, ROW_CHUNK), ROW_CHUNK), :] = (_rows(b_ref, c) * zc).astype(BF16)
            return carry

        lax.fori_loop(0, N_ROW_CHUNKS, chunk, 0)

    col = _spec((SEQ, SC_COLS), lambda j: (0, j))
    return pl.pallas_call(
        body, name="sc_mid_fwd", grid=(N_SC,), in_specs=_sc_specs() + [_spec((3, SC_COLS), lambda j: (0, j))], out_specs=col,
        out_shape=jax.ShapeDtypeStruct((SEQ, D_MODEL), BF16), compiler_params=_params(("parallel",)),
    )(p, p, p, conv_w)


def _sc_mid_bwd(p, conv_w, dy):
    def body(b_ref, c_ref, h_ref, w_ref, dy_ref, db_ref, dc_ref, dh_ref, dw_ref, dzc_ref):
        w = w_ref[...]

        def first(c, acc):
            z = _rows(c_ref, c) * _rows(h_ref, c)
            before = _rows_before(c_ref, c) * _rows_before(h_ref, c)
            zc, z1, z2 = _conv_rows(z, before, w)
            d = _rows(dy_ref, c)
            rows = pl.ds(pl.multiple_of(c * ROW_CHUNK, ROW_CHUNK), ROW_CHUNK)
            db_ref[rows, :] = (d * zc).astype(BF16)
            dzc = d * _rows(b_ref, c)
            dzc_ref[rows, :] = dzc
            return (acc[0] + jnp.sum(dzc * z2, axis=0, keepdims=True), acc[1] + jnp.sum(dzc * z1, axis=0, keepdims=True),
                    acc[2] + jnp.sum(dzc * z, axis=0, keepdims=True))

        zero = jnp.zeros((1, SC_COLS), F32)
        acc = lax.fori_loop(0, N_ROW_CHUNKS, first, (zero, zero, zero))
        for r in range(3):
            dw_ref[r:r + 1, :] = acc[r]

        def second(c, carry):
            dz = _conv_t_rows(_rows(dzc_ref, c), _rows_after(dzc_ref, c, N_ROW_CHUNKS), w)
            rows = pl.ds(pl.multiple_of(c * ROW_CHUNK, ROW_CHUNK), ROW_CHUNK)
            dc_ref[rows, :] = (dz * _rows(h_ref, c)).astype(BF16)
            dh_ref[rows, :] = (dz * _rows(c_ref, c)).astype(BF16)
            return carry

        lax.fori_loop(0, N_ROW_CHUNKS, second, 0)

    col = _spec((SEQ, SC_COLS), lambda j: (0, j))
    wspec = _spec((3, SC_COLS), lambda j: (0, j))
    act = jax.ShapeDtypeStruct((SEQ, D_MODEL), BF16)
    return pl.pallas_call(
        body, name="sc_mid_bwd", grid=(N_SC,), in_specs=_sc_specs() + [wspec, col], out_specs=[col, col, col, wspec],
        out_shape=[act, act, act, jax.ShapeDtypeStruct((3, D_MODEL), F32)],
        scratch_shapes=[pltpu.VMEM((SEQ, SC_COLS), F32)], compiler_params=_params(("parallel",)),
    )(p, p, p, conv_w, dy)


GLA_GROUP = 4
GLA_ROWS = GLA_GROUP * CHUNK
N_GROUPS = N_CHUNKS // GLA_GROUP
Q0, K0, V0, R0, G0 = 0, KEY_DIM, 2 * KEY_DIM, 2 * KEY_DIM + VALUE_DIM, 2 * KEY_DIM + 2 * VALUE_DIM


def _tri(strict):
    r = lax.broadcasted_iota(jnp.int32, (CHUNK, CHUNK), 0)
    c = lax.broadcasted_iota(jnp.int32, (CHUNK, CHUNK), 1)
    return jnp.where(c < r if strict else c <= r, 1.0, 0.0).astype(F32)


def _cumsum_rows(tri, x):
    return jnp.dot(tri, x, preferred_element_type=F32, precision=lax.Precision.HIGHEST)


def _gate_logits(gl, wgu, b_gate):
    return jnp.dot(gl, wgu, preferred_element_type=F32) + b_gate


def _log_decay(logits):
    return (jnp.minimum(logits, 0.0) - jnp.log(1.0 + jnp.exp(-jnp.abs(logits)))) * (1.0 / GATE_NORMALIZER)


def _head(x, h, width):
    return x[:, h * width:(h + 1) * width]


def _gla_fwd(proj, wgu, b_gate, gn):
    def body(p_ref, wgu_ref, b_ref, gn_ref, o_ref, og_ref, st_ref, state):
        @pl.when(pl.program_id(0) == 0)
        def _():
            state[...] = jnp.zeros_like(state)

        tri = _tri(False)
        la = _log_decay(_gate_logits(p_ref[:, G0:G0 + GATE_PAD], wgu_ref[...], b_ref[...]))
        for c in range(GLA_GROUP):
            rows = slice(c * CHUNK, (c + 1) * CHUNK)
            cum = _cumsum_rows(tri, la[rows])
            tot = cum[CHUNK - 1:CHUNK, :]
            kd = (p_ref[rows, K0:K0 + KEY_DIM].astype(F32) * jnp.exp(tot - cum)).astype(BF16)
            decay = jnp.exp(tot)
            q = (p_ref[rows, Q0:Q0 + KEY_DIM].astype(F32) * (HEAD_K ** -0.5)).astype(BF16)
            v = p_ref[rows, V0:V0 + VALUE_DIM]
            for h in range(GLA_HEADS):
                upd = lax.dot_general(_head(v, h, HEAD_V), _head(kd, h, HEAD_K), (TN, ((), ())), preferred_element_type=F32)
                s = state[h] * _head(decay, h, HEAD_K) + upd
                state[h] = s
                st_ref[c, h] = s
                o_ref[rows, h * HEAD_V:(h + 1) * HEAD_V] = lax.dot_general(
                    _head(q, h, HEAD_K), s.astype(BF16), (NT, ((), ())), preferred_element_type=F32)
        r = p_ref[:, R0:R0 + VALUE_DIM].astype(F32)
        gate = r * _sigmoid(r) * gn_ref[...]
        for h in range(GLA_HEADS):
            cols = slice(h * HEAD_V, (h + 1) * HEAD_V)
            o = o_ref[:, cols]
            og_ref[:, cols] = (o * _rstd(o) * gate[:, cols]).astype(BF16)

    rows = _spec((GLA_ROWS, VALUE_DIM), lambda i: (i, 0))
    const = lambda shape: _spec(shape, lambda i: (0,) * len(shape))
    return pl.pallas_call(
        body, name="gla_fwd", grid=(N_GROUPS,),
        in_specs=[_spec((GLA_ROWS, PROJ_A_PAD), lambda i: (i, 0)), const((GATE_PAD, KEY_DIM)), const((1, KEY_DIM)),
                  const((1, VALUE_DIM))],
        out_specs=[rows, rows, _spec((GLA_GROUP, GLA_HEADS, HEAD_V, HEAD_K), lambda i: (i, 0, 0, 0))],
        out_shape=[jax.ShapeDtypeStruct((SEQ, VALUE_DIM), F32), jax.ShapeDtypeStruct((SEQ, VALUE_DIM), BF16),
                   jax.ShapeDtypeStruct((N_CHUNKS, GLA_HEADS, HEAD_V, HEAD_K), F32)],
        scratch_shapes=[pltpu.VMEM((GLA_HEADS, HEAD_V, HEAD_K), F32)], compiler_params=_params(("arbitrary",)),
    )(proj, wgu, b_gate, gn)


def _gla_bwd(proj, wgu, b_gate, gn, o, states, dog):
    last = N_GROUPS - 1

    def body(p_ref, wgu_ref, b_ref, gn_ref, o_ref, st_ref, stp_ref, dog_ref, dp_ref, dwgu_ref, db_ref, dgn_ref, carry, do_buf):
        step = pl.program_id(0)

        @pl.when(step == 0)
        def _():
            carry[...] = jnp.zeros_like(carry)

        r = p_ref[:, R0:R0 + VALUE_DIM].astype(F32)
        sr = _sigmoid(r)
        silu = r * sr
        gn_row = gn_ref[...]
        dog_rows = dog_ref[...].astype(F32)
        dn = dog_rows * silu
        dgn_cols = []
        for h in range(GLA_HEADS):
            cols = slice(h * HEAD_V, (h + 1) * HEAD_V)
            oh = o_ref[:, cols]
            rs = _rstd(oh)
            ohat = oh * rs
            dn_h = dn[:, cols]
            dgn_cols.append(jnp.sum(dn_h * ohat, axis=0, keepdims=True))
            dohat = dn_h * gn_row[:, cols]
            do_buf[:, cols] = rs * (dohat - ohat * jnp.mean(dohat * ohat, axis=-1, keepdims=True))
            n_h = ohat * gn_row[:, cols]
            dp_ref[:, R0 + h * HEAD_V:R0 + (h + 1) * HEAD_V] = (
                dog_rows[:, cols] * n_h * (sr[:, cols] * (1.0 + r[:, cols] * (1.0 - sr[:, cols])))).astype(BF16)
        dgn = jnp.concatenate(dgn_cols, axis=1)

        tri = _tri(False)
        tri_strict = _tri(True)
        gl = p_ref[:, G0:G0 + GATE_PAD]
        logits = _gate_logits(gl, wgu_ref[...], b_ref[...])
        la = _log_decay(logits)
        dlogit_rows = []
        for c in reversed(range(GLA_GROUP)):
            rows = slice(c * CHUNK, (c + 1) * CHUNK)
            cum = _cumsum_rows(tri, la[rows])
            tot = cum[CHUNK - 1:CHUNK, :]
            fade = jnp.exp(tot - cum)
            k = p_ref[rows, K0:K0 + KEY_DIM].astype(F32)
            kd32 = k * fade
            kd = kd32.astype(BF16)
            decay = jnp.exp(tot)
            q = (p_ref[rows, Q0:Q0 + KEY_DIM].astype(F32) * (HEAD_K ** -0.5)).astype(BF16)
            v = p_ref[rows, V0:V0 + VALUE_DIM]
            do = do_buf[rows, :].astype(BF16)
            dkd_cols, ddecay_cols = [], []
            for h in range(GLA_HEADS):
                do_h = _head(do, h, HEAD_V)
                s = st_ref[c, h]
                dq = jnp.dot(do_h, s.astype(BF16), preferred_element_type=F32) * (HEAD_K ** -0.5)
                dp_ref[rows, Q0 + h * HEAD_K:Q0 + (h + 1) * HEAD_K] = dq.astype(BF16)
                g = carry[h] + lax.dot_general(do_h, _head(q, h, HEAD_K), (TN, ((), ())), preferred_element_type=F32)
                g16 = g.astype(BF16)
                dkd_cols.append(jnp.dot(_head(v, h, HEAD_V), g16, preferred_element_type=F32))
                dv = lax.dot_general(_head(kd, h, HEAD_K), g16, (NT, ((), ())), preferred_element_type=F32)
                dp_ref[rows, V0 + h * HEAD_V:V0 + (h + 1) * HEAD_V] = dv.astype(BF16)
                if c > 0:
                    s_prev = st_ref[c - 1, h]
                else:
                    s_prev = jnp.where(step < last, stp_ref[0, h], 0.0)
                ddecay_cols.append(jnp.sum(g * s_prev, axis=0, keepdims=True))
                carry[h] = g * _head(decay, h, HEAD_K)
            dkd = jnp.concatenate(dkd_cols, axis=1)
            ddecay = jnp.concatenate(ddecay_cols, axis=1)
            dp_ref[rows, K0:K0 + KEY_DIM] = (dkd * fade).astype(BF16)
            e = dkd * kd32
            dla = ddecay * decay + _cumsum_rows(tri_strict, e)
            dlogit_rows.append(dla * (1.0 / GATE_NORMALIZER) * (1.0 - _sigmoid(logits[rows])))
        dlogit = jnp.concatenate(dlogit_rows[::-1], axis=0)
        dlogit16 = dlogit.astype(BF16)
        dp_ref[:, G0:G0 + GATE_PAD] = lax.dot_general(
            dlogit16, wgu_ref[...], (NT, ((), ())), preferred_element_type=F32).astype(BF16)
        dwgu = lax.dot_general(gl, dlogit16, (TN, ((), ())), preferred_element_type=F32)
        db = jnp.sum(dlogit, axis=0, keepdims=True)

        @pl.when(step == 0)
        def _():
            dwgu_ref[...] = dwgu
            db_ref[...] = db
            dgn_ref[...] = dgn

        @pl.when(step > 0)
        def _():
            dwgu_ref[...] += dwgu
            db_ref[...] += db
            dgn_ref[...] += dgn

    rev = lambda i: (last - i, 0)
    rows = _spec((GLA_ROWS, VALUE_DIM), rev)
    const = lambda shape: _spec(shape, lambda i: (0,) * len(shape))
    st_shape = (GLA_HEADS, HEAD_V, HEAD_K)
    return pl.pallas_call(
        body, name="gla_bwd", grid=(N_GROUPS,),
        in_specs=[_spec((GLA_ROWS, PROJ_A_PAD), rev), const((GATE_PAD, KEY_DIM)), const((1, KEY_DIM)), const((1, VALUE_DIM)),
                  rows, _spec((GLA_GROUP,) + st_shape, lambda i: (last - i, 0, 0, 0)),
                  _spec((1,) + st_shape, lambda i: (jnp.maximum((last - i) * GLA_GROUP - 1, 0), 0, 0, 0)), rows],
        out_specs=[_spec((GLA_ROWS, PROJ_A_PAD), rev), const((GATE_PAD, KEY_DIM)), const((1, KEY_DIM)), const((1, VALUE_DIM))],
        out_shape=[jax.ShapeDtypeStruct((SEQ, PROJ_A_PAD), BF16), jax.ShapeDtypeStruct((GATE_PAD, KEY_DIM), F32),
                   jax.ShapeDtypeStruct((1, KEY_DIM), F32), jax.ShapeDtypeStruct((1, VALUE_DIM), F32)],
        scratch_shapes=[pltpu.VMEM(st_shape, F32), pltpu.VMEM((GLA_ROWS, VALUE_DIM), F32)],
        compiler_params=_params(("arbitrary",)),
    )(proj, wgu, b_gate, gn, o, states, states, dog)


WGRAD_FF_TILE = D_FF // 2


def _ffn_fwd(tag, x, gamma, w_up_t, conv_w, w_down):
    h = _norm_fwd(f"ffn{tag}_norm", x, gamma)
    gu = _proj_halves_nt(f"ffn{tag}_up", h, w_up_t)
    a = _ffn_mid_fwd(f"ffn{tag}_mid", gu, conv_w)
    return _wide_nn(f"ffn{tag}_down", a, w_down, x=x), (h, gu, a)


def _owner_blocks(d, rows=None):
    if rows is not None:
        d = d[:rows]
    return d.reshape((N_DEV, -1) + d.shape[-1:])


def _ffn_bwd(tag, x, gamma, w_up_t, conv_w, w_down, saved, dx, dx16, swap):
    h, gu, a = saved
    da = _wide_nt(f"ffn{tag}_da", dx16, w_down)
    d_w_down = _owner_blocks(_wgrad_cols_tn(f"ffn{tag}_dwdown", a, WGRAD_FF_TILE, dx16))
    dgu, d_conv = _ffn_mid_bwd(f"ffn{tag}_mid_bwd", gu, conv_w, da)
    d_w_up_t = _owner_blocks(_wgrad_halves_tn(f"ffn{tag}_dwup", dgu, WGRAD_FF_TILE, h))
    parts = (d_w_up_t, d_w_down)
    dx, dx16, d_gamma, *received = _sum_blocks_nn(
        f"ffn{tag}_dh", dgu, w_up_t, norm=(x, gamma, dx), swap=parts if swap else ())
    return dx, dx16, d_gamma, d_conv, parts, received


def _local_step(x, target, w, fetch=None, emit=None):
    if fetch is None:
        local = dict(a=(w.get("a_w_in"), w.get("a_w_out")), b=(w.get("b_w_in"), w.get("b_w_out")))
        for layer in range(2):
            local[f"f{layer}"] = (w["f_w_up"][layer], w["f_w_down"][layer]) if "f_w_up" in w else None
        fetch = lambda group, after: local[group]
    swap = emit is not None
    if emit is None:
        emit = lambda group, parts, received, dx: dx
    f_norm = (w["f_norm"][0:1], w["f_norm"][1:2])

    x0 = x
    a_w_in, a_w_out = fetch("a", x0)
    h0 = _norm_fwd("a_norm", x0, w["a_norm"])
    proj = _proj_rows_nt("a_in", h0, a_w_in, PA_TILE)
    o, og, states = _gla_fwd(proj, w["a_w_gate_up"], w["a_b_gate"], w["a_gn"])
    x1 = _square("a_out", og, a_w_out, NN, x0)
    up0, down0 = fetch("f0", x1)
    x2, ffn0 = _ffn_fwd(0, x1, f_norm[0], up0, w["f_conv"][0], down0)
    b_w_in, b_w_out = fetch("b", x2)
    h2 = _norm_fwd("b_norm", x2, w["b_norm"])
    p = _proj_cols_nn("b_in", h2, b_w_in)
    y = _sc_mid_fwd(p, w["b_conv"])
    x3 = _square("b_out", y, b_w_out, NN, x2)
    up1, down1 = fetch("f1", x3)
    x4, ffn1 = _ffn_fwd(1, x3, f_norm[1], up1, w["f_conv"][1], down1)
    loss, dx, dx16, d_final_norm = _loss_head(x4, w["final_norm"], target)

    dx, dx16, d_f_norm1, d_fconv1, parts_f1, got = _ffn_bwd(
        1, x3, f_norm[1], up1, w["f_conv"][1], down1, ffn1, dx, dx16, swap)
    dx16 = emit("f1", parts_f1, got, dx16)

    dy = _square("b_dy", dx16, b_w_out, NT)
    d_b_w_out = _owner_blocks(_wgrad_cols_tn("b_dwout", y, OUT_TILE, dx16))
    db, dc, dhh, d_b_conv = _sc_mid_bwd(p, w["b_conv"], dy)
    dp = jnp.concatenate([db, dc, dhh], axis=1)
    parts_b = (_wgrad_cols_transposed_tn("b_dwin", h2, dp, B_SHARD), d_b_w_out)
    dx, dx16, d_b_norm, *got = _sum_cols_nt("b_dh", dp, b_w_in, norm=(x2, w["b_norm"], dx), swap=parts_b if swap else ())
    dx16 = emit("b", parts_b, got, dx16)

    dx, dx16, d_f_norm0, d_fconv0, parts_f0, got = _ffn_bwd(
        0, x1, f_norm[0], up0, w["f_conv"][0], down0, ffn0, dx, dx16, swap)
    dx16 = emit("f0", parts_f0, got, dx16)

    dog = _square("a_dog", dx16, a_w_out, NT)
    d_a_w_out = _owner_blocks(_wgrad_cols_tn("a_dwout", og, OUT_TILE, dx16))
    dproj, d_wgu, d_b_gate, d_gn = _gla_bwd(proj, w["a_w_gate_up"], w["a_b_gate"], w["a_gn"], o, states, dog)
    parts_a = (_owner_blocks(_wgrad_cols_tn("a_dwin", dproj, PA_TILE, h0), PROJ_A), d_a_w_out)
    dx, _, d_a_norm, *got = _wide_nn("a_dh", dproj, a_w_in, norm=(x0, w["a_norm"], dx), swap=parts_a if swap else ())
    emit("a", parts_a, got, dx)

    grads = dict(
        a_norm=d_a_norm, a_w_in=parts_a[0], a_w_gate_up=d_wgu, a_b_gate=d_b_gate, a_gn=d_gn, a_w_out=parts_a[1],
        b_norm=d_b_norm, b_w_in=parts_b[0], b_conv=d_b_conv, b_w_out=parts_b[1],
        f_norm=(d_f_norm0, d_f_norm1), f_w_up=(parts_f0[0], parts_f1[0]), f_conv=(d_fconv0, d_fconv1),
        f_w_down=(parts_f0[1], parts_f1[1]), final_norm=d_final_norm)
    return loss[0, 0], dx, grads


MESH_ID = pl.DeviceIdType.MESH
ANY = pl.BlockSpec(memory_space=pl.ANY)
N_PEERS = N_DEV - 1


def _position():
    return lax.axis_index("x"), lax.axis_index("y"), lax.axis_index("c")


def _slot(px, py, pc):
    return 4 * px + 2 * py + pc


def _all_gather(name, shards):
    n = len(shards)

    def body(*refs):
        ins, outs = refs[:n], refs[n:2 * n]
        send_sems, recv_sems, local_sems = refs[2 * n:]
        x, y, c = _position()
        me, sibling = (x, y, c), (x, y, 1 - c)
        chips = [(1 - x, y), (x, 1 - y), (1 - x, 1 - y)]

        def copy(t, k, block, to, from_input=False):
            dst = outs[t].at[_slot(*block)]
            return pltpu.make_async_remote_copy(
                src_ref=ins[t] if from_input else dst, dst_ref=dst, send_sem=send_sems.at[t, k], recv_sem=recv_sems.at[t, k],
                device_id=to, device_id_type=MESH_ID)

        mine = [pltpu.make_async_copy(ins[t], outs[t].at[_slot(*me)], local_sems.at[t]) for t in range(n)]
        for cp in mine:
            cp.start()
        first = []
        for t in range(n):
            first.append(copy(t, 0, me, sibling, True))
            first += [copy(t, 1 + j, me, (*chip, c), True) for j, chip in enumerate(chips)]
        for cp in first:
            cp.start()
        passed = []
        for t in range(n):
            for j, chip in enumerate(chips):
                copy(t, 1 + j, (*chip, c), me).wait_recv()
                fwd = copy(t, 4 + j, (*chip, c), sibling)
                fwd.start()
                passed.append(fwd)
        for t in range(n):
            copy(t, 0, sibling, me).wait_recv()
            for j, chip in enumerate(chips):
                copy(t, 4 + j, (*chip, 1 - c), me).wait_recv()
        for cp in first + passed:
            cp.wait_send()
        for cp in mine:
            cp.wait()

    return pl.pallas_call(
        body, name=name, in_specs=[ANY] * n, out_specs=[ANY] * n,
        out_shape=[jax.ShapeDtypeStruct((N_DEV,) + s.shape, s.dtype) for s in shards],
        scratch_shapes=[pltpu.SemaphoreType.DMA((n, N_PEERS)), pltpu.SemaphoreType.DMA((n, N_PEERS)), pltpu.SemaphoreType.DMA((n,))],
    )(*shards)


SIBLING_AND_SAME_CORE = (1, 2, 4, 6)
SAME_CORE = (2, 4, 6)


def _flip(x, y, c, k):
    return x ^ (k >> 2), y ^ ((k >> 1) & 1), c ^ (k & 1)


N_CHIPS = N_DEV // 2


def _chip(px, py):
    return 2 * px + py


def _pair_copies(parts, received, send_sems, recv_sems):
    x, y, c = lax.axis_index("x"), lax.axis_index("y"), lax.axis_index("c")
    sibling = (x, y, 1 - c)
    copies = []
    for t in range(len(parts)):
        for q in range(N_DEV // 2):
            send = pltpu.make_async_remote_copy(
                src_ref=parts[t].at[2 * q + 1 - c], dst_ref=received[t].at[q], send_sem=send_sems.at[t, q],
                recv_sem=recv_sems.at[t, q], device_id=sibling, device_id_type=pl.DeviceIdType.MESH)
            landed = received[t].at[q]
            arrival = pltpu.make_async_remote_copy(
                src_ref=landed, dst_ref=landed, send_sem=send_sems.at[t, q], recv_sem=recv_sems.at[t, q],
                device_id=sibling, device_id_type=pl.DeviceIdType.MESH)
            copies.append((send, arrival))
    return copies


PAIR_ROWS = 1024


def _pair_add(name, part, received, side):
    _, rows, cols = part.shape
    tiles = [t for t in range(PAIR_ROWS, 0, -BF16_ROWS) if rows % t == 0]
    tr = tiles[0] if tiles else rows

    def body(side_ref, p_ref, r_ref, o_ref):
        o_ref[...] = (p_ref[...].astype(F32) + r_ref[...].astype(F32)).astype(BF16)

    tile = _spec((None, tr, cols), lambda q, i, side_ref: (q, i, 0))
    return pl.pallas_call(
        body, name=name,
        grid_spec=pltpu.PrefetchScalarGridSpec(
            num_scalar_prefetch=1, grid=(N_CHIPS, rows // tr),
            in_specs=[_spec((None, tr, cols), lambda q, i, side_ref: (2 * q + side_ref[0], i, 0)), tile], out_specs=tile),
        out_shape=jax.ShapeDtypeStruct((N_CHIPS, rows, cols), BF16), compiler_params=_params(("parallel", "parallel")),
    )(side, part, received)


def _send_copy(parts, landing, send_sems, recv_sems, t, s, k):
    x, y, c = _position()
    px, py, _ = _flip(x, y, c, k)
    return pltpu.make_async_remote_copy(
        src_ref=parts[t].at[_chip(px, py)], dst_ref=landing[t].at[_chip(x, y)], send_sem=send_sems.at[s],
        recv_sem=recv_sems.at[s], device_id=(px, py, c), device_id_type=MESH_ID)


def _send_arrival(landing, send_sems, recv_sems, t, s, k):
    x, y, c = _position()
    px, py, _ = _flip(x, y, c, k)
    landed = landing[t].at[_chip(px, py)]
    return pltpu.make_async_remote_copy(
        src_ref=landed, dst_ref=landed, send_sem=send_sems.at[s], recv_sem=recv_sems.at[s],
        device_id=(px, py, c), device_id_type=MESH_ID)


def _handshake(peers):
    x, y, c = _position()
    barrier = pltpu.get_barrier_semaphore()
    for k in peers:
        pl.semaphore_signal(barrier, inc=1, device_id=_flip(x, y, c, k), device_id_type=MESH_ID)
    pl.semaphore_wait(barrier, len(peers))


def _sequencer(name, collective_id, n_copies, body, operands, out_type):
    n_arrays = len(operands)
    return pl.kernel(
        body, out_type=out_type, mesh=plsc.ScalarSubcoreMesh(axis_name="sequencer", num_cores=1), name=name,
        scratch_types=(pltpu.SemaphoreType.DMA((n_copies,)), pltpu.SemaphoreType.DMA((n_copies,)),
                       pltpu.SemaphoreType.DMA((n_arrays,))),
        compiler_params=pltpu.CompilerParams(collective_id=collective_id))(*operands)


def _sequencer_exchange(name, collective_id, parts, after=()):
    n, n_peers, n_in = len(parts), len(SAME_CORE), len(parts) + len(after)

    def body(*refs):
        src, landing = refs[:n], refs[n_in:n_in + n]
        send_sems, recv_sems, local_sems = refs[n_in + n:]
        _handshake(SAME_CORE)
        x, y, _ = _position()
        mine = [pltpu.make_async_copy(src[t].at[_chip(x, y)], landing[t].at[_chip(x, y)], local_sems.at[t]) for t in range(n)]
        for cp in mine:
            cp.start()
        sent = [_send_copy(src, landing, send_sems, recv_sems, t, t * n_peers + j, k)
                for t in range(n) for j, k in enumerate(SAME_CORE)]
        for cp in sent:
            cp.start()
        for t in range(n):
            for j, k in enumerate(SAME_CORE):
                _send_arrival(landing, send_sems, recv_sems, t, t * n_peers + j, k).wait_recv()
        for cp in sent:
            cp.wait_send()
        for cp in mine:
            cp.wait()

    landing = [jax.ShapeDtypeStruct(p.shape, p.dtype) for p in parts]
    return _sequencer(name, collective_id, n * n_peers, body, list(parts) + list(after), landing)


def _sequencer_gather(name, collective_id, shards):
    n, per = len(shards), N_PEERS

    def body(*refs):
        src, out = refs[:n], refs[n:2 * n]
        send_sems, recv_sems, local_sems = refs[2 * n:]
        _handshake(SIBLING_AND_SAME_CORE)
        x, y, c = _position()
        me, sibling = (x, y, c), (x, y, 1 - c)

        def copy(t, j, block, to, from_input=False):
            dst = out[t].at[_slot(*block)]
            return pltpu.make_async_remote_copy(
                src_ref=src[t] if from_input else dst, dst_ref=dst, send_sem=send_sems.at[t * per + j],
                recv_sem=recv_sems.at[t * per + j], device_id=to, device_id_type=MESH_ID)

        mine = [pltpu.make_async_copy(src[t], out[t].at[_slot(*me)], local_sems.at[t]) for t in range(n)]
        for cp in mine:
            cp.start()
        sent = [copy(t, j, me, _flip(x, y, c, k), True) for t in range(n) for j, k in enumerate(SIBLING_AND_SAME_CORE)]
        for cp in sent:
            cp.start()
        for t in range(n):
            for j, k in enumerate(SAME_CORE):
                block = _flip(x, y, c, k)
                copy(t, 1 + j, block, me).wait_recv()
                forward = copy(t, 4 + j, block, sibling)
                forward.start()
                sent.append(forward)
        for t in range(n):
            copy(t, 0, sibling, me).wait_recv()
            for j, k in enumerate(SAME_CORE):
                copy(t, 4 + j, _flip(x, y, 1 - c, k), me).wait_recv()
        for cp in sent:
            cp.wait_send()
        for cp in mine:
            cp.wait()

    gathered = [jax.ShapeDtypeStruct((N_DEV,) + s.shape, s.dtype) for s in shards]
    return _sequencer(name, collective_id, n * per, body, shards, gathered)


ADAM_ROWS = 512
BF16_ROWS = 16


def _adam_update(w, g, m, v):
    m = ADAM_B1 * m + (1.0 - ADAM_B1) * g
    v = ADAM_B2 * v + (1.0 - ADAM_B2) * (g * g)
    m_hat = m / (1.0 - ADAM_B1 ** ADAM_STEP)
    v_hat = v / (1.0 - ADAM_B2 ** ADAM_STEP)
    delta = -ADAM_LR * (m_hat / (jnp.sqrt(v_hat) + ADAM_EPS) + ADAM_WD * w)
    return delta, m, v


def _sum_slots(ref):
    total = ref[0].astype(F32)
    for d in range(1, ref.shape[0]):
        total = total + ref[d].astype(F32)
    return total


def _adamw_sum(name, landed, w, m, v):
    layers, rows, cols = w.shape
    tiles = [t for t in range(ADAM_ROWS, 0, -BF16_ROWS) if rows % t == 0]
    tr = tiles[0] if tiles else rows
    nt = rows // tr

    def body(*refs):
        parts = refs[:layers]
        w_ref, m_ref, v_ref, g_ref, d_ref, nm_ref, nv_ref = refs[layers:]
        layer = pl.program_id(0)
        g = _sum_slots(parts[0])
        for q in range(1, layers):
            g = jnp.where(layer == q, _sum_slots(parts[q]), g)
        delta, new_m, new_v = _adam_update(w_ref[...], g, m_ref[...], v_ref[...])
        g_ref[...] = g
        d_ref[...] = delta
        nm_ref[...] = new_m
        nv_ref[...] = new_v

    def part_spec(q):
        return _spec((N_CHIPS, tr, cols), lambda l, i: (0, jnp.where(l == q, i, jnp.where(l < q, 0, nt - 1)), 0))

    tile = _spec((None, tr, cols), lambda l, i: (l, i, 0))
    out = jax.ShapeDtypeStruct((layers, rows, cols), F32)
    return pl.pallas_call(
        body, name=name, grid=(layers, nt), in_specs=[part_spec(q) for q in range(layers)] + [tile] * 3,
        out_specs=[tile] * 4, out_shape=[out] * 4, compiler_params=_params(("arbitrary", "arbitrary")),
    )(*landed, w, m, v)


def _sum_small(landed):
    def body(in_ref, out_ref):
        out_ref[...] = _sum_slots(in_ref)

    return pl.pallas_call(body, name="small_grad_sum", out_shape=jax.ShapeDtypeStruct(landed.shape[1:], F32))(landed)


def _adamw_small(name, g, w, m, v):
    def body(g_ref, w_ref, m_ref, v_ref, d_ref, nm_ref, nv_ref):
        d_ref[...], nm_ref[...], nv_ref[...] = _adam_update(w_ref[...], g_ref[...], m_ref[...], v_ref[...])

    out = jax.ShapeDtypeStruct(w.shape, F32)
    return pl.pallas_call(body, name=name, out_shape=[out] * 3)(g, w, m, v)


LANES = 128
SUBLANES = 8
F_CONV_SHARD = D_FF // N_DEV
GATE_SHARD = KEY_DIM // N_DEV
NORM_SHARD = D_MODEL // N_DEV


def _tile_rows(a):
    flat = a.reshape(-1)
    size = -(-flat.shape[0] // (SUBLANES * LANES)) * SUBLANES * LANES
    return jnp.pad(flat, (0, size - flat.shape[0])).reshape(-1, LANES)


def _pack_rows(pieces):
    return jnp.concatenate([_tile_rows(p) for p in pieces], axis=0)


def _unpack_rows(packed, shapes):
    out, row = [], 0
    for shape in shapes:
        size = 1
        for s in shape:
            size *= s
        rows = -(-size // (SUBLANES * LANES)) * SUBLANES
        piece = packed[..., row:row + rows, :]
        out.append(piece.reshape(piece.shape[:-2] + (rows * LANES,))[..., :size])
        row += rows
    return out


SMALL_SHARDS = ((GATE_RANK, GATE_SHARD), (1, NORM_SHARD), (3, NORM_SHARD), (2, 3, F_CONV_SHARD))


def _unpack_small_shards(g):
    gate, b_norm, b_conv, f_conv = _unpack_rows(g, SMALL_SHARDS)
    gate = gate.reshape(N_DEV, GATE_RANK, GATE_SHARD).transpose(1, 0, 2).reshape(GATE_RANK, KEY_DIM)
    b_norm = b_norm.reshape(1, D_MODEL)
    b_conv = b_conv.reshape(N_DEV, 3, NORM_SHARD).transpose(1, 0, 2).reshape(3, D_MODEL)
    f_conv = f_conv.reshape(N_DEV, 2, 3, F_CONV_SHARD).transpose(1, 2, 0, 3).reshape(2, 3, D_FF)
    return gate, b_norm, b_conv, f_conv


SMALL_LAYOUT = (("a_norm", (1, D_MODEL)), ("a_w_gate_up", (GATE_RANK, KEY_DIM)), ("a_b_gate", (1, KEY_DIM)), ("a_gn", (1, VALUE_DIM)),
                ("b_norm", (1, D_MODEL)), ("b_conv", (3, D_MODEL)), ("f_norm0", (1, D_MODEL)), ("f_norm1", (1, D_MODEL)),
                ("f_conv0", (3, D_FF)), ("f_conv1", (3, D_FF)), ("final_norm", (1, D_MODEL)))


def _pack_small_grads(g):
    full = dict(g)
    full["a_w_gate_up"] = g["a_w_gate_up"][:GATE_RANK]
    for layer in range(2):
        full[f"f_norm{layer}"] = g["f_norm"][layer]
        full[f"f_conv{layer}"] = g["f_conv"][layer]
    return _pack_rows([full[name] for name, _ in SMALL_LAYOUT])


def _unpack_small_grads(packed):
    pieces = _unpack_rows(packed, [shape for _, shape in SMALL_LAYOUT])
    out = {name: piece.reshape(shape) for (name, shape), piece in zip(SMALL_LAYOUT, pieces)}
    out["f_norm"] = jnp.stack([out["f_norm0"][0], out["f_norm1"][0]])
    out["f_conv"] = jnp.stack([out["f_conv0"], out["f_conv1"]])
    return out


def kernel(x, a_norm, a_w_in, a_w_gate_up, a_b_gate, a_gn, a_w_out, b_norm, b_w_in, b_conv, b_w_out, f_norm, f_w_up, f_conv, f_w_down, final_norm, loss_target, m_a_norm, m_a_w_in, m_a_w_gate_up, m_a_b_gate, m_a_gn, m_a_w_out, m_b_norm, m_b_w_in, m_b_conv, m_b_w_out, m_f_norm, m_f_w_up, m_f_conv, m_f_w_down, m_final_norm, v_a_norm, v_a_w_in, v_a_w_gate_up, v_a_b_gate, v_a_gn, v_a_w_out, v_b_norm, v_b_w_in, v_b_conv, v_b_w_out, v_f_norm, v_f_w_up, v_f_conv, v_f_w_down, v_final_norm):
    my_slot = _slot(*_position())

    transposed = lambda w: jnp.swapaxes(w, 1, 2)
    a_w_in_t, f_w_up_t = transposed(a_w_in), transposed(f_w_up)
    first = _all_gather("weight_gather", [a_w_in_t[0].astype(BF16), a_w_out[0].astype(BF16),
                                          _pack_rows([a_w_gate_up[0], b_norm, b_conv[0], f_conv])])
    gathers, small_shards = {}, first[2]
    later = (("f0", f_w_up_t[0], f_w_down[0]), ("b", b_w_in[0], b_w_out[0]), ("f1", f_w_up_t[1], f_w_down[1]))
    for collective_id, (group, w_in, w_out) in enumerate(later):
        w_in, w_out, small_shards = lax.optimization_barrier((w_in.astype(BF16), w_out.astype(BF16), small_shards))
        gathers[group] = _sequencer_gather(f"gather_{group}", collective_id, [w_in, w_out])
    gate_full, b_norm_full, b_conv_full, f_conv_full = _unpack_small_shards(small_shards)
    a_w_in_full = jnp.pad(first[0].reshape(PROJ_A, D_MODEL), ((0, PROJ_A_PAD - PROJ_A), (0, 0)))
    weights = dict(
        a_norm=a_norm, a_w_gate_up=jnp.pad(gate_full, ((0, GATE_PAD - GATE_RANK), (0, 0))).astype(BF16), a_b_gate=a_b_gate,
        a_gn=a_gn, b_norm=b_norm_full, b_conv=b_conv_full, f_norm=f_norm, f_conv=f_conv_full,
        final_norm=final_norm.reshape(1, D_MODEL))

    def fetch(group, after):
        if group == "a":
            return a_w_in_full, first[1].reshape(D_MODEL, D_MODEL)
        w_in, w_out = gathers[group]
        if group == "b":
            return w_in, w_out.reshape(D_MODEL, D_MODEL)
        return w_in.reshape(2, D_FF, D_MODEL), w_out.reshape(D_FF, D_MODEL)

    exchanges, pending = {}, []
    exchange_ids = dict(b=3, f0=4, a=5)
    side = lax.axis_index("c").astype(jnp.int32).reshape(1)

    def emit(group, parts, received, carry):
        sums = [_pair_add(f"pair_add_{group}_{i}", part, got, side) for i, (part, got) in enumerate(zip(parts, received))]
        carry, *sums = lax.optimization_barrier((carry, *sums))
        pending.extend(sums)
        if group != "f1":
            after = list(exchanges.values())[-1][:1] if exchanges else ()
            exchanges[group] = _sequencer_exchange(f"grads_{group}", exchange_ids[group], list(pending), after)
            pending.clear()
        return carry

    loss, dx, g = _local_step(x[0], loss_target[0], weights, fetch, emit)
    loss = lax.psum(loss, MESH_AXES)
    small_landed = _all_gather("small_grad_gather", [_pack_small_grads(g)])[0]

    (up1, down1, d_b_in, d_b_out), (up0, down0), (d_a_in, d_a_out) = (exchanges[group] for group in ("b", "f0", "a"))
    back = lambda results: tuple(transposed(r) for r in results)
    big = dict(
        b_w_in=_adamw_sum("adam_b_w_in", [d_b_in], b_w_in, m_b_w_in, v_b_w_in),
        b_w_out=_adamw_sum("adam_b_w_out", [d_b_out], b_w_out, m_b_w_out, v_b_w_out),
        f_w_up=back(_adamw_sum("adam_f_w_up", [up0, up1], f_w_up_t, transposed(m_f_w_up), transposed(v_f_w_up))),
        f_w_down=_adamw_sum("adam_f_w_down", [down0, down1], f_w_down, m_f_w_down, v_f_w_down),
        a_w_in=back(_adamw_sum("adam_a_w_in", [d_a_in], a_w_in_t, transposed(m_a_w_in), transposed(v_a_w_in))),
        a_w_out=_adamw_sum("adam_a_w_out", [d_a_out], a_w_out, m_a_w_out, v_a_w_out))
    small_g = _unpack_small_grads(_sum_small(small_landed))
    small_g["a_w_gate_up"] = lax.dynamic_slice_in_dim(small_g["a_w_gate_up"], my_slot * GATE_SHARD, GATE_SHARD, axis=1)
    small_g["b_norm"] = lax.dynamic_slice_in_dim(small_g["b_norm"], my_slot * NORM_SHARD, NORM_SHARD, axis=1)
    small_g["b_conv"] = lax.dynamic_slice_in_dim(small_g["b_conv"], my_slot * NORM_SHARD, NORM_SHARD, axis=1)
    small_g["f_conv"] = lax.dynamic_slice_in_dim(small_g["f_conv"], my_slot * F_CONV_SHARD, F_CONV_SHARD, axis=2)
    small_w = dict(
        a_norm=(a_norm, m_a_norm, v_a_norm), a_w_gate_up=(a_w_gate_up, m_a_w_gate_up, v_a_w_gate_up),
        a_b_gate=(a_b_gate, m_a_b_gate, v_a_b_gate), a_gn=(a_gn, m_a_gn, v_a_gn), b_norm=(b_norm, m_b_norm, v_b_norm),
        b_conv=(b_conv, m_b_conv, v_b_conv), f_norm=(f_norm, m_f_norm, v_f_norm), f_conv=(f_conv, m_f_conv, v_f_conv),
        final_norm=(final_norm, m_final_norm, v_final_norm))
    small = {}
    for name, (w, m, v) in small_w.items():
        flat = (w.shape[-1],) if w.ndim == 1 else w.shape[-2:]
        two_d = (-1, flat[-1])
        grad = small_g[name].reshape(w.shape)
        delta, new_m, new_v = _adamw_small(
            "adam_" + name, grad.reshape(two_d), w.reshape(two_d), m.reshape(two_d), v.reshape(two_d))
        small[name] = (grad, delta.reshape(w.shape), new_m.reshape(w.shape), new_v.reshape(w.shape))

    order = ["a_norm", "a_w_in", "a_w_gate_up", "a_b_gate", "a_gn", "a_w_out", "b_norm", "b_w_in", "b_conv", "b_w_out",
             "f_norm", "f_w_up", "f_conv", "f_w_down", "final_norm"]
    results = {**big, **small}
    outputs = [loss, dx.reshape(1, SEQ, D_MODEL)]
    for kind in range(4):
        outputs += [results[name][kind] for name in order]
    return tuple(outputs)
```

```python
import jax
import jax.numpy as jnp
from jax import lax
from jax.experimental import pallas as pl
from jax.experimental.pallas import tpu as pltpu
from jax.experimental.pallas import tpu_sc as plsc

F32 = jnp.float32
BF16 = jnp.bfloat16

N_DEV = 8
SEQ = 2048
D_MODEL = 1024
CHUNK = 64
N_CHUNKS = SEQ // CHUNK
RMS_EPS = 1e-6
GLA_HEADS = 4
KEY_DIM = 512
VALUE_DIM = 1024
HEAD_K = KEY_DIM // GLA_HEADS
HEAD_V = VALUE_DIM // GLA_HEADS
GATE_RANK = 16
GATE_PAD = 128
GATE_NORMALIZER = 16.0
PROJ_A = 2 * KEY_DIM + 2 * VALUE_DIM + GATE_RANK
PROJ_A_PAD = 2 * KEY_DIM + 2 * VALUE_DIM + GATE_PAD
A_SHARD = PROJ_A // N_DEV
B_SHARD = 3 * D_MODEL // N_DEV
D_FF = 2816
ADAM_LR = 0.001
ADAM_B1 = 0.9
ADAM_B2 = 0.999
ADAM_EPS = 1e-08
ADAM_WD = 0.01
ADAM_STEP = 10
MESH_AXES = ("x", "y", "c")

VMEM_LIMIT = 56 * 1024 * 1024
ROW_CHUNK = 256
HALO = 16


def _params(sem=None, vmem=VMEM_LIMIT):
    return pltpu.CompilerParams(dimension_semantics=sem, vmem_limit_bytes=vmem)


NN = ((1,), (0,))
NT = ((1,), (1,))
TN = ((0,), (0,))


def _matmul(name, a, a_spec, b, b_spec, dims, grid, out_shape, out_spec, k_blocks=None, a_block_cols=None, res=None,
            res_spec=None, transpose_out=False, norm=None, swap=()):
    has_res = res is not None
    n_swap = len(swap)

    def body(*refs):
        a_ref, b_ref = refs[0], refs[1]
        r_ref = refs[2] if has_res else None

        def product(lhs, rhs):
            return lax.dot_general(lhs.astype(BF16), rhs, (dims, ((), ())), preferred_element_type=F32)

        if k_blocks is None:
            v = product(a_ref[...], b_ref[...])
        else:
            v = None
            for k in range(k_blocks):
                lhs = a_ref[k] if a_block_cols is None else a_ref[:, k * a_block_cols:(k + 1) * a_block_cols]
                p = product(lhs, b_ref[k])
                v = p if v is None else v + p
        if transpose_out:
            v = v.T
        if has_res:
            v = v + r_ref[...]
        if norm is None:
            o_ref = refs[2 + has_res]
            o_ref[...] = v.astype(o_ref.dtype)
            return
        n_in = 5 + has_res
        x_ref, g_ref, dxi_ref = refs[2 + has_res:n_in]
        dx_ref, dx16_ref, dg_ref = refs[n_in + n_swap:n_in + n_swap + 3]
        if n_swap:
            copies = _pair_copies(refs[n_in:n_in + n_swap], refs[n_in + n_swap + 3:n_in + 2 * n_swap + 3], *refs[-2:])

            @pl.when(pl.program_id(0) == 0)
            def _():
                for send, _ in copies:
                    send.start()

            @pl.when(pl.program_id(0) == grid[0] - 1)
            def _():
                for send, arrival in copies:
                    arrival.wait_recv()
                    send.wait_send()

        dx, dg = _norm_bwd_rows(x_ref[...], g_ref[...], v)
        dx = dxi_ref[...] + dx
        dx_ref[...] = dx
        dx16_ref[...] = dx.astype(BF16)

        @pl.when(pl.program_id(0) == 0)
        def _():
            dg_ref[...] = dg

        @pl.when(pl.program_id(0) > 0)
        def _():
            dg_ref[...] += dg

    operands = [a, b] + ([res] if has_res else [])
    in_specs = [a_spec, b_spec] + ([res_spec] if has_res else [])
    semantics = ("parallel",) * len(grid)
    scratch = []
    if norm is not None:
        vec = _spec((1, D_MODEL), lambda i: (0, 0))
        any_space = pl.BlockSpec(memory_space=pl.ANY)
        operands += list(norm) + list(swap)
        in_specs += [out_spec, vec, out_spec] + [any_space] * n_swap
        out_shape = [_act(dtype=F32), _act(), jax.ShapeDtypeStruct((1, D_MODEL), F32)]
        out_shape += [jax.ShapeDtypeStruct((N_DEV // 2,) + p.shape[1:], p.dtype) for p in swap]
        out_spec = [out_spec, out_spec, vec] + [any_space] * n_swap
        semantics = ("arbitrary",)
        if n_swap:
            scratch = [pltpu.SemaphoreType.DMA((n_swap, N_DEV // 2))] * 2
    return pl.pallas_call(
        body, name=name, grid=grid, in_specs=in_specs, out_specs=out_spec, out_shape=out_shape, scratch_shapes=scratch,
        compiler_params=_params(semantics),
    )(*operands)


def _resident(shape):
    return pl.BlockSpec(shape, lambda *_: (0,) * len(shape), pipeline_mode=pl.Buffered(1))


TM = 512
N_TM = SEQ // TM
PA_TILE = 640
N_PA = PROJ_A_PAD // PA_TILE
OUT_TILE = 256


def _spec(shape, fn):
    return pl.BlockSpec(shape, fn)


def _act(shape=(SEQ, D_MODEL), dtype=BF16):
    return jax.ShapeDtypeStruct(shape, dtype)


def _proj_rows_nt(name, h, wt, n_tile):
    n = wt.shape[0]
    return _matmul(name, h, _resident((SEQ, D_MODEL)), wt, _spec((n_tile, D_MODEL), lambda j: (j, 0)), NT,
                   (n // n_tile,), _act((SEQ, n)), _spec((SEQ, n_tile), lambda j: (0, j)))


def _proj_cols_nn(name, h, w_blocks):
    nb, _, n = w_blocks.shape
    return _matmul(name, h, _resident((SEQ, D_MODEL)), w_blocks, _spec((None, D_MODEL, n), lambda j: (j, 0, 0)),
                   NN, (nb,), _act((SEQ, nb * n)), _spec((SEQ, n), lambda j: (0, j)))


def _square(name, a, w, dims, x=None):
    row = _spec((TM, D_MODEL), lambda i: (i, 0))
    return _matmul(name, a, row, w, _resident((D_MODEL, D_MODEL)), dims, (N_TM,),
                   _act(dtype=F32 if x is not None else BF16), row, res=x, res_spec=row if x is not None else None)


def _sum_blocks_nn(name, a_blocks, w_blocks, x=None, norm=None, swap=()):
    nb, _, n = a_blocks.shape
    row = _spec((TM, D_MODEL), lambda i: (i, 0))
    return _matmul(name, a_blocks, _spec((nb, TM, n), lambda i: (0, i, 0)), w_blocks, _resident((nb, n, D_MODEL)),
                   NN, (N_TM,), _act(dtype=F32), row, k_blocks=nb, res=x, res_spec=row if x is not None else None, norm=norm, swap=swap)


def _sum_cols_nt(name, d, w_blocks, norm=None, swap=()):
    nb, _, n = w_blocks.shape
    return _matmul(name, d, _spec((TM, nb * n), lambda i: (i, 0)), w_blocks, _resident((nb, D_MODEL, n)), NT,
                   (N_TM,), _act(dtype=F32), _spec((TM, D_MODEL), lambda i: (i, 0)), k_blocks=nb, a_block_cols=n, norm=norm, swap=swap)


def _wide_nn(name, d, wt, x=None, norm=None, swap=()):
    n = wt.shape[0]
    row = _spec((TM, D_MODEL), lambda i: (i, 0))
    return _matmul(name, d, _spec((TM, n), lambda i: (i, 0)), wt, _resident((n, D_MODEL)), NN, (N_TM,),
                   _act(dtype=F32), row, res=x, res_spec=row if x is not None else None, norm=norm, swap=swap)


def _wide_nt(name, d, w):
    n = w.shape[0]
    return _matmul(name, d, _spec((TM, D_MODEL), lambda i: (i, 0)), w, _resident((n, D_MODEL)), NT, (N_TM,),
                   _act((SEQ, n)), _spec((TM, n), lambda i: (i, 0)))


def _proj_halves_nt(name, h, wt):
    _, n, _ = wt.shape
    return _matmul(name, h, _spec((TM, D_MODEL), lambda p, i: (i, 0)), wt, _spec((None, n, D_MODEL), lambda p, i: (p, 0, 0)), NT,
                   (2, N_TM), _act((2, SEQ, n)), _spec((None, TM, n), lambda p, i: (p, i, 0)))


def _wgrad_halves_tn(name, d, n_tile, h):
    _, _, n = d.shape
    return _matmul(name, d, _spec((None, SEQ, n_tile), lambda p, j: (p, 0, j)), h, _resident((SEQ, D_MODEL)), TN,
                   (2, n // n_tile), _act((2, n, D_MODEL)), _spec((None, n_tile, D_MODEL), lambda p, j: (p, j, 0)))


def _wgrad_cols_tn(name, d, n_tile, h):
    n = d.shape[1]
    return _matmul(name, d, _spec((SEQ, n_tile), lambda j: (0, j)), h, _resident((SEQ, D_MODEL)), TN,
                   (n // n_tile,), _act((n, D_MODEL)), _spec((n_tile, D_MODEL), lambda j: (j, 0)))


def _wgrad_cols_transposed_tn(name, h, d, n_tile):
    nb = d.shape[1] // n_tile
    return _matmul(name, d, _spec((SEQ, n_tile), lambda j: (0, j)), h, _resident((SEQ, D_MODEL)), TN, (nb,),
                   _act((nb, D_MODEL, n_tile)), _spec((None, D_MODEL, n_tile), lambda j: (j, 0, 0)), transpose_out=True)


NORM_ROWS = 512


def _rstd(x):
    return lax.rsqrt(jnp.mean(x * x, axis=-1, keepdims=True) + RMS_EPS)


def _norm_fwd(name, x, gamma):
    def body(x_ref, g_ref, h_ref):
        x = x_ref[...]
        h_ref[...] = (x * _rstd(x) * g_ref[...]).astype(BF16)

    row = _spec((NORM_ROWS, D_MODEL), lambda i: (i, 0))
    return pl.pallas_call(
        body, name=name, grid=(SEQ // NORM_ROWS,), in_specs=[row, _spec((1, D_MODEL), lambda i: (0, 0))], out_specs=row,
        out_shape=jax.ShapeDtypeStruct((SEQ, D_MODEL), BF16), compiler_params=_params(("parallel",)),
    )(x, gamma)


def _norm_bwd_rows(x, gamma, dh):
    r = _rstd(x)
    xh = x * r
    dxh = dh * gamma
    dx = r * (dxh - xh * jnp.mean(dxh * xh, axis=-1, keepdims=True))
    return dx, jnp.sum(dh * xh, axis=0, keepdims=True)


def _loss_head(x, gamma, target):
    def body(x_ref, g_ref, t_ref, loss_ref, dx_ref, dx16_ref, dg_ref):
        x = x_ref[...]
        gamma = g_ref[...]
        err = x * _rstd(x) * gamma - t_ref[...]
        dy = err * (1.0 / D_MODEL)
        dx, dg = _norm_bwd_rows(x, gamma, dy)
        dx_ref[...] = dx
        dx16_ref[...] = dx.astype(BF16)
        part = 0.5 * jnp.sum(jnp.sum(err * err, axis=-1, keepdims=True) * (1.0 / D_MODEL), axis=0, keepdims=True)
        part = jnp.broadcast_to(part, loss_ref.shape)

        @pl.when(pl.program_id(0) == 0)
        def _():
            dg_ref[...] = dg
            loss_ref[...] = part

        @pl.when(pl.program_id(0) > 0)
        def _():
            dg_ref[...] += dg
            loss_ref[...] += part

    row = _spec((NORM_ROWS, D_MODEL), lambda i: (i, 0))
    vec = _spec((1, D_MODEL), lambda i: (0, 0))
    return pl.pallas_call(
        body, name="loss_head", grid=(SEQ // NORM_ROWS,), in_specs=[row, vec, row],
        out_specs=[_spec((1, 128), lambda i: (0, 0)), row, row, vec],
        out_shape=[jax.ShapeDtypeStruct((1, 128), F32), _act(dtype=F32), _act(), jax.ShapeDtypeStruct((1, D_MODEL), F32)],
        compiler_params=_params(("arbitrary",)),
    )(x, gamma, target)


def _sigmoid(x):
    return 1.0 / (1.0 + jnp.exp(-x))


def _rows(ref, c):
    return ref[pl.ds(pl.multiple_of(c * ROW_CHUNK, ROW_CHUNK), ROW_CHUNK), :].astype(F32)


def _rows_before(ref, c):
    start = pl.multiple_of(jnp.maximum(c * ROW_CHUNK - HALO, 0), HALO)
    rows = ref[pl.ds(start, HALO), :].astype(F32)
    return jnp.where(c > 0, rows, 0.0)


def _rows_after(ref, c, n_chunks):
    start = pl.multiple_of(jnp.minimum((c + 1) * ROW_CHUNK, SEQ - HALO), HALO)
    rows = ref[pl.ds(start, HALO), :].astype(F32)
    return jnp.where(c < n_chunks - 1, rows, 0.0)


def _shift_down(z, before, n):
    return pltpu.roll(jnp.concatenate([before, z], axis=0), n, 0)[HALO:]


def _shift_up(z, after, n):
    rows = z.shape[0]
    return pltpu.roll(jnp.concatenate([z, after], axis=0), rows + HALO - n, 0)[:rows]


def _conv_rows(z, before, w):
    z1 = _shift_down(z, before, 1)
    z2 = _shift_down(z, before, 2)
    return w[2:3, :] * z + w[1:2, :] * z1 + w[0:1, :] * z2, z1, z2


def _conv_t_rows(dy, after, w):
    return w[2:3, :] * dy + w[1:2, :] * _shift_up(dy, after, 1) + w[0:1, :] * _shift_up(dy, after, 2)


N_ROW_CHUNKS = SEQ // ROW_CHUNK


FF_COLS = 256
N_FF_COLS = D_FF // FF_COLS


def _ffn_mid_fwd(name, gu, conv_w):
    def body(gu_ref, w_ref, a_ref):
        w = w_ref[...]

        def chunk(c, carry):
            g = _rows(gu_ref.at[0], c)
            u = _rows(gu_ref.at[1], c)
            gc, _, _ = _conv_rows(g, _rows_before(gu_ref.at[0], c), w)
            a_ref[pl.ds(pl.multiple_of(c * ROW_CHUNK, ROW_CHUNK), ROW_CHUNK), :] = (gc * _sigmoid(gc) * u).astype(BF16)
            return carry

        lax.fori_loop(0, N_ROW_CHUNKS, chunk, 0)

    col = _spec((SEQ, FF_COLS), lambda j: (0, j))
    return pl.pallas_call(
        body, name=name, grid=(N_FF_COLS,),
        in_specs=[_spec((2, SEQ, FF_COLS), lambda j: (0, 0, j)), _spec((3, FF_COLS), lambda j: (0, j))], out_specs=col,
        out_shape=_act((SEQ, D_FF)), compiler_params=_params(("parallel",)),
    )(gu, conv_w)


def _ffn_mid_bwd(name, gu, conv_w, da):
    def body(gu_ref, w_ref, da_ref, dgu_ref, dw_ref, dgc_ref):
        w = w_ref[...]

        def first(c, acc):
            g = _rows(gu_ref.at[0], c)
            u = _rows(gu_ref.at[1], c)
            d = _rows(da_ref, c)
            gc, g1, g2 = _conv_rows(g, _rows_before(gu_ref.at[0], c), w)
            sg = _sigmoid(gc)
            rows = pl.ds(pl.multiple_of(c * ROW_CHUNK, ROW_CHUNK), ROW_CHUNK)
            dgu_ref[1, rows, :] = (d * gc * sg).astype(BF16)
            dgc = d * u * (sg * (1.0 + gc * (1.0 - sg)))
            dgc_ref[rows, :] = dgc
            return (acc[0] + jnp.sum(dgc * g2, axis=0, keepdims=True), acc[1] + jnp.sum(dgc * g1, axis=0, keepdims=True),
                    acc[2] + jnp.sum(dgc * g, axis=0, keepdims=True))

        zero = jnp.zeros((1, FF_COLS), F32)
        acc = lax.fori_loop(0, N_ROW_CHUNKS, first, (zero, zero, zero))
        for r in range(3):
            dw_ref[r:r + 1, :] = acc[r]

        def second(c, carry):
            dgc = _rows(dgc_ref, c)
            dg = _conv_t_rows(dgc, _rows_after(dgc_ref, c, N_ROW_CHUNKS), w)
            dgu_ref[0, pl.ds(pl.multiple_of(c * ROW_CHUNK, ROW_CHUNK), ROW_CHUNK), :] = dg.astype(BF16)
            return carry

        lax.fori_loop(0, N_ROW_CHUNKS, second, 0)

    pair = _spec((2, SEQ, FF_COLS), lambda j: (0, 0, j))
    wspec = _spec((3, FF_COLS), lambda j: (0, j))
    return pl.pallas_call(
        body, name=name, grid=(N_FF_COLS,), in_specs=[pair, wspec, _spec((SEQ, FF_COLS), lambda j: (0, j))],
        out_specs=[pair, wspec], out_shape=[_act((2, SEQ, D_FF)), jax.ShapeDtypeStruct((3, D_FF), F32)],
        scratch_shapes=[pltpu.VMEM((SEQ, FF_COLS), F32)],
        compiler_params=_params(("parallel",)),
    )(gu, conv_w, da)


SC_COLS = 256
N_SC = D_MODEL // SC_COLS


def _sc_specs():
    return [_spec((SEQ, SC_COLS), lambda j, part=part: (0, part * N_SC + j)) for part in range(3)]


def _sc_mid_fwd(p, conv_w):
    def body(b_ref, c_ref, h_ref, w_ref, y_ref):
        w = w_ref[...]

        def chunk(c, carry):
            z = _rows(c_ref, c) * _rows(h_ref, c)
            before = _rows_before(c_ref, c) * _rows_before(h_ref, c)
            zc, _, _ = _conv_rows(z, before, w)
            y_ref[pl.ds(pl.multiple_of(c * ROW_CHUNK, ROW_CHUNK), ROW_CHUNK), :] = (_rows(b_ref, c) * zc).astype(BF16)
            return carry

        lax.fori_loop(0, N_ROW_CHUNKS, chunk, 0)

    col = _spec((SEQ, SC_COLS), lambda j: (0, j))
    return pl.pallas_call(
        body, name="sc_mid_fwd", grid=(N_SC,), in_specs=_sc_specs() + [_spec((3, SC_COLS), lambda j: (0, j))], out_specs=col,
        out_shape=jax.ShapeDtypeStruct((SEQ, D_MODEL), BF16), compiler_params=_params(("parallel",)),
    )(p, p, p, conv_w)


def _sc_mid_bwd(p, conv_w, dy):
    def body(b_ref, c_ref, h_ref, w_ref, dy_ref, db_ref, dc_ref, dh_ref, dw_ref, dzc_ref):
        w = w_ref[...]

        def first(c, acc):
            z = _rows(c_ref, c) * _rows(h_ref, c)
            before = _rows_before(c_ref, c) * _rows_before(h_ref, c)
            zc, z1, z2 = _conv_rows(z, before, w)
            d = _rows(dy_ref, c)
            rows = pl.ds(pl.multiple_of(c * ROW_CHUNK, ROW_CHUNK), ROW_CHUNK)
            db_ref[rows, :] = (d * zc).astype(BF16)
            dzc = d * _rows(b_ref, c)
            dzc_ref[rows, :] = dzc
            return (acc[0] + jnp.sum(dzc * z2, axis=0, keepdims=True), acc[1] + jnp.sum(dzc * z1, axis=0, keepdims=True),
                    acc[2] + jnp.sum(dzc * z, axis=0, keepdims=True))

        zero = jnp.zeros((1, SC_COLS), F32)
        acc = lax.fori_loop(0, N_ROW_CHUNKS, first, (zero, zero, zero))
        for r in range(3):
            dw_ref[r:r + 1, :] = acc[r]

        def second(c, carry):
            dz = _conv_t_rows(_rows(dzc_ref, c), _rows_after(dzc_ref, c, N_ROW_CHUNKS), w)
            rows = pl.ds(pl.multiple_of(c * ROW_CHUNK, ROW_CHUNK), ROW_CHUNK)
            dc_ref[rows, :] = (dz * _rows(h_ref, c)).astype(BF16)
            dh_ref[rows, :] = (dz * _rows(c_ref, c)).astype(BF16)
            return carry

        lax.fori_loop(0, N_ROW_CHUNKS, second, 0)

    col = _spec((SEQ, SC_COLS), lambda j: (0, j))
    wspec = _spec((3, SC_COLS), lambda j: (0, j))
    act = jax.ShapeDtypeStruct((SEQ, D_MODEL), BF16)
    return pl.pallas_call(
        body, name="sc_mid_bwd", grid=(N_SC,), in_specs=_sc_specs() + [wspec, col], out_specs=[col, col, col, wspec],
        out_shape=[act, act, act, jax.ShapeDtypeStruct((3, D_MODEL), F32)],
        scratch_shapes=[pltpu.VMEM((SEQ, SC_COLS), F32)], compiler_params=_params(("parallel",)),
    )(p, p, p, conv_w, dy)


GLA_GROUP = 4
GLA_ROWS = GLA_GROUP * CHUNK
N_GROUPS = N_CHUNKS // GLA_GROUP
Q0, K0, V0, R0, G0 = 0, KEY_DIM, 2 * KEY_DIM, 2 * KEY_DIM + VALUE_DIM, 2 * KEY_DIM + 2 * VALUE_DIM


def _tri(strict):
    r = lax.broadcasted_iota(jnp.int32, (CHUNK, CHUNK), 0)
    c = lax.broadcasted_iota(jnp.int32, (CHUNK, CHUNK), 1)
    return jnp.where(c < r if strict else c <= r, 1.0, 0.0).astype(F32)


def _cumsum_rows(tri, x):
    return jnp.dot(tri, x, preferred_element_type=F32, precision=lax.Precision.HIGHEST)


def _gate_logits(gl, wgu, b_gate):
    return jnp.dot(gl, wgu, preferred_element_type=F32) + b_gate


def _log_decay(logits):
    return (jnp.minimum(logits, 0.0) - jnp.log(1.0 + jnp.exp(-jnp.abs(logits)))) * (1.0 / GATE_NORMALIZER)


def _head(x, h, width):
    return x[:, h * width:(h + 1) * width]


def _gla_fwd(proj, wgu, b_gate, gn):
    def body(p_ref, wgu_ref, b_ref, gn_ref, o_ref, og_ref, st_ref, state):
        @pl.when(pl.program_id(0) == 0)
        def _():
            state[...] = jnp.zeros_like(state)

        tri = _tri(False)
        la = _log_decay(_gate_logits(p_ref[:, G0:G0 + GATE_PAD], wgu_ref[...], b_ref[...]))
        for c in range(GLA_GROUP):
            rows = slice(c * CHUNK, (c + 1) * CHUNK)
            cum = _cumsum_rows(tri, la[rows])
            tot = cum[CHUNK - 1:CHUNK, :]
            kd = (p_ref[rows, K0:K0 + KEY_DIM].astype(F32) * jnp.exp(tot - cum)).astype(BF16)
            decay = jnp.exp(tot)
            q = (p_ref[rows, Q0:Q0 + KEY_DIM].astype(F32) * (HEAD_K ** -0.5)).astype(BF16)
            v = p_ref[rows, V0:V0 + VALUE_DIM]
            for h in range(GLA_HEADS):
                upd = lax.dot_general(_head(v, h, HEAD_V), _head(kd, h, HEAD_K), (TN, ((), ())), preferred_element_type=F32)
                s = state[h] * _head(decay, h, HEAD_K) + upd
                state[h] = s
                st_ref[c, h] = s
                o_ref[rows, h * HEAD_V:(h + 1) * HEAD_V] = lax.dot_general(
                    _head(q, h, HEAD_K), s.astype(BF16), (NT, ((), ())), preferred_element_type=F32)
        r = p_ref[:, R0:R0 + VALUE_DIM].astype(F32)
        gate = r * _sigmoid(r) * gn_ref[...]
        for h in range(GLA_HEADS):
            cols = slice(h * HEAD_V, (h + 1) * HEAD_V)
            o = o_ref[:, cols]
            og_ref[:, cols] = (o * _rstd(o) * gate[:, cols]).astype(BF16)

    rows = _spec((GLA_ROWS, VALUE_DIM), lambda i: (i, 0))
    const = lambda shape: _spec(shape, lambda i: (0,) * len(shape))
    return pl.pallas_call(
        body, name="gla_fwd", grid=(N_GROUPS,),
        in_specs=[_spec((GLA_ROWS, PROJ_A_PAD), lambda i: (i, 0)), const((GATE_PAD, KEY_DIM)), const((1, KEY_DIM)),
                  const((1, VALUE_DIM))],
        out_specs=[rows, rows, _spec((GLA_GROUP, GLA_HEADS, HEAD_V, HEAD_K), lambda i: (i, 0, 0, 0))],
        out_shape=[jax.ShapeDtypeStruct((SEQ, VALUE_DIM), F32), jax.ShapeDtypeStruct((SEQ, VALUE_DIM), BF16),
                   jax.ShapeDtypeStruct((N_CHUNKS, GLA_HEADS, HEAD_V, HEAD_K), F32)],
        scratch_shapes=[pltpu.VMEM((GLA_HEADS, HEAD_V, HEAD_K), F32)], compiler_params=_params(("arbitrary",)),
    )(proj, wgu, b_gate, gn)


def _gla_bwd(proj, wgu, b_gate, gn, o, states, dog):
    last = N_GROUPS - 1

    def body(p_ref, wgu_ref, b_ref, gn_ref, o_ref, st_ref, stp_ref, dog_ref, dp_ref, dwgu_ref, db_ref, dgn_ref, carry, do_buf):
        step = pl.program_id(0)

        @pl.when(step == 0)
        def _():
            carry[...] = jnp.zeros_like(carry)

        r = p_ref[:, R0:R0 + VALUE_DIM].astype(F32)
        sr = _sigmoid(r)
        silu = r * sr
        gn_row = gn_ref[...]
        dog_rows = dog_ref[...].astype(F32)
        dn = dog_rows * silu
        dgn_cols = []
        for h in range(GLA_HEADS):
            cols = slice(h * HEAD_V, (h + 1) * HEAD_V)
            oh = o_ref[:, cols]
            rs = _rstd(oh)
            ohat = oh * rs
            dn_h = dn[:, cols]
            dgn_cols.append(jnp.sum(dn_h * ohat, axis=0, keepdims=True))
            dohat = dn_h * gn_row[:, cols]
            do_buf[:, cols] = rs * (dohat - ohat * jnp.mean(dohat * ohat, axis=-1, keepdims=True))
            n_h = ohat * gn_row[:, cols]
            dp_ref[:, R0 + h * HEAD_V:R0 + (h + 1) * HEAD_V] = (
                dog_rows[:, cols] * n_h * (sr[:, cols] * (1.0 + r[:, cols] * (1.0 - sr[:, cols])))).astype(BF16)
        dgn = jnp.concatenate(dgn_cols, axis=1)

        tri = _tri(False)
        tri_strict = _tri(True)
        gl = p_ref[:, G0:G0 + GATE_PAD]
        logits = _gate_logits(gl, wgu_ref[...], b_ref[...])
        la = _log_decay(logits)
        dlogit_rows = []
        for c in reversed(range(GLA_GROUP)):
            rows = slice(c * CHUNK, (c + 1) * CHUNK)
            cum = _cumsum_rows(tri, la[rows])
            tot = cum[CHUNK - 1:CHUNK, :]
            fade = jnp.exp(tot - cum)
            k = p_ref[rows, K0:K0 + KEY_DIM].astype(F32)
            kd32 = k * fade
            kd = kd32.astype(BF16)
            decay = jnp.exp(tot)
            q = (p_ref[rows, Q0:Q0 + KEY_DIM].astype(F32) * (HEAD_K ** -0.5)).astype(BF16)
            v = p_ref[rows, V0:V0 + VALUE_DIM]
            do = do_buf[rows, :].astype(BF16)
            dkd_cols, ddecay_cols = [], []
            for h in range(GLA_HEADS):
                do_h = _head(do, h, HEAD_V)
                s = st_ref[c, h]
                dq = jnp.dot(do_h, s.astype(BF16), preferred_element_type=F32) * (HEAD_K ** -0.5)
                dp_ref[rows, Q0 + h * HEAD_K:Q0 + (h + 1) * HEAD_K] = dq.astype(BF16)
                g = carry[h] + lax.dot_general(do_h, _head(q, h, HEAD_K), (TN, ((), ())), preferred_element_type=F32)
                g16 = g.astype(BF16)
                dkd_cols.append(jnp.dot(_head(v, h, HEAD_V), g16, preferred_element_type=F32))
                dv = lax.dot_general(_head(kd, h, HEAD_K), g16, (NT, ((), ())), preferred_element_type=F32)
                dp_ref[rows, V0 + h * HEAD_V:V0 + (h + 1) * HEAD_V] = dv.astype(BF16)
                if c > 0:
                    s_prev = st_ref[c - 1, h]
                else:
                    s_prev = jnp.where(step < last, stp_ref[0, h], 0.0)
                ddecay_cols.append(jnp.sum(g * s_prev, axis=0, keepdims=True))
                carry[h] = g * _head(decay, h, HEAD_K)
            dkd = jnp.concatenate(dkd_cols, axis=1)
            ddecay = jnp.concatenate(ddecay_cols, axis=1)
            dp_ref[rows, K0:K0 + KEY_DIM] = (dkd * fade).astype(BF16)
            e = dkd * kd32
            dla = ddecay * decay + _cumsum_rows(tri_strict, e)
            dlogit_rows.append(dla * (1.0 / GATE_NORMALIZER) * (1.0 - _sigmoid(logits[rows])))
        dlogit = jnp.concatenate(dlogit_rows[::-1], axis=0)
        dlogit16 = dlogit.astype(BF16)
        dp_ref[:, G0:G0 + GATE_PAD] = lax.dot_general(
            dlogit16, wgu_ref[...], (NT, ((), ())), preferred_element_type=F32).astype(BF16)
        dwgu = lax.dot_general(gl, dlogit16, (TN, ((), ())), preferred_element_type=F32)
        db = jnp.sum(dlogit, axis=0, keepdims=True)

        @pl.when(step == 0)
        def _():
            dwgu_ref[...] = dwgu
            db_ref[...] = db
            dgn_ref[...] = dgn

        @pl.when(step > 0)
        def _():
            dwgu_ref[...] += dwgu
            db_ref[...] += db
            dgn_ref[...] += dgn

    rev = lambda i: (last - i, 0)
    rows = _spec((GLA_ROWS, VALUE_DIM), rev)
    const = lambda shape: _spec(shape, lambda i: (0,) * len(shape))
    st_shape = (GLA_HEADS, HEAD_V, HEAD_K)
    return pl.pallas_call(
        body, name="gla_bwd", grid=(N_GROUPS,),
        in_specs=[_spec((GLA_ROWS, PROJ_A_PAD), rev), const((GATE_PAD, KEY_DIM)), const((1, KEY_DIM)), const((1, VALUE_DIM)),
                  rows, _spec((GLA_GROUP,) + st_shape, lambda i: (last - i, 0, 0, 0)),
                  _spec((1,) + st_shape, lambda i: (jnp.maximum((last - i) * GLA_GROUP - 1, 0), 0, 0, 0)), rows],
        out_specs=[_spec((GLA_ROWS, PROJ_A_PAD), rev), const((GATE_PAD, KEY_DIM)), const((1, KEY_DIM)), const((1, VALUE_DIM))],
        out_shape=[jax.ShapeDtypeStruct((SEQ, PROJ_A_PAD), BF16), jax.ShapeDtypeStruct((GATE_PAD, KEY_DIM), F32),
                   jax.ShapeDtypeStruct((1, KEY_DIM), F32), jax.ShapeDtypeStruct((1, VALUE_DIM), F32)],
        scratch_shapes=[pltpu.VMEM(st_shape, F32), pltpu.VMEM((GLA_ROWS, VALUE_DIM), F32)],
        compiler_params=_params(("arbitrary",)),
    )(proj, wgu, b_gate, gn, o, states, states, dog)


WGRAD_FF_TILE = D_FF // 2


def _ffn_fwd(tag, x, gamma, w_up_t, conv_w, w_down):
    h = _norm_fwd(f"ffn{tag}_norm", x, gamma)
    gu = _proj_halves_nt(f"ffn{tag}_up", h, w_up_t)
    a = _ffn_mid_fwd(f"ffn{tag}_mid", gu, conv_w)
    return _wide_nn(f"ffn{tag}_down", a, w_down, x=x), (h, gu, a)


def _owner_blocks(d, rows=None):
    if rows is not None:
        d = d[:rows]
    return d.reshape((N_DEV, -1) + d.shape[-1:])


def _ffn_bwd(tag, x, gamma, w_up_t, conv_w, w_down, saved, dx, dx16, swap):
    h, gu, a = saved
    da = _wide_nt(f"ffn{tag}_da", dx16, w_down)
    d_w_down = _owner_blocks(_wgrad_cols_tn(f"ffn{tag}_dwdown", a, WGRAD_FF_TILE, dx16))
    dgu, d_conv = _ffn_mid_bwd(f"ffn{tag}_mid_bwd", gu, conv_w, da)
    d_w_up_t = _owner_blocks(_wgrad_halves_tn(f"ffn{tag}_dwup", dgu, WGRAD_FF_TILE, h))
    parts = (d_w_up_t, d_w_down)
    dx, dx16, d_gamma, *received = _sum_blocks_nn(
        f"ffn{tag}_dh", dgu, w_up_t, norm=(x, gamma, dx), swap=parts if swap else ())
    return dx, dx16, d_gamma, d_conv, parts, received


def _local_step(x, target, w, fetch=None, emit=None):
    if fetch is None:
        local = dict(a=(w.get("a_w_in"), w.get("a_w_out")), b=(w.get("b_w_in"), w.get("b_w_out")))
        for layer in range(2):
            local[f"f{layer}"] = (w["f_w_up"][layer], w["f_w_down"][layer]) if "f_w_up" in w else None
        fetch = lambda group, after: local[group]
    swap = emit is not None
    if emit is None:
        emit = lambda group, parts, received, dx: dx
    f_norm = (w["f_norm"][0:1], w["f_norm"][1:2])

    x0 = x
    a_w_in, a_w_out = fetch("a", x0)
    h0 = _norm_fwd("a_norm", x0, w["a_norm"])
    proj = _proj_rows_nt("a_in", h0, a_w_in, PA_TILE)
    o, og, states = _gla_fwd(proj, w["a_w_gate_up"], w["a_b_gate"], w["a_gn"])
    x1 = _square("a_out", og, a_w_out, NN, x0)
    up0, down0 = fetch("f0", x1)
    x2, ffn0 = _ffn_fwd(0, x1, f_norm[0], up0, w["f_conv"][0], down0)
    b_w_in, b_w_out = fetch("b", x2)
    h2 = _norm_fwd("b_norm", x2, w["b_norm"])
    p = _proj_cols_nn("b_in", h2, b_w_in)
    y = _sc_mid_fwd(p, w["b_conv"])
    x3 = _square("b_out", y, b_w_out, NN, x2)
    up1, down1 = fetch("f1", x3)
    x4, ffn1 = _ffn_fwd(1, x3, f_norm[1], up1, w["f_conv"][1], down1)
    loss, dx, dx16, d_final_norm = _loss_head(x4, w["final_norm"], target)

    dx, dx16, d_f_norm1, d_fconv1, parts_f1, got = _ffn_bwd(
        1, x3, f_norm[1], up1, w["f_conv"][1], down1, ffn1, dx, dx16, swap)
    dx16 = emit("f1", parts_f1, got, dx16)

    dy = _square("b_dy", dx16, b_w_out, NT)
    d_b_w_out = _owner_blocks(_wgrad_cols_tn("b_dwout", y, OUT_TILE, dx16))
    db, dc, dhh, d_b_conv = _sc_mid_bwd(p, w["b_conv"], dy)
    dp = jnp.concatenate([db, dc, dhh], axis=1)
    parts_b = (_wgrad_cols_transposed_tn("b_dwin", h2, dp, B_SHARD), d_b_w_out)
    dx, dx16, d_b_norm, *got = _sum_cols_nt("b_dh", dp, b_w_in, norm=(x2, w["b_norm"], dx), swap=parts_b if swap else ())
    dx16 = emit("b", parts_b, got, dx16)

    dx, dx16, d_f_norm0, d_fconv0, parts_f0, got = _ffn_bwd(
        0, x1, f_norm[0], up0, w["f_conv"][0], down0, ffn0, dx, dx16, swap)
    dx16 = emit("f0", parts_f0, got, dx16)

    dog = _square("a_dog", dx16, a_w_out, NT)
    d_a_w_out = _owner_blocks(_wgrad_cols_tn("a_dwout", og, OUT_TILE, dx16))
    dproj, d_wgu, d_b_gate, d_gn = _gla_bwd(proj, w["a_w_gate_up"], w["a_b_gate"], w["a_gn"], o, states, dog)
    parts_a = (_owner_blocks(_wgrad_cols_tn("a_dwin", dproj, PA_TILE, h0), PROJ_A), d_a_w_out)
    dx, _, d_a_norm, *got = _wide_nn("a_dh", dproj, a_w_in, norm=(x0, w["a_norm"], dx), swap=parts_a if swap else ())
    emit("a", parts_a, got, dx)

    grads = dict(
        a_norm=d_a_norm, a_w_in=parts_a[0], a_w_gate_up=d_wgu, a_b_gate=d_b_gate, a_gn=d_gn, a_w_out=parts_a[1],
        b_norm=d_b_norm, b_w_in=parts_b[0], b_conv=d_b_conv, b_w_out=parts_b[1],
        f_norm=(d_f_norm0, d_f_norm1), f_w_up=(parts_f0[0], parts_f1[0]), f_conv=(d_fconv0, d_fconv1),
        f_w_down=(parts_f0[1], parts_f1[1]), final_norm=d_final_norm)
    return loss[0, 0], dx, grads


MESH_ID = pl.DeviceIdType.MESH
ANY = pl.BlockSpec(memory_space=pl.ANY)
N_PEERS = N_DEV - 1


def _position():
    return lax.axis_index("x"), lax.axis_index("y"), lax.axis_index("c")


def _slot(px, py, pc):
    return 4 * px + 2 * py + pc


def _all_gather(name, shards):
    n = len(shards)

    def body(*refs):
        ins, outs = refs[:n], refs[n:2 * n]
        send_sems, recv_sems, local_sems = refs[2 * n:]
        x, y, c = _position()
        me, sibling = (x, y, c), (x, y, 1 - c)
        chips = [(1 - x, y), (x, 1 - y), (1 - x, 1 - y)]

        def copy(t, k, block, to, from_input=False):
            dst = outs[t].at[_slot(*block)]
            return pltpu.make_async_remote_copy(
                src_ref=ins[t] if from_input else dst, dst_ref=dst, send_sem=send_sems.at[t, k], recv_sem=recv_sems.at[t, k],
                device_id=to, device_id_type=MESH_ID)

        mine = [pltpu.make_async_copy(ins[t], outs[t].at[_slot(*me)], local_sems.at[t]) for t in range(n)]
        for cp in mine:
            cp.start()
        first = []
        for t in range(n):
            first.append(copy(t, 0, me, sibling, True))
            first += [copy(t, 1 + j, me, (*chip, c), True) for j, chip in enumerate(chips)]
        for cp in first:
            cp.start()
        passed = []
        for t in range(n):
            for j, chip in enumerate(chips):
                copy(t, 1 + j, (*chip, c), me).wait_recv()
                fwd = copy(t, 4 + j, (*chip, c), sibling)
                fwd.start()
                passed.append(fwd)
        for t in range(n):
            copy(t, 0, sibling, me).wait_recv()
            for j, chip in enumerate(chips):
                copy(t, 4 + j, (*chip, 1 - c), me).wait_recv()
        for cp in first + passed:
            cp.wait_send()
        for cp in mine:
            cp.wait()

    return pl.pallas_call(
        body, name=name, in_specs=[ANY] * n, out_specs=[ANY] * n,
        out_shape=[jax.ShapeDtypeStruct((N_DEV,) + s.shape, s.dtype) for s in shards],
        scratch_shapes=[pltpu.SemaphoreType.DMA((n, N_PEERS)), pltpu.SemaphoreType.DMA((n, N_PEERS)), pltpu.SemaphoreType.DMA((n,))],
    )(*shards)


SIBLING_AND_SAME_CORE = (1, 2, 4, 6)
SAME_CORE = (2, 4, 6)


def _flip(x, y, c, k):
    return x ^ (k >> 2), y ^ ((k >> 1) & 1), c ^ (k & 1)


N_CHIPS = N_DEV // 2


def _chip(px, py):
    return 2 * px + py


def _pair_copies(parts, received, send_sems, recv_sems):
    x, y, c = lax.axis_index("x"), lax.axis_index("y"), lax.axis_index("c")
    sibling = (x, y, 1 - c)
    copies = []
    for t in range(len(parts)):
        for q in range(N_DEV // 2):
            send = pltpu.make_async_remote_copy(
                src_ref=parts[t].at[2 * q + 1 - c], dst_ref=received[t].at[q], send_sem=send_sems.at[t, q],
                recv_sem=recv_sems.at[t, q], device_id=sibling, device_id_type=pl.DeviceIdType.MESH)
            landed = received[t].at[q]
            arrival = pltpu.make_async_remote_copy(
                src_ref=landed, dst_ref=landed, send_sem=send_sems.at[t, q], recv_sem=recv_sems.at[t, q],
                device_id=sibling, device_id_type=pl.DeviceIdType.MESH)
            copies.append((send, arrival))
    return copies


PAIR_ROWS = 1024


def _pair_add(name, part, received, side):
    _, rows, cols = part.shape
    tiles = [t for t in range(PAIR_ROWS, 0, -BF16_ROWS) if rows % t == 0]
    tr = tiles[0] if tiles else rows

    def body(side_ref, p_ref, r_ref, o_ref):
        o_ref[...] = (p_ref[...].astype(F32) + r_ref[...].astype(F32)).astype(BF16)

    tile = _spec((None, tr, cols), lambda q, i, side_ref: (q, i, 0))
    return pl.pallas_call(
        body, name=name,
        grid_spec=pltpu.PrefetchScalarGridSpec(
            num_scalar_prefetch=1, grid=(N_CHIPS, rows // tr),
            in_specs=[_spec((None, tr, cols), lambda q, i, side_ref: (2 * q + side_ref[0], i, 0)), tile], out_specs=tile),
        out_shape=jax.ShapeDtypeStruct((N_CHIPS, rows, cols), BF16), compiler_params=_params(("parallel", "parallel")),
    )(side, part, received)


def _send_copy(parts, landing, send_sems, recv_sems, t, s, k):
    x, y, c = _position()
    px, py, _ = _flip(x, y, c, k)
    return pltpu.make_async_remote_copy(
        src_ref=parts[t].at[_chip(px, py)], dst_ref=landing[t].at[_chip(x, y)], send_sem=send_sems.at[s],
        recv_sem=recv_sems.at[s], device_id=(px, py, c), device_id_type=MESH_ID)


def _send_arrival(landing, send_sems, recv_sems, t, s, k):
    x, y, c = _position()
    px, py, _ = _flip(x, y, c, k)
    landed = landing[t].at[_chip(px, py)]
    return pltpu.make_async_remote_copy(
        src_ref=landed, dst_ref=landed, send_sem=send_sems.at[s], recv_sem=recv_sems.at[s],
        device_id=(px, py, c), device_id_type=MESH_ID)


def _handshake(peers):
    x, y, c = _position()
    barrier = pltpu.get_barrier_semaphore()
    for k in peers:
        pl.semaphore_signal(barrier, inc=1, device_id=_flip(x, y, c, k), device_id_type=MESH_ID)
    pl.semaphore_wait(barrier, len(peers))


def _sequencer(name, collective_id, n_copies, body, operands, out_type):
    n_arrays = len(operands)
    return pl.kernel(
        body, out_type=out_type, mesh=plsc.ScalarSubcoreMesh(axis_name="sequencer", num_cores=1), name=name,
        scratch_types=(pltpu.SemaphoreType.DMA((n_copies,)), pltpu.SemaphoreType.DMA((n_copies,)),
                       pltpu.SemaphoreType.DMA((n_arrays,))),
        compiler_params=pltpu.CompilerParams(collective_id=collective_id))(*operands)


def _sequencer_exchange(name, collective_id, parts, after=()):
    n, n_peers, n_in = len(parts), len(SAME_CORE), len(parts) + len(after)

    def body(*refs):
        src, landing = refs[:n], refs[n_in:n_in + n]
        send_sems, recv_sems, local_sems = refs[n_in + n:]
        _handshake(SAME_CORE)
        x, y, _ = _position()
        mine = [pltpu.make_async_copy(src[t].at[_chip(x, y)], landing[t].at[_chip(x, y)], local_sems.at[t]) for t in range(n)]
        for cp in mine:
            cp.start()
        sent = [_send_copy(src, landing, send_sems, recv_sems, t, t * n_peers + j, k)
                for t in range(n) for j, k in enumerate(SAME_CORE)]
        for cp in sent:
            cp.start()
        for t in range(n):
            for j, k in enumerate(SAME_CORE):
                _send_arrival(landing, send_sems, recv_sems, t, t * n_peers + j, k).wait_recv()
        for cp in sent:
            cp.wait_send()
        for cp in mine:
            cp.wait()

    landing = [jax.ShapeDtypeStruct(p.shape, p.dtype) for p in parts]
    return _sequencer(name, collective_id, n * n_peers, body, list(parts) + list(after), landing)


def _sequencer_gather(name, collective_id, shards):
    n, per = len(shards), N_PEERS

    def body(*refs):
        src, out = refs[:n], refs[n:2 * n]
        send_sems, recv_sems, local_sems = refs[2 * n:]
        _handshake(SIBLING_AND_SAME_CORE)
        x, y, c = _position()
        me, sibling = (x, y, c), (x, y, 1 - c)

        def copy(t, j, block, to, from_input=False):
            dst = out[t].at[_slot(*block)]
            return pltpu.make_async_remote_copy(
                src_ref=src[t] if from_input else dst, dst_ref=dst, send_sem=send_sems.at[t * per + j],
                recv_sem=recv_sems.at[t * per + j], device_id=to, device_id_type=MESH_ID)

        mine = [pltpu.make_async_copy(src[t], out[t].at[_slot(*me)], local_sems.at[t]) for t in range(n)]
        for cp in mine:
            cp.start()
        sent = [copy(t, j, me, _flip(x, y, c, k), True) for t in range(n) for j, k in enumerate(SIBLING_AND_SAME_CORE)]
        for cp in sent:
            cp.start()
        for t in range(n):
            for j, k in enumerate(SAME_CORE):
                block = _flip(x, y, c, k)
                copy(t, 1 + j, block, me).wait_recv()
                forward = copy(t, 4 + j, block, sibling)
                forward.start()
                sent.append(forward)
        for t in range(n):
            copy(t, 0, sibling, me).wait_recv()
            for j, k in enumerate(SAME_CORE):
                copy(t, 4 + j, _flip(x, y, 1 - c, k), me).wait_recv()
        for cp in sent:
            cp.wait_send()
        for cp in mine:
            cp.wait()

    gathered = [jax.ShapeDtypeStruct((N_DEV,) + s.shape, s.dtype) for s in shards]
    return _sequencer(name, collective_id, n * per, body, shards, gathered)


ADAM_ROWS = 512
BF16_ROWS = 16


def _adam_update(w, g, m, v):
    m = ADAM_B1 * m + (1.0 - ADAM_B1) * g
    v = ADAM_B2 * v + (1.0 - ADAM_B2) * (g * g)
    m_hat = m / (1.0 - ADAM_B1 ** ADAM_STEP)
    v_hat = v / (1.0 - ADAM_B2 ** ADAM_STEP)
    delta = -ADAM_LR * (m_hat / (jnp.sqrt(v_hat) + ADAM_EPS) + ADAM_WD * w)
    return delta, m, v


def _sum_slots(ref):
    total = ref[0].astype(F32)
    for d in range(1, ref.shape[0]):
        total = total + ref[d].astype(F32)
    return total


def _adamw_sum(name, landed, w, m, v):
    layers, rows, cols = w.shape
    tiles = [t for t in range(ADAM_ROWS, 0, -BF16_ROWS) if rows % t == 0]
    tr = tiles[0] if tiles else rows
    nt = rows // tr

    def body(*refs):
        parts = refs[:layers]
        w_ref, m_ref, v_ref, g_ref, d_ref, nm_ref, nv_ref = refs[layers:]
        layer = pl.program_id(0)
        g = _sum_slots(parts[0])
        for q in range(1, layers):
            g = jnp.where(layer == q, _sum_slots(parts[q]), g)
        delta, new_m, new_v = _adam_update(w_ref[...], g, m_ref[...], v_ref[...])
        g_ref[...] = g
        d_ref[...] = delta
        nm_ref[...] = new_m
        nv_ref[...] = new_v

    def part_spec(q):
        return _spec((N_CHIPS, tr, cols), lambda l, i: (0, jnp.where(l == q, i, jnp.where(l < q, 0, nt - 1)), 0))

    tile = _spec((None, tr, cols), lambda l, i: (l, i, 0))
    out = jax.ShapeDtypeStruct((layers, rows, cols), F32)
    return pl.pallas_call(
        body, name=name, grid=(layers, nt), in_specs=[part_spec(q) for q in range(layers)] + [tile] * 3,
        out_specs=[tile] * 4, out_shape=[out] * 4, compiler_params=_params(("arbitrary", "arbitrary")),
    )(*landed, w, m, v)


def _sum_small(landed):
    def body(in_ref, out_ref):
        out_ref[...] = _sum_slots(in_ref)

    return pl.pallas_call(body, name="small_grad_sum", out_shape=jax.ShapeDtypeStruct(landed.shape[1:], F32))(landed)


def _adamw_small(arrays):
    n = len(arrays)

    def body(*refs):
        for i in range(n):
            g_ref, w_ref, m_ref, v_ref = refs[4 * i:4 * i + 4]
            d_ref, nm_ref, nv_ref = refs[4 * n + 3 * i:4 * n + 3 * i + 3]
            d_ref[...], nm_ref[...], nv_ref[...] = _adam_update(w_ref[...], g_ref[...], m_ref[...], v_ref[...])

    out = [jax.ShapeDtypeStruct(w.shape, F32) for _, w, _, _ in arrays for _ in range(3)]
    flat = pl.pallas_call(body, name="adam_small", out_shape=out)(*[a for group in arrays for a in group])
    return [tuple(flat[3 * i:3 * i + 3]) for i in range(n)]


LANES = 128
SUBLANES = 8
F_CONV_SHARD = D_FF // N_DEV
GATE_SHARD = KEY_DIM // N_DEV
NORM_SHARD = D_MODEL // N_DEV


def _tile_rows(a):
    flat = a.reshape(-1)
    size = -(-flat.shape[0] // (SUBLANES * LANES)) * SUBLANES * LANES
    return jnp.pad(flat, (0, size - flat.shape[0])).reshape(-1, LANES)


def _pack_rows(pieces):
    return jnp.concatenate([_tile_rows(p) for p in pieces], axis=0)


def _unpack_rows(packed, shapes):
    out, row = [], 0
    for shape in shapes:
        size = 1
        for s in shape:
            size *= s
        rows = -(-size // (SUBLANES * LANES)) * SUBLANES
        piece = packed[..., row:row + rows, :]
        out.append(piece.reshape(piece.shape[:-2] + (rows * LANES,))[..., :size])
        row += rows
    return out


SMALL_SHARDS = ((GATE_RANK, GATE_SHARD), (1, NORM_SHARD), (3, NORM_SHARD), (2, 3, F_CONV_SHARD))


def _unpack_small_shards(g):
    gate, b_norm, b_conv, f_conv = _unpack_rows(g, SMALL_SHARDS)
    gate = gate.reshape(N_DEV, GATE_RANK, GATE_SHARD).transpose(1, 0, 2).reshape(GATE_RANK, KEY_DIM)
    b_norm = b_norm.reshape(1, D_MODEL)
    b_conv = b_conv.reshape(N_DEV, 3, NORM_SHARD).transpose(1, 0, 2).reshape(3, D_MODEL)
    f_conv = f_conv.reshape(N_DEV, 2, 3, F_CONV_SHARD).transpose(1, 2, 0, 3).reshape(2, 3, D_FF)
    return gate, b_norm, b_conv, f_conv


SMALL_LAYOUT = (("a_norm", (1, D_MODEL)), ("a_w_gate_up", (GATE_RANK, KEY_DIM)), ("a_b_gate", (1, KEY_DIM)), ("a_gn", (1, VALUE_DIM)),
                ("b_norm", (1, D_MODEL)), ("b_conv", (3, D_MODEL)), ("f_norm0", (1, D_MODEL)), ("f_norm1", (1, D_MODEL)),
                ("f_conv0", (3, D_FF)), ("f_conv1", (3, D_FF)), ("final_norm", (1, D_MODEL)))


def _pack_small_grads(g):
    full = dict(g)
    full["a_w_gate_up"] = g["a_w_gate_up"][:GATE_RANK]
    for layer in range(2):
        full[f"f_norm{layer}"] = g["f_norm"][layer]
        full[f"f_conv{layer}"] = g["f_conv"][layer]
    return _pack_rows([full[name] for name, _ in SMALL_LAYOUT])


def _unpack_small_grads(packed):
    pieces = _unpack_rows(packed, [shape for _, shape in SMALL_LAYOUT])
    out = {name: piece.reshape(shape) for (name, shape), piece in zip(SMALL_LAYOUT, pieces)}
    out["f_norm"] = jnp.stack([out["f_norm0"][0], out["f_norm1"][0]])
    out["f_conv"] = jnp.stack([out["f_conv0"], out["f_conv1"]])
    return out


def kernel(x, a_norm, a_w_in, a_w_gate_up, a_b_gate, a_gn, a_w_out, b_norm, b_w_in, b_conv, b_w_out, f_norm, f_w_up, f_conv, f_w_down, final_norm, loss_target, m_a_norm, m_a_w_in, m_a_w_gate_up, m_a_b_gate, m_a_gn, m_a_w_out, m_b_norm, m_b_w_in, m_b_conv, m_b_w_out, m_f_norm, m_f_w_up, m_f_conv, m_f_w_down, m_final_norm, v_a_norm, v_a_w_in, v_a_w_gate_up, v_a_b_gate, v_a_gn, v_a_w_out, v_b_norm, v_b_w_in, v_b_conv, v_b_w_out, v_f_norm, v_f_w_up, v_f_conv, v_f_w_down, v_final_norm):
    my_slot = _slot(*_position())

    transposed = lambda w: jnp.swapaxes(w, 1, 2)
    a_w_in_t, f_w_up_t = transposed(a_w_in), transposed(f_w_up)
    first = _all_gather("weight_gather", [a_w_in_t[0].astype(BF16), a_w_out[0].astype(BF16),
                                          _pack_rows([a_w_gate_up[0], b_norm, b_conv[0], f_conv])])
    gathers, small_shards = {}, first[2]
    later = (("f0", f_w_up_t[0], f_w_down[0]), ("b", b_w_in[0], b_w_out[0]), ("f1", f_w_up_t[1], f_w_down[1]))
    for collective_id, (group, w_in, w_out) in enumerate(later):
        w_in, w_out, small_shards = lax.optimization_barrier((w_in.astype(BF16), w_out.astype(BF16), small_shards))
        gathers[group] = _sequencer_gather(f"gather_{group}", collective_id, [w_in, w_out])
    gate_full, b_norm_full, b_conv_full, f_conv_full = _unpack_small_shards(small_shards)
    a_w_in_full = jnp.pad(first[0].reshape(PROJ_A, D_MODEL), ((0, PROJ_A_PAD - PROJ_A), (0, 0)))
    weights = dict(
        a_norm=a_norm, a_w_gate_up=jnp.pad(gate_full, ((0, GATE_PAD - GATE_RANK), (0, 0))).astype(BF16), a_b_gate=a_b_gate,
        a_gn=a_gn, b_norm=b_norm_full, b_conv=b_conv_full, f_norm=f_norm, f_conv=f_conv_full,
        final_norm=final_norm.reshape(1, D_MODEL))

    def fetch(group, after):
        if group == "a":
            return a_w_in_full, first[1].reshape(D_MODEL, D_MODEL)
        w_in, w_out = gathers[group]
        if group == "b":
            return w_in, w_out.reshape(D_MODEL, D_MODEL)
        return w_in.reshape(2, D_FF, D_MODEL), w_out.reshape(D_FF, D_MODEL)

    exchanges, pending = {}, []
    exchange_ids = dict(b=3, f0=4, a=5)
    side = lax.axis_index("c").astype(jnp.int32).reshape(1)

    def emit(group, parts, received, carry):
        sums = [_pair_add(f"pair_add_{group}_{i}", part, got, side) for i, (part, got) in enumerate(zip(parts, received))]
        carry, *sums = lax.optimization_barrier((carry, *sums))
        pending.extend(sums)
        if group != "f1":
            after = list(exchanges.values())[-1][:1] if exchanges else ()
            exchanges[group] = _sequencer_exchange(f"grads_{group}", exchange_ids[group], list(pending), after)
            pending.clear()
        return carry

    loss, dx, g = _local_step(x[0], loss_target[0], weights, fetch, emit)
    loss = lax.psum(loss, MESH_AXES)

    (up1, down1, d_b_in, d_b_out), (up0, down0), (d_a_in, d_a_out) = (exchanges[group] for group in ("b", "f0", "a"))
    back = lambda results: tuple(transposed(r) for r in results)
    big = dict(
        b_w_in=_adamw_sum("adam_b_w_in", [d_b_in], b_w_in, m_b_w_in, v_b_w_in),
        b_w_out=_adamw_sum("adam_b_w_out", [d_b_out], b_w_out, m_b_w_out, v_b_w_out),
        f_w_up=back(_adamw_sum("adam_f_w_up", [up0, up1], f_w_up_t, transposed(m_f_w_up), transposed(v_f_w_up))),
        f_w_down=_adamw_sum("adam_f_w_down", [down0, down1], f_w_down, m_f_w_down, v_f_w_down))
    small_packed, *updated = lax.optimization_barrier((_pack_small_grads(g), *big["f_w_down"]))
    big["f_w_down"] = tuple(updated)
    small_landed = _all_gather("small_grad_gather", [small_packed])[0]
    big.update(
        a_w_in=back(_adamw_sum("adam_a_w_in", [d_a_in], a_w_in_t, transposed(m_a_w_in), transposed(v_a_w_in))),
        a_w_out=_adamw_sum("adam_a_w_out", [d_a_out], a_w_out, m_a_w_out, v_a_w_out))
    small_g = _unpack_small_grads(_sum_small(small_landed))
    small_g["a_w_gate_up"] = lax.dynamic_slice_in_dim(small_g["a_w_gate_up"], my_slot * GATE_SHARD, GATE_SHARD, axis=1)
    small_g["b_norm"] = lax.dynamic_slice_in_dim(small_g["b_norm"], my_slot * NORM_SHARD, NORM_SHARD, axis=1)
    small_g["b_conv"] = lax.dynamic_slice_in_dim(small_g["b_conv"], my_slot * NORM_SHARD, NORM_SHARD, axis=1)
    small_g["f_conv"] = lax.dynamic_slice_in_dim(small_g["f_conv"], my_slot * F_CONV_SHARD, F_CONV_SHARD, axis=2)
    small_w = dict(
        a_norm=(a_norm, m_a_norm, v_a_norm), a_w_gate_up=(a_w_gate_up, m_a_w_gate_up, v_a_w_gate_up),
        a_b_gate=(a_b_gate, m_a_b_gate, v_a_b_gate), a_gn=(a_gn, m_a_gn, v_a_gn), b_norm=(b_norm, m_b_norm, v_b_norm),
        b_conv=(b_conv, m_b_conv, v_b_conv), f_norm=(f_norm, m_f_norm, v_f_norm), f_conv=(f_conv, m_f_conv, v_f_conv),
        final_norm=(final_norm, m_final_norm, v_final_norm))
    two_d = lambda a: a.reshape(-1, a.shape[-1])
    updates = _adamw_small([tuple(two_d(a.reshape(w.shape)) for a in (small_g[name], w, m, v)) for name, (w, m, v) in small_w.items()])
    small = {}
    for (name, (w, _, _)), update in zip(small_w.items(), updates):
        small[name] = (small_g[name].reshape(w.shape),) + tuple(u.reshape(w.shape) for u in update)

    order = ["a_norm", "a_w_in", "a_w_gate_up", "a_b_gate", "a_gn", "a_w_out", "b_norm", "b_w_in", "b_conv", "b_w_out",
             "f_norm", "f_w_up", "f_conv", "f_w_down", "final_norm"]
    results = {**big, **small}
    outputs = [loss, dx.reshape(1, SEQ, D_MODEL)]
    for kind in range(4):
        outputs += [results[name][kind] for name in order]
    return tuple(outputs)
```

```python
import jax
import jax.numpy as jnp
from jax import lax
from jax.experimental import pallas as pl
from jax.experimental.pallas import tpu as pltpu
from jax.experimental.pallas import tpu_sc as plsc

F32 = jnp.float32
BF16 = jnp.bfloat16

N_DEV = 8
SEQ = 2048
D_MODEL = 1024
CHUNK = 64
N_CHUNKS = SEQ // CHUNK
RMS_EPS = 1e-6
GLA_HEADS = 4
KEY_DIM = 512
VALUE_DIM = 1024
HEAD_K = KEY_DIM // GLA_HEADS
HEAD_V = VALUE_DIM // GLA_HEADS
GATE_RANK = 16
GATE_PAD = 128
GATE_NORMALIZER = 16.0
PROJ_A = 2 * KEY_DIM + 2 * VALUE_DIM + GATE_RANK
PROJ_A_PAD = 2 * KEY_DIM + 2 * VALUE_DIM + GATE_PAD
A_SHARD = PROJ_A // N_DEV
B_SHARD = 3 * D_MODEL // N_DEV
D_FF = 2816
ADAM_LR = 0.001
ADAM_B1 = 0.9
ADAM_B2 = 0.999
ADAM_EPS = 1e-08
ADAM_WD = 0.01
ADAM_STEP = 10
MESH_AXES = ("x", "y", "c")

VMEM_LIMIT = 56 * 1024 * 1024
ROW_CHUNK = 256
HALO = 16


def _params(sem=None, vmem=VMEM_LIMIT):
    return pltpu.CompilerParams(dimension_semantics=sem, vmem_limit_bytes=vmem)


NN = ((1,), (0,))
NT = ((1,), (1,))
TN = ((0,), (0,))


def _matmul(name, a, a_spec, b, b_spec, dims, grid, out_shape, out_spec, k_blocks=None, a_block_cols=None, res=None,
            res_spec=None, transpose_out=False, norm=None, swap=()):
    has_res = res is not None
    n_swap = len(swap)

    def body(*refs):
        a_ref, b_ref = refs[0], refs[1]
        r_ref = refs[2] if has_res else None

        def product(lhs, rhs):
            return lax.dot_general(lhs.astype(BF16), rhs, (dims, ((), ())), preferred_element_type=F32)

        if k_blocks is None:
            v = product(a_ref[...], b_ref[...])
        else:
            v = None
            for k in range(k_blocks):
                lhs = a_ref[k] if a_block_cols is None else a_ref[:, k * a_block_cols:(k + 1) * a_block_cols]
                p = product(lhs, b_ref[k])
                v = p if v is None else v + p
        if transpose_out:
            v = v.T
        if has_res:
            v = v + r_ref[...]
        if norm is None:
            o_ref = refs[2 + has_res]
            o_ref[...] = v.astype(o_ref.dtype)
            return
        n_in = 5 + has_res
        x_ref, g_ref, dxi_ref = refs[2 + has_res:n_in]
        dx_ref, dx16_ref, dg_ref = refs[n_in + n_swap:n_in + n_swap + 3]
        if n_swap:
            copies = _pair_copies(refs[n_in:n_in + n_swap], refs[n_in + n_swap + 3:n_in + 2 * n_swap + 3], *refs[-2:])

            @pl.when(pl.program_id(0) == 0)
            def _():
                for send, _ in copies:
                    send.start()

            @pl.when(pl.program_id(0) == grid[0] - 1)
            def _():
                for send, arrival in copies:
                    arrival.wait_recv()
                    send.wait_send()

        dx, dg = _norm_bwd_rows(x_ref[...], g_ref[...], v)
        dx = dxi_ref[...] + dx
        dx_ref[...] = dx
        dx16_ref[...] = dx.astype(BF16)

        @pl.when(pl.program_id(0) == 0)
        def _():
            dg_ref[...] = dg

        @pl.when(pl.program_id(0) > 0)
        def _():
            dg_ref[...] += dg

    operands = [a, b] + ([res] if has_res else [])
    in_specs = [a_spec, b_spec] + ([res_spec] if has_res else [])
    semantics = ("parallel",) * len(grid)
    scratch = []
    if norm is not None:
        vec = _spec((1, D_MODEL), lambda i: (0, 0))
        any_space = pl.BlockSpec(memory_space=pl.ANY)
        operands += list(norm) + list(swap)
        in_specs += [out_spec, vec, out_spec] + [any_space] * n_swap
        out_shape = [_act(dtype=F32), _act(), jax.ShapeDtypeStruct((1, D_MODEL), F32)]
        out_shape += [jax.ShapeDtypeStruct((N_DEV // 2,) + p.shape[1:], p.dtype) for p in swap]
        out_spec = [out_spec, out_spec, vec] + [any_space] * n_swap
        semantics = ("arbitrary",)
        if n_swap:
            scratch = [pltpu.SemaphoreType.DMA((n_swap, N_DEV // 2))] * 2
    return pl.pallas_call(
        body, name=name, grid=grid, in_specs=in_specs, out_specs=out_spec, out_shape=out_shape, scratch_shapes=scratch,
        compiler_params=_params(semantics),
    )(*operands)


def _resident(shape):
    return pl.BlockSpec(shape, lambda *_: (0,) * len(shape), pipeline_mode=pl.Buffered(1))


TM = 512
N_TM = SEQ // TM
PA_TILE = 640
N_PA = PROJ_A_PAD // PA_TILE
OUT_TILE = 256


def _spec(shape, fn):
    return pl.BlockSpec(shape, fn)


def _act(shape=(SEQ, D_MODEL), dtype=BF16):
    return jax.ShapeDtypeStruct(shape, dtype)


def _proj_rows_nt(name, h, wt, n_tile):
    n = wt.shape[0]
    return _matmul(name, h, _resident((SEQ, D_MODEL)), wt, _spec((n_tile, D_MODEL), lambda j: (j, 0)), NT,
                   (n // n_tile,), _act((SEQ, n)), _spec((SEQ, n_tile), lambda j: (0, j)))


def _proj_cols_nn(name, h, w_blocks):
    nb, _, n = w_blocks.shape
    return _matmul(name, h, _resident((SEQ, D_MODEL)), w_blocks, _spec((None, D_MODEL, n), lambda j: (j, 0, 0)),
                   NN, (nb,), _act((SEQ, nb * n)), _spec((SEQ, n), lambda j: (0, j)))


def _square(name, a, w, dims, x=None):
    row = _spec((TM, D_MODEL), lambda i: (i, 0))
    return _matmul(name, a, row, w, _resident((D_MODEL, D_MODEL)), dims, (N_TM,),
                   _act(dtype=F32 if x is not None else BF16), row, res=x, res_spec=row if x is not None else None)


def _sum_blocks_nn(name, a_blocks, w_blocks, x=None, norm=None, swap=()):
    nb, _, n = a_blocks.shape
    row = _spec((TM, D_MODEL), lambda i: (i, 0))
    return _matmul(name, a_blocks, _spec((nb, TM, n), lambda i: (0, i, 0)), w_blocks, _resident((nb, n, D_MODEL)),
                   NN, (N_TM,), _act(dtype=F32), row, k_blocks=nb, res=x, res_spec=row if x is not None else None, norm=norm, swap=swap)


def _sum_cols_nt(name, d, w_blocks, norm=None, swap=()):
    nb, _, n = w_blocks.shape
    return _matmul(name, d, _spec((TM, nb * n), lambda i: (i, 0)), w_blocks, _resident((nb, D_MODEL, n)), NT,
                   (N_TM,), _act(dtype=F32), _spec((TM, D_MODEL), lambda i: (i, 0)), k_blocks=nb, a_block_cols=n, norm=norm, swap=swap)


def _wide_nn(name, d, wt, x=None, norm=None, swap=()):
    n = wt.shape[0]
    row = _spec((TM, D_MODEL), lambda i: (i, 0))
    return _matmul(name, d, _spec((TM, n), lambda i: (i, 0)), wt, _resident((n, D_MODEL)), NN, (N_TM,),
                   _act(dtype=F32), row, res=x, res_spec=row if x is not None else None, norm=norm, swap=swap)


def _wide_nt(name, d, w):
    n = w.shape[0]
    return _matmul(name, d, _spec((TM, D_MODEL), lambda i: (i, 0)), w, _resident((n, D_MODEL)), NT, (N_TM,),
                   _act((SEQ, n)), _spec((TM, n), lambda i: (i, 0)))


def _proj_halves_nt(name, h, wt):
    _, n, _ = wt.shape
    return _matmul(name, h, _spec((TM, D_MODEL), lambda p, i: (i, 0)), wt, _spec((None, n, D_MODEL), lambda p, i: (p, 0, 0)), NT,
                   (2, N_TM), _act((2, SEQ, n)), _spec((None, TM, n), lambda p, i: (p, i, 0)))


def _wgrad_halves_tn(name, d, n_tile, h):
    _, _, n = d.shape
    return _matmul(name, d, _spec((None, SEQ, n_tile), lambda p, j: (p, 0, j)), h, _resident((SEQ, D_MODEL)), TN,
                   (2, n // n_tile), _act((2, n, D_MODEL)), _spec((None, n_tile, D_MODEL), lambda p, j: (p, j, 0)))


def _wgrad_cols_tn(name, d, n_tile, h):
    n = d.shape[1]
    return _matmul(name, d, _spec((SEQ, n_tile), lambda j: (0, j)), h, _resident((SEQ, D_MODEL)), TN,
                   (n // n_tile,), _act((n, D_MODEL)), _spec((n_tile, D_MODEL), lambda j: (j, 0)))


def _wgrad_cols_transposed_tn(name, h, d, n_tile):
    nb = d.shape[1] // n_tile
    return _matmul(name, d, _spec((SEQ, n_tile), lambda j: (0, j)), h, _resident((SEQ, D_MODEL)), TN, (nb,),
                   _act((nb, D_MODEL, n_tile)), _spec((None, D_MODEL, n_tile), lambda j: (j, 0, 0)), transpose_out=True)


NORM_ROWS = 512


def _rstd(x):
    return lax.rsqrt(jnp.mean(x * x, axis=-1, keepdims=True) + RMS_EPS)


def _norm_fwd(name, x, gamma):
    def body(x_ref, g_ref, h_ref):
        x = x_ref[...]
        h_ref[...] = (x * _rstd(x) * g_ref[...]).astype(BF16)

    row = _spec((NORM_ROWS, D_MODEL), lambda i: (i, 0))
    return pl.pallas_call(
        body, name=name, grid=(SEQ // NORM_ROWS,), in_specs=[row, _spec((1, D_MODEL), lambda i: (0, 0))], out_specs=row,
        out_shape=jax.ShapeDtypeStruct((SEQ, D_MODEL), BF16), compiler_params=_params(("parallel",)),
    )(x, gamma)


def _norm_bwd_rows(x, gamma, dh):
    r = _rstd(x)
    xh = x * r
    dxh = dh * gamma
    dx = r * (dxh - xh * jnp.mean(dxh * xh, axis=-1, keepdims=True))
    return dx, jnp.sum(dh * xh, axis=0, keepdims=True)


def _loss_head(x, gamma, target):
    def body(x_ref, g_ref, t_ref, loss_ref, dx_ref, dx16_ref, dg_ref):
        x = x_ref[...]
        gamma = g_ref[...]
        err = x * _rstd(x) * gamma - t_ref[...]
        dy = err * (1.0 / D_MODEL)
        dx, dg = _norm_bwd_rows(x, gamma, dy)
        dx_ref[...] = dx
        dx16_ref[...] = dx.astype(BF16)
        part = 0.5 * jnp.sum(jnp.sum(err * err, axis=-1, keepdims=True) * (1.0 / D_MODEL), axis=0, keepdims=True)
        part = jnp.broadcast_to(part, loss_ref.shape)

        @pl.when(pl.program_id(0) == 0)
        def _():
            dg_ref[...] = dg
            loss_ref[...] = part

        @pl.when(pl.program_id(0) > 0)
        def _():
            dg_ref[...] += dg
            loss_ref[...] += part

    row = _spec((NORM_ROWS, D_MODEL), lambda i: (i, 0))
    vec = _spec((1, D_MODEL), lambda i: (0, 0))
    return pl.pallas_call(
        body, name="loss_head", grid=(SEQ // NORM_ROWS,), in_specs=[row, vec, row],
        out_specs=[_spec((1, 128), lambda i: (0, 0)), row, row, vec],
        out_shape=[jax.ShapeDtypeStruct((1, 128), F32), _act(dtype=F32), _act(), jax.ShapeDtypeStruct((1, D_MODEL), F32)],
        compiler_params=_params(("arbitrary",)),
    )(x, gamma, target)


def _sigmoid(x):
    return 1.0 / (1.0 + jnp.exp(-x))


def _rows(ref, c):
    return ref[pl.ds(pl.multiple_of(c * ROW_CHUNK, ROW_CHUNK), ROW_CHUNK), :].astype(F32)


def _rows_before(ref, c):
    start = pl.multiple_of(jnp.maximum(c * ROW_CHUNK - HALO, 0), HALO)
    rows = ref[pl.ds(start, HALO), :].astype(F32)
    return jnp.where(c > 0, rows, 0.0)


def _rows_after(ref, c, n_chunks):
    start = pl.multiple_of(jnp.minimum((c + 1) * ROW_CHUNK, SEQ - HALO), HALO)
    rows = ref[pl.ds(start, HALO), :].astype(F32)
    return jnp.where(c < n_chunks - 1, rows, 0.0)


def _shift_down(z, before, n):
    return pltpu.roll(jnp.concatenate([before, z], axis=0), n, 0)[HALO:]


def _shift_up(z, after, n):
    rows = z.shape[0]
    return pltpu.roll(jnp.concatenate([z, after], axis=0), rows + HALO - n, 0)[:rows]


def _conv_rows(z, before, w):
    z1 = _shift_down(z, before, 1)
    z2 = _shift_down(z, before, 2)
    return w[2:3, :] * z + w[1:2, :] * z1 + w[0:1, :] * z2, z1, z2


def _conv_t_rows(dy, after, w):
    return w[2:3, :] * dy + w[1:2, :] * _shift_up(dy, after, 1) + w[0:1, :] * _shift_up(dy, after, 2)


N_ROW_CHUNKS = SEQ // ROW_CHUNK


FF_COLS = 256
N_FF_COLS = D_FF // FF_COLS


def _ffn_mid_fwd(name, gu, conv_w):
    def body(gu_ref, w_ref, a_ref):
        w = w_ref[...]

        def chunk(c, carry):
            g = _rows(gu_ref.at[0], c)
            u = _rows(gu_ref.at[1], c)
            gc, _, _ = _conv_rows(g, _rows_before(gu_ref.at[0], c), w)
            a_ref[pl.ds(pl.multiple_of(c * ROW_CHUNK, ROW_CHUNK), ROW_CHUNK), :] = (gc * _sigmoid(gc) * u).astype(BF16)
            return carry

        lax.fori_loop(0, N_ROW_CHUNKS, chunk, 0)

    col = _spec((SEQ, FF_COLS), lambda j: (0, j))
    return pl.pallas_call(
        body, name=name, grid=(N_FF_COLS,),
        in_specs=[_spec((2, SEQ, FF_COLS), lambda j: (0, 0, j)), _spec((3, FF_COLS), lambda j: (0, j))], out_specs=col,
        out_shape=_act((SEQ, D_FF)), compiler_params=_params(("parallel",)),
    )(gu, conv_w)


def _ffn_mid_bwd(name, gu, conv_w, da):
    def body(gu_ref, w_ref, da_ref, dgu_ref, dw_ref, dgc_ref):
        w = w_ref[...]

        def first(c, acc):
            g = _rows(gu_ref.at[0], c)
            u = _rows(gu_ref.at[1], c)
            d = _rows(da_ref, c)
            gc, g1, g2 = _conv_rows(g, _rows_before(gu_ref.at[0], c), w)
            sg = _sigmoid(gc)
            rows = pl.ds(pl.multiple_of(c * ROW_CHUNK, ROW_CHUNK), ROW_CHUNK)
            dgu_ref[1, rows, :] = (d * gc * sg).astype(BF16)
            dgc = d * u * (sg * (1.0 + gc * (1.0 - sg)))
            dgc_ref[rows, :] = dgc
            return (acc[0] + jnp.sum(dgc * g2, axis=0, keepdims=True), acc[1] + jnp.sum(dgc * g1, axis=0, keepdims=True),
                    acc[2] + jnp.sum(dgc * g, axis=0, keepdims=True))

        zero = jnp.zeros((1, FF_COLS), F32)
        acc = lax.fori_loop(0, N_ROW_CHUNKS, first, (zero, zero, zero))
        for r in range(3):
            dw_ref[r:r + 1, :] = acc[r]

        def second(c, carry):
            dgc = _rows(dgc_ref, c)
            dg = _conv_t_rows(dgc, _rows_after(dgc_ref, c, N_ROW_CHUNKS), w)
            dgu_ref[0, pl.ds(pl.multiple_of(c * ROW_CHUNK, ROW_CHUNK), ROW_CHUNK), :] = dg.astype(BF16)
            return carry

        lax.fori_loop(0, N_ROW_CHUNKS, second, 0)

    pair = _spec((2, SEQ, FF_COLS), lambda j: (0, 0, j))
    wspec = _spec((3, FF_COLS), lambda j: (0, j))
    return pl.pallas_call(
        body, name=name, grid=(N_FF_COLS,), in_specs=[pair, wspec, _spec((SEQ, FF_COLS), lambda j: (0, j))],
        out_specs=[pair, wspec], out_shape=[_act((2, SEQ, D_FF)), jax.ShapeDtypeStruct((3, D_FF), F32)],
        scratch_shapes=[pltpu.VMEM((SEQ, FF_COLS), F32)],
        compiler_params=_params(("parallel",)),
    )(gu, conv_w, da)


SC_COLS = 256
N_SC = D_MODEL // SC_COLS


def _sc_specs():
    return [_spec((SEQ, SC_COLS), lambda j, part=part: (0, part * N_SC + j)) for part in range(3)]


def _sc_mid_fwd(p, conv_w):
    def body(b_ref, c_ref, h_ref, w_ref, y_ref):
        w = w_ref[...]

        def chunk(c, carry):
            z = _rows(c_ref, c) * _rows(h_ref, c)
            before = _rows_before(c_ref, c) * _rows_before(h_ref, c)
            zc, _, _ = _conv_rows(z, before, w)
            y_ref[pl.ds(pl.multiple_of(c * ROW_CHUNK, ROW_CHUNK), ROW_CHUNK), :] = (_rows(b_ref, c) * zc).astype(BF16)
            return carry

        lax.fori_loop(0, N_ROW_CHUNKS, chunk, 0)

    col = _spec((SEQ, SC_COLS), lambda j: (0, j))
    return pl.pallas_call(
        body, name="sc_mid_fwd", grid=(N_SC,), in_specs=_sc_specs() + [_spec((3, SC_COLS), lambda j: (0, j))], out_specs=col,
        out_shape=jax.ShapeDtypeStruct((SEQ, D_MODEL), BF16), compiler_params=_params(("parallel",)),
    )(p, p, p, conv_w)


def _sc_mid_bwd(p, conv_w, dy):
    def body(b_ref, c_ref, h_ref, w_ref, dy_ref, db_ref, dc_ref, dh_ref, dw_ref, dzc_ref):
        w = w_ref[...]

        def first(c, acc):
            z = _rows(c_ref, c) * _rows(h_ref, c)
            before = _rows_before(c_ref, c) * _rows_before(h_ref, c)
            zc, z1, z2 = _conv_rows(z, before, w)
            d = _rows(dy_ref, c)
            rows = pl.ds(pl.multiple_of(c * ROW_CHUNK, ROW_CHUNK), ROW_CHUNK)
            db_ref[rows, :] = (d * zc).astype(BF16)
            dzc = d * _rows(b_ref, c)
            dzc_ref[rows, :] = dzc
            return (acc[0] + jnp.sum(dzc * z2, axis=0, keepdims=True), acc[1] + jnp.sum(dzc * z1, axis=0, keepdims=True),
                    acc[2] + jnp.sum(dzc * z, axis=0, keepdims=True))

        zero = jnp.zeros((1, SC_COLS), F32)
        acc = lax.fori_loop(0, N_ROW_CHUNKS, first, (zero, zero, zero))
        for r in range(3):
            dw_ref[r:r + 1, :] = acc[r]

        def second(c, carry):
            dz = _conv_t_rows(_rows(dzc_ref, c), _rows_after(dzc_ref, c, N_ROW_CHUNKS), w)
            rows = pl.ds(pl.multiple_of(c * ROW_CHUNK, ROW_CHUNK), ROW_CHUNK)
            dc_ref[rows, :] = (dz * _rows(h_ref, c)).astype(BF16)
            dh_ref[rows, :] = (dz * _rows(c_ref, c)).astype(BF16)
            return carry

        lax.fori_loop(0, N_ROW_CHUNKS, second, 0)

    col = _spec((SEQ, SC_COLS), lambda j: (0, j))
    wspec = _spec((3, SC_COLS), lambda j: (0, j))
    act = jax.ShapeDtypeStruct((SEQ, D_MODEL), BF16)
    return pl.pallas_call(
        body, name="sc_mid_bwd", grid=(N_SC,), in_specs=_sc_specs() + [wspec, col], out_specs=[col, col, col, wspec],
        out_shape=[act, act, act, jax.ShapeDtypeStruct((3, D_MODEL), F32)],
        scratch_shapes=[pltpu.VMEM((SEQ, SC_COLS), F32)], compiler_params=_params(("parallel",)),
    )(p, p, p, conv_w, dy)


GLA_GROUP = 4
GLA_ROWS = GLA_GROUP * CHUNK
N_GROUPS = N_CHUNKS // GLA_GROUP
Q0, K0, V0, R0, G0 = 0, KEY_DIM, 2 * KEY_DIM, 2 * KEY_DIM + VALUE_DIM, 2 * KEY_DIM + 2 * VALUE_DIM


def _tri(strict):
    r = lax.broadcasted_iota(jnp.int32, (CHUNK, CHUNK), 0)
    c = lax.broadcasted_iota(jnp.int32, (CHUNK, CHUNK), 1)
    return jnp.where(c < r if strict else c <= r, 1.0, 0.0).astype(F32)


def _cumsum_rows(tri, x):
    return jnp.dot(tri, x, preferred_element_type=F32, precision=lax.Precision.HIGHEST)


def _gate_logits(gl, wgu, b_gate):
    return jnp.dot(gl, wgu, preferred_element_type=F32) + b_gate


def _log_decay(logits):
    return (jnp.minimum(logits, 0.0) - jnp.log(1.0 + jnp.exp(-jnp.abs(logits)))) * (1.0 / GATE_NORMALIZER)


def _head(x, h, width):
    return x[:, h * width:(h + 1) * width]


def _gla_fwd(proj, wgu, b_gate, gn):
    def body(p_ref, wgu_ref, b_ref, gn_ref, o_ref, og_ref, st_ref, state):
        @pl.when(pl.program_id(0) == 0)
        def _():
            state[...] = jnp.zeros_like(state)

        tri = _tri(False)
        la = _log_decay(_gate_logits(p_ref[:, G0:G0 + GATE_PAD], wgu_ref[...], b_ref[...]))
        for c in range(GLA_GROUP):
            rows = slice(c * CHUNK, (c + 1) * CHUNK)
            cum = _cumsum_rows(tri, la[rows])
            tot = cum[CHUNK - 1:CHUNK, :]
            kd = (p_ref[rows, K0:K0 + KEY_DIM].astype(F32) * jnp.exp(tot - cum)).astype(BF16)
            decay = jnp.exp(tot)
            q = (p_ref[rows, Q0:Q0 + KEY_DIM].astype(F32) * (HEAD_K ** -0.5)).astype(BF16)
            v = p_ref[rows, V0:V0 + VALUE_DIM]
            for h in range(GLA_HEADS):
                upd = lax.dot_general(_head(v, h, HEAD_V), _head(kd, h, HEAD_K), (TN, ((), ())), preferred_element_type=F32)
                s = state[h] * _head(decay, h, HEAD_K) + upd
                state[h] = s
                st_ref[c, h] = s
                o_ref[rows, h * HEAD_V:(h + 1) * HEAD_V] = lax.dot_general(
                    _head(q, h, HEAD_K), s.astype(BF16), (NT, ((), ())), preferred_element_type=F32)
        r = p_ref[:, R0:R0 + VALUE_DIM].astype(F32)
        gate = r * _sigmoid(r) * gn_ref[...]
        for h in range(GLA_HEADS):
            cols = slice(h * HEAD_V, (h + 1) * HEAD_V)
            o = o_ref[:, cols]
            og_ref[:, cols] = (o * _rstd(o) * gate[:, cols]).astype(BF16)

    rows = _spec((GLA_ROWS, VALUE_DIM), lambda i: (i, 0))
    const = lambda shape: _spec(shape, lambda i: (0,) * len(shape))
    return pl.pallas_call(
        body, name="gla_fwd", grid=(N_GROUPS,),
        in_specs=[_spec((GLA_ROWS, PROJ_A_PAD), lambda i: (i, 0)), const((GATE_PAD, KEY_DIM)), const((1, KEY_DIM)),
                  const((1, VALUE_DIM))],
        out_specs=[rows, rows, _spec((GLA_GROUP, GLA_HEADS, HEAD_V, HEAD_K), lambda i: (i, 0, 0, 0))],
        out_shape=[jax.ShapeDtypeStruct((SEQ, VALUE_DIM), F32), jax.ShapeDtypeStruct((SEQ, VALUE_DIM), BF16),
                   jax.ShapeDtypeStruct((N_CHUNKS, GLA_HEADS, HEAD_V, HEAD_K), F32)],
        scratch_shapes=[pltpu.VMEM((GLA_HEADS, HEAD_V, HEAD_K), F32)], compiler_params=_params(("arbitrary",)),
    )(proj, wgu, b_gate, gn)


def _gla_bwd(proj, wgu, b_gate, gn, o, states, dog):
    last = N_GROUPS - 1

    def body(p_ref, wgu_ref, b_ref, gn_ref, o_ref, st_ref, stp_ref, dog_ref, dp_ref, dwgu_ref, db_ref, dgn_ref, carry, do_buf):
        step = pl.program_id(0)

        @pl.when(step == 0)
        def _():
            carry[...] = jnp.zeros_like(carry)

        r = p_ref[:, R0:R0 + VALUE_DIM].astype(F32)
        sr = _sigmoid(r)
        silu = r * sr
        gn_row = gn_ref[...]
        dog_rows = dog_ref[...].astype(F32)
        dn = dog_rows * silu
        dgn_cols = []
        for h in range(GLA_HEADS):
            cols = slice(h * HEAD_V, (h + 1) * HEAD_V)
            oh = o_ref[:, cols]
            rs = _rstd(oh)
            ohat = oh * rs
            dn_h = dn[:, cols]
            dgn_cols.append(jnp.sum(dn_h * ohat, axis=0, keepdims=True))
            dohat = dn_h * gn_row[:, cols]
            do_buf[:, cols] = rs * (dohat - ohat * jnp.mean(dohat * ohat, axis=-1, keepdims=True))
            n_h = ohat * gn_row[:, cols]
            dp_ref[:, R0 + h * HEAD_V:R0 + (h + 1) * HEAD_V] = (
                dog_rows[:, cols] * n_h * (sr[:, cols] * (1.0 + r[:, cols] * (1.0 - sr[:, cols])))).astype(BF16)
        dgn = jnp.concatenate(dgn_cols, axis=1)

        tri = _tri(False)
        tri_strict = _tri(True)
        gl = p_ref[:, G0:G0 + GATE_PAD]
        logits = _gate_logits(gl, wgu_ref[...], b_ref[...])
        la = _log_decay(logits)
        dlogit_rows = []
        for c in reversed(range(GLA_GROUP)):
            rows = slice(c * CHUNK, (c + 1) * CHUNK)
            cum = _cumsum_rows(tri, la[rows])
            tot = cum[CHUNK - 1:CHUNK, :]
            fade = jnp.exp(tot - cum)
            k = p_ref[rows, K0:K0 + KEY_DIM].astype(F32)
            kd32 = k * fade
            kd = kd32.astype(BF16)
            decay = jnp.exp(tot)
            q = (p_ref[rows, Q0:Q0 + KEY_DIM].astype(F32) * (HEAD_K ** -0.5)).astype(BF16)
            v = p_ref[rows, V0:V0 + VALUE_DIM]
            do = do_buf[rows, :].astype(BF16)
            dkd_cols, ddecay_cols = [], []
            for h in range(GLA_HEADS):
                do_h = _head(do, h, HEAD_V)
                s = st_ref[c, h]
                dq = jnp.dot(do_h, s.astype(BF16), preferred_element_type=F32) * (HEAD_K ** -0.5)
                dp_ref[rows, Q0 + h * HEAD_K:Q0 + (h + 1) * HEAD_K] = dq.astype(BF16)
                g = carry[h] + lax.dot_general(do_h, _head(q, h, HEAD_K), (TN, ((), ())), preferred_element_type=F32)
                g16 = g.astype(BF16)
                dkd_cols.append(jnp.dot(_head(v, h, HEAD_V), g16, preferred_element_type=F32))
                dv = lax.dot_general(_head(kd, h, HEAD_K), g16, (NT, ((), ())), preferred_element_type=F32)
                dp_ref[rows, V0 + h * HEAD_V:V0 + (h + 1) * HEAD_V] = dv.astype(BF16)
                if c > 0:
                    s_prev = st_ref[c - 1, h]
                else:
                    s_prev = jnp.where(step < last, stp_ref[0, h], 0.0)
                ddecay_cols.append(jnp.sum(g * s_prev, axis=0, keepdims=True))
                carry[h] = g * _head(decay, h, HEAD_K)
            dkd = jnp.concatenate(dkd_cols, axis=1)
            ddecay = jnp.concatenate(ddecay_cols, axis=1)
            dp_ref[rows, K0:K0 + KEY_DIM] = (dkd * fade).astype(BF16)
            e = dkd * kd32
            dla = ddecay * decay + _cumsum_rows(tri_strict, e)
            dlogit_rows.append(dla * (1.0 / GATE_NORMALIZER) * (1.0 - _sigmoid(logits[rows])))
        dlogit = jnp.concatenate(dlogit_rows[::-1], axis=0)
        dlogit16 = dlogit.astype(BF16)
        dp_ref[:, G0:G0 + GATE_PAD] = lax.dot_general(
            dlogit16, wgu_ref[...], (NT, ((), ())), preferred_element_type=F32).astype(BF16)
        dwgu = lax.dot_general(gl, dlogit16, (TN, ((), ())), preferred_element_type=F32)
        db = jnp.sum(dlogit, axis=0, keepdims=True)

        @pl.when(step == 0)
        def _():
            dwgu_ref[...] = dwgu
            db_ref[...] = db
            dgn_ref[...] = dgn

        @pl.when(step > 0)
        def _():
            dwgu_ref[...] += dwgu
            db_ref[...] += db
            dgn_ref[...] += dgn

    rev = lambda i: (last - i, 0)
    rows = _spec((GLA_ROWS, VALUE_DIM), rev)
    const = lambda shape: _spec(shape, lambda i: (0,) * len(shape))
    st_shape = (GLA_HEADS, HEAD_V, HEAD_K)
    return pl.pallas_call(
        body, name="gla_bwd", grid=(N_GROUPS,),
        in_specs=[_spec((GLA_ROWS, PROJ_A_PAD), rev), const((GATE_PAD, KEY_DIM)), const((1, KEY_DIM)), const((1, VALUE_DIM)),
                  rows, _spec((GLA_GROUP,) + st_shape, lambda i: (last - i, 0, 0, 0)),
                  _spec((1,) + st_shape, lambda i: (jnp.maximum((last - i) * GLA_GROUP - 1, 0), 0, 0, 0)), rows],
        out_specs=[_spec((GLA_ROWS, PROJ_A_PAD), rev), const((GATE_PAD, KEY_DIM)), const((1, KEY_DIM)), const((1, VALUE_DIM))],
        out_shape=[jax.ShapeDtypeStruct((SEQ, PROJ_A_PAD), BF16), jax.ShapeDtypeStruct((GATE_PAD, KEY_DIM), F32),
                   jax.ShapeDtypeStruct((1, KEY_DIM), F32), jax.ShapeDtypeStruct((1, VALUE_DIM), F32)],
        scratch_shapes=[pltpu.VMEM(st_shape, F32), pltpu.VMEM((GLA_ROWS, VALUE_DIM), F32)],
        compiler_params=_params(("arbitrary",)),
    )(proj, wgu, b_gate, gn, o, states, states, dog)


WGRAD_FF_TILE = D_FF // 2


def _ffn_fwd(tag, x, gamma, w_up_t, conv_w, w_down):
    h = _norm_fwd(f"ffn{tag}_norm", x, gamma)
    gu = _proj_halves_nt(f"ffn{tag}_up", h, w_up_t)
    a = _ffn_mid_fwd(f"ffn{tag}_mid", gu, conv_w)
    return _wide_nn(f"ffn{tag}_down", a, w_down, x=x), (h, gu, a)


def _owner_blocks(d, rows=None):
    if rows is not None:
        d = d[:rows]
    return d.reshape((N_DEV, -1) + d.shape[-1:])


def _ffn_bwd(tag, x, gamma, w_up_t, conv_w, w_down, saved, dx, dx16, swap):
    h, gu, a = saved
    da = _wide_nt(f"ffn{tag}_da", dx16, w_down)
    d_w_down = _owner_blocks(_wgrad_cols_tn(f"ffn{tag}_dwdown", a, WGRAD_FF_TILE, dx16))
    dgu, d_conv = _ffn_mid_bwd(f"ffn{tag}_mid_bwd", gu, conv_w, da)
    d_w_up_t = _owner_blocks(_wgrad_halves_tn(f"ffn{tag}_dwup", dgu, WGRAD_FF_TILE, h))
    parts = (d_w_up_t, d_w_down)
    dx, dx16, d_gamma, *received = _sum_blocks_nn(
        f"ffn{tag}_dh", dgu, w_up_t, norm=(x, gamma, dx), swap=parts if swap else ())
    return dx, dx16, d_gamma, d_conv, parts, received


def _local_step(x, target, w, fetch=None, emit=None):
    if fetch is None:
        local = dict(a=(w.get("a_w_in"), w.get("a_w_out")), b=(w.get("b_w_in"), w.get("b_w_out")))
        for layer in range(2):
            local[f"f{layer}"] = (w["f_w_up"][layer], w["f_w_down"][layer]) if "f_w_up" in w else None
        fetch = lambda group, after: local[group]
    swap = emit is not None
    if emit is None:
        emit = lambda group, parts, received, dx: dx
    f_norm = (w["f_norm"][0:1], w["f_norm"][1:2])

    x0 = x
    a_w_in, a_w_out = fetch("a", x0)
    h0 = _norm_fwd("a_norm", x0, w["a_norm"])
    proj = _proj_rows_nt("a_in", h0, a_w_in, PA_TILE)
    o, og, states = _gla_fwd(proj, w["a_w_gate_up"], w["a_b_gate"], w["a_gn"])
    x1 = _square("a_out", og, a_w_out, NN, x0)
    up0, down0 = fetch("f0", x1)
    x2, ffn0 = _ffn_fwd(0, x1, f_norm[0], up0, w["f_conv"][0], down0)
    b_w_in, b_w_out = fetch("b", x2)
    h2 = _norm_fwd("b_norm", x2, w["b_norm"])
    p = _proj_cols_nn("b_in", h2, b_w_in)
    y = _sc_mid_fwd(p, w["b_conv"])
    x3 = _square("b_out", y, b_w_out, NN, x2)
    up1, down1 = fetch("f1", x3)
    x4, ffn1 = _ffn_fwd(1, x3, f_norm[1], up1, w["f_conv"][1], down1)
    loss, dx, dx16, d_final_norm = _loss_head(x4, w["final_norm"], target)

    dx, dx16, d_f_norm1, d_fconv1, parts_f1, got = _ffn_bwd(
        1, x3, f_norm[1], up1, w["f_conv"][1], down1, ffn1, dx, dx16, swap)
    dx16 = emit("f1", parts_f1, got, dx16)

    dy = _square("b_dy", dx16, b_w_out, NT)
    d_b_w_out = _owner_blocks(_wgrad_cols_tn("b_dwout", y, OUT_TILE, dx16))
    db, dc, dhh, d_b_conv = _sc_mid_bwd(p, w["b_conv"], dy)
    dp = jnp.concatenate([db, dc, dhh], axis=1)
    parts_b = (_wgrad_cols_transposed_tn("b_dwin", h2, dp, B_SHARD), d_b_w_out)
    dx, dx16, d_b_norm, *got = _sum_cols_nt("b_dh", dp, b_w_in, norm=(x2, w["b_norm"], dx), swap=parts_b if swap else ())
    dx16 = emit("b", parts_b, got, dx16)

    dx, dx16, d_f_norm0, d_fconv0, parts_f0, got = _ffn_bwd(
        0, x1, f_norm[0], up0, w["f_conv"][0], down0, ffn0, dx, dx16, swap)
    dx16 = emit("f0", parts_f0, got, dx16)

    dog = _square("a_dog", dx16, a_w_out, NT)
    d_a_w_out = _owner_blocks(_wgrad_cols_tn("a_dwout", og, OUT_TILE, dx16))
    dproj, d_wgu, d_b_gate, d_gn = _gla_bwd(proj, w["a_w_gate_up"], w["a_b_gate"], w["a_gn"], o, states, dog)
    parts_a = (_owner_blocks(_wgrad_cols_tn("a_dwin", dproj, PA_TILE, h0), PROJ_A), d_a_w_out)
    dx, _, d_a_norm, *got = _wide_nn("a_dh", dproj, a_w_in, norm=(x0, w["a_norm"], dx), swap=parts_a if swap else ())
    emit("a", parts_a, got, dx)

    grads = dict(
        a_norm=d_a_norm, a_w_in=parts_a[0], a_w_gate_up=d_wgu, a_b_gate=d_b_gate, a_gn=d_gn, a_w_out=parts_a[1],
        b_norm=d_b_norm, b_w_in=parts_b[0], b_conv=d_b_conv, b_w_out=parts_b[1],
        f_norm=(d_f_norm0, d_f_norm1), f_w_up=(parts_f0[0], parts_f1[0]), f_conv=(d_fconv0, d_fconv1),
        f_w_down=(parts_f0[1], parts_f1[1]), final_norm=d_final_norm)
    grads["loss"] = loss
    return dx, grads


MESH_ID = pl.DeviceIdType.MESH
ANY = pl.BlockSpec(memory_space=pl.ANY)
N_PEERS = N_DEV - 1


def _position():
    return lax.axis_index("x"), lax.axis_index("y"), lax.axis_index("c")


def _slot(px, py, pc):
    return 4 * px + 2 * py + pc


def _all_gather(name, shards):
    n = len(shards)

    def body(*refs):
        ins, outs = refs[:n], refs[n:2 * n]
        send_sems, recv_sems, local_sems = refs[2 * n:]
        x, y, c = _position()
        me, sibling = (x, y, c), (x, y, 1 - c)
        chips = [(1 - x, y), (x, 1 - y), (1 - x, 1 - y)]

        def copy(t, k, block, to, from_input=False):
            dst = outs[t].at[_slot(*block)]
            return pltpu.make_async_remote_copy(
                src_ref=ins[t] if from_input else dst, dst_ref=dst, send_sem=send_sems.at[t, k], recv_sem=recv_sems.at[t, k],
                device_id=to, device_id_type=MESH_ID)

        mine = [pltpu.make_async_copy(ins[t], outs[t].at[_slot(*me)], local_sems.at[t]) for t in range(n)]
        for cp in mine:
            cp.start()
        first = []
        for t in range(n):
            first.append(copy(t, 0, me, sibling, True))
            first += [copy(t, 1 + j, me, (*chip, c), True) for j, chip in enumerate(chips)]
        for cp in first:
            cp.start()
        passed = []
        for t in range(n):
            for j, chip in enumerate(chips):
                copy(t, 1 + j, (*chip, c), me).wait_recv()
                fwd = copy(t, 4 + j, (*chip, c), sibling)
                fwd.start()
                passed.append(fwd)
        for t in range(n):
            copy(t, 0, sibling, me).wait_recv()
            for j, chip in enumerate(chips):
                copy(t, 4 + j, (*chip, 1 - c), me).wait_recv()
        for cp in first + passed:
            cp.wait_send()
        for cp in mine:
            cp.wait()

    return pl.pallas_call(
        body, name=name, in_specs=[ANY] * n, out_specs=[ANY] * n,
        out_shape=[jax.ShapeDtypeStruct((N_DEV,) + s.shape, s.dtype) for s in shards],
        scratch_shapes=[pltpu.SemaphoreType.DMA((n, N_PEERS)), pltpu.SemaphoreType.DMA((n, N_PEERS)), pltpu.SemaphoreType.DMA((n,))],
    )(*shards)


SIBLING_AND_SAME_CORE = (1, 2, 4, 6)
SAME_CORE = (2, 4, 6)


def _flip(x, y, c, k):
    return x ^ (k >> 2), y ^ ((k >> 1) & 1), c ^ (k & 1)


N_CHIPS = N_DEV // 2


def _chip(px, py):
    return 2 * px + py


def _pair_copies(parts, received, send_sems, recv_sems):
    x, y, c = lax.axis_index("x"), lax.axis_index("y"), lax.axis_index("c")
    sibling = (x, y, 1 - c)
    copies = []
    for t in range(len(parts)):
        for q in range(N_DEV // 2):
            send = pltpu.make_async_remote_copy(
                src_ref=parts[t].at[2 * q + 1 - c], dst_ref=received[t].at[q], send_sem=send_sems.at[t, q],
                recv_sem=recv_sems.at[t, q], device_id=sibling, device_id_type=pl.DeviceIdType.MESH)
            landed = received[t].at[q]
            arrival = pltpu.make_async_remote_copy(
                src_ref=landed, dst_ref=landed, send_sem=send_sems.at[t, q], recv_sem=recv_sems.at[t, q],
                device_id=sibling, device_id_type=pl.DeviceIdType.MESH)
            copies.append((send, arrival))
    return copies


PAIR_ROWS = 1024


def _pair_add(name, part, received, side):
    _, rows, cols = part.shape
    tiles = [t for t in range(PAIR_ROWS, 0, -BF16_ROWS) if rows % t == 0]
    tr = tiles[0] if tiles else rows

    def body(side_ref, p_ref, r_ref, o_ref):
        o_ref[...] = (p_ref[...].astype(F32) + r_ref[...].astype(F32)).astype(BF16)

    tile = _spec((None, tr, cols), lambda q, i, side_ref: (q, i, 0))
    return pl.pallas_call(
        body, name=name,
        grid_spec=pltpu.PrefetchScalarGridSpec(
            num_scalar_prefetch=1, grid=(N_CHIPS, rows // tr),
            in_specs=[_spec((None, tr, cols), lambda q, i, side_ref: (2 * q + side_ref[0], i, 0)), tile], out_specs=tile),
        out_shape=jax.ShapeDtypeStruct((N_CHIPS, rows, cols), BF16), compiler_params=_params(("parallel", "parallel")),
    )(side, part, received)


def _send_copy(parts, landing, send_sems, recv_sems, t, s, k):
    x, y, c = _position()
    px, py, _ = _flip(x, y, c, k)
    return pltpu.make_async_remote_copy(
        src_ref=parts[t].at[_chip(px, py)], dst_ref=landing[t].at[_chip(x, y)], send_sem=send_sems.at[s],
        recv_sem=recv_sems.at[s], device_id=(px, py, c), device_id_type=MESH_ID)


def _send_arrival(landing, send_sems, recv_sems, t, s, k):
    x, y, c = _position()
    px, py, _ = _flip(x, y, c, k)
    landed = landing[t].at[_chip(px, py)]
    return pltpu.make_async_remote_copy(
        src_ref=landed, dst_ref=landed, send_sem=send_sems.at[s], recv_sem=recv_sems.at[s],
        device_id=(px, py, c), device_id_type=MESH_ID)


def _handshake(peers):
    x, y, c = _position()
    barrier = pltpu.get_barrier_semaphore()
    for k in peers:
        pl.semaphore_signal(barrier, inc=1, device_id=_flip(x, y, c, k), device_id_type=MESH_ID)
    pl.semaphore_wait(barrier, len(peers))


def _sequencer(name, collective_id, n_copies, body, operands, out_type):
    n_arrays = len(operands)
    return pl.kernel(
        body, out_type=out_type, mesh=plsc.ScalarSubcoreMesh(axis_name="sequencer", num_cores=1), name=name,
        scratch_types=(pltpu.SemaphoreType.DMA((n_copies,)), pltpu.SemaphoreType.DMA((n_copies,)),
                       pltpu.SemaphoreType.DMA((n_arrays,))),
        compiler_params=pltpu.CompilerParams(collective_id=collective_id))(*operands)


def _sequencer_exchange(name, collective_id, parts, after=()):
    n, n_peers, n_in = len(parts), len(SAME_CORE), len(parts) + len(after)

    def body(*refs):
        src, landing = refs[:n], refs[n_in:n_in + n]
        send_sems, recv_sems, local_sems = refs[n_in + n:]
        _handshake(SAME_CORE)
        x, y, _ = _position()
        mine = [pltpu.make_async_copy(src[t].at[_chip(x, y)], landing[t].at[_chip(x, y)], local_sems.at[t]) for t in range(n)]
        for cp in mine:
            cp.start()
        sent = [_send_copy(src, landing, send_sems, recv_sems, t, t * n_peers + j, k)
                for t in range(n) for j, k in enumerate(SAME_CORE)]
        for cp in sent:
            cp.start()
        for t in range(n):
            for j, k in enumerate(SAME_CORE):
                _send_arrival(landing, send_sems, recv_sems, t, t * n_peers + j, k).wait_recv()
        for cp in sent:
            cp.wait_send()
        for cp in mine:
            cp.wait()

    landing = [jax.ShapeDtypeStruct(p.shape, p.dtype) for p in parts]
    return _sequencer(name, collective_id, n * n_peers, body, list(parts) + list(after), landing)


def _sequencer_gather(name, collective_id, shards):
    n, per = len(shards), N_PEERS

    def body(*refs):
        src, out = refs[:n], refs[n:2 * n]
        send_sems, recv_sems, local_sems = refs[2 * n:]
        _handshake(SIBLING_AND_SAME_CORE)
        x, y, c = _position()
        me, sibling = (x, y, c), (x, y, 1 - c)

        def copy(t, j, block, to, from_input=False):
            dst = out[t].at[_slot(*block)]
            return pltpu.make_async_remote_copy(
                src_ref=src[t] if from_input else dst, dst_ref=dst, send_sem=send_sems.at[t * per + j],
                recv_sem=recv_sems.at[t * per + j], device_id=to, device_id_type=MESH_ID)

        mine = [pltpu.make_async_copy(src[t], out[t].at[_slot(*me)], local_sems.at[t]) for t in range(n)]
        for cp in mine:
            cp.start()
        sent = [copy(t, j, me, _flip(x, y, c, k), True) for t in range(n) for j, k in enumerate(SIBLING_AND_SAME_CORE)]
        for cp in sent:
            cp.start()
        for t in range(n):
            for j, k in enumerate(SAME_CORE):
                block = _flip(x, y, c, k)
                copy(t, 1 + j, block, me).wait_recv()
                forward = copy(t, 4 + j, block, sibling)
                forward.start()
                sent.append(forward)
        for t in range(n):
            copy(t, 0, sibling, me).wait_recv()
            for j, k in enumerate(SAME_CORE):
                copy(t, 4 + j, _flip(x, y, 1 - c, k), me).wait_recv()
        for cp in sent:
            cp.wait_send()
        for cp in mine:
            cp.wait()

    gathered = [jax.ShapeDtypeStruct((N_DEV,) + s.shape, s.dtype) for s in shards]
    return _sequencer(name, collective_id, n * per, body, shards, gathered)


ADAM_ROWS = 512
BF16_ROWS = 16


def _adam_update(w, g, m, v):
    m = ADAM_B1 * m + (1.0 - ADAM_B1) * g
    v = ADAM_B2 * v + (1.0 - ADAM_B2) * (g * g)
    m_hat = m / (1.0 - ADAM_B1 ** ADAM_STEP)
    v_hat = v / (1.0 - ADAM_B2 ** ADAM_STEP)
    delta = -ADAM_LR * (m_hat / (jnp.sqrt(v_hat) + ADAM_EPS) + ADAM_WD * w)
    return delta, m, v


def _sum_slots(ref):
    total = ref[0].astype(F32)
    for d in range(1, ref.shape[0]):
        total = total + ref[d].astype(F32)
    return total


def _adamw_sum(name, landed, w, m, v):
    layers, rows, cols = w.shape
    tiles = [t for t in range(ADAM_ROWS, 0, -BF16_ROWS) if rows % t == 0]
    tr = tiles[0] if tiles else rows
    nt = rows // tr

    def body(*refs):
        parts = refs[:layers]
        w_ref, m_ref, v_ref, g_ref, d_ref, nm_ref, nv_ref = refs[layers:]
        layer = pl.program_id(0)
        g = _sum_slots(parts[0])
        for q in range(1, layers):
            g = jnp.where(layer == q, _sum_slots(parts[q]), g)
        delta, new_m, new_v = _adam_update(w_ref[...], g, m_ref[...], v_ref[...])
        g_ref[...] = g
        d_ref[...] = delta
        nm_ref[...] = new_m
        nv_ref[...] = new_v

    def part_spec(q):
        return _spec((N_CHIPS, tr, cols), lambda l, i: (0, jnp.where(l == q, i, jnp.where(l < q, 0, nt - 1)), 0))

    tile = _spec((None, tr, cols), lambda l, i: (l, i, 0))
    out = jax.ShapeDtypeStruct((layers, rows, cols), F32)
    return pl.pallas_call(
        body, name=name, grid=(layers, nt), in_specs=[part_spec(q) for q in range(layers)] + [tile] * 3,
        out_specs=[tile] * 4, out_shape=[out] * 4, compiler_params=_params(("arbitrary", "arbitrary")),
    )(*landed, w, m, v)


def _sum_small(landed):
    def body(in_ref, out_ref):
        out_ref[...] = _sum_slots(in_ref)

    return pl.pallas_call(body, name="small_grad_sum", out_shape=jax.ShapeDtypeStruct(landed.shape[1:], F32))(landed)


def _adamw_small(arrays):
    n = len(arrays)

    def body(*refs):
        for i in range(n):
            g_ref, w_ref, m_ref, v_ref = refs[4 * i:4 * i + 4]
            d_ref, nm_ref, nv_ref = refs[4 * n + 3 * i:4 * n + 3 * i + 3]
            d_ref[...], nm_ref[...], nv_ref[...] = _adam_update(w_ref[...], g_ref[...], m_ref[...], v_ref[...])

    out = [jax.ShapeDtypeStruct(w.shape, F32) for _, w, _, _ in arrays for _ in range(3)]
    flat = pl.pallas_call(body, name="adam_small", out_shape=out)(*[a for group in arrays for a in group])
    return [tuple(flat[3 * i:3 * i + 3]) for i in range(n)]


LANES = 128
SUBLANES = 8
F_CONV_SHARD = D_FF // N_DEV
GATE_SHARD = KEY_DIM // N_DEV
NORM_SHARD = D_MODEL // N_DEV


def _tile_rows(a):
    flat = a.reshape(-1)
    size = -(-flat.shape[0] // (SUBLANES * LANES)) * SUBLANES * LANES
    return jnp.pad(flat, (0, size - flat.shape[0])).reshape(-1, LANES)


def _pack_rows(pieces):
    return jnp.concatenate([_tile_rows(p) for p in pieces], axis=0)


def _unpack_rows(packed, shapes):
    out, row = [], 0
    for shape in shapes:
        size = 1
        for s in shape:
            size *= s
        rows = -(-size // (SUBLANES * LANES)) * SUBLANES
        piece = packed[..., row:row + rows, :]
        out.append(piece.reshape(piece.shape[:-2] + (rows * LANES,))[..., :size])
        row += rows
    return out


SMALL_SHARDS = ((GATE_RANK, GATE_SHARD), (1, NORM_SHARD), (3, NORM_SHARD), (2, 3, F_CONV_SHARD))


def _unpack_small_shards(g):
    gate, b_norm, b_conv, f_conv = _unpack_rows(g, SMALL_SHARDS)
    gate = gate.reshape(N_DEV, GATE_RANK, GATE_SHARD).transpose(1, 0, 2).reshape(GATE_RANK, KEY_DIM)
    b_norm = b_norm.reshape(1, D_MODEL)
    b_conv = b_conv.reshape(N_DEV, 3, NORM_SHARD).transpose(1, 0, 2).reshape(3, D_MODEL)
    f_conv = f_conv.reshape(N_DEV, 2, 3, F_CONV_SHARD).transpose(1, 2, 0, 3).reshape(2, 3, D_FF)
    return gate, b_norm, b_conv, f_conv


SMALL_LAYOUT = (("a_norm", (1, D_MODEL)), ("a_w_gate_up", (GATE_RANK, KEY_DIM)), ("a_b_gate", (1, KEY_DIM)), ("a_gn", (1, VALUE_DIM)),
                ("b_norm", (1, D_MODEL)), ("b_conv", (3, D_MODEL)), ("f_norm0", (1, D_MODEL)), ("f_norm1", (1, D_MODEL)),
                ("f_conv0", (3, D_FF)), ("f_conv1", (3, D_FF)), ("final_norm", (1, D_MODEL)), ("loss", (1, LANES)))


def _pack_small_grads(g):
    full = dict(g)
    full["a_w_gate_up"] = g["a_w_gate_up"][:GATE_RANK]
    for layer in range(2):
        full[f"f_norm{layer}"] = g["f_norm"][layer]
        full[f"f_conv{layer}"] = g["f_conv"][layer]
    return _pack_rows([full[name] for name, _ in SMALL_LAYOUT])


def _unpack_small_grads(packed):
    pieces = _unpack_rows(packed, [shape for _, shape in SMALL_LAYOUT])
    out = {name: piece.reshape(shape) for (name, shape), piece in zip(SMALL_LAYOUT, pieces)}
    out["f_norm"] = jnp.stack([out["f_norm0"][0], out["f_norm1"][0]])
    out["f_conv"] = jnp.stack([out["f_conv0"], out["f_conv1"]])
    return out


def kernel(x, a_norm, a_w_in, a_w_gate_up, a_b_gate, a_gn, a_w_out, b_norm, b_w_in, b_conv, b_w_out, f_norm, f_w_up, f_conv, f_w_down, final_norm, loss_target, m_a_norm, m_a_w_in, m_a_w_gate_up, m_a_b_gate, m_a_gn, m_a_w_out, m_b_norm, m_b_w_in, m_b_conv, m_b_w_out, m_f_norm, m_f_w_up, m_f_conv, m_f_w_down, m_final_norm, v_a_norm, v_a_w_in, v_a_w_gate_up, v_a_b_gate, v_a_gn, v_a_w_out, v_b_norm, v_b_w_in, v_b_conv, v_b_w_out, v_f_norm, v_f_w_up, v_f_conv, v_f_w_down, v_final_norm):
    my_slot = _slot(*_position())

    transposed = lambda w: jnp.swapaxes(w, 1, 2)
    a_transposed = lambda w: w.reshape(D_MODEL, A_SHARD).T.reshape(1, A_SHARD, D_MODEL)
    a_w_in_t, f_w_up_t = a_transposed(a_w_in), transposed(f_w_up)
    first = _all_gather("weight_gather", [a_w_in_t[0].astype(BF16), a_w_out[0].astype(BF16),
                                          _pack_rows([a_w_gate_up[0], b_norm, b_conv[0], f_conv])])
    gathers, small_shards = {}, first[2]
    later = (("f0", f_w_up_t[0], f_w_down[0]), ("b", b_w_in[0], b_w_out[0]), ("f1", f_w_up_t[1], f_w_down[1]))
    for collective_id, (group, w_in, w_out) in enumerate(later):
        w_in, w_out, small_shards = lax.optimization_barrier((w_in.astype(BF16), w_out.astype(BF16), small_shards))
        gathers[group] = _sequencer_gather(f"gather_{group}", collective_id, [w_in, w_out])
    gate_full, b_norm_full, b_conv_full, f_conv_full = _unpack_small_shards(small_shards)
    a_w_in_full = jnp.pad(first[0].reshape(PROJ_A, D_MODEL), ((0, PROJ_A_PAD - PROJ_A), (0, 0)))
    weights = dict(
        a_norm=a_norm, a_w_gate_up=jnp.pad(gate_full, ((0, GATE_PAD - GATE_RANK), (0, 0))).astype(BF16), a_b_gate=a_b_gate,
        a_gn=a_gn, b_norm=b_norm_full, b_conv=b_conv_full, f_norm=f_norm, f_conv=f_conv_full,
        final_norm=final_norm.reshape(1, D_MODEL))

    def fetch(group, after):
        if group == "a":
            return a_w_in_full, first[1].reshape(D_MODEL, D_MODEL)
        w_in, w_out = gathers[group]
        if group == "b":
            return w_in, w_out.reshape(D_MODEL, D_MODEL)
        return w_in.reshape(2, D_FF, D_MODEL), w_out.reshape(D_FF, D_MODEL)

    exchanges, pending = {}, []
    exchange_ids = dict(b=3, f0=4, a=5)
    side = lax.axis_index("c").astype(jnp.int32).reshape(1)

    def emit(group, parts, received, carry):
        sums = [_pair_add(f"pair_add_{group}_{i}", part, got, side) for i, (part, got) in enumerate(zip(parts, received))]
        carry, *sums = lax.optimization_barrier((carry, *sums))
        pending.extend(sums)
        if group != "f1":
            after = list(exchanges.values())[-1][:1] if exchanges else ()
            exchanges[group] = _sequencer_exchange(f"grads_{group}", exchange_ids[group], list(pending), after)
            pending.clear()
        return carry

    dx, g = _local_step(x[0], loss_target[0], weights, fetch, emit)

    (up1, down1, d_b_in, d_b_out), (up0, down0), (d_a_in, d_a_out) = (exchanges[group] for group in ("b", "f0", "a"))
    back = lambda results: tuple(transposed(r) for r in results)
    big = dict(
        b_w_in=_adamw_sum("adam_b_w_in", [d_b_in], b_w_in, m_b_w_in, v_b_w_in),
        b_w_out=_adamw_sum("adam_b_w_out", [d_b_out], b_w_out, m_b_w_out, v_b_w_out),
        f_w_up=back(_adamw_sum("adam_f_w_up", [up0, up1], f_w_up_t, transposed(m_f_w_up), transposed(v_f_w_up))),
        f_w_down=_adamw_sum("adam_f_w_down", [down0, down1], f_w_down, m_f_w_down, v_f_w_down))
    small_packed, *updated = lax.optimization_barrier((_pack_small_grads(g), *big["f_w_down"]))
    big["f_w_down"] = tuple(updated)
    small_landed = _all_gather("small_grad_gather", [small_packed])[0]
    big.update(
        a_w_in=tuple(r.reshape(A_SHARD, D_MODEL).T.reshape(1, D_MODEL, A_SHARD) for r in _adamw_sum(
            "adam_a_w_in", [d_a_in], a_w_in_t, a_transposed(m_a_w_in), a_transposed(v_a_w_in))),
        a_w_out=_adamw_sum("adam_a_w_out", [d_a_out], a_w_out, m_a_w_out, v_a_w_out))
    small_g = _unpack_small_grads(_sum_small(small_landed))
    loss = small_g["loss"][0, 0]
    small_g["a_w_gate_up"] = lax.dynamic_slice_in_dim(small_g["a_w_gate_up"], my_slot * GATE_SHARD, GATE_SHARD, axis=1)
    small_g["b_norm"] = lax.dynamic_slice_in_dim(small_g["b_norm"], my_slot * NORM_SHARD, NORM_SHARD, axis=1)
    small_g["b_conv"] = lax.dynamic_slice_in_dim(small_g["b_conv"], my_slot * NORM_SHARD, NORM_SHARD, axis=1)
    small_g["f_conv"] = lax.dynamic_slice_in_dim(small_g["f_conv"], my_slot * F_CONV_SHARD, F_CONV_SHARD, axis=2)
    small_w = dict(
        a_norm=(a_norm, m_a_norm, v_a_norm), a_w_gate_up=(a_w_gate_up, m_a_w_gate_up, v_a_w_gate_up),
        a_b_gate=(a_b_gate, m_a_b_gate, v_a_b_gate), a_gn=(a_gn, m_a_gn, v_a_gn), b_norm=(b_norm, m_b_norm, v_b_norm),
        b_conv=(b_conv, m_b_conv, v_b_conv), f_norm=(f_norm, m_f_norm, v_f_norm), f_conv=(f_conv, m_f_conv, v_f_conv),
        final_norm=(final_norm, m_final_norm, v_final_norm))
    two_d = lambda a: a.reshape(-1, a.shape[-1])
    updates = _adamw_small([tuple(two_d(a.reshape(w.shape)) for a in (small_g[name], w, m, v)) for name, (w, m, v) in small_w.items()])
    small = {}
    for (name, (w, _, _)), update in zip(small_w.items(), updates):
        small[name] = (small_g[name].reshape(w.shape),) + tuple(u.reshape(w.shape) for u in update)

    order = ["a_norm", "a_w_in", "a_w_gate_up", "a_b_gate", "a_gn", "a_w_out", "b_norm", "b_w_in", "b_conv", "b_w_out",
             "f_norm", "f_w_up", "f_conv", "f_w_down", "final_norm"]
    results = {**big, **small}
    outputs = [loss, dx.reshape(1, SEQ, D_MODEL)]
    for kind in range(4):
        outputs += [results[name][kind] for name in order]
    return tuple(outputs)
```

```python
import jax
import jax.numpy as jnp
from jax import lax
from jax.experimental import pallas as pl
from jax.experimental.pallas import tpu as pltpu
from jax.experimental.pallas import tpu_sc as plsc

F32 = jnp.float32
BF16 = jnp.bfloat16

N_DEV = 8
SEQ = 2048
D_MODEL = 1024
CHUNK = 64
N_CHUNKS = SEQ // CHUNK
RMS_EPS = 1e-6
GLA_HEADS = 4
KEY_DIM = 512
VALUE_DIM = 1024
HEAD_K = KEY_DIM // GLA_HEADS
HEAD_V = VALUE_DIM // GLA_HEADS
GATE_RANK = 16
GATE_PAD = 128
GATE_NORMALIZER = 16.0
PROJ_A = 2 * KEY_DIM + 2 * VALUE_DIM + GATE_RANK
PROJ_A_PAD = 2 * KEY_DIM + 2 * VALUE_DIM + GATE_PAD
A_SHARD = PROJ_A // N_DEV
B_SHARD = 3 * D_MODEL // N_DEV
D_FF = 2816
ADAM_LR = 0.001
ADAM_B1 = 0.9
ADAM_B2 = 0.999
ADAM_EPS = 1e-08
ADAM_WD = 0.01
ADAM_STEP = 10
MESH_AXES = ("x", "y", "c")

VMEM_LIMIT = 56 * 1024 * 1024
ROW_CHUNK = 256
HALO = 16


def _params(sem=None, vmem=VMEM_LIMIT):
    return pltpu.CompilerParams(dimension_semantics=sem, vmem_limit_bytes=vmem)


NN = ((1,), (0,))
NT = ((1,), (1,))
TN = ((0,), (0,))


def _matmul(name, a, a_spec, b, b_spec, dims, grid, out_shape, out_spec, k_blocks=None, a_block_cols=None, res=None,
            res_spec=None, transpose_out=False, norm=None, swap=()):
    has_res = res is not None
    n_swap = len(swap)

    def body(*refs):
        a_ref, b_ref = refs[0], refs[1]
        r_ref = refs[2] if has_res else None

        def product(lhs, rhs):
            return lax.dot_general(lhs.astype(BF16), rhs, (dims, ((), ())), preferred_element_type=F32)

        if k_blocks is None:
            v = product(a_ref[...], b_ref[...])
        else:
            v = None
            for k in range(k_blocks):
                lhs = a_ref[k] if a_block_cols is None else a_ref[:, k * a_block_cols:(k + 1) * a_block_cols]
                p = product(lhs, b_ref[k])
                v = p if v is None else v + p
        if transpose_out:
            v = v.T
        if has_res:
            v = v + r_ref[...]
        if norm is None:
            o_ref = refs[2 + has_res]
            o_ref[...] = v.astype(o_ref.dtype)
            return
        n_in = 5 + has_res
        x_ref, g_ref, dxi_ref = refs[2 + has_res:n_in]
        dx_ref, dx16_ref, dg_ref = refs[n_in + n_swap:n_in + n_swap + 3]
        if n_swap:
            copies = _pair_copies(refs[n_in:n_in + n_swap], refs[n_in + n_swap + 3:n_in + 2 * n_swap + 3], *refs[-2:])

            @pl.when(pl.program_id(0) == 0)
            def _():
                for send, _ in copies:
                    send.start()

            @pl.when(pl.program_id(0) == grid[0] - 1)
            def _():
                for send, arrival in copies:
                    arrival.wait_recv()
                    send.wait_send()

        dx, dg = _norm_bwd_rows(x_ref[...], g_ref[...], v)
        dx = dxi_ref[...] + dx
        dx_ref[...] = dx
        dx16_ref[...] = dx.astype(BF16)

        @pl.when(pl.program_id(0) == 0)
        def _():
            dg_ref[...] = dg

        @pl.when(pl.program_id(0) > 0)
        def _():
            dg_ref[...] += dg

    operands = [a, b] + ([res] if has_res else [])
    in_specs = [a_spec, b_spec] + ([res_spec] if has_res else [])
    semantics = ("parallel",) * len(grid)
    scratch = []
    if norm is not None:
        vec = _spec((1, D_MODEL), lambda i: (0, 0))
        any_space = pl.BlockSpec(memory_space=pl.ANY)
        operands += list(norm) + list(swap)
        in_specs += [out_spec, vec, out_spec] + [any_space] * n_swap
        out_shape = [_act(dtype=F32), _act(), jax.ShapeDtypeStruct((1, D_MODEL), F32)]
        out_shape += [jax.ShapeDtypeStruct((N_DEV // 2,) + p.shape[1:], p.dtype) for p in swap]
        out_spec = [out_spec, out_spec, vec] + [any_space] * n_swap
        semantics = ("arbitrary",)
        if n_swap:
            scratch = [pltpu.SemaphoreType.DMA((n_swap, N_DEV // 2))] * 2
    return pl.pallas_call(
        body, name=name, grid=grid, in_specs=in_specs, out_specs=out_spec, out_shape=out_shape, scratch_shapes=scratch,
        compiler_params=_params(semantics),
    )(*operands)


def _resident(shape):
    return pl.BlockSpec(shape, lambda *_: (0,) * len(shape), pipeline_mode=pl.Buffered(1))


TM = 512
N_TM = SEQ // TM
PA_TILE = 640
N_PA = PROJ_A_PAD // PA_TILE
OUT_TILE = 256


def _spec(shape, fn):
    return pl.BlockSpec(shape, fn)


def _act(shape=(SEQ, D_MODEL), dtype=BF16):
    return jax.ShapeDtypeStruct(shape, dtype)


def _proj_rows_nt(name, h, wt, n_tile):
    n = wt.shape[0]
    return _matmul(name, h, _resident((SEQ, D_MODEL)), wt, _spec((n_tile, D_MODEL), lambda j: (j, 0)), NT,
                   (n // n_tile,), _act((SEQ, n)), _spec((SEQ, n_tile), lambda j: (0, j)))


def _proj_cols_nn(name, h, w_blocks):
    nb, _, n = w_blocks.shape
    return _matmul(name, h, _resident((SEQ, D_MODEL)), w_blocks, _spec((None, D_MODEL, n), lambda j: (j, 0, 0)),
                   NN, (nb,), _act((SEQ, nb * n)), _spec((SEQ, n), lambda j: (0, j)))


def _square(name, a, w, dims, x=None):
    row = _spec((TM, D_MODEL), lambda i: (i, 0))
    return _matmul(name, a, row, w, _resident((D_MODEL, D_MODEL)), dims, (N_TM,),
                   _act(dtype=F32 if x is not None else BF16), row, res=x, res_spec=row if x is not None else None)


def _sum_blocks_nn(name, a_blocks, w_blocks, x=None, norm=None, swap=()):
    nb, _, n = a_blocks.shape
    row = _spec((TM, D_MODEL), lambda i: (i, 0))
    return _matmul(name, a_blocks, _spec((nb, TM, n), lambda i: (0, i, 0)), w_blocks, _resident((nb, n, D_MODEL)),
                   NN, (N_TM,), _act(dtype=F32), row, k_blocks=nb, res=x, res_spec=row if x is not None else None, norm=norm, swap=swap)


def _sum_cols_nt(name, d, w_blocks, norm=None, swap=()):
    nb, _, n = w_blocks.shape
    return _matmul(name, d, _spec((TM, nb * n), lambda i: (i, 0)), w_blocks, _resident((nb, D_MODEL, n)), NT,
                   (N_TM,), _act(dtype=F32), _spec((TM, D_MODEL), lambda i: (i, 0)), k_blocks=nb, a_block_cols=n, norm=norm, swap=swap)


def _wide_nn(name, d, wt, x=None, norm=None, swap=()):
    n = wt.shape[0]
    row = _spec((TM, D_MODEL), lambda i: (i, 0))
    return _matmul(name, d, _spec((TM, n), lambda i: (i, 0)), wt, _resident((n, D_MODEL)), NN, (N_TM,),
                   _act(dtype=F32), row, res=x, res_spec=row if x is not None else None, norm=norm, swap=swap)


def _wide_nt(name, d, w):
    n = w.shape[0]
    return _matmul(name, d, _spec((TM, D_MODEL), lambda i: (i, 0)), w, _resident((n, D_MODEL)), NT, (N_TM,),
                   _act((SEQ, n)), _spec((TM, n), lambda i: (i, 0)))


def _proj_halves_nt(name, h, wt):
    _, n, _ = wt.shape
    return _matmul(name, h, _spec((TM, D_MODEL), lambda p, i: (i, 0)), wt, _spec((None, n, D_MODEL), lambda p, i: (p, 0, 0)), NT,
                   (2, N_TM), _act((2, SEQ, n)), _spec((None, TM, n), lambda p, i: (p, i, 0)))


def _wgrad_halves_tn(name, d, n_tile, h):
    _, _, n = d.shape
    return _matmul(name, d, _spec((None, SEQ, n_tile), lambda p, j: (p, 0, j)), h, _resident((SEQ, D_MODEL)), TN,
                   (2, n // n_tile), _act((2, n, D_MODEL)), _spec((None, n_tile, D_MODEL), lambda p, j: (p, j, 0)))


def _wgrad_cols_tn(name, d, n_tile, h):
    n = d.shape[1]
    return _matmul(name, d, _spec((SEQ, n_tile), lambda j: (0, j)), h, _resident((SEQ, D_MODEL)), TN,
                   (n // n_tile,), _act((n, D_MODEL)), _spec((n_tile, D_MODEL), lambda j: (j, 0)))


def _wgrad_cols_transposed_tn(name, h, d, n_tile):
    nb = d.shape[1] // n_tile
    return _matmul(name, d, _spec((SEQ, n_tile), lambda j: (0, j)), h, _resident((SEQ, D_MODEL)), TN, (nb,),
                   _act((nb, D_MODEL, n_tile)), _spec((None, D_MODEL, n_tile), lambda j: (j, 0, 0)), transpose_out=True)


NORM_ROWS = 512


def _rstd(x):
    return lax.rsqrt(jnp.mean(x * x, axis=-1, keepdims=True) + RMS_EPS)


def _norm_fwd(name, x, gamma):
    def body(x_ref, g_ref, h_ref):
        x = x_ref[...]
        h_ref[...] = (x * _rstd(x) * g_ref[...]).astype(BF16)

    row = _spec((NORM_ROWS, D_MODEL), lambda i: (i, 0))
    return pl.pallas_call(
        body, name=name, grid=(SEQ // NORM_ROWS,), in_specs=[row, _spec((1, D_MODEL), lambda i: (0, 0))], out_specs=row,
        out_shape=jax.ShapeDtypeStruct((SEQ, D_MODEL), BF16), compiler_params=_params(("parallel",)),
    )(x, gamma)


def _norm_bwd_rows(x, gamma, dh):
    r = _rstd(x)
    xh = x * r
    dxh = dh * gamma
    dx = r * (dxh - xh * jnp.mean(dxh * xh, axis=-1, keepdims=True))
    return dx, jnp.sum(dh * xh, axis=0, keepdims=True)


def _loss_head(x, gamma, target):
    def body(x_ref, g_ref, t_ref, loss_ref, dx_ref, dx16_ref, dg_ref):
        x = x_ref[...]
        gamma = g_ref[...]
        err = x * _rstd(x) * gamma - t_ref[...]
        dy = err * (1.0 / D_MODEL)
        dx, dg = _norm_bwd_rows(x, gamma, dy)
        dx_ref[...] = dx
        dx16_ref[...] = dx.astype(BF16)
        part = 0.5 * jnp.sum(jnp.sum(err * err, axis=-1, keepdims=True) * (1.0 / D_MODEL), axis=0, keepdims=True)
        part = jnp.broadcast_to(part, loss_ref.shape)

        @pl.when(pl.program_id(0) == 0)
        def _():
            dg_ref[...] = dg
            loss_ref[...] = part

        @pl.when(pl.program_id(0) > 0)
        def _():
            dg_ref[...] += dg
            loss_ref[...] += part

    row = _spec((NORM_ROWS, D_MODEL), lambda i: (i, 0))
    vec = _spec((1, D_MODEL), lambda i: (0, 0))
    return pl.pallas_call(
        body, name="loss_head", grid=(SEQ // NORM_ROWS,), in_specs=[row, vec, row],
        out_specs=[_spec((1, 128), lambda i: (0, 0)), row, row, vec],
        out_shape=[jax.ShapeDtypeStruct((1, 128), F32), _act(dtype=F32), _act(), jax.ShapeDtypeStruct((1, D_MODEL), F32)],
        compiler_params=_params(("arbitrary",)),
    )(x, gamma, target)


def _sigmoid(x):
    return 1.0 / (1.0 + jnp.exp(-x))


def _rows(ref, c):
    return ref[pl.ds(pl.multiple_of(c * ROW_CHUNK, ROW_CHUNK), ROW_CHUNK), :].astype(F32)


def _rows_before(ref, c):
    start = pl.multiple_of(jnp.maximum(c * ROW_CHUNK - HALO, 0), HALO)
    rows = ref[pl.ds(start, HALO), :].astype(F32)
    return jnp.where(c > 0, rows, 0.0)


def _rows_after(ref, c, n_chunks):
    start = pl.multiple_of(jnp.minimum((c + 1) * ROW_CHUNK, SEQ - HALO), HALO)
    rows = ref[pl.ds(start, HALO), :].astype(F32)
    return jnp.where(c < n_chunks - 1, rows, 0.0)


def _shift_down(z, before, n):
    return pltpu.roll(jnp.concatenate([before, z], axis=0), n, 0)[HALO:]


def _shift_up(z, after, n):
    rows = z.shape[0]
    return pltpu.roll(jnp.concatenate([z, after], axis=0), rows + HALO - n, 0)[:rows]


def _conv_rows(z, before, w):
    z1 = _shift_down(z, before, 1)
    z2 = _shift_down(z, before, 2)
    return w[2:3, :] * z + w[1:2, :] * z1 + w[0:1, :] * z2, z1, z2


def _conv_t_rows(dy, after, w):
    return w[2:3, :] * dy + w[1:2, :] * _shift_up(dy, after, 1) + w[0:1, :] * _shift_up(dy, after, 2)


N_ROW_CHUNKS = SEQ // ROW_CHUNK


FF_COLS = 256
N_FF_COLS = D_FF // FF_COLS


def _ffn_mid_fwd(name, gu, conv_w):
    def body(gu_ref, w_ref, a_ref):
        w = w_ref[...]

        def chunk(c, carry):
            g = _rows(gu_ref.at[0], c)
            u = _rows(gu_ref.at[1], c)
            gc, _, _ = _conv_rows(g, _rows_before(gu_ref.at[0], c), w)
            a_ref[pl.ds(pl.multiple_of(c * ROW_CHUNK, ROW_CHUNK), ROW_CHUNK), :] = (gc * _sigmoid(gc) * u).astype(BF16)
            return carry

        lax.fori_loop(0, N_ROW_CHUNKS, chunk, 0)

    col = _spec((SEQ, FF_COLS), lambda j: (0, j))
    return pl.pallas_call(
        body, name=name, grid=(N_FF_COLS,),
        in_specs=[_spec((2, SEQ, FF_COLS), lambda j: (0, 0, j)), _spec((3, FF_COLS), lambda j: (0, j))], out_specs=col,
        out_shape=_act((SEQ, D_FF)), compiler_params=_params(("parallel",)),
    )(gu, conv_w)


def _ffn_mid_bwd(name, gu, conv_w, da):
    def body(gu_ref, w_ref, da_ref, dgu_ref, dw_ref, dgc_ref):
        w = w_ref[...]

        def first(c, acc):
            g = _rows(gu_ref.at[0], c)
            u = _rows(gu_ref.at[1], c)
            d = _rows(da_ref, c)
            gc, g1, g2 = _conv_rows(g, _rows_before(gu_ref.at[0], c), w)
            sg = _sigmoid(gc)
            rows = pl.ds(pl.multiple_of(c * ROW_CHUNK, ROW_CHUNK), ROW_CHUNK)
            dgu_ref[1, rows, :] = (d * gc * sg).astype(BF16)
            dgc = d * u * (sg * (1.0 + gc * (1.0 - sg)))
            dgc_ref[rows, :] = dgc
            return (acc[0] + jnp.sum(dgc * g2, axis=0, keepdims=True), acc[1] + jnp.sum(dgc * g1, axis=0, keepdims=True),
                    acc[2] + jnp.sum(dgc * g, axis=0, keepdims=True))

        zero = jnp.zeros((1, FF_COLS), F32)
        acc = lax.fori_loop(0, N_ROW_CHUNKS, first, (zero, zero, zero))
        for r in range(3):
            dw_ref[r:r + 1, :] = acc[r]

        def second(c, carry):
            dgc = _rows(dgc_ref, c)
            dg = _conv_t_rows(dgc, _rows_after(dgc_ref, c, N_ROW_CHUNKS), w)
            dgu_ref[0, pl.ds(pl.multiple_of(c * ROW_CHUNK, ROW_CHUNK), ROW_CHUNK), :] = dg.astype(BF16)
            return carry

        lax.fori_loop(0, N_ROW_CHUNKS, second, 0)

    pair = _spec((2, SEQ, FF_COLS), lambda j: (0, 0, j))
    wspec = _spec((3, FF_COLS), lambda j: (0, j))
    return pl.pallas_call(
        body, name=name, grid=(N_FF_COLS,), in_specs=[pair, wspec, _spec((SEQ, FF_COLS), lambda j: (0, j))],
        out_specs=[pair, wspec], out_shape=[_act((2, SEQ, D_FF)), jax.ShapeDtypeStruct((3, D_FF), F32)],
        scratch_shapes=[pltpu.VMEM((SEQ, FF_COLS), F32)],
        compiler_params=_params(("parallel",)),
    )(gu, conv_w, da)


SC_COLS = 256
N_SC = D_MODEL // SC_COLS


def _sc_specs():
    return [_spec((SEQ, SC_COLS), lambda j, part=part: (0, part * N_SC + j)) for part in range(3)]


def _sc_mid_fwd(p, conv_w):
    def body(b_ref, c_ref, h_ref, w_ref, y_ref):
        w = w_ref[...]

        def chunk(c, carry):
            z = _rows(c_ref, c) * _rows(h_ref, c)
            before = _rows_before(c_ref, c) * _rows_before(h_ref, c)
            zc, _, _ = _conv_rows(z, before, w)
            y_ref[pl.ds(pl.multiple_of(c * ROW_CHUNK, ROW_CHUNK), ROW_CHUNK), :] = (_rows(b_ref, c) * zc).astype(BF16)
            return carry

        lax.fori_loop(0, N_ROW_CHUNKS, chunk, 0)

    col = _spec((SEQ, SC_COLS), lambda j: (0, j))
    return pl.pallas_call(
        body, name="sc_mid_fwd", grid=(N_SC,), in_specs=_sc_specs() + [_spec((3, SC_COLS), lambda j: (0, j))], out_specs=col,
        out_shape=jax.ShapeDtypeStruct((SEQ, D_MODEL), BF16), compiler_params=_params(("parallel",)),
    )(p, p, p, conv_w)


def _sc_mid_bwd(p, conv_w, dy):
    def body(b_ref, c_ref, h_ref, w_ref, dy_ref, db_ref, dc_ref, dh_ref, dw_ref, dzc_ref):
        w = w_ref[...]

        def first(c, acc):
            z = _rows(c_ref, c) * _rows(h_ref, c)
            before = _rows_before(c_ref, c) * _rows_before(h_ref, c)
            zc, z1, z2 = _conv_rows(z, before, w)
            d = _rows(dy_ref, c)
            rows = pl.ds(pl.multiple_of(c * ROW_CHUNK, ROW_CHUNK), ROW_CHUNK)
            db_ref[rows, :] = (d * zc).astype(BF16)
            dzc = d * _rows(b_ref, c)
            dzc_ref[rows, :] = dzc
            return (acc[0] + jnp.sum(dzc * z2, axis=0, keepdims=True), acc[1] + jnp.sum(dzc * z1, axis=0, keepdims=True),
                    acc[2] + jnp.sum(dzc * z, axis=0, keepdims=True))

        zero = jnp.zeros((1, SC_COLS), F32)
        acc = lax.fori_loop(0, N_ROW_CHUNKS, first, (zero, zero, zero))
        for r in range(3):
            dw_ref[r:r + 1, :] = acc[r]

        def second(c, carry):
            dz = _conv_t_rows(_rows(dzc_ref, c), _rows_after(dzc_ref, c, N_ROW_CHUNKS), w)
            rows = pl.ds(pl.multiple_of(c * ROW_CHUNK, ROW_CHUNK), ROW_CHUNK)
            dc_ref[rows, :] = (dz * _rows(h_ref, c)).astype(BF16)
            dh_ref[rows, :] = (dz * _rows(c_ref, c)).astype(BF16)
            return carry

        lax.fori_loop(0, N_ROW_CHUNKS, second, 0)

    col = _spec((SEQ, SC_COLS), lambda j: (0, j))
    wspec = _spec((3, SC_COLS), lambda j: (0, j))
    act = jax.ShapeDtypeStruct((SEQ, D_MODEL), BF16)
    return pl.pallas_call(
        body, name="sc_mid_bwd", grid=(N_SC,), in_specs=_sc_specs() + [wspec, col], out_specs=[col, col, col, wspec],
        out_shape=[act, act, act, jax.ShapeDtypeStruct((3, D_MODEL), F32)],
        scratch_shapes=[pltpu.VMEM((SEQ, SC_COLS), F32)], compiler_params=_params(("parallel",)),
    )(p, p, p, conv_w, dy)


GLA_GROUP = 4
GLA_ROWS = GLA_GROUP * CHUNK
N_GROUPS = N_CHUNKS // GLA_GROUP
Q0, K0, V0, R0, G0 = 0, KEY_DIM, 2 * KEY_DIM, 2 * KEY_DIM + VALUE_DIM, 2 * KEY_DIM + 2 * VALUE_DIM


def _tri(strict):
    r = lax.broadcasted_iota(jnp.int32, (CHUNK, CHUNK), 0)
    c = lax.broadcasted_iota(jnp.int32, (CHUNK, CHUNK), 1)
    return jnp.where(c < r if strict else c <= r, 1.0, 0.0).astype(F32)


def _cumsum_rows(tri, x):
    return jnp.dot(tri, x, preferred_element_type=F32, precision=lax.Precision.HIGHEST)


def _gate_logits(gl, wgu, b_gate):
    return jnp.dot(gl, wgu, preferred_element_type=F32) + b_gate


def _log_decay(logits):
    return (jnp.minimum(logits, 0.0) - jnp.log(1.0 + jnp.exp(-jnp.abs(logits)))) * (1.0 / GATE_NORMALIZER)


def _head(x, h, width):
    return x[:, h * width:(h + 1) * width]


def _gla_fwd(proj, wgu, b_gate, gn):
    def body(p_ref, wgu_ref, b_ref, gn_ref, o_ref, og_ref, st_ref, state):
        @pl.when(pl.program_id(0) == 0)
        def _():
            state[...] = jnp.zeros_like(state)

        tri = _tri(False)
        la = _log_decay(_gate_logits(p_ref[:, G0:G0 + GATE_PAD], wgu_ref[...], b_ref[...]))
        for c in range(GLA_GROUP):
            rows = slice(c * CHUNK, (c + 1) * CHUNK)
            cum = _cumsum_rows(tri, la[rows])
            tot = cum[CHUNK - 1:CHUNK, :]
            kd = (p_ref[rows, K0:K0 + KEY_DIM].astype(F32) * jnp.exp(tot - cum)).astype(BF16)
            decay = jnp.exp(tot)
            q = (p_ref[rows, Q0:Q0 + KEY_DIM].astype(F32) * (HEAD_K ** -0.5)).astype(BF16)
            v = p_ref[rows, V0:V0 + VALUE_DIM]
            for h in range(GLA_HEADS):
                upd = lax.dot_general(_head(v, h, HEAD_V), _head(kd, h, HEAD_K), (TN, ((), ())), preferred_element_type=F32)
                s = state[h] * _head(decay, h, HEAD_K) + upd
                state[h] = s
                st_ref[c, h] = s
                o_ref[rows, h * HEAD_V:(h + 1) * HEAD_V] = lax.dot_general(
                    _head(q, h, HEAD_K), s.astype(BF16), (NT, ((), ())), preferred_element_type=F32)
        r = p_ref[:, R0:R0 + VALUE_DIM].astype(F32)
        gate = r * _sigmoid(r) * gn_ref[...]
        for h in range(GLA_HEADS):
            cols = slice(h * HEAD_V, (h + 1) * HEAD_V)
            o = o_ref[:, cols]
            og_ref[:, cols] = (o * _rstd(o) * gate[:, cols]).astype(BF16)

    rows = _spec((GLA_ROWS, VALUE_DIM), lambda i: (i, 0))
    const = lambda shape: _spec(shape, lambda i: (0,) * len(shape))
    return pl.pallas_call(
        body, name="gla_fwd", grid=(N_GROUPS,),
        in_specs=[_spec((GLA_ROWS, PROJ_A_PAD), lambda i: (i, 0)), const((GATE_PAD, KEY_DIM)), const((1, KEY_DIM)),
                  const((1, VALUE_DIM))],
        out_specs=[rows, rows, _spec((GLA_GROUP, GLA_HEADS, HEAD_V, HEAD_K), lambda i: (i, 0, 0, 0))],
        out_shape=[jax.ShapeDtypeStruct((SEQ, VALUE_DIM), F32), jax.ShapeDtypeStruct((SEQ, VALUE_DIM), BF16),
                   jax.ShapeDtypeStruct((N_CHUNKS, GLA_HEADS, HEAD_V, HEAD_K), F32)],
        scratch_shapes=[pltpu.VMEM((GLA_HEADS, HEAD_V, HEAD_K), F32)], compiler_params=_params(("arbitrary",)),
    )(proj, wgu, b_gate, gn)


def _gla_bwd(proj, wgu, b_gate, gn, o, states, dog):
    last = N_GROUPS - 1

    def body(p_ref, wgu_ref, b_ref, gn_ref, o_ref, st_ref, stp_ref, dog_ref, dp_ref, dwgu_ref, db_ref, dgn_ref, carry, do_buf):
        step = pl.program_id(0)

        @pl.when(step == 0)
        def _():
            carry[...] = jnp.zeros_like(carry)

        r = p_ref[:, R0:R0 + VALUE_DIM].astype(F32)
        sr = _sigmoid(r)
        silu = r * sr
        gn_row = gn_ref[...]
        dog_rows = dog_ref[...].astype(F32)
        dn = dog_rows * silu
        dgn_cols = []
        for h in range(GLA_HEADS):
            cols = slice(h * HEAD_V, (h + 1) * HEAD_V)
            oh = o_ref[:, cols]
            rs = _rstd(oh)
            ohat = oh * rs
            dn_h = dn[:, cols]
            dgn_cols.append(jnp.sum(dn_h * ohat, axis=0, keepdims=True))
            dohat = dn_h * gn_row[:, cols]
            do_buf[:, cols] = rs * (dohat - ohat * jnp.mean(dohat * ohat, axis=-1, keepdims=True))
            n_h = ohat * gn_row[:, cols]
            dp_ref[:, R0 + h * HEAD_V:R0 + (h + 1) * HEAD_V] = (
                dog_rows[:, cols] * n_h * (sr[:, cols] * (1.0 + r[:, cols] * (1.0 - sr[:, cols])))).astype(BF16)
        dgn = jnp.concatenate(dgn_cols, axis=1)

        tri = _tri(False)
        tri_strict = _tri(True)
        gl = p_ref[:, G0:G0 + GATE_PAD]
        logits = _gate_logits(gl, wgu_ref[...], b_ref[...])
        la = _log_decay(logits)
        dlogit_rows = []
        for c in reversed(range(GLA_GROUP)):
            rows = slice(c * CHUNK, (c + 1) * CHUNK)
            cum = _cumsum_rows(tri, la[rows])
            tot = cum[CHUNK - 1:CHUNK, :]
            fade = jnp.exp(tot - cum)
            k = p_ref[rows, K0:K0 + KEY_DIM].astype(F32)
            kd32 = k * fade
            kd = kd32.astype(BF16)
            decay = jnp.exp(tot)
            q = (p_ref[rows, Q0:Q0 + KEY_DIM].astype(F32) * (HEAD_K ** -0.5)).astype(BF16)
            v = p_ref[rows, V0:V0 + VALUE_DIM]
            do = do_buf[rows, :].astype(BF16)
            dkd_cols, ddecay_cols = [], []
            for h in range(GLA_HEADS):
                do_h = _head(do, h, HEAD_V)
                s = st_ref[c, h]
                dq = jnp.dot(do_h, s.astype(BF16), preferred_element_type=F32) * (HEAD_K ** -0.5)
                dp_ref[rows, Q0 + h * HEAD_K:Q0 + (h + 1) * HEAD_K] = dq.astype(BF16)
                g = carry[h] + lax.dot_general(do_h, _head(q, h, HEAD_K), (TN, ((), ())), preferred_element_type=F32)
                g16 = g.astype(BF16)
                dkd_cols.append(jnp.dot(_head(v, h, HEAD_V), g16, preferred_element_type=F32))
                dv = lax.dot_general(_head(kd, h, HEAD_K), g16, (NT, ((), ())), preferred_element_type=F32)
                dp_ref[rows, V0 + h * HEAD_V:V0 + (h + 1) * HEAD_V] = dv.astype(BF16)
                if c > 0:
                    s_prev = st_ref[c - 1, h]
                else:
                    s_prev = jnp.where(step < last, stp_ref[0, h], 0.0)
                ddecay_cols.append(jnp.sum(g * s_prev, axis=0, keepdims=True))
                carry[h] = g * _head(decay, h, HEAD_K)
            dkd = jnp.concatenate(dkd_cols, axis=1)
            ddecay = jnp.concatenate(ddecay_cols, axis=1)
            dp_ref[rows, K0:K0 + KEY_DIM] = (dkd * fade).astype(BF16)
            e = dkd * kd32
            dla = ddecay * decay + _cumsum_rows(tri_strict, e)
            dlogit_rows.append(dla * (1.0 / GATE_NORMALIZER) * (1.0 - _sigmoid(logits[rows])))
        dlogit = jnp.concatenate(dlogit_rows[::-1], axis=0)
        dlogit16 = dlogit.astype(BF16)
        dp_ref[:, G0:G0 + GATE_PAD] = lax.dot_general(
            dlogit16, wgu_ref[...], (NT, ((), ())), preferred_element_type=F32).astype(BF16)
        dwgu = lax.dot_general(gl, dlogit16, (TN, ((), ())), preferred_element_type=F32)
        db = jnp.sum(dlogit, axis=0, keepdims=True)

        @pl.when(step == 0)
        def _():
            dwgu_ref[...] = dwgu
            db_ref[...] = db
            dgn_ref[...] = dgn

        @pl.when(step > 0)
        def _():
            dwgu_ref[...] += dwgu
            db_ref[...] += db
            dgn_ref[...] += dgn

    rev = lambda i: (last - i, 0)
    rows = _spec((GLA_ROWS, VALUE_DIM), rev)
    const = lambda shape: _spec(shape, lambda i: (0,) * len(shape))
    st_shape = (GLA_HEADS, HEAD_V, HEAD_K)
    return pl.pallas_call(
        body, name="gla_bwd", grid=(N_GROUPS,),
        in_specs=[_spec((GLA_ROWS, PROJ_A_PAD), rev), const((GATE_PAD, KEY_DIM)), const((1, KEY_DIM)), const((1, VALUE_DIM)),
                  rows, _spec((GLA_GROUP,) + st_shape, lambda i: (last - i, 0, 0, 0)),
                  _spec((1,) + st_shape, lambda i: (jnp.maximum((last - i) * GLA_GROUP - 1, 0), 0, 0, 0)), rows],
        out_specs=[_spec((GLA_ROWS, PROJ_A_PAD), rev), const((GATE_PAD, KEY_DIM)), const((1, KEY_DIM)), const((1, VALUE_DIM))],
        out_shape=[jax.ShapeDtypeStruct((SEQ, PROJ_A_PAD), BF16), jax.ShapeDtypeStruct((GATE_PAD, KEY_DIM), F32),
                   jax.ShapeDtypeStruct((1, KEY_DIM), F32), jax.ShapeDtypeStruct((1, VALUE_DIM), F32)],
        scratch_shapes=[pltpu.VMEM(st_shape, F32), pltpu.VMEM((GLA_ROWS, VALUE_DIM), F32)],
        compiler_params=_params(("arbitrary",)),
    )(proj, wgu, b_gate, gn, o, states, states, dog)


WGRAD_FF_TILE = D_FF // 2


def _ffn_fwd(tag, x, gamma, w_up_t, conv_w, w_down):
    h = _norm_fwd(f"ffn{tag}_norm", x, gamma)
    gu = _proj_halves_nt(f"ffn{tag}_up", h, w_up_t)
    a = _ffn_mid_fwd(f"ffn{tag}_mid", gu, conv_w)
    return _wide_nn(f"ffn{tag}_down", a, w_down, x=x), (h, gu, a)


def _owner_blocks(d, rows=None):
    if rows is not None:
        d = d[:rows]
    return d.reshape((N_DEV, -1) + d.shape[-1:])


def _ffn_bwd(tag, x, gamma, w_up_t, conv_w, w_down, saved, dx, dx16, swap):
    h, gu, a = saved
    da = _wide_nt(f"ffn{tag}_da", dx16, w_down)
    d_w_down = _owner_blocks(_wgrad_cols_tn(f"ffn{tag}_dwdown", a, WGRAD_FF_TILE, dx16))
    dgu, d_conv = _ffn_mid_bwd(f"ffn{tag}_mid_bwd", gu, conv_w, da)
    d_w_up_t = _owner_blocks(_wgrad_halves_tn(f"ffn{tag}_dwup", dgu, WGRAD_FF_TILE, h))
    parts = (d_w_up_t, d_w_down)
    dx, dx16, d_gamma, *received = _sum_blocks_nn(
        f"ffn{tag}_dh", dgu, w_up_t, norm=(x, gamma, dx), swap=parts if swap else ())
    return dx, dx16, d_gamma, d_conv, parts, received


def _local_step(x, target, w, fetch=None, emit=None):
    if fetch is None:
        local = dict(a=(w.get("a_w_in"), w.get("a_w_out")), b=(w.get("b_w_in"), w.get("b_w_out")))
        for layer in range(2):
            local[f"f{layer}"] = (w["f_w_up"][layer], w["f_w_down"][layer]) if "f_w_up" in w else None
        fetch = lambda group, after: local[group]
    swap = emit is not None
    if emit is None:
        emit = lambda group, parts, received, dx: dx
    f_norm = (w["f_norm"][0:1], w["f_norm"][1:2])

    x0 = x
    a_w_in, a_w_out = fetch("a", x0)
    h0 = _norm_fwd("a_norm", x0, w["a_norm"])
    proj = _proj_rows_nt("a_in", h0, a_w_in, PA_TILE)
    o, og, states = _gla_fwd(proj, w["a_w_gate_up"], w["a_b_gate"], w["a_gn"])
    x1 = _square("a_out", og, a_w_out, NN, x0)
    up0, down0 = fetch("f0", x1)
    x2, ffn0 = _ffn_fwd(0, x1, f_norm[0], up0, w["f_conv"][0], down0)
    b_w_in, b_w_out = fetch("b", x2)
    h2 = _norm_fwd("b_norm", x2, w["b_norm"])
    p = _proj_cols_nn("b_in", h2, b_w_in)
    y = _sc_mid_fwd(p, w["b_conv"])
    x3 = _square("b_out", y, b_w_out, NN, x2)
    up1, down1 = fetch("f1", x3)
    x4, ffn1 = _ffn_fwd(1, x3, f_norm[1], up1, w["f_conv"][1], down1)
    loss, dx, dx16, d_final_norm = _loss_head(x4, w["final_norm"], target)

    dx, dx16, d_f_norm1, d_fconv1, parts_f1, got = _ffn_bwd(
        1, x3, f_norm[1], up1, w["f_conv"][1], down1, ffn1, dx, dx16, swap)
    dx16 = emit("f1", parts_f1, got, dx16)

    dy = _square("b_dy", dx16, b_w_out, NT)
    d_b_w_out = _owner_blocks(_wgrad_cols_tn("b_dwout", y, OUT_TILE, dx16))
    db, dc, dhh, d_b_conv = _sc_mid_bwd(p, w["b_conv"], dy)
    dp = jnp.concatenate([db, dc, dhh], axis=1)
    parts_b = (_wgrad_cols_transposed_tn("b_dwin", h2, dp, B_SHARD), d_b_w_out)
    dx, dx16, d_b_norm, *got = _sum_cols_nt("b_dh", dp, b_w_in, norm=(x2, w["b_norm"], dx), swap=parts_b if swap else ())
    dx16 = emit("b", parts_b, got, dx16)

    dx, dx16, d_f_norm0, d_fconv0, parts_f0, got = _ffn_bwd(
        0, x1, f_norm[0], up0, w["f_conv"][0], down0, ffn0, dx, dx16, swap)
    dx16 = emit("f0", parts_f0, got, dx16)

    dog = _square("a_dog", dx16, a_w_out, NT)
    d_a_w_out = _owner_blocks(_wgrad_cols_tn("a_dwout", og, OUT_TILE, dx16))
    dproj, d_wgu, d_b_gate, d_gn = _gla_bwd(proj, w["a_w_gate_up"], w["a_b_gate"], w["a_gn"], o, states, dog)
    parts_a = (_owner_blocks(_wgrad_cols_tn("a_dwin", dproj, PA_TILE, h0), PROJ_A), d_a_w_out)
    dx, _, d_a_norm, *got = _wide_nn("a_dh", dproj, a_w_in, norm=(x0, w["a_norm"], dx), swap=parts_a if swap else ())
    emit("a", parts_a, got, dx)

    grads = dict(
        a_norm=d_a_norm, a_w_in=parts_a[0], a_w_gate_up=d_wgu, a_b_gate=d_b_gate, a_gn=d_gn, a_w_out=parts_a[1],
        b_norm=d_b_norm, b_w_in=parts_b[0], b_conv=d_b_conv, b_w_out=parts_b[1],
        f_norm=(d_f_norm0, d_f_norm1), f_w_up=(parts_f0[0], parts_f1[0]), f_conv=(d_fconv0, d_fconv1),
        f_w_down=(parts_f0[1], parts_f1[1]), final_norm=d_final_norm)
    grads["loss"] = loss
    return dx, grads


MESH_ID = pl.DeviceIdType.MESH
ANY = pl.BlockSpec(memory_space=pl.ANY)
N_PEERS = N_DEV - 1


def _position():
    return lax.axis_index("x"), lax.axis_index("y"), lax.axis_index("c")


def _slot(px, py, pc):
    return 4 * px + 2 * py + pc


def _all_gather(name, shards):
    n = len(shards)

    def body(*refs):
        ins, outs = refs[:n], refs[n:2 * n]
        send_sems, recv_sems, local_sems = refs[2 * n:]
        x, y, c = _position()
        me, sibling = (x, y, c), (x, y, 1 - c)
        chips = [(1 - x, y), (x, 1 - y), (1 - x, 1 - y)]

        def copy(t, k, block, to, from_input=False):
            dst = outs[t].at[_slot(*block)]
            return pltpu.make_async_remote_copy(
                src_ref=ins[t] if from_input else dst, dst_ref=dst, send_sem=send_sems.at[t, k], recv_sem=recv_sems.at[t, k],
                device_id=to, device_id_type=MESH_ID)

        mine = [pltpu.make_async_copy(ins[t], outs[t].at[_slot(*me)], local_sems.at[t]) for t in range(n)]
        for cp in mine:
            cp.start()
        first = []
        for t in range(n):
            first.append(copy(t, 0, me, sibling, True))
            first += [copy(t, 1 + j, me, (*chip, c), True) for j, chip in enumerate(chips)]
        for cp in first:
            cp.start()
        passed = []
        for t in range(n):
            for j, chip in enumerate(chips):
                copy(t, 1 + j, (*chip, c), me).wait_recv()
                fwd = copy(t, 4 + j, (*chip, c), sibling)
                fwd.start()
                passed.append(fwd)
        for t in range(n):
            copy(t, 0, sibling, me).wait_recv()
            for j, chip in enumerate(chips):
                copy(t, 4 + j, (*chip, 1 - c), me).wait_recv()
        for cp in first + passed:
            cp.wait_send()
        for cp in mine:
            cp.wait()

    return pl.pallas_call(
        body, name=name, in_specs=[ANY] * n, out_specs=[ANY] * n,
        out_shape=[jax.ShapeDtypeStruct((N_DEV,) + s.shape, s.dtype) for s in shards],
        scratch_shapes=[pltpu.SemaphoreType.DMA((n, N_PEERS)), pltpu.SemaphoreType.DMA((n, N_PEERS)), pltpu.SemaphoreType.DMA((n,))],
    )(*shards)


SIBLING_AND_SAME_CORE = (1, 2, 4, 6)
SAME_CORE = (2, 4, 6)


def _flip(x, y, c, k):
    return x ^ (k >> 2), y ^ ((k >> 1) & 1), c ^ (k & 1)


N_CHIPS = N_DEV // 2


def _chip(px, py):
    return 2 * px + py


def _pair_copies(parts, received, send_sems, recv_sems):
    x, y, c = lax.axis_index("x"), lax.axis_index("y"), lax.axis_index("c")
    sibling = (x, y, 1 - c)
    copies = []
    for t in range(len(parts)):
        for q in range(N_DEV // 2):
            send = pltpu.make_async_remote_copy(
                src_ref=parts[t].at[2 * q + 1 - c], dst_ref=received[t].at[q], send_sem=send_sems.at[t, q],
                recv_sem=recv_sems.at[t, q], device_id=sibling, device_id_type=pl.DeviceIdType.MESH)
            landed = received[t].at[q]
            arrival = pltpu.make_async_remote_copy(
                src_ref=landed, dst_ref=landed, send_sem=send_sems.at[t, q], recv_sem=recv_sems.at[t, q],
                device_id=sibling, device_id_type=pl.DeviceIdType.MESH)
            copies.append((send, arrival))
    return copies


PAIR_ROWS = 1024


def _pair_add(name, part, received, side):
    _, rows, cols = part.shape
    tiles = [t for t in range(PAIR_ROWS, 0, -BF16_ROWS) if rows % t == 0]
    tr = tiles[0] if tiles else rows

    def body(side_ref, p_ref, r_ref, o_ref):
        o_ref[...] = (p_ref[...].astype(F32) + r_ref[...].astype(F32)).astype(BF16)

    tile = _spec((None, tr, cols), lambda q, i, side_ref: (q, i, 0))
    return pl.pallas_call(
        body, name=name,
        grid_spec=pltpu.PrefetchScalarGridSpec(
            num_scalar_prefetch=1, grid=(N_CHIPS, rows // tr),
            in_specs=[_spec((None, tr, cols), lambda q, i, side_ref: (2 * q + side_ref[0], i, 0)), tile], out_specs=tile),
        out_shape=jax.ShapeDtypeStruct((N_CHIPS, rows, cols), BF16), compiler_params=_params(("parallel", "parallel")),
    )(side, part, received)


def _send_copy(parts, landing, send_sems, recv_sems, t, s, k):
    x, y, c = _position()
    px, py, _ = _flip(x, y, c, k)
    return pltpu.make_async_remote_copy(
        src_ref=parts[t].at[_chip(px, py)], dst_ref=landing[t].at[_chip(x, y)], send_sem=send_sems.at[s],
        recv_sem=recv_sems.at[s], device_id=(px, py, c), device_id_type=MESH_ID)


def _send_arrival(landing, send_sems, recv_sems, t, s, k):
    x, y, c = _position()
    px, py, _ = _flip(x, y, c, k)
    landed = landing[t].at[_chip(px, py)]
    return pltpu.make_async_remote_copy(
        src_ref=landed, dst_ref=landed, send_sem=send_sems.at[s], recv_sem=recv_sems.at[s],
        device_id=(px, py, c), device_id_type=MESH_ID)


def _handshake(peers):
    x, y, c = _position()
    barrier = pltpu.get_barrier_semaphore()
    for k in peers:
        pl.semaphore_signal(barrier, inc=1, device_id=_flip(x, y, c, k), device_id_type=MESH_ID)
    pl.semaphore_wait(barrier, len(peers))


def _sequencer(name, collective_id, n_copies, body, operands, out_type):
    n_arrays = len(operands)
    return pl.kernel(
        body, out_type=out_type, mesh=plsc.ScalarSubcoreMesh(axis_name="sequencer", num_cores=1), name=name,
        scratch_types=(pltpu.SemaphoreType.DMA((n_copies,)), pltpu.SemaphoreType.DMA((n_copies,)),
                       pltpu.SemaphoreType.DMA((n_arrays,))),
        compiler_params=pltpu.CompilerParams(collective_id=collective_id))(*operands)


def _sequencer_exchange(name, collective_id, parts, after=()):
    n, n_peers, n_in = len(parts), len(SAME_CORE), len(parts) + len(after)

    def body(*refs):
        src, landing = refs[:n], refs[n_in:n_in + n]
        send_sems, recv_sems, local_sems = refs[n_in + n:]
        _handshake(SAME_CORE)
        x, y, _ = _position()
        mine = [pltpu.make_async_copy(src[t].at[_chip(x, y)], landing[t].at[_chip(x, y)], local_sems.at[t]) for t in range(n)]
        for cp in mine:
            cp.start()
        sent = [_send_copy(src, landing, send_sems, recv_sems, t, t * n_peers + j, k)
                for t in range(n) for j, k in enumerate(SAME_CORE)]
        for cp in sent:
            cp.start()
        for t in range(n):
            for j, k in enumerate(SAME_CORE):
                _send_arrival(landing, send_sems, recv_sems, t, t * n_peers + j, k).wait_recv()
        for cp in sent:
            cp.wait_send()
        for cp in mine:
            cp.wait()

    landing = [jax.ShapeDtypeStruct(p.shape, p.dtype) for p in parts]
    return _sequencer(name, collective_id, n * n_peers, body, list(parts) + list(after), landing)


def _sequencer_gather(name, collective_id, shards):
    n, per = len(shards), 8
    to_sibling, to_x, to_y, x_on_to_y, y_on_to_x, x_to_sibling, y_to_sibling, diagonal_to_sibling = range(per)

    def body(*refs):
        src, out = refs[:n], refs[n:2 * n]
        send_sems, recv_sems, local_sems = refs[2 * n:]
        _handshake((1, 2, 4))
        x, y, c = _position()
        me, sibling = (x, y, c), (x, y, 1 - c)
        x_side, y_side, diagonal = (1 - x, y), (x, 1 - y), (1 - x, 1 - y)

        def rows_of(t, half):
            half_rows = src[t].shape[0] // 2 // BF16_ROWS * BF16_ROWS
            return (pl.ds(0, src[t].shape[0]), pl.ds(0, half_rows), pl.ds(half_rows, src[t].shape[0] - half_rows))[half]

        def copy(t, j, block, to, half=0, from_input=False):
            dst = out[t].at[_slot(*block), rows_of(t, half)]
            return pltpu.make_async_remote_copy(
                src_ref=src[t] if from_input else dst, dst_ref=dst, send_sem=send_sems.at[t * per + j],
                recv_sem=recv_sems.at[t * per + j], device_id=to, device_id_type=MESH_ID)

        mine = [pltpu.make_async_copy(src[t], out[t].at[_slot(*me)], local_sems.at[t]) for t in range(n)]
        for cp in mine:
            cp.start()
        sent = []

        def start(cp):
            cp.start()
            sent.append(cp)

        for t in range(n):
            start(copy(t, to_sibling, me, sibling, from_input=True))
            start(copy(t, to_x, me, (*x_side, c), from_input=True))
            start(copy(t, to_y, me, (*y_side, c), from_input=True))
        for t in range(n):
            copy(t, to_x, (*x_side, c), me).wait_recv()
            start(copy(t, x_on_to_y, (*x_side, c), (*y_side, c), half=1))
            start(copy(t, x_to_sibling, (*x_side, c), sibling))
            copy(t, to_y, (*y_side, c), me).wait_recv()
            start(copy(t, y_on_to_x, (*y_side, c), (*x_side, c), half=2))
            start(copy(t, y_to_sibling, (*y_side, c), sibling))
        for t in range(n):
            copy(t, x_on_to_y, (*diagonal, c), me, half=1).wait_recv()
            copy(t, y_on_to_x, (*diagonal, c), me, half=2).wait_recv()
            start(copy(t, diagonal_to_sibling, (*diagonal, c), sibling))
        for t in range(n):
            copy(t, to_sibling, sibling, me).wait_recv()
            for j, chip in ((x_to_sibling, x_side), (y_to_sibling, y_side), (diagonal_to_sibling, diagonal)):
                copy(t, j, (*chip, 1 - c), me).wait_recv()
        for cp in sent:
            cp.wait_send()
        for cp in mine:
            cp.wait()

    gathered = [jax.ShapeDtypeStruct((N_DEV,) + s.shape, s.dtype) for s in shards]
    return _sequencer(name, collective_id, n * per, body, shards, gathered)


ADAM_ROWS = 512
BF16_ROWS = 16


def _adam_update(w, g, m, v):
    m = ADAM_B1 * m + (1.0 - ADAM_B1) * g
    v = ADAM_B2 * v + (1.0 - ADAM_B2) * (g * g)
    m_hat = m / (1.0 - ADAM_B1 ** ADAM_STEP)
    v_hat = v / (1.0 - ADAM_B2 ** ADAM_STEP)
    delta = -ADAM_LR * (m_hat / (jnp.sqrt(v_hat) + ADAM_EPS) + ADAM_WD * w)
    return delta, m, v


def _sum_slots(ref):
    total = ref[0].astype(F32)
    for d in range(1, ref.shape[0]):
        total = total + ref[d].astype(F32)
    return total


def _adamw_sum(name, landed, w, m, v):
    layers, rows, cols = w.shape
    tiles = [t for t in range(ADAM_ROWS, 0, -BF16_ROWS) if rows % t == 0]
    tr = tiles[0] if tiles else rows
    nt = rows // tr

    def body(*refs):
        parts = refs[:layers]
        w_ref, m_ref, v_ref, g_ref, d_ref, nm_ref, nv_ref = refs[layers:]
        layer = pl.program_id(0)
        g = _sum_slots(parts[0])
        for q in range(1, layers):
            g = jnp.where(layer == q, _sum_slots(parts[q]), g)
        delta, new_m, new_v = _adam_update(w_ref[...], g, m_ref[...], v_ref[...])
        g_ref[...] = g
        d_ref[...] = delta
        nm_ref[...] = new_m
        nv_ref[...] = new_v

    def part_spec(q):
        return _spec((N_CHIPS, tr, cols), lambda l, i: (0, jnp.where(l == q, i, jnp.where(l < q, 0, nt - 1)), 0))

    tile = _spec((None, tr, cols), lambda l, i: (l, i, 0))
    out = jax.ShapeDtypeStruct((layers, rows, cols), F32)
    return pl.pallas_call(
        body, name=name, grid=(layers, nt), in_specs=[part_spec(q) for q in range(layers)] + [tile] * 3,
        out_specs=[tile] * 4, out_shape=[out] * 4, compiler_params=_params(("arbitrary", "arbitrary")),
    )(*landed, w, m, v)


def _sum_small(landed):
    def body(in_ref, out_ref):
        out_ref[...] = _sum_slots(in_ref)

    return pl.pallas_call(body, name="small_grad_sum", out_shape=jax.ShapeDtypeStruct(landed.shape[1:], F32))(landed)


def _adamw_small(arrays):
    n = len(arrays)

    def body(*refs):
        for i in range(n):
            g_ref, w_ref, m_ref, v_ref = refs[4 * i:4 * i + 4]
            d_ref, nm_ref, nv_ref = refs[4 * n + 3 * i:4 * n + 3 * i + 3]
            d_ref[...], nm_ref[...], nv_ref[...] = _adam_update(w_ref[...], g_ref[...], m_ref[...], v_ref[...])

    out = [jax.ShapeDtypeStruct(w.shape, F32) for _, w, _, _ in arrays for _ in range(3)]
    flat = pl.pallas_call(body, name="adam_small", out_shape=out)(*[a for group in arrays for a in group])
    return [tuple(flat[3 * i:3 * i + 3]) for i in range(n)]


LANES = 128
SUBLANES = 8
F_CONV_SHARD = D_FF // N_DEV
GATE_SHARD = KEY_DIM // N_DEV
NORM_SHARD = D_MODEL // N_DEV


def _tile_rows(a):
    flat = a.reshape(-1)
    size = -(-flat.shape[0] // (SUBLANES * LANES)) * SUBLANES * LANES
    return jnp.pad(flat, (0, size - flat.shape[0])).reshape(-1, LANES)


def _pack_rows(pieces):
    return jnp.concatenate([_tile_rows(p) for p in pieces], axis=0)


def _unpack_rows(packed, shapes):
    out, row = [], 0
    for shape in shapes:
        size = 1
        for s in shape:
            size *= s
        rows = -(-size // (SUBLANES * LANES)) * SUBLANES
        piece = packed[..., row:row + rows, :]
        out.append(piece.reshape(piece.shape[:-2] + (rows * LANES,))[..., :size])
        row += rows
    return out


SMALL_SHARDS = ((GATE_RANK, GATE_SHARD), (1, NORM_SHARD), (3, NORM_SHARD), (2, 3, F_CONV_SHARD))


def _unpack_small_shards(g):
    gate, b_norm, b_conv, f_conv = _unpack_rows(g, SMALL_SHARDS)
    gate = gate.reshape(N_DEV, GATE_RANK, GATE_SHARD).transpose(1, 0, 2).reshape(GATE_RANK, KEY_DIM)
    b_norm = b_norm.reshape(1, D_MODEL)
    b_conv = b_conv.reshape(N_DEV, 3, NORM_SHARD).transpose(1, 0, 2).reshape(3, D_MODEL)
    f_conv = f_conv.reshape(N_DEV, 2, 3, F_CONV_SHARD).transpose(1, 2, 0, 3).reshape(2, 3, D_FF)
    return gate, b_norm, b_conv, f_conv


SMALL_LAYOUT = (("a_norm", (1, D_MODEL)), ("a_w_gate_up", (GATE_RANK, KEY_DIM)), ("a_b_gate", (1, KEY_DIM)), ("a_gn", (1, VALUE_DIM)),
                ("b_norm", (1, D_MODEL)), ("b_conv", (3, D_MODEL)), ("f_norm0", (1, D_MODEL)), ("f_norm1", (1, D_MODEL)),
                ("f_conv0", (3, D_FF)), ("f_conv1", (3, D_FF)), ("final_norm", (1, D_MODEL)), ("loss", (1, LANES)))


def _pack_small_grads(g):
    full = dict(g)
    full["a_w_gate_up"] = g["a_w_gate_up"][:GATE_RANK]
    for layer in range(2):
        full[f"f_norm{layer}"] = g["f_norm"][layer]
        full[f"f_conv{layer}"] = g["f_conv"][layer]
    return _pack_rows([full[name] for name, _ in SMALL_LAYOUT])


def _unpack_small_grads(packed):
    pieces = _unpack_rows(packed, [shape for _, shape in SMALL_LAYOUT])
    out = {name: piece.reshape(shape) for (name, shape), piece in zip(SMALL_LAYOUT, pieces)}
    out["f_norm"] = jnp.stack([out["f_norm0"][0], out["f_norm1"][0]])
    out["f_conv"] = jnp.stack([out["f_conv0"], out["f_conv1"]])
    return out


def kernel(x, a_norm, a_w_in, a_w_gate_up, a_b_gate, a_gn, a_w_out, b_norm, b_w_in, b_conv, b_w_out, f_norm, f_w_up, f_conv, f_w_down, final_norm, loss_target, m_a_norm, m_a_w_in, m_a_w_gate_up, m_a_b_gate, m_a_gn, m_a_w_out, m_b_norm, m_b_w_in, m_b_conv, m_b_w_out, m_f_norm, m_f_w_up, m_f_conv, m_f_w_down, m_final_norm, v_a_norm, v_a_w_in, v_a_w_gate_up, v_a_b_gate, v_a_gn, v_a_w_out, v_b_norm, v_b_w_in, v_b_conv, v_b_w_out, v_f_norm, v_f_w_up, v_f_conv, v_f_w_down, v_final_norm):
    my_slot = _slot(*_position())

    transposed = lambda w: jnp.swapaxes(w, 1, 2)
    a_transposed = lambda w: w.reshape(D_MODEL, A_SHARD).T.reshape(1, A_SHARD, D_MODEL)
    a_w_in_t, f_w_up_t = a_transposed(a_w_in), transposed(f_w_up)
    first = _all_gather("weight_gather", [a_w_in_t[0].astype(BF16), a_w_out[0].astype(BF16),
                                          _pack_rows([a_w_gate_up[0], b_norm, b_conv[0], f_conv])])
    gathers, small_shards = {}, first[2]
    later = (("f0", f_w_up_t[0], f_w_down[0]), ("b", b_w_in[0], b_w_out[0]), ("f1", f_w_up_t[1], f_w_down[1]))
    for collective_id, (group, w_in, w_out) in enumerate(later):
        w_in, w_out, small_shards = lax.optimization_barrier((w_in.astype(BF16), w_out.astype(BF16), small_shards))
        gathers[group] = _sequencer_gather(f"gather_{group}", collective_id, [w_in, w_out])
    gate_full, b_norm_full, b_conv_full, f_conv_full = _unpack_small_shards(small_shards)
    a_w_in_full = jnp.pad(first[0].reshape(PROJ_A, D_MODEL), ((0, PROJ_A_PAD - PROJ_A), (0, 0)))
    weights = dict(
        a_norm=a_norm, a_w_gate_up=jnp.pad(gate_full, ((0, GATE_PAD - GATE_RANK), (0, 0))).astype(BF16), a_b_gate=a_b_gate,
        a_gn=a_gn, b_norm=b_norm_full, b_conv=b_conv_full, f_norm=f_norm, f_conv=f_conv_full,
        final_norm=final_norm.reshape(1, D_MODEL))

    def fetch(group, after):
        if group == "a":
            return a_w_in_full, first[1].reshape(D_MODEL, D_MODEL)
        w_in, w_out = gathers[group]
        if group == "b":
            return w_in, w_out.reshape(D_MODEL, D_MODEL)
        return w_in.reshape(2, D_FF, D_MODEL), w_out.reshape(D_FF, D_MODEL)

    exchanges, pending = {}, []
    exchange_ids = dict(b=3, f0=4, a=5)
    side = lax.axis_index("c").astype(jnp.int32).reshape(1)

    def emit(group, parts, received, carry):
        sums = [_pair_add(f"pair_add_{group}_{i}", part, got, side) for i, (part, got) in enumerate(zip(parts, received))]
        carry, *sums = lax.optimization_barrier((carry, *sums))
        pending.extend(sums)
        if group != "f1":
            after = list(exchanges.values())[-1][:1] if exchanges else ()
            exchanges[group] = _sequencer_exchange(f"grads_{group}", exchange_ids[group], list(pending), after)
            pending.clear()
        return carry

    dx, g = _local_step(x[0], loss_target[0], weights, fetch, emit)

    (up1, down1, d_b_in, d_b_out), (up0, down0), (d_a_in, d_a_out) = (exchanges[group] for group in ("b", "f0", "a"))
    back = lambda results: tuple(transposed(r) for r in results)
    big = dict(
        b_w_in=_adamw_sum("adam_b_w_in", [d_b_in], b_w_in, m_b_w_in, v_b_w_in),
        b_w_out=_adamw_sum("adam_b_w_out", [d_b_out], b_w_out, m_b_w_out, v_b_w_out),
        f_w_up=back(_adamw_sum("adam_f_w_up", [up0, up1], f_w_up_t, transposed(m_f_w_up), transposed(v_f_w_up))),
        f_w_down=_adamw_sum("adam_f_w_down", [down0, down1], f_w_down, m_f_w_down, v_f_w_down))
    small_packed, *updated = lax.optimization_barrier((_pack_small_grads(g), *big["f_w_down"]))
    big["f_w_down"] = tuple(updated)
    small_landed = _all_gather("small_grad_gather", [small_packed])[0]
    big.update(
        a_w_in=tuple(r.reshape(A_SHARD, D_MODEL).T.reshape(1, D_MODEL, A_SHARD) for r in _adamw_sum(
            "adam_a_w_in", [d_a_in], a_w_in_t, a_transposed(m_a_w_in), a_transposed(v_a_w_in))),
        a_w_out=_adamw_sum("adam_a_w_out", [d_a_out], a_w_out, m_a_w_out, v_a_w_out))
    small_g = _unpack_small_grads(_sum_small(small_landed))
    loss = small_g["loss"][0, 0]
    small_g["a_w_gate_up"] = lax.dynamic_slice_in_dim(small_g["a_w_gate_up"], my_slot * GATE_SHARD, GATE_SHARD, axis=1)
    small_g["b_norm"] = lax.dynamic_slice_in_dim(small_g["b_norm"], my_slot * NORM_SHARD, NORM_SHARD, axis=1)
    small_g["b_conv"] = lax.dynamic_slice_in_dim(small_g["b_conv"], my_slot * NORM_SHARD, NORM_SHARD, axis=1)
    small_g["f_conv"] = lax.dynamic_slice_in_dim(small_g["f_conv"], my_slot * F_CONV_SHARD, F_CONV_SHARD, axis=2)
    small_w = dict(
        a_norm=(a_norm, m_a_norm, v_a_norm), a_w_gate_up=(a_w_gate_up, m_a_w_gate_up, v_a_w_gate_up),
        a_b_gate=(a_b_gate, m_a_b_gate, v_a_b_gate), a_gn=(a_gn, m_a_gn, v_a_gn), b_norm=(b_norm, m_b_norm, v_b_norm),
        b_conv=(b_conv, m_b_conv, v_b_conv), f_norm=(f_norm, m_f_norm, v_f_norm), f_conv=(f_conv, m_f_conv, v_f_conv),
        final_norm=(final_norm, m_final_norm, v_final_norm))
    two_d = lambda a: a.reshape(-1, a.shape[-1])
    updates = _adamw_small([tuple(two_d(a.reshape(w.shape)) for a in (small_g[name], w, m, v)) for name, (w, m, v) in small_w.items()])
    small = {}
    for (name, (w, _, _)), update in zip(small_w.items(), updates):
        small[name] = (small_g[name].reshape(w.shape),) + tuple(u.reshape(w.shape) for u in update)

    order = ["a_norm", "a_w_in", "a_w_gate_up", "a_b_gate", "a_gn", "a_w_out", "b_norm", "b_w_in", "b_conv", "b_w_out",
             "f_norm", "f_w_up", "f_conv", "f_w_down", "final_norm"]
    results = {**big, **small}
    outputs = [loss, dx.reshape(1, SEQ, D_MODEL)]
    for kind in range(4):
        outputs += [results[name][kind] for name in order]
    return tuple(outputs)
```

```python
import jax
import jax.numpy as jnp
from jax import lax
from jax.experimental import pallas as pl
from jax.experimental.pallas import tpu as pltpu
from jax.experimental.pallas import tpu_sc as plsc

F32 = jnp.float32
BF16 = jnp.bfloat16

N_DEV = 8
SEQ = 2048
D_MODEL = 1024
CHUNK = 64
N_CHUNKS = SEQ // CHUNK
RMS_EPS = 1e-6
GLA_HEADS = 4
KEY_DIM = 512
VALUE_DIM = 1024
HEAD_K = KEY_DIM // GLA_HEADS
HEAD_V = VALUE_DIM // GLA_HEADS
GATE_RANK = 16
GATE_PAD = 128
GATE_NORMALIZER = 16.0
PROJ_A = 2 * KEY_DIM + 2 * VALUE_DIM + GATE_RANK
PROJ_A_PAD = 2 * KEY_DIM + 2 * VALUE_DIM + GATE_PAD
A_SHARD = PROJ_A // N_DEV
B_SHARD = 3 * D_MODEL // N_DEV
D_FF = 2816
ADAM_LR = 0.001
ADAM_B1 = 0.9
ADAM_B2 = 0.999
ADAM_EPS = 1e-08
ADAM_WD = 0.01
ADAM_STEP = 10
MESH_AXES = ("x", "y", "c")

VMEM_LIMIT = 56 * 1024 * 1024
ROW_CHUNK = 256
HALO = 16


def _params(sem=None, vmem=VMEM_LIMIT):
    return pltpu.CompilerParams(dimension_semantics=sem, vmem_limit_bytes=vmem)


NN = ((1,), (0,))
NT = ((1,), (1,))
TN = ((0,), (0,))


def _matmul(name, a, a_spec, b, b_spec, dims, grid, out_shape, out_spec, k_blocks=None, a_block_cols=None, res=None,
            res_spec=None, transpose_out=False, norm=None, swap=()):
    has_res = res is not None
    n_swap = len(swap)

    def body(*refs):
        a_ref, b_ref = refs[0], refs[1]
        r_ref = refs[2] if has_res else None

        def product(lhs, rhs):
            return lax.dot_general(lhs.astype(BF16), rhs, (dims, ((), ())), preferred_element_type=F32)

        if k_blocks is None:
            v = product(a_ref[...], b_ref[...])
        else:
            v = None
            for k in range(k_blocks):
                lhs = a_ref[k] if a_block_cols is None else a_ref[:, k * a_block_cols:(k + 1) * a_block_cols]
                p = product(lhs, b_ref[k])
                v = p if v is None else v + p
        if transpose_out:
            v = v.T
        if has_res:
            v = v + r_ref[...]
        if norm is None:
            o_ref = refs[2 + has_res]
            o_ref[...] = v.astype(o_ref.dtype)
            return
        n_in = 5 + has_res
        x_ref, g_ref, dxi_ref = refs[2 + has_res:n_in]
        dx_ref, dx16_ref, dg_ref = refs[n_in + n_swap:n_in + n_swap + 3]
        if n_swap:
            copies = _pair_copies(refs[n_in:n_in + n_swap], refs[n_in + n_swap + 3:n_in + 2 * n_swap + 3], *refs[-2:])

            @pl.when(pl.program_id(0) == 0)
            def _():
                for send, _ in copies:
                    send.start()

            @pl.when(pl.program_id(0) == grid[0] - 1)
            def _():
                for send, arrival in copies:
                    arrival.wait_recv()
                    send.wait_send()

        dx, dg = _norm_bwd_rows(x_ref[...], g_ref[...], v)
        dx = dxi_ref[...] + dx
        dx_ref[...] = dx
        dx16_ref[...] = dx.astype(BF16)

        @pl.when(pl.program_id(0) == 0)
        def _():
            dg_ref[...] = dg

        @pl.when(pl.program_id(0) > 0)
        def _():
            dg_ref[...] += dg

    operands = [a, b] + ([res] if has_res else [])
    in_specs = [a_spec, b_spec] + ([res_spec] if has_res else [])
    semantics = ("parallel",) * len(grid)
    scratch = []
    if norm is not None:
        vec = _spec((1, D_MODEL), lambda i: (0, 0))
        any_space = pl.BlockSpec(memory_space=pl.ANY)
        operands += list(norm) + list(swap)
        in_specs += [out_spec, vec, out_spec] + [any_space] * n_swap
        out_shape = [_act(dtype=F32), _act(), jax.ShapeDtypeStruct((1, D_MODEL), F32)]
        out_shape += [jax.ShapeDtypeStruct((N_DEV // 2,) + p.shape[1:], p.dtype) for p in swap]
        out_spec = [out_spec, out_spec, vec] + [any_space] * n_swap
        semantics = ("arbitrary",)
        if n_swap:
            scratch = [pltpu.SemaphoreType.DMA((n_swap, N_DEV // 2))] * 2
    return pl.pallas_call(
        body, name=name, grid=grid, in_specs=in_specs, out_specs=out_spec, out_shape=out_shape, scratch_shapes=scratch,
        compiler_params=_params(semantics),
    )(*operands)


def _resident(shape):
    return pl.BlockSpec(shape, lambda *_: (0,) * len(shape), pipeline_mode=pl.Buffered(1))


TM = 512
N_TM = SEQ // TM
PA_TILE = 640
N_PA = PROJ_A_PAD // PA_TILE
OUT_TILE = 256


def _spec(shape, fn):
    return pl.BlockSpec(shape, fn)


def _act(shape=(SEQ, D_MODEL), dtype=BF16):
    return jax.ShapeDtypeStruct(shape, dtype)


def _proj_rows_nt(name, h, wt, n_tile):
    n = wt.shape[0]
    return _matmul(name, h, _resident((SEQ, D_MODEL)), wt, _spec((n_tile, D_MODEL), lambda j: (j, 0)), NT,
                   (n // n_tile,), _act((SEQ, n)), _spec((SEQ, n_tile), lambda j: (0, j)))


def _proj_cols_nn(name, h, w_blocks):
    nb, _, n = w_blocks.shape
    return _matmul(name, h, _resident((SEQ, D_MODEL)), w_blocks, _spec((None, D_MODEL, n), lambda j: (j, 0, 0)),
                   NN, (nb,), _act((SEQ, nb * n)), _spec((SEQ, n), lambda j: (0, j)))


def _square(name, a, w, dims, x=None):
    row = _spec((TM, D_MODEL), lambda i: (i, 0))
    return _matmul(name, a, row, w, _resident((D_MODEL, D_MODEL)), dims, (N_TM,),
                   _act(dtype=F32 if x is not None else BF16), row, res=x, res_spec=row if x is not None else None)


def _sum_blocks_nn(name, a_blocks, w_blocks, x=None, norm=None, swap=()):
    nb, _, n = a_blocks.shape
    row = _spec((TM, D_MODEL), lambda i: (i, 0))
    return _matmul(name, a_blocks, _spec((nb, TM, n), lambda i: (0, i, 0)), w_blocks, _resident((nb, n, D_MODEL)),
                   NN, (N_TM,), _act(dtype=F32), row, k_blocks=nb, res=x, res_spec=row if x is not None else None, norm=norm, swap=swap)


def _sum_cols_nt(name, d, w_blocks, norm=None, swap=()):
    nb, _, n = w_blocks.shape
    return _matmul(name, d, _spec((TM, nb * n), lambda i: (i, 0)), w_blocks, _resident((nb, D_MODEL, n)), NT,
                   (N_TM,), _act(dtype=F32), _spec((TM, D_MODEL), lambda i: (i, 0)), k_blocks=nb, a_block_cols=n, norm=norm, swap=swap)


def _wide_nn(name, d, wt, x=None, norm=None, swap=()):
    n = wt.shape[0]
    row = _spec((TM, D_MODEL), lambda i: (i, 0))
    return _matmul(name, d, _spec((TM, n), lambda i: (i, 0)), wt, _resident((n, D_MODEL)), NN, (N_TM,),
                   _act(dtype=F32), row, res=x, res_spec=row if x is not None else None, norm=norm, swap=swap)


def _wide_nt(name, d, w):
    n = w.shape[0]
    return _matmul(name, d, _spec((TM, D_MODEL), lambda i: (i, 0)), w, _resident((n, D_MODEL)), NT, (N_TM,),
                   _act((SEQ, n)), _spec((TM, n), lambda i: (i, 0)))


def _proj_halves_nt(name, h, wt):
    _, n, _ = wt.shape
    return _matmul(name, h, _spec((TM, D_MODEL), lambda p, i: (i, 0)), wt, _spec((None, n, D_MODEL), lambda p, i: (p, 0, 0)), NT,
                   (2, N_TM), _act((2, SEQ, n)), _spec((None, TM, n), lambda p, i: (p, i, 0)))


def _wgrad_halves_tn(name, d, n_tile, h):
    _, _, n = d.shape
    return _matmul(name, d, _spec((None, SEQ, n_tile), lambda p, j: (p, 0, j)), h, _resident((SEQ, D_MODEL)), TN,
                   (2, n // n_tile), _act((2, n, D_MODEL)), _spec((None, n_tile, D_MODEL), lambda p, j: (p, j, 0)))


def _wgrad_cols_tn(name, d, n_tile, h):
    n = d.shape[1]
    return _matmul(name, d, _spec((SEQ, n_tile), lambda j: (0, j)), h, _resident((SEQ, D_MODEL)), TN,
                   (n // n_tile,), _act((n, D_MODEL)), _spec((n_tile, D_MODEL), lambda j: (j, 0)))


def _wgrad_cols_transposed_tn(name, h, d, n_tile):
    nb = d.shape[1] // n_tile
    return _matmul(name, d, _spec((SEQ, n_tile), lambda j: (0, j)), h, _resident((SEQ, D_MODEL)), TN, (nb,),
                   _act((nb, D_MODEL, n_tile)), _spec((None, D_MODEL, n_tile), lambda j: (j, 0, 0)), transpose_out=True)


NORM_ROWS = 512


def _rstd(x):
    return lax.rsqrt(jnp.mean(x * x, axis=-1, keepdims=True) + RMS_EPS)


def _norm_fwd(name, x, gamma):
    def body(x_ref, g_ref, h_ref):
        x = x_ref[...]
        h_ref[...] = (x * _rstd(x) * g_ref[...]).astype(BF16)

    row = _spec((NORM_ROWS, D_MODEL), lambda i: (i, 0))
    return pl.pallas_call(
        body, name=name, grid=(SEQ // NORM_ROWS,), in_specs=[row, _spec((1, D_MODEL), lambda i: (0, 0))], out_specs=row,
        out_shape=jax.ShapeDtypeStruct((SEQ, D_MODEL), BF16), compiler_params=_params(("parallel",)),
    )(x, gamma)


def _norm_bwd_rows(x, gamma, dh):
    r = _rstd(x)
    xh = x * r
    dxh = dh * gamma
    dx = r * (dxh - xh * jnp.mean(dxh * xh, axis=-1, keepdims=True))
    return dx, jnp.sum(dh * xh, axis=0, keepdims=True)


def _loss_head(x, gamma, target):
    def body(x_ref, g_ref, t_ref, loss_ref, dx_ref, dx16_ref, dg_ref):
        x = x_ref[...]
        gamma = g_ref[...]
        err = x * _rstd(x) * gamma - t_ref[...]
        dy = err * (1.0 / D_MODEL)
        dx, dg = _norm_bwd_rows(x, gamma, dy)
        dx_ref[...] = dx
        dx16_ref[...] = dx.astype(BF16)
        part = 0.5 * jnp.sum(jnp.sum(err * err, axis=-1, keepdims=True) * (1.0 / D_MODEL), axis=0, keepdims=True)
        part = jnp.broadcast_to(part, loss_ref.shape)

        @pl.when(pl.program_id(0) == 0)
        def _():
            dg_ref[...] = dg
            loss_ref[...] = part

        @pl.when(pl.program_id(0) > 0)
        def _():
            dg_ref[...] += dg
            loss_ref[...] += part

    row = _spec((NORM_ROWS, D_MODEL), lambda i: (i, 0))
    vec = _spec((1, D_MODEL), lambda i: (0, 0))
    return pl.pallas_call(
        body, name="loss_head", grid=(SEQ // NORM_ROWS,), in_specs=[row, vec, row],
        out_specs=[_spec((1, 128), lambda i: (0, 0)), row, row, vec],
        out_shape=[jax.ShapeDtypeStruct((1, 128), F32), _act(dtype=F32), _act(), jax.ShapeDtypeStruct((1, D_MODEL), F32)],
        compiler_params=_params(("arbitrary",)),
    )(x, gamma, target)


def _sigmoid(x):
    return 1.0 / (1.0 + jnp.exp(-x))


def _rows(ref, c):
    return ref[pl.ds(pl.multiple_of(c * ROW_CHUNK, ROW_CHUNK), ROW_CHUNK), :].astype(F32)


def _rows_before(ref, c):
    start = pl.multiple_of(jnp.maximum(c * ROW_CHUNK - HALO, 0), HALO)
    rows = ref[pl.ds(start, HALO), :].astype(F32)
    return jnp.where(c > 0, rows, 0.0)


def _rows_after(ref, c, n_chunks):
    start = pl.multiple_of(jnp.minimum((c + 1) * ROW_CHUNK, SEQ - HALO), HALO)
    rows = ref[pl.ds(start, HALO), :].astype(F32)
    return jnp.where(c < n_chunks - 1, rows, 0.0)


def _shift_down(z, before, n):
    return pltpu.roll(jnp.concatenate([before, z], axis=0), n, 0)[HALO:]


def _shift_up(z, after, n):
    rows = z.shape[0]
    return pltpu.roll(jnp.concatenate([z, after], axis=0), rows + HALO - n, 0)[:rows]


def _conv_rows(z, before, w):
    z1 = _shift_down(z, before, 1)
    z2 = _shift_down(z, before, 2)
    return w[2:3, :] * z + w[1:2, :] * z1 + w[0:1, :] * z2, z1, z2


def _conv_t_rows(dy, after, w):
    return w[2:3, :] * dy + w[1:2, :] * _shift_up(dy, after, 1) + w[0:1, :] * _shift_up(dy, after, 2)


N_ROW_CHUNKS = SEQ // ROW_CHUNK


FF_COLS = 256
N_FF_COLS = D_FF // FF_COLS


def _ffn_mid_fwd(name, gu, conv_w):
    def body(gu_ref, w_ref, a_ref):
        w = w_ref[...]

        def chunk(c, carry):
            g = _rows(gu_ref.at[0], c)
            u = _rows(gu_ref.at[1], c)
            gc, _, _ = _conv_rows(g, _rows_before(gu_ref.at[0], c), w)
            a_ref[pl.ds(pl.multiple_of(c * ROW_CHUNK, ROW_CHUNK), ROW_CHUNK), :] = (gc * _sigmoid(gc) * u).astype(BF16)
            return carry

        lax.fori_loop(0, N_ROW_CHUNKS, chunk, 0)

    col = _spec((SEQ, FF_COLS), lambda j: (0, j))
    return pl.pallas_call(
        body, name=name, grid=(N_FF_COLS,),
        in_specs=[_spec((2, SEQ, FF_COLS), lambda j: (0, 0, j)), _spec((3, FF_COLS), lambda j: (0, j))], out_specs=col,
        out_shape=_act((SEQ, D_FF)), compiler_params=_params(("parallel",)),
    )(gu, conv_w)


def _ffn_mid_bwd(name, gu, conv_w, da):
    def body(gu_ref, w_ref, da_ref, dgu_ref, dw_ref, dgc_ref):
        w = w_ref[...]

        def first(c, acc):
            g = _rows(gu_ref.at[0], c)
            u = _rows(gu_ref.at[1], c)
            d = _rows(da_ref, c)
            gc, g1, g2 = _conv_rows(g, _rows_before(gu_ref.at[0], c), w)
            sg = _sigmoid(gc)
            rows = pl.ds(pl.multiple_of(c * ROW_CHUNK, ROW_CHUNK), ROW_CHUNK)
            dgu_ref[1, rows, :] = (d * gc * sg).astype(BF16)
            dgc = d * u * (sg * (1.0 + gc * (1.0 - sg)))
            dgc_ref[rows, :] = dgc
            return (acc[0] + jnp.sum(dgc * g2, axis=0, keepdims=True), acc[1] + jnp.sum(dgc * g1, axis=0, keepdims=True),
                    acc[2] + jnp.sum(dgc * g, axis=0, keepdims=True))

        zero = jnp.zeros((1, FF_COLS), F32)
        acc = lax.fori_loop(0, N_ROW_CHUNKS, first, (zero, zero, zero))
        for r in range(3):
            dw_ref[r:r + 1, :] = acc[r]

        def second(c, carry):
            dgc = _rows(dgc_ref, c)
            dg = _conv_t_rows(dgc, _rows_after(dgc_ref, c, N_ROW_CHUNKS), w)
            dgu_ref[0, pl.ds(pl.multiple_of(c * ROW_CHUNK, ROW_CHUNK), ROW_CHUNK), :] = dg.astype(BF16)
            return carry

        lax.fori_loop(0, N_ROW_CHUNKS, second, 0)

    pair = _spec((2, SEQ, FF_COLS), lambda j: (0, 0, j))
    wspec = _spec((3, FF_COLS), lambda j: (0, j))
    return pl.pallas_call(
        body, name=name, grid=(N_FF_COLS,), in_specs=[pair, wspec, _spec((SEQ, FF_COLS), lambda j: (0, j))],
        out_specs=[pair, wspec], out_shape=[_act((2, SEQ, D_FF)), jax.ShapeDtypeStruct((3, D_FF), F32)],
        scratch_shapes=[pltpu.VMEM((SEQ, FF_COLS), F32)],
        compiler_params=_params(("parallel",)),
    )(gu, conv_w, da)


SC_COLS = 256
N_SC = D_MODEL // SC_COLS


def _sc_specs():
    return [_spec((SEQ, SC_COLS), lambda j, part=part: (0, part * N_SC + j)) for part in range(3)]


def _sc_mid_fwd(p, conv_w):
    def body(b_ref, c_ref, h_ref, w_ref, y_ref):
        w = w_ref[...]

        def chunk(c, carry):
            z = _rows(c_ref, c) * _rows(h_ref, c)
            before = _rows_before(c_ref, c) * _rows_before(h_ref, c)
            zc, _, _ = _conv_rows(z, before, w)
            y_ref[pl.ds(pl.multiple_of(c * ROW_CHUNK, ROW_CHUNK), ROW_CHUNK), :] = (_rows(b_ref, c) * zc).astype(BF16)
            return carry

        lax.fori_loop(0, N_ROW_CHUNKS, chunk, 0)

    col = _spec((SEQ, SC_COLS), lambda j: (0, j))
    return pl.pallas_call(
        body, name="sc_mid_fwd", grid=(N_SC,), in_specs=_sc_specs() + [_spec((3, SC_COLS), lambda j: (0, j))], out_specs=col,
        out_shape=jax.ShapeDtypeStruct((SEQ, D_MODEL), BF16), compiler_params=_params(("parallel",)),
    )(p, p, p, conv_w)


def _sc_mid_bwd(p, conv_w, dy):
    def body(b_ref, c_ref, h_ref, w_ref, dy_ref, db_ref, dc_ref, dh_ref, dw_ref, dzc_ref):
        w = w_ref[...]

        def first(c, acc):
            z = _rows(c_ref, c) * _rows(h_ref, c)
            before = _rows_before(c_ref, c) * _rows_before(h_ref, c)
            zc, z1, z2 = _conv_rows(z, before, w)
            d = _rows(dy_ref, c)
            rows = pl.ds(pl.multiple_of(c * ROW_CHUNK, ROW_CHUNK), ROW_CHUNK)
            db_ref[rows, :] = (d * zc).astype(BF16)
            dzc = d * _rows(b_ref, c)
            dzc_ref[rows, :] = dzc
            return (acc[0] + jnp.sum(dzc * z2, axis=0, keepdims=True), acc[1] + jnp.sum(dzc * z1, axis=0, keepdims=True),
                    acc[2] + jnp.sum(dzc * z, axis=0, keepdims=True))

        zero = jnp.zeros((1, SC_COLS), F32)
        acc = lax.fori_loop(0, N_ROW_CHUNKS, first, (zero, zero, zero))
        for r in range(3):
            dw_ref[r:r + 1, :] = acc[r]

        def second(c, carry):
            dz = _conv_t_rows(_rows(dzc_ref, c), _rows_after(dzc_ref, c, N_ROW_CHUNKS), w)
            rows = pl.ds(pl.multiple_of(c * ROW_CHUNK, ROW_CHUNK), ROW_CHUNK)
            dc_ref[rows, :] = (dz * _rows(h_ref, c)).astype(BF16)
            dh_ref[rows, :] = (dz * _rows(c_ref, c)).astype(BF16)
            return carry

        lax.fori_loop(0, N_ROW_CHUNKS, second, 0)

    col = _spec((SEQ, SC_COLS), lambda j: (0, j))
    wspec = _spec((3, SC_COLS), lambda j: (0, j))
    act = jax.ShapeDtypeStruct((SEQ, D_MODEL), BF16)
    return pl.pallas_call(
        body, name="sc_mid_bwd", grid=(N_SC,), in_specs=_sc_specs() + [wspec, col], out_specs=[col, col, col, wspec],
        out_shape=[act, act, act, jax.ShapeDtypeStruct((3, D_MODEL), F32)],
        scratch_shapes=[pltpu.VMEM((SEQ, SC_COLS), F32)], compiler_params=_params(("parallel",)),
    )(p, p, p, conv_w, dy)


GLA_GROUP = 4
GLA_ROWS = GLA_GROUP * CHUNK
N_GROUPS = N_CHUNKS // GLA_GROUP
Q0, K0, V0, R0, G0 = 0, KEY_DIM, 2 * KEY_DIM, 2 * KEY_DIM + VALUE_DIM, 2 * KEY_DIM + 2 * VALUE_DIM


def _tri(strict):
    r = lax.broadcasted_iota(jnp.int32, (CHUNK, CHUNK), 0)
    c = lax.broadcasted_iota(jnp.int32, (CHUNK, CHUNK), 1)
    return jnp.where(c < r if strict else c <= r, 1.0, 0.0).astype(F32)


def _cumsum_rows(tri, x):
    return jnp.dot(tri, x, preferred_element_type=F32, precision=lax.Precision.HIGHEST)


def _gate_logits(gl, wgu, b_gate):
    return jnp.dot(gl, wgu, preferred_element_type=F32) + b_gate


def _log_decay(logits):
    return (jnp.minimum(logits, 0.0) - jnp.log(1.0 + jnp.exp(-jnp.abs(logits)))) * (1.0 / GATE_NORMALIZER)


def _head(x, h, width):
    return x[:, h * width:(h + 1) * width]


def _gla_fwd(proj, wgu, b_gate, gn):
    def body(p_ref, wgu_ref, b_ref, gn_ref, o_ref, og_ref, st_ref, state):
        @pl.when(pl.program_id(0) == 0)
        def _():
            state[...] = jnp.zeros_like(state)

        tri = _tri(False)
        la = _log_decay(_gate_logits(p_ref[:, G0:G0 + GATE_PAD], wgu_ref[...], b_ref[...]))
        for c in range(GLA_GROUP):
            rows = slice(c * CHUNK, (c + 1) * CHUNK)
            cum = _cumsum_rows(tri, la[rows])
            tot = cum[CHUNK - 1:CHUNK, :]
            kd = (p_ref[rows, K0:K0 + KEY_DIM].astype(F32) * jnp.exp(tot - cum)).astype(BF16)
            decay = jnp.exp(tot)
            q = (p_ref[rows, Q0:Q0 + KEY_DIM].astype(F32) * (HEAD_K ** -0.5)).astype(BF16)
            v = p_ref[rows, V0:V0 + VALUE_DIM]
            for h in range(GLA_HEADS):
                upd = lax.dot_general(_head(v, h, HEAD_V), _head(kd, h, HEAD_K), (TN, ((), ())), preferred_element_type=F32)
                s = state[h] * _head(decay, h, HEAD_K) + upd
                state[h] = s
                st_ref[c, h] = s
                o_ref[rows, h * HEAD_V:(h + 1) * HEAD_V] = lax.dot_general(
                    _head(q, h, HEAD_K), s.astype(BF16), (NT, ((), ())), preferred_element_type=F32)
        r = p_ref[:, R0:R0 + VALUE_DIM].astype(F32)
        gate = r * _sigmoid(r) * gn_ref[...]
        for h in range(GLA_HEADS):
            cols = slice(h * HEAD_V, (h + 1) * HEAD_V)
            o = o_ref[:, cols]
            og_ref[:, cols] = (o * _rstd(o) * gate[:, cols]).astype(BF16)

    rows = _spec((GLA_ROWS, VALUE_DIM), lambda i: (i, 0))
    const = lambda shape: _spec(shape, lambda i: (0,) * len(shape))
    return pl.pallas_call(
        body, name="gla_fwd", grid=(N_GROUPS,),
        in_specs=[_spec((GLA_ROWS, PROJ_A_PAD), lambda i: (i, 0)), const((GATE_PAD, KEY_DIM)), const((1, KEY_DIM)),
                  const((1, VALUE_DIM))],
        out_specs=[rows, rows, _spec((GLA_GROUP, GLA_HEADS, HEAD_V, HEAD_K), lambda i: (i, 0, 0, 0))],
        out_shape=[jax.ShapeDtypeStruct((SEQ, VALUE_DIM), F32), jax.ShapeDtypeStruct((SEQ, VALUE_DIM), BF16),
                   jax.ShapeDtypeStruct((N_CHUNKS, GLA_HEADS, HEAD_V, HEAD_K), F32)],
        scratch_shapes=[pltpu.VMEM((GLA_HEADS, HEAD_V, HEAD_K), F32)], compiler_params=_params(("arbitrary",)),
    )(proj, wgu, b_gate, gn)


def _gla_bwd(proj, wgu, b_gate, gn, o, states, dog):
    last = N_GROUPS - 1

    def body(p_ref, wgu_ref, b_ref, gn_ref, o_ref, st_ref, stp_ref, dog_ref, dp_ref, dwgu_ref, db_ref, dgn_ref, carry, do_buf):
        step = pl.program_id(0)

        @pl.when(step == 0)
        def _():
            carry[...] = jnp.zeros_like(carry)

        r = p_ref[:, R0:R0 + VALUE_DIM].astype(F32)
        sr = _sigmoid(r)
        silu = r * sr
        gn_row = gn_ref[...]
        dog_rows = dog_ref[...].astype(F32)
        dn = dog_rows * silu
        dgn_cols = []
        for h in range(GLA_HEADS):
            cols = slice(h * HEAD_V, (h + 1) * HEAD_V)
            oh = o_ref[:, cols]
            rs = _rstd(oh)
            ohat = oh * rs
            dn_h = dn[:, cols]
            dgn_cols.append(jnp.sum(dn_h * ohat, axis=0, keepdims=True))
            dohat = dn_h * gn_row[:, cols]
            do_buf[:, cols] = rs * (dohat - ohat * jnp.mean(dohat * ohat, axis=-1, keepdims=True))
            n_h = ohat * gn_row[:, cols]
            dp_ref[:, R0 + h * HEAD_V:R0 + (h + 1) * HEAD_V] = (
                dog_rows[:, cols] * n_h * (sr[:, cols] * (1.0 + r[:, cols] * (1.0 - sr[:, cols])))).astype(BF16)
        dgn = jnp.concatenate(dgn_cols, axis=1)

        tri = _tri(False)
        tri_strict = _tri(True)
        gl = p_ref[:, G0:G0 + GATE_PAD]
        logits = _gate_logits(gl, wgu_ref[...], b_ref[...])
        la = _log_decay(logits)
        dlogit_rows = []
        for c in reversed(range(GLA_GROUP)):
            rows = slice(c * CHUNK, (c + 1) * CHUNK)
            cum = _cumsum_rows(tri, la[rows])
            tot = cum[CHUNK - 1:CHUNK, :]
            fade = jnp.exp(tot - cum)
            k = p_ref[rows, K0:K0 + KEY_DIM].astype(F32)
            kd32 = k * fade
            kd = kd32.astype(BF16)
            decay = jnp.exp(tot)
            q = (p_ref[rows, Q0:Q0 + KEY_DIM].astype(F32) * (HEAD_K ** -0.5)).astype(BF16)
            v = p_ref[rows, V0:V0 + VALUE_DIM]
            do = do_buf[rows, :].astype(BF16)
            dkd_cols, ddecay_cols = [], []
            for h in range(GLA_HEADS):
                do_h = _head(do, h, HEAD_V)
                s = st_ref[c, h]
                dq = jnp.dot(do_h, s.astype(BF16), preferred_element_type=F32) * (HEAD_K ** -0.5)
                dp_ref[rows, Q0 + h * HEAD_K:Q0 + (h + 1) * HEAD_K] = dq.astype(BF16)
                g = carry[h] + lax.dot_general(do_h, _head(q, h, HEAD_K), (TN, ((), ())), preferred_element_type=F32)
                g16 = g.astype(BF16)
                dkd_cols.append(jnp.dot(_head(v, h, HEAD_V), g16, preferred_element_type=F32))
                dv = lax.dot_general(_head(kd, h, HEAD_K), g16, (NT, ((), ())), preferred_element_type=F32)
                dp_ref[rows, V0 + h * HEAD_V:V0 + (h + 1) * HEAD_V] = dv.astype(BF16)
                if c > 0:
                    s_prev = st_ref[c - 1, h]
                else:
                    s_prev = jnp.where(step < last, stp_ref[0, h], 0.0)
                ddecay_cols.append(jnp.sum(g * s_prev, axis=0, keepdims=True))
                carry[h] = g * _head(decay, h, HEAD_K)
            dkd = jnp.concatenate(dkd_cols, axis=1)
            ddecay = jnp.concatenate(ddecay_cols, axis=1)
            dp_ref[rows, K0:K0 + KEY_DIM] = (dkd * fade).astype(BF16)
            e = dkd * kd32
            dla = ddecay * decay + _cumsum_rows(tri_strict, e)
            dlogit_rows.append(dla * (1.0 / GATE_NORMALIZER) * (1.0 - _sigmoid(logits[rows])))
        dlogit = jnp.concatenate(dlogit_rows[::-1], axis=0)
        dlogit16 = dlogit.astype(BF16)
        dp_ref[:, G0:G0 + GATE_PAD] = lax.dot_general(
            dlogit16, wgu_ref[...], (NT, ((), ())), preferred_element_type=F32).astype(BF16)
        dwgu = lax.dot_general(gl, dlogit16, (TN, ((), ())), preferred_element_type=F32)
        db = jnp.sum(dlogit, axis=0, keepdims=True)

        @pl.when(step == 0)
        def _():
            dwgu_ref[...] = dwgu
            db_ref[...] = db
            dgn_ref[...] = dgn

        @pl.when(step > 0)
        def _():
            dwgu_ref[...] += dwgu
            db_ref[...] += db
            dgn_ref[...] += dgn

    rev = lambda i: (last - i, 0)
    rows = _spec((GLA_ROWS, VALUE_DIM), rev)
    const = lambda shape: _spec(shape, lambda i: (0,) * len(shape))
    st_shape = (GLA_HEADS, HEAD_V, HEAD_K)
    return pl.pallas_call(
        body, name="gla_bwd", grid=(N_GROUPS,),
        in_specs=[_spec((GLA_ROWS, PROJ_A_PAD), rev), const((GATE_PAD, KEY_DIM)), const((1, KEY_DIM)), const((1, VALUE_DIM)),
                  rows, _spec((GLA_GROUP,) + st_shape, lambda i: (last - i, 0, 0, 0)),
                  _spec((1,) + st_shape, lambda i: (jnp.maximum((last - i) * GLA_GROUP - 1, 0), 0, 0, 0)), rows],
        out_specs=[_spec((GLA_ROWS, PROJ_A_PAD), rev), const((GATE_PAD, KEY_DIM)), const((1, KEY_DIM)), const((1, VALUE_DIM))],
        out_shape=[jax.ShapeDtypeStruct((SEQ, PROJ_A_PAD), BF16), jax.ShapeDtypeStruct((GATE_PAD, KEY_DIM), F32),
                   jax.ShapeDtypeStruct((1, KEY_DIM), F32), jax.ShapeDtypeStruct((1, VALUE_DIM), F32)],
        scratch_shapes=[pltpu.VMEM(st_shape, F32), pltpu.VMEM((GLA_ROWS, VALUE_DIM), F32)],
        compiler_params=_params(("arbitrary",)),
    )(proj, wgu, b_gate, gn, o, states, states, dog)


WGRAD_FF_TILE = D_FF // 2


def _ffn_fwd(tag, x, gamma, w_up_t, conv_w, w_down):
    h = _norm_fwd(f"ffn{tag}_norm", x, gamma)
    gu = _proj_halves_nt(f"ffn{tag}_up", h, w_up_t)
    a = _ffn_mid_fwd(f"ffn{tag}_mid", gu, conv_w)
    return _wide_nn(f"ffn{tag}_down", a, w_down, x=x), (h, gu, a)


def _owner_blocks(d, rows=None):
    if rows is not None:
        d = d[:rows]
    return d.reshape((N_DEV, -1) + d.shape[-1:])


def _ffn_bwd(tag, x, gamma, w_up_t, conv_w, w_down, saved, dx, dx16, swap):
    h, gu, a = saved
    da = _wide_nt(f"ffn{tag}_da", dx16, w_down)
    d_w_down = _owner_blocks(_wgrad_cols_tn(f"ffn{tag}_dwdown", a, WGRAD_FF_TILE, dx16))
    dgu, d_conv = _ffn_mid_bwd(f"ffn{tag}_mid_bwd", gu, conv_w, da)
    d_w_up_t = _owner_blocks(_wgrad_halves_tn(f"ffn{tag}_dwup", dgu, WGRAD_FF_TILE, h))
    parts = (d_w_up_t, d_w_down)
    dx, dx16, d_gamma, *received = _sum_blocks_nn(
        f"ffn{tag}_dh", dgu, w_up_t, norm=(x, gamma, dx), swap=parts if swap else ())
    return dx, dx16, d_gamma, d_conv, parts, received


def _local_step(x, target, w, fetch=None, emit=None):
    if fetch is None:
        local = dict(a=(w.get("a_w_in"), w.get("a_w_out")), b=(w.get("b_w_in"), w.get("b_w_out")))
        for layer in range(2):
            local[f"f{layer}"] = (w["f_w_up"][layer], w["f_w_down"][layer]) if "f_w_up" in w else None
        fetch = lambda group, after: local[group]
    swap = emit is not None
    if emit is None:
        emit = lambda group, parts, received, dx: dx
    f_norm = (w["f_norm"][0:1], w["f_norm"][1:2])

    x0 = x
    a_w_in, a_w_out = fetch("a", x0)
    h0 = _norm_fwd("a_norm", x0, w["a_norm"])
    proj = _proj_rows_nt("a_in", h0, a_w_in, PA_TILE)
    o, og, states = _gla_fwd(proj, w["a_w_gate_up"], w["a_b_gate"], w["a_gn"])
    x1 = _square("a_out", og, a_w_out, NN, x0)
    up0, down0 = fetch("f0", x1)
    x2, ffn0 = _ffn_fwd(0, x1, f_norm[0], up0, w["f_conv"][0], down0)
    b_w_in, b_w_out = fetch("b", x2)
    h2 = _norm_fwd("b_norm", x2, w["b_norm"])
    p = _proj_cols_nn("b_in", h2, b_w_in)
    y = _sc_mid_fwd(p, w["b_conv"])
    x3 = _square("b_out", y, b_w_out, NN, x2)
    up1, down1 = fetch("f1", x3)
    x4, ffn1 = _ffn_fwd(1, x3, f_norm[1], up1, w["f_conv"][1], down1)
    loss, dx, dx16, d_final_norm = _loss_head(x4, w["final_norm"], target)

    dx, dx16, d_f_norm1, d_fconv1, parts_f1, got = _ffn_bwd(
        1, x3, f_norm[1], up1, w["f_conv"][1], down1, ffn1, dx, dx16, swap)
    dx16 = emit("f1", parts_f1, got, dx16)

    dy = _square("b_dy", dx16, b_w_out, NT)
    d_b_w_out = _owner_blocks(_wgrad_cols_tn("b_dwout", y, OUT_TILE, dx16))
    db, dc, dhh, d_b_conv = _sc_mid_bwd(p, w["b_conv"], dy)
    dp = jnp.concatenate([db, dc, dhh], axis=1)
    parts_b = (_wgrad_cols_transposed_tn("b_dwin", h2, dp, B_SHARD), d_b_w_out)
    dx, dx16, d_b_norm, *got = _sum_cols_nt("b_dh", dp, b_w_in, norm=(x2, w["b_norm"], dx), swap=parts_b if swap else ())
    dx16 = emit("b", parts_b, got, dx16)

    dx, dx16, d_f_norm0, d_fconv0, parts_f0, got = _ffn_bwd(
        0, x1, f_norm[0], up0, w["f_conv"][0], down0, ffn0, dx, dx16, swap)
    dx16 = emit("f0", parts_f0, got, dx16)

    dog = _square("a_dog", dx16, a_w_out, NT)
    d_a_w_out = _owner_blocks(_wgrad_cols_tn("a_dwout", og, OUT_TILE, dx16))
    dproj, d_wgu, d_b_gate, d_gn = _gla_bwd(proj, w["a_w_gate_up"], w["a_b_gate"], w["a_gn"], o, states, dog)
    parts_a = (_owner_blocks(_wgrad_cols_tn("a_dwin", dproj, PA_TILE, h0), PROJ_A), d_a_w_out)
    dx, _, d_a_norm, *got = _wide_nn("a_dh", dproj, a_w_in, norm=(x0, w["a_norm"], dx), swap=parts_a if swap else ())
    emit("a", parts_a, got, dx)

    grads = dict(
        a_norm=d_a_norm, a_w_in=parts_a[0], a_w_gate_up=d_wgu, a_b_gate=d_b_gate, a_gn=d_gn, a_w_out=parts_a[1],
        b_norm=d_b_norm, b_w_in=parts_b[0], b_conv=d_b_conv, b_w_out=parts_b[1],
        f_norm=(d_f_norm0, d_f_norm1), f_w_up=(parts_f0[0], parts_f1[0]), f_conv=(d_fconv0, d_fconv1),
        f_w_down=(parts_f0[1], parts_f1[1]), final_norm=d_final_norm)
    grads["loss"] = loss
    return dx, grads


MESH_ID = pl.DeviceIdType.MESH
ANY = pl.BlockSpec(memory_space=pl.ANY)
N_PEERS = N_DEV - 1


def _position():
    return lax.axis_index("x"), lax.axis_index("y"), lax.axis_index("c")


def _slot(px, py, pc):
    return 4 * px + 2 * py + pc


GATHER_COPIES = 8
HALF_ROWS = 16


def _gather_copies(src, out, send_sems, recv_sems, local_sems):
    n = len(src)
    to_sibling, to_x, to_y, x_on_to_y, y_on_to_x, x_to_sibling, y_to_sibling, diagonal_to_sibling = range(GATHER_COPIES)
    x, y, c = _position()
    me, sibling = (x, y, c), (x, y, 1 - c)
    x_side, y_side, diagonal = (1 - x, y), (x, 1 - y), (1 - x, 1 - y)

    def rows_of(t, half):
        rows = src[t].shape[0]
        half_rows = rows // 2 // HALF_ROWS * HALF_ROWS
        return (pl.ds(0, rows), pl.ds(0, half_rows), pl.ds(half_rows, rows - half_rows))[half]

    def copy(t, j, block, to, half=0, from_input=False):
        dst = out[t].at[_slot(*block), rows_of(t, half)]
        return pltpu.make_async_remote_copy(
            src_ref=src[t] if from_input else dst, dst_ref=dst, send_sem=send_sems.at[GATHER_COPIES * t + j],
            recv_sem=recv_sems.at[GATHER_COPIES * t + j], device_id=to, device_id_type=MESH_ID)

    mine = [pltpu.make_async_copy(src[t], out[t].at[_slot(*me)], local_sems.at[t]) for t in range(n)]
    for cp in mine:
        cp.start()
    sent = []

    def start(cp):
        cp.start()
        sent.append(cp)

    for t in range(n):
        start(copy(t, to_sibling, me, sibling, from_input=True))
        start(copy(t, to_x, me, (*x_side, c), from_input=True))
        start(copy(t, to_y, me, (*y_side, c), from_input=True))
    for t in range(n):
        copy(t, to_x, (*x_side, c), me).wait_recv()
        start(copy(t, x_on_to_y, (*x_side, c), (*y_side, c), half=1))
        start(copy(t, x_to_sibling, (*x_side, c), sibling))
        copy(t, to_y, (*y_side, c), me).wait_recv()
        start(copy(t, y_on_to_x, (*y_side, c), (*x_side, c), half=2))
        start(copy(t, y_to_sibling, (*y_side, c), sibling))
    for t in range(n):
        copy(t, x_on_to_y, (*diagonal, c), me, half=1).wait_recv()
        copy(t, y_on_to_x, (*diagonal, c), me, half=2).wait_recv()
        start(copy(t, diagonal_to_sibling, (*diagonal, c), sibling))
    for t in range(n):
        copy(t, to_sibling, sibling, me).wait_recv()
        for j, chip in ((x_to_sibling, x_side), (y_to_sibling, y_side), (diagonal_to_sibling, diagonal)):
            copy(t, j, (*chip, 1 - c), me).wait_recv()
    for cp in sent:
        cp.wait_send()
    for cp in mine:
        cp.wait()


def _all_gather(name, shards):
    n = len(shards)

    def body(*refs):
        _gather_copies(refs[:n], refs[n:2 * n], *refs[2 * n:])

    sems = pltpu.SemaphoreType.DMA((GATHER_COPIES * n,))
    return pl.pallas_call(
        body, name=name, in_specs=[ANY] * n, out_specs=[ANY] * n,
        out_shape=[jax.ShapeDtypeStruct((N_DEV,) + s.shape, s.dtype) for s in shards],
        scratch_shapes=[sems, sems, pltpu.SemaphoreType.DMA((n,))],
    )(*shards)


SIBLING_AND_NEIGHBOURS = (1, 2, 4)
SAME_CORE = (2, 4, 6)


def _flip(x, y, c, k):
    return x ^ (k >> 2), y ^ ((k >> 1) & 1), c ^ (k & 1)


N_CHIPS = N_DEV // 2


def _chip(px, py):
    return 2 * px + py


def _pair_copies(parts, received, send_sems, recv_sems):
    x, y, c = lax.axis_index("x"), lax.axis_index("y"), lax.axis_index("c")
    sibling = (x, y, 1 - c)
    copies = []
    for t in range(len(parts)):
        for q in range(N_DEV // 2):
            send = pltpu.make_async_remote_copy(
                src_ref=parts[t].at[2 * q + 1 - c], dst_ref=received[t].at[q], send_sem=send_sems.at[t, q],
                recv_sem=recv_sems.at[t, q], device_id=sibling, device_id_type=pl.DeviceIdType.MESH)
            landed = received[t].at[q]
            arrival = pltpu.make_async_remote_copy(
                src_ref=landed, dst_ref=landed, send_sem=send_sems.at[t, q], recv_sem=recv_sems.at[t, q],
                device_id=sibling, device_id_type=pl.DeviceIdType.MESH)
            copies.append((send, arrival))
    return copies


PAIR_ROWS = 1024


def _pair_add(name, part, received, side):
    _, rows, cols = part.shape
    tiles = [t for t in range(PAIR_ROWS, 0, -BF16_ROWS) if rows % t == 0]
    tr = tiles[0] if tiles else rows

    def body(side_ref, p_ref, r_ref, o_ref):
        o_ref[...] = (p_ref[...].astype(F32) + r_ref[...].astype(F32)).astype(BF16)

    tile = _spec((None, tr, cols), lambda q, i, side_ref: (q, i, 0))
    return pl.pallas_call(
        body, name=name,
        grid_spec=pltpu.PrefetchScalarGridSpec(
            num_scalar_prefetch=1, grid=(N_CHIPS, rows // tr),
            in_specs=[_spec((None, tr, cols), lambda q, i, side_ref: (2 * q + side_ref[0], i, 0)), tile], out_specs=tile),
        out_shape=jax.ShapeDtypeStruct((N_CHIPS, rows, cols), BF16), compiler_params=_params(("parallel", "parallel")),
    )(side, part, received)


def _send_copy(parts, landing, send_sems, recv_sems, t, s, k):
    x, y, c = _position()
    px, py, _ = _flip(x, y, c, k)
    return pltpu.make_async_remote_copy(
        src_ref=parts[t].at[_chip(px, py)], dst_ref=landing[t].at[_chip(x, y)], send_sem=send_sems.at[s],
        recv_sem=recv_sems.at[s], device_id=(px, py, c), device_id_type=MESH_ID)


def _send_arrival(landing, send_sems, recv_sems, t, s, k):
    x, y, c = _position()
    px, py, _ = _flip(x, y, c, k)
    landed = landing[t].at[_chip(px, py)]
    return pltpu.make_async_remote_copy(
        src_ref=landed, dst_ref=landed, send_sem=send_sems.at[s], recv_sem=recv_sems.at[s],
        device_id=(px, py, c), device_id_type=MESH_ID)


def _handshake(peers):
    x, y, c = _position()
    barrier = pltpu.get_barrier_semaphore()
    for k in peers:
        pl.semaphore_signal(barrier, inc=1, device_id=_flip(x, y, c, k), device_id_type=MESH_ID)
    pl.semaphore_wait(barrier, len(peers))


def _sequencer(name, collective_id, n_copies, body, operands, out_type):
    n_arrays = len(operands)
    return pl.kernel(
        body, out_type=out_type, mesh=plsc.ScalarSubcoreMesh(axis_name="sequencer", num_cores=1), name=name,
        scratch_types=(pltpu.SemaphoreType.DMA((n_copies,)), pltpu.SemaphoreType.DMA((n_copies,)),
                       pltpu.SemaphoreType.DMA((n_arrays,))),
        compiler_params=pltpu.CompilerParams(collective_id=collective_id))(*operands)


def _sequencer_exchange(name, collective_id, parts, after=()):
    n, n_peers, n_in = len(parts), len(SAME_CORE), len(parts) + len(after)

    def body(*refs):
        src, landing = refs[:n], refs[n_in:n_in + n]
        send_sems, recv_sems, local_sems = refs[n_in + n:]
        _handshake(SAME_CORE)
        x, y, _ = _position()
        mine = [pltpu.make_async_copy(src[t].at[_chip(x, y)], landing[t].at[_chip(x, y)], local_sems.at[t]) for t in range(n)]
        for cp in mine:
            cp.start()
        sent = [_send_copy(src, landing, send_sems, recv_sems, t, t * n_peers + j, k)
                for t in range(n) for j, k in enumerate(SAME_CORE)]
        for cp in sent:
            cp.start()
        for t in range(n):
            for j, k in enumerate(SAME_CORE):
                _send_arrival(landing, send_sems, recv_sems, t, t * n_peers + j, k).wait_recv()
        for cp in sent:
            cp.wait_send()
        for cp in mine:
            cp.wait()

    landing = [jax.ShapeDtypeStruct(p.shape, p.dtype) for p in parts]
    return _sequencer(name, collective_id, n * n_peers, body, list(parts) + list(after), landing)


def _sequencer_gather(name, collective_id, shards):
    n = len(shards)

    def body(*refs):
        _handshake(SIBLING_AND_NEIGHBOURS)
        _gather_copies(refs[:n], refs[n:2 * n], *refs[2 * n:])

    gathered = [jax.ShapeDtypeStruct((N_DEV,) + s.shape, s.dtype) for s in shards]
    return _sequencer(name, collective_id, GATHER_COPIES * n, body, shards, gathered)


ADAM_ROWS = 512
BF16_ROWS = 16


def _adam_update(w, g, m, v):
    m = ADAM_B1 * m + (1.0 - ADAM_B1) * g
    v = ADAM_B2 * v + (1.0 - ADAM_B2) * (g * g)
    m_hat = m / (1.0 - ADAM_B1 ** ADAM_STEP)
    v_hat = v / (1.0 - ADAM_B2 ** ADAM_STEP)
    delta = -ADAM_LR * (m_hat / (jnp.sqrt(v_hat) + ADAM_EPS) + ADAM_WD * w)
    return delta, m, v


def _sum_slots(ref):
    total = ref[0].astype(F32)
    for d in range(1, ref.shape[0]):
        total = total + ref[d].astype(F32)
    return total


def _adamw_sum(name, landed, w, m, v):
    layers, rows, cols = w.shape
    tiles = [t for t in range(ADAM_ROWS, 0, -BF16_ROWS) if rows % t == 0]
    tr = tiles[0] if tiles else rows
    nt = rows // tr

    def body(*refs):
        parts = refs[:layers]
        w_ref, m_ref, v_ref, g_ref, d_ref, nm_ref, nv_ref = refs[layers:]
        layer = pl.program_id(0)
        g = _sum_slots(parts[0])
        for q in range(1, layers):
            g = jnp.where(layer == q, _sum_slots(parts[q]), g)
        delta, new_m, new_v = _adam_update(w_ref[...], g, m_ref[...], v_ref[...])
        g_ref[...] = g
        d_ref[...] = delta
        nm_ref[...] = new_m
        nv_ref[...] = new_v

    def part_spec(q):
        return _spec((N_CHIPS, tr, cols), lambda l, i: (0, jnp.where(l == q, i, jnp.where(l < q, 0, nt - 1)), 0))

    tile = _spec((None, tr, cols), lambda l, i: (l, i, 0))
    out = jax.ShapeDtypeStruct((layers, rows, cols), F32)
    return pl.pallas_call(
        body, name=name, grid=(layers, nt), in_specs=[part_spec(q) for q in range(layers)] + [tile] * 3,
        out_specs=[tile] * 4, out_shape=[out] * 4, compiler_params=_params(("arbitrary", "arbitrary")),
    )(*landed, w, m, v)


def _sum_small(landed):
    def body(in_ref, out_ref):
        out_ref[...] = _sum_slots(in_ref)

    return pl.pallas_call(body, name="small_grad_sum", out_shape=jax.ShapeDtypeStruct(landed.shape[1:], F32))(landed)


def _adamw_small(arrays):
    n = len(arrays)

    def body(*refs):
        for i in range(n):
            g_ref, w_ref, m_ref, v_ref = refs[4 * i:4 * i + 4]
            d_ref, nm_ref, nv_ref = refs[4 * n + 3 * i:4 * n + 3 * i + 3]
            d_ref[...], nm_ref[...], nv_ref[...] = _adam_update(w_ref[...], g_ref[...], m_ref[...], v_ref[...])

    out = [jax.ShapeDtypeStruct(w.shape, F32) for _, w, _, _ in arrays for _ in range(3)]
    flat = pl.pallas_call(body, name="adam_small", out_shape=out)(*[a for group in arrays for a in group])
    return [tuple(flat[3 * i:3 * i + 3]) for i in range(n)]


LANES = 128
SUBLANES = 8
F_CONV_SHARD = D_FF // N_DEV
GATE_SHARD = KEY_DIM // N_DEV
NORM_SHARD = D_MODEL // N_DEV


def _tile_rows(a):
    flat = a.reshape(-1)
    size = -(-flat.shape[0] // (SUBLANES * LANES)) * SUBLANES * LANES
    return jnp.pad(flat, (0, size - flat.shape[0])).reshape(-1, LANES)


def _pack_rows(pieces):
    return jnp.concatenate([_tile_rows(p) for p in pieces], axis=0)


def _unpack_rows(packed, shapes):
    out, row = [], 0
    for shape in shapes:
        size = 1
        for s in shape:
            size *= s
        rows = -(-size // (SUBLANES * LANES)) * SUBLANES
        piece = packed[..., row:row + rows, :]
        out.append(piece.reshape(piece.shape[:-2] + (rows * LANES,))[..., :size])
        row += rows
    return out


SMALL_SHARDS = ((GATE_RANK, GATE_SHARD), (1, NORM_SHARD), (3, NORM_SHARD), (2, 3, F_CONV_SHARD))


def _unpack_small_shards(g):
    gate, b_norm, b_conv, f_conv = _unpack_rows(g, SMALL_SHARDS)
    gate = gate.reshape(N_DEV, GATE_RANK, GATE_SHARD).transpose(1, 0, 2).reshape(GATE_RANK, KEY_DIM)
    b_norm = b_norm.reshape(1, D_MODEL)
    b_conv = b_conv.reshape(N_DEV, 3, NORM_SHARD).transpose(1, 0, 2).reshape(3, D_MODEL)
    f_conv = f_conv.reshape(N_DEV, 2, 3, F_CONV_SHARD).transpose(1, 2, 0, 3).reshape(2, 3, D_FF)
    return gate, b_norm, b_conv, f_conv


SMALL_LAYOUT = (("a_norm", (1, D_MODEL)), ("a_w_gate_up", (GATE_RANK, KEY_DIM)), ("a_b_gate", (1, KEY_DIM)), ("a_gn", (1, VALUE_DIM)),
                ("b_norm", (1, D_MODEL)), ("b_conv", (3, D_MODEL)), ("f_norm0", (1, D_MODEL)), ("f_norm1", (1, D_MODEL)),
                ("f_conv0", (3, D_FF)), ("f_conv1", (3, D_FF)), ("final_norm", (1, D_MODEL)), ("loss", (1, LANES)))


def _pack_small_grads(g):
    full = dict(g)
    full["a_w_gate_up"] = g["a_w_gate_up"][:GATE_RANK]
    for layer in range(2):
        full[f"f_norm{layer}"] = g["f_norm"][layer]
        full[f"f_conv{layer}"] = g["f_conv"][layer]
    return _pack_rows([full[name] for name, _ in SMALL_LAYOUT])


def _unpack_small_grads(packed):
    pieces = _unpack_rows(packed, [shape for _, shape in SMALL_LAYOUT])
    out = {name: piece.reshape(shape) for (name, shape), piece in zip(SMALL_LAYOUT, pieces)}
    out["f_norm"] = jnp.stack([out["f_norm0"][0], out["f_norm1"][0]])
    out["f_conv"] = jnp.stack([out["f_conv0"], out["f_conv1"]])
    return out


def kernel(x, a_norm, a_w_in, a_w_gate_up, a_b_gate, a_gn, a_w_out, b_norm, b_w_in, b_conv, b_w_out, f_norm, f_w_up, f_conv, f_w_down, final_norm, loss_target, m_a_norm, m_a_w_in, m_a_w_gate_up, m_a_b_gate, m_a_gn, m_a_w_out, m_b_norm, m_b_w_in, m_b_conv, m_b_w_out, m_f_norm, m_f_w_up, m_f_conv, m_f_w_down, m_final_norm, v_a_norm, v_a_w_in, v_a_w_gate_up, v_a_b_gate, v_a_gn, v_a_w_out, v_b_norm, v_b_w_in, v_b_conv, v_b_w_out, v_f_norm, v_f_w_up, v_f_conv, v_f_w_down, v_final_norm):
    my_slot = _slot(*_position())

    transposed = lambda w: jnp.swapaxes(w, 1, 2)
    a_transposed = lambda w: w.reshape(D_MODEL, A_SHARD).T.reshape(1, A_SHARD, D_MODEL)
    a_w_in_t, f_w_up_t = a_transposed(a_w_in), transposed(f_w_up)
    first = _all_gather("weight_gather", [a_w_in_t[0].astype(BF16), a_w_out[0].astype(BF16),
                                          _pack_rows([a_w_gate_up[0], b_norm, b_conv[0], f_conv])])
    gathers, small_shards = {}, first[2]
    later = (("f0", f_w_up_t[0], f_w_down[0]), ("b", b_w_in[0], b_w_out[0]), ("f1", f_w_up_t[1], f_w_down[1]))
    for collective_id, (group, w_in, w_out) in enumerate(later):
        w_in, w_out, small_shards = lax.optimization_barrier((w_in.astype(BF16), w_out.astype(BF16), small_shards))
        gathers[group] = _sequencer_gather(f"gather_{group}", collective_id, [w_in, w_out])
    gate_full, b_norm_full, b_conv_full, f_conv_full = _unpack_small_shards(small_shards)
    a_w_in_full = jnp.pad(first[0].reshape(PROJ_A, D_MODEL), ((0, PROJ_A_PAD - PROJ_A), (0, 0)))
    weights = dict(
        a_norm=a_norm, a_w_gate_up=jnp.pad(gate_full, ((0, GATE_PAD - GATE_RANK), (0, 0))).astype(BF16), a_b_gate=a_b_gate,
        a_gn=a_gn, b_norm=b_norm_full, b_conv=b_conv_full, f_norm=f_norm, f_conv=f_conv_full,
        final_norm=final_norm.reshape(1, D_MODEL))

    def fetch(group, after):
        if group == "a":
            return a_w_in_full, first[1].reshape(D_MODEL, D_MODEL)
        w_in, w_out = gathers[group]
        if group == "b":
            return w_in, w_out.reshape(D_MODEL, D_MODEL)
        return w_in.reshape(2, D_FF, D_MODEL), w_out.reshape(D_FF, D_MODEL)

    exchanges, pending = {}, []
    exchange_ids = dict(b=3, f0=4, a=5)
    side = lax.axis_index("c").astype(jnp.int32).reshape(1)

    def emit(group, parts, received, carry):
        sums = [_pair_add(f"pair_add_{group}_{i}", part, got, side) for i, (part, got) in enumerate(zip(parts, received))]
        carry, *sums = lax.optimization_barrier((carry, *sums))
        pending.extend(sums)
        if group != "f1":
            after = list(exchanges.values())[-1][:1] if exchanges else ()
            exchanges[group] = _sequencer_exchange(f"grads_{group}", exchange_ids[group], list(pending), after)
            pending.clear()
        return carry

    dx, g = _local_step(x[0], loss_target[0], weights, fetch, emit)

    (up1, down1, d_b_in, d_b_out), (up0, down0), (d_a_in, d_a_out) = (exchanges[group] for group in ("b", "f0", "a"))
    back = lambda results: tuple(transposed(r) for r in results)
    big = dict(
        b_w_in=_adamw_sum("adam_b_w_in", [d_b_in], b_w_in, m_b_w_in, v_b_w_in),
        b_w_out=_adamw_sum("adam_b_w_out", [d_b_out], b_w_out, m_b_w_out, v_b_w_out),
        f_w_up=back(_adamw_sum("adam_f_w_up", [up0, up1], f_w_up_t, transposed(m_f_w_up), transposed(v_f_w_up))),
        f_w_down=_adamw_sum("adam_f_w_down", [down0, down1], f_w_down, m_f_w_down, v_f_w_down))
    small_packed, *updated = lax.optimization_barrier((_pack_small_grads(g), *big["f_w_down"]))
    big["f_w_down"] = tuple(updated)
    small_landed = _all_gather("small_grad_gather", [small_packed])[0]
    big.update(
        a_w_in=tuple(r.reshape(A_SHARD, D_MODEL).T.reshape(1, D_MODEL, A_SHARD) for r in _adamw_sum(
            "adam_a_w_in", [d_a_in], a_w_in_t, a_transposed(m_a_w_in), a_transposed(v_a_w_in))),
        a_w_out=_adamw_sum("adam_a_w_out", [d_a_out], a_w_out, m_a_w_out, v_a_w_out))
    small_g = _unpack_small_grads(_sum_small(small_landed))
    loss = small_g["loss"][0, 0]
    small_g["a_w_gate_up"] = lax.dynamic_slice_in_dim(small_g["a_w_gate_up"], my_slot * GATE_SHARD, GATE_SHARD, axis=1)
    small_g["b_norm"] = lax.dynamic_slice_in_dim(small_g["b_norm"], my_slot * NORM_SHARD, NORM_SHARD, axis=1)
    small_g["b_conv"] = lax.dynamic_slice_in_dim(small_g["b_conv"], my_slot * NORM_SHARD, NORM_SHARD, axis=1)
    small_g["f_conv"] = lax.dynamic_slice_in_dim(small_g["f_conv"], my_slot * F_CONV_SHARD, F_CONV_SHARD, axis=2)
    small_w = dict(
        a_norm=(a_norm, m_a_norm, v_a_norm), a_w_gate_up=(a_w_gate_up, m_a_w_gate_up, v_a_w_gate_up),
        a_b_gate=(a_b_gate, m_a_b_gate, v_a_b_gate), a_gn=(a_gn, m_a_gn, v_a_gn), b_norm=(b_norm, m_b_norm, v_b_norm),
        b_conv=(b_conv, m_b_conv, v_b_conv), f_norm=(f_norm, m_f_norm, v_f_norm), f_conv=(f_conv, m_f_conv, v_f_conv),
        final_norm=(final_norm, m_final_norm, v_final_norm))
    two_d = lambda a: a.reshape(-1, a.shape[-1])
    updates = _adamw_small([tuple(two_d(a.reshape(w.shape)) for a in (small_g[name], w, m, v)) for name, (w, m, v) in small_w.items()])
    small = {}
    for (name, (w, _, _)), update in zip(small_w.items(), updates):
        small[name] = (small_g[name].reshape(w.shape),) + tuple(u.reshape(w.shape) for u in update)

    order = ["a_norm", "a_w_in", "a_w_gate_up", "a_b_gate", "a_gn", "a_w_out", "b_norm", "b_w_in", "b_conv", "b_w_out",
             "f_norm", "f_w_up", "f_conv", "f_w_down", "final_norm"]
    results = {**big, **small}
    outputs = [loss, dx.reshape(1, SEQ, D_MODEL)]
    for kind in range(4):
        outputs += [results[name][kind] for name in order]
    return tuple(outputs)
```

```python
import jax
import jax.numpy as jnp
from jax import lax
from jax.experimental import pallas as pl
from jax.experimental.pallas import tpu as pltpu
from jax.experimental.pallas import tpu_sc as plsc

F32 = jnp.float32
BF16 = jnp.bfloat16

N_DEV = 8
SEQ = 2048
D_MODEL = 1024
CHUNK = 64
N_CHUNKS = SEQ // CHUNK
RMS_EPS = 1e-6
GLA_HEADS = 4
KEY_DIM = 512
VALUE_DIM = 1024
HEAD_K = KEY_DIM // GLA_HEADS
HEAD_V = VALUE_DIM // GLA_HEADS
GATE_RANK = 16
GATE_PAD = 128
GATE_NORMALIZER = 16.0
PROJ_A = 2 * KEY_DIM + 2 * VALUE_DIM + GATE_RANK
PROJ_A_PAD = 2 * KEY_DIM + 2 * VALUE_DIM + GATE_PAD
A_SHARD = PROJ_A // N_DEV
B_SHARD = 3 * D_MODEL // N_DEV
D_FF = 2816
ADAM_LR = 0.001
ADAM_B1 = 0.9
ADAM_B2 = 0.999
ADAM_EPS = 1e-08
ADAM_WD = 0.01
ADAM_STEP = 10
MESH_AXES = ("x", "y", "c")

VMEM_LIMIT = 56 * 1024 * 1024
ROW_CHUNK = 256
HALO = 16


def _params(sem=None, vmem=VMEM_LIMIT):
    return pltpu.CompilerParams(dimension_semantics=sem, vmem_limit_bytes=vmem)


NN = ((1,), (0,))
NT = ((1,), (1,))
TN = ((0,), (0,))


def _matmul(name, a, a_spec, b, b_spec, dims, grid, out_shape, out_spec, k_blocks=None, a_block_cols=None, res=None,
            res_spec=None, transpose_out=False, norm=None, swap=()):
    has_res = res is not None
    n_swap = len(swap)

    def body(*refs):
        a_ref, b_ref = refs[0], refs[1]
        r_ref = refs[2] if has_res else None

        def product(lhs, rhs):
            return lax.dot_general(lhs.astype(BF16), rhs, (dims, ((), ())), preferred_element_type=F32)

        if k_blocks is None:
            v = product(a_ref[...], b_ref[...])
        else:
            v = None
            for k in range(k_blocks):
                lhs = a_ref[k] if a_block_cols is None else a_ref[:, k * a_block_cols:(k + 1) * a_block_cols]
                p = product(lhs, b_ref[k])
                v = p if v is None else v + p
        if transpose_out:
            v = v.T
        if has_res:
            v = v + r_ref[...]
        if norm is None:
            o_ref = refs[2 + has_res]
            o_ref[...] = v.astype(o_ref.dtype)
            return
        n_in = 5 + has_res
        x_ref, g_ref, dxi_ref = refs[2 + has_res:n_in]
        dx_ref, dx16_ref, dg_ref = refs[n_in + n_swap:n_in + n_swap + 3]
        if n_swap:
            copies = _pair_copies(refs[n_in:n_in + n_swap], refs[n_in + n_swap + 3:n_in + 2 * n_swap + 3], *refs[-2:])

            @pl.when(pl.program_id(0) == 0)
            def _():
                for send, _ in copies:
                    send.start()

            @pl.when(pl.program_id(0) == grid[0] - 1)
            def _():
                for send, arrival in copies:
                    arrival.wait_recv()
                    send.wait_send()

        dx, dg = _norm_bwd_rows(x_ref[...], g_ref[...], v)
        dx = dxi_ref[...] + dx
        dx_ref[...] = dx
        dx16_ref[...] = dx.astype(BF16)

        @pl.when(pl.program_id(0) == 0)
        def _():
            dg_ref[...] = dg

        @pl.when(pl.program_id(0) > 0)
        def _():
            dg_ref[...] += dg

    operands = [a, b] + ([res] if has_res else [])
    in_specs = [a_spec, b_spec] + ([res_spec] if has_res else [])
    semantics = ("parallel",) * len(grid)
    scratch = []
    if norm is not None:
        vec = _spec((1, D_MODEL), lambda i: (0, 0))
        any_space = pl.BlockSpec(memory_space=pl.ANY)
        operands += list(norm) + list(swap)
        in_specs += [out_spec, vec, out_spec] + [any_space] * n_swap
        out_shape = [_act(dtype=F32), _act(), jax.ShapeDtypeStruct((1, D_MODEL), F32)]
        out_shape += [jax.ShapeDtypeStruct((N_DEV // 2,) + p.shape[1:], p.dtype) for p in swap]
        out_spec = [out_spec, out_spec, vec] + [any_space] * n_swap
        semantics = ("arbitrary",)
        if n_swap:
            scratch = [pltpu.SemaphoreType.DMA((n_swap, N_DEV // 2))] * 2
    return pl.pallas_call(
        body, name=name, grid=grid, in_specs=in_specs, out_specs=out_spec, out_shape=out_shape, scratch_shapes=scratch,
        compiler_params=_params(semantics),
    )(*operands)


def _resident(shape):
    return pl.BlockSpec(shape, lambda *_: (0,) * len(shape), pipeline_mode=pl.Buffered(1))


TM = 512
N_TM = SEQ // TM
PA_TILE = 640
N_PA = PROJ_A_PAD // PA_TILE
OUT_TILE = 256


def _spec(shape, fn):
    return pl.BlockSpec(shape, fn)


def _act(shape=(SEQ, D_MODEL), dtype=BF16):
    return jax.ShapeDtypeStruct(shape, dtype)


def _norm_proj(name, x, gamma, w):
    blocks = w.ndim == 3
    n_out = w.shape[0] * w.shape[2] if blocks else w.shape[0]

    def body(x_ref, g_ref, w_ref, h_ref, o_ref):
        x = x_ref[...]
        h = (x * _rstd(x) * g_ref[...]).astype(BF16)
        h_ref[...] = h
        if blocks:
            n = w.shape[2]
            for j in range(w.shape[0]):
                o_ref[:, j * n:(j + 1) * n] = jnp.dot(h, w_ref[j], preferred_element_type=F32).astype(BF16)
        else:
            o_ref[...] = lax.dot_general(h, w_ref[...], (NT, ((), ())), preferred_element_type=F32).astype(BF16)

    row = _spec((TM, D_MODEL), lambda i: (i, 0))
    return pl.pallas_call(
        body, name=name, grid=(N_TM,), in_specs=[row, _resident((1, D_MODEL)), _resident(w.shape)],
        out_specs=[row, _spec((TM, n_out), lambda i: (i, 0))], out_shape=[_act(), _act((SEQ, n_out))],
        compiler_params=_params(("parallel",)),
    )(x, gamma, w)


def _square(name, a, w, dims, x=None):
    row = _spec((TM, D_MODEL), lambda i: (i, 0))
    return _matmul(name, a, row, w, _resident((D_MODEL, D_MODEL)), dims, (N_TM,),
                   _act(dtype=F32 if x is not None else BF16), row, res=x, res_spec=row if x is not None else None)


def _sum_blocks_nn(name, a_blocks, w_blocks, x=None, norm=None, swap=()):
    nb, _, n = a_blocks.shape
    row = _spec((TM, D_MODEL), lambda i: (i, 0))
    return _matmul(name, a_blocks, _spec((nb, TM, n), lambda i: (0, i, 0)), w_blocks, _resident((nb, n, D_MODEL)),
                   NN, (N_TM,), _act(dtype=F32), row, k_blocks=nb, res=x, res_spec=row if x is not None else None, norm=norm, swap=swap)


def _sum_cols_nt(name, d, w_blocks, norm=None, swap=()):
    nb, _, n = w_blocks.shape
    return _matmul(name, d, _spec((TM, nb * n), lambda i: (i, 0)), w_blocks, _resident((nb, D_MODEL, n)), NT,
                   (N_TM,), _act(dtype=F32), _spec((TM, D_MODEL), lambda i: (i, 0)), k_blocks=nb, a_block_cols=n, norm=norm, swap=swap)


def _wide_nn(name, d, wt, x=None, norm=None, swap=()):
    n = wt.shape[0]
    row = _spec((TM, D_MODEL), lambda i: (i, 0))
    return _matmul(name, d, _spec((TM, n), lambda i: (i, 0)), wt, _resident((n, D_MODEL)), NN, (N_TM,),
                   _act(dtype=F32), row, res=x, res_spec=row if x is not None else None, norm=norm, swap=swap)


def _wide_nt(name, d, w):
    n = w.shape[0]
    return _matmul(name, d, _spec((TM, D_MODEL), lambda i: (i, 0)), w, _resident((n, D_MODEL)), NT, (N_TM,),
                   _act((SEQ, n)), _spec((TM, n), lambda i: (i, 0)))


def _wgrad_halves_tn(name, d, n_tile, h):
    _, _, n = d.shape
    return _matmul(name, d, _spec((None, SEQ, n_tile), lambda p, j: (p, 0, j)), h, _resident((SEQ, D_MODEL)), TN,
                   (2, n // n_tile), _act((2, n, D_MODEL)), _spec((None, n_tile, D_MODEL), lambda p, j: (p, j, 0)))


def _wgrad_cols_tn(name, d, n_tile, h):
    n = d.shape[1]
    return _matmul(name, d, _spec((SEQ, n_tile), lambda j: (0, j)), h, _resident((SEQ, D_MODEL)), TN,
                   (n // n_tile,), _act((n, D_MODEL)), _spec((n_tile, D_MODEL), lambda j: (j, 0)))


def _wgrad_cols_transposed_tn(name, h, d, n_tile):
    nb = d.shape[1] // n_tile
    return _matmul(name, d, _spec((SEQ, n_tile), lambda j: (0, j)), h, _resident((SEQ, D_MODEL)), TN, (nb,),
                   _act((nb, D_MODEL, n_tile)), _spec((None, D_MODEL, n_tile), lambda j: (j, 0, 0)), transpose_out=True)


NORM_ROWS = 512


def _rstd(x):
    return lax.rsqrt(jnp.mean(x * x, axis=-1, keepdims=True) + RMS_EPS)


def _norm_bwd_rows(x, gamma, dh):
    r = _rstd(x)
    xh = x * r
    dxh = dh * gamma
    dx = r * (dxh - xh * jnp.mean(dxh * xh, axis=-1, keepdims=True))
    return dx, jnp.sum(dh * xh, axis=0, keepdims=True)


def _loss_head(x, gamma, target):
    def body(x_ref, g_ref, t_ref, loss_ref, dx_ref, dx16_ref, dg_ref):
        x = x_ref[...]
        gamma = g_ref[...]
        err = x * _rstd(x) * gamma - t_ref[...]
        dy = err * (1.0 / D_MODEL)
        dx, dg = _norm_bwd_rows(x, gamma, dy)
        dx_ref[...] = dx
        dx16_ref[...] = dx.astype(BF16)
        part = 0.5 * jnp.sum(jnp.sum(err * err, axis=-1, keepdims=True) * (1.0 / D_MODEL), axis=0, keepdims=True)
        part = jnp.broadcast_to(part, loss_ref.shape)

        @pl.when(pl.program_id(0) == 0)
        def _():
            dg_ref[...] = dg
            loss_ref[...] = part

        @pl.when(pl.program_id(0) > 0)
        def _():
            dg_ref[...] += dg
            loss_ref[...] += part

    row = _spec((NORM_ROWS, D_MODEL), lambda i: (i, 0))
    vec = _spec((1, D_MODEL), lambda i: (0, 0))
    return pl.pallas_call(
        body, name="loss_head", grid=(SEQ // NORM_ROWS,), in_specs=[row, vec, row],
        out_specs=[_spec((1, 128), lambda i: (0, 0)), row, row, vec],
        out_shape=[jax.ShapeDtypeStruct((1, 128), F32), _act(dtype=F32), _act(), jax.ShapeDtypeStruct((1, D_MODEL), F32)],
        compiler_params=_params(("arbitrary",)),
    )(x, gamma, target)


def _sigmoid(x):
    return 1.0 / (1.0 + jnp.exp(-x))


def _rows(ref, c):
    return ref[pl.ds(pl.multiple_of(c * ROW_CHUNK, ROW_CHUNK), ROW_CHUNK), :].astype(F32)


def _rows_before(ref, c):
    start = pl.multiple_of(jnp.maximum(c * ROW_CHUNK - HALO, 0), HALO)
    rows = ref[pl.ds(start, HALO), :].astype(F32)
    return jnp.where(c > 0, rows, 0.0)


def _rows_after(ref, c, n_chunks):
    start = pl.multiple_of(jnp.minimum((c + 1) * ROW_CHUNK, SEQ - HALO), HALO)
    rows = ref[pl.ds(start, HALO), :].astype(F32)
    return jnp.where(c < n_chunks - 1, rows, 0.0)


def _shift_down(z, before, n):
    return pltpu.roll(jnp.concatenate([before, z], axis=0), n, 0)[before.shape[0]:]


def _shift_up(z, after, n):
    rows = z.shape[0]
    return pltpu.roll(jnp.concatenate([z, after], axis=0), rows + HALO - n, 0)[:rows]


def _conv_rows(z, before, w):
    z1 = _shift_down(z, before, 1)
    z2 = _shift_down(z, before, 2)
    return w[2:3, :] * z + w[1:2, :] * z1 + w[0:1, :] * z2, z1, z2


def _conv_t_rows(dy, after, w):
    return w[2:3, :] * dy + w[1:2, :] * _shift_up(dy, after, 1) + w[0:1, :] * _shift_up(dy, after, 2)


N_ROW_CHUNKS = SEQ // ROW_CHUNK


FF_COLS = 256
N_FF_COLS = D_FF // FF_COLS


def _ffn_mid_bwd(name, gu, conv_w, da):
    def body(gu_ref, w_ref, da_ref, dgu_ref, dw_ref, dgc_ref):
        w = w_ref[...]

        def first(c, acc):
            g = _rows(gu_ref.at[0], c)
            u = _rows(gu_ref.at[1], c)
            d = _rows(da_ref, c)
            gc, g1, g2 = _conv_rows(g, _rows_before(gu_ref.at[0], c), w)
            sg = _sigmoid(gc)
            rows = pl.ds(pl.multiple_of(c * ROW_CHUNK, ROW_CHUNK), ROW_CHUNK)
            dgu_ref[1, rows, :] = (d * gc * sg).astype(BF16)
            dgc = d * u * (sg * (1.0 + gc * (1.0 - sg)))
            dgc_ref[rows, :] = dgc
            return (acc[0] + jnp.sum(dgc * g2, axis=0, keepdims=True), acc[1] + jnp.sum(dgc * g1, axis=0, keepdims=True),
                    acc[2] + jnp.sum(dgc * g, axis=0, keepdims=True))

        zero = jnp.zeros((1, FF_COLS), F32)
        acc = lax.fori_loop(0, N_ROW_CHUNKS, first, (zero, zero, zero))
        for r in range(3):
            dw_ref[r:r + 1, :] = acc[r]

        def second(c, carry):
            dgc = _rows(dgc_ref, c)
            dg = _conv_t_rows(dgc, _rows_after(dgc_ref, c, N_ROW_CHUNKS), w)
            dgu_ref[0, pl.ds(pl.multiple_of(c * ROW_CHUNK, ROW_CHUNK), ROW_CHUNK), :] = dg.astype(BF16)
            return carry

        lax.fori_loop(0, N_ROW_CHUNKS, second, 0)

    pair = _spec((2, SEQ, FF_COLS), lambda j: (0, 0, j))
    wspec = _spec((3, FF_COLS), lambda j: (0, j))
    return pl.pallas_call(
        body, name=name, grid=(N_FF_COLS,), in_specs=[pair, wspec, _spec((SEQ, FF_COLS), lambda j: (0, j))],
        out_specs=[pair, wspec], out_shape=[_act((2, SEQ, D_FF)), jax.ShapeDtypeStruct((3, D_FF), F32)],
        scratch_shapes=[pltpu.VMEM((SEQ, FF_COLS), F32)],
        compiler_params=_params(("parallel",)),
    )(gu, conv_w, da)


SC_COLS = 256
N_SC = D_MODEL // SC_COLS


def _sc_specs():
    return [_spec((SEQ, SC_COLS), lambda j, part=part: (0, part * N_SC + j)) for part in range(3)]


def _sc_mid_fwd(p, conv_w):
    def body(b_ref, c_ref, h_ref, w_ref, y_ref):
        w = w_ref[...]

        def chunk(c, carry):
            z = _rows(c_ref, c) * _rows(h_ref, c)
            before = _rows_before(c_ref, c) * _rows_before(h_ref, c)
            zc, _, _ = _conv_rows(z, before, w)
            y_ref[pl.ds(pl.multiple_of(c * ROW_CHUNK, ROW_CHUNK), ROW_CHUNK), :] = (_rows(b_ref, c) * zc).astype(BF16)
            return carry

        lax.fori_loop(0, N_ROW_CHUNKS, chunk, 0)

    col = _spec((SEQ, SC_COLS), lambda j: (0, j))
    return pl.pallas_call(
        body, name="sc_mid_fwd", grid=(N_SC,), in_specs=_sc_specs() + [_spec((3, SC_COLS), lambda j: (0, j))], out_specs=col,
        out_shape=jax.ShapeDtypeStruct((SEQ, D_MODEL), BF16), compiler_params=_params(("parallel",)),
    )(p, p, p, conv_w)


def _sc_mid_bwd(p, conv_w, dy):
    def body(b_ref, c_ref, h_ref, w_ref, dy_ref, db_ref, dc_ref, dh_ref, dw_ref, dzc_ref):
        w = w_ref[...]

        def first(c, acc):
            z = _rows(c_ref, c) * _rows(h_ref, c)
            before = _rows_before(c_ref, c) * _rows_before(h_ref, c)
            zc, z1, z2 = _conv_rows(z, before, w)
            d = _rows(dy_ref, c)
            rows = pl.ds(pl.multiple_of(c * ROW_CHUNK, ROW_CHUNK), ROW_CHUNK)
            db_ref[rows, :] = (d * zc).astype(BF16)
            dzc = d * _rows(b_ref, c)
            dzc_ref[rows, :] = dzc
            return (acc[0] + jnp.sum(dzc * z2, axis=0, keepdims=True), acc[1] + jnp.sum(dzc * z1, axis=0, keepdims=True),
                    acc[2] + jnp.sum(dzc * z, axis=0, keepdims=True))

        zero = jnp.zeros((1, SC_COLS), F32)
        acc = lax.fori_loop(0, N_ROW_CHUNKS, first, (zero, zero, zero))
        for r in range(3):
            dw_ref[r:r + 1, :] = acc[r]

        def second(c, carry):
            dz = _conv_t_rows(_rows(dzc_ref, c), _rows_after(dzc_ref, c, N_ROW_CHUNKS), w)
            rows = pl.ds(pl.multiple_of(c * ROW_CHUNK, ROW_CHUNK), ROW_CHUNK)
            dc_ref[rows, :] = (dz * _rows(h_ref, c)).astype(BF16)
            dh_ref[rows, :] = (dz * _rows(c_ref, c)).astype(BF16)
            return carry

        lax.fori_loop(0, N_ROW_CHUNKS, second, 0)

    col = _spec((SEQ, SC_COLS), lambda j: (0, j))
    wspec = _spec((3, SC_COLS), lambda j: (0, j))
    act = jax.ShapeDtypeStruct((SEQ, D_MODEL), BF16)
    return pl.pallas_call(
        body, name="sc_mid_bwd", grid=(N_SC,), in_specs=_sc_specs() + [wspec, col], out_specs=[col, col, col, wspec],
        out_shape=[act, act, act, jax.ShapeDtypeStruct((3, D_MODEL), F32)],
        scratch_shapes=[pltpu.VMEM((SEQ, SC_COLS), F32)], compiler_params=_params(("parallel",)),
    )(p, p, p, conv_w, dy)


GLA_GROUP = 4
GLA_ROWS = GLA_GROUP * CHUNK
N_GROUPS = N_CHUNKS // GLA_GROUP
Q0, K0, V0, R0, G0 = 0, KEY_DIM, 2 * KEY_DIM, 2 * KEY_DIM + VALUE_DIM, 2 * KEY_DIM + 2 * VALUE_DIM


def _tri(strict):
    r = lax.broadcasted_iota(jnp.int32, (CHUNK, CHUNK), 0)
    c = lax.broadcasted_iota(jnp.int32, (CHUNK, CHUNK), 1)
    return jnp.where(c < r if strict else c <= r, 1.0, 0.0).astype(F32)


def _cumsum_rows(tri, x):
    return jnp.dot(tri, x, preferred_element_type=F32, precision=lax.Precision.HIGHEST)


def _gate_logits(gl, wgu, b_gate):
    return jnp.dot(gl, wgu, preferred_element_type=F32) + b_gate


def _log_decay(logits):
    return (jnp.minimum(logits, 0.0) - jnp.log(1.0 + jnp.exp(-jnp.abs(logits)))) * (1.0 / GATE_NORMALIZER)


def _head(x, h, width):
    return x[:, h * width:(h + 1) * width]


def _gla_fwd(proj, wgu, b_gate, gn):
    def body(p_ref, wgu_ref, b_ref, gn_ref, o_ref, og_ref, st_ref, state):
        @pl.when(pl.program_id(0) == 0)
        def _():
            state[...] = jnp.zeros_like(state)

        tri = _tri(False)
        la = _log_decay(_gate_logits(p_ref[:, G0:G0 + GATE_PAD], wgu_ref[...], b_ref[...]))
        for c in range(GLA_GROUP):
            rows = slice(c * CHUNK, (c + 1) * CHUNK)
            cum = _cumsum_rows(tri, la[rows])
            tot = cum[CHUNK - 1:CHUNK, :]
            kd = (p_ref[rows, K0:K0 + KEY_DIM].astype(F32) * jnp.exp(tot - cum)).astype(BF16)
            decay = jnp.exp(tot)
            q = (p_ref[rows, Q0:Q0 + KEY_DIM].astype(F32) * (HEAD_K ** -0.5)).astype(BF16)
            v = p_ref[rows, V0:V0 + VALUE_DIM]
            for h in range(GLA_HEADS):
                upd = lax.dot_general(_head(v, h, HEAD_V), _head(kd, h, HEAD_K), (TN, ((), ())), preferred_element_type=F32)
                s = state[h] * _head(decay, h, HEAD_K) + upd
                state[h] = s
                st_ref[c, h] = s
                o_ref[rows, h * HEAD_V:(h + 1) * HEAD_V] = lax.dot_general(
                    _head(q, h, HEAD_K), s.astype(BF16), (NT, ((), ())), preferred_element_type=F32)
        r = p_ref[:, R0:R0 + VALUE_DIM].astype(F32)
        gate = r * _sigmoid(r) * gn_ref[...]
        for h in range(GLA_HEADS):
            cols = slice(h * HEAD_V, (h + 1) * HEAD_V)
            o = o_ref[:, cols]
            og_ref[:, cols] = (o * _rstd(o) * gate[:, cols]).astype(BF16)

    rows = _spec((GLA_ROWS, VALUE_DIM), lambda i: (i, 0))
    const = lambda shape: _spec(shape, lambda i: (0,) * len(shape))
    return pl.pallas_call(
        body, name="gla_fwd", grid=(N_GROUPS,),
        in_specs=[_spec((GLA_ROWS, PROJ_A_PAD), lambda i: (i, 0)), const((GATE_PAD, KEY_DIM)), const((1, KEY_DIM)),
                  const((1, VALUE_DIM))],
        out_specs=[rows, rows, _spec((GLA_GROUP, GLA_HEADS, HEAD_V, HEAD_K), lambda i: (i, 0, 0, 0))],
        out_shape=[jax.ShapeDtypeStruct((SEQ, VALUE_DIM), F32), jax.ShapeDtypeStruct((SEQ, VALUE_DIM), BF16),
                   jax.ShapeDtypeStruct((N_CHUNKS, GLA_HEADS, HEAD_V, HEAD_K), F32)],
        scratch_shapes=[pltpu.VMEM((GLA_HEADS, HEAD_V, HEAD_K), F32)], compiler_params=_params(("arbitrary",)),
    )(proj, wgu, b_gate, gn)


def _gla_bwd(proj, wgu, b_gate, gn, o, states, dog):
    last = N_GROUPS - 1

    def body(p_ref, wgu_ref, b_ref, gn_ref, o_ref, st_ref, stp_ref, dog_ref, dp_ref, dwgu_ref, db_ref, dgn_ref, carry, do_buf):
        step = pl.program_id(0)

        @pl.when(step == 0)
        def _():
            carry[...] = jnp.zeros_like(carry)

        r = p_ref[:, R0:R0 + VALUE_DIM].astype(F32)
        sr = _sigmoid(r)
        silu = r * sr
        gn_row = gn_ref[...]
        dog_rows = dog_ref[...].astype(F32)
        dn = dog_rows * silu
        dgn_cols = []
        for h in range(GLA_HEADS):
            cols = slice(h * HEAD_V, (h + 1) * HEAD_V)
            oh = o_ref[:, cols]
            rs = _rstd(oh)
            ohat = oh * rs
            dn_h = dn[:, cols]
            dgn_cols.append(jnp.sum(dn_h * ohat, axis=0, keepdims=True))
            dohat = dn_h * gn_row[:, cols]
            do_buf[:, cols] = rs * (dohat - ohat * jnp.mean(dohat * ohat, axis=-1, keepdims=True))
            n_h = ohat * gn_row[:, cols]
            dp_ref[:, R0 + h * HEAD_V:R0 + (h + 1) * HEAD_V] = (
                dog_rows[:, cols] * n_h * (sr[:, cols] * (1.0 + r[:, cols] * (1.0 - sr[:, cols])))).astype(BF16)
        dgn = jnp.concatenate(dgn_cols, axis=1)

        tri = _tri(False)
        tri_strict = _tri(True)
        gl = p_ref[:, G0:G0 + GATE_PAD]
        logits = _gate_logits(gl, wgu_ref[...], b_ref[...])
        la = _log_decay(logits)
        dlogit_rows = []
        for c in reversed(range(GLA_GROUP)):
            rows = slice(c * CHUNK, (c + 1) * CHUNK)
            cum = _cumsum_rows(tri, la[rows])
            tot = cum[CHUNK - 1:CHUNK, :]
            fade = jnp.exp(tot - cum)
            k = p_ref[rows, K0:K0 + KEY_DIM].astype(F32)
            kd32 = k * fade
            kd = kd32.astype(BF16)
            decay = jnp.exp(tot)
            q = (p_ref[rows, Q0:Q0 + KEY_DIM].astype(F32) * (HEAD_K ** -0.5)).astype(BF16)
            v = p_ref[rows, V0:V0 + VALUE_DIM]
            do = do_buf[rows, :].astype(BF16)
            dkd_cols, ddecay_cols = [], []
            for h in range(GLA_HEADS):
                do_h = _head(do, h, HEAD_V)
                s = st_ref[c, h]
                dq = jnp.dot(do_h, s.astype(BF16), preferred_element_type=F32) * (HEAD_K ** -0.5)
                dp_ref[rows, Q0 + h * HEAD_K:Q0 + (h + 1) * HEAD_K] = dq.astype(BF16)
                g = carry[h] + lax.dot_general(do_h, _head(q, h, HEAD_K), (TN, ((), ())), preferred_element_type=F32)
                g16 = g.astype(BF16)
                dkd_cols.append(jnp.dot(_head(v, h, HEAD_V), g16, preferred_element_type=F32))
                dv = lax.dot_general(_head(kd, h, HEAD_K), g16, (NT, ((), ())), preferred_element_type=F32)
                dp_ref[rows, V0 + h * HEAD_V:V0 + (h + 1) * HEAD_V] = dv.astype(BF16)
                if c > 0:
                    s_prev = st_ref[c - 1, h]
                else:
                    s_prev = jnp.where(step < last, stp_ref[0, h], 0.0)
                ddecay_cols.append(jnp.sum(g * s_prev, axis=0, keepdims=True))
                carry[h] = g * _head(decay, h, HEAD_K)
            dkd = jnp.concatenate(dkd_cols, axis=1)
            ddecay = jnp.concatenate(ddecay_cols, axis=1)
            dp_ref[rows, K0:K0 + KEY_DIM] = (dkd * fade).astype(BF16)
            e = dkd * kd32
            dla = ddecay * decay + _cumsum_rows(tri_strict, e)
            dlogit_rows.append(dla * (1.0 / GATE_NORMALIZER) * (1.0 - _sigmoid(logits[rows])))
        dlogit = jnp.concatenate(dlogit_rows[::-1], axis=0)
        dlogit16 = dlogit.astype(BF16)
        dp_ref[:, G0:G0 + GATE_PAD] = lax.dot_general(
            dlogit16, wgu_ref[...], (NT, ((), ())), preferred_element_type=F32).astype(BF16)
        dwgu = lax.dot_general(gl, dlogit16, (TN, ((), ())), preferred_element_type=F32)
        db = jnp.sum(dlogit, axis=0, keepdims=True)

        @pl.when(step == 0)
        def _():
            dwgu_ref[...] = dwgu
            db_ref[...] = db
            dgn_ref[...] = dgn

        @pl.when(step > 0)
        def _():
            dwgu_ref[...] += dwgu
            db_ref[...] += db
            dgn_ref[...] += dgn

    rev = lambda i: (last - i, 0)
    rows = _spec((GLA_ROWS, VALUE_DIM), rev)
    const = lambda shape: _spec(shape, lambda i: (0,) * len(shape))
    st_shape = (GLA_HEADS, HEAD_V, HEAD_K)
    return pl.pallas_call(
        body, name="gla_bwd", grid=(N_GROUPS,),
        in_specs=[_spec((GLA_ROWS, PROJ_A_PAD), rev), const((GATE_PAD, KEY_DIM)), const((1, KEY_DIM)), const((1, VALUE_DIM)),
                  rows, _spec((GLA_GROUP,) + st_shape, lambda i: (last - i, 0, 0, 0)),
                  _spec((1,) + st_shape, lambda i: (jnp.maximum((last - i) * GLA_GROUP - 1, 0), 0, 0, 0)), rows],
        out_specs=[_spec((GLA_ROWS, PROJ_A_PAD), rev), const((GATE_PAD, KEY_DIM)), const((1, KEY_DIM)), const((1, VALUE_DIM))],
        out_shape=[jax.ShapeDtypeStruct((SEQ, PROJ_A_PAD), BF16), jax.ShapeDtypeStruct((GATE_PAD, KEY_DIM), F32),
                   jax.ShapeDtypeStruct((1, KEY_DIM), F32), jax.ShapeDtypeStruct((1, VALUE_DIM), F32)],
        scratch_shapes=[pltpu.VMEM(st_shape, F32), pltpu.VMEM((GLA_ROWS, VALUE_DIM), F32)],
        compiler_params=_params(("arbitrary",)),
    )(proj, wgu, b_gate, gn, o, states, states, dog)


WGRAD_FF_TILE = D_FF // 2


CARRY_ROWS = 8


def _ffn_up_mid(name, x, gamma, w_up_t, conv_w):
    def body(x_ref, g_ref, w_ref, c_ref, h_ref, gu_ref, a_ref, carry):
        @pl.when(pl.program_id(0) == 0)
        def _():
            carry[...] = jnp.zeros_like(carry)

        x_tile = x_ref[...]
        h_tile = (x_tile * _rstd(x_tile) * g_ref[...]).astype(BF16)
        h_ref[...] = h_tile
        for k in range(N_FF_COLS):
            cols = slice(k * FF_COLS, (k + 1) * FF_COLS)
            g, u = (lax.dot_general(h_tile, w_ref[p, cols, :], (NT, ((), ())), preferred_element_type=F32).astype(BF16)
                    for p in range(2))
            gu_ref[0, :, cols] = g
            gu_ref[1, :, cols] = u
            g = g.astype(F32)
            w = c_ref[:, cols]
            before = carry[:, cols]
            gc = w[2:3, :] * g + w[1:2, :] * _shift_down(g, before, 1) + w[0:1, :] * _shift_down(g, before, 2)
            a_ref[:, cols] = (gc * _sigmoid(gc) * u.astype(F32)).astype(BF16)
            carry[:, cols] = g[TM - CARRY_ROWS:, :]

    row = _spec((TM, D_MODEL), lambda i: (i, 0))
    return pl.pallas_call(
        body, name=name, grid=(N_TM,),
        in_specs=[row, _resident((1, D_MODEL)), _resident((2, D_FF, D_MODEL)), _resident((3, D_FF))],
        out_specs=[row, _spec((2, TM, D_FF), lambda i: (0, i, 0)), _spec((TM, D_FF), lambda i: (i, 0))],
        out_shape=[_act(), _act((2, SEQ, D_FF)), _act((SEQ, D_FF))], scratch_shapes=[pltpu.VMEM((CARRY_ROWS, D_FF), F32)],
        compiler_params=_params(("arbitrary",)),
    )(x, gamma, w_up_t, conv_w)


def _ffn_fwd(tag, x, gamma, w_up_t, conv_w, w_down):
    h, gu, a = _ffn_up_mid(f"ffn{tag}_up_mid", x, gamma, w_up_t, conv_w)
    return _wide_nn(f"ffn{tag}_down", a, w_down, x=x), (h, gu, a)


def _owner_blocks(d, rows=None):
    if rows is not None:
        d = d[:rows]
    return d.reshape((N_DEV, -1) + d.shape[-1:])


def _ffn_bwd(tag, x, gamma, w_up_t, conv_w, w_down, saved, dx, dx16, swap):
    h, gu, a = saved
    da = _wide_nt(f"ffn{tag}_da", dx16, w_down)
    d_w_down = _owner_blocks(_wgrad_cols_tn(f"ffn{tag}_dwdown", a, WGRAD_FF_TILE, dx16))
    dgu, d_conv = _ffn_mid_bwd(f"ffn{tag}_mid_bwd", gu, conv_w, da)
    d_w_up_t = _owner_blocks(_wgrad_halves_tn(f"ffn{tag}_dwup", dgu, WGRAD_FF_TILE, h))
    parts = (d_w_up_t, d_w_down)
    dx, dx16, d_gamma, *received = _sum_blocks_nn(
        f"ffn{tag}_dh", dgu, w_up_t, norm=(x, gamma, dx), swap=parts if swap else ())
    return dx, dx16, d_gamma, d_conv, parts, received


def _local_step(x, target, w, fetch=None, emit=None):
    if fetch is None:
        local = dict(a=(w.get("a_w_in"), w.get("a_w_out")), b=(w.get("b_w_in"), w.get("b_w_out")))
        for layer in range(2):
            local[f"f{layer}"] = (w["f_w_up"][layer], w["f_w_down"][layer]) if "f_w_up" in w else None
        fetch = lambda group, after: local[group]
    swap = emit is not None
    if emit is None:
        emit = lambda group, parts, received, dx: dx
    f_norm = (w["f_norm"][0:1], w["f_norm"][1:2])

    x0 = x
    a_w_in, a_w_out = fetch("a", x0)
    h0, proj = _norm_proj("a_in", x0, w["a_norm"], a_w_in)
    o, og, states = _gla_fwd(proj, w["a_w_gate_up"], w["a_b_gate"], w["a_gn"])
    x1 = _square("a_out", og, a_w_out, NN, x0)
    up0, down0 = fetch("f0", x1)
    x2, ffn0 = _ffn_fwd(0, x1, f_norm[0], up0, w["f_conv"][0], down0)
    b_w_in, b_w_out = fetch("b", x2)
    h2, p = _norm_proj("b_in", x2, w["b_norm"], b_w_in)
    y = _sc_mid_fwd(p, w["b_conv"])
    x3 = _square("b_out", y, b_w_out, NN, x2)
    up1, down1 = fetch("f1", x3)
    x4, ffn1 = _ffn_fwd(1, x3, f_norm[1], up1, w["f_conv"][1], down1)
    loss, dx, dx16, d_final_norm = _loss_head(x4, w["final_norm"], target)

    dx, dx16, d_f_norm1, d_fconv1, parts_f1, got = _ffn_bwd(
        1, x3, f_norm[1], up1, w["f_conv"][1], down1, ffn1, dx, dx16, swap)
    dx16 = emit("f1", parts_f1, got, dx16)

    dy = _square("b_dy", dx16, b_w_out, NT)
    d_b_w_out = _owner_blocks(_wgrad_cols_tn("b_dwout", y, OUT_TILE, dx16))
    db, dc, dhh, d_b_conv = _sc_mid_bwd(p, w["b_conv"], dy)
    dp = jnp.concatenate([db, dc, dhh], axis=1)
    parts_b = (_wgrad_cols_transposed_tn("b_dwin", h2, dp, B_SHARD), d_b_w_out)
    dx, dx16, d_b_norm, *got = _sum_cols_nt("b_dh", dp, b_w_in, norm=(x2, w["b_norm"], dx), swap=parts_b if swap else ())
    dx16 = emit("b", parts_b, got, dx16)

    dx, dx16, d_f_norm0, d_fconv0, parts_f0, got = _ffn_bwd(
        0, x1, f_norm[0], up0, w["f_conv"][0], down0, ffn0, dx, dx16, swap)
    dx16 = emit("f0", parts_f0, got, dx16)

    dog = _square("a_dog", dx16, a_w_out, NT)
    d_a_w_out = _owner_blocks(_wgrad_cols_tn("a_dwout", og, OUT_TILE, dx16))
    dproj, d_wgu, d_b_gate, d_gn = _gla_bwd(proj, w["a_w_gate_up"], w["a_b_gate"], w["a_gn"], o, states, dog)
    parts_a = (_owner_blocks(_wgrad_cols_tn("a_dwin", dproj, PA_TILE, h0), PROJ_A), d_a_w_out)
    dx, _, d_a_norm, *got = _wide_nn("a_dh", dproj, a_w_in, norm=(x0, w["a_norm"], dx), swap=parts_a if swap else ())
    emit("a", parts_a, got, dx)

    grads = dict(
        a_norm=d_a_norm, a_w_in=parts_a[0], a_w_gate_up=d_wgu, a_b_gate=d_b_gate, a_gn=d_gn, a_w_out=parts_a[1],
        b_norm=d_b_norm, b_w_in=parts_b[0], b_conv=d_b_conv, b_w_out=parts_b[1],
        f_norm=(d_f_norm0, d_f_norm1), f_w_up=(parts_f0[0], parts_f1[0]), f_conv=(d_fconv0, d_fconv1),
        f_w_down=(parts_f0[1], parts_f1[1]), final_norm=d_final_norm)
    grads["loss"] = loss
    return dx, grads


MESH_ID = pl.DeviceIdType.MESH
ANY = pl.BlockSpec(memory_space=pl.ANY)
N_PEERS = N_DEV - 1


def _position():
    return lax.axis_index("x"), lax.axis_index("y"), lax.axis_index("c")


def _slot(px, py, pc):
    return 4 * px + 2 * py + pc


GATHER_COPIES = 8
HALF_ROWS = 16


def _gather_copies(src, out, send_sems, recv_sems, local_sems):
    n = len(src)
    to_sibling, to_x, to_y, x_on_to_y, y_on_to_x, x_to_sibling, y_to_sibling, diagonal_to_sibling = range(GATHER_COPIES)
    x, y, c = _position()
    me, sibling = (x, y, c), (x, y, 1 - c)
    x_side, y_side, diagonal = (1 - x, y), (x, 1 - y), (1 - x, 1 - y)

    def rows_of(t, half):
        rows = src[t].shape[0]
        half_rows = rows // 2 // HALF_ROWS * HALF_ROWS
        return (pl.ds(0, rows), pl.ds(0, half_rows), pl.ds(half_rows, rows - half_rows))[half]

    def copy(t, j, block, to, half=0, from_input=False):
        dst = out[t].at[_slot(*block), rows_of(t, half)]
        return pltpu.make_async_remote_copy(
            src_ref=src[t] if from_input else dst, dst_ref=dst, send_sem=send_sems.at[GATHER_COPIES * t + j],
            recv_sem=recv_sems.at[GATHER_COPIES * t + j], device_id=to, device_id_type=MESH_ID)

    mine = [pltpu.make_async_copy(src[t], out[t].at[_slot(*me)], local_sems.at[t]) for t in range(n)]
    for cp in mine:
        cp.start()
    sent = []

    def start(cp):
        cp.start()
        sent.append(cp)

    for t in range(n):
        start(copy(t, to_sibling, me, sibling, from_input=True))
        start(copy(t, to_x, me, (*x_side, c), from_input=True))
        start(copy(t, to_y, me, (*y_side, c), from_input=True))
    for t in range(n):
        copy(t, to_x, (*x_side, c), me).wait_recv()
        start(copy(t, x_on_to_y, (*x_side, c), (*y_side, c), half=1))
        start(copy(t, x_to_sibling, (*x_side, c), sibling))
        copy(t, to_y, (*y_side, c), me).wait_recv()
        start(copy(t, y_on_to_x, (*y_side, c), (*x_side, c), half=2))
        start(copy(t, y_to_sibling, (*y_side, c), sibling))
    for t in range(n):
        copy(t, x_on_to_y, (*diagonal, c), me, half=1).wait_recv()
        copy(t, y_on_to_x, (*diagonal, c), me, half=2).wait_recv()
        start(copy(t, diagonal_to_sibling, (*diagonal, c), sibling))
    for t in range(n):
        copy(t, to_sibling, sibling, me).wait_recv()
        for j, chip in ((x_to_sibling, x_side), (y_to_sibling, y_side), (diagonal_to_sibling, diagonal)):
            copy(t, j, (*chip, 1 - c), me).wait_recv()
    for cp in sent:
        cp.wait_send()
    for cp in mine:
        cp.wait()


def _all_gather(name, shards):
    n = len(shards)

    def body(*refs):
        _gather_copies(refs[:n], refs[n:2 * n], *refs[2 * n:])

    sems = pltpu.SemaphoreType.DMA((GATHER_COPIES * n,))
    return pl.pallas_call(
        body, name=name, in_specs=[ANY] * n, out_specs=[ANY] * n,
        out_shape=[jax.ShapeDtypeStruct((N_DEV,) + s.shape, s.dtype) for s in shards],
        scratch_shapes=[sems, sems, pltpu.SemaphoreType.DMA((n,))],
    )(*shards)


SIBLING_AND_NEIGHBOURS = (1, 2, 4)
SAME_CORE = (2, 4, 6)


def _flip(x, y, c, k):
    return x ^ (k >> 2), y ^ ((k >> 1) & 1), c ^ (k & 1)


N_CHIPS = N_DEV // 2


def _chip(px, py):
    return 2 * px + py


def _pair_copies(parts, received, send_sems, recv_sems):
    x, y, c = lax.axis_index("x"), lax.axis_index("y"), lax.axis_index("c")
    sibling = (x, y, 1 - c)
    copies = []
    for t in range(len(parts)):
        for q in range(N_DEV // 2):
            send = pltpu.make_async_remote_copy(
                src_ref=parts[t].at[2 * q + 1 - c], dst_ref=received[t].at[q], send_sem=send_sems.at[t, q],
                recv_sem=recv_sems.at[t, q], device_id=sibling, device_id_type=pl.DeviceIdType.MESH)
            landed = received[t].at[q]
            arrival = pltpu.make_async_remote_copy(
                src_ref=landed, dst_ref=landed, send_sem=send_sems.at[t, q], recv_sem=recv_sems.at[t, q],
                device_id=sibling, device_id_type=pl.DeviceIdType.MESH)
            copies.append((send, arrival))
    return copies


PAIR_ROWS = 1024


def _pair_add(name, part, received, side):
    _, rows, cols = part.shape
    tiles = [t for t in range(PAIR_ROWS, 0, -BF16_ROWS) if rows % t == 0]
    tr = tiles[0] if tiles else rows

    def body(side_ref, p_ref, r_ref, o_ref):
        o_ref[...] = (p_ref[...].astype(F32) + r_ref[...].astype(F32)).astype(BF16)

    tile = _spec((None, tr, cols), lambda q, i, side_ref: (q, i, 0))
    return pl.pallas_call(
        body, name=name,
        grid_spec=pltpu.PrefetchScalarGridSpec(
            num_scalar_prefetch=1, grid=(N_CHIPS, rows // tr),
            in_specs=[_spec((None, tr, cols), lambda q, i, side_ref: (2 * q + side_ref[0], i, 0)), tile], out_specs=tile),
        out_shape=jax.ShapeDtypeStruct((N_CHIPS, rows, cols), BF16), compiler_params=_params(("parallel", "parallel")),
    )(side, part, received)


def _send_copy(parts, landing, send_sems, recv_sems, t, s, k):
    x, y, c = _position()
    px, py, _ = _flip(x, y, c, k)
    return pltpu.make_async_remote_copy(
        src_ref=parts[t].at[_chip(px, py)], dst_ref=landing[t].at[_chip(x, y)], send_sem=send_sems.at[s],
        recv_sem=recv_sems.at[s], device_id=(px, py, c), device_id_type=MESH_ID)


def _send_arrival(landing, send_sems, recv_sems, t, s, k):
    x, y, c = _position()
    px, py, _ = _flip(x, y, c, k)
    landed = landing[t].at[_chip(px, py)]
    return pltpu.make_async_remote_copy(
        src_ref=landed, dst_ref=landed, send_sem=send_sems.at[s], recv_sem=recv_sems.at[s],
        device_id=(px, py, c), device_id_type=MESH_ID)


def _handshake(peers):
    x, y, c = _position()
    barrier = pltpu.get_barrier_semaphore()
    for k in peers:
        pl.semaphore_signal(barrier, inc=1, device_id=_flip(x, y, c, k), device_id_type=MESH_ID)
    pl.semaphore_wait(barrier, len(peers))


def _sequencer(name, collective_id, n_copies, body, operands, out_type):
    n_arrays = len(operands)
    return pl.kernel(
        body, out_type=out_type, mesh=plsc.ScalarSubcoreMesh(axis_name="sequencer", num_cores=1), name=name,
        scratch_types=(pltpu.SemaphoreType.DMA((n_copies,)), pltpu.SemaphoreType.DMA((n_copies,)),
                       pltpu.SemaphoreType.DMA((n_arrays,))),
        compiler_params=pltpu.CompilerParams(collective_id=collective_id))(*operands)


def _sequencer_exchange(name, collective_id, parts, after=()):
    n, n_peers, n_in = len(parts), len(SAME_CORE), len(parts) + len(after)

    def body(*refs):
        src, landing = refs[:n], refs[n_in:n_in + n]
        send_sems, recv_sems, local_sems = refs[n_in + n:]
        _handshake(SAME_CORE)
        x, y, _ = _position()
        mine = [pltpu.make_async_copy(src[t].at[_chip(x, y)], landing[t].at[_chip(x, y)], local_sems.at[t]) for t in range(n)]
        for cp in mine:
            cp.start()
        sent = [_send_copy(src, landing, send_sems, recv_sems, t, t * n_peers + j, k)
                for t in range(n) for j, k in enumerate(SAME_CORE)]
        for cp in sent:
            cp.start()
        for t in range(n):
            for j, k in enumerate(SAME_CORE):
                _send_arrival(landing, send_sems, recv_sems, t, t * n_peers + j, k).wait_recv()
        for cp in sent:
            cp.wait_send()
        for cp in mine:
            cp.wait()

    landing = [jax.ShapeDtypeStruct(p.shape, p.dtype) for p in parts]
    return _sequencer(name, collective_id, n * n_peers, body, list(parts) + list(after), landing)


def _sequencer_gather(name, collective_id, shards):
    n = len(shards)

    def body(*refs):
        _handshake(SIBLING_AND_NEIGHBOURS)
        _gather_copies(refs[:n], refs[n:2 * n], *refs[2 * n:])

    gathered = [jax.ShapeDtypeStruct((N_DEV,) + s.shape, s.dtype) for s in shards]
    return _sequencer(name, collective_id, GATHER_COPIES * n, body, shards, gathered)


ADAM_ROWS = 512
BF16_ROWS = 16


def _adam_update(w, g, m, v):
    m = ADAM_B1 * m + (1.0 - ADAM_B1) * g
    v = ADAM_B2 * v + (1.0 - ADAM_B2) * (g * g)
    m_hat = m / (1.0 - ADAM_B1 ** ADAM_STEP)
    v_hat = v / (1.0 - ADAM_B2 ** ADAM_STEP)
    delta = -ADAM_LR * (m_hat / (jnp.sqrt(v_hat) + ADAM_EPS) + ADAM_WD * w)
    return delta, m, v


def _sum_slots(ref):
    total = ref[0].astype(F32)
    for d in range(1, ref.shape[0]):
        total = total + ref[d].astype(F32)
    return total


def _adamw_sum(name, landed, w, m, v):
    layers, rows, cols = w.shape
    tiles = [t for t in range(ADAM_ROWS, 0, -BF16_ROWS) if rows % t == 0]
    tr = tiles[0] if tiles else rows
    nt = rows // tr

    def body(*refs):
        parts = refs[:layers]
        w_ref, m_ref, v_ref, g_ref, d_ref, nm_ref, nv_ref = refs[layers:]
        layer = pl.program_id(0)
        g = _sum_slots(parts[0])
        for q in range(1, layers):
            g = jnp.where(layer == q, _sum_slots(parts[q]), g)
        delta, new_m, new_v = _adam_update(w_ref[...], g, m_ref[...], v_ref[...])
        g_ref[...] = g
        d_ref[...] = delta
        nm_ref[...] = new_m
        nv_ref[...] = new_v

    def part_spec(q):
        return _spec((N_CHIPS, tr, cols), lambda l, i: (0, jnp.where(l == q, i, jnp.where(l < q, 0, nt - 1)), 0))

    tile = _spec((None, tr, cols), lambda l, i: (l, i, 0))
    out = jax.ShapeDtypeStruct((layers, rows, cols), F32)
    return pl.pallas_call(
        body, name=name, grid=(layers, nt), in_specs=[part_spec(q) for q in range(layers)] + [tile] * 3,
        out_specs=[tile] * 4, out_shape=[out] * 4, compiler_params=_params(("arbitrary", "arbitrary")),
    )(*landed, w, m, v)


def _sum_small(landed):
    def body(in_ref, out_ref):
        out_ref[...] = _sum_slots(in_ref)

    return pl.pallas_call(body, name="small_grad_sum", out_shape=jax.ShapeDtypeStruct(landed.shape[1:], F32))(landed)


def _adamw_small(arrays):
    n = len(arrays)

    def body(*refs):
        for i in range(n):
            g_ref, w_ref, m_ref, v_ref = refs[4 * i:4 * i + 4]
            d_ref, nm_ref, nv_ref = refs[4 * n + 3 * i:4 * n + 3 * i + 3]
            d_ref[...], nm_ref[...], nv_ref[...] = _adam_update(w_ref[...], g_ref[...], m_ref[...], v_ref[...])

    out = [jax.ShapeDtypeStruct(w.shape, F32) for _, w, _, _ in arrays for _ in range(3)]
    flat = pl.pallas_call(body, name="adam_small", out_shape=out)(*[a for group in arrays for a in group])
    return [tuple(flat[3 * i:3 * i + 3]) for i in range(n)]


LANES = 128
SUBLANES = 8
F_CONV_SHARD = D_FF // N_DEV
GATE_SHARD = KEY_DIM // N_DEV
NORM_SHARD = D_MODEL // N_DEV


def _tile_rows(a):
    flat = a.reshape(-1)
    size = -(-flat.shape[0] // (SUBLANES * LANES)) * SUBLANES * LANES
    return jnp.pad(flat, (0, size - flat.shape[0])).reshape(-1, LANES)


def _pack_rows(pieces):
    return jnp.concatenate([_tile_rows(p) for p in pieces], axis=0)


def _unpack_rows(packed, shapes):
    out, row = [], 0
    for shape in shapes:
        size = 1
        for s in shape:
            size *= s
        rows = -(-size // (SUBLANES * LANES)) * SUBLANES
        piece = packed[..., row:row + rows, :]
        out.append(piece.reshape(piece.shape[:-2] + (rows * LANES,))[..., :size])
        row += rows
    return out


SMALL_SHARDS = ((GATE_RANK, GATE_SHARD), (1, NORM_SHARD), (3, NORM_SHARD), (2, 3, F_CONV_SHARD))


def _unpack_small_shards(g):
    gate, b_norm, b_conv, f_conv = _unpack_rows(g, SMALL_SHARDS)
    gate = gate.reshape(N_DEV, GATE_RANK, GATE_SHARD).transpose(1, 0, 2).reshape(GATE_RANK, KEY_DIM)
    b_norm = b_norm.reshape(1, D_MODEL)
    b_conv = b_conv.reshape(N_DEV, 3, NORM_SHARD).transpose(1, 0, 2).reshape(3, D_MODEL)
    f_conv = f_conv.reshape(N_DEV, 2, 3, F_CONV_SHARD).transpose(1, 2, 0, 3).reshape(2, 3, D_FF)
    return gate, b_norm, b_conv, f_conv


SMALL_LAYOUT = (("a_norm", (1, D_MODEL)), ("a_w_gate_up", (GATE_RANK, KEY_DIM)), ("a_b_gate", (1, KEY_DIM)), ("a_gn", (1, VALUE_DIM)),
                ("b_norm", (1, D_MODEL)), ("b_conv", (3, D_MODEL)), ("f_norm0", (1, D_MODEL)), ("f_norm1", (1, D_MODEL)),
                ("f_conv0", (3, D_FF)), ("f_conv1", (3, D_FF)), ("final_norm", (1, D_MODEL)), ("loss", (1, LANES)))


def _pack_small_grads(g):
    full = dict(g)
    full["a_w_gate_up"] = g["a_w_gate_up"][:GATE_RANK]
    for layer in range(2):
        full[f"f_norm{layer}"] = g["f_norm"][layer]
        full[f"f_conv{layer}"] = g["f_conv"][layer]
    return _pack_rows([full[name] for name, _ in SMALL_LAYOUT])


def _unpack_small_grads(packed):
    pieces = _unpack_rows(packed, [shape for _, shape in SMALL_LAYOUT])
    out = {name: piece.reshape(shape) for (name, shape), piece in zip(SMALL_LAYOUT, pieces)}
    out["f_norm"] = jnp.stack([out["f_norm0"][0], out["f_norm1"][0]])
    out["f_conv"] = jnp.stack([out["f_conv0"], out["f_conv1"]])
    return out


def kernel(x, a_norm, a_w_in, a_w_gate_up, a_b_gate, a_gn, a_w_out, b_norm, b_w_in, b_conv, b_w_out, f_norm, f_w_up, f_conv, f_w_down, final_norm, loss_target, m_a_norm, m_a_w_in, m_a_w_gate_up, m_a_b_gate, m_a_gn, m_a_w_out, m_b_norm, m_b_w_in, m_b_conv, m_b_w_out, m_f_norm, m_f_w_up, m_f_conv, m_f_w_down, m_final_norm, v_a_norm, v_a_w_in, v_a_w_gate_up, v_a_b_gate, v_a_gn, v_a_w_out, v_b_norm, v_b_w_in, v_b_conv, v_b_w_out, v_f_norm, v_f_w_up, v_f_conv, v_f_w_down, v_final_norm):
    my_slot = _slot(*_position())

    transposed = lambda w: jnp.swapaxes(w, 1, 2)
    a_transposed = lambda w: w.reshape(D_MODEL, A_SHARD).T.reshape(1, A_SHARD, D_MODEL)
    a_w_in_t, f_w_up_t = a_transposed(a_w_in), transposed(f_w_up)
    first = _all_gather("weight_gather", [a_w_in_t[0].astype(BF16), a_w_out[0].astype(BF16),
                                          _pack_rows([a_w_gate_up[0], b_norm, b_conv[0], f_conv])])
    gathers, small_shards = {}, first[2]
    later = (("f0", f_w_up_t[0], f_w_down[0]), ("b", b_w_in[0], b_w_out[0]), ("f1", f_w_up_t[1], f_w_down[1]))
    for collective_id, (group, w_in, w_out) in enumerate(later):
        w_in, w_out, small_shards = lax.optimization_barrier((w_in.astype(BF16), w_out.astype(BF16), small_shards))
        gathers[group] = _sequencer_gather(f"gather_{group}", collective_id, [w_in, w_out])
    gate_full, b_norm_full, b_conv_full, f_conv_full = _unpack_small_shards(small_shards)
    a_w_in_full = jnp.pad(first[0].reshape(PROJ_A, D_MODEL), ((0, PROJ_A_PAD - PROJ_A), (0, 0)))
    weights = dict(
        a_norm=a_norm, a_w_gate_up=jnp.pad(gate_full, ((0, GATE_PAD - GATE_RANK), (0, 0))).astype(BF16), a_b_gate=a_b_gate,
        a_gn=a_gn, b_norm=b_norm_full, b_conv=b_conv_full, f_norm=f_norm, f_conv=f_conv_full,
        final_norm=final_norm.reshape(1, D_MODEL))

    def fetch(group, after):
        if group == "a":
            return a_w_in_full, first[1].reshape(D_MODEL, D_MODEL)
        w_in, w_out = gathers[group]
        if group == "b":
            return w_in, w_out.reshape(D_MODEL, D_MODEL)
        return w_in.reshape(2, D_FF, D_MODEL), w_out.reshape(D_FF, D_MODEL)

    exchanges, pending = {}, []
    exchange_ids = dict(b=3, f0=4, a=5)
    side = lax.axis_index("c").astype(jnp.int32).reshape(1)

    def emit(group, parts, received, carry):
        sums = [_pair_add(f"pair_add_{group}_{i}", part, got, side) for i, (part, got) in enumerate(zip(parts, received))]
        carry, *sums = lax.optimization_barrier((carry, *sums))
        pending.extend(sums)
        if group != "f1":
            after = list(exchanges.values())[-1][:1] if exchanges else ()
            exchanges[group] = _sequencer_exchange(f"grads_{group}", exchange_ids[group], list(pending), after)
            pending.clear()
        return carry

    dx, g = _local_step(x[0], loss_target[0], weights, fetch, emit)

    (up1, down1, d_b_in, d_b_out), (up0, down0), (d_a_in, d_a_out) = (exchanges[group] for group in ("b", "f0", "a"))
    back = lambda results: tuple(transposed(r) for r in results)
    big = dict(
        b_w_in=_adamw_sum("adam_b_w_in", [d_b_in], b_w_in, m_b_w_in, v_b_w_in),
        b_w_out=_adamw_sum("adam_b_w_out", [d_b_out], b_w_out, m_b_w_out, v_b_w_out),
        f_w_up=back(_adamw_sum("adam_f_w_up", [up0, up1], f_w_up_t, transposed(m_f_w_up), transposed(v_f_w_up))),
        f_w_down=_adamw_sum("adam_f_w_down", [down0, down1], f_w_down, m_f_w_down, v_f_w_down))
    small_packed, *updated = lax.optimization_barrier((_pack_small_grads(g), *big["f_w_down"]))
    big["f_w_down"] = tuple(updated)
    small_landed = _all_gather("small_grad_gather", [small_packed])[0]
    big.update(
        a_w_in=tuple(r.reshape(A_SHARD, D_MODEL).T.reshape(1, D_MODEL, A_SHARD) for r in _adamw_sum(
            "adam_a_w_in", [d_a_in], a_w_in_t, a_transposed(m_a_w_in), a_transposed(v_a_w_in))),
        a_w_out=_adamw_sum("adam_a_w_out", [d_a_out], a_w_out, m_a_w_out, v_a_w_out))
    small_g = _unpack_small_grads(_sum_small(small_landed))
    loss = small_g["loss"][0, 0]
    small_g["a_w_gate_up"] = lax.dynamic_slice_in_dim(small_g["a_w_gate_up"], my_slot * GATE_SHARD, GATE_SHARD, axis=1)
    small_g["b_norm"] = lax.dynamic_slice_in_dim(small_g["b_norm"], my_slot * NORM_SHARD, NORM_SHARD, axis=1)
    small_g["b_conv"] = lax.dynamic_slice_in_dim(small_g["b_conv"], my_slot * NORM_SHARD, NORM_SHARD, axis=1)
    small_g["f_conv"] = lax.dynamic_slice_in_dim(small_g["f_conv"], my_slot * F_CONV_SHARD, F_CONV_SHARD, axis=2)
    small_w = dict(
        a_norm=(a_norm, m_a_norm, v_a_norm), a_w_gate_up=(a_w_gate_up, m_a_w_gate_up, v_a_w_gate_up),
        a_b_gate=(a_b_gate, m_a_b_gate, v_a_b_gate), a_gn=(a_gn, m_a_gn, v_a_gn), b_norm=(b_norm, m_b_norm, v_b_norm),
        b_conv=(b_conv, m_b_conv, v_b_conv), f_norm=(f_norm, m_f_norm, v_f_norm), f_conv=(f_conv, m_f_conv, v_f_conv),
        final_norm=(final_norm, m_final_norm, v_final_norm))
    two_d = lambda a: a.reshape(-1, a.shape[-1])
    updates = _adamw_small([tuple(two_d(a.reshape(w.shape)) for a in (small_g[name], w, m, v)) for name, (w, m, v) in small_w.items()])
    small = {}
    for (name, (w, _, _)), update in zip(small_w.items(), updates):
        small[name] = (small_g[name].reshape(w.shape),) + tuple(u.reshape(w.shape) for u in update)

    order = ["a_norm", "a_w_in", "a_w_gate_up", "a_b_gate", "a_gn", "a_w_out", "b_norm", "b_w_in", "b_conv", "b_w_out",
             "f_norm", "f_w_up", "f_conv", "f_w_down", "final_norm"]
    results = {**big, **small}
    outputs = [loss, dx.reshape(1, SEQ, D_MODEL)]
    for kind in range(4):
        outputs += [results[name][kind] for name in order]
    return tuple(outputs)
```

```python
import jax
import jax.numpy as jnp
from jax import lax
from jax.experimental import pallas as pl
from jax.experimental.pallas import tpu as pltpu
from jax.experimental.pallas import tpu_sc as plsc

F32 = jnp.float32
BF16 = jnp.bfloat16

N_DEV = 8
SEQ = 2048
D_MODEL = 1024
CHUNK = 64
N_CHUNKS = SEQ // CHUNK
RMS_EPS = 1e-6
GLA_HEADS = 4
KEY_DIM = 512
VALUE_DIM = 1024
HEAD_K = KEY_DIM // GLA_HEADS
HEAD_V = VALUE_DIM // GLA_HEADS
GATE_RANK = 16
GATE_PAD = 128
GATE_NORMALIZER = 16.0
PROJ_A = 2 * KEY_DIM + 2 * VALUE_DIM + GATE_RANK
PROJ_A_PAD = 2 * KEY_DIM + 2 * VALUE_DIM + GATE_PAD
A_SHARD = PROJ_A // N_DEV
B_SHARD = 3 * D_MODEL // N_DEV
D_FF = 2816
ADAM_LR = 0.001
ADAM_B1 = 0.9
ADAM_B2 = 0.999
ADAM_EPS = 1e-08
ADAM_WD = 0.01
ADAM_STEP = 10
MESH_AXES = ("x", "y", "c")

VMEM_LIMIT = 56 * 1024 * 1024
ROW_CHUNK = 256
HALO = 16


def _params(sem=None, vmem=VMEM_LIMIT):
    return pltpu.CompilerParams(dimension_semantics=sem, vmem_limit_bytes=vmem)


NN = ((1,), (0,))
NT = ((1,), (1,))
TN = ((0,), (0,))


def _matmul(name, a, a_spec, b, b_spec, dims, grid, out_shape, out_spec, k_blocks=None, a_block_cols=None, res=None,
            res_spec=None, transpose_out=False, norm=None, swap=()):
    has_res = res is not None
    n_swap = len(swap)

    def body(*refs):
        a_ref, b_ref = refs[0], refs[1]
        r_ref = refs[2] if has_res else None

        def product(lhs, rhs):
            return lax.dot_general(lhs.astype(BF16), rhs, (dims, ((), ())), preferred_element_type=F32)

        if k_blocks is None:
            v = product(a_ref[...], b_ref[...])
        else:
            v = None
            for k in range(k_blocks):
                lhs = a_ref[k] if a_block_cols is None else a_ref[:, k * a_block_cols:(k + 1) * a_block_cols]
                p = product(lhs, b_ref[k])
                v = p if v is None else v + p
        if transpose_out:
            v = v.T
        if has_res:
            v = v + r_ref[...]
        if norm is None:
            o_ref = refs[2 + has_res]
            o_ref[...] = v.astype(o_ref.dtype)
            return
        n_in = 5 + has_res
        x_ref, g_ref, dxi_ref = refs[2 + has_res:n_in]
        dx_ref, dx16_ref, dg_ref = refs[n_in + n_swap:n_in + n_swap + 3]
        if n_swap:
            copies = _pair_copies(refs[n_in:n_in + n_swap], refs[n_in + n_swap + 3:n_in + 2 * n_swap + 3], *refs[-2:])

            @pl.when(pl.program_id(0) == 0)
            def _():
                for send, _ in copies:
                    send.start()

            @pl.when(pl.program_id(0) == grid[0] - 1)
            def _():
                for send, arrival in copies:
                    arrival.wait_recv()
                    send.wait_send()

        dx, dg = _norm_bwd_rows(x_ref[...], g_ref[...], v)
        dx = dxi_ref[...] + dx
        dx_ref[...] = dx
        dx16_ref[...] = dx.astype(BF16)

        @pl.when(pl.program_id(0) == 0)
        def _():
            dg_ref[...] = dg

        @pl.when(pl.program_id(0) > 0)
        def _():
            dg_ref[...] += dg

    operands = [a, b] + ([res] if has_res else [])
    in_specs = [a_spec, b_spec] + ([res_spec] if has_res else [])
    semantics = ("parallel",) * len(grid)
    scratch = []
    if norm is not None:
        vec = _spec((1, D_MODEL), lambda i: (0, 0))
        any_space = pl.BlockSpec(memory_space=pl.ANY)
        operands += list(norm) + list(swap)
        in_specs += [out_spec, vec, out_spec] + [any_space] * n_swap
        out_shape = [_act(dtype=F32), _act(), jax.ShapeDtypeStruct((1, D_MODEL), F32)]
        out_shape += [jax.ShapeDtypeStruct((N_DEV // 2,) + p.shape[1:], p.dtype) for p in swap]
        out_spec = [out_spec, out_spec, vec] + [any_space] * n_swap
        semantics = ("arbitrary",)
        if n_swap:
            scratch = [pltpu.SemaphoreType.DMA((n_swap, N_DEV // 2))] * 2
    return pl.pallas_call(
        body, name=name, grid=grid, in_specs=in_specs, out_specs=out_spec, out_shape=out_shape, scratch_shapes=scratch,
        compiler_params=_params(semantics),
    )(*operands)


def _resident(shape):
    return pl.BlockSpec(shape, lambda *_: (0,) * len(shape), pipeline_mode=pl.Buffered(1))


TM = 512
N_TM = SEQ // TM
PA_TILE = 640
N_PA = PROJ_A_PAD // PA_TILE
OUT_TILE = 256


def _spec(shape, fn):
    return pl.BlockSpec(shape, fn)


def _act(shape=(SEQ, D_MODEL), dtype=BF16):
    return jax.ShapeDtypeStruct(shape, dtype)


def _norm_proj(name, x, gamma, w):
    blocks = w.ndim == 3
    n_out = w.shape[0] * w.shape[2] if blocks else w.shape[0]

    def body(x_ref, g_ref, w_ref, h_ref, o_ref):
        x = x_ref[...]
        h = (x * _rstd(x) * g_ref[...]).astype(BF16)
        h_ref[...] = h
        if blocks:
            n = w.shape[2]
            for j in range(w.shape[0]):
                o_ref[:, j * n:(j + 1) * n] = jnp.dot(h, w_ref[j], preferred_element_type=F32).astype(BF16)
        else:
            o_ref[...] = lax.dot_general(h, w_ref[...], (NT, ((), ())), preferred_element_type=F32).astype(BF16)

    row = _spec((TM, D_MODEL), lambda i: (i, 0))
    return pl.pallas_call(
        body, name=name, grid=(N_TM,), in_specs=[row, _resident((1, D_MODEL)), _resident(w.shape)],
        out_specs=[row, _spec((TM, n_out), lambda i: (i, 0))], out_shape=[_act(), _act((SEQ, n_out))],
        compiler_params=_params(("parallel",)),
    )(x, gamma, w)


def _square(name, a, w, dims, x=None):
    row = _spec((TM, D_MODEL), lambda i: (i, 0))
    return _matmul(name, a, row, w, _resident((D_MODEL, D_MODEL)), dims, (N_TM,),
                   _act(dtype=F32 if x is not None else BF16), row, res=x, res_spec=row if x is not None else None)


def _sum_blocks_nn(name, a_blocks, w_blocks, x=None, norm=None, swap=()):
    nb, _, n = a_blocks.shape
    row = _spec((TM, D_MODEL), lambda i: (i, 0))
    return _matmul(name, a_blocks, _spec((nb, TM, n), lambda i: (0, i, 0)), w_blocks, _resident((nb, n, D_MODEL)),
                   NN, (N_TM,), _act(dtype=F32), row, k_blocks=nb, res=x, res_spec=row if x is not None else None, norm=norm, swap=swap)


def _sum_cols_nt(name, d, w_blocks, norm=None, swap=()):
    nb, _, n = w_blocks.shape
    return _matmul(name, d, _spec((TM, nb * n), lambda i: (i, 0)), w_blocks, _resident((nb, D_MODEL, n)), NT,
                   (N_TM,), _act(dtype=F32), _spec((TM, D_MODEL), lambda i: (i, 0)), k_blocks=nb, a_block_cols=n, norm=norm, swap=swap)


def _wide_nn(name, d, wt, x=None, norm=None, swap=()):
    n = wt.shape[0]
    row = _spec((TM, D_MODEL), lambda i: (i, 0))
    return _matmul(name, d, _spec((TM, n), lambda i: (i, 0)), wt, _resident((n, D_MODEL)), NN, (N_TM,),
                   _act(dtype=F32), row, res=x, res_spec=row if x is not None else None, norm=norm, swap=swap)


def _wide_nt(name, d, w):
    n = w.shape[0]
    return _matmul(name, d, _spec((TM, D_MODEL), lambda i: (i, 0)), w, _resident((n, D_MODEL)), NT, (N_TM,),
                   _act((SEQ, n)), _spec((TM, n), lambda i: (i, 0)))


def _wgrad_halves_tn(name, d, n_tile, h):
    _, _, n = d.shape
    return _matmul(name, d, _spec((None, SEQ, n_tile), lambda p, j: (p, 0, j)), h, _resident((SEQ, D_MODEL)), TN,
                   (2, n // n_tile), _act((2, n, D_MODEL)), _spec((None, n_tile, D_MODEL), lambda p, j: (p, j, 0)))


def _wgrad_cols_tn(name, d, n_tile, h):
    n = d.shape[1]
    return _matmul(name, d, _spec((SEQ, n_tile), lambda j: (0, j)), h, _resident((SEQ, D_MODEL)), TN,
                   (n // n_tile,), _act((n, D_MODEL)), _spec((n_tile, D_MODEL), lambda j: (j, 0)))


def _wgrad_cols_transposed_tn(name, h, d, n_tile):
    nb = d.shape[1] // n_tile
    return _matmul(name, d, _spec((SEQ, n_tile), lambda j: (0, j)), h, _resident((SEQ, D_MODEL)), TN, (nb,),
                   _act((nb, D_MODEL, n_tile)), _spec((None, D_MODEL, n_tile), lambda j: (j, 0, 0)), transpose_out=True)


NORM_ROWS = 512


def _rstd(x):
    return lax.rsqrt(jnp.mean(x * x, axis=-1, keepdims=True) + RMS_EPS)


def _norm_bwd_rows(x, gamma, dh):
    r = _rstd(x)
    xh = x * r
    dxh = dh * gamma
    dx = r * (dxh - xh * jnp.mean(dxh * xh, axis=-1, keepdims=True))
    return dx, jnp.sum(dh * xh, axis=0, keepdims=True)


def _loss_head(x, gamma, target):
    def body(x_ref, g_ref, t_ref, loss_ref, dx_ref, dx16_ref, dg_ref):
        x = x_ref[...]
        gamma = g_ref[...]
        err = x * _rstd(x) * gamma - t_ref[...]
        dy = err * (1.0 / D_MODEL)
        dx, dg = _norm_bwd_rows(x, gamma, dy)
        dx_ref[...] = dx
        dx16_ref[...] = dx.astype(BF16)
        part = 0.5 * jnp.sum(jnp.sum(err * err, axis=-1, keepdims=True) * (1.0 / D_MODEL), axis=0, keepdims=True)
        part = jnp.broadcast_to(part, loss_ref.shape)

        @pl.when(pl.program_id(0) == 0)
        def _():
            dg_ref[...] = dg
            loss_ref[...] = part

        @pl.when(pl.program_id(0) > 0)
        def _():
            dg_ref[...] += dg
            loss_ref[...] += part

    row = _spec((NORM_ROWS, D_MODEL), lambda i: (i, 0))
    vec = _spec((1, D_MODEL), lambda i: (0, 0))
    return pl.pallas_call(
        body, name="loss_head", grid=(SEQ // NORM_ROWS,), in_specs=[row, vec, row],
        out_specs=[_spec((1, 128), lambda i: (0, 0)), row, row, vec],
        out_shape=[jax.ShapeDtypeStruct((1, 128), F32), _act(dtype=F32), _act(), jax.ShapeDtypeStruct((1, D_MODEL), F32)],
        compiler_params=_params(("arbitrary",)),
    )(x, gamma, target)


def _sigmoid(x):
    return 1.0 / (1.0 + jnp.exp(-x))


def _rows(ref, c):
    return ref[pl.ds(pl.multiple_of(c * ROW_CHUNK, ROW_CHUNK), ROW_CHUNK), :].astype(F32)


def _rows_before(ref, c):
    start = pl.multiple_of(jnp.maximum(c * ROW_CHUNK - HALO, 0), HALO)
    rows = ref[pl.ds(start, HALO), :].astype(F32)
    return jnp.where(c > 0, rows, 0.0)


def _rows_after(ref, c, n_chunks):
    start = pl.multiple_of(jnp.minimum((c + 1) * ROW_CHUNK, SEQ - HALO), HALO)
    rows = ref[pl.ds(start, HALO), :].astype(F32)
    return jnp.where(c < n_chunks - 1, rows, 0.0)


def _shift_down(z, before, n):
    return pltpu.roll(jnp.concatenate([before, z], axis=0), n, 0)[before.shape[0]:]


def _shift_up(z, after, n):
    rows = z.shape[0]
    return pltpu.roll(jnp.concatenate([z, after], axis=0), rows + HALO - n, 0)[:rows]


def _conv_rows(z, before, w):
    z1 = _shift_down(z, before, 1)
    z2 = _shift_down(z, before, 2)
    return w[2:3, :] * z + w[1:2, :] * z1 + w[0:1, :] * z2, z1, z2


def _conv_t_rows(dy, after, w):
    return w[2:3, :] * dy + w[1:2, :] * _shift_up(dy, after, 1) + w[0:1, :] * _shift_up(dy, after, 2)


N_ROW_CHUNKS = SEQ // ROW_CHUNK


FF_COLS = 256
N_FF_COLS = D_FF // FF_COLS


def _ffn_mid_bwd(name, gu, conv_w, da):
    def body(gu_ref, w_ref, da_ref, dgu_ref, dw_ref, dgc_ref):
        w = w_ref[...]

        def first(c, acc):
            g = _rows(gu_ref.at[0], c)
            u = _rows(gu_ref.at[1], c)
            d = _rows(da_ref, c)
            gc, g1, g2 = _conv_rows(g, _rows_before(gu_ref.at[0], c), w)
            sg = _sigmoid(gc)
            rows = pl.ds(pl.multiple_of(c * ROW_CHUNK, ROW_CHUNK), ROW_CHUNK)
            dgu_ref[1, rows, :] = (d * gc * sg).astype(BF16)
            dgc = d * u * (sg * (1.0 + gc * (1.0 - sg)))
            dgc_ref[rows, :] = dgc
            return (acc[0] + jnp.sum(dgc * g2, axis=0, keepdims=True), acc[1] + jnp.sum(dgc * g1, axis=0, keepdims=True),
                    acc[2] + jnp.sum(dgc * g, axis=0, keepdims=True))

        zero = jnp.zeros((1, FF_COLS), F32)
        acc = lax.fori_loop(0, N_ROW_CHUNKS, first, (zero, zero, zero))
        for r in range(3):
            dw_ref[r:r + 1, :] = acc[r]

        def second(c, carry):
            dgc = _rows(dgc_ref, c)
            dg = _conv_t_rows(dgc, _rows_after(dgc_ref, c, N_ROW_CHUNKS), w)
            dgu_ref[0, pl.ds(pl.multiple_of(c * ROW_CHUNK, ROW_CHUNK), ROW_CHUNK), :] = dg.astype(BF16)
            return carry

        lax.fori_loop(0, N_ROW_CHUNKS, second, 0)

    pair = _spec((2, SEQ, FF_COLS), lambda j: (0, 0, j))
    wspec = _spec((3, FF_COLS), lambda j: (0, j))
    return pl.pallas_call(
        body, name=name, grid=(N_FF_COLS,), in_specs=[pair, wspec, _spec((SEQ, FF_COLS), lambda j: (0, j))],
        out_specs=[pair, wspec], out_shape=[_act((2, SEQ, D_FF)), jax.ShapeDtypeStruct((3, D_FF), F32)],
        scratch_shapes=[pltpu.VMEM((SEQ, FF_COLS), F32)],
        compiler_params=_params(("parallel",)),
    )(gu, conv_w, da)


SC_COLS = 256
N_SC = D_MODEL // SC_COLS


def _sc_specs():
    return [_spec((SEQ, SC_COLS), lambda j, part=part: (0, part * N_SC + j)) for part in range(3)]


def _sc_mid_fwd(p, conv_w):
    def body(b_ref, c_ref, h_ref, w_ref, y_ref):
        w = w_ref[...]

        def chunk(c, carry):
            z = _rows(c_ref, c) * _rows(h_ref, c)
            before = _rows_before(c_ref, c) * _rows_before(h_ref, c)
            zc, _, _ = _conv_rows(z, before, w)
            y_ref[pl.ds(pl.multiple_of(c * ROW_CHUNK, ROW_CHUNK), ROW_CHUNK), :] = (_rows(b_ref, c) * zc).astype(BF16)
            return carry

        lax.fori_loop(0, N_ROW_CHUNKS, chunk, 0)

    col = _spec((SEQ, SC_COLS), lambda j: (0, j))
    return pl.pallas_call(
        body, name="sc_mid_fwd", grid=(N_SC,), in_specs=_sc_specs() + [_spec((3, SC_COLS), lambda j: (0, j))], out_specs=col,
        out_shape=jax.ShapeDtypeStruct((SEQ, D_MODEL), BF16), compiler_params=_params(("parallel",)),
    )(p, p, p, conv_w)


def _sc_mid_bwd(p, conv_w, dy):
    def body(b_ref, c_ref, h_ref, w_ref, dy_ref, db_ref, dc_ref, dh_ref, dw_ref, dzc_ref):
        w = w_ref[...]

        def first(c, acc):
            z = _rows(c_ref, c) * _rows(h_ref, c)
            before = _rows_before(c_ref, c) * _rows_before(h_ref, c)
            zc, z1, z2 = _conv_rows(z, before, w)
            d = _rows(dy_ref, c)
            rows = pl.ds(pl.multiple_of(c * ROW_CHUNK, ROW_CHUNK), ROW_CHUNK)
            db_ref[rows, :] = (d * zc).astype(BF16)
            dzc = d * _rows(b_ref, c)
            dzc_ref[rows, :] = dzc
            return (acc[0] + jnp.sum(dzc * z2, axis=0, keepdims=True), acc[1] + jnp.sum(dzc * z1, axis=0, keepdims=True),
                    acc[2] + jnp.sum(dzc * z, axis=0, keepdims=True))

        zero = jnp.zeros((1, SC_COLS), F32)
        acc = lax.fori_loop(0, N_ROW_CHUNKS, first, (zero, zero, zero))
        for r in range(3):
            dw_ref[r:r + 1, :] = acc[r]

        def second(c, carry):
            dz = _conv_t_rows(_rows(dzc_ref, c), _rows_after(dzc_ref, c, N_ROW_CHUNKS), w)
            rows = pl.ds(pl.multiple_of(c * ROW_CHUNK, ROW_CHUNK), ROW_CHUNK)
            dc_ref[rows, :] = (dz * _rows(h_ref, c)).astype(BF16)
            dh_ref[rows, :] = (dz * _rows(c_ref, c)).astype(BF16)
            return carry

        lax.fori_loop(0, N_ROW_CHUNKS, second, 0)

    col = _spec((SEQ, SC_COLS), lambda j: (0, j))
    wspec = _spec((3, SC_COLS), lambda j: (0, j))
    act = jax.ShapeDtypeStruct((SEQ, D_MODEL), BF16)
    return pl.pallas_call(
        body, name="sc_mid_bwd", grid=(N_SC,), in_specs=_sc_specs() + [wspec, col], out_specs=[col, col, col, wspec],
        out_shape=[act, act, act, jax.ShapeDtypeStruct((3, D_MODEL), F32)],
        scratch_shapes=[pltpu.VMEM((SEQ, SC_COLS), F32)], compiler_params=_params(("parallel",)),
    )(p, p, p, conv_w, dy)


GLA_GROUP = 4
GLA_ROWS = GLA_GROUP * CHUNK
N_GROUPS = N_CHUNKS // GLA_GROUP
Q0, K0, V0, R0, G0 = 0, KEY_DIM, 2 * KEY_DIM, 2 * KEY_DIM + VALUE_DIM, 2 * KEY_DIM + 2 * VALUE_DIM


def _tri(strict):
    r = lax.broadcasted_iota(jnp.int32, (CHUNK, CHUNK), 0)
    c = lax.broadcasted_iota(jnp.int32, (CHUNK, CHUNK), 1)
    return jnp.where(c < r if strict else c <= r, 1.0, 0.0).astype(F32)


def _cumsum_rows(tri, x):
    tri = tri.astype(BF16)
    total = None
    for _ in range(3):
        term = x.astype(BF16)
        x = x - term.astype(F32)
        product = jnp.dot(tri, term, preferred_element_type=F32)
        total = product if total is None else total + product
    return total


def _gate_logits(gl, wgu, b_gate):
    return jnp.dot(gl, wgu, preferred_element_type=F32) + b_gate


def _log_decay(logits):
    return (jnp.minimum(logits, 0.0) - jnp.log(1.0 + jnp.exp(-jnp.abs(logits)))) * (1.0 / GATE_NORMALIZER)


def _head(x, h, width):
    return x[:, h * width:(h + 1) * width]


def _gla_fwd(proj, wgu, b_gate, gn):
    def body(p_ref, wgu_ref, b_ref, gn_ref, o_ref, og_ref, st_ref, state):
        @pl.when(pl.program_id(0) == 0)
        def _():
            state[...] = jnp.zeros_like(state)

        tri = _tri(False)
        la = _log_decay(_gate_logits(p_ref[:, G0:G0 + GATE_PAD], wgu_ref[...], b_ref[...]))
        decays = []
        for c in range(GLA_GROUP):
            rows = slice(c * CHUNK, (c + 1) * CHUNK)
            cum = _cumsum_rows(tri, la[rows])
            tot = cum[CHUNK - 1:CHUNK, :]
            kd = (p_ref[rows, K0:K0 + KEY_DIM].astype(F32) * jnp.exp(tot - cum)).astype(BF16)
            decays.append(jnp.exp(tot))
            v = p_ref[rows, V0:V0 + VALUE_DIM]
            for h in range(GLA_HEADS):
                st_ref[c, h] = lax.dot_general(
                    _head(v, h, HEAD_V), _head(kd, h, HEAD_K), (TN, ((), ())), preferred_element_type=F32)
        for c in range(GLA_GROUP):
            for h in range(GLA_HEADS):
                s = state[h] * _head(decays[c], h, HEAD_K) + st_ref[c, h]
                state[h] = s
                st_ref[c, h] = s
        for c in range(GLA_GROUP):
            rows = slice(c * CHUNK, (c + 1) * CHUNK)
            q = (p_ref[rows, Q0:Q0 + KEY_DIM].astype(F32) * (HEAD_K ** -0.5)).astype(BF16)
            for h in range(GLA_HEADS):
                o_ref[rows, h * HEAD_V:(h + 1) * HEAD_V] = lax.dot_general(
                    _head(q, h, HEAD_K), st_ref[c, h].astype(BF16), (NT, ((), ())), preferred_element_type=F32)
        r = p_ref[:, R0:R0 + VALUE_DIM].astype(F32)
        gate = r * _sigmoid(r) * gn_ref[...]
        for h in range(GLA_HEADS):
            cols = slice(h * HEAD_V, (h + 1) * HEAD_V)
            o = o_ref[:, cols]
            og_ref[:, cols] = (o * _rstd(o) * gate[:, cols]).astype(BF16)

    rows = _spec((GLA_ROWS, VALUE_DIM), lambda i: (i, 0))
    const = lambda shape: _spec(shape, lambda i: (0,) * len(shape))
    return pl.pallas_call(
        body, name="gla_fwd", grid=(N_GROUPS,),
        in_specs=[_spec((GLA_ROWS, PROJ_A_PAD), lambda i: (i, 0)), const((GATE_PAD, KEY_DIM)), const((1, KEY_DIM)),
                  const((1, VALUE_DIM))],
        out_specs=[rows, rows, _spec((GLA_GROUP, GLA_HEADS, HEAD_V, HEAD_K), lambda i: (i, 0, 0, 0))],
        out_shape=[jax.ShapeDtypeStruct((SEQ, VALUE_DIM), F32), jax.ShapeDtypeStruct((SEQ, VALUE_DIM), BF16),
                   jax.ShapeDtypeStruct((N_CHUNKS, GLA_HEADS, HEAD_V, HEAD_K), F32)],
        scratch_shapes=[pltpu.VMEM((GLA_HEADS, HEAD_V, HEAD_K), F32)], compiler_params=_params(("arbitrary",)),
    )(proj, wgu, b_gate, gn)


def _gla_bwd(proj, wgu, b_gate, gn, o, states, dog):
    last = N_GROUPS - 1

    def body(p_ref, wgu_ref, b_ref, gn_ref, o_ref, st_ref, stp_ref, dog_ref, dp_ref, dwgu_ref, db_ref, dgn_ref, carry, do_buf,
             g_buf):
        step = pl.program_id(0)

        @pl.when(step == 0)
        def _():
            carry[...] = jnp.zeros_like(carry)

        r = p_ref[:, R0:R0 + VALUE_DIM].astype(F32)
        sr = _sigmoid(r)
        silu = r * sr
        gn_row = gn_ref[...]
        dog_rows = dog_ref[...].astype(F32)
        dn = dog_rows * silu
        dgn_cols = []
        for h in range(GLA_HEADS):
            cols = slice(h * HEAD_V, (h + 1) * HEAD_V)
            oh = o_ref[:, cols]
            rs = _rstd(oh)
            ohat = oh * rs
            dn_h = dn[:, cols]
            dgn_cols.append(jnp.sum(dn_h * ohat, axis=0, keepdims=True))
            dohat = dn_h * gn_row[:, cols]
            do_buf[:, cols] = rs * (dohat - ohat * jnp.mean(dohat * ohat, axis=-1, keepdims=True))
            n_h = ohat * gn_row[:, cols]
            dp_ref[:, R0 + h * HEAD_V:R0 + (h + 1) * HEAD_V] = (
                dog_rows[:, cols] * n_h * (sr[:, cols] * (1.0 + r[:, cols] * (1.0 - sr[:, cols])))).astype(BF16)
        dgn = jnp.concatenate(dgn_cols, axis=1)

        tri = _tri(False)
        tri_strict = _tri(True)
        gl = p_ref[:, G0:G0 + GATE_PAD]
        logits = _gate_logits(gl, wgu_ref[...], b_ref[...])
        la = _log_decay(logits)
        fades, kds, decays = [], [], []
        for c in range(GLA_GROUP):
            rows = slice(c * CHUNK, (c + 1) * CHUNK)
            cum = _cumsum_rows(tri, la[rows])
            tot = cum[CHUNK - 1:CHUNK, :]
            fades.append(jnp.exp(tot - cum))
            kds.append(p_ref[rows, K0:K0 + KEY_DIM].astype(F32) * fades[c])
            decays.append(jnp.exp(tot))
            q = (p_ref[rows, Q0:Q0 + KEY_DIM].astype(F32) * (HEAD_K ** -0.5)).astype(BF16)
            do = do_buf[rows, :].astype(BF16)
            for h in range(GLA_HEADS):
                do_h = _head(do, h, HEAD_V)
                dq = jnp.dot(do_h, st_ref[c, h].astype(BF16), preferred_element_type=F32) * (HEAD_K ** -0.5)
                dp_ref[rows, Q0 + h * HEAD_K:Q0 + (h + 1) * HEAD_K] = dq.astype(BF16)
                g_buf[c, h] = lax.dot_general(do_h, _head(q, h, HEAD_K), (TN, ((), ())), preferred_element_type=F32)
        for c in reversed(range(GLA_GROUP)):
            for h in range(GLA_HEADS):
                g = carry[h] + g_buf[c, h]
                g_buf[c, h] = g
                carry[h] = g * _head(decays[c], h, HEAD_K)
        dlogit_rows = []
        for c in range(GLA_GROUP):
            rows = slice(c * CHUNK, (c + 1) * CHUNK)
            v = p_ref[rows, V0:V0 + VALUE_DIM]
            kd = kds[c].astype(BF16)
            dkd_cols, ddecay_cols = [], []
            for h in range(GLA_HEADS):
                g = g_buf[c, h]
                g16 = g.astype(BF16)
                dkd_cols.append(jnp.dot(_head(v, h, HEAD_V), g16, preferred_element_type=F32))
                dv = lax.dot_general(_head(kd, h, HEAD_K), g16, (NT, ((), ())), preferred_element_type=F32)
                dp_ref[rows, V0 + h * HEAD_V:V0 + (h + 1) * HEAD_V] = dv.astype(BF16)
                if c > 0:
                    s_prev = st_ref[c - 1, h]
                else:
                    s_prev = jnp.where(step < last, stp_ref[0, h], 0.0)
                ddecay_cols.append(jnp.sum(g * s_prev, axis=0, keepdims=True))
            dkd = jnp.concatenate(dkd_cols, axis=1)
            ddecay = jnp.concatenate(ddecay_cols, axis=1)
            dp_ref[rows, K0:K0 + KEY_DIM] = (dkd * fades[c]).astype(BF16)
            e = dkd * kds[c]
            dla = ddecay * decays[c] + _cumsum_rows(tri_strict, e)
            dlogit_rows.append(dla * (1.0 / GATE_NORMALIZER) * (1.0 - _sigmoid(logits[rows])))
        dlogit = jnp.concatenate(dlogit_rows, axis=0)
        dlogit16 = dlogit.astype(BF16)
        dp_ref[:, G0:G0 + GATE_PAD] = lax.dot_general(
            dlogit16, wgu_ref[...], (NT, ((), ())), preferred_element_type=F32).astype(BF16)
        dwgu = lax.dot_general(gl, dlogit16, (TN, ((), ())), preferred_element_type=F32)
        db = jnp.sum(dlogit, axis=0, keepdims=True)

        @pl.when(step == 0)
        def _():
            dwgu_ref[...] = dwgu
            db_ref[...] = db
            dgn_ref[...] = dgn

        @pl.when(step > 0)
        def _():
            dwgu_ref[...] += dwgu
            db_ref[...] += db
            dgn_ref[...] += dgn

    rev = lambda i: (last - i, 0)
    rows = _spec((GLA_ROWS, VALUE_DIM), rev)
    const = lambda shape: _spec(shape, lambda i: (0,) * len(shape))
    st_shape = (GLA_HEADS, HEAD_V, HEAD_K)
    return pl.pallas_call(
        body, name="gla_bwd", grid=(N_GROUPS,),
        in_specs=[_spec((GLA_ROWS, PROJ_A_PAD), rev), const((GATE_PAD, KEY_DIM)), const((1, KEY_DIM)), const((1, VALUE_DIM)),
                  rows, _spec((GLA_GROUP,) + st_shape, lambda i: (last - i, 0, 0, 0)),
                  _spec((1,) + st_shape, lambda i: (jnp.maximum((last - i) * GLA_GROUP - 1, 0), 0, 0, 0)), rows],
        out_specs=[_spec((GLA_ROWS, PROJ_A_PAD), rev), const((GATE_PAD, KEY_DIM)), const((1, KEY_DIM)), const((1, VALUE_DIM))],
        out_shape=[jax.ShapeDtypeStruct((SEQ, PROJ_A_PAD), BF16), jax.ShapeDtypeStruct((GATE_PAD, KEY_DIM), F32),
                   jax.ShapeDtypeStruct((1, KEY_DIM), F32), jax.ShapeDtypeStruct((1, VALUE_DIM), F32)],
        scratch_shapes=[pltpu.VMEM(st_shape, F32), pltpu.VMEM((GLA_ROWS, VALUE_DIM), F32), pltpu.VMEM((GLA_GROUP,) + st_shape, F32)],
        compiler_params=_params(("arbitrary",)),
    )(proj, wgu, b_gate, gn, o, states, states, dog)


WGRAD_FF_TILE = D_FF // 2


CARRY_ROWS = 8
UP_ROWS = 512


def _ffn_up_mid(name, x, gamma, w_up_t, conv_w):
    def body(x_ref, g_ref, w_ref, c_ref, h_ref, gu_ref, a_ref, carry):
        @pl.when(pl.program_id(0) == 0)
        def _():
            carry[...] = jnp.zeros_like(carry)

        x_tile = x_ref[...]
        h_tile = (x_tile * _rstd(x_tile) * g_ref[...]).astype(BF16)
        h_ref[...] = h_tile
        for k in range(N_FF_COLS):
            cols = slice(k * FF_COLS, (k + 1) * FF_COLS)
            g, u = (lax.dot_general(h_tile, w_ref[p, cols, :], (NT, ((), ())), preferred_element_type=F32).astype(BF16)
                    for p in range(2))
            gu_ref[0, :, cols] = g
            gu_ref[1, :, cols] = u
            g = g.astype(F32)
            w = c_ref[:, cols]
            before = carry[:, cols]
            gc = w[2:3, :] * g + w[1:2, :] * _shift_down(g, before, 1) + w[0:1, :] * _shift_down(g, before, 2)
            a_ref[:, cols] = (gc * _sigmoid(gc) * u.astype(F32)).astype(BF16)
            carry[:, cols] = g[UP_ROWS - CARRY_ROWS:, :]

    row = _spec((UP_ROWS, D_MODEL), lambda i: (i, 0))
    return pl.pallas_call(
        body, name=name, grid=(SEQ // UP_ROWS,),
        in_specs=[row, _resident((1, D_MODEL)), _resident((2, D_FF, D_MODEL)), _resident((3, D_FF))],
        out_specs=[row, _spec((2, UP_ROWS, D_FF), lambda i: (0, i, 0)), _spec((UP_ROWS, D_FF), lambda i: (i, 0))],
        out_shape=[_act(), _act((2, SEQ, D_FF)), _act((SEQ, D_FF))], scratch_shapes=[pltpu.VMEM((CARRY_ROWS, D_FF), F32)],
        compiler_params=_params(("arbitrary",)),
    )(x, gamma, w_up_t, conv_w)


def _ffn_fwd(tag, x, gamma, w_up_t, conv_w, w_down):
    h, gu, a = _ffn_up_mid(f"ffn{tag}_up_mid", x, gamma, w_up_t, conv_w)
    return _wide_nn(f"ffn{tag}_down", a, w_down, x=x), (h, gu, a)


def _owner_blocks(d, rows=None):
    if rows is not None:
        d = d[:rows]
    return d.reshape((N_DEV, -1) + d.shape[-1:])


def _ffn_bwd(tag, x, gamma, w_up_t, conv_w, w_down, saved, dx, dx16, swap):
    h, gu, a = saved
    da = _wide_nt(f"ffn{tag}_da", dx16, w_down)
    d_w_down = _owner_blocks(_wgrad_cols_tn(f"ffn{tag}_dwdown", a, WGRAD_FF_TILE, dx16))
    dgu, d_conv = _ffn_mid_bwd(f"ffn{tag}_mid_bwd", gu, conv_w, da)
    d_w_up_t = _owner_blocks(_wgrad_halves_tn(f"ffn{tag}_dwup", dgu, WGRAD_FF_TILE, h))
    parts = (d_w_up_t, d_w_down)
    dx, dx16, d_gamma, *received = _sum_blocks_nn(
        f"ffn{tag}_dh", dgu, w_up_t, norm=(x, gamma, dx), swap=parts if swap else ())
    return dx, dx16, d_gamma, d_conv, parts, received


def _local_step(x, target, w, fetch=None, emit=None):
    if fetch is None:
        local = dict(a=(w.get("a_w_in"), w.get("a_w_out")), b=(w.get("b_w_in"), w.get("b_w_out")))
        for layer in range(2):
            local[f"f{layer}"] = (w["f_w_up"][layer], w["f_w_down"][layer]) if "f_w_up" in w else None
        fetch = lambda group, after: local[group]
    swap = emit is not None
    if emit is None:
        emit = lambda group, parts, received, dx: dx
    f_norm = (w["f_norm"][0:1], w["f_norm"][1:2])

    x0 = x
    a_w_in, a_w_out = fetch("a", x0)
    h0, proj = _norm_proj("a_in", x0, w["a_norm"], a_w_in)
    o, og, states = _gla_fwd(proj, w["a_w_gate_up"], w["a_b_gate"], w["a_gn"])
    x1 = _square("a_out", og, a_w_out, NN, x0)
    up0, down0 = fetch("f0", x1)
    x2, ffn0 = _ffn_fwd(0, x1, f_norm[0], up0, w["f_conv"][0], down0)
    b_w_in, b_w_out = fetch("b", x2)
    h2, p = _norm_proj("b_in", x2, w["b_norm"], b_w_in)
    y = _sc_mid_fwd(p, w["b_conv"])
    x3 = _square("b_out", y, b_w_out, NN, x2)
    up1, down1 = fetch("f1", x3)
    x4, ffn1 = _ffn_fwd(1, x3, f_norm[1], up1, w["f_conv"][1], down1)
    loss, dx, dx16, d_final_norm = _loss_head(x4, w["final_norm"], target)

    dx, dx16, d_f_norm1, d_fconv1, parts_f1, got = _ffn_bwd(
        1, x3, f_norm[1], up1, w["f_conv"][1], down1, ffn1, dx, dx16, swap)
    dx16 = emit("f1", parts_f1, got, dx16)

    dy = _square("b_dy", dx16, b_w_out, NT)
    d_b_w_out = _owner_blocks(_wgrad_cols_tn("b_dwout", y, OUT_TILE, dx16))
    db, dc, dhh, d_b_conv = _sc_mid_bwd(p, w["b_conv"], dy)
    dp = jnp.concatenate([db, dc, dhh], axis=1)
    parts_b = (_wgrad_cols_transposed_tn("b_dwin", h2, dp, B_SHARD), d_b_w_out)
    dx, dx16, d_b_norm, *got = _sum_cols_nt("b_dh", dp, b_w_in, norm=(x2, w["b_norm"], dx), swap=parts_b if swap else ())
    dx16 = emit("b", parts_b, got, dx16)

    dx, dx16, d_f_norm0, d_fconv0, parts_f0, got = _ffn_bwd(
        0, x1, f_norm[0], up0, w["f_conv"][0], down0, ffn0, dx, dx16, swap)
    dx16 = emit("f0", parts_f0, got, dx16)

    dog = _square("a_dog", dx16, a_w_out, NT)
    d_a_w_out = _owner_blocks(_wgrad_cols_tn("a_dwout", og, OUT_TILE, dx16))
    dproj, d_wgu, d_b_gate, d_gn = _gla_bwd(proj, w["a_w_gate_up"], w["a_b_gate"], w["a_gn"], o, states, dog)
    parts_a = (_owner_blocks(_wgrad_cols_tn("a_dwin", dproj, PA_TILE, h0), PROJ_A), d_a_w_out)
    dx, _, d_a_norm, *got = _wide_nn("a_dh", dproj, a_w_in, norm=(x0, w["a_norm"], dx), swap=parts_a if swap else ())
    emit("a", parts_a, got, dx)

    grads = dict(
        a_norm=d_a_norm, a_w_in=parts_a[0], a_w_gate_up=d_wgu, a_b_gate=d_b_gate, a_gn=d_gn, a_w_out=parts_a[1],
        b_norm=d_b_norm, b_w_in=parts_b[0], b_conv=d_b_conv, b_w_out=parts_b[1],
        f_norm=(d_f_norm0, d_f_norm1), f_w_up=(parts_f0[0], parts_f1[0]), f_conv=(d_fconv0, d_fconv1),
        f_w_down=(parts_f0[1], parts_f1[1]), final_norm=d_final_norm)
    grads["loss"] = loss
    return dx, grads


MESH_ID = pl.DeviceIdType.MESH
ANY = pl.BlockSpec(memory_space=pl.ANY)
N_PEERS = N_DEV - 1


def _position():
    return lax.axis_index("x"), lax.axis_index("y"), lax.axis_index("c")


def _slot(px, py, pc):
    return 4 * px + 2 * py + pc


GATHER_COPIES = 8
HALF_ROWS = 16


def _gather_copies(src, out, send_sems, recv_sems, local_sems):
    n = len(src)
    to_sibling, to_x, to_y, x_on_to_y, y_on_to_x, x_to_sibling, y_to_sibling, diagonal_to_sibling = range(GATHER_COPIES)
    x, y, c = _position()
    me, sibling = (x, y, c), (x, y, 1 - c)
    x_side, y_side, diagonal = (1 - x, y), (x, 1 - y), (1 - x, 1 - y)

    def rows_of(t, half):
        rows = src[t].shape[0]
        half_rows = rows // 2 // HALF_ROWS * HALF_ROWS
        return (pl.ds(0, rows), pl.ds(0, half_rows), pl.ds(half_rows, rows - half_rows))[half]

    def copy(t, j, block, to, half=0, from_input=False):
        dst = out[t].at[_slot(*block), rows_of(t, half)]
        return pltpu.make_async_remote_copy(
            src_ref=src[t] if from_input else dst, dst_ref=dst, send_sem=send_sems.at[GATHER_COPIES * t + j],
            recv_sem=recv_sems.at[GATHER_COPIES * t + j], device_id=to, device_id_type=MESH_ID)

    mine = [pltpu.make_async_copy(src[t], out[t].at[_slot(*me)], local_sems.at[t]) for t in range(n)]
    for cp in mine:
        cp.start()
    sent = []

    def start(cp):
        cp.start()
        sent.append(cp)

    for t in range(n):
        start(copy(t, to_sibling, me, sibling, from_input=True))
        start(copy(t, to_x, me, (*x_side, c), from_input=True))
        start(copy(t, to_y, me, (*y_side, c), from_input=True))
    for t in range(n):
        copy(t, to_x, (*x_side, c), me).wait_recv()
        start(copy(t, x_on_to_y, (*x_side, c), (*y_side, c), half=1))
        start(copy(t, x_to_sibling, (*x_side, c), sibling))
        copy(t, to_y, (*y_side, c), me).wait_recv()
        start(copy(t, y_on_to_x, (*y_side, c), (*x_side, c), half=2))
        start(copy(t, y_to_sibling, (*y_side, c), sibling))
    for t in range(n):
        copy(t, x_on_to_y, (*diagonal, c), me, half=1).wait_recv()
        copy(t, y_on_to_x, (*diagonal, c), me, half=2).wait_recv()
        start(copy(t, diagonal_to_sibling, (*diagonal, c), sibling))
    for t in range(n):
        copy(t, to_sibling, sibling, me).wait_recv()
        for j, chip in ((x_to_sibling, x_side), (y_to_sibling, y_side), (diagonal_to_sibling, diagonal)):
            copy(t, j, (*chip, 1 - c), me).wait_recv()
    for cp in sent:
        cp.wait_send()
    for cp in mine:
        cp.wait()


def _all_gather(name, shards):
    n = len(shards)

    def body(*refs):
        _gather_copies(refs[:n], refs[n:2 * n], *refs[2 * n:])

    sems = pltpu.SemaphoreType.DMA((GATHER_COPIES * n,))
    return pl.pallas_call(
        body, name=name, in_specs=[ANY] * n, out_specs=[ANY] * n,
        out_shape=[jax.ShapeDtypeStruct((N_DEV,) + s.shape, s.dtype) for s in shards],
        scratch_shapes=[sems, sems, pltpu.SemaphoreType.DMA((n,))],
    )(*shards)


SIBLING_AND_NEIGHBOURS = (1, 2, 4)
SAME_CORE = (2, 4, 6)


def _flip(x, y, c, k):
    return x ^ (k >> 2), y ^ ((k >> 1) & 1), c ^ (k & 1)


N_CHIPS = N_DEV // 2


def _chip(px, py):
    return 2 * px + py


def _pair_copies(parts, received, send_sems, recv_sems):
    x, y, c = lax.axis_index("x"), lax.axis_index("y"), lax.axis_index("c")
    sibling = (x, y, 1 - c)
    copies = []
    for t in range(len(parts)):
        for q in range(N_DEV // 2):
            send = pltpu.make_async_remote_copy(
                src_ref=parts[t].at[2 * q + 1 - c], dst_ref=received[t].at[q], send_sem=send_sems.at[t, q],
                recv_sem=recv_sems.at[t, q], device_id=sibling, device_id_type=pl.DeviceIdType.MESH)
            landed = received[t].at[q]
            arrival = pltpu.make_async_remote_copy(
                src_ref=landed, dst_ref=landed, send_sem=send_sems.at[t, q], recv_sem=recv_sems.at[t, q],
                device_id=sibling, device_id_type=pl.DeviceIdType.MESH)
            copies.append((send, arrival))
    return copies


PAIR_ROWS = 1024


def _pair_add(name, part, received, side):
    _, rows, cols = part.shape
    tiles = [t for t in range(PAIR_ROWS, 0, -BF16_ROWS) if rows % t == 0]
    tr = tiles[0] if tiles else rows

    def body(side_ref, p_ref, r_ref, o_ref):
        o_ref[...] = (p_ref[...].astype(F32) + r_ref[...].astype(F32)).astype(BF16)

    tile = _spec((None, tr, cols), lambda q, i, side_ref: (q, i, 0))
    return pl.pallas_call(
        body, name=name,
        grid_spec=pltpu.PrefetchScalarGridSpec(
            num_scalar_prefetch=1, grid=(N_CHIPS, rows // tr),
            in_specs=[_spec((None, tr, cols), lambda q, i, side_ref: (2 * q + side_ref[0], i, 0)), tile], out_specs=tile),
        out_shape=jax.ShapeDtypeStruct((N_CHIPS, rows, cols), BF16), compiler_params=_params(("parallel", "parallel")),
    )(side, part, received)


def _send_copy(parts, landing, send_sems, recv_sems, t, s, k):
    x, y, c = _position()
    px, py, _ = _flip(x, y, c, k)
    return pltpu.make_async_remote_copy(
        src_ref=parts[t].at[_chip(px, py)], dst_ref=landing[t].at[_chip(x, y)], send_sem=send_sems.at[s],
        recv_sem=recv_sems.at[s], device_id=(px, py, c), device_id_type=MESH_ID)


def _send_arrival(landing, send_sems, recv_sems, t, s, k):
    x, y, c = _position()
    px, py, _ = _flip(x, y, c, k)
    landed = landing[t].at[_chip(px, py)]
    return pltpu.make_async_remote_copy(
        src_ref=landed, dst_ref=landed, send_sem=send_sems.at[s], recv_sem=recv_sems.at[s],
        device_id=(px, py, c), device_id_type=MESH_ID)


def _handshake(peers):
    x, y, c = _position()
    barrier = pltpu.get_barrier_semaphore()
    for k in peers:
        pl.semaphore_signal(barrier, inc=1, device_id=_flip(x, y, c, k), device_id_type=MESH_ID)
    pl.semaphore_wait(barrier, len(peers))


def _sequencer(name, collective_id, n_copies, body, operands, out_type):
    n_arrays = len(operands)
    return pl.kernel(
        body, out_type=out_type, mesh=plsc.ScalarSubcoreMesh(axis_name="sequencer", num_cores=1), name=name,
        scratch_types=(pltpu.SemaphoreType.DMA((n_copies,)), pltpu.SemaphoreType.DMA((n_copies,)),
                       pltpu.SemaphoreType.DMA((n_arrays,))),
        compiler_params=pltpu.CompilerParams(collective_id=collective_id))(*operands)


def _sequencer_exchange(name, collective_id, parts, after=()):
    n, n_peers, n_in = len(parts), len(SAME_CORE), len(parts) + len(after)

    def body(*refs):
        src, landing = refs[:n], refs[n_in:n_in + n]
        send_sems, recv_sems, local_sems = refs[n_in + n:]
        _handshake(SAME_CORE)
        x, y, _ = _position()
        mine = [pltpu.make_async_copy(src[t].at[_chip(x, y)], landing[t].at[_chip(x, y)], local_sems.at[t]) for t in range(n)]
        for cp in mine:
            cp.start()
        sent = [_send_copy(src, landing, send_sems, recv_sems, t, t * n_peers + j, k)
                for t in range(n) for j, k in enumerate(SAME_CORE)]
        for cp in sent:
            cp.start()
        for t in range(n):
            for j, k in enumerate(SAME_CORE):
                _send_arrival(landing, send_sems, recv_sems, t, t * n_peers + j, k).wait_recv()
        for cp in sent:
            cp.wait_send()
        for cp in mine:
            cp.wait()

    landing = [jax.ShapeDtypeStruct(p.shape, p.dtype) for p in parts]
    return _sequencer(name, collective_id, n * n_peers, body, list(parts) + list(after), landing)


def _sequencer_gather(name, collective_id, shards):
    n = len(shards)

    def body(*refs):
        _handshake(SIBLING_AND_NEIGHBOURS)
        _gather_copies(refs[:n], refs[n:2 * n], *refs[2 * n:])

    gathered = [jax.ShapeDtypeStruct((N_DEV,) + s.shape, s.dtype) for s in shards]
    return _sequencer(name, collective_id, GATHER_COPIES * n, body, shards, gathered)


ADAM_ROWS = 512
BF16_ROWS = 16


def _adam_update(w, g, m, v):
    m = ADAM_B1 * m + (1.0 - ADAM_B1) * g
    v = ADAM_B2 * v + (1.0 - ADAM_B2) * (g * g)
    m_hat = m / (1.0 - ADAM_B1 ** ADAM_STEP)
    v_hat = v / (1.0 - ADAM_B2 ** ADAM_STEP)
    delta = -ADAM_LR * (m_hat / (jnp.sqrt(v_hat) + ADAM_EPS) + ADAM_WD * w)
    return delta, m, v


def _sum_slots(ref):
    total = ref[0].astype(F32)
    for d in range(1, ref.shape[0]):
        total = total + ref[d].astype(F32)
    return total


def _adamw_sum(name, landed, w, m, v):
    layers, rows, cols = w.shape
    tiles = [t for t in range(ADAM_ROWS, 0, -BF16_ROWS) if rows % t == 0]
    tr = tiles[0] if tiles else rows
    nt = rows // tr

    def body(*refs):
        parts = refs[:layers]
        w_ref, m_ref, v_ref, g_ref, d_ref, nm_ref, nv_ref = refs[layers:]
        layer = pl.program_id(0)
        g = _sum_slots(parts[0])
        for q in range(1, layers):
            g = jnp.where(layer == q, _sum_slots(parts[q]), g)
        delta, new_m, new_v = _adam_update(w_ref[...], g, m_ref[...], v_ref[...])
        g_ref[...] = g
        d_ref[...] = delta
        nm_ref[...] = new_m
        nv_ref[...] = new_v

    def part_spec(q):
        return _spec((N_CHIPS, tr, cols), lambda l, i: (0, jnp.where(l == q, i, jnp.where(l < q, 0, nt - 1)), 0))

    tile = _spec((None, tr, cols), lambda l, i: (l, i, 0))
    out = jax.ShapeDtypeStruct((layers, rows, cols), F32)
    return pl.pallas_call(
        body, name=name, grid=(layers, nt), in_specs=[part_spec(q) for q in range(layers)] + [tile] * 3,
        out_specs=[tile] * 4, out_shape=[out] * 4, compiler_params=_params(("arbitrary", "arbitrary")),
    )(*landed, w, m, v)


def _sum_small(landed):
    def body(in_ref, out_ref):
        out_ref[...] = _sum_slots(in_ref)

    return pl.pallas_call(body, name="small_grad_sum", out_shape=jax.ShapeDtypeStruct(landed.shape[1:], F32))(landed)


def _adamw_small(arrays):
    n = len(arrays)

    def body(*refs):
        for i in range(n):
            g_ref, w_ref, m_ref, v_ref = refs[4 * i:4 * i + 4]
            d_ref, nm_ref, nv_ref = refs[4 * n + 3 * i:4 * n + 3 * i + 3]
            d_ref[...], nm_ref[...], nv_ref[...] = _adam_update(w_ref[...], g_ref[...], m_ref[...], v_ref[...])

    out = [jax.ShapeDtypeStruct(w.shape, F32) for _, w, _, _ in arrays for _ in range(3)]
    flat = pl.pallas_call(body, name="adam_small", out_shape=out)(*[a for group in arrays for a in group])
    return [tuple(flat[3 * i:3 * i + 3]) for i in range(n)]


LANES = 128
SUBLANES = 8
F_CONV_SHARD = D_FF // N_DEV
GATE_SHARD = KEY_DIM // N_DEV
NORM_SHARD = D_MODEL // N_DEV


def _tile_rows(a):
    flat = a.reshape(-1)
    size = -(-flat.shape[0] // (SUBLANES * LANES)) * SUBLANES * LANES
    return jnp.pad(flat, (0, size - flat.shape[0])).reshape(-1, LANES)


def _pack_rows(pieces):
    return jnp.concatenate([_tile_rows(p) for p in pieces], axis=0)


def _unpack_rows(packed, shapes):
    out, row = [], 0
    for shape in shapes:
        size = 1
        for s in shape:
            size *= s
        rows = -(-size // (SUBLANES * LANES)) * SUBLANES
        piece = packed[..., row:row + rows, :]
        out.append(piece.reshape(piece.shape[:-2] + (rows * LANES,))[..., :size])
        row += rows
    return out


SMALL_SHARDS = ((GATE_RANK, GATE_SHARD), (1, NORM_SHARD), (3, NORM_SHARD), (2, 3, F_CONV_SHARD))


def _unpack_small_shards(g):
    gate, b_norm, b_conv, f_conv = _unpack_rows(g, SMALL_SHARDS)
    gate = gate.reshape(N_DEV, GATE_RANK, GATE_SHARD).transpose(1, 0, 2).reshape(GATE_RANK, KEY_DIM)
    b_norm = b_norm.reshape(1, D_MODEL)
    b_conv = b_conv.reshape(N_DEV, 3, NORM_SHARD).transpose(1, 0, 2).reshape(3, D_MODEL)
    f_conv = f_conv.reshape(N_DEV, 2, 3, F_CONV_SHARD).transpose(1, 2, 0, 3).reshape(2, 3, D_FF)
    return gate, b_norm, b_conv, f_conv


SMALL_LAYOUT = (("a_norm", (1, D_MODEL)), ("a_w_gate_up", (GATE_RANK, KEY_DIM)), ("a_b_gate", (1, KEY_DIM)), ("a_gn", (1, VALUE_DIM)),
                ("b_norm", (1, D_MODEL)), ("b_conv", (3, D_MODEL)), ("f_norm0", (1, D_MODEL)), ("f_norm1", (1, D_MODEL)),
                ("f_conv0", (3, D_FF)), ("f_conv1", (3, D_FF)), ("final_norm", (1, D_MODEL)), ("loss", (1, LANES)))


def _pack_small_grads(g):
    full = dict(g)
    full["a_w_gate_up"] = g["a_w_gate_up"][:GATE_RANK]
    for layer in range(2):
        full[f"f_norm{layer}"] = g["f_norm"][layer]
        full[f"f_conv{layer}"] = g["f_conv"][layer]
    return _pack_rows([full[name] for name, _ in SMALL_LAYOUT])


def _unpack_small_grads(packed):
    pieces = _unpack_rows(packed, [shape for _, shape in SMALL_LAYOUT])
    out = {name: piece.reshape(shape) for (name, shape), piece in zip(SMALL_LAYOUT, pieces)}
    out["f_norm"] = jnp.stack([out["f_norm0"][0], out["f_norm1"][0]])
    out["f_conv"] = jnp.stack([out["f_conv0"], out["f_conv1"]])
    return out


def kernel(x, a_norm, a_w_in, a_w_gate_up, a_b_gate, a_gn, a_w_out, b_norm, b_w_in, b_conv, b_w_out, f_norm, f_w_up, f_conv, f_w_down, final_norm, loss_target, m_a_norm, m_a_w_in, m_a_w_gate_up, m_a_b_gate, m_a_gn, m_a_w_out, m_b_norm, m_b_w_in, m_b_conv, m_b_w_out, m_f_norm, m_f_w_up, m_f_conv, m_f_w_down, m_final_norm, v_a_norm, v_a_w_in, v_a_w_gate_up, v_a_b_gate, v_a_gn, v_a_w_out, v_b_norm, v_b_w_in, v_b_conv, v_b_w_out, v_f_norm, v_f_w_up, v_f_conv, v_f_w_down, v_final_norm):
    my_slot = _slot(*_position())

    transposed = lambda w: jnp.swapaxes(w, 1, 2)
    a_transposed = lambda w: w.reshape(D_MODEL, A_SHARD).T.reshape(1, A_SHARD, D_MODEL)
    a_w_in_t, f_w_up_t = a_transposed(a_w_in), transposed(f_w_up)
    first = _all_gather("weight_gather", [a_w_in_t[0].astype(BF16), a_w_out[0].astype(BF16),
                                          _pack_rows([a_w_gate_up[0], b_norm, b_conv[0], f_conv])])
    gathers, small_shards = {}, first[2]
    later = (("f0", f_w_up_t[0], f_w_down[0]), ("b", b_w_in[0], b_w_out[0]), ("f1", f_w_up_t[1], f_w_down[1]))
    for collective_id, (group, w_in, w_out) in enumerate(later):
        w_in, w_out, small_shards = lax.optimization_barrier((w_in.astype(BF16), w_out.astype(BF16), small_shards))
        gathers[group] = _sequencer_gather(f"gather_{group}", collective_id, [w_in, w_out])
    gate_full, b_norm_full, b_conv_full, f_conv_full = _unpack_small_shards(small_shards)
    a_w_in_full = jnp.pad(first[0].reshape(PROJ_A, D_MODEL), ((0, PROJ_A_PAD - PROJ_A), (0, 0)))
    weights = dict(
        a_norm=a_norm, a_w_gate_up=jnp.pad(gate_full, ((0, GATE_PAD - GATE_RANK), (0, 0))).astype(BF16), a_b_gate=a_b_gate,
        a_gn=a_gn, b_norm=b_norm_full, b_conv=b_conv_full, f_norm=f_norm, f_conv=f_conv_full,
        final_norm=final_norm.reshape(1, D_MODEL))

    def fetch(group, after):
        if group == "a":
            return a_w_in_full, first[1].reshape(D_MODEL, D_MODEL)
        w_in, w_out = gathers[group]
        if group == "b":
            return w_in, w_out.reshape(D_MODEL, D_MODEL)
        return w_in.reshape(2, D_FF, D_MODEL), w_out.reshape(D_FF, D_MODEL)

    exchanges, pending = {}, []
    exchange_ids = dict(b=3, f0=4, a=5)
    side = lax.axis_index("c").astype(jnp.int32).reshape(1)

    def emit(group, parts, received, carry):
        sums = [_pair_add(f"pair_add_{group}_{i}", part, got, side) for i, (part, got) in enumerate(zip(parts, received))]
        carry, *sums = lax.optimization_barrier((carry, *sums))
        pending.extend(sums)
        if group != "f1":
            after = list(exchanges.values())[-1][:1] if exchanges else ()
            exchanges[group] = _sequencer_exchange(f"grads_{group}", exchange_ids[group], list(pending), after)
            pending.clear()
        return carry

    dx, g = _local_step(x[0], loss_target[0], weights, fetch, emit)

    (up1, down1, d_b_in, d_b_out), (up0, down0), (d_a_in, d_a_out) = (exchanges[group] for group in ("b", "f0", "a"))
    back = lambda results: tuple(transposed(r) for r in results)
    big = dict(
        b_w_in=_adamw_sum("adam_b_w_in", [d_b_in], b_w_in, m_b_w_in, v_b_w_in),
        b_w_out=_adamw_sum("adam_b_w_out", [d_b_out], b_w_out, m_b_w_out, v_b_w_out),
        f_w_up=back(_adamw_sum("adam_f_w_up", [up0, up1], f_w_up_t, transposed(m_f_w_up), transposed(v_f_w_up))),
        f_w_down=_adamw_sum("adam_f_w_down", [down0, down1], f_w_down, m_f_w_down, v_f_w_down))
    small_packed, *updated = lax.optimization_barrier((_pack_small_grads(g), *big["f_w_down"]))
    big["f_w_down"] = tuple(updated)
    small_landed = _all_gather("small_grad_gather", [small_packed])[0]
    big.update(
        a_w_in=tuple(r.reshape(A_SHARD, D_MODEL).T.reshape(1, D_MODEL, A_SHARD) for r in _adamw_sum(
            "adam_a_w_in", [d_a_in], a_w_in_t, a_transposed(m_a_w_in), a_transposed(v_a_w_in))),
        a_w_out=_adamw_sum("adam_a_w_out", [d_a_out], a_w_out, m_a_w_out, v_a_w_out))
    small_g = _unpack_small_grads(_sum_small(small_landed))
    loss = small_g["loss"][0, 0]
    small_g["a_w_gate_up"] = lax.dynamic_slice_in_dim(small_g["a_w_gate_up"], my_slot * GATE_SHARD, GATE_SHARD, axis=1)
    small_g["b_norm"] = lax.dynamic_slice_in_dim(small_g["b_norm"], my_slot * NORM_SHARD, NORM_SHARD, axis=1)
    small_g["b_conv"] = lax.dynamic_slice_in_dim(small_g["b_conv"], my_slot * NORM_SHARD, NORM_SHARD, axis=1)
    small_g["f_conv"] = lax.dynamic_slice_in_dim(small_g["f_conv"], my_slot * F_CONV_SHARD, F_CONV_SHARD, axis=2)
    small_w = dict(
        a_norm=(a_norm, m_a_norm, v_a_norm), a_w_gate_up=(a_w_gate_up, m_a_w_gate_up, v_a_w_gate_up),
        a_b_gate=(a_b_gate, m_a_b_gate, v_a_b_gate), a_gn=(a_gn, m_a_gn, v_a_gn), b_norm=(b_norm, m_b_norm, v_b_norm),
        b_conv=(b_conv, m_b_conv, v_b_conv), f_norm=(f_norm, m_f_norm, v_f_norm), f_conv=(f_conv, m_f_conv, v_f_conv),
        final_norm=(final_norm, m_final_norm, v_final_norm))
    two_d = lambda a: a.reshape(-1, a.shape[-1])
    updates = _adamw_small([tuple(two_d(a.reshape(w.shape)) for a in (small_g[name], w, m, v)) for name, (w, m, v) in small_w.items()])
    small = {}
    for (name, (w, _, _)), update in zip(small_w.items(), updates):
        small[name] = (small_g[name].reshape(w.shape),) + tuple(u.reshape(w.shape) for u in update)

    order = ["a_norm", "a_w_in", "a_w_gate_up", "a_b_gate", "a_gn", "a_w_out", "b_norm", "b_w_in", "b_conv", "b_w_out",
             "f_norm", "f_w_up", "f_conv", "f_w_down", "final_norm"]
    results = {**big, **small}
    outputs = [loss, dx.reshape(1, SEQ, D_MODEL)]
    for kind in range(4):
        outputs += [results[name][kind] for name in order]
    return tuple(outputs)
```

```python
import jax
import jax.numpy as jnp
from jax import lax
from jax.experimental import pallas as pl
from jax.experimental.pallas import tpu as pltpu
from jax.experimental.pallas import tpu_sc as plsc

F32 = jnp.float32
BF16 = jnp.bfloat16

N_DEV = 8
SEQ = 2048
D_MODEL = 1024
CHUNK = 64
N_CHUNKS = SEQ // CHUNK
RMS_EPS = 1e-6
GLA_HEADS = 4
KEY_DIM = 512
VALUE_DIM = 1024
HEAD_K = KEY_DIM // GLA_HEADS
HEAD_V = VALUE_DIM // GLA_HEADS
GATE_RANK = 16
GATE_PAD = 128
GATE_NORMALIZER = 16.0
PROJ_A = 2 * KEY_DIM + 2 * VALUE_DIM + GATE_RANK
PROJ_A_PAD = 2 * KEY_DIM + 2 * VALUE_DIM + GATE_PAD
A_SHARD = PROJ_A // N_DEV
B_SHARD = 3 * D_MODEL // N_DEV
D_FF = 2816
ADAM_LR = 0.001
ADAM_B1 = 0.9
ADAM_B2 = 0.999
ADAM_EPS = 1e-08
ADAM_WD = 0.01
ADAM_STEP = 10
MESH_AXES = ("x", "y", "c")

VMEM_LIMIT = 56 * 1024 * 1024
ROW_CHUNK = 256
HALO = 16


def _params(sem=None, vmem=VMEM_LIMIT):
    return pltpu.CompilerParams(dimension_semantics=sem, vmem_limit_bytes=vmem)


NN = ((1,), (0,))
NT = ((1,), (1,))
TN = ((0,), (0,))


def _matmul(name, a, a_spec, b, b_spec, dims, grid, out_shape, out_spec, k_blocks=None, a_block_cols=None, res=None,
            res_spec=None, transpose_out=False, norm=None, swap=()):
    has_res = res is not None
    n_swap = len(swap)

    def body(*refs):
        a_ref, b_ref = refs[0], refs[1]
        r_ref = refs[2] if has_res else None

        def product(lhs, rhs):
            return lax.dot_general(lhs.astype(BF16), rhs, (dims, ((), ())), preferred_element_type=F32)

        if k_blocks is None:
            v = product(a_ref[...], b_ref[...])
        else:
            v = None
            for k in range(k_blocks):
                lhs = a_ref[k] if a_block_cols is None else a_ref[:, k * a_block_cols:(k + 1) * a_block_cols]
                p = product(lhs, b_ref[k])
                v = p if v is None else v + p
        if transpose_out:
            v = v.T
        if has_res:
            v = v + r_ref[...]
        if norm is None:
            o_ref = refs[2 + has_res]
            o_ref[...] = v.astype(o_ref.dtype)
            return
        n_in = 5 + has_res
        x_ref, g_ref, dxi_ref = refs[2 + has_res:n_in]
        dx_ref, dx16_ref, dg_ref = refs[n_in + n_swap:n_in + n_swap + 3]
        if n_swap:
            copies = _pair_copies(refs[n_in:n_in + n_swap], refs[n_in + n_swap + 3:n_in + 2 * n_swap + 3], *refs[-2:])

            @pl.when(pl.program_id(0) == 0)
            def _():
                for send, _ in copies:
                    send.start()

            @pl.when(pl.program_id(0) == grid[0] - 1)
            def _():
                for send, arrival in copies:
                    arrival.wait_recv()
                    send.wait_send()

        dx, dg = _norm_bwd_rows(x_ref[...], g_ref[...], v)
        dx = dxi_ref[...] + dx
        dx_ref[...] = dx
        dx16_ref[...] = dx.astype(BF16)

        @pl.when(pl.program_id(0) == 0)
        def _():
            dg_ref[...] = dg

        @pl.when(pl.program_id(0) > 0)
        def _():
            dg_ref[...] += dg

    operands = [a, b] + ([res] if has_res else [])
    in_specs = [a_spec, b_spec] + ([res_spec] if has_res else [])
    semantics = ("parallel",) * len(grid)
    scratch = []
    if norm is not None:
        vec = _spec((1, D_MODEL), lambda i: (0, 0))
        any_space = pl.BlockSpec(memory_space=pl.ANY)
        operands += list(norm) + list(swap)
        in_specs += [out_spec, vec, out_spec] + [any_space] * n_swap
        out_shape = [_act(dtype=F32), _act(), jax.ShapeDtypeStruct((1, D_MODEL), F32)]
        out_shape += [jax.ShapeDtypeStruct((N_DEV // 2,) + p.shape[1:], p.dtype) for p in swap]
        out_spec = [out_spec, out_spec, vec] + [any_space] * n_swap
        semantics = ("arbitrary",)
        if n_swap:
            scratch = [pltpu.SemaphoreType.DMA((n_swap, N_DEV // 2))] * 2
    return pl.pallas_call(
        body, name=name, grid=grid, in_specs=in_specs, out_specs=out_spec, out_shape=out_shape, scratch_shapes=scratch,
        compiler_params=_params(semantics),
    )(*operands)


def _resident(shape):
    return pl.BlockSpec(shape, lambda *_: (0,) * len(shape), pipeline_mode=pl.Buffered(1))


TM = 512
N_TM = SEQ // TM
PA_TILE = 640
N_PA = PROJ_A_PAD // PA_TILE
OUT_TILE = 256


def _spec(shape, fn):
    return pl.BlockSpec(shape, fn)


def _act(shape=(SEQ, D_MODEL), dtype=BF16):
    return jax.ShapeDtypeStruct(shape, dtype)


def _norm_proj(name, x, gamma, w):
    blocks = w.ndim == 3
    n_out = w.shape[0] * w.shape[2] if blocks else w.shape[0]

    def body(x_ref, g_ref, w_ref, h_ref, o_ref):
        x = x_ref[...]
        h = (x * _rstd(x) * g_ref[...]).astype(BF16)
        h_ref[...] = h
        if blocks:
            n = w.shape[2]
            for j in range(w.shape[0]):
                o_ref[:, j * n:(j + 1) * n] = jnp.dot(h, w_ref[j], preferred_element_type=F32).astype(BF16)
        else:
            o_ref[...] = lax.dot_general(h, w_ref[...], (NT, ((), ())), preferred_element_type=F32).astype(BF16)

    row = _spec((TM, D_MODEL), lambda i: (i, 0))
    return pl.pallas_call(
        body, name=name, grid=(N_TM,), in_specs=[row, _resident((1, D_MODEL)), _resident(w.shape)],
        out_specs=[row, _spec((TM, n_out), lambda i: (i, 0))], out_shape=[_act(), _act((SEQ, n_out))],
        compiler_params=_params(("parallel",)),
    )(x, gamma, w)


def _square(name, a, w, dims, x=None):
    row = _spec((TM, D_MODEL), lambda i: (i, 0))
    return _matmul(name, a, row, w, _resident((D_MODEL, D_MODEL)), dims, (N_TM,),
                   _act(dtype=F32 if x is not None else BF16), row, res=x, res_spec=row if x is not None else None)


def _sum_blocks_nn(name, a_blocks, w_blocks, x=None, norm=None, swap=()):
    nb, _, n = a_blocks.shape
    row = _spec((TM, D_MODEL), lambda i: (i, 0))
    return _matmul(name, a_blocks, _spec((nb, TM, n), lambda i: (0, i, 0)), w_blocks, _resident((nb, n, D_MODEL)),
                   NN, (N_TM,), _act(dtype=F32), row, k_blocks=nb, res=x, res_spec=row if x is not None else None, norm=norm, swap=swap)


def _sum_cols_nt(name, d, w_blocks, norm=None, swap=()):
    nb, _, n = w_blocks.shape
    return _matmul(name, d, _spec((TM, nb * n), lambda i: (i, 0)), w_blocks, _resident((nb, D_MODEL, n)), NT,
                   (N_TM,), _act(dtype=F32), _spec((TM, D_MODEL), lambda i: (i, 0)), k_blocks=nb, a_block_cols=n, norm=norm, swap=swap)


def _wide_nn(name, d, wt, x=None, norm=None, swap=()):
    n = wt.shape[0]
    row = _spec((TM, D_MODEL), lambda i: (i, 0))
    return _matmul(name, d, _spec((TM, n), lambda i: (i, 0)), wt, _resident((n, D_MODEL)), NN, (N_TM,),
                   _act(dtype=F32), row, res=x, res_spec=row if x is not None else None, norm=norm, swap=swap)


def _wide_nt(name, d, w):
    n = w.shape[0]
    return _matmul(name, d, _spec((TM, D_MODEL), lambda i: (i, 0)), w, _resident((n, D_MODEL)), NT, (N_TM,),
                   _act((SEQ, n)), _spec((TM, n), lambda i: (i, 0)))


def _wgrad_halves_tn(name, d, n_tile, h):
    _, _, n = d.shape
    return _matmul(name, d, _spec((None, SEQ, n_tile), lambda p, j: (p, 0, j)), h, _resident((SEQ, D_MODEL)), TN,
                   (2, n // n_tile), _act((2, n, D_MODEL)), _spec((None, n_tile, D_MODEL), lambda p, j: (p, j, 0)))


def _wgrad_cols_tn(name, d, n_tile, h):
    n = d.shape[1]
    return _matmul(name, d, _spec((SEQ, n_tile), lambda j: (0, j)), h, _resident((SEQ, D_MODEL)), TN,
                   (n // n_tile,), _act((n, D_MODEL)), _spec((n_tile, D_MODEL), lambda j: (j, 0)))


def _wgrad_cols_transposed_tn(name, h, d, n_tile):
    nb = d.shape[1] // n_tile
    return _matmul(name, d, _spec((SEQ, n_tile), lambda j: (0, j)), h, _resident((SEQ, D_MODEL)), TN, (nb,),
                   _act((nb, D_MODEL, n_tile)), _spec((None, D_MODEL, n_tile), lambda j: (j, 0, 0)), transpose_out=True)


NORM_ROWS = 512


def _rstd(x):
    return lax.rsqrt(jnp.mean(x * x, axis=-1, keepdims=True) + RMS_EPS)


def _norm_bwd_rows(x, gamma, dh):
    r = _rstd(x)
    xh = x * r
    dxh = dh * gamma
    dx = r * (dxh - xh * jnp.mean(dxh * xh, axis=-1, keepdims=True))
    return dx, jnp.sum(dh * xh, axis=0, keepdims=True)


def _down_loss_head(a, w_down, x_in, gamma, target):
    def body(a_ref, w_ref, x_ref, g_ref, t_ref, loss_ref, dx_ref, dx16_ref, dg_ref):
        x = x_ref[...] + jnp.dot(a_ref[...], w_ref[...], preferred_element_type=F32)
        gamma = g_ref[...]
        err = x * _rstd(x) * gamma - t_ref[...]
        dy = err * (1.0 / D_MODEL)
        dx, dg = _norm_bwd_rows(x, gamma, dy)
        dx_ref[...] = dx
        dx16_ref[...] = dx.astype(BF16)
        part = 0.5 * jnp.sum(jnp.sum(err * err, axis=-1, keepdims=True) * (1.0 / D_MODEL), axis=0, keepdims=True)
        part = jnp.broadcast_to(part, loss_ref.shape)

        @pl.when(pl.program_id(0) == 0)
        def _():
            dg_ref[...] = dg
            loss_ref[...] = part

        @pl.when(pl.program_id(0) > 0)
        def _():
            dg_ref[...] += dg
            loss_ref[...] += part

    row = _spec((TM, D_MODEL), lambda i: (i, 0))
    vec = _spec((1, D_MODEL), lambda i: (0, 0))
    return pl.pallas_call(
        body, name="ffn1_down_loss_head", grid=(N_TM,),
        in_specs=[_spec((TM, D_FF), lambda i: (i, 0)), _resident((D_FF, D_MODEL)), row, vec, row],
        out_specs=[_spec((1, 128), lambda i: (0, 0)), row, row, vec],
        out_shape=[jax.ShapeDtypeStruct((1, 128), F32), _act(dtype=F32), _act(), jax.ShapeDtypeStruct((1, D_MODEL), F32)],
        compiler_params=_params(("arbitrary",)),
    )(a, w_down, x_in, gamma, target)


def _sigmoid(x):
    return 1.0 / (1.0 + jnp.exp(-x))


def _rows(ref, c):
    return ref[pl.ds(pl.multiple_of(c * ROW_CHUNK, ROW_CHUNK), ROW_CHUNK), :].astype(F32)


def _rows_before(ref, c):
    start = pl.multiple_of(jnp.maximum(c * ROW_CHUNK - HALO, 0), HALO)
    rows = ref[pl.ds(start, HALO), :].astype(F32)
    return jnp.where(c > 0, rows, 0.0)


def _rows_after(ref, c, n_chunks):
    start = pl.multiple_of(jnp.minimum((c + 1) * ROW_CHUNK, SEQ - HALO), HALO)
    rows = ref[pl.ds(start, HALO), :].astype(F32)
    return jnp.where(c < n_chunks - 1, rows, 0.0)


def _shift_down(z, before, n):
    return pltpu.roll(jnp.concatenate([before, z], axis=0), n, 0)[before.shape[0]:]


def _shift_up(z, after, n):
    rows = z.shape[0]
    return pltpu.roll(jnp.concatenate([z, after], axis=0), rows + HALO - n, 0)[:rows]


def _conv_rows(z, before, w):
    z1 = _shift_down(z, before, 1)
    z2 = _shift_down(z, before, 2)
    return w[2:3, :] * z + w[1:2, :] * z1 + w[0:1, :] * z2, z1, z2


def _conv_t_rows(dy, after, w):
    return w[2:3, :] * dy + w[1:2, :] * _shift_up(dy, after, 1) + w[0:1, :] * _shift_up(dy, after, 2)


N_ROW_CHUNKS = SEQ // ROW_CHUNK


FF_COLS = 256
N_FF_COLS = D_FF // FF_COLS


def _ffn_mid_bwd(name, gu, conv_w, da):
    def body(gu_ref, w_ref, da_ref, dgu_ref, dw_ref, dgc_ref):
        w = w_ref[...]

        def first(c, acc):
            g = _rows(gu_ref.at[0], c)
            u = _rows(gu_ref.at[1], c)
            d = _rows(da_ref, c)
            gc, g1, g2 = _conv_rows(g, _rows_before(gu_ref.at[0], c), w)
            sg = _sigmoid(gc)
            rows = pl.ds(pl.multiple_of(c * ROW_CHUNK, ROW_CHUNK), ROW_CHUNK)
            dgu_ref[1, rows, :] = (d * gc * sg).astype(BF16)
            dgc = d * u * (sg * (1.0 + gc * (1.0 - sg)))
            dgc_ref[rows, :] = dgc
            return (acc[0] + jnp.sum(dgc * g2, axis=0, keepdims=True), acc[1] + jnp.sum(dgc * g1, axis=0, keepdims=True),
                    acc[2] + jnp.sum(dgc * g, axis=0, keepdims=True))

        zero = jnp.zeros((1, FF_COLS), F32)
        acc = lax.fori_loop(0, N_ROW_CHUNKS, first, (zero, zero, zero))
        for r in range(3):
            dw_ref[r:r + 1, :] = acc[r]

        def second(c, carry):
            dgc = _rows(dgc_ref, c)
            dg = _conv_t_rows(dgc, _rows_after(dgc_ref, c, N_ROW_CHUNKS), w)
            dgu_ref[0, pl.ds(pl.multiple_of(c * ROW_CHUNK, ROW_CHUNK), ROW_CHUNK), :] = dg.astype(BF16)
            return carry

        lax.fori_loop(0, N_ROW_CHUNKS, second, 0)

    pair = _spec((2, SEQ, FF_COLS), lambda j: (0, 0, j))
    wspec = _spec((3, FF_COLS), lambda j: (0, j))
    return pl.pallas_call(
        body, name=name, grid=(N_FF_COLS,), in_specs=[pair, wspec, _spec((SEQ, FF_COLS), lambda j: (0, j))],
        out_specs=[pair, wspec], out_shape=[_act((2, SEQ, D_FF)), jax.ShapeDtypeStruct((3, D_FF), F32)],
        scratch_shapes=[pltpu.VMEM((SEQ, FF_COLS), F32)],
        compiler_params=_params(("parallel",)),
    )(gu, conv_w, da)


SC_COLS = 256
N_SC = D_MODEL // SC_COLS


def _sc_specs():
    return [_spec((SEQ, SC_COLS), lambda j, part=part: (0, part * N_SC + j)) for part in range(3)]


def _sc_mid_fwd(p, conv_w):
    def body(b_ref, c_ref, h_ref, w_ref, y_ref):
        w = w_ref[...]

        def chunk(c, carry):
            z = _rows(c_ref, c) * _rows(h_ref, c)
            before = _rows_before(c_ref, c) * _rows_before(h_ref, c)
            zc, _, _ = _conv_rows(z, before, w)
            y_ref[pl.ds(pl.multiple_of(c * ROW_CHUNK, ROW_CHUNK), ROW_CHUNK), :] = (_rows(b_ref, c) * zc).astype(BF16)
            return carry

        lax.fori_loop(0, N_ROW_CHUNKS, chunk, 0)

    col = _spec((SEQ, SC_COLS), lambda j: (0, j))
    return pl.pallas_call(
        body, name="sc_mid_fwd", grid=(N_SC,), in_specs=_sc_specs() + [_spec((3, SC_COLS), lambda j: (0, j))], out_specs=col,
        out_shape=jax.ShapeDtypeStruct((SEQ, D_MODEL), BF16), compiler_params=_params(("parallel",)),
    )(p, p, p, conv_w)


def _sc_mid_bwd(p, conv_w, dy):
    def body(b_ref, c_ref, h_ref, w_ref, dy_ref, db_ref, dc_ref, dh_ref, dw_ref, dzc_ref):
        w = w_ref[...]

        def first(c, acc):
            z = _rows(c_ref, c) * _rows(h_ref, c)
            before = _rows_before(c_ref, c) * _rows_before(h_ref, c)
            zc, z1, z2 = _conv_rows(z, before, w)
            d = _rows(dy_ref, c)
            rows = pl.ds(pl.multiple_of(c * ROW_CHUNK, ROW_CHUNK), ROW_CHUNK)
            db_ref[rows, :] = (d * zc).astype(BF16)
            dzc = d * _rows(b_ref, c)
            dzc_ref[rows, :] = dzc
            return (acc[0] + jnp.sum(dzc * z2, axis=0, keepdims=True), acc[1] + jnp.sum(dzc * z1, axis=0, keepdims=True),
                    acc[2] + jnp.sum(dzc * z, axis=0, keepdims=True))

        zero = jnp.zeros((1, SC_COLS), F32)
        acc = lax.fori_loop(0, N_ROW_CHUNKS, first, (zero, zero, zero))
        for r in range(3):
            dw_ref[r:r + 1, :] = acc[r]

        def second(c, carry):
            dz = _conv_t_rows(_rows(dzc_ref, c), _rows_after(dzc_ref, c, N_ROW_CHUNKS), w)
            rows = pl.ds(pl.multiple_of(c * ROW_CHUNK, ROW_CHUNK), ROW_CHUNK)
            dc_ref[rows, :] = (dz * _rows(h_ref, c)).astype(BF16)
            dh_ref[rows, :] = (dz * _rows(c_ref, c)).astype(BF16)
            return carry

        lax.fori_loop(0, N_ROW_CHUNKS, second, 0)

    col = _spec((SEQ, SC_COLS), lambda j: (0, j))
    wspec = _spec((3, SC_COLS), lambda j: (0, j))
    act = jax.ShapeDtypeStruct((SEQ, D_MODEL), BF16)
    return pl.pallas_call(
        body, name="sc_mid_bwd", grid=(N_SC,), in_specs=_sc_specs() + [wspec, col], out_specs=[col, col, col, wspec],
        out_shape=[act, act, act, jax.ShapeDtypeStruct((3, D_MODEL), F32)],
        scratch_shapes=[pltpu.VMEM((SEQ, SC_COLS), F32)], compiler_params=_params(("parallel",)),
    )(p, p, p, conv_w, dy)


GLA_GROUP = 8
GLA_ROWS = GLA_GROUP * CHUNK
N_GROUPS = N_CHUNKS // GLA_GROUP
Q0, K0, V0, R0, G0 = 0, KEY_DIM, 2 * KEY_DIM, 2 * KEY_DIM + VALUE_DIM, 2 * KEY_DIM + 2 * VALUE_DIM


def _tri(strict):
    r = lax.broadcasted_iota(jnp.int32, (CHUNK, CHUNK), 0)
    c = lax.broadcasted_iota(jnp.int32, (CHUNK, CHUNK), 1)
    return jnp.where(c < r if strict else c <= r, 1.0, 0.0).astype(F32)


def _cumsum_rows(tri, x):
    tri = tri.astype(BF16)
    total = None
    for _ in range(3):
        term = x.astype(BF16)
        x = x - term.astype(F32)
        product = jnp.dot(tri, term, preferred_element_type=F32)
        total = product if total is None else total + product
    return total


def _gate_logits(gl, wgu, b_gate):
    return jnp.dot(gl, wgu, preferred_element_type=F32) + b_gate


def _log_decay(logits):
    return (jnp.minimum(logits, 0.0) - jnp.log(1.0 + jnp.exp(-jnp.abs(logits)))) * (1.0 / GATE_NORMALIZER)


def _head(x, h, width):
    return x[:, h * width:(h + 1) * width]


def _gla_fwd(proj, wgu, b_gate, gn):
    def body(p_ref, wgu_ref, b_ref, gn_ref, o_ref, og_ref, st_ref, state):
        @pl.when(pl.program_id(0) == 0)
        def _():
            state[...] = jnp.zeros_like(state)

        tri = _tri(False)
        la = _log_decay(_gate_logits(p_ref[:, G0:G0 + GATE_PAD], wgu_ref[...], b_ref[...]))
        decays = []
        for c in range(GLA_GROUP):
            rows = slice(c * CHUNK, (c + 1) * CHUNK)
            cum = _cumsum_rows(tri, la[rows])
            tot = cum[CHUNK - 1:CHUNK, :]
            kd = (p_ref[rows, K0:K0 + KEY_DIM].astype(F32) * jnp.exp(tot - cum)).astype(BF16)
            decays.append(jnp.exp(tot))
            v = p_ref[rows, V0:V0 + VALUE_DIM]
            for h in range(GLA_HEADS):
                st_ref[c, h] = lax.dot_general(
                    _head(v, h, HEAD_V), _head(kd, h, HEAD_K), (TN, ((), ())), preferred_element_type=F32)
        for c in range(GLA_GROUP):
            for h in range(GLA_HEADS):
                s = state[h] * _head(decays[c], h, HEAD_K) + st_ref[c, h]
                state[h] = s
                st_ref[c, h] = s
        for c in range(GLA_GROUP):
            rows = slice(c * CHUNK, (c + 1) * CHUNK)
            q = (p_ref[rows, Q0:Q0 + KEY_DIM].astype(F32) * (HEAD_K ** -0.5)).astype(BF16)
            for h in range(GLA_HEADS):
                o_ref[rows, h * HEAD_V:(h + 1) * HEAD_V] = lax.dot_general(
                    _head(q, h, HEAD_K), st_ref[c, h].astype(BF16), (NT, ((), ())), preferred_element_type=F32)
        r = p_ref[:, R0:R0 + VALUE_DIM].astype(F32)
        gate = r * _sigmoid(r) * gn_ref[...]
        for h in range(GLA_HEADS):
            cols = slice(h * HEAD_V, (h + 1) * HEAD_V)
            o = o_ref[:, cols]
            og_ref[:, cols] = (o * _rstd(o) * gate[:, cols]).astype(BF16)

    rows = _spec((GLA_ROWS, VALUE_DIM), lambda i: (i, 0))
    const = lambda shape: _spec(shape, lambda i: (0,) * len(shape))
    return pl.pallas_call(
        body, name="gla_fwd", grid=(N_GROUPS,),
        in_specs=[_spec((GLA_ROWS, PROJ_A_PAD), lambda i: (i, 0)), const((GATE_PAD, KEY_DIM)), const((1, KEY_DIM)),
                  const((1, VALUE_DIM))],
        out_specs=[rows, rows, _spec((GLA_GROUP, GLA_HEADS, HEAD_V, HEAD_K), lambda i: (i, 0, 0, 0))],
        out_shape=[jax.ShapeDtypeStruct((SEQ, VALUE_DIM), F32), jax.ShapeDtypeStruct((SEQ, VALUE_DIM), BF16),
                   jax.ShapeDtypeStruct((N_CHUNKS, GLA_HEADS, HEAD_V, HEAD_K), F32)],
        scratch_shapes=[pltpu.VMEM((GLA_HEADS, HEAD_V, HEAD_K), F32)], compiler_params=_params(("arbitrary",)),
    )(proj, wgu, b_gate, gn)


def _gla_bwd(proj, wgu, b_gate, gn, o, states, dog):
    last = N_GROUPS - 1

    def body(p_ref, wgu_ref, b_ref, gn_ref, o_ref, st_ref, stp_ref, dog_ref, dp_ref, dwgu_ref, db_ref, dgn_ref, carry, do_buf,
             g_buf):
        step = pl.program_id(0)

        @pl.when(step == 0)
        def _():
            carry[...] = jnp.zeros_like(carry)

        r = p_ref[:, R0:R0 + VALUE_DIM].astype(F32)
        sr = _sigmoid(r)
        silu = r * sr
        gn_row = gn_ref[...]
        dog_rows = dog_ref[...].astype(F32)
        dn = dog_rows * silu
        dgn_cols = []
        for h in range(GLA_HEADS):
            cols = slice(h * HEAD_V, (h + 1) * HEAD_V)
            oh = o_ref[:, cols]
            rs = _rstd(oh)
            ohat = oh * rs
            dn_h = dn[:, cols]
            dgn_cols.append(jnp.sum(dn_h * ohat, axis=0, keepdims=True))
            dohat = dn_h * gn_row[:, cols]
            do_buf[:, cols] = rs * (dohat - ohat * jnp.mean(dohat * ohat, axis=-1, keepdims=True))
            n_h = ohat * gn_row[:, cols]
            dp_ref[:, R0 + h * HEAD_V:R0 + (h + 1) * HEAD_V] = (
                dog_rows[:, cols] * n_h * (sr[:, cols] * (1.0 + r[:, cols] * (1.0 - sr[:, cols])))).astype(BF16)
        dgn = jnp.concatenate(dgn_cols, axis=1)

        tri = _tri(False)
        tri_strict = _tri(True)
        gl = p_ref[:, G0:G0 + GATE_PAD]
        logits = _gate_logits(gl, wgu_ref[...], b_ref[...])
        la = _log_decay(logits)
        fades, kds, decays = [], [], []
        for c in range(GLA_GROUP):
            rows = slice(c * CHUNK, (c + 1) * CHUNK)
            cum = _cumsum_rows(tri, la[rows])
            tot = cum[CHUNK - 1:CHUNK, :]
            fades.append(jnp.exp(tot - cum))
            kds.append(p_ref[rows, K0:K0 + KEY_DIM].astype(F32) * fades[c])
            decays.append(jnp.exp(tot))
            q = (p_ref[rows, Q0:Q0 + KEY_DIM].astype(F32) * (HEAD_K ** -0.5)).astype(BF16)
            do = do_buf[rows, :].astype(BF16)
            for h in range(GLA_HEADS):
                do_h = _head(do, h, HEAD_V)
                dq = jnp.dot(do_h, st_ref[c, h].astype(BF16), preferred_element_type=F32) * (HEAD_K ** -0.5)
                dp_ref[rows, Q0 + h * HEAD_K:Q0 + (h + 1) * HEAD_K] = dq.astype(BF16)
                g_buf[c, h] = lax.dot_general(do_h, _head(q, h, HEAD_K), (TN, ((), ())), preferred_element_type=F32)
        for c in reversed(range(GLA_GROUP)):
            for h in range(GLA_HEADS):
                g = carry[h] + g_buf[c, h]
                g_buf[c, h] = g
                carry[h] = g * _head(decays[c], h, HEAD_K)
        dlogit_rows = []
        for c in range(GLA_GROUP):
            rows = slice(c * CHUNK, (c + 1) * CHUNK)
            v = p_ref[rows, V0:V0 + VALUE_DIM]
            kd = kds[c].astype(BF16)
            dkd_cols, ddecay_cols = [], []
            for h in range(GLA_HEADS):
                g = g_buf[c, h]
                g16 = g.astype(BF16)
                dkd_cols.append(jnp.dot(_head(v, h, HEAD_V), g16, preferred_element_type=F32))
                dv = lax.dot_general(_head(kd, h, HEAD_K), g16, (NT, ((), ())), preferred_element_type=F32)
                dp_ref[rows, V0 + h * HEAD_V:V0 + (h + 1) * HEAD_V] = dv.astype(BF16)
                if c > 0:
                    s_prev = st_ref[c - 1, h]
                else:
                    s_prev = jnp.where(step < last, stp_ref[0, h], 0.0)
                ddecay_cols.append(jnp.sum(g * s_prev, axis=0, keepdims=True))
            dkd = jnp.concatenate(dkd_cols, axis=1)
            ddecay = jnp.concatenate(ddecay_cols, axis=1)
            dp_ref[rows, K0:K0 + KEY_DIM] = (dkd * fades[c]).astype(BF16)
            e = dkd * kds[c]
            dla = ddecay * decays[c] + _cumsum_rows(tri_strict, e)
            dlogit_rows.append(dla * (1.0 / GATE_NORMALIZER) * (1.0 - _sigmoid(logits[rows])))
        dlogit = jnp.concatenate(dlogit_rows, axis=0)
        dlogit16 = dlogit.astype(BF16)
        dp_ref[:, G0:G0 + GATE_PAD] = lax.dot_general(
            dlogit16, wgu_ref[...], (NT, ((), ())), preferred_element_type=F32).astype(BF16)
        dwgu = lax.dot_general(gl, dlogit16, (TN, ((), ())), preferred_element_type=F32)
        db = jnp.sum(dlogit, axis=0, keepdims=True)

        @pl.when(step == 0)
        def _():
            dwgu_ref[...] = dwgu
            db_ref[...] = db
            dgn_ref[...] = dgn

        @pl.when(step > 0)
        def _():
            dwgu_ref[...] += dwgu
            db_ref[...] += db
            dgn_ref[...] += dgn

    rev = lambda i: (last - i, 0)
    rows = _spec((GLA_ROWS, VALUE_DIM), rev)
    const = lambda shape: _spec(shape, lambda i: (0,) * len(shape))
    st_shape = (GLA_HEADS, HEAD_V, HEAD_K)
    return pl.pallas_call(
        body, name="gla_bwd", grid=(N_GROUPS,),
        in_specs=[_spec((GLA_ROWS, PROJ_A_PAD), rev), const((GATE_PAD, KEY_DIM)), const((1, KEY_DIM)), const((1, VALUE_DIM)),
                  rows, _spec((GLA_GROUP,) + st_shape, lambda i: (last - i, 0, 0, 0)),
                  _spec((1,) + st_shape, lambda i: (jnp.maximum((last - i) * GLA_GROUP - 1, 0), 0, 0, 0)), rows],
        out_specs=[_spec((GLA_ROWS, PROJ_A_PAD), rev), const((GATE_PAD, KEY_DIM)), const((1, KEY_DIM)), const((1, VALUE_DIM))],
        out_shape=[jax.ShapeDtypeStruct((SEQ, PROJ_A_PAD), BF16), jax.ShapeDtypeStruct((GATE_PAD, KEY_DIM), F32),
                   jax.ShapeDtypeStruct((1, KEY_DIM), F32), jax.ShapeDtypeStruct((1, VALUE_DIM), F32)],
        scratch_shapes=[pltpu.VMEM(st_shape, F32), pltpu.VMEM((GLA_ROWS, VALUE_DIM), F32), pltpu.VMEM((GLA_GROUP,) + st_shape, F32)],
        compiler_params=_params(("arbitrary",)),
    )(proj, wgu, b_gate, gn, o, states, states, dog)


WGRAD_FF_TILE = D_FF // 2


CARRY_ROWS = 8
UP_ROWS = 512


def _ffn_up_mid(name, x, gamma, w_up_t, conv_w):
    def body(x_ref, g_ref, w_ref, c_ref, h_ref, gu_ref, a_ref, carry):
        @pl.when(pl.program_id(0) == 0)
        def _():
            carry[...] = jnp.zeros_like(carry)

        x_tile = x_ref[...]
        h_tile = (x_tile * _rstd(x_tile) * g_ref[...]).astype(BF16)
        h_ref[...] = h_tile
        for k in range(N_FF_COLS):
            cols = slice(k * FF_COLS, (k + 1) * FF_COLS)
            g, u = (lax.dot_general(h_tile, w_ref[p, cols, :], (NT, ((), ())), preferred_element_type=F32).astype(BF16)
                    for p in range(2))
            gu_ref[0, :, cols] = g
            gu_ref[1, :, cols] = u
            g = g.astype(F32)
            w = c_ref[:, cols]
            before = carry[:, cols]
            gc = w[2:3, :] * g + w[1:2, :] * _shift_down(g, before, 1) + w[0:1, :] * _shift_down(g, before, 2)
            a_ref[:, cols] = (gc * _sigmoid(gc) * u.astype(F32)).astype(BF16)
            carry[:, cols] = g[UP_ROWS - CARRY_ROWS:, :]

    row = _spec((UP_ROWS, D_MODEL), lambda i: (i, 0))
    return pl.pallas_call(
        body, name=name, grid=(SEQ // UP_ROWS,),
        in_specs=[row, _resident((1, D_MODEL)), _resident((2, D_FF, D_MODEL)), _resident((3, D_FF))],
        out_specs=[row, _spec((2, UP_ROWS, D_FF), lambda i: (0, i, 0)), _spec((UP_ROWS, D_FF), lambda i: (i, 0))],
        out_shape=[_act(), _act((2, SEQ, D_FF)), _act((SEQ, D_FF))], scratch_shapes=[pltpu.VMEM((CARRY_ROWS, D_FF), F32)],
        compiler_params=_params(("arbitrary",)),
    )(x, gamma, w_up_t, conv_w)


def _ffn_fwd(tag, x, gamma, w_up_t, conv_w, w_down):
    h, gu, a = _ffn_up_mid(f"ffn{tag}_up_mid", x, gamma, w_up_t, conv_w)
    return _wide_nn(f"ffn{tag}_down", a, w_down, x=x), (h, gu, a)


def _owner_blocks(d, rows=None):
    if rows is not None:
        d = d[:rows]
    return d.reshape((N_DEV, -1) + d.shape[-1:])


def _ffn_bwd(tag, x, gamma, w_up_t, conv_w, w_down, saved, dx, dx16, swap):
    h, gu, a = saved
    da = _wide_nt(f"ffn{tag}_da", dx16, w_down)
    d_w_down = _owner_blocks(_wgrad_cols_tn(f"ffn{tag}_dwdown", a, WGRAD_FF_TILE, dx16))
    dgu, d_conv = _ffn_mid_bwd(f"ffn{tag}_mid_bwd", gu, conv_w, da)
    d_w_up_t = _owner_blocks(_wgrad_halves_tn(f"ffn{tag}_dwup", dgu, WGRAD_FF_TILE, h))
    parts = (d_w_up_t, d_w_down)
    dx, dx16, d_gamma, *received = _sum_blocks_nn(
        f"ffn{tag}_dh", dgu, w_up_t, norm=(x, gamma, dx), swap=parts if swap else ())
    return dx, dx16, d_gamma, d_conv, parts, received


def _local_step(x, target, w, fetch=None, emit=None):
    if fetch is None:
        local = dict(a=(w.get("a_w_in"), w.get("a_w_out")), b=(w.get("b_w_in"), w.get("b_w_out")))
        for layer in range(2):
            local[f"f{layer}"] = (w["f_w_up"][layer], w["f_w_down"][layer]) if "f_w_up" in w else None
        fetch = lambda group, after: local[group]
    swap = emit is not None
    if emit is None:
        emit = lambda group, parts, received, dx: dx
    f_norm = (w["f_norm"][0:1], w["f_norm"][1:2])

    x0 = x
    a_w_in, a_w_out = fetch("a", x0)
    h0, proj = _norm_proj("a_in", x0, w["a_norm"], a_w_in)
    o, og, states = _gla_fwd(proj, w["a_w_gate_up"], w["a_b_gate"], w["a_gn"])
    x1 = _square("a_out", og, a_w_out, NN, x0)
    up0, down0 = fetch("f0", x1)
    x2, ffn0 = _ffn_fwd(0, x1, f_norm[0], up0, w["f_conv"][0], down0)
    b_w_in, b_w_out = fetch("b", x2)
    h2, p = _norm_proj("b_in", x2, w["b_norm"], b_w_in)
    y = _sc_mid_fwd(p, w["b_conv"])
    x3 = _square("b_out", y, b_w_out, NN, x2)
    up1, down1 = fetch("f1", x3)
    ffn1 = _ffn_up_mid("ffn1_up_mid", x3, f_norm[1], up1, w["f_conv"][1])
    loss, dx, dx16, d_final_norm = _down_loss_head(ffn1[2], down1, x3, w["final_norm"], target)

    dx, dx16, d_f_norm1, d_fconv1, parts_f1, got = _ffn_bwd(
        1, x3, f_norm[1], up1, w["f_conv"][1], down1, ffn1, dx, dx16, swap)
    dx16 = emit("f1", parts_f1, got, dx16)

    dy = _square("b_dy", dx16, b_w_out, NT)
    d_b_w_out = _owner_blocks(_wgrad_cols_tn("b_dwout", y, OUT_TILE, dx16))
    db, dc, dhh, d_b_conv = _sc_mid_bwd(p, w["b_conv"], dy)
    dp = jnp.concatenate([db, dc, dhh], axis=1)
    parts_b = (_wgrad_cols_transposed_tn("b_dwin", h2, dp, B_SHARD), d_b_w_out)
    dx, dx16, d_b_norm, *got = _sum_cols_nt("b_dh", dp, b_w_in, norm=(x2, w["b_norm"], dx), swap=parts_b if swap else ())
    dx16 = emit("b", parts_b, got, dx16)

    dx, dx16, d_f_norm0, d_fconv0, parts_f0, got = _ffn_bwd(
        0, x1, f_norm[0], up0, w["f_conv"][0], down0, ffn0, dx, dx16, swap)
    dx16 = emit("f0", parts_f0, got, dx16)

    dog = _square("a_dog", dx16, a_w_out, NT)
    d_a_w_out = _owner_blocks(_wgrad_cols_tn("a_dwout", og, OUT_TILE, dx16))
    dproj, d_wgu, d_b_gate, d_gn = _gla_bwd(proj, w["a_w_gate_up"], w["a_b_gate"], w["a_gn"], o, states, dog)
    parts_a = (_owner_blocks(_wgrad_cols_tn("a_dwin", dproj, PA_TILE, h0), PROJ_A), d_a_w_out)
    dx, _, d_a_norm, *got = _wide_nn("a_dh", dproj, a_w_in, norm=(x0, w["a_norm"], dx), swap=parts_a if swap else ())
    emit("a", parts_a, got, dx)

    grads = dict(
        a_norm=d_a_norm, a_w_in=parts_a[0], a_w_gate_up=d_wgu, a_b_gate=d_b_gate, a_gn=d_gn, a_w_out=parts_a[1],
        b_norm=d_b_norm, b_w_in=parts_b[0], b_conv=d_b_conv, b_w_out=parts_b[1],
        f_norm=(d_f_norm0, d_f_norm1), f_w_up=(parts_f0[0], parts_f1[0]), f_conv=(d_fconv0, d_fconv1),
        f_w_down=(parts_f0[1], parts_f1[1]), final_norm=d_final_norm)
    grads["loss"] = loss
    return dx, grads


MESH_ID = pl.DeviceIdType.MESH
ANY = pl.BlockSpec(memory_space=pl.ANY)
N_PEERS = N_DEV - 1


def _position():
    return lax.axis_index("x"), lax.axis_index("y"), lax.axis_index("c")


def _slot(px, py, pc):
    return 4 * px + 2 * py + pc


GATHER_COPIES = 8
HALF_ROWS = 16


def _gather_copies(src, out, send_sems, recv_sems, local_sems):
    n = len(src)
    to_sibling, to_x, to_y, x_on_to_y, y_on_to_x, x_to_sibling, y_to_sibling, diagonal_to_sibling = range(GATHER_COPIES)
    x, y, c = _position()
    me, sibling = (x, y, c), (x, y, 1 - c)
    x_side, y_side, diagonal = (1 - x, y), (x, 1 - y), (1 - x, 1 - y)

    def rows_of(t, half):
        rows = src[t].shape[0]
        half_rows = rows // 2 // HALF_ROWS * HALF_ROWS
        return (pl.ds(0, rows), pl.ds(0, half_rows), pl.ds(half_rows, rows - half_rows))[half]

    def copy(t, j, block, to, half=0, from_input=False):
        dst = out[t].at[_slot(*block), rows_of(t, half)]
        return pltpu.make_async_remote_copy(
            src_ref=src[t] if from_input else dst, dst_ref=dst, send_sem=send_sems.at[GATHER_COPIES * t + j],
            recv_sem=recv_sems.at[GATHER_COPIES * t + j], device_id=to, device_id_type=MESH_ID)

    mine = [pltpu.make_async_copy(src[t], out[t].at[_slot(*me)], local_sems.at[t]) for t in range(n)]
    for cp in mine:
        cp.start()
    sent = []

    def start(cp):
        cp.start()
        sent.append(cp)

    for t in range(n):
        start(copy(t, to_sibling, me, sibling, from_input=True))
        start(copy(t, to_x, me, (*x_side, c), from_input=True))
        start(copy(t, to_y, me, (*y_side, c), from_input=True))
    for t in range(n):
        copy(t, to_x, (*x_side, c), me).wait_recv()
        start(copy(t, x_on_to_y, (*x_side, c), (*y_side, c), half=1))
        start(copy(t, x_to_sibling, (*x_side, c), sibling))
        copy(t, to_y, (*y_side, c), me).wait_recv()
        start(copy(t, y_on_to_x, (*y_side, c), (*x_side, c), half=2))
        start(copy(t, y_to_sibling, (*y_side, c), sibling))
    for t in range(n):
        copy(t, x_on_to_y, (*diagonal, c), me, half=1).wait_recv()
        copy(t, y_on_to_x, (*diagonal, c), me, half=2).wait_recv()
        start(copy(t, diagonal_to_sibling, (*diagonal, c), sibling))
    for t in range(n):
        copy(t, to_sibling, sibling, me).wait_recv()
        for j, chip in ((x_to_sibling, x_side), (y_to_sibling, y_side), (diagonal_to_sibling, diagonal)):
            copy(t, j, (*chip, 1 - c), me).wait_recv()
    for cp in sent:
        cp.wait_send()
    for cp in mine:
        cp.wait()


def _all_gather(name, shards):
    n = len(shards)

    def body(*refs):
        _gather_copies(refs[:n], refs[n:2 * n], *refs[2 * n:])

    sems = pltpu.SemaphoreType.DMA((GATHER_COPIES * n,))
    return pl.pallas_call(
        body, name=name, in_specs=[ANY] * n, out_specs=[ANY] * n,
        out_shape=[jax.ShapeDtypeStruct((N_DEV,) + s.shape, s.dtype) for s in shards],
        scratch_shapes=[sems, sems, pltpu.SemaphoreType.DMA((n,))],
    )(*shards)


SIBLING_AND_NEIGHBOURS = (1, 2, 4)
SAME_CORE = (2, 4, 6)


def _flip(x, y, c, k):
    return x ^ (k >> 2), y ^ ((k >> 1) & 1), c ^ (k & 1)


N_CHIPS = N_DEV // 2


def _chip(px, py):
    return 2 * px + py


def _pair_copies(parts, received, send_sems, recv_sems):
    x, y, c = lax.axis_index("x"), lax.axis_index("y"), lax.axis_index("c")
    sibling = (x, y, 1 - c)
    copies = []
    for t in range(len(parts)):
        for q in range(N_DEV // 2):
            send = pltpu.make_async_remote_copy(
                src_ref=parts[t].at[2 * q + 1 - c], dst_ref=received[t].at[q], send_sem=send_sems.at[t, q],
                recv_sem=recv_sems.at[t, q], device_id=sibling, device_id_type=pl.DeviceIdType.MESH)
            landed = received[t].at[q]
            arrival = pltpu.make_async_remote_copy(
                src_ref=landed, dst_ref=landed, send_sem=send_sems.at[t, q], recv_sem=recv_sems.at[t, q],
                device_id=sibling, device_id_type=pl.DeviceIdType.MESH)
            copies.append((send, arrival))
    return copies


PAIR_ROWS = 1024


def _pair_add(name, part, received, side):
    _, rows, cols = part.shape
    tiles = [t for t in range(PAIR_ROWS, 0, -BF16_ROWS) if rows % t == 0]
    tr = tiles[0] if tiles else rows

    def body(side_ref, p_ref, r_ref, o_ref):
        o_ref[...] = (p_ref[...].astype(F32) + r_ref[...].astype(F32)).astype(BF16)

    tile = _spec((None, tr, cols), lambda q, i, side_ref: (q, i, 0))
    return pl.pallas_call(
        body, name=name,
        grid_spec=pltpu.PrefetchScalarGridSpec(
            num_scalar_prefetch=1, grid=(N_CHIPS, rows // tr),
            in_specs=[_spec((None, tr, cols), lambda q, i, side_ref: (2 * q + side_ref[0], i, 0)), tile], out_specs=tile),
        out_shape=jax.ShapeDtypeStruct((N_CHIPS, rows, cols), BF16), compiler_params=_params(("parallel", "parallel")),
    )(side, part, received)


def _send_copy(parts, landing, send_sems, recv_sems, t, s, k):
    x, y, c = _position()
    px, py, _ = _flip(x, y, c, k)
    return pltpu.make_async_remote_copy(
        src_ref=parts[t].at[_chip(px, py)], dst_ref=landing[t].at[_chip(x, y)], send_sem=send_sems.at[s],
        recv_sem=recv_sems.at[s], device_id=(px, py, c), device_id_type=MESH_ID)


def _send_arrival(landing, send_sems, recv_sems, t, s, k):
    x, y, c = _position()
    px, py, _ = _flip(x, y, c, k)
    landed = landing[t].at[_chip(px, py)]
    return pltpu.make_async_remote_copy(
        src_ref=landed, dst_ref=landed, send_sem=send_sems.at[s], recv_sem=recv_sems.at[s],
        device_id=(px, py, c), device_id_type=MESH_ID)


def _handshake(peers):
    x, y, c = _position()
    barrier = pltpu.get_barrier_semaphore()
    for k in peers:
        pl.semaphore_signal(barrier, inc=1, device_id=_flip(x, y, c, k), device_id_type=MESH_ID)
    pl.semaphore_wait(barrier, len(peers))


def _sequencer(name, collective_id, n_copies, body, operands, out_type):
    n_arrays = len(operands)
    return pl.kernel(
        body, out_type=out_type, mesh=plsc.ScalarSubcoreMesh(axis_name="sequencer", num_cores=1), name=name,
        scratch_types=(pltpu.SemaphoreType.DMA((n_copies,)), pltpu.SemaphoreType.DMA((n_copies,)),
                       pltpu.SemaphoreType.DMA((n_arrays,))),
        compiler_params=pltpu.CompilerParams(collective_id=collective_id))(*operands)


def _sequencer_exchange(name, collective_id, parts, after=()):
    n, n_peers, n_in = len(parts), len(SAME_CORE), len(parts) + len(after)

    def body(*refs):
        src, landing = refs[:n], refs[n_in:n_in + n]
        send_sems, recv_sems, local_sems = refs[n_in + n:]
        _handshake(SAME_CORE)
        x, y, _ = _position()
        mine = [pltpu.make_async_copy(src[t].at[_chip(x, y)], landing[t].at[_chip(x, y)], local_sems.at[t]) for t in range(n)]
        for cp in mine:
            cp.start()
        sent = [_send_copy(src, landing, send_sems, recv_sems, t, t * n_peers + j, k)
                for t in range(n) for j, k in enumerate(SAME_CORE)]
        for cp in sent:
            cp.start()
        for t in range(n):
            for j, k in enumerate(SAME_CORE):
                _send_arrival(landing, send_sems, recv_sems, t, t * n_peers + j, k).wait_recv()
        for cp in sent:
            cp.wait_send()
        for cp in mine:
            cp.wait()

    landing = [jax.ShapeDtypeStruct(p.shape, p.dtype) for p in parts]
    return _sequencer(name, collective_id, n * n_peers, body, list(parts) + list(after), landing)


def _sequencer_gather(name, collective_id, shards):
    n = len(shards)

    def body(*refs):
        _handshake(SIBLING_AND_NEIGHBOURS)
        _gather_copies(refs[:n], refs[n:2 * n], *refs[2 * n:])

    gathered = [jax.ShapeDtypeStruct((N_DEV,) + s.shape, s.dtype) for s in shards]
    return _sequencer(name, collective_id, GATHER_COPIES * n, body, shards, gathered)


ADAM_ROWS = 512
BF16_ROWS = 16


def _adam_update(w, g, m, v):
    m = ADAM_B1 * m + (1.0 - ADAM_B1) * g
    v = ADAM_B2 * v + (1.0 - ADAM_B2) * (g * g)
    m_hat = m / (1.0 - ADAM_B1 ** ADAM_STEP)
    v_hat = v / (1.0 - ADAM_B2 ** ADAM_STEP)
    delta = -ADAM_LR * (m_hat / (jnp.sqrt(v_hat) + ADAM_EPS) + ADAM_WD * w)
    return delta, m, v


def _sum_slots(ref):
    total = ref[0].astype(F32)
    for d in range(1, ref.shape[0]):
        total = total + ref[d].astype(F32)
    return total


def _adamw_sum(name, landed, w, m, v):
    layers, rows, cols = w.shape
    tiles = [t for t in range(ADAM_ROWS, 0, -BF16_ROWS) if rows % t == 0]
    tr = tiles[0] if tiles else rows
    nt = rows // tr

    def body(*refs):
        parts = refs[:layers]
        w_ref, m_ref, v_ref, g_ref, d_ref, nm_ref, nv_ref = refs[layers:]
        layer = pl.program_id(0)
        g = _sum_slots(parts[0])
        for q in range(1, layers):
            g = jnp.where(layer == q, _sum_slots(parts[q]), g)
        delta, new_m, new_v = _adam_update(w_ref[...], g, m_ref[...], v_ref[...])
        g_ref[...] = g
        d_ref[...] = delta
        nm_ref[...] = new_m
        nv_ref[...] = new_v

    def part_spec(q):
        return _spec((N_CHIPS, tr, cols), lambda l, i: (0, jnp.where(l == q, i, jnp.where(l < q, 0, nt - 1)), 0))

    tile = _spec((None, tr, cols), lambda l, i: (l, i, 0))
    out = jax.ShapeDtypeStruct((layers, rows, cols), F32)
    return pl.pallas_call(
        body, name=name, grid=(layers, nt), in_specs=[part_spec(q) for q in range(layers)] + [tile] * 3,
        out_specs=[tile] * 4, out_shape=[out] * 4, compiler_params=_params(("arbitrary", "arbitrary")),
    )(*landed, w, m, v)


def _sum_small(landed):
    def body(in_ref, out_ref):
        out_ref[...] = _sum_slots(in_ref)

    return pl.pallas_call(body, name="small_grad_sum", out_shape=jax.ShapeDtypeStruct(landed.shape[1:], F32))(landed)


def _adamw_small(arrays):
    n = len(arrays)

    def body(*refs):
        for i in range(n):
            g_ref, w_ref, m_ref, v_ref = refs[4 * i:4 * i + 4]
            d_ref, nm_ref, nv_ref = refs[4 * n + 3 * i:4 * n + 3 * i + 3]
            d_ref[...], nm_ref[...], nv_ref[...] = _adam_update(w_ref[...], g_ref[...], m_ref[...], v_ref[...])

    out = [jax.ShapeDtypeStruct(w.shape, F32) for _, w, _, _ in arrays for _ in range(3)]
    flat = pl.pallas_call(body, name="adam_small", out_shape=out)(*[a for group in arrays for a in group])
    return [tuple(flat[3 * i:3 * i + 3]) for i in range(n)]


LANES = 128
SUBLANES = 8
F_CONV_SHARD = D_FF // N_DEV
GATE_SHARD = KEY_DIM // N_DEV
NORM_SHARD = D_MODEL // N_DEV


def _tile_rows(a):
    flat = a.reshape(-1)
    size = -(-flat.shape[0] // (SUBLANES * LANES)) * SUBLANES * LANES
    return jnp.pad(flat, (0, size - flat.shape[0])).reshape(-1, LANES)


def _pack_rows(pieces):
    return jnp.concatenate([_tile_rows(p) for p in pieces], axis=0)


def _unpack_rows(packed, shapes):
    out, row = [], 0
    for shape in shapes:
        size = 1
        for s in shape:
            size *= s
        rows = -(-size // (SUBLANES * LANES)) * SUBLANES
        piece = packed[..., row:row + rows, :]
        out.append(piece.reshape(piece.shape[:-2] + (rows * LANES,))[..., :size])
        row += rows
    return out


SMALL_SHARDS = ((GATE_RANK, GATE_SHARD), (1, NORM_SHARD), (3, NORM_SHARD), (2, 3, F_CONV_SHARD))


def _unpack_small_shards(g):
    gate, b_norm, b_conv, f_conv = _unpack_rows(g, SMALL_SHARDS)
    gate = gate.reshape(N_DEV, GATE_RANK, GATE_SHARD).transpose(1, 0, 2).reshape(GATE_RANK, KEY_DIM)
    b_norm = b_norm.reshape(1, D_MODEL)
    b_conv = b_conv.reshape(N_DEV, 3, NORM_SHARD).transpose(1, 0, 2).reshape(3, D_MODEL)
    f_conv = f_conv.reshape(N_DEV, 2, 3, F_CONV_SHARD).transpose(1, 2, 0, 3).reshape(2, 3, D_FF)
    return gate, b_norm, b_conv, f_conv


SMALL_LAYOUT = (("a_norm", (1, D_MODEL)), ("a_w_gate_up", (GATE_RANK, KEY_DIM)), ("a_b_gate", (1, KEY_DIM)), ("a_gn", (1, VALUE_DIM)),
                ("b_norm", (1, D_MODEL)), ("b_conv", (3, D_MODEL)), ("f_norm0", (1, D_MODEL)), ("f_norm1", (1, D_MODEL)),
                ("f_conv0", (3, D_FF)), ("f_conv1", (3, D_FF)), ("final_norm", (1, D_MODEL)), ("loss", (1, LANES)))


def _pack_small_grads(g):
    full = dict(g)
    full["a_w_gate_up"] = g["a_w_gate_up"][:GATE_RANK]
    for layer in range(2):
        full[f"f_norm{layer}"] = g["f_norm"][layer]
        full[f"f_conv{layer}"] = g["f_conv"][layer]
    return _pack_rows([full[name] for name, _ in SMALL_LAYOUT])


def _unpack_small_grads(packed):
    pieces = _unpack_rows(packed, [shape for _, shape in SMALL_LAYOUT])
    out = {name: piece.reshape(shape) for (name, shape), piece in zip(SMALL_LAYOUT, pieces)}
    out["f_norm"] = jnp.stack([out["f_norm0"][0], out["f_norm1"][0]])
    out["f_conv"] = jnp.stack([out["f_conv0"], out["f_conv1"]])
    return out


def kernel(x, a_norm, a_w_in, a_w_gate_up, a_b_gate, a_gn, a_w_out, b_norm, b_w_in, b_conv, b_w_out, f_norm, f_w_up, f_conv, f_w_down, final_norm, loss_target, m_a_norm, m_a_w_in, m_a_w_gate_up, m_a_b_gate, m_a_gn, m_a_w_out, m_b_norm, m_b_w_in, m_b_conv, m_b_w_out, m_f_norm, m_f_w_up, m_f_conv, m_f_w_down, m_final_norm, v_a_norm, v_a_w_in, v_a_w_gate_up, v_a_b_gate, v_a_gn, v_a_w_out, v_b_norm, v_b_w_in, v_b_conv, v_b_w_out, v_f_norm, v_f_w_up, v_f_conv, v_f_w_down, v_final_norm):
    my_slot = _slot(*_position())

    transposed = lambda w: jnp.swapaxes(w, 1, 2)
    a_transposed = lambda w: w.reshape(D_MODEL, A_SHARD).T.reshape(1, A_SHARD, D_MODEL)
    a_w_in_t, f_w_up_t = a_transposed(a_w_in), transposed(f_w_up)
    first = _all_gather("weight_gather", [a_w_in_t[0].astype(BF16), a_w_out[0].astype(BF16),
                                          _pack_rows([a_w_gate_up[0], b_norm, b_conv[0], f_conv])])
    gathers, small_shards = {}, first[2]
    later = (("f0", f_w_up_t[0], f_w_down[0]), ("b", b_w_in[0], b_w_out[0]), ("f1", f_w_up_t[1], f_w_down[1]))
    for collective_id, (group, w_in, w_out) in enumerate(later):
        w_in, w_out, small_shards = lax.optimization_barrier((w_in.astype(BF16), w_out.astype(BF16), small_shards))
        gathers[group] = _sequencer_gather(f"gather_{group}", collective_id, [w_in, w_out])
    gate_full, b_norm_full, b_conv_full, f_conv_full = _unpack_small_shards(small_shards)
    a_w_in_full = jnp.pad(first[0].reshape(PROJ_A, D_MODEL), ((0, PROJ_A_PAD - PROJ_A), (0, 0)))
    weights = dict(
        a_norm=a_norm, a_w_gate_up=jnp.pad(gate_full, ((0, GATE_PAD - GATE_RANK), (0, 0))).astype(BF16), a_b_gate=a_b_gate,
        a_gn=a_gn, b_norm=b_norm_full, b_conv=b_conv_full, f_norm=f_norm, f_conv=f_conv_full,
        final_norm=final_norm.reshape(1, D_MODEL))

    def fetch(group, after):
        if group == "a":
            return a_w_in_full, first[1].reshape(D_MODEL, D_MODEL)
        w_in, w_out = gathers[group]
        if group == "b":
            return w_in, w_out.reshape(D_MODEL, D_MODEL)
        return w_in.reshape(2, D_FF, D_MODEL), w_out.reshape(D_FF, D_MODEL)

    exchanges, pending = {}, []
    exchange_ids = dict(b=3, f0=4, a=5)
    side = lax.axis_index("c").astype(jnp.int32).reshape(1)

    def emit(group, parts, received, carry):
        sums = [_pair_add(f"pair_add_{group}_{i}", part, got, side) for i, (part, got) in enumerate(zip(parts, received))]
        carry, *sums = lax.optimization_barrier((carry, *sums))
        pending.extend(sums)
        if group != "f1":
            after = list(exchanges.values())[-1][:1] if exchanges else ()
            exchanges[group] = _sequencer_exchange(f"grads_{group}", exchange_ids[group], list(pending), after)
            pending.clear()
        return carry

    dx, g = _local_step(x[0], loss_target[0], weights, fetch, emit)

    (up1, down1, d_b_in, d_b_out), (up0, down0), (d_a_in, d_a_out) = (exchanges[group] for group in ("b", "f0", "a"))
    back = lambda results: tuple(transposed(r) for r in results)
    big = dict(
        b_w_in=_adamw_sum("adam_b_w_in", [d_b_in], b_w_in, m_b_w_in, v_b_w_in),
        b_w_out=_adamw_sum("adam_b_w_out", [d_b_out], b_w_out, m_b_w_out, v_b_w_out),
        f_w_up=back(_adamw_sum("adam_f_w_up", [up0, up1], f_w_up_t, transposed(m_f_w_up), transposed(v_f_w_up))),
        f_w_down=_adamw_sum("adam_f_w_down", [down0, down1], f_w_down, m_f_w_down, v_f_w_down))
    small_packed, *updated = lax.optimization_barrier((_pack_small_grads(g), *big["f_w_down"]))
    big["f_w_down"] = tuple(updated)
    small_landed = _all_gather("small_grad_gather", [small_packed])[0]
    big.update(
        a_w_in=tuple(r.reshape(A_SHARD, D_MODEL).T.reshape(1, D_MODEL, A_SHARD) for r in _adamw_sum(
            "adam_a_w_in", [d_a_in], a_w_in_t, a_transposed(m_a_w_in), a_transposed(v_a_w_in))),
        a_w_out=_adamw_sum("adam_a_w_out", [d_a_out], a_w_out, m_a_w_out, v_a_w_out))
    small_g = _unpack_small_grads(_sum_small(small_landed))
    loss = small_g["loss"][0, 0]
    small_g["a_w_gate_up"] = lax.dynamic_slice_in_dim(small_g["a_w_gate_up"], my_slot * GATE_SHARD, GATE_SHARD, axis=1)
    small_g["b_norm"] = lax.dynamic_slice_in_dim(small_g["b_norm"], my_slot * NORM_SHARD, NORM_SHARD, axis=1)
    small_g["b_conv"] = lax.dynamic_slice_in_dim(small_g["b_conv"], my_slot * NORM_SHARD, NORM_SHARD, axis=1)
    small_g["f_conv"] = lax.dynamic_slice_in_dim(small_g["f_conv"], my_slot * F_CONV_SHARD, F_CONV_SHARD, axis=2)
    small_w = dict(
        a_norm=(a_norm, m_a_norm, v_a_norm), a_w_gate_up=(a_w_gate_up, m_a_w_gate_up, v_a_w_gate_up),
        a_b_gate=(a_b_gate, m_a_b_gate, v_a_b_gate), a_gn=(a_gn, m_a_gn, v_a_gn), b_norm=(b_norm, m_b_norm, v_b_norm),
        b_conv=(b_conv, m_b_conv, v_b_conv), f_norm=(f_norm, m_f_norm, v_f_norm), f_conv=(f_conv, m_f_conv, v_f_conv),
        final_norm=(final_norm, m_final_norm, v_final_norm))
    two_d = lambda a: a.reshape(-1, a.shape[-1])
    updates = _adamw_small([tuple(two_d(a.reshape(w.shape)) for a in (small_g[name], w, m, v)) for name, (w, m, v) in small_w.items()])
    small = {}
    for (name, (w, _, _)), update in zip(small_w.items(), updates):
        small[name] = (small_g[name].reshape(w.shape),) + tuple(u.reshape(w.shape) for u in update)

    order = ["a_norm", "a_w_in", "a_w_gate_up", "a_b_gate", "a_gn", "a_w_out", "b_norm", "b_w_in", "b_conv", "b_w_out",
             "f_norm", "f_w_up", "f_conv", "f_w_down", "final_norm"]
    results = {**big, **small}
    outputs = [loss, dx.reshape(1, SEQ, D_MODEL)]
    for kind in range(4):
        outputs += [results[name][kind] for name in order]
    return tuple(outputs)
```

```python
import jax
import jax.numpy as jnp
from jax import lax
from jax.experimental import pallas as pl
from jax.experimental.pallas import tpu as pltpu
from jax.experimental.pallas import tpu_sc as plsc

F32 = jnp.float32
BF16 = jnp.bfloat16

N_DEV = 8
SEQ = 2048
D_MODEL = 1024
CHUNK = 64
N_CHUNKS = SEQ // CHUNK
RMS_EPS = 1e-6
GLA_HEADS = 4
KEY_DIM = 512
VALUE_DIM = 1024
HEAD_K = KEY_DIM // GLA_HEADS
HEAD_V = VALUE_DIM // GLA_HEADS
GATE_RANK = 16
GATE_PAD = 128
GATE_NORMALIZER = 16.0
PROJ_A = 2 * KEY_DIM + 2 * VALUE_DIM + GATE_RANK
PROJ_A_PAD = 2 * KEY_DIM + 2 * VALUE_DIM + GATE_PAD
A_SHARD = PROJ_A // N_DEV
B_SHARD = 3 * D_MODEL // N_DEV
D_FF = 2816
ADAM_LR = 0.001
ADAM_B1 = 0.9
ADAM_B2 = 0.999
ADAM_EPS = 1e-08
ADAM_WD = 0.01
ADAM_STEP = 10
MESH_AXES = ("x", "y", "c")

VMEM_LIMIT = 56 * 1024 * 1024
ROW_CHUNK = 256
HALO = 16


def _params(sem=None, vmem=VMEM_LIMIT):
    return pltpu.CompilerParams(dimension_semantics=sem, vmem_limit_bytes=vmem)


NN = ((1,), (0,))
NT = ((1,), (1,))
TN = ((0,), (0,))


def _matmul(name, a, a_spec, b, b_spec, dims, grid, out_shape, out_spec, k_blocks=None, a_block_cols=None, res=None,
            res_spec=None, transpose_out=False, norm=None, swap=()):
    has_res = res is not None
    n_swap = len(swap)

    def body(*refs):
        a_ref, b_ref = refs[0], refs[1]
        r_ref = refs[2] if has_res else None

        def product(lhs, rhs):
            return lax.dot_general(lhs.astype(BF16), rhs, (dims, ((), ())), preferred_element_type=F32)

        if k_blocks is None:
            v = product(a_ref[...], b_ref[...])
        else:
            v = None
            for k in range(k_blocks):
                lhs = a_ref[k] if a_block_cols is None else a_ref[:, k * a_block_cols:(k + 1) * a_block_cols]
                p = product(lhs, b_ref[k])
                v = p if v is None else v + p
        if transpose_out:
            v = v.T
        if has_res:
            v = v + r_ref[...]
        if norm is None:
            o_ref = refs[2 + has_res]
            o_ref[...] = v.astype(o_ref.dtype)
            return
        n_in = 5 + has_res
        x_ref, g_ref, dxi_ref = refs[2 + has_res:n_in]
        dx_ref, dx16_ref, dg_ref = refs[n_in + n_swap:n_in + n_swap + 3]
        if n_swap:
            copies = _pair_copies(refs[n_in:n_in + n_swap], refs[n_in + n_swap + 3:n_in + 2 * n_swap + 3], *refs[-2:])

            @pl.when(pl.program_id(0) == 0)
            def _():
                for send, _ in copies:
                    send.start()

            @pl.when(pl.program_id(0) == grid[0] - 1)
            def _():
                for send, arrival in copies:
                    arrival.wait_recv()
                    send.wait_send()

        dx, dg = _norm_bwd_rows(x_ref[...], g_ref[...], v)
        dx = dxi_ref[...] + dx
        dx_ref[...] = dx
        dx16_ref[...] = dx.astype(BF16)

        @pl.when(pl.program_id(0) == 0)
        def _():
            dg_ref[...] = dg

        @pl.when(pl.program_id(0) > 0)
        def _():
            dg_ref[...] += dg

    operands = [a, b] + ([res] if has_res else [])
    in_specs = [a_spec, b_spec] + ([res_spec] if has_res else [])
    semantics = ("parallel",) * len(grid)
    scratch = []
    if norm is not None:
        vec = _spec((1, D_MODEL), lambda i: (0, 0))
        any_space = pl.BlockSpec(memory_space=pl.ANY)
        operands += list(norm) + list(swap)
        in_specs += [out_spec, vec, out_spec] + [any_space] * n_swap
        out_shape = [_act(dtype=F32), _act(), jax.ShapeDtypeStruct((1, D_MODEL), F32)]
        out_shape += [jax.ShapeDtypeStruct((N_DEV // 2,) + p.shape[1:], p.dtype) for p in swap]
        out_spec = [out_spec, out_spec, vec] + [any_space] * n_swap
        semantics = ("arbitrary",)
        if n_swap:
            scratch = [pltpu.SemaphoreType.DMA((n_swap, N_DEV // 2))] * 2
    return pl.pallas_call(
        body, name=name, grid=grid, in_specs=in_specs, out_specs=out_spec, out_shape=out_shape, scratch_shapes=scratch,
        compiler_params=_params(semantics),
    )(*operands)


def _resident(shape):
    return pl.BlockSpec(shape, lambda *_: (0,) * len(shape), pipeline_mode=pl.Buffered(1))


TM = 512
N_TM = SEQ // TM
PA_TILE = 640
N_PA = PROJ_A_PAD // PA_TILE
OUT_TILE = 256


def _spec(shape, fn):
    return pl.BlockSpec(shape, fn)


def _act(shape=(SEQ, D_MODEL), dtype=BF16):
    return jax.ShapeDtypeStruct(shape, dtype)


def _norm_proj(name, x, gamma, w):
    blocks = w.ndim == 3
    n_out = w.shape[0] * w.shape[2] if blocks else w.shape[0]

    def body(x_ref, g_ref, w_ref, h_ref, o_ref):
        x = x_ref[...]
        h = (x * _rstd(x) * g_ref[...]).astype(BF16)
        h_ref[...] = h
        if blocks:
            n = w.shape[2]
            for j in range(w.shape[0]):
                o_ref[:, j * n:(j + 1) * n] = jnp.dot(h, w_ref[j], preferred_element_type=F32).astype(BF16)
        else:
            o_ref[...] = lax.dot_general(h, w_ref[...], (NT, ((), ())), preferred_element_type=F32).astype(BF16)

    row = _spec((TM, D_MODEL), lambda i: (i, 0))
    return pl.pallas_call(
        body, name=name, grid=(N_TM,), in_specs=[row, _resident((1, D_MODEL)), _resident(w.shape)],
        out_specs=[row, _spec((TM, n_out), lambda i: (i, 0))], out_shape=[_act(), _act((SEQ, n_out))],
        compiler_params=_params(("parallel",)),
    )(x, gamma, w)


N_TILE = 256


def _cols_matmul(name, a, w, dims, x=None):
    k = a.shape[1]
    n = w.shape[1] if dims == NN else w.shape[0]
    w_spec = _spec((k, N_TILE), lambda j: (0, j)) if dims == NN else _spec((N_TILE, k), lambda j: (j, 0))
    col = _spec((SEQ, N_TILE), lambda j: (0, j))
    return _matmul(name, a, _resident((SEQ, k)), w, w_spec, dims, (n // N_TILE,),
                   _act((SEQ, n), F32 if x is not None else BF16), col, res=x, res_spec=col if x is not None else None)


def _sum_blocks_nn(name, a_blocks, w_blocks, x=None, norm=None, swap=()):
    nb, _, n = a_blocks.shape
    row = _spec((TM, D_MODEL), lambda i: (i, 0))
    return _matmul(name, a_blocks, _spec((nb, TM, n), lambda i: (0, i, 0)), w_blocks, _resident((nb, n, D_MODEL)),
                   NN, (N_TM,), _act(dtype=F32), row, k_blocks=nb, res=x, res_spec=row if x is not None else None, norm=norm, swap=swap)


def _sum_cols_nt(name, d, w_blocks, norm=None, swap=()):
    nb, _, n = w_blocks.shape
    return _matmul(name, d, _spec((TM, nb * n), lambda i: (i, 0)), w_blocks, _resident((nb, D_MODEL, n)), NT,
                   (N_TM,), _act(dtype=F32), _spec((TM, D_MODEL), lambda i: (i, 0)), k_blocks=nb, a_block_cols=n, norm=norm, swap=swap)


def _wide_nn(name, d, wt, x=None, norm=None, swap=()):
    n = wt.shape[0]
    row = _spec((TM, D_MODEL), lambda i: (i, 0))
    return _matmul(name, d, _spec((TM, n), lambda i: (i, 0)), wt, _resident((n, D_MODEL)), NN, (N_TM,),
                   _act(dtype=F32), row, res=x, res_spec=row if x is not None else None, norm=norm, swap=swap)


def _wgrad_halves_tn(name, d, n_tile, h):
    _, _, n = d.shape
    return _matmul(name, d, _spec((None, SEQ, n_tile), lambda p, j: (p, 0, j)), h, _resident((SEQ, D_MODEL)), TN,
                   (2, n // n_tile), _act((2, n, D_MODEL)), _spec((None, n_tile, D_MODEL), lambda p, j: (p, j, 0)))


def _wgrad_cols_tn(name, d, n_tile, h):
    n = d.shape[1]
    return _matmul(name, d, _spec((SEQ, n_tile), lambda j: (0, j)), h, _resident((SEQ, D_MODEL)), TN,
                   (n // n_tile,), _act((n, D_MODEL)), _spec((n_tile, D_MODEL), lambda j: (j, 0)))


def _wgrad_cols_transposed_tn(name, h, d, n_tile):
    nb = d.shape[1] // n_tile
    return _matmul(name, d, _spec((SEQ, n_tile), lambda j: (0, j)), h, _resident((SEQ, D_MODEL)), TN, (nb,),
                   _act((nb, D_MODEL, n_tile)), _spec((None, D_MODEL, n_tile), lambda j: (j, 0, 0)), transpose_out=True)


NORM_ROWS = 512


def _rstd(x):
    return lax.rsqrt(jnp.mean(x * x, axis=-1, keepdims=True) + RMS_EPS)


def _norm_bwd_rows(x, gamma, dh):
    r = _rstd(x)
    xh = x * r
    dxh = dh * gamma
    dx = r * (dxh - xh * jnp.mean(dxh * xh, axis=-1, keepdims=True))
    return dx, jnp.sum(dh * xh, axis=0, keepdims=True)


def _down_loss_head(a, w_down, x_in, gamma, target):
    def body(a_ref, w_ref, x_ref, g_ref, t_ref, loss_ref, dx_ref, dx16_ref, dg_ref):
        x = x_ref[...] + jnp.dot(a_ref[...], w_ref[...], preferred_element_type=F32)
        gamma = g_ref[...]
        err = x * _rstd(x) * gamma - t_ref[...]
        dy = err * (1.0 / D_MODEL)
        dx, dg = _norm_bwd_rows(x, gamma, dy)
        dx_ref[...] = dx
        dx16_ref[...] = dx.astype(BF16)
        part = 0.5 * jnp.sum(jnp.sum(err * err, axis=-1, keepdims=True) * (1.0 / D_MODEL), axis=0, keepdims=True)
        part = jnp.broadcast_to(part, loss_ref.shape)

        @pl.when(pl.program_id(0) == 0)
        def _():
            dg_ref[...] = dg
            loss_ref[...] = part

        @pl.when(pl.program_id(0) > 0)
        def _():
            dg_ref[...] += dg
            loss_ref[...] += part

    row = _spec((TM, D_MODEL), lambda i: (i, 0))
    vec = _spec((1, D_MODEL), lambda i: (0, 0))
    return pl.pallas_call(
        body, name="ffn1_down_loss_head", grid=(N_TM,),
        in_specs=[_spec((TM, D_FF), lambda i: (i, 0)), _resident((D_FF, D_MODEL)), row, vec, row],
        out_specs=[_spec((1, 128), lambda i: (0, 0)), row, row, vec],
        out_shape=[jax.ShapeDtypeStruct((1, 128), F32), _act(dtype=F32), _act(), jax.ShapeDtypeStruct((1, D_MODEL), F32)],
        compiler_params=_params(("arbitrary",)),
    )(a, w_down, x_in, gamma, target)


def _sigmoid(x):
    return 1.0 / (1.0 + jnp.exp(-x))


def _rows(ref, c):
    return ref[pl.ds(pl.multiple_of(c * ROW_CHUNK, ROW_CHUNK), ROW_CHUNK), :].astype(F32)


def _rows_before(ref, c):
    start = pl.multiple_of(jnp.maximum(c * ROW_CHUNK - HALO, 0), HALO)
    rows = ref[pl.ds(start, HALO), :].astype(F32)
    return jnp.where(c > 0, rows, 0.0)


def _rows_after(ref, c, n_chunks):
    start = pl.multiple_of(jnp.minimum((c + 1) * ROW_CHUNK, SEQ - HALO), HALO)
    rows = ref[pl.ds(start, HALO), :].astype(F32)
    return jnp.where(c < n_chunks - 1, rows, 0.0)


def _shift_down(z, before, n):
    return pltpu.roll(jnp.concatenate([before, z], axis=0), n, 0)[before.shape[0]:]


def _shift_up(z, after, n):
    rows = z.shape[0]
    return pltpu.roll(jnp.concatenate([z, after], axis=0), rows + HALO - n, 0)[:rows]


def _conv_rows(z, before, w):
    z1 = _shift_down(z, before, 1)
    z2 = _shift_down(z, before, 2)
    return w[2:3, :] * z + w[1:2, :] * z1 + w[0:1, :] * z2, z1, z2


def _conv_t_rows(dy, after, w):
    return w[2:3, :] * dy + w[1:2, :] * _shift_up(dy, after, 1) + w[0:1, :] * _shift_up(dy, after, 2)


N_ROW_CHUNKS = SEQ // ROW_CHUNK


FF_COLS = 256
N_FF_COLS = D_FF // FF_COLS


def _ffn_mid_bwd(name, gu, conv_w, da):
    def body(gu_ref, w_ref, da_ref, dgu_ref, dw_ref, dgc_ref):
        w = w_ref[...]

        def first(c, acc):
            g = _rows(gu_ref.at[0], c)
            u = _rows(gu_ref.at[1], c)
            d = _rows(da_ref, c)
            gc, g1, g2 = _conv_rows(g, _rows_before(gu_ref.at[0], c), w)
            sg = _sigmoid(gc)
            rows = pl.ds(pl.multiple_of(c * ROW_CHUNK, ROW_CHUNK), ROW_CHUNK)
            dgu_ref[1, rows, :] = (d * gc * sg).astype(BF16)
            dgc = d * u * (sg * (1.0 + gc * (1.0 - sg)))
            dgc_ref[rows, :] = dgc
            return (acc[0] + jnp.sum(dgc * g2, axis=0, keepdims=True), acc[1] + jnp.sum(dgc * g1, axis=0, keepdims=True),
                    acc[2] + jnp.sum(dgc * g, axis=0, keepdims=True))

        zero = jnp.zeros((1, FF_COLS), F32)
        acc = lax.fori_loop(0, N_ROW_CHUNKS, first, (zero, zero, zero))
        for r in range(3):
            dw_ref[r:r + 1, :] = acc[r]

        def second(c, carry):
            dgc = _rows(dgc_ref, c)
            dg = _conv_t_rows(dgc, _rows_after(dgc_ref, c, N_ROW_CHUNKS), w)
            dgu_ref[0, pl.ds(pl.multiple_of(c * ROW_CHUNK, ROW_CHUNK), ROW_CHUNK), :] = dg.astype(BF16)
            return carry

        lax.fori_loop(0, N_ROW_CHUNKS, second, 0)

    pair = _spec((2, SEQ, FF_COLS), lambda j: (0, 0, j))
    wspec = _spec((3, FF_COLS), lambda j: (0, j))
    return pl.pallas_call(
        body, name=name, grid=(N_FF_COLS,), in_specs=[pair, wspec, _spec((SEQ, FF_COLS), lambda j: (0, j))],
        out_specs=[pair, wspec], out_shape=[_act((2, SEQ, D_FF)), jax.ShapeDtypeStruct((3, D_FF), F32)],
        scratch_shapes=[pltpu.VMEM((SEQ, FF_COLS), F32)],
        compiler_params=_params(("parallel",)),
    )(gu, conv_w, da)


SC_COLS = 256
N_SC = D_MODEL // SC_COLS


def _sc_specs():
    return [_spec((SEQ, SC_COLS), lambda j, part=part: (0, part * N_SC + j)) for part in range(3)]


def _sc_mid_fwd(p, conv_w):
    def body(b_ref, c_ref, h_ref, w_ref, y_ref):
        w = w_ref[...]

        def chunk(c, carry):
            z = _rows(c_ref, c) * _rows(h_ref, c)
            before = _rows_before(c_ref, c) * _rows_before(h_ref, c)
            zc, _, _ = _conv_rows(z, before, w)
            y_ref[pl.ds(pl.multiple_of(c * ROW_CHUNK, ROW_CHUNK), ROW_CHUNK), :] = (_rows(b_ref, c) * zc).astype(BF16)
            return carry

        lax.fori_loop(0, N_ROW_CHUNKS, chunk, 0)

    col = _spec((SEQ, SC_COLS), lambda j: (0, j))
    return pl.pallas_call(
        body, name="sc_mid_fwd", grid=(N_SC,), in_specs=_sc_specs() + [_spec((3, SC_COLS), lambda j: (0, j))], out_specs=col,
        out_shape=jax.ShapeDtypeStruct((SEQ, D_MODEL), BF16), compiler_params=_params(("parallel",)),
    )(p, p, p, conv_w)


def _sc_mid_bwd(p, conv_w, dy):
    def body(b_ref, c_ref, h_ref, w_ref, dy_ref, db_ref, dc_ref, dh_ref, dw_ref, dzc_ref):
        w = w_ref[...]

        def first(c, acc):
            z = _rows(c_ref, c) * _rows(h_ref, c)
            before = _rows_before(c_ref, c) * _rows_before(h_ref, c)
            zc, z1, z2 = _conv_rows(z, before, w)
            d = _rows(dy_ref, c)
            rows = pl.ds(pl.multiple_of(c * ROW_CHUNK, ROW_CHUNK), ROW_CHUNK)
            db_ref[rows, :] = (d * zc).astype(BF16)
            dzc = d * _rows(b_ref, c)
            dzc_ref[rows, :] = dzc
            return (acc[0] + jnp.sum(dzc * z2, axis=0, keepdims=True), acc[1] + jnp.sum(dzc * z1, axis=0, keepdims=True),
                    acc[2] + jnp.sum(dzc * z, axis=0, keepdims=True))

        zero = jnp.zeros((1, SC_COLS), F32)
        acc = lax.fori_loop(0, N_ROW_CHUNKS, first, (zero, zero, zero))
        for r in range(3):
            dw_ref[r:r + 1, :] = acc[r]

        def second(c, carry):
            dz = _conv_t_rows(_rows(dzc_ref, c), _rows_after(dzc_ref, c, N_ROW_CHUNKS), w)
            rows = pl.ds(pl.multiple_of(c * ROW_CHUNK, ROW_CHUNK), ROW_CHUNK)
            dc_ref[rows, :] = (dz * _rows(h_ref, c)).astype(BF16)
            dh_ref[rows, :] = (dz * _rows(c_ref, c)).astype(BF16)
            return carry

        lax.fori_loop(0, N_ROW_CHUNKS, second, 0)

    col = _spec((SEQ, SC_COLS), lambda j: (0, j))
    wspec = _spec((3, SC_COLS), lambda j: (0, j))
    act = jax.ShapeDtypeStruct((SEQ, D_MODEL), BF16)
    return pl.pallas_call(
        body, name="sc_mid_bwd", grid=(N_SC,), in_specs=_sc_specs() + [wspec, col], out_specs=[col, col, col, wspec],
        out_shape=[act, act, act, jax.ShapeDtypeStruct((3, D_MODEL), F32)],
        scratch_shapes=[pltpu.VMEM((SEQ, SC_COLS), F32)], compiler_params=_params(("parallel",)),
    )(p, p, p, conv_w, dy)


GLA_GROUP = 4
GLA_ROWS = GLA_GROUP * CHUNK
N_GROUPS = N_CHUNKS // GLA_GROUP
Q0, K0, V0, R0, G0 = 0, KEY_DIM, 2 * KEY_DIM, 2 * KEY_DIM + VALUE_DIM, 2 * KEY_DIM + 2 * VALUE_DIM


def _tri(strict):
    r = lax.broadcasted_iota(jnp.int32, (CHUNK, CHUNK), 0)
    c = lax.broadcasted_iota(jnp.int32, (CHUNK, CHUNK), 1)
    return jnp.where(c < r if strict else c <= r, 1.0, 0.0).astype(F32)


def _cumsum_rows(tri, x):
    tri = tri.astype(BF16)
    total = None
    for _ in range(3):
        term = x.astype(BF16)
        x = x - term.astype(F32)
        product = jnp.dot(tri, term, preferred_element_type=F32)
        total = product if total is None else total + product
    return total


def _gate_logits(gl, wgu, b_gate):
    return jnp.dot(gl, wgu, preferred_element_type=F32) + b_gate


def _log_decay(logits):
    return (jnp.minimum(logits, 0.0) - jnp.log(1.0 + jnp.exp(-jnp.abs(logits)))) * (1.0 / GATE_NORMALIZER)


def _head(x, h, width):
    return x[:, h * width:(h + 1) * width]


def _gla_fwd(proj, wgu, b_gate, gn):
    def body(p_ref, wgu_ref, b_ref, gn_ref, o_ref, og_ref, st_ref, state):
        @pl.when(pl.program_id(0) == 0)
        def _():
            state[...] = jnp.zeros_like(state)

        tri = _tri(False)
        la = _log_decay(_gate_logits(p_ref[:, G0:G0 + GATE_PAD], wgu_ref[...], b_ref[...]))
        decays = []
        for c in range(GLA_GROUP):
            rows = slice(c * CHUNK, (c + 1) * CHUNK)
            cum = _cumsum_rows(tri, la[rows])
            tot = cum[CHUNK - 1:CHUNK, :]
            kd = (p_ref[rows, K0:K0 + KEY_DIM].astype(F32) * jnp.exp(tot - cum)).astype(BF16)
            decays.append(jnp.exp(tot))
            v = p_ref[rows, V0:V0 + VALUE_DIM]
            for h in range(GLA_HEADS):
                st_ref[c, h] = lax.dot_general(
                    _head(v, h, HEAD_V), _head(kd, h, HEAD_K), (TN, ((), ())), preferred_element_type=F32)
        for c in range(GLA_GROUP):
            for h in range(GLA_HEADS):
                s = state[h] * _head(decays[c], h, HEAD_K) + st_ref[c, h]
                state[h] = s
                st_ref[c, h] = s
        for c in range(GLA_GROUP):
            rows = slice(c * CHUNK, (c + 1) * CHUNK)
            q = (p_ref[rows, Q0:Q0 + KEY_DIM].astype(F32) * (HEAD_K ** -0.5)).astype(BF16)
            for h in range(GLA_HEADS):
                o_ref[rows, h * HEAD_V:(h + 1) * HEAD_V] = lax.dot_general(
                    _head(q, h, HEAD_K), st_ref[c, h].astype(BF16), (NT, ((), ())), preferred_element_type=F32)
        r = p_ref[:, R0:R0 + VALUE_DIM].astype(F32)
        gate = r * _sigmoid(r) * gn_ref[...]
        for h in range(GLA_HEADS):
            cols = slice(h * HEAD_V, (h + 1) * HEAD_V)
            o = o_ref[:, cols]
            og_ref[:, cols] = (o * _rstd(o) * gate[:, cols]).astype(BF16)

    rows = _spec((GLA_ROWS, VALUE_DIM), lambda i: (i, 0))
    const = lambda shape: _spec(shape, lambda i: (0,) * len(shape))
    return pl.pallas_call(
        body, name="gla_fwd", grid=(N_GROUPS,),
        in_specs=[_spec((GLA_ROWS, PROJ_A_PAD), lambda i: (i, 0)), const((GATE_PAD, KEY_DIM)), const((1, KEY_DIM)),
                  const((1, VALUE_DIM))],
        out_specs=[rows, rows, _spec((GLA_GROUP, GLA_HEADS, HEAD_V, HEAD_K), lambda i: (i, 0, 0, 0))],
        out_shape=[jax.ShapeDtypeStruct((SEQ, VALUE_DIM), F32), jax.ShapeDtypeStruct((SEQ, VALUE_DIM), BF16),
                   jax.ShapeDtypeStruct((N_CHUNKS, GLA_HEADS, HEAD_V, HEAD_K), F32)],
        scratch_shapes=[pltpu.VMEM((GLA_HEADS, HEAD_V, HEAD_K), F32)], compiler_params=_params(("arbitrary",)),
    )(proj, wgu, b_gate, gn)


def _gla_bwd(proj, wgu, b_gate, gn, o, states, dog):
    last = N_GROUPS - 1

    def body(p_ref, wgu_ref, b_ref, gn_ref, o_ref, st_ref, stp_ref, dog_ref, dp_ref, dwgu_ref, db_ref, dgn_ref, carry, do_buf,
             g_buf):
        step = pl.program_id(0)

        @pl.when(step == 0)
        def _():
            carry[...] = jnp.zeros_like(carry)

        r = p_ref[:, R0:R0 + VALUE_DIM].astype(F32)
        sr = _sigmoid(r)
        silu = r * sr
        gn_row = gn_ref[...]
        dog_rows = dog_ref[...].astype(F32)
        dn = dog_rows * silu
        dgn_cols = []
        for h in range(GLA_HEADS):
            cols = slice(h * HEAD_V, (h + 1) * HEAD_V)
            oh = o_ref[:, cols]
            rs = _rstd(oh)
            ohat = oh * rs
            dn_h = dn[:, cols]
            dgn_cols.append(jnp.sum(dn_h * ohat, axis=0, keepdims=True))
            dohat = dn_h * gn_row[:, cols]
            do_buf[:, cols] = rs * (dohat - ohat * jnp.mean(dohat * ohat, axis=-1, keepdims=True))
            n_h = ohat * gn_row[:, cols]
            dp_ref[:, R0 + h * HEAD_V:R0 + (h + 1) * HEAD_V] = (
                dog_rows[:, cols] * n_h * (sr[:, cols] * (1.0 + r[:, cols] * (1.0 - sr[:, cols])))).astype(BF16)
        dgn = jnp.concatenate(dgn_cols, axis=1)

        tri = _tri(False)
        tri_strict = _tri(True)
        gl = p_ref[:, G0:G0 + GATE_PAD]
        logits = _gate_logits(gl, wgu_ref[...], b_ref[...])
        la = _log_decay(logits)
        fades, kds, decays = [], [], []
        for c in range(GLA_GROUP):
            rows = slice(c * CHUNK, (c + 1) * CHUNK)
            cum = _cumsum_rows(tri, la[rows])
            tot = cum[CHUNK - 1:CHUNK, :]
            fades.append(jnp.exp(tot - cum))
            kds.append(p_ref[rows, K0:K0 + KEY_DIM].astype(F32) * fades[c])
            decays.append(jnp.exp(tot))
            q = (p_ref[rows, Q0:Q0 + KEY_DIM].astype(F32) * (HEAD_K ** -0.5)).astype(BF16)
            do = do_buf[rows, :].astype(BF16)
            for h in range(GLA_HEADS):
                do_h = _head(do, h, HEAD_V)
                dq = jnp.dot(do_h, st_ref[c, h].astype(BF16), preferred_element_type=F32) * (HEAD_K ** -0.5)
                dp_ref[rows, Q0 + h * HEAD_K:Q0 + (h + 1) * HEAD_K] = dq.astype(BF16)
                g_buf[c, h] = lax.dot_general(do_h, _head(q, h, HEAD_K), (TN, ((), ())), preferred_element_type=F32)
        for c in reversed(range(GLA_GROUP)):
            for h in range(GLA_HEADS):
                g = carry[h] + g_buf[c, h]
                g_buf[c, h] = g
                carry[h] = g * _head(decays[c], h, HEAD_K)
        dlogit_rows = []
        for c in range(GLA_GROUP):
            rows = slice(c * CHUNK, (c + 1) * CHUNK)
            v = p_ref[rows, V0:V0 + VALUE_DIM]
            kd = kds[c].astype(BF16)
            dkd_cols, ddecay_cols = [], []
            for h in range(GLA_HEADS):
                g = g_buf[c, h]
                g16 = g.astype(BF16)
                dkd_cols.append(jnp.dot(_head(v, h, HEAD_V), g16, preferred_element_type=F32))
                dv = lax.dot_general(_head(kd, h, HEAD_K), g16, (NT, ((), ())), preferred_element_type=F32)
                dp_ref[rows, V0 + h * HEAD_V:V0 + (h + 1) * HEAD_V] = dv.astype(BF16)
                if c > 0:
                    s_prev = st_ref[c - 1, h]
                else:
                    s_prev = jnp.where(step < last, stp_ref[0, h], 0.0)
                ddecay_cols.append(jnp.sum(g * s_prev, axis=0, keepdims=True))
            dkd = jnp.concatenate(dkd_cols, axis=1)
            ddecay = jnp.concatenate(ddecay_cols, axis=1)
            dp_ref[rows, K0:K0 + KEY_DIM] = (dkd * fades[c]).astype(BF16)
            e = dkd * kds[c]
            dla = ddecay * decays[c] + _cumsum_rows(tri_strict, e)
            dlogit_rows.append(dla * (1.0 / GATE_NORMALIZER) * (1.0 - _sigmoid(logits[rows])))
        dlogit = jnp.concatenate(dlogit_rows, axis=0)
        dlogit16 = dlogit.astype(BF16)
        dp_ref[:, G0:G0 + GATE_PAD] = lax.dot_general(
            dlogit16, wgu_ref[...], (NT, ((), ())), preferred_element_type=F32).astype(BF16)
        dwgu = lax.dot_general(gl, dlogit16, (TN, ((), ())), preferred_element_type=F32)
        db = jnp.sum(dlogit, axis=0, keepdims=True)

        @pl.when(step == 0)
        def _():
            dwgu_ref[...] = dwgu
            db_ref[...] = db
            dgn_ref[...] = dgn

        @pl.when(step > 0)
        def _():
            dwgu_ref[...] += dwgu
            db_ref[...] += db
            dgn_ref[...] += dgn

    rev = lambda i: (last - i, 0)
    rows = _spec((GLA_ROWS, VALUE_DIM), rev)
    const = lambda shape: _spec(shape, lambda i: (0,) * len(shape))
    st_shape = (GLA_HEADS, HEAD_V, HEAD_K)
    return pl.pallas_call(
        body, name="gla_bwd", grid=(N_GROUPS,),
        in_specs=[_spec((GLA_ROWS, PROJ_A_PAD), rev), const((GATE_PAD, KEY_DIM)), const((1, KEY_DIM)), const((1, VALUE_DIM)),
                  rows, _spec((GLA_GROUP,) + st_shape, lambda i: (last - i, 0, 0, 0)),
                  _spec((1,) + st_shape, lambda i: (jnp.maximum((last - i) * GLA_GROUP - 1, 0), 0, 0, 0)), rows],
        out_specs=[_spec((GLA_ROWS, PROJ_A_PAD), rev), const((GATE_PAD, KEY_DIM)), const((1, KEY_DIM)), const((1, VALUE_DIM))],
        out_shape=[jax.ShapeDtypeStruct((SEQ, PROJ_A_PAD), BF16), jax.ShapeDtypeStruct((GATE_PAD, KEY_DIM), F32),
                   jax.ShapeDtypeStruct((1, KEY_DIM), F32), jax.ShapeDtypeStruct((1, VALUE_DIM), F32)],
        scratch_shapes=[pltpu.VMEM(st_shape, F32), pltpu.VMEM((GLA_ROWS, VALUE_DIM), F32), pltpu.VMEM((GLA_GROUP,) + st_shape, F32)],
        compiler_params=_params(("arbitrary",)),
    )(proj, wgu, b_gate, gn, o, states, states, dog)


WGRAD_FF_TILE = D_FF // 2


CARRY_ROWS = 8
UP_ROWS = 512


def _ffn_up_mid(name, x, gamma, w_up_t, conv_w):
    def body(x_ref, g_ref, w_ref, c_ref, h_ref, gu_ref, a_ref, carry):
        @pl.when(pl.program_id(0) == 0)
        def _():
            carry[...] = jnp.zeros_like(carry)

        x_tile = x_ref[...]
        h_tile = (x_tile * _rstd(x_tile) * g_ref[...]).astype(BF16)
        h_ref[...] = h_tile
        for k in range(N_FF_COLS):
            cols = slice(k * FF_COLS, (k + 1) * FF_COLS)
            g, u = (lax.dot_general(h_tile, w_ref[p, cols, :], (NT, ((), ())), preferred_element_type=F32).astype(BF16)
                    for p in range(2))
            gu_ref[0, :, cols] = g
            gu_ref[1, :, cols] = u
            g = g.astype(F32)
            w = c_ref[:, cols]
            before = carry[:, cols]
            gc = w[2:3, :] * g + w[1:2, :] * _shift_down(g, before, 1) + w[0:1, :] * _shift_down(g, before, 2)
            a_ref[:, cols] = (gc * _sigmoid(gc) * u.astype(F32)).astype(BF16)
            carry[:, cols] = g[UP_ROWS - CARRY_ROWS:, :]

    row = _spec((UP_ROWS, D_MODEL), lambda i: (i, 0))
    return pl.pallas_call(
        body, name=name, grid=(SEQ // UP_ROWS,),
        in_specs=[row, _resident((1, D_MODEL)), _resident((2, D_FF, D_MODEL)), _resident((3, D_FF))],
        out_specs=[row, _spec((2, UP_ROWS, D_FF), lambda i: (0, i, 0)), _spec((UP_ROWS, D_FF), lambda i: (i, 0))],
        out_shape=[_act(), _act((2, SEQ, D_FF)), _act((SEQ, D_FF))], scratch_shapes=[pltpu.VMEM((CARRY_ROWS, D_FF), F32)],
        compiler_params=_params(("arbitrary",)),
    )(x, gamma, w_up_t, conv_w)


def _ffn_fwd(tag, x, gamma, w_up_t, conv_w, w_down):
    h, gu, a = _ffn_up_mid(f"ffn{tag}_up_mid", x, gamma, w_up_t, conv_w)
    return _cols_matmul(f"ffn{tag}_down", a, w_down, NN, x), (h, gu, a)


def _owner_blocks(d, rows=None):
    if rows is not None:
        d = d[:rows]
    return d.reshape((N_DEV, -1) + d.shape[-1:])


def _ffn_bwd(tag, x, gamma, w_up_t, conv_w, w_down, saved, dx, dx16, swap):
    h, gu, a = saved
    da = _cols_matmul(f"ffn{tag}_da", dx16, w_down, NT)
    d_w_down = _owner_blocks(_wgrad_cols_tn(f"ffn{tag}_dwdown", a, WGRAD_FF_TILE, dx16))
    dgu, d_conv = _ffn_mid_bwd(f"ffn{tag}_mid_bwd", gu, conv_w, da)
    d_w_up_t = _owner_blocks(_wgrad_halves_tn(f"ffn{tag}_dwup", dgu, WGRAD_FF_TILE, h))
    parts = (d_w_up_t, d_w_down)
    dx, dx16, d_gamma, *received = _sum_blocks_nn(
        f"ffn{tag}_dh", dgu, w_up_t, norm=(x, gamma, dx), swap=parts if swap else ())
    return dx, dx16, d_gamma, d_conv, parts, received


def _local_step(x, target, w, fetch=None, emit=None):
    if fetch is None:
        local = dict(a=(w.get("a_w_in"), w.get("a_w_out")), b=(w.get("b_w_in"), w.get("b_w_out")))
        for layer in range(2):
            local[f"f{layer}"] = (w["f_w_up"][layer], w["f_w_down"][layer]) if "f_w_up" in w else None
        fetch = lambda group, after: local[group]
    swap = emit is not None
    if emit is None:
        emit = lambda group, parts, received, dx: dx
    f_norm = (w["f_norm"][0:1], w["f_norm"][1:2])

    x0 = x
    a_w_in, a_w_out = fetch("a", x0)
    h0, proj = _norm_proj("a_in", x0, w["a_norm"], a_w_in)
    o, og, states = _gla_fwd(proj, w["a_w_gate_up"], w["a_b_gate"], w["a_gn"])
    x1 = _cols_matmul("a_out", og, a_w_out, NN, x0)
    up0, down0 = fetch("f0", x1)
    x2, ffn0 = _ffn_fwd(0, x1, f_norm[0], up0, w["f_conv"][0], down0)
    b_w_in, b_w_out = fetch("b", x2)
    h2, p = _norm_proj("b_in", x2, w["b_norm"], b_w_in)
    y = _sc_mid_fwd(p, w["b_conv"])
    x3 = _cols_matmul("b_out", y, b_w_out, NN, x2)
    up1, down1 = fetch("f1", x3)
    ffn1 = _ffn_up_mid("ffn1_up_mid", x3, f_norm[1], up1, w["f_conv"][1])
    loss, dx, dx16, d_final_norm = _down_loss_head(ffn1[2], down1, x3, w["final_norm"], target)

    dx, dx16, d_f_norm1, d_fconv1, parts_f1, got = _ffn_bwd(
        1, x3, f_norm[1], up1, w["f_conv"][1], down1, ffn1, dx, dx16, swap)
    dx16 = emit("f1", parts_f1, got, dx16)

    dy = _cols_matmul("b_dy", dx16, b_w_out, NT)
    d_b_w_out = _owner_blocks(_wgrad_cols_tn("b_dwout", y, OUT_TILE, dx16))
    db, dc, dhh, d_b_conv = _sc_mid_bwd(p, w["b_conv"], dy)
    dp = jnp.concatenate([db, dc, dhh], axis=1)
    parts_b = (_wgrad_cols_transposed_tn("b_dwin", h2, dp, B_SHARD), d_b_w_out)
    dx, dx16, d_b_norm, *got = _sum_cols_nt("b_dh", dp, b_w_in, norm=(x2, w["b_norm"], dx), swap=parts_b if swap else ())
    dx16 = emit("b", parts_b, got, dx16)

    dx, dx16, d_f_norm0, d_fconv0, parts_f0, got = _ffn_bwd(
        0, x1, f_norm[0], up0, w["f_conv"][0], down0, ffn0, dx, dx16, swap)
    dx16 = emit("f0", parts_f0, got, dx16)

    dog = _cols_matmul("a_dog", dx16, a_w_out, NT)
    d_a_w_out = _owner_blocks(_wgrad_cols_tn("a_dwout", og, OUT_TILE, dx16))
    dproj, d_wgu, d_b_gate, d_gn = _gla_bwd(proj, w["a_w_gate_up"], w["a_b_gate"], w["a_gn"], o, states, dog)
    parts_a = (_owner_blocks(_wgrad_cols_tn("a_dwin", dproj, PA_TILE, h0), PROJ_A), d_a_w_out)
    dx, _, d_a_norm, *got = _wide_nn("a_dh", dproj, a_w_in, norm=(x0, w["a_norm"], dx), swap=parts_a if swap else ())
    emit("a", parts_a, got, dx)

    grads = dict(
        a_norm=d_a_norm, a_w_in=parts_a[0], a_w_gate_up=d_wgu, a_b_gate=d_b_gate, a_gn=d_gn, a_w_out=parts_a[1],
        b_norm=d_b_norm, b_w_in=parts_b[0], b_conv=d_b_conv, b_w_out=parts_b[1],
        f_norm=(d_f_norm0, d_f_norm1), f_w_up=(parts_f0[0], parts_f1[0]), f_conv=(d_fconv0, d_fconv1),
        f_w_down=(parts_f0[1], parts_f1[1]), final_norm=d_final_norm)
    grads["loss"] = loss
    return dx, grads


MESH_ID = pl.DeviceIdType.MESH
ANY = pl.BlockSpec(memory_space=pl.ANY)
N_PEERS = N_DEV - 1


def _position():
    return lax.axis_index("x"), lax.axis_index("y"), lax.axis_index("c")


def _slot(px, py, pc):
    return 4 * px + 2 * py + pc


GATHER_COPIES = 8
HALF_ROWS = 16


def _gather_copies(src, out, send_sems, recv_sems, local_sems):
    n = len(src)
    to_sibling, to_x, to_y, x_on_to_y, y_on_to_x, x_to_sibling, y_to_sibling, diagonal_to_sibling = range(GATHER_COPIES)
    x, y, c = _position()
    me, sibling = (x, y, c), (x, y, 1 - c)
    x_side, y_side, diagonal = (1 - x, y), (x, 1 - y), (1 - x, 1 - y)

    def rows_of(t, half):
        rows = src[t].shape[0]
        half_rows = rows // 2 // HALF_ROWS * HALF_ROWS
        return (pl.ds(0, rows), pl.ds(0, half_rows), pl.ds(half_rows, rows - half_rows))[half]

    def copy(t, j, block, to, half=0, from_input=False):
        dst = out[t].at[_slot(*block), rows_of(t, half)]
        return pltpu.make_async_remote_copy(
            src_ref=src[t] if from_input else dst, dst_ref=dst, send_sem=send_sems.at[GATHER_COPIES * t + j],
            recv_sem=recv_sems.at[GATHER_COPIES * t + j], device_id=to, device_id_type=MESH_ID)

    mine = [pltpu.make_async_copy(src[t], out[t].at[_slot(*me)], local_sems.at[t]) for t in range(n)]
    for cp in mine:
        cp.start()
    sent = []

    def start(cp):
        cp.start()
        sent.append(cp)

    for t in range(n):
        start(copy(t, to_sibling, me, sibling, from_input=True))
        start(copy(t, to_x, me, (*x_side, c), from_input=True))
        start(copy(t, to_y, me, (*y_side, c), from_input=True))
    for t in range(n):
        copy(t, to_x, (*x_side, c), me).wait_recv()
        start(copy(t, x_on_to_y, (*x_side, c), (*y_side, c), half=1))
        start(copy(t, x_to_sibling, (*x_side, c), sibling))
        copy(t, to_y, (*y_side, c), me).wait_recv()
        start(copy(t, y_on_to_x, (*y_side, c), (*x_side, c), half=2))
        start(copy(t, y_to_sibling, (*y_side, c), sibling))
    for t in range(n):
        copy(t, x_on_to_y, (*diagonal, c), me, half=1).wait_recv()
        copy(t, y_on_to_x, (*diagonal, c), me, half=2).wait_recv()
        start(copy(t, diagonal_to_sibling, (*diagonal, c), sibling))
    for t in range(n):
        copy(t, to_sibling, sibling, me).wait_recv()
        for j, chip in ((x_to_sibling, x_side), (y_to_sibling, y_side), (diagonal_to_sibling, diagonal)):
            copy(t, j, (*chip, 1 - c), me).wait_recv()
    for cp in sent:
        cp.wait_send()
    for cp in mine:
        cp.wait()


def _all_gather(name, shards):
    n = len(shards)

    def body(*refs):
        _gather_copies(refs[:n], refs[n:2 * n], *refs[2 * n:])

    sems = pltpu.SemaphoreType.DMA((GATHER_COPIES * n,))
    return pl.pallas_call(
        body, name=name, in_specs=[ANY] * n, out_specs=[ANY] * n,
        out_shape=[jax.ShapeDtypeStruct((N_DEV,) + s.shape, s.dtype) for s in shards],
        scratch_shapes=[sems, sems, pltpu.SemaphoreType.DMA((n,))],
    )(*shards)


SIBLING_AND_NEIGHBOURS = (1, 2, 4)
SAME_CORE = (2, 4, 6)


def _flip(x, y, c, k):
    return x ^ (k >> 2), y ^ ((k >> 1) & 1), c ^ (k & 1)


N_CHIPS = N_DEV // 2


def _chip(px, py):
    return 2 * px + py


def _pair_copies(parts, received, send_sems, recv_sems):
    x, y, c = lax.axis_index("x"), lax.axis_index("y"), lax.axis_index("c")
    sibling = (x, y, 1 - c)
    copies = []
    for t in range(len(parts)):
        for q in range(N_DEV // 2):
            send = pltpu.make_async_remote_copy(
                src_ref=parts[t].at[2 * q + 1 - c], dst_ref=received[t].at[q], send_sem=send_sems.at[t, q],
                recv_sem=recv_sems.at[t, q], device_id=sibling, device_id_type=pl.DeviceIdType.MESH)
            landed = received[t].at[q]
            arrival = pltpu.make_async_remote_copy(
                src_ref=landed, dst_ref=landed, send_sem=send_sems.at[t, q], recv_sem=recv_sems.at[t, q],
                device_id=sibling, device_id_type=pl.DeviceIdType.MESH)
            copies.append((send, arrival))
    return copies


PAIR_ROWS = 1024


def _pair_add(name, part, received, side):
    _, rows, cols = part.shape
    tiles = [t for t in range(PAIR_ROWS, 0, -BF16_ROWS) if rows % t == 0]
    tr = tiles[0] if tiles else rows

    def body(side_ref, p_ref, r_ref, o_ref):
        o_ref[...] = (p_ref[...].astype(F32) + r_ref[...].astype(F32)).astype(BF16)

    tile = _spec((None, tr, cols), lambda q, i, side_ref: (q, i, 0))
    return pl.pallas_call(
        body, name=name,
        grid_spec=pltpu.PrefetchScalarGridSpec(
            num_scalar_prefetch=1, grid=(N_CHIPS, rows // tr),
            in_specs=[_spec((None, tr, cols), lambda q, i, side_ref: (2 * q + side_ref[0], i, 0)), tile], out_specs=tile),
        out_shape=jax.ShapeDtypeStruct((N_CHIPS, rows, cols), BF16), compiler_params=_params(("parallel", "parallel")),
    )(side, part, received)


def _send_copy(parts, landing, send_sems, recv_sems, t, s, k):
    x, y, c = _position()
    px, py, _ = _flip(x, y, c, k)
    return pltpu.make_async_remote_copy(
        src_ref=parts[t].at[_chip(px, py)], dst_ref=landing[t].at[_chip(x, y)], send_sem=send_sems.at[s],
        recv_sem=recv_sems.at[s], device_id=(px, py, c), device_id_type=MESH_ID)


def _send_arrival(landing, send_sems, recv_sems, t, s, k):
    x, y, c = _position()
    px, py, _ = _flip(x, y, c, k)
    landed = landing[t].at[_chip(px, py)]
    return pltpu.make_async_remote_copy(
        src_ref=landed, dst_ref=landed, send_sem=send_sems.at[s], recv_sem=recv_sems.at[s],
        device_id=(px, py, c), device_id_type=MESH_ID)


def _handshake(peers):
    x, y, c = _position()
    barrier = pltpu.get_barrier_semaphore()
    for k in peers:
        pl.semaphore_signal(barrier, inc=1, device_id=_flip(x, y, c, k), device_id_type=MESH_ID)
    pl.semaphore_wait(barrier, len(peers))


def _sequencer(name, collective_id, n_copies, body, operands, out_type):
    n_arrays = len(operands)
    return pl.kernel(
        body, out_type=out_type, mesh=plsc.ScalarSubcoreMesh(axis_name="sequencer", num_cores=1), name=name,
        scratch_types=(pltpu.SemaphoreType.DMA((n_copies,)), pltpu.SemaphoreType.DMA((n_copies,)),
                       pltpu.SemaphoreType.DMA((n_arrays,))),
        compiler_params=pltpu.CompilerParams(collective_id=collective_id))(*operands)


def _sequencer_exchange(name, collective_id, parts, after=()):
    n, n_peers, n_in = len(parts), len(SAME_CORE), len(parts) + len(after)

    def body(*refs):
        src, landing = refs[:n], refs[n_in:n_in + n]
        send_sems, recv_sems, local_sems = refs[n_in + n:]
        _handshake(SAME_CORE)
        x, y, _ = _position()
        mine = [pltpu.make_async_copy(src[t].at[_chip(x, y)], landing[t].at[_chip(x, y)], local_sems.at[t]) for t in range(n)]
        for cp in mine:
            cp.start()
        sent = [_send_copy(src, landing, send_sems, recv_sems, t, t * n_peers + j, k)
                for t in range(n) for j, k in enumerate(SAME_CORE)]
        for cp in sent:
            cp.start()
        for t in range(n):
            for j, k in enumerate(SAME_CORE):
                _send_arrival(landing, send_sems, recv_sems, t, t * n_peers + j, k).wait_recv()
        for cp in sent:
            cp.wait_send()
        for cp in mine:
            cp.wait()

    landing = [jax.ShapeDtypeStruct(p.shape, p.dtype) for p in parts]
    return _sequencer(name, collective_id, n * n_peers, body, list(parts) + list(after), landing)


def _sequencer_gather(name, collective_id, shards):
    n = len(shards)

    def body(*refs):
        _handshake(SIBLING_AND_NEIGHBOURS)
        _gather_copies(refs[:n], refs[n:2 * n], *refs[2 * n:])

    gathered = [jax.ShapeDtypeStruct((N_DEV,) + s.shape, s.dtype) for s in shards]
    return _sequencer(name, collective_id, GATHER_COPIES * n, body, shards, gathered)


ADAM_ROWS = 512
BF16_ROWS = 16


def _adam_update(w, g, m, v):
    m = ADAM_B1 * m + (1.0 - ADAM_B1) * g
    v = ADAM_B2 * v + (1.0 - ADAM_B2) * (g * g)
    m_hat = m / (1.0 - ADAM_B1 ** ADAM_STEP)
    v_hat = v / (1.0 - ADAM_B2 ** ADAM_STEP)
    delta = -ADAM_LR * (m_hat / (jnp.sqrt(v_hat) + ADAM_EPS) + ADAM_WD * w)
    return delta, m, v


def _sum_slots(ref):
    total = ref[0].astype(F32)
    for d in range(1, ref.shape[0]):
        total = total + ref[d].astype(F32)
    return total


def _adamw_sum(name, landed, w, m, v):
    layers, rows, cols = w.shape
    tiles = [t for t in range(ADAM_ROWS, 0, -BF16_ROWS) if rows % t == 0]
    tr = tiles[0] if tiles else rows
    nt = rows // tr

    def body(*refs):
        parts = refs[:layers]
        w_ref, m_ref, v_ref, g_ref, d_ref, nm_ref, nv_ref = refs[layers:]
        layer = pl.program_id(0)
        g = _sum_slots(parts[0])
        for q in range(1, layers):
            g = jnp.where(layer == q, _sum_slots(parts[q]), g)
        delta, new_m, new_v = _adam_update(w_ref[...], g, m_ref[...], v_ref[...])
        g_ref[...] = g
        d_ref[...] = delta
        nm_ref[...] = new_m
        nv_ref[...] = new_v

    def part_spec(q):
        return _spec((N_CHIPS, tr, cols), lambda l, i: (0, jnp.where(l == q, i, jnp.where(l < q, 0, nt - 1)), 0))

    tile = _spec((None, tr, cols), lambda l, i: (l, i, 0))
    out = jax.ShapeDtypeStruct((layers, rows, cols), F32)
    return pl.pallas_call(
        body, name=name, grid=(layers, nt), in_specs=[part_spec(q) for q in range(layers)] + [tile] * 3,
        out_specs=[tile] * 4, out_shape=[out] * 4, compiler_params=_params(("arbitrary", "arbitrary")),
    )(*landed, w, m, v)


def _sum_small(landed):
    def body(in_ref, out_ref):
        out_ref[...] = _sum_slots(in_ref)

    return pl.pallas_call(body, name="small_grad_sum", out_shape=jax.ShapeDtypeStruct(landed.shape[1:], F32))(landed)


def _adamw_small(arrays):
    n = len(arrays)

    def body(*refs):
        for i in range(n):
            g_ref, w_ref, m_ref, v_ref = refs[4 * i:4 * i + 4]
            d_ref, nm_ref, nv_ref = refs[4 * n + 3 * i:4 * n + 3 * i + 3]
            d_ref[...], nm_ref[...], nv_ref[...] = _adam_update(w_ref[...], g_ref[...], m_ref[...], v_ref[...])

    out = [jax.ShapeDtypeStruct(w.shape, F32) for _, w, _, _ in arrays for _ in range(3)]
    flat = pl.pallas_call(body, name="adam_small", out_shape=out)(*[a for group in arrays for a in group])
    return [tuple(flat[3 * i:3 * i + 3]) for i in range(n)]


LANES = 128
SUBLANES = 8
F_CONV_SHARD = D_FF // N_DEV
GATE_SHARD = KEY_DIM // N_DEV
NORM_SHARD = D_MODEL // N_DEV


def _tile_rows(a):
    flat = a.reshape(-1)
    size = -(-flat.shape[0] // (SUBLANES * LANES)) * SUBLANES * LANES
    return jnp.pad(flat, (0, size - flat.shape[0])).reshape(-1, LANES)


def _pack_rows(pieces):
    return jnp.concatenate([_tile_rows(p) for p in pieces], axis=0)


def _unpack_rows(packed, shapes):
    out, row = [], 0
    for shape in shapes:
        size = 1
        for s in shape:
            size *= s
        rows = -(-size // (SUBLANES * LANES)) * SUBLANES
        piece = packed[..., row:row + rows, :]
        out.append(piece.reshape(piece.shape[:-2] + (rows * LANES,))[..., :size])
        row += rows
    return out


SMALL_SHARDS = ((GATE_RANK, GATE_SHARD), (1, NORM_SHARD), (3, NORM_SHARD), (2, 3, F_CONV_SHARD))


def _unpack_small_shards(g):
    gate, b_norm, b_conv, f_conv = _unpack_rows(g, SMALL_SHARDS)
    gate = gate.reshape(N_DEV, GATE_RANK, GATE_SHARD).transpose(1, 0, 2).reshape(GATE_RANK, KEY_DIM)
    b_norm = b_norm.reshape(1, D_MODEL)
    b_conv = b_conv.reshape(N_DEV, 3, NORM_SHARD).transpose(1, 0, 2).reshape(3, D_MODEL)
    f_conv = f_conv.reshape(N_DEV, 2, 3, F_CONV_SHARD).transpose(1, 2, 0, 3).reshape(2, 3, D_FF)
    return gate, b_norm, b_conv, f_conv


SMALL_LAYOUT = (("a_norm", (1, D_MODEL)), ("a_w_gate_up", (GATE_RANK, KEY_DIM)), ("a_b_gate", (1, KEY_DIM)), ("a_gn", (1, VALUE_DIM)),
                ("b_norm", (1, D_MODEL)), ("b_conv", (3, D_MODEL)), ("f_norm0", (1, D_MODEL)), ("f_norm1", (1, D_MODEL)),
                ("f_conv0", (3, D_FF)), ("f_conv1", (3, D_FF)), ("final_norm", (1, D_MODEL)), ("loss", (1, LANES)))


def _pack_small_grads(g):
    full = dict(g)
    full["a_w_gate_up"] = g["a_w_gate_up"][:GATE_RANK]
    for layer in range(2):
        full[f"f_norm{layer}"] = g["f_norm"][layer]
        full[f"f_conv{layer}"] = g["f_conv"][layer]
    return _pack_rows([full[name] for name, _ in SMALL_LAYOUT])


def _unpack_small_grads(packed):
    pieces = _unpack_rows(packed, [shape for _, shape in SMALL_LAYOUT])
    out = {name: piece.reshape(shape) for (name, shape), piece in zip(SMALL_LAYOUT, pieces)}
    out["f_norm"] = jnp.stack([out["f_norm0"][0], out["f_norm1"][0]])
    out["f_conv"] = jnp.stack([out["f_conv0"], out["f_conv1"]])
    return out


def kernel(x, a_norm, a_w_in, a_w_gate_up, a_b_gate, a_gn, a_w_out, b_norm, b_w_in, b_conv, b_w_out, f_norm, f_w_up, f_conv, f_w_down, final_norm, loss_target, m_a_norm, m_a_w_in, m_a_w_gate_up, m_a_b_gate, m_a_gn, m_a_w_out, m_b_norm, m_b_w_in, m_b_conv, m_b_w_out, m_f_norm, m_f_w_up, m_f_conv, m_f_w_down, m_final_norm, v_a_norm, v_a_w_in, v_a_w_gate_up, v_a_b_gate, v_a_gn, v_a_w_out, v_b_norm, v_b_w_in, v_b_conv, v_b_w_out, v_f_norm, v_f_w_up, v_f_conv, v_f_w_down, v_final_norm):
    my_slot = _slot(*_position())

    transposed = lambda w: jnp.swapaxes(w, 1, 2)
    a_transposed = lambda w: w.reshape(D_MODEL, A_SHARD).T.reshape(1, A_SHARD, D_MODEL)
    a_w_in_t, f_w_up_t = a_transposed(a_w_in), transposed(f_w_up)
    first = _all_gather("weight_gather", [a_w_in_t[0].astype(BF16), a_w_out[0].astype(BF16),
                                          _pack_rows([a_w_gate_up[0], b_norm, b_conv[0], f_conv])])
    gathers, small_shards = {}, first[2]
    later = (("f0", f_w_up_t[0], f_w_down[0]), ("b", b_w_in[0], b_w_out[0]), ("f1", f_w_up_t[1], f_w_down[1]))
    for collective_id, (group, w_in, w_out) in enumerate(later):
        w_in, w_out, small_shards = lax.optimization_barrier((w_in.astype(BF16), w_out.astype(BF16), small_shards))
        gathers[group] = _sequencer_gather(f"gather_{group}", collective_id, [w_in, w_out])
    gate_full, b_norm_full, b_conv_full, f_conv_full = _unpack_small_shards(small_shards)
    a_w_in_full = jnp.pad(first[0].reshape(PROJ_A, D_MODEL), ((0, PROJ_A_PAD - PROJ_A), (0, 0)))
    weights = dict(
        a_norm=a_norm, a_w_gate_up=jnp.pad(gate_full, ((0, GATE_PAD - GATE_RANK), (0, 0))).astype(BF16), a_b_gate=a_b_gate,
        a_gn=a_gn, b_norm=b_norm_full, b_conv=b_conv_full, f_norm=f_norm, f_conv=f_conv_full,
        final_norm=final_norm.reshape(1, D_MODEL))

    def fetch(group, after):
        if group == "a":
            return a_w_in_full, first[1].reshape(D_MODEL, D_MODEL)
        w_in, w_out = gathers[group]
        if group == "b":
            return w_in, w_out.reshape(D_MODEL, D_MODEL)
        return w_in.reshape(2, D_FF, D_MODEL), w_out.reshape(D_FF, D_MODEL)

    exchanges, pending = {}, []
    exchange_ids = dict(b=3, f0=4, a=5)
    side = lax.axis_index("c").astype(jnp.int32).reshape(1)

    def emit(group, parts, received, carry):
        sums = [_pair_add(f"pair_add_{group}_{i}", part, got, side) for i, (part, got) in enumerate(zip(parts, received))]
        carry, *sums = lax.optimization_barrier((carry, *sums))
        pending.extend(sums)
        if group != "f1":
            after = list(exchanges.values())[-1][:1] if exchanges else ()
            exchanges[group] = _sequencer_exchange(f"grads_{group}", exchange_ids[group], list(pending), after)
            pending.clear()
        return carry

    dx, g = _local_step(x[0], loss_target[0], weights, fetch, emit)

    (up1, down1, d_b_in, d_b_out), (up0, down0), (d_a_in, d_a_out) = (exchanges[group] for group in ("b", "f0", "a"))
    back = lambda results: tuple(transposed(r) for r in results)
    big = dict(
        b_w_in=_adamw_sum("adam_b_w_in", [d_b_in], b_w_in, m_b_w_in, v_b_w_in),
        b_w_out=_adamw_sum("adam_b_w_out", [d_b_out], b_w_out, m_b_w_out, v_b_w_out),
        f_w_up=back(_adamw_sum("adam_f_w_up", [up0, up1], f_w_up_t, transposed(m_f_w_up), transposed(v_f_w_up))),
        f_w_down=_adamw_sum("adam_f_w_down", [down0, down1], f_w_down, m_f_w_down, v_f_w_down))
    small_packed, *updated = lax.optimization_barrier((_pack_small_grads(g), *big["f_w_down"]))
    big["f_w_down"] = tuple(updated)
    small_landed = _all_gather("small_grad_gather", [small_packed])[0]
    big.update(
        a_w_in=tuple(r.reshape(A_SHARD, D_MODEL).T.reshape(1, D_MODEL, A_SHARD) for r in _adamw_sum(
            "adam_a_w_in", [d_a_in], a_w_in_t, a_transposed(m_a_w_in), a_transposed(v_a_w_in))),
        a_w_out=_adamw_sum("adam_a_w_out", [d_a_out], a_w_out, m_a_w_out, v_a_w_out))
    small_g = _unpack_small_grads(_sum_small(small_landed))
    loss = small_g["loss"][0, 0]
    small_g["a_w_gate_up"] = lax.dynamic_slice_in_dim(small_g["a_w_gate_up"], my_slot * GATE_SHARD, GATE_SHARD, axis=1)
    small_g["b_norm"] = lax.dynamic_slice_in_dim(small_g["b_norm"], my_slot * NORM_SHARD, NORM_SHARD, axis=1)
    small_g["b_conv"] = lax.dynamic_slice_in_dim(small_g["b_conv"], my_slot * NORM_SHARD, NORM_SHARD, axis=1)
    small_g["f_conv"] = lax.dynamic_slice_in_dim(small_g["f_conv"], my_slot * F_CONV_SHARD, F_CONV_SHARD, axis=2)
    small_w = dict(
        a_norm=(a_norm, m_a_norm, v_a_norm), a_w_gate_up=(a_w_gate_up, m_a_w_gate_up, v_a_w_gate_up),
        a_b_gate=(a_b_gate, m_a_b_gate, v_a_b_gate), a_gn=(a_gn, m_a_gn, v_a_gn), b_norm=(b_norm, m_b_norm, v_b_norm),
        b_conv=(b_conv, m_b_conv, v_b_conv), f_norm=(f_norm, m_f_norm, v_f_norm), f_conv=(f_conv, m_f_conv, v_f_conv),
        final_norm=(final_norm, m_final_norm, v_final_norm))
    two_d = lambda a: a.reshape(-1, a.shape[-1])
    updates = _adamw_small([tuple(two_d(a.reshape(w.shape)) for a in (small_g[name], w, m, v)) for name, (w, m, v) in small_w.items()])
    small = {}
    for (name, (w, _, _)), update in zip(small_w.items(), updates):
        small[name] = (small_g[name].reshape(w.shape),) + tuple(u.reshape(w.shape) for u in update)

    order = ["a_norm", "a_w_in", "a_w_gate_up", "a_b_gate", "a_gn", "a_w_out", "b_norm", "b_w_in", "b_conv", "b_w_out",
             "f_norm", "f_w_up", "f_conv", "f_w_down", "final_norm"]
    results = {**big, **small}
    outputs = [loss, dx.reshape(1, SEQ, D_MODEL)]
    for kind in range(4):
        outputs += [results[name][kind] for name in order]
    return tuple(outputs)
```

```python
import jax
import jax.numpy as jnp
from jax import lax
from jax.experimental import pallas as pl
from jax.experimental.pallas import tpu as pltpu
from jax.experimental.pallas import tpu_sc as plsc

F32 = jnp.float32
BF16 = jnp.bfloat16

N_DEV = 8
SEQ = 2048
D_MODEL = 1024
CHUNK = 64
N_CHUNKS = SEQ // CHUNK
RMS_EPS = 1e-6
GLA_HEADS = 4
KEY_DIM = 512
VALUE_DIM = 1024
HEAD_K = KEY_DIM // GLA_HEADS
HEAD_V = VALUE_DIM // GLA_HEADS
GATE_RANK = 16
GATE_PAD = 128
GATE_NORMALIZER = 16.0
PROJ_A = 2 * KEY_DIM + 2 * VALUE_DIM + GATE_RANK
PROJ_A_PAD = 2 * KEY_DIM + 2 * VALUE_DIM + GATE_PAD
A_SHARD = PROJ_A // N_DEV
B_SHARD = 3 * D_MODEL // N_DEV
D_FF = 2816
ADAM_LR = 0.001
ADAM_B1 = 0.9
ADAM_B2 = 0.999
ADAM_EPS = 1e-08
ADAM_WD = 0.01
ADAM_STEP = 10
MESH_AXES = ("x", "y", "c")

VMEM_LIMIT = 56 * 1024 * 1024
ROW_CHUNK = 256
HALO = 16


def _params(sem=None, vmem=VMEM_LIMIT):
    return pltpu.CompilerParams(dimension_semantics=sem, vmem_limit_bytes=vmem)


NN = ((1,), (0,))
NT = ((1,), (1,))
TN = ((0,), (0,))


def _matmul(name, a, a_spec, b, b_spec, dims, grid, out_shape, out_spec, k_blocks=None, a_block_cols=None, res=None,
            res_spec=None, transpose_out=False, norm=None, swap=()):
    has_res = res is not None
    n_swap = len(swap)

    def body(*refs):
        a_ref, b_ref = refs[0], refs[1]
        r_ref = refs[2] if has_res else None

        def product(lhs, rhs):
            return lax.dot_general(lhs.astype(BF16), rhs, (dims, ((), ())), preferred_element_type=F32)

        if k_blocks is None:
            v = product(a_ref[...], b_ref[...])
        else:
            v = None
            for k in range(k_blocks):
                lhs = a_ref[k] if a_block_cols is None else a_ref[:, k * a_block_cols:(k + 1) * a_block_cols]
                p = product(lhs, b_ref[k])
                v = p if v is None else v + p
        if transpose_out:
            v = v.T
        if has_res:
            v = v + r_ref[...]
        if norm is None:
            o_ref = refs[2 + has_res]
            o_ref[...] = v.astype(o_ref.dtype)
            return
        n_in = 5 + has_res
        x_ref, g_ref, dxi_ref = refs[2 + has_res:n_in]
        dx_ref, dx16_ref, dg_ref = refs[n_in + n_swap:n_in + n_swap + 3]
        if n_swap:
            copies = _pair_copies(refs[n_in:n_in + n_swap], refs[n_in + n_swap + 3:n_in + 2 * n_swap + 3], *refs[-2:])

            @pl.when(pl.program_id(0) == 0)
            def _():
                for send, _ in copies:
                    send.start()

            @pl.when(pl.program_id(0) == grid[0] - 1)
            def _():
                for send, arrival in copies:
                    arrival.wait_recv()
                    send.wait_send()

        dx, dg = _norm_bwd_rows(x_ref[...], g_ref[...], v)
        dx = dxi_ref[...] + dx
        dx_ref[...] = dx
        dx16_ref[...] = dx.astype(BF16)

        @pl.when(pl.program_id(0) == 0)
        def _():
            dg_ref[...] = dg

        @pl.when(pl.program_id(0) > 0)
        def _():
            dg_ref[...] += dg

    operands = [a, b] + ([res] if has_res else [])
    in_specs = [a_spec, b_spec] + ([res_spec] if has_res else [])
    semantics = ("parallel",) * len(grid)
    scratch = []
    if norm is not None:
        vec = _spec((1, D_MODEL), lambda i: (0, 0))
        any_space = pl.BlockSpec(memory_space=pl.ANY)
        operands += list(norm) + list(swap)
        in_specs += [out_spec, vec, out_spec] + [any_space] * n_swap
        out_shape = [_act(dtype=F32), _act(), jax.ShapeDtypeStruct((1, D_MODEL), F32)]
        out_shape += [jax.ShapeDtypeStruct((N_DEV // 2,) + p.shape[1:], p.dtype) for p in swap]
        out_spec = [out_spec, out_spec, vec] + [any_space] * n_swap
        semantics = ("arbitrary",)
        if n_swap:
            scratch = [pltpu.SemaphoreType.DMA((n_swap, N_DEV // 2))] * 2
    return pl.pallas_call(
        body, name=name, grid=grid, in_specs=in_specs, out_specs=out_spec, out_shape=out_shape, scratch_shapes=scratch,
        compiler_params=_params(semantics),
    )(*operands)


def _resident(shape):
    return pl.BlockSpec(shape, lambda *_: (0,) * len(shape), pipeline_mode=pl.Buffered(1))


TM = 512
N_TM = SEQ // TM
PA_TILE = 640
N_PA = PROJ_A_PAD // PA_TILE
OUT_TILE = 256


def _spec(shape, fn):
    return pl.BlockSpec(shape, fn)


def _act(shape=(SEQ, D_MODEL), dtype=BF16):
    return jax.ShapeDtypeStruct(shape, dtype)


def _norm_proj(name, x, gamma, w):
    blocks = w.ndim == 3
    n_out = w.shape[0] * w.shape[2] if blocks else w.shape[0]

    def body(x_ref, g_ref, w_ref, h_ref, o_ref):
        x = x_ref[...]
        h = (x * _rstd(x) * g_ref[...]).astype(BF16)
        h_ref[...] = h
        if blocks:
            n = w.shape[2]
            for j in range(w.shape[0]):
                o_ref[:, j * n:(j + 1) * n] = jnp.dot(h, w_ref[j], preferred_element_type=F32).astype(BF16)
        else:
            o_ref[...] = lax.dot_general(h, w_ref[...], (NT, ((), ())), preferred_element_type=F32).astype(BF16)

    row = _spec((TM, D_MODEL), lambda i: (i, 0))
    return pl.pallas_call(
        body, name=name, grid=(N_TM,), in_specs=[row, _resident((1, D_MODEL)), _resident(w.shape)],
        out_specs=[row, _spec((TM, n_out), lambda i: (i, 0))], out_shape=[_act(), _act((SEQ, n_out))],
        compiler_params=_params(("parallel",)),
    )(x, gamma, w)


def _rows_matmul(name, a, w, dims, x=None):
    k = a.shape[1]
    n = w.shape[1] if dims == NN else w.shape[0]
    row = _spec((TM, n), lambda i: (i, 0))
    return _matmul(name, a, _spec((TM, k), lambda i: (i, 0)), w, _resident(w.shape), dims, (N_TM,),
                   _act((SEQ, n), F32 if x is not None else BF16), row, res=x, res_spec=row if x is not None else None)


def _sum_blocks_nn(name, a_blocks, w_blocks, x=None, norm=None, swap=()):
    nb, _, n = a_blocks.shape
    row = _spec((TM, D_MODEL), lambda i: (i, 0))
    return _matmul(name, a_blocks, _spec((nb, TM, n), lambda i: (0, i, 0)), w_blocks, _resident((nb, n, D_MODEL)),
                   NN, (N_TM,), _act(dtype=F32), row, k_blocks=nb, res=x, res_spec=row if x is not None else None, norm=norm, swap=swap)


def _sum_cols_nt(name, d, w_blocks, norm=None, swap=()):
    nb, _, n = w_blocks.shape
    return _matmul(name, d, _spec((TM, nb * n), lambda i: (i, 0)), w_blocks, _resident((nb, D_MODEL, n)), NT,
                   (N_TM,), _act(dtype=F32), _spec((TM, D_MODEL), lambda i: (i, 0)), k_blocks=nb, a_block_cols=n, norm=norm, swap=swap)


def _wide_nn(name, d, wt, x=None, norm=None, swap=()):
    n = wt.shape[0]
    row = _spec((TM, D_MODEL), lambda i: (i, 0))
    return _matmul(name, d, _spec((TM, n), lambda i: (i, 0)), wt, _resident((n, D_MODEL)), NN, (N_TM,),
                   _act(dtype=F32), row, res=x, res_spec=row if x is not None else None, norm=norm, swap=swap)


def _wgrad_halves_tn(name, d, n_tile, h):
    _, _, n = d.shape
    return _matmul(name, d, _spec((None, SEQ, n_tile), lambda p, j: (p, 0, j)), h, _resident((SEQ, D_MODEL)), TN,
                   (2, n // n_tile), _act((2, n, D_MODEL)), _spec((None, n_tile, D_MODEL), lambda p, j: (p, j, 0)))


def _wgrad_cols_tn(name, d, n_tile, h):
    n = d.shape[1]
    return _matmul(name, d, _spec((SEQ, n_tile), lambda j: (0, j)), h, _resident((SEQ, D_MODEL)), TN,
                   (n // n_tile,), _act((n, D_MODEL)), _spec((n_tile, D_MODEL), lambda j: (j, 0)))


def _wgrad_cols_transposed_tn(name, h, d, n_tile):
    nb = d.shape[1] // n_tile
    return _matmul(name, d, _spec((SEQ, n_tile), lambda j: (0, j)), h, _resident((SEQ, D_MODEL)), TN, (nb,),
                   _act((nb, D_MODEL, n_tile)), _spec((None, D_MODEL, n_tile), lambda j: (j, 0, 0)), transpose_out=True)


NORM_ROWS = 512


def _rstd(x):
    return lax.rsqrt(jnp.mean(x * x, axis=-1, keepdims=True) + RMS_EPS)


def _norm_bwd_rows(x, gamma, dh):
    r = _rstd(x)
    xh = x * r
    dxh = dh * gamma
    dx = r * (dxh - xh * jnp.mean(dxh * xh, axis=-1, keepdims=True))
    return dx, jnp.sum(dh * xh, axis=0, keepdims=True)


def _down_loss_head(a, w_down, x_in, gamma, target):
    def body(a_ref, w_ref, x_ref, g_ref, t_ref, loss_ref, dx_ref, dx16_ref, dg_ref):
        x = x_ref[...] + jnp.dot(a_ref[...], w_ref[...], preferred_element_type=F32)
        gamma = g_ref[...]
        err = x * _rstd(x) * gamma - t_ref[...]
        dy = err * (1.0 / D_MODEL)
        dx, dg = _norm_bwd_rows(x, gamma, dy)
        dx_ref[...] = dx
        dx16_ref[...] = dx.astype(BF16)
        part = 0.5 * jnp.sum(jnp.sum(err * err, axis=-1, keepdims=True) * (1.0 / D_MODEL), axis=0, keepdims=True)
        part = jnp.broadcast_to(part, loss_ref.shape)

        @pl.when(pl.program_id(0) == 0)
        def _():
            dg_ref[...] = dg
            loss_ref[...] = part

        @pl.when(pl.program_id(0) > 0)
        def _():
            dg_ref[...] += dg
            loss_ref[...] += part

    row = _spec((TM, D_MODEL), lambda i: (i, 0))
    vec = _spec((1, D_MODEL), lambda i: (0, 0))
    return pl.pallas_call(
        body, name="ffn1_down_loss_head", grid=(N_TM,),
        in_specs=[_spec((TM, D_FF), lambda i: (i, 0)), _resident((D_FF, D_MODEL)), row, vec, row],
        out_specs=[_spec((1, 128), lambda i: (0, 0)), row, row, vec],
        out_shape=[jax.ShapeDtypeStruct((1, 128), F32), _act(dtype=F32), _act(), jax.ShapeDtypeStruct((1, D_MODEL), F32)],
        compiler_params=_params(("arbitrary",)),
    )(a, w_down, x_in, gamma, target)


def _sigmoid(x):
    return 1.0 / (1.0 + jnp.exp(-x))


def _rows(ref, c):
    return ref[pl.ds(pl.multiple_of(c * ROW_CHUNK, ROW_CHUNK), ROW_CHUNK), :].astype(F32)


def _rows_before(ref, c):
    start = pl.multiple_of(jnp.maximum(c * ROW_CHUNK - HALO, 0), HALO)
    rows = ref[pl.ds(start, HALO), :].astype(F32)
    return jnp.where(c > 0, rows, 0.0)


def _rows_after(ref, c, n_chunks):
    start = pl.multiple_of(jnp.minimum((c + 1) * ROW_CHUNK, SEQ - HALO), HALO)
    rows = ref[pl.ds(start, HALO), :].astype(F32)
    return jnp.where(c < n_chunks - 1, rows, 0.0)


def _shift_down(z, before, n):
    return pltpu.roll(jnp.concatenate([before, z], axis=0), n, 0)[before.shape[0]:]


def _shift_up(z, after, n):
    rows = z.shape[0]
    return pltpu.roll(jnp.concatenate([z, after], axis=0), rows + HALO - n, 0)[:rows]


def _conv_rows(z, before, w):
    z1 = _shift_down(z, before, 1)
    z2 = _shift_down(z, before, 2)
    return w[2:3, :] * z + w[1:2, :] * z1 + w[0:1, :] * z2, z1, z2


def _conv_t_rows(dy, after, w):
    return w[2:3, :] * dy + w[1:2, :] * _shift_up(dy, after, 1) + w[0:1, :] * _shift_up(dy, after, 2)


N_ROW_CHUNKS = SEQ // ROW_CHUNK


FF_COLS = 256
N_FF_COLS = D_FF // FF_COLS


def _ffn_mid_bwd(name, gu, conv_w, da):
    def body(gu_ref, w_ref, da_ref, dgu_ref, dw_ref, dgc_ref):
        w = w_ref[...]

        def first(c, acc):
            g = _rows(gu_ref.at[0], c)
            u = _rows(gu_ref.at[1], c)
            d = _rows(da_ref, c)
            gc, g1, g2 = _conv_rows(g, _rows_before(gu_ref.at[0], c), w)
            sg = _sigmoid(gc)
            rows = pl.ds(pl.multiple_of(c * ROW_CHUNK, ROW_CHUNK), ROW_CHUNK)
            dgu_ref[1, rows, :] = (d * gc * sg).astype(BF16)
            dgc = d * u * (sg * (1.0 + gc * (1.0 - sg)))
            dgc_ref[rows, :] = dgc
            return (acc[0] + jnp.sum(dgc * g2, axis=0, keepdims=True), acc[1] + jnp.sum(dgc * g1, axis=0, keepdims=True),
                    acc[2] + jnp.sum(dgc * g, axis=0, keepdims=True))

        zero = jnp.zeros((1, FF_COLS), F32)
        acc = lax.fori_loop(0, N_ROW_CHUNKS, first, (zero, zero, zero))
        for r in range(3):
            dw_ref[r:r + 1, :] = acc[r]

        def second(c, carry):
            dgc = _rows(dgc_ref, c)
            dg = _conv_t_rows(dgc, _rows_after(dgc_ref, c, N_ROW_CHUNKS), w)
            dgu_ref[0, pl.ds(pl.multiple_of(c * ROW_CHUNK, ROW_CHUNK), ROW_CHUNK), :] = dg.astype(BF16)
            return carry

        lax.fori_loop(0, N_ROW_CHUNKS, second, 0)

    pair = _spec((2, SEQ, FF_COLS), lambda j: (0, 0, j))
    wspec = _spec((3, FF_COLS), lambda j: (0, j))
    return pl.pallas_call(
        body, name=name, grid=(N_FF_COLS,), in_specs=[pair, wspec, _spec((SEQ, FF_COLS), lambda j: (0, j))],
        out_specs=[pair, wspec], out_shape=[_act((2, SEQ, D_FF)), jax.ShapeDtypeStruct((3, D_FF), F32)],
        scratch_shapes=[pltpu.VMEM((SEQ, FF_COLS), F32)],
        compiler_params=_params(("parallel",)),
    )(gu, conv_w, da)


SC_COLS = 256
N_SC = D_MODEL // SC_COLS


def _sc_specs():
    return [_spec((SEQ, SC_COLS), lambda j, part=part: (0, part * N_SC + j)) for part in range(3)]


def _sc_mid_fwd(p, conv_w):
    def body(b_ref, c_ref, h_ref, w_ref, y_ref):
        w = w_ref[...]

        def chunk(c, carry):
            z = _rows(c_ref, c) * _rows(h_ref, c)
            before = _rows_before(c_ref, c) * _rows_before(h_ref, c)
            zc, _, _ = _conv_rows(z, before, w)
            y_ref[pl.ds(pl.multiple_of(c * ROW_CHUNK, ROW_CHUNK), ROW_CHUNK), :] = (_rows(b_ref, c) * zc).astype(BF16)
            return carry

        lax.fori_loop(0, N_ROW_CHUNKS, chunk, 0)

    col = _spec((SEQ, SC_COLS), lambda j: (0, j))
    return pl.pallas_call(
        body, name="sc_mid_fwd", grid=(N_SC,), in_specs=_sc_specs() + [_spec((3, SC_COLS), lambda j: (0, j))], out_specs=col,
        out_shape=jax.ShapeDtypeStruct((SEQ, D_MODEL), BF16), compiler_params=_params(("parallel",)),
    )(p, p, p, conv_w)


def _sc_mid_bwd(p, conv_w, dy):
    def body(b_ref, c_ref, h_ref, w_ref, dy_ref, db_ref, dc_ref, dh_ref, dw_ref, dzc_ref):
        w = w_ref[...]

        def first(c, acc):
            z = _rows(c_ref, c) * _rows(h_ref, c)
            before = _rows_before(c_ref, c) * _rows_before(h_ref, c)
            zc, z1, z2 = _conv_rows(z, before, w)
            d = _rows(dy_ref, c)
            rows = pl.ds(pl.multiple_of(c * ROW_CHUNK, ROW_CHUNK), ROW_CHUNK)
            db_ref[rows, :] = (d * zc).astype(BF16)
            dzc = d * _rows(b_ref, c)
            dzc_ref[rows, :] = dzc
            return (acc[0] + jnp.sum(dzc * z2, axis=0, keepdims=True), acc[1] + jnp.sum(dzc * z1, axis=0, keepdims=True),
                    acc[2] + jnp.sum(dzc * z, axis=0, keepdims=True))

        zero = jnp.zeros((1, SC_COLS), F32)
        acc = lax.fori_loop(0, N_ROW_CHUNKS, first, (zero, zero, zero))
        for r in range(3):
            dw_ref[r:r + 1, :] = acc[r]

        def second(c, carry):
            dz = _conv_t_rows(_rows(dzc_ref, c), _rows_after(dzc_ref, c, N_ROW_CHUNKS), w)
            rows = pl.ds(pl.multiple_of(c * ROW_CHUNK, ROW_CHUNK), ROW_CHUNK)
            dc_ref[rows, :] = (dz * _rows(h_ref, c)).astype(BF16)
            dh_ref[rows, :] = (dz * _rows(c_ref, c)).astype(BF16)
            return carry

        lax.fori_loop(0, N_ROW_CHUNKS, second, 0)

    col = _spec((SEQ, SC_COLS), lambda j: (0, j))
    wspec = _spec((3, SC_COLS), lambda j: (0, j))
    act = jax.ShapeDtypeStruct((SEQ, D_MODEL), BF16)
    return pl.pallas_call(
        body, name="sc_mid_bwd", grid=(N_SC,), in_specs=_sc_specs() + [wspec, col], out_specs=[col, col, col, wspec],
        out_shape=[act, act, act, jax.ShapeDtypeStruct((3, D_MODEL), F32)],
        scratch_shapes=[pltpu.VMEM((SEQ, SC_COLS), F32)], compiler_params=_params(("parallel",)),
    )(p, p, p, conv_w, dy)


GLA_GROUP = 4
GLA_ROWS = GLA_GROUP * CHUNK
N_GROUPS = N_CHUNKS // GLA_GROUP
Q0, K0, V0, R0, G0 = 0, KEY_DIM, 2 * KEY_DIM, 2 * KEY_DIM + VALUE_DIM, 2 * KEY_DIM + 2 * VALUE_DIM


def _tri(strict):
    r = lax.broadcasted_iota(jnp.int32, (CHUNK, CHUNK), 0)
    c = lax.broadcasted_iota(jnp.int32, (CHUNK, CHUNK), 1)
    return jnp.where(c < r if strict else c <= r, 1.0, 0.0).astype(F32)


def _cumsum_rows(tri, x):
    tri = tri.astype(BF16)
    total = None
    for _ in range(3):
        term = x.astype(BF16)
        x = x - term.astype(F32)
        product = jnp.dot(tri, term, preferred_element_type=F32)
        total = product if total is None else total + product
    return total


def _gate_logits(gl, wgu, b_gate):
    return jnp.dot(gl, wgu, preferred_element_type=F32) + b_gate


def _log_decay(logits):
    return (jnp.minimum(logits, 0.0) - jnp.log(1.0 + jnp.exp(-jnp.abs(logits)))) * (1.0 / GATE_NORMALIZER)


def _head(x, h, width):
    return x[:, h * width:(h + 1) * width]


def _gla_fwd(proj, wgu, b_gate, gn):
    def body(p_ref, wgu_ref, b_ref, gn_ref, o_ref, og_ref, st_ref, state):
        @pl.when(pl.program_id(0) == 0)
        def _():
            state[...] = jnp.zeros_like(state)

        tri = _tri(False)
        la = _log_decay(_gate_logits(p_ref[:, G0:G0 + GATE_PAD], wgu_ref[...], b_ref[...]))
        decays = []
        for c in range(GLA_GROUP):
            rows = slice(c * CHUNK, (c + 1) * CHUNK)
            cum = _cumsum_rows(tri, la[rows])
            tot = cum[CHUNK - 1:CHUNK, :]
            kd = (p_ref[rows, K0:K0 + KEY_DIM].astype(F32) * jnp.exp(tot - cum)).astype(BF16)
            decays.append(jnp.exp(tot))
            v = p_ref[rows, V0:V0 + VALUE_DIM]
            for h in range(GLA_HEADS):
                st_ref[c, h] = lax.dot_general(
                    _head(v, h, HEAD_V), _head(kd, h, HEAD_K), (TN, ((), ())), preferred_element_type=F32)
        for c in range(GLA_GROUP):
            for h in range(GLA_HEADS):
                s = state[h] * _head(decays[c], h, HEAD_K) + st_ref[c, h]
                state[h] = s
                st_ref[c, h] = s
        for c in range(GLA_GROUP):
            rows = slice(c * CHUNK, (c + 1) * CHUNK)
            q = (p_ref[rows, Q0:Q0 + KEY_DIM].astype(F32) * (HEAD_K ** -0.5)).astype(BF16)
            for h in range(GLA_HEADS):
                o_ref[rows, h * HEAD_V:(h + 1) * HEAD_V] = lax.dot_general(
                    _head(q, h, HEAD_K), st_ref[c, h].astype(BF16), (NT, ((), ())), preferred_element_type=F32)
        r = p_ref[:, R0:R0 + VALUE_DIM].astype(F32)
        gate = r * _sigmoid(r) * gn_ref[...]
        for h in range(GLA_HEADS):
            cols = slice(h * HEAD_V, (h + 1) * HEAD_V)
            o = o_ref[:, cols]
            og_ref[:, cols] = (o * _rstd(o) * gate[:, cols]).astype(BF16)

    rows = _spec((GLA_ROWS, VALUE_DIM), lambda i: (i, 0))
    const = lambda shape: _spec(shape, lambda i: (0,) * len(shape))
    return pl.pallas_call(
        body, name="gla_fwd", grid=(N_GROUPS,),
        in_specs=[_spec((GLA_ROWS, PROJ_A_PAD), lambda i: (i, 0)), const((GATE_PAD, KEY_DIM)), const((1, KEY_DIM)),
                  const((1, VALUE_DIM))],
        out_specs=[rows, rows, _spec((GLA_GROUP, GLA_HEADS, HEAD_V, HEAD_K), lambda i: (i, 0, 0, 0))],
        out_shape=[jax.ShapeDtypeStruct((SEQ, VALUE_DIM), F32), jax.ShapeDtypeStruct((SEQ, VALUE_DIM), BF16),
                   jax.ShapeDtypeStruct((N_CHUNKS, GLA_HEADS, HEAD_V, HEAD_K), F32)],
        scratch_shapes=[pltpu.VMEM((GLA_HEADS, HEAD_V, HEAD_K), F32)], compiler_params=_params(("arbitrary",)),
    )(proj, wgu, b_gate, gn)


def _gla_bwd(proj, wgu, b_gate, gn, o, states, dog):
    last = N_GROUPS - 1

    def body(p_ref, wgu_ref, b_ref, gn_ref, o_ref, st_ref, stp_ref, dog_ref, dp_ref, dwgu_ref, db_ref, dgn_ref, carry, do_buf,
             g_buf):
        step = pl.program_id(0)

        @pl.when(step == 0)
        def _():
            carry[...] = jnp.zeros_like(carry)

        r = p_ref[:, R0:R0 + VALUE_DIM].astype(F32)
        sr = _sigmoid(r)
        silu = r * sr
        gn_row = gn_ref[...]
        dog_rows = dog_ref[...].astype(F32)
        dn = dog_rows * silu
        dgn_cols = []
        for h in range(GLA_HEADS):
            cols = slice(h * HEAD_V, (h + 1) * HEAD_V)
            oh = o_ref[:, cols]
            rs = _rstd(oh)
            ohat = oh * rs
            dn_h = dn[:, cols]
            dgn_cols.append(jnp.sum(dn_h * ohat, axis=0, keepdims=True))
            dohat = dn_h * gn_row[:, cols]
            do_buf[:, cols] = rs * (dohat - ohat * jnp.mean(dohat * ohat, axis=-1, keepdims=True))
            n_h = ohat * gn_row[:, cols]
            dp_ref[:, R0 + h * HEAD_V:R0 + (h + 1) * HEAD_V] = (
                dog_rows[:, cols] * n_h * (sr[:, cols] * (1.0 + r[:, cols] * (1.0 - sr[:, cols])))).astype(BF16)
        dgn = jnp.concatenate(dgn_cols, axis=1)

        tri = _tri(False)
        tri_strict = _tri(True)
        gl = p_ref[:, G0:G0 + GATE_PAD]
        logits = _gate_logits(gl, wgu_ref[...], b_ref[...])
        la = _log_decay(logits)
        fades, kds, decays = [], [], []
        for c in range(GLA_GROUP):
            rows = slice(c * CHUNK, (c + 1) * CHUNK)
            cum = _cumsum_rows(tri, la[rows])
            tot = cum[CHUNK - 1:CHUNK, :]
            fades.append(jnp.exp(tot - cum))
            kds.append(p_ref[rows, K0:K0 + KEY_DIM].astype(F32) * fades[c])
            decays.append(jnp.exp(tot))
            q = (p_ref[rows, Q0:Q0 + KEY_DIM].astype(F32) * (HEAD_K ** -0.5)).astype(BF16)
            do = do_buf[rows, :].astype(BF16)
            for h in range(GLA_HEADS):
                do_h = _head(do, h, HEAD_V)
                dq = jnp.dot(do_h, st_ref[c, h].astype(BF16), preferred_element_type=F32) * (HEAD_K ** -0.5)
                dp_ref[rows, Q0 + h * HEAD_K:Q0 + (h + 1) * HEAD_K] = dq.astype(BF16)
                g_buf[c, h] = lax.dot_general(do_h, _head(q, h, HEAD_K), (TN, ((), ())), preferred_element_type=F32)
        for c in reversed(range(GLA_GROUP)):
            for h in range(GLA_HEADS):
                g = carry[h] + g_buf[c, h]
                g_buf[c, h] = g
                carry[h] = g * _head(decays[c], h, HEAD_K)
        dlogit_rows = []
        for c in range(GLA_GROUP):
            rows = slice(c * CHUNK, (c + 1) * CHUNK)
            v = p_ref[rows, V0:V0 + VALUE_DIM]
            kd = kds[c].astype(BF16)
            dkd_cols, ddecay_cols = [], []
            for h in range(GLA_HEADS):
                g = g_buf[c, h]
                g16 = g.astype(BF16)
                dkd_cols.append(jnp.dot(_head(v, h, HEAD_V), g16, preferred_element_type=F32))
                dv = lax.dot_general(_head(kd, h, HEAD_K), g16, (NT, ((), ())), preferred_element_type=F32)
                dp_ref[rows, V0 + h * HEAD_V:V0 + (h + 1) * HEAD_V] = dv.astype(BF16)
                if c > 0:
                    s_prev = st_ref[c - 1, h]
                else:
                    s_prev = jnp.where(step < last, stp_ref[0, h], 0.0)
                ddecay_cols.append(jnp.sum(g * s_prev, axis=0, keepdims=True))
            dkd = jnp.concatenate(dkd_cols, axis=1)
            ddecay = jnp.concatenate(ddecay_cols, axis=1)
            dp_ref[rows, K0:K0 + KEY_DIM] = (dkd * fades[c]).astype(BF16)
            e = dkd * kds[c]
            dla = ddecay * decays[c] + _cumsum_rows(tri_strict, e)
            dlogit_rows.append(dla * (1.0 / GATE_NORMALIZER) * (1.0 - _sigmoid(logits[rows])))
        dlogit = jnp.concatenate(dlogit_rows, axis=0)
        dlogit16 = dlogit.astype(BF16)
        dp_ref[:, G0:G0 + GATE_PAD] = lax.dot_general(
            dlogit16, wgu_ref[...], (NT, ((), ())), preferred_element_type=F32).astype(BF16)
        dwgu = lax.dot_general(gl, dlogit16, (TN, ((), ())), preferred_element_type=F32)
        db = jnp.sum(dlogit, axis=0, keepdims=True)

        @pl.when(step == 0)
        def _():
            dwgu_ref[...] = dwgu
            db_ref[...] = db
            dgn_ref[...] = dgn

        @pl.when(step > 0)
        def _():
            dwgu_ref[...] += dwgu
            db_ref[...] += db
            dgn_ref[...] += dgn

    rev = lambda i: (last - i, 0)
    rows = _spec((GLA_ROWS, VALUE_DIM), rev)
    const = lambda shape: _spec(shape, lambda i: (0,) * len(shape))
    st_shape = (GLA_HEADS, HEAD_V, HEAD_K)
    return pl.pallas_call(
        body, name="gla_bwd", grid=(N_GROUPS,),
        in_specs=[_spec((GLA_ROWS, PROJ_A_PAD), rev), const((GATE_PAD, KEY_DIM)), const((1, KEY_DIM)), const((1, VALUE_DIM)),
                  rows, _spec((GLA_GROUP,) + st_shape, lambda i: (last - i, 0, 0, 0)),
                  _spec((1,) + st_shape, lambda i: (jnp.maximum((last - i) * GLA_GROUP - 1, 0), 0, 0, 0)), rows],
        out_specs=[_spec((GLA_ROWS, PROJ_A_PAD), rev), const((GATE_PAD, KEY_DIM)), const((1, KEY_DIM)), const((1, VALUE_DIM))],
        out_shape=[jax.ShapeDtypeStruct((SEQ, PROJ_A_PAD), BF16), jax.ShapeDtypeStruct((GATE_PAD, KEY_DIM), F32),
                   jax.ShapeDtypeStruct((1, KEY_DIM), F32), jax.ShapeDtypeStruct((1, VALUE_DIM), F32)],
        scratch_shapes=[pltpu.VMEM(st_shape, F32), pltpu.VMEM((GLA_ROWS, VALUE_DIM), F32), pltpu.VMEM((GLA_GROUP,) + st_shape, F32)],
        compiler_params=_params(("arbitrary",)),
    )(proj, wgu, b_gate, gn, o, states, states, dog)


WGRAD_FF_TILE = D_FF // 2


CARRY_ROWS = 8
UP_ROWS = 512


def _ffn_up_mid(name, x, gamma, w_up_t, conv_w):
    def body(x_ref, g_ref, w_ref, c_ref, h_ref, gu_ref, a_ref, carry):
        @pl.when(pl.program_id(0) == 0)
        def _():
            carry[...] = jnp.zeros_like(carry)

        x_tile = x_ref[...]
        h_tile = (x_tile * _rstd(x_tile) * g_ref[...]).astype(BF16)
        h_ref[...] = h_tile
        for k in range(N_FF_COLS):
            cols = slice(k * FF_COLS, (k + 1) * FF_COLS)
            g, u = (lax.dot_general(h_tile, w_ref[p, cols, :], (NT, ((), ())), preferred_element_type=F32).astype(BF16)
                    for p in range(2))
            gu_ref[0, :, cols] = g
            gu_ref[1, :, cols] = u
            g = g.astype(F32)
            w = c_ref[:, cols]
            before = carry[:, cols]
            gc = w[2:3, :] * g + w[1:2, :] * _shift_down(g, before, 1) + w[0:1, :] * _shift_down(g, before, 2)
            a_ref[:, cols] = (gc * _sigmoid(gc) * u.astype(F32)).astype(BF16)
            carry[:, cols] = g[UP_ROWS - CARRY_ROWS:, :]

    row = _spec((UP_ROWS, D_MODEL), lambda i: (i, 0))
    return pl.pallas_call(
        body, name=name, grid=(SEQ // UP_ROWS,),
        in_specs=[row, _resident((1, D_MODEL)), _resident((2, D_FF, D_MODEL)), _resident((3, D_FF))],
        out_specs=[row, _spec((2, UP_ROWS, D_FF), lambda i: (0, i, 0)), _spec((UP_ROWS, D_FF), lambda i: (i, 0))],
        out_shape=[_act(), _act((2, SEQ, D_FF)), _act((SEQ, D_FF))], scratch_shapes=[pltpu.VMEM((CARRY_ROWS, D_FF), F32)],
        compiler_params=_params(("arbitrary",)),
    )(x, gamma, w_up_t, conv_w)


def _ffn_fwd(tag, x, gamma, w_up_t, conv_w, w_down):
    h, gu, a = _ffn_up_mid(f"ffn{tag}_up_mid", x, gamma, w_up_t, conv_w)
    return _rows_matmul(f"ffn{tag}_down", a, w_down, NN, x), (h, gu, a)


def _owner_blocks(d, rows=None):
    if rows is not None:
        d = d[:rows]
    return d.reshape((N_DEV, -1) + d.shape[-1:])


def _ffn_bwd(tag, x, gamma, w_up_t, conv_w, w_down, saved, dx, dx16, swap):
    h, gu, a = saved
    da = _rows_matmul(f"ffn{tag}_da", dx16, w_down, NT)
    d_w_down = _owner_blocks(_wgrad_cols_tn(f"ffn{tag}_dwdown", a, WGRAD_FF_TILE, dx16))
    dgu, d_conv = _ffn_mid_bwd(f"ffn{tag}_mid_bwd", gu, conv_w, da)
    d_w_up_t = _owner_blocks(_wgrad_halves_tn(f"ffn{tag}_dwup", dgu, WGRAD_FF_TILE, h))
    parts = (d_w_up_t, d_w_down)
    dx, dx16, d_gamma, *received = _sum_blocks_nn(
        f"ffn{tag}_dh", dgu, w_up_t, norm=(x, gamma, dx), swap=parts if swap else ())
    return dx, dx16, d_gamma, d_conv, parts, received


def _local_step(x, target, w, fetch=None, emit=None):
    if fetch is None:
        local = dict(a=(w.get("a_w_in"), w.get("a_w_out")), b=(w.get("b_w_in"), w.get("b_w_out")))
        for layer in range(2):
            local[f"f{layer}"] = (w["f_w_up"][layer], w["f_w_down"][layer]) if "f_w_up" in w else None
        fetch = lambda group, after: local[group]
    swap = emit is not None
    if emit is None:
        emit = lambda group, parts, received, dx: dx
    f_norm = (w["f_norm"][0:1], w["f_norm"][1:2])

    x0 = x
    a_w_in, a_w_out = fetch("a", x0)
    h0, proj = _norm_proj("a_in", x0, w["a_norm"], a_w_in)
    o, og, states = _gla_fwd(proj, w["a_w_gate_up"], w["a_b_gate"], w["a_gn"])
    x1 = _rows_matmul("a_out", og, a_w_out, NN, x0)
    up0, down0 = fetch("f0", x1)
    x2, ffn0 = _ffn_fwd(0, x1, f_norm[0], up0, w["f_conv"][0], down0)
    b_w_in, b_w_out = fetch("b", x2)
    h2, p = _norm_proj("b_in", x2, w["b_norm"], b_w_in)
    y = _sc_mid_fwd(p, w["b_conv"])
    x3 = _rows_matmul("b_out", y, b_w_out, NN, x2)
    up1, down1 = fetch("f1", x3)
    ffn1 = _ffn_up_mid("ffn1_up_mid", x3, f_norm[1], up1, w["f_conv"][1])
    loss, dx, dx16, d_final_norm = _down_loss_head(ffn1[2], down1, x3, w["final_norm"], target)

    dx, dx16, d_f_norm1, d_fconv1, parts_f1, got = _ffn_bwd(
        1, x3, f_norm[1], up1, w["f_conv"][1], down1, ffn1, dx, dx16, swap)
    dx16 = emit("f1", parts_f1, got, dx16)

    dy = _rows_matmul("b_dy", dx16, b_w_out, NT)
    d_b_w_out = _owner_blocks(_wgrad_cols_tn("b_dwout", y, OUT_TILE, dx16))
    db, dc, dhh, d_b_conv = _sc_mid_bwd(p, w["b_conv"], dy)
    dp = jnp.concatenate([db, dc, dhh], axis=1)
    parts_b = (_wgrad_cols_transposed_tn("b_dwin", h2, dp, B_SHARD), d_b_w_out)
    dx, dx16, d_b_norm, *got = _sum_cols_nt("b_dh", dp, b_w_in, norm=(x2, w["b_norm"], dx), swap=parts_b if swap else ())
    dx16 = emit("b", parts_b, got, dx16)

    dx, dx16, d_f_norm0, d_fconv0, parts_f0, got = _ffn_bwd(
        0, x1, f_norm[0], up0, w["f_conv"][0], down0, ffn0, dx, dx16, swap)
    dx16 = emit("f0", parts_f0, got, dx16)

    dog = _rows_matmul("a_dog", dx16, a_w_out, NT)
    d_a_w_out = _owner_blocks(_wgrad_cols_tn("a_dwout", og, OUT_TILE, dx16))
    dproj, d_wgu, d_b_gate, d_gn = _gla_bwd(proj, w["a_w_gate_up"], w["a_b_gate"], w["a_gn"], o, states, dog)
    parts_a = (_owner_blocks(_wgrad_cols_tn("a_dwin", dproj, PA_TILE, h0), PROJ_A), d_a_w_out)
    dx, _, d_a_norm, *got = _wide_nn("a_dh", dproj, a_w_in, norm=(x0, w["a_norm"], dx), swap=parts_a if swap else ())
    emit("a", parts_a, got, dx)

    grads = dict(
        a_norm=d_a_norm, a_w_in=parts_a[0], a_w_gate_up=d_wgu, a_b_gate=d_b_gate, a_gn=d_gn, a_w_out=parts_a[1],
        b_norm=d_b_norm, b_w_in=parts_b[0], b_conv=d_b_conv, b_w_out=parts_b[1],
        f_norm=(d_f_norm0, d_f_norm1), f_w_up=(parts_f0[0], parts_f1[0]), f_conv=(d_fconv0, d_fconv1),
        f_w_down=(parts_f0[1], parts_f1[1]), final_norm=d_final_norm)
    grads["loss"] = loss
    return dx, grads


MESH_ID = pl.DeviceIdType.MESH
ANY = pl.BlockSpec(memory_space=pl.ANY)
N_PEERS = N_DEV - 1


def _position():
    return lax.axis_index("x"), lax.axis_index("y"), lax.axis_index("c")


def _slot(px, py, pc):
    return 4 * px + 2 * py + pc


GATHER_COPIES = 8
HALF_ROWS = 16


def _gather_copies(src, out, send_sems, recv_sems, local_sems):
    n = len(src)
    to_sibling, to_x, to_y, x_on_to_y, y_on_to_x, x_to_sibling, y_to_sibling, diagonal_to_sibling = range(GATHER_COPIES)
    x, y, c = _position()
    me, sibling = (x, y, c), (x, y, 1 - c)
    x_side, y_side, diagonal = (1 - x, y), (x, 1 - y), (1 - x, 1 - y)

    def rows_of(t, half):
        rows = src[t].shape[0]
        half_rows = rows // 2 // HALF_ROWS * HALF_ROWS
        return (pl.ds(0, rows), pl.ds(0, half_rows), pl.ds(half_rows, rows - half_rows))[half]

    def copy(t, j, block, to, half=0, from_input=False):
        dst = out[t].at[_slot(*block), rows_of(t, half)]
        return pltpu.make_async_remote_copy(
            src_ref=src[t] if from_input else dst, dst_ref=dst, send_sem=send_sems.at[GATHER_COPIES * t + j],
            recv_sem=recv_sems.at[GATHER_COPIES * t + j], device_id=to, device_id_type=MESH_ID)

    mine = [pltpu.make_async_copy(src[t], out[t].at[_slot(*me)], local_sems.at[t]) for t in range(n)]
    for cp in mine:
        cp.start()
    sent = []

    def start(cp):
        cp.start()
        sent.append(cp)

    for t in range(n):
        start(copy(t, to_sibling, me, sibling, from_input=True))
        start(copy(t, to_x, me, (*x_side, c), from_input=True))
        start(copy(t, to_y, me, (*y_side, c), from_input=True))
    for t in range(n):
        copy(t, to_x, (*x_side, c), me).wait_recv()
        start(copy(t, x_on_to_y, (*x_side, c), (*y_side, c), half=1))
        start(copy(t, x_to_sibling, (*x_side, c), sibling))
        copy(t, to_y, (*y_side, c), me).wait_recv()
        start(copy(t, y_on_to_x, (*y_side, c), (*x_side, c), half=2))
        start(copy(t, y_to_sibling, (*y_side, c), sibling))
    for t in range(n):
        copy(t, x_on_to_y, (*diagonal, c), me, half=1).wait_recv()
        copy(t, y_on_to_x, (*diagonal, c), me, half=2).wait_recv()
        start(copy(t, diagonal_to_sibling, (*diagonal, c), sibling))
    for t in range(n):
        copy(t, to_sibling, sibling, me).wait_recv()
        for j, chip in ((x_to_sibling, x_side), (y_to_sibling, y_side), (diagonal_to_sibling, diagonal)):
            copy(t, j, (*chip, 1 - c), me).wait_recv()
    for cp in sent:
        cp.wait_send()
    for cp in mine:
        cp.wait()


def _all_gather(name, shards):
    n = len(shards)

    def body(*refs):
        _gather_copies(refs[:n], refs[n:2 * n], *refs[2 * n:])

    sems = pltpu.SemaphoreType.DMA((GATHER_COPIES * n,))
    return pl.pallas_call(
        body, name=name, in_specs=[ANY] * n, out_specs=[ANY] * n,
        out_shape=[jax.ShapeDtypeStruct((N_DEV,) + s.shape, s.dtype) for s in shards],
        scratch_shapes=[sems, sems, pltpu.SemaphoreType.DMA((n,))],
    )(*shards)


SIBLING_AND_NEIGHBOURS = (1, 2, 4)
SAME_CORE = (2, 4, 6)


def _flip(x, y, c, k):
    return x ^ (k >> 2), y ^ ((k >> 1) & 1), c ^ (k & 1)


N_CHIPS = N_DEV // 2


def _chip(px, py):
    return 2 * px + py


def _pair_copies(parts, received, send_sems, recv_sems):
    x, y, c = lax.axis_index("x"), lax.axis_index("y"), lax.axis_index("c")
    sibling = (x, y, 1 - c)
    copies = []
    for t in range(len(parts)):
        for q in range(N_DEV // 2):
            send = pltpu.make_async_remote_copy(
                src_ref=parts[t].at[2 * q + 1 - c], dst_ref=received[t].at[q], send_sem=send_sems.at[t, q],
                recv_sem=recv_sems.at[t, q], device_id=sibling, device_id_type=pl.DeviceIdType.MESH)
            landed = received[t].at[q]
            arrival = pltpu.make_async_remote_copy(
                src_ref=landed, dst_ref=landed, send_sem=send_sems.at[t, q], recv_sem=recv_sems.at[t, q],
                device_id=sibling, device_id_type=pl.DeviceIdType.MESH)
            copies.append((send, arrival))
    return copies


PAIR_ROWS = 1024


def _pair_add(name, part, received, side):
    _, rows, cols = part.shape
    tiles = [t for t in range(PAIR_ROWS, 0, -BF16_ROWS) if rows % t == 0]
    tr = tiles[0] if tiles else rows

    def body(side_ref, p_ref, r_ref, o_ref):
        o_ref[...] = (p_ref[...].astype(F32) + r_ref[...].astype(F32)).astype(BF16)

    tile = _spec((None, tr, cols), lambda q, i, side_ref: (q, i, 0))
    return pl.pallas_call(
        body, name=name,
        grid_spec=pltpu.PrefetchScalarGridSpec(
            num_scalar_prefetch=1, grid=(N_CHIPS, rows // tr),
            in_specs=[_spec((None, tr, cols), lambda q, i, side_ref: (2 * q + side_ref[0], i, 0)), tile], out_specs=tile),
        out_shape=jax.ShapeDtypeStruct((N_CHIPS, rows, cols), BF16), compiler_params=_params(("parallel", "parallel")),
    )(side, part, received)


def _send_copy(parts, landing, send_sems, recv_sems, t, s, k):
    x, y, c = _position()
    px, py, _ = _flip(x, y, c, k)
    return pltpu.make_async_remote_copy(
        src_ref=parts[t].at[_chip(px, py)], dst_ref=landing[t].at[_chip(x, y)], send_sem=send_sems.at[s],
        recv_sem=recv_sems.at[s], device_id=(px, py, c), device_id_type=MESH_ID)


def _send_arrival(landing, send_sems, recv_sems, t, s, k):
    x, y, c = _position()
    px, py, _ = _flip(x, y, c, k)
    landed = landing[t].at[_chip(px, py)]
    return pltpu.make_async_remote_copy(
        src_ref=landed, dst_ref=landed, send_sem=send_sems.at[s], recv_sem=recv_sems.at[s],
        device_id=(px, py, c), device_id_type=MESH_ID)


def _handshake(peers):
    x, y, c = _position()
    barrier = pltpu.get_barrier_semaphore()
    for k in peers:
        pl.semaphore_signal(barrier, inc=1, device_id=_flip(x, y, c, k), device_id_type=MESH_ID)
    pl.semaphore_wait(barrier, len(peers))


def _sequencer(name, collective_id, n_copies, body, operands, out_type):
    n_arrays = len(operands)
    return pl.kernel(
        body, out_type=out_type, mesh=plsc.ScalarSubcoreMesh(axis_name="sequencer", num_cores=1), name=name,
        scratch_types=(pltpu.SemaphoreType.DMA((n_copies,)), pltpu.SemaphoreType.DMA((n_copies,)),
                       pltpu.SemaphoreType.DMA((n_arrays,))),
        compiler_params=pltpu.CompilerParams(collective_id=collective_id))(*operands)


def _sequencer_exchange(name, collective_id, parts, after=()):
    n, n_peers, n_in = len(parts), len(SAME_CORE), len(parts) + len(after)

    def body(*refs):
        src, landing = refs[:n], refs[n_in:n_in + n]
        send_sems, recv_sems, local_sems = refs[n_in + n:]
        _handshake(SAME_CORE)
        x, y, _ = _position()
        mine = [pltpu.make_async_copy(src[t].at[_chip(x, y)], landing[t].at[_chip(x, y)], local_sems.at[t]) for t in range(n)]
        for cp in mine:
            cp.start()
        sent = [_send_copy(src, landing, send_sems, recv_sems, t, t * n_peers + j, k)
                for t in range(n) for j, k in enumerate(SAME_CORE)]
        for cp in sent:
            cp.start()
        for t in range(n):
            for j, k in enumerate(SAME_CORE):
                _send_arrival(landing, send_sems, recv_sems, t, t * n_peers + j, k).wait_recv()
        for cp in sent:
            cp.wait_send()
        for cp in mine:
            cp.wait()

    landing = [jax.ShapeDtypeStruct(p.shape, p.dtype) for p in parts]
    return _sequencer(name, collective_id, n * n_peers, body, list(parts) + list(after), landing)


def _sequencer_gather(name, collective_id, shards):
    n = len(shards)

    def body(*refs):
        _handshake(SIBLING_AND_NEIGHBOURS)
        _gather_copies(refs[:n], refs[n:2 * n], *refs[2 * n:])

    gathered = [jax.ShapeDtypeStruct((N_DEV,) + s.shape, s.dtype) for s in shards]
    return _sequencer(name, collective_id, GATHER_COPIES * n, body, shards, gathered)


ADAM_ROWS = 512
BF16_ROWS = 16


def _adam_update(w, g, m, v):
    m = ADAM_B1 * m + (1.0 - ADAM_B1) * g
    v = ADAM_B2 * v + (1.0 - ADAM_B2) * (g * g)
    m_hat = m / (1.0 - ADAM_B1 ** ADAM_STEP)
    v_hat = v / (1.0 - ADAM_B2 ** ADAM_STEP)
    delta = -ADAM_LR * (m_hat / (jnp.sqrt(v_hat) + ADAM_EPS) + ADAM_WD * w)
    return delta, m, v


def _sum_slots(ref):
    total = ref[0].astype(F32)
    for d in range(1, ref.shape[0]):
        total = total + ref[d].astype(F32)
    return total


def _adamw_sum(name, landed, w, m, v):
    layers, rows, cols = w.shape
    tiles = [t for t in range(ADAM_ROWS, 0, -BF16_ROWS) if rows % t == 0]
    tr = tiles[0] if tiles else rows
    nt = rows // tr

    def body(*refs):
        parts = refs[:layers]
        w_ref, m_ref, v_ref, g_ref, d_ref, nm_ref, nv_ref = refs[layers:]
        layer = pl.program_id(0)
        g = _sum_slots(parts[0])
        for q in range(1, layers):
            g = jnp.where(layer == q, _sum_slots(parts[q]), g)
        delta, new_m, new_v = _adam_update(w_ref[...], g, m_ref[...], v_ref[...])
        g_ref[...] = g
        d_ref[...] = delta
        nm_ref[...] = new_m
        nv_ref[...] = new_v

    def part_spec(q):
        return _spec((N_CHIPS, tr, cols), lambda l, i: (0, jnp.where(l == q, i, jnp.where(l < q, 0, nt - 1)), 0))

    tile = _spec((None, tr, cols), lambda l, i: (l, i, 0))
    out = jax.ShapeDtypeStruct((layers, rows, cols), F32)
    return pl.pallas_call(
        body, name=name, grid=(layers, nt), in_specs=[part_spec(q) for q in range(layers)] + [tile] * 3,
        out_specs=[tile] * 4, out_shape=[out] * 4, compiler_params=_params(("arbitrary", "arbitrary")),
    )(*landed, w, m, v)


def _sum_small(landed):
    def body(in_ref, out_ref):
        out_ref[...] = _sum_slots(in_ref)

    return pl.pallas_call(body, name="small_grad_sum", out_shape=jax.ShapeDtypeStruct(landed.shape[1:], F32))(landed)


def _adamw_small(arrays):
    n = len(arrays)

    def body(*refs):
        for i in range(n):
            g_ref, w_ref, m_ref, v_ref = refs[4 * i:4 * i + 4]
            d_ref, nm_ref, nv_ref = refs[4 * n + 3 * i:4 * n + 3 * i + 3]
            d_ref[...], nm_ref[...], nv_ref[...] = _adam_update(w_ref[...], g_ref[...], m_ref[...], v_ref[...])

    out = [jax.ShapeDtypeStruct(w.shape, F32) for _, w, _, _ in arrays for _ in range(3)]
    flat = pl.pallas_call(body, name="adam_small", out_shape=out)(*[a for group in arrays for a in group])
    return [tuple(flat[3 * i:3 * i + 3]) for i in range(n)]


LANES = 128
SUBLANES = 8
F_CONV_SHARD = D_FF // N_DEV
GATE_SHARD = KEY_DIM // N_DEV
NORM_SHARD = D_MODEL // N_DEV


def _tile_rows(a):
    flat = a.reshape(-1)
    size = -(-flat.shape[0] // (SUBLANES * LANES)) * SUBLANES * LANES
    return jnp.pad(flat, (0, size - flat.shape[0])).reshape(-1, LANES)


def _pack_rows(pieces):
    return jnp.concatenate([_tile_rows(p) for p in pieces], axis=0)


def _unpack_rows(packed, shapes):
    out, row = [], 0
    for shape in shapes:
        size = 1
        for s in shape:
            size *= s
        rows = -(-size // (SUBLANES * LANES)) * SUBLANES
        piece = packed[..., row:row + rows, :]
        out.append(piece.reshape(piece.shape[:-2] + (rows * LANES,))[..., :size])
        row += rows
    return out


SMALL_SHARDS = ((GATE_RANK, GATE_SHARD), (1, NORM_SHARD), (3, NORM_SHARD), (2, 3, F_CONV_SHARD))


def _unpack_small_shards(g):
    gate, b_norm, b_conv, f_conv = _unpack_rows(g, SMALL_SHARDS)
    gate = gate.reshape(N_DEV, GATE_RANK, GATE_SHARD).transpose(1, 0, 2).reshape(GATE_RANK, KEY_DIM)
    b_norm = b_norm.reshape(1, D_MODEL)
    b_conv = b_conv.reshape(N_DEV, 3, NORM_SHARD).transpose(1, 0, 2).reshape(3, D_MODEL)
    f_conv = f_conv.reshape(N_DEV, 2, 3, F_CONV_SHARD).transpose(1, 2, 0, 3).reshape(2, 3, D_FF)
    return gate, b_norm, b_conv, f_conv


SMALL_LAYOUT = (("a_norm", (1, D_MODEL)), ("a_w_gate_up", (GATE_RANK, KEY_DIM)), ("a_b_gate", (1, KEY_DIM)), ("a_gn", (1, VALUE_DIM)),
                ("b_norm", (1, D_MODEL)), ("b_conv", (3, D_MODEL)), ("f_norm0", (1, D_MODEL)), ("f_norm1", (1, D_MODEL)),
                ("f_conv0", (3, D_FF)), ("f_conv1", (3, D_FF)), ("final_norm", (1, D_MODEL)), ("loss", (1, LANES)))


def _pack_small_grads(g):
    full = dict(g)
    full["a_w_gate_up"] = g["a_w_gate_up"][:GATE_RANK]
    for layer in range(2):
        full[f"f_norm{layer}"] = g["f_norm"][layer]
        full[f"f_conv{layer}"] = g["f_conv"][layer]
    return _pack_rows([full[name] for name, _ in SMALL_LAYOUT])


def _unpack_small_grads(packed):
    pieces = _unpack_rows(packed, [shape for _, shape in SMALL_LAYOUT])
    out = {name: piece.reshape(shape) for (name, shape), piece in zip(SMALL_LAYOUT, pieces)}
    out["f_norm"] = jnp.stack([out["f_norm0"][0], out["f_norm1"][0]])
    out["f_conv"] = jnp.stack([out["f_conv0"], out["f_conv1"]])
    return out


def kernel(x, a_norm, a_w_in, a_w_gate_up, a_b_gate, a_gn, a_w_out, b_norm, b_w_in, b_conv, b_w_out, f_norm, f_w_up, f_conv, f_w_down, final_norm, loss_target, m_a_norm, m_a_w_in, m_a_w_gate_up, m_a_b_gate, m_a_gn, m_a_w_out, m_b_norm, m_b_w_in, m_b_conv, m_b_w_out, m_f_norm, m_f_w_up, m_f_conv, m_f_w_down, m_final_norm, v_a_norm, v_a_w_in, v_a_w_gate_up, v_a_b_gate, v_a_gn, v_a_w_out, v_b_norm, v_b_w_in, v_b_conv, v_b_w_out, v_f_norm, v_f_w_up, v_f_conv, v_f_w_down, v_final_norm):
    my_slot = _slot(*_position())

    transposed = lambda w: jnp.swapaxes(w, 1, 2)
    a_transposed = lambda w: w.reshape(D_MODEL, A_SHARD).T.reshape(1, A_SHARD, D_MODEL)
    a_w_in_t, f_w_up_t = a_transposed(a_w_in), transposed(f_w_up)
    first = _all_gather("weight_gather", [a_w_in_t[0].astype(BF16), a_w_out[0].astype(BF16),
                                          _pack_rows([a_w_gate_up[0], b_norm, b_conv[0], f_conv])])
    gathers, small_shards = {}, first[2]
    later = (("f0", f_w_up_t[0], f_w_down[0]), ("b", b_w_in[0], b_w_out[0]), ("f1", f_w_up_t[1], f_w_down[1]))
    for collective_id, (group, w_in, w_out) in enumerate(later):
        w_in, w_out, small_shards = lax.optimization_barrier((w_in.astype(BF16), w_out.astype(BF16), small_shards))
        gathers[group] = _sequencer_gather(f"gather_{group}", collective_id, [w_in, w_out])
    gate_full, b_norm_full, b_conv_full, f_conv_full = _unpack_small_shards(small_shards)
    a_w_in_full = jnp.pad(first[0].reshape(PROJ_A, D_MODEL), ((0, PROJ_A_PAD - PROJ_A), (0, 0)))
    weights = dict(
        a_norm=a_norm, a_w_gate_up=jnp.pad(gate_full, ((0, GATE_PAD - GATE_RANK), (0, 0))).astype(BF16), a_b_gate=a_b_gate,
        a_gn=a_gn, b_norm=b_norm_full, b_conv=b_conv_full, f_norm=f_norm, f_conv=f_conv_full,
        final_norm=final_norm.reshape(1, D_MODEL))

    def fetch(group, after):
        if group == "a":
            return a_w_in_full, first[1].reshape(D_MODEL, D_MODEL)
        w_in, w_out = gathers[group]
        if group == "b":
            return w_in, w_out.reshape(D_MODEL, D_MODEL)
        return w_in.reshape(2, D_FF, D_MODEL), w_out.reshape(D_FF, D_MODEL)

    exchanges, pending = {}, []
    exchange_ids = dict(b=3, f0=4, a=5)
    side = lax.axis_index("c").astype(jnp.int32).reshape(1)

    def emit(group, parts, received, carry):
        sums = [_pair_add(f"pair_add_{group}_{i}", part, got, side) for i, (part, got) in enumerate(zip(parts, received))]
        carry, *sums = lax.optimization_barrier((carry, *sums))
        pending.extend(sums)
        if group != "f1":
            after = list(exchanges.values())[-1][:1] if exchanges else ()
            exchanges[group] = _sequencer_exchange(f"grads_{group}", exchange_ids[group], list(pending), after)
            pending.clear()
        return carry

    dx, g = _local_step(x[0], loss_target[0], weights, fetch, emit)

    (up1, down1, d_b_in, d_b_out), (up0, down0), (d_a_in, d_a_out) = (exchanges[group] for group in ("b", "f0", "a"))
    back = lambda results: tuple(transposed(r) for r in results)
    big = dict(
        b_w_in=_adamw_sum("adam_b_w_in", [d_b_in], b_w_in, m_b_w_in, v_b_w_in),
        b_w_out=_adamw_sum("adam_b_w_out", [d_b_out], b_w_out, m_b_w_out, v_b_w_out),
        f_w_up=back(_adamw_sum("adam_f_w_up", [up0, up1], f_w_up_t, transposed(m_f_w_up), transposed(v_f_w_up))),
        f_w_down=_adamw_sum("adam_f_w_down", [down0, down1], f_w_down, m_f_w_down, v_f_w_down))
    small_packed, *updated = lax.optimization_barrier((_pack_small_grads(g), *big["f_w_down"]))
    big["f_w_down"] = tuple(updated)
    small_landed = _all_gather("small_grad_gather", [small_packed])[0]
    big.update(
        a_w_in=tuple(r.reshape(A_SHARD, D_MODEL).T.reshape(1, D_MODEL, A_SHARD) for r in _adamw_sum(
            "adam_a_w_in", [d_a_in], a_w_in_t, a_transposed(m_a_w_in), a_transposed(v_a_w_in))),
        a_w_out=_adamw_sum("adam_a_w_out", [d_a_out], a_w_out, m_a_w_out, v_a_w_out))
    small_g = _unpack_small_grads(_sum_small(small_landed))
    loss = small_g["loss"][0, 0]
    small_g["a_w_gate_up"] = lax.dynamic_slice_in_dim(small_g["a_w_gate_up"], my_slot * GATE_SHARD, GATE_SHARD, axis=1)
    small_g["b_norm"] = lax.dynamic_slice_in_dim(small_g["b_norm"], my_slot * NORM_SHARD, NORM_SHARD, axis=1)
    small_g["b_conv"] = lax.dynamic_slice_in_dim(small_g["b_conv"], my_slot * NORM_SHARD, NORM_SHARD, axis=1)
    small_g["f_conv"] = lax.dynamic_slice_in_dim(small_g["f_conv"], my_slot * F_CONV_SHARD, F_CONV_SHARD, axis=2)
    small_w = dict(
        a_norm=(a_norm, m_a_norm, v_a_norm), a_w_gate_up=(a_w_gate_up, m_a_w_gate_up, v_a_w_gate_up),
        a_b_gate=(a_b_gate, m_a_b_gate, v_a_b_gate), a_gn=(a_gn, m_a_gn, v_a_gn), b_norm=(b_norm, m_b_norm, v_b_norm),
        b_conv=(b_conv, m_b_conv, v_b_conv), f_norm=(f_norm, m_f_norm, v_f_norm), f_conv=(f_conv, m_f_conv, v_f_conv),
        final_norm=(final_norm, m_final_norm, v_final_norm))
    two_d = lambda a: a.reshape(-1, a.shape[-1])
    updates = _adamw_small([tuple(two_d(a.reshape(w.shape)) for a in (small_g[name], w, m, v)) for name, (w, m, v) in small_w.items()])
    small = {}
    for (name, (w, _, _)), update in zip(small_w.items(), updates):
        small[name] = (small_g[name].reshape(w.shape),) + tuple(u.reshape(w.shape) for u in update)

    order = ["a_norm", "a_w_in", "a_w_gate_up", "a_b_gate", "a_gn", "a_w_out", "b_norm", "b_w_in", "b_conv", "b_w_out",
             "f_norm", "f_w_up", "f_conv", "f_w_down", "final_norm"]
    results = {**big, **small}
    outputs = [loss, dx.reshape(1, SEQ, D_MODEL)]
    for kind in range(4):
        outputs += [results[name][kind] for name in order]
    return tuple(outputs)
```

```python
import jax
import jax.numpy as jnp
from jax import lax
from jax.experimental import pallas as pl
from jax.experimental.pallas import tpu as pltpu
from jax.experimental.pallas import tpu_sc as plsc

F32 = jnp.float32
BF16 = jnp.bfloat16

N_DEV = 8
SEQ = 2048
D_MODEL = 1024
CHUNK = 64
N_CHUNKS = SEQ // CHUNK
RMS_EPS = 1e-6
GLA_HEADS = 4
KEY_DIM = 512
VALUE_DIM = 1024
HEAD_K = KEY_DIM // GLA_HEADS
HEAD_V = VALUE_DIM // GLA_HEADS
GATE_RANK = 16
GATE_PAD = 128
GATE_NORMALIZER = 16.0
PROJ_A = 2 * KEY_DIM + 2 * VALUE_DIM + GATE_RANK
PROJ_A_PAD = 2 * KEY_DIM + 2 * VALUE_DIM + GATE_PAD
A_SHARD = PROJ_A // N_DEV
B_SHARD = 3 * D_MODEL // N_DEV
D_FF = 2816
ADAM_LR = 0.001
ADAM_B1 = 0.9
ADAM_B2 = 0.999
ADAM_EPS = 1e-08
ADAM_WD = 0.01
ADAM_STEP = 10
MESH_AXES = ("x", "y", "c")

VMEM_LIMIT = 56 * 1024 * 1024
ROW_CHUNK = 256
HALO = 16


def _params(sem=None, vmem=VMEM_LIMIT):
    return pltpu.CompilerParams(dimension_semantics=sem, vmem_limit_bytes=vmem)


NN = ((1,), (0,))
NT = ((1,), (1,))
TN = ((0,), (0,))


def _matmul(name, a, a_spec, b, b_spec, dims, grid, out_shape, out_spec, k_blocks=None, a_block_cols=None, res=None,
            res_spec=None, transpose_out=False, norm=None, swap=()):
    has_res = res is not None
    n_swap = len(swap)

    def body(*refs):
        a_ref, b_ref = refs[0], refs[1]
        r_ref = refs[2] if has_res else None

        def product(lhs, rhs):
            return lax.dot_general(lhs.astype(BF16), rhs, (dims, ((), ())), preferred_element_type=F32)

        if k_blocks is None:
            v = product(a_ref[...], b_ref[...])
        else:
            v = None
            for k in range(k_blocks):
                lhs = a_ref[k] if a_block_cols is None else a_ref[:, k * a_block_cols:(k + 1) * a_block_cols]
                p = product(lhs, b_ref[k])
                v = p if v is None else v + p
        if transpose_out:
            v = v.T
        if has_res:
            v = v + r_ref[...]
        if norm is None:
            o_ref = refs[2 + has_res]
            o_ref[...] = v.astype(o_ref.dtype)
            return
        n_in = 5 + has_res
        x_ref, g_ref, dxi_ref = refs[2 + has_res:n_in]
        dx_ref, dx16_ref, dg_ref = refs[n_in + n_swap:n_in + n_swap + 3]
        if n_swap:
            copies = _pair_copies(refs[n_in:n_in + n_swap], refs[n_in + n_swap + 3:n_in + 2 * n_swap + 3], *refs[-2:])

            @pl.when(pl.program_id(0) == 0)
            def _():
                for send, _ in copies:
                    send.start()

            @pl.when(pl.program_id(0) == grid[0] - 1)
            def _():
                for send, arrival in copies:
                    arrival.wait_recv()
                    send.wait_send()

        dx, dg = _norm_bwd_rows(x_ref[...], g_ref[...], v)
        dx = dxi_ref[...] + dx
        dx_ref[...] = dx
        dx16_ref[...] = dx.astype(BF16)

        @pl.when(pl.program_id(0) == 0)
        def _():
            dg_ref[...] = dg

        @pl.when(pl.program_id(0) > 0)
        def _():
            dg_ref[...] += dg

    operands = [a, b] + ([res] if has_res else [])
    in_specs = [a_spec, b_spec] + ([res_spec] if has_res else [])
    semantics = ("parallel",) * len(grid)
    scratch = []
    if norm is not None:
        vec = _spec((1, D_MODEL), lambda i: (0, 0))
        any_space = pl.BlockSpec(memory_space=pl.ANY)
        operands += list(norm) + list(swap)
        in_specs += [out_spec, vec, out_spec] + [any_space] * n_swap
        out_shape = [_act(dtype=F32), _act(), jax.ShapeDtypeStruct((1, D_MODEL), F32)]
        out_shape += [jax.ShapeDtypeStruct((N_DEV // 2,) + p.shape[1:], p.dtype) for p in swap]
        out_spec = [out_spec, out_spec, vec] + [any_space] * n_swap
        semantics = ("arbitrary",)
        if n_swap:
            scratch = [pltpu.SemaphoreType.DMA((n_swap, N_DEV // 2))] * 2
    return pl.pallas_call(
        body, name=name, grid=grid, in_specs=in_specs, out_specs=out_spec, out_shape=out_shape, scratch_shapes=scratch,
        compiler_params=_params(semantics),
    )(*operands)


def _resident(shape):
    return pl.BlockSpec(shape, lambda *_: (0,) * len(shape), pipeline_mode=pl.Buffered(1))


TM = 512
N_TM = SEQ // TM
PA_TILE = 640
N_PA = PROJ_A_PAD // PA_TILE
OUT_TILE = 256


def _spec(shape, fn):
    return pl.BlockSpec(shape, fn)


def _act(shape=(SEQ, D_MODEL), dtype=BF16):
    return jax.ShapeDtypeStruct(shape, dtype)


def _norm_proj(name, x, gamma, w):
    blocks = w.ndim == 3
    n_out = w.shape[0] * w.shape[2] if blocks else w.shape[0]

    def body(x_ref, g_ref, w_ref, h_ref, o_ref):
        x = x_ref[...]
        h = (x * _rstd(x) * g_ref[...]).astype(BF16)
        h_ref[...] = h
        if blocks:
            n = w.shape[2]
            for j in range(w.shape[0]):
                o_ref[:, j * n:(j + 1) * n] = jnp.dot(h, w_ref[j], preferred_element_type=F32).astype(BF16)
        else:
            o_ref[...] = lax.dot_general(h, w_ref[...], (NT, ((), ())), preferred_element_type=F32).astype(BF16)

    row = _spec((TM, D_MODEL), lambda i: (i, 0))
    return pl.pallas_call(
        body, name=name, grid=(N_TM,), in_specs=[row, _resident((1, D_MODEL)), _resident(w.shape)],
        out_specs=[row, _spec((TM, n_out), lambda i: (i, 0))], out_shape=[_act(), _act((SEQ, n_out))],
        compiler_params=_params(("parallel",)),
    )(x, gamma, w)


def _rows_matmul(name, a, w, dims, x=None):
    k = a.shape[1]
    n = w.shape[1] if dims == NN else w.shape[0]
    row = _spec((TM, n), lambda i: (i, 0))
    return _matmul(name, a, _spec((TM, k), lambda i: (i, 0)), w, _resident(w.shape), dims, (N_TM,),
                   _act((SEQ, n), F32 if x is not None else BF16), row, res=x, res_spec=row if x is not None else None)


def _sum_blocks_nn(name, a_blocks, w_blocks, x=None, norm=None, swap=()):
    nb, _, n = a_blocks.shape
    row = _spec((TM, D_MODEL), lambda i: (i, 0))
    return _matmul(name, a_blocks, _spec((nb, TM, n), lambda i: (0, i, 0)), w_blocks, _resident((nb, n, D_MODEL)),
                   NN, (N_TM,), _act(dtype=F32), row, k_blocks=nb, res=x, res_spec=row if x is not None else None, norm=norm, swap=swap)


def _sum_cols_nt(name, d, w_blocks, norm=None, swap=()):
    nb, _, n = w_blocks.shape
    return _matmul(name, d, _spec((TM, nb * n), lambda i: (i, 0)), w_blocks, _resident((nb, D_MODEL, n)), NT,
                   (N_TM,), _act(dtype=F32), _spec((TM, D_MODEL), lambda i: (i, 0)), k_blocks=nb, a_block_cols=n, norm=norm, swap=swap)


def _wide_nn(name, d, wt, x=None, norm=None, swap=()):
    n = wt.shape[0]
    row = _spec((TM, D_MODEL), lambda i: (i, 0))
    return _matmul(name, d, _spec((TM, n), lambda i: (i, 0)), wt, _resident((n, D_MODEL)), NN, (N_TM,),
                   _act(dtype=F32), row, res=x, res_spec=row if x is not None else None, norm=norm, swap=swap)


def _wgrad_halves_tn(name, d, n_tile, h):
    _, _, n = d.shape
    return _matmul(name, d, _spec((None, SEQ, n_tile), lambda p, j: (p, 0, j)), h, _resident((SEQ, D_MODEL)), TN,
                   (2, n // n_tile), _act((2, n, D_MODEL)), _spec((None, n_tile, D_MODEL), lambda p, j: (p, j, 0)))


def _wgrad_cols_tn(name, d, n_tile, h):
    n = d.shape[1]
    return _matmul(name, d, _spec((SEQ, n_tile), lambda j: (0, j)), h, _resident((SEQ, D_MODEL)), TN,
                   (n // n_tile,), _act((n, D_MODEL)), _spec((n_tile, D_MODEL), lambda j: (j, 0)))


def _wgrad_cols_transposed_tn(name, h, d, n_tile):
    nb = d.shape[1] // n_tile
    return _matmul(name, d, _spec((SEQ, n_tile), lambda j: (0, j)), h, _resident((SEQ, D_MODEL)), TN, (nb,),
                   _act((nb, D_MODEL, n_tile)), _spec((None, D_MODEL, n_tile), lambda j: (j, 0, 0)), transpose_out=True)


NORM_ROWS = 512


def _rstd(x):
    return lax.rsqrt(jnp.mean(x * x, axis=-1, keepdims=True) + RMS_EPS)


def _norm_bwd_rows(x, gamma, dh):
    r = _rstd(x)
    xh = x * r
    dxh = dh * gamma
    dx = r * (dxh - xh * jnp.mean(dxh * xh, axis=-1, keepdims=True))
    return dx, jnp.sum(dh * xh, axis=0, keepdims=True)


def _down_loss_head(a, w_down, x_in, gamma, target):
    def body(a_ref, w_ref, x_ref, g_ref, t_ref, loss_ref, dx_ref, dx16_ref, dg_ref):
        x = x_ref[...] + jnp.dot(a_ref[...], w_ref[...], preferred_element_type=F32)
        gamma = g_ref[...]
        err = x * _rstd(x) * gamma - t_ref[...]
        dy = err * (1.0 / D_MODEL)
        dx, dg = _norm_bwd_rows(x, gamma, dy)
        dx_ref[...] = dx
        dx16_ref[...] = dx.astype(BF16)
        part = 0.5 * jnp.sum(jnp.sum(err * err, axis=-1, keepdims=True) * (1.0 / D_MODEL), axis=0, keepdims=True)
        part = jnp.broadcast_to(part, loss_ref.shape)

        @pl.when(pl.program_id(0) == 0)
        def _():
            dg_ref[...] = dg
            loss_ref[...] = part

        @pl.when(pl.program_id(0) > 0)
        def _():
            dg_ref[...] += dg
            loss_ref[...] += part

    row = _spec((TM, D_MODEL), lambda i: (i, 0))
    vec = _spec((1, D_MODEL), lambda i: (0, 0))
    return pl.pallas_call(
        body, name="ffn1_down_loss_head", grid=(N_TM,),
        in_specs=[_spec((TM, D_FF), lambda i: (i, 0)), _resident((D_FF, D_MODEL)), row, vec, row],
        out_specs=[_spec((1, 128), lambda i: (0, 0)), row, row, vec],
        out_shape=[jax.ShapeDtypeStruct((1, 128), F32), _act(dtype=F32), _act(), jax.ShapeDtypeStruct((1, D_MODEL), F32)],
        compiler_params=_params(("arbitrary",)),
    )(a, w_down, x_in, gamma, target)


def _sigmoid(x):
    return 1.0 / (1.0 + jnp.exp(-x))


def _rows(ref, c):
    return ref[pl.ds(pl.multiple_of(c * ROW_CHUNK, ROW_CHUNK), ROW_CHUNK), :].astype(F32)


def _rows_before(ref, c):
    start = pl.multiple_of(jnp.maximum(c * ROW_CHUNK - HALO, 0), HALO)
    rows = ref[pl.ds(start, HALO), :].astype(F32)
    return jnp.where(c > 0, rows, 0.0)


def _rows_after(ref, c, n_chunks):
    start = pl.multiple_of(jnp.minimum((c + 1) * ROW_CHUNK, SEQ - HALO), HALO)
    rows = ref[pl.ds(start, HALO), :].astype(F32)
    return jnp.where(c < n_chunks - 1, rows, 0.0)


def _shift_down(z, before, n):
    return pltpu.roll(jnp.concatenate([before, z], axis=0), n, 0)[before.shape[0]:]


def _shift_up(z, after, n):
    rows = z.shape[0]
    return pltpu.roll(jnp.concatenate([z, after], axis=0), rows + HALO - n, 0)[:rows]


def _conv_rows(z, before, w):
    z1 = _shift_down(z, before, 1)
    z2 = _shift_down(z, before, 2)
    return w[2:3, :] * z + w[1:2, :] * z1 + w[0:1, :] * z2, z1, z2


def _conv_t_rows(dy, after, w):
    return w[2:3, :] * dy + w[1:2, :] * _shift_up(dy, after, 1) + w[0:1, :] * _shift_up(dy, after, 2)


N_ROW_CHUNKS = SEQ // ROW_CHUNK


FF_COLS = 256
N_FF_COLS = D_FF // FF_COLS


def _ffn_mid_bwd(name, gu, conv_w, da):
    def body(gu_ref, w_ref, da_ref, dgu_ref, dw_ref, dgc_ref):
        w = w_ref[...]

        def first(c, acc):
            g = _rows(gu_ref.at[0], c)
            u = _rows(gu_ref.at[1], c)
            d = _rows(da_ref, c)
            gc, g1, g2 = _conv_rows(g, _rows_before(gu_ref.at[0], c), w)
            sg = _sigmoid(gc)
            rows = pl.ds(pl.multiple_of(c * ROW_CHUNK, ROW_CHUNK), ROW_CHUNK)
            silu = gc * sg
            dgu_ref[1, rows, :] = (d * silu).astype(BF16)
            dgc = d * u * (sg + silu * (1.0 - sg))
            dgc_ref[rows, :] = dgc
            return (acc[0] + jnp.sum(dgc * g2, axis=0, keepdims=True), acc[1] + jnp.sum(dgc * g1, axis=0, keepdims=True),
                    acc[2] + jnp.sum(dgc * g, axis=0, keepdims=True))

        zero = jnp.zeros((1, FF_COLS), F32)
        acc = lax.fori_loop(0, N_ROW_CHUNKS, first, (zero, zero, zero))
        for r in range(3):
            dw_ref[r:r + 1, :] = acc[r]

        def second(c, carry):
            dgc = _rows(dgc_ref, c)
            dg = _conv_t_rows(dgc, _rows_after(dgc_ref, c, N_ROW_CHUNKS), w)
            dgu_ref[0, pl.ds(pl.multiple_of(c * ROW_CHUNK, ROW_CHUNK), ROW_CHUNK), :] = dg.astype(BF16)
            return carry

        lax.fori_loop(0, N_ROW_CHUNKS, second, 0)

    pair = _spec((2, SEQ, FF_COLS), lambda j: (0, 0, j))
    wspec = _spec((3, FF_COLS), lambda j: (0, j))
    return pl.pallas_call(
        body, name=name, grid=(N_FF_COLS,), in_specs=[pair, wspec, _spec((SEQ, FF_COLS), lambda j: (0, j))],
        out_specs=[pair, wspec], out_shape=[_act((2, SEQ, D_FF)), jax.ShapeDtypeStruct((3, D_FF), F32)],
        scratch_shapes=[pltpu.VMEM((SEQ, FF_COLS), F32)],
        compiler_params=_params(("parallel",)),
    )(gu, conv_w, da)


SC_COLS = 256
N_SC = D_MODEL // SC_COLS


def _sc_specs():
    return [_spec((SEQ, SC_COLS), lambda j, part=part: (0, part * N_SC + j)) for part in range(3)]


def _sc_mid_fwd(p, conv_w):
    def body(b_ref, c_ref, h_ref, w_ref, y_ref):
        w = w_ref[...]

        def chunk(c, carry):
            z = _rows(c_ref, c) * _rows(h_ref, c)
            before = _rows_before(c_ref, c) * _rows_before(h_ref, c)
            zc, _, _ = _conv_rows(z, before, w)
            y_ref[pl.ds(pl.multiple_of(c * ROW_CHUNK, ROW_CHUNK), ROW_CHUNK), :] = (_rows(b_ref, c) * zc).astype(BF16)
            return carry

        lax.fori_loop(0, N_ROW_CHUNKS, chunk, 0)

    col = _spec((SEQ, SC_COLS), lambda j: (0, j))
    return pl.pallas_call(
        body, name="sc_mid_fwd", grid=(N_SC,), in_specs=_sc_specs() + [_spec((3, SC_COLS), lambda j: (0, j))], out_specs=col,
        out_shape=jax.ShapeDtypeStruct((SEQ, D_MODEL), BF16), compiler_params=_params(("parallel",)),
    )(p, p, p, conv_w)


def _sc_mid_bwd(p, conv_w, dy):
    def body(b_ref, c_ref, h_ref, w_ref, dy_ref, db_ref, dc_ref, dh_ref, dw_ref, dzc_ref):
        w = w_ref[...]

        def first(c, acc):
            z = _rows(c_ref, c) * _rows(h_ref, c)
            before = _rows_before(c_ref, c) * _rows_before(h_ref, c)
            zc, z1, z2 = _conv_rows(z, before, w)
            d = _rows(dy_ref, c)
            rows = pl.ds(pl.multiple_of(c * ROW_CHUNK, ROW_CHUNK), ROW_CHUNK)
            db_ref[rows, :] = (d * zc).astype(BF16)
            dzc = d * _rows(b_ref, c)
            dzc_ref[rows, :] = dzc
            return (acc[0] + jnp.sum(dzc * z2, axis=0, keepdims=True), acc[1] + jnp.sum(dzc * z1, axis=0, keepdims=True),
                    acc[2] + jnp.sum(dzc * z, axis=0, keepdims=True))

        zero = jnp.zeros((1, SC_COLS), F32)
        acc = lax.fori_loop(0, N_ROW_CHUNKS, first, (zero, zero, zero))
        for r in range(3):
            dw_ref[r:r + 1, :] = acc[r]

        def second(c, carry):
            dz = _conv_t_rows(_rows(dzc_ref, c), _rows_after(dzc_ref, c, N_ROW_CHUNKS), w)
            rows = pl.ds(pl.multiple_of(c * ROW_CHUNK, ROW_CHUNK), ROW_CHUNK)
            dc_ref[rows, :] = (dz * _rows(h_ref, c)).astype(BF16)
            dh_ref[rows, :] = (dz * _rows(c_ref, c)).astype(BF16)
            return carry

        lax.fori_loop(0, N_ROW_CHUNKS, second, 0)

    col = _spec((SEQ, SC_COLS), lambda j: (0, j))
    wspec = _spec((3, SC_COLS), lambda j: (0, j))
    act = jax.ShapeDtypeStruct((SEQ, D_MODEL), BF16)
    return pl.pallas_call(
        body, name="sc_mid_bwd", grid=(N_SC,), in_specs=_sc_specs() + [wspec, col], out_specs=[col, col, col, wspec],
        out_shape=[act, act, act, jax.ShapeDtypeStruct((3, D_MODEL), F32)],
        scratch_shapes=[pltpu.VMEM((SEQ, SC_COLS), F32)], compiler_params=_params(("parallel",)),
    )(p, p, p, conv_w, dy)


GLA_GROUP = 4
GLA_ROWS = GLA_GROUP * CHUNK
N_GROUPS = N_CHUNKS // GLA_GROUP
Q0, K0, V0, R0, G0 = 0, KEY_DIM, 2 * KEY_DIM, 2 * KEY_DIM + VALUE_DIM, 2 * KEY_DIM + 2 * VALUE_DIM


def _tri(strict):
    r = lax.broadcasted_iota(jnp.int32, (CHUNK, CHUNK), 0)
    c = lax.broadcasted_iota(jnp.int32, (CHUNK, CHUNK), 1)
    return jnp.where(c < r if strict else c <= r, 1.0, 0.0).astype(F32)


def _cumsum_rows(tri, x):
    tri = tri.astype(BF16)
    total = None
    for _ in range(3):
        term = x.astype(BF16)
        x = x - term.astype(F32)
        product = jnp.dot(tri, term, preferred_element_type=F32)
        total = product if total is None else total + product
    return total


def _gate_logits(gl, wgu, b_gate):
    return jnp.dot(gl, wgu, preferred_element_type=F32) + b_gate


def _log_decay(logits):
    return (jnp.minimum(logits, 0.0) - jnp.log(1.0 + jnp.exp(-jnp.abs(logits)))) * (1.0 / GATE_NORMALIZER)


def _head(x, h, width):
    return x[:, h * width:(h + 1) * width]


def _gla_fwd(proj, wgu, b_gate, gn):
    def body(p_ref, wgu_ref, b_ref, gn_ref, o_ref, og_ref, st_ref, state):
        @pl.when(pl.program_id(0) == 0)
        def _():
            state[...] = jnp.zeros_like(state)

        tri = _tri(False)
        la = _log_decay(_gate_logits(p_ref[:, G0:G0 + GATE_PAD], wgu_ref[...], b_ref[...]))
        decays = []
        for c in range(GLA_GROUP):
            rows = slice(c * CHUNK, (c + 1) * CHUNK)
            cum = _cumsum_rows(tri, la[rows])
            tot = cum[CHUNK - 1:CHUNK, :]
            kd = (p_ref[rows, K0:K0 + KEY_DIM].astype(F32) * jnp.exp(tot - cum)).astype(BF16)
            decays.append(jnp.exp(tot))
            v = p_ref[rows, V0:V0 + VALUE_DIM]
            for h in range(GLA_HEADS):
                st_ref[c, h] = lax.dot_general(
                    _head(v, h, HEAD_V), _head(kd, h, HEAD_K), (TN, ((), ())), preferred_element_type=F32)
        for c in range(GLA_GROUP):
            for h in range(GLA_HEADS):
                s = state[h] * _head(decays[c], h, HEAD_K) + st_ref[c, h]
                state[h] = s
                st_ref[c, h] = s
        for c in range(GLA_GROUP):
            rows = slice(c * CHUNK, (c + 1) * CHUNK)
            q = (p_ref[rows, Q0:Q0 + KEY_DIM].astype(F32) * (HEAD_K ** -0.5)).astype(BF16)
            for h in range(GLA_HEADS):
                o_ref[rows, h * HEAD_V:(h + 1) * HEAD_V] = lax.dot_general(
                    _head(q, h, HEAD_K), st_ref[c, h].astype(BF16), (NT, ((), ())), preferred_element_type=F32)
        r = p_ref[:, R0:R0 + VALUE_DIM].astype(F32)
        gate = r * _sigmoid(r) * gn_ref[...]
        for h in range(GLA_HEADS):
            cols = slice(h * HEAD_V, (h + 1) * HEAD_V)
            o = o_ref[:, cols]
            og_ref[:, cols] = (o * _rstd(o) * gate[:, cols]).astype(BF16)

    rows = _spec((GLA_ROWS, VALUE_DIM), lambda i: (i, 0))
    const = lambda shape: _spec(shape, lambda i: (0,) * len(shape))
    return pl.pallas_call(
        body, name="gla_fwd", grid=(N_GROUPS,),
        in_specs=[_spec((GLA_ROWS, PROJ_A_PAD), lambda i: (i, 0)), const((GATE_PAD, KEY_DIM)), const((1, KEY_DIM)),
                  const((1, VALUE_DIM))],
        out_specs=[rows, rows, _spec((GLA_GROUP, GLA_HEADS, HEAD_V, HEAD_K), lambda i: (i, 0, 0, 0))],
        out_shape=[jax.ShapeDtypeStruct((SEQ, VALUE_DIM), F32), jax.ShapeDtypeStruct((SEQ, VALUE_DIM), BF16),
                   jax.ShapeDtypeStruct((N_CHUNKS, GLA_HEADS, HEAD_V, HEAD_K), F32)],
        scratch_shapes=[pltpu.VMEM((GLA_HEADS, HEAD_V, HEAD_K), F32)], compiler_params=_params(("arbitrary",)),
    )(proj, wgu, b_gate, gn)


def _gla_bwd(proj, wgu, b_gate, gn, o, states, dog):
    last = N_GROUPS - 1

    def body(p_ref, wgu_ref, b_ref, gn_ref, o_ref, st_ref, stp_ref, dog_ref, dp_ref, dwgu_ref, db_ref, dgn_ref, carry, do_buf,
             g_buf):
        step = pl.program_id(0)

        @pl.when(step == 0)
        def _():
            carry[...] = jnp.zeros_like(carry)

        r = p_ref[:, R0:R0 + VALUE_DIM].astype(F32)
        sr = _sigmoid(r)
        silu = r * sr
        gn_row = gn_ref[...]
        dog_rows = dog_ref[...].astype(F32)
        dn = dog_rows * silu
        dgn_cols = []
        for h in range(GLA_HEADS):
            cols = slice(h * HEAD_V, (h + 1) * HEAD_V)
            oh = o_ref[:, cols]
            rs = _rstd(oh)
            ohat = oh * rs
            dn_h = dn[:, cols]
            dgn_cols.append(jnp.sum(dn_h * ohat, axis=0, keepdims=True))
            dohat = dn_h * gn_row[:, cols]
            do_buf[:, cols] = rs * (dohat - ohat * jnp.mean(dohat * ohat, axis=-1, keepdims=True))
            n_h = ohat * gn_row[:, cols]
            dp_ref[:, R0 + h * HEAD_V:R0 + (h + 1) * HEAD_V] = (
                dog_rows[:, cols] * n_h * (sr[:, cols] * (1.0 + r[:, cols] * (1.0 - sr[:, cols])))).astype(BF16)
        dgn = jnp.concatenate(dgn_cols, axis=1)

        tri = _tri(False)
        tri_strict = _tri(True)
        gl = p_ref[:, G0:G0 + GATE_PAD]
        logits = _gate_logits(gl, wgu_ref[...], b_ref[...])
        la = _log_decay(logits)
        fades, kds, decays = [], [], []
        for c in range(GLA_GROUP):
            rows = slice(c * CHUNK, (c + 1) * CHUNK)
            cum = _cumsum_rows(tri, la[rows])
            tot = cum[CHUNK - 1:CHUNK, :]
            fades.append(jnp.exp(tot - cum))
            kds.append(p_ref[rows, K0:K0 + KEY_DIM].astype(F32) * fades[c])
            decays.append(jnp.exp(tot))
            q = (p_ref[rows, Q0:Q0 + KEY_DIM].astype(F32) * (HEAD_K ** -0.5)).astype(BF16)
            do = do_buf[rows, :].astype(BF16)
            for h in range(GLA_HEADS):
                do_h = _head(do, h, HEAD_V)
                dq = jnp.dot(do_h, st_ref[c, h].astype(BF16), preferred_element_type=F32) * (HEAD_K ** -0.5)
                dp_ref[rows, Q0 + h * HEAD_K:Q0 + (h + 1) * HEAD_K] = dq.astype(BF16)
                g_buf[c, h] = lax.dot_general(do_h, _head(q, h, HEAD_K), (TN, ((), ())), preferred_element_type=F32)
        for c in reversed(range(GLA_GROUP)):
            for h in range(GLA_HEADS):
                g = carry[h] + g_buf[c, h]
                g_buf[c, h] = g
                carry[h] = g * _head(decays[c], h, HEAD_K)
        dlogit_rows = []
        for c in range(GLA_GROUP):
            rows = slice(c * CHUNK, (c + 1) * CHUNK)
            v = p_ref[rows, V0:V0 + VALUE_DIM]
            kd = kds[c].astype(BF16)
            dkd_cols, ddecay_cols = [], []
            for h in range(GLA_HEADS):
                g = g_buf[c, h]
                g16 = g.astype(BF16)
                dkd_cols.append(jnp.dot(_head(v, h, HEAD_V), g16, preferred_element_type=F32))
                dv = lax.dot_general(_head(kd, h, HEAD_K), g16, (NT, ((), ())), preferred_element_type=F32)
                dp_ref[rows, V0 + h * HEAD_V:V0 + (h + 1) * HEAD_V] = dv.astype(BF16)
                if c > 0:
                    s_prev = st_ref[c - 1, h]
                else:
                    s_prev = jnp.where(step < last, stp_ref[0, h], 0.0)
                ddecay_cols.append(jnp.sum(g * s_prev, axis=0, keepdims=True))
            dkd = jnp.concatenate(dkd_cols, axis=1)
            ddecay = jnp.concatenate(ddecay_cols, axis=1)
            dp_ref[rows, K0:K0 + KEY_DIM] = (dkd * fades[c]).astype(BF16)
            e = dkd * kds[c]
            dla = ddecay * decays[c] + _cumsum_rows(tri_strict, e)
            dlogit_rows.append(dla * (1.0 / GATE_NORMALIZER) * (1.0 - _sigmoid(logits[rows])))
        dlogit = jnp.concatenate(dlogit_rows, axis=0)
        dlogit16 = dlogit.astype(BF16)
        dp_ref[:, G0:G0 + GATE_PAD] = lax.dot_general(
            dlogit16, wgu_ref[...], (NT, ((), ())), preferred_element_type=F32).astype(BF16)
        dwgu = lax.dot_general(gl, dlogit16, (TN, ((), ())), preferred_element_type=F32)
        db = jnp.sum(dlogit, axis=0, keepdims=True)

        @pl.when(step == 0)
        def _():
            dwgu_ref[...] = dwgu
            db_ref[...] = db
            dgn_ref[...] = dgn

        @pl.when(step > 0)
        def _():
            dwgu_ref[...] += dwgu
            db_ref[...] += db
            dgn_ref[...] += dgn

    rev = lambda i: (last - i, 0)
    rows = _spec((GLA_ROWS, VALUE_DIM), rev)
    const = lambda shape: _spec(shape, lambda i: (0,) * len(shape))
    st_shape = (GLA_HEADS, HEAD_V, HEAD_K)
    return pl.pallas_call(
        body, name="gla_bwd", grid=(N_GROUPS,),
        in_specs=[_spec((GLA_ROWS, PROJ_A_PAD), rev), const((GATE_PAD, KEY_DIM)), const((1, KEY_DIM)), const((1, VALUE_DIM)),
                  rows, _spec((GLA_GROUP,) + st_shape, lambda i: (last - i, 0, 0, 0)),
                  _spec((1,) + st_shape, lambda i: (jnp.maximum((last - i) * GLA_GROUP - 1, 0), 0, 0, 0)), rows],
        out_specs=[_spec((GLA_ROWS, PROJ_A_PAD), rev), const((GATE_PAD, KEY_DIM)), const((1, KEY_DIM)), const((1, VALUE_DIM))],
        out_shape=[jax.ShapeDtypeStruct((SEQ, PROJ_A_PAD), BF16), jax.ShapeDtypeStruct((GATE_PAD, KEY_DIM), F32),
                   jax.ShapeDtypeStruct((1, KEY_DIM), F32), jax.ShapeDtypeStruct((1, VALUE_DIM), F32)],
        scratch_shapes=[pltpu.VMEM(st_shape, F32), pltpu.VMEM((GLA_ROWS, VALUE_DIM), F32), pltpu.VMEM((GLA_GROUP,) + st_shape, F32)],
        compiler_params=_params(("arbitrary",)),
    )(proj, wgu, b_gate, gn, o, states, states, dog)


WGRAD_FF_TILE = D_FF // 2


CARRY_ROWS = 8
UP_ROWS = 512


def _ffn_up_mid(name, x, gamma, w_up_t, conv_w):
    def body(x_ref, g_ref, w_ref, c_ref, h_ref, gu_ref, a_ref, carry):
        @pl.when(pl.program_id(0) == 0)
        def _():
            carry[...] = jnp.zeros_like(carry)

        x_tile = x_ref[...]
        h_tile = (x_tile * _rstd(x_tile) * g_ref[...]).astype(BF16)
        h_ref[...] = h_tile
        for k in range(N_FF_COLS):
            cols = slice(k * FF_COLS, (k + 1) * FF_COLS)
            g, u = (lax.dot_general(h_tile, w_ref[p, cols, :], (NT, ((), ())), preferred_element_type=F32).astype(BF16)
                    for p in range(2))
            gu_ref[0, :, cols] = g
            gu_ref[1, :, cols] = u
            g = g.astype(F32)
            w = c_ref[:, cols]
            before = carry[:, cols]
            gc = w[2:3, :] * g + w[1:2, :] * _shift_down(g, before, 1) + w[0:1, :] * _shift_down(g, before, 2)
            a_ref[:, cols] = (gc * _sigmoid(gc) * u.astype(F32)).astype(BF16)
            carry[:, cols] = g[UP_ROWS - CARRY_ROWS:, :]

    row = _spec((UP_ROWS, D_MODEL), lambda i: (i, 0))
    return pl.pallas_call(
        body, name=name, grid=(SEQ // UP_ROWS,),
        in_specs=[row, _resident((1, D_MODEL)), _resident((2, D_FF, D_MODEL)), _resident((3, D_FF))],
        out_specs=[row, _spec((2, UP_ROWS, D_FF), lambda i: (0, i, 0)), _spec((UP_ROWS, D_FF), lambda i: (i, 0))],
        out_shape=[_act(), _act((2, SEQ, D_FF)), _act((SEQ, D_FF))], scratch_shapes=[pltpu.VMEM((CARRY_ROWS, D_FF), F32)],
        compiler_params=_params(("arbitrary",)),
    )(x, gamma, w_up_t, conv_w)


def _ffn_fwd(tag, x, gamma, w_up_t, conv_w, w_down):
    h, gu, a = _ffn_up_mid(f"ffn{tag}_up_mid", x, gamma, w_up_t, conv_w)
    return _rows_matmul(f"ffn{tag}_down", a, w_down, NN, x), (h, gu, a)


def _owner_blocks(d, rows=None):
    if rows is not None:
        d = d[:rows]
    return d.reshape((N_DEV, -1) + d.shape[-1:])


def _ffn_bwd(tag, x, gamma, w_up_t, conv_w, w_down, saved, dx, dx16, swap):
    h, gu, a = saved
    da = _rows_matmul(f"ffn{tag}_da", dx16, w_down, NT)
    d_w_down = _owner_blocks(_wgrad_cols_tn(f"ffn{tag}_dwdown", a, WGRAD_FF_TILE, dx16))
    dgu, d_conv = _ffn_mid_bwd(f"ffn{tag}_mid_bwd", gu, conv_w, da)
    d_w_up_t = _owner_blocks(_wgrad_halves_tn(f"ffn{tag}_dwup", dgu, WGRAD_FF_TILE, h))
    parts = (d_w_up_t, d_w_down)
    dx, dx16, d_gamma, *received = _sum_blocks_nn(
        f"ffn{tag}_dh", dgu, w_up_t, norm=(x, gamma, dx), swap=parts if swap else ())
    return dx, dx16, d_gamma, d_conv, parts, received


def _local_step(x, target, w, fetch=None, emit=None):
    if fetch is None:
        local = dict(a=(w.get("a_w_in"), w.get("a_w_out")), b=(w.get("b_w_in"), w.get("b_w_out")))
        for layer in range(2):
            local[f"f{layer}"] = (w["f_w_up"][layer], w["f_w_down"][layer]) if "f_w_up" in w else None
        fetch = lambda group, after: local[group]
    swap = emit is not None
    if emit is None:
        emit = lambda group, parts, received, dx: dx
    f_norm = (w["f_norm"][0:1], w["f_norm"][1:2])

    x0 = x
    a_w_in, a_w_out = fetch("a", x0)
    h0, proj = _norm_proj("a_in", x0, w["a_norm"], a_w_in)
    o, og, states = _gla_fwd(proj, w["a_w_gate_up"], w["a_b_gate"], w["a_gn"])
    x1 = _rows_matmul("a_out", og, a_w_out, NN, x0)
    up0, down0 = fetch("f0", x1)
    x2, ffn0 = _ffn_fwd(0, x1, f_norm[0], up0, w["f_conv"][0], down0)
    b_w_in, b_w_out = fetch("b", x2)
    h2, p = _norm_proj("b_in", x2, w["b_norm"], b_w_in)
    y = _sc_mid_fwd(p, w["b_conv"])
    x3 = _rows_matmul("b_out", y, b_w_out, NN, x2)
    up1, down1 = fetch("f1", x3)
    ffn1 = _ffn_up_mid("ffn1_up_mid", x3, f_norm[1], up1, w["f_conv"][1])
    loss, dx, dx16, d_final_norm = _down_loss_head(ffn1[2], down1, x3, w["final_norm"], target)

    dx, dx16, d_f_norm1, d_fconv1, parts_f1, got = _ffn_bwd(
        1, x3, f_norm[1], up1, w["f_conv"][1], down1, ffn1, dx, dx16, swap)
    dx16 = emit("f1", parts_f1, got, dx16)

    dy = _rows_matmul("b_dy", dx16, b_w_out, NT)
    d_b_w_out = _owner_blocks(_wgrad_cols_tn("b_dwout", y, OUT_TILE, dx16))
    db, dc, dhh, d_b_conv = _sc_mid_bwd(p, w["b_conv"], dy)
    dp = jnp.concatenate([db, dc, dhh], axis=1)
    parts_b = (_wgrad_cols_transposed_tn("b_dwin", h2, dp, B_SHARD), d_b_w_out)
    dx, dx16, d_b_norm, *got = _sum_cols_nt("b_dh", dp, b_w_in, norm=(x2, w["b_norm"], dx), swap=parts_b if swap else ())
    dx16 = emit("b", parts_b, got, dx16)

    dx, dx16, d_f_norm0, d_fconv0, parts_f0, got = _ffn_bwd(
        0, x1, f_norm[0], up0, w["f_conv"][0], down0, ffn0, dx, dx16, swap)
    dx16 = emit("f0", parts_f0, got, dx16)

    dog = _rows_matmul("a_dog", dx16, a_w_out, NT)
    d_a_w_out = _owner_blocks(_wgrad_cols_tn("a_dwout", og, OUT_TILE, dx16))
    dproj, d_wgu, d_b_gate, d_gn = _gla_bwd(proj, w["a_w_gate_up"], w["a_b_gate"], w["a_gn"], o, states, dog)
    parts_a = (_owner_blocks(_wgrad_cols_tn("a_dwin", dproj, PA_TILE, h0), PROJ_A), d_a_w_out)
    dx, _, d_a_norm, *got = _wide_nn("a_dh", dproj, a_w_in, norm=(x0, w["a_norm"], dx), swap=parts_a if swap else ())
    emit("a", parts_a, got, dx)

    grads = dict(
        a_norm=d_a_norm, a_w_in=parts_a[0], a_w_gate_up=d_wgu, a_b_gate=d_b_gate, a_gn=d_gn, a_w_out=parts_a[1],
        b_norm=d_b_norm, b_w_in=parts_b[0], b_conv=d_b_conv, b_w_out=parts_b[1],
        f_norm=(d_f_norm0, d_f_norm1), f_w_up=(parts_f0[0], parts_f1[0]), f_conv=(d_fconv0, d_fconv1),
        f_w_down=(parts_f0[1], parts_f1[1]), final_norm=d_final_norm)
    grads["loss"] = loss
    return dx, grads


MESH_ID = pl.DeviceIdType.MESH
ANY = pl.BlockSpec(memory_space=pl.ANY)
N_PEERS = N_DEV - 1


def _position():
    return lax.axis_index("x"), lax.axis_index("y"), lax.axis_index("c")


def _slot(px, py, pc):
    return 4 * px + 2 * py + pc


GATHER_COPIES = 8
HALF_ROWS = 16


def _gather_copies(src, out, send_sems, recv_sems, local_sems):
    n = len(src)
    to_sibling, to_x, to_y, x_on_to_y, y_on_to_x, x_to_sibling, y_to_sibling, diagonal_to_sibling = range(GATHER_COPIES)
    x, y, c = _position()
    me, sibling = (x, y, c), (x, y, 1 - c)
    x_side, y_side, diagonal = (1 - x, y), (x, 1 - y), (1 - x, 1 - y)

    def rows_of(t, half):
        rows = src[t].shape[0]
        half_rows = rows // 2 // HALF_ROWS * HALF_ROWS
        return (pl.ds(0, rows), pl.ds(0, half_rows), pl.ds(half_rows, rows - half_rows))[half]

    def copy(t, j, block, to, half=0, from_input=False):
        dst = out[t].at[_slot(*block), rows_of(t, half)]
        return pltpu.make_async_remote_copy(
            src_ref=src[t] if from_input else dst, dst_ref=dst, send_sem=send_sems.at[GATHER_COPIES * t + j],
            recv_sem=recv_sems.at[GATHER_COPIES * t + j], device_id=to, device_id_type=MESH_ID)

    mine = [pltpu.make_async_copy(src[t], out[t].at[_slot(*me)], local_sems.at[t]) for t in range(n)]
    for cp in mine:
        cp.start()
    sent = []

    def start(cp):
        cp.start()
        sent.append(cp)

    for t in range(n):
        start(copy(t, to_sibling, me, sibling, from_input=True))
        start(copy(t, to_x, me, (*x_side, c), from_input=True))
        start(copy(t, to_y, me, (*y_side, c), from_input=True))
    for t in range(n):
        copy(t, to_x, (*x_side, c), me).wait_recv()
        start(copy(t, x_on_to_y, (*x_side, c), (*y_side, c), half=1))
        start(copy(t, x_to_sibling, (*x_side, c), sibling))
        copy(t, to_y, (*y_side, c), me).wait_recv()
        start(copy(t, y_on_to_x, (*y_side, c), (*x_side, c), half=2))
        start(copy(t, y_to_sibling, (*y_side, c), sibling))
    for t in range(n):
        copy(t, x_on_to_y, (*diagonal, c), me, half=1).wait_recv()
        copy(t, y_on_to_x, (*diagonal, c), me, half=2).wait_recv()
        start(copy(t, diagonal_to_sibling, (*diagonal, c), sibling))
    for t in range(n):
        copy(t, to_sibling, sibling, me).wait_recv()
        for j, chip in ((x_to_sibling, x_side), (y_to_sibling, y_side), (diagonal_to_sibling, diagonal)):
            copy(t, j, (*chip, 1 - c), me).wait_recv()
    for cp in sent:
        cp.wait_send()
    for cp in mine:
        cp.wait()


def _all_gather(name, shards):
    n = len(shards)

    def body(*refs):
        _gather_copies(refs[:n], refs[n:2 * n], *refs[2 * n:])

    sems = pltpu.SemaphoreType.DMA((GATHER_COPIES * n,))
    return pl.pallas_call(
        body, name=name, in_specs=[ANY] * n, out_specs=[ANY] * n,
        out_shape=[jax.ShapeDtypeStruct((N_DEV,) + s.shape, s.dtype) for s in shards],
        scratch_shapes=[sems, sems, pltpu.SemaphoreType.DMA((n,))],
    )(*shards)


SIBLING_AND_NEIGHBOURS = (1, 2, 4)
SAME_CORE = (2, 4, 6)


def _flip(x, y, c, k):
    return x ^ (k >> 2), y ^ ((k >> 1) & 1), c ^ (k & 1)


N_CHIPS = N_DEV // 2


def _chip(px, py):
    return 2 * px + py


def _pair_copies(parts, received, send_sems, recv_sems):
    x, y, c = lax.axis_index("x"), lax.axis_index("y"), lax.axis_index("c")
    sibling = (x, y, 1 - c)
    copies = []
    for t in range(len(parts)):
        for q in range(N_DEV // 2):
            send = pltpu.make_async_remote_copy(
                src_ref=parts[t].at[2 * q + 1 - c], dst_ref=received[t].at[q], send_sem=send_sems.at[t, q],
                recv_sem=recv_sems.at[t, q], device_id=sibling, device_id_type=pl.DeviceIdType.MESH)
            landed = received[t].at[q]
            arrival = pltpu.make_async_remote_copy(
                src_ref=landed, dst_ref=landed, send_sem=send_sems.at[t, q], recv_sem=recv_sems.at[t, q],
                device_id=sibling, device_id_type=pl.DeviceIdType.MESH)
            copies.append((send, arrival))
    return copies


def _pair_add(name, parts, received, side):
    n = len(parts)

    def body(side_ref, *refs):
        for t in range(n):
            refs[2 * n + t][...] = (refs[t][...].astype(F32) + refs[n + t][...].astype(F32)).astype(BF16)

    own = [_spec((None,) + p.shape[1:], lambda q, side_ref: (2 * q + side_ref[0], 0, 0)) for p in parts]
    chip = [_spec((None,) + p.shape[1:], lambda q, side_ref: (q, 0, 0)) for p in parts]
    return pl.pallas_call(
        body, name=name,
        grid_spec=pltpu.PrefetchScalarGridSpec(num_scalar_prefetch=1, grid=(N_CHIPS,), in_specs=own + chip, out_specs=chip),
        out_shape=[jax.ShapeDtypeStruct((N_CHIPS,) + p.shape[1:], BF16) for p in parts], compiler_params=_params(("parallel",)),
    )(side, *parts, *received)


def _send_copy(parts, landing, send_sems, recv_sems, t, s, k):
    x, y, c = _position()
    px, py, _ = _flip(x, y, c, k)
    return pltpu.make_async_remote_copy(
        src_ref=parts[t].at[_chip(px, py)], dst_ref=landing[t].at[_chip(x, y)], send_sem=send_sems.at[s],
        recv_sem=recv_sems.at[s], device_id=(px, py, c), device_id_type=MESH_ID)


def _send_arrival(landing, send_sems, recv_sems, t, s, k):
    x, y, c = _position()
    px, py, _ = _flip(x, y, c, k)
    landed = landing[t].at[_chip(px, py)]
    return pltpu.make_async_remote_copy(
        src_ref=landed, dst_ref=landed, send_sem=send_sems.at[s], recv_sem=recv_sems.at[s],
        device_id=(px, py, c), device_id_type=MESH_ID)


def _handshake(peers):
    x, y, c = _position()
    barrier = pltpu.get_barrier_semaphore()
    for k in peers:
        pl.semaphore_signal(barrier, inc=1, device_id=_flip(x, y, c, k), device_id_type=MESH_ID)
    pl.semaphore_wait(barrier, len(peers))


def _sequencer(name, collective_id, n_copies, body, operands, out_type):
    n_arrays = len(operands)
    return pl.kernel(
        body, out_type=out_type, mesh=plsc.ScalarSubcoreMesh(axis_name="sequencer", num_cores=1), name=name,
        scratch_types=(pltpu.SemaphoreType.DMA((n_copies,)), pltpu.SemaphoreType.DMA((n_copies,)),
                       pltpu.SemaphoreType.DMA((n_arrays,))),
        compiler_params=pltpu.CompilerParams(collective_id=collective_id))(*operands)


def _sequencer_exchange(name, collective_id, parts, after=()):
    n, n_peers, n_in = len(parts), len(SAME_CORE), len(parts) + len(after)

    def body(*refs):
        src, landing = refs[:n], refs[n_in:n_in + n]
        send_sems, recv_sems, local_sems = refs[n_in + n:]
        _handshake(SAME_CORE)
        x, y, _ = _position()
        mine = [pltpu.make_async_copy(src[t].at[_chip(x, y)], landing[t].at[_chip(x, y)], local_sems.at[t]) for t in range(n)]
        for cp in mine:
            cp.start()
        sent = [_send_copy(src, landing, send_sems, recv_sems, t, t * n_peers + j, k)
                for t in range(n) for j, k in enumerate(SAME_CORE)]
        for cp in sent:
            cp.start()
        for t in range(n):
            for j, k in enumerate(SAME_CORE):
                _send_arrival(landing, send_sems, recv_sems, t, t * n_peers + j, k).wait_recv()
        for cp in sent:
            cp.wait_send()
        for cp in mine:
            cp.wait()

    landing = [jax.ShapeDtypeStruct(p.shape, p.dtype) for p in parts]
    return _sequencer(name, collective_id, n * n_peers, body, list(parts) + list(after), landing)


def _sequencer_gather(name, collective_id, shards):
    n = len(shards)

    def body(*refs):
        _handshake(SIBLING_AND_NEIGHBOURS)
        _gather_copies(refs[:n], refs[n:2 * n], *refs[2 * n:])

    gathered = [jax.ShapeDtypeStruct((N_DEV,) + s.shape, s.dtype) for s in shards]
    return _sequencer(name, collective_id, GATHER_COPIES * n, body, shards, gathered)


ADAM_ROWS = 512
BF16_ROWS = 16


def _adam_update(w, g, m, v):
    m = ADAM_B1 * m + (1.0 - ADAM_B1) * g
    v = ADAM_B2 * v + (1.0 - ADAM_B2) * (g * g)
    m_hat = m / (1.0 - ADAM_B1 ** ADAM_STEP)
    v_hat = v / (1.0 - ADAM_B2 ** ADAM_STEP)
    delta = -ADAM_LR * (m_hat / (jnp.sqrt(v_hat) + ADAM_EPS) + ADAM_WD * w)
    return delta, m, v


def _sum_slots(ref):
    total = ref[0].astype(F32)
    for d in range(1, ref.shape[0]):
        total = total + ref[d].astype(F32)
    return total


def _adamw_sum(name, landed, w, m, v):
    layers, rows, cols = w.shape
    tiles = [t for t in range(ADAM_ROWS, 0, -BF16_ROWS) if rows % t == 0]
    tr = tiles[0] if tiles else rows
    nt = rows // tr

    def body(*refs):
        parts = refs[:layers]
        w_ref, m_ref, v_ref, g_ref, d_ref, nm_ref, nv_ref = refs[layers:]
        layer = pl.program_id(0)
        g = _sum_slots(parts[0])
        for q in range(1, layers):
            g = jnp.where(layer == q, _sum_slots(parts[q]), g)
        delta, new_m, new_v = _adam_update(w_ref[...], g, m_ref[...], v_ref[...])
        g_ref[...] = g
        d_ref[...] = delta
        nm_ref[...] = new_m
        nv_ref[...] = new_v

    def part_spec(q):
        return _spec((N_CHIPS, tr, cols), lambda l, i: (0, jnp.where(l == q, i, jnp.where(l < q, 0, nt - 1)), 0))

    tile = _spec((None, tr, cols), lambda l, i: (l, i, 0))
    out = jax.ShapeDtypeStruct((layers, rows, cols), F32)
    return pl.pallas_call(
        body, name=name, grid=(layers, nt), in_specs=[part_spec(q) for q in range(layers)] + [tile] * 3,
        out_specs=[tile] * 4, out_shape=[out] * 4, compiler_params=_params(("arbitrary", "arbitrary")),
    )(*landed, w, m, v)


def _sum_small(landed):
    def body(in_ref, out_ref):
        out_ref[...] = _sum_slots(in_ref)

    return pl.pallas_call(body, name="small_grad_sum", out_shape=jax.ShapeDtypeStruct(landed.shape[1:], F32))(landed)


def _adamw_small(arrays):
    n = len(arrays)

    def body(*refs):
        for i in range(n):
            g_ref, w_ref, m_ref, v_ref = refs[4 * i:4 * i + 4]
            d_ref, nm_ref, nv_ref = refs[4 * n + 3 * i:4 * n + 3 * i + 3]
            d_ref[...], nm_ref[...], nv_ref[...] = _adam_update(w_ref[...], g_ref[...], m_ref[...], v_ref[...])

    out = [jax.ShapeDtypeStruct(w.shape, F32) for _, w, _, _ in arrays for _ in range(3)]
    flat = pl.pallas_call(body, name="adam_small", out_shape=out)(*[a for group in arrays for a in group])
    return [tuple(flat[3 * i:3 * i + 3]) for i in range(n)]


LANES = 128
SUBLANES = 8
F_CONV_SHARD = D_FF // N_DEV
GATE_SHARD = KEY_DIM // N_DEV
NORM_SHARD = D_MODEL // N_DEV


def _tile_rows(a):
    flat = a.reshape(-1)
    size = -(-flat.shape[0] // (SUBLANES * LANES)) * SUBLANES * LANES
    return jnp.pad(flat, (0, size - flat.shape[0])).reshape(-1, LANES)


def _pack_rows(pieces):
    return jnp.concatenate([_tile_rows(p) for p in pieces], axis=0)


def _unpack_rows(packed, shapes):
    out, row = [], 0
    for shape in shapes:
        size = 1
        for s in shape:
            size *= s
        rows = -(-size // (SUBLANES * LANES)) * SUBLANES
        piece = packed[..., row:row + rows, :]
        out.append(piece.reshape(piece.shape[:-2] + (rows * LANES,))[..., :size])
        row += rows
    return out


SMALL_SHARDS = ((GATE_RANK, GATE_SHARD), (1, NORM_SHARD), (3, NORM_SHARD), (2, 3, F_CONV_SHARD))


def _unpack_small_shards(g):
    gate, b_norm, b_conv, f_conv = _unpack_rows(g, SMALL_SHARDS)
    gate = gate.reshape(N_DEV, GATE_RANK, GATE_SHARD).transpose(1, 0, 2).reshape(GATE_RANK, KEY_DIM)
    b_norm = b_norm.reshape(1, D_MODEL)
    b_conv = b_conv.reshape(N_DEV, 3, NORM_SHARD).transpose(1, 0, 2).reshape(3, D_MODEL)
    f_conv = f_conv.reshape(N_DEV, 2, 3, F_CONV_SHARD).transpose(1, 2, 0, 3).reshape(2, 3, D_FF)
    return gate, b_norm, b_conv, f_conv


SMALL_LAYOUT = (("a_norm", (1, D_MODEL)), ("a_w_gate_up", (GATE_RANK, KEY_DIM)), ("a_b_gate", (1, KEY_DIM)), ("a_gn", (1, VALUE_DIM)),
                ("b_norm", (1, D_MODEL)), ("b_conv", (3, D_MODEL)), ("f_norm0", (1, D_MODEL)), ("f_norm1", (1, D_MODEL)),
                ("f_conv0", (3, D_FF)), ("f_conv1", (3, D_FF)), ("final_norm", (1, D_MODEL)), ("loss", (1, LANES)))


def _pack_small_grads(g):
    full = dict(g)
    full["a_w_gate_up"] = g["a_w_gate_up"][:GATE_RANK]
    for layer in range(2):
        full[f"f_norm{layer}"] = g["f_norm"][layer]
        full[f"f_conv{layer}"] = g["f_conv"][layer]
    return _pack_rows([full[name] for name, _ in SMALL_LAYOUT])


def _unpack_small_grads(packed):
    pieces = _unpack_rows(packed, [shape for _, shape in SMALL_LAYOUT])
    out = {name: piece.reshape(shape) for (name, shape), piece in zip(SMALL_LAYOUT, pieces)}
    out["f_norm"] = jnp.stack([out["f_norm0"][0], out["f_norm1"][0]])
    out["f_conv"] = jnp.stack([out["f_conv0"], out["f_conv1"]])
    return out


def kernel(x, a_norm, a_w_in, a_w_gate_up, a_b_gate, a_gn, a_w_out, b_norm, b_w_in, b_conv, b_w_out, f_norm, f_w_up, f_conv, f_w_down, final_norm, loss_target, m_a_norm, m_a_w_in, m_a_w_gate_up, m_a_b_gate, m_a_gn, m_a_w_out, m_b_norm, m_b_w_in, m_b_conv, m_b_w_out, m_f_norm, m_f_w_up, m_f_conv, m_f_w_down, m_final_norm, v_a_norm, v_a_w_in, v_a_w_gate_up, v_a_b_gate, v_a_gn, v_a_w_out, v_b_norm, v_b_w_in, v_b_conv, v_b_w_out, v_f_norm, v_f_w_up, v_f_conv, v_f_w_down, v_final_norm):
    my_slot = _slot(*_position())

    transposed = lambda w: jnp.swapaxes(w, 1, 2)
    a_transposed = lambda w: w.reshape(D_MODEL, A_SHARD).T.reshape(1, A_SHARD, D_MODEL)
    a_w_in_t, f_w_up_t = a_transposed(a_w_in), transposed(f_w_up)
    first = _all_gather("weight_gather", [a_w_in_t[0].astype(BF16), a_w_out[0].astype(BF16),
                                          _pack_rows([a_w_gate_up[0], b_norm, b_conv[0], f_conv])])
    gathers, small_shards = {}, first[2]
    later = (("f0", f_w_up_t[0], f_w_down[0]), ("b", b_w_in[0], b_w_out[0]), ("f1", f_w_up_t[1], f_w_down[1]))
    for collective_id, (group, w_in, w_out) in enumerate(later):
        w_in, w_out, small_shards = lax.optimization_barrier((w_in.astype(BF16), w_out.astype(BF16), small_shards))
        gathers[group] = _sequencer_gather(f"gather_{group}", collective_id, [w_in, w_out])
    gate_full, b_norm_full, b_conv_full, f_conv_full = _unpack_small_shards(small_shards)
    a_w_in_full = jnp.pad(first[0].reshape(PROJ_A, D_MODEL), ((0, PROJ_A_PAD - PROJ_A), (0, 0)))
    weights = dict(
        a_norm=a_norm, a_w_gate_up=jnp.pad(gate_full, ((0, GATE_PAD - GATE_RANK), (0, 0))).astype(BF16), a_b_gate=a_b_gate,
        a_gn=a_gn, b_norm=b_norm_full, b_conv=b_conv_full, f_norm=f_norm, f_conv=f_conv_full,
        final_norm=final_norm.reshape(1, D_MODEL))

    def fetch(group, after):
        if group == "a":
            return a_w_in_full, first[1].reshape(D_MODEL, D_MODEL)
        w_in, w_out = gathers[group]
        if group == "b":
            return w_in, w_out.reshape(D_MODEL, D_MODEL)
        return w_in.reshape(2, D_FF, D_MODEL), w_out.reshape(D_FF, D_MODEL)

    exchanges, pending = {}, []
    exchange_ids = dict(b=3, f0=4, a=5)
    side = lax.axis_index("c").astype(jnp.int32).reshape(1)

    def emit(group, parts, received, carry):
        sums = _pair_add(f"pair_add_{group}", parts, received, side)
        carry, *sums = lax.optimization_barrier((carry, *sums))
        pending.extend(sums)
        if group != "f1":
            after = list(exchanges.values())[-1][:1] if exchanges else ()
            exchanges[group] = _sequencer_exchange(f"grads_{group}", exchange_ids[group], list(pending), after)
            pending.clear()
        return carry

    dx, g = _local_step(x[0], loss_target[0], weights, fetch, emit)

    (up1, down1, d_b_in, d_b_out), (up0, down0), (d_a_in, d_a_out) = (exchanges[group] for group in ("b", "f0", "a"))
    back = lambda results: tuple(transposed(r) for r in results)
    big = dict(
        b_w_in=_adamw_sum("adam_b_w_in", [d_b_in], b_w_in, m_b_w_in, v_b_w_in),
        b_w_out=_adamw_sum("adam_b_w_out", [d_b_out], b_w_out, m_b_w_out, v_b_w_out),
        f_w_up=back(_adamw_sum("adam_f_w_up", [up0, up1], f_w_up_t, transposed(m_f_w_up), transposed(v_f_w_up))),
        f_w_down=_adamw_sum("adam_f_w_down", [down0, down1], f_w_down, m_f_w_down, v_f_w_down))
    small_packed, *updated = lax.optimization_barrier((_pack_small_grads(g), *big["f_w_down"]))
    big["f_w_down"] = tuple(updated)
    small_landed = _all_gather("small_grad_gather", [small_packed])[0]
    big.update(
        a_w_in=tuple(r.reshape(A_SHARD, D_MODEL).T.reshape(1, D_MODEL, A_SHARD) for r in _adamw_sum(
            "adam_a_w_in", [d_a_in], a_w_in_t, a_transposed(m_a_w_in), a_transposed(v_a_w_in))),
        a_w_out=_adamw_sum("adam_a_w_out", [d_a_out], a_w_out, m_a_w_out, v_a_w_out))
    small_g = _unpack_small_grads(_sum_small(small_landed))
    loss = small_g["loss"][0, 0]
    small_g["a_w_gate_up"] = lax.dynamic_slice_in_dim(small_g["a_w_gate_up"], my_slot * GATE_SHARD, GATE_SHARD, axis=1)
    small_g["b_norm"] = lax.dynamic_slice_in_dim(small_g["b_norm"], my_slot * NORM_SHARD, NORM_SHARD, axis=1)
    small_g["b_conv"] = lax.dynamic_slice_in_dim(small_g["b_conv"], my_slot * NORM_SHARD, NORM_SHARD, axis=1)
    small_g["f_conv"] = lax.dynamic_slice_in_dim(small_g["f_conv"], my_slot * F_CONV_SHARD, F_CONV_SHARD, axis=2)
    small_w = dict(
        a_norm=(a_norm, m_a_norm, v_a_norm), a_w_gate_up=(a_w_gate_up, m_a_w_gate_up, v_a_w_gate_up),
        a_b_gate=(a_b_gate, m_a_b_gate, v_a_b_gate), a_gn=(a_gn, m_a_gn, v_a_gn), b_norm=(b_norm, m_b_norm, v_b_norm),
        b_conv=(b_conv, m_b_conv, v_b_conv), f_norm=(f_norm, m_f_norm, v_f_norm), f_conv=(f_conv, m_f_conv, v_f_conv),
        final_norm=(final_norm, m_final_norm, v_final_norm))
    two_d = lambda a: a.reshape(-1, a.shape[-1])
    updates = _adamw_small([tuple(two_d(a.reshape(w.shape)) for a in (small_g[name], w, m, v)) for name, (w, m, v) in small_w.items()])
    small = {}
    for (name, (w, _, _)), update in zip(small_w.items(), updates):
        small[name] = (small_g[name].reshape(w.shape),) + tuple(u.reshape(w.shape) for u in update)

    order = ["a_norm", "a_w_in", "a_w_gate_up", "a_b_gate", "a_gn", "a_w_out", "b_norm", "b_w_in", "b_conv", "b_w_out",
             "f_norm", "f_w_up", "f_conv", "f_w_down", "final_norm"]
    results = {**big, **small}
    outputs = [loss, dx.reshape(1, SEQ, D_MODEL)]
    for kind in range(4):
        outputs += [results[name][kind] for name in order]
    return tuple(outputs)
```

```python
import jax
import jax.numpy as jnp
from jax import lax
from jax.experimental import pallas as pl
from jax.experimental.pallas import tpu as pltpu
from jax.experimental.pallas import tpu_sc as plsc

F32 = jnp.float32
BF16 = jnp.bfloat16

N_DEV = 8
SEQ = 2048
D_MODEL = 1024
CHUNK = 64
N_CHUNKS = SEQ // CHUNK
RMS_EPS = 1e-6
GLA_HEADS = 4
KEY_DIM = 512
VALUE_DIM = 1024
HEAD_K = KEY_DIM // GLA_HEADS
HEAD_V = VALUE_DIM // GLA_HEADS
GATE_RANK = 16
GATE_PAD = 128
GATE_NORMALIZER = 16.0
PROJ_A = 2 * KEY_DIM + 2 * VALUE_DIM + GATE_RANK
PROJ_A_PAD = 2 * KEY_DIM + 2 * VALUE_DIM + GATE_PAD
A_SHARD = PROJ_A // N_DEV
B_SHARD = 3 * D_MODEL // N_DEV
D_FF = 2816
ADAM_LR = 0.001
ADAM_B1 = 0.9
ADAM_B2 = 0.999
ADAM_EPS = 1e-08
ADAM_WD = 0.01
ADAM_STEP = 10
MESH_AXES = ("x", "y", "c")

VMEM_LIMIT = 56 * 1024 * 1024
ROW_CHUNK = 256
HALO = 16


def _params(sem=None, vmem=VMEM_LIMIT):
    return pltpu.CompilerParams(dimension_semantics=sem, vmem_limit_bytes=vmem)


NN = ((1,), (0,))
NT = ((1,), (1,))
TN = ((0,), (0,))


def _matmul(name, a, a_spec, b, b_spec, dims, grid, out_shape, out_spec, k_blocks=None, a_block_cols=None, res=None,
            res_spec=None, transpose_out=False, norm=None, swap=()):
    has_res = res is not None
    n_swap = len(swap)

    def body(*refs):
        a_ref, b_ref = refs[0], refs[1]
        r_ref = refs[2] if has_res else None

        def product(lhs, rhs):
            return lax.dot_general(lhs.astype(BF16), rhs, (dims, ((), ())), preferred_element_type=F32)

        if k_blocks is None:
            v = product(a_ref[...], b_ref[...])
        else:
            v = None
            for k in range(k_blocks):
                lhs = a_ref[k] if a_block_cols is None else a_ref[:, k * a_block_cols:(k + 1) * a_block_cols]
                p = product(lhs, b_ref[k])
                v = p if v is None else v + p
        if transpose_out:
            v = v.T
        if has_res:
            v = v + r_ref[...]
        if norm is None:
            o_ref = refs[2 + has_res]
            o_ref[...] = v.astype(o_ref.dtype)
            return
        n_in = 5 + has_res
        x_ref, g_ref, dxi_ref = refs[2 + has_res:n_in]
        dx_ref, dx16_ref, dg_ref = refs[n_in + n_swap:n_in + n_swap + 3]
        if n_swap:
            copies = _pair_copies(refs[n_in:n_in + n_swap], refs[n_in + n_swap + 3:n_in + 2 * n_swap + 3], *refs[-2:])

            @pl.when(pl.program_id(0) == 0)
            def _():
                for send, _ in copies:
                    send.start()

            @pl.when(pl.program_id(0) == grid[0] - 1)
            def _():
                for send, arrival in copies:
                    arrival.wait_recv()
                    send.wait_send()

        dx, dg = _norm_bwd_rows(x_ref[...], g_ref[...], v)
        dx = dxi_ref[...] + dx
        dx_ref[...] = dx
        dx16_ref[...] = dx.astype(BF16)

        @pl.when(pl.program_id(0) == 0)
        def _():
            dg_ref[...] = dg

        @pl.when(pl.program_id(0) > 0)
        def _():
            dg_ref[...] += dg

    operands = [a, b] + ([res] if has_res else [])
    in_specs = [a_spec, b_spec] + ([res_spec] if has_res else [])
    semantics = ("parallel",) * len(grid)
    scratch = []
    if norm is not None:
        vec = _spec((1, D_MODEL), lambda i: (0, 0))
        any_space = pl.BlockSpec(memory_space=pl.ANY)
        operands += list(norm) + list(swap)
        in_specs += [out_spec, vec, out_spec] + [any_space] * n_swap
        out_shape = [_act(dtype=F32), _act(), jax.ShapeDtypeStruct((1, D_MODEL), F32)]
        out_shape += [jax.ShapeDtypeStruct((N_DEV // 2,) + p.shape[1:], p.dtype) for p in swap]
        out_spec = [out_spec, out_spec, vec] + [any_space] * n_swap
        semantics = ("arbitrary",)
        if n_swap:
            scratch = [pltpu.SemaphoreType.DMA((n_swap, N_DEV // 2))] * 2
    return pl.pallas_call(
        body, name=name, grid=grid, in_specs=in_specs, out_specs=out_spec, out_shape=out_shape, scratch_shapes=scratch,
        compiler_params=_params(semantics),
    )(*operands)


def _resident(shape):
    return pl.BlockSpec(shape, lambda *_: (0,) * len(shape), pipeline_mode=pl.Buffered(1))


TM = 512
N_TM = SEQ // TM
PA_TILE = 640
N_PA = PROJ_A_PAD // PA_TILE
OUT_TILE = 256


def _spec(shape, fn):
    return pl.BlockSpec(shape, fn)


def _act(shape=(SEQ, D_MODEL), dtype=BF16):
    return jax.ShapeDtypeStruct(shape, dtype)


def _norm_proj(name, x, gamma, w):
    blocks = w.ndim == 3
    n_out = w.shape[0] * w.shape[2] if blocks else w.shape[0]

    def body(x_ref, g_ref, w_ref, h_ref, o_ref):
        x = x_ref[...]
        h = (x * _rstd(x) * g_ref[...]).astype(BF16)
        h_ref[...] = h
        if blocks:
            n = w.shape[2]
            for j in range(w.shape[0]):
                o_ref[:, j * n:(j + 1) * n] = jnp.dot(h, w_ref[j], preferred_element_type=F32).astype(BF16)
        else:
            o_ref[...] = lax.dot_general(h, w_ref[...], (NT, ((), ())), preferred_element_type=F32).astype(BF16)

    row = _spec((TM, D_MODEL), lambda i: (i, 0))
    return pl.pallas_call(
        body, name=name, grid=(N_TM,), in_specs=[row, _resident((1, D_MODEL)), _resident(w.shape)],
        out_specs=[row, _spec((TM, n_out), lambda i: (i, 0))], out_shape=[_act(), _act((SEQ, n_out))],
        compiler_params=_params(("parallel",)),
    )(x, gamma, w)


def _rows_matmul(name, a, w, dims, x=None):
    k = a.shape[1]
    n = w.shape[1] if dims == NN else w.shape[0]
    row = _spec((TM, n), lambda i: (i, 0))
    return _matmul(name, a, _spec((TM, k), lambda i: (i, 0)), w, _resident(w.shape), dims, (N_TM,),
                   _act((SEQ, n), F32 if x is not None else BF16), row, res=x, res_spec=row if x is not None else None)


def _sum_blocks_nn(name, a_blocks, w_blocks, x=None, norm=None, swap=()):
    nb, _, n = a_blocks.shape
    row = _spec((TM, D_MODEL), lambda i: (i, 0))
    return _matmul(name, a_blocks, _spec((nb, TM, n), lambda i: (0, i, 0)), w_blocks, _resident((nb, n, D_MODEL)),
                   NN, (N_TM,), _act(dtype=F32), row, k_blocks=nb, res=x, res_spec=row if x is not None else None, norm=norm, swap=swap)


def _sum_cols_nt(name, d, w_blocks, norm=None, swap=()):
    nb, _, n = w_blocks.shape
    return _matmul(name, d, _spec((TM, nb * n), lambda i: (i, 0)), w_blocks, _resident((nb, D_MODEL, n)), NT,
                   (N_TM,), _act(dtype=F32), _spec((TM, D_MODEL), lambda i: (i, 0)), k_blocks=nb, a_block_cols=n, norm=norm, swap=swap)


def _wide_nn(name, d, wt, x=None, norm=None, swap=()):
    n = wt.shape[0]
    row = _spec((TM, D_MODEL), lambda i: (i, 0))
    return _matmul(name, d, _spec((TM, n), lambda i: (i, 0)), wt, _resident((n, D_MODEL)), NN, (N_TM,),
                   _act(dtype=F32), row, res=x, res_spec=row if x is not None else None, norm=norm, swap=swap)


def _wgrad_halves_tn(name, d, n_tile, h):
    _, _, n = d.shape
    return _matmul(name, d, _spec((None, SEQ, n_tile), lambda p, j: (p, 0, j)), h, _resident((SEQ, D_MODEL)), TN,
                   (2, n // n_tile), _act((2, n, D_MODEL)), _spec((None, n_tile, D_MODEL), lambda p, j: (p, j, 0)))


def _wgrad_cols_tn(name, d, n_tile, h):
    n = d.shape[1]
    return _matmul(name, d, _spec((SEQ, n_tile), lambda j: (0, j)), h, _resident((SEQ, D_MODEL)), TN,
                   (n // n_tile,), _act((n, D_MODEL)), _spec((n_tile, D_MODEL), lambda j: (j, 0)))


def _wgrad_cols_transposed_tn(name, h, d, n_tile):
    nb = d.shape[1] // n_tile
    return _matmul(name, d, _spec((SEQ, n_tile), lambda j: (0, j)), h, _resident((SEQ, D_MODEL)), TN, (nb,),
                   _act((nb, D_MODEL, n_tile)), _spec((None, D_MODEL, n_tile), lambda j: (j, 0, 0)), transpose_out=True)


NORM_ROWS = 512


def _rstd(x):
    return lax.rsqrt(jnp.mean(x * x, axis=-1, keepdims=True) + RMS_EPS)


def _norm_bwd_rows(x, gamma, dh):
    r = _rstd(x)
    xh = x * r
    dxh = dh * gamma
    dx = r * (dxh - xh * jnp.mean(dxh * xh, axis=-1, keepdims=True))
    return dx, jnp.sum(dh * xh, axis=0, keepdims=True)


def _down_loss_head(a, w_down, x_in, gamma, target):
    def body(a_ref, w_ref, x_ref, g_ref, t_ref, loss_ref, dx_ref, dx16_ref, dg_ref):
        x = x_ref[...] + jnp.dot(a_ref[...], w_ref[...], preferred_element_type=F32)
        gamma = g_ref[...]
        err = x * _rstd(x) * gamma - t_ref[...]
        dy = err * (1.0 / D_MODEL)
        dx, dg = _norm_bwd_rows(x, gamma, dy)
        dx_ref[...] = dx
        dx16_ref[...] = dx.astype(BF16)
        part = 0.5 * jnp.sum(jnp.sum(err * err, axis=-1, keepdims=True) * (1.0 / D_MODEL), axis=0, keepdims=True)
        part = jnp.broadcast_to(part, loss_ref.shape)

        @pl.when(pl.program_id(0) == 0)
        def _():
            dg_ref[...] = dg
            loss_ref[...] = part

        @pl.when(pl.program_id(0) > 0)
        def _():
            dg_ref[...] += dg
            loss_ref[...] += part

    row = _spec((TM, D_MODEL), lambda i: (i, 0))
    vec = _spec((1, D_MODEL), lambda i: (0, 0))
    return pl.pallas_call(
        body, name="ffn1_down_loss_head", grid=(N_TM,),
        in_specs=[_spec((TM, D_FF), lambda i: (i, 0)), _resident((D_FF, D_MODEL)), row, vec, row],
        out_specs=[_spec((1, 128), lambda i: (0, 0)), row, row, vec],
        out_shape=[jax.ShapeDtypeStruct((1, 128), F32), _act(dtype=F32), _act(), jax.ShapeDtypeStruct((1, D_MODEL), F32)],
        compiler_params=_params(("arbitrary",)),
    )(a, w_down, x_in, gamma, target)


def _sigmoid(x):
    return 1.0 / (1.0 + jnp.exp(-x))


def _rows(ref, c):
    return ref[pl.ds(pl.multiple_of(c * ROW_CHUNK, ROW_CHUNK), ROW_CHUNK), :].astype(F32)


def _rows_before(ref, c):
    start = pl.multiple_of(jnp.maximum(c * ROW_CHUNK - HALO, 0), HALO)
    rows = ref[pl.ds(start, HALO), :].astype(F32)
    return jnp.where(c > 0, rows, 0.0)


def _rows_after(ref, c, n_chunks):
    start = pl.multiple_of(jnp.minimum((c + 1) * ROW_CHUNK, SEQ - HALO), HALO)
    rows = ref[pl.ds(start, HALO), :].astype(F32)
    return jnp.where(c < n_chunks - 1, rows, 0.0)


def _shift_down(z, before, n):
    return pltpu.roll(jnp.concatenate([before, z], axis=0), n, 0)[before.shape[0]:]


def _shift_up(z, after, n):
    rows = z.shape[0]
    return pltpu.roll(jnp.concatenate([z, after], axis=0), rows + HALO - n, 0)[:rows]


def _conv_rows(z, before, w):
    z1 = _shift_down(z, before, 1)
    z2 = _shift_down(z, before, 2)
    return w[2:3, :] * z + w[1:2, :] * z1 + w[0:1, :] * z2, z1, z2


def _conv_t_rows(dy, after, w):
    return w[2:3, :] * dy + w[1:2, :] * _shift_up(dy, after, 1) + w[0:1, :] * _shift_up(dy, after, 2)


N_ROW_CHUNKS = SEQ // ROW_CHUNK


FF_COLS = 256
N_FF_COLS = D_FF // FF_COLS


def _ffn_mid_bwd(name, gu, conv_w, da):
    def body(gu_ref, w_ref, da_ref, dgu_ref, dw_ref, dgc_ref):
        w = w_ref[...]

        def first(c, acc):
            g = _rows(gu_ref.at[0], c)
            u = _rows(gu_ref.at[1], c)
            d = _rows(da_ref, c)
            gc, g1, g2 = _conv_rows(g, _rows_before(gu_ref.at[0], c), w)
            sg = _sigmoid(gc)
            rows = pl.ds(pl.multiple_of(c * ROW_CHUNK, ROW_CHUNK), ROW_CHUNK)
            silu = gc * sg
            dgu_ref[1, rows, :] = (d * silu).astype(BF16)
            dgc = d * u * (sg + silu * (1.0 - sg))
            dgc_ref[rows, :] = dgc
            return (acc[0] + jnp.sum(dgc * g2, axis=0, keepdims=True), acc[1] + jnp.sum(dgc * g1, axis=0, keepdims=True),
                    acc[2] + jnp.sum(dgc * g, axis=0, keepdims=True))

        zero = jnp.zeros((1, FF_COLS), F32)
        acc = lax.fori_loop(0, N_ROW_CHUNKS, first, (zero, zero, zero))
        for r in range(3):
            dw_ref[r:r + 1, :] = acc[r]

        def second(c, carry):
            dgc = _rows(dgc_ref, c)
            dg = _conv_t_rows(dgc, _rows_after(dgc_ref, c, N_ROW_CHUNKS), w)
            dgu_ref[0, pl.ds(pl.multiple_of(c * ROW_CHUNK, ROW_CHUNK), ROW_CHUNK), :] = dg.astype(BF16)
            return carry

        lax.fori_loop(0, N_ROW_CHUNKS, second, 0)

    pair = _spec((2, SEQ, FF_COLS), lambda j: (0, 0, j))
    wspec = _spec((3, FF_COLS), lambda j: (0, j))
    return pl.pallas_call(
        body, name=name, grid=(N_FF_COLS,), in_specs=[pair, wspec, _spec((SEQ, FF_COLS), lambda j: (0, j))],
        out_specs=[pair, wspec], out_shape=[_act((2, SEQ, D_FF)), jax.ShapeDtypeStruct((3, D_FF), F32)],
        scratch_shapes=[pltpu.VMEM((SEQ, FF_COLS), F32)],
        compiler_params=_params(("parallel",)),
    )(gu, conv_w, da)


SC_COLS = 256
N_SC = D_MODEL // SC_COLS


def _sc_specs():
    return [_spec((SEQ, SC_COLS), lambda j, part=part: (0, part * N_SC + j)) for part in range(3)]


def _sc_mid_fwd(p, conv_w):
    def body(b_ref, c_ref, h_ref, w_ref, y_ref):
        w = w_ref[...]

        def chunk(c, carry):
            z = _rows(c_ref, c) * _rows(h_ref, c)
            before = _rows_before(c_ref, c) * _rows_before(h_ref, c)
            zc, _, _ = _conv_rows(z, before, w)
            y_ref[pl.ds(pl.multiple_of(c * ROW_CHUNK, ROW_CHUNK), ROW_CHUNK), :] = (_rows(b_ref, c) * zc).astype(BF16)
            return carry

        lax.fori_loop(0, N_ROW_CHUNKS, chunk, 0)

    col = _spec((SEQ, SC_COLS), lambda j: (0, j))
    return pl.pallas_call(
        body, name="sc_mid_fwd", grid=(N_SC,), in_specs=_sc_specs() + [_spec((3, SC_COLS), lambda j: (0, j))], out_specs=col,
        out_shape=jax.ShapeDtypeStruct((SEQ, D_MODEL), BF16), compiler_params=_params(("parallel",)),
    )(p, p, p, conv_w)


def _sc_mid_bwd(p, conv_w, dy):
    def body(b_ref, c_ref, h_ref, w_ref, dy_ref, db_ref, dc_ref, dh_ref, dw_ref, dzc_ref):
        w = w_ref[...]

        def first(c, acc):
            z = _rows(c_ref, c) * _rows(h_ref, c)
            before = _rows_before(c_ref, c) * _rows_before(h_ref, c)
            zc, z1, z2 = _conv_rows(z, before, w)
            d = _rows(dy_ref, c)
            rows = pl.ds(pl.multiple_of(c * ROW_CHUNK, ROW_CHUNK), ROW_CHUNK)
            db_ref[rows, :] = (d * zc).astype(BF16)
            dzc = d * _rows(b_ref, c)
            dzc_ref[rows, :] = dzc
            return (acc[0] + jnp.sum(dzc * z2, axis=0, keepdims=True), acc[1] + jnp.sum(dzc * z1, axis=0, keepdims=True),
                    acc[2] + jnp.sum(dzc * z, axis=0, keepdims=True))

        zero = jnp.zeros((1, SC_COLS), F32)
        acc = lax.fori_loop(0, N_ROW_CHUNKS, first, (zero, zero, zero))
        for r in range(3):
            dw_ref[r:r + 1, :] = acc[r]

        def second(c, carry):
            dz = _conv_t_rows(_rows(dzc_ref, c), _rows_after(dzc_ref, c, N_ROW_CHUNKS), w)
            rows = pl.ds(pl.multiple_of(c * ROW_CHUNK, ROW_CHUNK), ROW_CHUNK)
            dc_ref[rows, :] = (dz * _rows(h_ref, c)).astype(BF16)
            dh_ref[rows, :] = (dz * _rows(c_ref, c)).astype(BF16)
            return carry

        lax.fori_loop(0, N_ROW_CHUNKS, second, 0)

    col = _spec((SEQ, SC_COLS), lambda j: (0, j))
    wspec = _spec((3, SC_COLS), lambda j: (0, j))
    act = jax.ShapeDtypeStruct((SEQ, D_MODEL), BF16)
    return pl.pallas_call(
        body, name="sc_mid_bwd", grid=(N_SC,), in_specs=_sc_specs() + [wspec, col], out_specs=[col, col, col, wspec],
        out_shape=[act, act, act, jax.ShapeDtypeStruct((3, D_MODEL), F32)],
        scratch_shapes=[pltpu.VMEM((SEQ, SC_COLS), F32)], compiler_params=_params(("parallel",)),
    )(p, p, p, conv_w, dy)


GLA_GROUP = 4
GLA_ROWS = GLA_GROUP * CHUNK
N_GROUPS = N_CHUNKS // GLA_GROUP
Q0, K0, V0, R0, G0 = 0, KEY_DIM, 2 * KEY_DIM, 2 * KEY_DIM + VALUE_DIM, 2 * KEY_DIM + 2 * VALUE_DIM


def _tri(strict):
    r = lax.broadcasted_iota(jnp.int32, (CHUNK, CHUNK), 0)
    c = lax.broadcasted_iota(jnp.int32, (CHUNK, CHUNK), 1)
    return jnp.where(c < r if strict else c <= r, 1.0, 0.0).astype(F32)


def _cumsum_rows(tri, x):
    tri = tri.astype(BF16)
    total = None
    for _ in range(3):
        term = x.astype(BF16)
        x = x - term.astype(F32)
        product = jnp.dot(tri, term, preferred_element_type=F32)
        total = product if total is None else total + product
    return total


def _gate_logits(gl, wgu, b_gate):
    return jnp.dot(gl, wgu, preferred_element_type=F32) + b_gate


def _log_decay(logits):
    return (jnp.minimum(logits, 0.0) - jnp.log(1.0 + jnp.exp(-jnp.abs(logits)))) * (1.0 / GATE_NORMALIZER)


def _head(x, h, width):
    return x[:, h * width:(h + 1) * width]


def _gla_fwd(proj, wgu, b_gate, gn):
    def body(p_ref, wgu_ref, b_ref, gn_ref, o_ref, og_ref, st_ref, state):
        @pl.when(pl.program_id(0) == 0)
        def _():
            state[...] = jnp.zeros_like(state)

        tri = _tri(False)
        la = _log_decay(_gate_logits(p_ref[:, G0:G0 + GATE_PAD], wgu_ref[...], b_ref[...]))
        decays = []
        for c in range(GLA_GROUP):
            rows = slice(c * CHUNK, (c + 1) * CHUNK)
            cum = _cumsum_rows(tri, la[rows])
            tot = cum[CHUNK - 1:CHUNK, :]
            kd = (p_ref[rows, K0:K0 + KEY_DIM].astype(F32) * jnp.exp(tot - cum)).astype(BF16)
            decays.append(jnp.exp(tot))
            v = p_ref[rows, V0:V0 + VALUE_DIM]
            for h in range(GLA_HEADS):
                st_ref[c, h] = lax.dot_general(
                    _head(v, h, HEAD_V), _head(kd, h, HEAD_K), (TN, ((), ())), preferred_element_type=F32)
        for c in range(GLA_GROUP):
            for h in range(GLA_HEADS):
                s = state[h] * _head(decays[c], h, HEAD_K) + st_ref[c, h]
                state[h] = s
                st_ref[c, h] = s
        for c in range(GLA_GROUP):
            rows = slice(c * CHUNK, (c + 1) * CHUNK)
            q = (p_ref[rows, Q0:Q0 + KEY_DIM].astype(F32) * (HEAD_K ** -0.5)).astype(BF16)
            for h in range(GLA_HEADS):
                o_ref[rows, h * HEAD_V:(h + 1) * HEAD_V] = lax.dot_general(
                    _head(q, h, HEAD_K), st_ref[c, h].astype(BF16), (NT, ((), ())), preferred_element_type=F32)
        r = p_ref[:, R0:R0 + VALUE_DIM].astype(F32)
        gate = r * _sigmoid(r) * gn_ref[...]
        for h in range(GLA_HEADS):
            cols = slice(h * HEAD_V, (h + 1) * HEAD_V)
            o = o_ref[:, cols]
            og_ref[:, cols] = (o * _rstd(o) * gate[:, cols]).astype(BF16)

    rows = _spec((GLA_ROWS, VALUE_DIM), lambda i: (i, 0))
    const = lambda shape: _spec(shape, lambda i: (0,) * len(shape))
    return pl.pallas_call(
        body, name="gla_fwd", grid=(N_GROUPS,),
        in_specs=[_spec((GLA_ROWS, PROJ_A_PAD), lambda i: (i, 0)), const((GATE_PAD, KEY_DIM)), const((1, KEY_DIM)),
                  const((1, VALUE_DIM))],
        out_specs=[rows, rows, _spec((GLA_GROUP, GLA_HEADS, HEAD_V, HEAD_K), lambda i: (i, 0, 0, 0))],
        out_shape=[jax.ShapeDtypeStruct((SEQ, VALUE_DIM), F32), jax.ShapeDtypeStruct((SEQ, VALUE_DIM), BF16),
                   jax.ShapeDtypeStruct((N_CHUNKS, GLA_HEADS, HEAD_V, HEAD_K), F32)],
        scratch_shapes=[pltpu.VMEM((GLA_HEADS, HEAD_V, HEAD_K), F32)], compiler_params=_params(("arbitrary",)),
    )(proj, wgu, b_gate, gn)


def _gla_bwd(proj, wgu, b_gate, gn, o, states, dog):
    last = N_GROUPS - 1

    def body(p_ref, wgu_ref, b_ref, gn_ref, o_ref, st_ref, stp_ref, dog_ref, dp_ref, dwgu_ref, db_ref, dgn_ref, carry, do_buf,
             g_buf):
        step = pl.program_id(0)

        @pl.when(step == 0)
        def _():
            carry[...] = jnp.zeros_like(carry)

        r = p_ref[:, R0:R0 + VALUE_DIM].astype(F32)
        sr = _sigmoid(r)
        silu = r * sr
        gn_row = gn_ref[...]
        dog_rows = dog_ref[...].astype(F32)
        dn = dog_rows * silu
        dgn_cols = []
        for h in range(GLA_HEADS):
            cols = slice(h * HEAD_V, (h + 1) * HEAD_V)
            oh = o_ref[:, cols]
            rs = _rstd(oh)
            ohat = oh * rs
            dn_h = dn[:, cols]
            dgn_cols.append(jnp.sum(dn_h * ohat, axis=0, keepdims=True))
            dohat = dn_h * gn_row[:, cols]
            do_buf[:, cols] = rs * (dohat - ohat * jnp.mean(dohat * ohat, axis=-1, keepdims=True))
            n_h = ohat * gn_row[:, cols]
            dp_ref[:, R0 + h * HEAD_V:R0 + (h + 1) * HEAD_V] = (
                dog_rows[:, cols] * n_h * (sr[:, cols] * (1.0 + r[:, cols] * (1.0 - sr[:, cols])))).astype(BF16)
        dgn = jnp.concatenate(dgn_cols, axis=1)

        tri = _tri(False)
        tri_strict = _tri(True)
        gl = p_ref[:, G0:G0 + GATE_PAD]
        logits = _gate_logits(gl, wgu_ref[...], b_ref[...])
        la = _log_decay(logits)
        fades, kds, decays = [], [], []
        for c in range(GLA_GROUP):
            rows = slice(c * CHUNK, (c + 1) * CHUNK)
            cum = _cumsum_rows(tri, la[rows])
            tot = cum[CHUNK - 1:CHUNK, :]
            fades.append(jnp.exp(tot - cum))
            kds.append(p_ref[rows, K0:K0 + KEY_DIM].astype(F32) * fades[c])
            decays.append(jnp.exp(tot))
            q = (p_ref[rows, Q0:Q0 + KEY_DIM].astype(F32) * (HEAD_K ** -0.5)).astype(BF16)
            do = do_buf[rows, :].astype(BF16)
            for h in range(GLA_HEADS):
                do_h = _head(do, h, HEAD_V)
                dq = jnp.dot(do_h, st_ref[c, h].astype(BF16), preferred_element_type=F32) * (HEAD_K ** -0.5)
                dp_ref[rows, Q0 + h * HEAD_K:Q0 + (h + 1) * HEAD_K] = dq.astype(BF16)
                g_buf[c, h] = lax.dot_general(do_h, _head(q, h, HEAD_K), (TN, ((), ())), preferred_element_type=F32)
        for c in reversed(range(GLA_GROUP)):
            for h in range(GLA_HEADS):
                g = carry[h] + g_buf[c, h]
                g_buf[c, h] = g
                carry[h] = g * _head(decays[c], h, HEAD_K)
        dlogit_rows = []
        for c in range(GLA_GROUP):
            rows = slice(c * CHUNK, (c + 1) * CHUNK)
            v = p_ref[rows, V0:V0 + VALUE_DIM]
            kd = kds[c].astype(BF16)
            dkd_cols, ddecay_cols = [], []
            for h in range(GLA_HEADS):
                g = g_buf[c, h]
                g16 = g.astype(BF16)
                dkd_cols.append(jnp.dot(_head(v, h, HEAD_V), g16, preferred_element_type=F32))
                dv = lax.dot_general(_head(kd, h, HEAD_K), g16, (NT, ((), ())), preferred_element_type=F32)
                dp_ref[rows, V0 + h * HEAD_V:V0 + (h + 1) * HEAD_V] = dv.astype(BF16)
                if c > 0:
                    s_prev = st_ref[c - 1, h]
                else:
                    s_prev = jnp.where(step < last, stp_ref[0, h], 0.0)
                ddecay_cols.append(jnp.sum(g * s_prev, axis=0, keepdims=True))
            dkd = jnp.concatenate(dkd_cols, axis=1)
            ddecay = jnp.concatenate(ddecay_cols, axis=1)
            dp_ref[rows, K0:K0 + KEY_DIM] = (dkd * fades[c]).astype(BF16)
            e = dkd * kds[c]
            dla = ddecay * decays[c] + _cumsum_rows(tri_strict, e)
            dlogit_rows.append(dla * (1.0 / GATE_NORMALIZER) * (1.0 - _sigmoid(logits[rows])))
        dlogit = jnp.concatenate(dlogit_rows, axis=0)
        dlogit16 = dlogit.astype(BF16)
        dp_ref[:, G0:G0 + GATE_PAD] = lax.dot_general(
            dlogit16, wgu_ref[...], (NT, ((), ())), preferred_element_type=F32).astype(BF16)
        dwgu = lax.dot_general(gl, dlogit16, (TN, ((), ())), preferred_element_type=F32)
        db = jnp.sum(dlogit, axis=0, keepdims=True)

        @pl.when(step == 0)
        def _():
            dwgu_ref[...] = dwgu
            db_ref[...] = db
            dgn_ref[...] = dgn

        @pl.when(step > 0)
        def _():
            dwgu_ref[...] += dwgu
            db_ref[...] += db
            dgn_ref[...] += dgn

    rev = lambda i: (last - i, 0)
    rows = _spec((GLA_ROWS, VALUE_DIM), rev)
    const = lambda shape: _spec(shape, lambda i: (0,) * len(shape))
    st_shape = (GLA_HEADS, HEAD_V, HEAD_K)
    return pl.pallas_call(
        body, name="gla_bwd", grid=(N_GROUPS,),
        in_specs=[_spec((GLA_ROWS, PROJ_A_PAD), rev), const((GATE_PAD, KEY_DIM)), const((1, KEY_DIM)), const((1, VALUE_DIM)),
                  rows, _spec((GLA_GROUP,) + st_shape, lambda i: (last - i, 0, 0, 0)),
                  _spec((1,) + st_shape, lambda i: (jnp.maximum((last - i) * GLA_GROUP - 1, 0), 0, 0, 0)), rows],
        out_specs=[_spec((GLA_ROWS, PROJ_A_PAD), rev), const((GATE_PAD, KEY_DIM)), const((1, KEY_DIM)), const((1, VALUE_DIM))],
        out_shape=[jax.ShapeDtypeStruct((SEQ, PROJ_A_PAD), BF16), jax.ShapeDtypeStruct((GATE_PAD, KEY_DIM), F32),
                   jax.ShapeDtypeStruct((1, KEY_DIM), F32), jax.ShapeDtypeStruct((1, VALUE_DIM), F32)],
        scratch_shapes=[pltpu.VMEM(st_shape, F32), pltpu.VMEM((GLA_ROWS, VALUE_DIM), F32), pltpu.VMEM((GLA_GROUP,) + st_shape, F32)],
        compiler_params=_params(("arbitrary",)),
    )(proj, wgu, b_gate, gn, o, states, states, dog)


WGRAD_FF_TILE = D_FF // 2


CARRY_ROWS = 8
UP_ROWS = 512


def _ffn_up_mid(name, x, gamma, w_up_t, conv_w):
    def body(x_ref, g_ref, w_ref, c_ref, h_ref, gu_ref, a_ref, carry):
        @pl.when(pl.program_id(0) == 0)
        def _():
            carry[...] = jnp.zeros_like(carry)

        x_tile = x_ref[...]
        h_tile = (x_tile * _rstd(x_tile) * g_ref[...]).astype(BF16)
        h_ref[...] = h_tile
        for k in range(N_FF_COLS):
            cols = slice(k * FF_COLS, (k + 1) * FF_COLS)
            g, u = (lax.dot_general(h_tile, w_ref[p, cols, :], (NT, ((), ())), preferred_element_type=F32).astype(BF16)
                    for p in range(2))
            gu_ref[0, :, cols] = g
            gu_ref[1, :, cols] = u
            g = g.astype(F32)
            w = c_ref[:, cols]
            before = carry[:, cols]
            gc = w[2:3, :] * g + w[1:2, :] * _shift_down(g, before, 1) + w[0:1, :] * _shift_down(g, before, 2)
            a_ref[:, cols] = (gc * _sigmoid(gc) * u.astype(F32)).astype(BF16)
            carry[:, cols] = g[UP_ROWS - CARRY_ROWS:, :]

    row = _spec((UP_ROWS, D_MODEL), lambda i: (i, 0))
    return pl.pallas_call(
        body, name=name, grid=(SEQ // UP_ROWS,),
        in_specs=[row, _resident((1, D_MODEL)), _resident((2, D_FF, D_MODEL)), _resident((3, D_FF))],
        out_specs=[row, _spec((2, UP_ROWS, D_FF), lambda i: (0, i, 0)), _spec((UP_ROWS, D_FF), lambda i: (i, 0))],
        out_shape=[_act(), _act((2, SEQ, D_FF)), _act((SEQ, D_FF))], scratch_shapes=[pltpu.VMEM((CARRY_ROWS, D_FF), F32)],
        compiler_params=_params(("arbitrary",)),
    )(x, gamma, w_up_t, conv_w)


def _ffn_fwd(tag, x, gamma, w_up_t, conv_w, w_down):
    h, gu, a = _ffn_up_mid(f"ffn{tag}_up_mid", x, gamma, w_up_t, conv_w)
    return _rows_matmul(f"ffn{tag}_down", a, w_down, NN, x), (h, gu, a)


def _owner_blocks(d, rows=None):
    if rows is not None:
        d = d[:rows]
    return d.reshape((N_DEV, -1) + d.shape[-1:])


def _ffn_bwd(tag, x, gamma, w_up_t, conv_w, w_down, saved, dx, dx16, swap):
    h, gu, a = saved
    da = _rows_matmul(f"ffn{tag}_da", dx16, w_down, NT)
    d_w_down = _owner_blocks(_wgrad_cols_tn(f"ffn{tag}_dwdown", a, WGRAD_FF_TILE, dx16))
    dgu, d_conv = _ffn_mid_bwd(f"ffn{tag}_mid_bwd", gu, conv_w, da)
    d_w_up_t = _owner_blocks(_wgrad_halves_tn(f"ffn{tag}_dwup", dgu, WGRAD_FF_TILE, h))
    parts = (d_w_up_t, d_w_down)
    dx, dx16, d_gamma, *received = _sum_blocks_nn(
        f"ffn{tag}_dh", dgu, w_up_t, norm=(x, gamma, dx), swap=parts if swap else ())
    return dx, dx16, d_gamma, d_conv, parts, received


def _local_step(x, target, w, fetch=None, emit=None):
    if fetch is None:
        local = dict(a=(w.get("a_w_in"), w.get("a_w_out")), b=(w.get("b_w_in"), w.get("b_w_out")))
        for layer in range(2):
            local[f"f{layer}"] = (w["f_w_up"][layer], w["f_w_down"][layer]) if "f_w_up" in w else None
        fetch = lambda group, after: local[group]
    swap = emit is not None
    if emit is None:
        emit = lambda group, parts, received, dx: dx
    f_norm = (w["f_norm"][0:1], w["f_norm"][1:2])

    x0 = x
    a_w_in, a_w_out = fetch("a", x0)
    h0, proj = _norm_proj("a_in", x0, w["a_norm"], a_w_in)
    o, og, states = _gla_fwd(proj, w["a_w_gate_up"], w["a_b_gate"], w["a_gn"])
    x1 = _rows_matmul("a_out", og, a_w_out, NN, x0)
    up0, down0 = fetch("f0", x1)
    x2, ffn0 = _ffn_fwd(0, x1, f_norm[0], up0, w["f_conv"][0], down0)
    b_w_in, b_w_out = fetch("b", x2)
    h2, p = _norm_proj("b_in", x2, w["b_norm"], b_w_in)
    y = _sc_mid_fwd(p, w["b_conv"])
    x3 = _rows_matmul("b_out", y, b_w_out, NN, x2)
    up1, down1 = fetch("f1", x3)
    ffn1 = _ffn_up_mid("ffn1_up_mid", x3, f_norm[1], up1, w["f_conv"][1])
    loss, dx, dx16, d_final_norm = _down_loss_head(ffn1[2], down1, x3, w["final_norm"], target)

    dx, dx16, d_f_norm1, d_fconv1, parts_f1, got = _ffn_bwd(
        1, x3, f_norm[1], up1, w["f_conv"][1], down1, ffn1, dx, dx16, swap)
    dx16 = emit("f1", parts_f1, got, dx16)

    dy = _rows_matmul("b_dy", dx16, b_w_out, NT)
    d_b_w_out = _owner_blocks(_wgrad_cols_tn("b_dwout", y, OUT_TILE, dx16))
    db, dc, dhh, d_b_conv = _sc_mid_bwd(p, w["b_conv"], dy)
    dp = jnp.concatenate([db, dc, dhh], axis=1)
    parts_b = (_wgrad_cols_transposed_tn("b_dwin", h2, dp, B_SHARD), d_b_w_out)
    dx, dx16, d_b_norm, *got = _sum_cols_nt("b_dh", dp, b_w_in, norm=(x2, w["b_norm"], dx), swap=parts_b if swap else ())
    dx16 = emit("b", parts_b, got, dx16)

    dx, dx16, d_f_norm0, d_fconv0, parts_f0, got = _ffn_bwd(
        0, x1, f_norm[0], up0, w["f_conv"][0], down0, ffn0, dx, dx16, swap)
    dx16 = emit("f0", parts_f0, got, dx16)

    dog = _rows_matmul("a_dog", dx16, a_w_out, NT)
    d_a_w_out = _owner_blocks(_wgrad_cols_tn("a_dwout", og, OUT_TILE, dx16))
    dproj, d_wgu, d_b_gate, d_gn = _gla_bwd(proj, w["a_w_gate_up"], w["a_b_gate"], w["a_gn"], o, states, dog)
    parts_a = (_owner_blocks(_wgrad_cols_tn("a_dwin", dproj, PA_TILE, h0), PROJ_A), d_a_w_out)
    dx, _, d_a_norm, *got = _wide_nn("a_dh", dproj, a_w_in, norm=(x0, w["a_norm"], dx), swap=parts_a if swap else ())
    emit("a", parts_a, got, dx)

    grads = dict(
        a_norm=d_a_norm, a_w_in=parts_a[0], a_w_gate_up=d_wgu, a_b_gate=d_b_gate, a_gn=d_gn, a_w_out=parts_a[1],
        b_norm=d_b_norm, b_w_in=parts_b[0], b_conv=d_b_conv, b_w_out=parts_b[1],
        f_norm=(d_f_norm0, d_f_norm1), f_w_up=(parts_f0[0], parts_f1[0]), f_conv=(d_fconv0, d_fconv1),
        f_w_down=(parts_f0[1], parts_f1[1]), final_norm=d_final_norm)
    grads["loss"] = loss
    return dx, grads


MESH_ID = pl.DeviceIdType.MESH
ANY = pl.BlockSpec(memory_space=pl.ANY)
N_PEERS = N_DEV - 1


def _position():
    return lax.axis_index("x"), lax.axis_index("y"), lax.axis_index("c")


def _slot(px, py, pc):
    return 4 * px + 2 * py + pc


GATHER_COPIES = 8
HALF_ROWS = 16


def _gather_copies(src, out, send_sems, recv_sems, local_sems):
    n = len(src)
    to_sibling, to_x, to_y, x_on_to_y, y_on_to_x, x_to_sibling, y_to_sibling, diagonal_to_sibling = range(GATHER_COPIES)
    x, y, c = _position()
    me, sibling = (x, y, c), (x, y, 1 - c)
    x_side, y_side, diagonal = (1 - x, y), (x, 1 - y), (1 - x, 1 - y)

    def rows_of(t, half):
        rows = src[t].shape[0]
        half_rows = rows // 2 // HALF_ROWS * HALF_ROWS
        return (pl.ds(0, rows), pl.ds(0, half_rows), pl.ds(half_rows, rows - half_rows))[half]

    def copy(t, j, block, to, half=0, from_input=False):
        dst = out[t].at[_slot(*block), rows_of(t, half)]
        return pltpu.make_async_remote_copy(
            src_ref=src[t] if from_input else dst, dst_ref=dst, send_sem=send_sems.at[GATHER_COPIES * t + j],
            recv_sem=recv_sems.at[GATHER_COPIES * t + j], device_id=to, device_id_type=MESH_ID)

    mine = [pltpu.make_async_copy(src[t], out[t].at[_slot(*me)], local_sems.at[t]) for t in range(n)]
    for cp in mine:
        cp.start()
    sent = []

    def start(cp):
        cp.start()
        sent.append(cp)

    for t in range(n):
        start(copy(t, to_sibling, me, sibling, from_input=True))
        start(copy(t, to_x, me, (*x_side, c), from_input=True))
        start(copy(t, to_y, me, (*y_side, c), from_input=True))
    for t in range(n):
        copy(t, to_x, (*x_side, c), me).wait_recv()
        start(copy(t, x_on_to_y, (*x_side, c), (*y_side, c), half=1))
        start(copy(t, x_to_sibling, (*x_side, c), sibling))
        copy(t, to_y, (*y_side, c), me).wait_recv()
        start(copy(t, y_on_to_x, (*y_side, c), (*x_side, c), half=2))
        start(copy(t, y_to_sibling, (*y_side, c), sibling))
    for t in range(n):
        copy(t, x_on_to_y, (*diagonal, c), me, half=1).wait_recv()
        copy(t, y_on_to_x, (*diagonal, c), me, half=2).wait_recv()
        start(copy(t, diagonal_to_sibling, (*diagonal, c), sibling))
    for t in range(n):
        copy(t, to_sibling, sibling, me).wait_recv()
        for j, chip in ((x_to_sibling, x_side), (y_to_sibling, y_side), (diagonal_to_sibling, diagonal)):
            copy(t, j, (*chip, 1 - c), me).wait_recv()
    for cp in sent:
        cp.wait_send()
    for cp in mine:
        cp.wait()


def _broadcast(name, packed):
    def body(src, out, send_sems, recv_sems, local_sems):
        x, y, c = _position()
        mine = pltpu.make_async_copy(src, out.at[_slot(x, y, c)], local_sems.at[0])
        mine.start()
        sent = []
        for k in range(1, N_DEV):
            peer = (x ^ (k >> 2), y ^ ((k >> 1) & 1), c ^ (k & 1))
            sent.append(pltpu.make_async_remote_copy(
                src_ref=src, dst_ref=out.at[_slot(x, y, c)], send_sem=send_sems.at[k - 1], recv_sem=recv_sems.at[k - 1],
                device_id=peer, device_id_type=MESH_ID))
            sent[-1].start()
        for k in range(1, N_DEV):
            landed = out.at[_slot(x ^ (k >> 2), y ^ ((k >> 1) & 1), c ^ (k & 1))]
            pltpu.make_async_remote_copy(
                src_ref=landed, dst_ref=landed, send_sem=send_sems.at[k - 1], recv_sem=recv_sems.at[k - 1],
                device_id=(x, y, c), device_id_type=MESH_ID).wait_recv()
        for cp in sent:
            cp.wait_send()
        mine.wait()

    sems = pltpu.SemaphoreType.DMA((N_PEERS,))
    return pl.pallas_call(
        body, name=name, in_specs=[ANY], out_specs=ANY, out_shape=jax.ShapeDtypeStruct((N_DEV,) + packed.shape, packed.dtype),
        scratch_shapes=[sems, sems, pltpu.SemaphoreType.DMA((1,))],
    )(packed)


def _all_gather(name, shards):
    n = len(shards)

    def body(*refs):
        _gather_copies(refs[:n], refs[n:2 * n], *refs[2 * n:])

    sems = pltpu.SemaphoreType.DMA((GATHER_COPIES * n,))
    return pl.pallas_call(
        body, name=name, in_specs=[ANY] * n, out_specs=[ANY] * n,
        out_shape=[jax.ShapeDtypeStruct((N_DEV,) + s.shape, s.dtype) for s in shards],
        scratch_shapes=[sems, sems, pltpu.SemaphoreType.DMA((n,))],
    )(*shards)


SIBLING_AND_NEIGHBOURS = (1, 2, 4)
SAME_CORE = (2, 4, 6)


def _flip(x, y, c, k):
    return x ^ (k >> 2), y ^ ((k >> 1) & 1), c ^ (k & 1)


N_CHIPS = N_DEV // 2


def _chip(px, py):
    return 2 * px + py


def _pair_copies(parts, received, send_sems, recv_sems):
    x, y, c = lax.axis_index("x"), lax.axis_index("y"), lax.axis_index("c")
    sibling = (x, y, 1 - c)
    copies = []
    for t in range(len(parts)):
        for q in range(N_DEV // 2):
            send = pltpu.make_async_remote_copy(
                src_ref=parts[t].at[2 * q + 1 - c], dst_ref=received[t].at[q], send_sem=send_sems.at[t, q],
                recv_sem=recv_sems.at[t, q], device_id=sibling, device_id_type=pl.DeviceIdType.MESH)
            landed = received[t].at[q]
            arrival = pltpu.make_async_remote_copy(
                src_ref=landed, dst_ref=landed, send_sem=send_sems.at[t, q], recv_sem=recv_sems.at[t, q],
                device_id=sibling, device_id_type=pl.DeviceIdType.MESH)
            copies.append((send, arrival))
    return copies


def _pair_add(name, parts, received, side):
    n = len(parts)

    def body(side_ref, *refs):
        for t in range(n):
            refs[2 * n + t][...] = (refs[t][...].astype(F32) + refs[n + t][...].astype(F32)).astype(BF16)

    own = [_spec((None,) + p.shape[1:], lambda q, side_ref: (2 * q + side_ref[0], 0, 0)) for p in parts]
    chip = [_spec((None,) + p.shape[1:], lambda q, side_ref: (q, 0, 0)) for p in parts]
    return pl.pallas_call(
        body, name=name,
        grid_spec=pltpu.PrefetchScalarGridSpec(num_scalar_prefetch=1, grid=(N_CHIPS,), in_specs=own + chip, out_specs=chip),
        out_shape=[jax.ShapeDtypeStruct((N_CHIPS,) + p.shape[1:], BF16) for p in parts], compiler_params=_params(("parallel",)),
    )(side, *parts, *received)


def _send_copy(parts, landing, send_sems, recv_sems, t, s, k):
    x, y, c = _position()
    px, py, _ = _flip(x, y, c, k)
    return pltpu.make_async_remote_copy(
        src_ref=parts[t].at[_chip(px, py)], dst_ref=landing[t].at[_chip(x, y)], send_sem=send_sems.at[s],
        recv_sem=recv_sems.at[s], device_id=(px, py, c), device_id_type=MESH_ID)


def _send_arrival(landing, send_sems, recv_sems, t, s, k):
    x, y, c = _position()
    px, py, _ = _flip(x, y, c, k)
    landed = landing[t].at[_chip(px, py)]
    return pltpu.make_async_remote_copy(
        src_ref=landed, dst_ref=landed, send_sem=send_sems.at[s], recv_sem=recv_sems.at[s],
        device_id=(px, py, c), device_id_type=MESH_ID)


def _handshake(peers):
    x, y, c = _position()
    barrier = pltpu.get_barrier_semaphore()
    for k in peers:
        pl.semaphore_signal(barrier, inc=1, device_id=_flip(x, y, c, k), device_id_type=MESH_ID)
    pl.semaphore_wait(barrier, len(peers))


def _sequencer(name, collective_id, n_copies, body, operands, out_type):
    n_arrays = len(operands)
    return pl.kernel(
        body, out_type=out_type, mesh=plsc.ScalarSubcoreMesh(axis_name="sequencer", num_cores=1), name=name,
        scratch_types=(pltpu.SemaphoreType.DMA((n_copies,)), pltpu.SemaphoreType.DMA((n_copies,)),
                       pltpu.SemaphoreType.DMA((n_arrays,))),
        compiler_params=pltpu.CompilerParams(collective_id=collective_id))(*operands)


def _sequencer_exchange(name, collective_id, parts, after=()):
    n, n_peers, n_in = len(parts), len(SAME_CORE), len(parts) + len(after)

    def body(*refs):
        src, landing = refs[:n], refs[n_in:n_in + n]
        send_sems, recv_sems, local_sems = refs[n_in + n:]
        _handshake(SAME_CORE)
        x, y, _ = _position()
        mine = [pltpu.make_async_copy(src[t].at[_chip(x, y)], landing[t].at[_chip(x, y)], local_sems.at[t]) for t in range(n)]
        for cp in mine:
            cp.start()
        sent = [_send_copy(src, landing, send_sems, recv_sems, t, t * n_peers + j, k)
                for t in range(n) for j, k in enumerate(SAME_CORE)]
        for cp in sent:
            cp.start()
        for t in range(n):
            for j, k in enumerate(SAME_CORE):
                _send_arrival(landing, send_sems, recv_sems, t, t * n_peers + j, k).wait_recv()
        for cp in sent:
            cp.wait_send()
        for cp in mine:
            cp.wait()

    landing = [jax.ShapeDtypeStruct(p.shape, p.dtype) for p in parts]
    return _sequencer(name, collective_id, n * n_peers, body, list(parts) + list(after), landing)


def _sequencer_gather(name, collective_id, shards):
    n = len(shards)

    def body(*refs):
        _handshake(SIBLING_AND_NEIGHBOURS)
        _gather_copies(refs[:n], refs[n:2 * n], *refs[2 * n:])

    gathered = [jax.ShapeDtypeStruct((N_DEV,) + s.shape, s.dtype) for s in shards]
    return _sequencer(name, collective_id, GATHER_COPIES * n, body, shards, gathered)


ADAM_ROWS = 512
BF16_ROWS = 16


def _adam_update(w, g, m, v):
    m = ADAM_B1 * m + (1.0 - ADAM_B1) * g
    v = ADAM_B2 * v + (1.0 - ADAM_B2) * (g * g)
    m_hat = m / (1.0 - ADAM_B1 ** ADAM_STEP)
    v_hat = v / (1.0 - ADAM_B2 ** ADAM_STEP)
    delta = -ADAM_LR * (m_hat / (jnp.sqrt(v_hat) + ADAM_EPS) + ADAM_WD * w)
    return delta, m, v


def _sum_slots(ref):
    total = ref[0].astype(F32)
    for d in range(1, ref.shape[0]):
        total = total + ref[d].astype(F32)
    return total


def _adamw_sum(name, landed, w, m, v):
    layers, rows, cols = w.shape
    tiles = [t for t in range(ADAM_ROWS, 0, -BF16_ROWS) if rows % t == 0]
    tr = tiles[0] if tiles else rows
    nt = rows // tr

    def body(*refs):
        parts = refs[:layers]
        w_ref, m_ref, v_ref, g_ref, d_ref, nm_ref, nv_ref = refs[layers:]
        layer = pl.program_id(0)
        g = _sum_slots(parts[0])
        for q in range(1, layers):
            g = jnp.where(layer == q, _sum_slots(parts[q]), g)
        delta, new_m, new_v = _adam_update(w_ref[...], g, m_ref[...], v_ref[...])
        g_ref[...] = g
        d_ref[...] = delta
        nm_ref[...] = new_m
        nv_ref[...] = new_v

    def part_spec(q):
        return _spec((N_CHIPS, tr, cols), lambda l, i: (0, jnp.where(l == q, i, jnp.where(l < q, 0, nt - 1)), 0))

    tile = _spec((None, tr, cols), lambda l, i: (l, i, 0))
    out = jax.ShapeDtypeStruct((layers, rows, cols), F32)
    return pl.pallas_call(
        body, name=name, grid=(layers, nt), in_specs=[part_spec(q) for q in range(layers)] + [tile] * 3,
        out_specs=[tile] * 4, out_shape=[out] * 4, compiler_params=_params(("arbitrary", "arbitrary")),
    )(*landed, w, m, v)


def _sum_small(landed):
    def body(in_ref, out_ref):
        out_ref[...] = _sum_slots(in_ref)

    return pl.pallas_call(body, name="small_grad_sum", out_shape=jax.ShapeDtypeStruct(landed.shape[1:], F32))(landed)


def _adamw_small(arrays):
    n = len(arrays)

    def body(*refs):
        for i in range(n):
            g_ref, w_ref, m_ref, v_ref = refs[4 * i:4 * i + 4]
            d_ref, nm_ref, nv_ref = refs[4 * n + 3 * i:4 * n + 3 * i + 3]
            d_ref[...], nm_ref[...], nv_ref[...] = _adam_update(w_ref[...], g_ref[...], m_ref[...], v_ref[...])

    out = [jax.ShapeDtypeStruct(w.shape, F32) for _, w, _, _ in arrays for _ in range(3)]
    flat = pl.pallas_call(body, name="adam_small", out_shape=out)(*[a for group in arrays for a in group])
    return [tuple(flat[3 * i:3 * i + 3]) for i in range(n)]


LANES = 128
SUBLANES = 8
F_CONV_SHARD = D_FF // N_DEV
GATE_SHARD = KEY_DIM // N_DEV
NORM_SHARD = D_MODEL // N_DEV


def _tile_rows(a):
    flat = a.reshape(-1)
    size = -(-flat.shape[0] // (SUBLANES * LANES)) * SUBLANES * LANES
    return jnp.pad(flat, (0, size - flat.shape[0])).reshape(-1, LANES)


def _pack_rows(pieces):
    return jnp.concatenate([_tile_rows(p) for p in pieces], axis=0)


def _unpack_rows(packed, shapes):
    out, row = [], 0
    for shape in shapes:
        size = 1
        for s in shape:
            size *= s
        rows = -(-size // (SUBLANES * LANES)) * SUBLANES
        piece = packed[..., row:row + rows, :]
        out.append(piece.reshape(piece.shape[:-2] + (rows * LANES,))[..., :size])
        row += rows
    return out


SMALL_SHARDS = ((GATE_RANK, GATE_SHARD), (1, NORM_SHARD), (3, NORM_SHARD), (2, 3, F_CONV_SHARD))


def _unpack_small_shards(g):
    gate, b_norm, b_conv, f_conv = _unpack_rows(g, SMALL_SHARDS)
    gate = gate.reshape(N_DEV, GATE_RANK, GATE_SHARD).transpose(1, 0, 2).reshape(GATE_RANK, KEY_DIM)
    b_norm = b_norm.reshape(1, D_MODEL)
    b_conv = b_conv.reshape(N_DEV, 3, NORM_SHARD).transpose(1, 0, 2).reshape(3, D_MODEL)
    f_conv = f_conv.reshape(N_DEV, 2, 3, F_CONV_SHARD).transpose(1, 2, 0, 3).reshape(2, 3, D_FF)
    return gate, b_norm, b_conv, f_conv


SMALL_LAYOUT = (("a_norm", (1, D_MODEL)), ("a_w_gate_up", (GATE_RANK, KEY_DIM)), ("a_b_gate", (1, KEY_DIM)), ("a_gn", (1, VALUE_DIM)),
                ("b_norm", (1, D_MODEL)), ("b_conv", (3, D_MODEL)), ("f_norm0", (1, D_MODEL)), ("f_norm1", (1, D_MODEL)),
                ("f_conv0", (3, D_FF)), ("f_conv1", (3, D_FF)), ("final_norm", (1, D_MODEL)), ("loss", (1, LANES)))


def _pack_small_grads(g):
    full = dict(g)
    full["a_w_gate_up"] = g["a_w_gate_up"][:GATE_RANK]
    for layer in range(2):
        full[f"f_norm{layer}"] = g["f_norm"][layer]
        full[f"f_conv{layer}"] = g["f_conv"][layer]
    return _pack_rows([full[name] for name, _ in SMALL_LAYOUT])


def _unpack_small_grads(packed):
    pieces = _unpack_rows(packed, [shape for _, shape in SMALL_LAYOUT])
    out = {name: piece.reshape(shape) for (name, shape), piece in zip(SMALL_LAYOUT, pieces)}
    out["f_norm"] = jnp.stack([out["f_norm0"][0], out["f_norm1"][0]])
    out["f_conv"] = jnp.stack([out["f_conv0"], out["f_conv1"]])
    return out


def kernel(x, a_norm, a_w_in, a_w_gate_up, a_b_gate, a_gn, a_w_out, b_norm, b_w_in, b_conv, b_w_out, f_norm, f_w_up, f_conv, f_w_down, final_norm, loss_target, m_a_norm, m_a_w_in, m_a_w_gate_up, m_a_b_gate, m_a_gn, m_a_w_out, m_b_norm, m_b_w_in, m_b_conv, m_b_w_out, m_f_norm, m_f_w_up, m_f_conv, m_f_w_down, m_final_norm, v_a_norm, v_a_w_in, v_a_w_gate_up, v_a_b_gate, v_a_gn, v_a_w_out, v_b_norm, v_b_w_in, v_b_conv, v_b_w_out, v_f_norm, v_f_w_up, v_f_conv, v_f_w_down, v_final_norm):
    my_slot = _slot(*_position())

    transposed = lambda w: jnp.swapaxes(w, 1, 2)
    a_transposed = lambda w: w.reshape(D_MODEL, A_SHARD).T.reshape(1, A_SHARD, D_MODEL)
    a_w_in_t, f_w_up_t = a_transposed(a_w_in), transposed(f_w_up)
    first = _all_gather("weight_gather", [a_w_in_t[0].astype(BF16), a_w_out[0].astype(BF16),
                                          _pack_rows([a_w_gate_up[0], b_norm, b_conv[0], f_conv])])
    gathers, small_shards = {}, first[2]
    later = (("f0", f_w_up_t[0], f_w_down[0]), ("b", b_w_in[0], b_w_out[0]), ("f1", f_w_up_t[1], f_w_down[1]))
    for collective_id, (group, w_in, w_out) in enumerate(later):
        w_in, w_out, small_shards = lax.optimization_barrier((w_in.astype(BF16), w_out.astype(BF16), small_shards))
        gathers[group] = _sequencer_gather(f"gather_{group}", collective_id, [w_in, w_out])
    gate_full, b_norm_full, b_conv_full, f_conv_full = _unpack_small_shards(small_shards)
    a_w_in_full = jnp.pad(first[0].reshape(PROJ_A, D_MODEL), ((0, PROJ_A_PAD - PROJ_A), (0, 0)))
    weights = dict(
        a_norm=a_norm, a_w_gate_up=jnp.pad(gate_full, ((0, GATE_PAD - GATE_RANK), (0, 0))).astype(BF16), a_b_gate=a_b_gate,
        a_gn=a_gn, b_norm=b_norm_full, b_conv=b_conv_full, f_norm=f_norm, f_conv=f_conv_full,
        final_norm=final_norm.reshape(1, D_MODEL))

    def fetch(group, after):
        if group == "a":
            return a_w_in_full, first[1].reshape(D_MODEL, D_MODEL)
        w_in, w_out = gathers[group]
        if group == "b":
            return w_in, w_out.reshape(D_MODEL, D_MODEL)
        return w_in.reshape(2, D_FF, D_MODEL), w_out.reshape(D_FF, D_MODEL)

    exchanges, pending = {}, []
    exchange_ids = dict(b=3, f0=4, a=5)
    side = lax.axis_index("c").astype(jnp.int32).reshape(1)

    def emit(group, parts, received, carry):
        sums = _pair_add(f"pair_add_{group}", parts, received, side)
        carry, *sums = lax.optimization_barrier((carry, *sums))
        pending.extend(sums)
        if group != "f1":
            after = list(exchanges.values())[-1][:1] if exchanges else ()
            exchanges[group] = _sequencer_exchange(f"grads_{group}", exchange_ids[group], list(pending), after)
            pending.clear()
        return carry

    dx, g = _local_step(x[0], loss_target[0], weights, fetch, emit)

    (up1, down1, d_b_in, d_b_out), (up0, down0), (d_a_in, d_a_out) = (exchanges[group] for group in ("b", "f0", "a"))
    back = lambda results: tuple(transposed(r) for r in results)
    big = dict(
        b_w_in=_adamw_sum("adam_b_w_in", [d_b_in], b_w_in, m_b_w_in, v_b_w_in),
        b_w_out=_adamw_sum("adam_b_w_out", [d_b_out], b_w_out, m_b_w_out, v_b_w_out),
        f_w_up=back(_adamw_sum("adam_f_w_up", [up0, up1], f_w_up_t, transposed(m_f_w_up), transposed(v_f_w_up))),
        f_w_down=_adamw_sum("adam_f_w_down", [down0, down1], f_w_down, m_f_w_down, v_f_w_down))
    small_packed, *updated = lax.optimization_barrier((_pack_small_grads(g), *big["f_w_down"]))
    big["f_w_down"] = tuple(updated)
    small_landed = _broadcast("small_grad_gather", small_packed)
    big.update(
        a_w_in=tuple(r.reshape(A_SHARD, D_MODEL).T.reshape(1, D_MODEL, A_SHARD) for r in _adamw_sum(
            "adam_a_w_in", [d_a_in], a_w_in_t, a_transposed(m_a_w_in), a_transposed(v_a_w_in))),
        a_w_out=_adamw_sum("adam_a_w_out", [d_a_out], a_w_out, m_a_w_out, v_a_w_out))
    small_g = _unpack_small_grads(_sum_small(small_landed))
    loss = small_g["loss"][0, 0]
    small_g["a_w_gate_up"] = lax.dynamic_slice_in_dim(small_g["a_w_gate_up"], my_slot * GATE_SHARD, GATE_SHARD, axis=1)
    small_g["b_norm"] = lax.dynamic_slice_in_dim(small_g["b_norm"], my_slot * NORM_SHARD, NORM_SHARD, axis=1)
    small_g["b_conv"] = lax.dynamic_slice_in_dim(small_g["b_conv"], my_slot * NORM_SHARD, NORM_SHARD, axis=1)
    small_g["f_conv"] = lax.dynamic_slice_in_dim(small_g["f_conv"], my_slot * F_CONV_SHARD, F_CONV_SHARD, axis=2)
    small_w = dict(
        a_norm=(a_norm, m_a_norm, v_a_norm), a_w_gate_up=(a_w_gate_up, m_a_w_gate_up, v_a_w_gate_up),
        a_b_gate=(a_b_gate, m_a_b_gate, v_a_b_gate), a_gn=(a_gn, m_a_gn, v_a_gn), b_norm=(b_norm, m_b_norm, v_b_norm),
        b_conv=(b_conv, m_b_conv, v_b_conv), f_norm=(f_norm, m_f_norm, v_f_norm), f_conv=(f_conv, m_f_conv, v_f_conv),
        final_norm=(final_norm, m_final_norm, v_final_norm))
    two_d = lambda a: a.reshape(-1, a.shape[-1])
    updates = _adamw_small([tuple(two_d(a.reshape(w.shape)) for a in (small_g[name], w, m, v)) for name, (w, m, v) in small_w.items()])
    small = {}
    for (name, (w, _, _)), update in zip(small_w.items(), updates):
        small[name] = (small_g[name].reshape(w.shape),) + tuple(u.reshape(w.shape) for u in update)

    order = ["a_norm", "a_w_in", "a_w_gate_up", "a_b_gate", "a_gn", "a_w_out", "b_norm", "b_w_in", "b_conv", "b_w_out",
             "f_norm", "f_w_up", "f_conv", "f_w_down", "final_norm"]
    results = {**big, **small}
    outputs = [loss, dx.reshape(1, SEQ, D_MODEL)]
    for kind in range(4):
        outputs += [results[name][kind] for name in order]
    return tuple(outputs)
```

```python
import jax
import jax.numpy as jnp
from jax import lax
from jax.experimental import pallas as pl
from jax.experimental.pallas import tpu as pltpu
from jax.experimental.pallas import tpu_sc as plsc

F32 = jnp.float32
BF16 = jnp.bfloat16

N_DEV = 8
SEQ = 2048
D_MODEL = 1024
CHUNK = 64
N_CHUNKS = SEQ // CHUNK
RMS_EPS = 1e-6
GLA_HEADS = 4
KEY_DIM = 512
VALUE_DIM = 1024
HEAD_K = KEY_DIM // GLA_HEADS
HEAD_V = VALUE_DIM // GLA_HEADS
GATE_RANK = 16
GATE_PAD = 128
GATE_NORMALIZER = 16.0
PROJ_A = 2 * KEY_DIM + 2 * VALUE_DIM + GATE_RANK
PROJ_A_PAD = 2 * KEY_DIM + 2 * VALUE_DIM + GATE_PAD
A_SHARD = PROJ_A // N_DEV
B_SHARD = 3 * D_MODEL // N_DEV
D_FF = 2816
ADAM_LR = 0.001
ADAM_B1 = 0.9
ADAM_B2 = 0.999
ADAM_EPS = 1e-08
ADAM_WD = 0.01
ADAM_STEP = 10
MESH_AXES = ("x", "y", "c")

VMEM_LIMIT = 56 * 1024 * 1024
ROW_CHUNK = 256
HALO = 16


def _params(sem=None, vmem=VMEM_LIMIT):
    return pltpu.CompilerParams(dimension_semantics=sem, vmem_limit_bytes=vmem)


NORM_PARTS = 2
NN = ((1,), (0,))
NT = ((1,), (1,))
TN = ((0,), (0,))


def _matmul(name, a, a_spec, b, b_spec, dims, grid, out_shape, out_spec, k_blocks=None, a_block_cols=None, res=None,
            res_spec=None, transpose_out=False, norm=None, swap=()):
    has_res = res is not None
    n_swap = len(swap)

    def body(*refs):
        a_ref, b_ref = refs[0], refs[1]
        r_ref = refs[2] if has_res else None

        def product(lhs, rhs):
            return lax.dot_general(lhs.astype(BF16), rhs, (dims, ((), ())), preferred_element_type=F32)

        def tile(rows=slice(None)):
            if k_blocks is None:
                return product(a_ref[rows, :] if norm is not None else a_ref[...], b_ref[...])
            v = None
            for k in range(k_blocks):
                lhs = a_ref[k, rows, :] if a_block_cols is None else a_ref[rows, k * a_block_cols:(k + 1) * a_block_cols]
                p = product(lhs, b_ref[k])
                v = p if v is None else v + p
            return v

        if norm is None:
            v = tile()
            if transpose_out:
                v = v.T
            if has_res:
                v = v + r_ref[...]
            o_ref = refs[2 + has_res]
            o_ref[...] = v.astype(o_ref.dtype)
            return
        n_in = 5 + has_res
        x_ref, g_ref, dxi_ref = refs[2 + has_res:n_in]
        dx_ref, dx16_ref, dg_ref = refs[n_in + n_swap:n_in + n_swap + 3]
        if n_swap:
            copies = _pair_copies(refs[n_in:n_in + n_swap], refs[n_in + n_swap + 3:n_in + 2 * n_swap + 3], *refs[-2:])

            @pl.when(pl.program_id(0) == 0)
            def _():
                for send, _ in copies:
                    send.start()

            @pl.when(pl.program_id(0) == grid[0] - 1)
            def _():
                for send, arrival in copies:
                    arrival.wait_recv()
                    send.wait_send()

        dg = None
        part = dx_ref.shape[0] // NORM_PARTS
        for rows in (slice(i * part, (i + 1) * part) for i in range(NORM_PARTS)):
            dx, dg_rows = _norm_bwd_rows(x_ref[rows, :], g_ref[...], tile(rows))
            dx = dxi_ref[rows, :] + dx
            dx_ref[rows, :] = dx
            dx16_ref[rows, :] = dx.astype(BF16)
            dg = dg_rows if dg is None else dg + dg_rows

        @pl.when(pl.program_id(0) == 0)
        def _():
            dg_ref[...] = dg

        @pl.when(pl.program_id(0) > 0)
        def _():
            dg_ref[...] += dg

    operands = [a, b] + ([res] if has_res else [])
    in_specs = [a_spec, b_spec] + ([res_spec] if has_res else [])
    semantics = ("parallel",) * len(grid)
    scratch = []
    if norm is not None:
        vec = _spec((1, D_MODEL), lambda i: (0, 0))
        any_space = pl.BlockSpec(memory_space=pl.ANY)
        operands += list(norm) + list(swap)
        in_specs += [out_spec, vec, out_spec] + [any_space] * n_swap
        out_shape = [_act(dtype=F32), _act(), jax.ShapeDtypeStruct((1, D_MODEL), F32)]
        out_shape += [jax.ShapeDtypeStruct((N_DEV // 2,) + p.shape[1:], p.dtype) for p in swap]
        out_spec = [out_spec, out_spec, vec] + [any_space] * n_swap
        semantics = ("arbitrary",)
        if n_swap:
            scratch = [pltpu.SemaphoreType.DMA((n_swap, N_DEV // 2))] * 2
    return pl.pallas_call(
        body, name=name, grid=grid, in_specs=in_specs, out_specs=out_spec, out_shape=out_shape, scratch_shapes=scratch,
        compiler_params=_params(semantics),
    )(*operands)


def _resident(shape):
    return pl.BlockSpec(shape, lambda *_: (0,) * len(shape), pipeline_mode=pl.Buffered(1))


TM = 512
N_TM = SEQ // TM
PA_TILE = 640
N_PA = PROJ_A_PAD // PA_TILE
OUT_TILE = 256


def _spec(shape, fn):
    return pl.BlockSpec(shape, fn)


def _act(shape=(SEQ, D_MODEL), dtype=BF16):
    return jax.ShapeDtypeStruct(shape, dtype)


def _norm_proj(name, x, gamma, w):
    blocks = w.ndim == 3
    n_out = w.shape[0] * w.shape[2] if blocks else w.shape[0]

    def body(x_ref, g_ref, w_ref, h_ref, o_ref):
        x = x_ref[...]
        h = (x * _rstd(x) * g_ref[...]).astype(BF16)
        h_ref[...] = h
        if blocks:
            n = w.shape[2]
            for j in range(w.shape[0]):
                o_ref[:, j * n:(j + 1) * n] = jnp.dot(h, w_ref[j], preferred_element_type=F32).astype(BF16)
        else:
            o_ref[...] = lax.dot_general(h, w_ref[...], (NT, ((), ())), preferred_element_type=F32).astype(BF16)

    row = _spec((TM, D_MODEL), lambda i: (i, 0))
    return pl.pallas_call(
        body, name=name, grid=(N_TM,), in_specs=[row, _resident((1, D_MODEL)), _resident(w.shape)],
        out_specs=[row, _spec((TM, n_out), lambda i: (i, 0))], out_shape=[_act(), _act((SEQ, n_out))],
        compiler_params=_params(("parallel",)),
    )(x, gamma, w)


def _rows_matmul(name, a, w, dims, x=None):
    k = a.shape[1]
    n = w.shape[1] if dims == NN else w.shape[0]
    row = _spec((TM, n), lambda i: (i, 0))
    return _matmul(name, a, _spec((TM, k), lambda i: (i, 0)), w, _resident(w.shape), dims, (N_TM,),
                   _act((SEQ, n), F32 if x is not None else BF16), row, res=x, res_spec=row if x is not None else None)


def _sum_blocks_nn(name, a_blocks, w_blocks, x=None, norm=None, swap=()):
    nb, _, n = a_blocks.shape
    row = _spec((TM, D_MODEL), lambda i: (i, 0))
    return _matmul(name, a_blocks, _spec((nb, TM, n), lambda i: (0, i, 0)), w_blocks, _resident((nb, n, D_MODEL)),
                   NN, (N_TM,), _act(dtype=F32), row, k_blocks=nb, res=x, res_spec=row if x is not None else None, norm=norm, swap=swap)


def _sum_cols_nt(name, d, w_blocks, norm=None, swap=()):
    nb, _, n = w_blocks.shape
    return _matmul(name, d, _spec((TM, nb * n), lambda i: (i, 0)), w_blocks, _resident((nb, D_MODEL, n)), NT,
                   (N_TM,), _act(dtype=F32), _spec((TM, D_MODEL), lambda i: (i, 0)), k_blocks=nb, a_block_cols=n, norm=norm, swap=swap)


def _wide_nn(name, d, wt, x=None, norm=None, swap=()):
    n = wt.shape[0]
    row = _spec((TM, D_MODEL), lambda i: (i, 0))
    return _matmul(name, d, _spec((TM, n), lambda i: (i, 0)), wt, _resident((n, D_MODEL)), NN, (N_TM,),
                   _act(dtype=F32), row, res=x, res_spec=row if x is not None else None, norm=norm, swap=swap)


def _wgrad_halves_tn(name, d, n_tile, h):
    _, _, n = d.shape
    return _matmul(name, d, _spec((None, SEQ, n_tile), lambda p, j: (p, 0, j)), h, _resident((SEQ, D_MODEL)), TN,
                   (2, n // n_tile), _act((2, n, D_MODEL)), _spec((None, n_tile, D_MODEL), lambda p, j: (p, j, 0)))


def _wgrad_cols_tn(name, d, n_tile, h):
    n = d.shape[1]
    return _matmul(name, d, _spec((SEQ, n_tile), lambda j: (0, j)), h, _resident((SEQ, D_MODEL)), TN,
                   (n // n_tile,), _act((n, D_MODEL)), _spec((n_tile, D_MODEL), lambda j: (j, 0)))


def _wgrad_cols_transposed_tn(name, h, d, n_tile):
    nb = d.shape[1] // n_tile
    return _matmul(name, d, _spec((SEQ, n_tile), lambda j: (0, j)), h, _resident((SEQ, D_MODEL)), TN, (nb,),
                   _act((nb, D_MODEL, n_tile)), _spec((None, D_MODEL, n_tile), lambda j: (j, 0, 0)), transpose_out=True)


NORM_ROWS = 512


def _rstd(x):
    return lax.rsqrt(jnp.mean(x * x, axis=-1, keepdims=True) + RMS_EPS)


def _norm_bwd_rows(x, gamma, dh):
    r = _rstd(x)
    xh = x * r
    dxh = dh * gamma
    dx = r * (dxh - xh * jnp.mean(dxh * xh, axis=-1, keepdims=True))
    return dx, jnp.sum(dh * xh, axis=0, keepdims=True)


def _down_loss_head(a, w_down, x_in, gamma, target):
    def body(a_ref, w_ref, x_ref, g_ref, t_ref, loss_ref, dx_ref, dx16_ref, dg_ref):
        gamma = g_ref[...]
        dg, part = 0.0, 0.0
        rows_per_part = TM // NORM_PARTS
        for rows in (slice(i * rows_per_part, (i + 1) * rows_per_part) for i in range(NORM_PARTS)):
            x = x_ref[rows, :] + jnp.dot(a_ref[rows, :], w_ref[...], preferred_element_type=F32)
            err = x * _rstd(x) * gamma - t_ref[rows, :]
            dy = err * (1.0 / D_MODEL)
            dx, dg_rows = _norm_bwd_rows(x, gamma, dy)
            dx_ref[rows, :] = dx
            dx16_ref[rows, :] = dx.astype(BF16)
            dg = dg + dg_rows
            part = part + 0.5 * jnp.sum(jnp.sum(err * err, axis=-1, keepdims=True) * (1.0 / D_MODEL), axis=0, keepdims=True)
        part = jnp.broadcast_to(part, loss_ref.shape)

        @pl.when(pl.program_id(0) == 0)
        def _():
            dg_ref[...] = dg
            loss_ref[...] = part

        @pl.when(pl.program_id(0) > 0)
        def _():
            dg_ref[...] += dg
            loss_ref[...] += part

    row = _spec((TM, D_MODEL), lambda i: (i, 0))
    vec = _spec((1, D_MODEL), lambda i: (0, 0))
    return pl.pallas_call(
        body, name="ffn1_down_loss_head", grid=(N_TM,),
        in_specs=[_spec((TM, D_FF), lambda i: (i, 0)), _resident((D_FF, D_MODEL)), row, vec, row],
        out_specs=[_spec((1, 128), lambda i: (0, 0)), row, row, vec],
        out_shape=[jax.ShapeDtypeStruct((1, 128), F32), _act(dtype=F32), _act(), jax.ShapeDtypeStruct((1, D_MODEL), F32)],
        compiler_params=_params(("arbitrary",)),
    )(a, w_down, x_in, gamma, target)


def _sigmoid(x):
    return 1.0 / (1.0 + jnp.exp(-x))


def _rows(ref, c):
    return ref[pl.ds(pl.multiple_of(c * ROW_CHUNK, ROW_CHUNK), ROW_CHUNK), :].astype(F32)


def _rows_before(ref, c):
    start = pl.multiple_of(jnp.maximum(c * ROW_CHUNK - HALO, 0), HALO)
    rows = ref[pl.ds(start, HALO), :].astype(F32)
    return jnp.where(c > 0, rows, 0.0)


def _rows_after(ref, c, n_chunks):
    start = pl.multiple_of(jnp.minimum((c + 1) * ROW_CHUNK, SEQ - HALO), HALO)
    rows = ref[pl.ds(start, HALO), :].astype(F32)
    return jnp.where(c < n_chunks - 1, rows, 0.0)


def _shift_down(z, before, n):
    return pltpu.roll(jnp.concatenate([before, z], axis=0), n, 0)[before.shape[0]:]


def _shift_up(z, after, n):
    rows = z.shape[0]
    return pltpu.roll(jnp.concatenate([z, after], axis=0), rows + HALO - n, 0)[:rows]


def _conv_rows(z, before, w):
    z1 = _shift_down(z, before, 1)
    z2 = _shift_down(z, before, 2)
    return w[2:3, :] * z + w[1:2, :] * z1 + w[0:1, :] * z2, z1, z2


def _conv_t_rows(dy, after, w):
    return w[2:3, :] * dy + w[1:2, :] * _shift_up(dy, after, 1) + w[0:1, :] * _shift_up(dy, after, 2)


N_ROW_CHUNKS = SEQ // ROW_CHUNK


FF_COLS = 256
N_FF_COLS = D_FF // FF_COLS


def _ffn_mid_bwd(name, gu, conv_w, da):
    def body(gu_ref, w_ref, da_ref, dgu_ref, dw_ref, dgc_ref):
        w = w_ref[...]

        def first(c, acc):
            g = _rows(gu_ref.at[0], c)
            u = _rows(gu_ref.at[1], c)
            d = _rows(da_ref, c)
            gc, g1, g2 = _conv_rows(g, _rows_before(gu_ref.at[0], c), w)
            sg = _sigmoid(gc)
            rows = pl.ds(pl.multiple_of(c * ROW_CHUNK, ROW_CHUNK), ROW_CHUNK)
            silu = gc * sg
            dgu_ref[1, rows, :] = (d * silu).astype(BF16)
            dgc = d * u * (sg + silu * (1.0 - sg))
            dgc_ref[rows, :] = dgc
            return (acc[0] + jnp.sum(dgc * g2, axis=0, keepdims=True), acc[1] + jnp.sum(dgc * g1, axis=0, keepdims=True),
                    acc[2] + jnp.sum(dgc * g, axis=0, keepdims=True))

        zero = jnp.zeros((1, FF_COLS), F32)
        acc = lax.fori_loop(0, N_ROW_CHUNKS, first, (zero, zero, zero))
        for r in range(3):
            dw_ref[r:r + 1, :] = acc[r]

        def second(c, carry):
            dgc = _rows(dgc_ref, c)
            dg = _conv_t_rows(dgc, _rows_after(dgc_ref, c, N_ROW_CHUNKS), w)
            dgu_ref[0, pl.ds(pl.multiple_of(c * ROW_CHUNK, ROW_CHUNK), ROW_CHUNK), :] = dg.astype(BF16)
            return carry

        lax.fori_loop(0, N_ROW_CHUNKS, second, 0)

    pair = _spec((2, SEQ, FF_COLS), lambda j: (0, 0, j))
    wspec = _spec((3, FF_COLS), lambda j: (0, j))
    return pl.pallas_call(
        body, name=name, grid=(N_FF_COLS,), in_specs=[pair, wspec, _spec((SEQ, FF_COLS), lambda j: (0, j))],
        out_specs=[pair, wspec], out_shape=[_act((2, SEQ, D_FF)), jax.ShapeDtypeStruct((3, D_FF), F32)],
        scratch_shapes=[pltpu.VMEM((SEQ, FF_COLS), F32)],
        compiler_params=_params(("parallel",)),
    )(gu, conv_w, da)


SC_COLS = 256
N_SC = D_MODEL // SC_COLS


def _sc_specs():
    return [_spec((SEQ, SC_COLS), lambda j, part=part: (0, part * N_SC + j)) for part in range(3)]


def _sc_mid_fwd(p, conv_w):
    def body(b_ref, c_ref, h_ref, w_ref, y_ref):
        w = w_ref[...]

        def chunk(c, carry):
            z = _rows(c_ref, c) * _rows(h_ref, c)
            before = _rows_before(c_ref, c) * _rows_before(h_ref, c)
            zc, _, _ = _conv_rows(z, before, w)
            y_ref[pl.ds(pl.multiple_of(c * ROW_CHUNK, ROW_CHUNK), ROW_CHUNK), :] = (_rows(b_ref, c) * zc).astype(BF16)
            return carry

        lax.fori_loop(0, N_ROW_CHUNKS, chunk, 0)

    col = _spec((SEQ, SC_COLS), lambda j: (0, j))
    return pl.pallas_call(
        body, name="sc_mid_fwd", grid=(N_SC,), in_specs=_sc_specs() + [_spec((3, SC_COLS), lambda j: (0, j))], out_specs=col,
        out_shape=jax.ShapeDtypeStruct((SEQ, D_MODEL), BF16), compiler_params=_params(("parallel",)),
    )(p, p, p, conv_w)


def _sc_mid_bwd(p, conv_w, dy):
    def body(b_ref, c_ref, h_ref, w_ref, dy_ref, db_ref, dc_ref, dh_ref, dw_ref, dzc_ref):
        w = w_ref[...]

        def first(c, acc):
            z = _rows(c_ref, c) * _rows(h_ref, c)
            before = _rows_before(c_ref, c) * _rows_before(h_ref, c)
            zc, z1, z2 = _conv_rows(z, before, w)
            d = _rows(dy_ref, c)
            rows = pl.ds(pl.multiple_of(c * ROW_CHUNK, ROW_CHUNK), ROW_CHUNK)
            db_ref[rows, :] = (d * zc).astype(BF16)
            dzc = d * _rows(b_ref, c)
            dzc_ref[rows, :] = dzc
            return (acc[0] + jnp.sum(dzc * z2, axis=0, keepdims=True), acc[1] + jnp.sum(dzc * z1, axis=0, keepdims=True),
                    acc[2] + jnp.sum(dzc * z, axis=0, keepdims=True))

        zero = jnp.zeros((1, SC_COLS), F32)
        acc = lax.fori_loop(0, N_ROW_CHUNKS, first, (zero, zero, zero))
        for r in range(3):
            dw_ref[r:r + 1, :] = acc[r]

        def second(c, carry):
            dz = _conv_t_rows(_rows(dzc_ref, c), _rows_after(dzc_ref, c, N_ROW_CHUNKS), w)
            rows = pl.ds(pl.multiple_of(c * ROW_CHUNK, ROW_CHUNK), ROW_CHUNK)
            dc_ref[rows, :] = (dz * _rows(h_ref, c)).astype(BF16)
            dh_ref[rows, :] = (dz * _rows(c_ref, c)).astype(BF16)
            return carry

        lax.fori_loop(0, N_ROW_CHUNKS, second, 0)

    col = _spec((SEQ, SC_COLS), lambda j: (0, j))
    wspec = _spec((3, SC_COLS), lambda j: (0, j))
    act = jax.ShapeDtypeStruct((SEQ, D_MODEL), BF16)
    return pl.pallas_call(
        body, name="sc_mid_bwd", grid=(N_SC,), in_specs=_sc_specs() + [wspec, col], out_specs=[col, col, col, wspec],
        out_shape=[act, act, act, jax.ShapeDtypeStruct((3, D_MODEL), F32)],
        scratch_shapes=[pltpu.VMEM((SEQ, SC_COLS), F32)], compiler_params=_params(("parallel",)),
    )(p, p, p, conv_w, dy)


GLA_GROUP = 4
GLA_ROWS = GLA_GROUP * CHUNK
N_GROUPS = N_CHUNKS // GLA_GROUP
Q0, K0, V0, R0, G0 = 0, KEY_DIM, 2 * KEY_DIM, 2 * KEY_DIM + VALUE_DIM, 2 * KEY_DIM + 2 * VALUE_DIM


def _tri(strict):
    r = lax.broadcasted_iota(jnp.int32, (CHUNK, CHUNK), 0)
    c = lax.broadcasted_iota(jnp.int32, (CHUNK, CHUNK), 1)
    return jnp.where(c < r if strict else c <= r, 1.0, 0.0).astype(F32)


def _cumsum_rows(tri, x):
    tri = tri.astype(BF16)
    total = None
    for _ in range(3):
        term = x.astype(BF16)
        x = x - term.astype(F32)
        product = jnp.dot(tri, term, preferred_element_type=F32)
        total = product if total is None else total + product
    return total


def _gate_logits(gl, wgu, b_gate):
    return jnp.dot(gl, wgu, preferred_element_type=F32) + b_gate


def _log_decay(logits):
    return (jnp.minimum(logits, 0.0) - jnp.log(1.0 + jnp.exp(-jnp.abs(logits)))) * (1.0 / GATE_NORMALIZER)


def _head(x, h, width):
    return x[:, h * width:(h + 1) * width]


def _gla_fwd(proj, wgu, b_gate, gn):
    def body(p_ref, wgu_ref, b_ref, gn_ref, o_ref, og_ref, st_ref, state):
        @pl.when(pl.program_id(0) == 0)
        def _():
            state[...] = jnp.zeros_like(state)

        tri = _tri(False)
        la = _log_decay(_gate_logits(p_ref[:, G0:G0 + GATE_PAD], wgu_ref[...], b_ref[...]))
        decays = []
        for c in range(GLA_GROUP):
            rows = slice(c * CHUNK, (c + 1) * CHUNK)
            cum = _cumsum_rows(tri, la[rows])
            tot = cum[CHUNK - 1:CHUNK, :]
            kd = (p_ref[rows, K0:K0 + KEY_DIM].astype(F32) * jnp.exp(tot - cum)).astype(BF16)
            decays.append(jnp.exp(tot))
            v = p_ref[rows, V0:V0 + VALUE_DIM]
            for h in range(GLA_HEADS):
                st_ref[c, h] = lax.dot_general(
                    _head(v, h, HEAD_V), _head(kd, h, HEAD_K), (TN, ((), ())), preferred_element_type=F32)
        for c in range(GLA_GROUP):
            for h in range(GLA_HEADS):
                s = state[h] * _head(decays[c], h, HEAD_K) + st_ref[c, h]
                state[h] = s
                st_ref[c, h] = s
        for c in range(GLA_GROUP):
            rows = slice(c * CHUNK, (c + 1) * CHUNK)
            q = (p_ref[rows, Q0:Q0 + KEY_DIM].astype(F32) * (HEAD_K ** -0.5)).astype(BF16)
            for h in range(GLA_HEADS):
                o_ref[rows, h * HEAD_V:(h + 1) * HEAD_V] = lax.dot_general(
                    _head(q, h, HEAD_K), st_ref[c, h].astype(BF16), (NT, ((), ())), preferred_element_type=F32)
        r = p_ref[:, R0:R0 + VALUE_DIM].astype(F32)
        gate = r * _sigmoid(r) * gn_ref[...]
        for h in range(GLA_HEADS):
            cols = slice(h * HEAD_V, (h + 1) * HEAD_V)
            o = o_ref[:, cols]
            og_ref[:, cols] = (o * _rstd(o) * gate[:, cols]).astype(BF16)

    rows = _spec((GLA_ROWS, VALUE_DIM), lambda i: (i, 0))
    const = lambda shape: _spec(shape, lambda i: (0,) * len(shape))
    return pl.pallas_call(
        body, name="gla_fwd", grid=(N_GROUPS,),
        in_specs=[_spec((GLA_ROWS, PROJ_A_PAD), lambda i: (i, 0)), const((GATE_PAD, KEY_DIM)), const((1, KEY_DIM)),
                  const((1, VALUE_DIM))],
        out_specs=[rows, rows, _spec((GLA_GROUP, GLA_HEADS, HEAD_V, HEAD_K), lambda i: (i, 0, 0, 0))],
        out_shape=[jax.ShapeDtypeStruct((SEQ, VALUE_DIM), F32), jax.ShapeDtypeStruct((SEQ, VALUE_DIM), BF16),
                   jax.ShapeDtypeStruct((N_CHUNKS, GLA_HEADS, HEAD_V, HEAD_K), F32)],
        scratch_shapes=[pltpu.VMEM((GLA_HEADS, HEAD_V, HEAD_K), F32)], compiler_params=_params(("arbitrary",)),
    )(proj, wgu, b_gate, gn)


def _gla_bwd(proj, wgu, b_gate, gn, o, states, dog):
    last = N_GROUPS - 1

    def body(p_ref, wgu_ref, b_ref, gn_ref, o_ref, st_ref, stp_ref, dog_ref, dp_ref, dwgu_ref, db_ref, dgn_ref, carry, do_buf,
             g_buf):
        step = pl.program_id(0)

        @pl.when(step == 0)
        def _():
            carry[...] = jnp.zeros_like(carry)

        r = p_ref[:, R0:R0 + VALUE_DIM].astype(F32)
        sr = _sigmoid(r)
        silu = r * sr
        gn_row = gn_ref[...]
        dog_rows = dog_ref[...].astype(F32)
        dn = dog_rows * silu
        dgn_cols = []
        for h in range(GLA_HEADS):
            cols = slice(h * HEAD_V, (h + 1) * HEAD_V)
            oh = o_ref[:, cols]
            rs = _rstd(oh)
            ohat = oh * rs
            dn_h = dn[:, cols]
            dgn_cols.append(jnp.sum(dn_h * ohat, axis=0, keepdims=True))
            dohat = dn_h * gn_row[:, cols]
            do_buf[:, cols] = rs * (dohat - ohat * jnp.mean(dohat * ohat, axis=-1, keepdims=True))
            n_h = ohat * gn_row[:, cols]
            dp_ref[:, R0 + h * HEAD_V:R0 + (h + 1) * HEAD_V] = (
                dog_rows[:, cols] * n_h * (sr[:, cols] * (1.0 + r[:, cols] * (1.0 - sr[:, cols])))).astype(BF16)
        dgn = jnp.concatenate(dgn_cols, axis=1)

        tri = _tri(False)
        tri_strict = _tri(True)
        gl = p_ref[:, G0:G0 + GATE_PAD]
        logits = _gate_logits(gl, wgu_ref[...], b_ref[...])
        la = _log_decay(logits)
        fades, kds, decays = [], [], []
        for c in range(GLA_GROUP):
            rows = slice(c * CHUNK, (c + 1) * CHUNK)
            cum = _cumsum_rows(tri, la[rows])
            tot = cum[CHUNK - 1:CHUNK, :]
            fades.append(jnp.exp(tot - cum))
            kds.append(p_ref[rows, K0:K0 + KEY_DIM].astype(F32) * fades[c])
            decays.append(jnp.exp(tot))
            q = (p_ref[rows, Q0:Q0 + KEY_DIM].astype(F32) * (HEAD_K ** -0.5)).astype(BF16)
            do = do_buf[rows, :].astype(BF16)
            for h in range(GLA_HEADS):
                do_h = _head(do, h, HEAD_V)
                dq = jnp.dot(do_h, st_ref[c, h].astype(BF16), preferred_element_type=F32) * (HEAD_K ** -0.5)
                dp_ref[rows, Q0 + h * HEAD_K:Q0 + (h + 1) * HEAD_K] = dq.astype(BF16)
                g_buf[c, h] = lax.dot_general(do_h, _head(q, h, HEAD_K), (TN, ((), ())), preferred_element_type=F32)
        for c in reversed(range(GLA_GROUP)):
            for h in range(GLA_HEADS):
                g = carry[h] + g_buf[c, h]
                g_buf[c, h] = g
                carry[h] = g * _head(decays[c], h, HEAD_K)
        dlogit_rows = []
        for c in range(GLA_GROUP):
            rows = slice(c * CHUNK, (c + 1) * CHUNK)
            v = p_ref[rows, V0:V0 + VALUE_DIM]
            kd = kds[c].astype(BF16)
            dkd_cols, ddecay_cols = [], []
            for h in range(GLA_HEADS):
                g = g_buf[c, h]
                g16 = g.astype(BF16)
                dkd_cols.append(jnp.dot(_head(v, h, HEAD_V), g16, preferred_element_type=F32))
                dv = lax.dot_general(_head(kd, h, HEAD_K), g16, (NT, ((), ())), preferred_element_type=F32)
                dp_ref[rows, V0 + h * HEAD_V:V0 + (h + 1) * HEAD_V] = dv.astype(BF16)
                if c > 0:
                    s_prev = st_ref[c - 1, h]
                else:
                    s_prev = jnp.where(step < last, stp_ref[0, h], 0.0)
                ddecay_cols.append(jnp.sum(g * s_prev, axis=0, keepdims=True))
            dkd = jnp.concatenate(dkd_cols, axis=1)
            ddecay = jnp.concatenate(ddecay_cols, axis=1)
            dp_ref[rows, K0:K0 + KEY_DIM] = (dkd * fades[c]).astype(BF16)
            e = dkd * kds[c]
            dla = ddecay * decays[c] + _cumsum_rows(tri_strict, e)
            dlogit_rows.append(dla * (1.0 / GATE_NORMALIZER) * (1.0 - _sigmoid(logits[rows])))
        dlogit = jnp.concatenate(dlogit_rows, axis=0)
        dlogit16 = dlogit.astype(BF16)
        dp_ref[:, G0:G0 + GATE_PAD] = lax.dot_general(
            dlogit16, wgu_ref[...], (NT, ((), ())), preferred_element_type=F32).astype(BF16)
        dwgu = lax.dot_general(gl, dlogit16, (TN, ((), ())), preferred_element_type=F32)
        db = jnp.sum(dlogit, axis=0, keepdims=True)

        @pl.when(step == 0)
        def _():
            dwgu_ref[...] = dwgu
            db_ref[...] = db
            dgn_ref[...] = dgn

        @pl.when(step > 0)
        def _():
            dwgu_ref[...] += dwgu
            db_ref[...] += db
            dgn_ref[...] += dgn

    rev = lambda i: (last - i, 0)
    rows = _spec((GLA_ROWS, VALUE_DIM), rev)
    const = lambda shape: _spec(shape, lambda i: (0,) * len(shape))
    st_shape = (GLA_HEADS, HEAD_V, HEAD_K)
    return pl.pallas_call(
        body, name="gla_bwd", grid=(N_GROUPS,),
        in_specs=[_spec((GLA_ROWS, PROJ_A_PAD), rev), const((GATE_PAD, KEY_DIM)), const((1, KEY_DIM)), const((1, VALUE_DIM)),
                  rows, _spec((GLA_GROUP,) + st_shape, lambda i: (last - i, 0, 0, 0)),
                  _spec((1,) + st_shape, lambda i: (jnp.maximum((last - i) * GLA_GROUP - 1, 0), 0, 0, 0)), rows],
        out_specs=[_spec((GLA_ROWS, PROJ_A_PAD), rev), const((GATE_PAD, KEY_DIM)), const((1, KEY_DIM)), const((1, VALUE_DIM))],
        out_shape=[jax.ShapeDtypeStruct((SEQ, PROJ_A_PAD), BF16), jax.ShapeDtypeStruct((GATE_PAD, KEY_DIM), F32),
                   jax.ShapeDtypeStruct((1, KEY_DIM), F32), jax.ShapeDtypeStruct((1, VALUE_DIM), F32)],
        scratch_shapes=[pltpu.VMEM(st_shape, F32), pltpu.VMEM((GLA_ROWS, VALUE_DIM), F32), pltpu.VMEM((GLA_GROUP,) + st_shape, F32)],
        compiler_params=_params(("arbitrary",)),
    )(proj, wgu, b_gate, gn, o, states, states, dog)


WGRAD_FF_TILE = D_FF // 2


CARRY_ROWS = 8
UP_ROWS = 512


def _ffn_up_mid(name, x, gamma, w_up_t, conv_w):
    def body(x_ref, g_ref, w_ref, c_ref, h_ref, gu_ref, a_ref, carry):
        @pl.when(pl.program_id(0) == 0)
        def _():
            carry[...] = jnp.zeros_like(carry)

        x_tile = x_ref[...]
        h_tile = (x_tile * _rstd(x_tile) * g_ref[...]).astype(BF16)
        h_ref[...] = h_tile
        for k in range(N_FF_COLS):
            cols = slice(k * FF_COLS, (k + 1) * FF_COLS)
            g, u = (lax.dot_general(h_tile, w_ref[p, cols, :], (NT, ((), ())), preferred_element_type=F32).astype(BF16)
                    for p in range(2))
            gu_ref[0, :, cols] = g
            gu_ref[1, :, cols] = u
            g = g.astype(F32)
            w = c_ref[:, cols]
            before = carry[:, cols]
            gc = w[2:3, :] * g + w[1:2, :] * _shift_down(g, before, 1) + w[0:1, :] * _shift_down(g, before, 2)
            a_ref[:, cols] = (gc * _sigmoid(gc) * u.astype(F32)).astype(BF16)
            carry[:, cols] = g[UP_ROWS - CARRY_ROWS:, :]

    row = _spec((UP_ROWS, D_MODEL), lambda i: (i, 0))
    return pl.pallas_call(
        body, name=name, grid=(SEQ // UP_ROWS,),
        in_specs=[row, _resident((1, D_MODEL)), _resident((2, D_FF, D_MODEL)), _resident((3, D_FF))],
        out_specs=[row, _spec((2, UP_ROWS, D_FF), lambda i: (0, i, 0)), _spec((UP_ROWS, D_FF), lambda i: (i, 0))],
        out_shape=[_act(), _act((2, SEQ, D_FF)), _act((SEQ, D_FF))], scratch_shapes=[pltpu.VMEM((CARRY_ROWS, D_FF), F32)],
        compiler_params=_params(("arbitrary",)),
    )(x, gamma, w_up_t, conv_w)


def _ffn_fwd(tag, x, gamma, w_up_t, conv_w, w_down):
    h, gu, a = _ffn_up_mid(f"ffn{tag}_up_mid", x, gamma, w_up_t, conv_w)
    return _rows_matmul(f"ffn{tag}_down", a, w_down, NN, x), (h, gu, a)


def _owner_blocks(d, rows=None):
    if rows is not None:
        d = d[:rows]
    return d.reshape((N_DEV, -1) + d.shape[-1:])


def _ffn_bwd(tag, x, gamma, w_up_t, conv_w, w_down, saved, dx, dx16, swap):
    h, gu, a = saved
    da = _rows_matmul(f"ffn{tag}_da", dx16, w_down, NT)
    d_w_down = _owner_blocks(_wgrad_cols_tn(f"ffn{tag}_dwdown", a, WGRAD_FF_TILE, dx16))
    dgu, d_conv = _ffn_mid_bwd(f"ffn{tag}_mid_bwd", gu, conv_w, da)
    d_w_up_t = _owner_blocks(_wgrad_halves_tn(f"ffn{tag}_dwup", dgu, WGRAD_FF_TILE, h))
    parts = (d_w_up_t, d_w_down)
    dx, dx16, d_gamma, *received = _sum_blocks_nn(
        f"ffn{tag}_dh", dgu, w_up_t, norm=(x, gamma, dx), swap=parts if swap else ())
    return dx, dx16, d_gamma, d_conv, parts, received


def _local_step(x, target, w, fetch=None, emit=None):
    if fetch is None:
        local = dict(a=(w.get("a_w_in"), w.get("a_w_out")), b=(w.get("b_w_in"), w.get("b_w_out")))
        for layer in range(2):
            local[f"f{layer}"] = (w["f_w_up"][layer], w["f_w_down"][layer]) if "f_w_up" in w else None
        fetch = lambda group, after: local[group]
    swap = emit is not None
    if emit is None:
        emit = lambda group, parts, received, dx: dx
    f_norm = (w["f_norm"][0:1], w["f_norm"][1:2])

    x0 = x
    a_w_in, a_w_out = fetch("a", x0)
    h0, proj = _norm_proj("a_in", x0, w["a_norm"], a_w_in)
    o, og, states = _gla_fwd(proj, w["a_w_gate_up"], w["a_b_gate"], w["a_gn"])
    x1 = _rows_matmul("a_out", og, a_w_out, NN, x0)
    up0, down0 = fetch("f0", x1)
    x2, ffn0 = _ffn_fwd(0, x1, f_norm[0], up0, w["f_conv"][0], down0)
    b_w_in, b_w_out = fetch("b", x2)
    h2, p = _norm_proj("b_in", x2, w["b_norm"], b_w_in)
    y = _sc_mid_fwd(p, w["b_conv"])
    x3 = _rows_matmul("b_out", y, b_w_out, NN, x2)
    up1, down1 = fetch("f1", x3)
    ffn1 = _ffn_up_mid("ffn1_up_mid", x3, f_norm[1], up1, w["f_conv"][1])
    loss, dx, dx16, d_final_norm = _down_loss_head(ffn1[2], down1, x3, w["final_norm"], target)

    dx, dx16, d_f_norm1, d_fconv1, parts_f1, got = _ffn_bwd(
        1, x3, f_norm[1], up1, w["f_conv"][1], down1, ffn1, dx, dx16, swap)
    dx16 = emit("f1", parts_f1, got, dx16)

    dy = _rows_matmul("b_dy", dx16, b_w_out, NT)
    d_b_w_out = _owner_blocks(_wgrad_cols_tn("b_dwout", y, OUT_TILE, dx16))
    db, dc, dhh, d_b_conv = _sc_mid_bwd(p, w["b_conv"], dy)
    dp = jnp.concatenate([db, dc, dhh], axis=1)
    parts_b = (_wgrad_cols_transposed_tn("b_dwin", h2, dp, B_SHARD), d_b_w_out)
    dx, dx16, d_b_norm, *got = _sum_cols_nt("b_dh", dp, b_w_in, norm=(x2, w["b_norm"], dx), swap=parts_b if swap else ())
    dx16 = emit("b", parts_b, got, dx16)

    dx, dx16, d_f_norm0, d_fconv0, parts_f0, got = _ffn_bwd(
        0, x1, f_norm[0], up0, w["f_conv"][0], down0, ffn0, dx, dx16, swap)
    dx16 = emit("f0", parts_f0, got, dx16)

    dog = _rows_matmul("a_dog", dx16, a_w_out, NT)
    d_a_w_out = _owner_blocks(_wgrad_cols_tn("a_dwout", og, OUT_TILE, dx16))
    dproj, d_wgu, d_b_gate, d_gn = _gla_bwd(proj, w["a_w_gate_up"], w["a_b_gate"], w["a_gn"], o, states, dog)
    parts_a = (_owner_blocks(_wgrad_cols_tn("a_dwin", dproj, PA_TILE, h0), PROJ_A), d_a_w_out)
    dx, _, d_a_norm, *got = _wide_nn("a_dh", dproj, a_w_in, norm=(x0, w["a_norm"], dx), swap=parts_a if swap else ())
    emit("a", parts_a, got, dx)

    grads = dict(
        a_norm=d_a_norm, a_w_in=parts_a[0], a_w_gate_up=d_wgu, a_b_gate=d_b_gate, a_gn=d_gn, a_w_out=parts_a[1],
        b_norm=d_b_norm, b_w_in=parts_b[0], b_conv=d_b_conv, b_w_out=parts_b[1],
        f_norm=(d_f_norm0, d_f_norm1), f_w_up=(parts_f0[0], parts_f1[0]), f_conv=(d_fconv0, d_fconv1),
        f_w_down=(parts_f0[1], parts_f1[1]), final_norm=d_final_norm)
    grads["loss"] = loss
    return dx, grads


MESH_ID = pl.DeviceIdType.MESH
ANY = pl.BlockSpec(memory_space=pl.ANY)
N_PEERS = N_DEV - 1


def _position():
    return lax.axis_index("x"), lax.axis_index("y"), lax.axis_index("c")


def _slot(px, py, pc):
    return 4 * px + 2 * py + pc


GATHER_COPIES = 8
HALF_ROWS = 16


def _gather_copies(src, out, send_sems, recv_sems, local_sems):
    n = len(src)
    to_sibling, to_x, to_y, x_on_to_y, y_on_to_x, x_to_sibling, y_to_sibling, diagonal_to_sibling = range(GATHER_COPIES)
    x, y, c = _position()
    me, sibling = (x, y, c), (x, y, 1 - c)
    x_side, y_side, diagonal = (1 - x, y), (x, 1 - y), (1 - x, 1 - y)

    def rows_of(t, half):
        rows = src[t].shape[0]
        half_rows = rows // 2 // HALF_ROWS * HALF_ROWS
        return (pl.ds(0, rows), pl.ds(0, half_rows), pl.ds(half_rows, rows - half_rows))[half]

    def copy(t, j, block, to, half=0, from_input=False):
        dst = out[t].at[_slot(*block), rows_of(t, half)]
        return pltpu.make_async_remote_copy(
            src_ref=src[t] if from_input else dst, dst_ref=dst, send_sem=send_sems.at[GATHER_COPIES * t + j],
            recv_sem=recv_sems.at[GATHER_COPIES * t + j], device_id=to, device_id_type=MESH_ID)

    mine = [pltpu.make_async_copy(src[t], out[t].at[_slot(*me)], local_sems.at[t]) for t in range(n)]
    for cp in mine:
        cp.start()
    sent = []

    def start(cp):
        cp.start()
        sent.append(cp)

    for t in range(n):
        start(copy(t, to_sibling, me, sibling, from_input=True))
        start(copy(t, to_x, me, (*x_side, c), from_input=True))
        start(copy(t, to_y, me, (*y_side, c), from_input=True))
    for t in range(n):
        copy(t, to_x, (*x_side, c), me).wait_recv()
        start(copy(t, x_on_to_y, (*x_side, c), (*y_side, c), half=1))
        start(copy(t, x_to_sibling, (*x_side, c), sibling))
        copy(t, to_y, (*y_side, c), me).wait_recv()
        start(copy(t, y_on_to_x, (*y_side, c), (*x_side, c), half=2))
        start(copy(t, y_to_sibling, (*y_side, c), sibling))
    for t in range(n):
        copy(t, x_on_to_y, (*diagonal, c), me, half=1).wait_recv()
        copy(t, y_on_to_x, (*diagonal, c), me, half=2).wait_recv()
        start(copy(t, diagonal_to_sibling, (*diagonal, c), sibling))
    for t in range(n):
        copy(t, to_sibling, sibling, me).wait_recv()
        for j, chip in ((x_to_sibling, x_side), (y_to_sibling, y_side), (diagonal_to_sibling, diagonal)):
            copy(t, j, (*chip, 1 - c), me).wait_recv()
    for cp in sent:
        cp.wait_send()
    for cp in mine:
        cp.wait()


def _all_gather(name, shards):
    n = len(shards)

    def body(*refs):
        _gather_copies(refs[:n], refs[n:2 * n], *refs[2 * n:])

    sems = pltpu.SemaphoreType.DMA((GATHER_COPIES * n,))
    return pl.pallas_call(
        body, name=name, in_specs=[ANY] * n, out_specs=[ANY] * n,
        out_shape=[jax.ShapeDtypeStruct((N_DEV,) + s.shape, s.dtype) for s in shards],
        scratch_shapes=[sems, sems, pltpu.SemaphoreType.DMA((n,))],
    )(*shards)


SIBLING_AND_NEIGHBOURS = (1, 2, 4)
SAME_CORE = (2, 4, 6)


def _flip(x, y, c, k):
    return x ^ (k >> 2), y ^ ((k >> 1) & 1), c ^ (k & 1)


N_CHIPS = N_DEV // 2


def _chip(px, py):
    return 2 * px + py


def _pair_copies(parts, received, send_sems, recv_sems):
    x, y, c = lax.axis_index("x"), lax.axis_index("y"), lax.axis_index("c")
    sibling = (x, y, 1 - c)
    copies = []
    for t in range(len(parts)):
        for q in range(N_DEV // 2):
            send = pltpu.make_async_remote_copy(
                src_ref=parts[t].at[2 * q + 1 - c], dst_ref=received[t].at[q], send_sem=send_sems.at[t, q],
                recv_sem=recv_sems.at[t, q], device_id=sibling, device_id_type=pl.DeviceIdType.MESH)
            landed = received[t].at[q]
            arrival = pltpu.make_async_remote_copy(
                src_ref=landed, dst_ref=landed, send_sem=send_sems.at[t, q], recv_sem=recv_sems.at[t, q],
                device_id=sibling, device_id_type=pl.DeviceIdType.MESH)
            copies.append((send, arrival))
    return copies


def _pair_add(name, parts, received, side):
    n = len(parts)

    def body(side_ref, *refs):
        for t in range(n):
            refs[2 * n + t][...] = (refs[t][...].astype(F32) + refs[n + t][...].astype(F32)).astype(BF16)

    own = [_spec((None,) + p.shape[1:], lambda q, side_ref: (2 * q + side_ref[0], 0, 0)) for p in parts]
    chip = [_spec((None,) + p.shape[1:], lambda q, side_ref: (q, 0, 0)) for p in parts]
    return pl.pallas_call(
        body, name=name,
        grid_spec=pltpu.PrefetchScalarGridSpec(num_scalar_prefetch=1, grid=(N_CHIPS,), in_specs=own + chip, out_specs=chip),
        out_shape=[jax.ShapeDtypeStruct((N_CHIPS,) + p.shape[1:], BF16) for p in parts], compiler_params=_params(("parallel",)),
    )(side, *parts, *received)


def _send_copy(parts, landing, send_sems, recv_sems, t, s, k):
    x, y, c = _position()
    px, py, _ = _flip(x, y, c, k)
    return pltpu.make_async_remote_copy(
        src_ref=parts[t].at[_chip(px, py)], dst_ref=landing[t].at[_chip(x, y)], send_sem=send_sems.at[s],
        recv_sem=recv_sems.at[s], device_id=(px, py, c), device_id_type=MESH_ID)


def _send_arrival(landing, send_sems, recv_sems, t, s, k):
    x, y, c = _position()
    px, py, _ = _flip(x, y, c, k)
    landed = landing[t].at[_chip(px, py)]
    return pltpu.make_async_remote_copy(
        src_ref=landed, dst_ref=landed, send_sem=send_sems.at[s], recv_sem=recv_sems.at[s],
        device_id=(px, py, c), device_id_type=MESH_ID)


def _handshake(peers):
    x, y, c = _position()
    barrier = pltpu.get_barrier_semaphore()
    for k in peers:
        pl.semaphore_signal(barrier, inc=1, device_id=_flip(x, y, c, k), device_id_type=MESH_ID)
    pl.semaphore_wait(barrier, len(peers))


def _sequencer(name, collective_id, n_copies, body, operands, out_type):
    n_arrays = len(operands)
    return pl.kernel(
        body, out_type=out_type, mesh=plsc.ScalarSubcoreMesh(axis_name="sequencer", num_cores=1), name=name,
        scratch_types=(pltpu.SemaphoreType.DMA((n_copies,)), pltpu.SemaphoreType.DMA((n_copies,)),
                       pltpu.SemaphoreType.DMA((n_arrays,))),
        compiler_params=pltpu.CompilerParams(collective_id=collective_id))(*operands)


def _sequencer_exchange(name, collective_id, parts, after=()):
    n, n_peers, n_in = len(parts), len(SAME_CORE), len(parts) + len(after)

    def body(*refs):
        src, landing = refs[:n], refs[n_in:n_in + n]
        send_sems, recv_sems, local_sems = refs[n_in + n:]
        _handshake(SAME_CORE)
        x, y, _ = _position()
        mine = [pltpu.make_async_copy(src[t].at[_chip(x, y)], landing[t].at[_chip(x, y)], local_sems.at[t]) for t in range(n)]
        for cp in mine:
            cp.start()
        sent = [_send_copy(src, landing, send_sems, recv_sems, t, t * n_peers + j, k)
                for t in range(n) for j, k in enumerate(SAME_CORE)]
        for cp in sent:
            cp.start()
        for t in range(n):
            for j, k in enumerate(SAME_CORE):
                _send_arrival(landing, send_sems, recv_sems, t, t * n_peers + j, k).wait_recv()
        for cp in sent:
            cp.wait_send()
        for cp in mine:
            cp.wait()

    landing = [jax.ShapeDtypeStruct(p.shape, p.dtype) for p in parts]
    return _sequencer(name, collective_id, n * n_peers, body, list(parts) + list(after), landing)


def _sequencer_gather(name, collective_id, shards):
    n = len(shards)

    def body(*refs):
        _handshake(SIBLING_AND_NEIGHBOURS)
        _gather_copies(refs[:n], refs[n:2 * n], *refs[2 * n:])

    gathered = [jax.ShapeDtypeStruct((N_DEV,) + s.shape, s.dtype) for s in shards]
    return _sequencer(name, collective_id, GATHER_COPIES * n, body, shards, gathered)


ADAM_ROWS = 512
BF16_ROWS = 16


def _adam_update(w, g, m, v):
    m = ADAM_B1 * m + (1.0 - ADAM_B1) * g
    v = ADAM_B2 * v + (1.0 - ADAM_B2) * (g * g)
    m_hat = m / (1.0 - ADAM_B1 ** ADAM_STEP)
    v_hat = v / (1.0 - ADAM_B2 ** ADAM_STEP)
    delta = -ADAM_LR * (m_hat / (jnp.sqrt(v_hat) + ADAM_EPS) + ADAM_WD * w)
    return delta, m, v


def _sum_slots(ref):
    total = ref[0].astype(F32)
    for d in range(1, ref.shape[0]):
        total = total + ref[d].astype(F32)
    return total


def _adamw_sum(name, landed, w, m, v):
    layers, rows, cols = w.shape
    tiles = [t for t in range(ADAM_ROWS, 0, -BF16_ROWS) if rows % t == 0]
    tr = tiles[0] if tiles else rows
    nt = rows // tr

    def body(*refs):
        parts = refs[:layers]
        w_ref, m_ref, v_ref, g_ref, d_ref, nm_ref, nv_ref = refs[layers:]
        layer = pl.program_id(0)
        g = _sum_slots(parts[0])
        for q in range(1, layers):
            g = jnp.where(layer == q, _sum_slots(parts[q]), g)
        delta, new_m, new_v = _adam_update(w_ref[...], g, m_ref[...], v_ref[...])
        g_ref[...] = g
        d_ref[...] = delta
        nm_ref[...] = new_m
        nv_ref[...] = new_v

    def part_spec(q):
        return _spec((N_CHIPS, tr, cols), lambda l, i: (0, jnp.where(l == q, i, jnp.where(l < q, 0, nt - 1)), 0))

    tile = _spec((None, tr, cols), lambda l, i: (l, i, 0))
    out = jax.ShapeDtypeStruct((layers, rows, cols), F32)
    return pl.pallas_call(
        body, name=name, grid=(layers, nt), in_specs=[part_spec(q) for q in range(layers)] + [tile] * 3,
        out_specs=[tile] * 4, out_shape=[out] * 4, compiler_params=_params(("arbitrary", "arbitrary")),
    )(*landed, w, m, v)


def _sum_small(landed):
    def body(in_ref, out_ref):
        out_ref[...] = _sum_slots(in_ref)

    return pl.pallas_call(body, name="small_grad_sum", out_shape=jax.ShapeDtypeStruct(landed.shape[1:], F32))(landed)


def _adamw_small(arrays):
    n = len(arrays)

    def body(*refs):
        for i in range(n):
            g_ref, w_ref, m_ref, v_ref = refs[4 * i:4 * i + 4]
            d_ref, nm_ref, nv_ref = refs[4 * n + 3 * i:4 * n + 3 * i + 3]
            d_ref[...], nm_ref[...], nv_ref[...] = _adam_update(w_ref[...], g_ref[...], m_ref[...], v_ref[...])

    out = [jax.ShapeDtypeStruct(w.shape, F32) for _, w, _, _ in arrays for _ in range(3)]
    flat = pl.pallas_call(body, name="adam_small", out_shape=out)(*[a for group in arrays for a in group])
    return [tuple(flat[3 * i:3 * i + 3]) for i in range(n)]


LANES = 128
SUBLANES = 8
F_CONV_SHARD = D_FF // N_DEV
GATE_SHARD = KEY_DIM // N_DEV
NORM_SHARD = D_MODEL // N_DEV


def _tile_rows(a):
    flat = a.reshape(-1)
    size = -(-flat.shape[0] // (SUBLANES * LANES)) * SUBLANES * LANES
    return jnp.pad(flat, (0, size - flat.shape[0])).reshape(-1, LANES)


def _pack_rows(pieces):
    return jnp.concatenate([_tile_rows(p) for p in pieces], axis=0)


def _unpack_rows(packed, shapes):
    out, row = [], 0
    for shape in shapes:
        size = 1
        for s in shape:
            size *= s
        rows = -(-size // (SUBLANES * LANES)) * SUBLANES
        piece = packed[..., row:row + rows, :]
        out.append(piece.reshape(piece.shape[:-2] + (rows * LANES,))[..., :size])
        row += rows
    return out


SMALL_SHARDS = ((GATE_RANK, GATE_SHARD), (1, NORM_SHARD), (3, NORM_SHARD), (2, 3, F_CONV_SHARD))


def _unpack_small_shards(g):
    gate, b_norm, b_conv, f_conv = _unpack_rows(g, SMALL_SHARDS)
    gate = gate.reshape(N_DEV, GATE_RANK, GATE_SHARD).transpose(1, 0, 2).reshape(GATE_RANK, KEY_DIM)
    b_norm = b_norm.reshape(1, D_MODEL)
    b_conv = b_conv.reshape(N_DEV, 3, NORM_SHARD).transpose(1, 0, 2).reshape(3, D_MODEL)
    f_conv = f_conv.reshape(N_DEV, 2, 3, F_CONV_SHARD).transpose(1, 2, 0, 3).reshape(2, 3, D_FF)
    return gate, b_norm, b_conv, f_conv


SMALL_LAYOUT = (("a_norm", (1, D_MODEL)), ("a_w_gate_up", (GATE_RANK, KEY_DIM)), ("a_b_gate", (1, KEY_DIM)), ("a_gn", (1, VALUE_DIM)),
                ("b_norm", (1, D_MODEL)), ("b_conv", (3, D_MODEL)), ("f_norm0", (1, D_MODEL)), ("f_norm1", (1, D_MODEL)),
                ("f_conv0", (3, D_FF)), ("f_conv1", (3, D_FF)), ("final_norm", (1, D_MODEL)), ("loss", (1, LANES)))


def _pack_small_grads(g):
    full = dict(g)
    full["a_w_gate_up"] = g["a_w_gate_up"][:GATE_RANK]
    for layer in range(2):
        full[f"f_norm{layer}"] = g["f_norm"][layer]
        full[f"f_conv{layer}"] = g["f_conv"][layer]
    return _pack_rows([full[name] for name, _ in SMALL_LAYOUT])


def _unpack_small_grads(packed):
    pieces = _unpack_rows(packed, [shape for _, shape in SMALL_LAYOUT])
    out = {name: piece.reshape(shape) for (name, shape), piece in zip(SMALL_LAYOUT, pieces)}
    out["f_norm"] = jnp.stack([out["f_norm0"][0], out["f_norm1"][0]])
    out["f_conv"] = jnp.stack([out["f_conv0"], out["f_conv1"]])
    return out


def kernel(x, a_norm, a_w_in, a_w_gate_up, a_b_gate, a_gn, a_w_out, b_norm, b_w_in, b_conv, b_w_out, f_norm, f_w_up, f_conv, f_w_down, final_norm, loss_target, m_a_norm, m_a_w_in, m_a_w_gate_up, m_a_b_gate, m_a_gn, m_a_w_out, m_b_norm, m_b_w_in, m_b_conv, m_b_w_out, m_f_norm, m_f_w_up, m_f_conv, m_f_w_down, m_final_norm, v_a_norm, v_a_w_in, v_a_w_gate_up, v_a_b_gate, v_a_gn, v_a_w_out, v_b_norm, v_b_w_in, v_b_conv, v_b_w_out, v_f_norm, v_f_w_up, v_f_conv, v_f_w_down, v_final_norm):
    my_slot = _slot(*_position())

    transposed = lambda w: jnp.swapaxes(w, 1, 2)
    a_transposed = lambda w: w.reshape(D_MODEL, A_SHARD).T.reshape(1, A_SHARD, D_MODEL)
    a_w_in_t, f_w_up_t = a_transposed(a_w_in), transposed(f_w_up)
    first = _all_gather("weight_gather", [a_w_in_t[0].astype(BF16), a_w_out[0].astype(BF16),
                                          _pack_rows([a_w_gate_up[0], b_norm, b_conv[0], f_conv])])
    gathers, small_shards = {}, first[2]
    later = (("f0", f_w_up_t[0], f_w_down[0]), ("b", b_w_in[0], b_w_out[0]), ("f1", f_w_up_t[1], f_w_down[1]))
    for collective_id, (group, w_in, w_out) in enumerate(later):
        w_in, w_out, small_shards = lax.optimization_barrier((w_in.astype(BF16), w_out.astype(BF16), small_shards))
        gathers[group] = _sequencer_gather(f"gather_{group}", collective_id, [w_in, w_out])
    gate_full, b_norm_full, b_conv_full, f_conv_full = _unpack_small_shards(small_shards)
    a_w_in_full = jnp.pad(first[0].reshape(PROJ_A, D_MODEL), ((0, PROJ_A_PAD - PROJ_A), (0, 0)))
    weights = dict(
        a_norm=a_norm, a_w_gate_up=jnp.pad(gate_full, ((0, GATE_PAD - GATE_RANK), (0, 0))).astype(BF16), a_b_gate=a_b_gate,
        a_gn=a_gn, b_norm=b_norm_full, b_conv=b_conv_full, f_norm=f_norm, f_conv=f_conv_full,
        final_norm=final_norm.reshape(1, D_MODEL))

    def fetch(group, after):
        if group == "a":
            return a_w_in_full, first[1].reshape(D_MODEL, D_MODEL)
        w_in, w_out = gathers[group]
        if group == "b":
            return w_in, w_out.reshape(D_MODEL, D_MODEL)
        return w_in.reshape(2, D_FF, D_MODEL), w_out.reshape(D_FF, D_MODEL)

    exchanges, pending = {}, []
    exchange_ids = dict(b=3, f0=4, a=5)
    side = lax.axis_index("c").astype(jnp.int32).reshape(1)

    def emit(group, parts, received, carry):
        sums = _pair_add(f"pair_add_{group}", parts, received, side)
        carry, *sums = lax.optimization_barrier((carry, *sums))
        pending.extend(sums)
        if group != "f1":
            after = list(exchanges.values())[-1][:1] if exchanges else ()
            exchanges[group] = _sequencer_exchange(f"grads_{group}", exchange_ids[group], list(pending), after)
            pending.clear()
        return carry

    dx, g = _local_step(x[0], loss_target[0], weights, fetch, emit)

    (up1, down1, d_b_in, d_b_out), (up0, down0), (d_a_in, d_a_out) = (exchanges[group] for group in ("b", "f0", "a"))
    back = lambda results: tuple(transposed(r) for r in results)
    big = dict(
        b_w_in=_adamw_sum("adam_b_w_in", [d_b_in], b_w_in, m_b_w_in, v_b_w_in),
        b_w_out=_adamw_sum("adam_b_w_out", [d_b_out], b_w_out, m_b_w_out, v_b_w_out),
        f_w_up=back(_adamw_sum("adam_f_w_up", [up0, up1], f_w_up_t, transposed(m_f_w_up), transposed(v_f_w_up))),
        f_w_down=_adamw_sum("adam_f_w_down", [down0, down1], f_w_down, m_f_w_down, v_f_w_down))
    small_packed, *updated = lax.optimization_barrier((_pack_small_grads(g), *big["f_w_down"]))
    big["f_w_down"] = tuple(updated)
    small_landed = _all_gather("small_grad_gather", [small_packed])[0]
    big.update(
        a_w_in=tuple(r.reshape(A_SHARD, D_MODEL).T.reshape(1, D_MODEL, A_SHARD) for r in _adamw_sum(
            "adam_a_w_in", [d_a_in], a_w_in_t, a_transposed(m_a_w_in), a_transposed(v_a_w_in))),
        a_w_out=_adamw_sum("adam_a_w_out", [d_a_out], a_w_out, m_a_w_out, v_a_w_out))
    small_g = _unpack_small_grads(_sum_small(small_landed))
    loss = small_g["loss"][0, 0]
    small_g["a_w_gate_up"] = lax.dynamic_slice_in_dim(small_g["a_w_gate_up"], my_slot * GATE_SHARD, GATE_SHARD, axis=1)
    small_g["b_norm"] = lax.dynamic_slice_in_dim(small_g["b_norm"], my_slot * NORM_SHARD, NORM_SHARD, axis=1)
    small_g["b_conv"] = lax.dynamic_slice_in_dim(small_g["b_conv"], my_slot * NORM_SHARD, NORM_SHARD, axis=1)
    small_g["f_conv"] = lax.dynamic_slice_in_dim(small_g["f_conv"], my_slot * F_CONV_SHARD, F_CONV_SHARD, axis=2)
    small_w = dict(
        a_norm=(a_norm, m_a_norm, v_a_norm), a_w_gate_up=(a_w_gate_up, m_a_w_gate_up, v_a_w_gate_up),
        a_b_gate=(a_b_gate, m_a_b_gate, v_a_b_gate), a_gn=(a_gn, m_a_gn, v_a_gn), b_norm=(b_norm, m_b_norm, v_b_norm),
        b_conv=(b_conv, m_b_conv, v_b_conv), f_norm=(f_norm, m_f_norm, v_f_norm), f_conv=(f_conv, m_f_conv, v_f_conv),
        final_norm=(final_norm, m_final_norm, v_final_norm))
    two_d = lambda a: a.reshape(-1, a.shape[-1])
    updates = _adamw_small([tuple(two_d(a.reshape(w.shape)) for a in (small_g[name], w, m, v)) for name, (w, m, v) in small_w.items()])
    small = {}
    for (name, (w, _, _)), update in zip(small_w.items(), updates):
        small[name] = (small_g[name].reshape(w.shape),) + tuple(u.reshape(w.shape) for u in update)

    order = ["a_norm", "a_w_in", "a_w_gate_up", "a_b_gate", "a_gn", "a_w_out", "b_norm", "b_w_in", "b_conv", "b_w_out",
             "f_norm", "f_w_up", "f_conv", "f_w_down", "final_norm"]
    results = {**big, **small}
    outputs = [loss, dx.reshape(1, SEQ, D_MODEL)]
    for kind in range(4):
        outputs += [results[name][kind] for name in order]
    return tuple(outputs)
```

```python
import jax
import jax.numpy as jnp
from jax import lax
from jax.experimental import pallas as pl
from jax.experimental.pallas import tpu as pltpu
from jax.experimental.pallas import tpu_sc as plsc

F32 = jnp.float32
BF16 = jnp.bfloat16

N_DEV = 8
SEQ = 2048
D_MODEL = 1024
CHUNK = 64
N_CHUNKS = SEQ // CHUNK
RMS_EPS = 1e-6
GLA_HEADS = 4
KEY_DIM = 512
VALUE_DIM = 1024
HEAD_K = KEY_DIM // GLA_HEADS
HEAD_V = VALUE_DIM // GLA_HEADS
GATE_RANK = 16
GATE_PAD = 128
GATE_NORMALIZER = 16.0
PROJ_A = 2 * KEY_DIM + 2 * VALUE_DIM + GATE_RANK
PROJ_A_PAD = 2 * KEY_DIM + 2 * VALUE_DIM + GATE_PAD
A_SHARD = PROJ_A // N_DEV
B_SHARD = 3 * D_MODEL // N_DEV
D_FF = 2816
ADAM_LR = 0.001
ADAM_B1 = 0.9
ADAM_B2 = 0.999
ADAM_EPS = 1e-08
ADAM_WD = 0.01
ADAM_STEP = 10
MESH_AXES = ("x", "y", "c")

VMEM_LIMIT = 56 * 1024 * 1024
ROW_CHUNK = 256
HALO = 16


def _params(sem=None, vmem=VMEM_LIMIT):
    return pltpu.CompilerParams(dimension_semantics=sem, vmem_limit_bytes=vmem)


NORM_PARTS = 2
NN = ((1,), (0,))
NT = ((1,), (1,))
TN = ((0,), (0,))


def _matmul(name, a, a_spec, b, b_spec, dims, grid, out_shape, out_spec, k_blocks=None, a_block_cols=None, res=None,
            res_spec=None, transpose_out=False, norm=None, swap=()):
    has_res = res is not None
    n_swap = len(swap)

    def body(*refs):
        a_ref, b_ref = refs[0], refs[1]
        r_ref = refs[2] if has_res else None

        def product(lhs, rhs):
            return lax.dot_general(lhs.astype(BF16), rhs, (dims, ((), ())), preferred_element_type=F32)

        def tile(rows=slice(None)):
            if k_blocks is None:
                return product(a_ref[rows, :] if norm is not None else a_ref[...], b_ref[...])
            v = None
            for k in range(k_blocks):
                lhs = a_ref[k, rows, :] if a_block_cols is None else a_ref[rows, k * a_block_cols:(k + 1) * a_block_cols]
                p = product(lhs, b_ref[k])
                v = p if v is None else v + p
            return v

        if norm is None:
            v = tile()
            if transpose_out:
                v = v.T
            if has_res:
                v = v + r_ref[...]
            o_ref = refs[2 + has_res]
            o_ref[...] = v.astype(o_ref.dtype)
            return
        n_in = 5 + has_res
        x_ref, g_ref, dxi_ref = refs[2 + has_res:n_in]
        dx_ref, dx16_ref, dg_ref = refs[n_in + n_swap:n_in + n_swap + 3]
        if n_swap:
            copies = _pair_copies(refs[n_in:n_in + n_swap], refs[n_in + n_swap + 3:n_in + 2 * n_swap + 3], *refs[-2:])

            @pl.when(pl.program_id(0) == 0)
            def _():
                for send, _ in copies:
                    send.start()

            @pl.when(pl.program_id(0) == grid[0] - 1)
            def _():
                for send, arrival in copies:
                    arrival.wait_recv()
                    send.wait_send()

        dg = None
        part = dx_ref.shape[0] // NORM_PARTS
        for rows in (slice(i * part, (i + 1) * part) for i in range(NORM_PARTS)):
            dx, dg_rows = _norm_bwd_rows(x_ref[rows, :], g_ref[...], tile(rows))
            dx = dxi_ref[rows, :] + dx
            dx_ref[rows, :] = dx
            dx16_ref[rows, :] = dx.astype(BF16)
            dg = dg_rows if dg is None else dg + dg_rows

        @pl.when(pl.program_id(0) == 0)
        def _():
            dg_ref[...] = dg

        @pl.when(pl.program_id(0) > 0)
        def _():
            dg_ref[...] += dg

    operands = [a, b] + ([res] if has_res else [])
    in_specs = [a_spec, b_spec] + ([res_spec] if has_res else [])
    semantics = ("parallel",) * len(grid)
    scratch = []
    if norm is not None:
        vec = _spec((1, D_MODEL), lambda i: (0, 0))
        any_space = pl.BlockSpec(memory_space=pl.ANY)
        operands += list(norm) + list(swap)
        in_specs += [out_spec, vec, out_spec] + [any_space] * n_swap
        out_shape = [_act(dtype=F32), _act(), jax.ShapeDtypeStruct((1, D_MODEL), F32)]
        out_shape += [jax.ShapeDtypeStruct((N_DEV // 2,) + p.shape[1:], p.dtype) for p in swap]
        out_spec = [out_spec, out_spec, vec] + [any_space] * n_swap
        semantics = ("arbitrary",)
        if n_swap:
            scratch = [pltpu.SemaphoreType.DMA((n_swap, N_DEV // 2))] * 2
    return pl.pallas_call(
        body, name=name, grid=grid, in_specs=in_specs, out_specs=out_spec, out_shape=out_shape, scratch_shapes=scratch,
        compiler_params=_params(semantics),
    )(*operands)


def _resident(shape):
    return pl.BlockSpec(shape, lambda *_: (0,) * len(shape), pipeline_mode=pl.Buffered(1))


TM = 512
N_TM = SEQ // TM
PA_TILE = 640
N_PA = PROJ_A_PAD // PA_TILE
OUT_TILE = 256


def _spec(shape, fn):
    return pl.BlockSpec(shape, fn)


def _act(shape=(SEQ, D_MODEL), dtype=BF16):
    return jax.ShapeDtypeStruct(shape, dtype)


def _norm_proj(name, x, gamma, w):
    blocks = w.ndim == 3
    n_out = w.shape[0] * w.shape[2] if blocks else w.shape[0]

    def body(x_ref, g_ref, w_ref, h_ref, o_ref):
        part = TM // NORM_PARTS
        for rows in (slice(i * part, (i + 1) * part) for i in range(NORM_PARTS)):
            x = x_ref[rows, :]
            h = (x * _rstd(x) * g_ref[...]).astype(BF16)
            h_ref[rows, :] = h
            if blocks:
                n = w.shape[2]
                for j in range(w.shape[0]):
                    o_ref[rows, j * n:(j + 1) * n] = jnp.dot(h, w_ref[j], preferred_element_type=F32).astype(BF16)
            else:
                o_ref[rows, :] = lax.dot_general(h, w_ref[...], (NT, ((), ())), preferred_element_type=F32).astype(BF16)

    row = _spec((TM, D_MODEL), lambda i: (i, 0))
    return pl.pallas_call(
        body, name=name, grid=(N_TM,), in_specs=[row, _resident((1, D_MODEL)), _resident(w.shape)],
        out_specs=[row, _spec((TM, n_out), lambda i: (i, 0))], out_shape=[_act(), _act((SEQ, n_out))],
        compiler_params=_params(("parallel",)),
    )(x, gamma, w)


def _rows_matmul(name, a, w, dims, x=None):
    k = a.shape[1]
    n = w.shape[1] if dims == NN else w.shape[0]
    row = _spec((TM, n), lambda i: (i, 0))
    return _matmul(name, a, _spec((TM, k), lambda i: (i, 0)), w, _resident(w.shape), dims, (N_TM,),
                   _act((SEQ, n), F32 if x is not None else BF16), row, res=x, res_spec=row if x is not None else None)


def _sum_blocks_nn(name, a_blocks, w_blocks, x=None, norm=None, swap=()):
    nb, _, n = a_blocks.shape
    row = _spec((TM, D_MODEL), lambda i: (i, 0))
    return _matmul(name, a_blocks, _spec((nb, TM, n), lambda i: (0, i, 0)), w_blocks, _resident((nb, n, D_MODEL)),
                   NN, (N_TM,), _act(dtype=F32), row, k_blocks=nb, res=x, res_spec=row if x is not None else None, norm=norm, swap=swap)


def _sum_cols_nt(name, d, w_blocks, norm=None, swap=()):
    nb, _, n = w_blocks.shape
    return _matmul(name, d, _spec((TM, nb * n), lambda i: (i, 0)), w_blocks, _resident((nb, D_MODEL, n)), NT,
                   (N_TM,), _act(dtype=F32), _spec((TM, D_MODEL), lambda i: (i, 0)), k_blocks=nb, a_block_cols=n, norm=norm, swap=swap)


def _wide_nn(name, d, wt, x=None, norm=None, swap=()):
    n = wt.shape[0]
    row = _spec((TM, D_MODEL), lambda i: (i, 0))
    return _matmul(name, d, _spec((TM, n), lambda i: (i, 0)), wt, _resident((n, D_MODEL)), NN, (N_TM,),
                   _act(dtype=F32), row, res=x, res_spec=row if x is not None else None, norm=norm, swap=swap)


def _wgrad_halves_tn(name, d, n_tile, h):
    _, _, n = d.shape
    return _matmul(name, d, _spec((None, SEQ, n_tile), lambda p, j: (p, 0, j)), h, _resident((SEQ, D_MODEL)), TN,
                   (2, n // n_tile), _act((2, n, D_MODEL)), _spec((None, n_tile, D_MODEL), lambda p, j: (p, j, 0)))


def _wgrad_cols_tn(name, d, n_tile, h):
    n = d.shape[1]
    return _matmul(name, d, _spec((SEQ, n_tile), lambda j: (0, j)), h, _resident((SEQ, D_MODEL)), TN,
                   (n // n_tile,), _act((n, D_MODEL)), _spec((n_tile, D_MODEL), lambda j: (j, 0)))


def _wgrad_cols_transposed_tn(name, h, d, n_tile):
    nb = d.shape[1] // n_tile
    return _matmul(name, d, _spec((SEQ, n_tile), lambda j: (0, j)), h, _resident((SEQ, D_MODEL)), TN, (nb,),
                   _act((nb, D_MODEL, n_tile)), _spec((None, D_MODEL, n_tile), lambda j: (j, 0, 0)), transpose_out=True)


NORM_ROWS = 512


def _rstd(x):
    return lax.rsqrt(jnp.mean(x * x, axis=-1, keepdims=True) + RMS_EPS)


def _norm_bwd_rows(x, gamma, dh):
    r = _rstd(x)
    xh = x * r
    dxh = dh * gamma
    dx = r * (dxh - xh * jnp.mean(dxh * xh, axis=-1, keepdims=True))
    return dx, jnp.sum(dh * xh, axis=0, keepdims=True)


def _down_loss_head(a, w_down, x_in, gamma, target):
    def body(a_ref, w_ref, x_ref, g_ref, t_ref, loss_ref, dx_ref, dx16_ref, dg_ref):
        gamma = g_ref[...]
        dg, part = 0.0, 0.0
        rows_per_part = TM // NORM_PARTS
        for rows in (slice(i * rows_per_part, (i + 1) * rows_per_part) for i in range(NORM_PARTS)):
            x = x_ref[rows, :] + jnp.dot(a_ref[rows, :], w_ref[...], preferred_element_type=F32)
            err = x * _rstd(x) * gamma - t_ref[rows, :]
            dy = err * (1.0 / D_MODEL)
            dx, dg_rows = _norm_bwd_rows(x, gamma, dy)
            dx_ref[rows, :] = dx
            dx16_ref[rows, :] = dx.astype(BF16)
            dg = dg + dg_rows
            part = part + 0.5 * jnp.sum(jnp.sum(err * err, axis=-1, keepdims=True) * (1.0 / D_MODEL), axis=0, keepdims=True)
        part = jnp.broadcast_to(part, loss_ref.shape)

        @pl.when(pl.program_id(0) == 0)
        def _():
            dg_ref[...] = dg
            loss_ref[...] = part

        @pl.when(pl.program_id(0) > 0)
        def _():
            dg_ref[...] += dg
            loss_ref[...] += part

    row = _spec((TM, D_MODEL), lambda i: (i, 0))
    vec = _spec((1, D_MODEL), lambda i: (0, 0))
    return pl.pallas_call(
        body, name="ffn1_down_loss_head", grid=(N_TM,),
        in_specs=[_spec((TM, D_FF), lambda i: (i, 0)), _resident((D_FF, D_MODEL)), row, vec, row],
        out_specs=[_spec((1, 128), lambda i: (0, 0)), row, row, vec],
        out_shape=[jax.ShapeDtypeStruct((1, 128), F32), _act(dtype=F32), _act(), jax.ShapeDtypeStruct((1, D_MODEL), F32)],
        compiler_params=_params(("arbitrary",)),
    )(a, w_down, x_in, gamma, target)


def _sigmoid(x):
    return 1.0 / (1.0 + jnp.exp(-x))


def _rows(ref, c):
    return ref[pl.ds(pl.multiple_of(c * ROW_CHUNK, ROW_CHUNK), ROW_CHUNK), :].astype(F32)


def _rows_before(ref, c):
    start = pl.multiple_of(jnp.maximum(c * ROW_CHUNK - HALO, 0), HALO)
    rows = ref[pl.ds(start, HALO), :].astype(F32)
    return jnp.where(c > 0, rows, 0.0)


def _rows_after(ref, c, n_chunks):
    start = pl.multiple_of(jnp.minimum((c + 1) * ROW_CHUNK, SEQ - HALO), HALO)
    rows = ref[pl.ds(start, HALO), :].astype(F32)
    return jnp.where(c < n_chunks - 1, rows, 0.0)


def _shift_down(z, before, n):
    return pltpu.roll(jnp.concatenate([before, z], axis=0), n, 0)[before.shape[0]:]


def _shift_up(z, after, n):
    rows = z.shape[0]
    return pltpu.roll(jnp.concatenate([z, after], axis=0), rows + HALO - n, 0)[:rows]


def _conv_rows(z, before, w):
    z1 = _shift_down(z, before, 1)
    z2 = _shift_down(z, before, 2)
    return w[2:3, :] * z + w[1:2, :] * z1 + w[0:1, :] * z2, z1, z2


def _conv_t_rows(dy, after, w):
    return w[2:3, :] * dy + w[1:2, :] * _shift_up(dy, after, 1) + w[0:1, :] * _shift_up(dy, after, 2)


N_ROW_CHUNKS = SEQ // ROW_CHUNK


FF_COLS = 256
N_FF_COLS = D_FF // FF_COLS


def _ffn_mid_bwd(name, gu, conv_w, da):
    def body(gu_ref, w_ref, da_ref, dgu_ref, dw_ref, dgc_ref):
        w = w_ref[...]

        def first(c, acc):
            g = _rows(gu_ref.at[0], c)
            u = _rows(gu_ref.at[1], c)
            d = _rows(da_ref, c)
            gc, g1, g2 = _conv_rows(g, _rows_before(gu_ref.at[0], c), w)
            sg = _sigmoid(gc)
            rows = pl.ds(pl.multiple_of(c * ROW_CHUNK, ROW_CHUNK), ROW_CHUNK)
            silu = gc * sg
            dgu_ref[1, rows, :] = (d * silu).astype(BF16)
            dgc = d * u * (sg + silu * (1.0 - sg))
            dgc_ref[rows, :] = dgc
            return (acc[0] + jnp.sum(dgc * g2, axis=0, keepdims=True), acc[1] + jnp.sum(dgc * g1, axis=0, keepdims=True),
                    acc[2] + jnp.sum(dgc * g, axis=0, keepdims=True))

        zero = jnp.zeros((1, FF_COLS), F32)
        acc = lax.fori_loop(0, N_ROW_CHUNKS, first, (zero, zero, zero))
        for r in range(3):
            dw_ref[r:r + 1, :] = acc[r]

        def second(c, carry):
            dgc = _rows(dgc_ref, c)
            dg = _conv_t_rows(dgc, _rows_after(dgc_ref, c, N_ROW_CHUNKS), w)
            dgu_ref[0, pl.ds(pl.multiple_of(c * ROW_CHUNK, ROW_CHUNK), ROW_CHUNK), :] = dg.astype(BF16)
            return carry

        lax.fori_loop(0, N_ROW_CHUNKS, second, 0)

    pair = _spec((2, SEQ, FF_COLS), lambda j: (0, 0, j))
    wspec = _spec((3, FF_COLS), lambda j: (0, j))
    return pl.pallas_call(
        body, name=name, grid=(N_FF_COLS,), in_specs=[pair, wspec, _spec((SEQ, FF_COLS), lambda j: (0, j))],
        out_specs=[pair, wspec], out_shape=[_act((2, SEQ, D_FF)), jax.ShapeDtypeStruct((3, D_FF), F32)],
        scratch_shapes=[pltpu.VMEM((SEQ, FF_COLS), F32)],
        compiler_params=_params(("parallel",)),
    )(gu, conv_w, da)


SC_COLS = 256
N_SC = D_MODEL // SC_COLS


def _sc_specs():
    return [_spec((SEQ, SC_COLS), lambda j, part=part: (0, part * N_SC + j)) for part in range(3)]


def _sc_mid_fwd(p, conv_w):
    def body(b_ref, c_ref, h_ref, w_ref, y_ref):
        w = w_ref[...]

        def chunk(c, carry):
            z = _rows(c_ref, c) * _rows(h_ref, c)
            before = _rows_before(c_ref, c) * _rows_before(h_ref, c)
            zc, _, _ = _conv_rows(z, before, w)
            y_ref[pl.ds(pl.multiple_of(c * ROW_CHUNK, ROW_CHUNK), ROW_CHUNK), :] = (_rows(b_ref, c) * zc).astype(BF16)
            return carry

        lax.fori_loop(0, N_ROW_CHUNKS, chunk, 0)

    col = _spec((SEQ, SC_COLS), lambda j: (0, j))
    return pl.pallas_call(
        body, name="sc_mid_fwd", grid=(N_SC,), in_specs=_sc_specs() + [_spec((3, SC_COLS), lambda j: (0, j))], out_specs=col,
        out_shape=jax.ShapeDtypeStruct((SEQ, D_MODEL), BF16), compiler_params=_params(("parallel",)),
    )(p, p, p, conv_w)


def _sc_mid_bwd(p, conv_w, dy):
    def body(b_ref, c_ref, h_ref, w_ref, dy_ref, db_ref, dc_ref, dh_ref, dw_ref, dzc_ref):
        w = w_ref[...]

        def first(c, acc):
            z = _rows(c_ref, c) * _rows(h_ref, c)
            before = _rows_before(c_ref, c) * _rows_before(h_ref, c)
            zc, z1, z2 = _conv_rows(z, before, w)
            d = _rows(dy_ref, c)
            rows = pl.ds(pl.multiple_of(c * ROW_CHUNK, ROW_CHUNK), ROW_CHUNK)
            db_ref[rows, :] = (d * zc).astype(BF16)
            dzc = d * _rows(b_ref, c)
            dzc_ref[rows, :] = dzc
            return (acc[0] + jnp.sum(dzc * z2, axis=0, keepdims=True), acc[1] + jnp.sum(dzc * z1, axis=0, keepdims=True),
                    acc[2] + jnp.sum(dzc * z, axis=0, keepdims=True))

        zero = jnp.zeros((1, SC_COLS), F32)
        acc = lax.fori_loop(0, N_ROW_CHUNKS, first, (zero, zero, zero))
        for r in range(3):
            dw_ref[r:r + 1, :] = acc[r]

        def second(c, carry):
            dz = _conv_t_rows(_rows(dzc_ref, c), _rows_after(dzc_ref, c, N_ROW_CHUNKS), w)
            rows = pl.ds(pl.multiple_of(c * ROW_CHUNK, ROW_CHUNK), ROW_CHUNK)
            dc_ref[rows, :] = (dz * _rows(h_ref, c)).astype(BF16)
            dh_ref[rows, :] = (dz * _rows(c_ref, c)).astype(BF16)
            return carry

        lax.fori_loop(0, N_ROW_CHUNKS, second, 0)

    col = _spec((SEQ, SC_COLS), lambda j: (0, j))
    wspec = _spec((3, SC_COLS), lambda j: (0, j))
    act = jax.ShapeDtypeStruct((SEQ, D_MODEL), BF16)
    return pl.pallas_call(
        body, name="sc_mid_bwd", grid=(N_SC,), in_specs=_sc_specs() + [wspec, col], out_specs=[col, col, col, wspec],
        out_shape=[act, act, act, jax.ShapeDtypeStruct((3, D_MODEL), F32)],
        scratch_shapes=[pltpu.VMEM((SEQ, SC_COLS), F32)], compiler_params=_params(("parallel",)),
    )(p, p, p, conv_w, dy)


GLA_GROUP = 4
GLA_ROWS = GLA_GROUP * CHUNK
N_GROUPS = N_CHUNKS // GLA_GROUP
Q0, K0, V0, R0, G0 = 0, KEY_DIM, 2 * KEY_DIM, 2 * KEY_DIM + VALUE_DIM, 2 * KEY_DIM + 2 * VALUE_DIM


def _tri(strict):
    r = lax.broadcasted_iota(jnp.int32, (CHUNK, CHUNK), 0)
    c = lax.broadcasted_iota(jnp.int32, (CHUNK, CHUNK), 1)
    return jnp.where(c < r if strict else c <= r, 1.0, 0.0).astype(F32)


def _cumsum_rows(tri, x):
    tri = tri.astype(BF16)
    total = None
    for _ in range(3):
        term = x.astype(BF16)
        x = x - term.astype(F32)
        product = jnp.dot(tri, term, preferred_element_type=F32)
        total = product if total is None else total + product
    return total


def _gate_logits(gl, wgu, b_gate):
    return jnp.dot(gl, wgu, preferred_element_type=F32) + b_gate


def _log_decay(logits):
    return (jnp.minimum(logits, 0.0) - jnp.log(1.0 + jnp.exp(-jnp.abs(logits)))) * (1.0 / GATE_NORMALIZER)


def _head(x, h, width):
    return x[:, h * width:(h + 1) * width]


def _gla_fwd(proj, wgu, b_gate, gn):
    def body(p_ref, wgu_ref, b_ref, gn_ref, o_ref, og_ref, st_ref, state):
        @pl.when(pl.program_id(0) == 0)
        def _():
            state[...] = jnp.zeros_like(state)

        tri = _tri(False)
        la = _log_decay(_gate_logits(p_ref[:, G0:G0 + GATE_PAD], wgu_ref[...], b_ref[...]))
        decays = []
        for c in range(GLA_GROUP):
            rows = slice(c * CHUNK, (c + 1) * CHUNK)
            cum = _cumsum_rows(tri, la[rows])
            tot = cum[CHUNK - 1:CHUNK, :]
            kd = (p_ref[rows, K0:K0 + KEY_DIM].astype(F32) * jnp.exp(tot - cum)).astype(BF16)
            decays.append(jnp.exp(tot))
            v = p_ref[rows, V0:V0 + VALUE_DIM]
            for h in range(GLA_HEADS):
                st_ref[c, h] = lax.dot_general(
                    _head(v, h, HEAD_V), _head(kd, h, HEAD_K), (TN, ((), ())), preferred_element_type=F32)
        for c in range(GLA_GROUP):
            for h in range(GLA_HEADS):
                s = state[h] * _head(decays[c], h, HEAD_K) + st_ref[c, h]
                state[h] = s
                st_ref[c, h] = s
        for c in range(GLA_GROUP):
            rows = slice(c * CHUNK, (c + 1) * CHUNK)
            q = (p_ref[rows, Q0:Q0 + KEY_DIM].astype(F32) * (HEAD_K ** -0.5)).astype(BF16)
            for h in range(GLA_HEADS):
                o_ref[rows, h * HEAD_V:(h + 1) * HEAD_V] = lax.dot_general(
                    _head(q, h, HEAD_K), st_ref[c, h].astype(BF16), (NT, ((), ())), preferred_element_type=F32)
        r = p_ref[:, R0:R0 + VALUE_DIM].astype(F32)
        gate = r * _sigmoid(r) * gn_ref[...]
        for h in range(GLA_HEADS):
            cols = slice(h * HEAD_V, (h + 1) * HEAD_V)
            o = o_ref[:, cols]
            og_ref[:, cols] = (o * _rstd(o) * gate[:, cols]).astype(BF16)

    rows = _spec((GLA_ROWS, VALUE_DIM), lambda i: (i, 0))
    const = lambda shape: _spec(shape, lambda i: (0,) * len(shape))
    return pl.pallas_call(
        body, name="gla_fwd", grid=(N_GROUPS,),
        in_specs=[_spec((GLA_ROWS, PROJ_A_PAD), lambda i: (i, 0)), const((GATE_PAD, KEY_DIM)), const((1, KEY_DIM)),
                  const((1, VALUE_DIM))],
        out_specs=[rows, rows, _spec((GLA_GROUP, GLA_HEADS, HEAD_V, HEAD_K), lambda i: (i, 0, 0, 0))],
        out_shape=[jax.ShapeDtypeStruct((SEQ, VALUE_DIM), F32), jax.ShapeDtypeStruct((SEQ, VALUE_DIM), BF16),
                   jax.ShapeDtypeStruct((N_CHUNKS, GLA_HEADS, HEAD_V, HEAD_K), F32)],
        scratch_shapes=[pltpu.VMEM((GLA_HEADS, HEAD_V, HEAD_K), F32)], compiler_params=_params(("arbitrary",)),
    )(proj, wgu, b_gate, gn)


def _gla_bwd(proj, wgu, b_gate, gn, o, states, dog):
    last = N_GROUPS - 1

    def body(p_ref, wgu_ref, b_ref, gn_ref, o_ref, st_ref, stp_ref, dog_ref, dp_ref, dwgu_ref, db_ref, dgn_ref, carry, do_buf,
             g_buf):
        step = pl.program_id(0)

        @pl.when(step == 0)
        def _():
            carry[...] = jnp.zeros_like(carry)

        r = p_ref[:, R0:R0 + VALUE_DIM].astype(F32)
        sr = _sigmoid(r)
        silu = r * sr
        gn_row = gn_ref[...]
        dog_rows = dog_ref[...].astype(F32)
        dn = dog_rows * silu
        dgn_cols = []
        for h in range(GLA_HEADS):
            cols = slice(h * HEAD_V, (h + 1) * HEAD_V)
            oh = o_ref[:, cols]
            rs = _rstd(oh)
            ohat = oh * rs
            dn_h = dn[:, cols]
            dgn_cols.append(jnp.sum(dn_h * ohat, axis=0, keepdims=True))
            dohat = dn_h * gn_row[:, cols]
            do_buf[:, cols] = rs * (dohat - ohat * jnp.mean(dohat * ohat, axis=-1, keepdims=True))
            n_h = ohat * gn_row[:, cols]
            dp_ref[:, R0 + h * HEAD_V:R0 + (h + 1) * HEAD_V] = (
                dog_rows[:, cols] * n_h * (sr[:, cols] * (1.0 + r[:, cols] * (1.0 - sr[:, cols])))).astype(BF16)
        dgn = jnp.concatenate(dgn_cols, axis=1)

        tri = _tri(False)
        tri_strict = _tri(True)
        gl = p_ref[:, G0:G0 + GATE_PAD]
        logits = _gate_logits(gl, wgu_ref[...], b_ref[...])
        la = _log_decay(logits)
        fades, kds, decays = [], [], []
        for c in range(GLA_GROUP):
            rows = slice(c * CHUNK, (c + 1) * CHUNK)
            cum = _cumsum_rows(tri, la[rows])
            tot = cum[CHUNK - 1:CHUNK, :]
            fades.append(jnp.exp(tot - cum))
            kds.append(p_ref[rows, K0:K0 + KEY_DIM].astype(F32) * fades[c])
            decays.append(jnp.exp(tot))
            q = (p_ref[rows, Q0:Q0 + KEY_DIM].astype(F32) * (HEAD_K ** -0.5)).astype(BF16)
            do = do_buf[rows, :].astype(BF16)
            for h in range(GLA_HEADS):
                do_h = _head(do, h, HEAD_V)
                dq = jnp.dot(do_h, st_ref[c, h].astype(BF16), preferred_element_type=F32) * (HEAD_K ** -0.5)
                dp_ref[rows, Q0 + h * HEAD_K:Q0 + (h + 1) * HEAD_K] = dq.astype(BF16)
                g_buf[c, h] = lax.dot_general(do_h, _head(q, h, HEAD_K), (TN, ((), ())), preferred_element_type=F32)
        for c in reversed(range(GLA_GROUP)):
            for h in range(GLA_HEADS):
                g = carry[h] + g_buf[c, h]
                g_buf[c, h] = g
                carry[h] = g * _head(decays[c], h, HEAD_K)
        dlogit_rows = []
        for c in range(GLA_GROUP):
            rows = slice(c * CHUNK, (c + 1) * CHUNK)
            v = p_ref[rows, V0:V0 + VALUE_DIM]
            kd = kds[c].astype(BF16)
            dkd_cols, ddecay_cols = [], []
            for h in range(GLA_HEADS):
                g = g_buf[c, h]
                g16 = g.astype(BF16)
                dkd_cols.append(jnp.dot(_head(v, h, HEAD_V), g16, preferred_element_type=F32))
                dv = lax.dot_general(_head(kd, h, HEAD_K), g16, (NT, ((), ())), preferred_element_type=F32)
                dp_ref[rows, V0 + h * HEAD_V:V0 + (h + 1) * HEAD_V] = dv.astype(BF16)
                if c > 0:
                    s_prev = st_ref[c - 1, h]
                else:
                    s_prev = jnp.where(step < last, stp_ref[0, h], 0.0)
                ddecay_cols.append(jnp.sum(g * s_prev, axis=0, keepdims=True))
            dkd = jnp.concatenate(dkd_cols, axis=1)
            ddecay = jnp.concatenate(ddecay_cols, axis=1)
            dp_ref[rows, K0:K0 + KEY_DIM] = (dkd * fades[c]).astype(BF16)
            e = dkd * kds[c]
            dla = ddecay * decays[c] + _cumsum_rows(tri_strict, e)
            dlogit_rows.append(dla * (1.0 / GATE_NORMALIZER) * (1.0 - _sigmoid(logits[rows])))
        dlogit = jnp.concatenate(dlogit_rows, axis=0)
        dlogit16 = dlogit.astype(BF16)
        dp_ref[:, G0:G0 + GATE_PAD] = lax.dot_general(
            dlogit16, wgu_ref[...], (NT, ((), ())), preferred_element_type=F32).astype(BF16)
        dwgu = lax.dot_general(gl, dlogit16, (TN, ((), ())), preferred_element_type=F32)
        db = jnp.sum(dlogit, axis=0, keepdims=True)

        @pl.when(step == 0)
        def _():
            dwgu_ref[...] = dwgu
            db_ref[...] = db
            dgn_ref[...] = dgn

        @pl.when(step > 0)
        def _():
            dwgu_ref[...] += dwgu
            db_ref[...] += db
            dgn_ref[...] += dgn

    rev = lambda i: (last - i, 0)
    rows = _spec((GLA_ROWS, VALUE_DIM), rev)
    const = lambda shape: _spec(shape, lambda i: (0,) * len(shape))
    st_shape = (GLA_HEADS, HEAD_V, HEAD_K)
    return pl.pallas_call(
        body, name="gla_bwd", grid=(N_GROUPS,),
        in_specs=[_spec((GLA_ROWS, PROJ_A_PAD), rev), const((GATE_PAD, KEY_DIM)), const((1, KEY_DIM)), const((1, VALUE_DIM)),
                  rows, _spec((GLA_GROUP,) + st_shape, lambda i: (last - i, 0, 0, 0)),
                  _spec((1,) + st_shape, lambda i: (jnp.maximum((last - i) * GLA_GROUP - 1, 0), 0, 0, 0)), rows],
        out_specs=[_spec((GLA_ROWS, PROJ_A_PAD), rev), const((GATE_PAD, KEY_DIM)), const((1, KEY_DIM)), const((1, VALUE_DIM))],
        out_shape=[jax.ShapeDtypeStruct((SEQ, PROJ_A_PAD), BF16), jax.ShapeDtypeStruct((GATE_PAD, KEY_DIM), F32),
                   jax.ShapeDtypeStruct((1, KEY_DIM), F32), jax.ShapeDtypeStruct((1, VALUE_DIM), F32)],
        scratch_shapes=[pltpu.VMEM(st_shape, F32), pltpu.VMEM((GLA_ROWS, VALUE_DIM), F32), pltpu.VMEM((GLA_GROUP,) + st_shape, F32)],
        compiler_params=_params(("arbitrary",)),
    )(proj, wgu, b_gate, gn, o, states, states, dog)


WGRAD_FF_TILE = D_FF // 2


CARRY_ROWS = 8
UP_ROWS = 512


def _ffn_up_mid(name, x, gamma, w_up_t, conv_w):
    def body(x_ref, g_ref, w_ref, c_ref, h_ref, gu_ref, a_ref, carry):
        @pl.when(pl.program_id(0) == 0)
        def _():
            carry[...] = jnp.zeros_like(carry)

        x_tile = x_ref[...]
        h_tile = (x_tile * _rstd(x_tile) * g_ref[...]).astype(BF16)
        h_ref[...] = h_tile
        for k in range(N_FF_COLS):
            cols = slice(k * FF_COLS, (k + 1) * FF_COLS)
            g, u = (lax.dot_general(h_tile, w_ref[p, cols, :], (NT, ((), ())), preferred_element_type=F32).astype(BF16)
                    for p in range(2))
            gu_ref[0, :, cols] = g
            gu_ref[1, :, cols] = u
            g = g.astype(F32)
            w = c_ref[:, cols]
            before = carry[:, cols]
            gc = w[2:3, :] * g + w[1:2, :] * _shift_down(g, before, 1) + w[0:1, :] * _shift_down(g, before, 2)
            a_ref[:, cols] = (gc * _sigmoid(gc) * u.astype(F32)).astype(BF16)
            carry[:, cols] = g[UP_ROWS - CARRY_ROWS:, :]

    row = _spec((UP_ROWS, D_MODEL), lambda i: (i, 0))
    return pl.pallas_call(
        body, name=name, grid=(SEQ // UP_ROWS,),
        in_specs=[row, _resident((1, D_MODEL)), _resident((2, D_FF, D_MODEL)), _resident((3, D_FF))],
        out_specs=[row, _spec((2, UP_ROWS, D_FF), lambda i: (0, i, 0)), _spec((UP_ROWS, D_FF), lambda i: (i, 0))],
        out_shape=[_act(), _act((2, SEQ, D_FF)), _act((SEQ, D_FF))], scratch_shapes=[pltpu.VMEM((CARRY_ROWS, D_FF), F32)],
        compiler_params=_params(("arbitrary",)),
    )(x, gamma, w_up_t, conv_w)


def _ffn_fwd(tag, x, gamma, w_up_t, conv_w, w_down):
    h, gu, a = _ffn_up_mid(f"ffn{tag}_up_mid", x, gamma, w_up_t, conv_w)
    return _rows_matmul(f"ffn{tag}_down", a, w_down, NN, x), (h, gu, a)


def _owner_blocks(d, rows=None):
    if rows is not None:
        d = d[:rows]
    return d.reshape((N_DEV, -1) + d.shape[-1:])


def _ffn_bwd(tag, x, gamma, w_up_t, conv_w, w_down, saved, dx, dx16, swap):
    h, gu, a = saved
    da = _rows_matmul(f"ffn{tag}_da", dx16, w_down, NT)
    d_w_down = _owner_blocks(_wgrad_cols_tn(f"ffn{tag}_dwdown", a, WGRAD_FF_TILE, dx16))
    dgu, d_conv = _ffn_mid_bwd(f"ffn{tag}_mid_bwd", gu, conv_w, da)
    d_w_up_t = _owner_blocks(_wgrad_halves_tn(f"ffn{tag}_dwup", dgu, WGRAD_FF_TILE, h))
    parts = (d_w_up_t, d_w_down)
    dx, dx16, d_gamma, *received = _sum_blocks_nn(
        f"ffn{tag}_dh", dgu, w_up_t, norm=(x, gamma, dx), swap=parts if swap else ())
    return dx, dx16, d_gamma, d_conv, parts, received


def _local_step(x, target, w, fetch=None, emit=None):
    if fetch is None:
        local = dict(a=(w.get("a_w_in"), w.get("a_w_out")), b=(w.get("b_w_in"), w.get("b_w_out")))
        for layer in range(2):
            local[f"f{layer}"] = (w["f_w_up"][layer], w["f_w_down"][layer]) if "f_w_up" in w else None
        fetch = lambda group, after: local[group]
    swap = emit is not None
    if emit is None:
        emit = lambda group, parts, received, dx: dx
    f_norm = (w["f_norm"][0:1], w["f_norm"][1:2])

    x0 = x
    a_w_in, a_w_out = fetch("a", x0)
    h0, proj = _norm_proj("a_in", x0, w["a_norm"], a_w_in)
    o, og, states = _gla_fwd(proj, w["a_w_gate_up"], w["a_b_gate"], w["a_gn"])
    x1 = _rows_matmul("a_out", og, a_w_out, NN, x0)
    up0, down0 = fetch("f0", x1)
    x2, ffn0 = _ffn_fwd(0, x1, f_norm[0], up0, w["f_conv"][0], down0)
    b_w_in, b_w_out = fetch("b", x2)
    h2, p = _norm_proj("b_in", x2, w["b_norm"], b_w_in)
    y = _sc_mid_fwd(p, w["b_conv"])
    x3 = _rows_matmul("b_out", y, b_w_out, NN, x2)
    up1, down1 = fetch("f1", x3)
    ffn1 = _ffn_up_mid("ffn1_up_mid", x3, f_norm[1], up1, w["f_conv"][1])
    loss, dx, dx16, d_final_norm = _down_loss_head(ffn1[2], down1, x3, w["final_norm"], target)

    dx, dx16, d_f_norm1, d_fconv1, parts_f1, got = _ffn_bwd(
        1, x3, f_norm[1], up1, w["f_conv"][1], down1, ffn1, dx, dx16, swap)
    dx16 = emit("f1", parts_f1, got, dx16)

    dy = _rows_matmul("b_dy", dx16, b_w_out, NT)
    d_b_w_out = _owner_blocks(_wgrad_cols_tn("b_dwout", y, OUT_TILE, dx16))
    db, dc, dhh, d_b_conv = _sc_mid_bwd(p, w["b_conv"], dy)
    dp = jnp.concatenate([db, dc, dhh], axis=1)
    parts_b = (_wgrad_cols_transposed_tn("b_dwin", h2, dp, B_SHARD), d_b_w_out)
    dx, dx16, d_b_norm, *got = _sum_cols_nt("b_dh", dp, b_w_in, norm=(x2, w["b_norm"], dx), swap=parts_b if swap else ())
    dx16 = emit("b", parts_b, got, dx16)

    dx, dx16, d_f_norm0, d_fconv0, parts_f0, got = _ffn_bwd(
        0, x1, f_norm[0], up0, w["f_conv"][0], down0, ffn0, dx, dx16, swap)
    dx16 = emit("f0", parts_f0, got, dx16)

    dog = _rows_matmul("a_dog", dx16, a_w_out, NT)
    d_a_w_out = _owner_blocks(_wgrad_cols_tn("a_dwout", og, OUT_TILE, dx16))
    dproj, d_wgu, d_b_gate, d_gn = _gla_bwd(proj, w["a_w_gate_up"], w["a_b_gate"], w["a_gn"], o, states, dog)
    parts_a = (_owner_blocks(_wgrad_cols_tn("a_dwin", dproj, PA_TILE, h0), PROJ_A), d_a_w_out)
    dx, _, d_a_norm, *got = _wide_nn("a_dh", dproj, a_w_in, norm=(x0, w["a_norm"], dx), swap=parts_a if swap else ())
    emit("a", parts_a, got, dx)

    grads = dict(
        a_norm=d_a_norm, a_w_in=parts_a[0], a_w_gate_up=d_wgu, a_b_gate=d_b_gate, a_gn=d_gn, a_w_out=parts_a[1],
        b_norm=d_b_norm, b_w_in=parts_b[0], b_conv=d_b_conv, b_w_out=parts_b[1],
        f_norm=(d_f_norm0, d_f_norm1), f_w_up=(parts_f0[0], parts_f1[0]), f_conv=(d_fconv0, d_fconv1),
        f_w_down=(parts_f0[1], parts_f1[1]), final_norm=d_final_norm)
    grads["loss"] = loss
    return dx, grads


MESH_ID = pl.DeviceIdType.MESH
ANY = pl.BlockSpec(memory_space=pl.ANY)
N_PEERS = N_DEV - 1


def _position():
    return lax.axis_index("x"), lax.axis_index("y"), lax.axis_index("c")


def _slot(px, py, pc):
    return 4 * px + 2 * py + pc


GATHER_COPIES = 8
HALF_ROWS = 16


def _gather_copies(src, out, send_sems, recv_sems, local_sems):
    n = len(src)
    to_sibling, to_x, to_y, x_on_to_y, y_on_to_x, x_to_sibling, y_to_sibling, diagonal_to_sibling = range(GATHER_COPIES)
    x, y, c = _position()
    me, sibling = (x, y, c), (x, y, 1 - c)
    x_side, y_side, diagonal = (1 - x, y), (x, 1 - y), (1 - x, 1 - y)

    def rows_of(t, half):
        rows = src[t].shape[0]
        half_rows = rows // 2 // HALF_ROWS * HALF_ROWS
        return (pl.ds(0, rows), pl.ds(0, half_rows), pl.ds(half_rows, rows - half_rows))[half]

    def copy(t, j, block, to, half=0, from_input=False):
        dst = out[t].at[_slot(*block), rows_of(t, half)]
        return pltpu.make_async_remote_copy(
            src_ref=src[t] if from_input else dst, dst_ref=dst, send_sem=send_sems.at[GATHER_COPIES * t + j],
            recv_sem=recv_sems.at[GATHER_COPIES * t + j], device_id=to, device_id_type=MESH_ID)

    mine = [pltpu.make_async_copy(src[t], out[t].at[_slot(*me)], local_sems.at[t]) for t in range(n)]
    for cp in mine:
        cp.start()
    sent = []

    def start(cp):
        cp.start()
        sent.append(cp)

    for t in range(n):
        start(copy(t, to_sibling, me, sibling, from_input=True))
        start(copy(t, to_x, me, (*x_side, c), from_input=True))
        start(copy(t, to_y, me, (*y_side, c), from_input=True))
    for t in range(n):
        copy(t, to_x, (*x_side, c), me).wait_recv()
        start(copy(t, x_on_to_y, (*x_side, c), (*y_side, c), half=1))
        start(copy(t, x_to_sibling, (*x_side, c), sibling))
        copy(t, to_y, (*y_side, c), me).wait_recv()
        start(copy(t, y_on_to_x, (*y_side, c), (*x_side, c), half=2))
        start(copy(t, y_to_sibling, (*y_side, c), sibling))
    for t in range(n):
        copy(t, x_on_to_y, (*diagonal, c), me, half=1).wait_recv()
        copy(t, y_on_to_x, (*diagonal, c), me, half=2).wait_recv()
        start(copy(t, diagonal_to_sibling, (*diagonal, c), sibling))
    for t in range(n):
        copy(t, to_sibling, sibling, me).wait_recv()
        for j, chip in ((x_to_sibling, x_side), (y_to_sibling, y_side), (diagonal_to_sibling, diagonal)):
            copy(t, j, (*chip, 1 - c), me).wait_recv()
    for cp in sent:
        cp.wait_send()
    for cp in mine:
        cp.wait()


def _all_gather(name, shards):
    n = len(shards)

    def body(*refs):
        _gather_copies(refs[:n], refs[n:2 * n], *refs[2 * n:])

    sems = pltpu.SemaphoreType.DMA((GATHER_COPIES * n,))
    return pl.pallas_call(
        body, name=name, in_specs=[ANY] * n, out_specs=[ANY] * n,
        out_shape=[jax.ShapeDtypeStruct((N_DEV,) + s.shape, s.dtype) for s in shards],
        scratch_shapes=[sems, sems, pltpu.SemaphoreType.DMA((n,))],
    )(*shards)


SIBLING_AND_NEIGHBOURS = (1, 2, 4)
SAME_CORE = (2, 4, 6)


def _flip(x, y, c, k):
    return x ^ (k >> 2), y ^ ((k >> 1) & 1), c ^ (k & 1)


N_CHIPS = N_DEV // 2


def _chip(px, py):
    return 2 * px + py


def _pair_copies(parts, received, send_sems, recv_sems):
    x, y, c = lax.axis_index("x"), lax.axis_index("y"), lax.axis_index("c")
    sibling = (x, y, 1 - c)
    copies = []
    for t in range(len(parts)):
        for q in range(N_DEV // 2):
            send = pltpu.make_async_remote_copy(
                src_ref=parts[t].at[2 * q + 1 - c], dst_ref=received[t].at[q], send_sem=send_sems.at[t, q],
                recv_sem=recv_sems.at[t, q], device_id=sibling, device_id_type=pl.DeviceIdType.MESH)
            landed = received[t].at[q]
            arrival = pltpu.make_async_remote_copy(
                src_ref=landed, dst_ref=landed, send_sem=send_sems.at[t, q], recv_sem=recv_sems.at[t, q],
                device_id=sibling, device_id_type=pl.DeviceIdType.MESH)
            copies.append((send, arrival))
    return copies


def _pair_add(name, parts, received, side):
    n = len(parts)

    def body(side_ref, *refs):
        for t in range(n):
            refs[2 * n + t][...] = (refs[t][...].astype(F32) + refs[n + t][...].astype(F32)).astype(BF16)

    own = [_spec((None,) + p.shape[1:], lambda q, side_ref: (2 * q + side_ref[0], 0, 0)) for p in parts]
    chip = [_spec((None,) + p.shape[1:], lambda q, side_ref: (q, 0, 0)) for p in parts]
    return pl.pallas_call(
        body, name=name,
        grid_spec=pltpu.PrefetchScalarGridSpec(num_scalar_prefetch=1, grid=(N_CHIPS,), in_specs=own + chip, out_specs=chip),
        out_shape=[jax.ShapeDtypeStruct((N_CHIPS,) + p.shape[1:], BF16) for p in parts], compiler_params=_params(("parallel",)),
    )(side, *parts, *received)


def _send_copy(parts, landing, send_sems, recv_sems, t, s, k):
    x, y, c = _position()
    px, py, _ = _flip(x, y, c, k)
    return pltpu.make_async_remote_copy(
        src_ref=parts[t].at[_chip(px, py)], dst_ref=landing[t].at[_chip(x, y)], send_sem=send_sems.at[s],
        recv_sem=recv_sems.at[s], device_id=(px, py, c), device_id_type=MESH_ID)


def _send_arrival(landing, send_sems, recv_sems, t, s, k):
    x, y, c = _position()
    px, py, _ = _flip(x, y, c, k)
    landed = landing[t].at[_chip(px, py)]
    return pltpu.make_async_remote_copy(
        src_ref=landed, dst_ref=landed, send_sem=send_sems.at[s], recv_sem=recv_sems.at[s],
        device_id=(px, py, c), device_id_type=MESH_ID)


def _handshake(peers):
    x, y, c = _position()
    barrier = pltpu.get_barrier_semaphore()
    for k in peers:
        pl.semaphore_signal(barrier, inc=1, device_id=_flip(x, y, c, k), device_id_type=MESH_ID)
    pl.semaphore_wait(barrier, len(peers))


def _sequencer(name, collective_id, n_copies, body, operands, out_type):
    n_arrays = len(operands)
    return pl.kernel(
        body, out_type=out_type, mesh=plsc.ScalarSubcoreMesh(axis_name="sequencer", num_cores=1), name=name,
        scratch_types=(pltpu.SemaphoreType.DMA((n_copies,)), pltpu.SemaphoreType.DMA((n_copies,)),
                       pltpu.SemaphoreType.DMA((n_arrays,))),
        compiler_params=pltpu.CompilerParams(collective_id=collective_id))(*operands)


def _sequencer_exchange(name, collective_id, parts, after=()):
    n, n_peers, n_in = len(parts), len(SAME_CORE), len(parts) + len(after)

    def body(*refs):
        src, landing = refs[:n], refs[n_in:n_in + n]
        send_sems, recv_sems, local_sems = refs[n_in + n:]
        _handshake(SAME_CORE)
        x, y, _ = _position()
        mine = [pltpu.make_async_copy(src[t].at[_chip(x, y)], landing[t].at[_chip(x, y)], local_sems.at[t]) for t in range(n)]
        for cp in mine:
            cp.start()
        sent = [_send_copy(src, landing, send_sems, recv_sems, t, t * n_peers + j, k)
                for t in range(n) for j, k in enumerate(SAME_CORE)]
        for cp in sent:
            cp.start()
        for t in range(n):
            for j, k in enumerate(SAME_CORE):
                _send_arrival(landing, send_sems, recv_sems, t, t * n_peers + j, k).wait_recv()
        for cp in sent:
            cp.wait_send()
        for cp in mine:
            cp.wait()

    landing = [jax.ShapeDtypeStruct(p.shape, p.dtype) for p in parts]
    return _sequencer(name, collective_id, n * n_peers, body, list(parts) + list(after), landing)


def _sequencer_gather(name, collective_id, shards):
    n = len(shards)

    def body(*refs):
        _handshake(SIBLING_AND_NEIGHBOURS)
        _gather_copies(refs[:n], refs[n:2 * n], *refs[2 * n:])

    gathered = [jax.ShapeDtypeStruct((N_DEV,) + s.shape, s.dtype) for s in shards]
    return _sequencer(name, collective_id, GATHER_COPIES * n, body, shards, gathered)


ADAM_ROWS = 512
BF16_ROWS = 16


def _adam_update(w, g, m, v):
    m = ADAM_B1 * m + (1.0 - ADAM_B1) * g
    v = ADAM_B2 * v + (1.0 - ADAM_B2) * (g * g)
    m_hat = m / (1.0 - ADAM_B1 ** ADAM_STEP)
    v_hat = v / (1.0 - ADAM_B2 ** ADAM_STEP)
    delta = -ADAM_LR * (m_hat / (jnp.sqrt(v_hat) + ADAM_EPS) + ADAM_WD * w)
    return delta, m, v


def _sum_slots(ref):
    total = ref[0].astype(F32)
    for d in range(1, ref.shape[0]):
        total = total + ref[d].astype(F32)
    return total


def _adamw_sum(name, landed, w, m, v):
    layers, rows, cols = w.shape
    tiles = [t for t in range(ADAM_ROWS, 0, -BF16_ROWS) if rows % t == 0]
    tr = tiles[0] if tiles else rows
    nt = rows // tr

    def body(*refs):
        parts = refs[:layers]
        w_ref, m_ref, v_ref, g_ref, d_ref, nm_ref, nv_ref = refs[layers:]
        layer = pl.program_id(0)
        g = _sum_slots(parts[0])
        for q in range(1, layers):
            g = jnp.where(layer == q, _sum_slots(parts[q]), g)
        delta, new_m, new_v = _adam_update(w_ref[...], g, m_ref[...], v_ref[...])
        g_ref[...] = g
        d_ref[...] = delta
        nm_ref[...] = new_m
        nv_ref[...] = new_v

    def part_spec(q):
        return _spec((N_CHIPS, tr, cols), lambda l, i: (0, jnp.where(l == q, i, jnp.where(l < q, 0, nt - 1)), 0))

    tile = _spec((None, tr, cols), lambda l, i: (l, i, 0))
    out = jax.ShapeDtypeStruct((layers, rows, cols), F32)
    return pl.pallas_call(
        body, name=name, grid=(layers, nt), in_specs=[part_spec(q) for q in range(layers)] + [tile] * 3,
        out_specs=[tile] * 4, out_shape=[out] * 4, compiler_params=_params(("arbitrary", "arbitrary")),
    )(*landed, w, m, v)


def _sum_small(landed):
    def body(in_ref, out_ref):
        out_ref[...] = _sum_slots(in_ref)

    return pl.pallas_call(body, name="small_grad_sum", out_shape=jax.ShapeDtypeStruct(landed.shape[1:], F32))(landed)


def _adamw_small(arrays):
    n = len(arrays)

    def body(*refs):
        for i in range(n):
            g_ref, w_ref, m_ref, v_ref = refs[4 * i:4 * i + 4]
            d_ref, nm_ref, nv_ref = refs[4 * n + 3 * i:4 * n + 3 * i + 3]
            d_ref[...], nm_ref[...], nv_ref[...] = _adam_update(w_ref[...], g_ref[...], m_ref[...], v_ref[...])

    out = [jax.ShapeDtypeStruct(w.shape, F32) for _, w, _, _ in arrays for _ in range(3)]
    flat = pl.pallas_call(body, name="adam_small", out_shape=out)(*[a for group in arrays for a in group])
    return [tuple(flat[3 * i:3 * i + 3]) for i in range(n)]


LANES = 128
SUBLANES = 8
F_CONV_SHARD = D_FF // N_DEV
GATE_SHARD = KEY_DIM // N_DEV
NORM_SHARD = D_MODEL // N_DEV


def _tile_rows(a):
    flat = a.reshape(-1)
    size = -(-flat.shape[0] // (SUBLANES * LANES)) * SUBLANES * LANES
    return jnp.pad(flat, (0, size - flat.shape[0])).reshape(-1, LANES)


def _pack_rows(pieces):
    return jnp.concatenate([_tile_rows(p) for p in pieces], axis=0)


def _unpack_rows(packed, shapes):
    out, row = [], 0
    for shape in shapes:
        size = 1
        for s in shape:
            size *= s
        rows = -(-size // (SUBLANES * LANES)) * SUBLANES
        piece = packed[..., row:row + rows, :]
        out.append(piece.reshape(piece.shape[:-2] + (rows * LANES,))[..., :size])
        row += rows
    return out


SMALL_SHARDS = ((GATE_RANK, GATE_SHARD), (1, NORM_SHARD), (3, NORM_SHARD), (2, 3, F_CONV_SHARD))


def _unpack_small_shards(g):
    gate, b_norm, b_conv, f_conv = _unpack_rows(g, SMALL_SHARDS)
    gate = gate.reshape(N_DEV, GATE_RANK, GATE_SHARD).transpose(1, 0, 2).reshape(GATE_RANK, KEY_DIM)
    b_norm = b_norm.reshape(1, D_MODEL)
    b_conv = b_conv.reshape(N_DEV, 3, NORM_SHARD).transpose(1, 0, 2).reshape(3, D_MODEL)
    f_conv = f_conv.reshape(N_DEV, 2, 3, F_CONV_SHARD).transpose(1, 2, 0, 3).reshape(2, 3, D_FF)
    return gate, b_norm, b_conv, f_conv


SMALL_LAYOUT = (("a_norm", (1, D_MODEL)), ("a_w_gate_up", (GATE_RANK, KEY_DIM)), ("a_b_gate", (1, KEY_DIM)), ("a_gn", (1, VALUE_DIM)),
                ("b_norm", (1, D_MODEL)), ("b_conv", (3, D_MODEL)), ("f_norm0", (1, D_MODEL)), ("f_norm1", (1, D_MODEL)),
                ("f_conv0", (3, D_FF)), ("f_conv1", (3, D_FF)), ("final_norm", (1, D_MODEL)), ("loss", (1, LANES)))


def _pack_small_grads(g):
    full = dict(g)
    full["a_w_gate_up"] = g["a_w_gate_up"][:GATE_RANK]
    for layer in range(2):
        full[f"f_norm{layer}"] = g["f_norm"][layer]
        full[f"f_conv{layer}"] = g["f_conv"][layer]
    return _pack_rows([full[name] for name, _ in SMALL_LAYOUT])


def _unpack_small_grads(packed):
    pieces = _unpack_rows(packed, [shape for _, shape in SMALL_LAYOUT])
    out = {name: piece.reshape(shape) for (name, shape), piece in zip(SMALL_LAYOUT, pieces)}
    out["f_norm"] = jnp.stack([out["f_norm0"][0], out["f_norm1"][0]])
    out["f_conv"] = jnp.stack([out["f_conv0"], out["f_conv1"]])
    return out


def kernel(x, a_norm, a_w_in, a_w_gate_up, a_b_gate, a_gn, a_w_out, b_norm, b_w_in, b_conv, b_w_out, f_norm, f_w_up, f_conv, f_w_down, final_norm, loss_target, m_a_norm, m_a_w_in, m_a_w_gate_up, m_a_b_gate, m_a_gn, m_a_w_out, m_b_norm, m_b_w_in, m_b_conv, m_b_w_out, m_f_norm, m_f_w_up, m_f_conv, m_f_w_down, m_final_norm, v_a_norm, v_a_w_in, v_a_w_gate_up, v_a_b_gate, v_a_gn, v_a_w_out, v_b_norm, v_b_w_in, v_b_conv, v_b_w_out, v_f_norm, v_f_w_up, v_f_conv, v_f_w_down, v_final_norm):
    my_slot = _slot(*_position())

    transposed = lambda w: jnp.swapaxes(w, 1, 2)
    a_transposed = lambda w: w.reshape(D_MODEL, A_SHARD).T.reshape(1, A_SHARD, D_MODEL)
    a_w_in_t, f_w_up_t = a_transposed(a_w_in), transposed(f_w_up)
    first = _all_gather("weight_gather", [a_w_in_t[0].astype(BF16), a_w_out[0].astype(BF16),
                                          _pack_rows([a_w_gate_up[0], b_norm, b_conv[0], f_conv])])
    gathers, small_shards = {}, first[2]
    later = (("f0", f_w_up_t[0], f_w_down[0]), ("b", b_w_in[0], b_w_out[0]), ("f1", f_w_up_t[1], f_w_down[1]))
    for collective_id, (group, w_in, w_out) in enumerate(later):
        w_in, w_out, small_shards = lax.optimization_barrier((w_in.astype(BF16), w_out.astype(BF16), small_shards))
        gathers[group] = _sequencer_gather(f"gather_{group}", collective_id, [w_in, w_out])
    gate_full, b_norm_full, b_conv_full, f_conv_full = _unpack_small_shards(small_shards)
    a_w_in_full = jnp.pad(first[0].reshape(PROJ_A, D_MODEL), ((0, PROJ_A_PAD - PROJ_A), (0, 0)))
    weights = dict(
        a_norm=a_norm, a_w_gate_up=jnp.pad(gate_full, ((0, GATE_PAD - GATE_RANK), (0, 0))).astype(BF16), a_b_gate=a_b_gate,
        a_gn=a_gn, b_norm=b_norm_full, b_conv=b_conv_full, f_norm=f_norm, f_conv=f_conv_full,
        final_norm=final_norm.reshape(1, D_MODEL))

    def fetch(group, after):
        if group == "a":
            return a_w_in_full, first[1].reshape(D_MODEL, D_MODEL)
        w_in, w_out = gathers[group]
        if group == "b":
            return w_in, w_out.reshape(D_MODEL, D_MODEL)
        return w_in.reshape(2, D_FF, D_MODEL), w_out.reshape(D_FF, D_MODEL)

    exchanges, pending = {}, []
    exchange_ids = dict(b=3, f0=4, a=5)
    side = lax.axis_index("c").astype(jnp.int32).reshape(1)

    def emit(group, parts, received, carry):
        sums = _pair_add(f"pair_add_{group}", parts, received, side)
        carry, *sums = lax.optimization_barrier((carry, *sums))
        pending.extend(sums)
        if group != "f1":
            after = list(exchanges.values())[-1][:1] if exchanges else ()
            exchanges[group] = _sequencer_exchange(f"grads_{group}", exchange_ids[group], list(pending), after)
            pending.clear()
        return carry

    dx, g = _local_step(x[0], loss_target[0], weights, fetch, emit)

    (up1, down1, d_b_in, d_b_out), (up0, down0), (d_a_in, d_a_out) = (exchanges[group] for group in ("b", "f0", "a"))
    back = lambda results: tuple(transposed(r) for r in results)
    big = dict(
        b_w_in=_adamw_sum("adam_b_w_in", [d_b_in], b_w_in, m_b_w_in, v_b_w_in),
        b_w_out=_adamw_sum("adam_b_w_out", [d_b_out], b_w_out, m_b_w_out, v_b_w_out),
        f_w_up=back(_adamw_sum("adam_f_w_up", [up0, up1], f_w_up_t, transposed(m_f_w_up), transposed(v_f_w_up))),
        f_w_down=_adamw_sum("adam_f_w_down", [down0, down1], f_w_down, m_f_w_down, v_f_w_down))
    small_packed, *updated = lax.optimization_barrier((_pack_small_grads(g), *big["f_w_down"]))
    big["f_w_down"] = tuple(updated)
    small_landed = _all_gather("small_grad_gather", [small_packed])[0]
    big.update(
        a_w_in=tuple(r.reshape(A_SHARD, D_MODEL).T.reshape(1, D_MODEL, A_SHARD) for r in _adamw_sum(
            "adam_a_w_in", [d_a_in], a_w_in_t, a_transposed(m_a_w_in), a_transposed(v_a_w_in))),
        a_w_out=_adamw_sum("adam_a_w_out", [d_a_out], a_w_out, m_a_w_out, v_a_w_out))
    small_g = _unpack_small_grads(_sum_small(small_landed))
    loss = small_g["loss"][0, 0]
    small_g["a_w_gate_up"] = lax.dynamic_slice_in_dim(small_g["a_w_gate_up"], my_slot * GATE_SHARD, GATE_SHARD, axis=1)
    small_g["b_norm"] = lax.dynamic_slice_in_dim(small_g["b_norm"], my_slot * NORM_SHARD, NORM_SHARD, axis=1)
    small_g["b_conv"] = lax.dynamic_slice_in_dim(small_g["b_conv"], my_slot * NORM_SHARD, NORM_SHARD, axis=1)
    small_g["f_conv"] = lax.dynamic_slice_in_dim(small_g["f_conv"], my_slot * F_CONV_SHARD, F_CONV_SHARD, axis=2)
    small_w = dict(
        a_norm=(a_norm, m_a_norm, v_a_norm), a_w_gate_up=(a_w_gate_up, m_a_w_gate_up, v_a_w_gate_up),
        a_b_gate=(a_b_gate, m_a_b_gate, v_a_b_gate), a_gn=(a_gn, m_a_gn, v_a_gn), b_norm=(b_norm, m_b_norm, v_b_norm),
        b_conv=(b_conv, m_b_conv, v_b_conv), f_norm=(f_norm, m_f_norm, v_f_norm), f_conv=(f_conv, m_f_conv, v_f_conv),
        final_norm=(final_norm, m_final_norm, v_final_norm))
    two_d = lambda a: a.reshape(-1, a.shape[-1])
    updates = _adamw_small([tuple(two_d(a.reshape(w.shape)) for a in (small_g[name], w, m, v)) for name, (w, m, v) in small_w.items()])
    small = {}
    for (name, (w, _, _)), update in zip(small_w.items(), updates):
        small[name] = (small_g[name].reshape(w.shape),) + tuple(u.reshape(w.shape) for u in update)

    order = ["a_norm", "a_w_in", "a_w_gate_up", "a_b_gate", "a_gn", "a_w_out", "b_norm", "b_w_in", "b_conv", "b_w_out",
             "f_norm", "f_w_up", "f_conv", "f_w_down", "final_norm"]
    results = {**big, **small}
    outputs = [loss, dx.reshape(1, SEQ, D_MODEL)]
    for kind in range(4):
        outputs += [results[name][kind] for name in order]
    return tuple(outputs)
```

```python
import jax
import jax.numpy as jnp
from jax import lax
from jax.experimental import pallas as pl
from jax.experimental.pallas import tpu as pltpu
from jax.experimental.pallas import tpu_sc as plsc

F32 = jnp.float32
BF16 = jnp.bfloat16

N_DEV = 8
SEQ = 2048
D_MODEL = 1024
CHUNK = 64
N_CHUNKS = SEQ // CHUNK
RMS_EPS = 1e-6
GLA_HEADS = 4
KEY_DIM = 512
VALUE_DIM = 1024
HEAD_K = KEY_DIM // GLA_HEADS
HEAD_V = VALUE_DIM // GLA_HEADS
GATE_RANK = 16
GATE_PAD = 128
GATE_NORMALIZER = 16.0
PROJ_A = 2 * KEY_DIM + 2 * VALUE_DIM + GATE_RANK
PROJ_A_PAD = 2 * KEY_DIM + 2 * VALUE_DIM + GATE_PAD
A_SHARD = PROJ_A // N_DEV
B_SHARD = 3 * D_MODEL // N_DEV
D_FF = 2816
ADAM_LR = 0.001
ADAM_B1 = 0.9
ADAM_B2 = 0.999
ADAM_EPS = 1e-08
ADAM_WD = 0.01
ADAM_STEP = 10
MESH_AXES = ("x", "y", "c")

VMEM_LIMIT = 56 * 1024 * 1024
ROW_CHUNK = 256
HALO = 16


def _params(sem=None, vmem=VMEM_LIMIT):
    return pltpu.CompilerParams(dimension_semantics=sem, vmem_limit_bytes=vmem)


SWAP_IDS = {"ffn1_dh": 6, "b_dh": 7, "ffn0_dh": 8, "a_dh": 9}
NORM_PARTS = 2
NN = ((1,), (0,))
NT = ((1,), (1,))
TN = ((0,), (0,))


def _matmul(name, a, a_spec, b, b_spec, dims, grid, out_shape, out_spec, k_blocks=None, a_block_cols=None, res=None,
            res_spec=None, transpose_out=False, norm=None, swap=()):
    has_res = res is not None
    n_swap = len(swap)

    def body(*refs):
        a_ref, b_ref = refs[0], refs[1]
        r_ref = refs[2] if has_res else None

        def product(lhs, rhs):
            return lax.dot_general(lhs.astype(BF16), rhs, (dims, ((), ())), preferred_element_type=F32)

        def tile(rows=slice(None)):
            if k_blocks is None:
                return product(a_ref[rows, :] if norm is not None else a_ref[...], b_ref[...])
            v = None
            for k in range(k_blocks):
                lhs = a_ref[k, rows, :] if a_block_cols is None else a_ref[rows, k * a_block_cols:(k + 1) * a_block_cols]
                p = product(lhs, b_ref[k])
                v = p if v is None else v + p
            return v

        if norm is None:
            v = tile()
            if transpose_out:
                v = v.T
            if has_res:
                v = v + r_ref[...]
            o_ref = refs[2 + has_res]
            o_ref[...] = v.astype(o_ref.dtype)
            return
        n_in = 5 + has_res
        x_ref, g_ref, dxi_ref = refs[2 + has_res:n_in]
        dx_ref, dx16_ref, dg_ref = refs[n_in + n_swap:n_in + n_swap + 3]
        if n_swap:
            copies = _pair_copies(refs[n_in:n_in + n_swap], refs[n_in + n_swap + 3:n_in + 2 * n_swap + 3], *refs[-2:])

            @pl.when(pl.program_id(0) == 0)
            def _():
                sibling = (lax.axis_index("x"), lax.axis_index("y"), 1 - lax.axis_index("c"))
                barrier = pltpu.get_barrier_semaphore()
                pl.semaphore_signal(barrier, inc=1, device_id=sibling, device_id_type=pl.DeviceIdType.MESH)
                pl.semaphore_wait(barrier, 1)
                for send, _ in copies:
                    send.start()

            @pl.when(pl.program_id(0) == grid[0] - 1)
            def _():
                for send, arrival in copies:
                    arrival.wait_recv()
                    send.wait_send()

        dg = None
        part = dx_ref.shape[0] // NORM_PARTS
        for rows in (slice(i * part, (i + 1) * part) for i in range(NORM_PARTS)):
            dx, dg_rows = _norm_bwd_rows(x_ref[rows, :], g_ref[...], tile(rows))
            dx = dxi_ref[rows, :] + dx
            dx_ref[rows, :] = dx
            dx16_ref[rows, :] = dx.astype(BF16)
            dg = dg_rows if dg is None else dg + dg_rows

        @pl.when(pl.program_id(0) == 0)
        def _():
            dg_ref[...] = dg

        @pl.when(pl.program_id(0) > 0)
        def _():
            dg_ref[...] += dg

    operands = [a, b] + ([res] if has_res else [])
    in_specs = [a_spec, b_spec] + ([res_spec] if has_res else [])
    semantics = ("parallel",) * len(grid)
    scratch = []
    if norm is not None:
        vec = _spec((1, D_MODEL), lambda i: (0, 0))
        any_space = pl.BlockSpec(memory_space=pl.ANY)
        operands += list(norm) + list(swap)
        in_specs += [out_spec, vec, out_spec] + [any_space] * n_swap
        out_shape = [_act(dtype=F32), _act(), jax.ShapeDtypeStruct((1, D_MODEL), F32)]
        out_shape += [jax.ShapeDtypeStruct((N_DEV // 2,) + p.shape[1:], p.dtype) for p in swap]
        out_spec = [out_spec, out_spec, vec] + [any_space] * n_swap
        semantics = ("arbitrary",)
        if n_swap:
            scratch = [pltpu.SemaphoreType.DMA((n_swap, N_DEV // 2))] * 2
    params = _params(semantics)
    if n_swap:
        params = pltpu.CompilerParams(dimension_semantics=semantics, vmem_limit_bytes=VMEM_LIMIT, collective_id=SWAP_IDS[name])
    return pl.pallas_call(
        body, name=name, grid=grid, in_specs=in_specs, out_specs=out_spec, out_shape=out_shape, scratch_shapes=scratch,
        compiler_params=params,
    )(*operands)


def _resident(shape):
    return pl.BlockSpec(shape, lambda *_: (0,) * len(shape), pipeline_mode=pl.Buffered(1))


TM = 512
N_TM = SEQ // TM
PA_TILE = 640
N_PA = PROJ_A_PAD // PA_TILE
OUT_TILE = 256


def _spec(shape, fn):
    return pl.BlockSpec(shape, fn)


def _act(shape=(SEQ, D_MODEL), dtype=BF16):
    return jax.ShapeDtypeStruct(shape, dtype)


def _norm_proj(name, x, gamma, w):
    blocks = w.ndim == 3
    n_out = w.shape[0] * w.shape[2] if blocks else w.shape[0]

    def body(x_ref, g_ref, w_ref, h_ref, o_ref):
        x = x_ref[...]
        h = (x * _rstd(x) * g_ref[...]).astype(BF16)
        h_ref[...] = h
        if blocks:
            n = w.shape[2]
            for j in range(w.shape[0]):
                o_ref[:, j * n:(j + 1) * n] = jnp.dot(h, w_ref[j], preferred_element_type=F32).astype(BF16)
        else:
            o_ref[...] = lax.dot_general(h, w_ref[...], (NT, ((), ())), preferred_element_type=F32).astype(BF16)

    row = _spec((TM, D_MODEL), lambda i: (i, 0))
    return pl.pallas_call(
        body, name=name, grid=(N_TM,), in_specs=[row, _resident((1, D_MODEL)), _resident(w.shape)],
        out_specs=[row, _spec((TM, n_out), lambda i: (i, 0))], out_shape=[_act(), _act((SEQ, n_out))],
        compiler_params=_params(("parallel",)),
    )(x, gamma, w)


def _rows_matmul(name, a, w, dims, x=None):
    k = a.shape[1]
    n = w.shape[1] if dims == NN else w.shape[0]
    row = _spec((TM, n), lambda i: (i, 0))
    return _matmul(name, a, _spec((TM, k), lambda i: (i, 0)), w, _resident(w.shape), dims, (N_TM,),
                   _act((SEQ, n), F32 if x is not None else BF16), row, res=x, res_spec=row if x is not None else None)


def _sum_blocks_nn(name, a_blocks, w_blocks, x=None, norm=None, swap=()):
    nb, _, n = a_blocks.shape
    row = _spec((TM, D_MODEL), lambda i: (i, 0))
    return _matmul(name, a_blocks, _spec((nb, TM, n), lambda i: (0, i, 0)), w_blocks, _resident((nb, n, D_MODEL)),
                   NN, (N_TM,), _act(dtype=F32), row, k_blocks=nb, res=x, res_spec=row if x is not None else None, norm=norm, swap=swap)


def _sum_cols_nt(name, d, w_blocks, norm=None, swap=()):
    nb, _, n = w_blocks.shape
    return _matmul(name, d, _spec((TM, nb * n), lambda i: (i, 0)), w_blocks, _resident((nb, D_MODEL, n)), NT,
                   (N_TM,), _act(dtype=F32), _spec((TM, D_MODEL), lambda i: (i, 0)), k_blocks=nb, a_block_cols=n, norm=norm, swap=swap)


def _wide_nn(name, d, wt, x=None, norm=None, swap=()):
    n = wt.shape[0]
    row = _spec((TM, D_MODEL), lambda i: (i, 0))
    return _matmul(name, d, _spec((TM, n), lambda i: (i, 0)), wt, _resident((n, D_MODEL)), NN, (N_TM,),
                   _act(dtype=F32), row, res=x, res_spec=row if x is not None else None, norm=norm, swap=swap)


def _wgrad_halves_tn(name, d, n_tile, h):
    _, _, n = d.shape
    return _matmul(name, d, _spec((None, SEQ, n_tile), lambda p, j: (p, 0, j)), h, _resident((SEQ, D_MODEL)), TN,
                   (2, n // n_tile), _act((2, n, D_MODEL)), _spec((None, n_tile, D_MODEL), lambda p, j: (p, j, 0)))


def _wgrad_cols_tn(name, d, n_tile, h):
    n = d.shape[1]
    return _matmul(name, d, _spec((SEQ, n_tile), lambda j: (0, j)), h, _resident((SEQ, D_MODEL)), TN,
                   (n // n_tile,), _act((n, D_MODEL)), _spec((n_tile, D_MODEL), lambda j: (j, 0)))


def _wgrad_cols_transposed_tn(name, h, d, n_tile):
    nb = d.shape[1] // n_tile
    return _matmul(name, d, _spec((SEQ, n_tile), lambda j: (0, j)), h, _resident((SEQ, D_MODEL)), TN, (nb,),
                   _act((nb, D_MODEL, n_tile)), _spec((None, D_MODEL, n_tile), lambda j: (j, 0, 0)), transpose_out=True)


NORM_ROWS = 512


def _rstd(x):
    return lax.rsqrt(jnp.mean(x * x, axis=-1, keepdims=True) + RMS_EPS)


def _norm_bwd_rows(x, gamma, dh):
    r = _rstd(x)
    xh = x * r
    dxh = dh * gamma
    dx = r * (dxh - xh * jnp.mean(dxh * xh, axis=-1, keepdims=True))
    return dx, jnp.sum(dh * xh, axis=0, keepdims=True)


def _down_loss_head(a, w_down, x_in, gamma, target):
    def body(a_ref, w_ref, x_ref, g_ref, t_ref, loss_ref, dx_ref, dx16_ref, dg_ref):
        gamma = g_ref[...]
        dg, part = 0.0, 0.0
        rows_per_part = TM // NORM_PARTS
        for rows in (slice(i * rows_per_part, (i + 1) * rows_per_part) for i in range(NORM_PARTS)):
            x = x_ref[rows, :] + jnp.dot(a_ref[rows, :], w_ref[...], preferred_element_type=F32)
            err = x * _rstd(x) * gamma - t_ref[rows, :]
            dy = err * (1.0 / D_MODEL)
            dx, dg_rows = _norm_bwd_rows(x, gamma, dy)
            dx_ref[rows, :] = dx
            dx16_ref[rows, :] = dx.astype(BF16)
            dg = dg + dg_rows
            part = part + 0.5 * jnp.sum(jnp.sum(err * err, axis=-1, keepdims=True) * (1.0 / D_MODEL), axis=0, keepdims=True)
        part = jnp.broadcast_to(part, loss_ref.shape)

        @pl.when(pl.program_id(0) == 0)
        def _():
            dg_ref[...] = dg
            loss_ref[...] = part

        @pl.when(pl.program_id(0) > 0)
        def _():
            dg_ref[...] += dg
            loss_ref[...] += part

    row = _spec((TM, D_MODEL), lambda i: (i, 0))
    vec = _spec((1, D_MODEL), lambda i: (0, 0))
    return pl.pallas_call(
        body, name="ffn1_down_loss_head", grid=(N_TM,),
        in_specs=[_spec((TM, D_FF), lambda i: (i, 0)), _resident((D_FF, D_MODEL)), row, vec, row],
        out_specs=[_spec((1, 128), lambda i: (0, 0)), row, row, vec],
        out_shape=[jax.ShapeDtypeStruct((1, 128), F32), _act(dtype=F32), _act(), jax.ShapeDtypeStruct((1, D_MODEL), F32)],
        compiler_params=_params(("arbitrary",)),
    )(a, w_down, x_in, gamma, target)


def _sigmoid(x):
    return 1.0 / (1.0 + jnp.exp(-x))


def _rows(ref, c):
    return ref[pl.ds(pl.multiple_of(c * ROW_CHUNK, ROW_CHUNK), ROW_CHUNK), :].astype(F32)


def _rows_before(ref, c):
    start = pl.multiple_of(jnp.maximum(c * ROW_CHUNK - HALO, 0), HALO)
    rows = ref[pl.ds(start, HALO), :].astype(F32)
    return jnp.where(c > 0, rows, 0.0)


def _rows_after(ref, c, n_chunks):
    start = pl.multiple_of(jnp.minimum((c + 1) * ROW_CHUNK, SEQ - HALO), HALO)
    rows = ref[pl.ds(start, HALO), :].astype(F32)
    return jnp.where(c < n_chunks - 1, rows, 0.0)


def _shift_down(z, before, n):
    return pltpu.roll(jnp.concatenate([before, z], axis=0), n, 0)[before.shape[0]:]


def _shift_up(z, after, n):
    rows = z.shape[0]
    return pltpu.roll(jnp.concatenate([z, after], axis=0), rows + HALO - n, 0)[:rows]


def _conv_rows(z, before, w):
    z1 = _shift_down(z, before, 1)
    z2 = _shift_down(z, before, 2)
    return w[2:3, :] * z + w[1:2, :] * z1 + w[0:1, :] * z2, z1, z2


def _conv_t_rows(dy, after, w):
    return w[2:3, :] * dy + w[1:2, :] * _shift_up(dy, after, 1) + w[0:1, :] * _shift_up(dy, after, 2)


N_ROW_CHUNKS = SEQ // ROW_CHUNK


FF_COLS = 256
N_FF_COLS = D_FF // FF_COLS


def _ffn_mid_bwd(name, gu, conv_w, da):
    def body(gu_ref, w_ref, da_ref, dgu_ref, dw_ref, dgc_ref):
        w = w_ref[...]

        def first(c, acc):
            g = _rows(gu_ref.at[0], c)
            u = _rows(gu_ref.at[1], c)
            d = _rows(da_ref, c)
            gc, g1, g2 = _conv_rows(g, _rows_before(gu_ref.at[0], c), w)
            sg = _sigmoid(gc)
            rows = pl.ds(pl.multiple_of(c * ROW_CHUNK, ROW_CHUNK), ROW_CHUNK)
            silu = gc * sg
            dgu_ref[1, rows, :] = (d * silu).astype(BF16)
            dgc = d * u * (sg + silu * (1.0 - sg))
            dgc_ref[rows, :] = dgc
            return (acc[0] + jnp.sum(dgc * g2, axis=0, keepdims=True), acc[1] + jnp.sum(dgc * g1, axis=0, keepdims=True),
                    acc[2] + jnp.sum(dgc * g, axis=0, keepdims=True))

        zero = jnp.zeros((1, FF_COLS), F32)
        acc = lax.fori_loop(0, N_ROW_CHUNKS, first, (zero, zero, zero))
        for r in range(3):
            dw_ref[r:r + 1, :] = acc[r]

        def second(c, carry):
            dgc = _rows(dgc_ref, c)
            dg = _conv_t_rows(dgc, _rows_after(dgc_ref, c, N_ROW_CHUNKS), w)
            dgu_ref[0, pl.ds(pl.multiple_of(c * ROW_CHUNK, ROW_CHUNK), ROW_CHUNK), :] = dg.astype(BF16)
            return carry

        lax.fori_loop(0, N_ROW_CHUNKS, second, 0)

    pair = _spec((2, SEQ, FF_COLS), lambda j: (0, 0, j))
    wspec = _spec((3, FF_COLS), lambda j: (0, j))
    return pl.pallas_call(
        body, name=name, grid=(N_FF_COLS,), in_specs=[pair, wspec, _spec((SEQ, FF_COLS), lambda j: (0, j))],
        out_specs=[pair, wspec], out_shape=[_act((2, SEQ, D_FF)), jax.ShapeDtypeStruct((3, D_FF), F32)],
        scratch_shapes=[pltpu.VMEM((SEQ, FF_COLS), F32)],
        compiler_params=_params(("parallel",)),
    )(gu, conv_w, da)


SC_COLS = 256
N_SC = D_MODEL // SC_COLS


def _sc_specs():
    return [_spec((SEQ, SC_COLS), lambda j, part=part: (0, part * N_SC + j)) for part in range(3)]


def _sc_mid_fwd(p, conv_w):
    def body(b_ref, c_ref, h_ref, w_ref, y_ref):
        w = w_ref[...]

        def chunk(c, carry):
            z = _rows(c_ref, c) * _rows(h_ref, c)
            before = _rows_before(c_ref, c) * _rows_before(h_ref, c)
            zc, _, _ = _conv_rows(z, before, w)
            y_ref[pl.ds(pl.multiple_of(c * ROW_CHUNK, ROW_CHUNK), ROW_CHUNK), :] = (_rows(b_ref, c) * zc).astype(BF16)
            return carry

        lax.fori_loop(0, N_ROW_CHUNKS, chunk, 0)

    col = _spec((SEQ, SC_COLS), lambda j: (0, j))
    return pl.pallas_call(
        body, name="sc_mid_fwd", grid=(N_SC,), in_specs=_sc_specs() + [_spec((3, SC_COLS), lambda j: (0, j))], out_specs=col,
        out_shape=jax.ShapeDtypeStruct((SEQ, D_MODEL), BF16), compiler_params=_params(("parallel",)),
    )(p, p, p, conv_w)


def _sc_mid_bwd(p, conv_w, dy):
    def body(b_ref, c_ref, h_ref, w_ref, dy_ref, db_ref, dc_ref, dh_ref, dw_ref, dzc_ref):
        w = w_ref[...]

        def first(c, acc):
            z = _rows(c_ref, c) * _rows(h_ref, c)
            before = _rows_before(c_ref, c) * _rows_before(h_ref, c)
            zc, z1, z2 = _conv_rows(z, before, w)
            d = _rows(dy_ref, c)
            rows = pl.ds(pl.multiple_of(c * ROW_CHUNK, ROW_CHUNK), ROW_CHUNK)
            db_ref[rows, :] = (d * zc).astype(BF16)
            dzc = d * _rows(b_ref, c)
            dzc_ref[rows, :] = dzc
            return (acc[0] + jnp.sum(dzc * z2, axis=0, keepdims=True), acc[1] + jnp.sum(dzc * z1, axis=0, keepdims=True),
                    acc[2] + jnp.sum(dzc * z, axis=0, keepdims=True))

        zero = jnp.zeros((1, SC_COLS), F32)
        acc = lax.fori_loop(0, N_ROW_CHUNKS, first, (zero, zero, zero))
        for r in range(3):
            dw_ref[r:r + 1, :] = acc[r]

        def second(c, carry):
            dz = _conv_t_rows(_rows(dzc_ref, c), _rows_after(dzc_ref, c, N_ROW_CHUNKS), w)
            rows = pl.ds(pl.multiple_of(c * ROW_CHUNK, ROW_CHUNK), ROW_CHUNK)
            dc_ref[rows, :] = (dz * _rows(h_ref, c)).astype(BF16)
            dh_ref[rows, :] = (dz * _rows(c_ref, c)).astype(BF16)
            return carry

        lax.fori_loop(0, N_ROW_CHUNKS, second, 0)

    col = _spec((SEQ, SC_COLS), lambda j: (0, j))
    wspec = _spec((3, SC_COLS), lambda j: (0, j))
    act = jax.ShapeDtypeStruct((SEQ, D_MODEL), BF16)
    return pl.pallas_call(
        body, name="sc_mid_bwd", grid=(N_SC,), in_specs=_sc_specs() + [wspec, col], out_specs=[col, col, col, wspec],
        out_shape=[act, act, act, jax.ShapeDtypeStruct((3, D_MODEL), F32)],
        scratch_shapes=[pltpu.VMEM((SEQ, SC_COLS), F32)], compiler_params=_params(("parallel",)),
    )(p, p, p, conv_w, dy)


GLA_GROUP = 4
GLA_ROWS = GLA_GROUP * CHUNK
N_GROUPS = N_CHUNKS // GLA_GROUP
Q0, K0, V0, R0, G0 = 0, KEY_DIM, 2 * KEY_DIM, 2 * KEY_DIM + VALUE_DIM, 2 * KEY_DIM + 2 * VALUE_DIM


def _tri(strict):
    r = lax.broadcasted_iota(jnp.int32, (CHUNK, CHUNK), 0)
    c = lax.broadcasted_iota(jnp.int32, (CHUNK, CHUNK), 1)
    return jnp.where(c < r if strict else c <= r, 1.0, 0.0).astype(F32)


def _cumsum_rows(tri, x):
    tri = tri.astype(BF16)
    total = None
    for _ in range(3):
        term = x.astype(BF16)
        x = x - term.astype(F32)
        product = jnp.dot(tri, term, preferred_element_type=F32)
        total = product if total is None else total + product
    return total


def _gate_logits(gl, wgu, b_gate):
    return jnp.dot(gl, wgu, preferred_element_type=F32) + b_gate


def _log_decay(logits):
    return (jnp.minimum(logits, 0.0) - jnp.log(1.0 + jnp.exp(-jnp.abs(logits)))) * (1.0 / GATE_NORMALIZER)


def _head(x, h, width):
    return x[:, h * width:(h + 1) * width]


def _gla_fwd(proj, wgu, b_gate, gn):
    def body(p_ref, wgu_ref, b_ref, gn_ref, o_ref, og_ref, st_ref, state):
        @pl.when(pl.program_id(0) == 0)
        def _():
            state[...] = jnp.zeros_like(state)

        tri = _tri(False)
        la = _log_decay(_gate_logits(p_ref[:, G0:G0 + GATE_PAD], wgu_ref[...], b_ref[...]))
        decays = []
        for c in range(GLA_GROUP):
            rows = slice(c * CHUNK, (c + 1) * CHUNK)
            cum = _cumsum_rows(tri, la[rows])
            tot = cum[CHUNK - 1:CHUNK, :]
            kd = (p_ref[rows, K0:K0 + KEY_DIM].astype(F32) * jnp.exp(tot - cum)).astype(BF16)
            decays.append(jnp.exp(tot))
            v = p_ref[rows, V0:V0 + VALUE_DIM]
            for h in range(GLA_HEADS):
                st_ref[c, h] = lax.dot_general(
                    _head(v, h, HEAD_V), _head(kd, h, HEAD_K), (TN, ((), ())), preferred_element_type=F32)
        for c in range(GLA_GROUP):
            for h in range(GLA_HEADS):
                s = state[h] * _head(decays[c], h, HEAD_K) + st_ref[c, h]
                state[h] = s
                st_ref[c, h] = s
        for c in range(GLA_GROUP):
            rows = slice(c * CHUNK, (c + 1) * CHUNK)
            q = (p_ref[rows, Q0:Q0 + KEY_DIM].astype(F32) * (HEAD_K ** -0.5)).astype(BF16)
            for h in range(GLA_HEADS):
                o_ref[rows, h * HEAD_V:(h + 1) * HEAD_V] = lax.dot_general(
                    _head(q, h, HEAD_K), st_ref[c, h].astype(BF16), (NT, ((), ())), preferred_element_type=F32)
        r = p_ref[:, R0:R0 + VALUE_DIM].astype(F32)
        gate = r * _sigmoid(r) * gn_ref[...]
        for h in range(GLA_HEADS):
            cols = slice(h * HEAD_V, (h + 1) * HEAD_V)
            o = o_ref[:, cols]
            og_ref[:, cols] = (o * _rstd(o) * gate[:, cols]).astype(BF16)

    rows = _spec((GLA_ROWS, VALUE_DIM), lambda i: (i, 0))
    const = lambda shape: _spec(shape, lambda i: (0,) * len(shape))
    return pl.pallas_call(
        body, name="gla_fwd", grid=(N_GROUPS,),
        in_specs=[_spec((GLA_ROWS, PROJ_A_PAD), lambda i: (i, 0)), const((GATE_PAD, KEY_DIM)), const((1, KEY_DIM)),
                  const((1, VALUE_DIM))],
        out_specs=[rows, rows, _spec((GLA_GROUP, GLA_HEADS, HEAD_V, HEAD_K), lambda i: (i, 0, 0, 0))],
        out_shape=[jax.ShapeDtypeStruct((SEQ, VALUE_DIM), F32), jax.ShapeDtypeStruct((SEQ, VALUE_DIM), BF16),
                   jax.ShapeDtypeStruct((N_CHUNKS, GLA_HEADS, HEAD_V, HEAD_K), F32)],
        scratch_shapes=[pltpu.VMEM((GLA_HEADS, HEAD_V, HEAD_K), F32)], compiler_params=_params(("arbitrary",)),
    )(proj, wgu, b_gate, gn)


def _gla_bwd(proj, wgu, b_gate, gn, o, states, dog):
    last = N_GROUPS - 1

    def body(p_ref, wgu_ref, b_ref, gn_ref, o_ref, st_ref, stp_ref, dog_ref, dp_ref, dwgu_ref, db_ref, dgn_ref, carry, do_buf,
             g_buf):
        step = pl.program_id(0)

        @pl.when(step == 0)
        def _():
            carry[...] = jnp.zeros_like(carry)

        r = p_ref[:, R0:R0 + VALUE_DIM].astype(F32)
        sr = _sigmoid(r)
        silu = r * sr
        gn_row = gn_ref[...]
        dog_rows = dog_ref[...].astype(F32)
        dn = dog_rows * silu
        dgn_cols = []
        for h in range(GLA_HEADS):
            cols = slice(h * HEAD_V, (h + 1) * HEAD_V)
            oh = o_ref[:, cols]
            rs = _rstd(oh)
            ohat = oh * rs
            dn_h = dn[:, cols]
            dgn_cols.append(jnp.sum(dn_h * ohat, axis=0, keepdims=True))
            dohat = dn_h * gn_row[:, cols]
            do_buf[:, cols] = rs * (dohat - ohat * jnp.mean(dohat * ohat, axis=-1, keepdims=True))
            n_h = ohat * gn_row[:, cols]
            dp_ref[:, R0 + h * HEAD_V:R0 + (h + 1) * HEAD_V] = (
                dog_rows[:, cols] * n_h * (sr[:, cols] * (1.0 + r[:, cols] * (1.0 - sr[:, cols])))).astype(BF16)
        dgn = jnp.concatenate(dgn_cols, axis=1)

        tri = _tri(False)
        tri_strict = _tri(True)
        gl = p_ref[:, G0:G0 + GATE_PAD]
        logits = _gate_logits(gl, wgu_ref[...], b_ref[...])
        la = _log_decay(logits)
        fades, kds, decays = [], [], []
        for c in range(GLA_GROUP):
            rows = slice(c * CHUNK, (c + 1) * CHUNK)
            cum = _cumsum_rows(tri, la[rows])
            tot = cum[CHUNK - 1:CHUNK, :]
            fades.append(jnp.exp(tot - cum))
            kds.append(p_ref[rows, K0:K0 + KEY_DIM].astype(F32) * fades[c])
            decays.append(jnp.exp(tot))
            q = (p_ref[rows, Q0:Q0 + KEY_DIM].astype(F32) * (HEAD_K ** -0.5)).astype(BF16)
            do = do_buf[rows, :].astype(BF16)
            for h in range(GLA_HEADS):
                do_h = _head(do, h, HEAD_V)
                dq = jnp.dot(do_h, st_ref[c, h].astype(BF16), preferred_element_type=F32) * (HEAD_K ** -0.5)
                dp_ref[rows, Q0 + h * HEAD_K:Q0 + (h + 1) * HEAD_K] = dq.astype(BF16)
                g_buf[c, h] = lax.dot_general(do_h, _head(q, h, HEAD_K), (TN, ((), ())), preferred_element_type=F32)
        for c in reversed(range(GLA_GROUP)):
            for h in range(GLA_HEADS):
                g = carry[h] + g_buf[c, h]
                g_buf[c, h] = g
                carry[h] = g * _head(decays[c], h, HEAD_K)
        dlogit_rows = []
        for c in range(GLA_GROUP):
            rows = slice(c * CHUNK, (c + 1) * CHUNK)
            v = p_ref[rows, V0:V0 + VALUE_DIM]
            kd = kds[c].astype(BF16)
            dkd_cols, ddecay_cols = [], []
            for h in range(GLA_HEADS):
                g = g_buf[c, h]
                g16 = g.astype(BF16)
                dkd_cols.append(jnp.dot(_head(v, h, HEAD_V), g16, preferred_element_type=F32))
                dv = lax.dot_general(_head(kd, h, HEAD_K), g16, (NT, ((), ())), preferred_element_type=F32)
                dp_ref[rows, V0 + h * HEAD_V:V0 + (h + 1) * HEAD_V] = dv.astype(BF16)
                if c > 0:
                    s_prev = st_ref[c - 1, h]
                else:
                    s_prev = jnp.where(step < last, stp_ref[0, h], 0.0)
                ddecay_cols.append(jnp.sum(g * s_prev, axis=0, keepdims=True))
            dkd = jnp.concatenate(dkd_cols, axis=1)
            ddecay = jnp.concatenate(ddecay_cols, axis=1)
            dp_ref[rows, K0:K0 + KEY_DIM] = (dkd * fades[c]).astype(BF16)
            e = dkd * kds[c]
            dla = ddecay * decays[c] + _cumsum_rows(tri_strict, e)
            dlogit_rows.append(dla * (1.0 / GATE_NORMALIZER) * (1.0 - _sigmoid(logits[rows])))
        dlogit = jnp.concatenate(dlogit_rows, axis=0)
        dlogit16 = dlogit.astype(BF16)
        dp_ref[:, G0:G0 + GATE_PAD] = lax.dot_general(
            dlogit16, wgu_ref[...], (NT, ((), ())), preferred_element_type=F32).astype(BF16)
        dwgu = lax.dot_general(gl, dlogit16, (TN, ((), ())), preferred_element_type=F32)
        db = jnp.sum(dlogit, axis=0, keepdims=True)

        @pl.when(step == 0)
        def _():
            dwgu_ref[...] = dwgu
            db_ref[...] = db
            dgn_ref[...] = dgn

        @pl.when(step > 0)
        def _():
            dwgu_ref[...] += dwgu
            db_ref[...] += db
            dgn_ref[...] += dgn

    rev = lambda i: (last - i, 0)
    rows = _spec((GLA_ROWS, VALUE_DIM), rev)
    const = lambda shape: _spec(shape, lambda i: (0,) * len(shape))
    st_shape = (GLA_HEADS, HEAD_V, HEAD_K)
    return pl.pallas_call(
        body, name="gla_bwd", grid=(N_GROUPS,),
        in_specs=[_spec((GLA_ROWS, PROJ_A_PAD), rev), const((GATE_PAD, KEY_DIM)), const((1, KEY_DIM)), const((1, VALUE_DIM)),
                  rows, _spec((GLA_GROUP,) + st_shape, lambda i: (last - i, 0, 0, 0)),
                  _spec((1,) + st_shape, lambda i: (jnp.maximum((last - i) * GLA_GROUP - 1, 0), 0, 0, 0)), rows],
        out_specs=[_spec((GLA_ROWS, PROJ_A_PAD), rev), const((GATE_PAD, KEY_DIM)), const((1, KEY_DIM)), const((1, VALUE_DIM))],
        out_shape=[jax.ShapeDtypeStruct((SEQ, PROJ_A_PAD), BF16), jax.ShapeDtypeStruct((GATE_PAD, KEY_DIM), F32),
                   jax.ShapeDtypeStruct((1, KEY_DIM), F32), jax.ShapeDtypeStruct((1, VALUE_DIM), F32)],
        scratch_shapes=[pltpu.VMEM(st_shape, F32), pltpu.VMEM((GLA_ROWS, VALUE_DIM), F32), pltpu.VMEM((GLA_GROUP,) + st_shape, F32)],
        compiler_params=_params(("arbitrary",)),
    )(proj, wgu, b_gate, gn, o, states, states, dog)


WGRAD_FF_TILE = D_FF // 2


CARRY_ROWS = 8
UP_ROWS = 512


def _ffn_up_mid(name, x, gamma, w_up_t, conv_w):
    def body(x_ref, g_ref, w_ref, c_ref, h_ref, gu_ref, a_ref, carry):
        @pl.when(pl.program_id(0) == 0)
        def _():
            carry[...] = jnp.zeros_like(carry)

        x_tile = x_ref[...]
        h_tile = (x_tile * _rstd(x_tile) * g_ref[...]).astype(BF16)
        h_ref[...] = h_tile
        for k in range(N_FF_COLS):
            cols = slice(k * FF_COLS, (k + 1) * FF_COLS)
            g, u = (lax.dot_general(h_tile, w_ref[p, cols, :], (NT, ((), ())), preferred_element_type=F32).astype(BF16)
                    for p in range(2))
            gu_ref[0, :, cols] = g
            gu_ref[1, :, cols] = u
            g = g.astype(F32)
            w = c_ref[:, cols]
            before = carry[:, cols]
            gc = w[2:3, :] * g + w[1:2, :] * _shift_down(g, before, 1) + w[0:1, :] * _shift_down(g, before, 2)
            a_ref[:, cols] = (gc * _sigmoid(gc) * u.astype(F32)).astype(BF16)
            carry[:, cols] = g[UP_ROWS - CARRY_ROWS:, :]

    row = _spec((UP_ROWS, D_MODEL), lambda i: (i, 0))
    return pl.pallas_call(
        body, name=name, grid=(SEQ // UP_ROWS,),
        in_specs=[row, _resident((1, D_MODEL)), _resident((2, D_FF, D_MODEL)), _resident((3, D_FF))],
        out_specs=[row, _spec((2, UP_ROWS, D_FF), lambda i: (0, i, 0)), _spec((UP_ROWS, D_FF), lambda i: (i, 0))],
        out_shape=[_act(), _act((2, SEQ, D_FF)), _act((SEQ, D_FF))], scratch_shapes=[pltpu.VMEM((CARRY_ROWS, D_FF), F32)],
        compiler_params=_params(("arbitrary",)),
    )(x, gamma, w_up_t, conv_w)


def _ffn_fwd(tag, x, gamma, w_up_t, conv_w, w_down):
    h, gu, a = _ffn_up_mid(f"ffn{tag}_up_mid", x, gamma, w_up_t, conv_w)
    return _rows_matmul(f"ffn{tag}_down", a, w_down, NN, x), (h, gu, a)


def _owner_blocks(d, rows=None):
    if rows is not None:
        d = d[:rows]
    return d.reshape((N_DEV, -1) + d.shape[-1:])


def _ffn_bwd(tag, x, gamma, w_up_t, conv_w, w_down, saved, dx, dx16, swap):
    h, gu, a = saved
    da = _rows_matmul(f"ffn{tag}_da", dx16, w_down, NT)
    d_w_down = _owner_blocks(_wgrad_cols_tn(f"ffn{tag}_dwdown", a, WGRAD_FF_TILE, dx16))
    dgu, d_conv = _ffn_mid_bwd(f"ffn{tag}_mid_bwd", gu, conv_w, da)
    d_w_up_t = _owner_blocks(_wgrad_halves_tn(f"ffn{tag}_dwup", dgu, WGRAD_FF_TILE, h))
    parts = (d_w_up_t, d_w_down)
    dx, dx16, d_gamma, *received = _sum_blocks_nn(
        f"ffn{tag}_dh", dgu, w_up_t, norm=(x, gamma, dx), swap=parts if swap else ())
    return dx, dx16, d_gamma, d_conv, parts, received


def _local_step(x, target, w, fetch=None, emit=None):
    if fetch is None:
        local = dict(a=(w.get("a_w_in"), w.get("a_w_out")), b=(w.get("b_w_in"), w.get("b_w_out")))
        for layer in range(2):
            local[f"f{layer}"] = (w["f_w_up"][layer], w["f_w_down"][layer]) if "f_w_up" in w else None
        fetch = lambda group, after: local[group]
    swap = emit is not None
    if emit is None:
        emit = lambda group, parts, received, dx: dx
    f_norm = (w["f_norm"][0:1], w["f_norm"][1:2])

    x0 = x
    a_w_in, a_w_out = fetch("a", x0)
    h0, proj = _norm_proj("a_in", x0, w["a_norm"], a_w_in)
    o, og, states = _gla_fwd(proj, w["a_w_gate_up"], w["a_b_gate"], w["a_gn"])
    x1 = _rows_matmul("a_out", og, a_w_out, NN, x0)
    up0, down0 = fetch("f0", x1)
    x2, ffn0 = _ffn_fwd(0, x1, f_norm[0], up0, w["f_conv"][0], down0)
    b_w_in, b_w_out = fetch("b", x2)
    h2, p = _norm_proj("b_in", x2, w["b_norm"], b_w_in)
    y = _sc_mid_fwd(p, w["b_conv"])
    x3 = _rows_matmul("b_out", y, b_w_out, NN, x2)
    up1, down1 = fetch("f1", x3)
    ffn1 = _ffn_up_mid("ffn1_up_mid", x3, f_norm[1], up1, w["f_conv"][1])
    loss, dx, dx16, d_final_norm = _down_loss_head(ffn1[2], down1, x3, w["final_norm"], target)

    dx, dx16, d_f_norm1, d_fconv1, parts_f1, got = _ffn_bwd(
        1, x3, f_norm[1], up1, w["f_conv"][1], down1, ffn1, dx, dx16, swap)
    dx16 = emit("f1", parts_f1, got, dx16)

    dy = _rows_matmul("b_dy", dx16, b_w_out, NT)
    d_b_w_out = _owner_blocks(_wgrad_cols_tn("b_dwout", y, OUT_TILE, dx16))
    db, dc, dhh, d_b_conv = _sc_mid_bwd(p, w["b_conv"], dy)
    dp = jnp.concatenate([db, dc, dhh], axis=1)
    parts_b = (_wgrad_cols_transposed_tn("b_dwin", h2, dp, B_SHARD), d_b_w_out)
    dx, dx16, d_b_norm, *got = _sum_cols_nt("b_dh", dp, b_w_in, norm=(x2, w["b_norm"], dx), swap=parts_b if swap else ())
    dx16 = emit("b", parts_b, got, dx16)

    dx, dx16, d_f_norm0, d_fconv0, parts_f0, got = _ffn_bwd(
        0, x1, f_norm[0], up0, w["f_conv"][0], down0, ffn0, dx, dx16, swap)
    dx16 = emit("f0", parts_f0, got, dx16)

    dog = _rows_matmul("a_dog", dx16, a_w_out, NT)
    d_a_w_out = _owner_blocks(_wgrad_cols_tn("a_dwout", og, OUT_TILE, dx16))
    dproj, d_wgu, d_b_gate, d_gn = _gla_bwd(proj, w["a_w_gate_up"], w["a_b_gate"], w["a_gn"], o, states, dog)
    parts_a = (_owner_blocks(_wgrad_cols_tn("a_dwin", dproj, PA_TILE, h0), PROJ_A), d_a_w_out)
    dx, _, d_a_norm, *got = _wide_nn("a_dh", dproj, a_w_in, norm=(x0, w["a_norm"], dx), swap=parts_a if swap else ())
    emit("a", parts_a, got, dx)

    grads = dict(
        a_norm=d_a_norm, a_w_in=parts_a[0], a_w_gate_up=d_wgu, a_b_gate=d_b_gate, a_gn=d_gn, a_w_out=parts_a[1],
        b_norm=d_b_norm, b_w_in=parts_b[0], b_conv=d_b_conv, b_w_out=parts_b[1],
        f_norm=(d_f_norm0, d_f_norm1), f_w_up=(parts_f0[0], parts_f1[0]), f_conv=(d_fconv0, d_fconv1),
        f_w_down=(parts_f0[1], parts_f1[1]), final_norm=d_final_norm)
    grads["loss"] = loss
    return dx, grads


MESH_ID = pl.DeviceIdType.MESH
ANY = pl.BlockSpec(memory_space=pl.ANY)
N_PEERS = N_DEV - 1


def _position():
    return lax.axis_index("x"), lax.axis_index("y"), lax.axis_index("c")


def _slot(px, py, pc):
    return 4 * px + 2 * py + pc


GATHER_COPIES = 8
HALF_ROWS = 16


def _gather_copies(src, out, send_sems, recv_sems, local_sems):
    n = len(src)
    to_sibling, to_x, to_y, x_on_to_y, y_on_to_x, x_to_sibling, y_to_sibling, diagonal_to_sibling = range(GATHER_COPIES)
    x, y, c = _position()
    me, sibling = (x, y, c), (x, y, 1 - c)
    x_side, y_side, diagonal = (1 - x, y), (x, 1 - y), (1 - x, 1 - y)

    def rows_of(t, half):
        rows = src[t].shape[0]
        half_rows = rows // 2 // HALF_ROWS * HALF_ROWS
        return (pl.ds(0, rows), pl.ds(0, half_rows), pl.ds(half_rows, rows - half_rows))[half]

    def copy(t, j, block, to, half=0, from_input=False):
        dst = out[t].at[_slot(*block), rows_of(t, half)]
        return pltpu.make_async_remote_copy(
            src_ref=src[t] if from_input else dst, dst_ref=dst, send_sem=send_sems.at[GATHER_COPIES * t + j],
            recv_sem=recv_sems.at[GATHER_COPIES * t + j], device_id=to, device_id_type=MESH_ID)

    mine = [pltpu.make_async_copy(src[t], out[t].at[_slot(*me)], local_sems.at[t]) for t in range(n)]
    for cp in mine:
        cp.start()
    sent = []

    def start(cp):
        cp.start()
        sent.append(cp)

    for t in range(n):
        start(copy(t, to_sibling, me, sibling, from_input=True))
        start(copy(t, to_x, me, (*x_side, c), from_input=True))
        start(copy(t, to_y, me, (*y_side, c), from_input=True))
    for t in range(n):
        copy(t, to_x, (*x_side, c), me).wait_recv()
        start(copy(t, x_on_to_y, (*x_side, c), (*y_side, c), half=1))
        start(copy(t, x_to_sibling, (*x_side, c), sibling))
        copy(t, to_y, (*y_side, c), me).wait_recv()
        start(copy(t, y_on_to_x, (*y_side, c), (*x_side, c), half=2))
        start(copy(t, y_to_sibling, (*y_side, c), sibling))
    for t in range(n):
        copy(t, x_on_to_y, (*diagonal, c), me, half=1).wait_recv()
        copy(t, y_on_to_x, (*diagonal, c), me, half=2).wait_recv()
        start(copy(t, diagonal_to_sibling, (*diagonal, c), sibling))
    for t in range(n):
        copy(t, to_sibling, sibling, me).wait_recv()
        for j, chip in ((x_to_sibling, x_side), (y_to_sibling, y_side), (diagonal_to_sibling, diagonal)):
            copy(t, j, (*chip, 1 - c), me).wait_recv()
    for cp in sent:
        cp.wait_send()
    for cp in mine:
        cp.wait()


def _all_gather(name, shards):
    n = len(shards)

    def body(*refs):
        _gather_copies(refs[:n], refs[n:2 * n], *refs[2 * n:])

    sems = pltpu.SemaphoreType.DMA((GATHER_COPIES * n,))
    return pl.pallas_call(
        body, name=name, in_specs=[ANY] * n, out_specs=[ANY] * n,
        out_shape=[jax.ShapeDtypeStruct((N_DEV,) + s.shape, s.dtype) for s in shards],
        scratch_shapes=[sems, sems, pltpu.SemaphoreType.DMA((n,))],
    )(*shards)


SIBLING_AND_NEIGHBOURS = (1, 2, 4)
SAME_CORE = (2, 4, 6)


def _flip(x, y, c, k):
    return x ^ (k >> 2), y ^ ((k >> 1) & 1), c ^ (k & 1)


N_CHIPS = N_DEV // 2


def _chip(px, py):
    return 2 * px + py


def _pair_copies(parts, received, send_sems, recv_sems):
    x, y, c = lax.axis_index("x"), lax.axis_index("y"), lax.axis_index("c")
    sibling = (x, y, 1 - c)
    copies = []
    for t in range(len(parts)):
        for q in range(N_DEV // 2):
            send = pltpu.make_async_remote_copy(
                src_ref=parts[t].at[2 * q + 1 - c], dst_ref=received[t].at[q], send_sem=send_sems.at[t, q],
                recv_sem=recv_sems.at[t, q], device_id=sibling, device_id_type=pl.DeviceIdType.MESH)
            landed = received[t].at[q]
            arrival = pltpu.make_async_remote_copy(
                src_ref=landed, dst_ref=landed, send_sem=send_sems.at[t, q], recv_sem=recv_sems.at[t, q],
                device_id=sibling, device_id_type=pl.DeviceIdType.MESH)
            copies.append((send, arrival))
    return copies


def _pair_add(name, parts, received, side):
    n = len(parts)

    def body(side_ref, *refs):
        for t in range(n):
            refs[2 * n + t][...] = (refs[t][...].astype(F32) + refs[n + t][...].astype(F32)).astype(BF16)

    own = [_spec((None,) + p.shape[1:], lambda q, side_ref: (2 * q + side_ref[0], 0, 0)) for p in parts]
    chip = [_spec((None,) + p.shape[1:], lambda q, side_ref: (q, 0, 0)) for p in parts]
    return pl.pallas_call(
        body, name=name,
        grid_spec=pltpu.PrefetchScalarGridSpec(num_scalar_prefetch=1, grid=(N_CHIPS,), in_specs=own + chip, out_specs=chip),
        out_shape=[jax.ShapeDtypeStruct((N_CHIPS,) + p.shape[1:], BF16) for p in parts], compiler_params=_params(("parallel",)),
    )(side, *parts, *received)


def _send_copy(parts, landing, send_sems, recv_sems, t, s, k):
    x, y, c = _position()
    px, py, _ = _flip(x, y, c, k)
    return pltpu.make_async_remote_copy(
        src_ref=parts[t].at[_chip(px, py)], dst_ref=landing[t].at[_chip(x, y)], send_sem=send_sems.at[s],
        recv_sem=recv_sems.at[s], device_id=(px, py, c), device_id_type=MESH_ID)


def _send_arrival(landing, send_sems, recv_sems, t, s, k):
    x, y, c = _position()
    px, py, _ = _flip(x, y, c, k)
    landed = landing[t].at[_chip(px, py)]
    return pltpu.make_async_remote_copy(
        src_ref=landed, dst_ref=landed, send_sem=send_sems.at[s], recv_sem=recv_sems.at[s],
        device_id=(px, py, c), device_id_type=MESH_ID)


def _handshake(peers):
    x, y, c = _position()
    barrier = pltpu.get_barrier_semaphore()
    for k in peers:
        pl.semaphore_signal(barrier, inc=1, device_id=_flip(x, y, c, k), device_id_type=MESH_ID)
    pl.semaphore_wait(barrier, len(peers))


def _sequencer(name, collective_id, n_copies, body, operands, out_type):
    n_arrays = len(operands)
    return pl.kernel(
        body, out_type=out_type, mesh=plsc.ScalarSubcoreMesh(axis_name="sequencer", num_cores=1), name=name,
        scratch_types=(pltpu.SemaphoreType.DMA((n_copies,)), pltpu.SemaphoreType.DMA((n_copies,)),
                       pltpu.SemaphoreType.DMA((n_arrays,))),
        compiler_params=pltpu.CompilerParams(collective_id=collective_id))(*operands)


def _sequencer_exchange(name, collective_id, parts, after=()):
    n, n_peers, n_in = len(parts), len(SAME_CORE), len(parts) + len(after)

    def body(*refs):
        src, landing = refs[:n], refs[n_in:n_in + n]
        send_sems, recv_sems, local_sems = refs[n_in + n:]
        _handshake(SAME_CORE)
        x, y, _ = _position()
        mine = [pltpu.make_async_copy(src[t].at[_chip(x, y)], landing[t].at[_chip(x, y)], local_sems.at[t]) for t in range(n)]
        for cp in mine:
            cp.start()
        sent = [_send_copy(src, landing, send_sems, recv_sems, t, t * n_peers + j, k)
                for t in range(n) for j, k in enumerate(SAME_CORE)]
        for cp in sent:
            cp.start()
        for t in range(n):
            for j, k in enumerate(SAME_CORE):
                _send_arrival(landing, send_sems, recv_sems, t, t * n_peers + j, k).wait_recv()
        for cp in sent:
            cp.wait_send()
        for cp in mine:
            cp.wait()

    landing = [jax.ShapeDtypeStruct(p.shape, p.dtype) for p in parts]
    return _sequencer(name, collective_id, n * n_peers, body, list(parts) + list(after), landing)


def _sequencer_gather(name, collective_id, shards):
    n = len(shards)

    def body(*refs):
        _handshake(SIBLING_AND_NEIGHBOURS)
        _gather_copies(refs[:n], refs[n:2 * n], *refs[2 * n:])

    gathered = [jax.ShapeDtypeStruct((N_DEV,) + s.shape, s.dtype) for s in shards]
    return _sequencer(name, collective_id, GATHER_COPIES * n, body, shards, gathered)


ADAM_ROWS = 512
BF16_ROWS = 16


def _adam_update(w, g, m, v):
    m = ADAM_B1 * m + (1.0 - ADAM_B1) * g
    v = ADAM_B2 * v + (1.0 - ADAM_B2) * (g * g)
    m_hat = m / (1.0 - ADAM_B1 ** ADAM_STEP)
    v_hat = v / (1.0 - ADAM_B2 ** ADAM_STEP)
    delta = -ADAM_LR * (m_hat / (jnp.sqrt(v_hat) + ADAM_EPS) + ADAM_WD * w)
    return delta, m, v


def _sum_slots(ref):
    total = ref[0].astype(F32)
    for d in range(1, ref.shape[0]):
        total = total + ref[d].astype(F32)
    return total


def _adamw_sum(name, landed, w, m, v):
    layers, rows, cols = w.shape
    tiles = [t for t in range(ADAM_ROWS, 0, -BF16_ROWS) if rows % t == 0]
    tr = tiles[0] if tiles else rows
    nt = rows // tr

    def body(*refs):
        parts = refs[:layers]
        w_ref, m_ref, v_ref, g_ref, d_ref, nm_ref, nv_ref = refs[layers:]
        layer = pl.program_id(0)
        g = _sum_slots(parts[0])
        for q in range(1, layers):
            g = jnp.where(layer == q, _sum_slots(parts[q]), g)
        delta, new_m, new_v = _adam_update(w_ref[...], g, m_ref[...], v_ref[...])
        g_ref[...] = g
        d_ref[...] = delta
        nm_ref[...] = new_m
        nv_ref[...] = new_v

    def part_spec(q):
        return _spec((N_CHIPS, tr, cols), lambda l, i: (0, jnp.where(l == q, i, jnp.where(l < q, 0, nt - 1)), 0))

    tile = _spec((None, tr, cols), lambda l, i: (l, i, 0))
    out = jax.ShapeDtypeStruct((layers, rows, cols), F32)
    return pl.pallas_call(
        body, name=name, grid=(layers, nt), in_specs=[part_spec(q) for q in range(layers)] + [tile] * 3,
        out_specs=[tile] * 4, out_shape=[out] * 4, compiler_params=_params(("arbitrary", "arbitrary")),
    )(*landed, w, m, v)


def _sum_small(landed):
    def body(in_ref, out_ref):
        out_ref[...] = _sum_slots(in_ref)

    return pl.pallas_call(body, name="small_grad_sum", out_shape=jax.ShapeDtypeStruct(landed.shape[1:], F32))(landed)


def _adamw_small(arrays):
    n = len(arrays)

    def body(*refs):
        for i in range(n):
            g_ref, w_ref, m_ref, v_ref = refs[4 * i:4 * i + 4]
            d_ref, nm_ref, nv_ref = refs[4 * n + 3 * i:4 * n + 3 * i + 3]
            d_ref[...], nm_ref[...], nv_ref[...] = _adam_update(w_ref[...], g_ref[...], m_ref[...], v_ref[...])

    out = [jax.ShapeDtypeStruct(w.shape, F32) for _, w, _, _ in arrays for _ in range(3)]
    flat = pl.pallas_call(body, name="adam_small", out_shape=out)(*[a for group in arrays for a in group])
    return [tuple(flat[3 * i:3 * i + 3]) for i in range(n)]


LANES = 128
SUBLANES = 8
F_CONV_SHARD = D_FF // N_DEV
GATE_SHARD = KEY_DIM // N_DEV
NORM_SHARD = D_MODEL // N_DEV


def _tile_rows(a):
    flat = a.reshape(-1)
    size = -(-flat.shape[0] // (SUBLANES * LANES)) * SUBLANES * LANES
    return jnp.pad(flat, (0, size - flat.shape[0])).reshape(-1, LANES)


def _pack_rows(pieces):
    return jnp.concatenate([_tile_rows(p) for p in pieces], axis=0)


def _unpack_rows(packed, shapes):
    out, row = [], 0
    for shape in shapes:
        size = 1
        for s in shape:
            size *= s
        rows = -(-size // (SUBLANES * LANES)) * SUBLANES
        piece = packed[..., row:row + rows, :]
        out.append(piece.reshape(piece.shape[:-2] + (rows * LANES,))[..., :size])
        row += rows
    return out


SMALL_SHARDS = ((GATE_RANK, GATE_SHARD), (1, NORM_SHARD), (3, NORM_SHARD), (2, 3, F_CONV_SHARD))


def _unpack_small_shards(g):
    gate, b_norm, b_conv, f_conv = _unpack_rows(g, SMALL_SHARDS)
    gate = gate.reshape(N_DEV, GATE_RANK, GATE_SHARD).transpose(1, 0, 2).reshape(GATE_RANK, KEY_DIM)
    b_norm = b_norm.reshape(1, D_MODEL)
    b_conv = b_conv.reshape(N_DEV, 3, NORM_SHARD).transpose(1, 0, 2).reshape(3, D_MODEL)
    f_conv = f_conv.reshape(N_DEV, 2, 3, F_CONV_SHARD).transpose(1, 2, 0, 3).reshape(2, 3, D_FF)
    return gate, b_norm, b_conv, f_conv


SMALL_LAYOUT = (("a_norm", (1, D_MODEL)), ("a_w_gate_up", (GATE_RANK, KEY_DIM)), ("a_b_gate", (1, KEY_DIM)), ("a_gn", (1, VALUE_DIM)),
                ("b_norm", (1, D_MODEL)), ("b_conv", (3, D_MODEL)), ("f_norm0", (1, D_MODEL)), ("f_norm1", (1, D_MODEL)),
                ("f_conv0", (3, D_FF)), ("f_conv1", (3, D_FF)), ("final_norm", (1, D_MODEL)), ("loss", (1, LANES)))


def _pack_small_grads(g):
    full = dict(g)
    full["a_w_gate_up"] = g["a_w_gate_up"][:GATE_RANK]
    for layer in range(2):
        full[f"f_norm{layer}"] = g["f_norm"][layer]
        full[f"f_conv{layer}"] = g["f_conv"][layer]
    return _pack_rows([full[name] for name, _ in SMALL_LAYOUT])


def _unpack_small_grads(packed):
    pieces = _unpack_rows(packed, [shape for _, shape in SMALL_LAYOUT])
    out = {name: piece.reshape(shape) for (name, shape), piece in zip(SMALL_LAYOUT, pieces)}
    out["f_norm"] = jnp.stack([out["f_norm0"][0], out["f_norm1"][0]])
    out["f_conv"] = jnp.stack([out["f_conv0"], out["f_conv1"]])
    return out


def kernel(x, a_norm, a_w_in, a_w_gate_up, a_b_gate, a_gn, a_w_out, b_norm, b_w_in, b_conv, b_w_out, f_norm, f_w_up, f_conv, f_w_down, final_norm, loss_target, m_a_norm, m_a_w_in, m_a_w_gate_up, m_a_b_gate, m_a_gn, m_a_w_out, m_b_norm, m_b_w_in, m_b_conv, m_b_w_out, m_f_norm, m_f_w_up, m_f_conv, m_f_w_down, m_final_norm, v_a_norm, v_a_w_in, v_a_w_gate_up, v_a_b_gate, v_a_gn, v_a_w_out, v_b_norm, v_b_w_in, v_b_conv, v_b_w_out, v_f_norm, v_f_w_up, v_f_conv, v_f_w_down, v_final_norm):
    my_slot = _slot(*_position())

    transposed = lambda w: jnp.swapaxes(w, 1, 2)
    a_transposed = lambda w: w.reshape(D_MODEL, A_SHARD).T.reshape(1, A_SHARD, D_MODEL)
    a_w_in_t, f_w_up_t = a_transposed(a_w_in), transposed(f_w_up)
    first = _all_gather("weight_gather", [a_w_in_t[0].astype(BF16), a_w_out[0].astype(BF16),
                                          _pack_rows([a_w_gate_up[0], b_norm, b_conv[0], f_conv])])
    gathers, small_shards = {}, first[2]
    later = (("f0", f_w_up_t[0], f_w_down[0]), ("b", b_w_in[0], b_w_out[0]), ("f1", f_w_up_t[1], f_w_down[1]))
    for collective_id, (group, w_in, w_out) in enumerate(later):
        w_in, w_out, small_shards = lax.optimization_barrier((w_in.astype(BF16), w_out.astype(BF16), small_shards))
        gathers[group] = _sequencer_gather(f"gather_{group}", collective_id, [w_in, w_out])
    gate_full, b_norm_full, b_conv_full, f_conv_full = _unpack_small_shards(small_shards)
    a_w_in_full = jnp.pad(first[0].reshape(PROJ_A, D_MODEL), ((0, PROJ_A_PAD - PROJ_A), (0, 0)))
    weights = dict(
        a_norm=a_norm, a_w_gate_up=jnp.pad(gate_full, ((0, GATE_PAD - GATE_RANK), (0, 0))).astype(BF16), a_b_gate=a_b_gate,
        a_gn=a_gn, b_norm=b_norm_full, b_conv=b_conv_full, f_norm=f_norm, f_conv=f_conv_full,
        final_norm=final_norm.reshape(1, D_MODEL))

    def fetch(group, after):
        if group == "a":
            return a_w_in_full, first[1].reshape(D_MODEL, D_MODEL)
        w_in, w_out = gathers[group]
        if group == "b":
            return w_in, w_out.reshape(D_MODEL, D_MODEL)
        return w_in.reshape(2, D_FF, D_MODEL), w_out.reshape(D_FF, D_MODEL)

    exchanges, pending = {}, []
    exchange_ids = dict(b=3, f0=4, a=5)
    side = lax.axis_index("c").astype(jnp.int32).reshape(1)

    def emit(group, parts, received, carry):
        sums = _pair_add(f"pair_add_{group}", parts, received, side)
        carry, *sums = lax.optimization_barrier((carry, *sums))
        pending.extend(sums)
        if group != "f1":
            after = list(exchanges.values())[-1][:1] if exchanges else ()
            exchanges[group] = _sequencer_exchange(f"grads_{group}", exchange_ids[group], list(pending), after)
            pending.clear()
        return carry

    dx, g = _local_step(x[0], loss_target[0], weights, fetch, emit)

    (up1, down1, d_b_in, d_b_out), (up0, down0), (d_a_in, d_a_out) = (exchanges[group] for group in ("b", "f0", "a"))
    back = lambda results: tuple(transposed(r) for r in results)
    big = dict(
        b_w_in=_adamw_sum("adam_b_w_in", [d_b_in], b_w_in, m_b_w_in, v_b_w_in),
        b_w_out=_adamw_sum("adam_b_w_out", [d_b_out], b_w_out, m_b_w_out, v_b_w_out),
        f_w_up=back(_adamw_sum("adam_f_w_up", [up0, up1], f_w_up_t, transposed(m_f_w_up), transposed(v_f_w_up))),
        f_w_down=_adamw_sum("adam_f_w_down", [down0, down1], f_w_down, m_f_w_down, v_f_w_down))
    small_packed, *updated = lax.optimization_barrier((_pack_small_grads(g), *big["f_w_down"]))
    big["f_w_down"] = tuple(updated)
    small_landed = _all_gather("small_grad_gather", [small_packed])[0]
    big.update(
        a_w_in=tuple(r.reshape(A_SHARD, D_MODEL).T.reshape(1, D_MODEL, A_SHARD) for r in _adamw_sum(
            "adam_a_w_in", [d_a_in], a_w_in_t, a_transposed(m_a_w_in), a_transposed(v_a_w_in))),
        a_w_out=_adamw_sum("adam_a_w_out", [d_a_out], a_w_out, m_a_w_out, v_a_w_out))
    small_g = _unpack_small_grads(_sum_small(small_landed))
    loss = small_g["loss"][0, 0]
    small_g["a_w_gate_up"] = lax.dynamic_slice_in_dim(small_g["a_w_gate_up"], my_slot * GATE_SHARD, GATE_SHARD, axis=1)
    small_g["b_norm"] = lax.dynamic_slice_in_dim(small_g["b_norm"], my_slot * NORM_SHARD, NORM_SHARD, axis=1)
    small_g["b_conv"] = lax.dynamic_slice_in_dim(small_g["b_conv"], my_slot * NORM_SHARD, NORM_SHARD, axis=1)
    small_g["f_conv"] = lax.dynamic_slice_in_dim(small_g["f_conv"], my_slot * F_CONV_SHARD, F_CONV_SHARD, axis=2)
    small_w = dict(
        a_norm=(a_norm, m_a_norm, v_a_norm), a_w_gate_up=(a_w_gate_up, m_a_w_gate_up, v_a_w_gate_up),
        a_b_gate=(a_b_gate, m_a_b_gate, v_a_b_gate), a_gn=(a_gn, m_a_gn, v_a_gn), b_norm=(b_norm, m_b_norm, v_b_norm),
        b_conv=(b_conv, m_b_conv, v_b_conv), f_norm=(f_norm, m_f_norm, v_f_norm), f_conv=(f_conv, m_f_conv, v_f_conv),
        final_norm=(final_norm, m_final_norm, v_final_norm))
    two_d = lambda a: a.reshape(-1, a.shape[-1])
    updates = _adamw_small([tuple(two_d(a.reshape(w.shape)) for a in (small_g[name], w, m, v)) for name, (w, m, v) in small_w.items()])
    small = {}
    for (name, (w, _, _)), update in zip(small_w.items(), updates):
        small[name] = (small_g[name].reshape(w.shape),) + tuple(u.reshape(w.shape) for u in update)

    order = ["a_norm", "a_w_in", "a_w_gate_up", "a_b_gate", "a_gn", "a_w_out", "b_norm", "b_w_in", "b_conv", "b_w_out",
             "f_norm", "f_w_up", "f_conv", "f_w_down", "final_norm"]
    results = {**big, **small}
    outputs = [loss, dx.reshape(1, SEQ, D_MODEL)]
    for kind in range(4):
        outputs += [results[name][kind] for name in order]
    return tuple(outputs)
```

```python
import jax
import jax.numpy as jnp
from jax import lax
from jax.experimental import pallas as pl
from jax.experimental.pallas import tpu as pltpu
from jax.experimental.pallas import tpu_sc as plsc

F32 = jnp.float32
BF16 = jnp.bfloat16

N_DEV = 8
SEQ = 2048
D_MODEL = 1024
CHUNK = 64
N_CHUNKS = SEQ // CHUNK
RMS_EPS = 1e-6
GLA_HEADS = 4
KEY_DIM = 512
VALUE_DIM = 1024
HEAD_K = KEY_DIM // GLA_HEADS
HEAD_V = VALUE_DIM // GLA_HEADS
GATE_RANK = 16
GATE_PAD = 128
GATE_NORMALIZER = 16.0
PROJ_A = 2 * KEY_DIM + 2 * VALUE_DIM + GATE_RANK
PROJ_A_PAD = 2 * KEY_DIM + 2 * VALUE_DIM + GATE_PAD
A_SHARD = PROJ_A // N_DEV
B_SHARD = 3 * D_MODEL // N_DEV
D_FF = 2816
ADAM_LR = 0.001
ADAM_B1 = 0.9
ADAM_B2 = 0.999
ADAM_EPS = 1e-08
ADAM_WD = 0.01
ADAM_STEP = 10
MESH_AXES = ("x", "y", "c")

VMEM_LIMIT = 56 * 1024 * 1024
ROW_CHUNK = 256
HALO = 16


def _params(sem=None, vmem=VMEM_LIMIT):
    return pltpu.CompilerParams(dimension_semantics=sem, vmem_limit_bytes=vmem)


SWAP_IDS = {"ffn1_dh": 6, "b_dh": 7, "ffn0_dh": 8, "a_dh": 9}
NORM_PARTS = 2
NN = ((1,), (0,))
NT = ((1,), (1,))
TN = ((0,), (0,))


def _matmul(name, a, a_spec, b, b_spec, dims, grid, out_shape, out_spec, k_blocks=None, a_block_cols=None, res=None,
            res_spec=None, transpose_out=False, norm=None, swap=()):
    has_res = res is not None
    n_swap = len(swap)

    def body(*refs):
        a_ref, b_ref = refs[0], refs[1]
        r_ref = refs[2] if has_res else None

        def product(lhs, rhs):
            return lax.dot_general(lhs.astype(BF16), rhs, (dims, ((), ())), preferred_element_type=F32)

        def tile(rows=slice(None)):
            if k_blocks is None:
                return product(a_ref[rows, :] if norm is not None else a_ref[...], b_ref[...])
            v = None
            for k in range(k_blocks):
                lhs = a_ref[k, rows, :] if a_block_cols is None else a_ref[rows, k * a_block_cols:(k + 1) * a_block_cols]
                p = product(lhs, b_ref[k])
                v = p if v is None else v + p
            return v

        if norm is None:
            v = tile()
            if transpose_out:
                v = v.T
            if has_res:
                v = v + r_ref[...]
            o_ref = refs[2 + has_res]
            o_ref[...] = v.astype(o_ref.dtype)
            return
        n_in = 5 + has_res
        x_ref, g_ref, dxi_ref = refs[2 + has_res:n_in]
        dx_ref, dx16_ref, dg_ref = refs[n_in + n_swap:n_in + n_swap + 3]
        if n_swap:
            copies = _pair_copies(refs[n_in:n_in + n_swap], refs[n_in + n_swap + 3:n_in + 2 * n_swap + 3], *refs[-2:])

            @pl.when(pl.program_id(0) == 0)
            def _():
                sibling = (lax.axis_index("x"), lax.axis_index("y"), 1 - lax.axis_index("c"))
                barrier = pltpu.get_barrier_semaphore()
                pl.semaphore_signal(barrier, inc=1, device_id=sibling, device_id_type=pl.DeviceIdType.MESH)
                pl.semaphore_wait(barrier, 1)
                for send, _ in copies:
                    send.start()

            @pl.when(pl.program_id(0) == grid[0] - 1)
            def _():
                for send, arrival in copies:
                    arrival.wait_recv()
                    send.wait_send()

        dg = None
        part = dx_ref.shape[0] // NORM_PARTS
        for rows in (slice(i * part, (i + 1) * part) for i in range(NORM_PARTS)):
            dx, dg_rows = _norm_bwd_rows(x_ref[rows, :], g_ref[...], tile(rows))
            dx = dxi_ref[rows, :] + dx
            dx_ref[rows, :] = dx
            dx16_ref[rows, :] = dx.astype(BF16)
            dg = dg_rows if dg is None else dg + dg_rows

        @pl.when(pl.program_id(0) == 0)
        def _():
            dg_ref[...] = dg

        @pl.when(pl.program_id(0) > 0)
        def _():
            dg_ref[...] += dg

    operands = [a, b] + ([res] if has_res else [])
    in_specs = [a_spec, b_spec] + ([res_spec] if has_res else [])
    semantics = ("parallel",) * len(grid)
    scratch = []
    if norm is not None:
        vec = _spec((1, D_MODEL), lambda i: (0, 0))
        any_space = pl.BlockSpec(memory_space=pl.ANY)
        operands += list(norm) + list(swap)
        in_specs += [out_spec, vec, out_spec] + [any_space] * n_swap
        out_shape = [_act(dtype=F32), _act(), jax.ShapeDtypeStruct((1, D_MODEL), F32)]
        out_shape += [jax.ShapeDtypeStruct((N_DEV // 2,) + p.shape[1:], p.dtype) for p in swap]
        out_spec = [out_spec, out_spec, vec] + [any_space] * n_swap
        semantics = ("arbitrary",)
        if n_swap:
            scratch = [pltpu.SemaphoreType.DMA((n_swap, N_DEV // 2))] * 2
    params = _params(semantics)
    if n_swap:
        params = pltpu.CompilerParams(dimension_semantics=semantics, vmem_limit_bytes=VMEM_LIMIT, collective_id=SWAP_IDS[name])
    return pl.pallas_call(
        body, name=name, grid=grid, in_specs=in_specs, out_specs=out_spec, out_shape=out_shape, scratch_shapes=scratch,
        compiler_params=params,
    )(*operands)


def _resident(shape):
    return pl.BlockSpec(shape, lambda *_: (0,) * len(shape), pipeline_mode=pl.Buffered(1))


TM = 512
N_TM = SEQ // TM
PA_TILE = 640
N_PA = PROJ_A_PAD // PA_TILE
OUT_TILE = 256


def _spec(shape, fn):
    return pl.BlockSpec(shape, fn)


def _act(shape=(SEQ, D_MODEL), dtype=BF16):
    return jax.ShapeDtypeStruct(shape, dtype)


def _norm_proj(name, x, gamma, w):
    blocks = w.ndim == 3
    n_out = w.shape[0] * w.shape[2] if blocks else w.shape[0]

    def body(x_ref, g_ref, w_ref, h_ref, o_ref):
        x = x_ref[...]
        h = (x * _rstd(x) * g_ref[...]).astype(BF16)
        h_ref[...] = h
        if blocks:
            n = w.shape[2]
            for j in range(w.shape[0]):
                o_ref[:, j * n:(j + 1) * n] = jnp.dot(h, w_ref[j], preferred_element_type=F32).astype(BF16)
        else:
            o_ref[...] = lax.dot_general(h, w_ref[...], (NT, ((), ())), preferred_element_type=F32).astype(BF16)

    row = _spec((TM, D_MODEL), lambda i: (i, 0))
    return pl.pallas_call(
        body, name=name, grid=(N_TM,), in_specs=[row, _resident((1, D_MODEL)), _resident(w.shape)],
        out_specs=[row, _spec((TM, n_out), lambda i: (i, 0))], out_shape=[_act(), _act((SEQ, n_out))],
        compiler_params=_params(("parallel",)),
    )(x, gamma, w)


def _rows_matmul(name, a, w, dims, x=None):
    k = a.shape[1]
    n = w.shape[1] if dims == NN else w.shape[0]
    row = _spec((TM, n), lambda i: (i, 0))
    return _matmul(name, a, _spec((TM, k), lambda i: (i, 0)), w, _resident(w.shape), dims, (N_TM,),
                   _act((SEQ, n), F32 if x is not None else BF16), row, res=x, res_spec=row if x is not None else None)


def _sum_blocks_nn(name, a_blocks, w_blocks, x=None, norm=None, swap=()):
    nb, _, n = a_blocks.shape
    row = _spec((TM, D_MODEL), lambda i: (i, 0))
    return _matmul(name, a_blocks, _spec((nb, TM, n), lambda i: (0, i, 0)), w_blocks, _resident((nb, n, D_MODEL)),
                   NN, (N_TM,), _act(dtype=F32), row, k_blocks=nb, res=x, res_spec=row if x is not None else None, norm=norm, swap=swap)


def _sum_cols_nt(name, d, w_blocks, norm=None, swap=()):
    nb, _, n = w_blocks.shape
    return _matmul(name, d, _spec((TM, nb * n), lambda i: (i, 0)), w_blocks, _resident((nb, D_MODEL, n)), NT,
                   (N_TM,), _act(dtype=F32), _spec((TM, D_MODEL), lambda i: (i, 0)), k_blocks=nb, a_block_cols=n, norm=norm, swap=swap)


def _wide_nn(name, d, wt, x=None, norm=None, swap=()):
    n = wt.shape[0]
    row = _spec((TM, D_MODEL), lambda i: (i, 0))
    return _matmul(name, d, _spec((TM, n), lambda i: (i, 0)), wt, _resident((n, D_MODEL)), NN, (N_TM,),
                   _act(dtype=F32), row, res=x, res_spec=row if x is not None else None, norm=norm, swap=swap)


def _wgrad_halves_tn(name, d, n_tile, h):
    _, _, n = d.shape
    return _matmul(name, d, _spec((None, SEQ, n_tile), lambda p, j: (p, 0, j)), h, _resident((SEQ, D_MODEL)), TN,
                   (2, n // n_tile), _act((2, n, D_MODEL)), _spec((None, n_tile, D_MODEL), lambda p, j: (p, j, 0)))


def _wgrad_cols_tn(name, d, n_tile, h):
    n = d.shape[1]
    return _matmul(name, d, _spec((SEQ, n_tile), lambda j: (0, j)), h, _resident((SEQ, D_MODEL)), TN,
                   (n // n_tile,), _act((n, D_MODEL)), _spec((n_tile, D_MODEL), lambda j: (j, 0)))


def _wgrad_cols_transposed_tn(name, h, d, n_tile):
    nb = d.shape[1] // n_tile
    return _matmul(name, d, _spec((SEQ, n_tile), lambda j: (0, j)), h, _resident((SEQ, D_MODEL)), TN, (nb,),
                   _act((nb, D_MODEL, n_tile)), _spec((None, D_MODEL, n_tile), lambda j: (j, 0, 0)), transpose_out=True)


NORM_ROWS = 512


def _rstd(x):
    return lax.rsqrt(jnp.mean(x * x, axis=-1, keepdims=True) + RMS_EPS)


def _norm_bwd_rows(x, gamma, dh):
    r = _rstd(x)
    xh = x * r
    dxh = dh * gamma
    dx = r * (dxh - xh * jnp.mean(dxh * xh, axis=-1, keepdims=True))
    return dx, jnp.sum(dh * xh, axis=0, keepdims=True)


def _down_loss_head(a, w_down, x_in, gamma, target):
    def body(a_ref, w_ref, x_ref, g_ref, t_ref, loss_ref, dx_ref, dx16_ref, dg_ref):
        gamma = g_ref[...]
        dg, part = 0.0, 0.0
        rows_per_part = TM // NORM_PARTS
        for rows in (slice(i * rows_per_part, (i + 1) * rows_per_part) for i in range(NORM_PARTS)):
            x = x_ref[rows, :] + jnp.dot(a_ref[rows, :], w_ref[...], preferred_element_type=F32)
            err = x * _rstd(x) * gamma - t_ref[rows, :]
            dy = err * (1.0 / D_MODEL)
            dx, dg_rows = _norm_bwd_rows(x, gamma, dy)
            dx_ref[rows, :] = dx
            dx16_ref[rows, :] = dx.astype(BF16)
            dg = dg + dg_rows
            part = part + 0.5 * jnp.sum(jnp.sum(err * err, axis=-1, keepdims=True) * (1.0 / D_MODEL), axis=0, keepdims=True)
        part = jnp.broadcast_to(part, loss_ref.shape)

        @pl.when(pl.program_id(0) == 0)
        def _():
            dg_ref[...] = dg
            loss_ref[...] = part

        @pl.when(pl.program_id(0) > 0)
        def _():
            dg_ref[...] += dg
            loss_ref[...] += part

    row = _spec((TM, D_MODEL), lambda i: (i, 0))
    vec = _spec((1, D_MODEL), lambda i: (0, 0))
    return pl.pallas_call(
        body, name="ffn1_down_loss_head", grid=(N_TM,),
        in_specs=[_spec((TM, D_FF), lambda i: (i, 0)), _resident((D_FF, D_MODEL)), row, vec, row],
        out_specs=[_spec((1, 128), lambda i: (0, 0)), row, row, vec],
        out_shape=[jax.ShapeDtypeStruct((1, 128), F32), _act(dtype=F32), _act(), jax.ShapeDtypeStruct((1, D_MODEL), F32)],
        compiler_params=_params(("arbitrary",)),
    )(a, w_down, x_in, gamma, target)


def _sigmoid(x):
    return 1.0 / (1.0 + jnp.exp(-x))


def _rows(ref, c):
    return ref[pl.ds(pl.multiple_of(c * ROW_CHUNK, ROW_CHUNK), ROW_CHUNK), :].astype(F32)


def _rows_before(ref, c):
    start = pl.multiple_of(jnp.maximum(c * ROW_CHUNK - HALO, 0), HALO)
    rows = ref[pl.ds(start, HALO), :].astype(F32)
    return jnp.where(c > 0, rows, 0.0)


def _rows_after(ref, c, n_chunks):
    start = pl.multiple_of(jnp.minimum((c + 1) * ROW_CHUNK, SEQ - HALO), HALO)
    rows = ref[pl.ds(start, HALO), :].astype(F32)
    return jnp.where(c < n_chunks - 1, rows, 0.0)


def _shift_down(z, before, n):
    return pltpu.roll(jnp.concatenate([before, z], axis=0), n, 0)[before.shape[0]:]


def _shift_up(z, after, n):
    rows = z.shape[0]
    return pltpu.roll(jnp.concatenate([z, after], axis=0), rows + HALO - n, 0)[:rows]


def _conv_rows(z, before, w):
    z1 = _shift_down(z, before, 1)
    z2 = _shift_down(z, before, 2)
    return w[2:3, :] * z + w[1:2, :] * z1 + w[0:1, :] * z2, z1, z2


def _conv_t_rows(dy, after, w):
    return w[2:3, :] * dy + w[1:2, :] * _shift_up(dy, after, 1) + w[0:1, :] * _shift_up(dy, after, 2)


N_ROW_CHUNKS = SEQ // ROW_CHUNK


FF_COLS = 256
N_FF_COLS = D_FF // FF_COLS


def _ffn_mid_bwd(name, gu, conv_w, da):
    def body(gu_ref, w_ref, da_ref, dgu_ref, dw_ref, dgc_ref):
        w = w_ref[...]

        def first(c, acc):
            g = _rows(gu_ref.at[0], c)
            u = _rows(gu_ref.at[1], c)
            d = _rows(da_ref, c)
            gc, g1, g2 = _conv_rows(g, _rows_before(gu_ref.at[0], c), w)
            sg = _sigmoid(gc)
            rows = pl.ds(pl.multiple_of(c * ROW_CHUNK, ROW_CHUNK), ROW_CHUNK)
            silu = gc * sg
            dgu_ref[1, rows, :] = (d * silu).astype(BF16)
            dgc = d * u * (sg + silu * (1.0 - sg))
            dgc_ref[rows, :] = dgc
            return (acc[0] + jnp.sum(dgc * g2, axis=0, keepdims=True), acc[1] + jnp.sum(dgc * g1, axis=0, keepdims=True),
                    acc[2] + jnp.sum(dgc * g, axis=0, keepdims=True))

        zero = jnp.zeros((1, FF_COLS), F32)
        acc = lax.fori_loop(0, N_ROW_CHUNKS, first, (zero, zero, zero))
        for r in range(3):
            dw_ref[r:r + 1, :] = acc[r]

        def second(c, carry):
            dgc = _rows(dgc_ref, c)
            dg = _conv_t_rows(dgc, _rows_after(dgc_ref, c, N_ROW_CHUNKS), w)
            dgu_ref[0, pl.ds(pl.multiple_of(c * ROW_CHUNK, ROW_CHUNK), ROW_CHUNK), :] = dg.astype(BF16)
            return carry

        lax.fori_loop(0, N_ROW_CHUNKS, second, 0)

    pair = _spec((2, SEQ, FF_COLS), lambda j: (0, 0, j))
    wspec = _spec((3, FF_COLS), lambda j: (0, j))
    return pl.pallas_call(
        body, name=name, grid=(N_FF_COLS,), in_specs=[pair, wspec, _spec((SEQ, FF_COLS), lambda j: (0, j))],
        out_specs=[pair, wspec], out_shape=[_act((2, SEQ, D_FF)), jax.ShapeDtypeStruct((3, D_FF), F32)],
        scratch_shapes=[pltpu.VMEM((SEQ, FF_COLS), F32)],
        compiler_params=_params(("parallel",)),
    )(gu, conv_w, da)


SC_COLS = 256
N_SC = D_MODEL // SC_COLS


def _sc_specs():
    return [_spec((SEQ, SC_COLS), lambda j, part=part: (0, part * N_SC + j)) for part in range(3)]


def _sc_mid_fwd(p, conv_w):
    def body(b_ref, c_ref, h_ref, w_ref, y_ref):
        w = w_ref[...]

        def chunk(c, carry):
            z = _rows(c_ref, c) * _rows(h_ref, c)
            before = _rows_before(c_ref, c) * _rows_before(h_ref, c)
            zc, _, _ = _conv_rows(z, before, w)
            y_ref[pl.ds(pl.multiple_of(c * ROW_CHUNK, ROW_CHUNK), ROW_CHUNK), :] = (_rows(b_ref, c) * zc).astype(BF16)
            return carry

        lax.fori_loop(0, N_ROW_CHUNKS, chunk, 0)

    col = _spec((SEQ, SC_COLS), lambda j: (0, j))
    return pl.pallas_call(
        body, name="sc_mid_fwd", grid=(N_SC,), in_specs=_sc_specs() + [_spec((3, SC_COLS), lambda j: (0, j))], out_specs=col,
        out_shape=jax.ShapeDtypeStruct((SEQ, D_MODEL), BF16), compiler_params=_params(("parallel",)),
    )(p, p, p, conv_w)


def _sc_mid_bwd(p, conv_w, dy):
    def body(b_ref, c_ref, h_ref, w_ref, dy_ref, db_ref, dc_ref, dh_ref, dw_ref, dzc_ref):
        w = w_ref[...]

        def first(c, acc):
            z = _rows(c_ref, c) * _rows(h_ref, c)
            before = _rows_before(c_ref, c) * _rows_before(h_ref, c)
            zc, z1, z2 = _conv_rows(z, before, w)
            d = _rows(dy_ref, c)
            rows = pl.ds(pl.multiple_of(c * ROW_CHUNK, ROW_CHUNK), ROW_CHUNK)
            db_ref[rows, :] = (d * zc).astype(BF16)
            dzc = d * _rows(b_ref, c)
            dzc_ref[rows, :] = dzc
            return (acc[0] + jnp.sum(dzc * z2, axis=0, keepdims=True), acc[1] + jnp.sum(dzc * z1, axis=0, keepdims=True),
                    acc[2] + jnp.sum(dzc * z, axis=0, keepdims=True))

        zero = jnp.zeros((1, SC_COLS), F32)
        acc = lax.fori_loop(0, N_ROW_CHUNKS, first, (zero, zero, zero))
        for r in range(3):
            dw_ref[r:r + 1, :] = acc[r]

        def second(c, carry):
            dz = _conv_t_rows(_rows(dzc_ref, c), _rows_after(dzc_ref, c, N_ROW_CHUNKS), w)
            rows = pl.ds(pl.multiple_of(c * ROW_CHUNK, ROW_CHUNK), ROW_CHUNK)
            dc_ref[rows, :] = (dz * _rows(h_ref, c)).astype(BF16)
            dh_ref[rows, :] = (dz * _rows(c_ref, c)).astype(BF16)
            return carry

        lax.fori_loop(0, N_ROW_CHUNKS, second, 0)

    col = _spec((SEQ, SC_COLS), lambda j: (0, j))
    wspec = _spec((3, SC_COLS), lambda j: (0, j))
    act = jax.ShapeDtypeStruct((SEQ, D_MODEL), BF16)
    return pl.pallas_call(
        body, name="sc_mid_bwd", grid=(N_SC,), in_specs=_sc_specs() + [wspec, col], out_specs=[col, col, col, wspec],
        out_shape=[act, act, act, jax.ShapeDtypeStruct((3, D_MODEL), F32)],
        scratch_shapes=[pltpu.VMEM((SEQ, SC_COLS), F32)], compiler_params=_params(("parallel",)),
    )(p, p, p, conv_w, dy)


GLA_GROUP = 4
GLA_ROWS = GLA_GROUP * CHUNK
N_GROUPS = N_CHUNKS // GLA_GROUP
Q0, K0, V0, R0, G0 = 0, KEY_DIM, 2 * KEY_DIM, 2 * KEY_DIM + VALUE_DIM, 2 * KEY_DIM + 2 * VALUE_DIM


def _tri(strict):
    r = lax.broadcasted_iota(jnp.int32, (CHUNK, CHUNK), 0)
    c = lax.broadcasted_iota(jnp.int32, (CHUNK, CHUNK), 1)
    return jnp.where(c < r if strict else c <= r, 1.0, 0.0).astype(F32)


def _cumsum_rows(tri, x):
    tri = tri.astype(BF16)
    total = None
    for _ in range(3):
        term = x.astype(BF16)
        x = x - term.astype(F32)
        product = jnp.dot(tri, term, preferred_element_type=F32)
        total = product if total is None else total + product
    return total


def _gate_logits(gl, wgu, b_gate):
    return jnp.dot(gl, wgu, preferred_element_type=F32) + b_gate


def _log_decay(logits):
    return (jnp.minimum(logits, 0.0) - jnp.log(1.0 + jnp.exp(-jnp.abs(logits)))) * (1.0 / GATE_NORMALIZER)


def _head(x, h, width):
    return x[:, h * width:(h + 1) * width]


def _gla_fwd(proj, wgu, b_gate, gn):
    def body(p_ref, wgu_ref, b_ref, gn_ref, o_ref, og_ref, st_ref, state):
        @pl.when(pl.program_id(0) == 0)
        def _():
            state[...] = jnp.zeros_like(state)

        tri = _tri(False)
        la = _log_decay(_gate_logits(p_ref[:, G0:G0 + GATE_PAD], wgu_ref[...], b_ref[...]))
        decays = []
        for c in range(GLA_GROUP):
            rows = slice(c * CHUNK, (c + 1) * CHUNK)
            cum = _cumsum_rows(tri, la[rows])
            tot = cum[CHUNK - 1:CHUNK, :]
            kd = (p_ref[rows, K0:K0 + KEY_DIM].astype(F32) * jnp.exp(tot - cum)).astype(BF16)
            decays.append(jnp.exp(tot))
            v = p_ref[rows, V0:V0 + VALUE_DIM]
            for h in range(GLA_HEADS):
                st_ref[c, h] = lax.dot_general(
                    _head(v, h, HEAD_V), _head(kd, h, HEAD_K), (TN, ((), ())), preferred_element_type=F32)
        for c in range(GLA_GROUP):
            for h in range(GLA_HEADS):
                s = state[h] * _head(decays[c], h, HEAD_K) + st_ref[c, h]
                state[h] = s
                st_ref[c, h] = s
        for c in range(GLA_GROUP):
            rows = slice(c * CHUNK, (c + 1) * CHUNK)
            q = (p_ref[rows, Q0:Q0 + KEY_DIM].astype(F32) * (HEAD_K ** -0.5)).astype(BF16)
            for h in range(GLA_HEADS):
                o_ref[rows, h * HEAD_V:(h + 1) * HEAD_V] = lax.dot_general(
                    _head(q, h, HEAD_K), st_ref[c, h].astype(BF16), (NT, ((), ())), preferred_element_type=F32)
        r = p_ref[:, R0:R0 + VALUE_DIM].astype(F32)
        gate = r * _sigmoid(r) * gn_ref[...]
        for h in range(GLA_HEADS):
            cols = slice(h * HEAD_V, (h + 1) * HEAD_V)
            o = o_ref[:, cols]
            og_ref[:, cols] = (o * _rstd(o) * gate[:, cols]).astype(BF16)

    rows = _spec((GLA_ROWS, VALUE_DIM), lambda i: (i, 0))
    const = lambda shape: _spec(shape, lambda i: (0,) * len(shape))
    return pl.pallas_call(
        body, name="gla_fwd", grid=(N_GROUPS,),
        in_specs=[_spec((GLA_ROWS, PROJ_A_PAD), lambda i: (i, 0)), const((GATE_PAD, KEY_DIM)), const((1, KEY_DIM)),
                  const((1, VALUE_DIM))],
        out_specs=[rows, rows, _spec((GLA_GROUP, GLA_HEADS, HEAD_V, HEAD_K), lambda i: (i, 0, 0, 0))],
        out_shape=[jax.ShapeDtypeStruct((SEQ, VALUE_DIM), F32), jax.ShapeDtypeStruct((SEQ, VALUE_DIM), BF16),
                   jax.ShapeDtypeStruct((N_CHUNKS, GLA_HEADS, HEAD_V, HEAD_K), F32)],
        scratch_shapes=[pltpu.VMEM((GLA_HEADS, HEAD_V, HEAD_K), F32)], compiler_params=_params(("arbitrary",)),
    )(proj, wgu, b_gate, gn)


def _gla_bwd(proj, wgu, b_gate, gn, o, states, dog):
    last = N_GROUPS - 1

    def body(p_ref, wgu_ref, b_ref, gn_ref, o_ref, st_ref, stp_ref, dog_ref, dp_ref, dwgu_ref, db_ref, dgn_ref, carry, do_buf,
             g_buf):
        step = pl.program_id(0)

        @pl.when(step == 0)
        def _():
            carry[...] = jnp.zeros_like(carry)

        r = p_ref[:, R0:R0 + VALUE_DIM].astype(F32)
        sr = _sigmoid(r)
        silu = r * sr
        gn_row = gn_ref[...]
        dog_rows = dog_ref[...].astype(F32)
        dn = dog_rows * silu
        dgn_cols = []
        for h in range(GLA_HEADS):
            cols = slice(h * HEAD_V, (h + 1) * HEAD_V)
            oh = o_ref[:, cols]
            rs = _rstd(oh)
            ohat = oh * rs
            dn_h = dn[:, cols]
            dgn_cols.append(jnp.sum(dn_h * ohat, axis=0, keepdims=True))
            dohat = dn_h * gn_row[:, cols]
            do_buf[:, cols] = rs * (dohat - ohat * jnp.mean(dohat * ohat, axis=-1, keepdims=True))
            n_h = ohat * gn_row[:, cols]
            dp_ref[:, R0 + h * HEAD_V:R0 + (h + 1) * HEAD_V] = (
                dog_rows[:, cols] * n_h * (sr[:, cols] * (1.0 + r[:, cols] * (1.0 - sr[:, cols])))).astype(BF16)
        dgn = jnp.concatenate(dgn_cols, axis=1)

        tri = _tri(False)
        tri_strict = _tri(True)
        gl = p_ref[:, G0:G0 + GATE_PAD]
        logits = _gate_logits(gl, wgu_ref[...], b_ref[...])
        la = _log_decay(logits)
        fades, kds, decays = [], [], []
        for c in range(GLA_GROUP):
            rows = slice(c * CHUNK, (c + 1) * CHUNK)
            cum = _cumsum_rows(tri, la[rows])
            tot = cum[CHUNK - 1:CHUNK, :]
            fades.append(jnp.exp(tot - cum))
            kds.append(p_ref[rows, K0:K0 + KEY_DIM].astype(F32) * fades[c])
            decays.append(jnp.exp(tot))
            q = (p_ref[rows, Q0:Q0 + KEY_DIM].astype(F32) * (HEAD_K ** -0.5)).astype(BF16)
            do = do_buf[rows, :].astype(BF16)
            for h in range(GLA_HEADS):
                do_h = _head(do, h, HEAD_V)
                dq = jnp.dot(do_h, st_ref[c, h].astype(BF16), preferred_element_type=F32) * (HEAD_K ** -0.5)
                dp_ref[rows, Q0 + h * HEAD_K:Q0 + (h + 1) * HEAD_K] = dq.astype(BF16)
                g_buf[c, h] = lax.dot_general(do_h, _head(q, h, HEAD_K), (TN, ((), ())), preferred_element_type=F32)
        for c in reversed(range(GLA_GROUP)):
            for h in range(GLA_HEADS):
                g = carry[h] + g_buf[c, h]
                g_buf[c, h] = g
                carry[h] = g * _head(decays[c], h, HEAD_K)
        dlogit_rows = []
        for c in range(GLA_GROUP):
            rows = slice(c * CHUNK, (c + 1) * CHUNK)
            v = p_ref[rows, V0:V0 + VALUE_DIM]
            kd = kds[c].astype(BF16)
            dkd_cols, ddecay_cols = [], []
            for h in range(GLA_HEADS):
                g = g_buf[c, h]
                g16 = g.astype(BF16)
                dkd_cols.append(jnp.dot(_head(v, h, HEAD_V), g16, preferred_element_type=F32))
                dv = lax.dot_general(_head(kd, h, HEAD_K), g16, (NT, ((), ())), preferred_element_type=F32)
                dp_ref[rows, V0 + h * HEAD_V:V0 + (h + 1) * HEAD_V] = dv.astype(BF16)
                if c > 0:
                    s_prev = st_ref[c - 1, h]
                else:
                    s_prev = jnp.where(step < last, stp_ref[0, h], 0.0)
                ddecay_cols.append(jnp.sum(g * s_prev, axis=0, keepdims=True))
            dkd = jnp.concatenate(dkd_cols, axis=1)
            ddecay = jnp.concatenate(ddecay_cols, axis=1)
            dp_ref[rows, K0:K0 + KEY_DIM] = (dkd * fades[c]).astype(BF16)
            e = dkd * kds[c]
            dla = ddecay * decays[c] + _cumsum_rows(tri_strict, e)
            dlogit_rows.append(dla * (1.0 / GATE_NORMALIZER) * (1.0 - _sigmoid(logits[rows])))
        dlogit = jnp.concatenate(dlogit_rows, axis=0)
        dlogit16 = dlogit.astype(BF16)
        dp_ref[:, G0:G0 + GATE_PAD] = lax.dot_general(
            dlogit16, wgu_ref[...], (NT, ((), ())), preferred_element_type=F32).astype(BF16)
        dwgu = lax.dot_general(gl, dlogit16, (TN, ((), ())), preferred_element_type=F32)
        db = jnp.sum(dlogit, axis=0, keepdims=True)

        @pl.when(step == 0)
        def _():
            dwgu_ref[...] = dwgu
            db_ref[...] = db
            dgn_ref[...] = dgn

        @pl.when(step > 0)
        def _():
            dwgu_ref[...] += dwgu
            db_ref[...] += db
            dgn_ref[...] += dgn

    rev = lambda i: (last - i, 0)
    rows = _spec((GLA_ROWS, VALUE_DIM), rev)
    const = lambda shape: _spec(shape, lambda i: (0,) * len(shape))
    st_shape = (GLA_HEADS, HEAD_V, HEAD_K)
    return pl.pallas_call(
        body, name="gla_bwd", grid=(N_GROUPS,),
        in_specs=[_spec((GLA_ROWS, PROJ_A_PAD), rev), const((GATE_PAD, KEY_DIM)), const((1, KEY_DIM)), const((1, VALUE_DIM)),
                  rows, _spec((GLA_GROUP,) + st_shape, lambda i: (last - i, 0, 0, 0)),
                  _spec((1,) + st_shape, lambda i: (jnp.maximum((last - i) * GLA_GROUP - 1, 0), 0, 0, 0)), rows],
        out_specs=[_spec((GLA_ROWS, PROJ_A_PAD), rev), const((GATE_PAD, KEY_DIM)), const((1, KEY_DIM)), const((1, VALUE_DIM))],
        out_shape=[jax.ShapeDtypeStruct((SEQ, PROJ_A_PAD), BF16), jax.ShapeDtypeStruct((GATE_PAD, KEY_DIM), F32),
                   jax.ShapeDtypeStruct((1, KEY_DIM), F32), jax.ShapeDtypeStruct((1, VALUE_DIM), F32)],
        scratch_shapes=[pltpu.VMEM(st_shape, F32), pltpu.VMEM((GLA_ROWS, VALUE_DIM), F32), pltpu.VMEM((GLA_GROUP,) + st_shape, F32)],
        compiler_params=_params(("arbitrary",)),
    )(proj, wgu, b_gate, gn, o, states, states, dog)


WGRAD_FF_TILE = D_FF // 2


CARRY_ROWS = 8
UP_ROWS = 512


def _ffn_up_mid(name, x, gamma, w_up_t, conv_w):
    def body(x_ref, g_ref, w_ref, c_ref, h_ref, gu_ref, a_ref, carry):
        @pl.when(pl.program_id(0) == 0)
        def _():
            carry[...] = jnp.zeros_like(carry)

        x_tile = x_ref[...]
        h_tile = (x_tile * _rstd(x_tile) * g_ref[...]).astype(BF16)
        h_ref[...] = h_tile
        for k in range(N_FF_COLS):
            cols = slice(k * FF_COLS, (k + 1) * FF_COLS)
            g, u = (lax.dot_general(h_tile, w_ref[p, cols, :], (NT, ((), ())), preferred_element_type=F32).astype(BF16)
                    for p in range(2))
            gu_ref[0, :, cols] = g
            gu_ref[1, :, cols] = u
            g = g.astype(F32)
            w = c_ref[:, cols]
            before = carry[:, cols]
            gc = w[2:3, :] * g + w[1:2, :] * _shift_down(g, before, 1) + w[0:1, :] * _shift_down(g, before, 2)
            a_ref[:, cols] = (gc * _sigmoid(gc) * u.astype(F32)).astype(BF16)
            carry[:, cols] = g[UP_ROWS - CARRY_ROWS:, :]

    row = _spec((UP_ROWS, D_MODEL), lambda i: (i, 0))
    return pl.pallas_call(
        body, name=name, grid=(SEQ // UP_ROWS,),
        in_specs=[row, _resident((1, D_MODEL)), _resident((2, D_FF, D_MODEL)), _resident((3, D_FF))],
        out_specs=[row, _spec((2, UP_ROWS, D_FF), lambda i: (0, i, 0)), _spec((UP_ROWS, D_FF), lambda i: (i, 0))],
        out_shape=[_act(), _act((2, SEQ, D_FF)), _act((SEQ, D_FF))], scratch_shapes=[pltpu.VMEM((CARRY_ROWS, D_FF), F32)],
        compiler_params=_params(("arbitrary",)),
    )(x, gamma, w_up_t, conv_w)


def _ffn_fwd(tag, x, gamma, w_up_t, conv_w, w_down):
    h, gu, a = _ffn_up_mid(f"ffn{tag}_up_mid", x, gamma, w_up_t, conv_w)
    return _rows_matmul(f"ffn{tag}_down", a, w_down, NN, x), (h, gu, a)


def _owner_blocks(d, rows=None):
    if rows is not None:
        d = d[:rows]
    return d.reshape((N_DEV, -1) + d.shape[-1:])


def _ffn_bwd(tag, x, gamma, w_up_t, conv_w, w_down, saved, dx, dx16, swap):
    h, gu, a = saved
    da = _rows_matmul(f"ffn{tag}_da", dx16, w_down, NT)
    d_w_down = _owner_blocks(_wgrad_cols_tn(f"ffn{tag}_dwdown", a, WGRAD_FF_TILE, dx16))
    dgu, d_conv = _ffn_mid_bwd(f"ffn{tag}_mid_bwd", gu, conv_w, da)
    d_w_up_t = _owner_blocks(_wgrad_halves_tn(f"ffn{tag}_dwup", dgu, WGRAD_FF_TILE, h))
    parts = (d_w_up_t, d_w_down)
    dx, dx16, d_gamma, *received = _sum_blocks_nn(
        f"ffn{tag}_dh", dgu, w_up_t, norm=(x, gamma, dx), swap=parts if swap else ())
    return dx, dx16, d_gamma, d_conv, parts, received


def _local_step(x, target, w, fetch=None, emit=None):
    if fetch is None:
        local = dict(a=(w.get("a_w_in"), w.get("a_w_out")), b=(w.get("b_w_in"), w.get("b_w_out")))
        for layer in range(2):
            local[f"f{layer}"] = (w["f_w_up"][layer], w["f_w_down"][layer]) if "f_w_up" in w else None
        fetch = lambda group, after: local[group]
    swap = emit is not None
    if emit is None:
        emit = lambda group, parts, received, dx: dx
    f_norm = (w["f_norm"][0:1], w["f_norm"][1:2])

    x0 = x
    a_w_in, a_w_out = fetch("a", x0)
    h0, proj = _norm_proj("a_in", x0, w["a_norm"], a_w_in)
    o, og, states = _gla_fwd(proj, w["a_w_gate_up"], w["a_b_gate"], w["a_gn"])
    x1 = _rows_matmul("a_out", og, a_w_out, NN, x0)
    up0, down0 = fetch("f0", x1)
    x2, ffn0 = _ffn_fwd(0, x1, f_norm[0], up0, w["f_conv"][0], down0)
    b_w_in, b_w_out = fetch("b", x2)
    h2, p = _norm_proj("b_in", x2, w["b_norm"], b_w_in)
    y = _sc_mid_fwd(p, w["b_conv"])
    x3 = _rows_matmul("b_out", y, b_w_out, NN, x2)
    up1, down1 = fetch("f1", x3)
    ffn1 = _ffn_up_mid("ffn1_up_mid", x3, f_norm[1], up1, w["f_conv"][1])
    loss, dx, dx16, d_final_norm = _down_loss_head(ffn1[2], down1, x3, w["final_norm"], target)

    dx, dx16, d_f_norm1, d_fconv1, parts_f1, got = _ffn_bwd(
        1, x3, f_norm[1], up1, w["f_conv"][1], down1, ffn1, dx, dx16, swap)
    dx16 = emit("f1", parts_f1, got, dx16)

    dy = _rows_matmul("b_dy", dx16, b_w_out, NT)
    d_b_w_out = _owner_blocks(_wgrad_cols_tn("b_dwout", y, OUT_TILE, dx16))
    db, dc, dhh, d_b_conv = _sc_mid_bwd(p, w["b_conv"], dy)
    dp = jnp.concatenate([db, dc, dhh], axis=1)
    parts_b = (_wgrad_cols_transposed_tn("b_dwin", h2, dp, B_SHARD), d_b_w_out)
    dx, dx16, d_b_norm, *got = _sum_cols_nt("b_dh", dp, b_w_in, norm=(x2, w["b_norm"], dx), swap=parts_b if swap else ())
    dx16 = emit("b", parts_b, got, dx16)

    dx, dx16, d_f_norm0, d_fconv0, parts_f0, got = _ffn_bwd(
        0, x1, f_norm[0], up0, w["f_conv"][0], down0, ffn0, dx, dx16, swap)
    dx16 = emit("f0", parts_f0, got, dx16)

    dog = _rows_matmul("a_dog", dx16, a_w_out, NT)
    d_a_w_out = _owner_blocks(_wgrad_cols_tn("a_dwout", og, OUT_TILE, dx16))
    dproj, d_wgu, d_b_gate, d_gn = _gla_bwd(proj, w["a_w_gate_up"], w["a_b_gate"], w["a_gn"], o, states, dog)
    parts_a = (_owner_blocks(_wgrad_cols_tn("a_dwin", dproj, PA_TILE, h0), PROJ_A), d_a_w_out)
    dx, _, d_a_norm, *got = _wide_nn("a_dh", dproj, a_w_in, norm=(x0, w["a_norm"], dx), swap=parts_a if swap else ())
    emit("a", parts_a, got, dx)

    grads = dict(
        a_norm=d_a_norm, a_w_in=parts_a[0], a_w_gate_up=d_wgu, a_b_gate=d_b_gate, a_gn=d_gn, a_w_out=parts_a[1],
        b_norm=d_b_norm, b_w_in=parts_b[0], b_conv=d_b_conv, b_w_out=parts_b[1],
        f_norm=(d_f_norm0, d_f_norm1), f_w_up=(parts_f0[0], parts_f1[0]), f_conv=(d_fconv0, d_fconv1),
        f_w_down=(parts_f0[1], parts_f1[1]), final_norm=d_final_norm)
    grads["loss"] = loss
    return dx, grads


MESH_ID = pl.DeviceIdType.MESH
ANY = pl.BlockSpec(memory_space=pl.ANY)
N_PEERS = N_DEV - 1


def _position():
    return lax.axis_index("x"), lax.axis_index("y"), lax.axis_index("c")


def _slot(px, py, pc):
    return 4 * px + 2 * py + pc


GATHER_COPIES = 8
HALF_ROWS = 16


def _gather_copies(src, out, send_sems, recv_sems, local_sems):
    n = len(src)
    to_sibling, to_x, to_y, x_on_to_y, y_on_to_x, x_to_sibling, y_to_sibling, diagonal_to_sibling = range(GATHER_COPIES)
    x, y, c = _position()
    me, sibling = (x, y, c), (x, y, 1 - c)
    x_side, y_side, diagonal = (1 - x, y), (x, 1 - y), (1 - x, 1 - y)

    def rows_of(t, half):
        rows = src[t].shape[0]
        half_rows = rows // 2 // HALF_ROWS * HALF_ROWS
        return (pl.ds(0, rows), pl.ds(0, half_rows), pl.ds(half_rows, rows - half_rows))[half]

    def copy(t, j, block, to, half=0, from_input=False):
        dst = out[t].at[_slot(*block), rows_of(t, half)]
        return pltpu.make_async_remote_copy(
            src_ref=src[t] if from_input else dst, dst_ref=dst, send_sem=send_sems.at[GATHER_COPIES * t + j],
            recv_sem=recv_sems.at[GATHER_COPIES * t + j], device_id=to, device_id_type=MESH_ID)

    mine = [pltpu.make_async_copy(src[t], out[t].at[_slot(*me)], local_sems.at[t]) for t in range(n)]
    for cp in mine:
        cp.start()
    sent = []

    def start(cp):
        cp.start()
        sent.append(cp)

    for t in range(n):
        start(copy(t, to_sibling, me, sibling, from_input=True))
        start(copy(t, to_x, me, (*x_side, c), from_input=True))
        start(copy(t, to_y, me, (*y_side, c), from_input=True))
    for t in range(n):
        copy(t, to_x, (*x_side, c), me).wait_recv()
        start(copy(t, x_on_to_y, (*x_side, c), (*y_side, c), half=1))
        start(copy(t, x_to_sibling, (*x_side, c), sibling))
        copy(t, to_y, (*y_side, c), me).wait_recv()
        start(copy(t, y_on_to_x, (*y_side, c), (*x_side, c), half=2))
        start(copy(t, y_to_sibling, (*y_side, c), sibling))
    for t in range(n):
        copy(t, x_on_to_y, (*diagonal, c), me, half=1).wait_recv()
        copy(t, y_on_to_x, (*diagonal, c), me, half=2).wait_recv()
        start(copy(t, diagonal_to_sibling, (*diagonal, c), sibling))
    for t in range(n):
        copy(t, to_sibling, sibling, me).wait_recv()
        for j, chip in ((x_to_sibling, x_side), (y_to_sibling, y_side), (diagonal_to_sibling, diagonal)):
            copy(t, j, (*chip, 1 - c), me).wait_recv()
    for cp in sent:
        cp.wait_send()
    for cp in mine:
        cp.wait()


def _all_gather(name, collective_id, shards):
    n = len(shards)

    def body(*refs):
        _handshake(SIBLING_AND_NEIGHBOURS)
        _gather_copies(refs[:n], refs[n:2 * n], *refs[2 * n:])

    sems = pltpu.SemaphoreType.DMA((GATHER_COPIES * n,))
    return pl.pallas_call(
        body, name=name, in_specs=[ANY] * n, out_specs=[ANY] * n,
        out_shape=[jax.ShapeDtypeStruct((N_DEV,) + s.shape, s.dtype) for s in shards],
        scratch_shapes=[sems, sems, pltpu.SemaphoreType.DMA((n,))],
        compiler_params=pltpu.CompilerParams(collective_id=collective_id),
    )(*shards)


SIBLING_AND_NEIGHBOURS = (1, 2, 4)
SAME_CORE = (2, 4, 6)


def _flip(x, y, c, k):
    return x ^ (k >> 2), y ^ ((k >> 1) & 1), c ^ (k & 1)


N_CHIPS = N_DEV // 2


def _chip(px, py):
    return 2 * px + py


def _pair_copies(parts, received, send_sems, recv_sems):
    x, y, c = lax.axis_index("x"), lax.axis_index("y"), lax.axis_index("c")
    sibling = (x, y, 1 - c)
    copies = []
    for t in range(len(parts)):
        for q in range(N_DEV // 2):
            send = pltpu.make_async_remote_copy(
                src_ref=parts[t].at[2 * q + 1 - c], dst_ref=received[t].at[q], send_sem=send_sems.at[t, q],
                recv_sem=recv_sems.at[t, q], device_id=sibling, device_id_type=pl.DeviceIdType.MESH)
            landed = received[t].at[q]
            arrival = pltpu.make_async_remote_copy(
                src_ref=landed, dst_ref=landed, send_sem=send_sems.at[t, q], recv_sem=recv_sems.at[t, q],
                device_id=sibling, device_id_type=pl.DeviceIdType.MESH)
            copies.append((send, arrival))
    return copies


def _pair_add(name, parts, received, side):
    n = len(parts)

    def body(side_ref, *refs):
        for t in range(n):
            refs[2 * n + t][...] = (refs[t][...].astype(F32) + refs[n + t][...].astype(F32)).astype(BF16)

    own = [_spec((None,) + p.shape[1:], lambda q, side_ref: (2 * q + side_ref[0], 0, 0)) for p in parts]
    chip = [_spec((None,) + p.shape[1:], lambda q, side_ref: (q, 0, 0)) for p in parts]
    return pl.pallas_call(
        body, name=name,
        grid_spec=pltpu.PrefetchScalarGridSpec(num_scalar_prefetch=1, grid=(N_CHIPS,), in_specs=own + chip, out_specs=chip),
        out_shape=[jax.ShapeDtypeStruct((N_CHIPS,) + p.shape[1:], BF16) for p in parts], compiler_params=_params(("parallel",)),
    )(side, *parts, *received)


def _send_copy(parts, landing, send_sems, recv_sems, t, s, k):
    x, y, c = _position()
    px, py, _ = _flip(x, y, c, k)
    return pltpu.make_async_remote_copy(
        src_ref=parts[t].at[_chip(px, py)], dst_ref=landing[t].at[_chip(x, y)], send_sem=send_sems.at[s],
        recv_sem=recv_sems.at[s], device_id=(px, py, c), device_id_type=MESH_ID)


def _send_arrival(landing, send_sems, recv_sems, t, s, k):
    x, y, c = _position()
    px, py, _ = _flip(x, y, c, k)
    landed = landing[t].at[_chip(px, py)]
    return pltpu.make_async_remote_copy(
        src_ref=landed, dst_ref=landed, send_sem=send_sems.at[s], recv_sem=recv_sems.at[s],
        device_id=(px, py, c), device_id_type=MESH_ID)


def _handshake(peers):
    x, y, c = _position()
    barrier = pltpu.get_barrier_semaphore()
    for k in peers:
        pl.semaphore_signal(barrier, inc=1, device_id=_flip(x, y, c, k), device_id_type=MESH_ID)
    pl.semaphore_wait(barrier, len(peers))


def _sequencer(name, collective_id, n_copies, body, operands, out_type):
    n_arrays = len(operands)
    return pl.kernel(
        body, out_type=out_type, mesh=plsc.ScalarSubcoreMesh(axis_name="sequencer", num_cores=1), name=name,
        scratch_types=(pltpu.SemaphoreType.DMA((n_copies,)), pltpu.SemaphoreType.DMA((n_copies,)),
                       pltpu.SemaphoreType.DMA((n_arrays,))),
        compiler_params=pltpu.CompilerParams(collective_id=collective_id))(*operands)


def _sequencer_exchange(name, collective_id, parts, after=()):
    n, n_peers, n_in = len(parts), len(SAME_CORE), len(parts) + len(after)

    def body(*refs):
        src, landing = refs[:n], refs[n_in:n_in + n]
        send_sems, recv_sems, local_sems = refs[n_in + n:]
        _handshake(SAME_CORE)
        x, y, _ = _position()
        mine = [pltpu.make_async_copy(src[t].at[_chip(x, y)], landing[t].at[_chip(x, y)], local_sems.at[t]) for t in range(n)]
        for cp in mine:
            cp.start()
        sent = [_send_copy(src, landing, send_sems, recv_sems, t, t * n_peers + j, k)
                for t in range(n) for j, k in enumerate(SAME_CORE)]
        for cp in sent:
            cp.start()
        for t in range(n):
            for j, k in enumerate(SAME_CORE):
                _send_arrival(landing, send_sems, recv_sems, t, t * n_peers + j, k).wait_recv()
        for cp in sent:
            cp.wait_send()
        for cp in mine:
            cp.wait()

    landing = [jax.ShapeDtypeStruct(p.shape, p.dtype) for p in parts]
    return _sequencer(name, collective_id, n * n_peers, body, list(parts) + list(after), landing)


def _sequencer_gather(name, collective_id, shards):
    n = len(shards)

    def body(*refs):
        _handshake(SIBLING_AND_NEIGHBOURS)
        _gather_copies(refs[:n], refs[n:2 * n], *refs[2 * n:])

    gathered = [jax.ShapeDtypeStruct((N_DEV,) + s.shape, s.dtype) for s in shards]
    return _sequencer(name, collective_id, GATHER_COPIES * n, body, shards, gathered)


ADAM_ROWS = 512
BF16_ROWS = 16


def _adam_update(w, g, m, v):
    m = ADAM_B1 * m + (1.0 - ADAM_B1) * g
    v = ADAM_B2 * v + (1.0 - ADAM_B2) * (g * g)
    m_hat = m / (1.0 - ADAM_B1 ** ADAM_STEP)
    v_hat = v / (1.0 - ADAM_B2 ** ADAM_STEP)
    delta = -ADAM_LR * (m_hat / (jnp.sqrt(v_hat) + ADAM_EPS) + ADAM_WD * w)
    return delta, m, v


def _sum_slots(ref):
    total = ref[0].astype(F32)
    for d in range(1, ref.shape[0]):
        total = total + ref[d].astype(F32)
    return total


def _adamw_sum(name, landed, w, m, v):
    layers, rows, cols = w.shape
    tiles = [t for t in range(ADAM_ROWS, 0, -BF16_ROWS) if rows % t == 0]
    tr = tiles[0] if tiles else rows
    nt = rows // tr

    def body(*refs):
        parts = refs[:layers]
        w_ref, m_ref, v_ref, g_ref, d_ref, nm_ref, nv_ref = refs[layers:]
        layer = pl.program_id(0)
        g = _sum_slots(parts[0])
        for q in range(1, layers):
            g = jnp.where(layer == q, _sum_slots(parts[q]), g)
        delta, new_m, new_v = _adam_update(w_ref[...], g, m_ref[...], v_ref[...])
        g_ref[...] = g
        d_ref[...] = delta
        nm_ref[...] = new_m
        nv_ref[...] = new_v

    def part_spec(q):
        return _spec((N_CHIPS, tr, cols), lambda l, i: (0, jnp.where(l == q, i, jnp.where(l < q, 0, nt - 1)), 0))

    tile = _spec((None, tr, cols), lambda l, i: (l, i, 0))
    out = jax.ShapeDtypeStruct((layers, rows, cols), F32)
    return pl.pallas_call(
        body, name=name, grid=(layers, nt), in_specs=[part_spec(q) for q in range(layers)] + [tile] * 3,
        out_specs=[tile] * 4, out_shape=[out] * 4, compiler_params=_params(("arbitrary", "arbitrary")),
    )(*landed, w, m, v)


def _sum_small(landed):
    def body(in_ref, out_ref):
        out_ref[...] = _sum_slots(in_ref)

    return pl.pallas_call(body, name="small_grad_sum", out_shape=jax.ShapeDtypeStruct(landed.shape[1:], F32))(landed)


def _adamw_small(arrays):
    n = len(arrays)

    def body(*refs):
        for i in range(n):
            g_ref, w_ref, m_ref, v_ref = refs[4 * i:4 * i + 4]
            d_ref, nm_ref, nv_ref = refs[4 * n + 3 * i:4 * n + 3 * i + 3]
            d_ref[...], nm_ref[...], nv_ref[...] = _adam_update(w_ref[...], g_ref[...], m_ref[...], v_ref[...])

    out = [jax.ShapeDtypeStruct(w.shape, F32) for _, w, _, _ in arrays for _ in range(3)]
    flat = pl.pallas_call(body, name="adam_small", out_shape=out)(*[a for group in arrays for a in group])
    return [tuple(flat[3 * i:3 * i + 3]) for i in range(n)]


LANES = 128
SUBLANES = 8
F_CONV_SHARD = D_FF // N_DEV
GATE_SHARD = KEY_DIM // N_DEV
NORM_SHARD = D_MODEL // N_DEV


def _tile_rows(a):
    flat = a.reshape(-1)
    size = -(-flat.shape[0] // (SUBLANES * LANES)) * SUBLANES * LANES
    return jnp.pad(flat, (0, size - flat.shape[0])).reshape(-1, LANES)


def _pack_rows(pieces):
    return jnp.concatenate([_tile_rows(p) for p in pieces], axis=0)


def _unpack_rows(packed, shapes):
    out, row = [], 0
    for shape in shapes:
        size = 1
        for s in shape:
            size *= s
        rows = -(-size // (SUBLANES * LANES)) * SUBLANES
        piece = packed[..., row:row + rows, :]
        out.append(piece.reshape(piece.shape[:-2] + (rows * LANES,))[..., :size])
        row += rows
    return out


SMALL_SHARDS = ((GATE_RANK, GATE_SHARD), (1, NORM_SHARD), (3, NORM_SHARD), (2, 3, F_CONV_SHARD))


def _unpack_small_shards(g):
    gate, b_norm, b_conv, f_conv = _unpack_rows(g, SMALL_SHARDS)
    gate = gate.reshape(N_DEV, GATE_RANK, GATE_SHARD).transpose(1, 0, 2).reshape(GATE_RANK, KEY_DIM)
    b_norm = b_norm.reshape(1, D_MODEL)
    b_conv = b_conv.reshape(N_DEV, 3, NORM_SHARD).transpose(1, 0, 2).reshape(3, D_MODEL)
    f_conv = f_conv.reshape(N_DEV, 2, 3, F_CONV_SHARD).transpose(1, 2, 0, 3).reshape(2, 3, D_FF)
    return gate, b_norm, b_conv, f_conv


SMALL_LAYOUT = (("a_norm", (1, D_MODEL)), ("a_w_gate_up", (GATE_RANK, KEY_DIM)), ("a_b_gate", (1, KEY_DIM)), ("a_gn", (1, VALUE_DIM)),
                ("b_norm", (1, D_MODEL)), ("b_conv", (3, D_MODEL)), ("f_norm0", (1, D_MODEL)), ("f_norm1", (1, D_MODEL)),
                ("f_conv0", (3, D_FF)), ("f_conv1", (3, D_FF)), ("final_norm", (1, D_MODEL)), ("loss", (1, LANES)))


def _pack_small_grads(g):
    full = dict(g)
    full["a_w_gate_up"] = g["a_w_gate_up"][:GATE_RANK]
    for layer in range(2):
        full[f"f_norm{layer}"] = g["f_norm"][layer]
        full[f"f_conv{layer}"] = g["f_conv"][layer]
    return _pack_rows([full[name] for name, _ in SMALL_LAYOUT])


def _unpack_small_grads(packed):
    pieces = _unpack_rows(packed, [shape for _, shape in SMALL_LAYOUT])
    out = {name: piece.reshape(shape) for (name, shape), piece in zip(SMALL_LAYOUT, pieces)}
    out["f_norm"] = jnp.stack([out["f_norm0"][0], out["f_norm1"][0]])
    out["f_conv"] = jnp.stack([out["f_conv0"], out["f_conv1"]])
    return out


def kernel(x, a_norm, a_w_in, a_w_gate_up, a_b_gate, a_gn, a_w_out, b_norm, b_w_in, b_conv, b_w_out, f_norm, f_w_up, f_conv, f_w_down, final_norm, loss_target, m_a_norm, m_a_w_in, m_a_w_gate_up, m_a_b_gate, m_a_gn, m_a_w_out, m_b_norm, m_b_w_in, m_b_conv, m_b_w_out, m_f_norm, m_f_w_up, m_f_conv, m_f_w_down, m_final_norm, v_a_norm, v_a_w_in, v_a_w_gate_up, v_a_b_gate, v_a_gn, v_a_w_out, v_b_norm, v_b_w_in, v_b_conv, v_b_w_out, v_f_norm, v_f_w_up, v_f_conv, v_f_w_down, v_final_norm):
    my_slot = _slot(*_position())

    transposed = lambda w: jnp.swapaxes(w, 1, 2)
    a_transposed = lambda w: w.reshape(D_MODEL, A_SHARD).T.reshape(1, A_SHARD, D_MODEL)
    a_w_in_t, f_w_up_t = a_transposed(a_w_in), transposed(f_w_up)
    first = _all_gather("weight_gather", 10, [a_w_in_t[0].astype(BF16), a_w_out[0].astype(BF16),
                                          _pack_rows([a_w_gate_up[0], b_norm, b_conv[0], f_conv])])
    gathers, small_shards = {}, first[2]
    later = (("f0", f_w_up_t[0], f_w_down[0]), ("b", b_w_in[0], b_w_out[0]), ("f1", f_w_up_t[1], f_w_down[1]))
    for collective_id, (group, w_in, w_out) in enumerate(later):
        w_in, w_out, small_shards = lax.optimization_barrier((w_in.astype(BF16), w_out.astype(BF16), small_shards))
        gathers[group] = _sequencer_gather(f"gather_{group}", collective_id, [w_in, w_out])
    gate_full, b_norm_full, b_conv_full, f_conv_full = _unpack_small_shards(small_shards)
    a_w_in_full = jnp.pad(first[0].reshape(PROJ_A, D_MODEL), ((0, PROJ_A_PAD - PROJ_A), (0, 0)))
    weights = dict(
        a_norm=a_norm, a_w_gate_up=jnp.pad(gate_full, ((0, GATE_PAD - GATE_RANK), (0, 0))).astype(BF16), a_b_gate=a_b_gate,
        a_gn=a_gn, b_norm=b_norm_full, b_conv=b_conv_full, f_norm=f_norm, f_conv=f_conv_full,
        final_norm=final_norm.reshape(1, D_MODEL))

    def fetch(group, after):
        if group == "a":
            return a_w_in_full, first[1].reshape(D_MODEL, D_MODEL)
        w_in, w_out = gathers[group]
        if group == "b":
            return w_in, w_out.reshape(D_MODEL, D_MODEL)
        return w_in.reshape(2, D_FF, D_MODEL), w_out.reshape(D_FF, D_MODEL)

    exchanges, pending = {}, []
    exchange_ids = dict(b=3, f0=4, a=5)
    side = lax.axis_index("c").astype(jnp.int32).reshape(1)

    def emit(group, parts, received, carry):
        sums = _pair_add(f"pair_add_{group}", parts, received, side)
        carry, *sums = lax.optimization_barrier((carry, *sums))
        pending.extend(sums)
        if group != "f1":
            after = list(exchanges.values())[-1][:1] if exchanges else ()
            exchanges[group] = _sequencer_exchange(f"grads_{group}", exchange_ids[group], list(pending), after)
            pending.clear()
        return carry

    dx, g = _local_step(x[0], loss_target[0], weights, fetch, emit)

    (up1, down1, d_b_in, d_b_out), (up0, down0), (d_a_in, d_a_out) = (exchanges[group] for group in ("b", "f0", "a"))
    back = lambda results: tuple(transposed(r) for r in results)
    big = dict(
        b_w_in=_adamw_sum("adam_b_w_in", [d_b_in], b_w_in, m_b_w_in, v_b_w_in),
        b_w_out=_adamw_sum("adam_b_w_out", [d_b_out], b_w_out, m_b_w_out, v_b_w_out),
        f_w_up=back(_adamw_sum("adam_f_w_up", [up0, up1], f_w_up_t, transposed(m_f_w_up), transposed(v_f_w_up))),
        f_w_down=_adamw_sum("adam_f_w_down", [down0, down1], f_w_down, m_f_w_down, v_f_w_down))
    small_packed, *updated = lax.optimization_barrier((_pack_small_grads(g), *big["f_w_down"]))
    big["f_w_down"] = tuple(updated)
    small_landed = _all_gather("small_grad_gather", 11, [small_packed])[0]
    big.update(
        a_w_in=tuple(r.reshape(A_SHARD, D_MODEL).T.reshape(1, D_MODEL, A_SHARD) for r in _adamw_sum(
            "adam_a_w_in", [d_a_in], a_w_in_t, a_transposed(m_a_w_in), a_transposed(v_a_w_in))),
        a_w_out=_adamw_sum("adam_a_w_out", [d_a_out], a_w_out, m_a_w_out, v_a_w_out))
    small_g = _unpack_small_grads(_sum_small(small_landed))
    loss = small_g["loss"][0, 0]
    small_g["a_w_gate_up"] = lax.dynamic_slice_in_dim(small_g["a_w_gate_up"], my_slot * GATE_SHARD, GATE_SHARD, axis=1)
    small_g["b_norm"] = lax.dynamic_slice_in_dim(small_g["b_norm"], my_slot * NORM_SHARD, NORM_SHARD, axis=1)
    small_g["b_conv"] = lax.dynamic_slice_in_dim(small_g["b_conv"], my_slot * NORM_SHARD, NORM_SHARD, axis=1)
    small_g["f_conv"] = lax.dynamic_slice_in_dim(small_g["f_conv"], my_slot * F_CONV_SHARD, F_CONV_SHARD, axis=2)
    small_w = dict(
        a_norm=(a_norm, m_a_norm, v_a_norm), a_w_gate_up=(a_w_gate_up, m_a_w_gate_up, v_a_w_gate_up),
        a_b_gate=(a_b_gate, m_a_b_gate, v_a_b_gate), a_gn=(a_gn, m_a_gn, v_a_gn), b_norm=(b_norm, m_b_norm, v_b_norm),
        b_conv=(b_conv, m_b_conv, v_b_conv), f_norm=(f_norm, m_f_norm, v_f_norm), f_conv=(f_conv, m_f_conv, v_f_conv),
        final_norm=(final_norm, m_final_norm, v_final_norm))
    two_d = lambda a: a.reshape(-1, a.shape[-1])
    updates = _adamw_small([tuple(two_d(a.reshape(w.shape)) for a in (small_g[name], w, m, v)) for name, (w, m, v) in small_w.items()])
    small = {}
    for (name, (w, _, _)), update in zip(small_w.items(), updates):
        small[name] = (small_g[name].reshape(w.shape),) + tuple(u.reshape(w.shape) for u in update)

    order = ["a_norm", "a_w_in", "a_w_gate_up", "a_b_gate", "a_gn", "a_w_out", "b_norm", "b_w_in", "b_conv", "b_w_out",
             "f_norm", "f_w_up", "f_conv", "f_w_down", "final_norm"]
    results = {**big, **small}
    outputs = [loss, dx.reshape(1, SEQ, D_MODEL)]
    for kind in range(4):
        outputs += [results[name][kind] for name in order]
    return tuple(outputs)
```

```python
import jax
import jax.numpy as jnp
from jax import lax
from jax.experimental import pallas as pl
from jax.experimental.pallas import tpu as pltpu
from jax.experimental.pallas import tpu_sc as plsc

F32 = jnp.float32
BF16 = jnp.bfloat16

N_DEV = 8
SEQ = 2048
D_MODEL = 1024
CHUNK = 64
N_CHUNKS = SEQ // CHUNK
RMS_EPS = 1e-6
GLA_HEADS = 4
KEY_DIM = 512
VALUE_DIM = 1024
HEAD_K = KEY_DIM // GLA_HEADS
HEAD_V = VALUE_DIM // GLA_HEADS
GATE_RANK = 16
GATE_PAD = 128
GATE_NORMALIZER = 16.0
PROJ_A = 2 * KEY_DIM + 2 * VALUE_DIM + GATE_RANK
PROJ_A_PAD = 2 * KEY_DIM + 2 * VALUE_DIM + GATE_PAD
A_SHARD = PROJ_A // N_DEV
B_SHARD = 3 * D_MODEL // N_DEV
D_FF = 2816
ADAM_LR = 0.001
ADAM_B1 = 0.9
ADAM_B2 = 0.999
ADAM_EPS = 1e-08
ADAM_WD = 0.01
ADAM_STEP = 10

VMEM_LIMIT = 56 * 1024 * 1024
ROW_CHUNK = 256
HALO = 16


def _params(sem=None, vmem=VMEM_LIMIT):
    return pltpu.CompilerParams(dimension_semantics=sem, vmem_limit_bytes=vmem)


SWAP_IDS = {"ffn1_dh": 6, "b_dh": 7, "ffn0_dh": 8, "pair_swap_a": 9}
NORM_PARTS = 2
NN = ((1,), (0,))
NT = ((1,), (1,))
TN = ((0,), (0,))


def _matmul(name, a, a_spec, b, b_spec, dims, grid, out_shape, out_spec, k_blocks=None, a_block_cols=None, res=None,
            res_spec=None, transpose_out=False, norm=None, swap=()):
    has_res = res is not None
    n_swap = len(swap)

    def body(*refs):
        a_ref, b_ref = refs[0], refs[1]
        r_ref = refs[2] if has_res else None

        def product(lhs, rhs):
            return lax.dot_general(lhs.astype(BF16), rhs, (dims, ((), ())), preferred_element_type=F32)

        def tile(rows=slice(None)):
            if k_blocks is None:
                return product(a_ref[rows, :] if norm is not None else a_ref[...], b_ref[...])
            v = None
            for k in range(k_blocks):
                lhs = a_ref[k, rows, :] if a_block_cols is None else a_ref[rows, k * a_block_cols:(k + 1) * a_block_cols]
                p = product(lhs, b_ref[k])
                v = p if v is None else v + p
            return v

        if norm is None:
            v = tile()
            if transpose_out:
                v = v.T
            if has_res:
                v = v + r_ref[...]
            o_ref = refs[2 + has_res]
            o_ref[...] = v.astype(o_ref.dtype)
            return
        n_in = 5 + has_res
        x_ref, g_ref, dxi_ref = refs[2 + has_res:n_in]
        dx_ref, dx16_ref, dg_ref = refs[n_in + n_swap:n_in + n_swap + 3]
        if n_swap:
            copies = _pair_copies(refs[n_in:n_in + n_swap], refs[n_in + n_swap + 3:n_in + 2 * n_swap + 3], *refs[-2:])

            @pl.when(pl.program_id(0) == 0)
            def _():
                sibling = (lax.axis_index("x"), lax.axis_index("y"), 1 - lax.axis_index("c"))
                barrier = pltpu.get_barrier_semaphore()
                pl.semaphore_signal(barrier, inc=1, device_id=sibling, device_id_type=pl.DeviceIdType.MESH)
                pl.semaphore_wait(barrier, 1)
                for send, _ in copies:
                    send.start()

            @pl.when(pl.program_id(0) == grid[0] - 1)
            def _():
                for send, arrival in copies:
                    arrival.wait_recv()
                    send.wait_send()

        dg = None
        part = dx_ref.shape[0] // NORM_PARTS
        for rows in (slice(i * part, (i + 1) * part) for i in range(NORM_PARTS)):
            dx, dg_rows = _norm_bwd_rows(x_ref[rows, :], g_ref[...], tile(rows))
            dx = dxi_ref[rows, :] + dx
            dx_ref[rows, :] = dx
            dx16_ref[rows, :] = dx.astype(BF16)
            dg = dg_rows if dg is None else dg + dg_rows

        @pl.when(pl.program_id(0) == 0)
        def _():
            dg_ref[...] = dg

        @pl.when(pl.program_id(0) > 0)
        def _():
            dg_ref[...] += dg

    operands = [a, b] + ([res] if has_res else [])
    in_specs = [a_spec, b_spec] + ([res_spec] if has_res else [])
    semantics = ("parallel",) * len(grid)
    scratch = []
    if norm is not None:
        vec = _spec((1, D_MODEL), lambda i: (0, 0))
        any_space = pl.BlockSpec(memory_space=pl.ANY)
        operands += list(norm) + list(swap)
        in_specs += [out_spec, vec, out_spec] + [any_space] * n_swap
        out_shape = [_act(dtype=F32), _act(), jax.ShapeDtypeStruct((1, D_MODEL), F32)]
        out_shape += [jax.ShapeDtypeStruct((N_DEV // 2,) + p.shape[1:], p.dtype) for p in swap]
        out_spec = [out_spec, out_spec, vec] + [any_space] * n_swap
        semantics = ("arbitrary",)
        if n_swap:
            scratch = [pltpu.SemaphoreType.DMA((n_swap, N_DEV // 2))] * 2
    params = _params(semantics)
    if n_swap:
        params = pltpu.CompilerParams(dimension_semantics=semantics, vmem_limit_bytes=VMEM_LIMIT, collective_id=SWAP_IDS[name])
    return pl.pallas_call(
        body, name=name, grid=grid, in_specs=in_specs, out_specs=out_spec, out_shape=out_shape, scratch_shapes=scratch,
        compiler_params=params,
    )(*operands)


def _resident(shape):
    return pl.BlockSpec(shape, lambda *_: (0,) * len(shape), pipeline_mode=pl.Buffered(1))


TM = 512
N_TM = SEQ // TM
PA_TILE = 640
OUT_TILE = 256


def _spec(shape, fn):
    return pl.BlockSpec(shape, fn)


def _act(shape=(SEQ, D_MODEL), dtype=BF16):
    return jax.ShapeDtypeStruct(shape, dtype)


def _norm_proj(name, x, gamma, w):
    blocks = w.ndim == 3
    n_out = w.shape[0] * w.shape[2] if blocks else w.shape[0]

    def body(x_ref, g_ref, w_ref, h_ref, o_ref):
        x = x_ref[...]
        h = (x * _rstd(x) * g_ref[...]).astype(BF16)
        h_ref[...] = h
        if blocks:
            n = w.shape[2]
            for j in range(w.shape[0]):
                o_ref[:, j * n:(j + 1) * n] = jnp.dot(h, w_ref[j], preferred_element_type=F32).astype(BF16)
        else:
            o_ref[...] = lax.dot_general(h, w_ref[...], (NT, ((), ())), preferred_element_type=F32).astype(BF16)

    row = _spec((TM, D_MODEL), lambda i: (i, 0))
    return pl.pallas_call(
        body, name=name, grid=(N_TM,), in_specs=[row, _resident((1, D_MODEL)), _resident(w.shape)],
        out_specs=[row, _spec((TM, n_out), lambda i: (i, 0))], out_shape=[_act(), _act((SEQ, n_out))],
        compiler_params=_params(("parallel",)),
    )(x, gamma, w)


def _rows_matmul(name, a, w, dims, x=None):
    k = a.shape[1]
    n = w.shape[1] if dims == NN else w.shape[0]
    row = _spec((TM, n), lambda i: (i, 0))
    return _matmul(name, a, _spec((TM, k), lambda i: (i, 0)), w, _resident(w.shape), dims, (N_TM,),
                   _act((SEQ, n), F32 if x is not None else BF16), row, res=x, res_spec=row if x is not None else None)


def _sum_blocks_nn(name, a_blocks, w_blocks, x=None, norm=None, swap=()):
    nb, _, n = a_blocks.shape
    row = _spec((TM, D_MODEL), lambda i: (i, 0))
    return _matmul(name, a_blocks, _spec((nb, TM, n), lambda i: (0, i, 0)), w_blocks, _resident((nb, n, D_MODEL)),
                   NN, (N_TM,), _act(dtype=F32), row, k_blocks=nb, res=x, res_spec=row if x is not None else None, norm=norm, swap=swap)


def _sum_cols_nt(name, d, w_blocks, norm=None, swap=()):
    nb, _, n = w_blocks.shape
    return _matmul(name, d, _spec((TM, nb * n), lambda i: (i, 0)), w_blocks, _resident((nb, D_MODEL, n)), NT,
                   (N_TM,), _act(dtype=F32), _spec((TM, D_MODEL), lambda i: (i, 0)), k_blocks=nb, a_block_cols=n, norm=norm, swap=swap)


def _wide_nn(name, d, wt, x=None, norm=None, swap=()):
    n = wt.shape[0]
    row = _spec((TM, D_MODEL), lambda i: (i, 0))
    return _matmul(name, d, _spec((TM, n), lambda i: (i, 0)), wt, _resident((n, D_MODEL)), NN, (N_TM,),
                   _act(dtype=F32), row, res=x, res_spec=row if x is not None else None, norm=norm, swap=swap)


def _wgrad_halves_tn(name, d, n_tile, h):
    _, _, n = d.shape
    return _matmul(name, d, _spec((None, SEQ, n_tile), lambda p, j: (p, 0, j)), h, _resident((SEQ, D_MODEL)), TN,
                   (2, n // n_tile), _act((2, n, D_MODEL)), _spec((None, n_tile, D_MODEL), lambda p, j: (p, j, 0)))


def _wgrad_cols_tn(name, d, n_tile, h):
    n = d.shape[1]
    return _matmul(name, d, _spec((SEQ, n_tile), lambda j: (0, j)), h, _resident((SEQ, D_MODEL)), TN,
                   (n // n_tile,), _act((n, D_MODEL)), _spec((n_tile, D_MODEL), lambda j: (j, 0)))


def _wgrad_cols_transposed_tn(name, h, d, n_tile):
    nb = d.shape[1] // n_tile
    return _matmul(name, d, _spec((SEQ, n_tile), lambda j: (0, j)), h, _resident((SEQ, D_MODEL)), TN, (nb,),
                   _act((nb, D_MODEL, n_tile)), _spec((None, D_MODEL, n_tile), lambda j: (j, 0, 0)), transpose_out=True)


def _rstd(x):
    return lax.rsqrt(jnp.mean(x * x, axis=-1, keepdims=True) + RMS_EPS)


def _norm_bwd_rows(x, gamma, dh):
    r = _rstd(x)
    xh = x * r
    dxh = dh * gamma
    dx = r * (dxh - xh * jnp.mean(dxh * xh, axis=-1, keepdims=True))
    return dx, jnp.sum(dh * xh, axis=0, keepdims=True)


def _down_loss_head(a, w_down, x_in, gamma, target):
    def body(a_ref, w_ref, x_ref, g_ref, t_ref, loss_ref, dx_ref, dx16_ref, dg_ref):
        gamma = g_ref[...]
        dg, part = 0.0, 0.0
        rows_per_part = TM // NORM_PARTS
        for rows in (slice(i * rows_per_part, (i + 1) * rows_per_part) for i in range(NORM_PARTS)):
            x = x_ref[rows, :] + jnp.dot(a_ref[rows, :], w_ref[...], preferred_element_type=F32)
            err = x * _rstd(x) * gamma - t_ref[rows, :]
            dy = err * (1.0 / D_MODEL)
            dx, dg_rows = _norm_bwd_rows(x, gamma, dy)
            dx_ref[rows, :] = dx
            dx16_ref[rows, :] = dx.astype(BF16)
            dg = dg + dg_rows
            part = part + 0.5 * jnp.sum(jnp.sum(err * err, axis=-1, keepdims=True) * (1.0 / D_MODEL), axis=0, keepdims=True)
        part = jnp.broadcast_to(part, loss_ref.shape)

        @pl.when(pl.program_id(0) == 0)
        def _():
            dg_ref[...] = dg
            loss_ref[...] = part

        @pl.when(pl.program_id(0) > 0)
        def _():
            dg_ref[...] += dg
            loss_ref[...] += part

    row = _spec((TM, D_MODEL), lambda i: (i, 0))
    vec = _spec((1, D_MODEL), lambda i: (0, 0))
    return pl.pallas_call(
        body, name="ffn1_down_loss_head", grid=(N_TM,),
        in_specs=[_spec((TM, D_FF), lambda i: (i, 0)), _resident((D_FF, D_MODEL)), row, vec, row],
        out_specs=[_spec((1, 128), lambda i: (0, 0)), row, row, vec],
        out_shape=[jax.ShapeDtypeStruct((1, 128), F32), _act(dtype=F32), _act(), jax.ShapeDtypeStruct((1, D_MODEL), F32)],
        compiler_params=_params(("arbitrary",)),
    )(a, w_down, x_in, gamma, target)


def _sigmoid(x):
    return 1.0 / (1.0 + jnp.exp(-x))


def _rows(ref, c):
    return ref[pl.ds(pl.multiple_of(c * ROW_CHUNK, ROW_CHUNK), ROW_CHUNK), :].astype(F32)


def _rows_before(ref, c):
    start = pl.multiple_of(jnp.maximum(c * ROW_CHUNK - HALO, 0), HALO)
    rows = ref[pl.ds(start, HALO), :].astype(F32)
    return jnp.where(c > 0, rows, 0.0)


def _rows_after(ref, c, n_chunks):
    start = pl.multiple_of(jnp.minimum((c + 1) * ROW_CHUNK, SEQ - HALO), HALO)
    rows = ref[pl.ds(start, HALO), :].astype(F32)
    return jnp.where(c < n_chunks - 1, rows, 0.0)


def _shift_down(z, before, n):
    return pltpu.roll(jnp.concatenate([before, z], axis=0), n, 0)[before.shape[0]:]


def _shift_up(z, after, n):
    rows = z.shape[0]
    return pltpu.roll(jnp.concatenate([z, after], axis=0), rows + HALO - n, 0)[:rows]


def _conv_rows(z, before, w):
    z1 = _shift_down(z, before, 1)
    z2 = _shift_down(z, before, 2)
    return w[2:3, :] * z + w[1:2, :] * z1 + w[0:1, :] * z2, z1, z2


def _conv_t_rows(dy, after, w):
    return w[2:3, :] * dy + w[1:2, :] * _shift_up(dy, after, 1) + w[0:1, :] * _shift_up(dy, after, 2)


N_ROW_CHUNKS = SEQ // ROW_CHUNK


FF_COLS = 256
N_FF_COLS = D_FF // FF_COLS


def _ffn_mid_bwd(name, gu, conv_w, da):
    def body(gu_ref, w_ref, da_ref, dgu_ref, dw_ref, dgc_ref):
        w = w_ref[...]

        def first(c, acc):
            g = _rows(gu_ref.at[0], c)
            u = _rows(gu_ref.at[1], c)
            d = _rows(da_ref, c)
            gc, g1, g2 = _conv_rows(g, _rows_before(gu_ref.at[0], c), w)
            sg = _sigmoid(gc)
            rows = pl.ds(pl.multiple_of(c * ROW_CHUNK, ROW_CHUNK), ROW_CHUNK)
            silu = gc * sg
            dgu_ref[1, rows, :] = (d * silu).astype(BF16)
            dgc = d * u * (sg + silu * (1.0 - sg))
            dgc_ref[rows, :] = dgc
            return (acc[0] + jnp.sum(dgc * g2, axis=0, keepdims=True), acc[1] + jnp.sum(dgc * g1, axis=0, keepdims=True),
                    acc[2] + jnp.sum(dgc * g, axis=0, keepdims=True))

        zero = jnp.zeros((1, FF_COLS), F32)
        acc = lax.fori_loop(0, N_ROW_CHUNKS, first, (zero, zero, zero))
        for r in range(3):
            dw_ref[r:r + 1, :] = acc[r]

        def second(c, carry):
            dgc = _rows(dgc_ref, c)
            dg = _conv_t_rows(dgc, _rows_after(dgc_ref, c, N_ROW_CHUNKS), w)
            dgu_ref[0, pl.ds(pl.multiple_of(c * ROW_CHUNK, ROW_CHUNK), ROW_CHUNK), :] = dg.astype(BF16)
            return carry

        lax.fori_loop(0, N_ROW_CHUNKS, second, 0)

    pair = _spec((2, SEQ, FF_COLS), lambda j: (0, 0, j))
    wspec = _spec((3, FF_COLS), lambda j: (0, j))
    return pl.pallas_call(
        body, name=name, grid=(N_FF_COLS,), in_specs=[pair, wspec, _spec((SEQ, FF_COLS), lambda j: (0, j))],
        out_specs=[pair, wspec], out_shape=[_act((2, SEQ, D_FF)), jax.ShapeDtypeStruct((3, D_FF), F32)],
        scratch_shapes=[pltpu.VMEM((SEQ, FF_COLS), F32)],
        compiler_params=_params(("parallel",)),
    )(gu, conv_w, da)


SC_COLS = 256
N_SC = D_MODEL // SC_COLS


def _sc_specs():
    return [_spec((SEQ, SC_COLS), lambda j, part=part: (0, part * N_SC + j)) for part in range(3)]


def _sc_mid_fwd(p, conv_w):
    def body(b_ref, c_ref, h_ref, w_ref, y_ref):
        w = w_ref[...]

        def chunk(c, carry):
            z = _rows(c_ref, c) * _rows(h_ref, c)
            before = _rows_before(c_ref, c) * _rows_before(h_ref, c)
            zc, _, _ = _conv_rows(z, before, w)
            y_ref[pl.ds(pl.multiple_of(c * ROW_CHUNK, ROW_CHUNK), ROW_CHUNK), :] = (_rows(b_ref, c) * zc).astype(BF16)
            return carry

        lax.fori_loop(0, N_ROW_CHUNKS, chunk, 0)

    col = _spec((SEQ, SC_COLS), lambda j: (0, j))
    return pl.pallas_call(
        body, name="sc_mid_fwd", grid=(N_SC,), in_specs=_sc_specs() + [_spec((3, SC_COLS), lambda j: (0, j))], out_specs=col,
        out_shape=jax.ShapeDtypeStruct((SEQ, D_MODEL), BF16), compiler_params=_params(("parallel",)),
    )(p, p, p, conv_w)


def _sc_mid_bwd(p, conv_w, dy):
    def body(b_ref, c_ref, h_ref, w_ref, dy_ref, db_ref, dc_ref, dh_ref, dw_ref, dzc_ref):
        w = w_ref[...]

        def first(c, acc):
            z = _rows(c_ref, c) * _rows(h_ref, c)
            before = _rows_before(c_ref, c) * _rows_before(h_ref, c)
            zc, z1, z2 = _conv_rows(z, before, w)
            d = _rows(dy_ref, c)
            rows = pl.ds(pl.multiple_of(c * ROW_CHUNK, ROW_CHUNK), ROW_CHUNK)
            db_ref[rows, :] = (d * zc).astype(BF16)
            dzc = d * _rows(b_ref, c)
            dzc_ref[rows, :] = dzc
            return (acc[0] + jnp.sum(dzc * z2, axis=0, keepdims=True), acc[1] + jnp.sum(dzc * z1, axis=0, keepdims=True),
                    acc[2] + jnp.sum(dzc * z, axis=0, keepdims=True))

        zero = jnp.zeros((1, SC_COLS), F32)
        acc = lax.fori_loop(0, N_ROW_CHUNKS, first, (zero, zero, zero))
        for r in range(3):
            dw_ref[r:r + 1, :] = acc[r]

        def second(c, carry):
            dz = _conv_t_rows(_rows(dzc_ref, c), _rows_after(dzc_ref, c, N_ROW_CHUNKS), w)
            rows = pl.ds(pl.multiple_of(c * ROW_CHUNK, ROW_CHUNK), ROW_CHUNK)
            dc_ref[rows, :] = (dz * _rows(h_ref, c)).astype(BF16)
            dh_ref[rows, :] = (dz * _rows(c_ref, c)).astype(BF16)
            return carry

        lax.fori_loop(0, N_ROW_CHUNKS, second, 0)

    col = _spec((SEQ, SC_COLS), lambda j: (0, j))
    wspec = _spec((3, SC_COLS), lambda j: (0, j))
    act = jax.ShapeDtypeStruct((SEQ, D_MODEL), BF16)
    return pl.pallas_call(
        body, name="sc_mid_bwd", grid=(N_SC,), in_specs=_sc_specs() + [wspec, col], out_specs=[col, col, col, wspec],
        out_shape=[act, act, act, jax.ShapeDtypeStruct((3, D_MODEL), F32)],
        scratch_shapes=[pltpu.VMEM((SEQ, SC_COLS), F32)], compiler_params=_params(("parallel",)),
    )(p, p, p, conv_w, dy)


GLA_GROUP = 4
GLA_ROWS = GLA_GROUP * CHUNK
N_GROUPS = N_CHUNKS // GLA_GROUP
Q0, K0, V0, R0, G0 = 0, KEY_DIM, 2 * KEY_DIM, 2 * KEY_DIM + VALUE_DIM, 2 * KEY_DIM + 2 * VALUE_DIM


def _tri(strict):
    r = lax.broadcasted_iota(jnp.int32, (CHUNK, CHUNK), 0)
    c = lax.broadcasted_iota(jnp.int32, (CHUNK, CHUNK), 1)
    return jnp.where(c < r if strict else c <= r, 1.0, 0.0).astype(F32)


def _cumsum_rows(tri, x):
    tri = tri.astype(BF16)
    total = None
    for _ in range(3):
        term = x.astype(BF16)
        x = x - term.astype(F32)
        product = jnp.dot(tri, term, preferred_element_type=F32)
        total = product if total is None else total + product
    return total


def _gate_logits(gl, wgu, b_gate):
    return jnp.dot(gl, wgu, preferred_element_type=F32) + b_gate


def _log_decay(logits):
    return (jnp.minimum(logits, 0.0) - jnp.log(1.0 + jnp.exp(-jnp.abs(logits)))) * (1.0 / GATE_NORMALIZER)


def _head(x, h, width):
    return x[:, h * width:(h + 1) * width]


def _gla_fwd(proj, wgu, b_gate, gn):
    def body(p_ref, wgu_ref, b_ref, gn_ref, o_ref, og_ref, st_ref, state):
        @pl.when(pl.program_id(0) == 0)
        def _():
            state[...] = jnp.zeros_like(state)

        tri = _tri(False)
        la = _log_decay(_gate_logits(p_ref[:, G0:G0 + GATE_PAD], wgu_ref[...], b_ref[...]))
        decays = []
        for c in range(GLA_GROUP):
            rows = slice(c * CHUNK, (c + 1) * CHUNK)
            cum = _cumsum_rows(tri, la[rows])
            tot = cum[CHUNK - 1:CHUNK, :]
            kd = (p_ref[rows, K0:K0 + KEY_DIM].astype(F32) * jnp.exp(tot - cum)).astype(BF16)
            decays.append(jnp.exp(tot))
            v = p_ref[rows, V0:V0 + VALUE_DIM]
            for h in range(GLA_HEADS):
                st_ref[c, h] = lax.dot_general(
                    _head(v, h, HEAD_V), _head(kd, h, HEAD_K), (TN, ((), ())), preferred_element_type=F32)
        for c in range(GLA_GROUP):
            for h in range(GLA_HEADS):
                s = state[h] * _head(decays[c], h, HEAD_K) + st_ref[c, h]
                state[h] = s
                st_ref[c, h] = s
        for c in range(GLA_GROUP):
            rows = slice(c * CHUNK, (c + 1) * CHUNK)
            q = (p_ref[rows, Q0:Q0 + KEY_DIM].astype(F32) * (HEAD_K ** -0.5)).astype(BF16)
            for h in range(GLA_HEADS):
                o_ref[rows, h * HEAD_V:(h + 1) * HEAD_V] = lax.dot_general(
                    _head(q, h, HEAD_K), st_ref[c, h].astype(BF16), (NT, ((), ())), preferred_element_type=F32)
        r = p_ref[:, R0:R0 + VALUE_DIM].astype(F32)
        gate = r * _sigmoid(r) * gn_ref[...]
        for h in range(GLA_HEADS):
            cols = slice(h * HEAD_V, (h + 1) * HEAD_V)
            o = o_ref[:, cols]
            og_ref[:, cols] = (o * _rstd(o) * gate[:, cols]).astype(BF16)

    rows = _spec((GLA_ROWS, VALUE_DIM), lambda i: (i, 0))
    const = lambda shape: _spec(shape, lambda i: (0,) * len(shape))
    return pl.pallas_call(
        body, name="gla_fwd", grid=(N_GROUPS,),
        in_specs=[_spec((GLA_ROWS, PROJ_A_PAD), lambda i: (i, 0)), const((GATE_PAD, KEY_DIM)), const((1, KEY_DIM)),
                  const((1, VALUE_DIM))],
        out_specs=[rows, rows, _spec((GLA_GROUP, GLA_HEADS, HEAD_V, HEAD_K), lambda i: (i, 0, 0, 0))],
        out_shape=[jax.ShapeDtypeStruct((SEQ, VALUE_DIM), F32), jax.ShapeDtypeStruct((SEQ, VALUE_DIM), BF16),
                   jax.ShapeDtypeStruct((N_CHUNKS, GLA_HEADS, HEAD_V, HEAD_K), F32)],
        scratch_shapes=[pltpu.VMEM((GLA_HEADS, HEAD_V, HEAD_K), F32)], compiler_params=_params(("arbitrary",)),
    )(proj, wgu, b_gate, gn)


def _gla_bwd(proj, wgu, b_gate, gn, o, states, dog):
    last = N_GROUPS - 1

    def body(p_ref, wgu_ref, b_ref, gn_ref, o_ref, st_ref, stp_ref, dog_ref, dp_ref, dwgu_ref, db_ref, dgn_ref, carry, do_buf,
             g_buf):
        step = pl.program_id(0)

        @pl.when(step == 0)
        def _():
            carry[...] = jnp.zeros_like(carry)

        r = p_ref[:, R0:R0 + VALUE_DIM].astype(F32)
        sr = _sigmoid(r)
        silu = r * sr
        gn_row = gn_ref[...]
        dog_rows = dog_ref[...].astype(F32)
        dn = dog_rows * silu
        dgn_cols = []
        for h in range(GLA_HEADS):
            cols = slice(h * HEAD_V, (h + 1) * HEAD_V)
            oh = o_ref[:, cols]
            rs = _rstd(oh)
            ohat = oh * rs
            dn_h = dn[:, cols]
            dgn_cols.append(jnp.sum(dn_h * ohat, axis=0, keepdims=True))
            dohat = dn_h * gn_row[:, cols]
            do_buf[:, cols] = rs * (dohat - ohat * jnp.mean(dohat * ohat, axis=-1, keepdims=True))
            n_h = ohat * gn_row[:, cols]
            dp_ref[:, R0 + h * HEAD_V:R0 + (h + 1) * HEAD_V] = (
                dog_rows[:, cols] * n_h * (sr[:, cols] * (1.0 + r[:, cols] * (1.0 - sr[:, cols])))).astype(BF16)
        dgn = jnp.concatenate(dgn_cols, axis=1)

        tri = _tri(False)
        tri_strict = _tri(True)
        gl = p_ref[:, G0:G0 + GATE_PAD]
        logits = _gate_logits(gl, wgu_ref[...], b_ref[...])
        la = _log_decay(logits)
        fades, kds, decays = [], [], []
        for c in range(GLA_GROUP):
            rows = slice(c * CHUNK, (c + 1) * CHUNK)
            cum = _cumsum_rows(tri, la[rows])
            tot = cum[CHUNK - 1:CHUNK, :]
            fades.append(jnp.exp(tot - cum))
            kds.append(p_ref[rows, K0:K0 + KEY_DIM].astype(F32) * fades[c])
            decays.append(jnp.exp(tot))
            q = (p_ref[rows, Q0:Q0 + KEY_DIM].astype(F32) * (HEAD_K ** -0.5)).astype(BF16)
            do = do_buf[rows, :].astype(BF16)
            for h in range(GLA_HEADS):
                do_h = _head(do, h, HEAD_V)
                dq = jnp.dot(do_h, st_ref[c, h].astype(BF16), preferred_element_type=F32) * (HEAD_K ** -0.5)
                dp_ref[rows, Q0 + h * HEAD_K:Q0 + (h + 1) * HEAD_K] = dq.astype(BF16)
                g_buf[c, h] = lax.dot_general(do_h, _head(q, h, HEAD_K), (TN, ((), ())), preferred_element_type=F32)
        for c in reversed(range(GLA_GROUP)):
            for h in range(GLA_HEADS):
                g = carry[h] + g_buf[c, h]
                g_buf[c, h] = g
                carry[h] = g * _head(decays[c], h, HEAD_K)
        dlogit_rows = []
        for c in range(GLA_GROUP):
            rows = slice(c * CHUNK, (c + 1) * CHUNK)
            v = p_ref[rows, V0:V0 + VALUE_DIM]
            kd = kds[c].astype(BF16)
            dkd_cols, ddecay_cols = [], []
            for h in range(GLA_HEADS):
                g = g_buf[c, h]
                g16 = g.astype(BF16)
                dkd_cols.append(jnp.dot(_head(v, h, HEAD_V), g16, preferred_element_type=F32))
                dv = lax.dot_general(_head(kd, h, HEAD_K), g16, (NT, ((), ())), preferred_element_type=F32)
                dp_ref[rows, V0 + h * HEAD_V:V0 + (h + 1) * HEAD_V] = dv.astype(BF16)
                if c > 0:
                    s_prev = st_ref[c - 1, h]
                else:
                    s_prev = jnp.where(step < last, stp_ref[0, h], 0.0)
                ddecay_cols.append(jnp.sum(g * s_prev, axis=0, keepdims=True))
            dkd = jnp.concatenate(dkd_cols, axis=1)
            ddecay = jnp.concatenate(ddecay_cols, axis=1)
            dp_ref[rows, K0:K0 + KEY_DIM] = (dkd * fades[c]).astype(BF16)
            e = dkd * kds[c]
            dla = ddecay * decays[c] + _cumsum_rows(tri_strict, e)
            dlogit_rows.append(dla * (1.0 / GATE_NORMALIZER) * (1.0 - _sigmoid(logits[rows])))
        dlogit = jnp.concatenate(dlogit_rows, axis=0)
        dlogit16 = dlogit.astype(BF16)
        dp_ref[:, G0:G0 + GATE_PAD] = lax.dot_general(
            dlogit16, wgu_ref[...], (NT, ((), ())), preferred_element_type=F32).astype(BF16)
        dwgu = lax.dot_general(gl, dlogit16, (TN, ((), ())), preferred_element_type=F32)
        db = jnp.sum(dlogit, axis=0, keepdims=True)

        @pl.when(step == 0)
        def _():
            dwgu_ref[...] = dwgu
            db_ref[...] = db
            dgn_ref[...] = dgn

        @pl.when(step > 0)
        def _():
            dwgu_ref[...] += dwgu
            db_ref[...] += db
            dgn_ref[...] += dgn

    rev = lambda i: (last - i, 0)
    rows = _spec((GLA_ROWS, VALUE_DIM), rev)
    const = lambda shape: _spec(shape, lambda i: (0,) * len(shape))
    st_shape = (GLA_HEADS, HEAD_V, HEAD_K)
    return pl.pallas_call(
        body, name="gla_bwd", grid=(N_GROUPS,),
        in_specs=[_spec((GLA_ROWS, PROJ_A_PAD), rev), const((GATE_PAD, KEY_DIM)), const((1, KEY_DIM)), const((1, VALUE_DIM)),
                  rows, _spec((GLA_GROUP,) + st_shape, lambda i: (last - i, 0, 0, 0)),
                  _spec((1,) + st_shape, lambda i: (jnp.maximum((last - i) * GLA_GROUP - 1, 0), 0, 0, 0)), rows],
        out_specs=[_spec((GLA_ROWS, PROJ_A_PAD), rev), const((GATE_PAD, KEY_DIM)), const((1, KEY_DIM)), const((1, VALUE_DIM))],
        out_shape=[jax.ShapeDtypeStruct((SEQ, PROJ_A_PAD), BF16), jax.ShapeDtypeStruct((GATE_PAD, KEY_DIM), F32),
                   jax.ShapeDtypeStruct((1, KEY_DIM), F32), jax.ShapeDtypeStruct((1, VALUE_DIM), F32)],
        scratch_shapes=[pltpu.VMEM(st_shape, F32), pltpu.VMEM((GLA_ROWS, VALUE_DIM), F32), pltpu.VMEM((GLA_GROUP,) + st_shape, F32)],
        compiler_params=_params(("arbitrary",)),
    )(proj, wgu, b_gate, gn, o, states, states, dog)


WGRAD_FF_TILE = D_FF // 2


CARRY_ROWS = 8
UP_ROWS = 512


def _ffn_up_mid(name, x, gamma, w_up_t, conv_w):
    def body(x_ref, g_ref, w_ref, c_ref, h_ref, gu_ref, a_ref, carry):
        @pl.when(pl.program_id(0) == 0)
        def _():
            carry[...] = jnp.zeros_like(carry)

        x_tile = x_ref[...]
        h_tile = (x_tile * _rstd(x_tile) * g_ref[...]).astype(BF16)
        h_ref[...] = h_tile
        for k in range(N_FF_COLS):
            cols = slice(k * FF_COLS, (k + 1) * FF_COLS)
            g, u = (lax.dot_general(h_tile, w_ref[p, cols, :], (NT, ((), ())), preferred_element_type=F32).astype(BF16)
                    for p in range(2))
            gu_ref[0, :, cols] = g
            gu_ref[1, :, cols] = u
            g = g.astype(F32)
            w = c_ref[:, cols]
            before = carry[:, cols]
            gc = w[2:3, :] * g + w[1:2, :] * _shift_down(g, before, 1) + w[0:1, :] * _shift_down(g, before, 2)
            a_ref[:, cols] = (gc * _sigmoid(gc) * u.astype(F32)).astype(BF16)
            carry[:, cols] = g[UP_ROWS - CARRY_ROWS:, :]

    row = _spec((UP_ROWS, D_MODEL), lambda i: (i, 0))
    return pl.pallas_call(
        body, name=name, grid=(SEQ // UP_ROWS,),
        in_specs=[row, _resident((1, D_MODEL)), _resident((2, D_FF, D_MODEL)), _resident((3, D_FF))],
        out_specs=[row, _spec((2, UP_ROWS, D_FF), lambda i: (0, i, 0)), _spec((UP_ROWS, D_FF), lambda i: (i, 0))],
        out_shape=[_act(), _act((2, SEQ, D_FF)), _act((SEQ, D_FF))], scratch_shapes=[pltpu.VMEM((CARRY_ROWS, D_FF), F32)],
        compiler_params=_params(("arbitrary",)),
    )(x, gamma, w_up_t, conv_w)


def _ffn_fwd(tag, x, gamma, w_up_t, conv_w, w_down):
    h, gu, a = _ffn_up_mid(f"ffn{tag}_up_mid", x, gamma, w_up_t, conv_w)
    return _rows_matmul(f"ffn{tag}_down", a, w_down, NN, x), (h, gu, a)


def _owner_blocks(d, rows=None):
    if rows is not None:
        d = d[:rows]
    return d.reshape((N_DEV, -1) + d.shape[-1:])


def _ffn_bwd(tag, x, gamma, w_up_t, conv_w, w_down, saved, dx, dx16, swap):
    h, gu, a = saved
    da = _rows_matmul(f"ffn{tag}_da", dx16, w_down, NT)
    d_w_down = _owner_blocks(_wgrad_cols_tn(f"ffn{tag}_dwdown", a, WGRAD_FF_TILE, dx16))
    dgu, d_conv = _ffn_mid_bwd(f"ffn{tag}_mid_bwd", gu, conv_w, da)
    d_w_up_t = _owner_blocks(_wgrad_halves_tn(f"ffn{tag}_dwup", dgu, WGRAD_FF_TILE, h))
    parts = (d_w_up_t, d_w_down)
    dx, dx16, d_gamma, *received = _sum_blocks_nn(
        f"ffn{tag}_dh", dgu, w_up_t, norm=(x, gamma, dx), swap=parts if swap else ())
    return dx, dx16, d_gamma, d_conv, parts, received


def _local_step(x, target, w, fetch=None, emit=None):
    if fetch is None:
        local = dict(a=(w.get("a_w_in"), w.get("a_w_out")), b=(w.get("b_w_in"), w.get("b_w_out")))
        for layer in range(2):
            local[f"f{layer}"] = (w["f_w_up"][layer], w["f_w_down"][layer]) if "f_w_up" in w else None
        fetch = lambda group, after: local[group]
    swap = emit is not None
    if emit is None:
        emit = lambda group, parts, received, dx: dx
    f_norm = (w["f_norm"][0:1], w["f_norm"][1:2])

    x0 = x
    a_w_in, a_w_out = fetch("a", x0)
    h0, proj = _norm_proj("a_in", x0, w["a_norm"], a_w_in)
    o, og, states = _gla_fwd(proj, w["a_w_gate_up"], w["a_b_gate"], w["a_gn"])
    x1 = _rows_matmul("a_out", og, a_w_out, NN, x0)
    up0, down0 = fetch("f0", x1)
    x2, ffn0 = _ffn_fwd(0, x1, f_norm[0], up0, w["f_conv"][0], down0)
    b_w_in, b_w_out = fetch("b", x2)
    h2, p = _norm_proj("b_in", x2, w["b_norm"], b_w_in)
    y = _sc_mid_fwd(p, w["b_conv"])
    x3 = _rows_matmul("b_out", y, b_w_out, NN, x2)
    up1, down1 = fetch("f1", x3)
    ffn1 = _ffn_up_mid("ffn1_up_mid", x3, f_norm[1], up1, w["f_conv"][1])
    loss, dx, dx16, d_final_norm = _down_loss_head(ffn1[2], down1, x3, w["final_norm"], target)

    dx, dx16, d_f_norm1, d_fconv1, parts_f1, got = _ffn_bwd(
        1, x3, f_norm[1], up1, w["f_conv"][1], down1, ffn1, dx, dx16, swap)
    dx16 = emit("f1", parts_f1, got, dx16)

    dy = _rows_matmul("b_dy", dx16, b_w_out, NT)
    d_b_w_out = _owner_blocks(_wgrad_cols_tn("b_dwout", y, OUT_TILE, dx16))
    db, dc, dhh, d_b_conv = _sc_mid_bwd(p, w["b_conv"], dy)
    dp = jnp.concatenate([db, dc, dhh], axis=1)
    parts_b = (_wgrad_cols_transposed_tn("b_dwin", h2, dp, B_SHARD), d_b_w_out)
    dx, dx16, d_b_norm, *got = _sum_cols_nt("b_dh", dp, b_w_in, norm=(x2, w["b_norm"], dx), swap=parts_b if swap else ())
    dx16 = emit("b", parts_b, got, dx16)

    dx, dx16, d_f_norm0, d_fconv0, parts_f0, got = _ffn_bwd(
        0, x1, f_norm[0], up0, w["f_conv"][0], down0, ffn0, dx, dx16, swap)
    dx16 = emit("f0", parts_f0, got, dx16)

    dog = _rows_matmul("a_dog", dx16, a_w_out, NT)
    d_a_w_out = _owner_blocks(_wgrad_cols_tn("a_dwout", og, OUT_TILE, dx16))
    dproj, d_wgu, d_b_gate, d_gn = _gla_bwd(proj, w["a_w_gate_up"], w["a_b_gate"], w["a_gn"], o, states, dog)
    parts_a = (_owner_blocks(_wgrad_cols_tn("a_dwin", dproj, PA_TILE, h0), PROJ_A), d_a_w_out)
    got = _pair_swap("pair_swap_a", parts_a) if swap else ()
    dproj = emit("a", parts_a, got, dproj)
    dx, _, d_a_norm = _wide_nn("a_dh", dproj, a_w_in, norm=(x0, w["a_norm"], dx))

    grads = dict(
        a_norm=d_a_norm, a_w_in=parts_a[0], a_w_gate_up=d_wgu, a_b_gate=d_b_gate, a_gn=d_gn, a_w_out=parts_a[1],
        b_norm=d_b_norm, b_w_in=parts_b[0], b_conv=d_b_conv, b_w_out=parts_b[1],
        f_norm=(d_f_norm0, d_f_norm1), f_w_up=(parts_f0[0], parts_f1[0]), f_conv=(d_fconv0, d_fconv1),
        f_w_down=(parts_f0[1], parts_f1[1]), final_norm=d_final_norm)
    grads["loss"] = loss
    return dx, grads


MESH_ID = pl.DeviceIdType.MESH
ANY = pl.BlockSpec(memory_space=pl.ANY)


def _position():
    return lax.axis_index("x"), lax.axis_index("y"), lax.axis_index("c")


def _slot(px, py, pc):
    return 4 * px + 2 * py + pc


GATHER_COPIES = 8
HALF_ROWS = 16


def _gather_copies(src, out, send_sems, recv_sems, local_sems):
    n = len(src)
    to_sibling, to_x, to_y, x_on_to_y, y_on_to_x, x_to_sibling, y_to_sibling, diagonal_to_sibling = range(GATHER_COPIES)
    x, y, c = _position()
    me, sibling = (x, y, c), (x, y, 1 - c)
    x_side, y_side, diagonal = (1 - x, y), (x, 1 - y), (1 - x, 1 - y)

    def rows_of(t, half):
        rows = src[t].shape[0]
        half_rows = rows // 2 // HALF_ROWS * HALF_ROWS
        return (pl.ds(0, rows), pl.ds(0, half_rows), pl.ds(half_rows, rows - half_rows))[half]

    def copy(t, j, block, to, half=0, from_input=False):
        dst = out[t].at[_slot(*block), rows_of(t, half)]
        return pltpu.make_async_remote_copy(
            src_ref=src[t] if from_input else dst, dst_ref=dst, send_sem=send_sems.at[GATHER_COPIES * t + j],
            recv_sem=recv_sems.at[GATHER_COPIES * t + j], device_id=to, device_id_type=MESH_ID)

    mine = [pltpu.make_async_copy(src[t], out[t].at[_slot(*me)], local_sems.at[t]) for t in range(n)]
    for cp in mine:
        cp.start()
    sent = []

    def start(cp):
        cp.start()
        sent.append(cp)

    for t in range(n):
        start(copy(t, to_sibling, me, sibling, from_input=True))
        start(copy(t, to_x, me, (*x_side, c), from_input=True))
        start(copy(t, to_y, me, (*y_side, c), from_input=True))
    for t in range(n):
        copy(t, to_x, (*x_side, c), me).wait_recv()
        start(copy(t, x_on_to_y, (*x_side, c), (*y_side, c), half=1))
        start(copy(t, x_to_sibling, (*x_side, c), sibling))
        copy(t, to_y, (*y_side, c), me).wait_recv()
        start(copy(t, y_on_to_x, (*y_side, c), (*x_side, c), half=2))
        start(copy(t, y_to_sibling, (*y_side, c), sibling))
    for t in range(n):
        copy(t, x_on_to_y, (*diagonal, c), me, half=1).wait_recv()
        copy(t, y_on_to_x, (*diagonal, c), me, half=2).wait_recv()
        start(copy(t, diagonal_to_sibling, (*diagonal, c), sibling))
    for t in range(n):
        copy(t, to_sibling, sibling, me).wait_recv()
        for j, chip in ((x_to_sibling, x_side), (y_to_sibling, y_side), (diagonal_to_sibling, diagonal)):
            copy(t, j, (*chip, 1 - c), me).wait_recv()
    for cp in sent:
        cp.wait_send()
    for cp in mine:
        cp.wait()


def _all_gather(name, collective_id, shards):
    n = len(shards)

    def body(*refs):
        _handshake(SIBLING_AND_NEIGHBOURS)
        _gather_copies(refs[:n], refs[n:2 * n], *refs[2 * n:])

    sems = pltpu.SemaphoreType.DMA((GATHER_COPIES * n,))
    return pl.pallas_call(
        body, name=name, in_specs=[ANY] * n, out_specs=[ANY] * n,
        out_shape=[jax.ShapeDtypeStruct((N_DEV,) + s.shape, s.dtype) for s in shards],
        scratch_shapes=[sems, sems, pltpu.SemaphoreType.DMA((n,))],
        compiler_params=pltpu.CompilerParams(collective_id=collective_id),
    )(*shards)


SIBLING_AND_NEIGHBOURS = (1, 2, 4)
SAME_CORE = (2, 4, 6)


def _flip(x, y, c, k):
    return x ^ (k >> 2), y ^ ((k >> 1) & 1), c ^ (k & 1)


N_CHIPS = N_DEV // 2


def _chip(px, py):
    return 2 * px + py


def _pair_swap(name, parts):
    n = len(parts)

    def body(*refs):
        sibling = (lax.axis_index("x"), lax.axis_index("y"), 1 - lax.axis_index("c"))
        barrier = pltpu.get_barrier_semaphore()
        pl.semaphore_signal(barrier, inc=1, device_id=sibling, device_id_type=pl.DeviceIdType.MESH)
        pl.semaphore_wait(barrier, 1)
        copies = _pair_copies(refs[:n], refs[n:2 * n], *refs[2 * n:])
        for send, _ in copies:
            send.start()
        for send, arrival in copies:
            arrival.wait_recv()
            send.wait_send()

    sems = pltpu.SemaphoreType.DMA((n, N_DEV // 2))
    any_space = pl.BlockSpec(memory_space=pl.ANY)
    return pl.pallas_call(
        body, name=name, in_specs=[any_space] * n, out_specs=[any_space] * n,
        out_shape=[jax.ShapeDtypeStruct((N_DEV // 2,) + p.shape[1:], p.dtype) for p in parts], scratch_shapes=[sems, sems],
        compiler_params=pltpu.CompilerParams(collective_id=SWAP_IDS[name]),
    )(*parts)


def _pair_copies(parts, received, send_sems, recv_sems):
    x, y, c = lax.axis_index("x"), lax.axis_index("y"), lax.axis_index("c")
    sibling = (x, y, 1 - c)
    copies = []
    for t in range(len(parts)):
        for q in range(N_DEV // 2):
            send = pltpu.make_async_remote_copy(
                src_ref=parts[t].at[2 * q + 1 - c], dst_ref=received[t].at[q], send_sem=send_sems.at[t, q],
                recv_sem=recv_sems.at[t, q], device_id=sibling, device_id_type=pl.DeviceIdType.MESH)
            landed = received[t].at[q]
            arrival = pltpu.make_async_remote_copy(
                src_ref=landed, dst_ref=landed, send_sem=send_sems.at[t, q], recv_sem=recv_sems.at[t, q],
                device_id=sibling, device_id_type=pl.DeviceIdType.MESH)
            copies.append((send, arrival))
    return copies


def _pair_add(name, parts, received, side):
    n = len(parts)

    def body(side_ref, *refs):
        for t in range(n):
            refs[2 * n + t][...] = (refs[t][...].astype(F32) + refs[n + t][...].astype(F32)).astype(BF16)

    own = [_spec((None,) + p.shape[1:], lambda q, side_ref: (2 * q + side_ref[0], 0, 0)) for p in parts]
    chip = [_spec((None,) + p.shape[1:], lambda q, side_ref: (q, 0, 0)) for p in parts]
    return pl.pallas_call(
        body, name=name,
        grid_spec=pltpu.PrefetchScalarGridSpec(num_scalar_prefetch=1, grid=(N_CHIPS,), in_specs=own + chip, out_specs=chip),
        out_shape=[jax.ShapeDtypeStruct((N_CHIPS,) + p.shape[1:], BF16) for p in parts], compiler_params=_params(("parallel",)),
    )(side, *parts, *received)


def _send_copy(parts, landing, send_sems, recv_sems, t, s, k):
    x, y, c = _position()
    px, py, _ = _flip(x, y, c, k)
    return pltpu.make_async_remote_copy(
        src_ref=parts[t].at[_chip(px, py)], dst_ref=landing[t].at[_chip(x, y)], send_sem=send_sems.at[s],
        recv_sem=recv_sems.at[s], device_id=(px, py, c), device_id_type=MESH_ID)


def _send_arrival(landing, send_sems, recv_sems, t, s, k):
    x, y, c = _position()
    px, py, _ = _flip(x, y, c, k)
    landed = landing[t].at[_chip(px, py)]
    return pltpu.make_async_remote_copy(
        src_ref=landed, dst_ref=landed, send_sem=send_sems.at[s], recv_sem=recv_sems.at[s],
        device_id=(px, py, c), device_id_type=MESH_ID)


def _handshake(peers):
    x, y, c = _position()
    barrier = pltpu.get_barrier_semaphore()
    for k in peers:
        pl.semaphore_signal(barrier, inc=1, device_id=_flip(x, y, c, k), device_id_type=MESH_ID)
    pl.semaphore_wait(barrier, len(peers))


def _sequencer(name, collective_id, n_copies, body, operands, out_type):
    n_arrays = len(operands)
    return pl.kernel(
        body, out_type=out_type, mesh=plsc.ScalarSubcoreMesh(axis_name="sequencer", num_cores=1), name=name,
        scratch_types=(pltpu.SemaphoreType.DMA((n_copies,)), pltpu.SemaphoreType.DMA((n_copies,)),
                       pltpu.SemaphoreType.DMA((n_arrays,))),
        compiler_params=pltpu.CompilerParams(collective_id=collective_id))(*operands)


def _sequencer_exchange(name, collective_id, parts, after=()):
    n, n_peers, n_in = len(parts), len(SAME_CORE), len(parts) + len(after)

    def body(*refs):
        src, landing = refs[:n], refs[n_in:n_in + n]
        send_sems, recv_sems, local_sems = refs[n_in + n:]
        _handshake(SAME_CORE)
        x, y, _ = _position()
        mine = [pltpu.make_async_copy(src[t].at[_chip(x, y)], landing[t].at[_chip(x, y)], local_sems.at[t]) for t in range(n)]
        for cp in mine:
            cp.start()
        sent = [_send_copy(src, landing, send_sems, recv_sems, t, t * n_peers + j, k)
                for t in range(n) for j, k in enumerate(SAME_CORE)]
        for cp in sent:
            cp.start()
        for t in range(n):
            for j, k in enumerate(SAME_CORE):
                _send_arrival(landing, send_sems, recv_sems, t, t * n_peers + j, k).wait_recv()
        for cp in sent:
            cp.wait_send()
        for cp in mine:
            cp.wait()

    landing = [jax.ShapeDtypeStruct(p.shape, p.dtype) for p in parts]
    return _sequencer(name, collective_id, n * n_peers, body, list(parts) + list(after), landing)


def _sequencer_gather(name, collective_id, shards):
    n = len(shards)

    def body(*refs):
        _handshake(SIBLING_AND_NEIGHBOURS)
        _gather_copies(refs[:n], refs[n:2 * n], *refs[2 * n:])

    gathered = [jax.ShapeDtypeStruct((N_DEV,) + s.shape, s.dtype) for s in shards]
    return _sequencer(name, collective_id, GATHER_COPIES * n, body, shards, gathered)


ADAM_ROWS = 512
BF16_ROWS = 16


def _adam_update(w, g, m, v):
    m = ADAM_B1 * m + (1.0 - ADAM_B1) * g
    v = ADAM_B2 * v + (1.0 - ADAM_B2) * (g * g)
    m_hat = m / (1.0 - ADAM_B1 ** ADAM_STEP)
    v_hat = v / (1.0 - ADAM_B2 ** ADAM_STEP)
    delta = -ADAM_LR * (m_hat / (jnp.sqrt(v_hat) + ADAM_EPS) + ADAM_WD * w)
    return delta, m, v


def _sum_slots(ref):
    total = ref[0].astype(F32)
    for d in range(1, ref.shape[0]):
        total = total + ref[d].astype(F32)
    return total


def _adamw_sum(name, landed, w, m, v):
    layers, rows, cols = w.shape
    tiles = [t for t in range(ADAM_ROWS, 0, -BF16_ROWS) if rows % t == 0]
    tr = tiles[0] if tiles else rows
    nt = rows // tr

    def body(*refs):
        parts = refs[:layers]
        w_ref, m_ref, v_ref, g_ref, d_ref, nm_ref, nv_ref = refs[layers:]
        layer = pl.program_id(0)
        g = _sum_slots(parts[0])
        for q in range(1, layers):
            g = jnp.where(layer == q, _sum_slots(parts[q]), g)
        delta, new_m, new_v = _adam_update(w_ref[...], g, m_ref[...], v_ref[...])
        g_ref[...] = g
        d_ref[...] = delta
        nm_ref[...] = new_m
        nv_ref[...] = new_v

    def part_spec(q):
        return _spec((N_CHIPS, tr, cols), lambda l, i: (0, jnp.where(l == q, i, jnp.where(l < q, 0, nt - 1)), 0))

    tile = _spec((None, tr, cols), lambda l, i: (l, i, 0))
    out = jax.ShapeDtypeStruct((layers, rows, cols), F32)
    return pl.pallas_call(
        body, name=name, grid=(layers, nt), in_specs=[part_spec(q) for q in range(layers)] + [tile] * 3,
        out_specs=[tile] * 4, out_shape=[out] * 4, compiler_params=_params(("arbitrary", "arbitrary")),
    )(*landed, w, m, v)


def _sum_small(landed):
    def body(in_ref, out_ref):
        out_ref[...] = _sum_slots(in_ref)

    return pl.pallas_call(body, name="small_grad_sum", out_shape=jax.ShapeDtypeStruct(landed.shape[1:], F32))(landed)


def _adamw_small(arrays):
    n = len(arrays)

    def body(*refs):
        for i in range(n):
            g_ref, w_ref, m_ref, v_ref = refs[4 * i:4 * i + 4]
            d_ref, nm_ref, nv_ref = refs[4 * n + 3 * i:4 * n + 3 * i + 3]
            d_ref[...], nm_ref[...], nv_ref[...] = _adam_update(w_ref[...], g_ref[...], m_ref[...], v_ref[...])

    out = [jax.ShapeDtypeStruct(w.shape, F32) for _, w, _, _ in arrays for _ in range(3)]
    flat = pl.pallas_call(body, name="adam_small", out_shape=out)(*[a for group in arrays for a in group])
    return [tuple(flat[3 * i:3 * i + 3]) for i in range(n)]


LANES = 128
SUBLANES = 8
F_CONV_SHARD = D_FF // N_DEV
GATE_SHARD = KEY_DIM // N_DEV
NORM_SHARD = D_MODEL // N_DEV


def _tile_rows(a):
    flat = a.reshape(-1)
    size = -(-flat.shape[0] // (SUBLANES * LANES)) * SUBLANES * LANES
    return jnp.pad(flat, (0, size - flat.shape[0])).reshape(-1, LANES)


def _pack_rows(pieces):
    return jnp.concatenate([_tile_rows(p) for p in pieces], axis=0)


def _unpack_rows(packed, shapes):
    out, row = [], 0
    for shape in shapes:
        size = 1
        for s in shape:
            size *= s
        rows = -(-size // (SUBLANES * LANES)) * SUBLANES
        piece = packed[..., row:row + rows, :]
        out.append(piece.reshape(piece.shape[:-2] + (rows * LANES,))[..., :size])
        row += rows
    return out


SMALL_SHARDS = ((GATE_RANK, GATE_SHARD), (1, NORM_SHARD), (3, NORM_SHARD), (2, 3, F_CONV_SHARD))


def _unpack_small_shards(g):
    gate, b_norm, b_conv, f_conv = _unpack_rows(g, SMALL_SHARDS)
    gate = gate.reshape(N_DEV, GATE_RANK, GATE_SHARD).transpose(1, 0, 2).reshape(GATE_RANK, KEY_DIM)
    b_norm = b_norm.reshape(1, D_MODEL)
    b_conv = b_conv.reshape(N_DEV, 3, NORM_SHARD).transpose(1, 0, 2).reshape(3, D_MODEL)
    f_conv = f_conv.reshape(N_DEV, 2, 3, F_CONV_SHARD).transpose(1, 2, 0, 3).reshape(2, 3, D_FF)
    return gate, b_norm, b_conv, f_conv


SMALL_LAYOUT = (("a_norm", (1, D_MODEL)), ("a_w_gate_up", (GATE_RANK, KEY_DIM)), ("a_b_gate", (1, KEY_DIM)), ("a_gn", (1, VALUE_DIM)),
                ("b_norm", (1, D_MODEL)), ("b_conv", (3, D_MODEL)), ("f_norm0", (1, D_MODEL)), ("f_norm1", (1, D_MODEL)),
                ("f_conv0", (3, D_FF)), ("f_conv1", (3, D_FF)), ("final_norm", (1, D_MODEL)), ("loss", (1, LANES)))


def _pack_small_grads(g):
    full = dict(g)
    full["a_w_gate_up"] = g["a_w_gate_up"][:GATE_RANK]
    for layer in range(2):
        full[f"f_norm{layer}"] = g["f_norm"][layer]
        full[f"f_conv{layer}"] = g["f_conv"][layer]
    return _pack_rows([full[name] for name, _ in SMALL_LAYOUT])


def _unpack_small_grads(packed):
    pieces = _unpack_rows(packed, [shape for _, shape in SMALL_LAYOUT])
    out = {name: piece.reshape(shape) for (name, shape), piece in zip(SMALL_LAYOUT, pieces)}
    out["f_norm"] = jnp.stack([out["f_norm0"][0], out["f_norm1"][0]])
    out["f_conv"] = jnp.stack([out["f_conv0"], out["f_conv1"]])
    return out


def kernel(x, a_norm, a_w_in, a_w_gate_up, a_b_gate, a_gn, a_w_out, b_norm, b_w_in, b_conv, b_w_out, f_norm, f_w_up, f_conv, f_w_down, final_norm, loss_target, m_a_norm, m_a_w_in, m_a_w_gate_up, m_a_b_gate, m_a_gn, m_a_w_out, m_b_norm, m_b_w_in, m_b_conv, m_b_w_out, m_f_norm, m_f_w_up, m_f_conv, m_f_w_down, m_final_norm, v_a_norm, v_a_w_in, v_a_w_gate_up, v_a_b_gate, v_a_gn, v_a_w_out, v_b_norm, v_b_w_in, v_b_conv, v_b_w_out, v_f_norm, v_f_w_up, v_f_conv, v_f_w_down, v_final_norm):
    my_slot = _slot(*_position())

    transposed = lambda w: jnp.swapaxes(w, 1, 2)
    a_transposed = lambda w: w.reshape(D_MODEL, A_SHARD).T.reshape(1, A_SHARD, D_MODEL)
    a_w_in_t, f_w_up_t = a_transposed(a_w_in), transposed(f_w_up)
    first = _all_gather("weight_gather", 10, [a_w_in_t[0].astype(BF16), a_w_out[0].astype(BF16),
                                          _pack_rows([a_w_gate_up[0], b_norm, b_conv[0], f_conv])])
    gathers, small_shards = {}, first[2]
    later = (("f0", f_w_up_t[0], f_w_down[0]), ("b", b_w_in[0], b_w_out[0]), ("f1", f_w_up_t[1], f_w_down[1]))
    for collective_id, (group, w_in, w_out) in enumerate(later):
        w_in, w_out, small_shards = lax.optimization_barrier((w_in.astype(BF16), w_out.astype(BF16), small_shards))
        gathers[group] = _sequencer_gather(f"gather_{group}", collective_id, [w_in, w_out])
    gate_full, b_norm_full, b_conv_full, f_conv_full = _unpack_small_shards(small_shards)
    a_w_in_full = jnp.pad(first[0].reshape(PROJ_A, D_MODEL), ((0, PROJ_A_PAD - PROJ_A), (0, 0)))
    weights = dict(
        a_norm=a_norm, a_w_gate_up=jnp.pad(gate_full, ((0, GATE_PAD - GATE_RANK), (0, 0))).astype(BF16), a_b_gate=a_b_gate,
        a_gn=a_gn, b_norm=b_norm_full, b_conv=b_conv_full, f_norm=f_norm, f_conv=f_conv_full,
        final_norm=final_norm.reshape(1, D_MODEL))

    def fetch(group, after):
        if group == "a":
            return a_w_in_full, first[1].reshape(D_MODEL, D_MODEL)
        w_in, w_out = gathers[group]
        if group == "b":
            return w_in, w_out.reshape(D_MODEL, D_MODEL)
        return w_in.reshape(2, D_FF, D_MODEL), w_out.reshape(D_FF, D_MODEL)

    exchanges, pending = {}, []
    exchange_ids = dict(b=3, f0=4, a=5)
    side = lax.axis_index("c").astype(jnp.int32).reshape(1)

    def emit(group, parts, received, carry):
        sums = _pair_add(f"pair_add_{group}", parts, received, side)
        carry, *sums = lax.optimization_barrier((carry, *sums))
        pending.extend(sums)
        if group != "f1":
            after = list(exchanges.values())[-1][:1] if exchanges else ()
            exchanges[group] = _sequencer_exchange(f"grads_{group}", exchange_ids[group], list(pending), after)
            pending.clear()
        return carry

    dx, g = _local_step(x[0], loss_target[0], weights, fetch, emit)

    (up1, down1, d_b_in, d_b_out), (up0, down0), (d_a_in, d_a_out) = (exchanges[group] for group in ("b", "f0", "a"))
    back = lambda results: tuple(transposed(r) for r in results)
    big = dict(
        b_w_in=_adamw_sum("adam_b_w_in", [d_b_in], b_w_in, m_b_w_in, v_b_w_in),
        b_w_out=_adamw_sum("adam_b_w_out", [d_b_out], b_w_out, m_b_w_out, v_b_w_out),
        f_w_up=back(_adamw_sum("adam_f_w_up", [up0, up1], f_w_up_t, transposed(m_f_w_up), transposed(v_f_w_up))),
        f_w_down=_adamw_sum("adam_f_w_down", [down0, down1], f_w_down, m_f_w_down, v_f_w_down))
    small_packed, *updated = lax.optimization_barrier((_pack_small_grads(g), *big["f_w_down"]))
    big["f_w_down"] = tuple(updated)
    small_landed = _all_gather("small_grad_gather", 11, [small_packed])[0]
    big.update(
        a_w_in=tuple(r.reshape(A_SHARD, D_MODEL).T.reshape(1, D_MODEL, A_SHARD) for r in _adamw_sum(
            "adam_a_w_in", [d_a_in], a_w_in_t, a_transposed(m_a_w_in), a_transposed(v_a_w_in))),
        a_w_out=_adamw_sum("adam_a_w_out", [d_a_out], a_w_out, m_a_w_out, v_a_w_out))
    small_g = _unpack_small_grads(_sum_small(small_landed))
    loss = small_g["loss"][0, 0]
    small_g["a_w_gate_up"] = lax.dynamic_slice_in_dim(small_g["a_w_gate_up"], my_slot * GATE_SHARD, GATE_SHARD, axis=1)
    small_g["b_norm"] = lax.dynamic_slice_in_dim(small_g["b_norm"], my_slot * NORM_SHARD, NORM_SHARD, axis=1)
    small_g["b_conv"] = lax.dynamic_slice_in_dim(small_g["b_conv"], my_slot * NORM_SHARD, NORM_SHARD, axis=1)
    small_g["f_conv"] = lax.dynamic_slice_in_dim(small_g["f_conv"], my_slot * F_CONV_SHARD, F_CONV_SHARD, axis=2)
    small_w = dict(
        a_norm=(a_norm, m_a_norm, v_a_norm), a_w_gate_up=(a_w_gate_up, m_a_w_gate_up, v_a_w_gate_up),
        a_b_gate=(a_b_gate, m_a_b_gate, v_a_b_gate), a_gn=(a_gn, m_a_gn, v_a_gn), b_norm=(b_norm, m_b_norm, v_b_norm),
        b_conv=(b_conv, m_b_conv, v_b_conv), f_norm=(f_norm, m_f_norm, v_f_norm), f_conv=(f_conv, m_f_conv, v_f_conv),
        final_norm=(final_norm, m_final_norm, v_final_norm))
    two_d = lambda a: a.reshape(-1, a.shape[-1])
    updates = _adamw_small([tuple(two_d(a.reshape(w.shape)) for a in (small_g[name], w, m, v)) for name, (w, m, v) in small_w.items()])
    small = {}
    for (name, (w, _, _)), update in zip(small_w.items(), updates):
        small[name] = (small_g[name].reshape(w.shape),) + tuple(u.reshape(w.shape) for u in update)

    order = ["a_norm", "a_w_in", "a_w_gate_up", "a_b_gate", "a_gn", "a_w_out", "b_norm", "b_w_in", "b_conv", "b_w_out",
             "f_norm", "f_w_up", "f_conv", "f_w_down", "final_norm"]
    results = {**big, **small}
    outputs = [loss, dx.reshape(1, SEQ, D_MODEL)]
    for kind in range(4):
        outputs += [results[name][kind] for name in order]
    return tuple(outputs)
```

```python
import jax
import jax.numpy as jnp
from jax import lax
from jax.experimental import pallas as pl
from jax.experimental.pallas import tpu as pltpu
from jax.experimental.pallas import tpu_sc as plsc

F32 = jnp.float32
BF16 = jnp.bfloat16

N_DEV = 8
SEQ = 2048
D_MODEL = 1024
CHUNK = 64
N_CHUNKS = SEQ // CHUNK
RMS_EPS = 1e-6
GLA_HEADS = 4
KEY_DIM = 512
VALUE_DIM = 1024
HEAD_K = KEY_DIM // GLA_HEADS
HEAD_V = VALUE_DIM // GLA_HEADS
GATE_RANK = 16
GATE_PAD = 128
GATE_NORMALIZER = 16.0
PROJ_A = 2 * KEY_DIM + 2 * VALUE_DIM + GATE_RANK
PROJ_A_PAD = 2 * KEY_DIM + 2 * VALUE_DIM + GATE_PAD
A_SHARD = PROJ_A // N_DEV
B_SHARD = 3 * D_MODEL // N_DEV
D_FF = 2816
ADAM_LR = 0.001
ADAM_B1 = 0.9
ADAM_B2 = 0.999
ADAM_EPS = 1e-08
ADAM_WD = 0.01
ADAM_STEP = 10

VMEM_LIMIT = 56 * 1024 * 1024
ROW_CHUNK = 256
HALO = 16


def _params(sem=None, vmem=VMEM_LIMIT):
    return pltpu.CompilerParams(dimension_semantics=sem, vmem_limit_bytes=vmem)


SWAP_IDS = {"ffn1_dh": 6, "b_dh": 7, "ffn0_dh": 8, "pair_swap_a": 9}
NORM_PARTS = 2
NN = ((1,), (0,))
NT = ((1,), (1,))
TN = ((0,), (0,))


def _matmul(name, a, a_spec, b, b_spec, dims, grid, out_shape, out_spec, k_blocks=None, a_block_cols=None, res=None,
            res_spec=None, transpose_out=False, norm=None, swap=()):
    has_res = res is not None
    n_swap = len(swap)

    def body(*refs):
        a_ref, b_ref = refs[0], refs[1]
        r_ref = refs[2] if has_res else None

        def product(lhs, rhs):
            return lax.dot_general(lhs.astype(BF16), rhs, (dims, ((), ())), preferred_element_type=F32)

        def tile(rows=slice(None)):
            if k_blocks is None:
                return product(a_ref[rows, :] if norm is not None else a_ref[...], b_ref[...])
            v = None
            for k in range(k_blocks):
                lhs = a_ref[k, rows, :] if a_block_cols is None else a_ref[rows, k * a_block_cols:(k + 1) * a_block_cols]
                p = product(lhs, b_ref[k])
                v = p if v is None else v + p
            return v

        if norm is None:
            v = tile()
            if transpose_out:
                v = v.T
            if has_res:
                v = v + r_ref[...]
            o_ref = refs[2 + has_res]
            o_ref[...] = v.astype(o_ref.dtype)
            return
        n_in = 5 + has_res
        x_ref, g_ref, dxi_ref = refs[2 + has_res:n_in]
        dx_ref, dx16_ref, dg_ref = refs[n_in + n_swap:n_in + n_swap + 3]
        if n_swap:
            copies = _pair_copies(refs[n_in:n_in + n_swap], refs[n_in + n_swap + 3:n_in + 2 * n_swap + 3], *refs[-2:])

            @pl.when(pl.program_id(0) == 0)
            def _():
                sibling = (lax.axis_index("x"), lax.axis_index("y"), 1 - lax.axis_index("c"))
                barrier = pltpu.get_barrier_semaphore()
                pl.semaphore_signal(barrier, inc=1, device_id=sibling, device_id_type=pl.DeviceIdType.MESH)
                pl.semaphore_wait(barrier, 1)
                for send, _ in copies:
                    send.start()

            @pl.when(pl.program_id(0) == grid[0] - 1)
            def _():
                for send, arrival in copies:
                    arrival.wait_recv()
                    send.wait_send()

        dg = None
        part = dx_ref.shape[0] // NORM_PARTS
        for rows in (slice(i * part, (i + 1) * part) for i in range(NORM_PARTS)):
            dx, dg_rows = _norm_bwd_rows(x_ref[rows, :], g_ref[...], tile(rows))
            dx = dxi_ref[rows, :] + dx
            dx_ref[rows, :] = dx
            dx16_ref[rows, :] = dx.astype(BF16)
            dg = dg_rows if dg is None else dg + dg_rows

        @pl.when(pl.program_id(0) == 0)
        def _():
            dg_ref[...] = dg

        @pl.when(pl.program_id(0) > 0)
        def _():
            dg_ref[...] += dg

    operands = [a, b] + ([res] if has_res else [])
    in_specs = [a_spec, b_spec] + ([res_spec] if has_res else [])
    semantics = ("parallel",) * len(grid)
    scratch = []
    if norm is not None:
        vec = _spec((1, D_MODEL), lambda i: (0, 0))
        any_space = pl.BlockSpec(memory_space=pl.ANY)
        operands += list(norm) + list(swap)
        in_specs += [out_spec, vec, out_spec] + [any_space] * n_swap
        out_shape = [_act(dtype=F32), _act(), jax.ShapeDtypeStruct((1, D_MODEL), F32)]
        out_shape += [jax.ShapeDtypeStruct((N_DEV // 2,) + p.shape[1:], p.dtype) for p in swap]
        out_spec = [out_spec, out_spec, vec] + [any_space] * n_swap
        semantics = ("arbitrary",)
        if n_swap:
            scratch = [pltpu.SemaphoreType.DMA((n_swap, N_DEV // 2))] * 2
    params = _params(semantics)
    if n_swap:
        params = pltpu.CompilerParams(dimension_semantics=semantics, vmem_limit_bytes=VMEM_LIMIT, collective_id=SWAP_IDS[name])
    return pl.pallas_call(
        body, name=name, grid=grid, in_specs=in_specs, out_specs=out_spec, out_shape=out_shape, scratch_shapes=scratch,
        compiler_params=params,
    )(*operands)


def _resident(shape):
    return pl.BlockSpec(shape, lambda *_: (0,) * len(shape), pipeline_mode=pl.Buffered(1))


TM = 512
N_TM = SEQ // TM
PA_TILE = 640
OUT_TILE = 256


def _spec(shape, fn):
    return pl.BlockSpec(shape, fn)


def _act(shape=(SEQ, D_MODEL), dtype=BF16):
    return jax.ShapeDtypeStruct(shape, dtype)


def _norm_proj(name, x, gamma, w):
    blocks = w.ndim == 3
    n_out = w.shape[0] * w.shape[2] if blocks else w.shape[0]

    def body(x_ref, g_ref, w_ref, h_ref, o_ref):
        x = x_ref[...]
        h = (x * _rstd(x) * g_ref[...]).astype(BF16)
        h_ref[...] = h
        if blocks:
            n = w.shape[2]
            for j in range(w.shape[0]):
                o_ref[:, j * n:(j + 1) * n] = jnp.dot(h, w_ref[j], preferred_element_type=F32).astype(BF16)
        else:
            o_ref[...] = lax.dot_general(h, w_ref[...], (NT, ((), ())), preferred_element_type=F32).astype(BF16)

    row = _spec((TM, D_MODEL), lambda i: (i, 0))
    return pl.pallas_call(
        body, name=name, grid=(N_TM,), in_specs=[row, _resident((1, D_MODEL)), _resident(w.shape)],
        out_specs=[row, _spec((TM, n_out), lambda i: (i, 0))], out_shape=[_act(), _act((SEQ, n_out))],
        compiler_params=_params(("parallel",)),
    )(x, gamma, w)


def _rows_matmul(name, a, w, dims, x=None):
    k = a.shape[1]
    n = w.shape[1] if dims == NN else w.shape[0]
    row = _spec((TM, n), lambda i: (i, 0))
    return _matmul(name, a, _spec((TM, k), lambda i: (i, 0)), w, _resident(w.shape), dims, (N_TM,),
                   _act((SEQ, n), F32 if x is not None else BF16), row, res=x, res_spec=row if x is not None else None)


def _sum_blocks_nn(name, a_blocks, w_blocks, x=None, norm=None, swap=()):
    nb, _, n = a_blocks.shape
    row = _spec((TM, D_MODEL), lambda i: (i, 0))
    return _matmul(name, a_blocks, _spec((nb, TM, n), lambda i: (0, i, 0)), w_blocks, _resident((nb, n, D_MODEL)),
                   NN, (N_TM,), _act(dtype=F32), row, k_blocks=nb, res=x, res_spec=row if x is not None else None, norm=norm, swap=swap)


def _sum_cols_nt(name, d, w_blocks, norm=None, swap=()):
    nb, _, n = w_blocks.shape
    return _matmul(name, d, _spec((TM, nb * n), lambda i: (i, 0)), w_blocks, _resident((nb, D_MODEL, n)), NT,
                   (N_TM,), _act(dtype=F32), _spec((TM, D_MODEL), lambda i: (i, 0)), k_blocks=nb, a_block_cols=n, norm=norm, swap=swap)


def _wide_nn(name, d, wt, x=None, norm=None, swap=()):
    n = wt.shape[0]
    row = _spec((TM, D_MODEL), lambda i: (i, 0))
    return _matmul(name, d, _spec((TM, n), lambda i: (i, 0)), wt, _resident((n, D_MODEL)), NN, (N_TM,),
                   _act(dtype=F32), row, res=x, res_spec=row if x is not None else None, norm=norm, swap=swap)


def _wgrad_halves_tn(name, d, n_tile, h):
    _, _, n = d.shape
    return _matmul(name, d, _spec((None, SEQ, n_tile), lambda p, j: (p, 0, j)), h, _resident((SEQ, D_MODEL)), TN,
                   (2, n // n_tile), _act((2, n, D_MODEL)), _spec((None, n_tile, D_MODEL), lambda p, j: (p, j, 0)))


def _wgrad_cols_tn(name, d, n_tile, h):
    n = d.shape[1]
    return _matmul(name, d, _spec((SEQ, n_tile), lambda j: (0, j)), h, _resident((SEQ, D_MODEL)), TN,
                   (n // n_tile,), _act((n, D_MODEL)), _spec((n_tile, D_MODEL), lambda j: (j, 0)))


def _wgrad_cols_transposed_tn(name, h, d, n_tile):
    nb = d.shape[1] // n_tile
    return _matmul(name, d, _spec((SEQ, n_tile), lambda j: (0, j)), h, _resident((SEQ, D_MODEL)), TN, (nb,),
                   _act((nb, D_MODEL, n_tile)), _spec((None, D_MODEL, n_tile), lambda j: (j, 0, 0)), transpose_out=True)


def _rstd(x):
    return lax.rsqrt(jnp.mean(x * x, axis=-1, keepdims=True) + RMS_EPS)


def _norm_bwd_rows(x, gamma, dh):
    r = _rstd(x)
    xh = x * r
    dxh = dh * gamma
    dx = r * (dxh - xh * jnp.mean(dxh * xh, axis=-1, keepdims=True))
    return dx, jnp.sum(dh * xh, axis=0, keepdims=True)


def _down_loss_head(a, w_down, x_in, gamma, target):
    def body(a_ref, w_ref, x_ref, g_ref, t_ref, loss_ref, dx_ref, dx16_ref, dg_ref):
        gamma = g_ref[...]
        dg, part = 0.0, 0.0
        rows_per_part = TM // NORM_PARTS
        for rows in (slice(i * rows_per_part, (i + 1) * rows_per_part) for i in range(NORM_PARTS)):
            x = x_ref[rows, :] + jnp.dot(a_ref[rows, :], w_ref[...], preferred_element_type=F32)
            err = x * _rstd(x) * gamma - t_ref[rows, :]
            dy = err * (1.0 / D_MODEL)
            dx, dg_rows = _norm_bwd_rows(x, gamma, dy)
            dx_ref[rows, :] = dx
            dx16_ref[rows, :] = dx.astype(BF16)
            dg = dg + dg_rows
            part = part + 0.5 * jnp.sum(jnp.sum(err * err, axis=-1, keepdims=True) * (1.0 / D_MODEL), axis=0, keepdims=True)
        part = jnp.broadcast_to(part, loss_ref.shape)

        @pl.when(pl.program_id(0) == 0)
        def _():
            dg_ref[...] = dg
            loss_ref[...] = part

        @pl.when(pl.program_id(0) > 0)
        def _():
            dg_ref[...] += dg
            loss_ref[...] += part

    row = _spec((TM, D_MODEL), lambda i: (i, 0))
    vec = _spec((1, D_MODEL), lambda i: (0, 0))
    return pl.pallas_call(
        body, name="ffn1_down_loss_head", grid=(N_TM,),
        in_specs=[_spec((TM, D_FF), lambda i: (i, 0)), _resident((D_FF, D_MODEL)), row, vec, row],
        out_specs=[_spec((1, 128), lambda i: (0, 0)), row, row, vec],
        out_shape=[jax.ShapeDtypeStruct((1, 128), F32), _act(dtype=F32), _act(), jax.ShapeDtypeStruct((1, D_MODEL), F32)],
        compiler_params=_params(("arbitrary",)),
    )(a, w_down, x_in, gamma, target)


def _sigmoid(x):
    return 1.0 / (1.0 + jnp.exp(-x))


def _rows(ref, c):
    return ref[pl.ds(pl.multiple_of(c * ROW_CHUNK, ROW_CHUNK), ROW_CHUNK), :].astype(F32)


def _rows_before(ref, c):
    start = pl.multiple_of(jnp.maximum(c * ROW_CHUNK - HALO, 0), HALO)
    rows = ref[pl.ds(start, HALO), :].astype(F32)
    return jnp.where(c > 0, rows, 0.0)


def _rows_after(ref, c, n_chunks):
    start = pl.multiple_of(jnp.minimum((c + 1) * ROW_CHUNK, SEQ - HALO), HALO)
    rows = ref[pl.ds(start, HALO), :].astype(F32)
    return jnp.where(c < n_chunks - 1, rows, 0.0)


def _shift_down(z, before, n):
    return pltpu.roll(jnp.concatenate([before, z], axis=0), n, 0)[before.shape[0]:]


def _shift_up(z, after, n):
    rows = z.shape[0]
    return pltpu.roll(jnp.concatenate([z, after], axis=0), rows + HALO - n, 0)[:rows]


def _conv_rows(z, before, w):
    z1 = _shift_down(z, before, 1)
    z2 = _shift_down(z, before, 2)
    return w[2:3, :] * z + w[1:2, :] * z1 + w[0:1, :] * z2, z1, z2


def _conv_t_rows(dy, after, w):
    return w[2:3, :] * dy + w[1:2, :] * _shift_up(dy, after, 1) + w[0:1, :] * _shift_up(dy, after, 2)


N_ROW_CHUNKS = SEQ // ROW_CHUNK


FF_COLS = 256
N_FF_COLS = D_FF // FF_COLS


def _ffn_mid_bwd(name, gu, conv_w, da):
    def body(gu_ref, w_ref, da_ref, dgu_ref, dw_ref, dgc_ref):
        w = w_ref[...]

        def first(c, acc):
            g = _rows(gu_ref.at[0], c)
            u = _rows(gu_ref.at[1], c)
            d = _rows(da_ref, c)
            gc, g1, g2 = _conv_rows(g, _rows_before(gu_ref.at[0], c), w)
            sg = _sigmoid(gc)
            rows = pl.ds(pl.multiple_of(c * ROW_CHUNK, ROW_CHUNK), ROW_CHUNK)
            silu = gc * sg
            dgu_ref[1, rows, :] = (d * silu).astype(BF16)
            dgc = d * u * (sg + silu * (1.0 - sg))
            dgc_ref[rows, :] = dgc
            return (acc[0] + jnp.sum(dgc * g2, axis=0, keepdims=True), acc[1] + jnp.sum(dgc * g1, axis=0, keepdims=True),
                    acc[2] + jnp.sum(dgc * g, axis=0, keepdims=True))

        zero = jnp.zeros((1, FF_COLS), F32)
        acc = lax.fori_loop(0, N_ROW_CHUNKS, first, (zero, zero, zero))
        for r in range(3):
            dw_ref[r:r + 1, :] = acc[r]

        def second(c, carry):
            dgc = _rows(dgc_ref, c)
            dg = _conv_t_rows(dgc, _rows_after(dgc_ref, c, N_ROW_CHUNKS), w)
            dgu_ref[0, pl.ds(pl.multiple_of(c * ROW_CHUNK, ROW_CHUNK), ROW_CHUNK), :] = dg.astype(BF16)
            return carry

        lax.fori_loop(0, N_ROW_CHUNKS, second, 0)

    pair = _spec((2, SEQ, FF_COLS), lambda j: (0, 0, j))
    wspec = _spec((3, FF_COLS), lambda j: (0, j))
    return pl.pallas_call(
        body, name=name, grid=(N_FF_COLS,), in_specs=[pair, wspec, _spec((SEQ, FF_COLS), lambda j: (0, j))],
        out_specs=[pair, wspec], out_shape=[_act((2, SEQ, D_FF)), jax.ShapeDtypeStruct((3, D_FF), F32)],
        scratch_shapes=[pltpu.VMEM((SEQ, FF_COLS), F32)],
        compiler_params=_params(("parallel",)),
    )(gu, conv_w, da)


SC_COLS = 256
N_SC = D_MODEL // SC_COLS


def _sc_specs():
    return [_spec((SEQ, SC_COLS), lambda j, part=part: (0, part * N_SC + j)) for part in range(3)]


def _sc_mid_fwd(p, conv_w):
    def body(b_ref, c_ref, h_ref, w_ref, y_ref):
        w = w_ref[...]

        def chunk(c, carry):
            z = _rows(c_ref, c) * _rows(h_ref, c)
            before = _rows_before(c_ref, c) * _rows_before(h_ref, c)
            zc, _, _ = _conv_rows(z, before, w)
            y_ref[pl.ds(pl.multiple_of(c * ROW_CHUNK, ROW_CHUNK), ROW_CHUNK), :] = (_rows(b_ref, c) * zc).astype(BF16)
            return carry

        lax.fori_loop(0, N_ROW_CHUNKS, chunk, 0)

    col = _spec((SEQ, SC_COLS), lambda j: (0, j))
    return pl.pallas_call(
        body, name="sc_mid_fwd", grid=(N_SC,), in_specs=_sc_specs() + [_spec((3, SC_COLS), lambda j: (0, j))], out_specs=col,
        out_shape=jax.ShapeDtypeStruct((SEQ, D_MODEL), BF16), compiler_params=_params(("parallel",)),
    )(p, p, p, conv_w)


def _sc_mid_bwd(p, conv_w, dy):
    def body(b_ref, c_ref, h_ref, w_ref, dy_ref, db_ref, dc_ref, dh_ref, dw_ref, dzc_ref):
        w = w_ref[...]

        def first(c, acc):
            z = _rows(c_ref, c) * _rows(h_ref, c)
            before = _rows_before(c_ref, c) * _rows_before(h_ref, c)
            zc, z1, z2 = _conv_rows(z, before, w)
            d = _rows(dy_ref, c)
            rows = pl.ds(pl.multiple_of(c * ROW_CHUNK, ROW_CHUNK), ROW_CHUNK)
            db_ref[rows, :] = (d * zc).astype(BF16)
            dzc = d * _rows(b_ref, c)
            dzc_ref[rows, :] = dzc
            return (acc[0] + jnp.sum(dzc * z2, axis=0, keepdims=True), acc[1] + jnp.sum(dzc * z1, axis=0, keepdims=True),
                    acc[2] + jnp.sum(dzc * z, axis=0, keepdims=True))

        zero = jnp.zeros((1, SC_COLS), F32)
        acc = lax.fori_loop(0, N_ROW_CHUNKS, first, (zero, zero, zero))
        for r in range(3):
            dw_ref[r:r + 1, :] = acc[r]

        def second(c, carry):
            dz = _conv_t_rows(_rows(dzc_ref, c), _rows_after(dzc_ref, c, N_ROW_CHUNKS), w)
            rows = pl.ds(pl.multiple_of(c * ROW_CHUNK, ROW_CHUNK), ROW_CHUNK)
            dc_ref[rows, :] = (dz * _rows(h_ref, c)).astype(BF16)
            dh_ref[rows, :] = (dz * _rows(c_ref, c)).astype(BF16)
            return carry

        lax.fori_loop(0, N_ROW_CHUNKS, second, 0)

    col = _spec((SEQ, SC_COLS), lambda j: (0, j))
    wspec = _spec((3, SC_COLS), lambda j: (0, j))
    act = jax.ShapeDtypeStruct((SEQ, D_MODEL), BF16)
    return pl.pallas_call(
        body, name="sc_mid_bwd", grid=(N_SC,), in_specs=_sc_specs() + [wspec, col], out_specs=[col, col, col, wspec],
        out_shape=[act, act, act, jax.ShapeDtypeStruct((3, D_MODEL), F32)],
        scratch_shapes=[pltpu.VMEM((SEQ, SC_COLS), F32)], compiler_params=_params(("parallel",)),
    )(p, p, p, conv_w, dy)


GLA_GROUP = 4
GLA_ROWS = GLA_GROUP * CHUNK
N_GROUPS = N_CHUNKS // GLA_GROUP
Q0, K0, V0, R0, G0 = 0, KEY_DIM, 2 * KEY_DIM, 2 * KEY_DIM + VALUE_DIM, 2 * KEY_DIM + 2 * VALUE_DIM


def _tri(strict):
    r = lax.broadcasted_iota(jnp.int32, (CHUNK, CHUNK), 0)
    c = lax.broadcasted_iota(jnp.int32, (CHUNK, CHUNK), 1)
    return jnp.where(c < r if strict else c <= r, 1.0, 0.0).astype(F32)


def _cumsum_rows(tri, x):
    tri = tri.astype(BF16)
    total = None
    for _ in range(3):
        term = x.astype(BF16)
        x = x - term.astype(F32)
        product = jnp.dot(tri, term, preferred_element_type=F32)
        total = product if total is None else total + product
    return total


def _gate_logits(gl, wgu, b_gate):
    return jnp.dot(gl, wgu, preferred_element_type=F32) + b_gate


def _log_decay(logits):
    return (jnp.minimum(logits, 0.0) - jnp.log(1.0 + jnp.exp(-jnp.abs(logits)))) * (1.0 / GATE_NORMALIZER)


def _head(x, h, width):
    return x[:, h * width:(h + 1) * width]


def _gla_fwd(proj, wgu, b_gate, gn):
    def body(p_ref, wgu_ref, b_ref, gn_ref, o_ref, og_ref, st_ref, state):
        @pl.when(pl.program_id(0) == 0)
        def _():
            state[...] = jnp.zeros_like(state)

        tri = _tri(False)
        la = _log_decay(_gate_logits(p_ref[:, G0:G0 + GATE_PAD], wgu_ref[...], b_ref[...]))
        decays = []
        for c in range(GLA_GROUP):
            rows = slice(c * CHUNK, (c + 1) * CHUNK)
            cum = _cumsum_rows(tri, la[rows])
            tot = cum[CHUNK - 1:CHUNK, :]
            kd = (p_ref[rows, K0:K0 + KEY_DIM].astype(F32) * jnp.exp(tot - cum)).astype(BF16)
            decays.append(jnp.exp(tot))
            v = p_ref[rows, V0:V0 + VALUE_DIM]
            for h in range(GLA_HEADS):
                st_ref[c, h] = lax.dot_general(
                    _head(v, h, HEAD_V), _head(kd, h, HEAD_K), (TN, ((), ())), preferred_element_type=F32)
        for c in range(GLA_GROUP):
            for h in range(GLA_HEADS):
                s = state[h] * _head(decays[c], h, HEAD_K) + st_ref[c, h]
                state[h] = s
                st_ref[c, h] = s
        for c in range(GLA_GROUP):
            rows = slice(c * CHUNK, (c + 1) * CHUNK)
            q = (p_ref[rows, Q0:Q0 + KEY_DIM].astype(F32) * (HEAD_K ** -0.5)).astype(BF16)
            for h in range(GLA_HEADS):
                o_ref[rows, h * HEAD_V:(h + 1) * HEAD_V] = lax.dot_general(
                    _head(q, h, HEAD_K), st_ref[c, h].astype(BF16), (NT, ((), ())), preferred_element_type=F32)
        r = p_ref[:, R0:R0 + VALUE_DIM].astype(F32)
        gate = r * _sigmoid(r) * gn_ref[...]
        for h in range(GLA_HEADS):
            cols = slice(h * HEAD_V, (h + 1) * HEAD_V)
            o = o_ref[:, cols]
            og_ref[:, cols] = (o * _rstd(o) * gate[:, cols]).astype(BF16)

    rows = _spec((GLA_ROWS, VALUE_DIM), lambda i: (i, 0))
    const = lambda shape: _spec(shape, lambda i: (0,) * len(shape))
    return pl.pallas_call(
        body, name="gla_fwd", grid=(N_GROUPS,),
        in_specs=[_spec((GLA_ROWS, PROJ_A_PAD), lambda i: (i, 0)), const((GATE_PAD, KEY_DIM)), const((1, KEY_DIM)),
                  const((1, VALUE_DIM))],
        out_specs=[rows, rows, _spec((GLA_GROUP, GLA_HEADS, HEAD_V, HEAD_K), lambda i: (i, 0, 0, 0))],
        out_shape=[jax.ShapeDtypeStruct((SEQ, VALUE_DIM), F32), jax.ShapeDtypeStruct((SEQ, VALUE_DIM), BF16),
                   jax.ShapeDtypeStruct((N_CHUNKS, GLA_HEADS, HEAD_V, HEAD_K), F32)],
        scratch_shapes=[pltpu.VMEM((GLA_HEADS, HEAD_V, HEAD_K), F32)], compiler_params=_params(("arbitrary",)),
    )(proj, wgu, b_gate, gn)


def _gla_bwd(proj, wgu, b_gate, gn, o, states, dog):
    last = N_GROUPS - 1

    def body(p_ref, wgu_ref, b_ref, gn_ref, o_ref, st_ref, stp_ref, dog_ref, dp_ref, dwgu_ref, db_ref, dgn_ref, carry, do_buf,
             g_buf):
        step = pl.program_id(0)

        @pl.when(step == 0)
        def _():
            carry[...] = jnp.zeros_like(carry)

        r = p_ref[:, R0:R0 + VALUE_DIM].astype(F32)
        sr = _sigmoid(r)
        silu = r * sr
        gn_row = gn_ref[...]
        dog_rows = dog_ref[...].astype(F32)
        dn = dog_rows * silu
        dgn_cols = []
        for h in range(GLA_HEADS):
            cols = slice(h * HEAD_V, (h + 1) * HEAD_V)
            oh = o_ref[:, cols]
            rs = _rstd(oh)
            ohat = oh * rs
            dn_h = dn[:, cols]
            dgn_cols.append(jnp.sum(dn_h * ohat, axis=0, keepdims=True))
            dohat = dn_h * gn_row[:, cols]
            do_buf[:, cols] = rs * (dohat - ohat * jnp.mean(dohat * ohat, axis=-1, keepdims=True))
            n_h = ohat * gn_row[:, cols]
            dp_ref[:, R0 + h * HEAD_V:R0 + (h + 1) * HEAD_V] = (
                dog_rows[:, cols] * n_h * (sr[:, cols] * (1.0 + r[:, cols] * (1.0 - sr[:, cols])))).astype(BF16)
        dgn = jnp.concatenate(dgn_cols, axis=1)

        tri = _tri(False)
        tri_strict = _tri(True)
        gl = p_ref[:, G0:G0 + GATE_PAD]
        logits = _gate_logits(gl, wgu_ref[...], b_ref[...])
        la = _log_decay(logits)
        fades, kds, decays = [], [], []
        for c in range(GLA_GROUP):
            rows = slice(c * CHUNK, (c + 1) * CHUNK)
            cum = _cumsum_rows(tri, la[rows])
            tot = cum[CHUNK - 1:CHUNK, :]
            fades.append(jnp.exp(tot - cum))
            kds.append(p_ref[rows, K0:K0 + KEY_DIM].astype(F32) * fades[c])
            decays.append(jnp.exp(tot))
            q = (p_ref[rows, Q0:Q0 + KEY_DIM].astype(F32) * (HEAD_K ** -0.5)).astype(BF16)
            do = do_buf[rows, :].astype(BF16)
            for h in range(GLA_HEADS):
                do_h = _head(do, h, HEAD_V)
                dq = jnp.dot(do_h, st_ref[c, h].astype(BF16), preferred_element_type=F32) * (HEAD_K ** -0.5)
                dp_ref[rows, Q0 + h * HEAD_K:Q0 + (h + 1) * HEAD_K] = dq.astype(BF16)
                g_buf[c, h] = lax.dot_general(do_h, _head(q, h, HEAD_K), (TN, ((), ())), preferred_element_type=F32)
        for c in reversed(range(GLA_GROUP)):
            for h in range(GLA_HEADS):
                g = carry[h] + g_buf[c, h]
                g_buf[c, h] = g
                carry[h] = g * _head(decays[c], h, HEAD_K)
        dlogit_rows = []
        for c in range(GLA_GROUP):
            rows = slice(c * CHUNK, (c + 1) * CHUNK)
            v = p_ref[rows, V0:V0 + VALUE_DIM]
            kd = kds[c].astype(BF16)
            dkd_cols, ddecay_cols = [], []
            for h in range(GLA_HEADS):
                g = g_buf[c, h]
                g16 = g.astype(BF16)
                dkd_cols.append(jnp.dot(_head(v, h, HEAD_V), g16, preferred_element_type=F32))
                dv = lax.dot_general(_head(kd, h, HEAD_K), g16, (NT, ((), ())), preferred_element_type=F32)
                dp_ref[rows, V0 + h * HEAD_V:V0 + (h + 1) * HEAD_V] = dv.astype(BF16)
                if c > 0:
                    s_prev = st_ref[c - 1, h]
                else:
                    s_prev = jnp.where(step < last, stp_ref[0, h], 0.0)
                ddecay_cols.append(jnp.sum(g * s_prev, axis=0, keepdims=True))
            dkd = jnp.concatenate(dkd_cols, axis=1)
            ddecay = jnp.concatenate(ddecay_cols, axis=1)
            dp_ref[rows, K0:K0 + KEY_DIM] = (dkd * fades[c]).astype(BF16)
            e = dkd * kds[c]
            dla = ddecay * decays[c] + _cumsum_rows(tri_strict, e)
            dlogit_rows.append(dla * (1.0 / GATE_NORMALIZER) * (1.0 - _sigmoid(logits[rows])))
        dlogit = jnp.concatenate(dlogit_rows, axis=0)
        dlogit16 = dlogit.astype(BF16)
        dp_ref[:, G0:G0 + GATE_PAD] = lax.dot_general(
            dlogit16, wgu_ref[...], (NT, ((), ())), preferred_element_type=F32).astype(BF16)
        dwgu = lax.dot_general(gl, dlogit16, (TN, ((), ())), preferred_element_type=F32)
        db = jnp.sum(dlogit, axis=0, keepdims=True)

        @pl.when(step == 0)
        def _():
            dwgu_ref[...] = dwgu
            db_ref[...] = db
            dgn_ref[...] = dgn

        @pl.when(step > 0)
        def _():
            dwgu_ref[...] += dwgu
            db_ref[...] += db
            dgn_ref[...] += dgn

    rev = lambda i: (last - i, 0)
    rows = _spec((GLA_ROWS, VALUE_DIM), rev)
    const = lambda shape: _spec(shape, lambda i: (0,) * len(shape))
    st_shape = (GLA_HEADS, HEAD_V, HEAD_K)
    return pl.pallas_call(
        body, name="gla_bwd", grid=(N_GROUPS,),
        in_specs=[_spec((GLA_ROWS, PROJ_A_PAD), rev), const((GATE_PAD, KEY_DIM)), const((1, KEY_DIM)), const((1, VALUE_DIM)),
                  rows, _spec((GLA_GROUP,) + st_shape, lambda i: (last - i, 0, 0, 0)),
                  _spec((1,) + st_shape, lambda i: (jnp.maximum((last - i) * GLA_GROUP - 1, 0), 0, 0, 0)), rows],
        out_specs=[_spec((GLA_ROWS, PROJ_A_PAD), rev), const((GATE_PAD, KEY_DIM)), const((1, KEY_DIM)), const((1, VALUE_DIM))],
        out_shape=[jax.ShapeDtypeStruct((SEQ, PROJ_A_PAD), BF16), jax.ShapeDtypeStruct((GATE_PAD, KEY_DIM), F32),
                   jax.ShapeDtypeStruct((1, KEY_DIM), F32), jax.ShapeDtypeStruct((1, VALUE_DIM), F32)],
        scratch_shapes=[pltpu.VMEM(st_shape, F32), pltpu.VMEM((GLA_ROWS, VALUE_DIM), F32), pltpu.VMEM((GLA_GROUP,) + st_shape, F32)],
        compiler_params=_params(("arbitrary",)),
    )(proj, wgu, b_gate, gn, o, states, states, dog)


WGRAD_FF_TILE = D_FF // 2


CARRY_ROWS = 8
UP_ROWS = 512


def _ffn_up_mid(name, x, gamma, w_up_t, conv_w):
    def body(x_ref, g_ref, w_ref, c_ref, h_ref, gu_ref, a_ref, carry):
        @pl.when(pl.program_id(0) == 0)
        def _():
            carry[...] = jnp.zeros_like(carry)

        x_tile = x_ref[...]
        h_tile = (x_tile * _rstd(x_tile) * g_ref[...]).astype(BF16)
        h_ref[...] = h_tile
        for k in range(N_FF_COLS):
            cols = slice(k * FF_COLS, (k + 1) * FF_COLS)
            g, u = (lax.dot_general(h_tile, w_ref[p, cols, :], (NT, ((), ())), preferred_element_type=F32).astype(BF16)
                    for p in range(2))
            gu_ref[0, :, cols] = g
            gu_ref[1, :, cols] = u
            g = g.astype(F32)
            w = c_ref[:, cols]
            before = carry[:, cols]
            gc = w[2:3, :] * g + w[1:2, :] * _shift_down(g, before, 1) + w[0:1, :] * _shift_down(g, before, 2)
            a_ref[:, cols] = (gc * _sigmoid(gc) * u.astype(F32)).astype(BF16)
            carry[:, cols] = g[UP_ROWS - CARRY_ROWS:, :]

    row = _spec((UP_ROWS, D_MODEL), lambda i: (i, 0))
    return pl.pallas_call(
        body, name=name, grid=(SEQ // UP_ROWS,),
        in_specs=[row, _resident((1, D_MODEL)), _resident((2, D_FF, D_MODEL)), _resident((3, D_FF))],
        out_specs=[row, _spec((2, UP_ROWS, D_FF), lambda i: (0, i, 0)), _spec((UP_ROWS, D_FF), lambda i: (i, 0))],
        out_shape=[_act(), _act((2, SEQ, D_FF)), _act((SEQ, D_FF))], scratch_shapes=[pltpu.VMEM((CARRY_ROWS, D_FF), F32)],
        compiler_params=_params(("arbitrary",)),
    )(x, gamma, w_up_t, conv_w)


def _ffn_fwd(tag, x, gamma, w_up_t, conv_w, w_down):
    h, gu, a = _ffn_up_mid(f"ffn{tag}_up_mid", x, gamma, w_up_t, conv_w)
    return _rows_matmul(f"ffn{tag}_down", a, w_down, NN, x), (h, gu, a)


def _owner_blocks(d, rows=None):
    if rows is not None:
        d = d[:rows]
    return d.reshape((N_DEV, -1) + d.shape[-1:])


def _ffn_bwd(tag, x, gamma, w_up_t, conv_w, w_down, saved, dx, dx16, swap):
    h, gu, a = saved
    da = _rows_matmul(f"ffn{tag}_da", dx16, w_down, NT)
    d_w_down = _owner_blocks(_wgrad_cols_tn(f"ffn{tag}_dwdown", a, WGRAD_FF_TILE, dx16))
    dgu, d_conv = _ffn_mid_bwd(f"ffn{tag}_mid_bwd", gu, conv_w, da)
    d_w_up_t = _owner_blocks(_wgrad_halves_tn(f"ffn{tag}_dwup", dgu, WGRAD_FF_TILE, h))
    parts = (d_w_up_t, d_w_down)
    dx, dx16, d_gamma, *received = _sum_blocks_nn(
        f"ffn{tag}_dh", dgu, w_up_t, norm=(x, gamma, dx), swap=parts if swap else ())
    return dx, dx16, d_gamma, d_conv, parts, received


def _local_step(x, target, w, fetch=None, emit=None):
    if fetch is None:
        local = dict(a=(w.get("a_w_in"), w.get("a_w_out")), b=(w.get("b_w_in"), w.get("b_w_out")))
        for layer in range(2):
            local[f"f{layer}"] = (w["f_w_up"][layer], w["f_w_down"][layer]) if "f_w_up" in w else None
        fetch = lambda group, after: local[group]
    swap = emit is not None
    if emit is None:
        emit = lambda group, parts, received, dx: dx
    f_norm = (w["f_norm"][0:1], w["f_norm"][1:2])

    x0 = x
    a_w_in, a_w_out = fetch("a", x0)
    h0, proj = _norm_proj("a_in", x0, w["a_norm"], a_w_in)
    o, og, states = _gla_fwd(proj, w["a_w_gate_up"], w["a_b_gate"], w["a_gn"])
    x1 = _rows_matmul("a_out", og, a_w_out, NN, x0)
    up0, down0 = fetch("f0", x1)
    x2, ffn0 = _ffn_fwd(0, x1, f_norm[0], up0, w["f_conv"][0], down0)
    b_w_in, b_w_out = fetch("b", x2)
    h2, p = _norm_proj("b_in", x2, w["b_norm"], b_w_in)
    y = _sc_mid_fwd(p, w["b_conv"])
    x3 = _rows_matmul("b_out", y, b_w_out, NN, x2)
    up1, down1 = fetch("f1", x3)
    ffn1 = _ffn_up_mid("ffn1_up_mid", x3, f_norm[1], up1, w["f_conv"][1])
    loss, dx, dx16, d_final_norm = _down_loss_head(ffn1[2], down1, x3, w["final_norm"], target)

    dx, dx16, d_f_norm1, d_fconv1, parts_f1, got = _ffn_bwd(
        1, x3, f_norm[1], up1, w["f_conv"][1], down1, ffn1, dx, dx16, swap)
    dx16 = emit("f1", parts_f1, got, dx16)

    dy = _rows_matmul("b_dy", dx16, b_w_out, NT)
    d_b_w_out = _owner_blocks(_wgrad_cols_tn("b_dwout", y, OUT_TILE, dx16))
    db, dc, dhh, d_b_conv = _sc_mid_bwd(p, w["b_conv"], dy)
    dp = jnp.concatenate([db, dc, dhh], axis=1)
    parts_b = (_wgrad_cols_transposed_tn("b_dwin", h2, dp, B_SHARD), d_b_w_out)
    dx, dx16, d_b_norm, *got = _sum_cols_nt("b_dh", dp, b_w_in, norm=(x2, w["b_norm"], dx), swap=parts_b if swap else ())
    dx16 = emit("b", parts_b, got, dx16)

    dx, dx16, d_f_norm0, d_fconv0, parts_f0, got = _ffn_bwd(
        0, x1, f_norm[0], up0, w["f_conv"][0], down0, ffn0, dx, dx16, swap)
    dx16 = emit("f0", parts_f0, got, dx16)

    dog = _rows_matmul("a_dog", dx16, a_w_out, NT)
    d_a_w_out = _owner_blocks(_wgrad_cols_tn("a_dwout", og, OUT_TILE, dx16))
    dproj, d_wgu, d_b_gate, d_gn = _gla_bwd(proj, w["a_w_gate_up"], w["a_b_gate"], w["a_gn"], o, states, dog)
    parts_a = (_owner_blocks(_wgrad_cols_tn("a_dwin", dproj, PA_TILE, h0), PROJ_A), d_a_w_out)
    got = _pair_swap("pair_swap_a", parts_a) if swap else ()
    dproj = emit("a", parts_a, got, dproj)
    dx, _, d_a_norm = _wide_nn("a_dh", dproj, a_w_in, norm=(x0, w["a_norm"], dx))

    grads = dict(
        a_norm=d_a_norm, a_w_in=parts_a[0], a_w_gate_up=d_wgu, a_b_gate=d_b_gate, a_gn=d_gn, a_w_out=parts_a[1],
        b_norm=d_b_norm, b_w_in=parts_b[0], b_conv=d_b_conv, b_w_out=parts_b[1],
        f_norm=(d_f_norm0, d_f_norm1), f_w_up=(parts_f0[0], parts_f1[0]), f_conv=(d_fconv0, d_fconv1),
        f_w_down=(parts_f0[1], parts_f1[1]), final_norm=d_final_norm)
    grads["loss"] = loss
    return dx, grads


MESH_ID = pl.DeviceIdType.MESH
ANY = pl.BlockSpec(memory_space=pl.ANY)


def _position():
    return lax.axis_index("x"), lax.axis_index("y"), lax.axis_index("c")


def _slot(px, py, pc):
    return 4 * px + 2 * py + pc


GATHER_COPIES = 8
HALF_ROWS = 16


def _gather_copies(src, out, send_sems, recv_sems, local_sems):
    n = len(src)
    to_sibling, to_x, to_y, x_on_to_y, y_on_to_x, x_to_sibling, y_to_sibling, diagonal_to_sibling = range(GATHER_COPIES)
    x, y, c = _position()
    me, sibling = (x, y, c), (x, y, 1 - c)
    x_side, y_side, diagonal = (1 - x, y), (x, 1 - y), (1 - x, 1 - y)

    def rows_of(t, half):
        rows = src[t].shape[0]
        half_rows = rows // 2 // HALF_ROWS * HALF_ROWS
        return (pl.ds(0, rows), pl.ds(0, half_rows), pl.ds(half_rows, rows - half_rows))[half]

    def copy(t, j, block, to, half=0, from_input=False):
        dst = out[t].at[_slot(*block), rows_of(t, half)]
        return pltpu.make_async_remote_copy(
            src_ref=src[t] if from_input else dst, dst_ref=dst, send_sem=send_sems.at[GATHER_COPIES * t + j],
            recv_sem=recv_sems.at[GATHER_COPIES * t + j], device_id=to, device_id_type=MESH_ID)

    mine = [pltpu.make_async_copy(src[t], out[t].at[_slot(*me)], local_sems.at[t]) for t in range(n)]
    for cp in mine:
        cp.start()
    sent = []

    def start(cp):
        cp.start()
        sent.append(cp)

    for t in range(n):
        start(copy(t, to_sibling, me, sibling, from_input=True))
        start(copy(t, to_x, me, (*x_side, c), from_input=True))
        start(copy(t, to_y, me, (*y_side, c), from_input=True))
    for t in range(n):
        copy(t, to_x, (*x_side, c), me).wait_recv()
        start(copy(t, x_on_to_y, (*x_side, c), (*y_side, c), half=1))
        start(copy(t, x_to_sibling, (*x_side, c), sibling))
        copy(t, to_y, (*y_side, c), me).wait_recv()
        start(copy(t, y_on_to_x, (*y_side, c), (*x_side, c), half=2))
        start(copy(t, y_to_sibling, (*y_side, c), sibling))
    for t in range(n):
        copy(t, x_on_to_y, (*diagonal, c), me, half=1).wait_recv()
        copy(t, y_on_to_x, (*diagonal, c), me, half=2).wait_recv()
        start(copy(t, diagonal_to_sibling, (*diagonal, c), sibling))
    for t in range(n):
        copy(t, to_sibling, sibling, me).wait_recv()
        for j, chip in ((x_to_sibling, x_side), (y_to_sibling, y_side), (diagonal_to_sibling, diagonal)):
            copy(t, j, (*chip, 1 - c), me).wait_recv()
    for cp in sent:
        cp.wait_send()
    for cp in mine:
        cp.wait()


def _all_gather(name, collective_id, shards):
    n = len(shards)

    def body(*refs):
        _handshake(SIBLING_AND_NEIGHBOURS)
        _gather_copies(refs[:n], refs[n:2 * n], *refs[2 * n:])

    sems = pltpu.SemaphoreType.DMA((GATHER_COPIES * n,))
    return pl.pallas_call(
        body, name=name, in_specs=[ANY] * n, out_specs=[ANY] * n,
        out_shape=[jax.ShapeDtypeStruct((N_DEV,) + s.shape, s.dtype) for s in shards],
        scratch_shapes=[sems, sems, pltpu.SemaphoreType.DMA((n,))],
        compiler_params=pltpu.CompilerParams(collective_id=collective_id),
    )(*shards)


SIBLING_AND_NEIGHBOURS = (1, 2, 4)
SAME_CORE = (2, 4, 6)


def _flip(x, y, c, k):
    return x ^ (k >> 2), y ^ ((k >> 1) & 1), c ^ (k & 1)


N_CHIPS = N_DEV // 2


def _chip(px, py):
    return 2 * px + py


def _pair_swap(name, parts):
    n = len(parts)

    def body(*refs):
        sibling = (lax.axis_index("x"), lax.axis_index("y"), 1 - lax.axis_index("c"))
        barrier = pltpu.get_barrier_semaphore()
        pl.semaphore_signal(barrier, inc=1, device_id=sibling, device_id_type=pl.DeviceIdType.MESH)
        pl.semaphore_wait(barrier, 1)
        copies = _pair_copies(refs[:n], refs[n:2 * n], *refs[2 * n:])
        for send, _ in copies:
            send.start()
        for send, arrival in copies:
            arrival.wait_recv()
            send.wait_send()

    sems = pltpu.SemaphoreType.DMA((n, N_DEV // 2))
    any_space = pl.BlockSpec(memory_space=pl.ANY)
    return pl.pallas_call(
        body, name=name, in_specs=[any_space] * n, out_specs=[any_space] * n,
        out_shape=[jax.ShapeDtypeStruct((N_DEV // 2,) + p.shape[1:], p.dtype) for p in parts], scratch_shapes=[sems, sems],
        compiler_params=pltpu.CompilerParams(collective_id=SWAP_IDS[name]),
    )(*parts)


def _pair_copies(parts, received, send_sems, recv_sems):
    x, y, c = lax.axis_index("x"), lax.axis_index("y"), lax.axis_index("c")
    sibling = (x, y, 1 - c)
    copies = []
    for t in range(len(parts)):
        for q in range(N_DEV // 2):
            send = pltpu.make_async_remote_copy(
                src_ref=parts[t].at[2 * q + 1 - c], dst_ref=received[t].at[q], send_sem=send_sems.at[t, q],
                recv_sem=recv_sems.at[t, q], device_id=sibling, device_id_type=pl.DeviceIdType.MESH)
            landed = received[t].at[q]
            arrival = pltpu.make_async_remote_copy(
                src_ref=landed, dst_ref=landed, send_sem=send_sems.at[t, q], recv_sem=recv_sems.at[t, q],
                device_id=sibling, device_id_type=pl.DeviceIdType.MESH)
            copies.append((send, arrival))
    return copies


def _pair_add(name, parts, received, side):
    n = len(parts)

    def body(side_ref, *refs):
        for t in range(n):
            refs[2 * n + t][...] = (refs[t][...].astype(F32) + refs[n + t][...].astype(F32)).astype(BF16)

    own = [_spec((None,) + p.shape[1:], lambda q, side_ref: (2 * q + side_ref[0], 0, 0)) for p in parts]
    chip = [_spec((None,) + p.shape[1:], lambda q, side_ref: (q, 0, 0)) for p in parts]
    return pl.pallas_call(
        body, name=name,
        grid_spec=pltpu.PrefetchScalarGridSpec(num_scalar_prefetch=1, grid=(N_CHIPS,), in_specs=own + chip, out_specs=chip),
        out_shape=[jax.ShapeDtypeStruct((N_CHIPS,) + p.shape[1:], BF16) for p in parts], compiler_params=_params(("parallel",)),
    )(side, *parts, *received)


def _send_copy(parts, landing, send_sems, recv_sems, t, s, k):
    x, y, c = _position()
    px, py, _ = _flip(x, y, c, k)
    return pltpu.make_async_remote_copy(
        src_ref=parts[t].at[_chip(px, py)], dst_ref=landing[t].at[_chip(x, y)], send_sem=send_sems.at[s],
        recv_sem=recv_sems.at[s], device_id=(px, py, c), device_id_type=MESH_ID)


def _send_arrival(landing, send_sems, recv_sems, t, s, k):
    x, y, c = _position()
    px, py, _ = _flip(x, y, c, k)
    landed = landing[t].at[_chip(px, py)]
    return pltpu.make_async_remote_copy(
        src_ref=landed, dst_ref=landed, send_sem=send_sems.at[s], recv_sem=recv_sems.at[s],
        device_id=(px, py, c), device_id_type=MESH_ID)


def _handshake(peers):
    x, y, c = _position()
    barrier = pltpu.get_barrier_semaphore()
    for k in peers:
        pl.semaphore_signal(barrier, inc=1, device_id=_flip(x, y, c, k), device_id_type=MESH_ID)
    pl.semaphore_wait(barrier, len(peers))


def _sequencer(name, collective_id, n_copies, body, operands, out_type):
    n_arrays = len(operands)
    return pl.kernel(
        body, out_type=out_type, mesh=plsc.ScalarSubcoreMesh(axis_name="sequencer", num_cores=1), name=name,
        scratch_types=(pltpu.SemaphoreType.DMA((n_copies,)), pltpu.SemaphoreType.DMA((n_copies,)),
                       pltpu.SemaphoreType.DMA((n_arrays,))),
        compiler_params=pltpu.CompilerParams(collective_id=collective_id))(*operands)


def _sequencer_exchange(name, collective_id, parts, after=()):
    n, n_peers, n_in = len(parts), len(SAME_CORE), len(parts) + len(after)

    def body(*refs):
        src, landing = refs[:n], refs[n_in:n_in + n]
        send_sems, recv_sems, local_sems = refs[n_in + n:]
        _handshake(SAME_CORE)
        x, y, _ = _position()
        mine = [pltpu.make_async_copy(src[t].at[_chip(x, y)], landing[t].at[_chip(x, y)], local_sems.at[t]) for t in range(n)]
        for cp in mine:
            cp.start()
        sent = [_send_copy(src, landing, send_sems, recv_sems, t, t * n_peers + j, k)
                for t in range(n) for j, k in enumerate(SAME_CORE)]
        for cp in sent:
            cp.start()
        for t in range(n):
            for j, k in enumerate(SAME_CORE):
                _send_arrival(landing, send_sems, recv_sems, t, t * n_peers + j, k).wait_recv()
        for cp in sent:
            cp.wait_send()
        for cp in mine:
            cp.wait()

    landing = [jax.ShapeDtypeStruct(p.shape, p.dtype) for p in parts]
    return _sequencer(name, collective_id, n * n_peers, body, list(parts) + list(after), landing)


def _sequencer_gather(name, collective_id, shards, after=()):
    n, n_in = len(shards), len(shards) + len(after)

    def body(*refs):
        _handshake(SIBLING_AND_NEIGHBOURS)
        _gather_copies(refs[:n], refs[n_in:n_in + n], *refs[n_in + n:])

    gathered = [jax.ShapeDtypeStruct((N_DEV,) + s.shape, s.dtype) for s in shards]
    return _sequencer(name, collective_id, GATHER_COPIES * n, body, list(shards) + list(after), gathered)


ADAM_ROWS = 512
BF16_ROWS = 16


def _adam_update(w, g, m, v):
    m = ADAM_B1 * m + (1.0 - ADAM_B1) * g
    v = ADAM_B2 * v + (1.0 - ADAM_B2) * (g * g)
    m_hat = m / (1.0 - ADAM_B1 ** ADAM_STEP)
    v_hat = v / (1.0 - ADAM_B2 ** ADAM_STEP)
    delta = -ADAM_LR * (m_hat / (jnp.sqrt(v_hat) + ADAM_EPS) + ADAM_WD * w)
    return delta, m, v


def _sum_slots(ref):
    total = ref[0].astype(F32)
    for d in range(1, ref.shape[0]):
        total = total + ref[d].astype(F32)
    return total


def _adamw_sum(name, landed, w, m, v):
    layers, rows, cols = w.shape
    tiles = [t for t in range(ADAM_ROWS, 0, -BF16_ROWS) if rows % t == 0]
    tr = tiles[0] if tiles else rows
    nt = rows // tr

    def body(*refs):
        parts = refs[:layers]
        w_ref, m_ref, v_ref, g_ref, d_ref, nm_ref, nv_ref = refs[layers:]
        layer = pl.program_id(0)
        g = _sum_slots(parts[0])
        for q in range(1, layers):
            g = jnp.where(layer == q, _sum_slots(parts[q]), g)
        delta, new_m, new_v = _adam_update(w_ref[...], g, m_ref[...], v_ref[...])
        g_ref[...] = g
        d_ref[...] = delta
        nm_ref[...] = new_m
        nv_ref[...] = new_v

    def part_spec(q):
        return _spec((N_CHIPS, tr, cols), lambda l, i: (0, jnp.where(l == q, i, jnp.where(l < q, 0, nt - 1)), 0))

    tile = _spec((None, tr, cols), lambda l, i: (l, i, 0))
    out = jax.ShapeDtypeStruct((layers, rows, cols), F32)
    return pl.pallas_call(
        body, name=name, grid=(layers, nt), in_specs=[part_spec(q) for q in range(layers)] + [tile] * 3,
        out_specs=[tile] * 4, out_shape=[out] * 4, compiler_params=_params(("arbitrary", "arbitrary")),
    )(*landed, w, m, v)


def _sum_small(landed):
    def body(in_ref, out_ref):
        out_ref[...] = _sum_slots(in_ref)

    return pl.pallas_call(body, name="small_grad_sum", out_shape=jax.ShapeDtypeStruct(landed.shape[1:], F32))(landed)


def _adamw_small(arrays):
    n = len(arrays)

    def body(*refs):
        for i in range(n):
            g_ref, w_ref, m_ref, v_ref = refs[4 * i:4 * i + 4]
            d_ref, nm_ref, nv_ref = refs[4 * n + 3 * i:4 * n + 3 * i + 3]
            d_ref[...], nm_ref[...], nv_ref[...] = _adam_update(w_ref[...], g_ref[...], m_ref[...], v_ref[...])

    out = [jax.ShapeDtypeStruct(w.shape, F32) for _, w, _, _ in arrays for _ in range(3)]
    flat = pl.pallas_call(body, name="adam_small", out_shape=out)(*[a for group in arrays for a in group])
    return [tuple(flat[3 * i:3 * i + 3]) for i in range(n)]


LANES = 128
SUBLANES = 8
F_CONV_SHARD = D_FF // N_DEV
GATE_SHARD = KEY_DIM // N_DEV
NORM_SHARD = D_MODEL // N_DEV


def _tile_rows(a):
    flat = a.reshape(-1)
    size = -(-flat.shape[0] // (SUBLANES * LANES)) * SUBLANES * LANES
    return jnp.pad(flat, (0, size - flat.shape[0])).reshape(-1, LANES)


def _pack_rows(pieces):
    return jnp.concatenate([_tile_rows(p) for p in pieces], axis=0)


def _unpack_rows(packed, shapes):
    out, row = [], 0
    for shape in shapes:
        size = 1
        for s in shape:
            size *= s
        rows = -(-size // (SUBLANES * LANES)) * SUBLANES
        piece = packed[..., row:row + rows, :]
        out.append(piece.reshape(piece.shape[:-2] + (rows * LANES,))[..., :size])
        row += rows
    return out


SMALL_SHARDS = ((GATE_RANK, GATE_SHARD), (1, NORM_SHARD), (3, NORM_SHARD), (2, 3, F_CONV_SHARD))


def _unpack_small_shards(g):
    gate, b_norm, b_conv, f_conv = _unpack_rows(g, SMALL_SHARDS)
    gate = gate.reshape(N_DEV, GATE_RANK, GATE_SHARD).transpose(1, 0, 2).reshape(GATE_RANK, KEY_DIM)
    b_norm = b_norm.reshape(1, D_MODEL)
    b_conv = b_conv.reshape(N_DEV, 3, NORM_SHARD).transpose(1, 0, 2).reshape(3, D_MODEL)
    f_conv = f_conv.reshape(N_DEV, 2, 3, F_CONV_SHARD).transpose(1, 2, 0, 3).reshape(2, 3, D_FF)
    return gate, b_norm, b_conv, f_conv


SMALL_LAYOUT = (("a_norm", (1, D_MODEL)), ("a_w_gate_up", (GATE_RANK, KEY_DIM)), ("a_b_gate", (1, KEY_DIM)), ("a_gn", (1, VALUE_DIM)),
                ("b_norm", (1, D_MODEL)), ("b_conv", (3, D_MODEL)), ("f_norm0", (1, D_MODEL)), ("f_norm1", (1, D_MODEL)),
                ("f_conv0", (3, D_FF)), ("f_conv1", (3, D_FF)), ("final_norm", (1, D_MODEL)), ("loss", (1, LANES)))


def _pack_small_grads(g):
    full = dict(g)
    full["a_w_gate_up"] = g["a_w_gate_up"][:GATE_RANK]
    for layer in range(2):
        full[f"f_norm{layer}"] = g["f_norm"][layer]
        full[f"f_conv{layer}"] = g["f_conv"][layer]
    return _pack_rows([full[name] for name, _ in SMALL_LAYOUT])


def _unpack_small_grads(packed):
    pieces = _unpack_rows(packed, [shape for _, shape in SMALL_LAYOUT])
    out = {name: piece.reshape(shape) for (name, shape), piece in zip(SMALL_LAYOUT, pieces)}
    out["f_norm"] = jnp.stack([out["f_norm0"][0], out["f_norm1"][0]])
    out["f_conv"] = jnp.stack([out["f_conv0"], out["f_conv1"]])
    return out


def kernel(x, a_norm, a_w_in, a_w_gate_up, a_b_gate, a_gn, a_w_out, b_norm, b_w_in, b_conv, b_w_out, f_norm, f_w_up, f_conv, f_w_down, final_norm, loss_target, m_a_norm, m_a_w_in, m_a_w_gate_up, m_a_b_gate, m_a_gn, m_a_w_out, m_b_norm, m_b_w_in, m_b_conv, m_b_w_out, m_f_norm, m_f_w_up, m_f_conv, m_f_w_down, m_final_norm, v_a_norm, v_a_w_in, v_a_w_gate_up, v_a_b_gate, v_a_gn, v_a_w_out, v_b_norm, v_b_w_in, v_b_conv, v_b_w_out, v_f_norm, v_f_w_up, v_f_conv, v_f_w_down, v_final_norm):
    my_slot = _slot(*_position())

    transposed = lambda w: jnp.swapaxes(w, 1, 2)
    a_transposed = lambda w: w.reshape(D_MODEL, A_SHARD).T.reshape(1, A_SHARD, D_MODEL)
    a_w_in_t, f_w_up_t = a_transposed(a_w_in), transposed(f_w_up)
    first = _all_gather("weight_gather", 10, [a_w_in_t[0].astype(BF16), a_w_out[0].astype(BF16),
                                          _pack_rows([a_w_gate_up[0], b_norm, b_conv[0], f_conv])])
    gathers, small_shards = {}, first[2]
    later = (("f0", f_w_up_t[0], f_w_down[0]), ("b", b_w_in[0], b_w_out[0]), ("f1", f_w_up_t[1], f_w_down[1]))
    for collective_id, (group, w_in, w_out) in enumerate(later):
        w_in, w_out, small_shards = lax.optimization_barrier((w_in.astype(BF16), w_out.astype(BF16), small_shards))
        gathers[group] = _sequencer_gather(f"gather_{group}", collective_id, [w_in, w_out])
    gate_full, b_norm_full, b_conv_full, f_conv_full = _unpack_small_shards(small_shards)
    a_w_in_full = jnp.pad(first[0].reshape(PROJ_A, D_MODEL), ((0, PROJ_A_PAD - PROJ_A), (0, 0)))
    weights = dict(
        a_norm=a_norm, a_w_gate_up=jnp.pad(gate_full, ((0, GATE_PAD - GATE_RANK), (0, 0))).astype(BF16), a_b_gate=a_b_gate,
        a_gn=a_gn, b_norm=b_norm_full, b_conv=b_conv_full, f_norm=f_norm, f_conv=f_conv_full,
        final_norm=final_norm.reshape(1, D_MODEL))

    def fetch(group, after):
        if group == "a":
            return a_w_in_full, first[1].reshape(D_MODEL, D_MODEL)
        w_in, w_out = gathers[group]
        if group == "b":
            return w_in, w_out.reshape(D_MODEL, D_MODEL)
        return w_in.reshape(2, D_FF, D_MODEL), w_out.reshape(D_FF, D_MODEL)

    exchanges, pending = {}, []
    exchange_ids = dict(b=3, f0=4, a=5)
    side = lax.axis_index("c").astype(jnp.int32).reshape(1)

    def emit(group, parts, received, carry):
        sums = _pair_add(f"pair_add_{group}", parts, received, side)
        carry, *sums = lax.optimization_barrier((carry, *sums))
        pending.extend(sums)
        if group != "f1":
            after = list(exchanges.values())[-1][:1] if exchanges else ()
            exchanges[group] = _sequencer_exchange(f"grads_{group}", exchange_ids[group], list(pending), after)
            pending.clear()
        return carry

    dx, g = _local_step(x[0], loss_target[0], weights, fetch, emit)

    (up1, down1, d_b_in, d_b_out), (up0, down0), (d_a_in, d_a_out) = (exchanges[group] for group in ("b", "f0", "a"))
    back = lambda results: tuple(transposed(r) for r in results)
    big = dict(
        b_w_in=_adamw_sum("adam_b_w_in", [d_b_in], b_w_in, m_b_w_in, v_b_w_in),
        b_w_out=_adamw_sum("adam_b_w_out", [d_b_out], b_w_out, m_b_w_out, v_b_w_out),
        f_w_up=back(_adamw_sum("adam_f_w_up", [up0, up1], f_w_up_t, transposed(m_f_w_up), transposed(v_f_w_up))),
        f_w_down=_adamw_sum("adam_f_w_down", [down0, down1], f_w_down, m_f_w_down, v_f_w_down))
    small_packed, *updated = lax.optimization_barrier((_pack_small_grads(g), *big["f_w_down"]))
    big["f_w_down"] = tuple(updated)
    small_landed = _sequencer_gather("gather_small", 11, [small_packed], after=(d_a_in,))[0]
    small_landed, updated_in, updated_out = lax.optimization_barrier((
        small_landed,
        _adamw_sum("adam_a_w_in", [d_a_in], a_w_in_t, a_transposed(m_a_w_in), a_transposed(v_a_w_in)),
        _adamw_sum("adam_a_w_out", [d_a_out], a_w_out, m_a_w_out, v_a_w_out)))
    big.update(
        a_w_in=tuple(r.reshape(A_SHARD, D_MODEL).T.reshape(1, D_MODEL, A_SHARD) for r in updated_in),
        a_w_out=tuple(updated_out))
    small_g = _unpack_small_grads(_sum_small(small_landed))
    loss = small_g["loss"][0, 0]
    small_g["a_w_gate_up"] = lax.dynamic_slice_in_dim(small_g["a_w_gate_up"], my_slot * GATE_SHARD, GATE_SHARD, axis=1)
    small_g["b_norm"] = lax.dynamic_slice_in_dim(small_g["b_norm"], my_slot * NORM_SHARD, NORM_SHARD, axis=1)
    small_g["b_conv"] = lax.dynamic_slice_in_dim(small_g["b_conv"], my_slot * NORM_SHARD, NORM_SHARD, axis=1)
    small_g["f_conv"] = lax.dynamic_slice_in_dim(small_g["f_conv"], my_slot * F_CONV_SHARD, F_CONV_SHARD, axis=2)
    small_w = dict(
        a_norm=(a_norm, m_a_norm, v_a_norm), a_w_gate_up=(a_w_gate_up, m_a_w_gate_up, v_a_w_gate_up),
        a_b_gate=(a_b_gate, m_a_b_gate, v_a_b_gate), a_gn=(a_gn, m_a_gn, v_a_gn), b_norm=(b_norm, m_b_norm, v_b_norm),
        b_conv=(b_conv, m_b_conv, v_b_conv), f_norm=(f_norm, m_f_norm, v_f_norm), f_conv=(f_conv, m_f_conv, v_f_conv),
        final_norm=(final_norm, m_final_norm, v_final_norm))
    two_d = lambda a: a.reshape(-1, a.shape[-1])
    updates = _adamw_small([tuple(two_d(a.reshape(w.shape)) for a in (small_g[name], w, m, v)) for name, (w, m, v) in small_w.items()])
    small = {}
    for (name, (w, _, _)), update in zip(small_w.items(), updates):
        small[name] = (small_g[name].reshape(w.shape),) + tuple(u.reshape(w.shape) for u in update)

    order = ["a_norm", "a_w_in", "a_w_gate_up", "a_b_gate", "a_gn", "a_w_out", "b_norm", "b_w_in", "b_conv", "b_w_out",
             "f_norm", "f_w_up", "f_conv", "f_w_down", "final_norm"]
    results = {**big, **small}
    outputs = [loss, dx.reshape(1, SEQ, D_MODEL)]
    for kind in range(4):
        outputs += [results[name][kind] for name in order]
    return tuple(outputs)
```

```python
import jax
import jax.numpy as jnp
from jax import lax
from jax.experimental import pallas as pl
from jax.experimental.pallas import tpu as pltpu
from jax.experimental.pallas import tpu_sc as plsc

F32 = jnp.float32
BF16 = jnp.bfloat16

N_DEV = 8
SEQ = 2048
D_MODEL = 1024
CHUNK = 64
N_CHUNKS = SEQ // CHUNK
RMS_EPS = 1e-6
GLA_HEADS = 4
KEY_DIM = 512
VALUE_DIM = 1024
HEAD_K = KEY_DIM // GLA_HEADS
HEAD_V = VALUE_DIM // GLA_HEADS
GATE_RANK = 16
GATE_PAD = 128
GATE_NORMALIZER = 16.0
PROJ_A = 2 * KEY_DIM + 2 * VALUE_DIM + GATE_RANK
PROJ_A_PAD = 2 * KEY_DIM + 2 * VALUE_DIM + GATE_PAD
A_SHARD = PROJ_A // N_DEV
B_SHARD = 3 * D_MODEL // N_DEV
D_FF = 2816
ADAM_LR = 0.001
ADAM_B1 = 0.9
ADAM_B2 = 0.999
ADAM_EPS = 1e-08
ADAM_WD = 0.01
ADAM_STEP = 10

VMEM_LIMIT = 56 * 1024 * 1024
ROW_CHUNK = 256
HALO = 16


def _params(sem=None, vmem=VMEM_LIMIT):
    return pltpu.CompilerParams(dimension_semantics=sem, vmem_limit_bytes=vmem)


SWAP_IDS = {"ffn1_dh": 6, "b_dh": 7, "ffn0_dh": 8, "pair_swap_a": 9}
NORM_PARTS = 2
NN = ((1,), (0,))
NT = ((1,), (1,))
TN = ((0,), (0,))


def _matmul(name, a, a_spec, b, b_spec, dims, grid, out_shape, out_spec, k_blocks=None, a_block_cols=None, res=None,
            res_spec=None, transpose_out=False, norm=None, swap=()):
    has_res = res is not None
    n_swap = len(swap)

    def body(*refs):
        a_ref, b_ref = refs[0], refs[1]
        r_ref = refs[2] if has_res else None

        def product(lhs, rhs):
            return lax.dot_general(lhs.astype(BF16), rhs, (dims, ((), ())), preferred_element_type=F32)

        def tile(rows=slice(None)):
            if k_blocks is None:
                return product(a_ref[rows, :] if norm is not None else a_ref[...], b_ref[...])
            v = None
            for k in range(k_blocks):
                lhs = a_ref[k, rows, :] if a_block_cols is None else a_ref[rows, k * a_block_cols:(k + 1) * a_block_cols]
                p = product(lhs, b_ref[k])
                v = p if v is None else v + p
            return v

        if norm is None:
            v = tile()
            if transpose_out:
                v = v.T
            if has_res:
                v = v + r_ref[...]
            o_ref = refs[2 + has_res]
            o_ref[...] = v.astype(o_ref.dtype)
            return
        n_in = 5 + has_res
        x_ref, g_ref, dxi_ref = refs[2 + has_res:n_in]
        dx_ref, dx16_ref, dg_ref = refs[n_in + n_swap:n_in + n_swap + 3]
        if n_swap:
            copies = _pair_copies(refs[n_in:n_in + n_swap], refs[n_in + n_swap + 3:n_in + 2 * n_swap + 3], *refs[-2:])

            @pl.when(pl.program_id(0) == 0)
            def _():
                sibling = (lax.axis_index("x"), lax.axis_index("y"), 1 - lax.axis_index("c"))
                barrier = pltpu.get_barrier_semaphore()
                pl.semaphore_signal(barrier, inc=1, device_id=sibling, device_id_type=pl.DeviceIdType.MESH)
                pl.semaphore_wait(barrier, 1)
                for send, _ in copies:
                    send.start()

            @pl.when(pl.program_id(0) == grid[0] - 1)
            def _():
                for send, arrival in copies:
                    arrival.wait_recv()
                    send.wait_send()

        dg = None
        part = dx_ref.shape[0] // NORM_PARTS
        for rows in (slice(i * part, (i + 1) * part) for i in range(NORM_PARTS)):
            dx, dg_rows = _norm_bwd_rows(x_ref[rows, :], g_ref[...], tile(rows))
            dx = dxi_ref[rows, :] + dx
            dx_ref[rows, :] = dx
            dx16_ref[rows, :] = dx.astype(BF16)
            dg = dg_rows if dg is None else dg + dg_rows

        @pl.when(pl.program_id(0) == 0)
        def _():
            dg_ref[...] = dg

        @pl.when(pl.program_id(0) > 0)
        def _():
            dg_ref[...] += dg

    operands = [a, b] + ([res] if has_res else [])
    in_specs = [a_spec, b_spec] + ([res_spec] if has_res else [])
    semantics = ("parallel",) * len(grid)
    scratch = []
    if norm is not None:
        vec = _spec((1, D_MODEL), lambda i: (0, 0))
        any_space = pl.BlockSpec(memory_space=pl.ANY)
        operands += list(norm) + list(swap)
        in_specs += [out_spec, vec, out_spec] + [any_space] * n_swap
        out_shape = [_act(dtype=F32), _act(), jax.ShapeDtypeStruct((1, D_MODEL), F32)]
        out_shape += [jax.ShapeDtypeStruct((N_DEV // 2,) + p.shape[1:], p.dtype) for p in swap]
        out_spec = [out_spec, out_spec, vec] + [any_space] * n_swap
        semantics = ("arbitrary",)
        if n_swap:
            scratch = [pltpu.SemaphoreType.DMA((n_swap, N_DEV // 2))] * 2
    params = _params(semantics)
    if n_swap:
        params = pltpu.CompilerParams(dimension_semantics=semantics, vmem_limit_bytes=VMEM_LIMIT, collective_id=SWAP_IDS[name])
    return pl.pallas_call(
        body, name=name, grid=grid, in_specs=in_specs, out_specs=out_spec, out_shape=out_shape, scratch_shapes=scratch,
        compiler_params=params,
    )(*operands)


def _resident(shape):
    return pl.BlockSpec(shape, lambda *_: (0,) * len(shape), pipeline_mode=pl.Buffered(1))


TM = 512
N_TM = SEQ // TM
PA_TILE = 640
OUT_TILE = 256


def _spec(shape, fn):
    return pl.BlockSpec(shape, fn)


def _act(shape=(SEQ, D_MODEL), dtype=BF16):
    return jax.ShapeDtypeStruct(shape, dtype)


def _norm_proj(name, x, gamma, w):
    blocks = w.ndim == 3
    n_out = w.shape[0] * w.shape[2] if blocks else w.shape[0]

    def body(x_ref, g_ref, w_ref, h_ref, o_ref):
        x = x_ref[...]
        h = (x * _rstd(x) * g_ref[...]).astype(BF16)
        h_ref[...] = h
        if blocks:
            n = w.shape[2]
            for j in range(w.shape[0]):
                o_ref[:, j * n:(j + 1) * n] = jnp.dot(h, w_ref[j], preferred_element_type=F32).astype(BF16)
        else:
            o_ref[...] = lax.dot_general(h, w_ref[...], (NT, ((), ())), preferred_element_type=F32).astype(BF16)

    row = _spec((TM, D_MODEL), lambda i: (i, 0))
    return pl.pallas_call(
        body, name=name, grid=(N_TM,), in_specs=[row, _resident((1, D_MODEL)), _resident(w.shape)],
        out_specs=[row, _spec((TM, n_out), lambda i: (i, 0))], out_shape=[_act(), _act((SEQ, n_out))],
        compiler_params=_params(("parallel",)),
    )(x, gamma, w)


def _rows_matmul(name, a, w, dims, x=None):
    k = a.shape[1]
    n = w.shape[1] if dims == NN else w.shape[0]
    row = _spec((TM, n), lambda i: (i, 0))
    return _matmul(name, a, _spec((TM, k), lambda i: (i, 0)), w, _resident(w.shape), dims, (N_TM,),
                   _act((SEQ, n), F32 if x is not None else BF16), row, res=x, res_spec=row if x is not None else None)


def _sum_blocks_nn(name, a_blocks, w_blocks, x=None, norm=None, swap=()):
    nb, _, n = a_blocks.shape
    row = _spec((TM, D_MODEL), lambda i: (i, 0))
    return _matmul(name, a_blocks, _spec((nb, TM, n), lambda i: (0, i, 0)), w_blocks, _resident((nb, n, D_MODEL)),
                   NN, (N_TM,), _act(dtype=F32), row, k_blocks=nb, res=x, res_spec=row if x is not None else None, norm=norm, swap=swap)


def _sum_cols_nt(name, d, w_blocks, norm=None, swap=()):
    nb, _, n = w_blocks.shape
    return _matmul(name, d, _spec((TM, nb * n), lambda i: (i, 0)), w_blocks, _resident((nb, D_MODEL, n)), NT,
                   (N_TM,), _act(dtype=F32), _spec((TM, D_MODEL), lambda i: (i, 0)), k_blocks=nb, a_block_cols=n, norm=norm, swap=swap)


def _wide_nn(name, d, wt, x=None, norm=None, swap=()):
    n = wt.shape[0]
    row = _spec((TM, D_MODEL), lambda i: (i, 0))
    return _matmul(name, d, _spec((TM, n), lambda i: (i, 0)), wt, _resident((n, D_MODEL)), NN, (N_TM,),
                   _act(dtype=F32), row, res=x, res_spec=row if x is not None else None, norm=norm, swap=swap)


def _wgrad_halves_tn(name, d, n_tile, h):
    _, _, n = d.shape
    return _matmul(name, d, _spec((None, SEQ, n_tile), lambda p, j: (p, 0, j)), h, _resident((SEQ, D_MODEL)), TN,
                   (2, n // n_tile), _act((2, n, D_MODEL)), _spec((None, n_tile, D_MODEL), lambda p, j: (p, j, 0)))


def _wgrad_cols_tn(name, d, n_tile, h):
    n = d.shape[1]
    return _matmul(name, d, _spec((SEQ, n_tile), lambda j: (0, j)), h, _resident((SEQ, D_MODEL)), TN,
                   (n // n_tile,), _act((n, D_MODEL)), _spec((n_tile, D_MODEL), lambda j: (j, 0)))


def _wgrad_cols_transposed_tn(name, h, d, n_tile):
    nb = d.shape[1] // n_tile
    return _matmul(name, d, _spec((SEQ, n_tile), lambda j: (0, j)), h, _resident((SEQ, D_MODEL)), TN, (nb,),
                   _act((nb, D_MODEL, n_tile)), _spec((None, D_MODEL, n_tile), lambda j: (j, 0, 0)), transpose_out=True)


def _rstd(x):
    return lax.rsqrt(jnp.mean(x * x, axis=-1, keepdims=True) + RMS_EPS)


def _norm_bwd_rows(x, gamma, dh):
    r = _rstd(x)
    xh = x * r
    dxh = dh * gamma
    dx = r * (dxh - xh * jnp.mean(dxh * xh, axis=-1, keepdims=True))
    return dx, jnp.sum(dh * xh, axis=0, keepdims=True)


def _down_loss_head(a, w_down, x_in, gamma, target):
    def body(a_ref, w_ref, x_ref, g_ref, t_ref, loss_ref, dx_ref, dx16_ref, dg_ref):
        gamma = g_ref[...]
        dg, part = 0.0, 0.0
        rows_per_part = TM // NORM_PARTS
        for rows in (slice(i * rows_per_part, (i + 1) * rows_per_part) for i in range(NORM_PARTS)):
            x = x_ref[rows, :] + jnp.dot(a_ref[rows, :], w_ref[...], preferred_element_type=F32)
            err = x * _rstd(x) * gamma - t_ref[rows, :]
            dy = err * (1.0 / D_MODEL)
            dx, dg_rows = _norm_bwd_rows(x, gamma, dy)
            dx_ref[rows, :] = dx
            dx16_ref[rows, :] = dx.astype(BF16)
            dg = dg + dg_rows
            part = part + 0.5 * jnp.sum(jnp.sum(err * err, axis=-1, keepdims=True) * (1.0 / D_MODEL), axis=0, keepdims=True)
        part = jnp.broadcast_to(part, loss_ref.shape)

        @pl.when(pl.program_id(0) == 0)
        def _():
            dg_ref[...] = dg
            loss_ref[...] = part

        @pl.when(pl.program_id(0) > 0)
        def _():
            dg_ref[...] += dg
            loss_ref[...] += part

    row = _spec((TM, D_MODEL), lambda i: (i, 0))
    vec = _spec((1, D_MODEL), lambda i: (0, 0))
    return pl.pallas_call(
        body, name="ffn1_down_loss_head", grid=(N_TM,),
        in_specs=[_spec((TM, D_FF), lambda i: (i, 0)), _resident((D_FF, D_MODEL)), row, vec, row],
        out_specs=[_spec((1, 128), lambda i: (0, 0)), row, row, vec],
        out_shape=[jax.ShapeDtypeStruct((1, 128), F32), _act(dtype=F32), _act(), jax.ShapeDtypeStruct((1, D_MODEL), F32)],
        compiler_params=_params(("arbitrary",)),
    )(a, w_down, x_in, gamma, target)


def _sigmoid(x):
    return 1.0 / (1.0 + jnp.exp(-x))


def _rows(ref, c):
    return ref[pl.ds(pl.multiple_of(c * ROW_CHUNK, ROW_CHUNK), ROW_CHUNK), :].astype(F32)


def _rows_before(ref, c):
    start = pl.multiple_of(jnp.maximum(c * ROW_CHUNK - HALO, 0), HALO)
    rows = ref[pl.ds(start, HALO), :].astype(F32)
    return jnp.where(c > 0, rows, 0.0)


def _rows_after(ref, c, n_chunks):
    start = pl.multiple_of(jnp.minimum((c + 1) * ROW_CHUNK, SEQ - HALO), HALO)
    rows = ref[pl.ds(start, HALO), :].astype(F32)
    return jnp.where(c < n_chunks - 1, rows, 0.0)


def _shift_down(z, before, n):
    return pltpu.roll(jnp.concatenate([before, z], axis=0), n, 0)[before.shape[0]:]


def _shift_up(z, after, n):
    rows = z.shape[0]
    return pltpu.roll(jnp.concatenate([z, after], axis=0), rows + HALO - n, 0)[:rows]


def _conv_rows(z, before, w):
    z1 = _shift_down(z, before, 1)
    z2 = _shift_down(z, before, 2)
    return w[2:3, :] * z + w[1:2, :] * z1 + w[0:1, :] * z2, z1, z2


def _conv_t_rows(dy, after, w):
    return w[2:3, :] * dy + w[1:2, :] * _shift_up(dy, after, 1) + w[0:1, :] * _shift_up(dy, after, 2)


N_ROW_CHUNKS = SEQ // ROW_CHUNK


FF_COLS = 256
N_FF_COLS = D_FF // FF_COLS


def _ffn_mid_bwd(name, gu, conv_w, da):
    def body(gu_ref, w_ref, da_ref, dgu_ref, dw_ref, dgc_ref):
        w = w_ref[...]

        def first(c, acc):
            g = _rows(gu_ref.at[0], c)
            u = _rows(gu_ref.at[1], c)
            d = _rows(da_ref, c)
            gc, g1, g2 = _conv_rows(g, _rows_before(gu_ref.at[0], c), w)
            sg = _sigmoid(gc)
            rows = pl.ds(pl.multiple_of(c * ROW_CHUNK, ROW_CHUNK), ROW_CHUNK)
            silu = gc * sg
            dgu_ref[1, rows, :] = (d * silu).astype(BF16)
            dgc = d * u * (sg + silu * (1.0 - sg))
            dgc_ref[rows, :] = dgc
            return (acc[0] + jnp.sum(dgc * g2, axis=0, keepdims=True), acc[1] + jnp.sum(dgc * g1, axis=0, keepdims=True),
                    acc[2] + jnp.sum(dgc * g, axis=0, keepdims=True))

        zero = jnp.zeros((1, FF_COLS), F32)
        acc = lax.fori_loop(0, N_ROW_CHUNKS, first, (zero, zero, zero))
        for r in range(3):
            dw_ref[r:r + 1, :] = acc[r]

        def second(c, carry):
            dgc = _rows(dgc_ref, c)
            dg = _conv_t_rows(dgc, _rows_after(dgc_ref, c, N_ROW_CHUNKS), w)
            dgu_ref[0, pl.ds(pl.multiple_of(c * ROW_CHUNK, ROW_CHUNK), ROW_CHUNK), :] = dg.astype(BF16)
            return carry

        lax.fori_loop(0, N_ROW_CHUNKS, second, 0)

    pair = _spec((2, SEQ, FF_COLS), lambda j: (0, 0, j))
    wspec = _spec((3, FF_COLS), lambda j: (0, j))
    return pl.pallas_call(
        body, name=name, grid=(N_FF_COLS,), in_specs=[pair, wspec, _spec((SEQ, FF_COLS), lambda j: (0, j))],
        out_specs=[pair, wspec], out_shape=[_act((2, SEQ, D_FF)), jax.ShapeDtypeStruct((3, D_FF), F32)],
        scratch_shapes=[pltpu.VMEM((SEQ, FF_COLS), F32)],
        compiler_params=_params(("parallel",)),
    )(gu, conv_w, da)


SC_COLS = 256
N_SC = D_MODEL // SC_COLS


def _sc_specs():
    return [_spec((SEQ, SC_COLS), lambda j, part=part: (0, part * N_SC + j)) for part in range(3)]


def _sc_mid_fwd(p, conv_w):
    def body(b_ref, c_ref, h_ref, w_ref, y_ref):
        w = w_ref[...]

        def chunk(c, carry):
            z = _rows(c_ref, c) * _rows(h_ref, c)
            before = _rows_before(c_ref, c) * _rows_before(h_ref, c)
            zc, _, _ = _conv_rows(z, before, w)
            y_ref[pl.ds(pl.multiple_of(c * ROW_CHUNK, ROW_CHUNK), ROW_CHUNK), :] = (_rows(b_ref, c) * zc).astype(BF16)
            return carry

        lax.fori_loop(0, N_ROW_CHUNKS, chunk, 0)

    col = _spec((SEQ, SC_COLS), lambda j: (0, j))
    return pl.pallas_call(
        body, name="sc_mid_fwd", grid=(N_SC,), in_specs=_sc_specs() + [_spec((3, SC_COLS), lambda j: (0, j))], out_specs=col,
        out_shape=jax.ShapeDtypeStruct((SEQ, D_MODEL), BF16), compiler_params=_params(("parallel",)),
    )(p, p, p, conv_w)


def _sc_mid_bwd(p, conv_w, dy):
    def body(b_ref, c_ref, h_ref, w_ref, dy_ref, db_ref, dc_ref, dh_ref, dw_ref, dzc_ref):
        w = w_ref[...]

        def first(c, acc):
            z = _rows(c_ref, c) * _rows(h_ref, c)
            before = _rows_before(c_ref, c) * _rows_before(h_ref, c)
            zc, z1, z2 = _conv_rows(z, before, w)
            d = _rows(dy_ref, c)
            rows = pl.ds(pl.multiple_of(c * ROW_CHUNK, ROW_CHUNK), ROW_CHUNK)
            db_ref[rows, :] = (d * zc).astype(BF16)
            dzc = d * _rows(b_ref, c)
            dzc_ref[rows, :] = dzc
            return (acc[0] + jnp.sum(dzc * z2, axis=0, keepdims=True), acc[1] + jnp.sum(dzc * z1, axis=0, keepdims=True),
                    acc[2] + jnp.sum(dzc * z, axis=0, keepdims=True))

        zero = jnp.zeros((1, SC_COLS), F32)
        acc = lax.fori_loop(0, N_ROW_CHUNKS, first, (zero, zero, zero))
        for r in range(3):
            dw_ref[r:r + 1, :] = acc[r]

        def second(c, carry):
            dz = _conv_t_rows(_rows(dzc_ref, c), _rows_after(dzc_ref, c, N_ROW_CHUNKS), w)
            rows = pl.ds(pl.multiple_of(c * ROW_CHUNK, ROW_CHUNK), ROW_CHUNK)
            dc_ref[rows, :] = (dz * _rows(h_ref, c)).astype(BF16)
            dh_ref[rows, :] = (dz * _rows(c_ref, c)).astype(BF16)
            return carry

        lax.fori_loop(0, N_ROW_CHUNKS, second, 0)

    col = _spec((SEQ, SC_COLS), lambda j: (0, j))
    wspec = _spec((3, SC_COLS), lambda j: (0, j))
    act = jax.ShapeDtypeStruct((SEQ, D_MODEL), BF16)
    return pl.pallas_call(
        body, name="sc_mid_bwd", grid=(N_SC,), in_specs=_sc_specs() + [wspec, col], out_specs=[col, col, col, wspec],
        out_shape=[act, act, act, jax.ShapeDtypeStruct((3, D_MODEL), F32)],
        scratch_shapes=[pltpu.VMEM((SEQ, SC_COLS), F32)], compiler_params=_params(("parallel",)),
    )(p, p, p, conv_w, dy)


GLA_GROUP = 4
GLA_ROWS = GLA_GROUP * CHUNK
N_GROUPS = N_CHUNKS // GLA_GROUP
Q0, K0, V0, R0, G0 = 0, KEY_DIM, 2 * KEY_DIM, 2 * KEY_DIM + VALUE_DIM, 2 * KEY_DIM + 2 * VALUE_DIM


def _tri(strict):
    r = lax.broadcasted_iota(jnp.int32, (CHUNK, CHUNK), 0)
    c = lax.broadcasted_iota(jnp.int32, (CHUNK, CHUNK), 1)
    return jnp.where(c < r if strict else c <= r, 1.0, 0.0).astype(F32)


def _cumsum_rows(tri, x):
    tri = tri.astype(BF16)
    total = None
    for _ in range(3):
        term = x.astype(BF16)
        x = x - term.astype(F32)
        product = jnp.dot(tri, term, preferred_element_type=F32)
        total = product if total is None else total + product
    return total


def _gate_logits(gl, wgu, b_gate):
    return jnp.dot(gl, wgu, preferred_element_type=F32) + b_gate


def _log_decay(logits):
    return (jnp.minimum(logits, 0.0) - jnp.log(1.0 + jnp.exp(-jnp.abs(logits)))) * (1.0 / GATE_NORMALIZER)


def _head(x, h, width):
    return x[:, h * width:(h + 1) * width]


def _gla_fwd(proj, wgu, b_gate, gn):
    def body(p_ref, wgu_ref, b_ref, gn_ref, o_ref, og_ref, st_ref, state):
        @pl.when(pl.program_id(0) == 0)
        def _():
            state[...] = jnp.zeros_like(state)

        tri = _tri(False)
        la = _log_decay(_gate_logits(p_ref[:, G0:G0 + GATE_PAD], wgu_ref[...], b_ref[...]))
        decays = []
        for c in range(GLA_GROUP):
            rows = slice(c * CHUNK, (c + 1) * CHUNK)
            cum = _cumsum_rows(tri, la[rows])
            tot = cum[CHUNK - 1:CHUNK, :]
            kd = (p_ref[rows, K0:K0 + KEY_DIM].astype(F32) * jnp.exp(tot - cum)).astype(BF16)
            decays.append(jnp.exp(tot))
            v = p_ref[rows, V0:V0 + VALUE_DIM]
            for h in range(GLA_HEADS):
                st_ref[c, h] = lax.dot_general(
                    _head(v, h, HEAD_V), _head(kd, h, HEAD_K), (TN, ((), ())), preferred_element_type=F32)
        for c in range(GLA_GROUP):
            for h in range(GLA_HEADS):
                s = state[h] * _head(decays[c], h, HEAD_K) + st_ref[c, h]
                state[h] = s
                st_ref[c, h] = s
        for c in range(GLA_GROUP):
            rows = slice(c * CHUNK, (c + 1) * CHUNK)
            q = (p_ref[rows, Q0:Q0 + KEY_DIM].astype(F32) * (HEAD_K ** -0.5)).astype(BF16)
            for h in range(GLA_HEADS):
                o_ref[rows, h * HEAD_V:(h + 1) * HEAD_V] = lax.dot_general(
                    _head(q, h, HEAD_K), st_ref[c, h].astype(BF16), (NT, ((), ())), preferred_element_type=F32)
        r = p_ref[:, R0:R0 + VALUE_DIM].astype(F32)
        gate = r * _sigmoid(r) * gn_ref[...]
        for h in range(GLA_HEADS):
            cols = slice(h * HEAD_V, (h + 1) * HEAD_V)
            o = o_ref[:, cols]
            og_ref[:, cols] = (o * _rstd(o) * gate[:, cols]).astype(BF16)

    rows = _spec((GLA_ROWS, VALUE_DIM), lambda i: (i, 0))
    const = lambda shape: _spec(shape, lambda i: (0,) * len(shape))
    return pl.pallas_call(
        body, name="gla_fwd", grid=(N_GROUPS,),
        in_specs=[_spec((GLA_ROWS, PROJ_A_PAD), lambda i: (i, 0)), const((GATE_PAD, KEY_DIM)), const((1, KEY_DIM)),
                  const((1, VALUE_DIM))],
        out_specs=[rows, rows, _spec((GLA_GROUP, GLA_HEADS, HEAD_V, HEAD_K), lambda i: (i, 0, 0, 0))],
        out_shape=[jax.ShapeDtypeStruct((SEQ, VALUE_DIM), F32), jax.ShapeDtypeStruct((SEQ, VALUE_DIM), BF16),
                   jax.ShapeDtypeStruct((N_CHUNKS, GLA_HEADS, HEAD_V, HEAD_K), F32)],
        scratch_shapes=[pltpu.VMEM((GLA_HEADS, HEAD_V, HEAD_K), F32)], compiler_params=_params(("arbitrary",)),
    )(proj, wgu, b_gate, gn)


def _gla_bwd(proj, wgu, b_gate, gn, o, states, dog):
    last = N_GROUPS - 1

    def body(p_ref, wgu_ref, b_ref, gn_ref, o_ref, st_ref, stp_ref, dog_ref, dp_ref, dwgu_ref, db_ref, dgn_ref, carry, do_buf,
             g_buf):
        step = pl.program_id(0)

        @pl.when(step == 0)
        def _():
            carry[...] = jnp.zeros_like(carry)

        r = p_ref[:, R0:R0 + VALUE_DIM].astype(F32)
        sr = _sigmoid(r)
        silu = r * sr
        gn_row = gn_ref[...]
        dog_rows = dog_ref[...].astype(F32)
        dn = dog_rows * silu
        dgn_cols = []
        for h in range(GLA_HEADS):
            cols = slice(h * HEAD_V, (h + 1) * HEAD_V)
            oh = o_ref[:, cols]
            rs = _rstd(oh)
            ohat = oh * rs
            dn_h = dn[:, cols]
            dgn_cols.append(jnp.sum(dn_h * ohat, axis=0, keepdims=True))
            dohat = dn_h * gn_row[:, cols]
            do_buf[:, cols] = rs * (dohat - ohat * jnp.mean(dohat * ohat, axis=-1, keepdims=True))
            n_h = ohat * gn_row[:, cols]
            dp_ref[:, R0 + h * HEAD_V:R0 + (h + 1) * HEAD_V] = (
                dog_rows[:, cols] * n_h * (sr[:, cols] * (1.0 + r[:, cols] * (1.0 - sr[:, cols])))).astype(BF16)
        dgn = jnp.concatenate(dgn_cols, axis=1)

        tri = _tri(False)
        tri_strict = _tri(True)
        gl = p_ref[:, G0:G0 + GATE_PAD]
        logits = _gate_logits(gl, wgu_ref[...], b_ref[...])
        la = _log_decay(logits)
        fades, kds, decays = [], [], []
        for c in range(GLA_GROUP):
            rows = slice(c * CHUNK, (c + 1) * CHUNK)
            cum = _cumsum_rows(tri, la[rows])
            tot = cum[CHUNK - 1:CHUNK, :]
            fades.append(jnp.exp(tot - cum))
            kds.append(p_ref[rows, K0:K0 + KEY_DIM].astype(F32) * fades[c])
            decays.append(jnp.exp(tot))
            q = (p_ref[rows, Q0:Q0 + KEY_DIM].astype(F32) * (HEAD_K ** -0.5)).astype(BF16)
            do = do_buf[rows, :].astype(BF16)
            for h in range(GLA_HEADS):
                do_h = _head(do, h, HEAD_V)
                dq = jnp.dot(do_h, st_ref[c, h].astype(BF16), preferred_element_type=F32) * (HEAD_K ** -0.5)
                dp_ref[rows, Q0 + h * HEAD_K:Q0 + (h + 1) * HEAD_K] = dq.astype(BF16)
                g_buf[c, h] = lax.dot_general(do_h, _head(q, h, HEAD_K), (TN, ((), ())), preferred_element_type=F32)
        for c in reversed(range(GLA_GROUP)):
            for h in range(GLA_HEADS):
                g = carry[h] + g_buf[c, h]
                g_buf[c, h] = g
                carry[h] = g * _head(decays[c], h, HEAD_K)
        dlogit_rows = []
        for c in range(GLA_GROUP):
            rows = slice(c * CHUNK, (c + 1) * CHUNK)
            v = p_ref[rows, V0:V0 + VALUE_DIM]
            kd = kds[c].astype(BF16)
            dkd_cols, ddecay_cols = [], []
            for h in range(GLA_HEADS):
                g = g_buf[c, h]
                g16 = g.astype(BF16)
                dkd_cols.append(jnp.dot(_head(v, h, HEAD_V), g16, preferred_element_type=F32))
                dv = lax.dot_general(_head(kd, h, HEAD_K), g16, (NT, ((), ())), preferred_element_type=F32)
                dp_ref[rows, V0 + h * HEAD_V:V0 + (h + 1) * HEAD_V] = dv.astype(BF16)
                if c > 0:
                    s_prev = st_ref[c - 1, h]
                else:
                    s_prev = jnp.where(step < last, stp_ref[0, h], 0.0)
                ddecay_cols.append(jnp.sum(g * s_prev, axis=0, keepdims=True))
            dkd = jnp.concatenate(dkd_cols, axis=1)
            ddecay = jnp.concatenate(ddecay_cols, axis=1)
            dp_ref[rows, K0:K0 + KEY_DIM] = (dkd * fades[c]).astype(BF16)
            e = dkd * kds[c]
            dla = ddecay * decays[c] + _cumsum_rows(tri_strict, e)
            dlogit_rows.append(dla * (1.0 / GATE_NORMALIZER) * (1.0 - _sigmoid(logits[rows])))
        dlogit = jnp.concatenate(dlogit_rows, axis=0)
        dlogit16 = dlogit.astype(BF16)
        dp_ref[:, G0:G0 + GATE_PAD] = lax.dot_general(
            dlogit16, wgu_ref[...], (NT, ((), ())), preferred_element_type=F32).astype(BF16)
        dwgu = lax.dot_general(gl, dlogit16, (TN, ((), ())), preferred_element_type=F32)
        db = jnp.sum(dlogit, axis=0, keepdims=True)

        @pl.when(step == 0)
        def _():
            dwgu_ref[...] = dwgu
            db_ref[...] = db
            dgn_ref[...] = dgn

        @pl.when(step > 0)
        def _():
            dwgu_ref[...] += dwgu
            db_ref[...] += db
            dgn_ref[...] += dgn

    rev = lambda i: (last - i, 0)
    rows = _spec((GLA_ROWS, VALUE_DIM), rev)
    const = lambda shape: _spec(shape, lambda i: (0,) * len(shape))
    st_shape = (GLA_HEADS, HEAD_V, HEAD_K)
    return pl.pallas_call(
        body, name="gla_bwd", grid=(N_GROUPS,),
        in_specs=[_spec((GLA_ROWS, PROJ_A_PAD), rev), const((GATE_PAD, KEY_DIM)), const((1, KEY_DIM)), const((1, VALUE_DIM)),
                  rows, _spec((GLA_GROUP,) + st_shape, lambda i: (last - i, 0, 0, 0)),
                  _spec((1,) + st_shape, lambda i: (jnp.maximum((last - i) * GLA_GROUP - 1, 0), 0, 0, 0)), rows],
        out_specs=[_spec((GLA_ROWS, PROJ_A_PAD), rev), const((GATE_PAD, KEY_DIM)), const((1, KEY_DIM)), const((1, VALUE_DIM))],
        out_shape=[jax.ShapeDtypeStruct((SEQ, PROJ_A_PAD), BF16), jax.ShapeDtypeStruct((GATE_PAD, KEY_DIM), F32),
                   jax.ShapeDtypeStruct((1, KEY_DIM), F32), jax.ShapeDtypeStruct((1, VALUE_DIM), F32)],
        scratch_shapes=[pltpu.VMEM(st_shape, F32), pltpu.VMEM((GLA_ROWS, VALUE_DIM), F32), pltpu.VMEM((GLA_GROUP,) + st_shape, F32)],
        compiler_params=_params(("arbitrary",)),
    )(proj, wgu, b_gate, gn, o, states, states, dog)


WGRAD_FF_TILE = D_FF // 2


CARRY_ROWS = 8
UP_ROWS = 512


def _ffn_up_mid(name, x, gamma, w_up_t, conv_w):
    def body(x_ref, g_ref, w_ref, c_ref, h_ref, gu_ref, a_ref, carry):
        @pl.when(pl.program_id(0) == 0)
        def _():
            carry[...] = jnp.zeros_like(carry)

        x_tile = x_ref[...]
        h_tile = (x_tile * _rstd(x_tile) * g_ref[...]).astype(BF16)
        h_ref[...] = h_tile
        for k in range(N_FF_COLS):
            cols = slice(k * FF_COLS, (k + 1) * FF_COLS)
            g, u = (lax.dot_general(h_tile, w_ref[p, cols, :], (NT, ((), ())), preferred_element_type=F32).astype(BF16)
                    for p in range(2))
            gu_ref[0, :, cols] = g
            gu_ref[1, :, cols] = u
            g = g.astype(F32)
            w = c_ref[:, cols]
            before = carry[:, cols]
            gc = w[2:3, :] * g + w[1:2, :] * _shift_down(g, before, 1) + w[0:1, :] * _shift_down(g, before, 2)
            a_ref[:, cols] = (gc * _sigmoid(gc) * u.astype(F32)).astype(BF16)
            carry[:, cols] = g[UP_ROWS - CARRY_ROWS:, :]

    row = _spec((UP_ROWS, D_MODEL), lambda i: (i, 0))
    return pl.pallas_call(
        body, name=name, grid=(SEQ // UP_ROWS,),
        in_specs=[row, _resident((1, D_MODEL)), _resident((2, D_FF, D_MODEL)), _resident((3, D_FF))],
        out_specs=[row, _spec((2, UP_ROWS, D_FF), lambda i: (0, i, 0)), _spec((UP_ROWS, D_FF), lambda i: (i, 0))],
        out_shape=[_act(), _act((2, SEQ, D_FF)), _act((SEQ, D_FF))], scratch_shapes=[pltpu.VMEM((CARRY_ROWS, D_FF), F32)],
        compiler_params=_params(("arbitrary",)),
    )(x, gamma, w_up_t, conv_w)


def _ffn_fwd(tag, x, gamma, w_up_t, conv_w, w_down):
    h, gu, a = _ffn_up_mid(f"ffn{tag}_up_mid", x, gamma, w_up_t, conv_w)
    return _rows_matmul(f"ffn{tag}_down", a, w_down, NN, x), (h, gu, a)


def _owner_blocks(d, rows=None):
    if rows is not None:
        d = d[:rows]
    return d.reshape((N_DEV, -1) + d.shape[-1:])


def _ffn_bwd(tag, x, gamma, w_up_t, conv_w, w_down, saved, dx, dx16, swap):
    h, gu, a = saved
    da = _rows_matmul(f"ffn{tag}_da", dx16, w_down, NT)
    d_w_down = _owner_blocks(_wgrad_cols_tn(f"ffn{tag}_dwdown", a, WGRAD_FF_TILE, dx16))
    dgu, d_conv = _ffn_mid_bwd(f"ffn{tag}_mid_bwd", gu, conv_w, da)
    d_w_up_t = _owner_blocks(_wgrad_halves_tn(f"ffn{tag}_dwup", dgu, WGRAD_FF_TILE, h))
    parts = (d_w_up_t, d_w_down)
    dx, dx16, d_gamma, *received = _sum_blocks_nn(
        f"ffn{tag}_dh", dgu, w_up_t, norm=(x, gamma, dx), swap=parts if swap else ())
    return dx, dx16, d_gamma, d_conv, parts, received


def _local_step(x, target, w, fetch=None, emit=None):
    if fetch is None:
        local = dict(a=(w.get("a_w_in"), w.get("a_w_out")), b=(w.get("b_w_in"), w.get("b_w_out")))
        for layer in range(2):
            local[f"f{layer}"] = (w["f_w_up"][layer], w["f_w_down"][layer]) if "f_w_up" in w else None
        fetch = lambda group, after: local[group]
    swap = emit is not None
    if emit is None:
        emit = lambda group, parts, received, dx: dx
    f_norm = (w["f_norm"][0:1], w["f_norm"][1:2])

    x0 = x
    a_w_in, a_w_out = fetch("a", x0)
    h0, proj = _norm_proj("a_in", x0, w["a_norm"], a_w_in)
    o, og, states = _gla_fwd(proj, w["a_w_gate_up"], w["a_b_gate"], w["a_gn"])
    x1 = _rows_matmul("a_out", og, a_w_out, NN, x0)
    up0, down0 = fetch("f0", x1)
    x2, ffn0 = _ffn_fwd(0, x1, f_norm[0], up0, w["f_conv"][0], down0)
    b_w_in, b_w_out = fetch("b", x2)
    h2, p = _norm_proj("b_in", x2, w["b_norm"], b_w_in)
    y = _sc_mid_fwd(p, w["b_conv"])
    x3 = _rows_matmul("b_out", y, b_w_out, NN, x2)
    up1, down1 = fetch("f1", x3)
    ffn1 = _ffn_up_mid("ffn1_up_mid", x3, f_norm[1], up1, w["f_conv"][1])
    loss, dx, dx16, d_final_norm = _down_loss_head(ffn1[2], down1, x3, w["final_norm"], target)

    dx, dx16, d_f_norm1, d_fconv1, parts_f1, got = _ffn_bwd(
        1, x3, f_norm[1], up1, w["f_conv"][1], down1, ffn1, dx, dx16, swap)
    dx16 = emit("f1", parts_f1, got, dx16)

    dy = _rows_matmul("b_dy", dx16, b_w_out, NT)
    d_b_w_out = _owner_blocks(_wgrad_cols_tn("b_dwout", y, OUT_TILE, dx16))
    db, dc, dhh, d_b_conv = _sc_mid_bwd(p, w["b_conv"], dy)
    dp = jnp.concatenate([db, dc, dhh], axis=1)
    parts_b = (_wgrad_cols_transposed_tn("b_dwin", h2, dp, B_SHARD), d_b_w_out)
    dx, dx16, d_b_norm, *got = _sum_cols_nt("b_dh", dp, b_w_in, norm=(x2, w["b_norm"], dx), swap=parts_b if swap else ())
    dx16 = emit("b", parts_b, got, dx16)

    dx, dx16, d_f_norm0, d_fconv0, parts_f0, got = _ffn_bwd(
        0, x1, f_norm[0], up0, w["f_conv"][0], down0, ffn0, dx, dx16, swap)
    dx16 = emit("f0", parts_f0, got, dx16)

    dog = _rows_matmul("a_dog", dx16, a_w_out, NT)
    d_a_w_out = _owner_blocks(_wgrad_cols_tn("a_dwout", og, OUT_TILE, dx16))
    dproj, d_wgu, d_b_gate, d_gn = _gla_bwd(proj, w["a_w_gate_up"], w["a_b_gate"], w["a_gn"], o, states, dog)
    parts_a = (_owner_blocks(_wgrad_cols_tn("a_dwin", dproj, PA_TILE, h0), PROJ_A), d_a_w_out)
    got = _pair_swap("pair_swap_a", parts_a) if swap else ()
    dproj = emit("a", parts_a, got, dproj)
    dx, _, d_a_norm = _wide_nn("a_dh", dproj, a_w_in, norm=(x0, w["a_norm"], dx))

    grads = dict(
        a_norm=d_a_norm, a_w_in=parts_a[0], a_w_gate_up=d_wgu, a_b_gate=d_b_gate, a_gn=d_gn, a_w_out=parts_a[1],
        b_norm=d_b_norm, b_w_in=parts_b[0], b_conv=d_b_conv, b_w_out=parts_b[1],
        f_norm=(d_f_norm0, d_f_norm1), f_w_up=(parts_f0[0], parts_f1[0]), f_conv=(d_fconv0, d_fconv1),
        f_w_down=(parts_f0[1], parts_f1[1]), final_norm=d_final_norm)
    grads["loss"] = loss
    return dx, grads


MESH_ID = pl.DeviceIdType.MESH
ANY = pl.BlockSpec(memory_space=pl.ANY)


def _position():
    return lax.axis_index("x"), lax.axis_index("y"), lax.axis_index("c")


def _slot(px, py, pc):
    return 4 * px + 2 * py + pc


GATHER_COPIES = 8
HALF_ROWS = 16


def _gather_copies(src, out, send_sems, recv_sems, local_sems):
    n = len(src)
    to_sibling, to_x, to_y, x_on_to_y, y_on_to_x, x_to_sibling, y_to_sibling, diagonal_to_sibling = range(GATHER_COPIES)
    x, y, c = _position()
    me, sibling = (x, y, c), (x, y, 1 - c)
    x_side, y_side, diagonal = (1 - x, y), (x, 1 - y), (1 - x, 1 - y)

    def rows_of(t, half):
        rows = src[t].shape[0]
        half_rows = rows // 2 // HALF_ROWS * HALF_ROWS
        return (pl.ds(0, rows), pl.ds(0, half_rows), pl.ds(half_rows, rows - half_rows))[half]

    def copy(t, j, block, to, half=0, from_input=False):
        dst = out[t].at[_slot(*block), rows_of(t, half)]
        return pltpu.make_async_remote_copy(
            src_ref=src[t] if from_input else dst, dst_ref=dst, send_sem=send_sems.at[GATHER_COPIES * t + j],
            recv_sem=recv_sems.at[GATHER_COPIES * t + j], device_id=to, device_id_type=MESH_ID)

    mine = [pltpu.make_async_copy(src[t], out[t].at[_slot(*me)], local_sems.at[t]) for t in range(n)]
    for cp in mine:
        cp.start()
    sent = []

    def start(cp):
        cp.start()
        sent.append(cp)

    for t in range(n):
        start(copy(t, to_sibling, me, sibling, from_input=True))
        start(copy(t, to_x, me, (*x_side, c), from_input=True))
        start(copy(t, to_y, me, (*y_side, c), from_input=True))
    for t in range(n):
        copy(t, to_x, (*x_side, c), me).wait_recv()
        start(copy(t, x_on_to_y, (*x_side, c), (*y_side, c), half=1))
        start(copy(t, x_to_sibling, (*x_side, c), sibling))
        copy(t, to_y, (*y_side, c), me).wait_recv()
        start(copy(t, y_on_to_x, (*y_side, c), (*x_side, c), half=2))
        start(copy(t, y_to_sibling, (*y_side, c), sibling))
    for t in range(n):
        copy(t, x_on_to_y, (*diagonal, c), me, half=1).wait_recv()
        copy(t, y_on_to_x, (*diagonal, c), me, half=2).wait_recv()
        start(copy(t, diagonal_to_sibling, (*diagonal, c), sibling))
    for t in range(n):
        copy(t, to_sibling, sibling, me).wait_recv()
        for j, chip in ((x_to_sibling, x_side), (y_to_sibling, y_side), (diagonal_to_sibling, diagonal)):
            copy(t, j, (*chip, 1 - c), me).wait_recv()
    for cp in sent:
        cp.wait_send()
    for cp in mine:
        cp.wait()


def _all_gather(name, collective_id, shards):
    n = len(shards)

    def body(*refs):
        _handshake(SIBLING_AND_NEIGHBOURS)
        _gather_copies(refs[:n], refs[n:2 * n], *refs[2 * n:])

    sems = pltpu.SemaphoreType.DMA((GATHER_COPIES * n,))
    return pl.pallas_call(
        body, name=name, in_specs=[ANY] * n, out_specs=[ANY] * n,
        out_shape=[jax.ShapeDtypeStruct((N_DEV,) + s.shape, s.dtype) for s in shards],
        scratch_shapes=[sems, sems, pltpu.SemaphoreType.DMA((n,))],
        compiler_params=pltpu.CompilerParams(collective_id=collective_id),
    )(*shards)


SIBLING_AND_NEIGHBOURS = (1, 2, 4)
SAME_CORE = (2, 4, 6)


def _flip(x, y, c, k):
    return x ^ (k >> 2), y ^ ((k >> 1) & 1), c ^ (k & 1)


N_CHIPS = N_DEV // 2


def _chip(px, py):
    return 2 * px + py


def _pair_swap(name, parts):
    n = len(parts)

    def body(*refs):
        sibling = (lax.axis_index("x"), lax.axis_index("y"), 1 - lax.axis_index("c"))
        barrier = pltpu.get_barrier_semaphore()
        pl.semaphore_signal(barrier, inc=1, device_id=sibling, device_id_type=pl.DeviceIdType.MESH)
        pl.semaphore_wait(barrier, 1)
        copies = _pair_copies(refs[:n], refs[n:2 * n], *refs[2 * n:])
        for send, _ in copies:
            send.start()
        for send, arrival in copies:
            arrival.wait_recv()
            send.wait_send()

    sems = pltpu.SemaphoreType.DMA((n, N_DEV // 2))
    any_space = pl.BlockSpec(memory_space=pl.ANY)
    return pl.pallas_call(
        body, name=name, in_specs=[any_space] * n, out_specs=[any_space] * n,
        out_shape=[jax.ShapeDtypeStruct((N_DEV // 2,) + p.shape[1:], p.dtype) for p in parts], scratch_shapes=[sems, sems],
        compiler_params=pltpu.CompilerParams(collective_id=SWAP_IDS[name]),
    )(*parts)


def _pair_copies(parts, received, send_sems, recv_sems):
    x, y, c = lax.axis_index("x"), lax.axis_index("y"), lax.axis_index("c")
    sibling = (x, y, 1 - c)
    copies = []
    for t in range(len(parts)):
        for q in range(N_DEV // 2):
            send = pltpu.make_async_remote_copy(
                src_ref=parts[t].at[2 * q + 1 - c], dst_ref=received[t].at[q], send_sem=send_sems.at[t, q],
                recv_sem=recv_sems.at[t, q], device_id=sibling, device_id_type=pl.DeviceIdType.MESH)
            landed = received[t].at[q]
            arrival = pltpu.make_async_remote_copy(
                src_ref=landed, dst_ref=landed, send_sem=send_sems.at[t, q], recv_sem=recv_sems.at[t, q],
                device_id=sibling, device_id_type=pl.DeviceIdType.MESH)
            copies.append((send, arrival))
    return copies


def _pair_add(name, parts, received, side):
    n = len(parts)

    def body(side_ref, *refs):
        for t in range(n):
            refs[2 * n + t][...] = (refs[t][...].astype(F32) + refs[n + t][...].astype(F32)).astype(BF16)

    own = [_spec((None,) + p.shape[1:], lambda q, side_ref: (2 * q + side_ref[0], 0, 0)) for p in parts]
    chip = [_spec((None,) + p.shape[1:], lambda q, side_ref: (q, 0, 0)) for p in parts]
    return pl.pallas_call(
        body, name=name,
        grid_spec=pltpu.PrefetchScalarGridSpec(num_scalar_prefetch=1, grid=(N_CHIPS,), in_specs=own + chip, out_specs=chip),
        out_shape=[jax.ShapeDtypeStruct((N_CHIPS,) + p.shape[1:], BF16) for p in parts], compiler_params=_params(("parallel",)),
    )(side, *parts, *received)


def _send_copy(parts, landing, send_sems, recv_sems, t, s, k):
    x, y, c = _position()
    px, py, _ = _flip(x, y, c, k)
    return pltpu.make_async_remote_copy(
        src_ref=parts[t].at[_chip(px, py)], dst_ref=landing[t].at[_chip(x, y)], send_sem=send_sems.at[s],
        recv_sem=recv_sems.at[s], device_id=(px, py, c), device_id_type=MESH_ID)


def _send_arrival(landing, send_sems, recv_sems, t, s, k):
    x, y, c = _position()
    px, py, _ = _flip(x, y, c, k)
    landed = landing[t].at[_chip(px, py)]
    return pltpu.make_async_remote_copy(
        src_ref=landed, dst_ref=landed, send_sem=send_sems.at[s], recv_sem=recv_sems.at[s],
        device_id=(px, py, c), device_id_type=MESH_ID)


def _handshake(peers):
    x, y, c = _position()
    barrier = pltpu.get_barrier_semaphore()
    for k in peers:
        pl.semaphore_signal(barrier, inc=1, device_id=_flip(x, y, c, k), device_id_type=MESH_ID)
    pl.semaphore_wait(barrier, len(peers))


def _sequencer(name, collective_id, n_copies, body, operands, out_type):
    n_arrays = len(operands)
    return pl.kernel(
        body, out_type=out_type, mesh=plsc.ScalarSubcoreMesh(axis_name="sequencer", num_cores=1), name=name,
        scratch_types=(pltpu.SemaphoreType.DMA((n_copies,)), pltpu.SemaphoreType.DMA((n_copies,)),
                       pltpu.SemaphoreType.DMA((n_arrays,))),
        compiler_params=pltpu.CompilerParams(collective_id=collective_id))(*operands)


def _sequencer_exchange(name, collective_id, parts, after=()):
    n, n_peers, n_in = len(parts), len(SAME_CORE), len(parts) + len(after)

    def body(*refs):
        src, landing = refs[:n], refs[n_in:n_in + n]
        send_sems, recv_sems, local_sems = refs[n_in + n:]
        _handshake(SAME_CORE)
        x, y, _ = _position()
        mine = [pltpu.make_async_copy(src[t].at[_chip(x, y)], landing[t].at[_chip(x, y)], local_sems.at[t]) for t in range(n)]
        for cp in mine:
            cp.start()
        sent = [_send_copy(src, landing, send_sems, recv_sems, t, t * n_peers + j, k)
                for t in range(n) for j, k in enumerate(SAME_CORE)]
        for cp in sent:
            cp.start()
        for t in range(n):
            for j, k in enumerate(SAME_CORE):
                _send_arrival(landing, send_sems, recv_sems, t, t * n_peers + j, k).wait_recv()
        for cp in sent:
            cp.wait_send()
        for cp in mine:
            cp.wait()

    landing = [jax.ShapeDtypeStruct(p.shape, p.dtype) for p in parts]
    return _sequencer(name, collective_id, n * n_peers, body, list(parts) + list(after), landing)


def _sequencer_gather(name, collective_id, shards, after=()):
    n, n_in = len(shards), len(shards) + len(after)

    def body(*refs):
        _handshake(SIBLING_AND_NEIGHBOURS)
        _gather_copies(refs[:n], refs[n_in:n_in + n], *refs[n_in + n:])

    gathered = [jax.ShapeDtypeStruct((N_DEV,) + s.shape, s.dtype) for s in shards]
    return _sequencer(name, collective_id, GATHER_COPIES * n, body, list(shards) + list(after), gathered)


ADAM_ROWS = 512
BF16_ROWS = 16


def _adam_update(w, g, m, v):
    m = ADAM_B1 * m + (1.0 - ADAM_B1) * g
    v = ADAM_B2 * v + (1.0 - ADAM_B2) * (g * g)
    m_hat = m / (1.0 - ADAM_B1 ** ADAM_STEP)
    v_hat = v / (1.0 - ADAM_B2 ** ADAM_STEP)
    delta = -ADAM_LR * (m_hat / (jnp.sqrt(v_hat) + ADAM_EPS) + ADAM_WD * w)
    return delta, m, v


def _sum_slots(ref):
    total = ref[0].astype(F32)
    for d in range(1, ref.shape[0]):
        total = total + ref[d].astype(F32)
    return total


def _adamw_sum(name, landed, w, m, v):
    layers, rows, cols = w.shape
    tiles = [t for t in range(ADAM_ROWS, 0, -BF16_ROWS) if rows % t == 0]
    tr = tiles[0] if tiles else rows
    nt = rows // tr

    def body(*refs):
        parts = refs[:layers]
        w_ref, m_ref, v_ref, g_ref, d_ref, nm_ref, nv_ref = refs[layers:]
        layer = pl.program_id(0)
        g = _sum_slots(parts[0])
        for q in range(1, layers):
            g = jnp.where(layer == q, _sum_slots(parts[q]), g)
        delta, new_m, new_v = _adam_update(w_ref[...], g, m_ref[...], v_ref[...])
        g_ref[...] = g
        d_ref[...] = delta
        nm_ref[...] = new_m
        nv_ref[...] = new_v

    def part_spec(q):
        return _spec((N_CHIPS, tr, cols), lambda l, i: (0, jnp.where(l == q, i, jnp.where(l < q, 0, nt - 1)), 0))

    tile = _spec((None, tr, cols), lambda l, i: (l, i, 0))
    out = jax.ShapeDtypeStruct((layers, rows, cols), F32)
    return pl.pallas_call(
        body, name=name, grid=(layers, nt), in_specs=[part_spec(q) for q in range(layers)] + [tile] * 3,
        out_specs=[tile] * 4, out_shape=[out] * 4, compiler_params=_params(("arbitrary", "arbitrary")),
    )(*landed, w, m, v)


def _sum_small(landed):
    def body(in_ref, out_ref):
        out_ref[...] = _sum_slots(in_ref)

    return pl.pallas_call(body, name="small_grad_sum", out_shape=jax.ShapeDtypeStruct(landed.shape[1:], F32))(landed)


def _adamw_small(arrays):
    n = len(arrays)

    def body(*refs):
        for i in range(n):
            g_ref, w_ref, m_ref, v_ref = refs[4 * i:4 * i + 4]
            d_ref, nm_ref, nv_ref = refs[4 * n + 3 * i:4 * n + 3 * i + 3]
            d_ref[...], nm_ref[...], nv_ref[...] = _adam_update(w_ref[...], g_ref[...], m_ref[...], v_ref[...])

    out = [jax.ShapeDtypeStruct(w.shape, F32) for _, w, _, _ in arrays for _ in range(3)]
    flat = pl.pallas_call(body, name="adam_small", out_shape=out)(*[a for group in arrays for a in group])
    return [tuple(flat[3 * i:3 * i + 3]) for i in range(n)]


LANES = 128
SUBLANES = 8
F_CONV_SHARD = D_FF // N_DEV
GATE_SHARD = KEY_DIM // N_DEV
NORM_SHARD = D_MODEL // N_DEV


def _tile_rows(a):
    flat = a.reshape(-1)
    size = -(-flat.shape[0] // (SUBLANES * LANES)) * SUBLANES * LANES
    return jnp.pad(flat, (0, size - flat.shape[0])).reshape(-1, LANES)


def _pack_rows(pieces):
    return jnp.concatenate([_tile_rows(p) for p in pieces], axis=0)


def _unpack_rows(packed, shapes):
    out, row = [], 0
    for shape in shapes:
        size = 1
        for s in shape:
            size *= s
        rows = -(-size // (SUBLANES * LANES)) * SUBLANES
        piece = packed[..., row:row + rows, :]
        out.append(piece.reshape(piece.shape[:-2] + (rows * LANES,))[..., :size])
        row += rows
    return out


SMALL_SHARDS = ((GATE_RANK, GATE_SHARD), (1, NORM_SHARD), (3, NORM_SHARD), (2, 3, F_CONV_SHARD))


def _unpack_small_shards(g):
    gate, b_norm, b_conv, f_conv = _unpack_rows(g, SMALL_SHARDS)
    gate = gate.reshape(N_DEV, GATE_RANK, GATE_SHARD).transpose(1, 0, 2).reshape(GATE_RANK, KEY_DIM)
    b_norm = b_norm.reshape(1, D_MODEL)
    b_conv = b_conv.reshape(N_DEV, 3, NORM_SHARD).transpose(1, 0, 2).reshape(3, D_MODEL)
    f_conv = f_conv.reshape(N_DEV, 2, 3, F_CONV_SHARD).transpose(1, 2, 0, 3).reshape(2, 3, D_FF)
    return gate, b_norm, b_conv, f_conv


SMALL_LAYOUT = (("a_norm", (1, D_MODEL)), ("a_w_gate_up", (GATE_RANK, KEY_DIM)), ("a_b_gate", (1, KEY_DIM)), ("a_gn", (1, VALUE_DIM)),
                ("b_norm", (1, D_MODEL)), ("b_conv", (3, D_MODEL)), ("f_norm0", (1, D_MODEL)), ("f_norm1", (1, D_MODEL)),
                ("f_conv0", (3, D_FF)), ("f_conv1", (3, D_FF)), ("final_norm", (1, D_MODEL)), ("loss", (1, LANES)))


def _pack_small_grads(g):
    full = dict(g)
    full["a_w_gate_up"] = g["a_w_gate_up"][:GATE_RANK]
    for layer in range(2):
        full[f"f_norm{layer}"] = g["f_norm"][layer]
        full[f"f_conv{layer}"] = g["f_conv"][layer]
    return _pack_rows([full[name] for name, _ in SMALL_LAYOUT])


def _unpack_small_grads(packed):
    pieces = _unpack_rows(packed, [shape for _, shape in SMALL_LAYOUT])
    out = {name: piece.reshape(shape) for (name, shape), piece in zip(SMALL_LAYOUT, pieces)}
    out["f_norm"] = jnp.stack([out["f_norm0"][0], out["f_norm1"][0]])
    out["f_conv"] = jnp.stack([out["f_conv0"], out["f_conv1"]])
    return out


def kernel(x, a_norm, a_w_in, a_w_gate_up, a_b_gate, a_gn, a_w_out, b_norm, b_w_in, b_conv, b_w_out, f_norm, f_w_up, f_conv, f_w_down, final_norm, loss_target, m_a_norm, m_a_w_in, m_a_w_gate_up, m_a_b_gate, m_a_gn, m_a_w_out, m_b_norm, m_b_w_in, m_b_conv, m_b_w_out, m_f_norm, m_f_w_up, m_f_conv, m_f_w_down, m_final_norm, v_a_norm, v_a_w_in, v_a_w_gate_up, v_a_b_gate, v_a_gn, v_a_w_out, v_b_norm, v_b_w_in, v_b_conv, v_b_w_out, v_f_norm, v_f_w_up, v_f_conv, v_f_w_down, v_final_norm):
    my_slot = _slot(*_position())

    transposed = lambda w: jnp.swapaxes(w, 1, 2)
    a_transposed = lambda w: w.reshape(D_MODEL, A_SHARD).T.reshape(1, A_SHARD, D_MODEL)
    a_w_in_t, f_w_up_t = a_transposed(a_w_in), transposed(f_w_up)
    first = _all_gather("weight_gather", 10, [a_w_in_t[0].astype(BF16), a_w_out[0].astype(BF16),
                                          _pack_rows([a_w_gate_up[0], b_norm, b_conv[0], f_conv])])
    gathers, small_shards = {}, first[2]
    later = (("f0", f_w_up_t[0], f_w_down[0]), ("b", b_w_in[0], b_w_out[0]), ("f1", f_w_up_t[1], f_w_down[1]))
    for collective_id, (group, w_in, w_out) in enumerate(later):
        w_in, w_out, small_shards = lax.optimization_barrier((w_in.astype(BF16), w_out.astype(BF16), small_shards))
        gathers[group] = _sequencer_gather(f"gather_{group}", collective_id, [w_in, w_out])
    gate_full, b_norm_full, b_conv_full, f_conv_full = _unpack_small_shards(small_shards)
    a_w_in_full = jnp.pad(first[0].reshape(PROJ_A, D_MODEL), ((0, PROJ_A_PAD - PROJ_A), (0, 0)))
    weights = dict(
        a_norm=a_norm, a_w_gate_up=jnp.pad(gate_full, ((0, GATE_PAD - GATE_RANK), (0, 0))).astype(BF16), a_b_gate=a_b_gate,
        a_gn=a_gn, b_norm=b_norm_full, b_conv=b_conv_full, f_norm=f_norm, f_conv=f_conv_full,
        final_norm=final_norm.reshape(1, D_MODEL))

    def fetch(group, after):
        if group == "a":
            return a_w_in_full, first[1].reshape(D_MODEL, D_MODEL)
        w_in, w_out = gathers[group]
        if group == "b":
            return w_in, w_out.reshape(D_MODEL, D_MODEL)
        return w_in.reshape(2, D_FF, D_MODEL), w_out.reshape(D_FF, D_MODEL)

    exchanges, pending = {}, []
    exchange_ids = dict(b=3, f0=4, a=5)
    side = lax.axis_index("c").astype(jnp.int32).reshape(1)

    def emit(group, parts, received, carry):
        sums = _pair_add(f"pair_add_{group}", parts, received, side)
        carry, *sums = lax.optimization_barrier((carry, *sums))
        pending.extend(sums)
        if group != "f1":
            after = list(exchanges.values())[-1][:1] if exchanges else ()
            exchanges[group] = _sequencer_exchange(f"grads_{group}", exchange_ids[group], list(pending), after)
            pending.clear()
        return carry

    dx, g = _local_step(x[0], loss_target[0], weights, fetch, emit)

    (up1, down1, d_b_in, d_b_out), (up0, down0), (d_a_in, d_a_out) = (exchanges[group] for group in ("b", "f0", "a"))
    back = lambda results: tuple(transposed(r) for r in results)
    small_packed, updated_b_out, updated_up, updated_down = lax.optimization_barrier((
        _pack_small_grads(g),
        _adamw_sum("adam_b_w_out", [d_b_out], b_w_out, m_b_w_out, v_b_w_out),
        _adamw_sum("adam_f_w_up", [up0, up1], f_w_up_t, transposed(m_f_w_up), transposed(v_f_w_up)),
        _adamw_sum("adam_f_w_down", [down0, down1], f_w_down, m_f_w_down, v_f_w_down)))
    small_landed = _sequencer_gather("gather_small", 11, [small_packed], after=(d_a_in,))[0]
    small_landed, updated_b_in, updated_a_in, updated_a_out = lax.optimization_barrier((
        small_landed,
        _adamw_sum("adam_b_w_in", [d_b_in], b_w_in, m_b_w_in, v_b_w_in),
        _adamw_sum("adam_a_w_in", [d_a_in], a_w_in_t, a_transposed(m_a_w_in), a_transposed(v_a_w_in)),
        _adamw_sum("adam_a_w_out", [d_a_out], a_w_out, m_a_w_out, v_a_w_out)))
    big = dict(
        b_w_in=tuple(updated_b_in), b_w_out=tuple(updated_b_out), f_w_up=back(updated_up), f_w_down=tuple(updated_down),
        a_w_in=tuple(r.reshape(A_SHARD, D_MODEL).T.reshape(1, D_MODEL, A_SHARD) for r in updated_a_in),
        a_w_out=tuple(updated_a_out))
    small_g = _unpack_small_grads(_sum_small(small_landed))
    loss = small_g["loss"][0, 0]
    small_g["a_w_gate_up"] = lax.dynamic_slice_in_dim(small_g["a_w_gate_up"], my_slot * GATE_SHARD, GATE_SHARD, axis=1)
    small_g["b_norm"] = lax.dynamic_slice_in_dim(small_g["b_norm"], my_slot * NORM_SHARD, NORM_SHARD, axis=1)
    small_g["b_conv"] = lax.dynamic_slice_in_dim(small_g["b_conv"], my_slot * NORM_SHARD, NORM_SHARD, axis=1)
    small_g["f_conv"] = lax.dynamic_slice_in_dim(small_g["f_conv"], my_slot * F_CONV_SHARD, F_CONV_SHARD, axis=2)
    small_w = dict(
        a_norm=(a_norm, m_a_norm, v_a_norm), a_w_gate_up=(a_w_gate_up, m_a_w_gate_up, v_a_w_gate_up),
        a_b_gate=(a_b_gate, m_a_b_gate, v_a_b_gate), a_gn=(a_gn, m_a_gn, v_a_gn), b_norm=(b_norm, m_b_norm, v_b_norm),
        b_conv=(b_conv, m_b_conv, v_b_conv), f_norm=(f_norm, m_f_norm, v_f_norm), f_conv=(f_conv, m_f_conv, v_f_conv),
        final_norm=(final_norm, m_final_norm, v_final_norm))
    two_d = lambda a: a.reshape(-1, a.shape[-1])
    updates = _adamw_small([tuple(two_d(a.reshape(w.shape)) for a in (small_g[name], w, m, v)) for name, (w, m, v) in small_w.items()])
    small = {}
    for (name, (w, _, _)), update in zip(small_w.items(), updates):
        small[name] = (small_g[name].reshape(w.shape),) + tuple(u.reshape(w.shape) for u in update)

    order = ["a_norm", "a_w_in", "a_w_gate_up", "a_b_gate", "a_gn", "a_w_out", "b_norm", "b_w_in", "b_conv", "b_w_out",
             "f_norm", "f_w_up", "f_conv", "f_w_down", "final_norm"]
    results = {**big, **small}
    outputs = [loss, dx.reshape(1, SEQ, D_MODEL)]
    for kind in range(4):
        outputs += [results[name][kind] for name in order]
    return tuple(outputs)
```
